```python
import jax, jax.numpy as jnp
from jax import lax
import numpy as np

D_MODEL = 1024
BATCH = 8
SEQ = 2048
DEPTH = 1

CHUNK = 64
HEAD_DIM = 64
H_SB = 8
H_CH = 8
W_SB = H_SB * HEAD_DIM
W_CH = H_CH * HEAD_DIM
MIX_WIDTH = W_SB + W_CH
LOOKBACK = 8
BAND = (LOOKBACK + 1) * CHUNK
REL_CLIP = 128
Q_BLOCK = 128
D_FF = 2816
PLE_DIM = 256
EPS = 1e-6
NEG_INF = -1e30

kernel_name = "hybrid_stickbreak_chunkattn_macaron_block"


def rms_norm(x, g):
    xf = x.astype(jnp.float32)
    y = xf * lax.rsqrt(jnp.mean(xf * xf, axis=-1, keepdims=True) + EPS)
    return (y * g.astype(jnp.float32)).astype(x.dtype)


def swiglu(x, w_gate, w_up, w_down):
    return (jax.nn.silu(x @ w_gate) * (x @ w_up)) @ w_down


def split_heads(t, n_heads):
    b, s, _ = t.shape
    return t.reshape(b, s, n_heads, HEAD_DIM).transpose(0, 2, 1, 3)


def merge_heads(t):
    b, h, s, d = t.shape
    return t.transpose(0, 2, 1, 3).reshape(b, s, h * d)


def stick_breaking_attention(q, k, v):
    b, h, s, d = q.shape
    nq = s // Q_BLOCK
    scale = d ** -0.5
    q_blocks = q.reshape(b, h, nq, Q_BLOCK, d).transpose(2, 0, 1, 3, 4)
    starts = jnp.arange(nq, dtype=jnp.int32) * Q_BLOCK
    key_pos = jnp.arange(s, dtype=jnp.int32)

    def one_block(args):
        q_blk, start = args
        z = jnp.einsum('bhqd,bhkd->bhqk', q_blk, k,
                       preferred_element_type=jnp.float32) * scale
        q_pos = start + jnp.arange(Q_BLOCK, dtype=jnp.int32)
        before = key_pos[None, :] < q_pos[:, None]
        log_fail = jnp.where(before, jax.nn.log_sigmoid(-z), 0.0)
        later = lax.cumsum(log_fail, axis=3, reverse=True) - log_fail
        log_a = jax.nn.log_sigmoid(z) + later
        a = jnp.where(before, jnp.exp(jnp.where(before, log_a, 0.0)), 0.0)
        return jnp.einsum('bhqk,bhkd->bhqd', a.astype(v.dtype), v)

    out = lax.map(one_block, (q_blocks, starts))
    return out.transpose(1, 2, 0, 3, 4).reshape(b, h, s, d)


def rel_bias_index():
    i = np.arange(CHUNK)[:, None]
    j = np.arange(BAND)[None, :]
    dist = i + LOOKBACK * CHUNK - j
    return jnp.asarray(np.clip(dist, -REL_CLIP, REL_CLIP) + REL_CLIP, dtype=jnp.int32)


def chunk_band_attention(q, k, v, rel_bias):
    b, h, s, d = q.shape
    nc = s // CHUNK
    scale = d ** -0.5
    qc = q.reshape(b, h, nc, CHUNK, d)

    def band(t):
        tc = t.reshape(b, h, nc, CHUNK, d)
        tp = jnp.pad(tc, ((0, 0), (0, 0), (LOOKBACK, 0), (0, 0), (0, 0)))
        return jnp.concatenate([tp[:, :, w:w + nc] for w in range(LOOKBACK + 1)], axis=3)

    kb, vb = band(k), band(v)
    bias = rel_bias.astype(jnp.float32)[:, rel_bias_index()]
    z = jnp.einsum('bhnqd,bhnkd->bhnqk', qc, kb,
                   preferred_element_type=jnp.float32) * scale + bias[None, :, None]
    slot_chunk = jnp.arange(BAND, dtype=jnp.int32) // CHUNK
    chunk_id = jnp.arange(nc, dtype=jnp.int32)
    valid = (chunk_id[:, None] + slot_chunk[None, :] - LOOKBACK) >= 0
    z = jnp.where(valid[None, None, :, None, :], z, NEG_INF)
    prob = jax.nn.softmax(z, axis=-1)
    o = jnp.einsum('bhnqk,bhnkd->bhnqd', prob.astype(vb.dtype), vb)
    return o.reshape(b, h, s, d)


def _fwd_setup_inputs(seed: int = 0) -> dict:
    key = jax.random.key(seed)
    ks = jax.random.split(key, 24)
    f32 = jnp.float32

    def w(k, shape, fan_in):
        return jax.random.normal(k, shape, f32) * (fan_in ** -0.5)

    def gain(k, n):
        return 1.0 + 0.05 * jax.random.normal(k, (DEPTH, n), f32)

    return {
        "x": jax.random.normal(ks[0], (BATCH, SEQ, D_MODEL), f32),
        "p": jax.random.normal(ks[1], (DEPTH, BATCH, SEQ, PLE_DIM), f32),
        "g_ffn1_pre": gain(ks[2], D_MODEL),
        "g_ffn1_post": gain(ks[3], D_MODEL),
        "w_ffn1_gate": w(ks[4], (DEPTH, D_MODEL, D_FF), D_MODEL),
        "w_ffn1_up": w(ks[5], (DEPTH, D_MODEL, D_FF), D_MODEL),
        "w_ffn1_down": w(ks[6], (DEPTH, D_FF, D_MODEL), D_FF),
        "g_mix_pre": gain(ks[7], D_MODEL),
        "g_mix_post": gain(ks[8], D_MODEL),
        "w_in": w(ks[9], (DEPTH, D_MODEL, 3 * MIX_WIDTH), D_MODEL),
        "g_out_sb": gain(ks[10], W_SB),
        "g_out_ch": gain(ks[11], W_CH),
        "rel_bias": 0.02 * jax.random.normal(ks[12], (DEPTH, H_CH, 2 * REL_CLIP + 1), f32),
        "w_out": w(ks[13], (DEPTH, MIX_WIDTH, D_MODEL), MIX_WIDTH),
        "g_ffn2_pre": gain(ks[14], D_MODEL),
        "g_ffn2_post": gain(ks[15], D_MODEL),
        "w_ffn2_gate": w(ks[16], (DEPTH, D_MODEL, D_FF), D_MODEL),
        "w_ffn2_up": w(ks[17], (DEPTH, D_MODEL, D_FF), D_MODEL),
        "w_ffn2_down": w(ks[18], (DEPTH, D_FF, D_MODEL), D_FF),
        "w_ple_proj": w(ks[19], (DEPTH, PLE_DIM, D_MODEL), PLE_DIM),
        "w_ple_gate": w(ks[20], (DEPTH, D_MODEL, D_MODEL), D_MODEL),
        "g_ple_post": gain(ks[21], D_MODEL),
    }


def _fwd_reference(x, p, g_ffn1_pre, g_ffn1_post, w_ffn1_gate, w_ffn1_up, w_ffn1_down,
              g_mix_pre, g_mix_post, w_in, g_out_sb, g_out_ch, rel_bias, w_out,
              g_ffn2_pre, g_ffn2_post, w_ffn2_gate, w_ffn2_up, w_ffn2_down,
              w_ple_proj, w_ple_gate, g_ple_post):
    h = x
    for i in range(DEPTH):
        f = swiglu(rms_norm(h, g_ffn1_pre[i]), w_ffn1_gate[i], w_ffn1_up[i], w_ffn1_down[i])
        h = h + 0.5 * rms_norm(f, g_ffn1_post[i])

        u = rms_norm(h, g_mix_pre[i])
        qkv = u @ w_in[i]
        q_a, k_a, v_a, q_b, k_b, v_b = jnp.split(
            qkv, np.cumsum([W_SB, W_SB, W_SB, W_CH, W_CH])[:5].tolist(), axis=-1)
        o_a = stick_breaking_attention(split_heads(q_a, H_SB), split_heads(k_a, H_SB),
                                       split_heads(v_a, H_SB))
        o_b = chunk_band_attention(split_heads(q_b, H_CH), split_heads(k_b, H_CH),
                                   split_heads(v_b, H_CH), rel_bias[i])
        mixed = jnp.concatenate([rms_norm(merge_heads(o_a), g_out_sb[i]),
                                 rms_norm(merge_heads(o_b), g_out_ch[i])], axis=-1)
        h = h + rms_norm(mixed @ w_out[i], g_mix_post[i])

        f = swiglu(rms_norm(h, g_ffn2_pre[i]), w_ffn2_gate[i], w_ffn2_up[i], w_ffn2_down[i])
        h = h + 0.5 * rms_norm(f, g_ffn2_post[i])

        e = (p[i] @ w_ple_proj[i]) * jax.nn.sigmoid(h @ w_ple_gate[i])
        h = h + rms_norm(e, g_ple_post[i])
    return h


import jax as _jax
import jax.numpy as _jnp

TWIN_FORMAT = 'train_step'
FWD_PARAMS = ['x', 'p', 'g_ffn1_pre', 'g_ffn1_post', 'w_ffn1_gate', 'w_ffn1_up', 'w_ffn1_down', 'g_mix_pre', 'g_mix_post', 'w_in', 'g_out_sb', 'g_out_ch', 'rel_bias', 'w_out', 'g_ffn2_pre', 'g_ffn2_post', 'w_ffn2_gate', 'w_ffn2_up', 'w_ffn2_down', 'w_ple_proj', 'w_ple_gate', 'g_ple_post']
TWIN_WEIGHTS = ['g_ffn1_pre', 'g_ffn1_post', 'w_ffn1_gate', 'w_ffn1_up', 'w_ffn1_down', 'g_mix_pre', 'g_mix_post', 'w_in', 'g_out_sb', 'g_out_ch', 'rel_bias', 'w_out', 'g_ffn2_pre', 'g_ffn2_post', 'w_ffn2_gate', 'w_ffn2_up', 'w_ffn2_down', 'w_ple_proj', 'w_ple_gate', 'g_ple_post']
TWIN_DIFF_INPUT = 'x'
TWIN_INPUTS = ['x', 'p', 'g_ffn1_pre', 'g_ffn1_post', 'w_ffn1_gate', 'w_ffn1_up', 'w_ffn1_down', 'g_mix_pre', 'g_mix_post', 'w_in', 'g_out_sb', 'g_out_ch', 'rel_bias', 'w_out', 'g_ffn2_pre', 'g_ffn2_post', 'w_ffn2_gate', 'w_ffn2_up', 'w_ffn2_down', 'w_ple_proj', 'w_ple_gate', 'g_ple_post', 'loss_target', 'm_g_ffn1_pre', 'm_g_ffn1_post', 'm_w_ffn1_gate', 'm_w_ffn1_up', 'm_w_ffn1_down', 'm_g_mix_pre', 'm_g_mix_post', 'm_w_in', 'm_g_out_sb', 'm_g_out_ch', 'm_rel_bias', 'm_w_out', 'm_g_ffn2_pre', 'm_g_ffn2_post', 'm_w_ffn2_gate', 'm_w_ffn2_up', 'm_w_ffn2_down', 'm_w_ple_proj', 'm_w_ple_gate', 'm_g_ple_post', 'v_g_ffn1_pre', 'v_g_ffn1_post', 'v_w_ffn1_gate', 'v_w_ffn1_up', 'v_w_ffn1_down', 'v_g_mix_pre', 'v_g_mix_post', 'v_w_in', 'v_g_out_sb', 'v_g_out_ch', 'v_rel_bias', 'v_w_out', 'v_g_ffn2_pre', 'v_g_ffn2_post', 'v_w_ffn2_gate', 'v_w_ffn2_up', 'v_w_ffn2_down', 'v_w_ple_proj', 'v_w_ple_gate', 'v_g_ple_post']
TWIN_OUTPUTS = ['loss', 'grad_x', 'grad_g_ffn1_pre', 'grad_g_ffn1_post', 'grad_w_ffn1_gate', 'grad_w_ffn1_up', 'grad_w_ffn1_down', 'grad_g_mix_pre', 'grad_g_mix_post', 'grad_w_in', 'grad_g_out_sb', 'grad_g_out_ch', 'grad_rel_bias', 'grad_w_out', 'grad_g_ffn2_pre', 'grad_g_ffn2_post', 'grad_w_ffn2_gate', 'grad_w_ffn2_up', 'grad_w_ffn2_down', 'grad_w_ple_proj', 'grad_w_ple_gate', 'grad_g_ple_post', 'delta_g_ffn1_pre', 'delta_g_ffn1_post', 'delta_w_ffn1_gate', 'delta_w_ffn1_up', 'delta_w_ffn1_down', 'delta_g_mix_pre', 'delta_g_mix_post', 'delta_w_in', 'delta_g_out_sb', 'delta_g_out_ch', 'delta_rel_bias', 'delta_w_out', 'delta_g_ffn2_pre', 'delta_g_ffn2_post', 'delta_w_ffn2_gate', 'delta_w_ffn2_up', 'delta_w_ffn2_down', 'delta_w_ple_proj', 'delta_w_ple_gate', 'delta_g_ple_post', 'new_m_g_ffn1_pre', 'new_m_g_ffn1_post', 'new_m_w_ffn1_gate', 'new_m_w_ffn1_up', 'new_m_w_ffn1_down', 'new_m_g_mix_pre', 'new_m_g_mix_post', 'new_m_w_in', 'new_m_g_out_sb', 'new_m_g_out_ch', 'new_m_rel_bias', 'new_m_w_out', 'new_m_g_ffn2_pre', 'new_m_g_ffn2_post', 'new_m_w_ffn2_gate', 'new_m_w_ffn2_up', 'new_m_w_ffn2_down', 'new_m_w_ple_proj', 'new_m_w_ple_gate', 'new_m_g_ple_post', 'new_v_g_ffn1_pre', 'new_v_g_ffn1_post', 'new_v_w_ffn1_gate', 'new_v_w_ffn1_up', 'new_v_w_ffn1_down', 'new_v_g_mix_pre', 'new_v_g_mix_post', 'new_v_w_in', 'new_v_g_out_sb', 'new_v_g_out_ch', 'new_v_rel_bias', 'new_v_w_out', 'new_v_g_ffn2_pre', 'new_v_g_ffn2_post', 'new_v_w_ffn2_gate', 'new_v_w_ffn2_up', 'new_v_w_ffn2_down', 'new_v_w_ple_proj', 'new_v_w_ple_gate', 'new_v_g_ple_post']
TWIN_LEAF_KINDS = {'loss': 'loss', 'grad_x': 'grad_x', 'grad_g_ffn1_pre': 'grad_w', 'grad_g_ffn1_post': 'grad_w', 'grad_w_ffn1_gate': 'grad_w', 'grad_w_ffn1_up': 'grad_w', 'grad_w_ffn1_down': 'grad_w', 'grad_g_mix_pre': 'grad_w', 'grad_g_mix_post': 'grad_w', 'grad_w_in': 'grad_w', 'grad_g_out_sb': 'grad_w', 'grad_g_out_ch': 'grad_w', 'grad_rel_bias': 'grad_w', 'grad_w_out': 'grad_w', 'grad_g_ffn2_pre': 'grad_w', 'grad_g_ffn2_post': 'grad_w', 'grad_w_ffn2_gate': 'grad_w', 'grad_w_ffn2_up': 'grad_w', 'grad_w_ffn2_down': 'grad_w', 'grad_w_ple_proj': 'grad_w', 'grad_w_ple_gate': 'grad_w', 'grad_g_ple_post': 'grad_w', 'delta_g_ffn1_pre': 'delta_w', 'delta_g_ffn1_post': 'delta_w', 'delta_w_ffn1_gate': 'delta_w', 'delta_w_ffn1_up': 'delta_w', 'delta_w_ffn1_down': 'delta_w', 'delta_g_mix_pre': 'delta_w', 'delta_g_mix_post': 'delta_w', 'delta_w_in': 'delta_w', 'delta_g_out_sb': 'delta_w', 'delta_g_out_ch': 'delta_w', 'delta_rel_bias': 'delta_w', 'delta_w_out': 'delta_w', 'delta_g_ffn2_pre': 'delta_w', 'delta_g_ffn2_post': 'delta_w', 'delta_w_ffn2_gate': 'delta_w', 'delta_w_ffn2_up': 'delta_w', 'delta_w_ffn2_down': 'delta_w', 'delta_w_ple_proj': 'delta_w', 'delta_w_ple_gate': 'delta_w', 'delta_g_ple_post': 'delta_w', 'new_m_g_ffn1_pre': 'new_m', 'new_m_g_ffn1_post': 'new_m', 'new_m_w_ffn1_gate': 'new_m', 'new_m_w_ffn1_up': 'new_m', 'new_m_w_ffn1_down': 'new_m', 'new_m_g_mix_pre': 'new_m', 'new_m_g_mix_post': 'new_m', 'new_m_w_in': 'new_m', 'new_m_g_out_sb': 'new_m', 'new_m_g_out_ch': 'new_m', 'new_m_rel_bias': 'new_m', 'new_m_w_out': 'new_m', 'new_m_g_ffn2_pre': 'new_m', 'new_m_g_ffn2_post': 'new_m', 'new_m_w_ffn2_gate': 'new_m', 'new_m_w_ffn2_up': 'new_m', 'new_m_w_ffn2_down': 'new_m', 'new_m_w_ple_proj': 'new_m', 'new_m_w_ple_gate': 'new_m', 'new_m_g_ple_post': 'new_m', 'new_v_g_ffn1_pre': 'new_v', 'new_v_g_ffn1_post': 'new_v', 'new_v_w_ffn1_gate': 'new_v', 'new_v_w_ffn1_up': 'new_v', 'new_v_w_ffn1_down': 'new_v', 'new_v_g_mix_pre': 'new_v', 'new_v_g_mix_post': 'new_v', 'new_v_w_in': 'new_v', 'new_v_g_out_sb': 'new_v', 'new_v_g_out_ch': 'new_v', 'new_v_rel_bias': 'new_v', 'new_v_w_out': 'new_v', 'new_v_g_ffn2_pre': 'new_v', 'new_v_g_ffn2_post': 'new_v', 'new_v_w_ffn2_gate': 'new_v', 'new_v_w_ffn2_up': 'new_v', 'new_v_w_ffn2_down': 'new_v', 'new_v_w_ple_proj': 'new_v', 'new_v_w_ple_gate': 'new_v', 'new_v_g_ple_post': 'new_v'}


def _forward(args):
    return _fwd_reference(*[args[k] for k in FWD_PARAMS])


def _output_shape():
    out = _jax.eval_shape(lambda: _forward(_fwd_setup_inputs(0)))
    return out.shape, out.dtype

N_MICROBATCH = 1
ADAM_LR = 0.001
ADAM_B1 = 0.9
ADAM_B2 = 0.999
ADAM_EPS = 1e-08
ADAM_WD = 0.01
ADAM_STEP = 10
PER_EXAMPLE_BATCH_AXIS = {'x': 0, 'p': 1, 'loss_target': 0}
SHARED_INPUTS = []
_WEIGHT_DTYPES = {'g_ffn1_pre': _jnp.float32, 'g_ffn1_post': _jnp.float32, 'w_ffn1_gate': _jnp.float32, 'w_ffn1_up': _jnp.float32, 'w_ffn1_down': _jnp.float32, 'g_mix_pre': _jnp.float32, 'g_mix_post': _jnp.float32, 'w_in': _jnp.float32, 'g_out_sb': _jnp.float32, 'g_out_ch': _jnp.float32, 'rel_bias': _jnp.float32, 'w_out': _jnp.float32, 'g_ffn2_pre': _jnp.float32, 'g_ffn2_post': _jnp.float32, 'w_ffn2_gate': _jnp.float32, 'w_ffn2_up': _jnp.float32, 'w_ffn2_down': _jnp.float32, 'w_ple_proj': _jnp.float32, 'w_ple_gate': _jnp.float32, 'g_ple_post': _jnp.float32}
MOMENT_SCALE = {'g_ffn1_pre': 3.975701e-01, 'g_ffn1_post': 3.939009e+00, 'w_ffn1_gate': 1.546449e-01, 'w_ffn1_up': 1.558221e-01, 'w_ffn1_down': 2.597467e-01, 'g_mix_pre': 4.926112e-01, 'g_mix_post': 1.598907e+01, 'w_in': 2.781245e-01, 'g_out_sb': 3.198348e-01, 'g_out_ch': 3.787912e-01, 'rel_bias': 1.254859e-01, 'w_out': 3.440595e-01, 'g_ffn2_pre': 2.378106e-01, 'g_ffn2_post': 3.991170e+00, 'w_ffn2_gate': 9.135008e-02, 'w_ffn2_up': 1.157085e-01, 'w_ffn2_down': 1.918578e-01, 'w_ple_proj': 2.429843e-01, 'w_ple_gate': 1.282562e-01, 'g_ple_post': 1.609216e+01}


def _to_microbatches(a, axis):
    t = _jnp.moveaxis(a, axis, 0)
    t = t.reshape((N_MICROBATCH, t.shape[0] // N_MICROBATCH) + t.shape[1:])
    return _jnp.moveaxis(t, 1, axis + 1)


def setup_inputs(seed: int = 0) -> dict:
    inp = _fwd_setup_inputs(seed)
    key = _jax.random.fold_in(_jax.random.key(seed), 7919)
    shape, _ = _output_shape()
    out = dict(inp)
    out["loss_target"] = _jax.random.normal(_jax.random.fold_in(key, 0), shape, _jnp.float32)
    for i, name in enumerate(TWIN_WEIGHTS):
        w = inp[name].astype(_jnp.float32)
        if MOMENT_SCALE is None:
            s = _jnp.sqrt(_jnp.mean(_jnp.square(w)) + 1e-30)
        else:
            s = MOMENT_SCALE[name]
        km, kv = _jax.random.split(_jax.random.fold_in(key, i + 1))
        out[name] = w
        out["m_" + name] = s * _jax.random.normal(km, w.shape, _jnp.float32)
        out["v_" + name] = (s * s) * _jax.random.uniform(kv, w.shape, _jnp.float32, 0.5, 1.5)
    if N_MICROBATCH > 1:
        for name, axis in PER_EXAMPLE_BATCH_AXIS.items():
            out[name] = _to_microbatches(out[name], axis)
    return {'x': out['x'], 'p': out['p'], 'g_ffn1_pre': out['g_ffn1_pre'], 'g_ffn1_post': out['g_ffn1_post'], 'w_ffn1_gate': out['w_ffn1_gate'], 'w_ffn1_up': out['w_ffn1_up'], 'w_ffn1_down': out['w_ffn1_down'], 'g_mix_pre': out['g_mix_pre'], 'g_mix_post': out['g_mix_post'], 'w_in': out['w_in'], 'g_out_sb': out['g_out_sb'], 'g_out_ch': out['g_out_ch'], 'rel_bias': out['rel_bias'], 'w_out': out['w_out'], 'g_ffn2_pre': out['g_ffn2_pre'], 'g_ffn2_post': out['g_ffn2_post'], 'w_ffn2_gate': out['w_ffn2_gate'], 'w_ffn2_up': out['w_ffn2_up'], 'w_ffn2_down': out['w_ffn2_down'], 'w_ple_proj': out['w_ple_proj'], 'w_ple_gate': out['w_ple_gate'], 'g_ple_post': out['g_ple_post'], 'loss_target': out['loss_target'], 'm_g_ffn1_pre': out['m_g_ffn1_pre'], 'm_g_ffn1_post': out['m_g_ffn1_post'], 'm_w_ffn1_gate': out['m_w_ffn1_gate'], 'm_w_ffn1_up': out['m_w_ffn1_up'], 'm_w_ffn1_down': out['m_w_ffn1_down'], 'm_g_mix_pre': out['m_g_mix_pre'], 'm_g_mix_post': out['m_g_mix_post'], 'm_w_in': out['m_w_in'], 'm_g_out_sb': out['m_g_out_sb'], 'm_g_out_ch': out['m_g_out_ch'], 'm_rel_bias': out['m_rel_bias'], 'm_w_out': out['m_w_out'], 'm_g_ffn2_pre': out['m_g_ffn2_pre'], 'm_g_ffn2_post': out['m_g_ffn2_post'], 'm_w_ffn2_gate': out['m_w_ffn2_gate'], 'm_w_ffn2_up': out['m_w_ffn2_up'], 'm_w_ffn2_down': out['m_w_ffn2_down'], 'm_w_ple_proj': out['m_w_ple_proj'], 'm_w_ple_gate': out['m_w_ple_gate'], 'm_g_ple_post': out['m_g_ple_post'], 'v_g_ffn1_pre': out['v_g_ffn1_pre'], 'v_g_ffn1_post': out['v_g_ffn1_post'], 'v_w_ffn1_gate': out['v_w_ffn1_gate'], 'v_w_ffn1_up': out['v_w_ffn1_up'], 'v_w_ffn1_down': out['v_w_ffn1_down'], 'v_g_mix_pre': out['v_g_mix_pre'], 'v_g_mix_post': out['v_g_mix_post'], 'v_w_in': out['v_w_in'], 'v_g_out_sb': out['v_g_out_sb'], 'v_g_out_ch': out['v_g_out_ch'], 'v_rel_bias': out['v_rel_bias'], 'v_w_out': out['v_w_out'], 'v_g_ffn2_pre': out['v_g_ffn2_pre'], 'v_g_ffn2_post': out['v_g_ffn2_post'], 'v_w_ffn2_gate': out['v_w_ffn2_gate'], 'v_w_ffn2_up': out['v_w_ffn2_up'], 'v_w_ffn2_down': out['v_w_ffn2_down'], 'v_w_ple_proj': out['v_w_ple_proj'], 'v_w_ple_gate': out['v_w_ple_gate'], 'v_g_ple_post': out['v_g_ple_post']}


def _loss(weights, diff, rest, loss_target):
    with _jax.named_scope("forward"):
        args = {**rest, TWIN_DIFF_INPUT: diff, **{k: w.astype(_WEIGHT_DTYPES[k]) for k, w in weights.items()}}
        y = _forward(args)
    with _jax.named_scope("loss_head"):
        err = _jnp.square(y.astype(_jnp.float32) - loss_target)
        return 0.5 * _jnp.sum(_jnp.mean(err, axis=-1)) if err.ndim else 0.5 * err


def _adamw(w, g, m, v):
    m = ADAM_B1 * m + (1.0 - ADAM_B1) * g
    v = ADAM_B2 * v + (1.0 - ADAM_B2) * _jnp.square(g)
    m_hat = m / (1.0 - ADAM_B1 ** ADAM_STEP)
    v_hat = v / (1.0 - ADAM_B2 ** ADAM_STEP)
    delta = -ADAM_LR * (m_hat / (_jnp.sqrt(v_hat) + ADAM_EPS) + ADAM_WD * w)
    return delta, m, v


def reference(x, p, g_ffn1_pre, g_ffn1_post, w_ffn1_gate, w_ffn1_up, w_ffn1_down, g_mix_pre, g_mix_post, w_in, g_out_sb, g_out_ch, rel_bias, w_out, g_ffn2_pre, g_ffn2_post, w_ffn2_gate, w_ffn2_up, w_ffn2_down, w_ple_proj, w_ple_gate, g_ple_post, loss_target, m_g_ffn1_pre, m_g_ffn1_post, m_w_ffn1_gate, m_w_ffn1_up, m_w_ffn1_down, m_g_mix_pre, m_g_mix_post, m_w_in, m_g_out_sb, m_g_out_ch, m_rel_bias, m_w_out, m_g_ffn2_pre, m_g_ffn2_post, m_w_ffn2_gate, m_w_ffn2_up, m_w_ffn2_down, m_w_ple_proj, m_w_ple_gate, m_g_ple_post, v_g_ffn1_pre, v_g_ffn1_post, v_w_ffn1_gate, v_w_ffn1_up, v_w_ffn1_down, v_g_mix_pre, v_g_mix_post, v_w_in, v_g_out_sb, v_g_out_ch, v_rel_bias, v_w_out, v_g_ffn2_pre, v_g_ffn2_post, v_w_ffn2_gate, v_w_ffn2_up, v_w_ffn2_down, v_w_ple_proj, v_w_ple_gate, v_g_ple_post):
    given = dict(x=x, p=p, g_ffn1_pre=g_ffn1_pre, g_ffn1_post=g_ffn1_post, w_ffn1_gate=w_ffn1_gate, w_ffn1_up=w_ffn1_up, w_ffn1_down=w_ffn1_down, g_mix_pre=g_mix_pre, g_mix_post=g_mix_post, w_in=w_in, g_out_sb=g_out_sb, g_out_ch=g_out_ch, rel_bias=rel_bias, w_out=w_out, g_ffn2_pre=g_ffn2_pre, g_ffn2_post=g_ffn2_post, w_ffn2_gate=w_ffn2_gate, w_ffn2_up=w_ffn2_up, w_ffn2_down=w_ffn2_down, w_ple_proj=w_ple_proj, w_ple_gate=w_ple_gate, g_ple_post=g_ple_post, loss_target=loss_target, m_g_ffn1_pre=m_g_ffn1_pre, m_g_ffn1_post=m_g_ffn1_post, m_w_ffn1_gate=m_w_ffn1_gate, m_w_ffn1_up=m_w_ffn1_up, m_w_ffn1_down=m_w_ffn1_down, m_g_mix_pre=m_g_mix_pre, m_g_mix_post=m_g_mix_post, m_w_in=m_w_in, m_g_out_sb=m_g_out_sb, m_g_out_ch=m_g_out_ch, m_rel_bias=m_rel_bias, m_w_out=m_w_out, m_g_ffn2_pre=m_g_ffn2_pre, m_g_ffn2_post=m_g_ffn2_post, m_w_ffn2_gate=m_w_ffn2_gate, m_w_ffn2_up=m_w_ffn2_up, m_w_ffn2_down=m_w_ffn2_down, m_w_ple_proj=m_w_ple_proj, m_w_ple_gate=m_w_ple_gate, m_g_ple_post=m_g_ple_post, v_g_ffn1_pre=v_g_ffn1_pre, v_g_ffn1_post=v_g_ffn1_post, v_w_ffn1_gate=v_w_ffn1_gate, v_w_ffn1_up=v_w_ffn1_up, v_w_ffn1_down=v_w_ffn1_down, v_g_mix_pre=v_g_mix_pre, v_g_mix_post=v_g_mix_post, v_w_in=v_w_in, v_g_out_sb=v_g_out_sb, v_g_out_ch=v_g_out_ch, v_rel_bias=v_rel_bias, v_w_out=v_w_out, v_g_ffn2_pre=v_g_ffn2_pre, v_g_ffn2_post=v_g_ffn2_post, v_w_ffn2_gate=v_w_ffn2_gate, v_w_ffn2_up=v_w_ffn2_up, v_w_ffn2_down=v_w_ffn2_down, v_w_ple_proj=v_w_ple_proj, v_w_ple_gate=v_w_ple_gate, v_g_ple_post=v_g_ple_post)
    weights = {n: given[n] for n in TWIN_WEIGHTS}
    shared = {n: given[n] for n in SHARED_INPUTS}
    per_example = {n: given[n] for n in ['x', 'p']}
    grad_fn = _jax.value_and_grad(_loss, argnums=(0, 1))

    def one_microbatch(ex, loss_target):
        ex = dict(ex)
        diff = ex.pop(TWIN_DIFF_INPUT)
        return grad_fn(weights, diff, {**shared, **ex}, loss_target)

    if N_MICROBATCH == 1:
        loss, (grad_w, grad_x) = one_microbatch(per_example, given["loss_target"])
    else:
        def body(carry, xs):
            loss_sum, grad_sum = carry
            l_k, (gw_k, gx_k) = one_microbatch(xs[0], xs[1])
            with _jax.named_scope("update"):
                return (loss_sum + l_k, _jax.tree.map(_jnp.add, grad_sum, gw_k)), gx_k

        init = (_jnp.zeros((), _jnp.float32), _jax.tree.map(_jnp.zeros_like, weights))
        (loss, grad_w), grad_x = _jax.lax.scan(body, init, (per_example, given["loss_target"]))
    with _jax.named_scope("update"):
        delta_w, new_m, new_v = {}, {}, {}
        for n in TWIN_WEIGHTS:
            delta_w[n], new_m[n], new_v[n] = _adamw(weights[n], grad_w[n], given["m_" + n], given["v_" + n])
    return (loss, grad_x, *[grad_w[n] for n in TWIN_WEIGHTS], *[delta_w[n] for n in TWIN_WEIGHTS],
            *[new_m[n] for n in TWIN_WEIGHTS], *[new_v[n] for n in TWIN_WEIGHTS])
```

```python
import functools

import jax
import jax.numpy as jnp
from jax import lax
from jax.experimental import pallas as pl
from jax.experimental.pallas import tpu as pltpu

F32 = jnp.float32
BF16 = jnp.bfloat16
EPS = 1e-6
N_CHIPS = 4
HEAD_DIM = 64
N_HEADS = 8
CHUNK = 64
LOOKBACK = 8
BAND = (LOOKBACK + 1) * CHUNK
PAD = LOOKBACK * CHUNK
REL_CLIP = 128
N_REL = 2 * REL_CLIP + 1
N_REL_PAD = 384
SB_BLOCK = 128
ATT_SCALE = HEAD_DIM ** -0.5
NEG_INF = -1e30
ROW_BLOCK = 512
VMEM_LIMIT = 48 * 1024 * 1024
MESH = pl.DeviceIdType.MESH

ADAM_LR = 0.001
ADAM_B1 = 0.9
ADAM_B2 = 0.999
ADAM_EPS = 1e-08
ADAM_WD = 0.01
ADAM_STEP = 10

NT = (((1,), (1,)), ((), ()))
TN = (((0,), (0,)), ((), ()))


def _params(n_grid, vmem=None):
    return pltpu.CompilerParams(dimension_semantics=("arbitrary",) * n_grid, vmem_limit_bytes=vmem)


def _dot(a, b, dims=None):
    if dims is None:
        return jnp.dot(a, b, preferred_element_type=F32)
    return lax.dot_general(a, b, dims, preferred_element_type=F32)


def _sigmoid(x):
    return 1.0 / (1.0 + jnp.exp(-x))


def _rms_fwd(x, g):
    r = lax.rsqrt(jnp.mean(x * x, axis=-1, keepdims=True) + EPS)
    return x * r * g


def _rms_bwd(x, g, dy):
    r = lax.rsqrt(jnp.mean(x * x, axis=-1, keepdims=True) + EPS)
    xh = x * r
    dg = jnp.sum(dy * xh, axis=0, keepdims=True)
    t = dy * g
    dx = r * (t - xh * jnp.mean(t * xh, axis=-1, keepdims=True))
    return dx, dg


def _accumulate(ref, val, first):
    @pl.when(first)
    def _():
        ref[...] = val

    @pl.when(jnp.logical_not(first))
    def _():
        ref[...] += val


def _split2(x):
    hi = x.astype(BF16)
    lo = (x - hi.astype(F32)).astype(BF16)
    return hi, lo


def _ffn_fwd(x, g_pre, g_post, wg, wu, wd, name):
    T, D = x.shape
    S, _, FS = wg.shape
    tm = min(ROW_BLOCK, T)

    def body(x_ref, gpre_ref, gpost_ref, wg_ref, wu_ref, wd_ref,
             h_ref, xn_ref, g_ref, u_ref, a_ref, f_ref, xn_s, acc_s):
        k = pl.program_id(1)

        @pl.when(k == 0)
        def _():
            xn_s[...] = _rms_fwd(x_ref[...], gpre_ref[...]).astype(BF16)
            xn_ref[...] = xn_s[...]

        xn = xn_s[...]
        g = _dot(xn, wg_ref[0])
        u = _dot(xn, wu_ref[0])
        g_ref[0] = g
        u_ref[0] = u
        a = (g * _sigmoid(g) * u).astype(BF16)
        a_ref[0] = a
        _accumulate(acc_s, _dot(a, wd_ref[0]), k == 0)

        @pl.when(k == S - 1)
        def _():
            f = acc_s[...]
            f_ref[...] = f
            h_ref[...] = x_ref[...] + 0.5 * _rms_fwd(f, gpost_ref[...])

    row = pl.BlockSpec((tm, D), lambda i, k: (i, 0))
    vec = pl.BlockSpec((1, D), lambda i, k: (0, 0))
    act = pl.BlockSpec((1, tm, FS), lambda i, k: (k, i, 0))
    return pl.pallas_call(
        body, name=name, grid=(T // tm, S),
        in_specs=[row, vec, vec,
                  pl.BlockSpec((1, D, FS), lambda i, k: (k, 0, 0)),
                  pl.BlockSpec((1, D, FS), lambda i, k: (k, 0, 0)),
                  pl.BlockSpec((1, FS, D), lambda i, k: (k, 0, 0))],
        out_specs=[row, row, act, act, act, row],
        out_shape=[jax.ShapeDtypeStruct((T, D), F32), jax.ShapeDtypeStruct((T, D), BF16),
                   jax.ShapeDtypeStruct((S, T, FS), F32), jax.ShapeDtypeStruct((S, T, FS), F32),
                   jax.ShapeDtypeStruct((S, T, FS), BF16), jax.ShapeDtypeStruct((T, D), F32)],
        scratch_shapes=[pltpu.VMEM((tm, D), BF16), pltpu.VMEM((tm, D), F32)],
        compiler_params=_params(2, VMEM_LIMIT),
    )(x, g_pre, g_post, wg, wu, wd)


def _ffn_bwd_act(dh, f, g_post, wd, g_act, u_act, name):
    T, D = dh.shape
    S, FS, _ = wd.shape
    tm = min(ROW_BLOCK, T)

    def body(dh_ref, f_ref, gpost_ref, wd_ref, g_ref, u_ref, dgp_ref, dup_ref, df_ref, dgain_ref, df_s):
        i, k = pl.program_id(0), pl.program_id(1)

        @pl.when(k == 0)
        def _():
            df, dgain = _rms_bwd(f_ref[...], gpost_ref[...], 0.5 * dh_ref[...])
            df_s[...] = df.astype(BF16)
            df_ref[...] = df_s[...]
            _accumulate(dgain_ref, dgain, i == 0)

        da = _dot(df_s[...], wd_ref[0], NT)
        g = g_ref[0]
        s = _sigmoid(g)
        dup_ref[0] = (da * (g * s)).astype(BF16)
        dgp_ref[0] = (da * u_ref[0] * (s * (1.0 + g * (1.0 - s)))).astype(BF16)

    row = pl.BlockSpec((tm, D), lambda i, k: (i, 0))
    vec = pl.BlockSpec((1, D), lambda i, k: (0, 0))
    act = pl.BlockSpec((1, tm, FS), lambda i, k: (k, i, 0))
    return pl.pallas_call(
        body, name=name, grid=(T // tm, S),
        in_specs=[row, row, vec, pl.BlockSpec((1, FS, D), lambda i, k: (k, 0, 0)), act, act],
        out_specs=[act, act, row, vec],
        out_shape=[jax.ShapeDtypeStruct((S, T, FS), BF16), jax.ShapeDtypeStruct((S, T, FS), BF16),
                   jax.ShapeDtypeStruct((T, D), BF16), jax.ShapeDtypeStruct((1, D), F32)],
        scratch_shapes=[pltpu.VMEM((tm, D), BF16)],
        compiler_params=_params(2, VMEM_LIMIT),
    )(dh, f, g_post, wd, g_act, u_act)


def _proj_bwd(dys, ws, x, g_pre, dh, name):
    T, D = x.shape
    n = len(dys)
    S, _, N = ws[0].shape
    tm = min(ROW_BLOCK, T)

    def body(*refs):
        dy_refs, w_refs = refs[:n], refs[n:2 * n]
        x_ref, gpre_ref, dh_ref, dx_ref, dgain_ref, acc_s = refs[2 * n:]
        i, k = pl.program_id(0), pl.program_id(1)
        part = _dot(dy_refs[0][0], w_refs[0][0], NT)
        for dy_ref, w_ref in zip(dy_refs[1:], w_refs[1:]):
            part += _dot(dy_ref[0], w_ref[0], NT)
        _accumulate(acc_s, part, k == 0)

        @pl.when(k == S - 1)
        def _():
            dx, dgain = _rms_bwd(x_ref[...], gpre_ref[...], acc_s[...])
            dx_ref[...] = dh_ref[...] + dx
            _accumulate(dgain_ref, dgain, i == 0)

    row = pl.BlockSpec((tm, D), lambda i, k: (i, 0))
    vec = pl.BlockSpec((1, D), lambda i, k: (0, 0))
    return pl.pallas_call(
        body, name=name, grid=(T // tm, S),
        in_specs=[pl.BlockSpec((1, tm, N), lambda i, k: (k, i, 0))] * n
        + [pl.BlockSpec((1, D, N), lambda i, k: (k, 0, 0))] * n + [row, vec, row],
        out_specs=[row, vec],
        out_shape=[jax.ShapeDtypeStruct((T, D), F32), jax.ShapeDtypeStruct((1, D), F32)],
        scratch_shapes=[pltpu.VMEM((tm, D), F32)],
        compiler_params=_params(2, VMEM_LIMIT),
    )(*dys, *ws, x, g_pre, dh)


def _mm_tn(a, b, bm, name):
    ga, T, M = a.shape
    gb, _, N = b.shape
    G = max(ga, gb)

    def body(a_ref, b_ref, o_ref):
        o_ref[0] = _dot(a_ref[0].astype(BF16), b_ref[0].astype(BF16), TN)

    return pl.pallas_call(
        body, name=name, grid=(G, M // bm),
        in_specs=[pl.BlockSpec((1, T, bm), (lambda g, m: (g, 0, m)) if ga > 1 else (lambda g, m: (0, 0, m))),
                  pl.BlockSpec((1, T, N), (lambda g, m: (g, 0, 0)) if gb > 1 else (lambda g, m: (0, 0, 0)))],
        out_specs=pl.BlockSpec((1, bm, N), lambda g, m: (g, m, 0)),
        out_shape=jax.ShapeDtypeStruct((G, M, N), F32),
        compiler_params=_params(2, VMEM_LIMIT),
    )(a, b)


def _norm_proj(x, g_pre, w, name):
    T, D = x.shape
    S, _, N = w.shape
    tm = min(ROW_BLOCK, T)

    def body(x_ref, g_ref, w_ref, o_ref, xn_ref, xn_s):
        @pl.when(pl.program_id(1) == 0)
        def _():
            xn_s[...] = _rms_fwd(x_ref[...], g_ref[...]).astype(BF16)
            xn_ref[...] = xn_s[...]

        o_ref[0] = _dot(xn_s[...], w_ref[0]).astype(BF16)

    row = pl.BlockSpec((tm, D), lambda i, k: (i, 0))
    return pl.pallas_call(
        body, name=name, grid=(T // tm, S),
        in_specs=[row, pl.BlockSpec((1, D), lambda i, k: (0, 0)), pl.BlockSpec((1, D, N), lambda i, k: (k, 0, 0))],
        out_specs=[pl.BlockSpec((1, tm, N), lambda i, k: (k, i, 0)), row],
        out_shape=[jax.ShapeDtypeStruct((S, T, N), BF16), jax.ShapeDtypeStruct((T, D), BF16)],
        scratch_shapes=[pltpu.VMEM((tm, D), BF16)],
        compiler_params=_params(2, VMEM_LIMIT),
    )(x, g_pre, w)


def _mix_out_fwd(h, o_all, g_sb, g_ch, w_out, g_post, name):
    T, D = h.shape
    W = g_sb.shape[1]
    tm = min(ROW_BLOCK, T)

    def body(h_ref, o_ref, gsb_ref, gch_ref, w_ref, gpost_ref, h2_ref, mixed_ref, mo_ref):
        mixed_ref[:, :W] = _rms_fwd(o_ref[:, :W], gsb_ref[...]).astype(BF16)
        mixed_ref[:, W:] = _rms_fwd(o_ref[:, W:], gch_ref[...]).astype(BF16)
        mo = _dot(mixed_ref[...], w_ref[...])
        mo_ref[...] = mo
        h2_ref[...] = h_ref[...] + _rms_fwd(mo, gpost_ref[...])

    row = pl.BlockSpec((tm, D), lambda i: (i, 0))
    half = pl.BlockSpec((1, W), lambda i: (0, 0))
    return pl.pallas_call(
        body, name=name, grid=(T // tm,),
        in_specs=[row, row, half, half, pl.BlockSpec((D, D), lambda i: (0, 0)), pl.BlockSpec((1, D), lambda i: (0, 0))],
        out_specs=[row, row, row],
        out_shape=[jax.ShapeDtypeStruct((T, D), F32), jax.ShapeDtypeStruct((T, D), BF16),
                   jax.ShapeDtypeStruct((T, D), F32)],
        compiler_params=_params(1, VMEM_LIMIT),
    )(h, o_all, g_sb, g_ch, w_out, g_post)


def _mix_out_bwd(dh, mo, g_post, w_out, o_all, g_sb, g_ch, name):
    T, D = dh.shape
    W = g_sb.shape[1]
    tm = min(ROW_BLOCK, T)

    def body(dh_ref, mo_ref, gpost_ref, w_ref, o_ref, gsb_ref, gch_ref,
             dmo_ref, do_ref, dgpost_ref, dgsb_ref, dgch_ref):
        first = pl.program_id(0) == 0
        dmo, dgpost = _rms_bwd(mo_ref[...], gpost_ref[...], dh_ref[...])
        dmo_ref[...] = dmo.astype(BF16)
        dmix = _dot(dmo_ref[...], w_ref[...], NT)
        doa, dgsb = _rms_bwd(o_ref[:, :W], gsb_ref[...], dmix[:, :W])
        dob, dgch = _rms_bwd(o_ref[:, W:], gch_ref[...], dmix[:, W:])
        do_ref[:, :W] = doa
        do_ref[:, W:] = dob
        _accumulate(dgpost_ref, dgpost, first)
        _accumulate(dgsb_ref, dgsb, first)
        _accumulate(dgch_ref, dgch, first)

    row = pl.BlockSpec((tm, D), lambda i: (i, 0))
    vec = pl.BlockSpec((1, D), lambda i: (0, 0))
    half = pl.BlockSpec((1, W), lambda i: (0, 0))
    return pl.pallas_call(
        body, name=name, grid=(T // tm,),
        in_specs=[row, row, vec, pl.BlockSpec((D, D), lambda i: (0, 0)), row, half, half],
        out_specs=[row, row, vec, half, half],
        out_shape=[jax.ShapeDtypeStruct((T, D), BF16), jax.ShapeDtypeStruct((T, D), F32),
                   jax.ShapeDtypeStruct((1, D), F32), jax.ShapeDtypeStruct((1, W), F32),
                   jax.ShapeDtypeStruct((1, W), F32)],
        compiler_params=_params(1, VMEM_LIMIT),
    )(dh, mo, g_post, w_out, o_all, g_sb, g_ch)


def _ple_loss(h, p, target, w_proj, w_gate, g_post, name):
    T, D = h.shape
    P = p.shape[1]
    S = N_CHIPS
    C = D // S
    tm = min(ROW_BLOCK, T)

    def body(h_ref, p_ref, t_ref, wp_ref, wg_ref, g_ref, loss_ref, dh_ref, dproj_ref, dgate_ref, dgain_ref):
        first = pl.program_id(0) == 0
        h3 = h_ref[...]
        proj = _dot(p_ref[...].astype(BF16), wp_ref[...])
        s = _sigmoid(_dot(h3.astype(BF16), wg_ref[...]))
        e = proj * s
        diff = h3 + _rms_fwd(e, g_ref[...]) - t_ref[...]
        part = 0.5 * jnp.sum(jnp.mean(diff * diff, axis=-1, keepdims=True), axis=0, keepdims=True)
        _accumulate(loss_ref, jnp.broadcast_to(part, loss_ref.shape), first)
        dy = diff * (1.0 / D)
        de, dgain = _rms_bwd(e, g_ref[...], dy)
        _accumulate(dgain_ref, dgain, first)
        dproj = (de * s).astype(BF16)
        for j in range(S):
            dproj_ref[j] = dproj[:, j * C:(j + 1) * C]
        dgate_ref[...] = (de * proj * s * (1.0 - s)).astype(BF16)
        dh_ref[...] = dy + _dot(dgate_ref[...], wg_ref[...], NT)

    row = pl.BlockSpec((tm, D), lambda i: (i, 0))
    vec = pl.BlockSpec((1, D), lambda i: (0, 0))
    return pl.pallas_call(
        body, name=name, grid=(T // tm,),
        in_specs=[row, pl.BlockSpec((tm, P), lambda i: (i, 0)), row,
                  pl.BlockSpec((P, D), lambda i: (0, 0)), pl.BlockSpec((D, D), lambda i: (0, 0)), vec],
        out_specs=[pl.BlockSpec((8, 128), lambda i: (0, 0)), row,
                   pl.BlockSpec((S, tm, C), lambda i: (0, i, 0)), row, vec],
        out_shape=[jax.ShapeDtypeStruct((8, 128), F32), jax.ShapeDtypeStruct((T, D), F32),
                   jax.ShapeDtypeStruct((S, T, C), BF16), jax.ShapeDtypeStruct((T, D), BF16),
                   jax.ShapeDtypeStruct((1, D), F32)],
        compiler_params=_params(1, VMEM_LIMIT),
    )(h, p, target, w_proj, w_gate, g_post)


def _sb_scores(q, kj, diag, rows, cols):
    z = _dot(q, kj, NT) * ATT_SCALE
    sp = jnp.maximum(z, 0.0) + jnp.log(1.0 + jnp.exp(-jnp.abs(z)))
    mask = jnp.logical_or(jnp.logical_not(diag), cols < rows)
    lf = jnp.where(mask, -sp, 0.0)
    return z, sp, mask, lf


def _tri(cmp):
    r = lax.broadcasted_iota(jnp.int32, (SB_BLOCK, SB_BLOCK), 0)
    c = lax.broadcasted_iota(jnp.int32, (SB_BLOCK, SB_BLOCK), 1)
    return jnp.where(cmp(r, c), 1.0, 0.0).astype(BF16)


def _cum(x, tri):
    hi, lo = _split2(x)
    return _dot(hi, tri) + _dot(lo, tri)


def _sb_fwd(qkvh, name):
    _, T, Dh = qkvh.shape
    B = SB_BLOCK

    def body(q_ref, k_ref, v_ref, o_ref, tot_ref):
        i = pl.program_id(1)
        q = q_ref[0]
        rows = lax.broadcasted_iota(jnp.int32, (B, B), 0)
        cols = lax.broadcasted_iota(jnp.int32, (B, B), 1)
        after = _tri(lambda r, c: r > c)

        def step(jj, carry):
            run, acc = carry
            j = i - jj
            at = pl.ds(pl.multiple_of(j * B, B), B)
            z, sp, mask, lf = _sb_scores(q, k_ref[0, at, :], jj == 0, rows, cols)
            log_a = (z - sp) + _cum(lf, after) + run
            a = jnp.where(mask, jnp.exp(jnp.where(mask, log_a, 0.0)), 0.0)
            acc = acc + _dot(a.astype(BF16), v_ref[0, at, :])
            return run + jnp.sum(lf, axis=1, keepdims=True), acc

        run, acc = lax.fori_loop(0, i + 1, step, (jnp.zeros((B, 1), F32), jnp.zeros((B, Dh), F32)))
        o_ref[0] = acc
        tot_ref[0] = run

    full = lambda off: pl.BlockSpec((1, T, Dh), lambda h, i: (h + off, 0, 0))
    return pl.pallas_call(
        body, name=name, grid=(N_HEADS, T // B),
        in_specs=[pl.BlockSpec((1, B, Dh), lambda h, i: (h, i, 0)), full(N_HEADS), full(2 * N_HEADS)],
        out_specs=[pl.BlockSpec((1, B, Dh), lambda h, i: (h, i, 0)), pl.BlockSpec((1, B, 1), lambda h, i: (h, i, 0))],
        out_shape=[jax.ShapeDtypeStruct((N_HEADS, T, Dh), F32), jax.ShapeDtypeStruct((N_HEADS, T, 1), F32)],
        compiler_params=_params(2),
    )(qkvh, qkvh, qkvh)


def _sb_bwd(qkvh, do, tot, name):
    _, T, Dh = qkvh.shape
    B = SB_BLOCK

    def body(q_ref, k_ref, v_ref, do_ref, tot_ref, dq_ref, dk_ref, dv_ref):
        i = pl.program_id(1)

        @pl.when(i == 0)
        def _():
            dk_ref[...] = jnp.zeros_like(dk_ref)
            dv_ref[...] = jnp.zeros_like(dv_ref)

        q = q_ref[0]
        dob = do_ref[0].astype(BF16)
        tot_i = tot_ref[0]
        rows = lax.broadcasted_iota(jnp.int32, (B, B), 0)
        cols = lax.broadcasted_iota(jnp.int32, (B, B), 1)
        upto = _tri(lambda r, c: r <= c)
        below = _tri(lambda r, c: r < c)

        def step(j, carry):
            pre_lf, pre_g, dq = carry
            at = pl.ds(pl.multiple_of(j * B, B), B)
            kj, vj = k_ref[0, at, :], v_ref[0, at, :]
            z, sp, mask, lf = _sb_scores(q, kj, j == i, rows, cols)
            later = tot_i - pre_lf - _cum(lf, upto)
            a = jnp.where(mask, jnp.exp(jnp.where(mask, (z - sp) + later, 0.0)), 0.0)
            g = a * _dot(dob, vj, NT)
            g_before = pre_g + _cum(g, below)
            dz = jnp.where(mask, g * jnp.exp(-sp) - jnp.exp(z - sp) * g_before, 0.0) * ATT_SCALE
            dzb = dz.astype(BF16)
            dk_ref[0, at, :] += _dot(dzb, q, TN)
            dv_ref[0, at, :] += _dot(a.astype(BF16), dob, TN)
            return (pre_lf + jnp.sum(lf, axis=1, keepdims=True), pre_g + jnp.sum(g, axis=1, keepdims=True),
                    dq + _dot(dzb, kj))

        zero = jnp.zeros((B, 1), F32)
        _, _, dq = lax.fori_loop(0, i + 1, step, (zero, zero, jnp.zeros((B, Dh), F32)))
        dq_ref[0] = dq

    blk = lambda off: pl.BlockSpec((1, B, Dh), lambda h, i: (h + off, i, 0))
    full = lambda off: pl.BlockSpec((1, T, Dh), lambda h, i: (h + off, 0, 0))
    out = jax.ShapeDtypeStruct((N_HEADS, T, Dh), F32)
    return pl.pallas_call(
        body, name=name, grid=(N_HEADS, T // B),
        in_specs=[blk(0), full(N_HEADS), full(2 * N_HEADS), blk(0), pl.BlockSpec((1, B, 1), lambda h, i: (h, i, 0))],
        out_specs=[blk(0), full(0), full(0)],
        out_shape=[out, out, out],
        compiler_params=_params(2),
    )(qkvh, qkvh, qkvh, do, tot)


def _rel_onehot(i, transposed):
    shape = (BAND, N_REL_PAD) if transposed else (N_REL_PAD, BAND)
    j = lax.broadcasted_iota(jnp.int32, shape, 0 if transposed else 1)
    r = lax.broadcasted_iota(jnp.int32, shape, 1 if transposed else 0)
    idx = jnp.clip(i + PAD - j, -REL_CLIP, REL_CLIP) + REL_CLIP
    return jnp.where(idx == r, 1.0, 0.0).astype(BF16)


def _bias_table(rel_bias_pad, name):
    def body(rb_ref, o_ref):
        onehot = _rel_onehot(pl.program_id(0), False)
        rb = rb_ref[...]
        hi, lo = _split2(rb)
        lo2 = (rb - hi.astype(F32) - lo.astype(F32)).astype(BF16)
        o_ref[0] = _dot(hi, onehot) + _dot(lo, onehot) + _dot(lo2, onehot)

    return pl.pallas_call(
        body, name=name, grid=(CHUNK,),
        in_specs=[pl.BlockSpec((N_HEADS, N_REL_PAD), lambda i: (0, 0))],
        out_specs=pl.BlockSpec((1, N_HEADS, BAND), lambda i: (i, 0, 0)),
        out_shape=jax.ShapeDtypeStruct((CHUNK, N_HEADS, BAND), F32),
        compiler_params=_params(1),
    )(rel_bias_pad)


def _bias_grad(dbias_t, name):
    def body(d_ref, o_ref):
        onehot = _rel_onehot(pl.program_id(0), True)
        hi, lo = _split2(d_ref[0])
        _accumulate(o_ref, _dot(hi, onehot) + _dot(lo, onehot), pl.program_id(0) == 0)

    return pl.pallas_call(
        body, name=name, grid=(CHUNK,),
        in_specs=[pl.BlockSpec((1, N_HEADS, BAND), lambda i: (i, 0, 0))],
        out_specs=pl.BlockSpec((N_HEADS, N_REL_PAD), lambda i: (0, 0)),
        out_shape=jax.ShapeDtypeStruct((N_HEADS, N_REL_PAD), F32),
        compiler_params=_params(1),
    )(dbias_t)


def _ch_probs(q, kw, bias, n):
    z = _dot(q, kw, NT) * ATT_SCALE + bias
    slot = lax.broadcasted_iota(jnp.int32, (CHUNK, BAND), 1) // CHUNK
    z = jnp.where(n + slot - LOOKBACK >= 0, z, NEG_INF)
    e = jnp.exp(z - jnp.max(z, axis=-1, keepdims=True))
    return e / jnp.sum(e, axis=-1, keepdims=True)


def _ch_fwd(qkvh, kpad, vpad, bias, name):
    _, T, Dh = qkvh.shape

    def body(q_ref, k_ref, v_ref, b_ref, o_ref):
        n = pl.program_id(1)
        win = pl.ds(pl.multiple_of(n * CHUNK, CHUNK), BAND)
        p = _ch_probs(q_ref[0], k_ref[0, win, :], b_ref[0], n)
        o_ref[0] = _dot(p.astype(BF16), v_ref[0, win, :])

    full = pl.BlockSpec((1, PAD + T, Dh), lambda h, n: (h, 0, 0))
    return pl.pallas_call(
        body, name=name, grid=(N_HEADS, T // CHUNK),
        in_specs=[pl.BlockSpec((1, CHUNK, Dh), lambda h, n: (h + 3 * N_HEADS, n, 0)), full, full,
                  pl.BlockSpec((1, CHUNK, BAND), lambda h, n: (h, 0, 0))],
        out_specs=pl.BlockSpec((1, CHUNK, Dh), lambda h, n: (h, n, 0)),
        out_shape=jax.ShapeDtypeStruct((N_HEADS, T, Dh), F32),
        compiler_params=_params(2),
    )(qkvh, kpad, vpad, bias)


def _ch_bwd(qkvh, kpad, vpad, bias, do, name):
    _, T, Dh = qkvh.shape

    def body(q_ref, k_ref, v_ref, b_ref, do_ref, dq_ref, dk_ref, dv_ref, db_ref):
        n = pl.program_id(1)

        @pl.when(n == 0)
        def _():
            dk_ref[...] = jnp.zeros_like(dk_ref)
            dv_ref[...] = jnp.zeros_like(dv_ref)
            db_ref[...] = jnp.zeros_like(db_ref)

        win = pl.ds(pl.multiple_of(n * CHUNK, CHUNK), BAND)
        q, kw, vw = q_ref[0], k_ref[0, win, :], v_ref[0, win, :]
        p = _ch_probs(q, kw, b_ref[0], n)
        dob = do_ref[0].astype(BF16)
        dp = _dot(dob, vw, NT)
        dz = p * (dp - jnp.sum(dp * p, axis=-1, keepdims=True))
        db_ref[0] += dz
        dzb = (dz * ATT_SCALE).astype(BF16)
        dq_ref[0] = _dot(dzb, kw)
        dk_ref[0, win, :] += _dot(dzb, q, TN)
        dv_ref[0, win, :] += _dot(p.astype(BF16), dob, TN)

    full = pl.BlockSpec((1, PAD + T, Dh), lambda h, n: (h, 0, 0))
    blk = pl.BlockSpec((1, CHUNK, Dh), lambda h, n: (h, n, 0))
    tab = pl.BlockSpec((1, CHUNK, BAND), lambda h, n: (h, 0, 0))
    padded = jax.ShapeDtypeStruct((N_HEADS, PAD + T, Dh), F32)
    return pl.pallas_call(
        body, name=name, grid=(N_HEADS, T // CHUNK),
        in_specs=[pl.BlockSpec((1, CHUNK, Dh), lambda h, n: (h + 3 * N_HEADS, n, 0)), full, full, tab, blk],
        out_specs=[blk, full, full, tab],
        out_shape=[jax.ShapeDtypeStruct((N_HEADS, T, Dh), F32), padded, padded,
                   jax.ShapeDtypeStruct((N_HEADS, CHUNK, BAND), F32)],
        compiler_params=_params(2),
    )(qkvh, kpad, vpad, bias, do)


def _rows_split(a, parts):
    return a.reshape(a.shape[:-2] + (parts, a.shape[-2] // parts, a.shape[-1]))


def _cast_bf16(ws, name):
    parts = 4
    ws = [_rows_split(w, parts) for w in ws]

    def body(*refs):
        n = len(refs) // 2
        for src, dst in zip(refs[:n], refs[n:]):
            dst[...] = src[...].astype(BF16)

    spec = lambda w: pl.BlockSpec((1,) + w.shape[1:], lambda i: (i, 0, 0))
    outs = pl.pallas_call(
        body, name=name, grid=(parts,),
        in_specs=[spec(w) for w in ws], out_specs=[spec(w) for w in ws],
        out_shape=[jax.ShapeDtypeStruct(w.shape, BF16) for w in ws],
        compiler_params=_params(1, VMEM_LIMIT),
    )(*ws)
    return [o.reshape(o.shape[0] * o.shape[1], o.shape[2]) for o in outs]


def _pair_add(c, mine, got, name):
    parts = 2
    mine = [_rows_split(m, parts) for m in mine]
    got = [_rows_split(g, parts) for g in got]
    n = len(mine)

    def body(c_ref, *refs):
        for a, b, o in zip(refs[:n], refs[n:2 * n], refs[2 * n:]):
            o[0, 0] = (a[0, 0, 0] + b[0, 0]).astype(BF16)

    def mine_spec(m):
        return pl.BlockSpec((1, 1, 1) + m.shape[3:], lambda j, r, c_ref: (j, c_ref[0], r, 0, 0))

    def got_spec(g):
        return pl.BlockSpec((1, 1) + g.shape[2:], lambda j, r, c_ref: (j, r, 0, 0))

    outs = pl.pallas_call(
        body, name=name,
        grid_spec=pltpu.PrefetchScalarGridSpec(
            num_scalar_prefetch=1, grid=(N_CHIPS, parts),
            in_specs=[mine_spec(m) for m in mine] + [got_spec(g) for g in got],
            out_specs=[got_spec(g) for g in got]),
        out_shape=[jax.ShapeDtypeStruct(g.shape, BF16) for g in got],
        compiler_params=_params(2, VMEM_LIMIT),
    )(c, *mine, *got)
    return [o.reshape(o.shape[0], o.shape[1] * o.shape[2], o.shape[3]) for o in outs]


def _chip_add(parts_in, name):
    parts = 2
    xs = [_rows_split(x, parts) for x in parts_in]

    def body(*refs):
        n = len(refs) // 2
        for x, o in zip(refs[:n], refs[n:]):
            acc = x[0, 0].astype(F32)
            for m in range(1, N_CHIPS):
                acc = acc + x[m, 0].astype(F32)
            o[0] = acc

    outs = pl.pallas_call(
        body, name=name, grid=(parts,),
        in_specs=[pl.BlockSpec((N_CHIPS, 1) + x.shape[2:], lambda r: (0, r, 0, 0)) for x in xs],
        out_specs=[pl.BlockSpec((1,) + x.shape[2:], lambda r: (r, 0, 0)) for x in xs],
        out_shape=[jax.ShapeDtypeStruct(x.shape[1:], F32) for x in xs],
        compiler_params=_params(1, VMEM_LIMIT),
    )(*xs)
    return [o.reshape(o.shape[0] * o.shape[1], o.shape[2]) for o in outs]


def _adamw_math(w, g, m, v):
    m = ADAM_B1 * m + (1.0 - ADAM_B1) * g
    v = ADAM_B2 * v + (1.0 - ADAM_B2) * (g * g)
    m_hat = m / (1.0 - ADAM_B1 ** ADAM_STEP)
    v_hat = v / (1.0 - ADAM_B2 ** ADAM_STEP)
    delta = -ADAM_LR * (m_hat / (jnp.sqrt(v_hat) + ADAM_EPS) + ADAM_WD * w)
    return delta, m, v


def _adamw(ws, gs, ms, vs, parts, name):
    n = len(ws)
    flat = [_rows_split(a, parts) for a in (*ws, *gs, *ms, *vs)]

    def body(*refs):
        ins, outs = refs[:4 * n], refs[4 * n:]
        for k in range(n):
            d, m, v = _adamw_math(ins[k][...], ins[n + k][...], ins[2 * n + k][...], ins[3 * n + k][...])
            outs[k][...] = d
            outs[n + k][...] = m
            outs[2 * n + k][...] = v

    spec = lambda a: pl.BlockSpec((1,) + a.shape[1:], lambda i: (i, 0, 0))
    outs = pl.pallas_call(
        body, name=name, grid=(parts,),
        in_specs=[spec(a) for a in flat], out_specs=[spec(a) for a in flat[:n]] * 3,
        out_shape=[jax.ShapeDtypeStruct(a.shape, F32) for a in flat[:n]] * 3,
        compiler_params=_params(1, VMEM_LIMIT),
    )(*flat)
    outs = [o.reshape(o.shape[0] * o.shape[1], o.shape[2]) for o in outs]
    return outs[:n], outs[n:2 * n], outs[2 * n:]


def _place():
    x, y, c = lax.axis_index("x"), lax.axis_index("y"), lax.axis_index("c")
    others = [(1 - x, y), (x, 1 - y), (1 - x, 1 - y)]
    return x, y, c, others


ANY = pl.BlockSpec(memory_space=pl.ANY)


def _gather_weights(shards, name):
    n = len(shards)
    shards = [_rows_split(s, 2) for s in shards]

    def body(*refs):
        src, dst = refs[:n], refs[n:2 * n]
        send_ici, recv_ici, send_d2d, recv_d2d, local_sem = refs[2 * n:]
        x, y, c, others = _place()
        me = 2 * x + y
        sibling = (x, y, 1 - c)

        local = [pltpu.make_async_copy(src[a], dst[a].at[me], local_sem.at[a]) for a in range(n)]
        for cp in local:
            cp.start()

        def ici(a, k, chip):
            return pltpu.make_async_remote_copy(
                src_ref=src[a].at[c], dst_ref=dst[a].at[me, c],
                send_sem=send_ici.at[a * 3 + k], recv_sem=recv_ici.at[a * 3 + k],
                device_id=(*chip, c), device_id_type=MESH)

        def landed(a, k, chip):
            return pltpu.make_async_remote_copy(
                src_ref=src[a].at[c], dst_ref=dst[a].at[2 * chip[0] + chip[1], c],
                send_sem=send_ici.at[a * 3 + k], recv_sem=recv_ici.at[a * 3 + k],
                device_id=(*chip, c), device_id_type=MESH)

        def d2d(a, k, chip, half):
            slab = dst[a].at[2 * chip[0] + chip[1], half]
            return pltpu.make_async_remote_copy(
                src_ref=slab, dst_ref=slab, send_sem=send_d2d.at[a * 3 + k], recv_sem=recv_d2d.at[a * 3 + k],
                device_id=sibling, device_id_type=MESH)

        for a in range(n):
            for k, chip in enumerate(others):
                ici(a, k, chip).start()
        for a in range(n):
            for k, chip in enumerate(others):
                landed(a, k, chip).wait_recv()
                d2d(a, k, chip, c).start()
        for a in range(n):
            for k, chip in enumerate(others):
                d2d(a, k, chip, 1 - c).wait_recv()
        for a in range(n):
            for k, chip in enumerate(others):
                ici(a, k, chip).wait_send()
                d2d(a, k, chip, c).wait_send()
        for cp in local:
            cp.wait()

    outs = pl.pallas_call(
        body, name=name,
        in_specs=[ANY] * n, out_specs=[ANY] * n,
        out_shape=[jax.ShapeDtypeStruct((N_CHIPS,) + s.shape, BF16) for s in shards],
        scratch_shapes=[pltpu.SemaphoreType.DMA((3 * n,))] * 4 + [pltpu.SemaphoreType.DMA((n,))],
    )(*shards)
    return [o.reshape(N_CHIPS, o.shape[1] * o.shape[2], o.shape[3]) for o in outs]


def _pair_swap(grads, name):
    n = len(grads)

    def body(*refs):
        src, dst = refs[:n], refs[n:2 * n]
        send_sem, recv_sem = refs[2 * n:]
        x, y, c, _ = _place()
        copies = [pltpu.make_async_remote_copy(
            src_ref=src[a].at[:, 1 - c], dst_ref=dst[a], send_sem=send_sem.at[a], recv_sem=recv_sem.at[a],
            device_id=(x, y, 1 - c), device_id_type=MESH) for a in range(n)]
        for cp in copies:
            cp.start()
        for cp in copies:
            cp.wait()

    return pl.pallas_call(
        body, name=name,
        in_specs=[ANY] * n, out_specs=[ANY] * n,
        out_shape=[jax.ShapeDtypeStruct((N_CHIPS,) + g.shape[2:], F32) for g in grads],
        scratch_shapes=[pltpu.SemaphoreType.DMA((n,))] * 2,
    )(*grads)


def _chip_scatter(partials, name):
    n = len(partials)

    def body(*refs):
        src, dst = refs[:n], refs[n:2 * n]
        send_sem, recv_sem, local_sem = refs[2 * n:]
        x, y, c, others = _place()
        me = 2 * x + y
        local = [pltpu.make_async_copy(src[a].at[me], dst[a].at[me], local_sem.at[a]) for a in range(n)]
        for cp in local:
            cp.start()
        copies = []
        for a in range(n):
            for k, chip in enumerate(others):
                copies.append(pltpu.make_async_remote_copy(
                    src_ref=src[a].at[2 * chip[0] + chip[1]], dst_ref=dst[a].at[me],
                    send_sem=send_sem.at[a * 3 + k], recv_sem=recv_sem.at[a * 3 + k],
                    device_id=(*chip, c), device_id_type=MESH))
        for cp in copies:
            cp.start()
        for cp in copies:
            cp.wait()
        for cp in local:
            cp.wait()

    return pl.pallas_call(
        body, name=name,
        in_specs=[ANY] * n, out_specs=[ANY] * n,
        out_shape=[jax.ShapeDtypeStruct(p.shape, BF16) for p in partials],
        scratch_shapes=[pltpu.SemaphoreType.DMA((3 * n,))] * 2 + [pltpu.SemaphoreType.DMA((n,))],
    )(*partials)


def _pair_join(halves, name):
    n = len(halves)

    def body(*refs):
        src, dst = refs[:n], refs[n:2 * n]
        send_sem, recv_sem, local_sem = refs[2 * n:]
        x, y, c, _ = _place()
        local = [pltpu.make_async_copy(src[a], dst[a].at[c], local_sem.at[a]) for a in range(n)]
        for cp in local:
            cp.start()
        copies = [pltpu.make_async_remote_copy(
            src_ref=src[a], dst_ref=dst[a].at[c], send_sem=send_sem.at[a], recv_sem=recv_sem.at[a],
            device_id=(x, y, 1 - c), device_id_type=MESH) for a in range(n)]
        for cp in copies:
            cp.start()
        for cp in copies:
            cp.wait()
        for cp in local:
            cp.wait()

    outs = pl.pallas_call(
        body, name=name,
        in_specs=[ANY] * n, out_specs=[ANY] * n,
        out_shape=[jax.ShapeDtypeStruct((2,) + h.shape, F32) for h in halves],
        scratch_shapes=[pltpu.SemaphoreType.DMA((n,))] * 3,
    )(*halves)
    return [o.reshape(2 * o.shape[1], o.shape[2]) for o in outs]


def _all_sum_small(v, name):
    R, C = v.shape
    n_dev = 8

    def body(v_ref, o_ref, buf, send_sem, recv_sem):
        x, y, c, _ = _place()
        me = 4 * x + 2 * y + c
        buf[me] = v_ref[...]
        copies = []
        for k in range(1, n_dev):
            peer = (x ^ (k >> 2), y ^ ((k >> 1) & 1), c ^ (k & 1))
            copies.append(pltpu.make_async_remote_copy(
                src_ref=v_ref, dst_ref=buf.at[me], send_sem=send_sem.at[k - 1], recv_sem=recv_sem.at[k - 1],
                device_id=peer, device_id_type=MESH))
        for cp in copies:
            cp.start()
        for cp in copies:
            cp.wait()
        acc = buf[0]
        for m in range(1, n_dev):
            acc = acc + buf[m]
        o_ref[...] = acc

    return pl.pallas_call(
        body, name=name,
        in_specs=[pl.BlockSpec(memory_space=pltpu.VMEM)], out_specs=pl.BlockSpec(memory_space=pltpu.VMEM),
        out_shape=jax.ShapeDtypeStruct((R, C), F32),
        scratch_shapes=[pltpu.VMEM((n_dev, R, C), F32), pltpu.SemaphoreType.DMA((n_dev - 1,)),
                        pltpu.SemaphoreType.DMA((n_dev - 1,))],
    )(v)


def _heads_major(qkv):
    S, T, N = qkv.shape
    per = N // HEAD_DIM
    return qkv.reshape(S, T, per, HEAD_DIM).transpose(0, 2, 1, 3).reshape(S * per, T, HEAD_DIM)


def _heads_minor(t, S):
    n, T, _ = t.shape
    per = n // S
    return t.reshape(S, per, T, HEAD_DIM).transpose(0, 2, 1, 3).reshape(S, T, per * HEAD_DIM)


def _local_step(x, p, target, gains, rel_bias, w):
    T, D = x.shape
    S = N_CHIPS
    pad3 = ((0, 0), (PAD, 0), (0, 0))

    h1, xn1, g1, u1, a1, f1 = _ffn_fwd(x, gains["ffn1_pre"], gains["ffn1_post"], w["ffn1_gate"], w["ffn1_up"],
                                       w["ffn1_down"], "ffn1_fwd")
    qkv, un = _norm_proj(h1, gains["mix_pre"], w["in"], "qkv_proj")
    qkvh = _heads_major(qkv)
    kpad = jnp.pad(qkvh[4 * N_HEADS:5 * N_HEADS], pad3)
    vpad = jnp.pad(qkvh[5 * N_HEADS:], pad3)
    rb_pad = jnp.pad(rel_bias, ((0, 0), (0, N_REL_PAD - N_REL)))
    bias = _bias_table(rb_pad, "bias_table").transpose(1, 0, 2)
    o_a, tot = _sb_fwd(qkvh, "sb_fwd")
    o_b = _ch_fwd(qkvh, kpad, vpad, bias, "ch_fwd")
    o_all = jnp.concatenate([o_a, o_b], axis=0).transpose(1, 0, 2).reshape(T, D)
    w_out = w["out"].reshape(D, D)
    h2, mixed, mo = _mix_out_fwd(h1, o_all, gains["out_sb"], gains["out_ch"], w_out, gains["mix_post"], "mix_out_fwd")
    h3, xn2, g2, u2, a2, f2 = _ffn_fwd(h2, gains["ffn2_pre"], gains["ffn2_post"], w["ffn2_gate"], w["ffn2_up"],
                                       w["ffn2_down"], "ffn2_fwd")
    w_ple_proj = w["ple_proj"].transpose(1, 0, 2).reshape(p.shape[1], D)
    w_ple_gate = w["ple_gate"].reshape(D, D)

    loss, dh3, dproj, dgate, dg_ple = _ple_loss(h3, p, target, w_ple_proj, w_ple_gate, gains["ple_post"], "ple_loss")
    gw, gg = {}, {"ple_post": dg_ple}
    gw["ple_proj"] = _mm_tn(p[None], dproj, p.shape[1], "dw_ple_proj")
    gw["ple_gate"] = _mm_tn(h3[None], dgate[None], 512, "dw_ple_gate").reshape(S, D // S, D)

    def ffn_bwd(tag, dh, x_in, xn, g_act, u_act, a_act, f):
        dgp, dup, df, gg[tag + "_post"] = _ffn_bwd_act(dh, f, gains[tag + "_post"], w[tag + "_down"], g_act, u_act,
                                                       tag + "_bwd_act")
        gw[tag + "_gate"] = _mm_tn(xn[None], dgp, 512, "dw_" + tag + "_gate")
        gw[tag + "_up"] = _mm_tn(xn[None], dup, 512, "dw_" + tag + "_up")
        gw[tag + "_down"] = _mm_tn(a_act, df[None], a_act.shape[2], "dw_" + tag + "_down")
        dx, gg[tag + "_pre"] = _proj_bwd([dgp, dup], [w[tag + "_gate"], w[tag + "_up"]], x_in, gains[tag + "_pre"], dh,
                                         tag + "_bwd_in")
        return dx

    dh2 = ffn_bwd("ffn2", dh3, h2, xn2, g2, u2, a2, f2)
    dmo, do_all, gg["mix_post"], gg["out_sb"], gg["out_ch"] = _mix_out_bwd(
        dh2, mo, gains["mix_post"], w_out, o_all, gains["out_sb"], gains["out_ch"], "mix_out_bwd")
    gw["out"] = _mm_tn(mixed[None], dmo[None], 512, "dw_out").reshape(S, D // S, D)
    do_h = do_all.reshape(T, 2 * N_HEADS, HEAD_DIM).transpose(1, 0, 2)
    dq_a, dk_a, dv_a = _sb_bwd(qkvh, do_h[:N_HEADS], tot, "sb_bwd")
    dq_b, dk_b, dv_b, dbias = _ch_bwd(qkvh, kpad, vpad, bias, do_h[N_HEADS:], "ch_bwd")
    g_rel = _bias_grad(dbias.transpose(1, 0, 2), "bias_grad")[:, :N_REL]
    dqkvh = jnp.concatenate([dq_a, dk_a, dv_a, dq_b, dk_b[:, PAD:], dv_b[:, PAD:]], axis=0).astype(BF16)
    dqkv = _heads_minor(dqkvh, S)
    gw["in"] = _mm_tn(un[None], dqkv, 512, "dw_in")
    dh1, gg["mix_pre"] = _proj_bwd([dqkv], [w["in"]], h1, gains["mix_pre"], dh2, "qkv_bwd_in")
    dx = ffn_bwd("ffn1", dh1, x, xn1, g1, u1, a1, f1)
    return loss, dx, gw, gg, g_rel


BIG = ["ffn1_gate", "ffn1_up", "ffn1_down", "in", "out", "ffn2_gate", "ffn2_up", "ffn2_down", "ple_proj", "ple_gate"]
GAINS = ["ffn1_pre", "ffn1_post", "mix_pre", "mix_post", "out_sb", "out_ch", "ffn2_pre", "ffn2_post", "ple_post"]
ORDER = ["g_ffn1_pre", "g_ffn1_post", "w_ffn1_gate", "w_ffn1_up", "w_ffn1_down", "g_mix_pre", "g_mix_post", "w_in",
         "g_out_sb", "g_out_ch", "rel_bias", "w_out", "g_ffn2_pre", "g_ffn2_post", "w_ffn2_gate", "w_ffn2_up",
         "w_ffn2_down", "w_ple_proj", "w_ple_gate", "g_ple_post"]


def kernel(x, p, g_ffn1_pre, g_ffn1_post, w_ffn1_gate, w_ffn1_up, w_ffn1_down, g_mix_pre, g_mix_post, w_in, g_out_sb, g_out_ch, rel_bias, w_out, g_ffn2_pre, g_ffn2_post, w_ffn2_gate, w_ffn2_up, w_ffn2_down, w_ple_proj, w_ple_gate, g_ple_post, loss_target, m_g_ffn1_pre, m_g_ffn1_post, m_w_ffn1_gate, m_w_ffn1_up, m_w_ffn1_down, m_g_mix_pre, m_g_mix_post, m_w_in, m_g_out_sb, m_g_out_ch, m_rel_bias, m_w_out, m_g_ffn2_pre, m_g_ffn2_post, m_w_ffn2_gate, m_w_ffn2_up, m_w_ffn2_down, m_w_ple_proj, m_w_ple_gate, m_g_ple_post, v_g_ffn1_pre, v_g_ffn1_post, v_w_ffn1_gate, v_w_ffn1_up, v_w_ffn1_down, v_g_mix_pre, v_g_mix_post, v_w_in, v_g_out_sb, v_g_out_ch, v_rel_bias, v_w_out, v_g_ffn2_pre, v_g_ffn2_post, v_w_ffn2_gate, v_w_ffn2_up, v_w_ffn2_down, v_w_ple_proj, v_w_ple_gate, v_g_ple_post):
    args = dict(locals())
    wts = {n: args[n][0] for n in ORDER}
    ms = {n: args["m_" + n][0] for n in ORDER}
    vs = {n: args["v_" + n][0] for n in ORDER}
    gains = {n: wts["g_" + n][None] for n in GAINS}

    shards = _cast_bf16([wts["w_" + n] for n in BIG], "cast_weights")
    full = dict(zip(BIG, _gather_weights(shards, "gather_weights")))

    loss, dx, gw, gg, g_rel = _local_step(x[0], p[0, 0], loss_target[0], gains, wts["rel_bias"], full)

    c_idx = lax.axis_index("c").astype(jnp.int32).reshape(1)
    mine = [gw[n].reshape(N_CHIPS, 2, gw[n].shape[1] // 2, gw[n].shape[2]) for n in BIG]
    got = _pair_swap(mine, "grad_pair_swap")
    groups = [[0, 1, 2], [5, 6, 7], [3, 4, 8, 9]]
    partial = [None] * len(BIG)
    for gi, idx in enumerate(groups):
        for k, o in zip(idx, _pair_add(c_idx, [mine[k] for k in idx], [got[k] for k in idx], "grad_pair_add%d" % gi)):
            partial[k] = o
    from_chips = _chip_scatter(partial, "grad_chip_scatter")
    halves = [None] * len(BIG)
    for gi, idx in enumerate(groups):
        for k, o in zip(idx, _chip_add([from_chips[k] for k in idx], "grad_chip_add%d" % gi)):
            halves[k] = o
    grads = {"w_" + n: g for n, g in zip(BIG, _pair_join(halves, "grad_pair_join"))}

    pieces = [gg[n].reshape(-1, 128) for n in GAINS] + [jnp.pad(g_rel, ((0, 0), (0, N_REL_PAD - N_REL))).reshape(-1, 128)]
    summed = _all_sum_small(jnp.concatenate(pieces, axis=0), "small_grad_sum")
    at = 0
    for n, piece in zip(GAINS, pieces[:-1]):
        grads["g_" + n] = summed[at:at + piece.shape[0]].reshape(1, -1)[0]
        at += piece.shape[0]
    grads["rel_bias"] = summed[at:].reshape(N_HEADS, N_REL_PAD)[:, :N_REL]

    delta, new_m, new_v = {}, {}, {}
    for gi, idx in enumerate(groups):
        names = ["w_" + BIG[k] for k in idx]
        d, m, v = _adamw([wts[n] for n in names], [grads[n] for n in names], [ms[n] for n in names],
                         [vs[n] for n in names], 8, "adamw%d" % gi)
        for n, dd, mm, vv in zip(names, d, m, v):
            delta[n], new_m[n], new_v[n] = dd, mm, vv
    small = ["g_" + n for n in GAINS] + ["rel_bias"]
    as_rows = lambda a: (a.reshape(-1, 128) if a.size % 128 == 0 else jnp.pad(a, ((0, 0), (0, N_REL_PAD - N_REL))).reshape(-1, 128))
    d, m, v = _adamw([as_rows(wts[n]) for n in small], [as_rows(grads[n]) for n in small],
                     [as_rows(ms[n]) for n in small], [as_rows(vs[n]) for n in small], 1, "adamw_small")
    for n, dd, mm, vv in zip(small, d, m, v):
        back = (lambda a: a.reshape(N_HEADS, N_REL_PAD)[:, :N_REL]) if n == "rel_bias" else (lambda a: a.reshape(-1))
        delta[n], new_m[n], new_v[n] = back(dd), back(mm), back(vv)

    loss = lax.psum(loss[0, 0], ("x", "y", "c"))
    outs = [loss, dx[None]]
    for table in (grads, delta, new_m, new_v):
        outs += [table[n][None] for n in ORDER]
    return tuple(outs)
```

```python
import functools

import jax
import jax.numpy as jnp
from jax import lax
from jax.experimental import pallas as pl
from jax.experimental.pallas import tpu as pltpu

F32 = jnp.float32
BF16 = jnp.bfloat16
EPS = 1e-6
N_CHIPS = 4
HEAD_DIM = 64
N_HEADS = 8
CHUNK = 64
LOOKBACK = 8
BAND = (LOOKBACK + 1) * CHUNK
PAD = LOOKBACK * CHUNK
REL_CLIP = 128
N_REL = 2 * REL_CLIP + 1
N_REL_PAD = 384
SB_BLOCK = 256
SB_HEADS = 2
ATT_SCALE = HEAD_DIM ** -0.5
NEG_INF = -1e30
ROW_BLOCK = 512
VMEM_LIMIT = 48 * 1024 * 1024
MESH = pl.DeviceIdType.MESH

ADAM_LR = 0.001
ADAM_B1 = 0.9
ADAM_B2 = 0.999
ADAM_EPS = 1e-08
ADAM_WD = 0.01
ADAM_STEP = 10

NT = (((1,), (1,)), ((), ()))
TN = (((0,), (0,)), ((), ()))


def _params(n_grid, vmem=None):
    return pltpu.CompilerParams(dimension_semantics=("arbitrary",) * n_grid, vmem_limit_bytes=vmem)


def _dot(a, b, dims=None):
    if dims is None:
        return jnp.dot(a, b, preferred_element_type=F32)
    return lax.dot_general(a, b, dims, preferred_element_type=F32)


def _sigmoid(x):
    return 1.0 / (1.0 + jnp.exp(-x))


def _rms_fwd(x, g):
    r = lax.rsqrt(jnp.mean(x * x, axis=-1, keepdims=True) + EPS)
    return x * r * g


def _rms_bwd(x, g, dy):
    r = lax.rsqrt(jnp.mean(x * x, axis=-1, keepdims=True) + EPS)
    xh = x * r
    dg = jnp.sum(dy * xh, axis=0, keepdims=True)
    t = dy * g
    dx = r * (t - xh * jnp.mean(t * xh, axis=-1, keepdims=True))
    return dx, dg


def _accumulate(ref, val, first):
    @pl.when(first)
    def _():
        ref[...] = val

    @pl.when(jnp.logical_not(first))
    def _():
        ref[...] += val


def _split2(x):
    hi = x.astype(BF16)
    lo = (x - hi.astype(F32)).astype(BF16)
    return hi, lo


def _ffn_fwd(x, g_pre, g_post, wg, wu, wd, name):
    T, D = x.shape
    S, _, FS = wg.shape
    tm = min(ROW_BLOCK, T)

    def body(x_ref, gpre_ref, gpost_ref, wg_ref, wu_ref, wd_ref,
             h_ref, xn_ref, g_ref, u_ref, a_ref, f_ref, xn_s, acc_s):
        k = pl.program_id(1)

        @pl.when(k == 0)
        def _():
            xn_s[...] = _rms_fwd(x_ref[...], gpre_ref[...]).astype(BF16)
            xn_ref[...] = xn_s[...]

        xn = xn_s[...]
        g = _dot(xn, wg_ref[0])
        u = _dot(xn, wu_ref[0])
        g_ref[0] = g
        u_ref[0] = u
        a = (g * _sigmoid(g) * u).astype(BF16)
        a_ref[0] = a
        _accumulate(acc_s, _dot(a, wd_ref[0]), k == 0)

        @pl.when(k == S - 1)
        def _():
            f = acc_s[...]
            f_ref[...] = f
            h_ref[...] = x_ref[...] + 0.5 * _rms_fwd(f, gpost_ref[...])

    row = pl.BlockSpec((tm, D), lambda i, k: (i, 0))
    vec = pl.BlockSpec((1, D), lambda i, k: (0, 0))
    act = pl.BlockSpec((1, tm, FS), lambda i, k: (k, i, 0))
    return pl.pallas_call(
        body, name=name, grid=(T // tm, S),
        in_specs=[row, vec, vec,
                  pl.BlockSpec((1, D, FS), lambda i, k: (k, 0, 0)),
                  pl.BlockSpec((1, D, FS), lambda i, k: (k, 0, 0)),
                  pl.BlockSpec((1, FS, D), lambda i, k: (k, 0, 0))],
        out_specs=[row, row, act, act, act, row],
        out_shape=[jax.ShapeDtypeStruct((T, D), F32), jax.ShapeDtypeStruct((T, D), BF16),
                   jax.ShapeDtypeStruct((S, T, FS), F32), jax.ShapeDtypeStruct((S, T, FS), F32),
                   jax.ShapeDtypeStruct((S, T, FS), BF16), jax.ShapeDtypeStruct((T, D), F32)],
        scratch_shapes=[pltpu.VMEM((tm, D), BF16), pltpu.VMEM((tm, D), F32)],
        compiler_params=_params(2, VMEM_LIMIT),
    )(x, g_pre, g_post, wg, wu, wd)


def _ffn_bwd_act(dh, f, g_post, wd, g_act, u_act, name):
    T, D = dh.shape
    S, FS, _ = wd.shape
    tm = min(ROW_BLOCK, T)

    def body(dh_ref, f_ref, gpost_ref, wd_ref, g_ref, u_ref, dgp_ref, dup_ref, df_ref, dgain_ref, df_s):
        i, k = pl.program_id(0), pl.program_id(1)

        @pl.when(k == 0)
        def _():
            df, dgain = _rms_bwd(f_ref[...], gpost_ref[...], 0.5 * dh_ref[...])
            df_s[...] = df.astype(BF16)
            df_ref[...] = df_s[...]
            _accumulate(dgain_ref, dgain, i == 0)

        da = _dot(df_s[...], wd_ref[0], NT)
        g = g_ref[0]
        s = _sigmoid(g)
        dup_ref[0] = (da * (g * s)).astype(BF16)
        dgp_ref[0] = (da * u_ref[0] * (s * (1.0 + g * (1.0 - s)))).astype(BF16)

    row = pl.BlockSpec((tm, D), lambda i, k: (i, 0))
    vec = pl.BlockSpec((1, D), lambda i, k: (0, 0))
    act = pl.BlockSpec((1, tm, FS), lambda i, k: (k, i, 0))
    return pl.pallas_call(
        body, name=name, grid=(T // tm, S),
        in_specs=[row, row, vec, pl.BlockSpec((1, FS, D), lambda i, k: (k, 0, 0)), act, act],
        out_specs=[act, act, row, vec],
        out_shape=[jax.ShapeDtypeStruct((S, T, FS), BF16), jax.ShapeDtypeStruct((S, T, FS), BF16),
                   jax.ShapeDtypeStruct((T, D), BF16), jax.ShapeDtypeStruct((1, D), F32)],
        scratch_shapes=[pltpu.VMEM((tm, D), BF16)],
        compiler_params=_params(2, VMEM_LIMIT),
    )(dh, f, g_post, wd, g_act, u_act)


def _proj_bwd(dys, ws, x, g_pre, dh, name):
    T, D = x.shape
    n = len(dys)
    S, _, N = ws[0].shape
    tm = min(ROW_BLOCK, T)

    def body(*refs):
        dy_refs, w_refs = refs[:n], refs[n:2 * n]
        x_ref, gpre_ref, dh_ref, dx_ref, dgain_ref, acc_s = refs[2 * n:]
        i, k = pl.program_id(0), pl.program_id(1)
        part = _dot(dy_refs[0][0], w_refs[0][0], NT)
        for dy_ref, w_ref in zip(dy_refs[1:], w_refs[1:]):
            part += _dot(dy_ref[0], w_ref[0], NT)
        _accumulate(acc_s, part, k == 0)

        @pl.when(k == S - 1)
        def _():
            dx, dgain = _rms_bwd(x_ref[...], gpre_ref[...], acc_s[...])
            dx_ref[...] = dh_ref[...] + dx
            _accumulate(dgain_ref, dgain, i == 0)

    row = pl.BlockSpec((tm, D), lambda i, k: (i, 0))
    vec = pl.BlockSpec((1, D), lambda i, k: (0, 0))
    return pl.pallas_call(
        body, name=name, grid=(T // tm, S),
        in_specs=[pl.BlockSpec((1, tm, N), lambda i, k: (k, i, 0))] * n
        + [pl.BlockSpec((1, D, N), lambda i, k: (k, 0, 0))] * n + [row, vec, row],
        out_specs=[row, vec],
        out_shape=[jax.ShapeDtypeStruct((T, D), F32), jax.ShapeDtypeStruct((1, D), F32)],
        scratch_shapes=[pltpu.VMEM((tm, D), F32)],
        compiler_params=_params(2, VMEM_LIMIT),
    )(*dys, *ws, x, g_pre, dh)


def _mm_tn(a, b, bm, name):
    ga, T, M = a.shape
    gb, _, N = b.shape
    G = max(ga, gb)

    def body(a_ref, b_ref, o_ref):
        o_ref[0] = _dot(a_ref[0].astype(BF16), b_ref[0].astype(BF16), TN)

    return pl.pallas_call(
        body, name=name, grid=(G, M // bm),
        in_specs=[pl.BlockSpec((1, T, bm), (lambda g, m: (g, 0, m)) if ga > 1 else (lambda g, m: (0, 0, m))),
                  pl.BlockSpec((1, T, N), (lambda g, m: (g, 0, 0)) if gb > 1 else (lambda g, m: (0, 0, 0)))],
        out_specs=pl.BlockSpec((1, bm, N), lambda g, m: (g, m, 0)),
        out_shape=jax.ShapeDtypeStruct((G, M, N), F32),
        compiler_params=_params(2, VMEM_LIMIT),
    )(a, b)


def _norm_proj(x, g_pre, w, name):
    T, D = x.shape
    S, _, N = w.shape
    tm = min(ROW_BLOCK, T)

    def body(x_ref, g_ref, w_ref, o_ref, xn_ref, xn_s):
        @pl.when(pl.program_id(1) == 0)
        def _():
            xn_s[...] = _rms_fwd(x_ref[...], g_ref[...]).astype(BF16)
            xn_ref[...] = xn_s[...]

        o_ref[0] = _dot(xn_s[...], w_ref[0]).astype(BF16)

    row = pl.BlockSpec((tm, D), lambda i, k: (i, 0))
    return pl.pallas_call(
        body, name=name, grid=(T // tm, S),
        in_specs=[row, pl.BlockSpec((1, D), lambda i, k: (0, 0)), pl.BlockSpec((1, D, N), lambda i, k: (k, 0, 0))],
        out_specs=[pl.BlockSpec((1, tm, N), lambda i, k: (k, i, 0)), row],
        out_shape=[jax.ShapeDtypeStruct((S, T, N), BF16), jax.ShapeDtypeStruct((T, D), BF16)],
        scratch_shapes=[pltpu.VMEM((tm, D), BF16)],
        compiler_params=_params(2, VMEM_LIMIT),
    )(x, g_pre, w)


def _mix_out_fwd(h, o_all, g_sb, g_ch, w_out, g_post, name):
    T, D = h.shape
    W = g_sb.shape[1]
    tm = min(ROW_BLOCK, T)

    def body(h_ref, o_ref, gsb_ref, gch_ref, w_ref, gpost_ref, h2_ref, mixed_ref, mo_ref):
        mixed_ref[:, :W] = _rms_fwd(o_ref[:, :W], gsb_ref[...]).astype(BF16)
        mixed_ref[:, W:] = _rms_fwd(o_ref[:, W:], gch_ref[...]).astype(BF16)
        mo = _dot(mixed_ref[...], w_ref[...])
        mo_ref[...] = mo
        h2_ref[...] = h_ref[...] + _rms_fwd(mo, gpost_ref[...])

    row = pl.BlockSpec((tm, D), lambda i: (i, 0))
    half = pl.BlockSpec((1, W), lambda i: (0, 0))
    return pl.pallas_call(
        body, name=name, grid=(T // tm,),
        in_specs=[row, row, half, half, pl.BlockSpec((D, D), lambda i: (0, 0)), pl.BlockSpec((1, D), lambda i: (0, 0))],
        out_specs=[row, row, row],
        out_shape=[jax.ShapeDtypeStruct((T, D), F32), jax.ShapeDtypeStruct((T, D), BF16),
                   jax.ShapeDtypeStruct((T, D), F32)],
        compiler_params=_params(1, VMEM_LIMIT),
    )(h, o_all, g_sb, g_ch, w_out, g_post)


def _mix_out_bwd(dh, mo, g_post, w_out, o_all, g_sb, g_ch, name):
    T, D = dh.shape
    W = g_sb.shape[1]
    tm = min(ROW_BLOCK, T)

    def body(dh_ref, mo_ref, gpost_ref, w_ref, o_ref, gsb_ref, gch_ref,
             dmo_ref, do_ref, dgpost_ref, dgsb_ref, dgch_ref):
        first = pl.program_id(0) == 0
        dmo, dgpost = _rms_bwd(mo_ref[...], gpost_ref[...], dh_ref[...])
        dmo_ref[...] = dmo.astype(BF16)
        dmix = _dot(dmo_ref[...], w_ref[...], NT)
        doa, dgsb = _rms_bwd(o_ref[:, :W], gsb_ref[...], dmix[:, :W])
        dob, dgch = _rms_bwd(o_ref[:, W:], gch_ref[...], dmix[:, W:])
        do_ref[:, :W] = doa
        do_ref[:, W:] = dob
        _accumulate(dgpost_ref, dgpost, first)
        _accumulate(dgsb_ref, dgsb, first)
        _accumulate(dgch_ref, dgch, first)

    row = pl.BlockSpec((tm, D), lambda i: (i, 0))
    vec = pl.BlockSpec((1, D), lambda i: (0, 0))
    half = pl.BlockSpec((1, W), lambda i: (0, 0))
    return pl.pallas_call(
        body, name=name, grid=(T // tm,),
        in_specs=[row, row, vec, pl.BlockSpec((D, D), lambda i: (0, 0)), row, half, half],
        out_specs=[row, row, vec, half, half],
        out_shape=[jax.ShapeDtypeStruct((T, D), BF16), jax.ShapeDtypeStruct((T, D), F32),
                   jax.ShapeDtypeStruct((1, D), F32), jax.ShapeDtypeStruct((1, W), F32),
                   jax.ShapeDtypeStruct((1, W), F32)],
        compiler_params=_params(1, VMEM_LIMIT),
    )(dh, mo, g_post, w_out, o_all, g_sb, g_ch)


def _ple_loss(h, p, target, w_proj, w_gate, g_post, name):
    T, D = h.shape
    P = p.shape[1]
    S = N_CHIPS
    C = D // S
    tm = min(ROW_BLOCK, T)

    def body(h_ref, p_ref, t_ref, wp_ref, wg_ref, g_ref, loss_ref, dh_ref, dproj_ref, dgate_ref, dgain_ref):
        first = pl.program_id(0) == 0
        h3 = h_ref[...]
        proj = _dot(p_ref[...].astype(BF16), wp_ref[...])
        s = _sigmoid(_dot(h3.astype(BF16), wg_ref[...]))
        e = proj * s
        diff = h3 + _rms_fwd(e, g_ref[...]) - t_ref[...]
        part = 0.5 * jnp.sum(jnp.mean(diff * diff, axis=-1, keepdims=True), axis=0, keepdims=True)
        _accumulate(loss_ref, jnp.broadcast_to(part, loss_ref.shape), first)
        dy = diff * (1.0 / D)
        de, dgain = _rms_bwd(e, g_ref[...], dy)
        _accumulate(dgain_ref, dgain, first)
        dproj = (de * s).astype(BF16)
        for j in range(S):
            dproj_ref[j] = dproj[:, j * C:(j + 1) * C]
        dgate_ref[...] = (de * proj * s * (1.0 - s)).astype(BF16)
        dh_ref[...] = dy + _dot(dgate_ref[...], wg_ref[...], NT)

    row = pl.BlockSpec((tm, D), lambda i: (i, 0))
    vec = pl.BlockSpec((1, D), lambda i: (0, 0))
    return pl.pallas_call(
        body, name=name, grid=(T // tm,),
        in_specs=[row, pl.BlockSpec((tm, P), lambda i: (i, 0)), row,
                  pl.BlockSpec((P, D), lambda i: (0, 0)), pl.BlockSpec((D, D), lambda i: (0, 0)), vec],
        out_specs=[pl.BlockSpec((8, 128), lambda i: (0, 0)), row,
                   pl.BlockSpec((S, tm, C), lambda i: (0, i, 0)), row, vec],
        out_shape=[jax.ShapeDtypeStruct((8, 128), F32), jax.ShapeDtypeStruct((T, D), F32),
                   jax.ShapeDtypeStruct((S, T, C), BF16), jax.ShapeDtypeStruct((T, D), BF16),
                   jax.ShapeDtypeStruct((1, D), F32)],
        compiler_params=_params(1, VMEM_LIMIT),
    )(h, p, target, w_proj, w_gate, g_post)


def _sb_scores(q, kj, mask):
    z = _dot(q, kj, NT) * ATT_SCALE
    sp = jnp.maximum(z, 0.0) + jnp.log(1.0 + jnp.exp(-jnp.abs(z)))
    lf = -sp if mask is None else jnp.where(mask, -sp, 0.0)
    return z, sp, lf


def _strict_causal():
    rows = lax.broadcasted_iota(jnp.int32, (SB_BLOCK, SB_BLOCK), 0)
    cols = lax.broadcasted_iota(jnp.int32, (SB_BLOCK, SB_BLOCK), 1)
    return cols < rows


def _tri(cmp):
    r = lax.broadcasted_iota(jnp.int32, (SB_BLOCK, SB_BLOCK), 0)
    c = lax.broadcasted_iota(jnp.int32, (SB_BLOCK, SB_BLOCK), 1)
    return jnp.where(cmp(r, c), 1.0, 0.0).astype(BF16)


def _cum(x, tri):
    hi, lo = _split2(x)
    return _dot(hi, tri) + _dot(lo, tri)


def _sb_fwd(qkvh, name):
    _, T, Dh = qkvh.shape
    B, HP = SB_BLOCK, SB_HEADS
    groups = N_HEADS // HP

    def body(q_ref, k_ref, v_ref, o_ref, tot_ref):
        i = pl.program_id(1)
        after = _tri(lambda r, c: r > c)

        def tile(h, j, carry, mask):
            run, acc = carry
            at = pl.ds(pl.multiple_of(j * B, B), B)
            z, sp, lf = _sb_scores(q_ref[h], k_ref[h, at, :], mask)
            a = jnp.exp((z - sp) + _cum(lf, after) + run)
            if mask is not None:
                a = jnp.where(mask, a, 0.0)
            return run + jnp.sum(lf, axis=1, keepdims=True), acc + _dot(a.astype(BF16), v_ref[h, at, :])

        zero = (jnp.zeros((B, 1), F32), jnp.zeros((B, Dh), F32))
        diag = _strict_causal()
        carries = tuple(tile(h, i, zero, diag) for h in range(HP))
        carries = lax.fori_loop(
            0, i, lambda jj, cs: tuple(tile(h, i - 1 - jj, cs[h], None) for h in range(HP)), carries)
        for h in range(HP):
            tot_ref[h], o_ref[h] = carries[h]

    full = lambda off: pl.BlockSpec((HP, T, Dh), lambda g, i: (g + off, 0, 0))
    return pl.pallas_call(
        body, name=name, grid=(groups, T // B),
        in_specs=[pl.BlockSpec((HP, B, Dh), lambda g, i: (g, i, 0)), full(groups), full(2 * groups)],
        out_specs=[pl.BlockSpec((HP, B, Dh), lambda g, i: (g, i, 0)), pl.BlockSpec((HP, B, 1), lambda g, i: (g, i, 0))],
        out_shape=[jax.ShapeDtypeStruct((N_HEADS, T, Dh), F32), jax.ShapeDtypeStruct((N_HEADS, T, 1), F32)],
        compiler_params=_params(2, VMEM_LIMIT),
    )(qkvh, qkvh, qkvh)


def _sb_bwd(qkvh, do, tot, name):
    _, T, Dh = qkvh.shape
    B, HP = SB_BLOCK, SB_HEADS
    groups = N_HEADS // HP

    def body(q_ref, k_ref, v_ref, do_ref, tot_ref, dq_ref, dk_ref, dv_ref):
        i = pl.program_id(1)

        @pl.when(i == 0)
        def _():
            dk_ref[...] = jnp.zeros_like(dk_ref)
            dv_ref[...] = jnp.zeros_like(dv_ref)

        upto = _tri(lambda r, c: r <= c)
        below = _tri(lambda r, c: r < c)

        def tile(h, j, carry, mask):
            pre_lf, pre_g, dq = carry
            at = pl.ds(pl.multiple_of(j * B, B), B)
            q, kj, vj = q_ref[h], k_ref[h, at, :], v_ref[h, at, :]
            dob = do_ref[h].astype(BF16)
            z, sp, lf = _sb_scores(q, kj, mask)
            later = tot_ref[h] - pre_lf - _cum(lf, upto)
            a = jnp.exp((z - sp) + later)
            if mask is not None:
                a = jnp.where(mask, a, 0.0)
            g = a * _dot(dob, vj, NT)
            g_before = pre_g + _cum(g, below)
            fail = jnp.exp(-sp)
            dz = (g * fail - (1.0 - fail) * g_before) * ATT_SCALE
            if mask is not None:
                dz = jnp.where(mask, dz, 0.0)
            dzb = dz.astype(BF16)
            dk_ref[h, at, :] += _dot(dzb, q, TN)
            dv_ref[h, at, :] += _dot(a.astype(BF16), dob, TN)
            return (pre_lf + jnp.sum(lf, axis=1, keepdims=True), pre_g + jnp.sum(g, axis=1, keepdims=True),
                    dq + _dot(dzb, kj))

        col = jnp.zeros((B, 1), F32)
        zero = (col, col, jnp.zeros((B, Dh), F32))
        carries = lax.fori_loop(
            0, i, lambda j, cs: tuple(tile(h, j, cs[h], None) for h in range(HP)), (zero,) * HP)
        diag = _strict_causal()
        for h in range(HP):
            dq_ref[h] = tile(h, i, carries[h], diag)[2]

    blk = lambda off: pl.BlockSpec((HP, B, Dh), lambda g, i: (g + off, i, 0))
    full = lambda off: pl.BlockSpec((HP, T, Dh), lambda g, i: (g + off, 0, 0))
    out = jax.ShapeDtypeStruct((N_HEADS, T, Dh), F32)
    return pl.pallas_call(
        body, name=name, grid=(groups, T // B),
        in_specs=[blk(0), full(groups), full(2 * groups), blk(0), pl.BlockSpec((HP, B, 1), lambda g, i: (g, i, 0))],
        out_specs=[blk(0), full(0), full(0)],
        out_shape=[out, out, out],
        compiler_params=_params(2, VMEM_LIMIT),
    )(qkvh, qkvh, qkvh, do, tot)


def _rel_onehot(i, transposed):
    shape = (BAND, N_REL_PAD) if transposed else (N_REL_PAD, BAND)
    j = lax.broadcasted_iota(jnp.int32, shape, 0 if transposed else 1)
    r = lax.broadcasted_iota(jnp.int32, shape, 1 if transposed else 0)
    idx = jnp.clip(i + PAD - j, -REL_CLIP, REL_CLIP) + REL_CLIP
    return jnp.where(idx == r, 1.0, 0.0).astype(BF16)


def _bias_table(rel_bias_pad, name):
    def body(rb_ref, o_ref):
        onehot = _rel_onehot(pl.program_id(0), False)
        rb = rb_ref[...]
        hi, lo = _split2(rb)
        lo2 = (rb - hi.astype(F32) - lo.astype(F32)).astype(BF16)
        o_ref[0] = _dot(hi, onehot) + _dot(lo, onehot) + _dot(lo2, onehot)

    return pl.pallas_call(
        body, name=name, grid=(CHUNK,),
        in_specs=[pl.BlockSpec((N_HEADS, N_REL_PAD), lambda i: (0, 0))],
        out_specs=pl.BlockSpec((1, N_HEADS, BAND), lambda i: (i, 0, 0)),
        out_shape=jax.ShapeDtypeStruct((CHUNK, N_HEADS, BAND), F32),
        compiler_params=_params(1),
    )(rel_bias_pad)


def _bias_grad(dbias_t, name):
    def body(d_ref, o_ref):
        onehot = _rel_onehot(pl.program_id(0), True)
        hi, lo = _split2(d_ref[0])
        _accumulate(o_ref, _dot(hi, onehot) + _dot(lo, onehot), pl.program_id(0) == 0)

    return pl.pallas_call(
        body, name=name, grid=(CHUNK,),
        in_specs=[pl.BlockSpec((1, N_HEADS, BAND), lambda i: (i, 0, 0))],
        out_specs=pl.BlockSpec((N_HEADS, N_REL_PAD), lambda i: (0, 0)),
        out_shape=jax.ShapeDtypeStruct((N_HEADS, N_REL_PAD), F32),
        compiler_params=_params(1),
    )(dbias_t)


def _ch_probs(q, kw, bias, valid):
    z = jnp.where(valid, _dot(q, kw, NT) * ATT_SCALE + bias, NEG_INF)
    e = jnp.exp(z - jnp.max(z, axis=-1, keepdims=True))
    return e / jnp.sum(e, axis=-1, keepdims=True)


def _ch_valid(n):
    slot = lax.broadcasted_iota(jnp.int32, (CHUNK, BAND), 1) // CHUNK
    return n + slot - LOOKBACK >= 0


def _ch_fwd(qkvh, kpad, vpad, bias, name):
    _, T, Dh = qkvh.shape

    def body(q_ref, k_ref, v_ref, b_ref, o_ref):
        n = pl.program_id(0)
        win = pl.ds(pl.multiple_of(n * CHUNK, CHUNK), BAND)
        valid = _ch_valid(n)
        for h in range(N_HEADS):
            p = _ch_probs(q_ref[h], k_ref[h, win, :], b_ref[h], valid)
            o_ref[h] = _dot(p.astype(BF16), v_ref[h, win, :])

    full = pl.BlockSpec((N_HEADS, PAD + T, Dh), lambda n: (0, 0, 0))
    return pl.pallas_call(
        body, name=name, grid=(T // CHUNK,),
        in_specs=[pl.BlockSpec((N_HEADS, CHUNK, Dh), lambda n: (3, n, 0)), full, full,
                  pl.BlockSpec((N_HEADS, CHUNK, BAND), lambda n: (0, 0, 0))],
        out_specs=pl.BlockSpec((N_HEADS, CHUNK, Dh), lambda n: (0, n, 0)),
        out_shape=jax.ShapeDtypeStruct((N_HEADS, T, Dh), F32),
        compiler_params=_params(1, VMEM_LIMIT),
    )(qkvh, kpad, vpad, bias)


def _ch_bwd(qkvh, kpad, vpad, bias, do, name):
    _, T, Dh = qkvh.shape

    def body(q_ref, k_ref, v_ref, b_ref, do_ref, dq_ref, dk_ref, dv_ref, db_ref):
        n = pl.program_id(0)

        @pl.when(n == 0)
        def _():
            dk_ref[...] = jnp.zeros_like(dk_ref)
            dv_ref[...] = jnp.zeros_like(dv_ref)
            db_ref[...] = jnp.zeros_like(db_ref)

        win = pl.ds(pl.multiple_of(n * CHUNK, CHUNK), BAND)
        valid = _ch_valid(n)
        for h in range(N_HEADS):
            q, kw, vw = q_ref[h], k_ref[h, win, :], v_ref[h, win, :]
            p = _ch_probs(q, kw, b_ref[h], valid)
            dob = do_ref[h].astype(BF16)
            dp = _dot(dob, vw, NT)
            dz = p * (dp - jnp.sum(dp * p, axis=-1, keepdims=True))
            db_ref[h] += dz
            dzb = (dz * ATT_SCALE).astype(BF16)
            dq_ref[h] = _dot(dzb, kw)
            dk_ref[h, win, :] += _dot(dzb, q, TN)
            dv_ref[h, win, :] += _dot(p.astype(BF16), dob, TN)

    full = pl.BlockSpec((N_HEADS, PAD + T, Dh), lambda n: (0, 0, 0))
    blk = pl.BlockSpec((N_HEADS, CHUNK, Dh), lambda n: (0, n, 0))
    tab = pl.BlockSpec((N_HEADS, CHUNK, BAND), lambda n: (0, 0, 0))
    padded = jax.ShapeDtypeStruct((N_HEADS, PAD + T, Dh), F32)
    return pl.pallas_call(
        body, name=name, grid=(T // CHUNK,),
        in_specs=[pl.BlockSpec((N_HEADS, CHUNK, Dh), lambda n: (3, n, 0)), full, full, tab, blk],
        out_specs=[blk, full, full, tab],
        out_shape=[jax.ShapeDtypeStruct((N_HEADS, T, Dh), F32), padded, padded,
                   jax.ShapeDtypeStruct((N_HEADS, CHUNK, BAND), F32)],
        compiler_params=_params(1, VMEM_LIMIT),
    )(qkvh, kpad, vpad, bias, do)


def _rows_split(a, parts):
    return a.reshape(a.shape[:-2] + (parts, a.shape[-2] // parts, a.shape[-1]))


def _cast_bf16(ws, name):
    parts = 4
    ws = [_rows_split(w, parts) for w in ws]

    def body(*refs):
        n = len(refs) // 2
        for src, dst in zip(refs[:n], refs[n:]):
            dst[...] = src[...].astype(BF16)

    spec = lambda w: pl.BlockSpec((1,) + w.shape[1:], lambda i: (i, 0, 0))
    outs = pl.pallas_call(
        body, name=name, grid=(parts,),
        in_specs=[spec(w) for w in ws], out_specs=[spec(w) for w in ws],
        out_shape=[jax.ShapeDtypeStruct(w.shape, BF16) for w in ws],
        compiler_params=_params(1, VMEM_LIMIT),
    )(*ws)
    return [o.reshape(o.shape[0] * o.shape[1], o.shape[2]) for o in outs]


def _pair_add(c, mine, got, name):
    parts = 2
    mine = [_rows_split(m, parts) for m in mine]
    got = [_rows_split(g, parts) for g in got]
    n = len(mine)

    def body(c_ref, *refs):
        for a, b, o in zip(refs[:n], refs[n:2 * n], refs[2 * n:]):
            o[0, 0] = (a[0, 0, 0] + b[0, 0]).astype(BF16)

    def mine_spec(m):
        return pl.BlockSpec((1, 1, 1) + m.shape[3:], lambda j, r, c_ref: (j, c_ref[0], r, 0, 0))

    def got_spec(g):
        return pl.BlockSpec((1, 1) + g.shape[2:], lambda j, r, c_ref: (j, r, 0, 0))

    outs = pl.pallas_call(
        body, name=name,
        grid_spec=pltpu.PrefetchScalarGridSpec(
            num_scalar_prefetch=1, grid=(N_CHIPS, parts),
            in_specs=[mine_spec(m) for m in mine] + [got_spec(g) for g in got],
            out_specs=[got_spec(g) for g in got]),
        out_shape=[jax.ShapeDtypeStruct(g.shape, BF16) for g in got],
        compiler_params=_params(2, VMEM_LIMIT),
    )(c, *mine, *got)
    return [o.reshape(o.shape[0], o.shape[1] * o.shape[2], o.shape[3]) for o in outs]


def _chip_add(parts_in, name):
    parts = 2
    xs = [_rows_split(x, parts) for x in parts_in]

    def body(*refs):
        n = len(refs) // 2
        for x, o in zip(refs[:n], refs[n:]):
            acc = x[0, 0].astype(F32)
            for m in range(1, N_CHIPS):
                acc = acc + x[m, 0].astype(F32)
            o[0] = acc

    outs = pl.pallas_call(
        body, name=name, grid=(parts,),
        in_specs=[pl.BlockSpec((N_CHIPS, 1) + x.shape[2:], lambda r: (0, r, 0, 0)) for x in xs],
        out_specs=[pl.BlockSpec((1,) + x.shape[2:], lambda r: (r, 0, 0)) for x in xs],
        out_shape=[jax.ShapeDtypeStruct(x.shape[1:], F32) for x in xs],
        compiler_params=_params(1, VMEM_LIMIT),
    )(*xs)
    return [o.reshape(o.shape[0] * o.shape[1], o.shape[2]) for o in outs]


def _adamw_math(w, g, m, v):
    m = ADAM_B1 * m + (1.0 - ADAM_B1) * g
    v = ADAM_B2 * v + (1.0 - ADAM_B2) * (g * g)
    m_hat = m / (1.0 - ADAM_B1 ** ADAM_STEP)
    v_hat = v / (1.0 - ADAM_B2 ** ADAM_STEP)
    delta = -ADAM_LR * (m_hat / (jnp.sqrt(v_hat) + ADAM_EPS) + ADAM_WD * w)
    return delta, m, v


def _adamw(ws, gs, ms, vs, parts, name):
    n = len(ws)
    flat = [_rows_split(a, parts) for a in (*ws, *gs, *ms, *vs)]

    def body(*refs):
        ins, outs = refs[:4 * n], refs[4 * n:]
        for k in range(n):
            d, m, v = _adamw_math(ins[k][...], ins[n + k][...], ins[2 * n + k][...], ins[3 * n + k][...])
            outs[k][...] = d
            outs[n + k][...] = m
            outs[2 * n + k][...] = v

    spec = lambda a: pl.BlockSpec((1,) + a.shape[1:], lambda i: (i, 0, 0))
    outs = pl.pallas_call(
        body, name=name, grid=(parts,),
        in_specs=[spec(a) for a in flat], out_specs=[spec(a) for a in flat[:n]] * 3,
        out_shape=[jax.ShapeDtypeStruct(a.shape, F32) for a in flat[:n]] * 3,
        compiler_params=_params(1, VMEM_LIMIT),
    )(*flat)
    outs = [o.reshape(o.shape[0] * o.shape[1], o.shape[2]) for o in outs]
    return outs[:n], outs[n:2 * n], outs[2 * n:]


def _place():
    x, y, c = lax.axis_index("x"), lax.axis_index("y"), lax.axis_index("c")
    others = [(1 - x, y), (x, 1 - y), (1 - x, 1 - y)]
    return x, y, c, others


ANY = pl.BlockSpec(memory_space=pl.ANY)


def _gather_weights(shards, name):
    n = len(shards)
    shards = [_rows_split(s, 2) for s in shards]

    def body(*refs):
        src, dst = refs[:n], refs[n:2 * n]
        send_ici, recv_ici, send_d2d, recv_d2d, local_sem = refs[2 * n:]
        x, y, c, others = _place()
        me = 2 * x + y
        sibling = (x, y, 1 - c)

        local = [pltpu.make_async_copy(src[a], dst[a].at[me], local_sem.at[a]) for a in range(n)]
        for cp in local:
            cp.start()

        def ici(a, k, chip):
            return pltpu.make_async_remote_copy(
                src_ref=src[a].at[c], dst_ref=dst[a].at[me, c],
                send_sem=send_ici.at[a * 3 + k], recv_sem=recv_ici.at[a * 3 + k],
                device_id=(*chip, c), device_id_type=MESH)

        def landed(a, k, chip):
            return pltpu.make_async_remote_copy(
                src_ref=src[a].at[c], dst_ref=dst[a].at[2 * chip[0] + chip[1], c],
                send_sem=send_ici.at[a * 3 + k], recv_sem=recv_ici.at[a * 3 + k],
                device_id=(*chip, c), device_id_type=MESH)

        def d2d(a, k, chip, half):
            slab = dst[a].at[2 * chip[0] + chip[1], half]
            return pltpu.make_async_remote_copy(
                src_ref=slab, dst_ref=slab, send_sem=send_d2d.at[a * 3 + k], recv_sem=recv_d2d.at[a * 3 + k],
                device_id=sibling, device_id_type=MESH)

        for a in range(n):
            for k, chip in enumerate(others):
                ici(a, k, chip).start()
        for a in range(n):
            for k, chip in enumerate(others):
                landed(a, k, chip).wait_recv()
                d2d(a, k, chip, c).start()
        for a in range(n):
            for k, chip in enumerate(others):
                d2d(a, k, chip, 1 - c).wait_recv()
        for a in range(n):
            for k, chip in enumerate(others):
                ici(a, k, chip).wait_send()
                d2d(a, k, chip, c).wait_send()
        for cp in local:
            cp.wait()

    outs = pl.pallas_call(
        body, name=name,
        in_specs=[ANY] * n, out_specs=[ANY] * n,
        out_shape=[jax.ShapeDtypeStruct((N_CHIPS,) + s.shape, BF16) for s in shards],
        scratch_shapes=[pltpu.SemaphoreType.DMA((3 * n,))] * 4 + [pltpu.SemaphoreType.DMA((n,))],
    )(*shards)
    return [o.reshape(N_CHIPS, o.shape[1] * o.shape[2], o.shape[3]) for o in outs]


def _pair_swap(grads, name):
    n = len(grads)

    def body(*refs):
        src, dst = refs[:n], refs[n:2 * n]
        send_sem, recv_sem = refs[2 * n:]
        x, y, c, _ = _place()
        copies = [pltpu.make_async_remote_copy(
            src_ref=src[a].at[:, 1 - c], dst_ref=dst[a], send_sem=send_sem.at[a], recv_sem=recv_sem.at[a],
            device_id=(x, y, 1 - c), device_id_type=MESH) for a in range(n)]
        for cp in copies:
            cp.start()
        for cp in copies:
            cp.wait()

    return pl.pallas_call(
        body, name=name,
        in_specs=[ANY] * n, out_specs=[ANY] * n,
        out_shape=[jax.ShapeDtypeStruct((N_CHIPS,) + g.shape[2:], F32) for g in grads],
        scratch_shapes=[pltpu.SemaphoreType.DMA((n,))] * 2,
    )(*grads)


def _chip_scatter(partials, name):
    n = len(partials)

    def body(*refs):
        src, dst = refs[:n], refs[n:2 * n]
        send_sem, recv_sem, local_sem = refs[2 * n:]
        x, y, c, others = _place()
        me = 2 * x + y
        local = [pltpu.make_async_copy(src[a].at[me], dst[a].at[me], local_sem.at[a]) for a in range(n)]
        for cp in local:
            cp.start()
        copies = []
        for a in range(n):
            for k, chip in enumerate(others):
                copies.append(pltpu.make_async_remote_copy(
                    src_ref=src[a].at[2 * chip[0] + chip[1]], dst_ref=dst[a].at[me],
                    send_sem=send_sem.at[a * 3 + k], recv_sem=recv_sem.at[a * 3 + k],
                    device_id=(*chip, c), device_id_type=MESH))
        for cp in copies:
            cp.start()
        for cp in copies:
            cp.wait()
        for cp in local:
            cp.wait()

    return pl.pallas_call(
        body, name=name,
        in_specs=[ANY] * n, out_specs=[ANY] * n,
        out_shape=[jax.ShapeDtypeStruct(p.shape, BF16) for p in partials],
        scratch_shapes=[pltpu.SemaphoreType.DMA((3 * n,))] * 2 + [pltpu.SemaphoreType.DMA((n,))],
    )(*partials)


def _pair_join(halves, name):
    n = len(halves)

    def body(*refs):
        src, dst = refs[:n], refs[n:2 * n]
        send_sem, recv_sem = refs[2 * n:]
        x, y, c, _ = _place()
        copies = [pltpu.make_async_remote_copy(
            src_ref=src[a], dst_ref=dst[a], send_sem=send_sem.at[a], recv_sem=recv_sem.at[a],
            device_id=(x, y, 1 - c), device_id_type=MESH) for a in range(n)]
        for cp in copies:
            cp.start()
        for cp in copies:
            cp.wait()

    return pl.pallas_call(
        body, name=name,
        in_specs=[ANY] * n, out_specs=[ANY] * n,
        out_shape=[jax.ShapeDtypeStruct(h.shape, F32) for h in halves],
        scratch_shapes=[pltpu.SemaphoreType.DMA((n,))] * 2,
    )(*halves)


def _all_sum_small(v, name):
    R, C = v.shape
    n_dev = 8

    def body(v_ref, o_ref, buf, send_sem, recv_sem):
        x, y, c, _ = _place()
        me = 4 * x + 2 * y + c
        buf[me] = v_ref[...]
        copies = []
        for k in range(1, n_dev):
            peer = (x ^ (k >> 2), y ^ ((k >> 1) & 1), c ^ (k & 1))
            copies.append(pltpu.make_async_remote_copy(
                src_ref=v_ref, dst_ref=buf.at[me], send_sem=send_sem.at[k - 1], recv_sem=recv_sem.at[k - 1],
                device_id=peer, device_id_type=MESH))
        for cp in copies:
            cp.start()
        for cp in copies:
            cp.wait()
        acc = buf[0]
        for m in range(1, n_dev):
            acc = acc + buf[m]
        o_ref[...] = acc

    return pl.pallas_call(
        body, name=name,
        in_specs=[pl.BlockSpec(memory_space=pltpu.VMEM)], out_specs=pl.BlockSpec(memory_space=pltpu.VMEM),
        out_shape=jax.ShapeDtypeStruct((R, C), F32),
        scratch_shapes=[pltpu.VMEM((n_dev, R, C), F32), pltpu.SemaphoreType.DMA((n_dev - 1,)),
                        pltpu.SemaphoreType.DMA((n_dev - 1,))],
    )(v)


def _heads_major(qkv):
    S, T, N = qkv.shape
    per = N // HEAD_DIM
    return qkv.reshape(S, T, per, HEAD_DIM).transpose(0, 2, 1, 3).reshape(S * per, T, HEAD_DIM)


def _heads_minor(t, S):
    n, T, _ = t.shape
    per = n // S
    return t.reshape(S, per, T, HEAD_DIM).transpose(0, 2, 1, 3).reshape(S, T, per * HEAD_DIM)


def _local_step(x, p, target, gains, rel_bias, w):
    T, D = x.shape
    S = N_CHIPS
    pad3 = ((0, 0), (PAD, 0), (0, 0))

    h1, xn1, g1, u1, a1, f1 = _ffn_fwd(x, gains["ffn1_pre"], gains["ffn1_post"], w["ffn1_gate"], w["ffn1_up"],
                                       w["ffn1_down"], "ffn1_fwd")
    qkv, un = _norm_proj(h1, gains["mix_pre"], w["in"], "qkv_proj")
    qkvh = _heads_major(qkv)
    kpad = jnp.pad(qkvh[4 * N_HEADS:5 * N_HEADS], pad3)
    vpad = jnp.pad(qkvh[5 * N_HEADS:], pad3)
    rb_pad = jnp.pad(rel_bias, ((0, 0), (0, N_REL_PAD - N_REL)))
    bias = _bias_table(rb_pad, "bias_table").transpose(1, 0, 2)
    o_a, tot = _sb_fwd(qkvh, "sb_fwd")
    o_b = _ch_fwd(qkvh, kpad, vpad, bias, "ch_fwd")
    o_all = jnp.concatenate([o_a, o_b], axis=0).transpose(1, 0, 2).reshape(T, D)
    w_out = w["out"].reshape(D, D)
    h2, mixed, mo = _mix_out_fwd(h1, o_all, gains["out_sb"], gains["out_ch"], w_out, gains["mix_post"], "mix_out_fwd")
    h3, xn2, g2, u2, a2, f2 = _ffn_fwd(h2, gains["ffn2_pre"], gains["ffn2_post"], w["ffn2_gate"], w["ffn2_up"],
                                       w["ffn2_down"], "ffn2_fwd")
    w_ple_proj = w["ple_proj"].transpose(1, 0, 2).reshape(p.shape[1], D)
    w_ple_gate = w["ple_gate"].reshape(D, D)

    loss, dh3, dproj, dgate, dg_ple = _ple_loss(h3, p, target, w_ple_proj, w_ple_gate, gains["ple_post"], "ple_loss")
    gw, gg = {}, {"ple_post": dg_ple}
    gw["ple_proj"] = _mm_tn(p[None], dproj, p.shape[1], "dw_ple_proj")
    gw["ple_gate"] = _mm_tn(h3[None], dgate[None], 512, "dw_ple_gate").reshape(S, D // S, D)

    def ffn_bwd(tag, dh, x_in, xn, g_act, u_act, a_act, f):
        dgp, dup, df, gg[tag + "_post"] = _ffn_bwd_act(dh, f, gains[tag + "_post"], w[tag + "_down"], g_act, u_act,
                                                       tag + "_bwd_act")
        gw[tag + "_gate"] = _mm_tn(xn[None], dgp, 512, "dw_" + tag + "_gate")
        gw[tag + "_up"] = _mm_tn(xn[None], dup, 512, "dw_" + tag + "_up")
        gw[tag + "_down"] = _mm_tn(a_act, df[None], a_act.shape[2], "dw_" + tag + "_down")
        dx, gg[tag + "_pre"] = _proj_bwd([dgp, dup], [w[tag + "_gate"], w[tag + "_up"]], x_in, gains[tag + "_pre"], dh,
                                         tag + "_bwd_in")
        return dx

    dh2 = ffn_bwd("ffn2", dh3, h2, xn2, g2, u2, a2, f2)
    dmo, do_all, gg["mix_post"], gg["out_sb"], gg["out_ch"] = _mix_out_bwd(
        dh2, mo, gains["mix_post"], w_out, o_all, gains["out_sb"], gains["out_ch"], "mix_out_bwd")
    gw["out"] = _mm_tn(mixed[None], dmo[None], 512, "dw_out").reshape(S, D // S, D)
    do_h = do_all.reshape(T, 2 * N_HEADS, HEAD_DIM).transpose(1, 0, 2)
    dq_a, dk_a, dv_a = _sb_bwd(qkvh, do_h[:N_HEADS], tot, "sb_bwd")
    dq_b, dk_b, dv_b, dbias = _ch_bwd(qkvh, kpad, vpad, bias, do_h[N_HEADS:], "ch_bwd")
    g_rel = _bias_grad(dbias.transpose(1, 0, 2), "bias_grad")[:, :N_REL]
    dqkvh = jnp.concatenate([dq_a, dk_a, dv_a, dq_b, dk_b[:, PAD:], dv_b[:, PAD:]], axis=0).astype(BF16)
    dqkv = _heads_minor(dqkvh, S)
    gw["in"] = _mm_tn(un[None], dqkv, 512, "dw_in")
    dh1, gg["mix_pre"] = _proj_bwd([dqkv], [w["in"]], h1, gains["mix_pre"], dh2, "qkv_bwd_in")
    dx = ffn_bwd("ffn1", dh1, x, xn1, g1, u1, a1, f1)
    return loss, dx, gw, gg, g_rel


BIG = ["ffn1_gate", "ffn1_up", "ffn1_down", "in", "out", "ffn2_gate", "ffn2_up", "ffn2_down", "ple_proj", "ple_gate"]
GAINS = ["ffn1_pre", "ffn1_post", "mix_pre", "mix_post", "out_sb", "out_ch", "ffn2_pre", "ffn2_post", "ple_post"]
ORDER = ["g_ffn1_pre", "g_ffn1_post", "w_ffn1_gate", "w_ffn1_up", "w_ffn1_down", "g_mix_pre", "g_mix_post", "w_in",
         "g_out_sb", "g_out_ch", "rel_bias", "w_out", "g_ffn2_pre", "g_ffn2_post", "w_ffn2_gate", "w_ffn2_up",
         "w_ffn2_down", "w_ple_proj", "w_ple_gate", "g_ple_post"]


def kernel(x, p, g_ffn1_pre, g_ffn1_post, w_ffn1_gate, w_ffn1_up, w_ffn1_down, g_mix_pre, g_mix_post, w_in, g_out_sb, g_out_ch, rel_bias, w_out, g_ffn2_pre, g_ffn2_post, w_ffn2_gate, w_ffn2_up, w_ffn2_down, w_ple_proj, w_ple_gate, g_ple_post, loss_target, m_g_ffn1_pre, m_g_ffn1_post, m_w_ffn1_gate, m_w_ffn1_up, m_w_ffn1_down, m_g_mix_pre, m_g_mix_post, m_w_in, m_g_out_sb, m_g_out_ch, m_rel_bias, m_w_out, m_g_ffn2_pre, m_g_ffn2_post, m_w_ffn2_gate, m_w_ffn2_up, m_w_ffn2_down, m_w_ple_proj, m_w_ple_gate, m_g_ple_post, v_g_ffn1_pre, v_g_ffn1_post, v_w_ffn1_gate, v_w_ffn1_up, v_w_ffn1_down, v_g_mix_pre, v_g_mix_post, v_w_in, v_g_out_sb, v_g_out_ch, v_rel_bias, v_w_out, v_g_ffn2_pre, v_g_ffn2_post, v_w_ffn2_gate, v_w_ffn2_up, v_w_ffn2_down, v_w_ple_proj, v_w_ple_gate, v_g_ple_post):
    args = dict(locals())
    wts = {n: args[n][0] for n in ORDER}
    ms = {n: args["m_" + n][0] for n in ORDER}
    vs = {n: args["v_" + n][0] for n in ORDER}
    gains = {n: wts["g_" + n][None] for n in GAINS}

    shards = _cast_bf16([wts["w_" + n] for n in BIG], "cast_weights")
    full = dict(zip(BIG, _gather_weights(shards, "gather_weights")))

    loss, dx, gw, gg, g_rel = _local_step(x[0], p[0, 0], loss_target[0], gains, wts["rel_bias"], full)

    c_idx = lax.axis_index("c").astype(jnp.int32).reshape(1)
    mine = [gw[n].reshape(N_CHIPS, 2, gw[n].shape[1] // 2, gw[n].shape[2]) for n in BIG]
    got = _pair_swap(mine, "grad_pair_swap")
    groups = [[0, 1, 2], [5, 6, 7], [3, 4, 8, 9]]
    partial = [None] * len(BIG)
    for gi, idx in enumerate(groups):
        for k, o in zip(idx, _pair_add(c_idx, [mine[k] for k in idx], [got[k] for k in idx], "grad_pair_add%d" % gi)):
            partial[k] = o
    from_chips = _chip_scatter(partial, "grad_chip_scatter")
    halves = [None] * len(BIG)
    for gi, idx in enumerate(groups):
        for k, o in zip(idx, _chip_add([from_chips[k] for k in idx], "grad_chip_add%d" % gi)):
            halves[k] = o
    south = lax.axis_index("c") == 0
    grads = {}
    for n, own, other in zip(BIG, halves, _pair_join(halves, "grad_pair_join")):
        grads["w_" + n] = jnp.concatenate([jnp.where(south, own, other), jnp.where(south, other, own)], axis=0)

    pieces = [gg[n].reshape(-1, 128) for n in GAINS] + [jnp.pad(g_rel, ((0, 0), (0, N_REL_PAD - N_REL))).reshape(-1, 128)]
    summed = _all_sum_small(jnp.concatenate(pieces, axis=0), "small_grad_sum")
    at = 0
    for n, piece in zip(GAINS, pieces[:-1]):
        grads["g_" + n] = summed[at:at + piece.shape[0]].reshape(1, -1)[0]
        at += piece.shape[0]
    grads["rel_bias"] = summed[at:].reshape(N_HEADS, N_REL_PAD)[:, :N_REL]

    delta, new_m, new_v = {}, {}, {}
    for gi, idx in enumerate(groups):
        names = ["w_" + BIG[k] for k in idx]
        d, m, v = _adamw([wts[n] for n in names], [grads[n] for n in names], [ms[n] for n in names],
                         [vs[n] for n in names], 8, "adamw%d" % gi)
        for n, dd, mm, vv in zip(names, d, m, v):
            delta[n], new_m[n], new_v[n] = dd, mm, vv
    small = ["g_" + n for n in GAINS] + ["rel_bias"]
    as_rows = lambda a: (a.reshape(-1, 128) if a.size % 128 == 0 else jnp.pad(a, ((0, 0), (0, N_REL_PAD - N_REL))).reshape(-1, 128))
    d, m, v = _adamw([as_rows(wts[n]) for n in small], [as_rows(grads[n]) for n in small],
                     [as_rows(ms[n]) for n in small], [as_rows(vs[n]) for n in small], 1, "adamw_small")
    for n, dd, mm, vv in zip(small, d, m, v):
        back = (lambda a: a.reshape(N_HEADS, N_REL_PAD)[:, :N_REL]) if n == "rel_bias" else (lambda a: a.reshape(-1))
        delta[n], new_m[n], new_v[n] = back(dd), back(mm), back(vv)

    loss = lax.psum(loss[0, 0], ("x", "y", "c"))
    outs = [loss, dx[None]]
    for table in (grads, delta, new_m, new_v):
        outs += [table[n][None] for n in ORDER]
    return tuple(outs)
```

```python
import functools

import jax
import jax.numpy as jnp
from jax import lax
from jax.experimental import pallas as pl
from jax.experimental.pallas import tpu as pltpu

F32 = jnp.float32
BF16 = jnp.bfloat16
EPS = 1e-6
N_CHIPS = 4
HEAD_DIM = 64
N_HEADS = 8
CHUNK = 64
LOOKBACK = 8
BAND = (LOOKBACK + 1) * CHUNK
PAD = LOOKBACK * CHUNK
REL_CLIP = 128
N_REL = 2 * REL_CLIP + 1
N_REL_PAD = 384
SB_BLOCK = 256
PAIR = 2 * HEAD_DIM
ATT_SCALE = HEAD_DIM ** -0.5
NEG_INF = -1e30
ROW_BLOCK = 512
VMEM_LIMIT = 48 * 1024 * 1024
MESH = pl.DeviceIdType.MESH

ADAM_LR = 0.001
ADAM_B1 = 0.9
ADAM_B2 = 0.999
ADAM_EPS = 1e-08
ADAM_WD = 0.01
ADAM_STEP = 10

NT = (((1,), (1,)), ((), ()))
TN = (((0,), (0,)), ((), ()))


def _params(n_grid, vmem=None):
    return pltpu.CompilerParams(dimension_semantics=("arbitrary",) * n_grid, vmem_limit_bytes=vmem)


def _dot(a, b, dims=None):
    if dims is None:
        return jnp.dot(a, b, preferred_element_type=F32)
    return lax.dot_general(a, b, dims, preferred_element_type=F32)


def _sigmoid(x):
    return 1.0 / (1.0 + jnp.exp(-x))


def _rms_fwd(x, g):
    r = lax.rsqrt(jnp.mean(x * x, axis=-1, keepdims=True) + EPS)
    return x * r * g


def _rms_bwd(x, g, dy):
    r = lax.rsqrt(jnp.mean(x * x, axis=-1, keepdims=True) + EPS)
    xh = x * r
    dg = jnp.sum(dy * xh, axis=0, keepdims=True)
    t = dy * g
    dx = r * (t - xh * jnp.mean(t * xh, axis=-1, keepdims=True))
    return dx, dg


def _accumulate(ref, val, first):
    @pl.when(first)
    def _():
        ref[...] = val

    @pl.when(jnp.logical_not(first))
    def _():
        ref[...] += val


def _split2(x):
    hi = x.astype(BF16)
    lo = (x - hi.astype(F32)).astype(BF16)
    return hi, lo


def _ffn_fwd(x, g_pre, g_post, wg, wu, wd, name):
    T, D = x.shape
    S, FS, _ = wg.shape
    tm = min(ROW_BLOCK, T)

    def body(x_ref, gpre_ref, gpost_ref, wg_ref, wu_ref, wd_ref,
             h_ref, xn_ref, g_ref, u_ref, a_ref, f_ref, xn_s, acc_s):
        k = pl.program_id(1)

        @pl.when(k == 0)
        def _():
            xn_s[...] = _rms_fwd(x_ref[...], gpre_ref[...]).astype(BF16)
            xn_ref[...] = xn_s[...]

        xn = xn_s[...]
        g = _dot(xn, wg_ref[0], NT)
        u = _dot(xn, wu_ref[0], NT)
        g_ref[0] = g
        u_ref[0] = u
        a = (g * _sigmoid(g) * u).astype(BF16)
        a_ref[0] = a
        _accumulate(acc_s, _dot(a, wd_ref[0]), k == 0)

        @pl.when(k == S - 1)
        def _():
            f = acc_s[...]
            f_ref[...] = f
            h_ref[...] = x_ref[...] + 0.5 * _rms_fwd(f, gpost_ref[...])

    row = pl.BlockSpec((tm, D), lambda i, k: (i, 0))
    vec = pl.BlockSpec((1, D), lambda i, k: (0, 0))
    act = pl.BlockSpec((1, tm, FS), lambda i, k: (k, i, 0))
    return pl.pallas_call(
        body, name=name, grid=(T // tm, S),
        in_specs=[row, vec, vec] + [pl.BlockSpec((1, FS, D), lambda i, k: (k, 0, 0))] * 3,
        out_specs=[row, row, act, act, act, row],
        out_shape=[jax.ShapeDtypeStruct((T, D), F32), jax.ShapeDtypeStruct((T, D), BF16),
                   jax.ShapeDtypeStruct((S, T, FS), F32), jax.ShapeDtypeStruct((S, T, FS), F32),
                   jax.ShapeDtypeStruct((S, T, FS), BF16), jax.ShapeDtypeStruct((T, D), F32)],
        scratch_shapes=[pltpu.VMEM((tm, D), BF16), pltpu.VMEM((tm, D), F32)],
        compiler_params=_params(2, VMEM_LIMIT),
    )(x, g_pre, g_post, wg, wu, wd)


def _ffn_bwd_act(dh, f, g_post, wd, g_act, u_act, name):
    T, D = dh.shape
    S, FS, _ = wd.shape
    tm = min(ROW_BLOCK, T)

    def body(dh_ref, f_ref, gpost_ref, wd_ref, g_ref, u_ref, dgp_ref, dup_ref, df_ref, dgain_ref, df_s):
        i, k = pl.program_id(0), pl.program_id(1)

        @pl.when(k == 0)
        def _():
            df, dgain = _rms_bwd(f_ref[...], gpost_ref[...], 0.5 * dh_ref[...])
            df_s[...] = df.astype(BF16)
            df_ref[...] = df_s[...]
            _accumulate(dgain_ref, dgain, i == 0)

        da = _dot(df_s[...], wd_ref[0], NT)
        g = g_ref[0]
        s = _sigmoid(g)
        dup_ref[0] = (da * (g * s)).astype(BF16)
        dgp_ref[0] = (da * u_ref[0] * (s * (1.0 + g * (1.0 - s)))).astype(BF16)

    row = pl.BlockSpec((tm, D), lambda i, k: (i, 0))
    vec = pl.BlockSpec((1, D), lambda i, k: (0, 0))
    act = pl.BlockSpec((1, tm, FS), lambda i, k: (k, i, 0))
    return pl.pallas_call(
        body, name=name, grid=(T // tm, S),
        in_specs=[row, row, vec, pl.BlockSpec((1, FS, D), lambda i, k: (k, 0, 0)), act, act],
        out_specs=[act, act, row, vec],
        out_shape=[jax.ShapeDtypeStruct((S, T, FS), BF16), jax.ShapeDtypeStruct((S, T, FS), BF16),
                   jax.ShapeDtypeStruct((T, D), BF16), jax.ShapeDtypeStruct((1, D), F32)],
        scratch_shapes=[pltpu.VMEM((tm, D), BF16)],
        compiler_params=_params(2, VMEM_LIMIT),
    )(dh, f, g_post, wd, g_act, u_act)


def _proj_bwd(dys, ws, x, g_pre, dh, name):
    T, D = x.shape
    n = len(dys)
    flat = dys[0].ndim == 2
    S = ws[0].shape[0]
    N = ws[0].shape[2] if flat else ws[0].shape[1]
    tm = min(ROW_BLOCK, T)

    def body(*refs):
        dy_refs, w_refs = refs[:n], refs[n:2 * n]
        x_ref, gpre_ref, dh_ref, dx_ref, dgain_ref, acc_s = refs[2 * n:]
        i, k = pl.program_id(0), pl.program_id(1)
        part = None
        for dy_ref, w_ref in zip(dy_refs, w_refs):
            term = _dot(dy_ref[...], w_ref[0], NT) if flat else _dot(dy_ref[0], w_ref[0])
            part = term if part is None else part + term
        _accumulate(acc_s, part, k == 0)

        @pl.when(k == S - 1)
        def _():
            dx, dgain = _rms_bwd(x_ref[...], gpre_ref[...], acc_s[...])
            dx_ref[...] = dh_ref[...] + dx
            _accumulate(dgain_ref, dgain, i == 0)

    row = pl.BlockSpec((tm, D), lambda i, k: (i, 0))
    vec = pl.BlockSpec((1, D), lambda i, k: (0, 0))
    return pl.pallas_call(
        body, name=name, grid=(T // tm, S),
        in_specs=[pl.BlockSpec((tm, N), lambda i, k: (i, k)) if flat else pl.BlockSpec((1, tm, N), lambda i, k: (k, i, 0))] * n
        + [pl.BlockSpec((1,) + ws[0].shape[1:], lambda i, k: (k, 0, 0))] * n + [row, vec, row],
        out_specs=[row, vec],
        out_shape=[jax.ShapeDtypeStruct((T, D), F32), jax.ShapeDtypeStruct((1, D), F32)],
        scratch_shapes=[pltpu.VMEM((tm, D), F32)],
        compiler_params=_params(2, VMEM_LIMIT),
    )(*dys, *ws, x, g_pre, dh)


def _mm_tn(a, b, bm, name, groups=None):
    ga, T, M = a.shape
    if groups is None:
        gb, _, N = b.shape
        b_spec = pl.BlockSpec((1, T, N), (lambda g, m: (g, 0, 0)) if gb > 1 else (lambda g, m: (0, 0, 0)))
    else:
        gb, N = groups, b.shape[1] // groups
        b_spec = pl.BlockSpec((T, N), lambda g, m: (0, g))
    G = max(ga, gb)

    def body(a_ref, b_ref, o_ref):
        bv = b_ref[0] if groups is None else b_ref[...]
        o_ref[0] = _dot(a_ref[0].astype(BF16), bv.astype(BF16), TN)

    return pl.pallas_call(
        body, name=name, grid=(G, M // bm),
        in_specs=[pl.BlockSpec((1, T, bm), (lambda g, m: (g, 0, m)) if ga > 1 else (lambda g, m: (0, 0, m))), b_spec],
        out_specs=pl.BlockSpec((1, bm, N), lambda g, m: (g, m, 0)),
        out_shape=jax.ShapeDtypeStruct((G, M, N), F32),
        compiler_params=_params(2, VMEM_LIMIT),
    )(a, b)


def _norm_proj(x, g_pre, w, name):
    T, D = x.shape
    S, _, N = w.shape
    tm = min(ROW_BLOCK, T)

    def body(x_ref, g_ref, w_ref, o_ref, xn_ref, xn_s):
        @pl.when(pl.program_id(1) == 0)
        def _():
            xn_s[...] = _rms_fwd(x_ref[...], g_ref[...]).astype(BF16)
            xn_ref[...] = xn_s[...]

        o_ref[...] = _dot(xn_s[...], w_ref[0]).astype(BF16)

    row = pl.BlockSpec((tm, D), lambda i, k: (i, 0))
    return pl.pallas_call(
        body, name=name, grid=(T // tm, S),
        in_specs=[row, pl.BlockSpec((1, D), lambda i, k: (0, 0)), pl.BlockSpec((1, D, N), lambda i, k: (k, 0, 0))],
        out_specs=[pl.BlockSpec((tm, N), lambda i, k: (i, k)), row],
        out_shape=[jax.ShapeDtypeStruct((T, S * N), BF16), jax.ShapeDtypeStruct((T, D), BF16)],
        scratch_shapes=[pltpu.VMEM((tm, D), BF16)],
        compiler_params=_params(2, VMEM_LIMIT),
    )(x, g_pre, w)


def _mix_out_fwd(h, o_a, o_b, g_sb, g_ch, w_out, g_post, name):
    T, D = h.shape
    W = g_sb.shape[1]
    tm = min(ROW_BLOCK, T)

    def body(h_ref, oa_ref, ob_ref, gsb_ref, gch_ref, w_ref, gpost_ref, h2_ref, mixed_ref, mo_ref):
        mixed_ref[:, :W] = _rms_fwd(oa_ref[...], gsb_ref[...]).astype(BF16)
        mixed_ref[:, W:] = _rms_fwd(ob_ref[...], gch_ref[...]).astype(BF16)
        mo = _dot(mixed_ref[...], w_ref[...])
        mo_ref[...] = mo
        h2_ref[...] = h_ref[...] + _rms_fwd(mo, gpost_ref[...])

    row = pl.BlockSpec((tm, D), lambda i: (i, 0))
    part = pl.BlockSpec((tm, W), lambda i: (i, 0))
    half = pl.BlockSpec((1, W), lambda i: (0, 0))
    return pl.pallas_call(
        body, name=name, grid=(T // tm,),
        in_specs=[row, part, part, half, half, pl.BlockSpec((D, D), lambda i: (0, 0)), pl.BlockSpec((1, D), lambda i: (0, 0))],
        out_specs=[row, row, row],
        out_shape=[jax.ShapeDtypeStruct((T, D), F32), jax.ShapeDtypeStruct((T, D), BF16),
                   jax.ShapeDtypeStruct((T, D), F32)],
        compiler_params=_params(1, VMEM_LIMIT),
    )(h, o_a, o_b, g_sb, g_ch, w_out, g_post)


def _mix_out_bwd(dh, mo, g_post, w_out, o_a, o_b, g_sb, g_ch, name):
    T, D = dh.shape
    W = g_sb.shape[1]
    tm = min(ROW_BLOCK, T)

    def body(dh_ref, mo_ref, gpost_ref, w_ref, oa_ref, ob_ref, gsb_ref, gch_ref,
             dmo_ref, doa_ref, dob_ref, dgpost_ref, dgsb_ref, dgch_ref):
        first = pl.program_id(0) == 0
        dmo, dgpost = _rms_bwd(mo_ref[...], gpost_ref[...], dh_ref[...])
        dmo_ref[...] = dmo.astype(BF16)
        dmix = _dot(dmo_ref[...], w_ref[...], NT)
        doa_ref[...], dgsb = _rms_bwd(oa_ref[...], gsb_ref[...], dmix[:, :W])
        dob_ref[...], dgch = _rms_bwd(ob_ref[...], gch_ref[...], dmix[:, W:])
        _accumulate(dgpost_ref, dgpost, first)
        _accumulate(dgsb_ref, dgsb, first)
        _accumulate(dgch_ref, dgch, first)

    row = pl.BlockSpec((tm, D), lambda i: (i, 0))
    part = pl.BlockSpec((tm, W), lambda i: (i, 0))
    vec = pl.BlockSpec((1, D), lambda i: (0, 0))
    half = pl.BlockSpec((1, W), lambda i: (0, 0))
    return pl.pallas_call(
        body, name=name, grid=(T // tm,),
        in_specs=[row, row, vec, pl.BlockSpec((D, D), lambda i: (0, 0)), part, part, half, half],
        out_specs=[row, part, part, vec, half, half],
        out_shape=[jax.ShapeDtypeStruct((T, D), BF16), jax.ShapeDtypeStruct((T, W), F32),
                   jax.ShapeDtypeStruct((T, W), F32), jax.ShapeDtypeStruct((1, D), F32),
                   jax.ShapeDtypeStruct((1, W), F32), jax.ShapeDtypeStruct((1, W), F32)],
        compiler_params=_params(1, VMEM_LIMIT),
    )(dh, mo, g_post, w_out, o_a, o_b, g_sb, g_ch)


def _ple_loss(h, p, target, w_proj, w_gate, g_post, name):
    T, D = h.shape
    P = p.shape[1]
    S = N_CHIPS
    C = D // S
    tm = min(ROW_BLOCK, T)

    def body(h_ref, p_ref, t_ref, wp_ref, wg_ref, g_ref, loss_ref, dh_ref, dproj_ref, dgate_ref, dgain_ref):
        first = pl.program_id(0) == 0
        h3 = h_ref[...]
        proj = _dot(p_ref[...].astype(BF16), wp_ref[...])
        s = _sigmoid(_dot(h3.astype(BF16), wg_ref[...]))
        e = proj * s
        diff = h3 + _rms_fwd(e, g_ref[...]) - t_ref[...]
        part = 0.5 * jnp.sum(jnp.mean(diff * diff, axis=-1, keepdims=True), axis=0, keepdims=True)
        _accumulate(loss_ref, jnp.broadcast_to(part, loss_ref.shape), first)
        dy = diff * (1.0 / D)
        de, dgain = _rms_bwd(e, g_ref[...], dy)
        _accumulate(dgain_ref, dgain, first)
        dproj = (de * s).astype(BF16)
        for j in range(S):
            dproj_ref[j] = dproj[:, j * C:(j + 1) * C]
        dgate_ref[...] = (de * proj * s * (1.0 - s)).astype(BF16)
        dh_ref[...] = dy + _dot(dgate_ref[...], wg_ref[...], NT)

    row = pl.BlockSpec((tm, D), lambda i: (i, 0))
    vec = pl.BlockSpec((1, D), lambda i: (0, 0))
    return pl.pallas_call(
        body, name=name, grid=(T // tm,),
        in_specs=[row, pl.BlockSpec((tm, P), lambda i: (i, 0)), row,
                  pl.BlockSpec((P, D), lambda i: (0, 0)), pl.BlockSpec((D, D), lambda i: (0, 0)), vec],
        out_specs=[pl.BlockSpec((8, 128), lambda i: (0, 0)), row,
                   pl.BlockSpec((S, tm, C), lambda i: (0, i, 0)), row, vec],
        out_shape=[jax.ShapeDtypeStruct((8, 128), F32), jax.ShapeDtypeStruct((T, D), F32),
                   jax.ShapeDtypeStruct((S, T, C), BF16), jax.ShapeDtypeStruct((T, D), BF16),
                   jax.ShapeDtypeStruct((1, D), F32)],
        compiler_params=_params(1, VMEM_LIMIT),
    )(h, p, target, w_proj, w_gate, g_post)


def _sb_scores(q, kj, mask):
    z = _dot(q, kj, NT) * ATT_SCALE
    sp = jnp.maximum(z, 0.0) + jnp.log(1.0 + jnp.exp(-jnp.abs(z)))
    lf = -sp if mask is None else jnp.where(mask, -sp, 0.0)
    return z, sp, lf


def _strict_causal():
    rows = lax.broadcasted_iota(jnp.int32, (SB_BLOCK, SB_BLOCK), 0)
    cols = lax.broadcasted_iota(jnp.int32, (SB_BLOCK, SB_BLOCK), 1)
    return cols < rows


def _tri(cmp):
    r = lax.broadcasted_iota(jnp.int32, (SB_BLOCK, SB_BLOCK), 0)
    c = lax.broadcasted_iota(jnp.int32, (SB_BLOCK, SB_BLOCK), 1)
    return jnp.where(cmp(r, c), 1.0, 0.0).astype(BF16)


def _cum(x, tri):
    hi, lo = _split2(x)
    return _dot(hi, tri) + _dot(lo, tri)


def _pair_lanes():
    lane = lax.broadcasted_iota(jnp.int32, (1, PAIR), 1)
    return [lane < HEAD_DIM, lane >= HEAD_DIM]


def _only(lanes, x):
    return jnp.where(lanes, x, jnp.zeros_like(x))


def _sb_fwd(qkv, name):
    T = qkv.shape[0]
    B = SB_BLOCK
    pairs = N_HEADS // 2

    def body(q_ref, k_ref, v_ref, o_ref, tot_ref):
        i = pl.program_id(1)
        after = _tri(lambda r, c: r > c)
        lanes = _pair_lanes()
        q = [_only(lanes[h], q_ref[...]) for h in range(2)]

        def tile(h, j, carry, mask):
            run, acc = carry
            at = pl.ds(pl.multiple_of(j * B, B), B)
            z, sp, lf = _sb_scores(q[h], k_ref[at, :], mask)
            a = jnp.exp((z - sp) + _cum(lf, after) + run)
            if mask is not None:
                a = jnp.where(mask, a, 0.0)
            return (run + jnp.sum(lf, axis=1, keepdims=True),
                    acc + _dot(a.astype(BF16), _only(lanes[h], v_ref[at, :])))

        zero = (jnp.zeros((B, 1), F32), jnp.zeros((B, PAIR), F32))
        diag = _strict_causal()
        carries = tuple(tile(h, i, zero, diag) for h in range(2))
        carries = lax.fori_loop(
            0, i, lambda jj, cs: tuple(tile(h, i - 1 - jj, cs[h], None) for h in range(2)), carries)
        o_ref[...] = carries[0][1] + carries[1][1]
        tot_ref[...] = jnp.where(lanes[0], carries[0][0], carries[1][0])

    blk = lambda off: pl.BlockSpec((B, PAIR), lambda g, i: (i, g + off))
    full = lambda off: pl.BlockSpec((T, PAIR), lambda g, i: (0, g + off))
    out = jax.ShapeDtypeStruct((T, pairs * PAIR), F32)
    return pl.pallas_call(
        body, name=name, grid=(pairs, T // B),
        in_specs=[blk(0), full(pairs), full(2 * pairs)],
        out_specs=[blk(0), blk(0)],
        out_shape=[out, out],
        compiler_params=_params(2, VMEM_LIMIT),
    )(qkv, qkv, qkv)


def _sb_bwd(qkv, do, tot, name):
    T = qkv.shape[0]
    B = SB_BLOCK
    pairs = N_HEADS // 2
    n_blocks = T // B

    def body(q_ref, k_ref, v_ref, do_ref, tot_ref, dq_ref, dk_ref, dv_ref, dk_s, dv_s):
        i = pl.program_id(1)

        @pl.when(i == 0)
        def _():
            dk_s[...] = jnp.zeros_like(dk_s)
            dv_s[...] = jnp.zeros_like(dv_s)

        upto = _tri(lambda r, c: r <= c)
        below = _tri(lambda r, c: r < c)
        lanes = _pair_lanes()
        q = [_only(lanes[h], q_ref[...]) for h in range(2)]
        dob = do_ref[...].astype(BF16)
        do = [_only(lanes[h], dob) for h in range(2)]
        tot = [tot_ref[:, 0:1], tot_ref[:, HEAD_DIM:HEAD_DIM + 1]]

        def tile(h, j, carry, mask):
            pre_lf, pre_g, dq = carry
            at = pl.ds(pl.multiple_of(j * B, B), B)
            kj, vj = k_ref[at, :], v_ref[at, :]
            z, sp, lf = _sb_scores(q[h], kj, mask)
            later = tot[h] - pre_lf - _cum(lf, upto)
            a = jnp.exp((z - sp) + later)
            if mask is not None:
                a = jnp.where(mask, a, 0.0)
            g = a * _dot(do[h], vj, NT)
            g_before = pre_g + _cum(g, below)
            fail = jnp.exp(-sp)
            dz = (g * fail - (1.0 - fail) * g_before) * ATT_SCALE
            if mask is not None:
                dz = jnp.where(mask, dz, 0.0)
            dzb = dz.astype(BF16)
            dk_s[at, :] += _dot(dzb, q[h], TN)
            dv_s[at, :] += _dot(a.astype(BF16), do[h], TN)
            return (pre_lf + jnp.sum(lf, axis=1, keepdims=True), pre_g + jnp.sum(g, axis=1, keepdims=True),
                    dq + _dot(dzb, _only(lanes[h], kj)))

        col = jnp.zeros((B, 1), F32)
        zero = (col, col, jnp.zeros((B, PAIR), F32))
        carries = lax.fori_loop(
            0, i, lambda j, cs: tuple(tile(h, j, cs[h], None) for h in range(2)), (zero, zero))
        diag = _strict_causal()
        dq_ref[...] = (tile(0, i, carries[0], diag)[2] + tile(1, i, carries[1], diag)[2]).astype(BF16)

        @pl.when(i == n_blocks - 1)
        def _():
            dk_ref[...] = dk_s[...].astype(BF16)
            dv_ref[...] = dv_s[...].astype(BF16)

    blk = lambda off: pl.BlockSpec((B, PAIR), lambda g, i: (i, g + off))
    full = lambda off: pl.BlockSpec((T, PAIR), lambda g, i: (0, g + off))
    out = jax.ShapeDtypeStruct((T, pairs * PAIR), BF16)
    return pl.pallas_call(
        body, name=name, grid=(pairs, n_blocks),
        in_specs=[blk(0), full(pairs), full(2 * pairs), blk(0), blk(0)],
        out_specs=[blk(0), full(0), full(0)],
        out_shape=[out, out, out],
        scratch_shapes=[pltpu.VMEM((T, PAIR), F32)] * 2,
        compiler_params=_params(2, VMEM_LIMIT),
    )(qkv, qkv, qkv, do, tot)


def _rel_onehot(i, transposed):
    shape = (BAND, N_REL_PAD) if transposed else (N_REL_PAD, BAND)
    j = lax.broadcasted_iota(jnp.int32, shape, 0 if transposed else 1)
    r = lax.broadcasted_iota(jnp.int32, shape, 1 if transposed else 0)
    idx = jnp.clip(i + PAD - j, -REL_CLIP, REL_CLIP) + REL_CLIP
    return jnp.where(idx == r, 1.0, 0.0).astype(BF16)


def _bias_table(rel_bias_pad, name):
    def body(rb_ref, o_ref):
        onehot = _rel_onehot(pl.program_id(0), False)
        rb = rb_ref[...]
        hi, lo = _split2(rb)
        lo2 = (rb - hi.astype(F32) - lo.astype(F32)).astype(BF16)
        o_ref[0] = _dot(hi, onehot) + _dot(lo, onehot) + _dot(lo2, onehot)

    return pl.pallas_call(
        body, name=name, grid=(CHUNK,),
        in_specs=[pl.BlockSpec((N_HEADS, N_REL_PAD), lambda i: (0, 0))],
        out_specs=pl.BlockSpec((1, N_HEADS, BAND), lambda i: (i, 0, 0)),
        out_shape=jax.ShapeDtypeStruct((CHUNK, N_HEADS, BAND), F32),
        compiler_params=_params(1),
    )(rel_bias_pad)


def _bias_grad(dbias_t, name):
    def body(d_ref, o_ref):
        onehot = _rel_onehot(pl.program_id(0), True)
        hi, lo = _split2(d_ref[0])
        _accumulate(o_ref, _dot(hi, onehot) + _dot(lo, onehot), pl.program_id(0) == 0)

    return pl.pallas_call(
        body, name=name, grid=(CHUNK,),
        in_specs=[pl.BlockSpec((1, N_HEADS, BAND), lambda i: (i, 0, 0))],
        out_specs=pl.BlockSpec((N_HEADS, N_REL_PAD), lambda i: (0, 0)),
        out_shape=jax.ShapeDtypeStruct((N_HEADS, N_REL_PAD), F32),
        compiler_params=_params(1),
    )(dbias_t)


def _ch_probs(q, kw, bias, valid):
    z = jnp.where(valid, _dot(q, kw, NT) * ATT_SCALE + bias, NEG_INF)
    e = jnp.exp(z - jnp.max(z, axis=-1, keepdims=True))
    return e / jnp.sum(e, axis=-1, keepdims=True)


def _ch_valid(n):
    slot = lax.broadcasted_iota(jnp.int32, (CHUNK, BAND), 1) // CHUNK
    return n + slot - LOOKBACK >= 0


def _ch_fwd(qkv, bias, name):
    T = qkv.shape[0]
    W = N_HEADS * HEAD_DIM

    def body(q_ref, k_ref, v_ref, b_ref, o_ref, kp, vp):
        n = pl.program_id(0)

        @pl.when(n == 0)
        def _():
            _ch_load_padded(k_ref, v_ref, kp, vp)

        win = pl.ds(pl.multiple_of(n * CHUNK, CHUNK), BAND)
        valid = _ch_valid(n)
        lanes = _pair_lanes()
        for pair in range(N_HEADS // 2):
            cols = slice(pair * PAIR, (pair + 1) * PAIR)
            q, kw, vw = q_ref[:, cols], kp[win, cols], vp[win, cols]
            o = None
            for h in range(2):
                p = _ch_probs(_only(lanes[h], q), kw, b_ref[2 * pair + h], valid)
                part = _dot(p.astype(BF16), _only(lanes[h], vw))
                o = part if o is None else o + part
            o_ref[:, cols] = o

    full = lambda col: pl.BlockSpec((T, W), lambda n: (0, col))
    return pl.pallas_call(
        body, name=name, grid=(T // CHUNK,),
        in_specs=[pl.BlockSpec((CHUNK, W), lambda n: (n, 3)), full(4), full(5),
                  pl.BlockSpec((N_HEADS, CHUNK, BAND), lambda n: (0, 0, 0))],
        out_specs=pl.BlockSpec((CHUNK, W), lambda n: (n, 0)),
        out_shape=jax.ShapeDtypeStruct((T, W), F32),
        scratch_shapes=[pltpu.VMEM((PAD + T, W), BF16)] * 2,
        compiler_params=_params(1, VMEM_LIMIT),
    )(qkv, qkv, qkv, bias)


def _ch_load_padded(k_ref, v_ref, kp, vp):
    for src, dst in ((k_ref, kp), (v_ref, vp)):
        dst[:PAD, :] = jnp.zeros((PAD, dst.shape[1]), dst.dtype)
        dst[PAD:, :] = src[...]


def _ch_bwd(qkv, bias, do, name):
    T = qkv.shape[0]
    W = N_HEADS * HEAD_DIM
    n_chunks = T // CHUNK

    def body(q_ref, k_ref, v_ref, b_ref, do_ref, dq_ref, dk_ref, dv_ref, db_ref, kp, vp, dk_s, dv_s):
        n = pl.program_id(0)

        @pl.when(n == 0)
        def _():
            _ch_load_padded(k_ref, v_ref, kp, vp)
            dk_s[...] = jnp.zeros_like(dk_s)
            dv_s[...] = jnp.zeros_like(dv_s)
            db_ref[...] = jnp.zeros_like(db_ref)

        win = pl.ds(pl.multiple_of(n * CHUNK, CHUNK), BAND)
        valid = _ch_valid(n)
        lanes = _pair_lanes()
        for pair in range(N_HEADS // 2):
            cols = slice(pair * PAIR, (pair + 1) * PAIR)
            q, kw, vw = q_ref[:, cols], kp[win, cols], vp[win, cols]
            dob = do_ref[:, cols].astype(BF16)
            dq = dk = dv = None
            for h in range(2):
                qh, doh = _only(lanes[h], q), _only(lanes[h], dob)
                p = _ch_probs(qh, kw, b_ref[2 * pair + h], valid)
                dp = _dot(doh, vw, NT)
                dz = p * (dp - jnp.sum(dp * p, axis=-1, keepdims=True))
                db_ref[2 * pair + h] += dz
                dzb = (dz * ATT_SCALE).astype(BF16)
                parts = (_dot(dzb, _only(lanes[h], kw)), _dot(dzb, qh, TN), _dot(p.astype(BF16), doh, TN))
                dq, dk, dv = parts if dq is None else (dq + parts[0], dk + parts[1], dv + parts[2])
            dq_ref[:, cols] = dq.astype(BF16)
            dk_s[win, cols] += dk
            dv_s[win, cols] += dv

        @pl.when(n == n_chunks - 1)
        def _():
            dk_ref[...] = dk_s[PAD:, :].astype(BF16)
            dv_ref[...] = dv_s[PAD:, :].astype(BF16)

    full = lambda col: pl.BlockSpec((T, W), lambda n: (0, col))
    blk = lambda col: pl.BlockSpec((CHUNK, W), lambda n: (n, col))
    tab = pl.BlockSpec((N_HEADS, CHUNK, BAND), lambda n: (0, 0, 0))
    out = jax.ShapeDtypeStruct((T, W), BF16)
    return pl.pallas_call(
        body, name=name, grid=(n_chunks,),
        in_specs=[blk(3), full(4), full(5), tab, blk(0)],
        out_specs=[blk(0), full(0), full(0), tab],
        out_shape=[out, out, out, jax.ShapeDtypeStruct((N_HEADS, CHUNK, BAND), F32)],
        scratch_shapes=[pltpu.VMEM((PAD + T, W), BF16)] * 2 + [pltpu.VMEM((PAD + T, W), F32)] * 2,
        compiler_params=_params(1, VMEM_LIMIT),
    )(qkv, qkv, qkv, bias, do)


def _rows_split(a, parts):
    return a.reshape(a.shape[:-2] + (parts, a.shape[-2] // parts, a.shape[-1]))


def _cast_bf16(ws, name):
    parts = 4
    ws = [_rows_split(w, parts) for w in ws]

    def body(*refs):
        n = len(refs) // 2
        for src, dst in zip(refs[:n], refs[n:]):
            dst[...] = src[...].astype(BF16)

    spec = lambda w: pl.BlockSpec((1,) + w.shape[1:], lambda i: (i, 0, 0))
    outs = pl.pallas_call(
        body, name=name, grid=(parts,),
        in_specs=[spec(w) for w in ws], out_specs=[spec(w) for w in ws],
        out_shape=[jax.ShapeDtypeStruct(w.shape, BF16) for w in ws],
        compiler_params=_params(1, VMEM_LIMIT),
    )(*ws)
    return [o.reshape(o.shape[0] * o.shape[1], o.shape[2]) for o in outs]


def _pair_add(c, mine, got, name):
    parts = 2
    mine = [_rows_split(m, parts) for m in mine]
    got = [_rows_split(g, parts) for g in got]
    n = len(mine)

    def body(c_ref, *refs):
        for a, b, o in zip(refs[:n], refs[n:2 * n], refs[2 * n:]):
            o[0, 0] = (a[0, 0, 0] + b[0, 0]).astype(BF16)

    def mine_spec(m):
        return pl.BlockSpec((1, 1, 1) + m.shape[3:], lambda j, r, c_ref: (j, c_ref[0], r, 0, 0))

    def got_spec(g):
        return pl.BlockSpec((1, 1) + g.shape[2:], lambda j, r, c_ref: (j, r, 0, 0))

    outs = pl.pallas_call(
        body, name=name,
        grid_spec=pltpu.PrefetchScalarGridSpec(
            num_scalar_prefetch=1, grid=(N_CHIPS, parts),
            in_specs=[mine_spec(m) for m in mine] + [got_spec(g) for g in got],
            out_specs=[got_spec(g) for g in got]),
        out_shape=[jax.ShapeDtypeStruct(g.shape, BF16) for g in got],
        compiler_params=_params(2, VMEM_LIMIT),
    )(c, *mine, *got)
    return [o.reshape(o.shape[0], o.shape[1] * o.shape[2], o.shape[3]) for o in outs]


def _chip_add(parts_in, name):
    parts = 2
    xs = [_rows_split(x, parts) for x in parts_in]

    def body(*refs):
        n = len(refs) // 2
        for x, o in zip(refs[:n], refs[n:]):
            acc = x[0, 0].astype(F32)
            for m in range(1, N_CHIPS):
                acc = acc + x[m, 0].astype(F32)
            o[0] = acc

    outs = pl.pallas_call(
        body, name=name, grid=(parts,),
        in_specs=[pl.BlockSpec((N_CHIPS, 1) + x.shape[2:], lambda r: (0, r, 0, 0)) for x in xs],
        out_specs=[pl.BlockSpec((1,) + x.shape[2:], lambda r: (r, 0, 0)) for x in xs],
        out_shape=[jax.ShapeDtypeStruct(x.shape[1:], F32) for x in xs],
        compiler_params=_params(1, VMEM_LIMIT),
    )(*xs)
    return [o.reshape(o.shape[0] * o.shape[1], o.shape[2]) for o in outs]


def _adamw_math(w, g, m, v):
    m = ADAM_B1 * m + (1.0 - ADAM_B1) * g
    v = ADAM_B2 * v + (1.0 - ADAM_B2) * (g * g)
    m_hat = m / (1.0 - ADAM_B1 ** ADAM_STEP)
    v_hat = v / (1.0 - ADAM_B2 ** ADAM_STEP)
    delta = -ADAM_LR * (m_hat / (jnp.sqrt(v_hat) + ADAM_EPS) + ADAM_WD * w)
    return delta, m, v


def _adamw(ws, gs, ms, vs, parts, name):
    n = len(ws)
    flat = [_rows_split(a, parts) for a in (*ws, *gs, *ms, *vs)]

    def body(*refs):
        ins, outs = refs[:4 * n], refs[4 * n:]
        for k in range(n):
            d, m, v = _adamw_math(ins[k][...], ins[n + k][...], ins[2 * n + k][...], ins[3 * n + k][...])
            outs[k][...] = d
            outs[n + k][...] = m
            outs[2 * n + k][...] = v

    spec = lambda a: pl.BlockSpec((1,) + a.shape[1:], lambda i: (i, 0, 0))
    outs = pl.pallas_call(
        body, name=name, grid=(parts,),
        in_specs=[spec(a) for a in flat], out_specs=[spec(a) for a in flat[:n]] * 3,
        out_shape=[jax.ShapeDtypeStruct(a.shape, F32) for a in flat[:n]] * 3,
        compiler_params=_params(1, VMEM_LIMIT),
    )(*flat)
    outs = [o.reshape(o.shape[0] * o.shape[1], o.shape[2]) for o in outs]
    return outs[:n], outs[n:2 * n], outs[2 * n:]


def _place():
    x, y, c = lax.axis_index("x"), lax.axis_index("y"), lax.axis_index("c")
    others = [(1 - x, y), (x, 1 - y), (1 - x, 1 - y)]
    return x, y, c, others


ANY = pl.BlockSpec(memory_space=pl.ANY)


def _gather_weights(shards, name):
    n = len(shards)
    shards = [_rows_split(s, 2) for s in shards]

    def body(*refs):
        src, dst = refs[:n], refs[n:2 * n]
        send_ici, recv_ici, send_d2d, recv_d2d, local_sem = refs[2 * n:]
        x, y, c, others = _place()
        me = 2 * x + y
        sibling = (x, y, 1 - c)

        local = [pltpu.make_async_copy(src[a], dst[a].at[me], local_sem.at[a]) for a in range(n)]
        for cp in local:
            cp.start()

        def ici(a, k, chip):
            return pltpu.make_async_remote_copy(
                src_ref=src[a].at[c], dst_ref=dst[a].at[me, c],
                send_sem=send_ici.at[a * 3 + k], recv_sem=recv_ici.at[a * 3 + k],
                device_id=(*chip, c), device_id_type=MESH)

        def landed(a, k, chip):
            return pltpu.make_async_remote_copy(
                src_ref=src[a].at[c], dst_ref=dst[a].at[2 * chip[0] + chip[1], c],
                send_sem=send_ici.at[a * 3 + k], recv_sem=recv_ici.at[a * 3 + k],
                device_id=(*chip, c), device_id_type=MESH)

        def d2d(a, k, chip, half):
            slab = dst[a].at[2 * chip[0] + chip[1], half]
            return pltpu.make_async_remote_copy(
                src_ref=slab, dst_ref=slab, send_sem=send_d2d.at[a * 3 + k], recv_sem=recv_d2d.at[a * 3 + k],
                device_id=sibling, device_id_type=MESH)

        for a in range(n):
            for k, chip in enumerate(others):
                ici(a, k, chip).start()
        for a in range(n):
            for k, chip in enumerate(others):
                landed(a, k, chip).wait_recv()
                d2d(a, k, chip, c).start()
        for a in range(n):
            for k, chip in enumerate(others):
                d2d(a, k, chip, 1 - c).wait_recv()
        for a in range(n):
            for k, chip in enumerate(others):
                ici(a, k, chip).wait_send()
                d2d(a, k, chip, c).wait_send()
        for cp in local:
            cp.wait()

    outs = pl.pallas_call(
        body, name=name,
        in_specs=[ANY] * n, out_specs=[ANY] * n,
        out_shape=[jax.ShapeDtypeStruct((N_CHIPS,) + s.shape, BF16) for s in shards],
        scratch_shapes=[pltpu.SemaphoreType.DMA((3 * n,))] * 4 + [pltpu.SemaphoreType.DMA((n,))],
    )(*shards)
    return [o.reshape(N_CHIPS, o.shape[1] * o.shape[2], o.shape[3]) for o in outs]


def _pair_swap(grads, name):
    n = len(grads)

    def body(*refs):
        src, dst = refs[:n], refs[n:2 * n]
        send_sem, recv_sem = refs[2 * n:]
        x, y, c, _ = _place()
        copies = [pltpu.make_async_remote_copy(
            src_ref=src[a].at[:, 1 - c], dst_ref=dst[a], send_sem=send_sem.at[a], recv_sem=recv_sem.at[a],
            device_id=(x, y, 1 - c), device_id_type=MESH) for a in range(n)]
        for cp in copies:
            cp.start()
        for cp in copies:
            cp.wait()

    return pl.pallas_call(
        body, name=name,
        in_specs=[ANY] * n, out_specs=[ANY] * n,
        out_shape=[jax.ShapeDtypeStruct((N_CHIPS,) + g.shape[2:], F32) for g in grads],
        scratch_shapes=[pltpu.SemaphoreType.DMA((n,))] * 2,
    )(*grads)


def _chip_scatter(partials, name):
    n = len(partials)

    def body(*refs):
        src, dst = refs[:n], refs[n:2 * n]
        send_sem, recv_sem, local_sem = refs[2 * n:]
        x, y, c, others = _place()
        me = 2 * x + y
        local = [pltpu.make_async_copy(src[a].at[me], dst[a].at[me], local_sem.at[a]) for a in range(n)]
        for cp in local:
            cp.start()
        copies = []
        for a in range(n):
            for k, chip in enumerate(others):
                copies.append(pltpu.make_async_remote_copy(
                    src_ref=src[a].at[2 * chip[0] + chip[1]], dst_ref=dst[a].at[me],
                    send_sem=send_sem.at[a * 3 + k], recv_sem=recv_sem.at[a * 3 + k],
                    device_id=(*chip, c), device_id_type=MESH))
        for cp in copies:
            cp.start()
        for cp in copies:
            cp.wait()
        for cp in local:
            cp.wait()

    return pl.pallas_call(
        body, name=name,
        in_specs=[ANY] * n, out_specs=[ANY] * n,
        out_shape=[jax.ShapeDtypeStruct(p.shape, BF16) for p in partials],
        scratch_shapes=[pltpu.SemaphoreType.DMA((3 * n,))] * 2 + [pltpu.SemaphoreType.DMA((n,))],
    )(*partials)


def _pair_join(halves, name):
    n = len(halves)

    def body(*refs):
        src, dst = refs[:n], refs[n:2 * n]
        send_sem, recv_sem = refs[2 * n:]
        x, y, c, _ = _place()
        copies = [pltpu.make_async_remote_copy(
            src_ref=src[a], dst_ref=dst[a], send_sem=send_sem.at[a], recv_sem=recv_sem.at[a],
            device_id=(x, y, 1 - c), device_id_type=MESH) for a in range(n)]
        for cp in copies:
            cp.start()
        for cp in copies:
            cp.wait()

    return pl.pallas_call(
        body, name=name,
        in_specs=[ANY] * n, out_specs=[ANY] * n,
        out_shape=[jax.ShapeDtypeStruct(h.shape, F32) for h in halves],
        scratch_shapes=[pltpu.SemaphoreType.DMA((n,))] * 2,
    )(*halves)


def _all_sum_small(v, name):
    R, C = v.shape
    n_dev = 8

    def body(v_ref, o_ref, buf, send_sem, recv_sem):
        x, y, c, _ = _place()
        me = 4 * x + 2 * y + c
        buf[me] = v_ref[...]
        copies = []
        for k in range(1, n_dev):
            peer = (x ^ (k >> 2), y ^ ((k >> 1) & 1), c ^ (k & 1))
            copies.append(pltpu.make_async_remote_copy(
                src_ref=v_ref, dst_ref=buf.at[me], send_sem=send_sem.at[k - 1], recv_sem=recv_sem.at[k - 1],
                device_id=peer, device_id_type=MESH))
        for cp in copies:
            cp.start()
        for cp in copies:
            cp.wait()
        acc = buf[0]
        for m in range(1, n_dev):
            acc = acc + buf[m]
        o_ref[...] = acc

    return pl.pallas_call(
        body, name=name,
        in_specs=[pl.BlockSpec(memory_space=pltpu.VMEM)], out_specs=pl.BlockSpec(memory_space=pltpu.VMEM),
        out_shape=jax.ShapeDtypeStruct((R, C), F32),
        scratch_shapes=[pltpu.VMEM((n_dev, R, C), F32), pltpu.SemaphoreType.DMA((n_dev - 1,)),
                        pltpu.SemaphoreType.DMA((n_dev - 1,))],
    )(v)


def _local_step(x, p, target, gains, rel_bias, w):
    T, D = x.shape
    S = N_CHIPS

    h1, xn1, g1, u1, a1, f1 = _ffn_fwd(x, gains["ffn1_pre"], gains["ffn1_post"], w["ffn1_gate"], w["ffn1_up"],
                                       w["ffn1_down"], "ffn1_fwd")
    qkv, un = _norm_proj(h1, gains["mix_pre"], w["in"], "qkv_proj")
    rb_pad = jnp.pad(rel_bias, ((0, 0), (0, N_REL_PAD - N_REL)))
    bias = _bias_table(rb_pad, "bias_table").transpose(1, 0, 2)
    o_a, tot = _sb_fwd(qkv, "sb_fwd")
    o_b = _ch_fwd(qkv, bias, "ch_fwd")
    w_out = w["out"].reshape(D, D)
    h2, mixed, mo = _mix_out_fwd(h1, o_a, o_b, gains["out_sb"], gains["out_ch"], w_out, gains["mix_post"],
                                 "mix_out_fwd")
    h3, xn2, g2, u2, a2, f2 = _ffn_fwd(h2, gains["ffn2_pre"], gains["ffn2_post"], w["ffn2_gate"], w["ffn2_up"],
                                       w["ffn2_down"], "ffn2_fwd")
    w_ple_proj = w["ple_proj"].transpose(1, 0, 2).reshape(p.shape[1], D)
    w_ple_gate = w["ple_gate"].reshape(D, D)

    loss, dh3, dproj, dgate, dg_ple = _ple_loss(h3, p, target, w_ple_proj, w_ple_gate, gains["ple_post"], "ple_loss")
    gw, gg = {}, {"ple_post": dg_ple}
    gw["ple_proj"] = _mm_tn(p[None], dproj, p.shape[1], "dw_ple_proj")
    gw["ple_gate"] = _mm_tn(h3[None], dgate[None], 512, "dw_ple_gate").reshape(S, D // S, D)

    def ffn_bwd(tag, dh, x_in, xn, g_act, u_act, a_act, f):
        dgp, dup, df, gg[tag + "_post"] = _ffn_bwd_act(dh, f, gains[tag + "_post"], w[tag + "_down"], g_act, u_act,
                                                       tag + "_bwd_act")
        gw[tag + "_gate"] = _mm_tn(dgp, xn[None], dgp.shape[2], "dw_" + tag + "_gate")
        gw[tag + "_up"] = _mm_tn(dup, xn[None], dup.shape[2], "dw_" + tag + "_up")
        gw[tag + "_down"] = _mm_tn(a_act, df[None], a_act.shape[2], "dw_" + tag + "_down")
        dx, gg[tag + "_pre"] = _proj_bwd([dgp, dup], [w[tag + "_gate"], w[tag + "_up"]], x_in, gains[tag + "_pre"], dh,
                                         tag + "_bwd_in")
        return dx

    dh2 = ffn_bwd("ffn2", dh3, h2, xn2, g2, u2, a2, f2)
    dmo, do_a, do_b, gg["mix_post"], gg["out_sb"], gg["out_ch"] = _mix_out_bwd(
        dh2, mo, gains["mix_post"], w_out, o_a, o_b, gains["out_sb"], gains["out_ch"], "mix_out_bwd")
    gw["out"] = _mm_tn(mixed[None], dmo[None], 512, "dw_out").reshape(S, D // S, D)
    dq_a, dk_a, dv_a = _sb_bwd(qkv, do_a, tot, "sb_bwd")
    dq_b, dk_b, dv_b, dbias = _ch_bwd(qkv, bias, do_b, "ch_bwd")
    g_rel = _bias_grad(dbias.transpose(1, 0, 2), "bias_grad")[:, :N_REL]
    dqkv = jnp.concatenate([dq_a, dk_a, dv_a, dq_b, dk_b, dv_b], axis=1)
    gw["in"] = _mm_tn(un[None], dqkv, 512, "dw_in", groups=S)
    dh1, gg["mix_pre"] = _proj_bwd([dqkv], [w["in"]], h1, gains["mix_pre"], dh2, "qkv_bwd_in")
    dx = ffn_bwd("ffn1", dh1, x, xn1, g1, u1, a1, f1)
    return loss, dx, gw, gg, g_rel


BIG = ["ffn1_gate", "ffn1_up", "ffn1_down", "in", "out", "ffn2_gate", "ffn2_up", "ffn2_down", "ple_proj", "ple_gate"]
GAINS = ["ffn1_pre", "ffn1_post", "mix_pre", "mix_post", "out_sb", "out_ch", "ffn2_pre", "ffn2_post", "ple_post"]
TRANSPOSED = ("w_ffn1_gate", "w_ffn1_up", "w_ffn2_gate", "w_ffn2_up")
ORDER = ["g_ffn1_pre", "g_ffn1_post", "w_ffn1_gate", "w_ffn1_up", "w_ffn1_down", "g_mix_pre", "g_mix_post", "w_in",
         "g_out_sb", "g_out_ch", "rel_bias", "w_out", "g_ffn2_pre", "g_ffn2_post", "w_ffn2_gate", "w_ffn2_up",
         "w_ffn2_down", "w_ple_proj", "w_ple_gate", "g_ple_post"]


def kernel(x, p, g_ffn1_pre, g_ffn1_post, w_ffn1_gate, w_ffn1_up, w_ffn1_down, g_mix_pre, g_mix_post, w_in, g_out_sb, g_out_ch, rel_bias, w_out, g_ffn2_pre, g_ffn2_post, w_ffn2_gate, w_ffn2_up, w_ffn2_down, w_ple_proj, w_ple_gate, g_ple_post, loss_target, m_g_ffn1_pre, m_g_ffn1_post, m_w_ffn1_gate, m_w_ffn1_up, m_w_ffn1_down, m_g_mix_pre, m_g_mix_post, m_w_in, m_g_out_sb, m_g_out_ch, m_rel_bias, m_w_out, m_g_ffn2_pre, m_g_ffn2_post, m_w_ffn2_gate, m_w_ffn2_up, m_w_ffn2_down, m_w_ple_proj, m_w_ple_gate, m_g_ple_post, v_g_ffn1_pre, v_g_ffn1_post, v_w_ffn1_gate, v_w_ffn1_up, v_w_ffn1_down, v_g_mix_pre, v_g_mix_post, v_w_in, v_g_out_sb, v_g_out_ch, v_rel_bias, v_w_out, v_g_ffn2_pre, v_g_ffn2_post, v_w_ffn2_gate, v_w_ffn2_up, v_w_ffn2_down, v_w_ple_proj, v_w_ple_gate, v_g_ple_post):
    args = dict(locals())
    take = lambda a, n: a[0].T if n in TRANSPOSED else a[0]
    wts = {n: take(args[n], n) for n in ORDER}
    ms = {n: take(args["m_" + n], n) for n in ORDER}
    vs = {n: take(args["v_" + n], n) for n in ORDER}
    gains = {n: wts["g_" + n][None] for n in GAINS}

    shards = _cast_bf16([wts["w_" + n] for n in BIG], "cast_weights")
    full = dict(zip(BIG, _gather_weights(shards, "gather_weights")))

    loss, dx, gw, gg, g_rel = _local_step(x[0], p[0, 0], loss_target[0], gains, wts["rel_bias"], full)

    c_idx = lax.axis_index("c").astype(jnp.int32).reshape(1)
    mine = [gw[n].reshape(N_CHIPS, 2, gw[n].shape[1] // 2, gw[n].shape[2]) for n in BIG]
    got = _pair_swap(mine, "grad_pair_swap")
    groups = [[0, 1, 2], [5, 6, 7], [3, 4, 8, 9]]
    partial = [None] * len(BIG)
    for gi, idx in enumerate(groups):
        for k, o in zip(idx, _pair_add(c_idx, [mine[k] for k in idx], [got[k] for k in idx], "grad_pair_add%d" % gi)):
            partial[k] = o
    from_chips = _chip_scatter(partial, "grad_chip_scatter")
    halves = [None] * len(BIG)
    for gi, idx in enumerate(groups):
        for k, o in zip(idx, _chip_add([from_chips[k] for k in idx], "grad_chip_add%d" % gi)):
            halves[k] = o
    south = lax.axis_index("c") == 0
    grads = {}
    for n, own, other in zip(BIG, halves, _pair_join(halves, "grad_pair_join")):
        grads["w_" + n] = jnp.concatenate([jnp.where(south, own, other), jnp.where(south, other, own)], axis=0)

    pieces = [gg[n].reshape(-1, 128) for n in GAINS] + [jnp.pad(g_rel, ((0, 0), (0, N_REL_PAD - N_REL))).reshape(-1, 128)]
    summed = _all_sum_small(jnp.concatenate(pieces, axis=0), "small_grad_sum")
    at = 0
    for n, piece in zip(GAINS, pieces[:-1]):
        grads["g_" + n] = summed[at:at + piece.shape[0]].reshape(1, -1)[0]
        at += piece.shape[0]
    grads["rel_bias"] = summed[at:].reshape(N_HEADS, N_REL_PAD)[:, :N_REL]

    delta, new_m, new_v = {}, {}, {}
    for gi, idx in enumerate(groups):
        names = ["w_" + BIG[k] for k in idx]
        d, m, v = _adamw([wts[n] for n in names], [grads[n] for n in names], [ms[n] for n in names],
                         [vs[n] for n in names], 8, "adamw%d" % gi)
        for n, dd, mm, vv in zip(names, d, m, v):
            delta[n], new_m[n], new_v[n] = dd, mm, vv
    small = ["g_" + n for n in GAINS] + ["rel_bias"]
    as_rows = lambda a: (a.reshape(-1, 128) if a.size % 128 == 0 else jnp.pad(a, ((0, 0), (0, N_REL_PAD - N_REL))).reshape(-1, 128))
    d, m, v = _adamw([as_rows(wts[n]) for n in small], [as_rows(grads[n]) for n in small],
                     [as_rows(ms[n]) for n in small], [as_rows(vs[n]) for n in small], 1, "adamw_small")
    for n, dd, mm, vv in zip(small, d, m, v):
        back = (lambda a: a.reshape(N_HEADS, N_REL_PAD)[:, :N_REL]) if n == "rel_bias" else (lambda a: a.reshape(-1))
        delta[n], new_m[n], new_v[n] = back(dd), back(mm), back(vv)

    loss = lax.psum(loss[0, 0], ("x", "y", "c"))
    outs = [loss, dx[None]]
    for table in (grads, delta, new_m, new_v):
        outs += [(table[n].T if n in TRANSPOSED else table[n])[None] for n in ORDER]
    return tuple(outs)
```

```python
import functools

import jax
import jax.numpy as jnp
from jax import lax
from jax.experimental import pallas as pl
from jax.experimental.pallas import tpu as pltpu

F32 = jnp.float32
BF16 = jnp.bfloat16
EPS = 1e-6
N_CHIPS = 4
HEAD_DIM = 64
N_HEADS = 8
CHUNK = 64
LOOKBACK = 8
BAND = (LOOKBACK + 1) * CHUNK
PAD = LOOKBACK * CHUNK
REL_CLIP = 128
N_REL = 2 * REL_CLIP + 1
N_REL_PAD = 384
SB_BLOCK = 256
PAIR = 2 * HEAD_DIM
ATT_SCALE = HEAD_DIM ** -0.5
NEG_INF = -1e30
ROW_BLOCK = 512
VMEM_LIMIT = 48 * 1024 * 1024
MESH = pl.DeviceIdType.MESH

ADAM_LR = 0.001
ADAM_B1 = 0.9
ADAM_B2 = 0.999
ADAM_EPS = 1e-08
ADAM_WD = 0.01
ADAM_STEP = 10

NT = (((1,), (1,)), ((), ()))
TN = (((0,), (0,)), ((), ()))


def _params(n_grid, vmem=None):
    return pltpu.CompilerParams(dimension_semantics=("arbitrary",) * n_grid, vmem_limit_bytes=vmem)


def _dot(a, b, dims=None):
    if dims is None:
        return jnp.dot(a, b, preferred_element_type=F32)
    return lax.dot_general(a, b, dims, preferred_element_type=F32)


def _sigmoid(x):
    return 1.0 / (1.0 + jnp.exp(-x))


def _rms_fwd(x, g):
    r = lax.rsqrt(jnp.mean(x * x, axis=-1, keepdims=True) + EPS)
    return x * r * g


def _rms_bwd(x, g, dy):
    r = lax.rsqrt(jnp.mean(x * x, axis=-1, keepdims=True) + EPS)
    xh = x * r
    dg = jnp.sum(dy * xh, axis=0, keepdims=True)
    t = dy * g
    dx = r * (t - xh * jnp.mean(t * xh, axis=-1, keepdims=True))
    return dx, dg


def _accumulate(ref, val, first):
    @pl.when(first)
    def _():
        ref[...] = val

    @pl.when(jnp.logical_not(first))
    def _():
        ref[...] += val


def _split2(x):
    hi = x.astype(BF16)
    lo = (x - hi.astype(F32)).astype(BF16)
    return hi, lo


def _ffn_fwd(x, g_pre, g_post, wg, wu, wd, name):
    T, D = x.shape
    S, FS, _ = wg.shape
    tm = min(ROW_BLOCK, T)

    def body(x_ref, gpre_ref, gpost_ref, wg_ref, wu_ref, wd_ref,
             h_ref, xn_ref, g_ref, u_ref, a_ref, f_ref, xn_s, acc_s):
        k = pl.program_id(1)

        @pl.when(k == 0)
        def _():
            xn_s[...] = _rms_fwd(x_ref[...], gpre_ref[...]).astype(BF16)
            xn_ref[...] = xn_s[...]

        xn = xn_s[...]
        g = _dot(xn, wg_ref[0], NT)
        u = _dot(xn, wu_ref[0], NT)
        g_ref[0] = g
        u_ref[0] = u
        a = (g * _sigmoid(g) * u).astype(BF16)
        a_ref[0] = a
        _accumulate(acc_s, _dot(a, wd_ref[0]), k == 0)

        @pl.when(k == S - 1)
        def _():
            f = acc_s[...]
            f_ref[...] = f
            h_ref[...] = x_ref[...] + 0.5 * _rms_fwd(f, gpost_ref[...])

    row = pl.BlockSpec((tm, D), lambda i, k: (i, 0))
    vec = pl.BlockSpec((1, D), lambda i, k: (0, 0))
    act = pl.BlockSpec((1, tm, FS), lambda i, k: (k, i, 0))
    return pl.pallas_call(
        body, name=name, grid=(T // tm, S),
        in_specs=[row, vec, vec] + [pl.BlockSpec((1, FS, D), lambda i, k: (k, 0, 0))] * 3,
        out_specs=[row, row, act, act, act, row],
        out_shape=[jax.ShapeDtypeStruct((T, D), F32), jax.ShapeDtypeStruct((T, D), BF16),
                   jax.ShapeDtypeStruct((S, T, FS), F32), jax.ShapeDtypeStruct((S, T, FS), F32),
                   jax.ShapeDtypeStruct((S, T, FS), BF16), jax.ShapeDtypeStruct((T, D), F32)],
        scratch_shapes=[pltpu.VMEM((tm, D), BF16), pltpu.VMEM((tm, D), F32)],
        compiler_params=_params(2, VMEM_LIMIT),
    )(x, g_pre, g_post, wg, wu, wd)


def _ffn_bwd_act(dh, f, g_post, wd, g_act, u_act, name):
    T, D = dh.shape
    S, FS, _ = wd.shape
    tm = min(ROW_BLOCK, T)

    def body(dh_ref, f_ref, gpost_ref, wd_ref, g_ref, u_ref, dgp_ref, dup_ref, df_ref, dgain_ref, df_s):
        i, k = pl.program_id(0), pl.program_id(1)

        @pl.when(k == 0)
        def _():
            df, dgain = _rms_bwd(f_ref[...], gpost_ref[...], 0.5 * dh_ref[...])
            df_s[...] = df.astype(BF16)
            df_ref[...] = df_s[...]
            _accumulate(dgain_ref, dgain, i == 0)

        da = _dot(df_s[...], wd_ref[0], NT)
        g = g_ref[0]
        s = _sigmoid(g)
        dup_ref[0] = (da * (g * s)).astype(BF16)
        dgp_ref[0] = (da * u_ref[0] * (s * (1.0 + g * (1.0 - s)))).astype(BF16)

    row = pl.BlockSpec((tm, D), lambda i, k: (i, 0))
    vec = pl.BlockSpec((1, D), lambda i, k: (0, 0))
    act = pl.BlockSpec((1, tm, FS), lambda i, k: (k, i, 0))
    return pl.pallas_call(
        body, name=name, grid=(T // tm, S),
        in_specs=[row, row, vec, pl.BlockSpec((1, FS, D), lambda i, k: (k, 0, 0)), act, act],
        out_specs=[act, act, row, vec],
        out_shape=[jax.ShapeDtypeStruct((S, T, FS), BF16), jax.ShapeDtypeStruct((S, T, FS), BF16),
                   jax.ShapeDtypeStruct((T, D), BF16), jax.ShapeDtypeStruct((1, D), F32)],
        scratch_shapes=[pltpu.VMEM((tm, D), BF16)],
        compiler_params=_params(2, VMEM_LIMIT),
    )(dh, f, g_post, wd, g_act, u_act)


def _proj_bwd(dys, ws, x, g_pre, dh, name):
    T, D = x.shape
    n = len(dys)
    flat = dys[0].ndim == 2
    S = ws[0].shape[0]
    N = ws[0].shape[2] if flat else ws[0].shape[1]
    tm = min(ROW_BLOCK, T)

    def body(*refs):
        dy_refs, w_refs = refs[:n], refs[n:2 * n]
        x_ref, gpre_ref, dh_ref, dx_ref, dgain_ref, acc_s = refs[2 * n:]
        i, k = pl.program_id(0), pl.program_id(1)
        part = None
        for dy_ref, w_ref in zip(dy_refs, w_refs):
            term = _dot(dy_ref[...], w_ref[0], NT) if flat else _dot(dy_ref[0], w_ref[0])
            part = term if part is None else part + term
        _accumulate(acc_s, part, k == 0)

        @pl.when(k == S - 1)
        def _():
            dx, dgain = _rms_bwd(x_ref[...], gpre_ref[...], acc_s[...])
            dx_ref[...] = dh_ref[...] + dx
            _accumulate(dgain_ref, dgain, i == 0)

    row = pl.BlockSpec((tm, D), lambda i, k: (i, 0))
    vec = pl.BlockSpec((1, D), lambda i, k: (0, 0))
    return pl.pallas_call(
        body, name=name, grid=(T // tm, S),
        in_specs=[pl.BlockSpec((tm, N), lambda i, k: (i, k)) if flat else pl.BlockSpec((1, tm, N), lambda i, k: (k, i, 0))] * n
        + [pl.BlockSpec((1,) + ws[0].shape[1:], lambda i, k: (k, 0, 0))] * n + [row, vec, row],
        out_specs=[row, vec],
        out_shape=[jax.ShapeDtypeStruct((T, D), F32), jax.ShapeDtypeStruct((1, D), F32)],
        scratch_shapes=[pltpu.VMEM((tm, D), F32)],
        compiler_params=_params(2, VMEM_LIMIT),
    )(*dys, *ws, x, g_pre, dh)


def _mm_tn(a, b, bm, name, groups=None):
    ga, T, M = a.shape
    if groups is None:
        gb, _, N = b.shape
        b_spec = pl.BlockSpec((1, T, N), (lambda g, m: (g, 0, 0)) if gb > 1 else (lambda g, m: (0, 0, 0)))
    else:
        gb, N = groups, b.shape[1] // groups
        b_spec = pl.BlockSpec((T, N), lambda g, m: (0, g))
    G = max(ga, gb)

    def body(a_ref, b_ref, o_ref):
        bv = b_ref[0] if groups is None else b_ref[...]
        o_ref[0] = _dot(a_ref[0].astype(BF16), bv.astype(BF16), TN)

    return pl.pallas_call(
        body, name=name, grid=(G, M // bm),
        in_specs=[pl.BlockSpec((1, T, bm), (lambda g, m: (g, 0, m)) if ga > 1 else (lambda g, m: (0, 0, m))), b_spec],
        out_specs=pl.BlockSpec((1, bm, N), lambda g, m: (g, m, 0)),
        out_shape=jax.ShapeDtypeStruct((G, M, N), F32),
        compiler_params=_params(2, VMEM_LIMIT),
    )(a, b)


def _norm_proj(x, g_pre, w, name):
    T, D = x.shape
    S, _, N = w.shape
    tm = min(ROW_BLOCK, T)

    def body(x_ref, g_ref, w_ref, o_ref, xn_ref, xn_s):
        @pl.when(pl.program_id(1) == 0)
        def _():
            xn_s[...] = _rms_fwd(x_ref[...], g_ref[...]).astype(BF16)
            xn_ref[...] = xn_s[...]

        o_ref[...] = _dot(xn_s[...], w_ref[0]).astype(BF16)

    row = pl.BlockSpec((tm, D), lambda i, k: (i, 0))
    return pl.pallas_call(
        body, name=name, grid=(T // tm, S),
        in_specs=[row, pl.BlockSpec((1, D), lambda i, k: (0, 0)), pl.BlockSpec((1, D, N), lambda i, k: (k, 0, 0))],
        out_specs=[pl.BlockSpec((tm, N), lambda i, k: (i, k)), row],
        out_shape=[jax.ShapeDtypeStruct((T, S * N), BF16), jax.ShapeDtypeStruct((T, D), BF16)],
        scratch_shapes=[pltpu.VMEM((tm, D), BF16)],
        compiler_params=_params(2, VMEM_LIMIT),
    )(x, g_pre, w)


def _mix_out_fwd(h, o_a, o_b, g_sb, g_ch, w_out, g_post, name):
    T, D = h.shape
    W = g_sb.shape[1]
    tm = min(ROW_BLOCK, T)

    def body(h_ref, oa_ref, ob_ref, gsb_ref, gch_ref, w_ref, gpost_ref, h2_ref, mixed_ref, mo_ref):
        mixed_ref[:, :W] = _rms_fwd(oa_ref[...], gsb_ref[...]).astype(BF16)
        mixed_ref[:, W:] = _rms_fwd(ob_ref[...], gch_ref[...]).astype(BF16)
        mo = _dot(mixed_ref[...], w_ref[...])
        mo_ref[...] = mo
        h2_ref[...] = h_ref[...] + _rms_fwd(mo, gpost_ref[...])

    row = pl.BlockSpec((tm, D), lambda i: (i, 0))
    part = pl.BlockSpec((tm, W), lambda i: (i, 0))
    half = pl.BlockSpec((1, W), lambda i: (0, 0))
    return pl.pallas_call(
        body, name=name, grid=(T // tm,),
        in_specs=[row, part, part, half, half, pl.BlockSpec((D, D), lambda i: (0, 0)), pl.BlockSpec((1, D), lambda i: (0, 0))],
        out_specs=[row, row, row],
        out_shape=[jax.ShapeDtypeStruct((T, D), F32), jax.ShapeDtypeStruct((T, D), BF16),
                   jax.ShapeDtypeStruct((T, D), F32)],
        compiler_params=_params(1, VMEM_LIMIT),
    )(h, o_a, o_b, g_sb, g_ch, w_out, g_post)


def _mix_out_bwd(dh, mo, g_post, w_out, o_a, o_b, g_sb, g_ch, name):
    T, D = dh.shape
    W = g_sb.shape[1]
    tm = min(ROW_BLOCK, T)

    def body(dh_ref, mo_ref, gpost_ref, w_ref, oa_ref, ob_ref, gsb_ref, gch_ref,
             dmo_ref, doa_ref, dob_ref, dgpost_ref, dgsb_ref, dgch_ref):
        first = pl.program_id(0) == 0
        dmo, dgpost = _rms_bwd(mo_ref[...], gpost_ref[...], dh_ref[...])
        dmo_ref[...] = dmo.astype(BF16)
        dmix = _dot(dmo_ref[...], w_ref[...], NT)
        doa_ref[...], dgsb = _rms_bwd(oa_ref[...], gsb_ref[...], dmix[:, :W])
        dob_ref[...], dgch = _rms_bwd(ob_ref[...], gch_ref[...], dmix[:, W:])
        _accumulate(dgpost_ref, dgpost, first)
        _accumulate(dgsb_ref, dgsb, first)
        _accumulate(dgch_ref, dgch, first)

    row = pl.BlockSpec((tm, D), lambda i: (i, 0))
    part = pl.BlockSpec((tm, W), lambda i: (i, 0))
    vec = pl.BlockSpec((1, D), lambda i: (0, 0))
    half = pl.BlockSpec((1, W), lambda i: (0, 0))
    return pl.pallas_call(
        body, name=name, grid=(T // tm,),
        in_specs=[row, row, vec, pl.BlockSpec((D, D), lambda i: (0, 0)), part, part, half, half],
        out_specs=[row, part, part, vec, half, half],
        out_shape=[jax.ShapeDtypeStruct((T, D), BF16), jax.ShapeDtypeStruct((T, W), F32),
                   jax.ShapeDtypeStruct((T, W), F32), jax.ShapeDtypeStruct((1, D), F32),
                   jax.ShapeDtypeStruct((1, W), F32), jax.ShapeDtypeStruct((1, W), F32)],
        compiler_params=_params(1, VMEM_LIMIT),
    )(dh, mo, g_post, w_out, o_a, o_b, g_sb, g_ch)


def _ple_loss(h, p, target, w_proj, w_gate, g_post, name):
    T, D = h.shape
    P = p.shape[1]
    S = N_CHIPS
    C = D // S
    tm = min(ROW_BLOCK, T)

    def body(h_ref, p_ref, t_ref, wp_ref, wg_ref, g_ref, loss_ref, dh_ref, dproj_ref, dgate_ref, dgain_ref):
        first = pl.program_id(0) == 0
        h3 = h_ref[...]
        proj = _dot(p_ref[...].astype(BF16), wp_ref[...])
        s = _sigmoid(_dot(h3.astype(BF16), wg_ref[...]))
        e = proj * s
        diff = h3 + _rms_fwd(e, g_ref[...]) - t_ref[...]
        part = 0.5 * jnp.sum(jnp.mean(diff * diff, axis=-1, keepdims=True), axis=0, keepdims=True)
        _accumulate(loss_ref, jnp.broadcast_to(part, loss_ref.shape), first)
        dy = diff * (1.0 / D)
        de, dgain = _rms_bwd(e, g_ref[...], dy)
        _accumulate(dgain_ref, dgain, first)
        dproj = (de * s).astype(BF16)
        for j in range(S):
            dproj_ref[j] = dproj[:, j * C:(j + 1) * C]
        dgate_ref[...] = (de * proj * s * (1.0 - s)).astype(BF16)
        dh_ref[...] = dy + _dot(dgate_ref[...], wg_ref[...], NT)

    row = pl.BlockSpec((tm, D), lambda i: (i, 0))
    vec = pl.BlockSpec((1, D), lambda i: (0, 0))
    return pl.pallas_call(
        body, name=name, grid=(T // tm,),
        in_specs=[row, pl.BlockSpec((tm, P), lambda i: (i, 0)), row,
                  pl.BlockSpec((P, D), lambda i: (0, 0)), pl.BlockSpec((D, D), lambda i: (0, 0)), vec],
        out_specs=[pl.BlockSpec((8, 128), lambda i: (0, 0)), row,
                   pl.BlockSpec((S, tm, C), lambda i: (0, i, 0)), row, vec],
        out_shape=[jax.ShapeDtypeStruct((8, 128), F32), jax.ShapeDtypeStruct((T, D), F32),
                   jax.ShapeDtypeStruct((S, T, C), BF16), jax.ShapeDtypeStruct((T, D), BF16),
                   jax.ShapeDtypeStruct((1, D), F32)],
        compiler_params=_params(1, VMEM_LIMIT),
    )(h, p, target, w_proj, w_gate, g_post)


def _sb_scores(q, kj, mask):
    z = _dot(q, kj, NT) * ATT_SCALE
    sp = jnp.maximum(z, 0.0) + jnp.log(1.0 + jnp.exp(-jnp.abs(z)))
    lf = -sp if mask is None else jnp.where(mask, -sp, 0.0)
    return z, sp, lf


def _strict_causal():
    rows = lax.broadcasted_iota(jnp.int32, (SB_BLOCK, SB_BLOCK), 0)
    cols = lax.broadcasted_iota(jnp.int32, (SB_BLOCK, SB_BLOCK), 1)
    return cols < rows


def _tri(cmp):
    r = lax.broadcasted_iota(jnp.int32, (SB_BLOCK, SB_BLOCK), 0)
    c = lax.broadcasted_iota(jnp.int32, (SB_BLOCK, SB_BLOCK), 1)
    return jnp.where(cmp(r, c), 1.0, 0.0).astype(BF16)


def _cum(x, tri):
    hi, lo = _split2(x)
    return _dot(hi, tri) + _dot(lo, tri)


def _pair_lanes():
    lane = lax.broadcasted_iota(jnp.int32, (1, PAIR), 1)
    return [lane < HEAD_DIM, lane >= HEAD_DIM]


def _only(lanes, x):
    return jnp.where(lanes, x, jnp.zeros_like(x))


def _sb_fwd(qkv, name):
    T = qkv.shape[0]
    B = SB_BLOCK
    pairs = N_HEADS // 2

    def body(q_ref, k_ref, v_ref, o_ref, tot_ref):
        i = pl.program_id(1)
        after = _tri(lambda r, c: r > c)
        lanes = _pair_lanes()
        q = [_only(lanes[h], q_ref[...]) for h in range(2)]

        def tile(h, j, carry, mask):
            run, acc = carry
            at = pl.ds(pl.multiple_of(j * B, B), B)
            z, sp, lf = _sb_scores(q[h], k_ref[at, :], mask)
            a = jnp.exp((z - sp) + _cum(lf, after) + run)
            if mask is not None:
                a = jnp.where(mask, a, 0.0)
            return (run + jnp.sum(lf, axis=1, keepdims=True),
                    acc + _dot(a.astype(BF16), _only(lanes[h], v_ref[at, :])))

        zero = (jnp.zeros((B, 1), F32), jnp.zeros((B, PAIR), F32))
        diag = _strict_causal()
        carries = tuple(tile(h, i, zero, diag) for h in range(2))
        carries = lax.fori_loop(
            0, i, lambda jj, cs: tuple(tile(h, i - 1 - jj, cs[h], None) for h in range(2)), carries)
        o_ref[...] = carries[0][1] + carries[1][1]
        tot_ref[...] = jnp.where(lanes[0], carries[0][0], carries[1][0])

    blk = lambda off: pl.BlockSpec((B, PAIR), lambda g, i: (i, g + off))
    full = lambda off: pl.BlockSpec((T, PAIR), lambda g, i: (0, g + off))
    out = jax.ShapeDtypeStruct((T, pairs * PAIR), F32)
    return pl.pallas_call(
        body, name=name, grid=(pairs, T // B),
        in_specs=[blk(0), full(pairs), full(2 * pairs)],
        out_specs=[blk(0), blk(0)],
        out_shape=[out, out],
        compiler_params=_params(2, VMEM_LIMIT),
    )(qkv, qkv, qkv)


def _sb_bwd(qkv, do, tot, name):
    T = qkv.shape[0]
    B = SB_BLOCK
    pairs = N_HEADS // 2
    n_blocks = T // B

    def body(q_ref, k_ref, v_ref, do_ref, tot_ref, dq_ref, dk_ref, dv_ref, dk_s, dv_s):
        i = pl.program_id(1)

        @pl.when(i == 0)
        def _():
            dk_s[...] = jnp.zeros_like(dk_s)
            dv_s[...] = jnp.zeros_like(dv_s)

        upto = _tri(lambda r, c: r <= c)
        below = _tri(lambda r, c: r < c)
        lanes = _pair_lanes()
        q = [_only(lanes[h], q_ref[...]) for h in range(2)]
        dob = do_ref[...].astype(BF16)
        do = [_only(lanes[h], dob) for h in range(2)]
        tot = [tot_ref[:, 0:1], tot_ref[:, HEAD_DIM:HEAD_DIM + 1]]

        def tile(h, j, carry, mask):
            pre_lf, pre_g, dq = carry
            at = pl.ds(pl.multiple_of(j * B, B), B)
            kj, vj = k_ref[at, :], v_ref[at, :]
            z, sp, lf = _sb_scores(q[h], kj, mask)
            later = tot[h] - pre_lf - _cum(lf, upto)
            a = jnp.exp((z - sp) + later)
            if mask is not None:
                a = jnp.where(mask, a, 0.0)
            g = a * _dot(do[h], vj, NT)
            g_before = pre_g + _cum(g, below)
            fail = jnp.exp(-sp)
            dz = (g * fail - (1.0 - fail) * g_before) * ATT_SCALE
            if mask is not None:
                dz = jnp.where(mask, dz, 0.0)
            dzb = dz.astype(BF16)
            dk_s[at, :] += _dot(dzb, q[h], TN)
            dv_s[at, :] += _dot(a.astype(BF16), do[h], TN)
            return (pre_lf + jnp.sum(lf, axis=1, keepdims=True), pre_g + jnp.sum(g, axis=1, keepdims=True),
                    dq + _dot(dzb, _only(lanes[h], kj)))

        col = jnp.zeros((B, 1), F32)
        zero = (col, col, jnp.zeros((B, PAIR), F32))
        carries = lax.fori_loop(
            0, i, lambda j, cs: tuple(tile(h, j, cs[h], None) for h in range(2)), (zero, zero))
        diag = _strict_causal()
        dq_ref[...] = (tile(0, i, carries[0], diag)[2] + tile(1, i, carries[1], diag)[2]).astype(BF16)

        @pl.when(i == n_blocks - 1)
        def _():
            dk_ref[...] = dk_s[...].astype(BF16)
            dv_ref[...] = dv_s[...].astype(BF16)

    blk = lambda off: pl.BlockSpec((B, PAIR), lambda g, i: (i, g + off))
    full = lambda off: pl.BlockSpec((T, PAIR), lambda g, i: (0, g + off))
    out = jax.ShapeDtypeStruct((T, pairs * PAIR), BF16)
    return pl.pallas_call(
        body, name=name, grid=(pairs, n_blocks),
        in_specs=[blk(0), full(pairs), full(2 * pairs), blk(0), blk(0)],
        out_specs=[blk(0), full(0), full(0)],
        out_shape=[out, out, out],
        scratch_shapes=[pltpu.VMEM((T, PAIR), F32)] * 2,
        compiler_params=_params(2, VMEM_LIMIT),
    )(qkv, qkv, qkv, do, tot)


def _rel_onehot(i, transposed):
    shape = (BAND, N_REL_PAD) if transposed else (N_REL_PAD, BAND)
    j = lax.broadcasted_iota(jnp.int32, shape, 0 if transposed else 1)
    r = lax.broadcasted_iota(jnp.int32, shape, 1 if transposed else 0)
    idx = jnp.clip(i + PAD - j, -REL_CLIP, REL_CLIP) + REL_CLIP
    return jnp.where(idx == r, 1.0, 0.0).astype(BF16)


def _bias_table(rel_bias_pad, name):
    def body(rb_ref, o_ref):
        onehot = _rel_onehot(pl.program_id(0), False)
        rb = rb_ref[...]
        hi, lo = _split2(rb)
        lo2 = (rb - hi.astype(F32) - lo.astype(F32)).astype(BF16)
        o_ref[0] = _dot(hi, onehot) + _dot(lo, onehot) + _dot(lo2, onehot)

    return pl.pallas_call(
        body, name=name, grid=(CHUNK,),
        in_specs=[pl.BlockSpec((N_HEADS, N_REL_PAD), lambda i: (0, 0))],
        out_specs=pl.BlockSpec((1, N_HEADS, BAND), lambda i: (i, 0, 0)),
        out_shape=jax.ShapeDtypeStruct((CHUNK, N_HEADS, BAND), F32),
        compiler_params=_params(1),
    )(rel_bias_pad)


def _bias_grad(dbias_t, name):
    def body(d_ref, o_ref):
        onehot = _rel_onehot(pl.program_id(0), True)
        hi, lo = _split2(d_ref[0])
        _accumulate(o_ref, _dot(hi, onehot) + _dot(lo, onehot), pl.program_id(0) == 0)

    return pl.pallas_call(
        body, name=name, grid=(CHUNK,),
        in_specs=[pl.BlockSpec((1, N_HEADS, BAND), lambda i: (i, 0, 0))],
        out_specs=pl.BlockSpec((N_HEADS, N_REL_PAD), lambda i: (0, 0)),
        out_shape=jax.ShapeDtypeStruct((N_HEADS, N_REL_PAD), F32),
        compiler_params=_params(1),
    )(dbias_t)


def _ch_probs(q, kw, bias, valid):
    z = jnp.where(valid, _dot(q, kw, NT) * ATT_SCALE + bias, NEG_INF)
    e = jnp.exp(z - jnp.max(z, axis=-1, keepdims=True))
    return e / jnp.sum(e, axis=-1, keepdims=True)


def _ch_valid(n):
    slot = lax.broadcasted_iota(jnp.int32, (CHUNK, BAND), 1) // CHUNK
    return n + slot - LOOKBACK >= 0


def _ch_fwd(qkv, bias, name):
    T = qkv.shape[0]
    W = N_HEADS * HEAD_DIM

    def body(q_ref, k_ref, v_ref, b_ref, o_ref, kp, vp):
        n = pl.program_id(0)

        @pl.when(n == 0)
        def _():
            _ch_load_padded(k_ref, v_ref, kp, vp)

        win = pl.ds(pl.multiple_of(n * CHUNK, CHUNK), BAND)
        valid = _ch_valid(n)
        lanes = _pair_lanes()
        for pair in range(N_HEADS // 2):
            cols = slice(pair * PAIR, (pair + 1) * PAIR)
            q, kw, vw = q_ref[:, cols], kp[win, cols], vp[win, cols]
            o = None
            for h in range(2):
                p = _ch_probs(_only(lanes[h], q), kw, b_ref[2 * pair + h], valid)
                part = _dot(p.astype(BF16), _only(lanes[h], vw))
                o = part if o is None else o + part
            o_ref[:, cols] = o

    full = lambda col: pl.BlockSpec((T, W), lambda n: (0, col))
    return pl.pallas_call(
        body, name=name, grid=(T // CHUNK,),
        in_specs=[pl.BlockSpec((CHUNK, W), lambda n: (n, 3)), full(4), full(5),
                  pl.BlockSpec((N_HEADS, CHUNK, BAND), lambda n: (0, 0, 0))],
        out_specs=pl.BlockSpec((CHUNK, W), lambda n: (n, 0)),
        out_shape=jax.ShapeDtypeStruct((T, W), F32),
        scratch_shapes=[pltpu.VMEM((PAD + T, W), BF16)] * 2,
        compiler_params=_params(1, VMEM_LIMIT),
    )(qkv, qkv, qkv, bias)


def _ch_load_padded(k_ref, v_ref, kp, vp):
    for src, dst in ((k_ref, kp), (v_ref, vp)):
        dst[:PAD, :] = jnp.zeros((PAD, dst.shape[1]), dst.dtype)
        dst[PAD:, :] = src[...]


def _ch_bwd(qkv, bias, do, name):
    T = qkv.shape[0]
    W = N_HEADS * HEAD_DIM
    n_chunks = T // CHUNK

    def body(q_ref, k_ref, v_ref, b_ref, do_ref, dq_ref, dk_ref, dv_ref, db_ref, kp, vp, dk_s, dv_s):
        n = pl.program_id(0)

        @pl.when(n == 0)
        def _():
            _ch_load_padded(k_ref, v_ref, kp, vp)
            dk_s[...] = jnp.zeros_like(dk_s)
            dv_s[...] = jnp.zeros_like(dv_s)
            db_ref[...] = jnp.zeros_like(db_ref)

        win = pl.ds(pl.multiple_of(n * CHUNK, CHUNK), BAND)
        valid = _ch_valid(n)
        lanes = _pair_lanes()
        for pair in range(N_HEADS // 2):
            cols = slice(pair * PAIR, (pair + 1) * PAIR)
            q, kw, vw = q_ref[:, cols], kp[win, cols], vp[win, cols]
            dob = do_ref[:, cols].astype(BF16)
            dq = dk = dv = None
            for h in range(2):
                qh, doh = _only(lanes[h], q), _only(lanes[h], dob)
                p = _ch_probs(qh, kw, b_ref[2 * pair + h], valid)
                dp = _dot(doh, vw, NT)
                dz = p * (dp - jnp.sum(dp * p, axis=-1, keepdims=True))
                db_ref[2 * pair + h] += dz
                dzb = (dz * ATT_SCALE).astype(BF16)
                parts = (_dot(dzb, _only(lanes[h], kw)), _dot(dzb, qh, TN), _dot(p.astype(BF16), doh, TN))
                dq, dk, dv = parts if dq is None else (dq + parts[0], dk + parts[1], dv + parts[2])
            dq_ref[:, cols] = dq.astype(BF16)
            dk_s[win, cols] += dk
            dv_s[win, cols] += dv

        @pl.when(n == n_chunks - 1)
        def _():
            dk_ref[...] = dk_s[PAD:, :].astype(BF16)
            dv_ref[...] = dv_s[PAD:, :].astype(BF16)

    full = lambda col: pl.BlockSpec((T, W), lambda n: (0, col))
    blk = lambda col: pl.BlockSpec((CHUNK, W), lambda n: (n, col))
    tab = pl.BlockSpec((N_HEADS, CHUNK, BAND), lambda n: (0, 0, 0))
    out = jax.ShapeDtypeStruct((T, W), BF16)
    return pl.pallas_call(
        body, name=name, grid=(n_chunks,),
        in_specs=[blk(3), full(4), full(5), tab, blk(0)],
        out_specs=[blk(0), full(0), full(0), tab],
        out_shape=[out, out, out, jax.ShapeDtypeStruct((N_HEADS, CHUNK, BAND), F32)],
        scratch_shapes=[pltpu.VMEM((PAD + T, W), BF16)] * 2 + [pltpu.VMEM((PAD + T, W), F32)] * 2,
        compiler_params=_params(1, VMEM_LIMIT),
    )(qkv, qkv, qkv, bias, do)


def _rows_split(a, parts):
    return a.reshape(a.shape[:-2] + (parts, a.shape[-2] // parts, a.shape[-1]))


def _cast_bf16(ws, name):
    parts = 4
    ws = [_rows_split(w, parts) for w in ws]

    def body(*refs):
        n = len(refs) // 2
        for src, dst in zip(refs[:n], refs[n:]):
            dst[...] = src[...].astype(BF16)

    spec = lambda w: pl.BlockSpec((1,) + w.shape[1:], lambda i: (i, 0, 0))
    outs = pl.pallas_call(
        body, name=name, grid=(parts,),
        in_specs=[spec(w) for w in ws], out_specs=[spec(w) for w in ws],
        out_shape=[jax.ShapeDtypeStruct(w.shape, BF16) for w in ws],
        compiler_params=_params(1, VMEM_LIMIT),
    )(*ws)
    return [o.reshape(o.shape[0] * o.shape[1], o.shape[2]) for o in outs]


def _pair_add(c, mine, got, permuted, name):
    parts = 2
    mine = [_rows_split(m, parts) for m in mine]
    got = [_rows_split(g, parts) for g in got]
    n = len(mine)

    def body(c_ref, *refs):
        for a, b, o in zip(refs[:n], refs[n:2 * n], refs[2 * n:]):
            o[0, 0] = (a[0, 0, 0] + b[0, 0]).astype(BF16)

    def mine_spec(m, perm):
        if perm:
            return pl.BlockSpec((1, 1, 1) + m.shape[3:], lambda j, r, c_ref: (j, 0, r, 0, 0))
        return pl.BlockSpec((1, 1, 1) + m.shape[3:], lambda j, r, c_ref: (j, c_ref[0], r, 0, 0))

    def got_spec(g):
        return pl.BlockSpec((1, 1) + g.shape[2:], lambda j, r, c_ref: (j, r, 0, 0))

    outs = pl.pallas_call(
        body, name=name,
        grid_spec=pltpu.PrefetchScalarGridSpec(
            num_scalar_prefetch=1, grid=(N_CHIPS, parts),
            in_specs=[mine_spec(m, perm) for m, perm in zip(mine, permuted)] + [got_spec(g) for g in got],
            out_specs=[got_spec(g) for g in got]),
        out_shape=[jax.ShapeDtypeStruct(g.shape, BF16) for g in got],
        compiler_params=_params(2, VMEM_LIMIT),
    )(c, *mine, *got)
    return [o.reshape(o.shape[0], o.shape[1] * o.shape[2], o.shape[3]) for o in outs]


def _chip_add(me, partials, landed, permuted, name):
    parts = 2
    ps = [_rows_split(x, parts) for x in partials]
    ls = [_rows_split(x, parts) for x in landed]
    n = len(ps)

    def body(me_ref, *refs):
        for own, got, o in zip(refs[:n], refs[n:2 * n], refs[2 * n:]):
            acc = own[0, 0].astype(F32)
            for r in range(N_CHIPS - 1):
                acc = acc + got[r, 0].astype(F32)
            o[0] = acc

    def own_spec(x, perm):
        if perm:
            return pl.BlockSpec((1, 1) + x.shape[2:], lambda r, me_ref: (0, r, 0, 0))
        return pl.BlockSpec((1, 1) + x.shape[2:], lambda r, me_ref: (me_ref[0], r, 0, 0))

    outs = pl.pallas_call(
        body, name=name,
        grid_spec=pltpu.PrefetchScalarGridSpec(
            num_scalar_prefetch=1, grid=(parts,),
            in_specs=[own_spec(x, perm) for x, perm in zip(ps, permuted)]
            + [pl.BlockSpec((N_CHIPS - 1, 1) + x.shape[2:], lambda r, me_ref: (0, r, 0, 0)) for x in ls],
            out_specs=[pl.BlockSpec((1,) + x.shape[2:], lambda r, me_ref: (r, 0, 0)) for x in ps]),
        out_shape=[jax.ShapeDtypeStruct(x.shape[1:], F32) for x in ps],
        compiler_params=_params(1, VMEM_LIMIT),
    )(me, *ps, *ls)
    return [o.reshape(o.shape[0] * o.shape[1], o.shape[2]) for o in outs]


def _adamw_math(w, g, m, v):
    m = ADAM_B1 * m + (1.0 - ADAM_B1) * g
    v = ADAM_B2 * v + (1.0 - ADAM_B2) * (g * g)
    m_hat = m / (1.0 - ADAM_B1 ** ADAM_STEP)
    v_hat = v / (1.0 - ADAM_B2 ** ADAM_STEP)
    delta = -ADAM_LR * (m_hat / (jnp.sqrt(v_hat) + ADAM_EPS) + ADAM_WD * w)
    return delta, m, v


def _adamw(ws, gs, ms, vs, parts, name):
    n = len(ws)
    flat = [_rows_split(a, parts) for a in (*ws, *gs, *ms, *vs)]

    def body(*refs):
        ins, outs = refs[:4 * n], refs[4 * n:]
        for k in range(n):
            d, m, v = _adamw_math(ins[k][...], ins[n + k][...], ins[2 * n + k][...], ins[3 * n + k][...])
            outs[k][...] = d
            outs[n + k][...] = m
            outs[2 * n + k][...] = v

    spec = lambda a: pl.BlockSpec((1,) + a.shape[1:], lambda i: (i, 0, 0))
    outs = pl.pallas_call(
        body, name=name, grid=(parts,),
        in_specs=[spec(a) for a in flat], out_specs=[spec(a) for a in flat[:n]] * 3,
        out_shape=[jax.ShapeDtypeStruct(a.shape, F32) for a in flat[:n]] * 3,
        compiler_params=_params(1, VMEM_LIMIT),
    )(*flat)
    outs = [o.reshape(o.shape[0] * o.shape[1], o.shape[2]) for o in outs]
    return outs[:n], outs[n:2 * n], outs[2 * n:]


def _place():
    x, y, c = lax.axis_index("x"), lax.axis_index("y"), lax.axis_index("c")
    peers = [(x ^ (r >> 1), y ^ (r & 1), c) for r in (1, 2, 3)]
    return x, y, c, peers


def _handshake(peers):
    barrier = pltpu.get_barrier_semaphore()
    for peer in peers:
        pl.semaphore_signal(barrier, inc=1, device_id=peer, device_id_type=MESH)
    pl.semaphore_wait(barrier, len(peers))


ANY = pl.BlockSpec(memory_space=pl.ANY)
HBM = pl.BlockSpec(memory_space=pltpu.HBM)
SEM = pl.BlockSpec(memory_space=pltpu.SEMAPHORE)
SPLIT_COPY = pltpu.SideEffectType.DATAFLOW_SIDE_EFFECTING


def _in_hbm(a):
    return pltpu.with_memory_space_constraint(a, pltpu.HBM)


def _split_start(body, name, collective_id, operands, n_sems):
    n = len(operands)

    def wrapped(*refs):
        body(refs[:n], refs[n], refs[n + 1])
        token = refs[-1]
        token[...] = jnp.zeros_like(token)

    outs = pl.pallas_call(
        wrapped, name=name,
        in_specs=[HBM] * n,
        out_shape=(pltpu.SemaphoreType.DMA((n_sems,)), pltpu.SemaphoreType.DMA((n_sems,)),
                   *[pltpu.HBM(a.shape, a.dtype) for a in operands], jax.ShapeDtypeStruct((8, 128), F32)),
        out_specs=(SEM, SEM, *[HBM] * n, pl.BlockSpec(memory_space=pltpu.VMEM)),
        input_output_aliases={i: 2 + i for i in range(n)},
        compiler_params=pltpu.CompilerParams(has_side_effects=SPLIT_COPY, collective_id=collective_id),
    )(*[_in_hbm(a) for a in operands])
    return outs[0], outs[1], list(outs[2:2 + n]), outs[-1]


def _split_wait(body, name, send_sem, recv_sem, operands, after):
    n = len(operands)

    def wrapped(*refs):
        body(refs[:n], refs[n], refs[n + 1])

    outs = pl.pallas_call(
        wrapped, name=name,
        in_specs=[HBM] * n + [SEM, SEM, ANY],
        out_shape=tuple(pltpu.HBM(a.shape, a.dtype) for a in operands),
        out_specs=tuple([HBM] * n),
        input_output_aliases={i: i for i in range(n)},
        compiler_params=pltpu.CompilerParams(has_side_effects=SPLIT_COPY),
    )(*operands, send_sem, recv_sem, after)
    return list(outs)


def _gather_copies(refs, send_sem, recv_sem):
    n = len(refs) // 2
    _, _, c, peers = _place()
    return [pltpu.make_async_remote_copy(
        src_ref=refs[a].at[c], dst_ref=refs[n + a].at[r + 1, 0],
        send_sem=send_sem.at[a * 3 + r], recv_sem=recv_sem.at[a * 3 + r],
        device_id=peers[r], device_id_type=MESH) for a in range(n) for r in range(3)]


def _gather_start(shards, name, collective_id):
    def body(refs, send_sem, recv_sem):
        _handshake(_place()[3])
        for cp in _gather_copies(refs, send_sem, recv_sem):
            cp.start()

    lands = [lax.empty((N_CHIPS,) + s.shape, s.dtype) for s in shards]
    return _split_start(body, name, collective_id, list(shards) + lands, 3 * len(shards))


def _gather_wait(send_sem, recv_sem, operands, after, name):
    def body(refs, send_sem, recv_sem):
        for cp in _gather_copies(refs, send_sem, recv_sem):
            cp.wait_send()
            cp.wait_recv()

    return _split_wait(body, name, send_sem, recv_sem, operands, after)


def _gather_finish(shards, lands, absolute, name):
    n = len(shards)
    with_ici = lands is None
    n_abs = sum(absolute)

    def body(*refs):
        src = refs[:n]
        land = refs[n:2 * n] if with_ici else refs[2 * n:3 * n]
        plain = refs[3 * n - n * with_ici:3 * n - n * with_ici + n_abs]
        send_ici, recv_ici, send_d2d, recv_d2d, own_sem, fix_sem = refs[-6:]
        x, y, c, _ = _place()
        me = 2 * x + y
        own = []
        for a in range(n):
            own.append(pltpu.make_async_copy(src[a].at[c], land[a].at[0, 0], own_sem.at[2 * a]))
            own.append(pltpu.make_async_copy(src[a].at[1 - c], land[a].at[0, 1], own_sem.at[2 * a + 1]))
        for cp in own:
            cp.start()
        ici = _gather_copies(list(src) + list(land), send_ici, recv_ici) if with_ici else []
        for cp in ici:
            cp.start()
        passed = [pltpu.make_async_remote_copy(
            src_ref=land[a].at[r + 1, 0], dst_ref=land[a].at[r + 1, 1],
            send_sem=send_d2d.at[a * 3 + r], recv_sem=recv_d2d.at[a * 3 + r],
            device_id=(x, y, 1 - c), device_id_type=MESH) for a in range(n) for r in range(3)]
        for k, cp in enumerate(passed):
            if with_ici:
                ici[k].wait_recv()
            cp.start()
        for cp in passed:
            cp.wait_recv()
        for cp in own:
            cp.wait()
        fixes, k = [], 0
        for a in range(n):
            if absolute[a]:
                for j in range(N_CHIPS):
                    for h in range(2):
                        fixes.append(pltpu.make_async_copy(
                            land[a].at[j ^ me, h ^ c], plain[k].at[j, h], fix_sem.at[len(fixes)]))
                k += 1
        for cp in fixes:
            cp.start()
        for cp in ici:
            cp.wait_send()
        for cp in passed:
            cp.wait_send()
        for cp in fixes:
            cp.wait()

    land_shapes = [jax.ShapeDtypeStruct((N_CHIPS,) + s.shape, s.dtype) for s in shards]
    outs = pl.pallas_call(
        body, name=name,
        in_specs=[ANY] * (n if with_ici else 2 * n),
        out_specs=[ANY] * (n + n_abs),
        out_shape=land_shapes + [l for l, ab in zip(land_shapes, absolute) if ab],
        input_output_aliases={} if with_ici else {n + a: a for a in range(n)},
        scratch_shapes=[pltpu.SemaphoreType.DMA((3 * n,))] * 4
        + [pltpu.SemaphoreType.DMA((2 * n,)), pltpu.SemaphoreType.DMA((max(8 * n_abs, 1),))],
    )(*shards, *([] if with_ici else lands))
    result, k = [], 0
    for a in range(n):
        o = outs[a]
        if absolute[a]:
            o = outs[n + k]
            k += 1
        result.append(o.reshape(N_CHIPS, o.shape[1] * o.shape[2], o.shape[3]))
    return result


def _pair_swap(grads, permuted, name):
    n = len(grads)

    def body(*refs):
        src, dst = refs[:n], refs[n:2 * n]
        send_sem, recv_sem = refs[2 * n:]
        x, y, c, _ = _place()
        copies = [pltpu.make_async_remote_copy(
            src_ref=src[a].at[:, 1] if permuted[a] else src[a].at[:, 1 - c], dst_ref=dst[a],
            send_sem=send_sem.at[a], recv_sem=recv_sem.at[a],
            device_id=(x, y, 1 - c), device_id_type=MESH) for a in range(n)]
        for cp in copies:
            cp.start()
        for cp in copies:
            cp.wait()

    return pl.pallas_call(
        body, name=name,
        in_specs=[ANY] * n, out_specs=[ANY] * n,
        out_shape=[jax.ShapeDtypeStruct((N_CHIPS,) + g.shape[2:], F32) for g in grads],
        scratch_shapes=[pltpu.SemaphoreType.DMA((n,))] * 2,
    )(*grads)


def _scatter_copies(refs, permuted, send_sem, recv_sem):
    n = len(refs) // 2
    x, y, _, peers = _place()
    me = 2 * x + y
    return [pltpu.make_async_remote_copy(
        src_ref=refs[a].at[r + 1] if permuted[a] else refs[a].at[me ^ (r + 1)], dst_ref=refs[n + a].at[r],
        send_sem=send_sem.at[a * 3 + r], recv_sem=recv_sem.at[a * 3 + r],
        device_id=peers[r], device_id_type=MESH) for a in range(n) for r in range(3)]


def _scatter_start(partials, permuted, name, collective_id):
    def body(refs, send_sem, recv_sem):
        _handshake(_place()[3])
        for cp in _scatter_copies(refs, permuted, send_sem, recv_sem):
            cp.start()

    lands = [lax.empty((N_CHIPS - 1,) + p.shape[1:], p.dtype) for p in partials]
    return _split_start(body, name, collective_id, list(partials) + lands, 3 * len(partials))


def _scatter_wait(send_sem, recv_sem, operands, permuted, after, name):
    def body(refs, send_sem, recv_sem):
        for cp in _scatter_copies(refs, permuted, send_sem, recv_sem):
            cp.wait_send()
            cp.wait_recv()

    return _split_wait(body, name, send_sem, recv_sem, operands, after)


def _pair_join(halves, name):
    n = len(halves)

    def body(*refs):
        src, dst = refs[:n], refs[n:2 * n]
        send_sem, recv_sem = refs[2 * n:]
        x, y, c, _ = _place()
        copies = [pltpu.make_async_remote_copy(
            src_ref=src[a], dst_ref=dst[a], send_sem=send_sem.at[a], recv_sem=recv_sem.at[a],
            device_id=(x, y, 1 - c), device_id_type=MESH) for a in range(n)]
        for cp in copies:
            cp.start()
        for cp in copies:
            cp.wait()

    return pl.pallas_call(
        body, name=name,
        in_specs=[ANY] * n, out_specs=[ANY] * n,
        out_shape=[jax.ShapeDtypeStruct(h.shape, F32) for h in halves],
        scratch_shapes=[pltpu.SemaphoreType.DMA((n,))] * 2,
    )(*halves)


def _all_sum_small(v, name):
    R, C = v.shape
    n_dev = 8

    def body(v_ref, o_ref, buf, send_sem, recv_sem):
        x, y, c, _ = _place()
        me = 4 * x + 2 * y + c
        buf[me] = v_ref[...]
        copies = []
        for k in range(1, n_dev):
            peer = (x ^ (k >> 2), y ^ ((k >> 1) & 1), c ^ (k & 1))
            copies.append(pltpu.make_async_remote_copy(
                src_ref=v_ref, dst_ref=buf.at[me], send_sem=send_sem.at[k - 1], recv_sem=recv_sem.at[k - 1],
                device_id=peer, device_id_type=MESH))
        for cp in copies:
            cp.start()
        for cp in copies:
            cp.wait()
        acc = buf[0]
        for m in range(1, n_dev):
            acc = acc + buf[m]
        o_ref[...] = acc

    return pl.pallas_call(
        body, name=name,
        in_specs=[pl.BlockSpec(memory_space=pltpu.VMEM)], out_specs=pl.BlockSpec(memory_space=pltpu.VMEM),
        out_shape=jax.ShapeDtypeStruct((R, C), F32),
        scratch_shapes=[pltpu.VMEM((n_dev, R, C), F32), pltpu.SemaphoreType.DMA((n_dev - 1,)),
                        pltpu.SemaphoreType.DMA((n_dev - 1,))],
    )(v)


class _WholeWeights:
    def __init__(self, w):
        self.w = w

    def weights(self, group, after=None):
        return self.w

    def grads_ready(self, group, gw):
        return None


def _local_step(x, p, target, gains, rel_bias, hooks):
    T, D = x.shape
    S = N_CHIPS

    w = dict(hooks.weights("ffn1"))
    h1, xn1, g1, u1, a1, f1 = _ffn_fwd(x, gains["ffn1_pre"], gains["ffn1_post"], w["ffn1_gate"], w["ffn1_up"],
                                       w["ffn1_down"], "ffn1_fwd")
    w.update(hooks.weights("in", h1))
    qkv, un = _norm_proj(h1, gains["mix_pre"], w["in"], "qkv_proj")
    rb_pad = jnp.pad(rel_bias, ((0, 0), (0, N_REL_PAD - N_REL)))
    bias = _bias_table(rb_pad, "bias_table").transpose(1, 0, 2)
    o_a, tot = _sb_fwd(qkv, "sb_fwd")
    o_b = _ch_fwd(qkv, bias, "ch_fwd")
    w.update(hooks.weights("rest", o_b))
    w_out = w["out"].reshape(D, D)
    h2, mixed, mo = _mix_out_fwd(h1, o_a, o_b, gains["out_sb"], gains["out_ch"], w_out, gains["mix_post"],
                                 "mix_out_fwd")
    h3, xn2, g2, u2, a2, f2 = _ffn_fwd(h2, gains["ffn2_pre"], gains["ffn2_post"], w["ffn2_gate"], w["ffn2_up"],
                                       w["ffn2_down"], "ffn2_fwd")
    w_ple_proj = w["ple_proj"].transpose(1, 0, 2).reshape(p.shape[1], D)
    w_ple_gate = w["ple_gate"].reshape(D, D)

    loss, dh3, dproj, dgate, dg_ple = _ple_loss(h3, p, target, w_ple_proj, w_ple_gate, gains["ple_post"], "ple_loss")
    gw, gg = {}, {"ple_post": dg_ple}
    gw["ple_proj"] = _mm_tn(p[None], dproj, p.shape[1], "dw_ple_proj")
    gw["ple_gate"] = _mm_tn(h3[None], dgate[None], 512, "dw_ple_gate").reshape(S, D // S, D)

    def ffn_bwd(tag, dh, x_in, xn, g_act, u_act, a_act, f):
        dgp, dup, df, gg[tag + "_post"] = _ffn_bwd_act(dh, f, gains[tag + "_post"], w[tag + "_down"], g_act, u_act,
                                                       tag + "_bwd_act")
        gw[tag + "_gate"] = _mm_tn(dgp, xn[None], dgp.shape[2], "dw_" + tag + "_gate")
        gw[tag + "_up"] = _mm_tn(dup, xn[None], dup.shape[2], "dw_" + tag + "_up")
        gw[tag + "_down"] = _mm_tn(a_act, df[None], a_act.shape[2], "dw_" + tag + "_down")
        dx, gg[tag + "_pre"] = _proj_bwd([dgp, dup], [w[tag + "_gate"], w[tag + "_up"]], x_in, gains[tag + "_pre"], dh,
                                         tag + "_bwd_in")
        return dx

    dh2 = ffn_bwd("ffn2", dh3, h2, xn2, g2, u2, a2, f2)
    dmo, do_a, do_b, gg["mix_post"], gg["out_sb"], gg["out_ch"] = _mix_out_bwd(
        dh2, mo, gains["mix_post"], w_out, o_a, o_b, gains["out_sb"], gains["out_ch"], "mix_out_bwd")
    gw["out"] = _mm_tn(mixed[None], dmo[None], 512, "dw_out").reshape(S, D // S, D)
    token = hooks.grads_ready("early", gw)
    if token is not None:
        tot = tot + token[0, 0]
    dq_a, dk_a, dv_a = _sb_bwd(qkv, do_a, tot, "sb_bwd")
    dq_b, dk_b, dv_b, dbias = _ch_bwd(qkv, bias, do_b, "ch_bwd")
    g_rel = _bias_grad(dbias.transpose(1, 0, 2), "bias_grad")[:, :N_REL]
    dqkv = jnp.concatenate([dq_a, dk_a, dv_a, dq_b, dk_b, dv_b], axis=1)
    gw["in"] = _mm_tn(un[None], dqkv, 512, "dw_in", groups=S)
    dh1, gg["mix_pre"] = _proj_bwd([dqkv], [w["in"]], h1, gains["mix_pre"], dh2, "qkv_bwd_in")
    dx = ffn_bwd("ffn1", dh1, x, xn1, g1, u1, a1, f1)
    return loss, dx, gw, gg, g_rel


BIG = ["ffn1_gate", "ffn1_up", "ffn1_down", "in", "out", "ffn2_gate", "ffn2_up", "ffn2_down", "ple_proj", "ple_gate"]
GAINS = ["ffn1_pre", "ffn1_post", "mix_pre", "mix_post", "out_sb", "out_ch", "ffn2_pre", "ffn2_post", "ple_post"]
TRANSPOSED = ("w_ffn1_gate", "w_ffn1_up", "w_ffn2_gate", "w_ffn2_up")
PERMUTED = ("ffn1_gate", "ffn1_up", "ffn1_down", "ffn2_gate", "ffn2_up", "ffn2_down")
W_GROUPS = {"ffn1": ["ffn1_gate", "ffn1_up", "ffn1_down"], "in": ["in"],
            "rest": ["out", "ffn2_gate", "ffn2_up", "ffn2_down", "ple_proj", "ple_gate"]}
G_GROUPS = {"early": ["ple_proj", "ple_gate", "ffn2_gate", "ffn2_up", "ffn2_down", "out"],
            "late": ["in", "ffn1_gate", "ffn1_up", "ffn1_down"]}
ORDER = ["g_ffn1_pre", "g_ffn1_post", "w_ffn1_gate", "w_ffn1_up", "w_ffn1_down", "g_mix_pre", "g_mix_post", "w_in",
         "g_out_sb", "g_out_ch", "rel_bias", "w_out", "g_ffn2_pre", "g_ffn2_post", "w_ffn2_gate", "w_ffn2_up",
         "w_ffn2_down", "w_ple_proj", "w_ple_gate", "g_ple_post"]


def kernel(x, p, g_ffn1_pre, g_ffn1_post, w_ffn1_gate, w_ffn1_up, w_ffn1_down, g_mix_pre, g_mix_post, w_in, g_out_sb, g_out_ch, rel_bias, w_out, g_ffn2_pre, g_ffn2_post, w_ffn2_gate, w_ffn2_up, w_ffn2_down, w_ple_proj, w_ple_gate, g_ple_post, loss_target, m_g_ffn1_pre, m_g_ffn1_post, m_w_ffn1_gate, m_w_ffn1_up, m_w_ffn1_down, m_g_mix_pre, m_g_mix_post, m_w_in, m_g_out_sb, m_g_out_ch, m_rel_bias, m_w_out, m_g_ffn2_pre, m_g_ffn2_post, m_w_ffn2_gate, m_w_ffn2_up, m_w_ffn2_down, m_w_ple_proj, m_w_ple_gate, m_g_ple_post, v_g_ffn1_pre, v_g_ffn1_post, v_w_ffn1_gate, v_w_ffn1_up, v_w_ffn1_down, v_g_mix_pre, v_g_mix_post, v_w_in, v_g_out_sb, v_g_out_ch, v_rel_bias, v_w_out, v_g_ffn2_pre, v_g_ffn2_post, v_w_ffn2_gate, v_w_ffn2_up, v_w_ffn2_down, v_w_ple_proj, v_w_ple_gate, v_g_ple_post):
    args = dict(locals())
    take = lambda a, n: a[0].T if n in TRANSPOSED else a[0]
    wts = {n: take(args[n], n) for n in ORDER}
    ms = {n: take(args["m_" + n], n) for n in ORDER}
    vs = {n: take(args["v_" + n], n) for n in ORDER}
    gains = {n: wts["g_" + n][None] for n in GAINS}

    c_idx = lax.axis_index("c").astype(jnp.int32).reshape(1)
    me_idx = (2 * lax.axis_index("x") + lax.axis_index("y")).astype(jnp.int32).reshape(1)
    south = lax.axis_index("c") == 0

    shards = dict(zip(BIG, [_rows_split(s, 2) for s in _cast_bf16([wts["w_" + n] for n in BIG], "cast_weights")]))
    absolute = lambda names: [n not in PERMUTED for n in names]
    ffn1 = dict(zip(W_GROUPS["ffn1"], _gather_finish([shards[n] for n in W_GROUPS["ffn1"]], None,
                                                     absolute(W_GROUPS["ffn1"]), "gather_ffn1")))
    tie = ffn1["ffn1_gate"][0, 0, 0].astype(F32) * 0.0
    started = {}
    for cid, group in ((1, "in"), (2, "rest")):
        first = shards[W_GROUPS[group][0]]
        first = first + tie.astype(first.dtype)
        started[group] = _gather_start([first] + [shards[n] for n in W_GROUPS[group][1:]],
                                       "gather_%s_start" % group, cid)
    gains["ffn1_pre"] = gains["ffn1_pre"] + started["in"][3][0, 0] + started["rest"][3][0, 0]

    class Overlapped:
        def __init__(self):
            self.early = None

        def weights(self, group, after=None):
            if group == "ffn1":
                return ffn1
            names = W_GROUPS[group]
            send_sem, recv_sem, operands, _ = started[group]
            operands = _gather_wait(send_sem, recv_sem, operands, after, "gather_%s_wait" % group)
            n = len(names)
            return dict(zip(names, _gather_finish(operands[:n], operands[n:], absolute(names),
                                                  "gather_%s_finish" % group)))

        def grads_ready(self, group, gw):
            self.early = reduce_start(G_GROUPS[group], gw, group, 3)
            return self.early[-1]

    def reduce_start(names, gw, tag, cid):
        perm = [n in PERMUTED for n in names]
        mine = [gw[n].reshape(N_CHIPS, 2, gw[n].shape[1] // 2, gw[n].shape[2]) for n in names]
        got = _pair_swap(mine, perm, "grad_pair_swap_" + tag)
        partial = _pair_add(c_idx, mine, got, perm, "grad_pair_add_" + tag)
        send_sem, recv_sem, operands, token = _scatter_start(partial, perm, "grad_scatter_start_" + tag, cid)
        return names, perm, send_sem, recv_sem, operands, token

    def reduce_finish(state, after, tag):
        names, perm, send_sem, recv_sem, operands, _ = state
        operands = _scatter_wait(send_sem, recv_sem, operands, perm, after, "grad_scatter_wait_" + tag)
        n = len(names)
        halves = _chip_add(me_idx, operands[:n], operands[n:], perm, "grad_chip_add_" + tag)
        out = {}
        for name, own, other in zip(names, halves, _pair_join(halves, "grad_pair_join_" + tag)):
            out["w_" + name] = jnp.concatenate([jnp.where(south, own, other), jnp.where(south, other, own)], axis=0)
        return out

    hooks = Overlapped()
    loss, dx, gw, gg, g_rel = _local_step(x[0], p[0, 0], loss_target[0], gains, wts["rel_bias"], hooks)

    late = reduce_start(G_GROUPS["late"], gw, "late", 4)
    grads = reduce_finish(hooks.early, late[-1], "early")

    pieces = [gg[n].reshape(-1, 128) for n in GAINS] + [jnp.pad(g_rel, ((0, 0), (0, N_REL_PAD - N_REL))).reshape(-1, 128)]
    summed = _all_sum_small(jnp.concatenate(pieces, axis=0), "small_grad_sum")
    at = 0
    for n, piece in zip(GAINS, pieces[:-1]):
        grads["g_" + n] = summed[at:at + piece.shape[0]].reshape(1, -1)[0]
        at += piece.shape[0]
    grads["rel_bias"] = summed[at:].reshape(N_HEADS, N_REL_PAD)[:, :N_REL]

    delta, new_m, new_v = {}, {}, {}

    def adamw_big(group):
        names = ["w_" + n for n in G_GROUPS[group]]
        d, m, v = _adamw([wts[n] for n in names], [grads[n] for n in names], [ms[n] for n in names],
                         [vs[n] for n in names], 8, "adamw_" + group)
        for n, dd, mm, vv in zip(names, d, m, v):
            delta[n], new_m[n], new_v[n] = dd, mm, vv
        return d[0]

    done_early = adamw_big("early")
    grads.update(reduce_finish(late, done_early, "late"))
    adamw_big("late")
    small = ["g_" + n for n in GAINS] + ["rel_bias"]
    as_rows = lambda a: (a.reshape(-1, 128) if a.size % 128 == 0 else jnp.pad(a, ((0, 0), (0, N_REL_PAD - N_REL))).reshape(-1, 128))
    d, m, v = _adamw([as_rows(wts[n]) for n in small], [as_rows(grads[n]) for n in small],
                     [as_rows(ms[n]) for n in small], [as_rows(vs[n]) for n in small], 1, "adamw_small")
    for n, dd, mm, vv in zip(small, d, m, v):
        back = (lambda a: a.reshape(N_HEADS, N_REL_PAD)[:, :N_REL]) if n == "rel_bias" else (lambda a: a.reshape(-1))
        delta[n], new_m[n], new_v[n] = back(dd), back(mm), back(vv)

    loss = lax.psum(loss[0, 0], ("x", "y", "c"))
    outs = [loss, dx[None]]
    for table in (grads, delta, new_m, new_v):
        outs += [(table[n].T if n in TRANSPOSED else table[n])[None] for n in ORDER]
    return tuple(outs)
```

```python
import functools

import jax
import jax.numpy as jnp
from jax import lax
from jax.experimental import pallas as pl
from jax.experimental.pallas import tpu as pltpu

F32 = jnp.float32
BF16 = jnp.bfloat16
EPS = 1e-6
N_CHIPS = 4
HEAD_DIM = 64
N_HEADS = 8
CHUNK = 64
LOOKBACK = 8
BAND = (LOOKBACK + 1) * CHUNK
PAD = LOOKBACK * CHUNK
REL_CLIP = 128
N_REL = 2 * REL_CLIP + 1
N_REL_PAD = 384
SB_BLOCK = 256
PAIR = 2 * HEAD_DIM
ATT_SCALE = HEAD_DIM ** -0.5
NEG_INF = -1e30
ROW_BLOCK = 512
VMEM_LIMIT = 48 * 1024 * 1024
MESH = pl.DeviceIdType.MESH

ADAM_LR = 0.001
ADAM_B1 = 0.9
ADAM_B2 = 0.999
ADAM_EPS = 1e-08
ADAM_WD = 0.01
ADAM_STEP = 10

NT = (((1,), (1,)), ((), ()))
TN = (((0,), (0,)), ((), ()))


def _params(n_grid, vmem=None):
    return pltpu.CompilerParams(dimension_semantics=("arbitrary",) * n_grid, vmem_limit_bytes=vmem)


def _dot(a, b, dims=None):
    if dims is None:
        return jnp.dot(a, b, preferred_element_type=F32)
    return lax.dot_general(a, b, dims, preferred_element_type=F32)


def _sigmoid(x):
    return 1.0 / (1.0 + jnp.exp(-x))


def _rms_fwd(x, g):
    r = lax.rsqrt(jnp.mean(x * x, axis=-1, keepdims=True) + EPS)
    return x * r * g


def _rms_bwd(x, g, dy):
    r = lax.rsqrt(jnp.mean(x * x, axis=-1, keepdims=True) + EPS)
    xh = x * r
    dg = jnp.sum(dy * xh, axis=0, keepdims=True)
    t = dy * g
    dx = r * (t - xh * jnp.mean(t * xh, axis=-1, keepdims=True))
    return dx, dg


def _accumulate(ref, val, first):
    @pl.when(first)
    def _():
        ref[...] = val

    @pl.when(jnp.logical_not(first))
    def _():
        ref[...] += val


def _split2(x):
    hi = x.astype(BF16)
    lo = (x - hi.astype(F32)).astype(BF16)
    return hi, lo


def _ffn_fwd(x, g_pre, g_post, wg, wu, wd, name):
    T, D = x.shape
    S, FS, _ = wg.shape
    tm = min(ROW_BLOCK, T)

    def body(x_ref, gpre_ref, gpost_ref, wg_ref, wu_ref, wd_ref,
             h_ref, xn_ref, g_ref, u_ref, a_ref, f_ref, xn_s, acc_s):
        k = pl.program_id(1)

        @pl.when(k == 0)
        def _():
            xn_s[...] = _rms_fwd(x_ref[...], gpre_ref[...]).astype(BF16)
            xn_ref[...] = xn_s[...]

        xn = xn_s[...]
        g = _dot(xn, wg_ref[0], NT)
        u = _dot(xn, wu_ref[0], NT)
        g_ref[0] = g
        u_ref[0] = u
        a = (g * _sigmoid(g) * u).astype(BF16)
        a_ref[0] = a
        _accumulate(acc_s, _dot(a, wd_ref[0]), k == 0)

        @pl.when(k == S - 1)
        def _():
            f = acc_s[...]
            f_ref[...] = f
            h_ref[...] = x_ref[...] + 0.5 * _rms_fwd(f, gpost_ref[...])

    row = pl.BlockSpec((tm, D), lambda i, k: (i, 0))
    vec = pl.BlockSpec((1, D), lambda i, k: (0, 0))
    act = pl.BlockSpec((1, tm, FS), lambda i, k: (k, i, 0))
    return pl.pallas_call(
        body, name=name, grid=(T // tm, S),
        in_specs=[row, vec, vec] + [pl.BlockSpec((1, FS, D), lambda i, k: (k, 0, 0))] * 3,
        out_specs=[row, row, act, act, act, row],
        out_shape=[jax.ShapeDtypeStruct((T, D), F32), jax.ShapeDtypeStruct((T, D), BF16),
                   jax.ShapeDtypeStruct((S, T, FS), F32), jax.ShapeDtypeStruct((S, T, FS), F32),
                   jax.ShapeDtypeStruct((S, T, FS), BF16), jax.ShapeDtypeStruct((T, D), F32)],
        scratch_shapes=[pltpu.VMEM((tm, D), BF16), pltpu.VMEM((tm, D), F32)],
        compiler_params=_params(2, VMEM_LIMIT),
    )(x, g_pre, g_post, wg, wu, wd)


def _ffn_bwd_act(dh, f, g_post, wd, g_act, u_act, name):
    T, D = dh.shape
    S, FS, _ = wd.shape
    tm = min(ROW_BLOCK, T)

    def body(dh_ref, f_ref, gpost_ref, wd_ref, g_ref, u_ref, dgp_ref, dup_ref, df_ref, dgain_ref, df_s):
        i, k = pl.program_id(0), pl.program_id(1)

        @pl.when(k == 0)
        def _():
            df, dgain = _rms_bwd(f_ref[...], gpost_ref[...], 0.5 * dh_ref[...])
            df_s[...] = df.astype(BF16)
            df_ref[...] = df_s[...]
            _accumulate(dgain_ref, dgain, i == 0)

        da = _dot(df_s[...], wd_ref[0], NT)
        g = g_ref[0]
        s = _sigmoid(g)
        dup_ref[0] = (da * (g * s)).astype(BF16)
        dgp_ref[0] = (da * u_ref[0] * (s * (1.0 + g * (1.0 - s)))).astype(BF16)

    row = pl.BlockSpec((tm, D), lambda i, k: (i, 0))
    vec = pl.BlockSpec((1, D), lambda i, k: (0, 0))
    act = pl.BlockSpec((1, tm, FS), lambda i, k: (k, i, 0))
    return pl.pallas_call(
        body, name=name, grid=(T // tm, S),
        in_specs=[row, row, vec, pl.BlockSpec((1, FS, D), lambda i, k: (k, 0, 0)), act, act],
        out_specs=[act, act, row, vec],
        out_shape=[jax.ShapeDtypeStruct((S, T, FS), BF16), jax.ShapeDtypeStruct((S, T, FS), BF16),
                   jax.ShapeDtypeStruct((T, D), BF16), jax.ShapeDtypeStruct((1, D), F32)],
        scratch_shapes=[pltpu.VMEM((tm, D), BF16)],
        compiler_params=_params(2, VMEM_LIMIT),
    )(dh, f, g_post, wd, g_act, u_act)


def _proj_bwd(dys, ws, x, g_pre, dh, name):
    T, D = x.shape
    n = len(dys)
    flat = dys[0].ndim == 2
    S = ws[0].shape[0]
    N = ws[0].shape[2] if flat else ws[0].shape[1]
    tm = min(ROW_BLOCK, T)

    def body(*refs):
        dy_refs, w_refs = refs[:n], refs[n:2 * n]
        x_ref, gpre_ref, dh_ref, dx_ref, dgain_ref, acc_s = refs[2 * n:]
        i, k = pl.program_id(0), pl.program_id(1)
        part = None
        for dy_ref, w_ref in zip(dy_refs, w_refs):
            term = _dot(dy_ref[...], w_ref[0], NT) if flat else _dot(dy_ref[0], w_ref[0])
            part = term if part is None else part + term
        _accumulate(acc_s, part, k == 0)

        @pl.when(k == S - 1)
        def _():
            dx, dgain = _rms_bwd(x_ref[...], gpre_ref[...], acc_s[...])
            dx_ref[...] = dh_ref[...] + dx
            _accumulate(dgain_ref, dgain, i == 0)

    row = pl.BlockSpec((tm, D), lambda i, k: (i, 0))
    vec = pl.BlockSpec((1, D), lambda i, k: (0, 0))
    return pl.pallas_call(
        body, name=name, grid=(T // tm, S),
        in_specs=[pl.BlockSpec((tm, N), lambda i, k: (i, k)) if flat else pl.BlockSpec((1, tm, N), lambda i, k: (k, i, 0))] * n
        + [pl.BlockSpec((1,) + ws[0].shape[1:], lambda i, k: (k, 0, 0))] * n + [row, vec, row],
        out_specs=[row, vec],
        out_shape=[jax.ShapeDtypeStruct((T, D), F32), jax.ShapeDtypeStruct((1, D), F32)],
        scratch_shapes=[pltpu.VMEM((tm, D), F32)],
        compiler_params=_params(2, VMEM_LIMIT),
    )(*dys, *ws, x, g_pre, dh)


def _mm_tn(a, b, bm, name, groups=None):
    ga, T, M = a.shape
    if groups is None:
        gb, _, N = b.shape
        b_spec = pl.BlockSpec((1, T, N), (lambda g, m: (g, 0, 0)) if gb > 1 else (lambda g, m: (0, 0, 0)))
    else:
        gb, N = groups, b.shape[1] // groups
        b_spec = pl.BlockSpec((T, N), lambda g, m: (0, g))
    G = max(ga, gb)

    def body(a_ref, b_ref, o_ref):
        bv = b_ref[0] if groups is None else b_ref[...]
        o_ref[0] = _dot(a_ref[0].astype(BF16), bv.astype(BF16), TN)

    return pl.pallas_call(
        body, name=name, grid=(G, M // bm),
        in_specs=[pl.BlockSpec((1, T, bm), (lambda g, m: (g, 0, m)) if ga > 1 else (lambda g, m: (0, 0, m))), b_spec],
        out_specs=pl.BlockSpec((1, bm, N), lambda g, m: (g, m, 0)),
        out_shape=jax.ShapeDtypeStruct((G, M, N), F32),
        compiler_params=_params(2, VMEM_LIMIT),
    )(a, b)


def _norm_proj(x, g_pre, w, name):
    T, D = x.shape
    S, _, N = w.shape
    tm = min(ROW_BLOCK, T)

    def body(x_ref, g_ref, w_ref, o_ref, xn_ref, xn_s):
        @pl.when(pl.program_id(1) == 0)
        def _():
            xn_s[...] = _rms_fwd(x_ref[...], g_ref[...]).astype(BF16)
            xn_ref[...] = xn_s[...]

        o_ref[...] = _dot(xn_s[...], w_ref[0]).astype(BF16)

    row = pl.BlockSpec((tm, D), lambda i, k: (i, 0))
    return pl.pallas_call(
        body, name=name, grid=(T // tm, S),
        in_specs=[row, pl.BlockSpec((1, D), lambda i, k: (0, 0)), pl.BlockSpec((1, D, N), lambda i, k: (k, 0, 0))],
        out_specs=[pl.BlockSpec((tm, N), lambda i, k: (i, k)), row],
        out_shape=[jax.ShapeDtypeStruct((T, S * N), BF16), jax.ShapeDtypeStruct((T, D), BF16)],
        scratch_shapes=[pltpu.VMEM((tm, D), BF16)],
        compiler_params=_params(2, VMEM_LIMIT),
    )(x, g_pre, w)


def _mix_out_fwd(h, o_a, o_b, g_sb, g_ch, w_out, g_post, name):
    T, D = h.shape
    W = g_sb.shape[1]
    tm = min(ROW_BLOCK, T)

    def body(h_ref, oa_ref, ob_ref, gsb_ref, gch_ref, w_ref, gpost_ref, h2_ref, mixed_ref, mo_ref):
        mixed_ref[:, :W] = _rms_fwd(oa_ref[...], gsb_ref[...]).astype(BF16)
        mixed_ref[:, W:] = _rms_fwd(ob_ref[...], gch_ref[...]).astype(BF16)
        mo = _dot(mixed_ref[...], w_ref[...])
        mo_ref[...] = mo
        h2_ref[...] = h_ref[...] + _rms_fwd(mo, gpost_ref[...])

    row = pl.BlockSpec((tm, D), lambda i: (i, 0))
    part = pl.BlockSpec((tm, W), lambda i: (i, 0))
    half = pl.BlockSpec((1, W), lambda i: (0, 0))
    return pl.pallas_call(
        body, name=name, grid=(T // tm,),
        in_specs=[row, part, part, half, half, pl.BlockSpec((D, D), lambda i: (0, 0)), pl.BlockSpec((1, D), lambda i: (0, 0))],
        out_specs=[row, row, row],
        out_shape=[jax.ShapeDtypeStruct((T, D), F32), jax.ShapeDtypeStruct((T, D), BF16),
                   jax.ShapeDtypeStruct((T, D), F32)],
        compiler_params=_params(1, VMEM_LIMIT),
    )(h, o_a, o_b, g_sb, g_ch, w_out, g_post)


def _mix_out_bwd(dh, mo, g_post, w_out, o_a, o_b, g_sb, g_ch, name):
    T, D = dh.shape
    W = g_sb.shape[1]
    tm = min(ROW_BLOCK, T)

    def body(dh_ref, mo_ref, gpost_ref, w_ref, oa_ref, ob_ref, gsb_ref, gch_ref,
             dmo_ref, doa_ref, dob_ref, dgpost_ref, dgsb_ref, dgch_ref):
        first = pl.program_id(0) == 0
        dmo, dgpost = _rms_bwd(mo_ref[...], gpost_ref[...], dh_ref[...])
        dmo_ref[...] = dmo.astype(BF16)
        dmix = _dot(dmo_ref[...], w_ref[...], NT)
        doa_ref[...], dgsb = _rms_bwd(oa_ref[...], gsb_ref[...], dmix[:, :W])
        dob_ref[...], dgch = _rms_bwd(ob_ref[...], gch_ref[...], dmix[:, W:])
        _accumulate(dgpost_ref, dgpost, first)
        _accumulate(dgsb_ref, dgsb, first)
        _accumulate(dgch_ref, dgch, first)

    row = pl.BlockSpec((tm, D), lambda i: (i, 0))
    part = pl.BlockSpec((tm, W), lambda i: (i, 0))
    vec = pl.BlockSpec((1, D), lambda i: (0, 0))
    half = pl.BlockSpec((1, W), lambda i: (0, 0))
    return pl.pallas_call(
        body, name=name, grid=(T // tm,),
        in_specs=[row, row, vec, pl.BlockSpec((D, D), lambda i: (0, 0)), part, part, half, half],
        out_specs=[row, part, part, vec, half, half],
        out_shape=[jax.ShapeDtypeStruct((T, D), BF16), jax.ShapeDtypeStruct((T, W), F32),
                   jax.ShapeDtypeStruct((T, W), F32), jax.ShapeDtypeStruct((1, D), F32),
                   jax.ShapeDtypeStruct((1, W), F32), jax.ShapeDtypeStruct((1, W), F32)],
        compiler_params=_params(1, VMEM_LIMIT),
    )(dh, mo, g_post, w_out, o_a, o_b, g_sb, g_ch)


def _ple_loss(h, p, target, w_proj, w_gate, g_post, name):
    T, D = h.shape
    P = p.shape[1]
    S = N_CHIPS
    C = D // S
    tm = min(ROW_BLOCK, T)

    def body(h_ref, p_ref, t_ref, wp_ref, wg_ref, g_ref, loss_ref, dh_ref, dproj_ref, dgate_ref, dgain_ref):
        first = pl.program_id(0) == 0
        h3 = h_ref[...]
        proj = _dot(p_ref[...].astype(BF16), wp_ref[...])
        s = _sigmoid(_dot(h3.astype(BF16), wg_ref[...]))
        e = proj * s
        diff = h3 + _rms_fwd(e, g_ref[...]) - t_ref[...]
        part = 0.5 * jnp.sum(jnp.mean(diff * diff, axis=-1, keepdims=True), axis=0, keepdims=True)
        _accumulate(loss_ref, jnp.broadcast_to(part, loss_ref.shape), first)
        dy = diff * (1.0 / D)
        de, dgain = _rms_bwd(e, g_ref[...], dy)
        _accumulate(dgain_ref, dgain, first)
        dproj = (de * s).astype(BF16)
        for j in range(S):
            dproj_ref[j] = dproj[:, j * C:(j + 1) * C]
        dgate_ref[...] = (de * proj * s * (1.0 - s)).astype(BF16)
        dh_ref[...] = dy + _dot(dgate_ref[...], wg_ref[...], NT)

    row = pl.BlockSpec((tm, D), lambda i: (i, 0))
    vec = pl.BlockSpec((1, D), lambda i: (0, 0))
    return pl.pallas_call(
        body, name=name, grid=(T // tm,),
        in_specs=[row, pl.BlockSpec((tm, P), lambda i: (i, 0)), row,
                  pl.BlockSpec((P, D), lambda i: (0, 0)), pl.BlockSpec((D, D), lambda i: (0, 0)), vec],
        out_specs=[pl.BlockSpec((8, 128), lambda i: (0, 0)), row,
                   pl.BlockSpec((S, tm, C), lambda i: (0, i, 0)), row, vec],
        out_shape=[jax.ShapeDtypeStruct((8, 128), F32), jax.ShapeDtypeStruct((T, D), F32),
                   jax.ShapeDtypeStruct((S, T, C), BF16), jax.ShapeDtypeStruct((T, D), BF16),
                   jax.ShapeDtypeStruct((1, D), F32)],
        compiler_params=_params(1, VMEM_LIMIT),
    )(h, p, target, w_proj, w_gate, g_post)


def _sb_scores(q, kj, mask):
    z = _dot(q, kj, NT) * ATT_SCALE
    sp = jnp.maximum(z, 0.0) + jnp.log(1.0 + jnp.exp(-jnp.abs(z)))
    lf = -sp if mask is None else jnp.where(mask, -sp, 0.0)
    return z, sp, lf


def _strict_causal():
    rows = lax.broadcasted_iota(jnp.int32, (SB_BLOCK, SB_BLOCK), 0)
    cols = lax.broadcasted_iota(jnp.int32, (SB_BLOCK, SB_BLOCK), 1)
    return cols < rows


def _tri(cmp):
    r = lax.broadcasted_iota(jnp.int32, (SB_BLOCK, SB_BLOCK), 0)
    c = lax.broadcasted_iota(jnp.int32, (SB_BLOCK, SB_BLOCK), 1)
    return jnp.where(cmp(r, c), 1.0, 0.0).astype(BF16)


def _cum(x, tri):
    hi, lo = _split2(x)
    return _dot(hi, tri) + _dot(lo, tri)


def _pair_lanes():
    lane = lax.broadcasted_iota(jnp.int32, (1, PAIR), 1)
    return [lane < HEAD_DIM, lane >= HEAD_DIM]


def _only(lanes, x):
    return jnp.where(lanes, x, jnp.zeros_like(x))


def _sb_fwd(qkv, name):
    T = qkv.shape[0]
    B = SB_BLOCK
    pairs = N_HEADS // 2

    def body(q_ref, k_ref, v_ref, o_ref, tot_ref):
        i = pl.program_id(1)
        after = _tri(lambda r, c: r > c)
        lanes = _pair_lanes()
        q = [_only(lanes[h], q_ref[...]) for h in range(2)]

        def tile(h, j, carry, mask):
            run, acc = carry
            at = pl.ds(pl.multiple_of(j * B, B), B)
            z, sp, lf = _sb_scores(q[h], k_ref[at, :], mask)
            a = jnp.exp((z - sp) + _cum(lf, after) + run)
            if mask is not None:
                a = jnp.where(mask, a, 0.0)
            return (run + jnp.sum(lf, axis=1, keepdims=True),
                    acc + _dot(a.astype(BF16), _only(lanes[h], v_ref[at, :])))

        zero = (jnp.zeros((B, 1), F32), jnp.zeros((B, PAIR), F32))
        diag = _strict_causal()
        carries = tuple(tile(h, i, zero, diag) for h in range(2))
        carries = lax.fori_loop(
            0, i, lambda jj, cs: tuple(tile(h, i - 1 - jj, cs[h], None) for h in range(2)), carries)
        o_ref[...] = carries[0][1] + carries[1][1]
        tot_ref[...] = jnp.where(lanes[0], carries[0][0], carries[1][0])

    blk = lambda off: pl.BlockSpec((B, PAIR), lambda g, i: (i, g + off))
    full = lambda off: pl.BlockSpec((T, PAIR), lambda g, i: (0, g + off))
    out = jax.ShapeDtypeStruct((T, pairs * PAIR), F32)
    return pl.pallas_call(
        body, name=name, grid=(pairs, T // B),
        in_specs=[blk(0), full(pairs), full(2 * pairs)],
        out_specs=[blk(0), blk(0)],
        out_shape=[out, out],
        compiler_params=_params(2, VMEM_LIMIT),
    )(qkv, qkv, qkv)


def _sb_bwd(qkv, do, tot, name):
    T = qkv.shape[0]
    B = SB_BLOCK
    pairs = N_HEADS // 2
    n_blocks = T // B

    def body(q_ref, k_ref, v_ref, do_ref, tot_ref, dq_ref, dk_ref, dv_ref, dk_s, dv_s):
        i = pl.program_id(1)

        @pl.when(i == 0)
        def _():
            dk_s[...] = jnp.zeros_like(dk_s)
            dv_s[...] = jnp.zeros_like(dv_s)

        upto = _tri(lambda r, c: r <= c)
        below = _tri(lambda r, c: r < c)
        lanes = _pair_lanes()
        q = [_only(lanes[h], q_ref[...]) for h in range(2)]
        dob = do_ref[...].astype(BF16)
        do = [_only(lanes[h], dob) for h in range(2)]
        tot = [tot_ref[:, 0:1], tot_ref[:, HEAD_DIM:HEAD_DIM + 1]]

        def tile(h, j, carry, mask):
            pre_lf, pre_g, dq = carry
            at = pl.ds(pl.multiple_of(j * B, B), B)
            kj, vj = k_ref[at, :], v_ref[at, :]
            z, sp, lf = _sb_scores(q[h], kj, mask)
            later = tot[h] - pre_lf - _cum(lf, upto)
            a = jnp.exp((z - sp) + later)
            if mask is not None:
                a = jnp.where(mask, a, 0.0)
            g = a * _dot(do[h], vj, NT)
            g_before = pre_g + _cum(g, below)
            fail = jnp.exp(-sp)
            dz = (g * fail - (1.0 - fail) * g_before) * ATT_SCALE
            if mask is not None:
                dz = jnp.where(mask, dz, 0.0)
            dzb = dz.astype(BF16)
            dk_s[at, :] += _dot(dzb, q[h], TN)
            dv_s[at, :] += _dot(a.astype(BF16), do[h], TN)
            return (pre_lf + jnp.sum(lf, axis=1, keepdims=True), pre_g + jnp.sum(g, axis=1, keepdims=True),
                    dq + _dot(dzb, _only(lanes[h], kj)))

        col = jnp.zeros((B, 1), F32)
        zero = (col, col, jnp.zeros((B, PAIR), F32))
        carries = lax.fori_loop(
            0, i, lambda j, cs: tuple(tile(h, j, cs[h], None) for h in range(2)), (zero, zero))
        diag = _strict_causal()
        dq_ref[...] = (tile(0, i, carries[0], diag)[2] + tile(1, i, carries[1], diag)[2]).astype(BF16)

        @pl.when(i == n_blocks - 1)
        def _():
            dk_ref[...] = dk_s[...].astype(BF16)
            dv_ref[...] = dv_s[...].astype(BF16)

    blk = lambda off: pl.BlockSpec((B, PAIR), lambda g, i: (i, g + off))
    full = lambda off: pl.BlockSpec((T, PAIR), lambda g, i: (0, g + off))
    out = jax.ShapeDtypeStruct((T, pairs * PAIR), BF16)
    return pl.pallas_call(
        body, name=name, grid=(pairs, n_blocks),
        in_specs=[blk(0), full(pairs), full(2 * pairs), blk(0), blk(0)],
        out_specs=[blk(0), full(0), full(0)],
        out_shape=[out, out, out],
        scratch_shapes=[pltpu.VMEM((T, PAIR), F32)] * 2,
        compiler_params=_params(2, VMEM_LIMIT),
    )(qkv, qkv, qkv, do, tot)


def _rel_onehot(i, transposed):
    shape = (BAND, N_REL_PAD) if transposed else (N_REL_PAD, BAND)
    j = lax.broadcasted_iota(jnp.int32, shape, 0 if transposed else 1)
    r = lax.broadcasted_iota(jnp.int32, shape, 1 if transposed else 0)
    idx = jnp.clip(i + PAD - j, -REL_CLIP, REL_CLIP) + REL_CLIP
    return jnp.where(idx == r, 1.0, 0.0).astype(BF16)


def _bias_table(rel_bias_pad, name):
    def body(rb_ref, o_ref):
        onehot = _rel_onehot(pl.program_id(0), False)
        rb = rb_ref[...]
        hi, lo = _split2(rb)
        lo2 = (rb - hi.astype(F32) - lo.astype(F32)).astype(BF16)
        o_ref[0] = _dot(hi, onehot) + _dot(lo, onehot) + _dot(lo2, onehot)

    return pl.pallas_call(
        body, name=name, grid=(CHUNK,),
        in_specs=[pl.BlockSpec((N_HEADS, N_REL_PAD), lambda i: (0, 0))],
        out_specs=pl.BlockSpec((1, N_HEADS, BAND), lambda i: (i, 0, 0)),
        out_shape=jax.ShapeDtypeStruct((CHUNK, N_HEADS, BAND), F32),
        compiler_params=_params(1),
    )(rel_bias_pad)


def _bias_grad(dbias_t, name):
    def body(d_ref, o_ref):
        onehot = _rel_onehot(pl.program_id(0), True)
        hi, lo = _split2(d_ref[0])
        _accumulate(o_ref, _dot(hi, onehot) + _dot(lo, onehot), pl.program_id(0) == 0)

    return pl.pallas_call(
        body, name=name, grid=(CHUNK,),
        in_specs=[pl.BlockSpec((1, N_HEADS, BAND), lambda i: (i, 0, 0))],
        out_specs=pl.BlockSpec((N_HEADS, N_REL_PAD), lambda i: (0, 0)),
        out_shape=jax.ShapeDtypeStruct((N_HEADS, N_REL_PAD), F32),
        compiler_params=_params(1),
    )(dbias_t)


def _ch_probs(q, kw, bias, valid):
    z = jnp.where(valid, _dot(q, kw, NT) * ATT_SCALE + bias, NEG_INF)
    e = jnp.exp(z - jnp.max(z, axis=-1, keepdims=True))
    return e / jnp.sum(e, axis=-1, keepdims=True)


def _ch_valid(n):
    slot = lax.broadcasted_iota(jnp.int32, (CHUNK, BAND), 1) // CHUNK
    return n + slot - LOOKBACK >= 0


def _ch_fwd(qkv, bias, name):
    T = qkv.shape[0]
    W = N_HEADS * HEAD_DIM

    def body(q_ref, k_ref, v_ref, b_ref, o_ref, kp, vp):
        n = pl.program_id(0)

        @pl.when(n == 0)
        def _():
            _ch_load_padded(k_ref, v_ref, kp, vp)

        win = pl.ds(pl.multiple_of(n * CHUNK, CHUNK), BAND)
        valid = _ch_valid(n)
        lanes = _pair_lanes()
        for pair in range(N_HEADS // 2):
            cols = slice(pair * PAIR, (pair + 1) * PAIR)
            q, kw, vw = q_ref[:, cols], kp[win, cols], vp[win, cols]
            o = None
            for h in range(2):
                p = _ch_probs(_only(lanes[h], q), kw, b_ref[2 * pair + h], valid)
                part = _dot(p.astype(BF16), _only(lanes[h], vw))
                o = part if o is None else o + part
            o_ref[:, cols] = o

    full = lambda col: pl.BlockSpec((T, W), lambda n: (0, col))
    return pl.pallas_call(
        body, name=name, grid=(T // CHUNK,),
        in_specs=[pl.BlockSpec((CHUNK, W), lambda n: (n, 3)), full(4), full(5),
                  pl.BlockSpec((N_HEADS, CHUNK, BAND), lambda n: (0, 0, 0))],
        out_specs=pl.BlockSpec((CHUNK, W), lambda n: (n, 0)),
        out_shape=jax.ShapeDtypeStruct((T, W), F32),
        scratch_shapes=[pltpu.VMEM((PAD + T, W), BF16)] * 2,
        compiler_params=_params(1, VMEM_LIMIT),
    )(qkv, qkv, qkv, bias)


def _ch_load_padded(k_ref, v_ref, kp, vp):
    for src, dst in ((k_ref, kp), (v_ref, vp)):
        dst[:PAD, :] = jnp.zeros((PAD, dst.shape[1]), dst.dtype)
        dst[PAD:, :] = src[...]


def _ch_bwd(qkv, bias, do, name):
    T = qkv.shape[0]
    W = N_HEADS * HEAD_DIM
    n_chunks = T // CHUNK

    def body(q_ref, k_ref, v_ref, b_ref, do_ref, dq_ref, dk_ref, dv_ref, db_ref, kp, vp, dk_s, dv_s):
        n = pl.program_id(0)

        @pl.when(n == 0)
        def _():
            _ch_load_padded(k_ref, v_ref, kp, vp)
            dk_s[...] = jnp.zeros_like(dk_s)
            dv_s[...] = jnp.zeros_like(dv_s)
            db_ref[...] = jnp.zeros_like(db_ref)

        win = pl.ds(pl.multiple_of(n * CHUNK, CHUNK), BAND)
        valid = _ch_valid(n)
        lanes = _pair_lanes()
        for pair in range(N_HEADS // 2):
            cols = slice(pair * PAIR, (pair + 1) * PAIR)
            q, kw, vw = q_ref[:, cols], kp[win, cols], vp[win, cols]
            dob = do_ref[:, cols].astype(BF16)
            dq = dk = dv = None
            for h in range(2):
                qh, doh = _only(lanes[h], q), _only(lanes[h], dob)
                p = _ch_probs(qh, kw, b_ref[2 * pair + h], valid)
                dp = _dot(doh, vw, NT)
                dz = p * (dp - jnp.sum(dp * p, axis=-1, keepdims=True))
                db_ref[2 * pair + h] += dz
                dzb = (dz * ATT_SCALE).astype(BF16)
                parts = (_dot(dzb, _only(lanes[h], kw)), _dot(dzb, qh, TN), _dot(p.astype(BF16), doh, TN))
                dq, dk, dv = parts if dq is None else (dq + parts[0], dk + parts[1], dv + parts[2])
            dq_ref[:, cols] = dq.astype(BF16)
            dk_s[win, cols] += dk
            dv_s[win, cols] += dv

        @pl.when(n == n_chunks - 1)
        def _():
            dk_ref[...] = dk_s[PAD:, :].astype(BF16)
            dv_ref[...] = dv_s[PAD:, :].astype(BF16)

    full = lambda col: pl.BlockSpec((T, W), lambda n: (0, col))
    blk = lambda col: pl.BlockSpec((CHUNK, W), lambda n: (n, col))
    tab = pl.BlockSpec((N_HEADS, CHUNK, BAND), lambda n: (0, 0, 0))
    out = jax.ShapeDtypeStruct((T, W), BF16)
    return pl.pallas_call(
        body, name=name, grid=(n_chunks,),
        in_specs=[blk(3), full(4), full(5), tab, blk(0)],
        out_specs=[blk(0), full(0), full(0), tab],
        out_shape=[out, out, out, jax.ShapeDtypeStruct((N_HEADS, CHUNK, BAND), F32)],
        scratch_shapes=[pltpu.VMEM((PAD + T, W), BF16)] * 2 + [pltpu.VMEM((PAD + T, W), F32)] * 2,
        compiler_params=_params(1, VMEM_LIMIT),
    )(qkv, qkv, qkv, bias, do)


def _rows_split(a, parts):
    return a.reshape(a.shape[:-2] + (parts, a.shape[-2] // parts, a.shape[-1]))


def _cast_into_slot0(c, ws, name):
    parts = 2
    ws = [_rows_split(_rows_split(w, 2), parts) for w in ws]
    n = len(ws)

    def body(c_ref, *refs):
        for src, dst in zip(refs[:n], refs[n:]):
            dst[0, 0, 0] = src[0, 0].astype(BF16)

    outs = pl.pallas_call(
        body, name=name,
        grid_spec=pltpu.PrefetchScalarGridSpec(
            num_scalar_prefetch=1, grid=(2, parts),
            in_specs=[pl.BlockSpec((1, 1) + w.shape[2:], lambda d, r, c_ref: (d ^ c_ref[0], r, 0, 0)) for w in ws],
            out_specs=[pl.BlockSpec((1, 1, 1) + w.shape[2:], lambda d, r, c_ref: (0, d, r, 0, 0)) for w in ws]),
        out_shape=[jax.ShapeDtypeStruct((N_CHIPS,) + w.shape, BF16) for w in ws],
        compiler_params=_params(2, VMEM_LIMIT),
    )(c, *ws)
    return [o.reshape(N_CHIPS, 2, o.shape[2] * o.shape[3], o.shape[4]) for o in outs]


def _chip_order(me, c, lands, name):
    parts = 2
    xs = [_rows_split(x, parts) for x in lands]

    def body(me_ref, c_ref, *refs):
        n = len(refs) // 2
        for src, dst in zip(refs[:n], refs[n:]):
            dst[...] = src[...]

    outs = pl.pallas_call(
        body, name=name,
        grid_spec=pltpu.PrefetchScalarGridSpec(
            num_scalar_prefetch=2, grid=(N_CHIPS, 2, parts),
            in_specs=[pl.BlockSpec((1, 1, 1) + x.shape[3:],
                                   lambda j, h, r, me_ref, c_ref: (j ^ me_ref[0], h ^ c_ref[0], r, 0, 0)) for x in xs],
            out_specs=[pl.BlockSpec((1, 1, 1) + x.shape[3:], lambda j, h, r, me_ref, c_ref: (j, h, r, 0, 0))
                       for x in xs]),
        out_shape=[jax.ShapeDtypeStruct(x.shape, x.dtype) for x in xs],
        compiler_params=_params(3, VMEM_LIMIT),
    )(me, c, *xs)
    return [o.reshape(o.shape[0], 2 * parts * o.shape[3], o.shape[4]) for o in outs]


def _pair_add(c, mine, got, permuted, name):
    parts = 2
    mine = [_rows_split(m, parts) for m in mine]
    got = [_rows_split(g, parts) for g in got]
    n = len(mine)

    def body(c_ref, *refs):
        for a, b, o in zip(refs[:n], refs[n:2 * n], refs[2 * n:]):
            o[0, 0] = (a[0, 0, 0] + b[0, 0]).astype(BF16)

    def mine_spec(m, perm):
        if perm:
            return pl.BlockSpec((1, 1, 1) + m.shape[3:], lambda j, r, c_ref: (j, 0, r, 0, 0))
        return pl.BlockSpec((1, 1, 1) + m.shape[3:], lambda j, r, c_ref: (j, c_ref[0], r, 0, 0))

    def got_spec(g):
        return pl.BlockSpec((1, 1) + g.shape[2:], lambda j, r, c_ref: (j, r, 0, 0))

    outs = pl.pallas_call(
        body, name=name,
        grid_spec=pltpu.PrefetchScalarGridSpec(
            num_scalar_prefetch=1, grid=(N_CHIPS, parts),
            in_specs=[mine_spec(m, perm) for m, perm in zip(mine, permuted)] + [got_spec(g) for g in got],
            out_specs=[got_spec(g) for g in got]),
        out_shape=[jax.ShapeDtypeStruct(g.shape, BF16) for g in got],
        compiler_params=_params(2, VMEM_LIMIT),
    )(c, *mine, *got)
    return [o.reshape(o.shape[0], o.shape[1] * o.shape[2], o.shape[3]) for o in outs]


def _chip_add(me, partials, landed, permuted, name):
    parts = 2
    ps = [_rows_split(x, parts) for x in partials]
    ls = [_rows_split(x, parts) for x in landed]
    n = len(ps)

    def body(me_ref, *refs):
        for own, got, o in zip(refs[:n], refs[n:2 * n], refs[2 * n:]):
            acc = own[0, 0].astype(F32)
            for r in range(N_CHIPS - 1):
                acc = acc + got[r, 0].astype(F32)
            o[0] = acc

    def own_spec(x, perm):
        if perm:
            return pl.BlockSpec((1, 1) + x.shape[2:], lambda r, me_ref: (0, r, 0, 0))
        return pl.BlockSpec((1, 1) + x.shape[2:], lambda r, me_ref: (me_ref[0], r, 0, 0))

    outs = pl.pallas_call(
        body, name=name,
        grid_spec=pltpu.PrefetchScalarGridSpec(
            num_scalar_prefetch=1, grid=(parts,),
            in_specs=[own_spec(x, perm) for x, perm in zip(ps, permuted)]
            + [pl.BlockSpec((N_CHIPS - 1, 1) + x.shape[2:], lambda r, me_ref: (0, r, 0, 0)) for x in ls],
            out_specs=[pl.BlockSpec((1,) + x.shape[2:], lambda r, me_ref: (r, 0, 0)) for x in ps]),
        out_shape=[jax.ShapeDtypeStruct(x.shape[1:], F32) for x in ps],
        compiler_params=_params(1, VMEM_LIMIT),
    )(me, *ps, *ls)
    return [o.reshape(o.shape[0] * o.shape[1], o.shape[2]) for o in outs]


def _adamw_math(w, g, m, v):
    m = ADAM_B1 * m + (1.0 - ADAM_B1) * g
    v = ADAM_B2 * v + (1.0 - ADAM_B2) * (g * g)
    m_hat = m / (1.0 - ADAM_B1 ** ADAM_STEP)
    v_hat = v / (1.0 - ADAM_B2 ** ADAM_STEP)
    delta = -ADAM_LR * (m_hat / (jnp.sqrt(v_hat) + ADAM_EPS) + ADAM_WD * w)
    return delta, m, v


def _adamw(ws, gs, ms, vs, parts, name):
    n = len(ws)
    flat = [_rows_split(a, parts) for a in (*ws, *gs, *ms, *vs)]

    def body(*refs):
        ins, outs = refs[:4 * n], refs[4 * n:]
        for k in range(n):
            d, m, v = _adamw_math(ins[k][...], ins[n + k][...], ins[2 * n + k][...], ins[3 * n + k][...])
            outs[k][...] = d
            outs[n + k][...] = m
            outs[2 * n + k][...] = v

    spec = lambda a: pl.BlockSpec((1,) + a.shape[1:], lambda i: (i, 0, 0))
    outs = pl.pallas_call(
        body, name=name, grid=(parts,),
        in_specs=[spec(a) for a in flat], out_specs=[spec(a) for a in flat[:n]] * 3,
        out_shape=[jax.ShapeDtypeStruct(a.shape, F32) for a in flat[:n]] * 3,
        compiler_params=_params(1, VMEM_LIMIT),
    )(*flat)
    outs = [o.reshape(o.shape[0] * o.shape[1], o.shape[2]) for o in outs]
    return outs[:n], outs[n:2 * n], outs[2 * n:]


def _place():
    x, y, c = lax.axis_index("x"), lax.axis_index("y"), lax.axis_index("c")
    peers = [(x ^ (r >> 1), y ^ (r & 1), c) for r in (1, 2, 3)]
    return x, y, c, peers


def _handshake(peers):
    barrier = pltpu.get_barrier_semaphore()
    for peer in peers:
        pl.semaphore_signal(barrier, inc=1, device_id=peer, device_id_type=MESH)
    pl.semaphore_wait(barrier, len(peers))


ANY = pl.BlockSpec(memory_space=pl.ANY)
HBM = pl.BlockSpec(memory_space=pltpu.HBM)
SEM = pl.BlockSpec(memory_space=pltpu.SEMAPHORE)
SPLIT_COPY = pltpu.SideEffectType.DATAFLOW_SIDE_EFFECTING


def _in_hbm(a):
    return pltpu.with_memory_space_constraint(a, pltpu.HBM)


def _split_start(body, name, collective_id, operands, n_sems, after=None):
    n = len(operands)
    extra = [] if after is None else [after]

    def wrapped(*refs):
        at = n + len(extra)
        body(refs[:n], refs[at], refs[at + 1])
        token = refs[-1]
        token[...] = jnp.zeros_like(token)

    outs = pl.pallas_call(
        wrapped, name=name,
        in_specs=[HBM] * n + [ANY] * len(extra),
        out_shape=(pltpu.SemaphoreType.DMA((n_sems,)), pltpu.SemaphoreType.DMA((n_sems,)),
                   *[pltpu.HBM(a.shape, a.dtype) for a in operands], jax.ShapeDtypeStruct((8, 128), F32)),
        out_specs=(SEM, SEM, *[HBM] * n, pl.BlockSpec(memory_space=pltpu.VMEM)),
        input_output_aliases={i: 2 + i for i in range(n)},
        compiler_params=pltpu.CompilerParams(has_side_effects=SPLIT_COPY, collective_id=collective_id),
    )(*[_in_hbm(a) for a in operands], *extra)
    return outs[0], outs[1], list(outs[2:2 + n]), outs[-1]


def _split_wait(body, name, send_sem, recv_sem, operands, after):
    n = len(operands)

    def wrapped(*refs):
        body(refs[:n], refs[n], refs[n + 1])

    outs = pl.pallas_call(
        wrapped, name=name,
        in_specs=[HBM] * n + [SEM, SEM, ANY],
        out_shape=tuple(pltpu.HBM(a.shape, a.dtype) for a in operands),
        out_specs=tuple([HBM] * n),
        input_output_aliases={i: i for i in range(n)},
        compiler_params=pltpu.CompilerParams(has_side_effects=SPLIT_COPY),
    )(*operands, send_sem, recv_sem, after)
    return list(outs)


def _gather_copies(lands, send_sem, recv_sem):
    peers = _place()[3]
    return [pltpu.make_async_remote_copy(
        src_ref=land.at[0, 0], dst_ref=land.at[r + 1, 0],
        send_sem=send_sem.at[a * 3 + r], recv_sem=recv_sem.at[a * 3 + r],
        device_id=peers[r], device_id_type=MESH) for a, land in enumerate(lands) for r in range(3)]


def _gather_start(lands, name, collective_id, after):
    def body(refs, send_sem, recv_sem):
        _handshake(_place()[3])
        for cp in _gather_copies(refs, send_sem, recv_sem):
            cp.start()

    return _split_start(body, name, collective_id, list(lands), 3 * len(lands), after)


def _gather_wait(send_sem, recv_sem, operands, after, name):
    def body(refs, send_sem, recv_sem):
        for cp in _gather_copies(refs, send_sem, recv_sem):
            cp.wait_send()
            cp.wait_recv()

    return _split_wait(body, name, send_sem, recv_sem, operands, after)


def _gather_finish(lands, with_ici, name):
    n = len(lands)

    def body(*refs):
        land = refs[n:2 * n]
        send_ici, recv_ici, send_d2d, recv_d2d = refs[2 * n:]
        x, y, c, _ = _place()
        ici = _gather_copies(land, send_ici, recv_ici) if with_ici else []
        for cp in ici:
            cp.start()
        passed = [pltpu.make_async_remote_copy(
            src_ref=land[a].at[r + 1, 0], dst_ref=land[a].at[r + 1, 1],
            send_sem=send_d2d.at[a * 3 + r], recv_sem=recv_d2d.at[a * 3 + r],
            device_id=(x, y, 1 - c), device_id_type=MESH) for a in range(n) for r in range(3)]
        for k, cp in enumerate(passed):
            if with_ici:
                ici[k].wait_recv()
            cp.start()
        for cp in passed:
            cp.wait_recv()
        for cp in ici:
            cp.wait_send()
        for cp in passed:
            cp.wait_send()

    outs = pl.pallas_call(
        body, name=name,
        in_specs=[ANY] * n, out_specs=[ANY] * n,
        out_shape=[jax.ShapeDtypeStruct(l.shape, l.dtype) for l in lands],
        input_output_aliases={a: a for a in range(n)},
        scratch_shapes=[pltpu.SemaphoreType.DMA((3 * n,))] * 4,
    )(*lands)
    return list(outs)


def _slabs(land):
    return land.reshape(N_CHIPS, 2 * land.shape[2], land.shape[3])


def _pair_swap(grads, permuted, name):
    n = len(grads)

    def body(*refs):
        src, dst = refs[:n], refs[n:2 * n]
        send_sem, recv_sem = refs[2 * n:]
        x, y, c, _ = _place()
        copies = [pltpu.make_async_remote_copy(
            src_ref=src[a].at[:, 1] if permuted[a] else src[a].at[:, 1 - c], dst_ref=dst[a],
            send_sem=send_sem.at[a], recv_sem=recv_sem.at[a],
            device_id=(x, y, 1 - c), device_id_type=MESH) for a in range(n)]
        for cp in copies:
            cp.start()
        for cp in copies:
            cp.wait()

    return pl.pallas_call(
        body, name=name,
        in_specs=[ANY] * n, out_specs=[ANY] * n,
        out_shape=[jax.ShapeDtypeStruct((N_CHIPS,) + g.shape[2:], F32) for g in grads],
        scratch_shapes=[pltpu.SemaphoreType.DMA((n,))] * 2,
    )(*grads)


def _scatter_copies(refs, permuted, send_sem, recv_sem):
    n = len(refs) // 2
    x, y, _, peers = _place()
    me = 2 * x + y
    return [pltpu.make_async_remote_copy(
        src_ref=refs[a].at[r + 1] if permuted[a] else refs[a].at[me ^ (r + 1)], dst_ref=refs[n + a].at[r],
        send_sem=send_sem.at[a * 3 + r], recv_sem=recv_sem.at[a * 3 + r],
        device_id=peers[r], device_id_type=MESH) for a in range(n) for r in range(3)]


def _scatter_start(partials, permuted, name, collective_id):
    def body(refs, send_sem, recv_sem):
        _handshake(_place()[3])
        for cp in _scatter_copies(refs, permuted, send_sem, recv_sem):
            cp.start()

    lands = [lax.empty((N_CHIPS - 1,) + p.shape[1:], p.dtype) for p in partials]
    return _split_start(body, name, collective_id, list(partials) + lands, 3 * len(partials))


def _scatter_wait(send_sem, recv_sem, operands, permuted, after, name):
    def body(refs, send_sem, recv_sem):
        for cp in _scatter_copies(refs, permuted, send_sem, recv_sem):
            cp.wait_send()
            cp.wait_recv()

    return _split_wait(body, name, send_sem, recv_sem, operands, after)


def _pair_join(halves, name):
    n = len(halves)

    def body(*refs):
        src, dst = refs[:n], refs[n:2 * n]
        send_sem, recv_sem = refs[2 * n:]
        x, y, c, _ = _place()
        copies = [pltpu.make_async_remote_copy(
            src_ref=src[a], dst_ref=dst[a], send_sem=send_sem.at[a], recv_sem=recv_sem.at[a],
            device_id=(x, y, 1 - c), device_id_type=MESH) for a in range(n)]
        for cp in copies:
            cp.start()
        for cp in copies:
            cp.wait()

    return pl.pallas_call(
        body, name=name,
        in_specs=[ANY] * n, out_specs=[ANY] * n,
        out_shape=[jax.ShapeDtypeStruct(h.shape, F32) for h in halves],
        scratch_shapes=[pltpu.SemaphoreType.DMA((n,))] * 2,
    )(*halves)


def _all_sum_small(v, name):
    R, C = v.shape
    n_dev = 8

    def body(v_ref, o_ref, buf, send_sem, recv_sem):
        x, y, c, _ = _place()
        me = 4 * x + 2 * y + c
        buf[me] = v_ref[...]
        copies = []
        for k in range(1, n_dev):
            peer = (x ^ (k >> 2), y ^ ((k >> 1) & 1), c ^ (k & 1))
            copies.append(pltpu.make_async_remote_copy(
                src_ref=v_ref, dst_ref=buf.at[me], send_sem=send_sem.at[k - 1], recv_sem=recv_sem.at[k - 1],
                device_id=peer, device_id_type=MESH))
        for cp in copies:
            cp.start()
        for cp in copies:
            cp.wait()
        acc = buf[0]
        for m in range(1, n_dev):
            acc = acc + buf[m]
        o_ref[...] = acc

    return pl.pallas_call(
        body, name=name,
        in_specs=[pl.BlockSpec(memory_space=pltpu.VMEM)], out_specs=pl.BlockSpec(memory_space=pltpu.VMEM),
        out_shape=jax.ShapeDtypeStruct((R, C), F32),
        scratch_shapes=[pltpu.VMEM((n_dev, R, C), F32), pltpu.SemaphoreType.DMA((n_dev - 1,)),
                        pltpu.SemaphoreType.DMA((n_dev - 1,))],
    )(v)


class _WholeWeights:
    def __init__(self, w):
        self.w = w

    def weights(self, group, after=None):
        return self.w

    def grads_ready(self, group, gw):
        return None


def _local_step(x, p, target, gains, rel_bias, hooks):
    T, D = x.shape
    S = N_CHIPS

    w = dict(hooks.weights("ffn1"))
    h1, xn1, g1, u1, a1, f1 = _ffn_fwd(x, gains["ffn1_pre"], gains["ffn1_post"], w["ffn1_gate"], w["ffn1_up"],
                                       w["ffn1_down"], "ffn1_fwd")
    w.update(hooks.weights("in", h1))
    qkv, un = _norm_proj(h1, gains["mix_pre"], w["in"], "qkv_proj")
    rb_pad = jnp.pad(rel_bias, ((0, 0), (0, N_REL_PAD - N_REL)))
    bias = _bias_table(rb_pad, "bias_table").transpose(1, 0, 2)
    o_a, tot = _sb_fwd(qkv, "sb_fwd")
    o_b = _ch_fwd(qkv, bias, "ch_fwd")
    w.update(hooks.weights("rest", o_b))
    w_out = w["out"].reshape(D, D)
    h2, mixed, mo = _mix_out_fwd(h1, o_a, o_b, gains["out_sb"], gains["out_ch"], w_out, gains["mix_post"],
                                 "mix_out_fwd")
    h3, xn2, g2, u2, a2, f2 = _ffn_fwd(h2, gains["ffn2_pre"], gains["ffn2_post"], w["ffn2_gate"], w["ffn2_up"],
                                       w["ffn2_down"], "ffn2_fwd")
    w_ple_proj = w["ple_proj"].transpose(1, 0, 2).reshape(p.shape[1], D)
    w_ple_gate = w["ple_gate"].reshape(D, D)

    loss, dh3, dproj, dgate, dg_ple = _ple_loss(h3, p, target, w_ple_proj, w_ple_gate, gains["ple_post"], "ple_loss")
    gw, gg = {}, {"ple_post": dg_ple}
    gw["ple_proj"] = _mm_tn(p[None], dproj, p.shape[1], "dw_ple_proj")
    gw["ple_gate"] = _mm_tn(h3[None], dgate[None], 512, "dw_ple_gate").reshape(S, D // S, D)

    def ffn_bwd(tag, dh, x_in, xn, g_act, u_act, a_act, f):
        dgp, dup, df, gg[tag + "_post"] = _ffn_bwd_act(dh, f, gains[tag + "_post"], w[tag + "_down"], g_act, u_act,
                                                       tag + "_bwd_act")
        gw[tag + "_gate"] = _mm_tn(dgp, xn[None], dgp.shape[2], "dw_" + tag + "_gate")
        gw[tag + "_up"] = _mm_tn(dup, xn[None], dup.shape[2], "dw_" + tag + "_up")
        gw[tag + "_down"] = _mm_tn(a_act, df[None], a_act.shape[2], "dw_" + tag + "_down")
        dx, gg[tag + "_pre"] = _proj_bwd([dgp, dup], [w[tag + "_gate"], w[tag + "_up"]], x_in, gains[tag + "_pre"], dh,
                                         tag + "_bwd_in")
        return dx

    dh2 = ffn_bwd("ffn2", dh3, h2, xn2, g2, u2, a2, f2)
    dmo, do_a, do_b, gg["mix_post"], gg["out_sb"], gg["out_ch"] = _mix_out_bwd(
        dh2, mo, gains["mix_post"], w_out, o_a, o_b, gains["out_sb"], gains["out_ch"], "mix_out_bwd")
    gw["out"] = _mm_tn(mixed[None], dmo[None], 512, "dw_out").reshape(S, D // S, D)
    token = hooks.grads_ready("early", gw)
    if token is not None:
        tot = tot + token[0, 0]
    dq_a, dk_a, dv_a = _sb_bwd(qkv, do_a, tot, "sb_bwd")
    dq_b, dk_b, dv_b, dbias = _ch_bwd(qkv, bias, do_b, "ch_bwd")
    g_rel = _bias_grad(dbias.transpose(1, 0, 2), "bias_grad")[:, :N_REL]
    dqkv = jnp.concatenate([dq_a, dk_a, dv_a, dq_b, dk_b, dv_b], axis=1)
    gw["in"] = _mm_tn(un[None], dqkv, 512, "dw_in", groups=S)
    dh1, gg["mix_pre"] = _proj_bwd([dqkv], [w["in"]], h1, gains["mix_pre"], dh2, "qkv_bwd_in")
    dx = ffn_bwd("ffn1", dh1, x, xn1, g1, u1, a1, f1)
    return loss, dx, gw, gg, g_rel


BIG = ["ffn1_gate", "ffn1_up", "ffn1_down", "in", "out", "ffn2_gate", "ffn2_up", "ffn2_down", "ple_proj", "ple_gate"]
GAINS = ["ffn1_pre", "ffn1_post", "mix_pre", "mix_post", "out_sb", "out_ch", "ffn2_pre", "ffn2_post", "ple_post"]
TRANSPOSED = ("w_ffn1_gate", "w_ffn1_up", "w_ffn2_gate", "w_ffn2_up")
PERMUTED = ("ffn1_gate", "ffn1_up", "ffn1_down", "ffn2_gate", "ffn2_up", "ffn2_down")
W_GROUPS = {"ffn1": ["ffn1_gate", "ffn1_up", "ffn1_down"], "in": ["in"],
            "rest": ["out", "ffn2_gate", "ffn2_up", "ffn2_down", "ple_proj", "ple_gate"]}
G_GROUPS = {"early": ["ple_proj", "ple_gate", "ffn2_gate", "ffn2_up", "ffn2_down", "out"],
            "late": ["in", "ffn1_gate", "ffn1_up", "ffn1_down"]}
ORDER = ["g_ffn1_pre", "g_ffn1_post", "w_ffn1_gate", "w_ffn1_up", "w_ffn1_down", "g_mix_pre", "g_mix_post", "w_in",
         "g_out_sb", "g_out_ch", "rel_bias", "w_out", "g_ffn2_pre", "g_ffn2_post", "w_ffn2_gate", "w_ffn2_up",
         "w_ffn2_down", "w_ple_proj", "w_ple_gate", "g_ple_post"]


def kernel(x, p, g_ffn1_pre, g_ffn1_post, w_ffn1_gate, w_ffn1_up, w_ffn1_down, g_mix_pre, g_mix_post, w_in, g_out_sb, g_out_ch, rel_bias, w_out, g_ffn2_pre, g_ffn2_post, w_ffn2_gate, w_ffn2_up, w_ffn2_down, w_ple_proj, w_ple_gate, g_ple_post, loss_target, m_g_ffn1_pre, m_g_ffn1_post, m_w_ffn1_gate, m_w_ffn1_up, m_w_ffn1_down, m_g_mix_pre, m_g_mix_post, m_w_in, m_g_out_sb, m_g_out_ch, m_rel_bias, m_w_out, m_g_ffn2_pre, m_g_ffn2_post, m_w_ffn2_gate, m_w_ffn2_up, m_w_ffn2_down, m_w_ple_proj, m_w_ple_gate, m_g_ple_post, v_g_ffn1_pre, v_g_ffn1_post, v_w_ffn1_gate, v_w_ffn1_up, v_w_ffn1_down, v_g_mix_pre, v_g_mix_post, v_w_in, v_g_out_sb, v_g_out_ch, v_rel_bias, v_w_out, v_g_ffn2_pre, v_g_ffn2_post, v_w_ffn2_gate, v_w_ffn2_up, v_w_ffn2_down, v_w_ple_proj, v_w_ple_gate, v_g_ple_post):
    args = dict(locals())
    take = lambda a, n: a[0].T if n in TRANSPOSED else a[0]
    wts = {n: take(args[n], n) for n in ORDER}
    ms = {n: take(args["m_" + n], n) for n in ORDER}
    vs = {n: take(args["v_" + n], n) for n in ORDER}
    gains = {n: wts["g_" + n][None] for n in GAINS}

    c_idx = lax.axis_index("c").astype(jnp.int32).reshape(1)
    me_idx = (2 * lax.axis_index("x") + lax.axis_index("y")).astype(jnp.int32).reshape(1)
    south = lax.axis_index("c") == 0

    lands = dict(zip(BIG, _cast_into_slot0(c_idx, [wts["w_" + n] for n in BIG], "cast_weights")))

    def in_order(names, zones):
        plain = [n for n in names if n not in PERMUTED]
        fixed = dict(zip(plain, _chip_order(me_idx, c_idx, [zones[n] for n in plain], "chip_order_" + plain[0]))
                     ) if plain else {}
        return {n: fixed[n] if n in fixed else _slabs(zones[n]) for n in names}

    names = W_GROUPS["ffn1"]
    ffn1 = dict(zip(names, _gather_finish([lands[n] for n in names], True, "gather_ffn1")))
    started = {}
    for cid, group in ((1, "in"), (2, "rest")):
        started[group] = _gather_start([lands[n] for n in W_GROUPS[group]], "gather_%s_start" % group, cid,
                                       after=ffn1["ffn1_gate"])
    gains["ffn1_pre"] = gains["ffn1_pre"] + started["in"][3][0, 0] + started["rest"][3][0, 0]

    class Overlapped:
        def __init__(self):
            self.early = None

        def weights(self, group, after=None):
            names = W_GROUPS[group]
            if group == "ffn1":
                return in_order(names, ffn1)
            send_sem, recv_sem, zones, _ = started[group]
            zones = _gather_wait(send_sem, recv_sem, zones, after, "gather_%s_wait" % group)
            return in_order(names, dict(zip(names, _gather_finish(zones, False, "gather_%s_finish" % group))))

        def grads_ready(self, group, gw):
            self.early = reduce_start(G_GROUPS[group], gw, group, 3)
            return self.early[-1]

    def reduce_start(names, gw, tag, cid):
        perm = [n in PERMUTED for n in names]
        mine = [gw[n].reshape(N_CHIPS, 2, gw[n].shape[1] // 2, gw[n].shape[2]) for n in names]
        got = _pair_swap(mine, perm, "grad_pair_swap_" + tag)
        partial = _pair_add(c_idx, mine, got, perm, "grad_pair_add_" + tag)
        send_sem, recv_sem, operands, token = _scatter_start(partial, perm, "grad_scatter_start_" + tag, cid)
        return names, perm, send_sem, recv_sem, operands, token

    def reduce_finish(state, after, tag):
        names, perm, send_sem, recv_sem, operands, _ = state
        operands = _scatter_wait(send_sem, recv_sem, operands, perm, after, "grad_scatter_wait_" + tag)
        n = len(names)
        halves = _chip_add(me_idx, operands[:n], operands[n:], perm, "grad_chip_add_" + tag)
        out = {}
        for name, own, other in zip(names, halves, _pair_join(halves, "grad_pair_join_" + tag)):
            out["w_" + name] = jnp.concatenate([jnp.where(south, own, other), jnp.where(south, other, own)], axis=0)
        return out

    hooks = Overlapped()
    loss, dx, gw, gg, g_rel = _local_step(x[0], p[0, 0], loss_target[0], gains, wts["rel_bias"], hooks)

    late = reduce_start(G_GROUPS["late"], gw, "late", 4)
    grads = reduce_finish(hooks.early, late[-1], "early")

    pieces = [gg[n].reshape(-1, 128) for n in GAINS] + [jnp.pad(g_rel, ((0, 0), (0, N_REL_PAD - N_REL))).reshape(-1, 128)]
    summed = _all_sum_small(jnp.concatenate(pieces, axis=0), "small_grad_sum")
    at = 0
    for n, piece in zip(GAINS, pieces[:-1]):
        grads["g_" + n] = summed[at:at + piece.shape[0]].reshape(1, -1)[0]
        at += piece.shape[0]
    grads["rel_bias"] = summed[at:].reshape(N_HEADS, N_REL_PAD)[:, :N_REL]

    delta, new_m, new_v = {}, {}, {}

    def adamw_big(group):
        names = ["w_" + n for n in G_GROUPS[group]]
        d, m, v = _adamw([wts[n] for n in names], [grads[n] for n in names], [ms[n] for n in names],
                         [vs[n] for n in names], 8, "adamw_" + group)
        for n, dd, mm, vv in zip(names, d, m, v):
            delta[n], new_m[n], new_v[n] = dd, mm, vv
        return d[0]

    done_early = adamw_big("early")
    grads.update(reduce_finish(late, done_early, "late"))
    adamw_big("late")
    small = ["g_" + n for n in GAINS] + ["rel_bias"]
    as_rows = lambda a: (a.reshape(-1, 128) if a.size % 128 == 0 else jnp.pad(a, ((0, 0), (0, N_REL_PAD - N_REL))).reshape(-1, 128))
    d, m, v = _adamw([as_rows(wts[n]) for n in small], [as_rows(grads[n]) for n in small],
                     [as_rows(ms[n]) for n in small], [as_rows(vs[n]) for n in small], 1, "adamw_small")
    for n, dd, mm, vv in zip(small, d, m, v):
        back = (lambda a: a.reshape(N_HEADS, N_REL_PAD)[:, :N_REL]) if n == "rel_bias" else (lambda a: a.reshape(-1))
        delta[n], new_m[n], new_v[n] = back(dd), back(mm), back(vv)

    loss = lax.psum(loss[0, 0], ("x", "y", "c"))
    outs = [loss, dx[None]]
    for table in (grads, delta, new_m, new_v):
        outs += [(table[n].T if n in TRANSPOSED else table[n])[None] for n in ORDER]
    return tuple(outs)
```

```python
import functools

import jax
import jax.numpy as jnp
from jax import lax
from jax.experimental import pallas as pl
from jax.experimental.pallas import tpu as pltpu

F32 = jnp.float32
BF16 = jnp.bfloat16
EPS = 1e-6
N_CHIPS = 4
HEAD_DIM = 64
N_HEADS = 8
CHUNK = 64
LOOKBACK = 8
BAND = (LOOKBACK + 1) * CHUNK
PAD = LOOKBACK * CHUNK
REL_CLIP = 128
N_REL = 2 * REL_CLIP + 1
N_REL_PAD = 384
SB_BLOCK = 256
PAIR = 2 * HEAD_DIM
ATT_SCALE = HEAD_DIM ** -0.5
NEG_INF = -1e30
ROW_BLOCK = 512
VMEM_LIMIT = 48 * 1024 * 1024
MESH = pl.DeviceIdType.MESH

ADAM_LR = 0.001
ADAM_B1 = 0.9
ADAM_B2 = 0.999
ADAM_EPS = 1e-08
ADAM_WD = 0.01
ADAM_STEP = 10

NT = (((1,), (1,)), ((), ()))
TN = (((0,), (0,)), ((), ()))


def _params(n_grid, vmem=None):
    return pltpu.CompilerParams(dimension_semantics=("arbitrary",) * n_grid, vmem_limit_bytes=vmem)


def _dot(a, b, dims=None):
    if dims is None:
        return jnp.dot(a, b, preferred_element_type=F32)
    return lax.dot_general(a, b, dims, preferred_element_type=F32)


def _sigmoid(x):
    return 1.0 / (1.0 + jnp.exp(-x))


def _rms_fwd(x, g):
    r = lax.rsqrt(jnp.mean(x * x, axis=-1, keepdims=True) + EPS)
    return x * r * g


def _rms_bwd(x, g, dy):
    r = lax.rsqrt(jnp.mean(x * x, axis=-1, keepdims=True) + EPS)
    xh = x * r
    dg = jnp.sum(dy * xh, axis=0, keepdims=True)
    t = dy * g
    dx = r * (t - xh * jnp.mean(t * xh, axis=-1, keepdims=True))
    return dx, dg


def _accumulate(ref, val, first):
    @pl.when(first)
    def _():
        ref[...] = val

    @pl.when(jnp.logical_not(first))
    def _():
        ref[...] += val


def _hidden_chunks(width, step=256):
    return [(lo, min(lo + step, width)) for lo in range(0, width, step)]


def _split2(x):
    hi = x.astype(BF16)
    lo = (x - hi.astype(F32)).astype(BF16)
    return hi, lo


def _ffn_fwd(x, g_pre, g_post, wg, wu, wd, name):
    T, D = x.shape
    S, FS, _ = wg.shape
    tm = min(ROW_BLOCK, T)

    def body(x_ref, gpre_ref, gpost_ref, wg_ref, wu_ref, wd_ref,
             h_ref, xn_ref, g_ref, u_ref, a_ref, f_ref, xn_s, acc_s):
        k = pl.program_id(1)

        @pl.when(k == 0)
        def _():
            xn_s[...] = _rms_fwd(x_ref[...], gpre_ref[...]).astype(BF16)
            xn_ref[...] = xn_s[...]

        xn = xn_s[...]
        part = None
        for lo, hi in _hidden_chunks(FS):
            g = _dot(xn, wg_ref[0, lo:hi, :], NT)
            u = _dot(xn, wu_ref[0, lo:hi, :], NT)
            g_ref[0, :, lo:hi] = g
            u_ref[0, :, lo:hi] = u
            a = (g * _sigmoid(g) * u).astype(BF16)
            a_ref[0, :, lo:hi] = a
            term = _dot(a, wd_ref[0, lo:hi, :])
            part = term if part is None else part + term
        _accumulate(acc_s, part, k == 0)

        @pl.when(k == S - 1)
        def _():
            f = acc_s[...]
            f_ref[...] = f
            h_ref[...] = x_ref[...] + 0.5 * _rms_fwd(f, gpost_ref[...])

    row = pl.BlockSpec((tm, D), lambda i, k: (i, 0))
    vec = pl.BlockSpec((1, D), lambda i, k: (0, 0))
    act = pl.BlockSpec((1, tm, FS), lambda i, k: (k, i, 0))
    return pl.pallas_call(
        body, name=name, grid=(T // tm, S),
        in_specs=[row, vec, vec] + [pl.BlockSpec((1, FS, D), lambda i, k: (k, 0, 0))] * 3,
        out_specs=[row, row, act, act, act, row],
        out_shape=[jax.ShapeDtypeStruct((T, D), F32), jax.ShapeDtypeStruct((T, D), BF16),
                   jax.ShapeDtypeStruct((S, T, FS), F32), jax.ShapeDtypeStruct((S, T, FS), F32),
                   jax.ShapeDtypeStruct((S, T, FS), BF16), jax.ShapeDtypeStruct((T, D), F32)],
        scratch_shapes=[pltpu.VMEM((tm, D), BF16), pltpu.VMEM((tm, D), F32)],
        compiler_params=_params(2, VMEM_LIMIT),
    )(x, g_pre, g_post, wg, wu, wd)


def _ffn_bwd_act(dh, f, g_post, wd, g_act, u_act, name):
    T, D = dh.shape
    S, FS, _ = wd.shape
    tm = min(ROW_BLOCK, T)

    def body(dh_ref, f_ref, gpost_ref, wd_ref, g_ref, u_ref, dgp_ref, dup_ref, df_ref, dgain_ref, df_s):
        i, k = pl.program_id(0), pl.program_id(1)

        @pl.when(k == 0)
        def _():
            df, dgain = _rms_bwd(f_ref[...], gpost_ref[...], 0.5 * dh_ref[...])
            df_s[...] = df.astype(BF16)
            df_ref[...] = df_s[...]
            _accumulate(dgain_ref, dgain, i == 0)

        df = df_s[...]
        for lo, hi in _hidden_chunks(FS):
            da = _dot(df, wd_ref[0, lo:hi, :], NT)
            g = g_ref[0, :, lo:hi]
            s = _sigmoid(g)
            dup_ref[0, :, lo:hi] = (da * (g * s)).astype(BF16)
            dgp_ref[0, :, lo:hi] = (da * u_ref[0, :, lo:hi] * (s * (1.0 + g * (1.0 - s)))).astype(BF16)

    row = pl.BlockSpec((tm, D), lambda i, k: (i, 0))
    vec = pl.BlockSpec((1, D), lambda i, k: (0, 0))
    act = pl.BlockSpec((1, tm, FS), lambda i, k: (k, i, 0))
    return pl.pallas_call(
        body, name=name, grid=(T // tm, S),
        in_specs=[row, row, vec, pl.BlockSpec((1, FS, D), lambda i, k: (k, 0, 0)), act, act],
        out_specs=[act, act, row, vec],
        out_shape=[jax.ShapeDtypeStruct((S, T, FS), BF16), jax.ShapeDtypeStruct((S, T, FS), BF16),
                   jax.ShapeDtypeStruct((T, D), BF16), jax.ShapeDtypeStruct((1, D), F32)],
        scratch_shapes=[pltpu.VMEM((tm, D), BF16)],
        compiler_params=_params(2, VMEM_LIMIT),
    )(dh, f, g_post, wd, g_act, u_act)


def _proj_bwd(dys, ws, x, g_pre, dh, name):
    T, D = x.shape
    n = len(dys)
    flat = dys[0].ndim == 2
    S = ws[0].shape[0]
    N = ws[0].shape[2] if flat else ws[0].shape[1]
    tm = min(ROW_BLOCK, T)

    def body(*refs):
        dy_refs, w_refs = refs[:n], refs[n:2 * n]
        x_ref, gpre_ref, dh_ref, dx_ref, dgain_ref, acc_s = refs[2 * n:]
        i, k = pl.program_id(0), pl.program_id(1)
        part = None
        for dy_ref, w_ref in zip(dy_refs, w_refs):
            term = _dot(dy_ref[...], w_ref[0], NT) if flat else _dot(dy_ref[0], w_ref[0])
            part = term if part is None else part + term
        _accumulate(acc_s, part, k == 0)

        @pl.when(k == S - 1)
        def _():
            dx, dgain = _rms_bwd(x_ref[...], gpre_ref[...], acc_s[...])
            dx_ref[...] = dh_ref[...] + dx
            _accumulate(dgain_ref, dgain, i == 0)

    row = pl.BlockSpec((tm, D), lambda i, k: (i, 0))
    vec = pl.BlockSpec((1, D), lambda i, k: (0, 0))
    return pl.pallas_call(
        body, name=name, grid=(T // tm, S),
        in_specs=[pl.BlockSpec((tm, N), lambda i, k: (i, k)) if flat else pl.BlockSpec((1, tm, N), lambda i, k: (k, i, 0))] * n
        + [pl.BlockSpec((1,) + ws[0].shape[1:], lambda i, k: (k, 0, 0))] * n + [row, vec, row],
        out_specs=[row, vec],
        out_shape=[jax.ShapeDtypeStruct((T, D), F32), jax.ShapeDtypeStruct((1, D), F32)],
        scratch_shapes=[pltpu.VMEM((tm, D), F32)],
        compiler_params=_params(2, VMEM_LIMIT),
    )(*dys, *ws, x, g_pre, dh)


def _mm_tn(a, b, bm, name, groups=None):
    ga, T, M = a.shape
    if groups is None:
        gb, _, N = b.shape
        b_spec = pl.BlockSpec((1, T, N), (lambda g, m: (g, 0, 0)) if gb > 1 else (lambda g, m: (0, 0, 0)))
    else:
        gb, N = groups, b.shape[1] // groups
        b_spec = pl.BlockSpec((T, N), lambda g, m: (0, g))
    G = max(ga, gb)

    def body(a_ref, b_ref, o_ref):
        bv = b_ref[0] if groups is None else b_ref[...]
        o_ref[0] = _dot(a_ref[0].astype(BF16), bv.astype(BF16), TN)

    return pl.pallas_call(
        body, name=name, grid=(G, M // bm),
        in_specs=[pl.BlockSpec((1, T, bm), (lambda g, m: (g, 0, m)) if ga > 1 else (lambda g, m: (0, 0, m))), b_spec],
        out_specs=pl.BlockSpec((1, bm, N), lambda g, m: (g, m, 0)),
        out_shape=jax.ShapeDtypeStruct((G, M, N), F32),
        compiler_params=_params(2, VMEM_LIMIT),
    )(a, b)


def _norm_proj(x, g_pre, w, name):
    T, D = x.shape
    S, _, N = w.shape
    tm = min(ROW_BLOCK, T)

    def body(x_ref, g_ref, w_ref, o_ref, xn_ref, xn_s):
        @pl.when(pl.program_id(1) == 0)
        def _():
            xn_s[...] = _rms_fwd(x_ref[...], g_ref[...]).astype(BF16)
            xn_ref[...] = xn_s[...]

        o_ref[...] = _dot(xn_s[...], w_ref[0]).astype(BF16)

    row = pl.BlockSpec((tm, D), lambda i, k: (i, 0))
    return pl.pallas_call(
        body, name=name, grid=(T // tm, S),
        in_specs=[row, pl.BlockSpec((1, D), lambda i, k: (0, 0)), pl.BlockSpec((1, D, N), lambda i, k: (k, 0, 0))],
        out_specs=[pl.BlockSpec((tm, N), lambda i, k: (i, k)), row],
        out_shape=[jax.ShapeDtypeStruct((T, S * N), BF16), jax.ShapeDtypeStruct((T, D), BF16)],
        scratch_shapes=[pltpu.VMEM((tm, D), BF16)],
        compiler_params=_params(2, VMEM_LIMIT),
    )(x, g_pre, w)


def _mix_out_fwd(h, o_a, o_b, g_sb, g_ch, w_out, g_post, name):
    T, D = h.shape
    W = g_sb.shape[1]
    tm = min(ROW_BLOCK, T)

    def body(h_ref, oa_ref, ob_ref, gsb_ref, gch_ref, w_ref, gpost_ref, h2_ref, mixed_ref, mo_ref):
        mixed_ref[:, :W] = _rms_fwd(oa_ref[...], gsb_ref[...]).astype(BF16)
        mixed_ref[:, W:] = _rms_fwd(ob_ref[...], gch_ref[...]).astype(BF16)
        mo = _dot(mixed_ref[...], w_ref[...])
        mo_ref[...] = mo
        h2_ref[...] = h_ref[...] + _rms_fwd(mo, gpost_ref[...])

    row = pl.BlockSpec((tm, D), lambda i: (i, 0))
    part = pl.BlockSpec((tm, W), lambda i: (i, 0))
    half = pl.BlockSpec((1, W), lambda i: (0, 0))
    return pl.pallas_call(
        body, name=name, grid=(T // tm,),
        in_specs=[row, part, part, half, half, pl.BlockSpec((D, D), lambda i: (0, 0)), pl.BlockSpec((1, D), lambda i: (0, 0))],
        out_specs=[row, row, row],
        out_shape=[jax.ShapeDtypeStruct((T, D), F32), jax.ShapeDtypeStruct((T, D), BF16),
                   jax.ShapeDtypeStruct((T, D), F32)],
        compiler_params=_params(1, VMEM_LIMIT),
    )(h, o_a, o_b, g_sb, g_ch, w_out, g_post)


def _mix_out_bwd(dh, mo, g_post, w_out, o_a, o_b, g_sb, g_ch, name):
    T, D = dh.shape
    W = g_sb.shape[1]
    tm = min(ROW_BLOCK, T)

    def body(dh_ref, mo_ref, gpost_ref, w_ref, oa_ref, ob_ref, gsb_ref, gch_ref,
             dmo_ref, doa_ref, dob_ref, dgpost_ref, dgsb_ref, dgch_ref):
        first = pl.program_id(0) == 0
        dmo, dgpost = _rms_bwd(mo_ref[...], gpost_ref[...], dh_ref[...])
        dmo_ref[...] = dmo.astype(BF16)
        dmix = _dot(dmo_ref[...], w_ref[...], NT)
        doa_ref[...], dgsb = _rms_bwd(oa_ref[...], gsb_ref[...], dmix[:, :W])
        dob_ref[...], dgch = _rms_bwd(ob_ref[...], gch_ref[...], dmix[:, W:])
        _accumulate(dgpost_ref, dgpost, first)
        _accumulate(dgsb_ref, dgsb, first)
        _accumulate(dgch_ref, dgch, first)

    row = pl.BlockSpec((tm, D), lambda i: (i, 0))
    part = pl.BlockSpec((tm, W), lambda i: (i, 0))
    vec = pl.BlockSpec((1, D), lambda i: (0, 0))
    half = pl.BlockSpec((1, W), lambda i: (0, 0))
    return pl.pallas_call(
        body, name=name, grid=(T // tm,),
        in_specs=[row, row, vec, pl.BlockSpec((D, D), lambda i: (0, 0)), part, part, half, half],
        out_specs=[row, part, part, vec, half, half],
        out_shape=[jax.ShapeDtypeStruct((T, D), BF16), jax.ShapeDtypeStruct((T, W), F32),
                   jax.ShapeDtypeStruct((T, W), F32), jax.ShapeDtypeStruct((1, D), F32),
                   jax.ShapeDtypeStruct((1, W), F32), jax.ShapeDtypeStruct((1, W), F32)],
        compiler_params=_params(1, VMEM_LIMIT),
    )(dh, mo, g_post, w_out, o_a, o_b, g_sb, g_ch)


def _ple_loss(h, p, target, w_proj, w_gate, g_post, name):
    T, D = h.shape
    P = p.shape[1]
    S = N_CHIPS
    C = D // S
    tm = min(ROW_BLOCK, T)

    def body(h_ref, p_ref, t_ref, wp_ref, wg_ref, g_ref, loss_ref, dh_ref, dproj_ref, dgate_ref, dgain_ref):
        first = pl.program_id(0) == 0
        h3 = h_ref[...]
        proj = _dot(p_ref[...].astype(BF16), wp_ref[...])
        s = _sigmoid(_dot(h3.astype(BF16), wg_ref[...]))
        e = proj * s
        diff = h3 + _rms_fwd(e, g_ref[...]) - t_ref[...]
        part = 0.5 * jnp.sum(jnp.mean(diff * diff, axis=-1, keepdims=True), axis=0, keepdims=True)
        _accumulate(loss_ref, jnp.broadcast_to(part, loss_ref.shape), first)
        dy = diff * (1.0 / D)
        de, dgain = _rms_bwd(e, g_ref[...], dy)
        _accumulate(dgain_ref, dgain, first)
        dproj = (de * s).astype(BF16)
        for j in range(S):
            dproj_ref[j] = dproj[:, j * C:(j + 1) * C]
        dgate_ref[...] = (de * proj * s * (1.0 - s)).astype(BF16)
        dh_ref[...] = dy + _dot(dgate_ref[...], wg_ref[...], NT)

    row = pl.BlockSpec((tm, D), lambda i: (i, 0))
    vec = pl.BlockSpec((1, D), lambda i: (0, 0))
    return pl.pallas_call(
        body, name=name, grid=(T // tm,),
        in_specs=[row, pl.BlockSpec((tm, P), lambda i: (i, 0)), row,
                  pl.BlockSpec((P, D), lambda i: (0, 0)), pl.BlockSpec((D, D), lambda i: (0, 0)), vec],
        out_specs=[pl.BlockSpec((8, 128), lambda i: (0, 0)), row,
                   pl.BlockSpec((S, tm, C), lambda i: (0, i, 0)), row, vec],
        out_shape=[jax.ShapeDtypeStruct((8, 128), F32), jax.ShapeDtypeStruct((T, D), F32),
                   jax.ShapeDtypeStruct((S, T, C), BF16), jax.ShapeDtypeStruct((T, D), BF16),
                   jax.ShapeDtypeStruct((1, D), F32)],
        compiler_params=_params(1, VMEM_LIMIT),
    )(h, p, target, w_proj, w_gate, g_post)


def _sb_scores(q, kj, mask):
    z = _dot(q, kj, NT) * ATT_SCALE
    sp = jnp.maximum(z, 0.0) + jnp.log(1.0 + jnp.exp(-jnp.abs(z)))
    lf = -sp if mask is None else jnp.where(mask, -sp, 0.0)
    return z, sp, lf


def _strict_causal():
    rows = lax.broadcasted_iota(jnp.int32, (SB_BLOCK, SB_BLOCK), 0)
    cols = lax.broadcasted_iota(jnp.int32, (SB_BLOCK, SB_BLOCK), 1)
    return cols < rows


def _tri(cmp):
    r = lax.broadcasted_iota(jnp.int32, (SB_BLOCK, SB_BLOCK), 0)
    c = lax.broadcasted_iota(jnp.int32, (SB_BLOCK, SB_BLOCK), 1)
    return jnp.where(cmp(r, c), 1.0, 0.0).astype(BF16)


def _cum(x, tri):
    hi, lo = _split2(x)
    return _dot(hi, tri) + _dot(lo, tri)


def _pair_lanes():
    lane = lax.broadcasted_iota(jnp.int32, (1, PAIR), 1)
    return [lane < HEAD_DIM, lane >= HEAD_DIM]


def _only(lanes, x):
    return jnp.where(lanes, x, jnp.zeros_like(x))


def _sb_fwd(qkv, name):
    T = qkv.shape[0]
    B = SB_BLOCK
    pairs = N_HEADS // 2

    def body(q_ref, k_ref, v_ref, o_ref, tot_ref):
        i = pl.program_id(1)
        after = _tri(lambda r, c: r > c)
        lanes = _pair_lanes()
        q = [_only(lanes[h], q_ref[...]) for h in range(2)]

        def tile(h, j, carry, mask):
            run, acc = carry
            at = pl.ds(pl.multiple_of(j * B, B), B)
            z, sp, lf = _sb_scores(q[h], k_ref[at, :], mask)
            a = jnp.exp((z - sp) + _cum(lf, after) + run)
            if mask is not None:
                a = jnp.where(mask, a, 0.0)
            return (run + jnp.sum(lf, axis=1, keepdims=True),
                    acc + _dot(a.astype(BF16), _only(lanes[h], v_ref[at, :])))

        zero = (jnp.zeros((B, 1), F32), jnp.zeros((B, PAIR), F32))
        diag = _strict_causal()
        carries = tuple(tile(h, i, zero, diag) for h in range(2))
        carries = lax.fori_loop(
            0, i, lambda jj, cs: tuple(tile(h, i - 1 - jj, cs[h], None) for h in range(2)), carries)
        o_ref[...] = carries[0][1] + carries[1][1]
        tot_ref[...] = jnp.where(lanes[0], carries[0][0], carries[1][0])

    blk = lambda off: pl.BlockSpec((B, PAIR), lambda g, i: (i, g + off))
    full = lambda off: pl.BlockSpec((T, PAIR), lambda g, i: (0, g + off))
    out = jax.ShapeDtypeStruct((T, pairs * PAIR), F32)
    return pl.pallas_call(
        body, name=name, grid=(pairs, T // B),
        in_specs=[blk(0), full(pairs), full(2 * pairs)],
        out_specs=[blk(0), blk(0)],
        out_shape=[out, out],
        compiler_params=_params(2, VMEM_LIMIT),
    )(qkv, qkv, qkv)


def _sb_bwd(qkv, do, tot, name):
    T = qkv.shape[0]
    B = SB_BLOCK
    pairs = N_HEADS // 2
    n_blocks = T // B

    def body(q_ref, k_ref, v_ref, do_ref, tot_ref, dq_ref, dk_ref, dv_ref, dk_s, dv_s):
        i = pl.program_id(1)

        @pl.when(i == 0)
        def _():
            dk_s[...] = jnp.zeros_like(dk_s)
            dv_s[...] = jnp.zeros_like(dv_s)

        upto = _tri(lambda r, c: r <= c)
        below = _tri(lambda r, c: r < c)
        lanes = _pair_lanes()
        q = [_only(lanes[h], q_ref[...]) for h in range(2)]
        dob = do_ref[...].astype(BF16)
        do = [_only(lanes[h], dob) for h in range(2)]
        tot = [tot_ref[:, 0:1], tot_ref[:, HEAD_DIM:HEAD_DIM + 1]]

        def tile(h, j, carry, mask):
            pre_lf, pre_g, dq = carry
            at = pl.ds(pl.multiple_of(j * B, B), B)
            kj, vj = k_ref[at, :], v_ref[at, :]
            z, sp, lf = _sb_scores(q[h], kj, mask)
            later = tot[h] - pre_lf - _cum(lf, upto)
            a = jnp.exp((z - sp) + later)
            if mask is not None:
                a = jnp.where(mask, a, 0.0)
            g = a * _dot(do[h], vj, NT)
            g_before = pre_g + _cum(g, below)
            fail = jnp.exp(-sp)
            dz = (g * fail - (1.0 - fail) * g_before) * ATT_SCALE
            if mask is not None:
                dz = jnp.where(mask, dz, 0.0)
            dzb = dz.astype(BF16)
            dk_s[at, :] += _dot(dzb, q[h], TN)
            dv_s[at, :] += _dot(a.astype(BF16), do[h], TN)
            return (pre_lf + jnp.sum(lf, axis=1, keepdims=True), pre_g + jnp.sum(g, axis=1, keepdims=True),
                    dq + _dot(dzb, _only(lanes[h], kj)))

        col = jnp.zeros((B, 1), F32)
        zero = (col, col, jnp.zeros((B, PAIR), F32))
        carries = lax.fori_loop(
            0, i, lambda j, cs: tuple(tile(h, j, cs[h], None) for h in range(2)), (zero, zero))
        diag = _strict_causal()
        dq_ref[...] = (tile(0, i, carries[0], diag)[2] + tile(1, i, carries[1], diag)[2]).astype(BF16)

        @pl.when(i == n_blocks - 1)
        def _():
            dk_ref[...] = dk_s[...].astype(BF16)
            dv_ref[...] = dv_s[...].astype(BF16)

    blk = lambda off: pl.BlockSpec((B, PAIR), lambda g, i: (i, g + off))
    full = lambda off: pl.BlockSpec((T, PAIR), lambda g, i: (0, g + off))
    out = jax.ShapeDtypeStruct((T, pairs * PAIR), BF16)
    return pl.pallas_call(
        body, name=name, grid=(pairs, n_blocks),
        in_specs=[blk(0), full(pairs), full(2 * pairs), blk(0), blk(0)],
        out_specs=[blk(0), full(0), full(0)],
        out_shape=[out, out, out],
        scratch_shapes=[pltpu.VMEM((T, PAIR), F32)] * 2,
        compiler_params=_params(2, VMEM_LIMIT),
    )(qkv, qkv, qkv, do, tot)


NEAR = BAND - PAD + REL_CLIP
FAR = BAND - NEAR
NEAR_REL = 2 * REL_CLIP
BIAS_ROWS = 8


def _rel_onehot(i, transposed):
    shape = (NEAR, NEAR_REL) if transposed else (NEAR_REL, NEAR)
    j = FAR + lax.broadcasted_iota(jnp.int32, shape, 0 if transposed else 1)
    r = lax.broadcasted_iota(jnp.int32, shape, 1 if transposed else 0)
    idx = jnp.clip(i + PAD - j, -REL_CLIP, REL_CLIP) + REL_CLIP
    return jnp.where(idx - 1 == r, 1.0, 0.0).astype(BF16)


def _bias_table(rel_bias, name):
    def body(near_ref, far_ref, o_ref):
        rb = near_ref[...]
        hi, lo = _split2(rb)
        lo2 = (rb - hi.astype(F32) - lo.astype(F32)).astype(BF16)
        far = jnp.broadcast_to(far_ref[...], (N_HEADS, FAR))
        for k in range(BIAS_ROWS):
            onehot = _rel_onehot(pl.program_id(0) * BIAS_ROWS + k, False)
            o_ref[k, :, :FAR] = far
            o_ref[k, :, FAR:] = _dot(hi, onehot) + _dot(lo, onehot) + _dot(lo2, onehot)

    return pl.pallas_call(
        body, name=name, grid=(CHUNK // BIAS_ROWS,),
        in_specs=[pl.BlockSpec((N_HEADS, NEAR_REL), lambda i: (0, 0)), pl.BlockSpec((N_HEADS, 1), lambda i: (0, 0))],
        out_specs=pl.BlockSpec((BIAS_ROWS, N_HEADS, BAND), lambda i: (i, 0, 0)),
        out_shape=jax.ShapeDtypeStruct((CHUNK, N_HEADS, BAND), F32),
        compiler_params=_params(1),
    )(rel_bias[:, 1:], rel_bias[:, N_REL - 1:])


def _bias_grad(dbias_t, name):
    def body(d_ref, near_ref, far_ref):
        near, far = None, None
        for k in range(BIAS_ROWS):
            onehot = _rel_onehot(pl.program_id(0) * BIAS_ROWS + k, True)
            hi, lo = _split2(d_ref[k, :, FAR:])
            part = _dot(hi, onehot) + _dot(lo, onehot)
            rest = jnp.sum(d_ref[k, :, :FAR], axis=1, keepdims=True)
            near, far = (part, rest) if near is None else (near + part, far + rest)
        first = pl.program_id(0) == 0
        _accumulate(near_ref, near, first)
        _accumulate(far_ref, jnp.broadcast_to(far, far_ref.shape), first)

    near, far = pl.pallas_call(
        body, name=name, grid=(CHUNK // BIAS_ROWS,),
        in_specs=[pl.BlockSpec((BIAS_ROWS, N_HEADS, BAND), lambda i: (i, 0, 0))],
        out_specs=[pl.BlockSpec((N_HEADS, NEAR_REL), lambda i: (0, 0)), pl.BlockSpec((N_HEADS, 128), lambda i: (0, 0))],
        out_shape=[jax.ShapeDtypeStruct((N_HEADS, NEAR_REL), F32), jax.ShapeDtypeStruct((N_HEADS, 128), F32)],
        compiler_params=_params(1),
    )(dbias_t)
    return jnp.pad(near, ((0, 0), (1, 0))).at[:, N_REL - 1].add(far[:, 0])


def _ch_probs(q, kw, bias, valid):
    z = jnp.where(valid, _dot(q, kw, NT) * ATT_SCALE + bias, NEG_INF)
    e = jnp.exp(z - jnp.max(z, axis=-1, keepdims=True))
    return e / jnp.sum(e, axis=-1, keepdims=True)


def _ch_valid(n):
    slot = lax.broadcasted_iota(jnp.int32, (CHUNK, BAND), 1) // CHUNK
    return n + slot - LOOKBACK >= 0


def _ch_fwd(qkv, bias, name):
    T = qkv.shape[0]
    W = N_HEADS * HEAD_DIM

    def body(q_ref, k_ref, v_ref, b_ref, o_ref, kp, vp):
        n = pl.program_id(0)

        @pl.when(n == 0)
        def _():
            _ch_load_padded(k_ref, v_ref, kp, vp)

        win = pl.ds(pl.multiple_of(n * CHUNK, CHUNK), BAND)
        valid = _ch_valid(n)
        lanes = _pair_lanes()
        for pair in range(N_HEADS // 2):
            cols = slice(pair * PAIR, (pair + 1) * PAIR)
            q, kw, vw = q_ref[:, cols], kp[win, cols], vp[win, cols]
            o = None
            for h in range(2):
                p = _ch_probs(_only(lanes[h], q), kw, b_ref[2 * pair + h], valid)
                part = _dot(p.astype(BF16), _only(lanes[h], vw))
                o = part if o is None else o + part
            o_ref[:, cols] = o

    full = lambda col: pl.BlockSpec((T, W), lambda n: (0, col))
    return pl.pallas_call(
        body, name=name, grid=(T // CHUNK,),
        in_specs=[pl.BlockSpec((CHUNK, W), lambda n: (n, 3)), full(4), full(5),
                  pl.BlockSpec((N_HEADS, CHUNK, BAND), lambda n: (0, 0, 0))],
        out_specs=pl.BlockSpec((CHUNK, W), lambda n: (n, 0)),
        out_shape=jax.ShapeDtypeStruct((T, W), F32),
        scratch_shapes=[pltpu.VMEM((PAD + T, W), BF16)] * 2,
        compiler_params=_params(1, VMEM_LIMIT),
    )(qkv, qkv, qkv, bias)


def _ch_load_padded(k_ref, v_ref, kp, vp):
    for src, dst in ((k_ref, kp), (v_ref, vp)):
        dst[:PAD, :] = jnp.zeros((PAD, dst.shape[1]), dst.dtype)
        dst[PAD:, :] = src[...]


def _ch_bwd(qkv, bias, do, name):
    T = qkv.shape[0]
    W = N_HEADS * HEAD_DIM
    n_chunks = T // CHUNK

    def body(q_ref, k_ref, v_ref, b_ref, do_ref, dq_ref, dk_ref, dv_ref, db_ref, kp, vp, dk_s, dv_s):
        n = pl.program_id(0)

        @pl.when(n == 0)
        def _():
            _ch_load_padded(k_ref, v_ref, kp, vp)
            dk_s[...] = jnp.zeros_like(dk_s)
            dv_s[...] = jnp.zeros_like(dv_s)
            db_ref[...] = jnp.zeros_like(db_ref)

        win = pl.ds(pl.multiple_of(n * CHUNK, CHUNK), BAND)
        valid = _ch_valid(n)
        lanes = _pair_lanes()
        for pair in range(N_HEADS // 2):
            cols = slice(pair * PAIR, (pair + 1) * PAIR)
            q, kw, vw = q_ref[:, cols], kp[win, cols], vp[win, cols]
            dob = do_ref[:, cols].astype(BF16)
            dq = dk = dv = None
            for h in range(2):
                qh, doh = _only(lanes[h], q), _only(lanes[h], dob)
                p = _ch_probs(qh, kw, b_ref[2 * pair + h], valid)
                dp = _dot(doh, vw, NT)
                dz = p * (dp - jnp.sum(dp * p, axis=-1, keepdims=True))
                db_ref[2 * pair + h] += dz
                dzb = (dz * ATT_SCALE).astype(BF16)
                parts = (_dot(dzb, _only(lanes[h], kw)), _dot(dzb, qh, TN), _dot(p.astype(BF16), doh, TN))
                dq, dk, dv = parts if dq is None else (dq + parts[0], dk + parts[1], dv + parts[2])
            dq_ref[:, cols] = dq.astype(BF16)
            dk_s[win, cols] += dk
            dv_s[win, cols] += dv

        @pl.when(n == n_chunks - 1)
        def _():
            dk_ref[...] = dk_s[PAD:, :].astype(BF16)
            dv_ref[...] = dv_s[PAD:, :].astype(BF16)

    full = lambda col: pl.BlockSpec((T, W), lambda n: (0, col))
    blk = lambda col: pl.BlockSpec((CHUNK, W), lambda n: (n, col))
    tab = pl.BlockSpec((N_HEADS, CHUNK, BAND), lambda n: (0, 0, 0))
    out = jax.ShapeDtypeStruct((T, W), BF16)
    return pl.pallas_call(
        body, name=name, grid=(n_chunks,),
        in_specs=[blk(3), full(4), full(5), tab, blk(0)],
        out_specs=[blk(0), full(0), full(0), tab],
        out_shape=[out, out, out, jax.ShapeDtypeStruct((N_HEADS, CHUNK, BAND), F32)],
        scratch_shapes=[pltpu.VMEM((PAD + T, W), BF16)] * 2 + [pltpu.VMEM((PAD + T, W), F32)] * 2,
        compiler_params=_params(1, VMEM_LIMIT),
    )(qkv, qkv, qkv, bias, do)


def _rows_split(a, parts):
    return a.reshape(a.shape[:-2] + (parts, a.shape[-2] // parts, a.shape[-1]))


def _cast_into_slot0(c, ws, name):
    parts = 2
    ws = [_rows_split(_rows_split(w, 2), parts) for w in ws]
    n = len(ws)

    def body(c_ref, *refs):
        for src, dst in zip(refs[:n], refs[n:]):
            dst[0, 0, 0] = src[0, 0].astype(BF16)

    outs = pl.pallas_call(
        body, name=name,
        grid_spec=pltpu.PrefetchScalarGridSpec(
            num_scalar_prefetch=1, grid=(2, parts),
            in_specs=[pl.BlockSpec((1, 1) + w.shape[2:], lambda d, r, c_ref: (d ^ c_ref[0], r, 0, 0)) for w in ws],
            out_specs=[pl.BlockSpec((1, 1, 1) + w.shape[2:], lambda d, r, c_ref: (0, d, r, 0, 0)) for w in ws]),
        out_shape=[jax.ShapeDtypeStruct((N_CHIPS,) + w.shape, BF16) for w in ws],
        compiler_params=_params(2, VMEM_LIMIT),
    )(c, *ws)
    return [o.reshape(N_CHIPS, 2, o.shape[2] * o.shape[3], o.shape[4]) for o in outs]


def _chip_order(me, c, lands, name):
    parts = 2
    xs = [_rows_split(x, parts) for x in lands]

    def body(me_ref, c_ref, *refs):
        n = len(refs) // 2
        for src, dst in zip(refs[:n], refs[n:]):
            dst[...] = src[...]

    outs = pl.pallas_call(
        body, name=name,
        grid_spec=pltpu.PrefetchScalarGridSpec(
            num_scalar_prefetch=2, grid=(N_CHIPS, 2, parts),
            in_specs=[pl.BlockSpec((1, 1, 1) + x.shape[3:],
                                   lambda j, h, r, me_ref, c_ref: (j ^ me_ref[0], h ^ c_ref[0], r, 0, 0)) for x in xs],
            out_specs=[pl.BlockSpec((1, 1, 1) + x.shape[3:], lambda j, h, r, me_ref, c_ref: (j, h, r, 0, 0))
                       for x in xs]),
        out_shape=[jax.ShapeDtypeStruct(x.shape, x.dtype) for x in xs],
        compiler_params=_params(3, VMEM_LIMIT),
    )(me, c, *xs)
    return [o.reshape(o.shape[0], 2 * parts * o.shape[3], o.shape[4]) for o in outs]


def _pair_add(c, mine, got, permuted, name):
    parts = 2
    mine = [_rows_split(m, parts) for m in mine]
    got = [_rows_split(g, parts) for g in got]
    n = len(mine)

    def body(c_ref, *refs):
        for a, b, o in zip(refs[:n], refs[n:2 * n], refs[2 * n:]):
            o[0, 0] = (a[0, 0, 0] + b[0, 0]).astype(BF16)

    def mine_spec(m, perm):
        if perm:
            return pl.BlockSpec((1, 1, 1) + m.shape[3:], lambda j, r, c_ref: (j, 0, r, 0, 0))
        return pl.BlockSpec((1, 1, 1) + m.shape[3:], lambda j, r, c_ref: (j, c_ref[0], r, 0, 0))

    def got_spec(g):
        return pl.BlockSpec((1, 1) + g.shape[2:], lambda j, r, c_ref: (j, r, 0, 0))

    outs = pl.pallas_call(
        body, name=name,
        grid_spec=pltpu.PrefetchScalarGridSpec(
            num_scalar_prefetch=1, grid=(N_CHIPS, parts),
            in_specs=[mine_spec(m, perm) for m, perm in zip(mine, permuted)] + [got_spec(g) for g in got],
            out_specs=[got_spec(g) for g in got]),
        out_shape=[jax.ShapeDtypeStruct(g.shape, BF16) for g in got],
        compiler_params=_params(2, VMEM_LIMIT),
    )(c, *mine, *got)
    return [o.reshape(o.shape[0], o.shape[1] * o.shape[2], o.shape[3]) for o in outs]


def _chip_add(me, partials, landed, permuted, name):
    parts = 2
    ps = [_rows_split(x, parts) for x in partials]
    ls = [_rows_split(x, parts) for x in landed]
    n = len(ps)

    def body(me_ref, *refs):
        for own, got, o in zip(refs[:n], refs[n:2 * n], refs[2 * n:]):
            acc = own[0, 0].astype(F32)
            for r in range(N_CHIPS - 1):
                acc = acc + got[r, 0].astype(F32)
            o[0] = acc

    def own_spec(x, perm):
        if perm:
            return pl.BlockSpec((1, 1) + x.shape[2:], lambda r, me_ref: (0, r, 0, 0))
        return pl.BlockSpec((1, 1) + x.shape[2:], lambda r, me_ref: (me_ref[0], r, 0, 0))

    outs = pl.pallas_call(
        body, name=name,
        grid_spec=pltpu.PrefetchScalarGridSpec(
            num_scalar_prefetch=1, grid=(parts,),
            in_specs=[own_spec(x, perm) for x, perm in zip(ps, permuted)]
            + [pl.BlockSpec((N_CHIPS - 1, 1) + x.shape[2:], lambda r, me_ref: (0, r, 0, 0)) for x in ls],
            out_specs=[pl.BlockSpec((1,) + x.shape[2:], lambda r, me_ref: (r, 0, 0)) for x in ps]),
        out_shape=[jax.ShapeDtypeStruct(x.shape[1:], F32) for x in ps],
        compiler_params=_params(1, VMEM_LIMIT),
    )(me, *ps, *ls)
    return [o.reshape(o.shape[0] * o.shape[1], o.shape[2]) for o in outs]


def _adamw_math(w, g, m, v):
    m = ADAM_B1 * m + (1.0 - ADAM_B1) * g
    v = ADAM_B2 * v + (1.0 - ADAM_B2) * (g * g)
    m_hat = m / (1.0 - ADAM_B1 ** ADAM_STEP)
    v_hat = v / (1.0 - ADAM_B2 ** ADAM_STEP)
    delta = -ADAM_LR * (m_hat / (jnp.sqrt(v_hat) + ADAM_EPS) + ADAM_WD * w)
    return delta, m, v


def _adamw(ws, gs, ms, vs, parts, name):
    n = len(ws)
    flat = [_rows_split(a, parts) for a in (*ws, *gs, *ms, *vs)]

    def body(*refs):
        ins, outs = refs[:4 * n], refs[4 * n:]
        for k in range(n):
            d, m, v = _adamw_math(ins[k][...], ins[n + k][...], ins[2 * n + k][...], ins[3 * n + k][...])
            outs[k][...] = d
            outs[n + k][...] = m
            outs[2 * n + k][...] = v

    spec = lambda a: pl.BlockSpec((1,) + a.shape[1:], lambda i: (i, 0, 0))
    outs = pl.pallas_call(
        body, name=name, grid=(parts,),
        in_specs=[spec(a) for a in flat], out_specs=[spec(a) for a in flat[:n]] * 3,
        out_shape=[jax.ShapeDtypeStruct(a.shape, F32) for a in flat[:n]] * 3,
        compiler_params=_params(1, VMEM_LIMIT),
    )(*flat)
    outs = [o.reshape(o.shape[0] * o.shape[1], o.shape[2]) for o in outs]
    return outs[:n], outs[n:2 * n], outs[2 * n:]


def _place():
    x, y, c = lax.axis_index("x"), lax.axis_index("y"), lax.axis_index("c")
    peers = [(x ^ (r >> 1), y ^ (r & 1), c) for r in (1, 2, 3)]
    return x, y, c, peers


def _handshake(peers):
    barrier = pltpu.get_barrier_semaphore()
    for peer in peers:
        pl.semaphore_signal(barrier, inc=1, device_id=peer, device_id_type=MESH)
    pl.semaphore_wait(barrier, len(peers))


ANY = pl.BlockSpec(memory_space=pl.ANY)
HBM = pl.BlockSpec(memory_space=pltpu.HBM)
SEM = pl.BlockSpec(memory_space=pltpu.SEMAPHORE)
SPLIT_COPY = pltpu.SideEffectType.DATAFLOW_SIDE_EFFECTING


def _in_hbm(a):
    return pltpu.with_memory_space_constraint(a, pltpu.HBM)


def _split_start(body, name, collective_id, operands, n_sems, after=None):
    n = len(operands)
    extra = [] if after is None else [after]

    def wrapped(*refs):
        at = n + len(extra)
        body(refs[:n], refs[at], refs[at + 1])
        token = refs[-1]
        token[...] = jnp.zeros_like(token)

    outs = pl.pallas_call(
        wrapped, name=name,
        in_specs=[HBM] * n + [ANY] * len(extra),
        out_shape=(pltpu.SemaphoreType.DMA((n_sems,)), pltpu.SemaphoreType.DMA((n_sems,)),
                   *[pltpu.HBM(a.shape, a.dtype) for a in operands], jax.ShapeDtypeStruct((8, 128), F32)),
        out_specs=(SEM, SEM, *[HBM] * n, pl.BlockSpec(memory_space=pltpu.VMEM)),
        input_output_aliases={i: 2 + i for i in range(n)},
        compiler_params=pltpu.CompilerParams(has_side_effects=SPLIT_COPY, collective_id=collective_id),
    )(*[_in_hbm(a) for a in operands], *extra)
    return outs[0], outs[1], list(outs[2:2 + n]), outs[-1]


def _split_wait(body, name, send_sem, recv_sem, operands, after):
    n = len(operands)

    def wrapped(*refs):
        body(refs[:n], refs[n], refs[n + 1])

    outs = pl.pallas_call(
        wrapped, name=name,
        in_specs=[HBM] * n + [SEM, SEM, ANY],
        out_shape=tuple(pltpu.HBM(a.shape, a.dtype) for a in operands),
        out_specs=tuple([HBM] * n),
        input_output_aliases={i: i for i in range(n)},
        compiler_params=pltpu.CompilerParams(has_side_effects=SPLIT_COPY),
    )(*operands, send_sem, recv_sem, after)
    return list(outs)


def _gather_copies(lands, send_sem, recv_sem):
    peers = _place()[3]
    return [pltpu.make_async_remote_copy(
        src_ref=land.at[0, 0], dst_ref=land.at[r + 1, 0],
        send_sem=send_sem.at[a * 3 + r], recv_sem=recv_sem.at[a * 3 + r],
        device_id=peers[r], device_id_type=MESH) for a, land in enumerate(lands) for r in range(3)]


def _gather_start(lands, name, collective_id, after):
    def body(refs, send_sem, recv_sem):
        _handshake(_place()[3])
        for cp in _gather_copies(refs, send_sem, recv_sem):
            cp.start()

    return _split_start(body, name, collective_id, list(lands), 3 * len(lands), after)


def _gather_wait(send_sem, recv_sem, operands, after, name):
    def body(refs, send_sem, recv_sem):
        for cp in _gather_copies(refs, send_sem, recv_sem):
            cp.wait_send()
            cp.wait_recv()

    return _split_wait(body, name, send_sem, recv_sem, operands, after)


def _gather_finish(lands, with_ici, name):
    n = len(lands)

    def body(*refs):
        land = refs[n:2 * n]
        send_ici, recv_ici, send_d2d, recv_d2d = refs[2 * n:]
        x, y, c, _ = _place()
        ici = _gather_copies(land, send_ici, recv_ici) if with_ici else []
        for cp in ici:
            cp.start()
        passed = [pltpu.make_async_remote_copy(
            src_ref=land[a].at[r + 1, 0], dst_ref=land[a].at[r + 1, 1],
            send_sem=send_d2d.at[a * 3 + r], recv_sem=recv_d2d.at[a * 3 + r],
            device_id=(x, y, 1 - c), device_id_type=MESH) for a in range(n) for r in range(3)]
        for k, cp in enumerate(passed):
            if with_ici:
                ici[k].wait_recv()
            cp.start()
        for cp in passed:
            cp.wait_recv()
        for cp in ici:
            cp.wait_send()
        for cp in passed:
            cp.wait_send()

    outs = pl.pallas_call(
        body, name=name,
        in_specs=[ANY] * n, out_specs=[ANY] * n,
        out_shape=[jax.ShapeDtypeStruct(l.shape, l.dtype) for l in lands],
        input_output_aliases={a: a for a in range(n)},
        scratch_shapes=[pltpu.SemaphoreType.DMA((3 * n,))] * 4,
    )(*lands)
    return list(outs)


def _slabs(land):
    return land.reshape(N_CHIPS, 2 * land.shape[2], land.shape[3])


def _pair_swap(grads, permuted, name):
    n = len(grads)

    def body(*refs):
        src, dst = refs[:n], refs[n:2 * n]
        send_sem, recv_sem = refs[2 * n:]
        x, y, c, _ = _place()
        copies = [pltpu.make_async_remote_copy(
            src_ref=src[a].at[:, 1] if permuted[a] else src[a].at[:, 1 - c], dst_ref=dst[a],
            send_sem=send_sem.at[a], recv_sem=recv_sem.at[a],
            device_id=(x, y, 1 - c), device_id_type=MESH) for a in range(n)]
        for cp in copies:
            cp.start()
        for cp in copies:
            cp.wait()

    return pl.pallas_call(
        body, name=name,
        in_specs=[ANY] * n, out_specs=[ANY] * n,
        out_shape=[jax.ShapeDtypeStruct((N_CHIPS,) + g.shape[2:], F32) for g in grads],
        scratch_shapes=[pltpu.SemaphoreType.DMA((n,))] * 2,
    )(*grads)


def _scatter_copies(refs, permuted, send_sem, recv_sem):
    n = len(refs) // 2
    x, y, _, peers = _place()
    me = 2 * x + y
    return [pltpu.make_async_remote_copy(
        src_ref=refs[a].at[r + 1] if permuted[a] else refs[a].at[me ^ (r + 1)], dst_ref=refs[n + a].at[r],
        send_sem=send_sem.at[a * 3 + r], recv_sem=recv_sem.at[a * 3 + r],
        device_id=peers[r], device_id_type=MESH) for a in range(n) for r in range(3)]


def _scatter_start(partials, permuted, name, collective_id):
    def body(refs, send_sem, recv_sem):
        _handshake(_place()[3])
        for cp in _scatter_copies(refs, permuted, send_sem, recv_sem):
            cp.start()

    lands = [lax.empty((N_CHIPS - 1,) + p.shape[1:], p.dtype) for p in partials]
    return _split_start(body, name, collective_id, list(partials) + lands, 3 * len(partials))


def _scatter_wait(send_sem, recv_sem, operands, permuted, after, name):
    def body(refs, send_sem, recv_sem):
        for cp in _scatter_copies(refs, permuted, send_sem, recv_sem):
            cp.wait_send()
            cp.wait_recv()

    return _split_wait(body, name, send_sem, recv_sem, operands, after)


def _pair_join(halves, name):
    n = len(halves)

    def body(*refs):
        src, dst = refs[:n], refs[n:2 * n]
        send_sem, recv_sem = refs[2 * n:]
        x, y, c, _ = _place()
        copies = [pltpu.make_async_remote_copy(
            src_ref=src[a], dst_ref=dst[a], send_sem=send_sem.at[a], recv_sem=recv_sem.at[a],
            device_id=(x, y, 1 - c), device_id_type=MESH) for a in range(n)]
        for cp in copies:
            cp.start()
        for cp in copies:
            cp.wait()

    return pl.pallas_call(
        body, name=name,
        in_specs=[ANY] * n, out_specs=[ANY] * n,
        out_shape=[jax.ShapeDtypeStruct(h.shape, F32) for h in halves],
        scratch_shapes=[pltpu.SemaphoreType.DMA((n,))] * 2,
    )(*halves)


def _all_sum_small(v, name):
    R, C = v.shape
    n_dev = 8

    def body(v_ref, o_ref, buf, send_sem, recv_sem):
        x, y, c, _ = _place()
        me = 4 * x + 2 * y + c
        buf[me] = v_ref[...]
        copies = []
        for k in range(1, n_dev):
            peer = (x ^ (k >> 2), y ^ ((k >> 1) & 1), c ^ (k & 1))
            copies.append(pltpu.make_async_remote_copy(
                src_ref=v_ref, dst_ref=buf.at[me], send_sem=send_sem.at[k - 1], recv_sem=recv_sem.at[k - 1],
                device_id=peer, device_id_type=MESH))
        for cp in copies:
            cp.start()
        for cp in copies:
            cp.wait()
        acc = buf[0]
        for m in range(1, n_dev):
            acc = acc + buf[m]
        o_ref[...] = acc

    return pl.pallas_call(
        body, name=name,
        in_specs=[pl.BlockSpec(memory_space=pltpu.VMEM)], out_specs=pl.BlockSpec(memory_space=pltpu.VMEM),
        out_shape=jax.ShapeDtypeStruct((R, C), F32),
        scratch_shapes=[pltpu.VMEM((n_dev, R, C), F32), pltpu.SemaphoreType.DMA((n_dev - 1,)),
                        pltpu.SemaphoreType.DMA((n_dev - 1,))],
    )(v)


class _WholeWeights:
    def __init__(self, w):
        self.w = w

    def weights(self, group, after=None):
        return self.w

    def grads_ready(self, group, gw):
        return None


def _local_step(x, p, target, gains, rel_bias, hooks):
    T, D = x.shape
    S = N_CHIPS

    w = dict(hooks.weights("first"))
    h1, xn1, g1, u1, a1, f1 = _ffn_fwd(x, gains["ffn1_pre"], gains["ffn1_post"], w["ffn1_gate"], w["ffn1_up"],
                                       w["ffn1_down"], "ffn1_fwd")
    qkv, un = _norm_proj(h1, gains["mix_pre"], w["in"], "qkv_proj")
    bias = _bias_table(rel_bias, "bias_table").transpose(1, 0, 2)
    o_a, tot = _sb_fwd(qkv, "sb_fwd")
    o_b = _ch_fwd(qkv, bias, "ch_fwd")
    w.update(hooks.weights("rest", o_b))
    w_out = w["out"].reshape(D, D)
    h2, mixed, mo = _mix_out_fwd(h1, o_a, o_b, gains["out_sb"], gains["out_ch"], w_out, gains["mix_post"],
                                 "mix_out_fwd")
    h3, xn2, g2, u2, a2, f2 = _ffn_fwd(h2, gains["ffn2_pre"], gains["ffn2_post"], w["ffn2_gate"], w["ffn2_up"],
                                       w["ffn2_down"], "ffn2_fwd")
    w_ple_proj = w["ple_proj"].transpose(1, 0, 2).reshape(p.shape[1], D)
    w_ple_gate = w["ple_gate"].reshape(D, D)

    loss, dh3, dproj, dgate, dg_ple = _ple_loss(h3, p, target, w_ple_proj, w_ple_gate, gains["ple_post"], "ple_loss")
    gw, gg = {}, {"ple_post": dg_ple}
    gw["ple_proj"] = _mm_tn(p[None], dproj, p.shape[1], "dw_ple_proj")
    gw["ple_gate"] = _mm_tn(h3[None], dgate[None], 512, "dw_ple_gate").reshape(S, D // S, D)

    def ffn_bwd(tag, dh, x_in, xn, g_act, u_act, a_act, f, group):
        dgp, dup, df, gg[tag + "_post"] = _ffn_bwd_act(dh, f, gains[tag + "_post"], w[tag + "_down"], g_act, u_act,
                                                       tag + "_bwd_act")
        gw[tag + "_gate"] = _mm_tn(dgp, xn[None], dgp.shape[2], "dw_" + tag + "_gate")
        gw[tag + "_up"] = _mm_tn(dup, xn[None], dup.shape[2], "dw_" + tag + "_up")
        gw[tag + "_down"] = _mm_tn(a_act, df[None], a_act.shape[2], "dw_" + tag + "_down")
        g_pre = gains[tag + "_pre"]
        if group is not None:
            token = hooks.grads_ready(group, gw)
            g_pre = g_pre if token is None else g_pre + token[0, 0]
        dx, gg[tag + "_pre"] = _proj_bwd([dgp, dup], [w[tag + "_gate"], w[tag + "_up"]], x_in, g_pre, dh,
                                         tag + "_bwd_in")
        return dx

    dh2 = ffn_bwd("ffn2", dh3, h2, xn2, g2, u2, a2, f2, None)
    dmo, do_a, do_b, gg["mix_post"], gg["out_sb"], gg["out_ch"] = _mix_out_bwd(
        dh2, mo, gains["mix_post"], w_out, o_a, o_b, gains["out_sb"], gains["out_ch"], "mix_out_bwd")
    gw["out"] = _mm_tn(mixed[None], dmo[None], 512, "dw_out").reshape(S, D // S, D)
    token = hooks.grads_ready("early", gw)
    if token is not None:
        tot = tot + token[0, 0]
    dq_a, dk_a, dv_a = _sb_bwd(qkv, do_a, tot, "sb_bwd")
    dq_b, dk_b, dv_b, dbias = _ch_bwd(qkv, bias, do_b, "ch_bwd")
    g_rel = _bias_grad(dbias.transpose(1, 0, 2), "bias_grad")
    dqkv = jnp.concatenate([dq_a, dk_a, dv_a, dq_b, dk_b, dv_b], axis=1)
    gw["in"] = _mm_tn(un[None], dqkv, 512, "dw_in", groups=S)
    dh1, gg["mix_pre"] = _proj_bwd([dqkv], [w["in"]], h1, gains["mix_pre"], dh2, "qkv_bwd_in")
    dx = ffn_bwd("ffn1", dh1, x, xn1, g1, u1, a1, f1, "late")
    return loss, dx, gw, gg, g_rel


BIG = ["ffn1_gate", "ffn1_up", "ffn1_down", "in", "out", "ffn2_gate", "ffn2_up", "ffn2_down", "ple_proj", "ple_gate"]
GAINS = ["ffn1_pre", "ffn1_post", "mix_pre", "mix_post", "out_sb", "out_ch", "ffn2_pre", "ffn2_post", "ple_post"]
TRANSPOSED = ("w_ffn1_gate", "w_ffn1_up", "w_ffn2_gate", "w_ffn2_up")
PERMUTED = ("ffn1_gate", "ffn1_up", "ffn1_down", "ffn2_gate", "ffn2_up", "ffn2_down")
W_GROUPS = {"first": ["ffn1_gate", "ffn1_up", "ffn1_down", "in"],
            "rest": ["out", "ffn2_gate", "ffn2_up", "ffn2_down", "ple_proj", "ple_gate"]}
G_GROUPS = {"early": ["ple_proj", "ple_gate", "ffn2_gate", "ffn2_up", "ffn2_down", "out"],
            "late": ["in", "ffn1_gate", "ffn1_up", "ffn1_down"]}
ORDER = ["g_ffn1_pre", "g_ffn1_post", "w_ffn1_gate", "w_ffn1_up", "w_ffn1_down", "g_mix_pre", "g_mix_post", "w_in",
         "g_out_sb", "g_out_ch", "rel_bias", "w_out", "g_ffn2_pre", "g_ffn2_post", "w_ffn2_gate", "w_ffn2_up",
         "w_ffn2_down", "w_ple_proj", "w_ple_gate", "g_ple_post"]


def kernel(x, p, g_ffn1_pre, g_ffn1_post, w_ffn1_gate, w_ffn1_up, w_ffn1_down, g_mix_pre, g_mix_post, w_in, g_out_sb, g_out_ch, rel_bias, w_out, g_ffn2_pre, g_ffn2_post, w_ffn2_gate, w_ffn2_up, w_ffn2_down, w_ple_proj, w_ple_gate, g_ple_post, loss_target, m_g_ffn1_pre, m_g_ffn1_post, m_w_ffn1_gate, m_w_ffn1_up, m_w_ffn1_down, m_g_mix_pre, m_g_mix_post, m_w_in, m_g_out_sb, m_g_out_ch, m_rel_bias, m_w_out, m_g_ffn2_pre, m_g_ffn2_post, m_w_ffn2_gate, m_w_ffn2_up, m_w_ffn2_down, m_w_ple_proj, m_w_ple_gate, m_g_ple_post, v_g_ffn1_pre, v_g_ffn1_post, v_w_ffn1_gate, v_w_ffn1_up, v_w_ffn1_down, v_g_mix_pre, v_g_mix_post, v_w_in, v_g_out_sb, v_g_out_ch, v_rel_bias, v_w_out, v_g_ffn2_pre, v_g_ffn2_post, v_w_ffn2_gate, v_w_ffn2_up, v_w_ffn2_down, v_w_ple_proj, v_w_ple_gate, v_g_ple_post):
    args = dict(locals())
    take = lambda a, n: a[0].T if n in TRANSPOSED else a[0]
    wts = {n: take(args[n], n) for n in ORDER}
    ms = {n: take(args["m_" + n], n) for n in ORDER}
    vs = {n: take(args["v_" + n], n) for n in ORDER}
    gains = {n: wts["g_" + n][None] for n in GAINS}

    c_idx = lax.axis_index("c").astype(jnp.int32).reshape(1)
    me_idx = (2 * lax.axis_index("x") + lax.axis_index("y")).astype(jnp.int32).reshape(1)
    south = lax.axis_index("c") == 0

    lands = dict(zip(BIG, _cast_into_slot0(c_idx, [wts["w_" + n] for n in BIG], "cast_weights")))

    def in_order(names, zones):
        plain = [n for n in names if n not in PERMUTED]
        fixed = dict(zip(plain, _chip_order(me_idx, c_idx, [zones[n] for n in plain], "chip_order_" + plain[0]))
                     ) if plain else {}
        return {n: fixed[n] if n in fixed else _slabs(zones[n]) for n in names}

    names = W_GROUPS["first"]
    first = dict(zip(names, _gather_finish([lands[n] for n in names], True, "gather_first")))
    started = _gather_start([lands[n] for n in W_GROUPS["rest"]], "gather_rest_start", 1, after=first["ffn1_gate"])
    gains["ffn1_pre"] = gains["ffn1_pre"] + started[3][0, 0]

    class Overlapped:
        def __init__(self):
            self.started = {}

        def weights(self, group, after=None):
            names = W_GROUPS[group]
            if group == "first":
                return in_order(names, first)
            send_sem, recv_sem, zones, _ = started
            zones = _gather_wait(send_sem, recv_sem, zones, after, "gather_rest_wait")
            return in_order(names, dict(zip(names, _gather_finish(zones, False, "gather_rest_finish"))))

        def grads_ready(self, group, gw):
            self.started[group] = reduce_start(G_GROUPS[group], gw, group, {"early": 2, "late": 3}[group])
            return self.started[group][-1]

    def reduce_start(names, gw, tag, cid):
        perm = [n in PERMUTED for n in names]
        mine = [gw[n].reshape(N_CHIPS, 2, gw[n].shape[1] // 2, gw[n].shape[2]) for n in names]
        got = _pair_swap(mine, perm, "grad_pair_swap_" + tag)
        partial = _pair_add(c_idx, mine, got, perm, "grad_pair_add_" + tag)
        send_sem, recv_sem, operands, token = _scatter_start(partial, perm, "grad_scatter_start_" + tag, cid)
        return names, perm, send_sem, recv_sem, operands, token

    def reduce_finish(state, after, tag):
        names, perm, send_sem, recv_sem, operands, _ = state
        operands = _scatter_wait(send_sem, recv_sem, operands, perm, after, "grad_scatter_wait_" + tag)
        n = len(names)
        halves = _chip_add(me_idx, operands[:n], operands[n:], perm, "grad_chip_add_" + tag)
        out = {}
        for name, own, other in zip(names, halves, _pair_join(halves, "grad_pair_join_" + tag)):
            out["w_" + name] = jnp.concatenate([jnp.where(south, own, other), jnp.where(south, other, own)], axis=0)
        return out

    hooks = Overlapped()
    loss, dx, gw, gg, g_rel = _local_step(x[0], p[0, 0], loss_target[0], gains, wts["rel_bias"], hooks)

    late = hooks.started["late"]
    grads = reduce_finish(hooks.started["early"], dx, "early")

    pieces = [gg[n].reshape(-1, 128) for n in GAINS] + [jnp.pad(g_rel, ((0, 0), (0, N_REL_PAD - N_REL))).reshape(-1, 128)]
    summed = _all_sum_small(jnp.concatenate(pieces, axis=0), "small_grad_sum")
    at = 0
    for n, piece in zip(GAINS, pieces[:-1]):
        grads["g_" + n] = summed[at:at + piece.shape[0]].reshape(1, -1)[0]
        at += piece.shape[0]
    grads["rel_bias"] = summed[at:].reshape(N_HEADS, N_REL_PAD)[:, :N_REL]

    delta, new_m, new_v = {}, {}, {}

    def adamw_big(group):
        names = ["w_" + n for n in G_GROUPS[group]]
        d, m, v = _adamw([wts[n] for n in names], [grads[n] for n in names], [ms[n] for n in names],
                         [vs[n] for n in names], 8, "adamw_" + group)
        for n, dd, mm, vv in zip(names, d, m, v):
            delta[n], new_m[n], new_v[n] = dd, mm, vv
        return d[0]

    done_early = adamw_big("early")
    grads.update(reduce_finish(late, done_early, "late"))
    adamw_big("late")
    small = ["g_" + n for n in GAINS] + ["rel_bias"]
    as_rows = lambda a: (a.reshape(-1, 128) if a.size % 128 == 0 else jnp.pad(a, ((0, 0), (0, N_REL_PAD - N_REL))).reshape(-1, 128))
    d, m, v = _adamw([as_rows(wts[n]) for n in small], [as_rows(grads[n]) for n in small],
                     [as_rows(ms[n]) for n in small], [as_rows(vs[n]) for n in small], 1, "adamw_small")
    for n, dd, mm, vv in zip(small, d, m, v):
        back = (lambda a: a.reshape(N_HEADS, N_REL_PAD)[:, :N_REL]) if n == "rel_bias" else (lambda a: a.reshape(-1))
        delta[n], new_m[n], new_v[n] = back(dd), back(mm), back(vv)

    loss = lax.psum(loss[0, 0], ("x", "y", "c"))
    outs = [loss, dx[None]]
    for table in (grads, delta, new_m, new_v):
        outs += [(table[n].T if n in TRANSPOSED else table[n])[None] for n in ORDER]
    return tuple(outs)
```

```python
import functools

import jax
import jax.numpy as jnp
from jax import lax
from jax.experimental import pallas as pl
from jax.experimental.pallas import tpu as pltpu

F32 = jnp.float32
BF16 = jnp.bfloat16
EPS = 1e-6
N_CHIPS = 4
HEAD_DIM = 64
N_HEADS = 8
CHUNK = 64
LOOKBACK = 8
BAND = (LOOKBACK + 1) * CHUNK
PAD = LOOKBACK * CHUNK
REL_CLIP = 128
N_REL = 2 * REL_CLIP + 1
N_REL_PAD = 384
SB_BLOCK = 256
PAIR = 2 * HEAD_DIM
ATT_SCALE = HEAD_DIM ** -0.5
NEG_INF = -1e30
ROW_BLOCK = 512
VMEM_LIMIT = 48 * 1024 * 1024
MESH = pl.DeviceIdType.MESH

ADAM_LR = 0.001
ADAM_B1 = 0.9
ADAM_B2 = 0.999
ADAM_EPS = 1e-08
ADAM_WD = 0.01
ADAM_STEP = 10

NT = (((1,), (1,)), ((), ()))
TN = (((0,), (0,)), ((), ()))


def _params(n_grid, vmem=None):
    return pltpu.CompilerParams(dimension_semantics=("arbitrary",) * n_grid, vmem_limit_bytes=vmem)


def _dot(a, b, dims=None):
    if dims is None:
        return jnp.dot(a, b, preferred_element_type=F32)
    return lax.dot_general(a, b, dims, preferred_element_type=F32)


def _sigmoid(x):
    return 1.0 / (1.0 + jnp.exp(-x))


def _rms_fwd(x, g):
    r = lax.rsqrt(jnp.mean(x * x, axis=-1, keepdims=True) + EPS)
    return x * r * g


def _rms_bwd(x, g, dy):
    r = lax.rsqrt(jnp.mean(x * x, axis=-1, keepdims=True) + EPS)
    xh = x * r
    dg = jnp.sum(dy * xh, axis=0, keepdims=True)
    t = dy * g
    dx = r * (t - xh * jnp.mean(t * xh, axis=-1, keepdims=True))
    return dx, dg


def _accumulate(ref, val, first):
    @pl.when(first)
    def _():
        ref[...] = val

    @pl.when(jnp.logical_not(first))
    def _():
        ref[...] += val


def _split2(x):
    hi = x.astype(BF16)
    lo = (x - hi.astype(F32)).astype(BF16)
    return hi, lo


def _ffn_fwd(x, g_pre, g_post, wg, wu, wd, name):
    T, D = x.shape
    S, FS, _ = wg.shape
    tm = min(ROW_BLOCK, T)

    def body(x_ref, gpre_ref, gpost_ref, wg_ref, wu_ref, wd_ref,
             h_ref, xn_ref, g_ref, u_ref, a_ref, f_ref, xn_s, acc_s):
        k = pl.program_id(1)

        @pl.when(k == 0)
        def _():
            xn_s[...] = _rms_fwd(x_ref[...], gpre_ref[...]).astype(BF16)
            xn_ref[...] = xn_s[...]

        xn = xn_s[...]
        g = _dot(xn, wg_ref[0], NT)
        u = _dot(xn, wu_ref[0], NT)
        g_ref[0] = g
        u_ref[0] = u
        a = (g * _sigmoid(g) * u).astype(BF16)
        a_ref[0] = a
        _accumulate(acc_s, _dot(a, wd_ref[0]), k == 0)

        @pl.when(k == S - 1)
        def _():
            f = acc_s[...]
            f_ref[...] = f
            h_ref[...] = x_ref[...] + 0.5 * _rms_fwd(f, gpost_ref[...])

    row = pl.BlockSpec((tm, D), lambda i, k: (i, 0))
    vec = pl.BlockSpec((1, D), lambda i, k: (0, 0))
    act = pl.BlockSpec((1, tm, FS), lambda i, k: (k, i, 0))
    return pl.pallas_call(
        body, name=name, grid=(T // tm, S),
        in_specs=[row, vec, vec] + [pl.BlockSpec((1, FS, D), lambda i, k: (k, 0, 0))] * 3,
        out_specs=[row, row, act, act, act, row],
        out_shape=[jax.ShapeDtypeStruct((T, D), F32), jax.ShapeDtypeStruct((T, D), BF16),
                   jax.ShapeDtypeStruct((S, T, FS), F32), jax.ShapeDtypeStruct((S, T, FS), F32),
                   jax.ShapeDtypeStruct((S, T, FS), BF16), jax.ShapeDtypeStruct((T, D), F32)],
        scratch_shapes=[pltpu.VMEM((tm, D), BF16), pltpu.VMEM((tm, D), F32)],
        compiler_params=_params(2, VMEM_LIMIT),
    )(x, g_pre, g_post, wg, wu, wd)


def _ffn_bwd_act(dh, f, g_post, wd, g_act, u_act, name):
    T, D = dh.shape
    S, FS, _ = wd.shape
    tm = min(ROW_BLOCK, T)

    def body(dh_ref, f_ref, gpost_ref, wd_ref, g_ref, u_ref, dgp_ref, dup_ref, df_ref, dgain_ref, df_s):
        i, k = pl.program_id(0), pl.program_id(1)

        @pl.when(k == 0)
        def _():
            df, dgain = _rms_bwd(f_ref[...], gpost_ref[...], 0.5 * dh_ref[...])
            df_s[...] = df.astype(BF16)
            df_ref[...] = df_s[...]
            _accumulate(dgain_ref, dgain, i == 0)

        da = _dot(df_s[...], wd_ref[0], NT)
        g = g_ref[0]
        s = _sigmoid(g)
        dup_ref[0] = (da * (g * s)).astype(BF16)
        dgp_ref[0] = (da * u_ref[0] * (s * (1.0 + g * (1.0 - s)))).astype(BF16)

    row = pl.BlockSpec((tm, D), lambda i, k: (i, 0))
    vec = pl.BlockSpec((1, D), lambda i, k: (0, 0))
    act = pl.BlockSpec((1, tm, FS), lambda i, k: (k, i, 0))
    return pl.pallas_call(
        body, name=name, grid=(T // tm, S),
        in_specs=[row, row, vec, pl.BlockSpec((1, FS, D), lambda i, k: (k, 0, 0)), act, act],
        out_specs=[act, act, row, vec],
        out_shape=[jax.ShapeDtypeStruct((S, T, FS), BF16), jax.ShapeDtypeStruct((S, T, FS), BF16),
                   jax.ShapeDtypeStruct((T, D), BF16), jax.ShapeDtypeStruct((1, D), F32)],
        scratch_shapes=[pltpu.VMEM((tm, D), BF16)],
        compiler_params=_params(2, VMEM_LIMIT),
    )(dh, f, g_post, wd, g_act, u_act)


def _proj_bwd(dys, ws, x, g_pre, dh, name):
    T, D = x.shape
    n = len(dys)
    flat = dys[0].ndim == 2
    S = ws[0].shape[0]
    N = ws[0].shape[2] if flat else ws[0].shape[1]
    tm = min(ROW_BLOCK, T)

    def body(*refs):
        dy_refs, w_refs = refs[:n], refs[n:2 * n]
        x_ref, gpre_ref, dh_ref, dx_ref, dgain_ref, acc_s = refs[2 * n:]
        i, k = pl.program_id(0), pl.program_id(1)
        part = None
        for dy_ref, w_ref in zip(dy_refs, w_refs):
            term = _dot(dy_ref[...], w_ref[0], NT) if flat else _dot(dy_ref[0], w_ref[0])
            part = term if part is None else part + term
        _accumulate(acc_s, part, k == 0)

        @pl.when(k == S - 1)
        def _():
            dx, dgain = _rms_bwd(x_ref[...], gpre_ref[...], acc_s[...])
            dx_ref[...] = dh_ref[...] + dx
            _accumulate(dgain_ref, dgain, i == 0)

    row = pl.BlockSpec((tm, D), lambda i, k: (i, 0))
    vec = pl.BlockSpec((1, D), lambda i, k: (0, 0))
    return pl.pallas_call(
        body, name=name, grid=(T // tm, S),
        in_specs=[pl.BlockSpec((tm, N), lambda i, k: (i, k)) if flat else pl.BlockSpec((1, tm, N), lambda i, k: (k, i, 0))] * n
        + [pl.BlockSpec((1,) + ws[0].shape[1:], lambda i, k: (k, 0, 0))] * n + [row, vec, row],
        out_specs=[row, vec],
        out_shape=[jax.ShapeDtypeStruct((T, D), F32), jax.ShapeDtypeStruct((1, D), F32)],
        scratch_shapes=[pltpu.VMEM((tm, D), F32)],
        compiler_params=_params(2, VMEM_LIMIT),
    )(*dys, *ws, x, g_pre, dh)


def _mm_tn(a, b, bm, name, groups=None):
    ga, T, M = a.shape
    if groups is None:
        gb, _, N = b.shape
        b_spec = pl.BlockSpec((1, T, N), (lambda g, m: (g, 0, 0)) if gb > 1 else (lambda g, m: (0, 0, 0)))
    else:
        gb, N = groups, b.shape[1] // groups
        b_spec = pl.BlockSpec((T, N), lambda g, m: (0, g))
    G = max(ga, gb)

    def body(a_ref, b_ref, o_ref):
        bv = b_ref[0] if groups is None else b_ref[...]
        o_ref[0] = _dot(a_ref[0].astype(BF16), bv.astype(BF16), TN)

    return pl.pallas_call(
        body, name=name, grid=(G, M // bm),
        in_specs=[pl.BlockSpec((1, T, bm), (lambda g, m: (g, 0, m)) if ga > 1 else (lambda g, m: (0, 0, m))), b_spec],
        out_specs=pl.BlockSpec((1, bm, N), lambda g, m: (g, m, 0)),
        out_shape=jax.ShapeDtypeStruct((G, M, N), F32),
        compiler_params=_params(2, VMEM_LIMIT),
    )(a, b)


def _norm_proj(x, g_pre, w, name):
    T, D = x.shape
    S, _, N = w.shape
    tm = min(ROW_BLOCK, T)

    def body(x_ref, g_ref, w_ref, o_ref, xn_ref, xn_s):
        @pl.when(pl.program_id(1) == 0)
        def _():
            xn_s[...] = _rms_fwd(x_ref[...], g_ref[...]).astype(BF16)
            xn_ref[...] = xn_s[...]

        o_ref[...] = _dot(xn_s[...], w_ref[0]).astype(BF16)

    row = pl.BlockSpec((tm, D), lambda i, k: (i, 0))
    return pl.pallas_call(
        body, name=name, grid=(T // tm, S),
        in_specs=[row, pl.BlockSpec((1, D), lambda i, k: (0, 0)), pl.BlockSpec((1, D, N), lambda i, k: (k, 0, 0))],
        out_specs=[pl.BlockSpec((tm, N), lambda i, k: (i, k)), row],
        out_shape=[jax.ShapeDtypeStruct((T, S * N), BF16), jax.ShapeDtypeStruct((T, D), BF16)],
        scratch_shapes=[pltpu.VMEM((tm, D), BF16)],
        compiler_params=_params(2, VMEM_LIMIT),
    )(x, g_pre, w)


def _mix_out_fwd(h, o_a, o_b, g_sb, g_ch, w_out, g_post, name):
    T, D = h.shape
    W = g_sb.shape[1]
    tm = min(ROW_BLOCK, T)

    def body(h_ref, oa_ref, ob_ref, gsb_ref, gch_ref, w_ref, gpost_ref, h2_ref, mixed_ref, mo_ref):
        mixed_ref[:, :W] = _rms_fwd(oa_ref[...], gsb_ref[...]).astype(BF16)
        mixed_ref[:, W:] = _rms_fwd(ob_ref[...], gch_ref[...]).astype(BF16)
        mo = _dot(mixed_ref[...], w_ref[...])
        mo_ref[...] = mo
        h2_ref[...] = h_ref[...] + _rms_fwd(mo, gpost_ref[...])

    row = pl.BlockSpec((tm, D), lambda i: (i, 0))
    part = pl.BlockSpec((tm, W), lambda i: (i, 0))
    half = pl.BlockSpec((1, W), lambda i: (0, 0))
    return pl.pallas_call(
        body, name=name, grid=(T // tm,),
        in_specs=[row, part, part, half, half, pl.BlockSpec((D, D), lambda i: (0, 0)), pl.BlockSpec((1, D), lambda i: (0, 0))],
        out_specs=[row, row, row],
        out_shape=[jax.ShapeDtypeStruct((T, D), F32), jax.ShapeDtypeStruct((T, D), BF16),
                   jax.ShapeDtypeStruct((T, D), F32)],
        compiler_params=_params(1, VMEM_LIMIT),
    )(h, o_a, o_b, g_sb, g_ch, w_out, g_post)


def _mix_out_bwd(dh, mo, g_post, w_out, o_a, o_b, g_sb, g_ch, name):
    T, D = dh.shape
    W = g_sb.shape[1]
    tm = min(ROW_BLOCK, T)

    def body(dh_ref, mo_ref, gpost_ref, w_ref, oa_ref, ob_ref, gsb_ref, gch_ref,
             dmo_ref, doa_ref, dob_ref, dgpost_ref, dgsb_ref, dgch_ref):
        first = pl.program_id(0) == 0
        dmo, dgpost = _rms_bwd(mo_ref[...], gpost_ref[...], dh_ref[...])
        dmo_ref[...] = dmo.astype(BF16)
        dmix = _dot(dmo_ref[...], w_ref[...], NT)
        doa_ref[...], dgsb = _rms_bwd(oa_ref[...], gsb_ref[...], dmix[:, :W])
        dob_ref[...], dgch = _rms_bwd(ob_ref[...], gch_ref[...], dmix[:, W:])
        _accumulate(dgpost_ref, dgpost, first)
        _accumulate(dgsb_ref, dgsb, first)
        _accumulate(dgch_ref, dgch, first)

    row = pl.BlockSpec((tm, D), lambda i: (i, 0))
    part = pl.BlockSpec((tm, W), lambda i: (i, 0))
    vec = pl.BlockSpec((1, D), lambda i: (0, 0))
    half = pl.BlockSpec((1, W), lambda i: (0, 0))
    return pl.pallas_call(
        body, name=name, grid=(T // tm,),
        in_specs=[row, row, vec, pl.BlockSpec((D, D), lambda i: (0, 0)), part, part, half, half],
        out_specs=[row, part, part, vec, half, half],
        out_shape=[jax.ShapeDtypeStruct((T, D), BF16), jax.ShapeDtypeStruct((T, W), F32),
                   jax.ShapeDtypeStruct((T, W), F32), jax.ShapeDtypeStruct((1, D), F32),
                   jax.ShapeDtypeStruct((1, W), F32), jax.ShapeDtypeStruct((1, W), F32)],
        compiler_params=_params(1, VMEM_LIMIT),
    )(dh, mo, g_post, w_out, o_a, o_b, g_sb, g_ch)


def _ple_loss(h, p, target, w_proj, w_gate, g_post, name):
    T, D = h.shape
    P = p.shape[1]
    S = N_CHIPS
    C = D // S
    tm = min(ROW_BLOCK, T)

    def body(h_ref, p_ref, t_ref, wp_ref, wg_ref, g_ref, loss_ref, dh_ref, dproj_ref, dgate_ref, dgain_ref):
        first = pl.program_id(0) == 0
        h3 = h_ref[...]
        proj = _dot(p_ref[...].astype(BF16), wp_ref[...])
        s = _sigmoid(_dot(h3.astype(BF16), wg_ref[...]))
        e = proj * s
        diff = h3 + _rms_fwd(e, g_ref[...]) - t_ref[...]
        part = 0.5 * jnp.sum(jnp.mean(diff * diff, axis=-1, keepdims=True), axis=0, keepdims=True)
        _accumulate(loss_ref, jnp.broadcast_to(part, loss_ref.shape), first)
        dy = diff * (1.0 / D)
        de, dgain = _rms_bwd(e, g_ref[...], dy)
        _accumulate(dgain_ref, dgain, first)
        dproj = (de * s).astype(BF16)
        for j in range(S):
            dproj_ref[j] = dproj[:, j * C:(j + 1) * C]
        dgate_ref[...] = (de * proj * s * (1.0 - s)).astype(BF16)
        dh_ref[...] = dy + _dot(dgate_ref[...], wg_ref[...], NT)

    row = pl.BlockSpec((tm, D), lambda i: (i, 0))
    vec = pl.BlockSpec((1, D), lambda i: (0, 0))
    return pl.pallas_call(
        body, name=name, grid=(T // tm,),
        in_specs=[row, pl.BlockSpec((tm, P), lambda i: (i, 0)), row,
                  pl.BlockSpec((P, D), lambda i: (0, 0)), pl.BlockSpec((D, D), lambda i: (0, 0)), vec],
        out_specs=[pl.BlockSpec((8, 128), lambda i: (0, 0)), row,
                   pl.BlockSpec((S, tm, C), lambda i: (0, i, 0)), row, vec],
        out_shape=[jax.ShapeDtypeStruct((8, 128), F32), jax.ShapeDtypeStruct((T, D), F32),
                   jax.ShapeDtypeStruct((S, T, C), BF16), jax.ShapeDtypeStruct((T, D), BF16),
                   jax.ShapeDtypeStruct((1, D), F32)],
        compiler_params=_params(1, VMEM_LIMIT),
    )(h, p, target, w_proj, w_gate, g_post)


def _sb_scores(q, kj, mask):
    z = _dot(q, kj, NT)
    sp = jnp.maximum(z, 0.0) + jnp.log(1.0 + jnp.exp(-jnp.abs(z)))
    return z, sp if mask is None else jnp.where(mask, sp, 0.0)


def _strict_causal():
    rows = lax.broadcasted_iota(jnp.int32, (SB_BLOCK, SB_BLOCK), 0)
    cols = lax.broadcasted_iota(jnp.int32, (SB_BLOCK, SB_BLOCK), 1)
    return cols < rows


def _tri(cmp):
    r = lax.broadcasted_iota(jnp.int32, (2 * SB_BLOCK, SB_BLOCK), 0) % SB_BLOCK
    c = lax.broadcasted_iota(jnp.int32, (2 * SB_BLOCK, SB_BLOCK), 1)
    return jnp.where(cmp(r, c), 1.0, 0.0).astype(BF16)


def _cum(x, tri):
    return _dot(jnp.concatenate(_split2(x), axis=1), tri)


def _pair_lanes():
    lane = lax.broadcasted_iota(jnp.int32, (1, PAIR), 1)
    return [lane < HEAD_DIM, lane >= HEAD_DIM]


def _only(lanes, x):
    return jnp.where(lanes, x, jnp.zeros_like(x))


def _sb_fwd(qkv, name):
    T = qkv.shape[0]
    B = SB_BLOCK
    pairs = N_HEADS // 2

    def body(q_ref, k_ref, v_ref, o_ref, tot_ref):
        i = pl.program_id(1)
        after = _tri(lambda r, c: r > c)
        lanes = _pair_lanes()
        q = [_only(lanes[h], q_ref[...] * ATT_SCALE) for h in range(2)]

        def tile(h, j, carry, mask):
            run, acc = carry
            at = pl.ds(pl.multiple_of(j * B, B), B)
            z, sp = _sb_scores(q[h], k_ref[at, :], mask)
            later = _cum(sp, after)
            a = jnp.exp(z - sp - later - run)
            if mask is not None:
                a = jnp.where(mask, a, 0.0)
            return (run + later[:, 0:1] + sp[:, 0:1],
                    acc + _dot(a.astype(BF16), _only(lanes[h], v_ref[at, :])))

        zero = (jnp.zeros((B, 1), F32), jnp.zeros((B, PAIR), F32))
        diag = _strict_causal()
        carries = tuple(tile(h, i, zero, diag) for h in range(2))
        carries = lax.fori_loop(
            0, i, lambda jj, cs: tuple(tile(h, i - 1 - jj, cs[h], None) for h in range(2)), carries)
        o_ref[...] = carries[0][1] + carries[1][1]
        tot_ref[...] = jnp.where(lanes[0], carries[0][0], carries[1][0])

    blk = lambda off: pl.BlockSpec((B, PAIR), lambda g, i: (i, g + off))
    full = lambda off: pl.BlockSpec((T, PAIR), lambda g, i: (0, g + off))
    out = jax.ShapeDtypeStruct((T, pairs * PAIR), F32)
    return pl.pallas_call(
        body, name=name, grid=(pairs, T // B),
        in_specs=[blk(0), full(pairs), full(2 * pairs)],
        out_specs=[blk(0), blk(0)],
        out_shape=[out, out],
        compiler_params=_params(2, VMEM_LIMIT),
    )(qkv, qkv, qkv)


def _sb_bwd(qkv, do, tot, name):
    T = qkv.shape[0]
    B = SB_BLOCK
    pairs = N_HEADS // 2
    n_blocks = T // B

    def body(q_ref, k_ref, v_ref, do_ref, tot_ref, dq_ref, dk_ref, dv_ref, dk_s, dv_s):
        i = pl.program_id(1)

        @pl.when(i == 0)
        def _():
            dk_s[...] = jnp.zeros_like(dk_s)
            dv_s[...] = jnp.zeros_like(dv_s)

        upto = _tri(lambda r, c: r <= c)
        below = _tri(lambda r, c: r < c)
        lanes = _pair_lanes()
        q = [_only(lanes[h], q_ref[...] * ATT_SCALE) for h in range(2)]
        dob = do_ref[...].astype(BF16)
        do = [_only(lanes[h], dob) for h in range(2)]
        tot = [tot_ref[:, 0:1], tot_ref[:, HEAD_DIM:HEAD_DIM + 1]]

        def tile(h, j, carry, mask):
            pre_sp, pre_g, dq = carry
            at = pl.ds(pl.multiple_of(j * B, B), B)
            kj, vj = k_ref[at, :], v_ref[at, :]
            z, sp = _sb_scores(q[h], kj, mask)
            through = _cum(sp, upto)
            a = jnp.exp(z - sp + through - (tot[h] - pre_sp))
            if mask is not None:
                a = jnp.where(mask, a, 0.0)
            g = a * _dot(do[h], vj, NT)
            before = _cum(g, below)
            g_before = pre_g + before
            fail = jnp.exp(-sp)
            dz = fail * (g + g_before) - g_before
            if mask is not None:
                dz = jnp.where(mask, dz, 0.0)
            dzb = dz.astype(BF16)
            dk_s[at, :] += _dot(dzb, q[h], TN)
            dv_s[at, :] += _dot(a.astype(BF16), do[h], TN)
            return (pre_sp + through[:, B - 1:B], pre_g + before[:, B - 1:B] + g[:, B - 1:B],
                    dq + _dot(dzb, _only(lanes[h], kj)))

        col = jnp.zeros((B, 1), F32)
        zero = (col, col, jnp.zeros((B, PAIR), F32))
        carries = lax.fori_loop(
            0, i, lambda j, cs: tuple(tile(h, j, cs[h], None) for h in range(2)), (zero, zero))
        diag = _strict_causal()
        dq = tile(0, i, carries[0], diag)[2] + tile(1, i, carries[1], diag)[2]
        dq_ref[...] = (dq * ATT_SCALE).astype(BF16)

        @pl.when(i == n_blocks - 1)
        def _():
            dk_ref[...] = dk_s[...].astype(BF16)
            dv_ref[...] = dv_s[...].astype(BF16)

    blk = lambda off: pl.BlockSpec((B, PAIR), lambda g, i: (i, g + off))
    full = lambda off: pl.BlockSpec((T, PAIR), lambda g, i: (0, g + off))
    out = jax.ShapeDtypeStruct((T, pairs * PAIR), BF16)
    return pl.pallas_call(
        body, name=name, grid=(pairs, n_blocks),
        in_specs=[blk(0), full(pairs), full(2 * pairs), blk(0), blk(0)],
        out_specs=[blk(0), full(0), full(0)],
        out_shape=[out, out, out],
        scratch_shapes=[pltpu.VMEM((T, PAIR), F32)] * 2,
        compiler_params=_params(2, VMEM_LIMIT),
    )(qkv, qkv, qkv, do, tot)


NEAR = BAND - PAD + REL_CLIP
FAR = BAND - NEAR
NEAR_REL = 2 * REL_CLIP
BIAS_ROWS = 8


def _rel_onehot(i, transposed):
    shape = (NEAR, NEAR_REL) if transposed else (NEAR_REL, NEAR)
    j = FAR + lax.broadcasted_iota(jnp.int32, shape, 0 if transposed else 1)
    r = lax.broadcasted_iota(jnp.int32, shape, 1 if transposed else 0)
    idx = jnp.clip(i + PAD - j, -REL_CLIP, REL_CLIP) + REL_CLIP
    return jnp.where(idx - 1 == r, 1.0, 0.0).astype(BF16)


def _bias_table(rel_bias, name):
    def body(near_ref, far_ref, o_ref):
        rb = near_ref[...]
        hi, lo = _split2(rb)
        lo2 = (rb - hi.astype(F32) - lo.astype(F32)).astype(BF16)
        far = jnp.broadcast_to(far_ref[...], (N_HEADS, FAR))
        for k in range(BIAS_ROWS):
            onehot = _rel_onehot(pl.program_id(0) * BIAS_ROWS + k, False)
            o_ref[k, :, :FAR] = far
            o_ref[k, :, FAR:] = _dot(hi, onehot) + _dot(lo, onehot) + _dot(lo2, onehot)

    return pl.pallas_call(
        body, name=name, grid=(CHUNK // BIAS_ROWS,),
        in_specs=[pl.BlockSpec((N_HEADS, NEAR_REL), lambda i: (0, 0)), pl.BlockSpec((N_HEADS, 1), lambda i: (0, 0))],
        out_specs=pl.BlockSpec((BIAS_ROWS, N_HEADS, BAND), lambda i: (i, 0, 0)),
        out_shape=jax.ShapeDtypeStruct((CHUNK, N_HEADS, BAND), F32),
        compiler_params=_params(1),
    )(rel_bias[:, 1:], rel_bias[:, N_REL - 1:])


def _bias_grad(dbias_t, name):
    def body(d_ref, near_ref, far_ref):
        near, far = None, None
        for k in range(BIAS_ROWS):
            onehot = _rel_onehot(pl.program_id(0) * BIAS_ROWS + k, True)
            hi, lo = _split2(d_ref[k, :, FAR:])
            part = _dot(hi, onehot) + _dot(lo, onehot)
            rest = jnp.sum(d_ref[k, :, :FAR], axis=1, keepdims=True)
            near, far = (part, rest) if near is None else (near + part, far + rest)
        first = pl.program_id(0) == 0
        _accumulate(near_ref, near, first)
        _accumulate(far_ref, jnp.broadcast_to(far, far_ref.shape), first)

    near, far = pl.pallas_call(
        body, name=name, grid=(CHUNK // BIAS_ROWS,),
        in_specs=[pl.BlockSpec((BIAS_ROWS, N_HEADS, BAND), lambda i: (i, 0, 0))],
        out_specs=[pl.BlockSpec((N_HEADS, NEAR_REL), lambda i: (0, 0)), pl.BlockSpec((N_HEADS, 128), lambda i: (0, 0))],
        out_shape=[jax.ShapeDtypeStruct((N_HEADS, NEAR_REL), F32), jax.ShapeDtypeStruct((N_HEADS, 128), F32)],
        compiler_params=_params(1),
    )(dbias_t)
    return jnp.pad(near, ((0, 0), (1, 0))).at[:, N_REL - 1].add(far[:, 0])


def _ch_probs(q, kw, bias, valid):
    z = jnp.where(valid, _dot(q, kw, NT) * ATT_SCALE + bias, NEG_INF)
    e = jnp.exp(z - jnp.max(z, axis=-1, keepdims=True))
    return e / jnp.sum(e, axis=-1, keepdims=True)


def _ch_valid(n):
    slot = lax.broadcasted_iota(jnp.int32, (CHUNK, BAND), 1) // CHUNK
    return n + slot - LOOKBACK >= 0


def _ch_fwd(qkv, bias, name):
    T = qkv.shape[0]
    W = N_HEADS * HEAD_DIM

    def body(q_ref, k_ref, v_ref, b_ref, o_ref, kp, vp):
        n = pl.program_id(0)

        @pl.when(n == 0)
        def _():
            _ch_load_padded(k_ref, v_ref, kp, vp)

        win = pl.ds(pl.multiple_of(n * CHUNK, CHUNK), BAND)
        valid = _ch_valid(n)
        lanes = _pair_lanes()
        for pair in range(N_HEADS // 2):
            cols = slice(pair * PAIR, (pair + 1) * PAIR)
            q, kw, vw = q_ref[:, cols], kp[win, cols], vp[win, cols]
            o = None
            for h in range(2):
                p = _ch_probs(_only(lanes[h], q), kw, b_ref[2 * pair + h], valid)
                part = _dot(p.astype(BF16), _only(lanes[h], vw))
                o = part if o is None else o + part
            o_ref[:, cols] = o

    full = lambda col: pl.BlockSpec((T, W), lambda n: (0, col))
    return pl.pallas_call(
        body, name=name, grid=(T // CHUNK,),
        in_specs=[pl.BlockSpec((CHUNK, W), lambda n: (n, 3)), full(4), full(5),
                  pl.BlockSpec((N_HEADS, CHUNK, BAND), lambda n: (0, 0, 0))],
        out_specs=pl.BlockSpec((CHUNK, W), lambda n: (n, 0)),
        out_shape=jax.ShapeDtypeStruct((T, W), F32),
        scratch_shapes=[pltpu.VMEM((PAD + T, W), BF16)] * 2,
        compiler_params=_params(1, VMEM_LIMIT),
    )(qkv, qkv, qkv, bias)


def _ch_load_padded(k_ref, v_ref, kp, vp):
    for src, dst in ((k_ref, kp), (v_ref, vp)):
        dst[:PAD, :] = jnp.zeros((PAD, dst.shape[1]), dst.dtype)
        dst[PAD:, :] = src[...]


def _ch_bwd(qkv, bias, do, name):
    T = qkv.shape[0]
    W = N_HEADS * HEAD_DIM
    n_chunks = T // CHUNK

    def body(q_ref, k_ref, v_ref, b_ref, do_ref, dq_ref, dk_ref, dv_ref, db_ref, kp, vp, dk_s, dv_s):
        n = pl.program_id(0)

        @pl.when(n == 0)
        def _():
            _ch_load_padded(k_ref, v_ref, kp, vp)
            dk_s[...] = jnp.zeros_like(dk_s)
            dv_s[...] = jnp.zeros_like(dv_s)
            db_ref[...] = jnp.zeros_like(db_ref)

        win = pl.ds(pl.multiple_of(n * CHUNK, CHUNK), BAND)
        valid = _ch_valid(n)
        lanes = _pair_lanes()
        for pair in range(N_HEADS // 2):
            cols = slice(pair * PAIR, (pair + 1) * PAIR)
            q, kw, vw = q_ref[:, cols], kp[win, cols], vp[win, cols]
            dob = do_ref[:, cols].astype(BF16)
            dq = dk = dv = None
            for h in range(2):
                qh, doh = _only(lanes[h], q), _only(lanes[h], dob)
                p = _ch_probs(qh, kw, b_ref[2 * pair + h], valid)
                dp = _dot(doh, vw, NT)
                dz = p * (dp - jnp.sum(dp * p, axis=-1, keepdims=True))
                db_ref[2 * pair + h] += dz
                dzb = (dz * ATT_SCALE).astype(BF16)
                parts = (_dot(dzb, _only(lanes[h], kw)), _dot(dzb, qh, TN), _dot(p.astype(BF16), doh, TN))
                dq, dk, dv = parts if dq is None else (dq + parts[0], dk + parts[1], dv + parts[2])
            dq_ref[:, cols] = dq.astype(BF16)
            dk_s[win, cols] += dk
            dv_s[win, cols] += dv

        @pl.when(n == n_chunks - 1)
        def _():
            dk_ref[...] = dk_s[PAD:, :].astype(BF16)
            dv_ref[...] = dv_s[PAD:, :].astype(BF16)

    full = lambda col: pl.BlockSpec((T, W), lambda n: (0, col))
    blk = lambda col: pl.BlockSpec((CHUNK, W), lambda n: (n, col))
    tab = pl.BlockSpec((N_HEADS, CHUNK, BAND), lambda n: (0, 0, 0))
    out = jax.ShapeDtypeStruct((T, W), BF16)
    return pl.pallas_call(
        body, name=name, grid=(n_chunks,),
        in_specs=[blk(3), full(4), full(5), tab, blk(0)],
        out_specs=[blk(0), full(0), full(0), tab],
        out_shape=[out, out, out, jax.ShapeDtypeStruct((N_HEADS, CHUNK, BAND), F32)],
        scratch_shapes=[pltpu.VMEM((PAD + T, W), BF16)] * 2 + [pltpu.VMEM((PAD + T, W), F32)] * 2,
        compiler_params=_params(1, VMEM_LIMIT),
    )(qkv, qkv, qkv, bias, do)


def _rows_split(a, parts):
    return a.reshape(a.shape[:-2] + (parts, a.shape[-2] // parts, a.shape[-1]))


def _cast_into_slot0(c, ws, name):
    parts = 2
    ws = [_rows_split(_rows_split(w, 2), parts) for w in ws]
    n = len(ws)

    def body(c_ref, *refs):
        for src, dst in zip(refs[:n], refs[n:]):
            dst[0, 0, 0] = src[0, 0].astype(BF16)

    outs = pl.pallas_call(
        body, name=name,
        grid_spec=pltpu.PrefetchScalarGridSpec(
            num_scalar_prefetch=1, grid=(2, parts),
            in_specs=[pl.BlockSpec((1, 1) + w.shape[2:], lambda d, r, c_ref: (d ^ c_ref[0], r, 0, 0)) for w in ws],
            out_specs=[pl.BlockSpec((1, 1, 1) + w.shape[2:], lambda d, r, c_ref: (0, d, r, 0, 0)) for w in ws]),
        out_shape=[jax.ShapeDtypeStruct((N_CHIPS,) + w.shape, BF16) for w in ws],
        compiler_params=_params(2, VMEM_LIMIT),
    )(c, *ws)
    return [o.reshape(N_CHIPS, 2, o.shape[2] * o.shape[3], o.shape[4]) for o in outs]


def _chip_order(me, c, lands, name):
    parts = 2
    xs = [_rows_split(x, parts) for x in lands]

    def body(me_ref, c_ref, *refs):
        n = len(refs) // 2
        for src, dst in zip(refs[:n], refs[n:]):
            dst[...] = src[...]

    outs = pl.pallas_call(
        body, name=name,
        grid_spec=pltpu.PrefetchScalarGridSpec(
            num_scalar_prefetch=2, grid=(N_CHIPS, 2, parts),
            in_specs=[pl.BlockSpec((1, 1, 1) + x.shape[3:],
                                   lambda j, h, r, me_ref, c_ref: (j ^ me_ref[0], h ^ c_ref[0], r, 0, 0)) for x in xs],
            out_specs=[pl.BlockSpec((1, 1, 1) + x.shape[3:], lambda j, h, r, me_ref, c_ref: (j, h, r, 0, 0))
                       for x in xs]),
        out_shape=[jax.ShapeDtypeStruct(x.shape, x.dtype) for x in xs],
        compiler_params=_params(3, VMEM_LIMIT),
    )(me, c, *xs)
    return [o.reshape(o.shape[0], 2 * parts * o.shape[3], o.shape[4]) for o in outs]


def _pair_add(c, mine, got, permuted, name):
    parts = 2
    mine = [_rows_split(m, parts) for m in mine]
    got = [_rows_split(g, parts) for g in got]
    n = len(mine)

    def body(c_ref, *refs):
        for a, b, o in zip(refs[:n], refs[n:2 * n], refs[2 * n:]):
            o[0, 0] = (a[0, 0, 0] + b[0, 0]).astype(BF16)

    def mine_spec(m, perm):
        if perm:
            return pl.BlockSpec((1, 1, 1) + m.shape[3:], lambda j, r, c_ref: (j, 0, r, 0, 0))
        return pl.BlockSpec((1, 1, 1) + m.shape[3:], lambda j, r, c_ref: (j, c_ref[0], r, 0, 0))

    def got_spec(g):
        return pl.BlockSpec((1, 1) + g.shape[2:], lambda j, r, c_ref: (j, r, 0, 0))

    outs = pl.pallas_call(
        body, name=name,
        grid_spec=pltpu.PrefetchScalarGridSpec(
            num_scalar_prefetch=1, grid=(N_CHIPS, parts),
            in_specs=[mine_spec(m, perm) for m, perm in zip(mine, permuted)] + [got_spec(g) for g in got],
            out_specs=[got_spec(g) for g in got]),
        out_shape=[jax.ShapeDtypeStruct(g.shape, BF16) for g in got],
        compiler_params=_params(2, VMEM_LIMIT),
    )(c, *mine, *got)
    return [o.reshape(o.shape[0], o.shape[1] * o.shape[2], o.shape[3]) for o in outs]


def _chip_add(me, partials, landed, permuted, name):
    parts = 2
    ps = [_rows_split(x, parts) for x in partials]
    ls = [_rows_split(x, parts) for x in landed]
    n = len(ps)

    def body(me_ref, *refs):
        for own, got, o in zip(refs[:n], refs[n:2 * n], refs[2 * n:]):
            acc = own[0, 0].astype(F32)
            for r in range(N_CHIPS - 1):
                acc = acc + got[r, 0].astype(F32)
            o[0] = acc

    def own_spec(x, perm):
        if perm:
            return pl.BlockSpec((1, 1) + x.shape[2:], lambda r, me_ref: (0, r, 0, 0))
        return pl.BlockSpec((1, 1) + x.shape[2:], lambda r, me_ref: (me_ref[0], r, 0, 0))

    outs = pl.pallas_call(
        body, name=name,
        grid_spec=pltpu.PrefetchScalarGridSpec(
            num_scalar_prefetch=1, grid=(parts,),
            in_specs=[own_spec(x, perm) for x, perm in zip(ps, permuted)]
            + [pl.BlockSpec((N_CHIPS - 1, 1) + x.shape[2:], lambda r, me_ref: (0, r, 0, 0)) for x in ls],
            out_specs=[pl.BlockSpec((1,) + x.shape[2:], lambda r, me_ref: (r, 0, 0)) for x in ps]),
        out_shape=[jax.ShapeDtypeStruct(x.shape[1:], F32) for x in ps],
        compiler_params=_params(1, VMEM_LIMIT),
    )(me, *ps, *ls)
    return [o.reshape(o.shape[0] * o.shape[1], o.shape[2]) for o in outs]


def _adamw_math(w, g, m, v):
    m = ADAM_B1 * m + (1.0 - ADAM_B1) * g
    v = ADAM_B2 * v + (1.0 - ADAM_B2) * (g * g)
    m_hat = m / (1.0 - ADAM_B1 ** ADAM_STEP)
    v_hat = v / (1.0 - ADAM_B2 ** ADAM_STEP)
    delta = -ADAM_LR * (m_hat / (jnp.sqrt(v_hat) + ADAM_EPS) + ADAM_WD * w)
    return delta, m, v


def _adamw(ws, gs, ms, vs, parts, name):
    n = len(ws)
    flat = [_rows_split(a, parts) for a in (*ws, *gs, *ms, *vs)]

    def body(*refs):
        ins, outs = refs[:4 * n], refs[4 * n:]
        for k in range(n):
            d, m, v = _adamw_math(ins[k][...], ins[n + k][...], ins[2 * n + k][...], ins[3 * n + k][...])
            outs[k][...] = d
            outs[n + k][...] = m
            outs[2 * n + k][...] = v

    spec = lambda a: pl.BlockSpec((1,) + a.shape[1:], lambda i: (i, 0, 0))
    outs = pl.pallas_call(
        body, name=name, grid=(parts,),
        in_specs=[spec(a) for a in flat], out_specs=[spec(a) for a in flat[:n]] * 3,
        out_shape=[jax.ShapeDtypeStruct(a.shape, F32) for a in flat[:n]] * 3,
        compiler_params=_params(1, VMEM_LIMIT),
    )(*flat)
    outs = [o.reshape(o.shape[0] * o.shape[1], o.shape[2]) for o in outs]
    return outs[:n], outs[n:2 * n], outs[2 * n:]


def _place():
    x, y, c = lax.axis_index("x"), lax.axis_index("y"), lax.axis_index("c")
    peers = [(x ^ (r >> 1), y ^ (r & 1), c) for r in (1, 2, 3)]
    return x, y, c, peers


def _handshake(peers):
    barrier = pltpu.get_barrier_semaphore()
    for peer in peers:
        pl.semaphore_signal(barrier, inc=1, device_id=peer, device_id_type=MESH)
    pl.semaphore_wait(barrier, len(peers))


ANY = pl.BlockSpec(memory_space=pl.ANY)
HBM = pl.BlockSpec(memory_space=pltpu.HBM)
SEM = pl.BlockSpec(memory_space=pltpu.SEMAPHORE)
SPLIT_COPY = pltpu.SideEffectType.DATAFLOW_SIDE_EFFECTING


def _in_hbm(a):
    return pltpu.with_memory_space_constraint(a, pltpu.HBM)


def _split_start(body, name, collective_id, operands, n_sems, after=None):
    n = len(operands)
    extra = [] if after is None else [after]

    def wrapped(*refs):
        at = n + len(extra)
        body(refs[:n], refs[at], refs[at + 1])
        token = refs[-1]
        token[...] = jnp.zeros_like(token)

    outs = pl.pallas_call(
        wrapped, name=name,
        in_specs=[HBM] * n + [ANY] * len(extra),
        out_shape=(pltpu.SemaphoreType.DMA((n_sems,)), pltpu.SemaphoreType.DMA((n_sems,)),
                   *[pltpu.HBM(a.shape, a.dtype) for a in operands], jax.ShapeDtypeStruct((8, 128), F32)),
        out_specs=(SEM, SEM, *[HBM] * n, pl.BlockSpec(memory_space=pltpu.VMEM)),
        input_output_aliases={i: 2 + i for i in range(n)},
        compiler_params=pltpu.CompilerParams(has_side_effects=SPLIT_COPY, collective_id=collective_id),
    )(*[_in_hbm(a) for a in operands], *extra)
    return outs[0], outs[1], list(outs[2:2 + n]), outs[-1]


def _split_wait(body, name, send_sem, recv_sem, operands, after):
    n = len(operands)

    def wrapped(*refs):
        body(refs[:n], refs[n], refs[n + 1])

    outs = pl.pallas_call(
        wrapped, name=name,
        in_specs=[HBM] * n + [SEM, SEM, ANY],
        out_shape=tuple(pltpu.HBM(a.shape, a.dtype) for a in operands),
        out_specs=tuple([HBM] * n),
        input_output_aliases={i: i for i in range(n)},
        compiler_params=pltpu.CompilerParams(has_side_effects=SPLIT_COPY),
    )(*operands, send_sem, recv_sem, after)
    return list(outs)


def _gather_copies(lands, send_sem, recv_sem):
    peers = _place()[3]
    return [pltpu.make_async_remote_copy(
        src_ref=land.at[0, 0], dst_ref=land.at[r + 1, 0],
        send_sem=send_sem.at[a * 3 + r], recv_sem=recv_sem.at[a * 3 + r],
        device_id=peers[r], device_id_type=MESH) for a, land in enumerate(lands) for r in range(3)]


def _gather_start(lands, name, collective_id, after):
    def body(refs, send_sem, recv_sem):
        _handshake(_place()[3])
        for cp in _gather_copies(refs, send_sem, recv_sem):
            cp.start()

    return _split_start(body, name, collective_id, list(lands), 3 * len(lands), after)


def _gather_wait(send_sem, recv_sem, operands, after, name):
    def body(refs, send_sem, recv_sem):
        for cp in _gather_copies(refs, send_sem, recv_sem):
            cp.wait_send()
            cp.wait_recv()

    return _split_wait(body, name, send_sem, recv_sem, operands, after)


def _gather_finish(lands, with_ici, name):
    n = len(lands)

    def body(*refs):
        land = refs[n:2 * n]
        send_ici, recv_ici, send_d2d, recv_d2d = refs[2 * n:]
        x, y, c, _ = _place()
        ici = _gather_copies(land, send_ici, recv_ici) if with_ici else []
        for cp in ici:
            cp.start()
        passed = [pltpu.make_async_remote_copy(
            src_ref=land[a].at[r + 1, 0], dst_ref=land[a].at[r + 1, 1],
            send_sem=send_d2d.at[a * 3 + r], recv_sem=recv_d2d.at[a * 3 + r],
            device_id=(x, y, 1 - c), device_id_type=MESH) for a in range(n) for r in range(3)]
        for k, cp in enumerate(passed):
            if with_ici:
                ici[k].wait_recv()
            cp.start()
        for cp in passed:
            cp.wait_recv()
        for cp in ici:
            cp.wait_send()
        for cp in passed:
            cp.wait_send()

    outs = pl.pallas_call(
        body, name=name,
        in_specs=[ANY] * n, out_specs=[ANY] * n,
        out_shape=[jax.ShapeDtypeStruct(l.shape, l.dtype) for l in lands],
        input_output_aliases={a: a for a in range(n)},
        scratch_shapes=[pltpu.SemaphoreType.DMA((3 * n,))] * 4,
    )(*lands)
    return list(outs)


def _slabs(land):
    return land.reshape(N_CHIPS, 2 * land.shape[2], land.shape[3])


def _pair_swap(grads, permuted, name):
    n = len(grads)

    def body(*refs):
        src, dst = refs[:n], refs[n:2 * n]
        send_sem, recv_sem = refs[2 * n:]
        x, y, c, _ = _place()
        copies = [pltpu.make_async_remote_copy(
            src_ref=src[a].at[:, 1] if permuted[a] else src[a].at[:, 1 - c], dst_ref=dst[a],
            send_sem=send_sem.at[a], recv_sem=recv_sem.at[a],
            device_id=(x, y, 1 - c), device_id_type=MESH) for a in range(n)]
        for cp in copies:
            cp.start()
        for cp in copies:
            cp.wait()

    return pl.pallas_call(
        body, name=name,
        in_specs=[ANY] * n, out_specs=[ANY] * n,
        out_shape=[jax.ShapeDtypeStruct((N_CHIPS,) + g.shape[2:], F32) for g in grads],
        scratch_shapes=[pltpu.SemaphoreType.DMA((n,))] * 2,
    )(*grads)


def _scatter_copies(refs, permuted, send_sem, recv_sem):
    n = len(refs) // 2
    x, y, _, peers = _place()
    me = 2 * x + y
    return [pltpu.make_async_remote_copy(
        src_ref=refs[a].at[r + 1] if permuted[a] else refs[a].at[me ^ (r + 1)], dst_ref=refs[n + a].at[r],
        send_sem=send_sem.at[a * 3 + r], recv_sem=recv_sem.at[a * 3 + r],
        device_id=peers[r], device_id_type=MESH) for a in range(n) for r in range(3)]


def _scatter_start(partials, permuted, name, collective_id):
    def body(refs, send_sem, recv_sem):
        _handshake(_place()[3])
        for cp in _scatter_copies(refs, permuted, send_sem, recv_sem):
            cp.start()

    lands = [lax.empty((N_CHIPS - 1,) + p.shape[1:], p.dtype) for p in partials]
    return _split_start(body, name, collective_id, list(partials) + lands, 3 * len(partials))


def _scatter_wait(send_sem, recv_sem, operands, permuted, after, name):
    def body(refs, send_sem, recv_sem):
        for cp in _scatter_copies(refs, permuted, send_sem, recv_sem):
            cp.wait_send()
            cp.wait_recv()

    return _split_wait(body, name, send_sem, recv_sem, operands, after)


def _pair_join(halves, name):
    n = len(halves)

    def body(*refs):
        src, dst = refs[:n], refs[n:2 * n]
        send_sem, recv_sem = refs[2 * n:]
        x, y, c, _ = _place()
        copies = [pltpu.make_async_remote_copy(
            src_ref=src[a], dst_ref=dst[a], send_sem=send_sem.at[a], recv_sem=recv_sem.at[a],
            device_id=(x, y, 1 - c), device_id_type=MESH) for a in range(n)]
        for cp in copies:
            cp.start()
        for cp in copies:
            cp.wait()

    return pl.pallas_call(
        body, name=name,
        in_specs=[ANY] * n, out_specs=[ANY] * n,
        out_shape=[jax.ShapeDtypeStruct(h.shape, F32) for h in halves],
        scratch_shapes=[pltpu.SemaphoreType.DMA((n,))] * 2,
    )(*halves)


def _all_sum_small(v, name):
    R, C = v.shape
    n_dev = 8

    def body(v_ref, o_ref, buf, send_sem, recv_sem):
        x, y, c, _ = _place()
        me = 4 * x + 2 * y + c
        buf[me] = v_ref[...]
        copies = []
        for k in range(1, n_dev):
            peer = (x ^ (k >> 2), y ^ ((k >> 1) & 1), c ^ (k & 1))
            copies.append(pltpu.make_async_remote_copy(
                src_ref=v_ref, dst_ref=buf.at[me], send_sem=send_sem.at[k - 1], recv_sem=recv_sem.at[k - 1],
                device_id=peer, device_id_type=MESH))
        for cp in copies:
            cp.start()
        for cp in copies:
            cp.wait()
        acc = buf[0]
        for m in range(1, n_dev):
            acc = acc + buf[m]
        o_ref[...] = acc

    return pl.pallas_call(
        body, name=name,
        in_specs=[pl.BlockSpec(memory_space=pltpu.VMEM)], out_specs=pl.BlockSpec(memory_space=pltpu.VMEM),
        out_shape=jax.ShapeDtypeStruct((R, C), F32),
        scratch_shapes=[pltpu.VMEM((n_dev, R, C), F32), pltpu.SemaphoreType.DMA((n_dev - 1,)),
                        pltpu.SemaphoreType.DMA((n_dev - 1,))],
    )(v)


class _WholeWeights:
    def __init__(self, w):
        self.w = w

    def weights(self, group, after=None):
        return self.w, None

    def grads_ready(self, group, gw):
        return None


def _local_step(x, p, target, gains, rel_bias, hooks):
    T, D = x.shape
    S = N_CHIPS

    tied = lambda gain, token: gain if token is None else gain + token[0, 0]
    w, token = hooks.weights("first")
    w = dict(w)
    h1, xn1, g1, u1, a1, f1 = _ffn_fwd(x, tied(gains["ffn1_pre"], token), gains["ffn1_post"], w["ffn1_gate"],
                                       w["ffn1_up"], w["ffn1_down"], "ffn1_fwd")
    more, token = hooks.weights("in", h1)
    w.update(more)
    qkv, un = _norm_proj(h1, tied(gains["mix_pre"], token), w["in"], "qkv_proj")
    bias = _bias_table(rel_bias, "bias_table").transpose(1, 0, 2)
    o_a, tot = _sb_fwd(qkv, "sb_fwd")
    o_b = _ch_fwd(qkv, bias, "ch_fwd")
    w.update(hooks.weights("rest", o_b)[0])
    w_out = w["out"].reshape(D, D)
    h2, mixed, mo = _mix_out_fwd(h1, o_a, o_b, gains["out_sb"], gains["out_ch"], w_out, gains["mix_post"],
                                 "mix_out_fwd")
    h3, xn2, g2, u2, a2, f2 = _ffn_fwd(h2, gains["ffn2_pre"], gains["ffn2_post"], w["ffn2_gate"], w["ffn2_up"],
                                       w["ffn2_down"], "ffn2_fwd")
    w_ple_proj = w["ple_proj"].transpose(1, 0, 2).reshape(p.shape[1], D)
    w_ple_gate = w["ple_gate"].reshape(D, D)

    loss, dh3, dproj, dgate, dg_ple = _ple_loss(h3, p, target, w_ple_proj, w_ple_gate, gains["ple_post"], "ple_loss")
    gw, gg = {}, {"ple_post": dg_ple}
    gw["ple_proj"] = _mm_tn(p[None], dproj, p.shape[1], "dw_ple_proj")
    gw["ple_gate"] = _mm_tn(h3[None], dgate[None], 512, "dw_ple_gate").reshape(S, D // S, D)

    def ffn_bwd(tag, dh, x_in, xn, g_act, u_act, a_act, f, group):
        dgp, dup, df, gg[tag + "_post"] = _ffn_bwd_act(dh, f, gains[tag + "_post"], w[tag + "_down"], g_act, u_act,
                                                       tag + "_bwd_act")
        gw[tag + "_gate"] = _mm_tn(dgp, xn[None], dgp.shape[2], "dw_" + tag + "_gate")
        gw[tag + "_up"] = _mm_tn(dup, xn[None], dup.shape[2], "dw_" + tag + "_up")
        gw[tag + "_down"] = _mm_tn(a_act, df[None], a_act.shape[2], "dw_" + tag + "_down")
        g_pre = gains[tag + "_pre"]
        if group is not None:
            token = hooks.grads_ready(group, gw)
            g_pre = g_pre if token is None else g_pre + token[0, 0]
        dx, gg[tag + "_pre"] = _proj_bwd([dgp, dup], [w[tag + "_gate"], w[tag + "_up"]], x_in, g_pre, dh,
                                         tag + "_bwd_in")
        return dx

    dh2 = ffn_bwd("ffn2", dh3, h2, xn2, g2, u2, a2, f2, None)
    dmo, do_a, do_b, gg["mix_post"], gg["out_sb"], gg["out_ch"] = _mix_out_bwd(
        dh2, mo, gains["mix_post"], w_out, o_a, o_b, gains["out_sb"], gains["out_ch"], "mix_out_bwd")
    gw["out"] = _mm_tn(mixed[None], dmo[None], 512, "dw_out").reshape(S, D // S, D)
    token = hooks.grads_ready("early", gw)
    if token is not None:
        tot = tot + token[0, 0]
    dq_a, dk_a, dv_a = _sb_bwd(qkv, do_a, tot, "sb_bwd")
    dq_b, dk_b, dv_b, dbias = _ch_bwd(qkv, bias, do_b, "ch_bwd")
    g_rel = _bias_grad(dbias.transpose(1, 0, 2), "bias_grad")
    dqkv = jnp.concatenate([dq_a, dk_a, dv_a, dq_b, dk_b, dv_b], axis=1)
    gw["in"] = _mm_tn(un[None], dqkv, 512, "dw_in", groups=S)
    dh1, gg["mix_pre"] = _proj_bwd([dqkv], [w["in"]], h1, gains["mix_pre"], dh2, "qkv_bwd_in")
    dx = ffn_bwd("ffn1", dh1, x, xn1, g1, u1, a1, f1, "late")
    return loss, dx, gw, gg, g_rel


BIG = ["ffn1_gate", "ffn1_up", "ffn1_down", "in", "out", "ffn2_gate", "ffn2_up", "ffn2_down", "ple_proj", "ple_gate"]
GAINS = ["ffn1_pre", "ffn1_post", "mix_pre", "mix_post", "out_sb", "out_ch", "ffn2_pre", "ffn2_post", "ple_post"]
TRANSPOSED = ("w_ffn1_gate", "w_ffn1_up", "w_ffn2_gate", "w_ffn2_up")
PERMUTED = ("ffn1_gate", "ffn1_up", "ffn1_down", "ffn2_gate", "ffn2_up", "ffn2_down")
W_GROUPS = {"first": ["ffn1_gate", "ffn1_up", "ffn1_down"], "in": ["in"],
            "rest": ["out", "ffn2_gate", "ffn2_up", "ffn2_down", "ple_proj", "ple_gate"]}
G_GROUPS = {"early": ["ple_proj", "ple_gate", "ffn2_gate", "ffn2_up", "ffn2_down", "out"],
            "late": ["in", "ffn1_gate", "ffn1_up", "ffn1_down"]}
ORDER = ["g_ffn1_pre", "g_ffn1_post", "w_ffn1_gate", "w_ffn1_up", "w_ffn1_down", "g_mix_pre", "g_mix_post", "w_in",
         "g_out_sb", "g_out_ch", "rel_bias", "w_out", "g_ffn2_pre", "g_ffn2_post", "w_ffn2_gate", "w_ffn2_up",
         "w_ffn2_down", "w_ple_proj", "w_ple_gate", "g_ple_post"]


def kernel(x, p, g_ffn1_pre, g_ffn1_post, w_ffn1_gate, w_ffn1_up, w_ffn1_down, g_mix_pre, g_mix_post, w_in, g_out_sb, g_out_ch, rel_bias, w_out, g_ffn2_pre, g_ffn2_post, w_ffn2_gate, w_ffn2_up, w_ffn2_down, w_ple_proj, w_ple_gate, g_ple_post, loss_target, m_g_ffn1_pre, m_g_ffn1_post, m_w_ffn1_gate, m_w_ffn1_up, m_w_ffn1_down, m_g_mix_pre, m_g_mix_post, m_w_in, m_g_out_sb, m_g_out_ch, m_rel_bias, m_w_out, m_g_ffn2_pre, m_g_ffn2_post, m_w_ffn2_gate, m_w_ffn2_up, m_w_ffn2_down, m_w_ple_proj, m_w_ple_gate, m_g_ple_post, v_g_ffn1_pre, v_g_ffn1_post, v_w_ffn1_gate, v_w_ffn1_up, v_w_ffn1_down, v_g_mix_pre, v_g_mix_post, v_w_in, v_g_out_sb, v_g_out_ch, v_rel_bias, v_w_out, v_g_ffn2_pre, v_g_ffn2_post, v_w_ffn2_gate, v_w_ffn2_up, v_w_ffn2_down, v_w_ple_proj, v_w_ple_gate, v_g_ple_post):
    args = dict(locals())
    take = lambda a, n: a[0].T if n in TRANSPOSED else a[0]
    wts = {n: take(args[n], n) for n in ORDER}
    ms = {n: take(args["m_" + n], n) for n in ORDER}
    vs = {n: take(args["v_" + n], n) for n in ORDER}
    gains = {n: wts["g_" + n][None] for n in GAINS}

    c_idx = lax.axis_index("c").astype(jnp.int32).reshape(1)
    me_idx = (2 * lax.axis_index("x") + lax.axis_index("y")).astype(jnp.int32).reshape(1)
    south = lax.axis_index("c") == 0

    lands = dict(zip(BIG, _cast_into_slot0(c_idx, [wts["w_" + n] for n in BIG], "cast_weights")))

    def in_order(names, zones):
        plain = [n for n in names if n not in PERMUTED]
        fixed = dict(zip(plain, _chip_order(me_idx, c_idx, [zones[n] for n in plain], "chip_order_" + plain[0]))
                     ) if plain else {}
        return {n: fixed[n] if n in fixed else _slabs(zones[n]) for n in names}

    class Overlapped:
        def __init__(self):
            self.started = {}
            self.flying = None

        def start(self, group, collective_id, after):
            self.flying = _gather_start([lands[n] for n in W_GROUPS[group]], "gather_%s_start" % group,
                                        collective_id, after)
            return self.flying[3]

        def weights(self, group, after=None):
            names = W_GROUPS[group]
            if group == "first":
                zones = _gather_finish([lands[n] for n in names], True, "gather_first")
                return in_order(names, dict(zip(names, zones))), self.start("in", 1, zones[0])
            send_sem, recv_sem, zones, _ = self.flying
            zones = _gather_wait(send_sem, recv_sem, zones, after, "gather_%s_wait" % group)
            zones = _gather_finish(zones, False, "gather_%s_finish" % group)
            token = self.start("rest", 4, zones[0]) if group == "in" else None
            return in_order(names, dict(zip(names, zones))), token

        def grads_ready(self, group, gw):
            self.started[group] = reduce_start(G_GROUPS[group], gw, group, {"early": 2, "late": 3}[group])
            return self.started[group][-1]

    def reduce_start(names, gw, tag, cid):
        perm = [n in PERMUTED for n in names]
        mine = [gw[n].reshape(N_CHIPS, 2, gw[n].shape[1] // 2, gw[n].shape[2]) for n in names]
        got = _pair_swap(mine, perm, "grad_pair_swap_" + tag)
        partial = _pair_add(c_idx, mine, got, perm, "grad_pair_add_" + tag)
        send_sem, recv_sem, operands, token = _scatter_start(partial, perm, "grad_scatter_start_" + tag, cid)
        return names, perm, send_sem, recv_sem, operands, token

    def reduce_finish(state, after, tag):
        names, perm, send_sem, recv_sem, operands, _ = state
        operands = _scatter_wait(send_sem, recv_sem, operands, perm, after, "grad_scatter_wait_" + tag)
        n = len(names)
        halves = _chip_add(me_idx, operands[:n], operands[n:], perm, "grad_chip_add_" + tag)
        out = {}
        for name, own, other in zip(names, halves, _pair_join(halves, "grad_pair_join_" + tag)):
            out["w_" + name] = jnp.concatenate([jnp.where(south, own, other), jnp.where(south, other, own)], axis=0)
        return out

    hooks = Overlapped()
    loss, dx, gw, gg, g_rel = _local_step(x[0], p[0, 0], loss_target[0], gains, wts["rel_bias"], hooks)

    late = hooks.started["late"]
    grads = reduce_finish(hooks.started["early"], dx, "early")

    pieces = [gg[n].reshape(-1, 128) for n in GAINS] + [jnp.pad(g_rel, ((0, 0), (0, N_REL_PAD - N_REL))).reshape(-1, 128)]
    summed = _all_sum_small(jnp.concatenate(pieces, axis=0), "small_grad_sum")
    at = 0
    for n, piece in zip(GAINS, pieces[:-1]):
        grads["g_" + n] = summed[at:at + piece.shape[0]].reshape(1, -1)[0]
        at += piece.shape[0]
    grads["rel_bias"] = summed[at:].reshape(N_HEADS, N_REL_PAD)[:, :N_REL]

    delta, new_m, new_v = {}, {}, {}

    def adamw_big(group):
        names = ["w_" + n for n in G_GROUPS[group]]
        d, m, v = _adamw([wts[n] for n in names], [grads[n] for n in names], [ms[n] for n in names],
                         [vs[n] for n in names], 8, "adamw_" + group)
        for n, dd, mm, vv in zip(names, d, m, v):
            delta[n], new_m[n], new_v[n] = dd, mm, vv
        return d[0]

    done_early = adamw_big("early")
    grads.update(reduce_finish(late, done_early, "late"))
    adamw_big("late")
    small = ["g_" + n for n in GAINS] + ["rel_bias"]
    as_rows = lambda a: (a.reshape(-1, 128) if a.size % 128 == 0 else jnp.pad(a, ((0, 0), (0, N_REL_PAD - N_REL))).reshape(-1, 128))
    d, m, v = _adamw([as_rows(wts[n]) for n in small], [as_rows(grads[n]) for n in small],
                     [as_rows(ms[n]) for n in small], [as_rows(vs[n]) for n in small], 1, "adamw_small")
    for n, dd, mm, vv in zip(small, d, m, v):
        back = (lambda a: a.reshape(N_HEADS, N_REL_PAD)[:, :N_REL]) if n == "rel_bias" else (lambda a: a.reshape(-1))
        delta[n], new_m[n], new_v[n] = back(dd), back(mm), back(vv)

    loss = lax.psum(loss[0, 0], ("x", "y", "c"))
    outs = [loss, dx[None]]
    for table in (grads, delta, new_m, new_v):
        outs += [(table[n].T if n in TRANSPOSED else table[n])[None] for n in ORDER]
    return tuple(outs)
```

```python
import functools

import jax
import jax.numpy as jnp
from jax import lax
from jax.experimental import pallas as pl
from jax.experimental.pallas import tpu as pltpu

F32 = jnp.float32
BF16 = jnp.bfloat16
EPS = 1e-6
N_CHIPS = 4
HEAD_DIM = 64
N_HEADS = 8
CHUNK = 64
LOOKBACK = 8
BAND = (LOOKBACK + 1) * CHUNK
PAD = LOOKBACK * CHUNK
REL_CLIP = 128
N_REL = 2 * REL_CLIP + 1
N_REL_PAD = 384
SB_BLOCK = 256
PAIR = 2 * HEAD_DIM
SB_PAIRS = 2
ATT_SCALE = HEAD_DIM ** -0.5
NEG_INF = -1e30
ROW_BLOCK = 512
VMEM_LIMIT = 48 * 1024 * 1024
MESH = pl.DeviceIdType.MESH

ADAM_LR = 0.001
ADAM_B1 = 0.9
ADAM_B2 = 0.999
ADAM_EPS = 1e-08
ADAM_WD = 0.01
ADAM_STEP = 10

NT = (((1,), (1,)), ((), ()))
TN = (((0,), (0,)), ((), ()))


def _params(n_grid, vmem=None):
    return pltpu.CompilerParams(dimension_semantics=("arbitrary",) * n_grid, vmem_limit_bytes=vmem)


def _dot(a, b, dims=None):
    if dims is None:
        return jnp.dot(a, b, preferred_element_type=F32)
    return lax.dot_general(a, b, dims, preferred_element_type=F32)


def _sigmoid(x):
    return 1.0 / (1.0 + jnp.exp(-x))


def _rms_fwd(x, g):
    r = lax.rsqrt(jnp.mean(x * x, axis=-1, keepdims=True) + EPS)
    return x * r * g


def _rms_bwd(x, g, dy):
    r = lax.rsqrt(jnp.mean(x * x, axis=-1, keepdims=True) + EPS)
    xh = x * r
    dg = jnp.sum(dy * xh, axis=0, keepdims=True)
    t = dy * g
    dx = r * (t - xh * jnp.mean(t * xh, axis=-1, keepdims=True))
    return dx, dg


def _accumulate(ref, val, first):
    @pl.when(first)
    def _():
        ref[...] = val

    @pl.when(jnp.logical_not(first))
    def _():
        ref[...] += val


def _split2(x):
    hi = x.astype(BF16)
    lo = (x - hi.astype(F32)).astype(BF16)
    return hi, lo


def _ffn_fwd(x, g_pre, g_post, wg, wu, wd, name):
    T, D = x.shape
    S, FS, _ = wg.shape
    tm = min(ROW_BLOCK, T)

    def body(x_ref, gpre_ref, gpost_ref, wg_ref, wu_ref, wd_ref,
             h_ref, xn_ref, g_ref, u_ref, a_ref, f_ref, xn_s, acc_s):
        k = pl.program_id(1)

        @pl.when(k == 0)
        def _():
            xn_s[...] = _rms_fwd(x_ref[...], gpre_ref[...]).astype(BF16)
            xn_ref[...] = xn_s[...]

        xn = xn_s[...]
        g = _dot(xn, wg_ref[0], NT)
        u = _dot(xn, wu_ref[0], NT)
        g_ref[0] = g
        u_ref[0] = u
        a = (g * _sigmoid(g) * u).astype(BF16)
        a_ref[0] = a
        _accumulate(acc_s, _dot(a, wd_ref[0]), k == 0)

        @pl.when(k == S - 1)
        def _():
            f = acc_s[...]
            f_ref[...] = f
            h_ref[...] = x_ref[...] + 0.5 * _rms_fwd(f, gpost_ref[...])

    row = pl.BlockSpec((tm, D), lambda i, k: (i, 0))
    vec = pl.BlockSpec((1, D), lambda i, k: (0, 0))
    act = pl.BlockSpec((1, tm, FS), lambda i, k: (k, i, 0))
    return pl.pallas_call(
        body, name=name, grid=(T // tm, S),
        in_specs=[row, vec, vec] + [pl.BlockSpec((1, FS, D), lambda i, k: (k, 0, 0))] * 3,
        out_specs=[row, row, act, act, act, row],
        out_shape=[jax.ShapeDtypeStruct((T, D), F32), jax.ShapeDtypeStruct((T, D), BF16),
                   jax.ShapeDtypeStruct((S, T, FS), F32), jax.ShapeDtypeStruct((S, T, FS), F32),
                   jax.ShapeDtypeStruct((S, T, FS), BF16), jax.ShapeDtypeStruct((T, D), F32)],
        scratch_shapes=[pltpu.VMEM((tm, D), BF16), pltpu.VMEM((tm, D), F32)],
        compiler_params=_params(2, VMEM_LIMIT),
    )(x, g_pre, g_post, wg, wu, wd)


def _ffn_bwd_act(dh, f, g_post, wd, g_act, u_act, name):
    T, D = dh.shape
    S, FS, _ = wd.shape
    tm = min(ROW_BLOCK, T)

    def body(dh_ref, f_ref, gpost_ref, wd_ref, g_ref, u_ref, dgp_ref, dup_ref, df_ref, dgain_ref, df_s):
        i, k = pl.program_id(0), pl.program_id(1)

        @pl.when(k == 0)
        def _():
            df, dgain = _rms_bwd(f_ref[...], gpost_ref[...], 0.5 * dh_ref[...])
            df_s[...] = df.astype(BF16)
            df_ref[...] = df_s[...]
            _accumulate(dgain_ref, dgain, i == 0)

        da = _dot(df_s[...], wd_ref[0], NT)
        g = g_ref[0]
        s = _sigmoid(g)
        dup_ref[0] = (da * (g * s)).astype(BF16)
        dgp_ref[0] = (da * u_ref[0] * (s * (1.0 + g * (1.0 - s)))).astype(BF16)

    row = pl.BlockSpec((tm, D), lambda i, k: (i, 0))
    vec = pl.BlockSpec((1, D), lambda i, k: (0, 0))
    act = pl.BlockSpec((1, tm, FS), lambda i, k: (k, i, 0))
    return pl.pallas_call(
        body, name=name, grid=(T // tm, S),
        in_specs=[row, row, vec, pl.BlockSpec((1, FS, D), lambda i, k: (k, 0, 0)), act, act],
        out_specs=[act, act, row, vec],
        out_shape=[jax.ShapeDtypeStruct((S, T, FS), BF16), jax.ShapeDtypeStruct((S, T, FS), BF16),
                   jax.ShapeDtypeStruct((T, D), BF16), jax.ShapeDtypeStruct((1, D), F32)],
        scratch_shapes=[pltpu.VMEM((tm, D), BF16)],
        compiler_params=_params(2, VMEM_LIMIT),
    )(dh, f, g_post, wd, g_act, u_act)


def _proj_bwd(dys, ws, x, g_pre, dh, name):
    T, D = x.shape
    n = len(dys)
    flat = dys[0].ndim == 2
    S = ws[0].shape[0]
    N = ws[0].shape[2] if flat else ws[0].shape[1]
    tm = min(ROW_BLOCK, T)

    def body(*refs):
        dy_refs, w_refs = refs[:n], refs[n:2 * n]
        x_ref, gpre_ref, dh_ref, dx_ref, dgain_ref, acc_s = refs[2 * n:]
        i, k = pl.program_id(0), pl.program_id(1)
        part = None
        for dy_ref, w_ref in zip(dy_refs, w_refs):
            term = _dot(dy_ref[...], w_ref[0], NT) if flat else _dot(dy_ref[0], w_ref[0])
            part = term if part is None else part + term
        _accumulate(acc_s, part, k == 0)

        @pl.when(k == S - 1)
        def _():
            dx, dgain = _rms_bwd(x_ref[...], gpre_ref[...], acc_s[...])
            dx_ref[...] = dh_ref[...] + dx
            _accumulate(dgain_ref, dgain, i == 0)

    row = pl.BlockSpec((tm, D), lambda i, k: (i, 0))
    vec = pl.BlockSpec((1, D), lambda i, k: (0, 0))
    return pl.pallas_call(
        body, name=name, grid=(T // tm, S),
        in_specs=[pl.BlockSpec((tm, N), lambda i, k: (i, k)) if flat else pl.BlockSpec((1, tm, N), lambda i, k: (k, i, 0))] * n
        + [pl.BlockSpec((1,) + ws[0].shape[1:], lambda i, k: (k, 0, 0))] * n + [row, vec, row],
        out_specs=[row, vec],
        out_shape=[jax.ShapeDtypeStruct((T, D), F32), jax.ShapeDtypeStruct((1, D), F32)],
        scratch_shapes=[pltpu.VMEM((tm, D), F32)],
        compiler_params=_params(2, VMEM_LIMIT),
    )(*dys, *ws, x, g_pre, dh)


def _mm_tn(a, b, bm, name, groups=None):
    ga, T, M = a.shape
    if groups is None:
        gb, _, N = b.shape
        b_spec = pl.BlockSpec((1, T, N), (lambda g, m: (g, 0, 0)) if gb > 1 else (lambda g, m: (0, 0, 0)))
    else:
        gb, N = groups, b.shape[1] // groups
        b_spec = pl.BlockSpec((T, N), lambda g, m: (0, g))
    G = max(ga, gb)

    def body(a_ref, b_ref, o_ref):
        bv = b_ref[0] if groups is None else b_ref[...]
        o_ref[0] = _dot(a_ref[0].astype(BF16), bv.astype(BF16), TN)

    return pl.pallas_call(
        body, name=name, grid=(G, M // bm),
        in_specs=[pl.BlockSpec((1, T, bm), (lambda g, m: (g, 0, m)) if ga > 1 else (lambda g, m: (0, 0, m))), b_spec],
        out_specs=pl.BlockSpec((1, bm, N), lambda g, m: (g, m, 0)),
        out_shape=jax.ShapeDtypeStruct((G, M, N), F32),
        compiler_params=_params(2, VMEM_LIMIT),
    )(a, b)


def _norm_proj(x, g_pre, w, name):
    T, D = x.shape
    S, _, N = w.shape
    tm = min(ROW_BLOCK, T)

    def body(x_ref, g_ref, w_ref, o_ref, xn_ref, xn_s):
        @pl.when(pl.program_id(1) == 0)
        def _():
            xn_s[...] = _rms_fwd(x_ref[...], g_ref[...]).astype(BF16)
            xn_ref[...] = xn_s[...]

        o_ref[...] = _dot(xn_s[...], w_ref[0]).astype(BF16)

    row = pl.BlockSpec((tm, D), lambda i, k: (i, 0))
    return pl.pallas_call(
        body, name=name, grid=(T // tm, S),
        in_specs=[row, pl.BlockSpec((1, D), lambda i, k: (0, 0)), pl.BlockSpec((1, D, N), lambda i, k: (k, 0, 0))],
        out_specs=[pl.BlockSpec((tm, N), lambda i, k: (i, k)), row],
        out_shape=[jax.ShapeDtypeStruct((T, S * N), BF16), jax.ShapeDtypeStruct((T, D), BF16)],
        scratch_shapes=[pltpu.VMEM((tm, D), BF16)],
        compiler_params=_params(2, VMEM_LIMIT),
    )(x, g_pre, w)


def _mix_out_fwd(h, o_a, o_b, g_sb, g_ch, w_out, g_post, name):
    T, D = h.shape
    W = g_sb.shape[1]
    tm = min(ROW_BLOCK, T)

    def body(h_ref, oa_ref, ob_ref, gsb_ref, gch_ref, w_ref, gpost_ref, h2_ref, mixed_ref, mo_ref):
        mixed_ref[:, :W] = _rms_fwd(oa_ref[...], gsb_ref[...]).astype(BF16)
        mixed_ref[:, W:] = _rms_fwd(ob_ref[...], gch_ref[...]).astype(BF16)
        mo = _dot(mixed_ref[...], w_ref[...])
        mo_ref[...] = mo
        h2_ref[...] = h_ref[...] + _rms_fwd(mo, gpost_ref[...])

    row = pl.BlockSpec((tm, D), lambda i: (i, 0))
    part = pl.BlockSpec((tm, W), lambda i: (i, 0))
    half = pl.BlockSpec((1, W), lambda i: (0, 0))
    return pl.pallas_call(
        body, name=name, grid=(T // tm,),
        in_specs=[row, part, part, half, half, pl.BlockSpec((D, D), lambda i: (0, 0)), pl.BlockSpec((1, D), lambda i: (0, 0))],
        out_specs=[row, row, row],
        out_shape=[jax.ShapeDtypeStruct((T, D), F32), jax.ShapeDtypeStruct((T, D), BF16),
                   jax.ShapeDtypeStruct((T, D), F32)],
        compiler_params=_params(1, VMEM_LIMIT),
    )(h, o_a, o_b, g_sb, g_ch, w_out, g_post)


def _mix_out_bwd(dh, mo, g_post, w_out, o_a, o_b, g_sb, g_ch, name):
    T, D = dh.shape
    W = g_sb.shape[1]
    tm = min(ROW_BLOCK, T)

    def body(dh_ref, mo_ref, gpost_ref, w_ref, oa_ref, ob_ref, gsb_ref, gch_ref,
             dmo_ref, doa_ref, dob_ref, dgpost_ref, dgsb_ref, dgch_ref):
        first = pl.program_id(0) == 0
        dmo, dgpost = _rms_bwd(mo_ref[...], gpost_ref[...], dh_ref[...])
        dmo_ref[...] = dmo.astype(BF16)
        dmix = _dot(dmo_ref[...], w_ref[...], NT)
        doa_ref[...], dgsb = _rms_bwd(oa_ref[...], gsb_ref[...], dmix[:, :W])
        dob_ref[...], dgch = _rms_bwd(ob_ref[...], gch_ref[...], dmix[:, W:])
        _accumulate(dgpost_ref, dgpost, first)
        _accumulate(dgsb_ref, dgsb, first)
        _accumulate(dgch_ref, dgch, first)

    row = pl.BlockSpec((tm, D), lambda i: (i, 0))
    part = pl.BlockSpec((tm, W), lambda i: (i, 0))
    vec = pl.BlockSpec((1, D), lambda i: (0, 0))
    half = pl.BlockSpec((1, W), lambda i: (0, 0))
    return pl.pallas_call(
        body, name=name, grid=(T // tm,),
        in_specs=[row, row, vec, pl.BlockSpec((D, D), lambda i: (0, 0)), part, part, half, half],
        out_specs=[row, part, part, vec, half, half],
        out_shape=[jax.ShapeDtypeStruct((T, D), BF16), jax.ShapeDtypeStruct((T, W), F32),
                   jax.ShapeDtypeStruct((T, W), F32), jax.ShapeDtypeStruct((1, D), F32),
                   jax.ShapeDtypeStruct((1, W), F32), jax.ShapeDtypeStruct((1, W), F32)],
        compiler_params=_params(1, VMEM_LIMIT),
    )(dh, mo, g_post, w_out, o_a, o_b, g_sb, g_ch)


def _ple_loss(h, p, target, w_proj, w_gate, g_post, name):
    T, D = h.shape
    P = p.shape[1]
    S = N_CHIPS
    C = D // S
    tm = min(ROW_BLOCK, T)

    def body(h_ref, p_ref, t_ref, wp_ref, wg_ref, g_ref, loss_ref, dh_ref, dproj_ref, dgate_ref, dgain_ref):
        first = pl.program_id(0) == 0
        h3 = h_ref[...]
        proj = _dot(p_ref[...].astype(BF16), wp_ref[...])
        s = _sigmoid(_dot(h3.astype(BF16), wg_ref[...]))
        e = proj * s
        diff = h3 + _rms_fwd(e, g_ref[...]) - t_ref[...]
        part = 0.5 * jnp.sum(jnp.mean(diff * diff, axis=-1, keepdims=True), axis=0, keepdims=True)
        _accumulate(loss_ref, jnp.broadcast_to(part, loss_ref.shape), first)
        dy = diff * (1.0 / D)
        de, dgain = _rms_bwd(e, g_ref[...], dy)
        _accumulate(dgain_ref, dgain, first)
        dproj = (de * s).astype(BF16)
        for j in range(S):
            dproj_ref[j] = dproj[:, j * C:(j + 1) * C]
        dgate_ref[...] = (de * proj * s * (1.0 - s)).astype(BF16)
        dh_ref[...] = dy + _dot(dgate_ref[...], wg_ref[...], NT)

    row = pl.BlockSpec((tm, D), lambda i: (i, 0))
    vec = pl.BlockSpec((1, D), lambda i: (0, 0))
    return pl.pallas_call(
        body, name=name, grid=(T // tm,),
        in_specs=[row, pl.BlockSpec((tm, P), lambda i: (i, 0)), row,
                  pl.BlockSpec((P, D), lambda i: (0, 0)), pl.BlockSpec((D, D), lambda i: (0, 0)), vec],
        out_specs=[pl.BlockSpec((8, 128), lambda i: (0, 0)), row,
                   pl.BlockSpec((S, tm, C), lambda i: (0, i, 0)), row, vec],
        out_shape=[jax.ShapeDtypeStruct((8, 128), F32), jax.ShapeDtypeStruct((T, D), F32),
                   jax.ShapeDtypeStruct((S, T, C), BF16), jax.ShapeDtypeStruct((T, D), BF16),
                   jax.ShapeDtypeStruct((1, D), F32)],
        compiler_params=_params(1, VMEM_LIMIT),
    )(h, p, target, w_proj, w_gate, g_post)


def _sb_scores(q, kj, mask):
    z = _dot(q, kj, NT)
    sp = jnp.maximum(z, 0.0) + jnp.log(1.0 + jnp.exp(-jnp.abs(z)))
    return z, sp if mask is None else jnp.where(mask, sp, 0.0)


def _strict_causal():
    rows = lax.broadcasted_iota(jnp.int32, (SB_BLOCK, SB_BLOCK), 0)
    cols = lax.broadcasted_iota(jnp.int32, (SB_BLOCK, SB_BLOCK), 1)
    return cols < rows


def _tri(cmp):
    r = lax.broadcasted_iota(jnp.int32, (2 * SB_BLOCK, SB_BLOCK), 0) % SB_BLOCK
    c = lax.broadcasted_iota(jnp.int32, (2 * SB_BLOCK, SB_BLOCK), 1)
    return jnp.where(cmp(r, c), 1.0, 0.0).astype(BF16)


def _cum(x, tri):
    return _dot(jnp.concatenate(_split2(x), axis=1), tri)


def _pair_lanes():
    lane = lax.broadcasted_iota(jnp.int32, (1, PAIR), 1)
    return [lane < HEAD_DIM, lane >= HEAD_DIM]


def _only(lanes, x):
    return jnp.where(lanes, x, jnp.zeros_like(x))


def _sb_fwd(qkv, name):
    T = qkv.shape[0]
    B = SB_BLOCK
    W = SB_PAIRS * PAIR
    steps = N_HEADS // (2 * SB_PAIRS)
    heads = [(p, h) for p in range(SB_PAIRS) for h in range(2)]

    def body(q_ref, k_ref, v_ref, o_ref, tot_ref):
        i = pl.program_id(1)
        after = _tri(lambda r, c: r > c)
        lanes = _pair_lanes()
        cols = [slice(p * PAIR, (p + 1) * PAIR) for p in range(SB_PAIRS)]
        q = {(p, h): _only(lanes[h], q_ref[:, cols[p]] * ATT_SCALE) for p, h in heads}

        def tiles(j, carries, mask):
            at = pl.ds(pl.multiple_of(j * B, B), B)
            scores = [_sb_scores(q[ph], k_ref[at, cols[ph[0]]], mask) for ph in heads]
            laters = [_cum(sp, after) for _, sp in scores]
            out = []
            for ph, (z, sp), later, (run, acc) in zip(heads, scores, laters, carries):
                a = jnp.exp(z - sp - later - run)
                if mask is not None:
                    a = jnp.where(mask, a, 0.0)
                out.append((run + later[:, 0:1] + sp[:, 0:1],
                            acc + _dot(a.astype(BF16), _only(lanes[ph[1]], v_ref[at, cols[ph[0]]]))))
            return tuple(out)

        zero = (jnp.zeros((B, 1), F32), jnp.zeros((B, PAIR), F32))
        carries = tiles(i, (zero,) * len(heads), _strict_causal())
        carries = lax.fori_loop(0, i, lambda jj, cs: tiles(i - 1 - jj, cs, None), carries)
        for p in range(SB_PAIRS):
            o_ref[:, cols[p]] = carries[2 * p][1] + carries[2 * p + 1][1]
            tot_ref[:, cols[p]] = jnp.where(lanes[0], carries[2 * p][0], carries[2 * p + 1][0])

    blk = lambda off: pl.BlockSpec((B, W), lambda g, i: (i, g + off))
    full = lambda off: pl.BlockSpec((T, W), lambda g, i: (0, g + off))
    out = jax.ShapeDtypeStruct((T, N_HEADS * HEAD_DIM), F32)
    return pl.pallas_call(
        body, name=name, grid=(steps, T // B),
        in_specs=[blk(0), full(steps), full(2 * steps)],
        out_specs=[blk(0), blk(0)],
        out_shape=[out, out],
        compiler_params=_params(2, VMEM_LIMIT),
    )(qkv, qkv, qkv)


def _sb_bwd(qkv, do, tot, name):
    T = qkv.shape[0]
    B = SB_BLOCK
    W = SB_PAIRS * PAIR
    steps = N_HEADS // (2 * SB_PAIRS)
    n_blocks = T // B
    heads = [(p, h) for p in range(SB_PAIRS) for h in range(2)]

    def body(q_ref, k_ref, v_ref, do_ref, tot_ref, dq_ref, dk_ref, dv_ref, dk_s, dv_s):
        i = pl.program_id(1)

        @pl.when(i == 0)
        def _():
            dk_s[...] = jnp.zeros_like(dk_s)
            dv_s[...] = jnp.zeros_like(dv_s)

        upto = _tri(lambda r, c: r <= c)
        below = _tri(lambda r, c: r < c)
        lanes = _pair_lanes()
        cols = [slice(p * PAIR, (p + 1) * PAIR) for p in range(SB_PAIRS)]
        q = {(p, h): _only(lanes[h], q_ref[:, cols[p]] * ATT_SCALE) for p, h in heads}
        do = {(p, h): _only(lanes[h], do_ref[:, cols[p]].astype(BF16)) for p, h in heads}
        tot = {(p, h): tot_ref[:, p * PAIR + h * HEAD_DIM:p * PAIR + h * HEAD_DIM + 1] for p, h in heads}

        def tiles(j, carries, mask):
            at = pl.ds(pl.multiple_of(j * B, B), B)
            ks = [k_ref[at, c] for c in cols]
            vs = [v_ref[at, c] for c in cols]
            scores = [_sb_scores(q[ph], ks[ph[0]], mask) for ph in heads]
            throughs = [_cum(sp, upto) for _, sp in scores]
            das = [_dot(do[ph], vs[ph[0]], NT) for ph in heads]
            a_s, gs = [], []
            for ph, (z, sp), through, da, carry in zip(heads, scores, throughs, das, carries):
                a = jnp.exp(z - sp + through - (tot[ph] - carry[0]))
                if mask is not None:
                    a = jnp.where(mask, a, 0.0)
                a_s.append(a)
                gs.append(a * da)
            befores = [_cum(g, below) for g in gs]
            dzs = []
            for (_, sp), g, before, carry in zip(scores, gs, befores, carries):
                g_before = carry[1] + before
                fail = jnp.exp(-sp)
                dz = fail * (g + g_before) - g_before
                if mask is not None:
                    dz = jnp.where(mask, dz, 0.0)
                dzs.append(dz.astype(BF16))
            out = []
            for ph, a, g, dz, through, before, carry in zip(heads, a_s, gs, dzs, throughs, befores, carries):
                dk_s[at, cols[ph[0]]] += _dot(dz, q[ph], TN)
                dv_s[at, cols[ph[0]]] += _dot(a.astype(BF16), do[ph], TN)
                out.append((carry[0] + through[:, B - 1:B], carry[1] + before[:, B - 1:B] + g[:, B - 1:B],
                            carry[2] + _dot(dz, _only(lanes[ph[1]], ks[ph[0]]))))
            return tuple(out)

        col = jnp.zeros((B, 1), F32)
        zero = (col, col, jnp.zeros((B, PAIR), F32))
        carries = lax.fori_loop(0, i, lambda j, cs: tiles(j, cs, None), (zero,) * len(heads))
        last = tiles(i, carries, _strict_causal())
        for p in range(SB_PAIRS):
            dq_ref[:, cols[p]] = ((last[2 * p][2] + last[2 * p + 1][2]) * ATT_SCALE).astype(BF16)

        @pl.when(i == n_blocks - 1)
        def _():
            dk_ref[...] = dk_s[...].astype(BF16)
            dv_ref[...] = dv_s[...].astype(BF16)

    blk = lambda off: pl.BlockSpec((B, W), lambda g, i: (i, g + off))
    full = lambda off: pl.BlockSpec((T, W), lambda g, i: (0, g + off))
    out = jax.ShapeDtypeStruct((T, N_HEADS * HEAD_DIM), BF16)
    return pl.pallas_call(
        body, name=name, grid=(steps, n_blocks),
        in_specs=[blk(0), full(steps), full(2 * steps), blk(0), blk(0)],
        out_specs=[blk(0), full(0), full(0)],
        out_shape=[out, out, out],
        scratch_shapes=[pltpu.VMEM((T, W), F32)] * 2,
        compiler_params=_params(2, VMEM_LIMIT),
    )(qkv, qkv, qkv, do, tot)


NEAR = BAND - PAD + REL_CLIP
FAR = BAND - NEAR
NEAR_REL = 2 * REL_CLIP
BIAS_ROWS = 8


def _rel_onehot(i, transposed):
    shape = (NEAR, NEAR_REL) if transposed else (NEAR_REL, NEAR)
    j = FAR + lax.broadcasted_iota(jnp.int32, shape, 0 if transposed else 1)
    r = lax.broadcasted_iota(jnp.int32, shape, 1 if transposed else 0)
    idx = jnp.clip(i + PAD - j, -REL_CLIP, REL_CLIP) + REL_CLIP
    return jnp.where(idx - 1 == r, 1.0, 0.0).astype(BF16)


def _bias_table(rel_bias, name):
    def body(near_ref, far_ref, o_ref):
        rb = near_ref[...]
        hi, lo = _split2(rb)
        lo2 = (rb - hi.astype(F32) - lo.astype(F32)).astype(BF16)
        far = jnp.broadcast_to(far_ref[...], (N_HEADS, FAR))
        for k in range(BIAS_ROWS):
            onehot = _rel_onehot(pl.program_id(0) * BIAS_ROWS + k, False)
            o_ref[k, :, :FAR] = far
            o_ref[k, :, FAR:] = _dot(hi, onehot) + _dot(lo, onehot) + _dot(lo2, onehot)

    return pl.pallas_call(
        body, name=name, grid=(CHUNK // BIAS_ROWS,),
        in_specs=[pl.BlockSpec((N_HEADS, NEAR_REL), lambda i: (0, 0)), pl.BlockSpec((N_HEADS, 1), lambda i: (0, 0))],
        out_specs=pl.BlockSpec((BIAS_ROWS, N_HEADS, BAND), lambda i: (i, 0, 0)),
        out_shape=jax.ShapeDtypeStruct((CHUNK, N_HEADS, BAND), F32),
        compiler_params=_params(1),
    )(rel_bias[:, 1:], rel_bias[:, N_REL - 1:])


def _bias_grad(dbias_t, name):
    def body(d_ref, near_ref, far_ref):
        near, far = None, None
        for k in range(BIAS_ROWS):
            onehot = _rel_onehot(pl.program_id(0) * BIAS_ROWS + k, True)
            hi, lo = _split2(d_ref[k, :, FAR:])
            part = _dot(hi, onehot) + _dot(lo, onehot)
            rest = jnp.sum(d_ref[k, :, :FAR], axis=1, keepdims=True)
            near, far = (part, rest) if near is None else (near + part, far + rest)
        first = pl.program_id(0) == 0
        _accumulate(near_ref, near, first)
        _accumulate(far_ref, jnp.broadcast_to(far, far_ref.shape), first)

    near, far = pl.pallas_call(
        body, name=name, grid=(CHUNK // BIAS_ROWS,),
        in_specs=[pl.BlockSpec((BIAS_ROWS, N_HEADS, BAND), lambda i: (i, 0, 0))],
        out_specs=[pl.BlockSpec((N_HEADS, NEAR_REL), lambda i: (0, 0)), pl.BlockSpec((N_HEADS, 128), lambda i: (0, 0))],
        out_shape=[jax.ShapeDtypeStruct((N_HEADS, NEAR_REL), F32), jax.ShapeDtypeStruct((N_HEADS, 128), F32)],
        compiler_params=_params(1),
    )(dbias_t)
    return jnp.pad(near, ((0, 0), (1, 0))).at[:, N_REL - 1].add(far[:, 0])


def _ch_probs(scores, bias, valid):
    z = jnp.where(valid, scores * ATT_SCALE + bias, NEG_INF)
    e = jnp.exp(z - jnp.max(z, axis=-1, keepdims=True))
    return e / jnp.sum(e, axis=-1, keepdims=True)


CH_HEADS = [(pair, h) for pair in range(N_HEADS // 2) for h in range(2)]
CH_COLS = [slice(pair * PAIR, (pair + 1) * PAIR) for pair in range(N_HEADS // 2)]


def _ch_valid(n):
    slot = lax.broadcasted_iota(jnp.int32, (CHUNK, BAND), 1) // CHUNK
    return n + slot - LOOKBACK >= 0


def _ch_fwd(qkv, bias, name):
    T = qkv.shape[0]
    W = N_HEADS * HEAD_DIM

    def body(q_ref, k_ref, v_ref, b_ref, o_ref, kp, vp):
        n = pl.program_id(0)

        @pl.when(n == 0)
        def _():
            _ch_load_padded(k_ref, v_ref, kp, vp)

        win = pl.ds(pl.multiple_of(n * CHUNK, CHUNK), BAND)
        valid = _ch_valid(n)
        lanes = _pair_lanes()
        scores = [_dot(_only(lanes[h], q_ref[:, CH_COLS[pair]]), kp[win, CH_COLS[pair]], NT) for pair, h in CH_HEADS]
        probs = [_ch_probs(s, b_ref[2 * pair + h], valid).astype(BF16) for s, (pair, h) in zip(scores, CH_HEADS)]
        outs = [_dot(p, _only(lanes[h], vp[win, CH_COLS[pair]])) for p, (pair, h) in zip(probs, CH_HEADS)]
        for pair, cols in enumerate(CH_COLS):
            o_ref[:, cols] = outs[2 * pair] + outs[2 * pair + 1]

    full = lambda col: pl.BlockSpec((T, W), lambda n: (0, col))
    return pl.pallas_call(
        body, name=name, grid=(T // CHUNK,),
        in_specs=[pl.BlockSpec((CHUNK, W), lambda n: (n, 3)), full(4), full(5),
                  pl.BlockSpec((N_HEADS, CHUNK, BAND), lambda n: (0, 0, 0))],
        out_specs=pl.BlockSpec((CHUNK, W), lambda n: (n, 0)),
        out_shape=jax.ShapeDtypeStruct((T, W), F32),
        scratch_shapes=[pltpu.VMEM((PAD + T, W), BF16)] * 2,
        compiler_params=_params(1, VMEM_LIMIT),
    )(qkv, qkv, qkv, bias)


def _ch_load_padded(k_ref, v_ref, kp, vp):
    for src, dst in ((k_ref, kp), (v_ref, vp)):
        dst[:PAD, :] = jnp.zeros((PAD, dst.shape[1]), dst.dtype)
        dst[PAD:, :] = src[...]


def _ch_bwd(qkv, bias, do, name):
    T = qkv.shape[0]
    W = N_HEADS * HEAD_DIM
    n_chunks = T // CHUNK

    def body(q_ref, k_ref, v_ref, b_ref, do_ref, dq_ref, dk_ref, dv_ref, db_ref, kp, vp, dk_s, dv_s):
        n = pl.program_id(0)

        @pl.when(n == 0)
        def _():
            _ch_load_padded(k_ref, v_ref, kp, vp)
            dk_s[...] = jnp.zeros_like(dk_s)
            dv_s[...] = jnp.zeros_like(dv_s)
            db_ref[...] = jnp.zeros_like(db_ref)

        win = pl.ds(pl.multiple_of(n * CHUNK, CHUNK), BAND)
        valid = _ch_valid(n)
        lanes = _pair_lanes()
        kws = [kp[win, cols] for cols in CH_COLS]
        vws = [vp[win, cols] for cols in CH_COLS]
        qs = [_only(lanes[h], q_ref[:, CH_COLS[pair]]) for pair, h in CH_HEADS]
        dos = [_only(lanes[h], do_ref[:, CH_COLS[pair]].astype(BF16)) for pair, h in CH_HEADS]
        scores = [_dot(q, kws[pair], NT) for q, (pair, _) in zip(qs, CH_HEADS)]
        dps = [_dot(do, vws[pair], NT) for do, (pair, _) in zip(dos, CH_HEADS)]
        probs = [_ch_probs(s, b_ref[2 * pair + h], valid) for s, (pair, h) in zip(scores, CH_HEADS)]
        dzs = [p * (dp - jnp.sum(dp * p, axis=-1, keepdims=True)) for p, dp in zip(probs, dps)]
        for k, dz in enumerate(dzs):
            db_ref[k] += dz
        dzbs = [(dz * ATT_SCALE).astype(BF16) for dz in dzs]
        dqs = [_dot(dz, _only(lanes[h], kws[pair])) for dz, (pair, h) in zip(dzbs, CH_HEADS)]
        dks = [_dot(dz, q, TN) for dz, q in zip(dzbs, qs)]
        dvs = [_dot(p.astype(BF16), do, TN) for p, do in zip(probs, dos)]
        for pair, cols in enumerate(CH_COLS):
            dq_ref[:, cols] = (dqs[2 * pair] + dqs[2 * pair + 1]).astype(BF16)
            dk_s[win, cols] += dks[2 * pair] + dks[2 * pair + 1]
            dv_s[win, cols] += dvs[2 * pair] + dvs[2 * pair + 1]

        @pl.when(n == n_chunks - 1)
        def _():
            dk_ref[...] = dk_s[PAD:, :].astype(BF16)
            dv_ref[...] = dv_s[PAD:, :].astype(BF16)

    full = lambda col: pl.BlockSpec((T, W), lambda n: (0, col))
    blk = lambda col: pl.BlockSpec((CHUNK, W), lambda n: (n, col))
    tab = pl.BlockSpec((N_HEADS, CHUNK, BAND), lambda n: (0, 0, 0))
    out = jax.ShapeDtypeStruct((T, W), BF16)
    return pl.pallas_call(
        body, name=name, grid=(n_chunks,),
        in_specs=[blk(3), full(4), full(5), tab, blk(0)],
        out_specs=[blk(0), full(0), full(0), tab],
        out_shape=[out, out, out, jax.ShapeDtypeStruct((N_HEADS, CHUNK, BAND), F32)],
        scratch_shapes=[pltpu.VMEM((PAD + T, W), BF16)] * 2 + [pltpu.VMEM((PAD + T, W), F32)] * 2,
        compiler_params=_params(1, VMEM_LIMIT),
    )(qkv, qkv, qkv, bias, do)


def _rows_split(a, parts):
    return a.reshape(a.shape[:-2] + (parts, a.shape[-2] // parts, a.shape[-1]))


def _cast_into_slot0(c, ws, name):
    parts = 2
    ws = [_rows_split(_rows_split(w, 2), parts) for w in ws]
    n = len(ws)

    def body(c_ref, *refs):
        for src, dst in zip(refs[:n], refs[n:]):
            dst[0, 0, 0] = src[0, 0].astype(BF16)

    outs = pl.pallas_call(
        body, name=name,
        grid_spec=pltpu.PrefetchScalarGridSpec(
            num_scalar_prefetch=1, grid=(2, parts),
            in_specs=[pl.BlockSpec((1, 1) + w.shape[2:], lambda d, r, c_ref: (d ^ c_ref[0], r, 0, 0)) for w in ws],
            out_specs=[pl.BlockSpec((1, 1, 1) + w.shape[2:], lambda d, r, c_ref: (0, d, r, 0, 0)) for w in ws]),
        out_shape=[jax.ShapeDtypeStruct((N_CHIPS,) + w.shape, BF16) for w in ws],
        compiler_params=_params(2, VMEM_LIMIT),
    )(c, *ws)
    return [o.reshape(N_CHIPS, 2, o.shape[2] * o.shape[3], o.shape[4]) for o in outs]


def _chip_order(me, c, lands, name):
    parts = 2
    xs = [_rows_split(x, parts) for x in lands]

    def body(me_ref, c_ref, *refs):
        n = len(refs) // 2
        for src, dst in zip(refs[:n], refs[n:]):
            dst[...] = src[...]

    outs = pl.pallas_call(
        body, name=name,
        grid_spec=pltpu.PrefetchScalarGridSpec(
            num_scalar_prefetch=2, grid=(N_CHIPS, 2, parts),
            in_specs=[pl.BlockSpec((1, 1, 1) + x.shape[3:],
                                   lambda j, h, r, me_ref, c_ref: (j ^ me_ref[0], h ^ c_ref[0], r, 0, 0)) for x in xs],
            out_specs=[pl.BlockSpec((1, 1, 1) + x.shape[3:], lambda j, h, r, me_ref, c_ref: (j, h, r, 0, 0))
                       for x in xs]),
        out_shape=[jax.ShapeDtypeStruct(x.shape, x.dtype) for x in xs],
        compiler_params=_params(3, VMEM_LIMIT),
    )(me, c, *xs)
    return [o.reshape(o.shape[0], 2 * parts * o.shape[3], o.shape[4]) for o in outs]


def _pair_add(c, mine, got, permuted, name):
    parts = 2
    mine = [_rows_split(m, parts) for m in mine]
    got = [_rows_split(g, parts) for g in got]
    n = len(mine)

    def body(c_ref, *refs):
        for a, b, o in zip(refs[:n], refs[n:2 * n], refs[2 * n:]):
            o[0, 0] = (a[0, 0, 0] + b[0, 0]).astype(BF16)

    def mine_spec(m, perm):
        if perm:
            return pl.BlockSpec((1, 1, 1) + m.shape[3:], lambda j, r, c_ref: (j, 0, r, 0, 0))
        return pl.BlockSpec((1, 1, 1) + m.shape[3:], lambda j, r, c_ref: (j, c_ref[0], r, 0, 0))

    def got_spec(g):
        return pl.BlockSpec((1, 1) + g.shape[2:], lambda j, r, c_ref: (j, r, 0, 0))

    outs = pl.pallas_call(
        body, name=name,
        grid_spec=pltpu.PrefetchScalarGridSpec(
            num_scalar_prefetch=1, grid=(N_CHIPS, parts),
            in_specs=[mine_spec(m, perm) for m, perm in zip(mine, permuted)] + [got_spec(g) for g in got],
            out_specs=[got_spec(g) for g in got]),
        out_shape=[jax.ShapeDtypeStruct(g.shape, BF16) for g in got],
        compiler_params=_params(2, VMEM_LIMIT),
    )(c, *mine, *got)
    return [o.reshape(o.shape[0], o.shape[1] * o.shape[2], o.shape[3]) for o in outs]


def _chip_add(me, partials, landed, permuted, name):
    parts = 2
    ps = [_rows_split(x, parts) for x in partials]
    ls = [_rows_split(x, parts) for x in landed]
    n = len(ps)

    def body(me_ref, *refs):
        for own, got, o in zip(refs[:n], refs[n:2 * n], refs[2 * n:]):
            acc = own[0, 0].astype(F32)
            for r in range(N_CHIPS - 1):
                acc = acc + got[r, 0].astype(F32)
            o[0] = acc

    def own_spec(x, perm):
        if perm:
            return pl.BlockSpec((1, 1) + x.shape[2:], lambda r, me_ref: (0, r, 0, 0))
        return pl.BlockSpec((1, 1) + x.shape[2:], lambda r, me_ref: (me_ref[0], r, 0, 0))

    outs = pl.pallas_call(
        body, name=name,
        grid_spec=pltpu.PrefetchScalarGridSpec(
            num_scalar_prefetch=1, grid=(parts,),
            in_specs=[own_spec(x, perm) for x, perm in zip(ps, permuted)]
            + [pl.BlockSpec((N_CHIPS - 1, 1) + x.shape[2:], lambda r, me_ref: (0, r, 0, 0)) for x in ls],
            out_specs=[pl.BlockSpec((1,) + x.shape[2:], lambda r, me_ref: (r, 0, 0)) for x in ps]),
        out_shape=[jax.ShapeDtypeStruct(x.shape[1:], F32) for x in ps],
        compiler_params=_params(1, VMEM_LIMIT),
    )(me, *ps, *ls)
    return [o.reshape(o.shape[0] * o.shape[1], o.shape[2]) for o in outs]


def _adamw_math(w, g, m, v):
    m = ADAM_B1 * m + (1.0 - ADAM_B1) * g
    v = ADAM_B2 * v + (1.0 - ADAM_B2) * (g * g)
    m_hat = m / (1.0 - ADAM_B1 ** ADAM_STEP)
    v_hat = v / (1.0 - ADAM_B2 ** ADAM_STEP)
    delta = -ADAM_LR * (m_hat / (jnp.sqrt(v_hat) + ADAM_EPS) + ADAM_WD * w)
    return delta, m, v


def _adamw(ws, gs, ms, vs, parts, name):
    n = len(ws)
    flat = [_rows_split(a, parts) for a in (*ws, *gs, *ms, *vs)]

    def body(*refs):
        ins, outs = refs[:4 * n], refs[4 * n:]
        for k in range(n):
            d, m, v = _adamw_math(ins[k][...], ins[n + k][...], ins[2 * n + k][...], ins[3 * n + k][...])
            outs[k][...] = d
            outs[n + k][...] = m
            outs[2 * n + k][...] = v

    spec = lambda a: pl.BlockSpec((1,) + a.shape[1:], lambda i: (i, 0, 0))
    outs = pl.pallas_call(
        body, name=name, grid=(parts,),
        in_specs=[spec(a) for a in flat], out_specs=[spec(a) for a in flat[:n]] * 3,
        out_shape=[jax.ShapeDtypeStruct(a.shape, F32) for a in flat[:n]] * 3,
        compiler_params=_params(1, VMEM_LIMIT),
    )(*flat)
    outs = [o.reshape(o.shape[0] * o.shape[1], o.shape[2]) for o in outs]
    return outs[:n], outs[n:2 * n], outs[2 * n:]


def _place():
    x, y, c = lax.axis_index("x"), lax.axis_index("y"), lax.axis_index("c")
    peers = [(x ^ (r >> 1), y ^ (r & 1), c) for r in (1, 2, 3)]
    return x, y, c, peers


def _handshake(peers):
    barrier = pltpu.get_barrier_semaphore()
    for peer in peers:
        pl.semaphore_signal(barrier, inc=1, device_id=peer, device_id_type=MESH)
    pl.semaphore_wait(barrier, len(peers))


ANY = pl.BlockSpec(memory_space=pl.ANY)
HBM = pl.BlockSpec(memory_space=pltpu.HBM)
SEM = pl.BlockSpec(memory_space=pltpu.SEMAPHORE)
SPLIT_COPY = pltpu.SideEffectType.DATAFLOW_SIDE_EFFECTING


def _in_hbm(a):
    return pltpu.with_memory_space_constraint(a, pltpu.HBM)


def _split_start(body, name, collective_id, operands, n_sems, after=None):
    n = len(operands)
    extra = [] if after is None else [after]

    def wrapped(*refs):
        at = n + len(extra)
        body(refs[:n], refs[at], refs[at + 1])
        token = refs[-1]
        token[...] = jnp.zeros_like(token)

    outs = pl.pallas_call(
        wrapped, name=name,
        in_specs=[HBM] * n + [ANY] * len(extra),
        out_shape=(pltpu.SemaphoreType.DMA((n_sems,)), pltpu.SemaphoreType.DMA((n_sems,)),
                   *[pltpu.HBM(a.shape, a.dtype) for a in operands], jax.ShapeDtypeStruct((8, 128), F32)),
        out_specs=(SEM, SEM, *[HBM] * n, pl.BlockSpec(memory_space=pltpu.VMEM)),
        input_output_aliases={i: 2 + i for i in range(n)},
        compiler_params=pltpu.CompilerParams(has_side_effects=SPLIT_COPY, collective_id=collective_id),
    )(*[_in_hbm(a) for a in operands], *extra)
    return outs[0], outs[1], list(outs[2:2 + n]), outs[-1]


def _split_wait(body, name, send_sem, recv_sem, operands, after):
    n = len(operands)

    def wrapped(*refs):
        body(refs[:n], refs[n], refs[n + 1])

    outs = pl.pallas_call(
        wrapped, name=name,
        in_specs=[HBM] * n + [SEM, SEM, ANY],
        out_shape=tuple(pltpu.HBM(a.shape, a.dtype) for a in operands),
        out_specs=tuple([HBM] * n),
        input_output_aliases={i: i for i in range(n)},
        compiler_params=pltpu.CompilerParams(has_side_effects=SPLIT_COPY),
    )(*operands, send_sem, recv_sem, after)
    return list(outs)


def _gather_copies(lands, send_sem, recv_sem):
    peers = _place()[3]
    return [pltpu.make_async_remote_copy(
        src_ref=land.at[0, 0], dst_ref=land.at[r + 1, 0],
        send_sem=send_sem.at[a * 3 + r], recv_sem=recv_sem.at[a * 3 + r],
        device_id=peers[r], device_id_type=MESH) for a, land in enumerate(lands) for r in range(3)]


def _gather_start(lands, name, collective_id, after):
    def body(refs, send_sem, recv_sem):
        _handshake(_place()[3])
        for cp in _gather_copies(refs, send_sem, recv_sem):
            cp.start()

    return _split_start(body, name, collective_id, list(lands), 3 * len(lands), after)


def _gather_wait(send_sem, recv_sem, operands, after, name):
    def body(refs, send_sem, recv_sem):
        for cp in _gather_copies(refs, send_sem, recv_sem):
            cp.wait_send()
            cp.wait_recv()

    return _split_wait(body, name, send_sem, recv_sem, operands, after)


def _gather_finish(lands, with_ici, name):
    n = len(lands)

    def body(*refs):
        land = refs[n:2 * n]
        send_ici, recv_ici, send_d2d, recv_d2d = refs[2 * n:]
        x, y, c, _ = _place()
        ici = _gather_copies(land, send_ici, recv_ici) if with_ici else []
        for cp in ici:
            cp.start()
        passed = [pltpu.make_async_remote_copy(
            src_ref=land[a].at[r + 1, 0], dst_ref=land[a].at[r + 1, 1],
            send_sem=send_d2d.at[a * 3 + r], recv_sem=recv_d2d.at[a * 3 + r],
            device_id=(x, y, 1 - c), device_id_type=MESH) for a in range(n) for r in range(3)]
        for k, cp in enumerate(passed):
            if with_ici:
                ici[k].wait_recv()
            cp.start()
        for cp in passed:
            cp.wait_recv()
        for cp in ici:
            cp.wait_send()
        for cp in passed:
            cp.wait_send()

    outs = pl.pallas_call(
        body, name=name,
        in_specs=[ANY] * n, out_specs=[ANY] * n,
        out_shape=[jax.ShapeDtypeStruct(l.shape, l.dtype) for l in lands],
        input_output_aliases={a: a for a in range(n)},
        scratch_shapes=[pltpu.SemaphoreType.DMA((3 * n,))] * 4,
    )(*lands)
    return list(outs)


def _slabs(land):
    return land.reshape(N_CHIPS, 2 * land.shape[2], land.shape[3])


def _pair_swap(grads, permuted, name):
    n = len(grads)

    def body(*refs):
        src, dst = refs[:n], refs[n:2 * n]
        send_sem, recv_sem = refs[2 * n:]
        x, y, c, _ = _place()
        copies = [pltpu.make_async_remote_copy(
            src_ref=src[a].at[:, 1] if permuted[a] else src[a].at[:, 1 - c], dst_ref=dst[a],
            send_sem=send_sem.at[a], recv_sem=recv_sem.at[a],
            device_id=(x, y, 1 - c), device_id_type=MESH) for a in range(n)]
        for cp in copies:
            cp.start()
        for cp in copies:
            cp.wait()

    return pl.pallas_call(
        body, name=name,
        in_specs=[ANY] * n, out_specs=[ANY] * n,
        out_shape=[jax.ShapeDtypeStruct((N_CHIPS,) + g.shape[2:], F32) for g in grads],
        scratch_shapes=[pltpu.SemaphoreType.DMA((n,))] * 2,
    )(*grads)


def _scatter_copies(refs, permuted, send_sem, recv_sem):
    n = len(refs) // 2
    x, y, _, peers = _place()
    me = 2 * x + y
    return [pltpu.make_async_remote_copy(
        src_ref=refs[a].at[r + 1] if permuted[a] else refs[a].at[me ^ (r + 1)], dst_ref=refs[n + a].at[r],
        send_sem=send_sem.at[a * 3 + r], recv_sem=recv_sem.at[a * 3 + r],
        device_id=peers[r], device_id_type=MESH) for a in range(n) for r in range(3)]


def _scatter_start(partials, permuted, name, collective_id):
    def body(refs, send_sem, recv_sem):
        _handshake(_place()[3])
        for cp in _scatter_copies(refs, permuted, send_sem, recv_sem):
            cp.start()

    lands = [lax.empty((N_CHIPS - 1,) + p.shape[1:], p.dtype) for p in partials]
    return _split_start(body, name, collective_id, list(partials) + lands, 3 * len(partials))


def _scatter_wait(send_sem, recv_sem, operands, permuted, after, name):
    def body(refs, send_sem, recv_sem):
        for cp in _scatter_copies(refs, permuted, send_sem, recv_sem):
            cp.wait_send()
            cp.wait_recv()

    return _split_wait(body, name, send_sem, recv_sem, operands, after)


def _pair_join(halves, name):
    n = len(halves)

    def body(*refs):
        src, dst = refs[:n], refs[n:2 * n]
        send_sem, recv_sem = refs[2 * n:]
        x, y, c, _ = _place()
        copies = [pltpu.make_async_remote_copy(
            src_ref=src[a], dst_ref=dst[a], send_sem=send_sem.at[a], recv_sem=recv_sem.at[a],
            device_id=(x, y, 1 - c), device_id_type=MESH) for a in range(n)]
        for cp in copies:
            cp.start()
        for cp in copies:
            cp.wait()

    return pl.pallas_call(
        body, name=name,
        in_specs=[ANY] * n, out_specs=[ANY] * n,
        out_shape=[jax.ShapeDtypeStruct(h.shape, F32) for h in halves],
        scratch_shapes=[pltpu.SemaphoreType.DMA((n,))] * 2,
    )(*halves)


def _all_sum_small(v, name):
    R, C = v.shape
    n_dev = 8

    def body(v_ref, o_ref, buf, send_sem, recv_sem):
        x, y, c, _ = _place()
        me = 4 * x + 2 * y + c
        buf[me] = v_ref[...]
        copies = []
        for k in range(1, n_dev):
            peer = (x ^ (k >> 2), y ^ ((k >> 1) & 1), c ^ (k & 1))
            copies.append(pltpu.make_async_remote_copy(
                src_ref=v_ref, dst_ref=buf.at[me], send_sem=send_sem.at[k - 1], recv_sem=recv_sem.at[k - 1],
                device_id=peer, device_id_type=MESH))
        for cp in copies:
            cp.start()
        for cp in copies:
            cp.wait()
        acc = buf[0]
        for m in range(1, n_dev):
            acc = acc + buf[m]
        o_ref[...] = acc

    return pl.pallas_call(
        body, name=name,
        in_specs=[pl.BlockSpec(memory_space=pltpu.VMEM)], out_specs=pl.BlockSpec(memory_space=pltpu.VMEM),
        out_shape=jax.ShapeDtypeStruct((R, C), F32),
        scratch_shapes=[pltpu.VMEM((n_dev, R, C), F32), pltpu.SemaphoreType.DMA((n_dev - 1,)),
                        pltpu.SemaphoreType.DMA((n_dev - 1,))],
    )(v)


class _WholeWeights:
    def __init__(self, w):
        self.w = w

    def weights(self, group, after=None):
        return self.w, None

    def grads_ready(self, group, gw):
        return None


def _local_step(x, p, target, gains, rel_bias, hooks):
    T, D = x.shape
    S = N_CHIPS

    tied = lambda gain, token: gain if token is None else gain + token[0, 0]
    w, token = hooks.weights("first")
    w = dict(w)
    h1, xn1, g1, u1, a1, f1 = _ffn_fwd(x, tied(gains["ffn1_pre"], token), gains["ffn1_post"], w["ffn1_gate"],
                                       w["ffn1_up"], w["ffn1_down"], "ffn1_fwd")
    more, token = hooks.weights("in", h1)
    w.update(more)
    qkv, un = _norm_proj(h1, tied(gains["mix_pre"], token), w["in"], "qkv_proj")
    bias = _bias_table(rel_bias, "bias_table").transpose(1, 0, 2)
    o_a, tot = _sb_fwd(qkv, "sb_fwd")
    o_b = _ch_fwd(qkv, bias, "ch_fwd")
    w.update(hooks.weights("rest", o_b)[0])
    w_out = w["out"].reshape(D, D)
    h2, mixed, mo = _mix_out_fwd(h1, o_a, o_b, gains["out_sb"], gains["out_ch"], w_out, gains["mix_post"],
                                 "mix_out_fwd")
    h3, xn2, g2, u2, a2, f2 = _ffn_fwd(h2, gains["ffn2_pre"], gains["ffn2_post"], w["ffn2_gate"], w["ffn2_up"],
                                       w["ffn2_down"], "ffn2_fwd")
    w_ple_proj = w["ple_proj"].transpose(1, 0, 2).reshape(p.shape[1], D)
    w_ple_gate = w["ple_gate"].reshape(D, D)

    loss, dh3, dproj, dgate, dg_ple = _ple_loss(h3, p, target, w_ple_proj, w_ple_gate, gains["ple_post"], "ple_loss")
    gw, gg = {}, {"ple_post": dg_ple}
    gw["ple_proj"] = _mm_tn(p[None], dproj, p.shape[1], "dw_ple_proj")
    gw["ple_gate"] = _mm_tn(h3[None], dgate[None], 512, "dw_ple_gate").reshape(S, D // S, D)

    def ffn_bwd(tag, dh, x_in, xn, g_act, u_act, a_act, f, group):
        dgp, dup, df, gg[tag + "_post"] = _ffn_bwd_act(dh, f, gains[tag + "_post"], w[tag + "_down"], g_act, u_act,
                                                       tag + "_bwd_act")
        gw[tag + "_gate"] = _mm_tn(dgp, xn[None], dgp.shape[2], "dw_" + tag + "_gate")
        gw[tag + "_up"] = _mm_tn(dup, xn[None], dup.shape[2], "dw_" + tag + "_up")
        gw[tag + "_down"] = _mm_tn(a_act, df[None], a_act.shape[2], "dw_" + tag + "_down")
        g_pre = gains[tag + "_pre"]
        if group is not None:
            token = hooks.grads_ready(group, gw)
            g_pre = g_pre if token is None else g_pre + token[0, 0]
        dx, gg[tag + "_pre"] = _proj_bwd([dgp, dup], [w[tag + "_gate"], w[tag + "_up"]], x_in, g_pre, dh,
                                         tag + "_bwd_in")
        return dx

    dh2 = ffn_bwd("ffn2", dh3, h2, xn2, g2, u2, a2, f2, None)
    dmo, do_a, do_b, gg["mix_post"], gg["out_sb"], gg["out_ch"] = _mix_out_bwd(
        dh2, mo, gains["mix_post"], w_out, o_a, o_b, gains["out_sb"], gains["out_ch"], "mix_out_bwd")
    gw["out"] = _mm_tn(mixed[None], dmo[None], 512, "dw_out").reshape(S, D // S, D)
    token = hooks.grads_ready("early", gw)
    if token is not None:
        tot = tot + token[0, 0]
    dq_a, dk_a, dv_a = _sb_bwd(qkv, do_a, tot, "sb_bwd")
    dq_b, dk_b, dv_b, dbias = _ch_bwd(qkv, bias, do_b, "ch_bwd")
    g_rel = _bias_grad(dbias.transpose(1, 0, 2), "bias_grad")
    dqkv = jnp.concatenate([dq_a, dk_a, dv_a, dq_b, dk_b, dv_b], axis=1)
    gw["in"] = _mm_tn(un[None], dqkv, 512, "dw_in", groups=S)
    dh1, gg["mix_pre"] = _proj_bwd([dqkv], [w["in"]], h1, gains["mix_pre"], dh2, "qkv_bwd_in")
    dx = ffn_bwd("ffn1", dh1, x, xn1, g1, u1, a1, f1, "late")
    return loss, dx, gw, gg, g_rel


BIG = ["ffn1_gate", "ffn1_up", "ffn1_down", "in", "out", "ffn2_gate", "ffn2_up", "ffn2_down", "ple_proj", "ple_gate"]
GAINS = ["ffn1_pre", "ffn1_post", "mix_pre", "mix_post", "out_sb", "out_ch", "ffn2_pre", "ffn2_post", "ple_post"]
TRANSPOSED = ("w_ffn1_gate", "w_ffn1_up", "w_ffn2_gate", "w_ffn2_up")
PERMUTED = ("ffn1_gate", "ffn1_up", "ffn1_down", "ffn2_gate", "ffn2_up", "ffn2_down")
W_GROUPS = {"first": ["ffn1_gate", "ffn1_up", "ffn1_down"], "in": ["in"],
            "rest": ["out", "ffn2_gate", "ffn2_up", "ffn2_down", "ple_proj", "ple_gate"]}
G_GROUPS = {"early": ["ple_proj", "ple_gate", "ffn2_gate", "ffn2_up", "ffn2_down", "out"],
            "late": ["in", "ffn1_gate", "ffn1_up", "ffn1_down"]}
ORDER = ["g_ffn1_pre", "g_ffn1_post", "w_ffn1_gate", "w_ffn1_up", "w_ffn1_down", "g_mix_pre", "g_mix_post", "w_in",
         "g_out_sb", "g_out_ch", "rel_bias", "w_out", "g_ffn2_pre", "g_ffn2_post", "w_ffn2_gate", "w_ffn2_up",
         "w_ffn2_down", "w_ple_proj", "w_ple_gate", "g_ple_post"]


def kernel(x, p, g_ffn1_pre, g_ffn1_post, w_ffn1_gate, w_ffn1_up, w_ffn1_down, g_mix_pre, g_mix_post, w_in, g_out_sb, g_out_ch, rel_bias, w_out, g_ffn2_pre, g_ffn2_post, w_ffn2_gate, w_ffn2_up, w_ffn2_down, w_ple_proj, w_ple_gate, g_ple_post, loss_target, m_g_ffn1_pre, m_g_ffn1_post, m_w_ffn1_gate, m_w_ffn1_up, m_w_ffn1_down, m_g_mix_pre, m_g_mix_post, m_w_in, m_g_out_sb, m_g_out_ch, m_rel_bias, m_w_out, m_g_ffn2_pre, m_g_ffn2_post, m_w_ffn2_gate, m_w_ffn2_up, m_w_ffn2_down, m_w_ple_proj, m_w_ple_gate, m_g_ple_post, v_g_ffn1_pre, v_g_ffn1_post, v_w_ffn1_gate, v_w_ffn1_up, v_w_ffn1_down, v_g_mix_pre, v_g_mix_post, v_w_in, v_g_out_sb, v_g_out_ch, v_rel_bias, v_w_out, v_g_ffn2_pre, v_g_ffn2_post, v_w_ffn2_gate, v_w_ffn2_up, v_w_ffn2_down, v_w_ple_proj, v_w_ple_gate, v_g_ple_post):
    args = dict(locals())
    take = lambda a, n: a[0].T if n in TRANSPOSED else a[0]
    wts = {n: take(args[n], n) for n in ORDER}
    ms = {n: take(args["m_" + n], n) for n in ORDER}
    vs = {n: take(args["v_" + n], n) for n in ORDER}
    gains = {n: wts["g_" + n][None] for n in GAINS}

    c_idx = lax.axis_index("c").astype(jnp.int32).reshape(1)
    me_idx = (2 * lax.axis_index("x") + lax.axis_index("y")).astype(jnp.int32).reshape(1)
    south = lax.axis_index("c") == 0

    lands = dict(zip(BIG, _cast_into_slot0(c_idx, [wts["w_" + n] for n in BIG], "cast_weights")))

    def in_order(names, zones):
        plain = [n for n in names if n not in PERMUTED]
        fixed = dict(zip(plain, _chip_order(me_idx, c_idx, [zones[n] for n in plain], "chip_order_" + plain[0]))
                     ) if plain else {}
        return {n: fixed[n] if n in fixed else _slabs(zones[n]) for n in names}

    class Overlapped:
        def __init__(self):
            self.started = {}
            self.flying = None

        def start(self, group, collective_id, after):
            self.flying = _gather_start([lands[n] for n in W_GROUPS[group]], "gather_%s_start" % group,
                                        collective_id, after)
            return self.flying[3]

        def weights(self, group, after=None):
            names = W_GROUPS[group]
            if group == "first":
                zones = _gather_finish([lands[n] for n in names], True, "gather_first")
                return in_order(names, dict(zip(names, zones))), self.start("in", 1, zones[0])
            send_sem, recv_sem, zones, _ = self.flying
            zones = _gather_wait(send_sem, recv_sem, zones, after, "gather_%s_wait" % group)
            zones = _gather_finish(zones, False, "gather_%s_finish" % group)
            token = self.start("rest", 4, zones[0]) if group == "in" else None
            return in_order(names, dict(zip(names, zones))), token

        def grads_ready(self, group, gw):
            self.started[group] = reduce_start(G_GROUPS[group], gw, group, {"early": 2, "late": 3}[group])
            return self.started[group][-1]

    def reduce_start(names, gw, tag, cid):
        perm = [n in PERMUTED for n in names]
        mine = [gw[n].reshape(N_CHIPS, 2, gw[n].shape[1] // 2, gw[n].shape[2]) for n in names]
        got = _pair_swap(mine, perm, "grad_pair_swap_" + tag)
        partial = _pair_add(c_idx, mine, got, perm, "grad_pair_add_" + tag)
        send_sem, recv_sem, operands, token = _scatter_start(partial, perm, "grad_scatter_start_" + tag, cid)
        return names, perm, send_sem, recv_sem, operands, token

    def reduce_finish(state, after, tag):
        names, perm, send_sem, recv_sem, operands, _ = state
        operands = _scatter_wait(send_sem, recv_sem, operands, perm, after, "grad_scatter_wait_" + tag)
        n = len(names)
        halves = _chip_add(me_idx, operands[:n], operands[n:], perm, "grad_chip_add_" + tag)
        out = {}
        for name, own, other in zip(names, halves, _pair_join(halves, "grad_pair_join_" + tag)):
            out["w_" + name] = jnp.concatenate([jnp.where(south, own, other), jnp.where(south, other, own)], axis=0)
        return out

    hooks = Overlapped()
    loss, dx, gw, gg, g_rel = _local_step(x[0], p[0, 0], loss_target[0], gains, wts["rel_bias"], hooks)

    late = hooks.started["late"]
    grads = reduce_finish(hooks.started["early"], dx, "early")

    pieces = [gg[n].reshape(-1, 128) for n in GAINS] + [jnp.pad(g_rel, ((0, 0), (0, N_REL_PAD - N_REL))).reshape(-1, 128)]
    summed = _all_sum_small(jnp.concatenate(pieces, axis=0), "small_grad_sum")
    at = 0
    for n, piece in zip(GAINS, pieces[:-1]):
        grads["g_" + n] = summed[at:at + piece.shape[0]].reshape(1, -1)[0]
        at += piece.shape[0]
    grads["rel_bias"] = summed[at:].reshape(N_HEADS, N_REL_PAD)[:, :N_REL]

    delta, new_m, new_v = {}, {}, {}

    def adamw_big(group):
        names = ["w_" + n for n in G_GROUPS[group]]
        d, m, v = _adamw([wts[n] for n in names], [grads[n] for n in names], [ms[n] for n in names],
                         [vs[n] for n in names], 8, "adamw_" + group)
        for n, dd, mm, vv in zip(names, d, m, v):
            delta[n], new_m[n], new_v[n] = dd, mm, vv
        return d[0]

    done_early = adamw_big("early")
    grads.update(reduce_finish(late, done_early, "late"))
    adamw_big("late")
    small = ["g_" + n for n in GAINS] + ["rel_bias"]
    as_rows = lambda a: (a.reshape(-1, 128) if a.size % 128 == 0 else jnp.pad(a, ((0, 0), (0, N_REL_PAD - N_REL))).reshape(-1, 128))
    d, m, v = _adamw([as_rows(wts[n]) for n in small], [as_rows(grads[n]) for n in small],
                     [as_rows(ms[n]) for n in small], [as_rows(vs[n]) for n in small], 1, "adamw_small")
    for n, dd, mm, vv in zip(small, d, m, v):
        back = (lambda a: a.reshape(N_HEADS, N_REL_PAD)[:, :N_REL]) if n == "rel_bias" else (lambda a: a.reshape(-1))
        delta[n], new_m[n], new_v[n] = back(dd), back(mm), back(vv)

    loss = lax.psum(loss[0, 0], ("x", "y", "c"))
    outs = [loss, dx[None]]
    for table in (grads, delta, new_m, new_v):
        outs += [(table[n].T if n in TRANSPOSED else table[n])[None] for n in ORDER]
    return tuple(outs)
```

```python
import functools

import jax
import jax.numpy as jnp
from jax import lax
from jax.experimental import pallas as pl
from jax.experimental.pallas import tpu as pltpu

F32 = jnp.float32
BF16 = jnp.bfloat16
EPS = 1e-6
N_CHIPS = 4
HEAD_DIM = 64
N_HEADS = 8
CHUNK = 64
LOOKBACK = 8
BAND = (LOOKBACK + 1) * CHUNK
PAD = LOOKBACK * CHUNK
REL_CLIP = 128
N_REL = 2 * REL_CLIP + 1
N_REL_PAD = 384
SB_BLOCK = 256
PAIR = 2 * HEAD_DIM
SB_PAIRS = 2
ATT_SCALE = HEAD_DIM ** -0.5
NEG_INF = -1e30
ROW_BLOCK = 512
VMEM_LIMIT = 48 * 1024 * 1024
MESH = pl.DeviceIdType.MESH

ADAM_LR = 0.001
ADAM_B1 = 0.9
ADAM_B2 = 0.999
ADAM_EPS = 1e-08
ADAM_WD = 0.01
ADAM_STEP = 10

NT = (((1,), (1,)), ((), ()))
TN = (((0,), (0,)), ((), ()))


def _params(n_grid, vmem=None):
    return pltpu.CompilerParams(dimension_semantics=("arbitrary",) * n_grid, vmem_limit_bytes=vmem)


def _hbm(*arrays):
    return [pltpu.with_memory_space_constraint(a, pltpu.HBM) for a in arrays]


def _dot(a, b, dims=None):
    if dims is None:
        return jnp.dot(a, b, preferred_element_type=F32)
    return lax.dot_general(a, b, dims, preferred_element_type=F32)


def _sigmoid(x):
    return 1.0 / (1.0 + jnp.exp(-x))


def _rms_fwd(x, g):
    r = lax.rsqrt(jnp.mean(x * x, axis=-1, keepdims=True) + EPS)
    return x * r * g


def _rms_bwd(x, g, dy):
    r = lax.rsqrt(jnp.mean(x * x, axis=-1, keepdims=True) + EPS)
    xh = x * r
    dg = jnp.sum(dy * xh, axis=0, keepdims=True)
    t = dy * g
    dx = r * (t - xh * jnp.mean(t * xh, axis=-1, keepdims=True))
    return dx, dg


def _accumulate(ref, val, first):
    @pl.when(first)
    def _():
        ref[...] = val

    @pl.when(jnp.logical_not(first))
    def _():
        ref[...] += val


def _split2(x):
    hi = x.astype(BF16)
    lo = (x - hi.astype(F32)).astype(BF16)
    return hi, lo


def _ffn_fwd(x, g_pre, g_post, wg, wu, wd, name):
    T, D = x.shape
    S, FS, _ = wg.shape
    tm = min(ROW_BLOCK, T)

    def body(x_ref, gpre_ref, gpost_ref, wg_ref, wu_ref, wd_ref,
             h_ref, xn_ref, g_ref, u_ref, a_ref, f_ref, xn_s, acc_s):
        k = pl.program_id(1)

        @pl.when(k == 0)
        def _():
            xn_s[...] = _rms_fwd(x_ref[...], gpre_ref[...]).astype(BF16)
            xn_ref[...] = xn_s[...]

        xn = xn_s[...]
        g = _dot(xn, wg_ref[0], NT)
        u = _dot(xn, wu_ref[0], NT)
        g_ref[0] = g
        u_ref[0] = u
        a = (g * _sigmoid(g) * u).astype(BF16)
        a_ref[0] = a
        _accumulate(acc_s, _dot(a, wd_ref[0]), k == 0)

        @pl.when(k == S - 1)
        def _():
            f = acc_s[...]
            f_ref[...] = f
            h_ref[...] = x_ref[...] + 0.5 * _rms_fwd(f, gpost_ref[...])

    row = pl.BlockSpec((tm, D), lambda i, k: (i, 0))
    vec = pl.BlockSpec((1, D), lambda i, k: (0, 0))
    act = pl.BlockSpec((1, tm, FS), lambda i, k: (k, i, 0))
    return pl.pallas_call(
        body, name=name, grid=(T // tm, S),
        in_specs=[row, vec, vec] + [pl.BlockSpec((1, FS, D), lambda i, k: (k, 0, 0))] * 3,
        out_specs=[row, row, act, act, act, row],
        out_shape=[jax.ShapeDtypeStruct((T, D), F32), jax.ShapeDtypeStruct((T, D), BF16),
                   jax.ShapeDtypeStruct((S, T, FS), F32), jax.ShapeDtypeStruct((S, T, FS), F32),
                   jax.ShapeDtypeStruct((S, T, FS), BF16), jax.ShapeDtypeStruct((T, D), F32)],
        scratch_shapes=[pltpu.VMEM((tm, D), BF16), pltpu.VMEM((tm, D), F32)],
        compiler_params=_params(2, VMEM_LIMIT),
    )(*_hbm(x, g_pre, g_post, wg, wu, wd))


def _ffn_bwd_act(dh, f, g_post, wd, g_act, u_act, name):
    T, D = dh.shape
    S, FS, _ = wd.shape
    tm = min(ROW_BLOCK, T)

    def body(dh_ref, f_ref, gpost_ref, wd_ref, g_ref, u_ref, dgp_ref, dup_ref, df_ref, dgain_ref, df_s):
        i, k = pl.program_id(0), pl.program_id(1)

        @pl.when(k == 0)
        def _():
            df, dgain = _rms_bwd(f_ref[...], gpost_ref[...], 0.5 * dh_ref[...])
            df_s[...] = df.astype(BF16)
            df_ref[...] = df_s[...]
            _accumulate(dgain_ref, dgain, i == 0)

        da = _dot(df_s[...], wd_ref[0], NT)
        g = g_ref[0]
        s = _sigmoid(g)
        dup_ref[0] = (da * (g * s)).astype(BF16)
        dgp_ref[0] = (da * u_ref[0] * (s * (1.0 + g * (1.0 - s)))).astype(BF16)

    row = pl.BlockSpec((tm, D), lambda i, k: (i, 0))
    vec = pl.BlockSpec((1, D), lambda i, k: (0, 0))
    act = pl.BlockSpec((1, tm, FS), lambda i, k: (k, i, 0))
    return pl.pallas_call(
        body, name=name, grid=(T // tm, S),
        in_specs=[row, row, vec, pl.BlockSpec((1, FS, D), lambda i, k: (k, 0, 0)), act, act],
        out_specs=[act, act, row, vec],
        out_shape=[jax.ShapeDtypeStruct((S, T, FS), BF16), jax.ShapeDtypeStruct((S, T, FS), BF16),
                   jax.ShapeDtypeStruct((T, D), BF16), jax.ShapeDtypeStruct((1, D), F32)],
        scratch_shapes=[pltpu.VMEM((tm, D), BF16)],
        compiler_params=_params(2, VMEM_LIMIT),
    )(*_hbm(dh, f, g_post, wd, g_act, u_act))


def _proj_bwd(dys, ws, x, g_pre, dh, name):
    T, D = x.shape
    n = len(dys)
    flat = dys[0].ndim == 2
    S = ws[0].shape[0]
    N = ws[0].shape[2] if flat else ws[0].shape[1]
    tm = min(ROW_BLOCK, T)

    def body(*refs):
        dy_refs, w_refs = refs[:n], refs[n:2 * n]
        x_ref, gpre_ref, dh_ref, dx_ref, dgain_ref, acc_s = refs[2 * n:]
        i, k = pl.program_id(0), pl.program_id(1)
        part = None
        for dy_ref, w_ref in zip(dy_refs, w_refs):
            term = _dot(dy_ref[...], w_ref[0], NT) if flat else _dot(dy_ref[0], w_ref[0])
            part = term if part is None else part + term
        _accumulate(acc_s, part, k == 0)

        @pl.when(k == S - 1)
        def _():
            dx, dgain = _rms_bwd(x_ref[...], gpre_ref[...], acc_s[...])
            dx_ref[...] = dh_ref[...] + dx
            _accumulate(dgain_ref, dgain, i == 0)

    row = pl.BlockSpec((tm, D), lambda i, k: (i, 0))
    vec = pl.BlockSpec((1, D), lambda i, k: (0, 0))
    return pl.pallas_call(
        body, name=name, grid=(T // tm, S),
        in_specs=[pl.BlockSpec((tm, N), lambda i, k: (i, k)) if flat else pl.BlockSpec((1, tm, N), lambda i, k: (k, i, 0))] * n
        + [pl.BlockSpec((1,) + ws[0].shape[1:], lambda i, k: (k, 0, 0))] * n + [row, vec, row],
        out_specs=[row, vec],
        out_shape=[jax.ShapeDtypeStruct((T, D), F32), jax.ShapeDtypeStruct((1, D), F32)],
        scratch_shapes=[pltpu.VMEM((tm, D), F32)],
        compiler_params=_params(2, VMEM_LIMIT),
    )(*_hbm(*dys, *ws, x, g_pre, dh))


def _mm_tn(a, b, bm, name, groups=None):
    ga, T, M = a.shape
    if groups is None:
        gb, _, N = b.shape
        b_spec = pl.BlockSpec((1, T, N), (lambda g, m: (g, 0, 0)) if gb > 1 else (lambda g, m: (0, 0, 0)))
    else:
        gb, N = groups, b.shape[1] // groups
        b_spec = pl.BlockSpec((T, N), lambda g, m: (0, g))
    G = max(ga, gb)

    def body(a_ref, b_ref, o_ref, narrow_ref):
        bv = b_ref[0] if groups is None else b_ref[...]
        o_ref[0] = _dot(a_ref[0].astype(BF16), bv.astype(BF16), TN)
        narrow_ref[0] = o_ref[0].astype(BF16)

    out = pl.BlockSpec((1, bm, N), lambda g, m: (g, m, 0))
    return pl.pallas_call(
        body, name=name, grid=(G, M // bm),
        in_specs=[pl.BlockSpec((1, T, bm), (lambda g, m: (g, 0, m)) if ga > 1 else (lambda g, m: (0, 0, m))), b_spec],
        out_specs=[out, out],
        out_shape=[jax.ShapeDtypeStruct((G, M, N), F32), jax.ShapeDtypeStruct((G, M, N), BF16)],
        compiler_params=_params(2, VMEM_LIMIT),
    )(*_hbm(a, b))


def _norm_proj(x, g_pre, w, name):
    T, D = x.shape
    S, _, N = w.shape
    tm = min(ROW_BLOCK, T)

    def body(x_ref, g_ref, w_ref, o_ref, xn_ref, xn_s):
        @pl.when(pl.program_id(1) == 0)
        def _():
            xn_s[...] = _rms_fwd(x_ref[...], g_ref[...]).astype(BF16)
            xn_ref[...] = xn_s[...]

        o_ref[...] = _dot(xn_s[...], w_ref[0]).astype(BF16)

    row = pl.BlockSpec((tm, D), lambda i, k: (i, 0))
    return pl.pallas_call(
        body, name=name, grid=(T // tm, S),
        in_specs=[row, pl.BlockSpec((1, D), lambda i, k: (0, 0)), pl.BlockSpec((1, D, N), lambda i, k: (k, 0, 0))],
        out_specs=[pl.BlockSpec((tm, N), lambda i, k: (i, k)), row],
        out_shape=[jax.ShapeDtypeStruct((T, S * N), BF16), jax.ShapeDtypeStruct((T, D), BF16)],
        scratch_shapes=[pltpu.VMEM((tm, D), BF16)],
        compiler_params=_params(2, VMEM_LIMIT),
    )(*_hbm(x, g_pre, w))


def _mix_out_fwd(h, o_a, o_b, g_sb, g_ch, w_out, g_post, name):
    T, D = h.shape
    W = g_sb.shape[1]
    tm = min(ROW_BLOCK, T)

    def body(h_ref, oa_ref, ob_ref, gsb_ref, gch_ref, w_ref, gpost_ref, h2_ref, mixed_ref, mo_ref):
        mixed_ref[:, :W] = _rms_fwd(oa_ref[...], gsb_ref[...]).astype(BF16)
        mixed_ref[:, W:] = _rms_fwd(ob_ref[...], gch_ref[...]).astype(BF16)
        mo = _dot(mixed_ref[...], w_ref[...])
        mo_ref[...] = mo
        h2_ref[...] = h_ref[...] + _rms_fwd(mo, gpost_ref[...])

    row = pl.BlockSpec((tm, D), lambda i: (i, 0))
    part = pl.BlockSpec((tm, W), lambda i: (i, 0))
    half = pl.BlockSpec((1, W), lambda i: (0, 0))
    return pl.pallas_call(
        body, name=name, grid=(T // tm,),
        in_specs=[row, part, part, half, half, pl.BlockSpec((D, D), lambda i: (0, 0)), pl.BlockSpec((1, D), lambda i: (0, 0))],
        out_specs=[row, row, row],
        out_shape=[jax.ShapeDtypeStruct((T, D), F32), jax.ShapeDtypeStruct((T, D), BF16),
                   jax.ShapeDtypeStruct((T, D), F32)],
        compiler_params=_params(1, VMEM_LIMIT),
    )(*_hbm(h, o_a, o_b, g_sb, g_ch, w_out, g_post))


def _mix_out_bwd(dh, mo, g_post, w_out, o_a, o_b, g_sb, g_ch, name):
    T, D = dh.shape
    W = g_sb.shape[1]
    tm = min(ROW_BLOCK, T)

    def body(dh_ref, mo_ref, gpost_ref, w_ref, oa_ref, ob_ref, gsb_ref, gch_ref,
             dmo_ref, doa_ref, dob_ref, dgpost_ref, dgsb_ref, dgch_ref):
        first = pl.program_id(0) == 0
        dmo, dgpost = _rms_bwd(mo_ref[...], gpost_ref[...], dh_ref[...])
        dmo_ref[...] = dmo.astype(BF16)
        dmix = _dot(dmo_ref[...], w_ref[...], NT)
        doa_ref[...], dgsb = _rms_bwd(oa_ref[...], gsb_ref[...], dmix[:, :W])
        dob_ref[...], dgch = _rms_bwd(ob_ref[...], gch_ref[...], dmix[:, W:])
        _accumulate(dgpost_ref, dgpost, first)
        _accumulate(dgsb_ref, dgsb, first)
        _accumulate(dgch_ref, dgch, first)

    row = pl.BlockSpec((tm, D), lambda i: (i, 0))
    part = pl.BlockSpec((tm, W), lambda i: (i, 0))
    vec = pl.BlockSpec((1, D), lambda i: (0, 0))
    half = pl.BlockSpec((1, W), lambda i: (0, 0))
    return pl.pallas_call(
        body, name=name, grid=(T // tm,),
        in_specs=[row, row, vec, pl.BlockSpec((D, D), lambda i: (0, 0)), part, part, half, half],
        out_specs=[row, part, part, vec, half, half],
        out_shape=[jax.ShapeDtypeStruct((T, D), BF16), jax.ShapeDtypeStruct((T, W), F32),
                   jax.ShapeDtypeStruct((T, W), F32), jax.ShapeDtypeStruct((1, D), F32),
                   jax.ShapeDtypeStruct((1, W), F32), jax.ShapeDtypeStruct((1, W), F32)],
        compiler_params=_params(1, VMEM_LIMIT),
    )(*_hbm(dh, mo, g_post, w_out, o_a, o_b, g_sb, g_ch))


def _ple_loss(h, p, target, w_proj, w_gate, g_post, name):
    T, D = h.shape
    P = p.shape[1]
    S = N_CHIPS
    C = D // S
    tm = min(ROW_BLOCK, T)

    def body(h_ref, p_ref, t_ref, wp_ref, wg_ref, g_ref, loss_ref, dh_ref, dproj_ref, dgate_ref, dgain_ref):
        first = pl.program_id(0) == 0
        h3 = h_ref[...]
        proj = _dot(p_ref[...].astype(BF16), wp_ref[...])
        s = _sigmoid(_dot(h3.astype(BF16), wg_ref[...]))
        e = proj * s
        diff = h3 + _rms_fwd(e, g_ref[...]) - t_ref[...]
        part = 0.5 * jnp.sum(jnp.mean(diff * diff, axis=-1, keepdims=True), axis=0, keepdims=True)
        _accumulate(loss_ref, jnp.broadcast_to(part, loss_ref.shape), first)
        dy = diff * (1.0 / D)
        de, dgain = _rms_bwd(e, g_ref[...], dy)
        _accumulate(dgain_ref, dgain, first)
        dproj = (de * s).astype(BF16)
        for j in range(S):
            dproj_ref[j] = dproj[:, j * C:(j + 1) * C]
        dgate_ref[...] = (de * proj * s * (1.0 - s)).astype(BF16)
        dh_ref[...] = dy + _dot(dgate_ref[...], wg_ref[...], NT)

    row = pl.BlockSpec((tm, D), lambda i: (i, 0))
    vec = pl.BlockSpec((1, D), lambda i: (0, 0))
    return pl.pallas_call(
        body, name=name, grid=(T // tm,),
        in_specs=[row, pl.BlockSpec((tm, P), lambda i: (i, 0)), row,
                  pl.BlockSpec((P, D), lambda i: (0, 0)), pl.BlockSpec((D, D), lambda i: (0, 0)), vec],
        out_specs=[pl.BlockSpec((8, 128), lambda i: (0, 0)), row,
                   pl.BlockSpec((S, tm, C), lambda i: (0, i, 0)), row, vec],
        out_shape=[jax.ShapeDtypeStruct((8, 128), F32), jax.ShapeDtypeStruct((T, D), F32),
                   jax.ShapeDtypeStruct((S, T, C), BF16), jax.ShapeDtypeStruct((T, D), BF16),
                   jax.ShapeDtypeStruct((1, D), F32)],
        compiler_params=_params(1, VMEM_LIMIT),
    )(*_hbm(h, p, target, w_proj, w_gate, g_post))


def _sb_scores(q, kj, mask):
    z = _dot(q, kj, NT)
    sp = jnp.maximum(z, 0.0) + jnp.log(1.0 + jnp.exp(-jnp.abs(z)))
    return z, sp if mask is None else jnp.where(mask, sp, 0.0)


def _strict_causal():
    rows = lax.broadcasted_iota(jnp.int32, (SB_BLOCK, SB_BLOCK), 0)
    cols = lax.broadcasted_iota(jnp.int32, (SB_BLOCK, SB_BLOCK), 1)
    return cols < rows


def _tri(cmp):
    r = lax.broadcasted_iota(jnp.int32, (2 * SB_BLOCK, SB_BLOCK), 0) % SB_BLOCK
    c = lax.broadcasted_iota(jnp.int32, (2 * SB_BLOCK, SB_BLOCK), 1)
    return jnp.where(cmp(r, c), 1.0, 0.0).astype(BF16)


def _cum(x, tri):
    return _dot(jnp.concatenate(_split2(x), axis=1), tri)


def _pair_lanes():
    lane = lax.broadcasted_iota(jnp.int32, (1, PAIR), 1)
    return [lane < HEAD_DIM, lane >= HEAD_DIM]


def _only(lanes, x):
    return jnp.where(lanes, x, jnp.zeros_like(x))


def _sb_fwd(qkv, name):
    T = qkv.shape[0]
    B = SB_BLOCK
    W = SB_PAIRS * PAIR
    steps = N_HEADS // (2 * SB_PAIRS)
    heads = [(p, h) for p in range(SB_PAIRS) for h in range(2)]

    def body(q_ref, k_ref, v_ref, o_ref, tot_ref):
        i = pl.program_id(1)
        after = _tri(lambda r, c: r > c)
        lanes = _pair_lanes()
        cols = [slice(p * PAIR, (p + 1) * PAIR) for p in range(SB_PAIRS)]
        q = {(p, h): _only(lanes[h], q_ref[:, cols[p]] * ATT_SCALE) for p, h in heads}

        def tiles(j, carries, mask):
            at = pl.ds(pl.multiple_of(j * B, B), B)
            scores = [_sb_scores(q[ph], k_ref[at, cols[ph[0]]], mask) for ph in heads]
            laters = [_cum(sp, after) for _, sp in scores]
            out = []
            for ph, (z, sp), later, (run, acc) in zip(heads, scores, laters, carries):
                a = jnp.exp(z - sp - later - run)
                if mask is not None:
                    a = jnp.where(mask, a, 0.0)
                out.append((run + later[:, 0:1] + sp[:, 0:1],
                            acc + _dot(a.astype(BF16), _only(lanes[ph[1]], v_ref[at, cols[ph[0]]]))))
            return tuple(out)

        zero = (jnp.zeros((B, 1), F32), jnp.zeros((B, PAIR), F32))
        carries = tiles(i, (zero,) * len(heads), _strict_causal())
        carries = lax.fori_loop(0, i, lambda jj, cs: tiles(i - 1 - jj, cs, None), carries)
        for p in range(SB_PAIRS):
            o_ref[:, cols[p]] = carries[2 * p][1] + carries[2 * p + 1][1]
            tot_ref[:, cols[p]] = jnp.where(lanes[0], carries[2 * p][0], carries[2 * p + 1][0])

    blk = lambda off: pl.BlockSpec((B, W), lambda g, i: (i, g + off))
    full = lambda off: pl.BlockSpec((T, W), lambda g, i: (0, g + off))
    out = jax.ShapeDtypeStruct((T, N_HEADS * HEAD_DIM), F32)
    return pl.pallas_call(
        body, name=name, grid=(steps, T // B),
        in_specs=[blk(0), full(steps), full(2 * steps)],
        out_specs=[blk(0), blk(0)],
        out_shape=[out, out],
        compiler_params=_params(2, VMEM_LIMIT),
    )(*_hbm(qkv, qkv, qkv))


def _sb_bwd(qkv, do, tot, name):
    T = qkv.shape[0]
    B = SB_BLOCK
    W = SB_PAIRS * PAIR
    steps = N_HEADS // (2 * SB_PAIRS)
    n_blocks = T // B
    heads = [(p, h) for p in range(SB_PAIRS) for h in range(2)]

    def body(q_ref, k_ref, v_ref, do_ref, tot_ref, dq_ref, dk_ref, dv_ref, dk_s, dv_s):
        i = pl.program_id(1)

        @pl.when(i == 0)
        def _():
            dk_s[...] = jnp.zeros_like(dk_s)
            dv_s[...] = jnp.zeros_like(dv_s)

        upto = _tri(lambda r, c: r <= c)
        below = _tri(lambda r, c: r < c)
        lanes = _pair_lanes()
        cols = [slice(p * PAIR, (p + 1) * PAIR) for p in range(SB_PAIRS)]
        q = {(p, h): _only(lanes[h], q_ref[:, cols[p]] * ATT_SCALE) for p, h in heads}
        do = {(p, h): _only(lanes[h], do_ref[:, cols[p]].astype(BF16)) for p, h in heads}
        tot = {(p, h): tot_ref[:, p * PAIR + h * HEAD_DIM:p * PAIR + h * HEAD_DIM + 1] for p, h in heads}

        def tiles(j, carries, mask):
            at = pl.ds(pl.multiple_of(j * B, B), B)
            ks = [k_ref[at, c] for c in cols]
            vs = [v_ref[at, c] for c in cols]
            scores = [_sb_scores(q[ph], ks[ph[0]], mask) for ph in heads]
            throughs = [_cum(sp, upto) for _, sp in scores]
            das = [_dot(do[ph], vs[ph[0]], NT) for ph in heads]
            a_s, gs = [], []
            for ph, (z, sp), through, da, carry in zip(heads, scores, throughs, das, carries):
                a = jnp.exp(z - sp + through - (tot[ph] - carry[0]))
                if mask is not None:
                    a = jnp.where(mask, a, 0.0)
                a_s.append(a)
                gs.append(a * da)
            befores = [_cum(g, below) for g in gs]
            dzs = []
            for (_, sp), g, before, carry in zip(scores, gs, befores, carries):
                g_before = carry[1] + before
                fail = jnp.exp(-sp)
                dz = fail * (g + g_before) - g_before
                if mask is not None:
                    dz = jnp.where(mask, dz, 0.0)
                dzs.append(dz.astype(BF16))
            out = []
            for ph, a, g, dz, through, before, carry in zip(heads, a_s, gs, dzs, throughs, befores, carries):
                dk_s[at, cols[ph[0]]] += _dot(dz, q[ph], TN)
                dv_s[at, cols[ph[0]]] += _dot(a.astype(BF16), do[ph], TN)
                out.append((carry[0] + through[:, B - 1:B], carry[1] + before[:, B - 1:B] + g[:, B - 1:B],
                            carry[2] + _dot(dz, _only(lanes[ph[1]], ks[ph[0]]))))
            return tuple(out)

        col = jnp.zeros((B, 1), F32)
        zero = (col, col, jnp.zeros((B, PAIR), F32))
        carries = lax.fori_loop(0, i, lambda j, cs: tiles(j, cs, None), (zero,) * len(heads))
        last = tiles(i, carries, _strict_causal())
        for p in range(SB_PAIRS):
            dq_ref[:, cols[p]] = ((last[2 * p][2] + last[2 * p + 1][2]) * ATT_SCALE).astype(BF16)

        @pl.when(i == n_blocks - 1)
        def _():
            dk_ref[...] = dk_s[...].astype(BF16)
            dv_ref[...] = dv_s[...].astype(BF16)

    blk = lambda off: pl.BlockSpec((B, W), lambda g, i: (i, g + off))
    full = lambda off: pl.BlockSpec((T, W), lambda g, i: (0, g + off))
    out = jax.ShapeDtypeStruct((T, N_HEADS * HEAD_DIM), BF16)
    return pl.pallas_call(
        body, name=name, grid=(steps, n_blocks),
        in_specs=[blk(0), full(steps), full(2 * steps), blk(0), blk(0)],
        out_specs=[blk(0), full(0), full(0)],
        out_shape=[out, out, out],
        scratch_shapes=[pltpu.VMEM((T, W), F32)] * 2,
        compiler_params=_params(2, VMEM_LIMIT),
    )(*_hbm(qkv, qkv, qkv, do, tot))


NEAR = BAND - PAD + REL_CLIP
FAR = BAND - NEAR
NEAR_REL = 2 * REL_CLIP
BIAS_ROWS = 8


def _rel_onehot(i, transposed):
    shape = (NEAR, NEAR_REL) if transposed else (NEAR_REL, NEAR)
    j = FAR + lax.broadcasted_iota(jnp.int32, shape, 0 if transposed else 1)
    r = lax.broadcasted_iota(jnp.int32, shape, 1 if transposed else 0)
    idx = jnp.clip(i + PAD - j, -REL_CLIP, REL_CLIP) + REL_CLIP
    return jnp.where(idx - 1 == r, 1.0, 0.0).astype(BF16)


def _bias_table(rel_bias, name):
    def body(near_ref, far_ref, o_ref):
        rb = near_ref[...]
        hi, lo = _split2(rb)
        lo2 = (rb - hi.astype(F32) - lo.astype(F32)).astype(BF16)
        far = jnp.broadcast_to(far_ref[...], (N_HEADS, FAR))
        for k in range(BIAS_ROWS):
            onehot = _rel_onehot(pl.program_id(0) * BIAS_ROWS + k, False)
            o_ref[k, :, :FAR] = far
            o_ref[k, :, FAR:] = _dot(hi, onehot) + _dot(lo, onehot) + _dot(lo2, onehot)

    return pl.pallas_call(
        body, name=name, grid=(CHUNK // BIAS_ROWS,),
        in_specs=[pl.BlockSpec((N_HEADS, NEAR_REL), lambda i: (0, 0)), pl.BlockSpec((N_HEADS, 1), lambda i: (0, 0))],
        out_specs=pl.BlockSpec((BIAS_ROWS, N_HEADS, BAND), lambda i: (i, 0, 0)),
        out_shape=jax.ShapeDtypeStruct((CHUNK, N_HEADS, BAND), F32),
        compiler_params=_params(1),
    )(*_hbm(rel_bias[:, 1:], rel_bias[:, N_REL - 1:]))


def _bias_grad(dbias_t, name):
    def body(d_ref, near_ref, far_ref):
        near, far = None, None
        for k in range(BIAS_ROWS):
            onehot = _rel_onehot(pl.program_id(0) * BIAS_ROWS + k, True)
            hi, lo = _split2(d_ref[k, :, FAR:])
            part = _dot(hi, onehot) + _dot(lo, onehot)
            rest = jnp.sum(d_ref[k, :, :FAR], axis=1, keepdims=True)
            near, far = (part, rest) if near is None else (near + part, far + rest)
        first = pl.program_id(0) == 0
        _accumulate(near_ref, near, first)
        _accumulate(far_ref, jnp.broadcast_to(far, far_ref.shape), first)

    near, far = pl.pallas_call(
        body, name=name, grid=(CHUNK // BIAS_ROWS,),
        in_specs=[pl.BlockSpec((BIAS_ROWS, N_HEADS, BAND), lambda i: (i, 0, 0))],
        out_specs=[pl.BlockSpec((N_HEADS, NEAR_REL), lambda i: (0, 0)), pl.BlockSpec((N_HEADS, 128), lambda i: (0, 0))],
        out_shape=[jax.ShapeDtypeStruct((N_HEADS, NEAR_REL), F32), jax.ShapeDtypeStruct((N_HEADS, 128), F32)],
        compiler_params=_params(1),
    )(*_hbm(dbias_t))
    return jnp.pad(near, ((0, 0), (1, 0))).at[:, N_REL - 1].add(far[:, 0])


def _ch_probs(scores, bias, valid):
    z = jnp.where(valid, scores * ATT_SCALE + bias, NEG_INF)
    e = jnp.exp(z - jnp.max(z, axis=-1, keepdims=True))
    return e / jnp.sum(e, axis=-1, keepdims=True)


CH_HEADS = [(pair, h) for pair in range(N_HEADS // 2) for h in range(2)]
CH_COLS = [slice(pair * PAIR, (pair + 1) * PAIR) for pair in range(N_HEADS // 2)]


def _ch_valid(n):
    slot = lax.broadcasted_iota(jnp.int32, (CHUNK, BAND), 1) // CHUNK
    return n + slot - LOOKBACK >= 0


def _ch_fwd(qkv, bias, name):
    T = qkv.shape[0]
    W = N_HEADS * HEAD_DIM

    def body(q_ref, k_ref, v_ref, b_ref, o_ref, kp, vp):
        n = pl.program_id(0)

        @pl.when(n == 0)
        def _():
            _ch_load_padded(k_ref, v_ref, kp, vp)

        win = pl.ds(pl.multiple_of(n * CHUNK, CHUNK), BAND)
        valid = _ch_valid(n)
        lanes = _pair_lanes()
        scores = [_dot(_only(lanes[h], q_ref[:, CH_COLS[pair]]), kp[win, CH_COLS[pair]], NT) for pair, h in CH_HEADS]
        probs = [_ch_probs(s, b_ref[2 * pair + h], valid).astype(BF16) for s, (pair, h) in zip(scores, CH_HEADS)]
        outs = [_dot(p, _only(lanes[h], vp[win, CH_COLS[pair]])) for p, (pair, h) in zip(probs, CH_HEADS)]
        for pair, cols in enumerate(CH_COLS):
            o_ref[:, cols] = outs[2 * pair] + outs[2 * pair + 1]

    full = lambda col: pl.BlockSpec((T, W), lambda n: (0, col))
    return pl.pallas_call(
        body, name=name, grid=(T // CHUNK,),
        in_specs=[pl.BlockSpec((CHUNK, W), lambda n: (n, 3)), full(4), full(5),
                  pl.BlockSpec((N_HEADS, CHUNK, BAND), lambda n: (0, 0, 0))],
        out_specs=pl.BlockSpec((CHUNK, W), lambda n: (n, 0)),
        out_shape=jax.ShapeDtypeStruct((T, W), F32),
        scratch_shapes=[pltpu.VMEM((PAD + T, W), BF16)] * 2,
        compiler_params=_params(1, VMEM_LIMIT),
    )(*_hbm(qkv, qkv, qkv, bias))


def _ch_load_padded(k_ref, v_ref, kp, vp):
    for src, dst in ((k_ref, kp), (v_ref, vp)):
        dst[:PAD, :] = jnp.zeros((PAD, dst.shape[1]), dst.dtype)
        dst[PAD:, :] = src[...]


def _ch_bwd(qkv, bias, do, name):
    T = qkv.shape[0]
    W = N_HEADS * HEAD_DIM
    n_chunks = T // CHUNK

    def body(q_ref, k_ref, v_ref, b_ref, do_ref, dq_ref, dk_ref, dv_ref, db_ref, kp, vp, dk_s, dv_s):
        n = pl.program_id(0)

        @pl.when(n == 0)
        def _():
            _ch_load_padded(k_ref, v_ref, kp, vp)
            dk_s[...] = jnp.zeros_like(dk_s)
            dv_s[...] = jnp.zeros_like(dv_s)
            db_ref[...] = jnp.zeros_like(db_ref)

        win = pl.ds(pl.multiple_of(n * CHUNK, CHUNK), BAND)
        valid = _ch_valid(n)
        lanes = _pair_lanes()
        kws = [kp[win, cols] for cols in CH_COLS]
        vws = [vp[win, cols] for cols in CH_COLS]
        qs = [_only(lanes[h], q_ref[:, CH_COLS[pair]]) for pair, h in CH_HEADS]
        dos = [_only(lanes[h], do_ref[:, CH_COLS[pair]].astype(BF16)) for pair, h in CH_HEADS]
        scores = [_dot(q, kws[pair], NT) for q, (pair, _) in zip(qs, CH_HEADS)]
        dps = [_dot(do, vws[pair], NT) for do, (pair, _) in zip(dos, CH_HEADS)]
        probs = [_ch_probs(s, b_ref[2 * pair + h], valid) for s, (pair, h) in zip(scores, CH_HEADS)]
        dzs = [p * (dp - jnp.sum(dp * p, axis=-1, keepdims=True)) for p, dp in zip(probs, dps)]
        for k, dz in enumerate(dzs):
            db_ref[k] += dz
        dzbs = [(dz * ATT_SCALE).astype(BF16) for dz in dzs]
        dqs = [_dot(dz, _only(lanes[h], kws[pair])) for dz, (pair, h) in zip(dzbs, CH_HEADS)]
        dks = [_dot(dz, q, TN) for dz, q in zip(dzbs, qs)]
        dvs = [_dot(p.astype(BF16), do, TN) for p, do in zip(probs, dos)]
        for pair, cols in enumerate(CH_COLS):
            dq_ref[:, cols] = (dqs[2 * pair] + dqs[2 * pair + 1]).astype(BF16)
            dk_s[win, cols] += dks[2 * pair] + dks[2 * pair + 1]
            dv_s[win, cols] += dvs[2 * pair] + dvs[2 * pair + 1]

        @pl.when(n == n_chunks - 1)
        def _():
            dk_ref[...] = dk_s[PAD:, :].astype(BF16)
            dv_ref[...] = dv_s[PAD:, :].astype(BF16)

    full = lambda col: pl.BlockSpec((T, W), lambda n: (0, col))
    blk = lambda col: pl.BlockSpec((CHUNK, W), lambda n: (n, col))
    tab = pl.BlockSpec((N_HEADS, CHUNK, BAND), lambda n: (0, 0, 0))
    out = jax.ShapeDtypeStruct((T, W), BF16)
    return pl.pallas_call(
        body, name=name, grid=(n_chunks,),
        in_specs=[blk(3), full(4), full(5), tab, blk(0)],
        out_specs=[blk(0), full(0), full(0), tab],
        out_shape=[out, out, out, jax.ShapeDtypeStruct((N_HEADS, CHUNK, BAND), F32)],
        scratch_shapes=[pltpu.VMEM((PAD + T, W), BF16)] * 2 + [pltpu.VMEM((PAD + T, W), F32)] * 2,
        compiler_params=_params(1, VMEM_LIMIT),
    )(*_hbm(qkv, qkv, qkv, bias, do))


def _rows_split(a, parts):
    return a.reshape(a.shape[:-2] + (parts, a.shape[-2] // parts, a.shape[-1]))


def _cast_into_slot0(c, ws, name):
    parts = 2
    ws = [_rows_split(_rows_split(w, 2), parts) for w in ws]
    n = len(ws)

    def body(c_ref, *refs):
        for src, dst in zip(refs[:n], refs[n:]):
            dst[0, 0, 0] = src[0, 0].astype(BF16)

    outs = pl.pallas_call(
        body, name=name,
        grid_spec=pltpu.PrefetchScalarGridSpec(
            num_scalar_prefetch=1, grid=(2, parts),
            in_specs=[pl.BlockSpec((1, 1) + w.shape[2:], lambda d, r, c_ref: (d ^ c_ref[0], r, 0, 0)) for w in ws],
            out_specs=[pl.BlockSpec((1, 1, 1) + w.shape[2:], lambda d, r, c_ref: (0, d, r, 0, 0)) for w in ws]),
        out_shape=[jax.ShapeDtypeStruct((N_CHIPS,) + w.shape, BF16) for w in ws],
        compiler_params=_params(2, VMEM_LIMIT),
    )(c, *_hbm(*ws))
    return [o.reshape(N_CHIPS, 2, o.shape[2] * o.shape[3], o.shape[4]) for o in outs]


def _chip_order(me, c, lands, name):
    parts = 2
    xs = [_rows_split(x, parts) for x in lands]

    def body(me_ref, c_ref, *refs):
        n = len(refs) // 2
        for src, dst in zip(refs[:n], refs[n:]):
            dst[...] = src[...]

    outs = pl.pallas_call(
        body, name=name,
        grid_spec=pltpu.PrefetchScalarGridSpec(
            num_scalar_prefetch=2, grid=(N_CHIPS, 2, parts),
            in_specs=[pl.BlockSpec((1, 1, 1) + x.shape[3:],
                                   lambda j, h, r, me_ref, c_ref: (j ^ me_ref[0], h ^ c_ref[0], r, 0, 0)) for x in xs],
            out_specs=[pl.BlockSpec((1, 1, 1) + x.shape[3:], lambda j, h, r, me_ref, c_ref: (j, h, r, 0, 0))
                       for x in xs]),
        out_shape=[jax.ShapeDtypeStruct(x.shape, x.dtype) for x in xs],
        compiler_params=_params(3, VMEM_LIMIT),
    )(me, c, *_hbm(*xs))
    return [o.reshape(o.shape[0], 2 * parts * o.shape[3], o.shape[4]) for o in outs]


def _pair_add(c, mine, got, permuted, name):
    parts = 2
    mine = [_rows_split(m, parts) for m in mine]
    got = [_rows_split(g, parts) for g in got]
    n = len(mine)

    def body(c_ref, *refs):
        for a, b, o in zip(refs[:n], refs[n:2 * n], refs[2 * n:]):
            o[0, 0] = (a[0, 0, 0] + b[0, 0].astype(F32)).astype(BF16)

    def mine_spec(m, perm):
        if perm:
            return pl.BlockSpec((1, 1, 1) + m.shape[3:], lambda j, r, c_ref: (j, 0, r, 0, 0))
        return pl.BlockSpec((1, 1, 1) + m.shape[3:], lambda j, r, c_ref: (j, c_ref[0], r, 0, 0))

    def got_spec(g):
        return pl.BlockSpec((1, 1) + g.shape[2:], lambda j, r, c_ref: (j, r, 0, 0))

    outs = pl.pallas_call(
        body, name=name,
        grid_spec=pltpu.PrefetchScalarGridSpec(
            num_scalar_prefetch=1, grid=(N_CHIPS, parts),
            in_specs=[mine_spec(m, perm) for m, perm in zip(mine, permuted)] + [got_spec(g) for g in got],
            out_specs=[got_spec(g) for g in got]),
        out_shape=[jax.ShapeDtypeStruct(g.shape, BF16) for g in got],
        compiler_params=_params(2, VMEM_LIMIT),
    )(c, *_hbm(*mine, *got))
    return [o.reshape(o.shape[0], o.shape[1] * o.shape[2], o.shape[3]) for o in outs]


def _chip_add(me, partials, landed, permuted, name):
    parts = 2
    ps = [_rows_split(x, parts) for x in partials]
    ls = [_rows_split(x, parts) for x in landed]
    n = len(ps)

    def body(me_ref, *refs):
        for own, got, o in zip(refs[:n], refs[n:2 * n], refs[2 * n:]):
            acc = own[0, 0].astype(F32)
            for r in range(N_CHIPS - 1):
                acc = acc + got[r, 0].astype(F32)
            o[0] = acc

    def own_spec(x, perm):
        if perm:
            return pl.BlockSpec((1, 1) + x.shape[2:], lambda r, me_ref: (0, r, 0, 0))
        return pl.BlockSpec((1, 1) + x.shape[2:], lambda r, me_ref: (me_ref[0], r, 0, 0))

    outs = pl.pallas_call(
        body, name=name,
        grid_spec=pltpu.PrefetchScalarGridSpec(
            num_scalar_prefetch=1, grid=(parts,),
            in_specs=[own_spec(x, perm) for x, perm in zip(ps, permuted)]
            + [pl.BlockSpec((N_CHIPS - 1, 1) + x.shape[2:], lambda r, me_ref: (0, r, 0, 0)) for x in ls],
            out_specs=[pl.BlockSpec((1,) + x.shape[2:], lambda r, me_ref: (r, 0, 0)) for x in ps]),
        out_shape=[jax.ShapeDtypeStruct(x.shape[1:], F32) for x in ps],
        compiler_params=_params(1, VMEM_LIMIT),
    )(me, *_hbm(*ps, *ls))
    return [o.reshape(o.shape[0] * o.shape[1], o.shape[2]) for o in outs]


def _adamw_math(w, g, m, v):
    m = ADAM_B1 * m + (1.0 - ADAM_B1) * g
    v = ADAM_B2 * v + (1.0 - ADAM_B2) * (g * g)
    m_hat = m / (1.0 - ADAM_B1 ** ADAM_STEP)
    v_hat = v / (1.0 - ADAM_B2 ** ADAM_STEP)
    delta = -ADAM_LR * (m_hat / (jnp.sqrt(v_hat) + ADAM_EPS) + ADAM_WD * w)
    return delta, m, v


def _adamw(ws, gs, ms, vs, parts, name):
    n = len(ws)
    flat = [_rows_split(a, parts) for a in (*ws, *gs, *ms, *vs)]

    def body(*refs):
        ins, outs = refs[:4 * n], refs[4 * n:]
        for k in range(n):
            d, m, v = _adamw_math(ins[k][...], ins[n + k][...], ins[2 * n + k][...], ins[3 * n + k][...])
            outs[k][...] = d
            outs[n + k][...] = m
            outs[2 * n + k][...] = v

    spec = lambda a: pl.BlockSpec((1,) + a.shape[1:], lambda i: (i, 0, 0))
    outs = pl.pallas_call(
        body, name=name, grid=(parts,),
        in_specs=[spec(a) for a in flat], out_specs=[spec(a) for a in flat[:n]] * 3,
        out_shape=[jax.ShapeDtypeStruct(a.shape, F32) for a in flat[:n]] * 3,
        compiler_params=_params(1, VMEM_LIMIT),
    )(*_hbm(*flat))
    outs = [o.reshape(o.shape[0] * o.shape[1], o.shape[2]) for o in outs]
    return outs[:n], outs[n:2 * n], outs[2 * n:]


def _adamw_halves(c, ws, owns, others, ms, vs, name):
    parts = 4
    n = len(ws)
    whole = [_rows_split(_rows_split(a, 2), parts) for a in (*ws, *ms, *vs)]
    halves = [_rows_split(a, parts) for a in (*owns, *others)]

    def body(c_ref, *refs):
        ins, outs = refs[:5 * n], refs[5 * n:]
        mine = pl.program_id(0) == c_ref[0]
        for k in range(n):
            g = jnp.where(mine, ins[3 * n + k][0], ins[4 * n + k][0])
            d, m, v = _adamw_math(ins[k][0, 0], g, ins[n + k][0, 0], ins[2 * n + k][0, 0])
            for slot, val in enumerate((g, d, m, v)):
                outs[slot * n + k][0, 0] = val

    wspec = lambda a: pl.BlockSpec((1, 1) + a.shape[2:], lambda h, r, c_ref: (h, r, 0, 0))
    hspec = lambda a: pl.BlockSpec((1,) + a.shape[1:], lambda h, r, c_ref: (r, 0, 0))
    outs = pl.pallas_call(
        body, name=name,
        grid_spec=pltpu.PrefetchScalarGridSpec(
            num_scalar_prefetch=1, grid=(2, parts),
            in_specs=[wspec(a) for a in whole] + [hspec(a) for a in halves],
            out_specs=[wspec(a) for a in whole[:n]] * 4),
        out_shape=[jax.ShapeDtypeStruct(a.shape, F32) for a in whole[:n]] * 4,
        compiler_params=_params(2, VMEM_LIMIT),
    )(c, *_hbm(*whole, *halves))
    outs = [o.reshape(2 * parts * o.shape[2], o.shape[3]) for o in outs]
    return outs[:n], outs[n:2 * n], outs[2 * n:3 * n], outs[3 * n:]


def _place():
    x, y, c = lax.axis_index("x"), lax.axis_index("y"), lax.axis_index("c")
    peers = [(x ^ (r >> 1), y ^ (r & 1), c) for r in (1, 2, 3)]
    return x, y, c, peers


def _handshake(peers):
    barrier = pltpu.get_barrier_semaphore()
    for peer in peers:
        pl.semaphore_signal(barrier, inc=1, device_id=peer, device_id_type=MESH)
    pl.semaphore_wait(barrier, len(peers))


ANY = pl.BlockSpec(memory_space=pl.ANY)
HBM = pl.BlockSpec(memory_space=pltpu.HBM)
SEM = pl.BlockSpec(memory_space=pltpu.SEMAPHORE)
SPLIT_COPY = pltpu.SideEffectType.DATAFLOW_SIDE_EFFECTING


def _in_hbm(a):
    return pltpu.with_memory_space_constraint(a, pltpu.HBM)


def _split_start(body, name, collective_id, operands, n_sems, after=None):
    n = len(operands)
    extra = [] if after is None else [after]

    def wrapped(*refs):
        at = n + len(extra)
        body(refs[:n], refs[at], refs[at + 1])
        token = refs[-1]
        token[...] = jnp.zeros_like(token)

    outs = pl.pallas_call(
        wrapped, name=name,
        in_specs=[HBM] * n + [ANY] * len(extra),
        out_shape=(pltpu.SemaphoreType.DMA((n_sems,)), pltpu.SemaphoreType.DMA((n_sems,)),
                   *[pltpu.HBM(a.shape, a.dtype) for a in operands], jax.ShapeDtypeStruct((8, 128), F32)),
        out_specs=(SEM, SEM, *[HBM] * n, pl.BlockSpec(memory_space=pltpu.VMEM)),
        input_output_aliases={i: 2 + i for i in range(n)},
        compiler_params=pltpu.CompilerParams(has_side_effects=SPLIT_COPY, collective_id=collective_id),
    )(*[_in_hbm(a) for a in operands], *extra)
    return outs[0], outs[1], list(outs[2:2 + n]), outs[-1]


def _split_wait(body, name, send_sem, recv_sem, operands, after):
    n = len(operands)

    def wrapped(*refs):
        body(refs[:n], refs[n], refs[n + 1])

    outs = pl.pallas_call(
        wrapped, name=name,
        in_specs=[HBM] * n + [SEM, SEM, ANY],
        out_shape=tuple(pltpu.HBM(a.shape, a.dtype) for a in operands),
        out_specs=tuple([HBM] * n),
        input_output_aliases={i: i for i in range(n)},
        compiler_params=pltpu.CompilerParams(has_side_effects=SPLIT_COPY),
    )(*operands, send_sem, recv_sem, after)
    return list(outs)


def _gather_copies(lands, send_sem, recv_sem):
    peers = _place()[3]
    return [pltpu.make_async_remote_copy(
        src_ref=land.at[0, 0], dst_ref=land.at[r + 1, 0],
        send_sem=send_sem.at[a * 3 + r], recv_sem=recv_sem.at[a * 3 + r],
        device_id=peers[r], device_id_type=MESH) for a, land in enumerate(lands) for r in range(3)]


def _gather_start(lands, name, collective_id, after):
    def body(refs, send_sem, recv_sem):
        _handshake(_place()[3])
        for cp in _gather_copies(refs, send_sem, recv_sem):
            cp.start()

    return _split_start(body, name, collective_id, list(lands), 3 * len(lands), after)


def _gather_wait(send_sem, recv_sem, operands, after, name):
    def body(refs, send_sem, recv_sem):
        for cp in _gather_copies(refs, send_sem, recv_sem):
            cp.wait_send()
            cp.wait_recv()

    return _split_wait(body, name, send_sem, recv_sem, operands, after)


def _gather_finish(lands, with_ici, name):
    n = len(lands)

    def body(*refs):
        land = refs[n:2 * n]
        send_ici, recv_ici, send_d2d, recv_d2d = refs[2 * n:]
        x, y, c, _ = _place()
        ici = _gather_copies(land, send_ici, recv_ici) if with_ici else []
        for cp in ici:
            cp.start()
        passed = [pltpu.make_async_remote_copy(
            src_ref=land[a].at[r + 1, 0], dst_ref=land[a].at[r + 1, 1],
            send_sem=send_d2d.at[a * 3 + r], recv_sem=recv_d2d.at[a * 3 + r],
            device_id=(x, y, 1 - c), device_id_type=MESH) for a in range(n) for r in range(3)]
        for k, cp in enumerate(passed):
            if with_ici:
                ici[k].wait_recv()
            cp.start()
        for cp in passed:
            cp.wait_recv()
        for cp in ici:
            cp.wait_send()
        for cp in passed:
            cp.wait_send()

    outs = pl.pallas_call(
        body, name=name,
        in_specs=[ANY] * n, out_specs=[ANY] * n,
        out_shape=[jax.ShapeDtypeStruct(l.shape, l.dtype) for l in lands],
        input_output_aliases={a: a for a in range(n)},
        scratch_shapes=[pltpu.SemaphoreType.DMA((3 * n,))] * 4,
    )(*lands)
    return list(outs)


def _slabs(land):
    return land.reshape(N_CHIPS, 2 * land.shape[2], land.shape[3])


def _pair_swap(grads, permuted, name):
    n = len(grads)

    def body(*refs):
        src, dst = refs[:n], refs[n:2 * n]
        send_sem, recv_sem = refs[2 * n:]
        x, y, c, _ = _place()
        copies = [pltpu.make_async_remote_copy(
            src_ref=src[a].at[:, 1] if permuted[a] else src[a].at[:, 1 - c], dst_ref=dst[a],
            send_sem=send_sem.at[a], recv_sem=recv_sem.at[a],
            device_id=(x, y, 1 - c), device_id_type=MESH) for a in range(n)]
        for cp in copies:
            cp.start()
        for cp in copies:
            cp.wait()

    return pl.pallas_call(
        body, name=name,
        in_specs=[ANY] * n, out_specs=[ANY] * n,
        out_shape=[jax.ShapeDtypeStruct((N_CHIPS,) + g.shape[2:], g.dtype) for g in grads],
        scratch_shapes=[pltpu.SemaphoreType.DMA((n,))] * 2,
    )(*grads)


def _scatter_copies(refs, permuted, send_sem, recv_sem):
    n = len(refs) // 2
    x, y, _, peers = _place()
    me = 2 * x + y
    return [pltpu.make_async_remote_copy(
        src_ref=refs[a].at[r + 1] if permuted[a] else refs[a].at[me ^ (r + 1)], dst_ref=refs[n + a].at[r],
        send_sem=send_sem.at[a * 3 + r], recv_sem=recv_sem.at[a * 3 + r],
        device_id=peers[r], device_id_type=MESH) for a in range(n) for r in range(3)]


def _scatter_start(partials, permuted, name, collective_id):
    def body(refs, send_sem, recv_sem):
        _handshake(_place()[3])
        for cp in _scatter_copies(refs, permuted, send_sem, recv_sem):
            cp.start()

    lands = [lax.empty((N_CHIPS - 1,) + p.shape[1:], p.dtype) for p in partials]
    return _split_start(body, name, collective_id, list(partials) + lands, 3 * len(partials))


def _scatter_wait(send_sem, recv_sem, operands, permuted, after, name):
    def body(refs, send_sem, recv_sem):
        for cp in _scatter_copies(refs, permuted, send_sem, recv_sem):
            cp.wait_send()
            cp.wait_recv()

    return _split_wait(body, name, send_sem, recv_sem, operands, after)


def _pair_join(halves, name):
    n = len(halves)

    def body(*refs):
        src, dst = refs[:n], refs[n:2 * n]
        send_sem, recv_sem = refs[2 * n:]
        x, y, c, _ = _place()
        copies = [pltpu.make_async_remote_copy(
            src_ref=src[a], dst_ref=dst[a], send_sem=send_sem.at[a], recv_sem=recv_sem.at[a],
            device_id=(x, y, 1 - c), device_id_type=MESH) for a in range(n)]
        for cp in copies:
            cp.start()
        for cp in copies:
            cp.wait()

    return pl.pallas_call(
        body, name=name,
        in_specs=[ANY] * n, out_specs=[ANY] * n,
        out_shape=[jax.ShapeDtypeStruct(h.shape, F32) for h in halves],
        scratch_shapes=[pltpu.SemaphoreType.DMA((n,))] * 2,
    )(*halves)


def _all_sum_small(v, after, name):
    R, C = v.shape
    n_dev = 8

    def body(v_ref, after_ref, o_ref, buf, send_sem, recv_sem):
        x, y, c, _ = _place()
        me = 4 * x + 2 * y + c
        buf[me] = v_ref[...]
        copies = []
        for k in range(1, n_dev):
            peer = (x ^ (k >> 2), y ^ ((k >> 1) & 1), c ^ (k & 1))
            copies.append(pltpu.make_async_remote_copy(
                src_ref=v_ref, dst_ref=buf.at[me], send_sem=send_sem.at[k - 1], recv_sem=recv_sem.at[k - 1],
                device_id=peer, device_id_type=MESH))
        for cp in copies:
            cp.start()
        for cp in copies:
            cp.wait()
        acc = buf[0]
        for m in range(1, n_dev):
            acc = acc + buf[m]
        o_ref[...] = acc

    return pl.pallas_call(
        body, name=name,
        in_specs=[pl.BlockSpec(memory_space=pltpu.VMEM), ANY], out_specs=pl.BlockSpec(memory_space=pltpu.VMEM),
        out_shape=jax.ShapeDtypeStruct((R, C), F32),
        scratch_shapes=[pltpu.VMEM((n_dev, R, C), F32), pltpu.SemaphoreType.DMA((n_dev - 1,)),
                        pltpu.SemaphoreType.DMA((n_dev - 1,))],
    )(v, after)


class _WholeWeights:
    def __init__(self, w):
        self.w = w

    def weights(self, group, after=None):
        return self.w, None

    def grads_ready(self, group, gw):
        return None


def _local_step(x, p, target, gains, rel_bias, hooks):
    T, D = x.shape
    S = N_CHIPS

    tied = lambda gain, token: gain if token is None else gain + token[0, 0]
    w, token = hooks.weights("first")
    w = dict(w)
    h1, xn1, g1, u1, a1, f1 = _ffn_fwd(x, tied(gains["ffn1_pre"], token), gains["ffn1_post"], w["ffn1_gate"],
                                       w["ffn1_up"], w["ffn1_down"], "ffn1_fwd")
    more, token = hooks.weights("in", h1)
    w.update(more)
    qkv, un = _norm_proj(h1, tied(gains["mix_pre"], token), w["in"], "qkv_proj")
    bias = _bias_table(rel_bias, "bias_table").transpose(1, 0, 2)
    o_a, tot = _sb_fwd(qkv, "sb_fwd")
    o_b = _ch_fwd(qkv, bias, "ch_fwd")
    w.update(hooks.weights("rest", o_b)[0])
    w_out = w["out"].reshape(D, D)
    h2, mixed, mo = _mix_out_fwd(h1, o_a, o_b, gains["out_sb"], gains["out_ch"], w_out, gains["mix_post"],
                                 "mix_out_fwd")
    h3, xn2, g2, u2, a2, f2 = _ffn_fwd(h2, gains["ffn2_pre"], gains["ffn2_post"], w["ffn2_gate"], w["ffn2_up"],
                                       w["ffn2_down"], "ffn2_fwd")
    w_ple_proj = w["ple_proj"].transpose(1, 0, 2).reshape(p.shape[1], D)
    w_ple_gate = w["ple_gate"].reshape(D, D)

    loss, dh3, dproj, dgate, dg_ple = _ple_loss(h3, p, target, w_ple_proj, w_ple_gate, gains["ple_post"], "ple_loss")
    gw, gg = {}, {"ple_post": dg_ple}
    gw["ple_proj"] = _mm_tn(p[None], dproj, p.shape[1], "dw_ple_proj")
    row_sharded = lambda pair: tuple(o.reshape(S, D // S, D) for o in pair)
    gw["ple_gate"] = row_sharded(_mm_tn(h3[None], dgate[None], 512, "dw_ple_gate"))

    def ffn_bwd(tag, dh, x_in, xn, g_act, u_act, a_act, f, group):
        dgp, dup, df, gg[tag + "_post"] = _ffn_bwd_act(dh, f, gains[tag + "_post"], w[tag + "_down"], g_act, u_act,
                                                       tag + "_bwd_act")
        gw[tag + "_gate"] = _mm_tn(dgp, xn[None], dgp.shape[2], "dw_" + tag + "_gate")
        gw[tag + "_up"] = _mm_tn(dup, xn[None], dup.shape[2], "dw_" + tag + "_up")
        gw[tag + "_down"] = _mm_tn(a_act, df[None], a_act.shape[2], "dw_" + tag + "_down")
        g_pre = gains[tag + "_pre"]
        if group is not None:
            token = hooks.grads_ready(group, gw)
            g_pre = g_pre if token is None else g_pre + token[0, 0]
        dx, gg[tag + "_pre"] = _proj_bwd([dgp, dup], [w[tag + "_gate"], w[tag + "_up"]], x_in, g_pre, dh,
                                         tag + "_bwd_in")
        return dx

    dh2 = ffn_bwd("ffn2", dh3, h2, xn2, g2, u2, a2, f2, None)
    dmo, do_a, do_b, gg["mix_post"], gg["out_sb"], gg["out_ch"] = _mix_out_bwd(
        dh2, mo, gains["mix_post"], w_out, o_a, o_b, gains["out_sb"], gains["out_ch"], "mix_out_bwd")
    gw["out"] = row_sharded(_mm_tn(mixed[None], dmo[None], 512, "dw_out"))
    token = hooks.grads_ready("early", gw)
    if token is not None:
        tot = tot + token[0, 0]
    dq_a, dk_a, dv_a = _sb_bwd(qkv, do_a, tot, "sb_bwd")
    dq_b, dk_b, dv_b, dbias = _ch_bwd(qkv, bias, do_b, "ch_bwd")
    g_rel = _bias_grad(dbias.transpose(1, 0, 2), "bias_grad")
    dqkv = jnp.concatenate([dq_a, dk_a, dv_a, dq_b, dk_b, dv_b], axis=1)
    gw["in"] = _mm_tn(un[None], dqkv, 512, "dw_in", groups=S)
    dh1, gg["mix_pre"] = _proj_bwd([dqkv], [w["in"]], h1, gains["mix_pre"], dh2, "qkv_bwd_in")
    dx = ffn_bwd("ffn1", dh1, x, xn1, g1, u1, a1, f1, "late")
    return loss, dx, gw, gg, g_rel


BIG = ["ffn1_gate", "ffn1_up", "ffn1_down", "in", "out", "ffn2_gate", "ffn2_up", "ffn2_down", "ple_proj", "ple_gate"]
GAINS = ["ffn1_pre", "ffn1_post", "mix_pre", "mix_post", "out_sb", "out_ch", "ffn2_pre", "ffn2_post", "ple_post"]
TRANSPOSED = ("w_ffn1_gate", "w_ffn1_up", "w_ffn2_gate", "w_ffn2_up")
PERMUTED = ("ffn1_gate", "ffn1_up", "ffn1_down", "ffn2_gate", "ffn2_up", "ffn2_down")
W_GROUPS = {"first": ["ffn1_gate", "ffn1_up", "ffn1_down"], "in": ["in"],
            "rest": ["out", "ffn2_gate", "ffn2_up", "ffn2_down", "ple_proj", "ple_gate"]}
G_GROUPS = {"early": ["ple_proj", "ple_gate", "ffn2_gate", "ffn2_up", "ffn2_down", "out"],
            "late": ["in", "ffn1_gate", "ffn1_up", "ffn1_down"]}
ORDER = ["g_ffn1_pre", "g_ffn1_post", "w_ffn1_gate", "w_ffn1_up", "w_ffn1_down", "g_mix_pre", "g_mix_post", "w_in",
         "g_out_sb", "g_out_ch", "rel_bias", "w_out", "g_ffn2_pre", "g_ffn2_post", "w_ffn2_gate", "w_ffn2_up",
         "w_ffn2_down", "w_ple_proj", "w_ple_gate", "g_ple_post"]


def kernel(x, p, g_ffn1_pre, g_ffn1_post, w_ffn1_gate, w_ffn1_up, w_ffn1_down, g_mix_pre, g_mix_post, w_in, g_out_sb, g_out_ch, rel_bias, w_out, g_ffn2_pre, g_ffn2_post, w_ffn2_gate, w_ffn2_up, w_ffn2_down, w_ple_proj, w_ple_gate, g_ple_post, loss_target, m_g_ffn1_pre, m_g_ffn1_post, m_w_ffn1_gate, m_w_ffn1_up, m_w_ffn1_down, m_g_mix_pre, m_g_mix_post, m_w_in, m_g_out_sb, m_g_out_ch, m_rel_bias, m_w_out, m_g_ffn2_pre, m_g_ffn2_post, m_w_ffn2_gate, m_w_ffn2_up, m_w_ffn2_down, m_w_ple_proj, m_w_ple_gate, m_g_ple_post, v_g_ffn1_pre, v_g_ffn1_post, v_w_ffn1_gate, v_w_ffn1_up, v_w_ffn1_down, v_g_mix_pre, v_g_mix_post, v_w_in, v_g_out_sb, v_g_out_ch, v_rel_bias, v_w_out, v_g_ffn2_pre, v_g_ffn2_post, v_w_ffn2_gate, v_w_ffn2_up, v_w_ffn2_down, v_w_ple_proj, v_w_ple_gate, v_g_ple_post):
    args = dict(locals())
    take = lambda a, n: a[0].T if n in TRANSPOSED else a[0]
    wts = {n: take(args[n], n) for n in ORDER}
    ms = {n: take(args["m_" + n], n) for n in ORDER}
    vs = {n: take(args["v_" + n], n) for n in ORDER}
    gains = {n: wts["g_" + n][None] for n in GAINS}

    c_idx = lax.axis_index("c").astype(jnp.int32).reshape(1)
    me_idx = (2 * lax.axis_index("x") + lax.axis_index("y")).astype(jnp.int32).reshape(1)
    south = lax.axis_index("c") == 0

    lands = dict(zip(BIG, _cast_into_slot0(c_idx, [wts["w_" + n] for n in BIG], "cast_weights")))

    def in_order(names, zones):
        plain = [n for n in names if n not in PERMUTED]
        fixed = dict(zip(plain, _chip_order(me_idx, c_idx, [zones[n] for n in plain], "chip_order_" + plain[0]))
                     ) if plain else {}
        return {n: fixed[n] if n in fixed else _slabs(zones[n]) for n in names}

    class Overlapped:
        def __init__(self):
            self.started = {}
            self.flying = None

        def start(self, group, collective_id, after):
            self.flying = _gather_start([lands[n] for n in W_GROUPS[group]], "gather_%s_start" % group,
                                        collective_id, after)
            return self.flying[3]

        def weights(self, group, after=None):
            names = W_GROUPS[group]
            if group == "first":
                zones = _gather_finish([lands[n] for n in names], True, "gather_first")
                return in_order(names, dict(zip(names, zones))), self.start("in", 1, zones[0])
            send_sem, recv_sem, zones, _ = self.flying
            zones = _gather_wait(send_sem, recv_sem, zones, after, "gather_%s_wait" % group)
            zones = _gather_finish(zones, False, "gather_%s_finish" % group)
            token = self.start("rest", 4, zones[0]) if group == "in" else None
            return in_order(names, dict(zip(names, zones))), token

        def grads_ready(self, group, gw):
            self.started[group] = reduce_start(G_GROUPS[group], gw, group, {"early": 2, "late": 3}[group])
            return self.started[group][-1]

    def reduce_start(names, gw, tag, cid):
        perm = [n in PERMUTED for n in names]
        halved = lambda g: g.reshape(N_CHIPS, 2, g.shape[1] // 2, g.shape[2])
        mine = [halved(gw[n][0]) for n in names]
        got = _pair_swap([halved(gw[n][1]) for n in names], perm, "grad_pair_swap_" + tag)
        partial = _pair_add(c_idx, mine, got, perm, "grad_pair_add_" + tag)
        send_sem, recv_sem, operands, token = _scatter_start(partial, perm, "grad_scatter_start_" + tag, cid)
        return names, perm, send_sem, recv_sem, operands, token

    def reduce_finish(state, after, tag):
        names, perm, send_sem, recv_sem, operands, _ = state
        operands = _scatter_wait(send_sem, recv_sem, operands, perm, after, "grad_scatter_wait_" + tag)
        n = len(names)
        own = _chip_add(me_idx, operands[:n], operands[n:], perm, "grad_chip_add_" + tag)
        return own, _pair_join(own, "grad_pair_join_" + tag)

    hooks = Overlapped()
    loss, dx, gw, gg, g_rel = _local_step(x[0], p[0, 0], loss_target[0], gains, wts["rel_bias"], hooks)

    grads, delta, new_m, new_v = {}, {}, {}, {}

    def finish(group, after):
        own, other = reduce_finish(hooks.started[group], after, group)
        names = ["w_" + n for n in G_GROUPS[group]]
        g, d, m, v = _adamw_halves(c_idx, [wts[n] for n in names], own, other, [ms[n] for n in names],
                                   [vs[n] for n in names], "adamw_" + group)
        for n, gg_, dd, mm, vv in zip(names, g, d, m, v):
            grads[n], delta[n], new_m[n], new_v[n] = gg_, dd, mm, vv
        return d[0]

    finish("late", finish("early", dx))

    pieces = [gg[n].reshape(-1, 128) for n in GAINS] + [jnp.pad(g_rel, ((0, 0), (0, N_REL_PAD - N_REL))).reshape(-1, 128)]
    summed = _all_sum_small(jnp.concatenate(pieces, axis=0), delta["w_in"], "small_grad_sum")
    at = 0
    for n, piece in zip(GAINS, pieces[:-1]):
        grads["g_" + n] = summed[at:at + piece.shape[0]].reshape(1, -1)[0]
        at += piece.shape[0]
    grads["rel_bias"] = summed[at:].reshape(N_HEADS, N_REL_PAD)[:, :N_REL]

    small = ["g_" + n for n in GAINS] + ["rel_bias"]
    as_rows = lambda a: (a.reshape(-1, 128) if a.size % 128 == 0 else jnp.pad(a, ((0, 0), (0, N_REL_PAD - N_REL))).reshape(-1, 128))
    d, m, v = _adamw([as_rows(wts[n]) for n in small], [as_rows(grads[n]) for n in small],
                     [as_rows(ms[n]) for n in small], [as_rows(vs[n]) for n in small], 1, "adamw_small")
    for n, dd, mm, vv in zip(small, d, m, v):
        back = (lambda a: a.reshape(N_HEADS, N_REL_PAD)[:, :N_REL]) if n == "rel_bias" else (lambda a: a.reshape(-1))
        delta[n], new_m[n], new_v[n] = back(dd), back(mm), back(vv)

    loss = lax.psum(loss[0, 0], ("x", "y", "c"))
    outs = [loss, dx[None]]
    for table in (grads, delta, new_m, new_v):
        outs += [(table[n].T if n in TRANSPOSED else table[n])[None] for n in ORDER]
    return tuple(outs)
```

```python
import functools

import jax
import jax.numpy as jnp
from jax import lax
from jax.experimental import pallas as pl
from jax.experimental.pallas import tpu as pltpu

F32 = jnp.float32
BF16 = jnp.bfloat16
EPS = 1e-6
N_CHIPS = 4
HEAD_DIM = 64
N_HEADS = 8
CHUNK = 64
LOOKBACK = 8
BAND = (LOOKBACK + 1) * CHUNK
PAD = LOOKBACK * CHUNK
REL_CLIP = 128
N_REL = 2 * REL_CLIP + 1
N_REL_PAD = 384
SB_BLOCK = 256
PAIR = 2 * HEAD_DIM
SB_PAIRS = 2
ATT_SCALE = HEAD_DIM ** -0.5
NEG_INF = -1e30
ROW_BLOCK = 512
VMEM_LIMIT = 48 * 1024 * 1024
MESH = pl.DeviceIdType.MESH

ADAM_LR = 0.001
ADAM_B1 = 0.9
ADAM_B2 = 0.999
ADAM_EPS = 1e-08
ADAM_WD = 0.01
ADAM_STEP = 10

NT = (((1,), (1,)), ((), ()))
TN = (((0,), (0,)), ((), ()))


def _params(n_grid, vmem=None):
    return pltpu.CompilerParams(dimension_semantics=("arbitrary",) * n_grid, vmem_limit_bytes=vmem)


def _hbm(*arrays):
    return [pltpu.with_memory_space_constraint(a, pltpu.HBM) for a in arrays]


def _out(shape, dtype):
    return pltpu.HBM(shape, dtype)


def _dot(a, b, dims=None):
    if dims is None:
        return jnp.dot(a, b, preferred_element_type=F32)
    return lax.dot_general(a, b, dims, preferred_element_type=F32)


def _sigmoid(x):
    return 1.0 / (1.0 + jnp.exp(-x))


def _rms_fwd(x, g):
    r = lax.rsqrt(jnp.mean(x * x, axis=-1, keepdims=True) + EPS)
    return x * r * g


def _rms_bwd(x, g, dy):
    r = lax.rsqrt(jnp.mean(x * x, axis=-1, keepdims=True) + EPS)
    xh = x * r
    dg = jnp.sum(dy * xh, axis=0, keepdims=True)
    t = dy * g
    dx = r * (t - xh * jnp.mean(t * xh, axis=-1, keepdims=True))
    return dx, dg


def _accumulate(ref, val, first):
    @pl.when(first)
    def _():
        ref[...] = val

    @pl.when(jnp.logical_not(first))
    def _():
        ref[...] += val


def _split2(x):
    hi = x.astype(BF16)
    lo = (x - hi.astype(F32)).astype(BF16)
    return hi, lo


def _ffn_fwd(x, g_pre, g_post, wg, wu, wd, name):
    T, D = x.shape
    S, FS, _ = wg.shape
    tm = min(ROW_BLOCK, T)

    def body(x_ref, gpre_ref, gpost_ref, wg_ref, wu_ref, wd_ref,
             h_ref, xn_ref, g_ref, u_ref, a_ref, f_ref, xn_s, acc_s):
        k = pl.program_id(1)

        @pl.when(k == 0)
        def _():
            xn_s[...] = _rms_fwd(x_ref[...], gpre_ref[...]).astype(BF16)
            xn_ref[...] = xn_s[...]

        xn = xn_s[...]
        g = _dot(xn, wg_ref[0], NT)
        u = _dot(xn, wu_ref[0], NT)
        g_ref[0] = g
        u_ref[0] = u
        a = (g * _sigmoid(g) * u).astype(BF16)
        a_ref[0] = a
        _accumulate(acc_s, _dot(a, wd_ref[0]), k == 0)

        @pl.when(k == S - 1)
        def _():
            f = acc_s[...]
            f_ref[...] = f
            h_ref[...] = x_ref[...] + 0.5 * _rms_fwd(f, gpost_ref[...])

    row = pl.BlockSpec((tm, D), lambda i, k: (i, 0))
    vec = pl.BlockSpec((1, D), lambda i, k: (0, 0))
    act = pl.BlockSpec((1, tm, FS), lambda i, k: (k, i, 0))
    return pl.pallas_call(
        body, name=name, grid=(T // tm, S),
        in_specs=[row, vec, vec] + [pl.BlockSpec((1, FS, D), lambda i, k: (k, 0, 0))] * 3,
        out_specs=[row, row, act, act, act, row],
        out_shape=[_out((T, D), F32), _out((T, D), BF16),
                   _out((S, T, FS), F32), _out((S, T, FS), F32),
                   _out((S, T, FS), BF16), _out((T, D), F32)],
        scratch_shapes=[pltpu.VMEM((tm, D), BF16), pltpu.VMEM((tm, D), F32)],
        compiler_params=_params(2, VMEM_LIMIT),
    )(*_hbm(x, g_pre, g_post, wg, wu, wd))


def _ffn_bwd_act(dh, f, g_post, wd, g_act, u_act, name):
    T, D = dh.shape
    S, FS, _ = wd.shape
    tm = min(ROW_BLOCK, T)

    def body(dh_ref, f_ref, gpost_ref, wd_ref, g_ref, u_ref, dgp_ref, dup_ref, df_ref, dgain_ref, df_s):
        i, k = pl.program_id(0), pl.program_id(1)

        @pl.when(k == 0)
        def _():
            df, dgain = _rms_bwd(f_ref[...], gpost_ref[...], 0.5 * dh_ref[...])
            df_s[...] = df.astype(BF16)
            df_ref[...] = df_s[...]
            _accumulate(dgain_ref, dgain, i == 0)

        da = _dot(df_s[...], wd_ref[0], NT)
        g = g_ref[0]
        s = _sigmoid(g)
        dup_ref[0] = (da * (g * s)).astype(BF16)
        dgp_ref[0] = (da * u_ref[0] * (s * (1.0 + g * (1.0 - s)))).astype(BF16)

    row = pl.BlockSpec((tm, D), lambda i, k: (i, 0))
    vec = pl.BlockSpec((1, D), lambda i, k: (0, 0))
    act = pl.BlockSpec((1, tm, FS), lambda i, k: (k, i, 0))
    return pl.pallas_call(
        body, name=name, grid=(T // tm, S),
        in_specs=[row, row, vec, pl.BlockSpec((1, FS, D), lambda i, k: (k, 0, 0)), act, act],
        out_specs=[act, act, row, vec],
        out_shape=[_out((S, T, FS), BF16), _out((S, T, FS), BF16),
                   _out((T, D), BF16), _out((1, D), F32)],
        scratch_shapes=[pltpu.VMEM((tm, D), BF16)],
        compiler_params=_params(2, VMEM_LIMIT),
    )(*_hbm(dh, f, g_post, wd, g_act, u_act))


def _proj_bwd(dys, ws, x, g_pre, dh, name):
    T, D = x.shape
    n = len(dys)
    flat = dys[0].ndim == 2
    S = ws[0].shape[0]
    N = ws[0].shape[2] if flat else ws[0].shape[1]
    tm = min(ROW_BLOCK, T)

    def body(*refs):
        dy_refs, w_refs = refs[:n], refs[n:2 * n]
        x_ref, gpre_ref, dh_ref, dx_ref, dgain_ref, acc_s = refs[2 * n:]
        i, k = pl.program_id(0), pl.program_id(1)
        part = None
        for dy_ref, w_ref in zip(dy_refs, w_refs):
            term = _dot(dy_ref[...], w_ref[0], NT) if flat else _dot(dy_ref[0], w_ref[0])
            part = term if part is None else part + term
        _accumulate(acc_s, part, k == 0)

        @pl.when(k == S - 1)
        def _():
            dx, dgain = _rms_bwd(x_ref[...], gpre_ref[...], acc_s[...])
            dx_ref[...] = dh_ref[...] + dx
            _accumulate(dgain_ref, dgain, i == 0)

    row = pl.BlockSpec((tm, D), lambda i, k: (i, 0))
    vec = pl.BlockSpec((1, D), lambda i, k: (0, 0))
    return pl.pallas_call(
        body, name=name, grid=(T // tm, S),
        in_specs=[pl.BlockSpec((tm, N), lambda i, k: (i, k)) if flat else pl.BlockSpec((1, tm, N), lambda i, k: (k, i, 0))] * n
        + [pl.BlockSpec((1,) + ws[0].shape[1:], lambda i, k: (k, 0, 0))] * n + [row, vec, row],
        out_specs=[row, vec],
        out_shape=[_out((T, D), F32), _out((1, D), F32)],
        scratch_shapes=[pltpu.VMEM((tm, D), F32)],
        compiler_params=_params(2, VMEM_LIMIT),
    )(*_hbm(*dys, *ws, x, g_pre, dh))


def _mm_tn(a, b, bm, name, groups=None):
    ga, T, M = a.shape
    if groups is None:
        gb, _, N = b.shape
        b_spec = pl.BlockSpec((1, T, N), (lambda g, m: (g, 0, 0)) if gb > 1 else (lambda g, m: (0, 0, 0)))
    else:
        gb, N = groups, b.shape[1] // groups
        b_spec = pl.BlockSpec((T, N), lambda g, m: (0, g))
    G = max(ga, gb)

    def body(a_ref, b_ref, o_ref, narrow_ref):
        bv = b_ref[0] if groups is None else b_ref[...]
        o_ref[0] = _dot(a_ref[0].astype(BF16), bv.astype(BF16), TN)
        narrow_ref[0] = o_ref[0].astype(BF16)

    out = pl.BlockSpec((1, bm, N), lambda g, m: (g, m, 0))
    return pl.pallas_call(
        body, name=name, grid=(G, M // bm),
        in_specs=[pl.BlockSpec((1, T, bm), (lambda g, m: (g, 0, m)) if ga > 1 else (lambda g, m: (0, 0, m))), b_spec],
        out_specs=[out, out],
        out_shape=[_out((G, M, N), F32), _out((G, M, N), BF16)],
        compiler_params=_params(2, VMEM_LIMIT),
    )(*_hbm(a, b))


def _norm_proj(x, g_pre, w, name):
    T, D = x.shape
    S, _, N = w.shape
    tm = min(ROW_BLOCK, T)

    def body(x_ref, g_ref, w_ref, o_ref, xn_ref, xn_s):
        @pl.when(pl.program_id(1) == 0)
        def _():
            xn_s[...] = _rms_fwd(x_ref[...], g_ref[...]).astype(BF16)
            xn_ref[...] = xn_s[...]

        o_ref[...] = _dot(xn_s[...], w_ref[0]).astype(BF16)

    row = pl.BlockSpec((tm, D), lambda i, k: (i, 0))
    return pl.pallas_call(
        body, name=name, grid=(T // tm, S),
        in_specs=[row, pl.BlockSpec((1, D), lambda i, k: (0, 0)), pl.BlockSpec((1, D, N), lambda i, k: (k, 0, 0))],
        out_specs=[pl.BlockSpec((tm, N), lambda i, k: (i, k)), row],
        out_shape=[_out((T, S * N), BF16), _out((T, D), BF16)],
        scratch_shapes=[pltpu.VMEM((tm, D), BF16)],
        compiler_params=_params(2, VMEM_LIMIT),
    )(*_hbm(x, g_pre, w))


def _mix_out_fwd(h, o_a, o_b, g_sb, g_ch, w_out, g_post, name):
    T, D = h.shape
    W = g_sb.shape[1]
    tm = min(ROW_BLOCK, T)

    def body(h_ref, oa_ref, ob_ref, gsb_ref, gch_ref, w_ref, gpost_ref, h2_ref, mixed_ref, mo_ref):
        mixed_ref[:, :W] = _rms_fwd(oa_ref[...], gsb_ref[...]).astype(BF16)
        mixed_ref[:, W:] = _rms_fwd(ob_ref[...], gch_ref[...]).astype(BF16)
        mo = _dot(mixed_ref[...], w_ref[...])
        mo_ref[...] = mo
        h2_ref[...] = h_ref[...] + _rms_fwd(mo, gpost_ref[...])

    row = pl.BlockSpec((tm, D), lambda i: (i, 0))
    part = pl.BlockSpec((tm, W), lambda i: (i, 0))
    half = pl.BlockSpec((1, W), lambda i: (0, 0))
    return pl.pallas_call(
        body, name=name, grid=(T // tm,),
        in_specs=[row, part, part, half, half, pl.BlockSpec((D, D), lambda i: (0, 0)), pl.BlockSpec((1, D), lambda i: (0, 0))],
        out_specs=[row, row, row],
        out_shape=[_out((T, D), F32), _out((T, D), BF16),
                   _out((T, D), F32)],
        compiler_params=_params(1, VMEM_LIMIT),
    )(*_hbm(h, o_a, o_b, g_sb, g_ch, w_out, g_post))


def _mix_out_bwd(dh, mo, g_post, w_out, o_a, o_b, g_sb, g_ch, name):
    T, D = dh.shape
    W = g_sb.shape[1]
    tm = min(ROW_BLOCK, T)

    def body(dh_ref, mo_ref, gpost_ref, w_ref, oa_ref, ob_ref, gsb_ref, gch_ref,
             dmo_ref, doa_ref, dob_ref, dgpost_ref, dgsb_ref, dgch_ref):
        first = pl.program_id(0) == 0
        dmo, dgpost = _rms_bwd(mo_ref[...], gpost_ref[...], dh_ref[...])
        dmo_ref[...] = dmo.astype(BF16)
        dmix = _dot(dmo_ref[...], w_ref[...], NT)
        doa_ref[...], dgsb = _rms_bwd(oa_ref[...], gsb_ref[...], dmix[:, :W])
        dob_ref[...], dgch = _rms_bwd(ob_ref[...], gch_ref[...], dmix[:, W:])
        _accumulate(dgpost_ref, dgpost, first)
        _accumulate(dgsb_ref, dgsb, first)
        _accumulate(dgch_ref, dgch, first)

    row = pl.BlockSpec((tm, D), lambda i: (i, 0))
    part = pl.BlockSpec((tm, W), lambda i: (i, 0))
    vec = pl.BlockSpec((1, D), lambda i: (0, 0))
    half = pl.BlockSpec((1, W), lambda i: (0, 0))
    return pl.pallas_call(
        body, name=name, grid=(T // tm,),
        in_specs=[row, row, vec, pl.BlockSpec((D, D), lambda i: (0, 0)), part, part, half, half],
        out_specs=[row, part, part, vec, half, half],
        out_shape=[_out((T, D), BF16), _out((T, W), F32),
                   _out((T, W), F32), _out((1, D), F32),
                   _out((1, W), F32), _out((1, W), F32)],
        compiler_params=_params(1, VMEM_LIMIT),
    )(*_hbm(dh, mo, g_post, w_out, o_a, o_b, g_sb, g_ch))


def _ple_loss(h, p, target, w_proj, w_gate, g_post, name):
    T, D = h.shape
    P = p.shape[1]
    S = N_CHIPS
    C = D // S
    tm = min(ROW_BLOCK, T)

    def body(h_ref, p_ref, t_ref, wp_ref, wg_ref, g_ref, loss_ref, dh_ref, dproj_ref, dgate_ref, dgain_ref):
        first = pl.program_id(0) == 0
        h3 = h_ref[...]
        proj = _dot(p_ref[...].astype(BF16), wp_ref[...])
        s = _sigmoid(_dot(h3.astype(BF16), wg_ref[...]))
        e = proj * s
        diff = h3 + _rms_fwd(e, g_ref[...]) - t_ref[...]
        part = 0.5 * jnp.sum(jnp.mean(diff * diff, axis=-1, keepdims=True), axis=0, keepdims=True)
        _accumulate(loss_ref, jnp.broadcast_to(part, loss_ref.shape), first)
        dy = diff * (1.0 / D)
        de, dgain = _rms_bwd(e, g_ref[...], dy)
        _accumulate(dgain_ref, dgain, first)
        dproj = (de * s).astype(BF16)
        for j in range(S):
            dproj_ref[j] = dproj[:, j * C:(j + 1) * C]
        dgate_ref[...] = (de * proj * s * (1.0 - s)).astype(BF16)
        dh_ref[...] = dy + _dot(dgate_ref[...], wg_ref[...], NT)

    row = pl.BlockSpec((tm, D), lambda i: (i, 0))
    vec = pl.BlockSpec((1, D), lambda i: (0, 0))
    return pl.pallas_call(
        body, name=name, grid=(T // tm,),
        in_specs=[row, pl.BlockSpec((tm, P), lambda i: (i, 0)), row,
                  pl.BlockSpec((P, D), lambda i: (0, 0)), pl.BlockSpec((D, D), lambda i: (0, 0)), vec],
        out_specs=[pl.BlockSpec((8, 128), lambda i: (0, 0)), row,
                   pl.BlockSpec((S, tm, C), lambda i: (0, i, 0)), row, vec],
        out_shape=[_out((8, 128), F32), _out((T, D), F32),
                   _out((S, T, C), BF16), _out((T, D), BF16),
                   _out((1, D), F32)],
        compiler_params=_params(1, VMEM_LIMIT),
    )(*_hbm(h, p, target, w_proj, w_gate, g_post))


def _sb_scores(q, kj, mask):
    z = _dot(q, kj, NT)
    sp = jnp.maximum(z, 0.0) + jnp.log(1.0 + jnp.exp(-jnp.abs(z)))
    return z, sp if mask is None else jnp.where(mask, sp, 0.0)


def _strict_causal():
    rows = lax.broadcasted_iota(jnp.int32, (SB_BLOCK, SB_BLOCK), 0)
    cols = lax.broadcasted_iota(jnp.int32, (SB_BLOCK, SB_BLOCK), 1)
    return cols < rows


def _tri(cmp):
    r = lax.broadcasted_iota(jnp.int32, (2 * SB_BLOCK, SB_BLOCK), 0) % SB_BLOCK
    c = lax.broadcasted_iota(jnp.int32, (2 * SB_BLOCK, SB_BLOCK), 1)
    return jnp.where(cmp(r, c), 1.0, 0.0).astype(BF16)


def _cum(x, tri):
    return _dot(jnp.concatenate(_split2(x), axis=1), tri)


def _pair_lanes():
    lane = lax.broadcasted_iota(jnp.int32, (1, PAIR), 1)
    return [lane < HEAD_DIM, lane >= HEAD_DIM]


def _only(lanes, x):
    return jnp.where(lanes, x, jnp.zeros_like(x))


def _sb_fwd(qkv, name):
    T = qkv.shape[0]
    B = SB_BLOCK
    W = SB_PAIRS * PAIR
    steps = N_HEADS // (2 * SB_PAIRS)
    heads = [(p, h) for p in range(SB_PAIRS) for h in range(2)]

    def body(q_ref, k_ref, v_ref, o_ref, tot_ref):
        i = pl.program_id(1)
        after = _tri(lambda r, c: r > c)
        lanes = _pair_lanes()
        cols = [slice(p * PAIR, (p + 1) * PAIR) for p in range(SB_PAIRS)]
        q = {(p, h): _only(lanes[h], q_ref[:, cols[p]] * ATT_SCALE) for p, h in heads}

        def tiles(j, carries, mask):
            at = pl.ds(pl.multiple_of(j * B, B), B)
            scores = [_sb_scores(q[ph], k_ref[at, cols[ph[0]]], mask) for ph in heads]
            laters = [_cum(sp, after) for _, sp in scores]
            out = []
            for ph, (z, sp), later, (run, acc) in zip(heads, scores, laters, carries):
                a = jnp.exp(z - sp - later - run)
                if mask is not None:
                    a = jnp.where(mask, a, 0.0)
                out.append((run + later[:, 0:1] + sp[:, 0:1],
                            acc + _dot(a.astype(BF16), _only(lanes[ph[1]], v_ref[at, cols[ph[0]]]))))
            return tuple(out)

        zero = (jnp.zeros((B, 1), F32), jnp.zeros((B, PAIR), F32))
        carries = tiles(i, (zero,) * len(heads), _strict_causal())
        carries = lax.fori_loop(0, i, lambda jj, cs: tiles(i - 1 - jj, cs, None), carries)
        for p in range(SB_PAIRS):
            o_ref[:, cols[p]] = carries[2 * p][1] + carries[2 * p + 1][1]
            tot_ref[:, cols[p]] = jnp.where(lanes[0], carries[2 * p][0], carries[2 * p + 1][0])

    blk = lambda off: pl.BlockSpec((B, W), lambda g, i: (i, g + off))
    full = lambda off: pl.BlockSpec((T, W), lambda g, i: (0, g + off))
    out = _out((T, N_HEADS * HEAD_DIM), F32)
    return pl.pallas_call(
        body, name=name, grid=(steps, T // B),
        in_specs=[blk(0), full(steps), full(2 * steps)],
        out_specs=[blk(0), blk(0)],
        out_shape=[out, out],
        compiler_params=_params(2, VMEM_LIMIT),
    )(*_hbm(qkv, qkv, qkv))


def _sb_bwd(qkv, do, tot, name):
    T = qkv.shape[0]
    B = SB_BLOCK
    W = SB_PAIRS * PAIR
    steps = N_HEADS // (2 * SB_PAIRS)
    n_blocks = T // B
    heads = [(p, h) for p in range(SB_PAIRS) for h in range(2)]

    def body(q_ref, k_ref, v_ref, do_ref, tot_ref, dq_ref, dk_ref, dv_ref, dk_s, dv_s):
        i = pl.program_id(1)

        @pl.when(i == 0)
        def _():
            dk_s[...] = jnp.zeros_like(dk_s)
            dv_s[...] = jnp.zeros_like(dv_s)

        upto = _tri(lambda r, c: r <= c)
        below = _tri(lambda r, c: r < c)
        lanes = _pair_lanes()
        cols = [slice(p * PAIR, (p + 1) * PAIR) for p in range(SB_PAIRS)]
        q = {(p, h): _only(lanes[h], q_ref[:, cols[p]] * ATT_SCALE) for p, h in heads}
        do = {(p, h): _only(lanes[h], do_ref[:, cols[p]].astype(BF16)) for p, h in heads}
        tot = {(p, h): tot_ref[:, p * PAIR + h * HEAD_DIM:p * PAIR + h * HEAD_DIM + 1] for p, h in heads}

        def tiles(j, carries, mask):
            at = pl.ds(pl.multiple_of(j * B, B), B)
            ks = [k_ref[at, c] for c in cols]
            vs = [v_ref[at, c] for c in cols]
            scores = [_sb_scores(q[ph], ks[ph[0]], mask) for ph in heads]
            throughs = [_cum(sp, upto) for _, sp in scores]
            das = [_dot(do[ph], vs[ph[0]], NT) for ph in heads]
            a_s, gs = [], []
            for ph, (z, sp), through, da, carry in zip(heads, scores, throughs, das, carries):
                a = jnp.exp(z - sp + through - (tot[ph] - carry[0]))
                if mask is not None:
                    a = jnp.where(mask, a, 0.0)
                a_s.append(a)
                gs.append(a * da)
            befores = [_cum(g, below) for g in gs]
            dzs = []
            for (_, sp), g, before, carry in zip(scores, gs, befores, carries):
                g_before = carry[1] + before
                fail = jnp.exp(-sp)
                dz = fail * (g + g_before) - g_before
                if mask is not None:
                    dz = jnp.where(mask, dz, 0.0)
                dzs.append(dz.astype(BF16))
            out = []
            for ph, a, g, dz, through, before, carry in zip(heads, a_s, gs, dzs, throughs, befores, carries):
                dk_s[at, cols[ph[0]]] += _dot(dz, q[ph], TN)
                dv_s[at, cols[ph[0]]] += _dot(a.astype(BF16), do[ph], TN)
                out.append((carry[0] + through[:, B - 1:B], carry[1] + before[:, B - 1:B] + g[:, B - 1:B],
                            carry[2] + _dot(dz, _only(lanes[ph[1]], ks[ph[0]]))))
            return tuple(out)

        col = jnp.zeros((B, 1), F32)
        zero = (col, col, jnp.zeros((B, PAIR), F32))
        carries = lax.fori_loop(0, i, lambda j, cs: tiles(j, cs, None), (zero,) * len(heads))
        last = tiles(i, carries, _strict_causal())
        for p in range(SB_PAIRS):
            dq_ref[:, cols[p]] = ((last[2 * p][2] + last[2 * p + 1][2]) * ATT_SCALE).astype(BF16)

        @pl.when(i == n_blocks - 1)
        def _():
            dk_ref[...] = dk_s[...].astype(BF16)
            dv_ref[...] = dv_s[...].astype(BF16)

    blk = lambda off: pl.BlockSpec((B, W), lambda g, i: (i, g + off))
    full = lambda off: pl.BlockSpec((T, W), lambda g, i: (0, g + off))
    out = _out((T, N_HEADS * HEAD_DIM), BF16)
    return pl.pallas_call(
        body, name=name, grid=(steps, n_blocks),
        in_specs=[blk(0), full(steps), full(2 * steps), blk(0), blk(0)],
        out_specs=[blk(0), full(0), full(0)],
        out_shape=[out, out, out],
        scratch_shapes=[pltpu.VMEM((T, W), F32)] * 2,
        compiler_params=_params(2, VMEM_LIMIT),
    )(*_hbm(qkv, qkv, qkv, do, tot))


NEAR = BAND - PAD + REL_CLIP
FAR = BAND - NEAR
NEAR_REL = 2 * REL_CLIP
BIAS_ROWS = 8


def _rel_onehot(i, transposed):
    shape = (NEAR, NEAR_REL) if transposed else (NEAR_REL, NEAR)
    j = FAR + lax.broadcasted_iota(jnp.int32, shape, 0 if transposed else 1)
    r = lax.broadcasted_iota(jnp.int32, shape, 1 if transposed else 0)
    idx = jnp.clip(i + PAD - j, -REL_CLIP, REL_CLIP) + REL_CLIP
    return jnp.where(idx - 1 == r, 1.0, 0.0).astype(BF16)


def _bias_table(rel_bias, name):
    def body(near_ref, far_ref, o_ref):
        rb = near_ref[...]
        hi, lo = _split2(rb)
        lo2 = (rb - hi.astype(F32) - lo.astype(F32)).astype(BF16)
        far = jnp.broadcast_to(far_ref[...], (N_HEADS, FAR))
        for k in range(BIAS_ROWS):
            onehot = _rel_onehot(pl.program_id(0) * BIAS_ROWS + k, False)
            o_ref[k, :, :FAR] = far
            o_ref[k, :, FAR:] = _dot(hi, onehot) + _dot(lo, onehot) + _dot(lo2, onehot)

    return pl.pallas_call(
        body, name=name, grid=(CHUNK // BIAS_ROWS,),
        in_specs=[pl.BlockSpec((N_HEADS, NEAR_REL), lambda i: (0, 0)), pl.BlockSpec((N_HEADS, 1), lambda i: (0, 0))],
        out_specs=pl.BlockSpec((BIAS_ROWS, N_HEADS, BAND), lambda i: (i, 0, 0)),
        out_shape=_out((CHUNK, N_HEADS, BAND), F32),
        compiler_params=_params(1),
    )(*_hbm(rel_bias[:, 1:], rel_bias[:, N_REL - 1:]))


def _bias_grad(dbias_t, name):
    def body(d_ref, near_ref, far_ref):
        near, far = None, None
        for k in range(BIAS_ROWS):
            onehot = _rel_onehot(pl.program_id(0) * BIAS_ROWS + k, True)
            hi, lo = _split2(d_ref[k, :, FAR:])
            part = _dot(hi, onehot) + _dot(lo, onehot)
            rest = jnp.sum(d_ref[k, :, :FAR], axis=1, keepdims=True)
            near, far = (part, rest) if near is None else (near + part, far + rest)
        first = pl.program_id(0) == 0
        _accumulate(near_ref, near, first)
        _accumulate(far_ref, jnp.broadcast_to(far, far_ref.shape), first)

    near, far = pl.pallas_call(
        body, name=name, grid=(CHUNK // BIAS_ROWS,),
        in_specs=[pl.BlockSpec((BIAS_ROWS, N_HEADS, BAND), lambda i: (i, 0, 0))],
        out_specs=[pl.BlockSpec((N_HEADS, NEAR_REL), lambda i: (0, 0)), pl.BlockSpec((N_HEADS, 128), lambda i: (0, 0))],
        out_shape=[_out((N_HEADS, NEAR_REL), F32), _out((N_HEADS, 128), F32)],
        compiler_params=_params(1),
    )(*_hbm(dbias_t))
    return jnp.pad(near, ((0, 0), (1, 0))).at[:, N_REL - 1].add(far[:, 0])


def _ch_probs(scores, bias, valid):
    z = jnp.where(valid, scores * ATT_SCALE + bias, NEG_INF)
    e = jnp.exp(z - jnp.max(z, axis=-1, keepdims=True))
    return e / jnp.sum(e, axis=-1, keepdims=True)


CH_HEADS = [(pair, h) for pair in range(N_HEADS // 2) for h in range(2)]
CH_COLS = [slice(pair * PAIR, (pair + 1) * PAIR) for pair in range(N_HEADS // 2)]


def _ch_valid(n):
    slot = lax.broadcasted_iota(jnp.int32, (CHUNK, BAND), 1) // CHUNK
    return n + slot - LOOKBACK >= 0


def _ch_fwd(qkv, bias, name):
    T = qkv.shape[0]
    W = N_HEADS * HEAD_DIM

    def body(q_ref, k_ref, v_ref, b_ref, o_ref, kp, vp):
        n = pl.program_id(0)

        @pl.when(n == 0)
        def _():
            _ch_load_padded(k_ref, v_ref, kp, vp)

        win = pl.ds(pl.multiple_of(n * CHUNK, CHUNK), BAND)
        valid = _ch_valid(n)
        lanes = _pair_lanes()
        scores = [_dot(_only(lanes[h], q_ref[:, CH_COLS[pair]]), kp[win, CH_COLS[pair]], NT) for pair, h in CH_HEADS]
        probs = [_ch_probs(s, b_ref[2 * pair + h], valid).astype(BF16) for s, (pair, h) in zip(scores, CH_HEADS)]
        outs = [_dot(p, _only(lanes[h], vp[win, CH_COLS[pair]])) for p, (pair, h) in zip(probs, CH_HEADS)]
        for pair, cols in enumerate(CH_COLS):
            o_ref[:, cols] = outs[2 * pair] + outs[2 * pair + 1]

    full = lambda col: pl.BlockSpec((T, W), lambda n: (0, col))
    return pl.pallas_call(
        body, name=name, grid=(T // CHUNK,),
        in_specs=[pl.BlockSpec((CHUNK, W), lambda n: (n, 3)), full(4), full(5),
                  pl.BlockSpec((N_HEADS, CHUNK, BAND), lambda n: (0, 0, 0))],
        out_specs=pl.BlockSpec((CHUNK, W), lambda n: (n, 0)),
        out_shape=_out((T, W), F32),
        scratch_shapes=[pltpu.VMEM((PAD + T, W), BF16)] * 2,
        compiler_params=_params(1, VMEM_LIMIT),
    )(*_hbm(qkv, qkv, qkv, bias))


def _ch_load_padded(k_ref, v_ref, kp, vp):
    for src, dst in ((k_ref, kp), (v_ref, vp)):
        dst[:PAD, :] = jnp.zeros((PAD, dst.shape[1]), dst.dtype)
        dst[PAD:, :] = src[...]


def _ch_bwd(qkv, bias, do, name):
    T = qkv.shape[0]
    W = N_HEADS * HEAD_DIM
    n_chunks = T // CHUNK

    def body(q_ref, k_ref, v_ref, b_ref, do_ref, dq_ref, dk_ref, dv_ref, db_ref, kp, vp, dk_s, dv_s):
        n = pl.program_id(0)

        @pl.when(n == 0)
        def _():
            _ch_load_padded(k_ref, v_ref, kp, vp)
            dk_s[...] = jnp.zeros_like(dk_s)
            dv_s[...] = jnp.zeros_like(dv_s)
            db_ref[...] = jnp.zeros_like(db_ref)

        win = pl.ds(pl.multiple_of(n * CHUNK, CHUNK), BAND)
        valid = _ch_valid(n)
        lanes = _pair_lanes()
        kws = [kp[win, cols] for cols in CH_COLS]
        vws = [vp[win, cols] for cols in CH_COLS]
        qs = [_only(lanes[h], q_ref[:, CH_COLS[pair]]) for pair, h in CH_HEADS]
        dos = [_only(lanes[h], do_ref[:, CH_COLS[pair]].astype(BF16)) for pair, h in CH_HEADS]
        scores = [_dot(q, kws[pair], NT) for q, (pair, _) in zip(qs, CH_HEADS)]
        dps = [_dot(do, vws[pair], NT) for do, (pair, _) in zip(dos, CH_HEADS)]
        probs = [_ch_probs(s, b_ref[2 * pair + h], valid) for s, (pair, h) in zip(scores, CH_HEADS)]
        dzs = [p * (dp - jnp.sum(dp * p, axis=-1, keepdims=True)) for p, dp in zip(probs, dps)]
        for k, dz in enumerate(dzs):
            db_ref[k] += dz
        dzbs = [(dz * ATT_SCALE).astype(BF16) for dz in dzs]
        dqs = [_dot(dz, _only(lanes[h], kws[pair])) for dz, (pair, h) in zip(dzbs, CH_HEADS)]
        dks = [_dot(dz, q, TN) for dz, q in zip(dzbs, qs)]
        dvs = [_dot(p.astype(BF16), do, TN) for p, do in zip(probs, dos)]
        for pair, cols in enumerate(CH_COLS):
            dq_ref[:, cols] = (dqs[2 * pair] + dqs[2 * pair + 1]).astype(BF16)
            dk_s[win, cols] += dks[2 * pair] + dks[2 * pair + 1]
            dv_s[win, cols] += dvs[2 * pair] + dvs[2 * pair + 1]

        @pl.when(n == n_chunks - 1)
        def _():
            dk_ref[...] = dk_s[PAD:, :].astype(BF16)
            dv_ref[...] = dv_s[PAD:, :].astype(BF16)

    full = lambda col: pl.BlockSpec((T, W), lambda n: (0, col))
    blk = lambda col: pl.BlockSpec((CHUNK, W), lambda n: (n, col))
    tab = pl.BlockSpec((N_HEADS, CHUNK, BAND), lambda n: (0, 0, 0))
    out = _out((T, W), BF16)
    return pl.pallas_call(
        body, name=name, grid=(n_chunks,),
        in_specs=[blk(3), full(4), full(5), tab, blk(0)],
        out_specs=[blk(0), full(0), full(0), tab],
        out_shape=[out, out, out, _out((N_HEADS, CHUNK, BAND), F32)],
        scratch_shapes=[pltpu.VMEM((PAD + T, W), BF16)] * 2 + [pltpu.VMEM((PAD + T, W), F32)] * 2,
        compiler_params=_params(1, VMEM_LIMIT),
    )(*_hbm(qkv, qkv, qkv, bias, do))


def _rows_split(a, parts):
    return a.reshape(a.shape[:-2] + (parts, a.shape[-2] // parts, a.shape[-1]))


def _cast_into_slot0(c, ws, name):
    parts = 2
    ws = [_rows_split(_rows_split(w, 2), parts) for w in ws]
    n = len(ws)

    def body(c_ref, *refs):
        for src, dst in zip(refs[:n], refs[n:]):
            dst[0, 0, 0] = src[0, 0].astype(BF16)

    outs = pl.pallas_call(
        body, name=name,
        grid_spec=pltpu.PrefetchScalarGridSpec(
            num_scalar_prefetch=1, grid=(2, parts),
            in_specs=[pl.BlockSpec((1, 1) + w.shape[2:], lambda d, r, c_ref: (d ^ c_ref[0], r, 0, 0)) for w in ws],
            out_specs=[pl.BlockSpec((1, 1, 1) + w.shape[2:], lambda d, r, c_ref: (0, d, r, 0, 0)) for w in ws]),
        out_shape=[_out((N_CHIPS,) + w.shape, BF16) for w in ws],
        compiler_params=_params(2, VMEM_LIMIT),
    )(c, *_hbm(*ws))
    return [o.reshape(N_CHIPS, 2, o.shape[2] * o.shape[3], o.shape[4]) for o in outs]


def _chip_order(me, c, lands, name):
    parts = 2
    xs = [_rows_split(x, parts) for x in lands]

    def body(me_ref, c_ref, *refs):
        n = len(refs) // 2
        for src, dst in zip(refs[:n], refs[n:]):
            dst[...] = src[...]

    outs = pl.pallas_call(
        body, name=name,
        grid_spec=pltpu.PrefetchScalarGridSpec(
            num_scalar_prefetch=2, grid=(N_CHIPS, 2, parts),
            in_specs=[pl.BlockSpec((1, 1, 1) + x.shape[3:],
                                   lambda j, h, r, me_ref, c_ref: (j ^ me_ref[0], h ^ c_ref[0], r, 0, 0)) for x in xs],
            out_specs=[pl.BlockSpec((1, 1, 1) + x.shape[3:], lambda j, h, r, me_ref, c_ref: (j, h, r, 0, 0))
                       for x in xs]),
        out_shape=[_out(x.shape, x.dtype) for x in xs],
        compiler_params=_params(3, VMEM_LIMIT),
    )(me, c, *_hbm(*xs))
    return [o.reshape(o.shape[0], 2 * parts * o.shape[3], o.shape[4]) for o in outs]


def _pair_add(c, mine, got, permuted, name):
    parts = 2
    mine = [_rows_split(m, parts) for m in mine]
    got = [_rows_split(g, parts) for g in got]
    n = len(mine)

    def body(c_ref, *refs):
        for a, b, o in zip(refs[:n], refs[n:2 * n], refs[2 * n:]):
            o[0, 0] = (a[0, 0, 0] + b[0, 0].astype(F32)).astype(BF16)

    def mine_spec(m, perm):
        if perm:
            return pl.BlockSpec((1, 1, 1) + m.shape[3:], lambda j, r, c_ref: (j, 0, r, 0, 0))
        return pl.BlockSpec((1, 1, 1) + m.shape[3:], lambda j, r, c_ref: (j, c_ref[0], r, 0, 0))

    def got_spec(g):
        return pl.BlockSpec((1, 1) + g.shape[2:], lambda j, r, c_ref: (j, r, 0, 0))

    outs = pl.pallas_call(
        body, name=name,
        grid_spec=pltpu.PrefetchScalarGridSpec(
            num_scalar_prefetch=1, grid=(N_CHIPS, parts),
            in_specs=[mine_spec(m, perm) for m, perm in zip(mine, permuted)] + [got_spec(g) for g in got],
            out_specs=[got_spec(g) for g in got]),
        out_shape=[_out(g.shape, BF16) for g in got],
        compiler_params=_params(2, VMEM_LIMIT),
    )(c, *_hbm(*mine, *got))
    return [o.reshape(o.shape[0], o.shape[1] * o.shape[2], o.shape[3]) for o in outs]


def _chip_add(me, partials, landed, permuted, name):
    parts = 2
    ps = [_rows_split(x, parts) for x in partials]
    ls = [_rows_split(x, parts) for x in landed]
    n = len(ps)

    def body(me_ref, *refs):
        for own, got, o in zip(refs[:n], refs[n:2 * n], refs[2 * n:]):
            acc = own[0, 0].astype(F32)
            for r in range(N_CHIPS - 1):
                acc = acc + got[r, 0].astype(F32)
            o[0] = acc

    def own_spec(x, perm):
        if perm:
            return pl.BlockSpec((1, 1) + x.shape[2:], lambda r, me_ref: (0, r, 0, 0))
        return pl.BlockSpec((1, 1) + x.shape[2:], lambda r, me_ref: (me_ref[0], r, 0, 0))

    outs = pl.pallas_call(
        body, name=name,
        grid_spec=pltpu.PrefetchScalarGridSpec(
            num_scalar_prefetch=1, grid=(parts,),
            in_specs=[own_spec(x, perm) for x, perm in zip(ps, permuted)]
            + [pl.BlockSpec((N_CHIPS - 1, 1) + x.shape[2:], lambda r, me_ref: (0, r, 0, 0)) for x in ls],
            out_specs=[pl.BlockSpec((1,) + x.shape[2:], lambda r, me_ref: (r, 0, 0)) for x in ps]),
        out_shape=[_out(x.shape[1:], F32) for x in ps],
        compiler_params=_params(1, VMEM_LIMIT),
    )(me, *_hbm(*ps, *ls))
    return [o.reshape(o.shape[0] * o.shape[1], o.shape[2]) for o in outs]


def _adamw_math(w, g, m, v):
    m = ADAM_B1 * m + (1.0 - ADAM_B1) * g
    v = ADAM_B2 * v + (1.0 - ADAM_B2) * (g * g)
    m_hat = m / (1.0 - ADAM_B1 ** ADAM_STEP)
    v_hat = v / (1.0 - ADAM_B2 ** ADAM_STEP)
    delta = -ADAM_LR * (m_hat / (jnp.sqrt(v_hat) + ADAM_EPS) + ADAM_WD * w)
    return delta, m, v


def _adamw(ws, gs, ms, vs, parts, name):
    n = len(ws)
    flat = [_rows_split(a, parts) for a in (*ws, *gs, *ms, *vs)]

    def body(*refs):
        ins, outs = refs[:4 * n], refs[4 * n:]
        for k in range(n):
            d, m, v = _adamw_math(ins[k][...], ins[n + k][...], ins[2 * n + k][...], ins[3 * n + k][...])
            outs[k][...] = d
            outs[n + k][...] = m
            outs[2 * n + k][...] = v

    spec = lambda a: pl.BlockSpec((1,) + a.shape[1:], lambda i: (i, 0, 0))
    outs = pl.pallas_call(
        body, name=name, grid=(parts,),
        in_specs=[spec(a) for a in flat], out_specs=[spec(a) for a in flat[:n]] * 3,
        out_shape=[_out(a.shape, F32) for a in flat[:n]] * 3,
        compiler_params=_params(1, VMEM_LIMIT),
    )(*_hbm(*flat))
    outs = [o.reshape(o.shape[0] * o.shape[1], o.shape[2]) for o in outs]
    return outs[:n], outs[n:2 * n], outs[2 * n:]


def _adamw_halves(c, ws, owns, others, ms, vs, name):
    parts = 4
    n = len(ws)
    whole = [_rows_split(_rows_split(a, 2), parts) for a in (*ws, *ms, *vs)]
    halves = [_rows_split(a, parts) for a in (*owns, *others)]

    def body(c_ref, *refs):
        ins, outs = refs[:5 * n], refs[5 * n:]
        mine = pl.program_id(0) == c_ref[0]
        for k in range(n):
            g = jnp.where(mine, ins[3 * n + k][0], ins[4 * n + k][0])
            d, m, v = _adamw_math(ins[k][0, 0], g, ins[n + k][0, 0], ins[2 * n + k][0, 0])
            for slot, val in enumerate((g, d, m, v)):
                outs[slot * n + k][0, 0] = val

    wspec = lambda a: pl.BlockSpec((1, 1) + a.shape[2:], lambda h, r, c_ref: (h, r, 0, 0))
    hspec = lambda a: pl.BlockSpec((1,) + a.shape[1:], lambda h, r, c_ref: (r, 0, 0))
    outs = pl.pallas_call(
        body, name=name,
        grid_spec=pltpu.PrefetchScalarGridSpec(
            num_scalar_prefetch=1, grid=(2, parts),
            in_specs=[wspec(a) for a in whole] + [hspec(a) for a in halves],
            out_specs=[wspec(a) for a in whole[:n]] * 4),
        out_shape=[_out(a.shape, F32) for a in whole[:n]] * 4,
        compiler_params=_params(2, VMEM_LIMIT),
    )(c, *_hbm(*whole, *halves))
    outs = [o.reshape(2 * parts * o.shape[2], o.shape[3]) for o in outs]
    return outs[:n], outs[n:2 * n], outs[2 * n:3 * n], outs[3 * n:]


def _place():
    x, y, c = lax.axis_index("x"), lax.axis_index("y"), lax.axis_index("c")
    peers = [(x ^ (r >> 1), y ^ (r & 1), c) for r in (1, 2, 3)]
    return x, y, c, peers


def _handshake(peers):
    barrier = pltpu.get_barrier_semaphore()
    for peer in peers:
        pl.semaphore_signal(barrier, inc=1, device_id=peer, device_id_type=MESH)
    pl.semaphore_wait(barrier, len(peers))


ANY = pl.BlockSpec(memory_space=pl.ANY)
HBM = pl.BlockSpec(memory_space=pltpu.HBM)
SEM = pl.BlockSpec(memory_space=pltpu.SEMAPHORE)
SPLIT_COPY = pltpu.SideEffectType.DATAFLOW_SIDE_EFFECTING


def _in_hbm(a):
    return pltpu.with_memory_space_constraint(a, pltpu.HBM)


def _split_start(body, name, collective_id, operands, n_sems, after=None):
    n = len(operands)
    extra = [] if after is None else [after]

    def wrapped(*refs):
        at = n + len(extra)
        body(refs[:n], refs[at], refs[at + 1])
        token = refs[-1]
        token[...] = jnp.zeros_like(token)

    outs = pl.pallas_call(
        wrapped, name=name,
        in_specs=[HBM] * n + [ANY] * len(extra),
        out_shape=(pltpu.SemaphoreType.DMA((n_sems,)), pltpu.SemaphoreType.DMA((n_sems,)),
                   *[pltpu.HBM(a.shape, a.dtype) for a in operands], jax.ShapeDtypeStruct((8, 128), F32)),
        out_specs=(SEM, SEM, *[HBM] * n, pl.BlockSpec(memory_space=pltpu.VMEM)),
        input_output_aliases={i: 2 + i for i in range(n)},
        compiler_params=pltpu.CompilerParams(has_side_effects=SPLIT_COPY, collective_id=collective_id),
    )(*[_in_hbm(a) for a in operands], *extra)
    return outs[0], outs[1], list(outs[2:2 + n]), outs[-1]


def _split_wait(body, name, send_sem, recv_sem, operands, after):
    n = len(operands)

    def wrapped(*refs):
        body(refs[:n], refs[n], refs[n + 1])

    outs = pl.pallas_call(
        wrapped, name=name,
        in_specs=[HBM] * n + [SEM, SEM, ANY],
        out_shape=tuple(pltpu.HBM(a.shape, a.dtype) for a in operands),
        out_specs=tuple([HBM] * n),
        input_output_aliases={i: i for i in range(n)},
        compiler_params=pltpu.CompilerParams(has_side_effects=SPLIT_COPY),
    )(*operands, send_sem, recv_sem, after)
    return list(outs)


def _gather_copies(lands, send_sem, recv_sem):
    peers = _place()[3]
    return [pltpu.make_async_remote_copy(
        src_ref=land.at[0, 0], dst_ref=land.at[r + 1, 0],
        send_sem=send_sem.at[a * 3 + r], recv_sem=recv_sem.at[a * 3 + r],
        device_id=peers[r], device_id_type=MESH) for a, land in enumerate(lands) for r in range(3)]


def _gather_start(lands, name, collective_id, after):
    def body(refs, send_sem, recv_sem):
        _handshake(_place()[3])
        for cp in _gather_copies(refs, send_sem, recv_sem):
            cp.start()

    return _split_start(body, name, collective_id, list(lands), 3 * len(lands), after)


def _gather_wait(send_sem, recv_sem, operands, after, name):
    def body(refs, send_sem, recv_sem):
        for cp in _gather_copies(refs, send_sem, recv_sem):
            cp.wait_send()
            cp.wait_recv()

    return _split_wait(body, name, send_sem, recv_sem, operands, after)


def _gather_finish(lands, with_ici, name):
    n = len(lands)

    def body(*refs):
        land = refs[n:2 * n]
        send_ici, recv_ici, send_d2d, recv_d2d = refs[2 * n:]
        x, y, c, _ = _place()
        ici = _gather_copies(land, send_ici, recv_ici) if with_ici else []
        for cp in ici:
            cp.start()
        passed = [pltpu.make_async_remote_copy(
            src_ref=land[a].at[r + 1, 0], dst_ref=land[a].at[r + 1, 1],
            send_sem=send_d2d.at[a * 3 + r], recv_sem=recv_d2d.at[a * 3 + r],
            device_id=(x, y, 1 - c), device_id_type=MESH) for a in range(n) for r in range(3)]
        for k, cp in enumerate(passed):
            if with_ici:
                ici[k].wait_recv()
            cp.start()
        for cp in passed:
            cp.wait_recv()
        for cp in ici:
            cp.wait_send()
        for cp in passed:
            cp.wait_send()

    outs = pl.pallas_call(
        body, name=name,
        in_specs=[ANY] * n, out_specs=[ANY] * n,
        out_shape=[_out(l.shape, l.dtype) for l in lands],
        input_output_aliases={a: a for a in range(n)},
        scratch_shapes=[pltpu.SemaphoreType.DMA((3 * n,))] * 4,
    )(*lands)
    return list(outs)


def _slabs(land):
    return land.reshape(N_CHIPS, 2 * land.shape[2], land.shape[3])


def _pair_swap(grads, permuted, name):
    n = len(grads)

    def body(*refs):
        src, dst = refs[:n], refs[n:2 * n]
        send_sem, recv_sem = refs[2 * n:]
        x, y, c, _ = _place()
        copies = [pltpu.make_async_remote_copy(
            src_ref=src[a].at[:, 1] if permuted[a] else src[a].at[:, 1 - c], dst_ref=dst[a],
            send_sem=send_sem.at[a], recv_sem=recv_sem.at[a],
            device_id=(x, y, 1 - c), device_id_type=MESH) for a in range(n)]
        for cp in copies:
            cp.start()
        for cp in copies:
            cp.wait()

    return pl.pallas_call(
        body, name=name,
        in_specs=[ANY] * n, out_specs=[ANY] * n,
        out_shape=[_out((N_CHIPS,) + g.shape[2:], g.dtype) for g in grads],
        scratch_shapes=[pltpu.SemaphoreType.DMA((n,))] * 2,
    )(*grads)


def _scatter_copies(refs, permuted, send_sem, recv_sem):
    n = len(refs) // 2
    x, y, _, peers = _place()
    me = 2 * x + y
    return [pltpu.make_async_remote_copy(
        src_ref=refs[a].at[r + 1] if permuted[a] else refs[a].at[me ^ (r + 1)], dst_ref=refs[n + a].at[r],
        send_sem=send_sem.at[a * 3 + r], recv_sem=recv_sem.at[a * 3 + r],
        device_id=peers[r], device_id_type=MESH) for a in range(n) for r in range(3)]


def _scatter_start(partials, permuted, name, collective_id):
    def body(refs, send_sem, recv_sem):
        _handshake(_place()[3])
        for cp in _scatter_copies(refs, permuted, send_sem, recv_sem):
            cp.start()

    lands = [lax.empty((N_CHIPS - 1,) + p.shape[1:], p.dtype) for p in partials]
    return _split_start(body, name, collective_id, list(partials) + lands, 3 * len(partials))


def _scatter_wait(send_sem, recv_sem, operands, permuted, after, name):
    def body(refs, send_sem, recv_sem):
        for cp in _scatter_copies(refs, permuted, send_sem, recv_sem):
            cp.wait_send()
            cp.wait_recv()

    return _split_wait(body, name, send_sem, recv_sem, operands, after)


def _pair_join(halves, name):
    n = len(halves)

    def body(*refs):
        src, dst = refs[:n], refs[n:2 * n]
        send_sem, recv_sem = refs[2 * n:]
        x, y, c, _ = _place()
        copies = [pltpu.make_async_remote_copy(
            src_ref=src[a], dst_ref=dst[a], send_sem=send_sem.at[a], recv_sem=recv_sem.at[a],
            device_id=(x, y, 1 - c), device_id_type=MESH) for a in range(n)]
        for cp in copies:
            cp.start()
        for cp in copies:
            cp.wait()

    return pl.pallas_call(
        body, name=name,
        in_specs=[ANY] * n, out_specs=[ANY] * n,
        out_shape=[_out(h.shape, F32) for h in halves],
        scratch_shapes=[pltpu.SemaphoreType.DMA((n,))] * 2,
    )(*halves)


def _all_sum_small(v, after, name):
    R, C = v.shape
    n_dev = 8

    def body(v_ref, after_ref, o_ref, buf, send_sem, recv_sem):
        x, y, c, _ = _place()
        me = 4 * x + 2 * y + c
        buf[me] = v_ref[...]
        copies = []
        for k in range(1, n_dev):
            peer = (x ^ (k >> 2), y ^ ((k >> 1) & 1), c ^ (k & 1))
            copies.append(pltpu.make_async_remote_copy(
                src_ref=v_ref, dst_ref=buf.at[me], send_sem=send_sem.at[k - 1], recv_sem=recv_sem.at[k - 1],
                device_id=peer, device_id_type=MESH))
        for cp in copies:
            cp.start()
        for cp in copies:
            cp.wait()
        acc = buf[0]
        for m in range(1, n_dev):
            acc = acc + buf[m]
        o_ref[...] = acc

    return pl.pallas_call(
        body, name=name,
        in_specs=[pl.BlockSpec(memory_space=pltpu.VMEM), ANY], out_specs=pl.BlockSpec(memory_space=pltpu.VMEM),
        out_shape=jax.ShapeDtypeStruct((R, C), F32),
        scratch_shapes=[pltpu.VMEM((n_dev, R, C), F32), pltpu.SemaphoreType.DMA((n_dev - 1,)),
                        pltpu.SemaphoreType.DMA((n_dev - 1,))],
    )(v, after)


class _WholeWeights:
    def __init__(self, w):
        self.w = w

    def weights(self, group, after=None):
        return self.w, None

    def grads_ready(self, group, gw):
        return None


def _local_step(x, p, target, gains, rel_bias, hooks):
    T, D = x.shape
    S = N_CHIPS

    tied = lambda gain, token: gain if token is None else gain + token[0, 0]
    w, token = hooks.weights("first")
    w = dict(w)
    h1, xn1, g1, u1, a1, f1 = _ffn_fwd(x, tied(gains["ffn1_pre"], token), gains["ffn1_post"], w["ffn1_gate"],
                                       w["ffn1_up"], w["ffn1_down"], "ffn1_fwd")
    more, token = hooks.weights("in", h1)
    w.update(more)
    qkv, un = _norm_proj(h1, tied(gains["mix_pre"], token), w["in"], "qkv_proj")
    bias = _bias_table(rel_bias, "bias_table").transpose(1, 0, 2)
    o_a, tot = _sb_fwd(qkv, "sb_fwd")
    o_b = _ch_fwd(qkv, bias, "ch_fwd")
    w.update(hooks.weights("rest", o_b)[0])
    w_out = w["out"].reshape(D, D)
    h2, mixed, mo = _mix_out_fwd(h1, o_a, o_b, gains["out_sb"], gains["out_ch"], w_out, gains["mix_post"],
                                 "mix_out_fwd")
    h3, xn2, g2, u2, a2, f2 = _ffn_fwd(h2, gains["ffn2_pre"], gains["ffn2_post"], w["ffn2_gate"], w["ffn2_up"],
                                       w["ffn2_down"], "ffn2_fwd")
    w_ple_proj = w["ple_proj"].transpose(1, 0, 2).reshape(p.shape[1], D)
    w_ple_gate = w["ple_gate"].reshape(D, D)

    loss, dh3, dproj, dgate, dg_ple = _ple_loss(h3, p, target, w_ple_proj, w_ple_gate, gains["ple_post"], "ple_loss")
    gw, gg = {}, {"ple_post": dg_ple}
    gw["ple_proj"] = _mm_tn(p[None], dproj, p.shape[1], "dw_ple_proj")
    row_sharded = lambda pair: tuple(o.reshape(S, D // S, D) for o in pair)
    gw["ple_gate"] = row_sharded(_mm_tn(h3[None], dgate[None], 512, "dw_ple_gate"))

    def ffn_bwd(tag, dh, x_in, xn, g_act, u_act, a_act, f, group):
        dgp, dup, df, gg[tag + "_post"] = _ffn_bwd_act(dh, f, gains[tag + "_post"], w[tag + "_down"], g_act, u_act,
                                                       tag + "_bwd_act")
        gw[tag + "_gate"] = _mm_tn(dgp, xn[None], dgp.shape[2], "dw_" + tag + "_gate")
        gw[tag + "_up"] = _mm_tn(dup, xn[None], dup.shape[2], "dw_" + tag + "_up")
        gw[tag + "_down"] = _mm_tn(a_act, df[None], a_act.shape[2], "dw_" + tag + "_down")
        g_pre = gains[tag + "_pre"]
        if group is not None:
            token = hooks.grads_ready(group, gw)
            g_pre = g_pre if token is None else g_pre + token[0, 0]
        dx, gg[tag + "_pre"] = _proj_bwd([dgp, dup], [w[tag + "_gate"], w[tag + "_up"]], x_in, g_pre, dh,
                                         tag + "_bwd_in")
        return dx

    dh2 = ffn_bwd("ffn2", dh3, h2, xn2, g2, u2, a2, f2, None)
    dmo, do_a, do_b, gg["mix_post"], gg["out_sb"], gg["out_ch"] = _mix_out_bwd(
        dh2, mo, gains["mix_post"], w_out, o_a, o_b, gains["out_sb"], gains["out_ch"], "mix_out_bwd")
    gw["out"] = row_sharded(_mm_tn(mixed[None], dmo[None], 512, "dw_out"))
    token = hooks.grads_ready("early", gw)
    if token is not None:
        tot = tot + token[0, 0]
    dq_a, dk_a, dv_a = _sb_bwd(qkv, do_a, tot, "sb_bwd")
    dq_b, dk_b, dv_b, dbias = _ch_bwd(qkv, bias, do_b, "ch_bwd")
    g_rel = _bias_grad(dbias.transpose(1, 0, 2), "bias_grad")
    dqkv = jnp.concatenate([dq_a, dk_a, dv_a, dq_b, dk_b, dv_b], axis=1)
    gw["in"] = _mm_tn(un[None], dqkv, 512, "dw_in", groups=S)
    dh1, gg["mix_pre"] = _proj_bwd([dqkv], [w["in"]], h1, gains["mix_pre"], dh2, "qkv_bwd_in")
    dx = ffn_bwd("ffn1", dh1, x, xn1, g1, u1, a1, f1, "late")
    return loss, dx, gw, gg, g_rel


BIG = ["ffn1_gate", "ffn1_up", "ffn1_down", "in", "out", "ffn2_gate", "ffn2_up", "ffn2_down", "ple_proj", "ple_gate"]
GAINS = ["ffn1_pre", "ffn1_post", "mix_pre", "mix_post", "out_sb", "out_ch", "ffn2_pre", "ffn2_post", "ple_post"]
TRANSPOSED = ("w_ffn1_gate", "w_ffn1_up", "w_ffn2_gate", "w_ffn2_up")
PERMUTED = ("ffn1_gate", "ffn1_up", "ffn1_down", "ffn2_gate", "ffn2_up", "ffn2_down")
W_GROUPS = {"first": ["ffn1_gate", "ffn1_up", "ffn1_down"], "in": ["in"],
            "rest": ["out", "ffn2_gate", "ffn2_up", "ffn2_down", "ple_proj", "ple_gate"]}
G_GROUPS = {"early": ["ple_proj", "ple_gate", "ffn2_gate", "ffn2_up", "ffn2_down", "out"],
            "late": ["in", "ffn1_gate", "ffn1_up", "ffn1_down"]}
ORDER = ["g_ffn1_pre", "g_ffn1_post", "w_ffn1_gate", "w_ffn1_up", "w_ffn1_down", "g_mix_pre", "g_mix_post", "w_in",
         "g_out_sb", "g_out_ch", "rel_bias", "w_out", "g_ffn2_pre", "g_ffn2_post", "w_ffn2_gate", "w_ffn2_up",
         "w_ffn2_down", "w_ple_proj", "w_ple_gate", "g_ple_post"]


def kernel(x, p, g_ffn1_pre, g_ffn1_post, w_ffn1_gate, w_ffn1_up, w_ffn1_down, g_mix_pre, g_mix_post, w_in, g_out_sb, g_out_ch, rel_bias, w_out, g_ffn2_pre, g_ffn2_post, w_ffn2_gate, w_ffn2_up, w_ffn2_down, w_ple_proj, w_ple_gate, g_ple_post, loss_target, m_g_ffn1_pre, m_g_ffn1_post, m_w_ffn1_gate, m_w_ffn1_up, m_w_ffn1_down, m_g_mix_pre, m_g_mix_post, m_w_in, m_g_out_sb, m_g_out_ch, m_rel_bias, m_w_out, m_g_ffn2_pre, m_g_ffn2_post, m_w_ffn2_gate, m_w_ffn2_up, m_w_ffn2_down, m_w_ple_proj, m_w_ple_gate, m_g_ple_post, v_g_ffn1_pre, v_g_ffn1_post, v_w_ffn1_gate, v_w_ffn1_up, v_w_ffn1_down, v_g_mix_pre, v_g_mix_post, v_w_in, v_g_out_sb, v_g_out_ch, v_rel_bias, v_w_out, v_g_ffn2_pre, v_g_ffn2_post, v_w_ffn2_gate, v_w_ffn2_up, v_w_ffn2_down, v_w_ple_proj, v_w_ple_gate, v_g_ple_post):
    args = dict(locals())
    take = lambda a, n: a[0].T if n in TRANSPOSED else a[0]
    wts = {n: take(args[n], n) for n in ORDER}
    ms = {n: take(args["m_" + n], n) for n in ORDER}
    vs = {n: take(args["v_" + n], n) for n in ORDER}
    gains = {n: wts["g_" + n][None] for n in GAINS}

    c_idx = lax.axis_index("c").astype(jnp.int32).reshape(1)
    me_idx = (2 * lax.axis_index("x") + lax.axis_index("y")).astype(jnp.int32).reshape(1)
    south = lax.axis_index("c") == 0

    lands = dict(zip(BIG, _cast_into_slot0(c_idx, [wts["w_" + n] for n in BIG], "cast_weights")))

    def in_order(names, zones):
        plain = [n for n in names if n not in PERMUTED]
        fixed = dict(zip(plain, _chip_order(me_idx, c_idx, [zones[n] for n in plain], "chip_order_" + plain[0]))
                     ) if plain else {}
        return {n: fixed[n] if n in fixed else _slabs(zones[n]) for n in names}

    class Overlapped:
        def __init__(self):
            self.started = {}
            self.flying = None

        def start(self, group, collective_id, after):
            self.flying = _gather_start([lands[n] for n in W_GROUPS[group]], "gather_%s_start" % group,
                                        collective_id, after)
            return self.flying[3]

        def weights(self, group, after=None):
            names = W_GROUPS[group]
            if group == "first":
                zones = _gather_finish([lands[n] for n in names], True, "gather_first")
                return in_order(names, dict(zip(names, zones))), self.start("in", 1, zones[0])
            send_sem, recv_sem, zones, _ = self.flying
            zones = _gather_wait(send_sem, recv_sem, zones, after, "gather_%s_wait" % group)
            zones = _gather_finish(zones, False, "gather_%s_finish" % group)
            token = self.start("rest", 4, zones[0]) if group == "in" else None
            return in_order(names, dict(zip(names, zones))), token

        def grads_ready(self, group, gw):
            self.started[group] = reduce_start(G_GROUPS[group], gw, group, {"early": 2, "late": 3}[group])
            return self.started[group][-1]

    def reduce_start(names, gw, tag, cid):
        perm = [n in PERMUTED for n in names]
        halved = lambda g: g.reshape(N_CHIPS, 2, g.shape[1] // 2, g.shape[2])
        mine = [halved(gw[n][0]) for n in names]
        got = _pair_swap([halved(gw[n][1]) for n in names], perm, "grad_pair_swap_" + tag)
        partial = _pair_add(c_idx, mine, got, perm, "grad_pair_add_" + tag)
        send_sem, recv_sem, operands, token = _scatter_start(partial, perm, "grad_scatter_start_" + tag, cid)
        return names, perm, send_sem, recv_sem, operands, token

    def reduce_finish(state, after, tag):
        names, perm, send_sem, recv_sem, operands, _ = state
        operands = _scatter_wait(send_sem, recv_sem, operands, perm, after, "grad_scatter_wait_" + tag)
        n = len(names)
        own = _chip_add(me_idx, operands[:n], operands[n:], perm, "grad_chip_add_" + tag)
        return own, _pair_join(own, "grad_pair_join_" + tag)

    hooks = Overlapped()
    loss, dx, gw, gg, g_rel = _local_step(x[0], p[0, 0], loss_target[0], gains, wts["rel_bias"], hooks)

    grads, delta, new_m, new_v = {}, {}, {}, {}

    def finish(group, after):
        own, other = reduce_finish(hooks.started[group], after, group)
        names = ["w_" + n for n in G_GROUPS[group]]
        g, d, m, v = _adamw_halves(c_idx, [wts[n] for n in names], own, other, [ms[n] for n in names],
                                   [vs[n] for n in names], "adamw_" + group)
        for n, gg_, dd, mm, vv in zip(names, g, d, m, v):
            grads[n], delta[n], new_m[n], new_v[n] = gg_, dd, mm, vv
        return d[0]

    finish("late", finish("early", dx))

    pieces = [gg[n].reshape(-1, 128) for n in GAINS] + [jnp.pad(g_rel, ((0, 0), (0, N_REL_PAD - N_REL))).reshape(-1, 128)]
    summed = _all_sum_small(jnp.concatenate(pieces + [loss], axis=0), delta["w_in"], "small_grad_sum")
    at = 0
    for n, piece in zip(GAINS, pieces[:-1]):
        grads["g_" + n] = summed[at:at + piece.shape[0]].reshape(1, -1)[0]
        at += piece.shape[0]
    grads["rel_bias"] = summed[at:at + pieces[-1].shape[0]].reshape(N_HEADS, N_REL_PAD)[:, :N_REL]
    loss = summed[at + pieces[-1].shape[0], 0]

    small = ["g_" + n for n in GAINS] + ["rel_bias"]
    as_rows = lambda a: (a.reshape(-1, 128) if a.size % 128 == 0 else jnp.pad(a, ((0, 0), (0, N_REL_PAD - N_REL))).reshape(-1, 128))
    d, m, v = _adamw([as_rows(wts[n]) for n in small], [as_rows(grads[n]) for n in small],
                     [as_rows(ms[n]) for n in small], [as_rows(vs[n]) for n in small], 1, "adamw_small")
    for n, dd, mm, vv in zip(small, d, m, v):
        back = (lambda a: a.reshape(N_HEADS, N_REL_PAD)[:, :N_REL]) if n == "rel_bias" else (lambda a: a.reshape(-1))
        delta[n], new_m[n], new_v[n] = back(dd), back(mm), back(vv)

    outs = [loss, dx[None]]
    for table in (grads, delta, new_m, new_v):
        outs += [(table[n].T if n in TRANSPOSED else table[n])[None] for n in ORDER]
    return tuple(outs)
```

```python
import functools

import jax
import jax.numpy as jnp
from jax import lax
from jax.experimental import pallas as pl
from jax.experimental.pallas import tpu as pltpu

F32 = jnp.float32
BF16 = jnp.bfloat16
EPS = 1e-6
N_CHIPS = 4
HEAD_DIM = 64
N_HEADS = 8
CHUNK = 64
LOOKBACK = 8
BAND = (LOOKBACK + 1) * CHUNK
PAD = LOOKBACK * CHUNK
REL_CLIP = 128
N_REL = 2 * REL_CLIP + 1
N_REL_PAD = 384
SB_BLOCK = 256
PAIR = 2 * HEAD_DIM
SB_PAIRS = 2
ATT_SCALE = HEAD_DIM ** -0.5
NEG_INF = -1e30
ROW_BLOCK = 512
VMEM_LIMIT = 48 * 1024 * 1024
MESH = pl.DeviceIdType.MESH

ADAM_LR = 0.001
ADAM_B1 = 0.9
ADAM_B2 = 0.999
ADAM_EPS = 1e-08
ADAM_WD = 0.01
ADAM_STEP = 10

NT = (((1,), (1,)), ((), ()))
TN = (((0,), (0,)), ((), ()))


def _params(n_grid, vmem=None):
    return pltpu.CompilerParams(dimension_semantics=("arbitrary",) * n_grid, vmem_limit_bytes=vmem)


def _hbm(*arrays):
    return [pltpu.with_memory_space_constraint(a, pltpu.HBM) for a in arrays]


def _out(shape, dtype):
    return pltpu.HBM(shape, dtype)


def _dot(a, b, dims=None):
    if dims is None:
        return jnp.dot(a, b, preferred_element_type=F32)
    return lax.dot_general(a, b, dims, preferred_element_type=F32)


def _sigmoid(x):
    return 1.0 / (1.0 + jnp.exp(-x))


def _rms_fwd(x, g):
    r = lax.rsqrt(jnp.mean(x * x, axis=-1, keepdims=True) + EPS)
    return x * r * g


def _rms_bwd(x, g, dy):
    r = lax.rsqrt(jnp.mean(x * x, axis=-1, keepdims=True) + EPS)
    xh = x * r
    dg = jnp.sum(dy * xh, axis=0, keepdims=True)
    t = dy * g
    dx = r * (t - xh * jnp.mean(t * xh, axis=-1, keepdims=True))
    return dx, dg


def _accumulate(ref, val, first):
    @pl.when(first)
    def _():
        ref[...] = val

    @pl.when(jnp.logical_not(first))
    def _():
        ref[...] += val


def _split2(x):
    hi = x.astype(BF16)
    lo = (x - hi.astype(F32)).astype(BF16)
    return hi, lo


def _ffn_fwd(x, g_pre, g_post, wg, wu, wd, name):
    T, D = x.shape
    S, FS, _ = wg.shape
    tm = min(ROW_BLOCK, T)

    def body(x_ref, gpre_ref, gpost_ref, wg_ref, wu_ref, wd_ref,
             h_ref, xn_ref, g_ref, u_ref, a_ref, f_ref, xn_s, acc_s):
        k = pl.program_id(1)

        @pl.when(k == 0)
        def _():
            xn_s[...] = _rms_fwd(x_ref[...], gpre_ref[...]).astype(BF16)
            xn_ref[...] = xn_s[...]

        xn = xn_s[...]
        g = _dot(xn, wg_ref[0], NT)
        u = _dot(xn, wu_ref[0], NT)
        g_ref[0] = g
        u_ref[0] = u
        a = (g * _sigmoid(g) * u).astype(BF16)
        a_ref[0] = a
        _accumulate(acc_s, _dot(a, wd_ref[0]), k == 0)

        @pl.when(k == S - 1)
        def _():
            f = acc_s[...]
            f_ref[...] = f
            h_ref[...] = x_ref[...] + 0.5 * _rms_fwd(f, gpost_ref[...])

    row = pl.BlockSpec((tm, D), lambda i, k: (i, 0))
    vec = pl.BlockSpec((1, D), lambda i, k: (0, 0))
    act = pl.BlockSpec((1, tm, FS), lambda i, k: (k, i, 0))
    return pl.pallas_call(
        body, name=name, grid=(T // tm, S),
        in_specs=[row, vec, vec] + [pl.BlockSpec((1, FS, D), lambda i, k: (k, 0, 0))] * 3,
        out_specs=[row, row, act, act, act, row],
        out_shape=[_out((T, D), F32), _out((T, D), BF16),
                   _out((S, T, FS), F32), _out((S, T, FS), F32),
                   _out((S, T, FS), BF16), _out((T, D), F32)],
        scratch_shapes=[pltpu.VMEM((tm, D), BF16), pltpu.VMEM((tm, D), F32)],
        compiler_params=_params(2, VMEM_LIMIT),
    )(*_hbm(x, g_pre, g_post, wg, wu, wd))


def _ffn_bwd_act(dh, f, g_post, wd, g_act, u_act, name):
    T, D = dh.shape
    S, FS, _ = wd.shape
    tm = min(ROW_BLOCK, T)

    def body(dh_ref, f_ref, gpost_ref, wd_ref, g_ref, u_ref, dgp_ref, dup_ref, df_ref, dgain_ref, df_s):
        i, k = pl.program_id(0), pl.program_id(1)

        @pl.when(k == 0)
        def _():
            df, dgain = _rms_bwd(f_ref[...], gpost_ref[...], 0.5 * dh_ref[...])
            df_s[...] = df.astype(BF16)
            df_ref[...] = df_s[...]
            _accumulate(dgain_ref, dgain, i == 0)

        da = _dot(df_s[...], wd_ref[0], NT)
        g = g_ref[0]
        s = _sigmoid(g)
        dup_ref[0] = (da * (g * s)).astype(BF16)
        dgp_ref[0] = (da * u_ref[0] * (s * (1.0 + g * (1.0 - s)))).astype(BF16)

    row = pl.BlockSpec((tm, D), lambda i, k: (i, 0))
    vec = pl.BlockSpec((1, D), lambda i, k: (0, 0))
    act = pl.BlockSpec((1, tm, FS), lambda i, k: (k, i, 0))
    return pl.pallas_call(
        body, name=name, grid=(T // tm, S),
        in_specs=[row, row, vec, pl.BlockSpec((1, FS, D), lambda i, k: (k, 0, 0)), act, act],
        out_specs=[act, act, row, vec],
        out_shape=[_out((S, T, FS), BF16), _out((S, T, FS), BF16),
                   _out((T, D), BF16), _out((1, D), F32)],
        scratch_shapes=[pltpu.VMEM((tm, D), BF16)],
        compiler_params=_params(2, VMEM_LIMIT),
    )(*_hbm(dh, f, g_post, wd, g_act, u_act))


def _proj_bwd(dys, ws, x, g_pre, dh, name):
    T, D = x.shape
    n = len(dys)
    flat = dys[0].ndim == 2
    S = ws[0].shape[0]
    N = ws[0].shape[2] if flat else ws[0].shape[1]
    tm = min(ROW_BLOCK, T)

    def body(*refs):
        dy_refs, w_refs = refs[:n], refs[n:2 * n]
        x_ref, gpre_ref, dh_ref, dx_ref, dgain_ref, acc_s = refs[2 * n:]
        i, k = pl.program_id(0), pl.program_id(1)
        part = None
        for dy_ref, w_ref in zip(dy_refs, w_refs):
            term = _dot(dy_ref[...], w_ref[0], NT) if flat else _dot(dy_ref[0], w_ref[0])
            part = term if part is None else part + term
        _accumulate(acc_s, part, k == 0)

        @pl.when(k == S - 1)
        def _():
            dx, dgain = _rms_bwd(x_ref[...], gpre_ref[...], acc_s[...])
            dx_ref[...] = dh_ref[...] + dx
            _accumulate(dgain_ref, dgain, i == 0)

    row = pl.BlockSpec((tm, D), lambda i, k: (i, 0))
    vec = pl.BlockSpec((1, D), lambda i, k: (0, 0))
    return pl.pallas_call(
        body, name=name, grid=(T // tm, S),
        in_specs=[pl.BlockSpec((tm, N), lambda i, k: (i, k)) if flat else pl.BlockSpec((1, tm, N), lambda i, k: (k, i, 0))] * n
        + [pl.BlockSpec((1,) + ws[0].shape[1:], lambda i, k: (k, 0, 0))] * n + [row, vec, row],
        out_specs=[row, vec],
        out_shape=[_out((T, D), F32), _out((1, D), F32)],
        scratch_shapes=[pltpu.VMEM((tm, D), F32)],
        compiler_params=_params(2, VMEM_LIMIT),
    )(*_hbm(*dys, *ws, x, g_pre, dh))


def _mm_tn(a, b, bm, name, groups=None):
    ga, T, M = a.shape
    if groups is None:
        gb, _, N = b.shape
        b_spec = pl.BlockSpec((1, T, N), (lambda g, m: (g, 0, 0)) if gb > 1 else (lambda g, m: (0, 0, 0)))
    else:
        gb, N = groups, b.shape[1] // groups
        b_spec = pl.BlockSpec((T, N), lambda g, m: (0, g))
    G = max(ga, gb)

    def body(a_ref, b_ref, o_ref, narrow_ref):
        bv = b_ref[0] if groups is None else b_ref[...]
        o_ref[0] = _dot(a_ref[0].astype(BF16), bv.astype(BF16), TN)
        narrow_ref[0] = o_ref[0].astype(BF16)

    out = pl.BlockSpec((1, bm, N), lambda g, m: (g, m, 0))
    return pl.pallas_call(
        body, name=name, grid=(G, M // bm),
        in_specs=[pl.BlockSpec((1, T, bm), (lambda g, m: (g, 0, m)) if ga > 1 else (lambda g, m: (0, 0, m))), b_spec],
        out_specs=[out, out],
        out_shape=[_out((G, M, N), F32), _out((G, M, N), BF16)],
        compiler_params=_params(2, VMEM_LIMIT),
    )(*_hbm(a, b))


def _norm_proj(x, g_pre, w, name):
    T, D = x.shape
    S, _, N = w.shape
    tm = min(ROW_BLOCK, T)

    def body(x_ref, g_ref, w_ref, o_ref, xn_ref, xn_s):
        @pl.when(pl.program_id(1) == 0)
        def _():
            xn_s[...] = _rms_fwd(x_ref[...], g_ref[...]).astype(BF16)
            xn_ref[...] = xn_s[...]

        o_ref[...] = _dot(xn_s[...], w_ref[0]).astype(BF16)

    row = pl.BlockSpec((tm, D), lambda i, k: (i, 0))
    return pl.pallas_call(
        body, name=name, grid=(T // tm, S),
        in_specs=[row, pl.BlockSpec((1, D), lambda i, k: (0, 0)), pl.BlockSpec((1, D, N), lambda i, k: (k, 0, 0))],
        out_specs=[pl.BlockSpec((tm, N), lambda i, k: (i, k)), row],
        out_shape=[_out((T, S * N), BF16), _out((T, D), BF16)],
        scratch_shapes=[pltpu.VMEM((tm, D), BF16)],
        compiler_params=_params(2, VMEM_LIMIT),
    )(*_hbm(x, g_pre, w))


def _mix_out_fwd(h, o_a, o_b, g_sb, g_ch, w_out, g_post, name):
    T, D = h.shape
    W = g_sb.shape[1]
    tm = min(ROW_BLOCK, T)

    def body(h_ref, oa_ref, ob_ref, gsb_ref, gch_ref, w_ref, gpost_ref, h2_ref, mixed_ref, mo_ref):
        mixed_ref[:, :W] = _rms_fwd(oa_ref[...], gsb_ref[...]).astype(BF16)
        mixed_ref[:, W:] = _rms_fwd(ob_ref[...], gch_ref[...]).astype(BF16)
        mo = _dot(mixed_ref[...], w_ref[...])
        mo_ref[...] = mo
        h2_ref[...] = h_ref[...] + _rms_fwd(mo, gpost_ref[...])

    row = pl.BlockSpec((tm, D), lambda i: (i, 0))
    part = pl.BlockSpec((tm, W), lambda i: (i, 0))
    half = pl.BlockSpec((1, W), lambda i: (0, 0))
    return pl.pallas_call(
        body, name=name, grid=(T // tm,),
        in_specs=[row, part, part, half, half, pl.BlockSpec((D, D), lambda i: (0, 0)), pl.BlockSpec((1, D), lambda i: (0, 0))],
        out_specs=[row, row, row],
        out_shape=[_out((T, D), F32), _out((T, D), BF16),
                   _out((T, D), F32)],
        compiler_params=_params(1, VMEM_LIMIT),
    )(*_hbm(h, o_a, o_b, g_sb, g_ch, w_out, g_post))


def _mix_out_bwd(dh, mo, g_post, w_out, o_a, o_b, g_sb, g_ch, name):
    T, D = dh.shape
    W = g_sb.shape[1]
    tm = min(ROW_BLOCK, T)

    def body(dh_ref, mo_ref, gpost_ref, w_ref, oa_ref, ob_ref, gsb_ref, gch_ref,
             dmo_ref, doa_ref, dob_ref, dgpost_ref, dgsb_ref, dgch_ref):
        first = pl.program_id(0) == 0
        dmo, dgpost = _rms_bwd(mo_ref[...], gpost_ref[...], dh_ref[...])
        dmo_ref[...] = dmo.astype(BF16)
        dmix = _dot(dmo_ref[...], w_ref[...], NT)
        doa_ref[...], dgsb = _rms_bwd(oa_ref[...], gsb_ref[...], dmix[:, :W])
        dob_ref[...], dgch = _rms_bwd(ob_ref[...], gch_ref[...], dmix[:, W:])
        _accumulate(dgpost_ref, dgpost, first)
        _accumulate(dgsb_ref, dgsb, first)
        _accumulate(dgch_ref, dgch, first)

    row = pl.BlockSpec((tm, D), lambda i: (i, 0))
    part = pl.BlockSpec((tm, W), lambda i: (i, 0))
    vec = pl.BlockSpec((1, D), lambda i: (0, 0))
    half = pl.BlockSpec((1, W), lambda i: (0, 0))
    return pl.pallas_call(
        body, name=name, grid=(T // tm,),
        in_specs=[row, row, vec, pl.BlockSpec((D, D), lambda i: (0, 0)), part, part, half, half],
        out_specs=[row, part, part, vec, half, half],
        out_shape=[_out((T, D), BF16), _out((T, W), F32),
                   _out((T, W), F32), _out((1, D), F32),
                   _out((1, W), F32), _out((1, W), F32)],
        compiler_params=_params(1, VMEM_LIMIT),
    )(*_hbm(dh, mo, g_post, w_out, o_a, o_b, g_sb, g_ch))


def _ple_loss(h, p, target, w_proj, w_gate, g_post, name):
    T, D = h.shape
    P = p.shape[1]
    S = N_CHIPS
    C = D // S
    tm = min(ROW_BLOCK, T)

    def body(h_ref, p_ref, t_ref, wp_ref, wg_ref, g_ref, loss_ref, dh_ref, dproj_ref, dgate_ref, dgain_ref):
        first = pl.program_id(0) == 0
        h3 = h_ref[...]
        proj = _dot(p_ref[...].astype(BF16), wp_ref[...])
        s = _sigmoid(_dot(h3.astype(BF16), wg_ref[...]))
        e = proj * s
        diff = h3 + _rms_fwd(e, g_ref[...]) - t_ref[...]
        part = 0.5 * jnp.sum(jnp.mean(diff * diff, axis=-1, keepdims=True), axis=0, keepdims=True)
        _accumulate(loss_ref, jnp.broadcast_to(part, loss_ref.shape), first)
        dy = diff * (1.0 / D)
        de, dgain = _rms_bwd(e, g_ref[...], dy)
        _accumulate(dgain_ref, dgain, first)
        dproj = (de * s).astype(BF16)
        for j in range(S):
            dproj_ref[j] = dproj[:, j * C:(j + 1) * C]
        dgate_ref[...] = (de * proj * s * (1.0 - s)).astype(BF16)
        dh_ref[...] = dy + _dot(dgate_ref[...], wg_ref[...], NT)

    row = pl.BlockSpec((tm, D), lambda i: (i, 0))
    vec = pl.BlockSpec((1, D), lambda i: (0, 0))
    return pl.pallas_call(
        body, name=name, grid=(T // tm,),
        in_specs=[row, pl.BlockSpec((tm, P), lambda i: (i, 0)), row,
                  pl.BlockSpec((P, D), lambda i: (0, 0)), pl.BlockSpec((D, D), lambda i: (0, 0)), vec],
        out_specs=[pl.BlockSpec((8, 128), lambda i: (0, 0)), row,
                   pl.BlockSpec((S, tm, C), lambda i: (0, i, 0)), row, vec],
        out_shape=[_out((8, 128), F32), _out((T, D), F32),
                   _out((S, T, C), BF16), _out((T, D), BF16),
                   _out((1, D), F32)],
        compiler_params=_params(1, VMEM_LIMIT),
    )(*_hbm(h, p, target, w_proj, w_gate, g_post))


def _sb_scores(q, kj, mask):
    z = _dot(q, kj, NT)
    sp = jnp.maximum(z, 0.0) + jnp.log(1.0 + jnp.exp(-jnp.abs(z)))
    return z, sp if mask is None else jnp.where(mask, sp, 0.0)


def _strict_causal():
    rows = lax.broadcasted_iota(jnp.int32, (SB_BLOCK, SB_BLOCK), 0)
    cols = lax.broadcasted_iota(jnp.int32, (SB_BLOCK, SB_BLOCK), 1)
    return cols < rows


def _tri(cmp):
    r = lax.broadcasted_iota(jnp.int32, (2 * SB_BLOCK, SB_BLOCK), 0) % SB_BLOCK
    c = lax.broadcasted_iota(jnp.int32, (2 * SB_BLOCK, SB_BLOCK), 1)
    return jnp.where(cmp(r, c), 1.0, 0.0).astype(BF16)


def _cum(x, tri):
    return _dot(jnp.concatenate(_split2(x), axis=1), tri)


def _pair_lanes():
    lane = lax.broadcasted_iota(jnp.int32, (1, PAIR), 1)
    return [lane < HEAD_DIM, lane >= HEAD_DIM]


def _only(lanes, x):
    return jnp.where(lanes, x, jnp.zeros_like(x))


def _sb_fwd(qkv, name):
    T = qkv.shape[0]
    B = SB_BLOCK
    W = SB_PAIRS * PAIR
    steps = N_HEADS // (2 * SB_PAIRS)
    heads = [(p, h) for p in range(SB_PAIRS) for h in range(2)]

    def body(q_ref, k_ref, v_ref, o_ref):
        i = pl.program_id(1)
        after = _tri(lambda r, c: r > c)
        lanes = _pair_lanes()
        cols = [slice(p * PAIR, (p + 1) * PAIR) for p in range(SB_PAIRS)]
        q = {(p, h): _only(lanes[h], q_ref[:, cols[p]] * ATT_SCALE) for p, h in heads}

        def tiles(j, carries, mask):
            at = pl.ds(pl.multiple_of(j * B, B), B)
            scores = [_sb_scores(q[ph], k_ref[at, cols[ph[0]]], mask) for ph in heads]
            laters = [_cum(sp, after) for _, sp in scores]
            out = []
            for ph, (z, sp), later, (run, acc) in zip(heads, scores, laters, carries):
                a = jnp.exp(z - sp - later - run)
                if mask is not None:
                    a = jnp.where(mask, a, 0.0)
                out.append((run + later[:, 0:1] + sp[:, 0:1],
                            acc + _dot(a.astype(BF16), _only(lanes[ph[1]], v_ref[at, cols[ph[0]]]))))
            return tuple(out)

        zero = (jnp.zeros((B, 1), F32), jnp.zeros((B, PAIR), F32))
        carries = tiles(i, (zero,) * len(heads), _strict_causal())
        carries = lax.fori_loop(0, i, lambda jj, cs: tiles(i - 1 - jj, cs, None), carries)
        for p in range(SB_PAIRS):
            o_ref[:, cols[p]] = carries[2 * p][1] + carries[2 * p + 1][1]

    blk = lambda off: pl.BlockSpec((B, W), lambda g, i: (i, g + off))
    full = lambda off: pl.BlockSpec((T, W), lambda g, i: (0, g + off))
    return pl.pallas_call(
        body, name=name, grid=(steps, T // B),
        in_specs=[blk(0), full(steps), full(2 * steps)],
        out_specs=blk(0),
        out_shape=_out((T, N_HEADS * HEAD_DIM), F32),
        compiler_params=_params(2, VMEM_LIMIT),
    )(*_hbm(qkv, qkv, qkv))


def _sb_bwd(qkv, do, o, name):
    T = qkv.shape[0]
    B = SB_BLOCK
    W = SB_PAIRS * PAIR
    steps = N_HEADS // (2 * SB_PAIRS)
    n_blocks = T // B
    heads = [(p, h) for p in range(SB_PAIRS) for h in range(2)]

    def body(q_ref, k_ref, v_ref, do_ref, o_ref, dq_ref, dk_ref, dv_ref, dk_s, dv_s):
        i = pl.program_id(1)

        @pl.when(i == 0)
        def _():
            dk_s[...] = jnp.zeros_like(dk_s)
            dv_s[...] = jnp.zeros_like(dv_s)

        after = _tri(lambda r, c: r > c)
        since = _tri(lambda r, c: r >= c)
        lanes = _pair_lanes()
        cols = [slice(p * PAIR, (p + 1) * PAIR) for p in range(SB_PAIRS)]
        q = {(p, h): _only(lanes[h], q_ref[:, cols[p]] * ATT_SCALE) for p, h in heads}
        do = {(p, h): _only(lanes[h], do_ref[:, cols[p]].astype(BF16)) for p, h in heads}
        total = {ph: jnp.sum(do[ph].astype(F32) * o_ref[:, cols[ph[0]]], axis=1, keepdims=True) for ph in heads}

        def tiles(j, carries, mask):
            at = pl.ds(pl.multiple_of(j * B, B), B)
            ks = [k_ref[at, c] for c in cols]
            vs = [v_ref[at, c] for c in cols]
            scores = [_sb_scores(q[ph], ks[ph[0]], mask) for ph in heads]
            laters = [_cum(sp, after) for _, sp in scores]
            das = [_dot(do[ph], vs[ph[0]], NT) for ph in heads]
            a_s, gs = [], []
            for (z, sp), later, da, carry in zip(scores, laters, das, carries):
                a = jnp.exp(z - sp - later - carry[0])
                if mask is not None:
                    a = jnp.where(mask, a, 0.0)
                a = a.astype(BF16)
                a_s.append(a)
                gs.append(a.astype(F32) * da)
            sinces = [_cum(g, since) for g in gs]
            dzs = []
            for ph, (_, sp), g, from_s, carry in zip(heads, scores, gs, sinces, carries):
                g_before = total[ph] - carry[1] - from_s
                fail = jnp.exp(-sp)
                dz = fail * (g + g_before) - g_before
                if mask is not None:
                    dz = jnp.where(mask, dz, 0.0)
                dzs.append(dz.astype(BF16))
            out = []
            for ph, (_, sp), a, dz, later, from_s, carry in zip(heads, scores, a_s, dzs, laters, sinces, carries):
                dk_s[at, cols[ph[0]]] += _dot(dz, q[ph], TN)
                dv_s[at, cols[ph[0]]] += _dot(a, do[ph], TN)
                out.append((carry[0] + later[:, 0:1] + sp[:, 0:1], carry[1] + from_s[:, 0:1],
                            carry[2] + _dot(dz, _only(lanes[ph[1]], ks[ph[0]]))))
            return tuple(out)

        col = jnp.zeros((B, 1), F32)
        zero = (col, col, jnp.zeros((B, PAIR), F32))
        carries = tiles(i, (zero,) * len(heads), _strict_causal())
        last = lax.fori_loop(0, i, lambda jj, cs: tiles(i - 1 - jj, cs, None), carries)
        for p in range(SB_PAIRS):
            dq_ref[:, cols[p]] = ((last[2 * p][2] + last[2 * p + 1][2]) * ATT_SCALE).astype(BF16)

        @pl.when(i == n_blocks - 1)
        def _():
            dk_ref[...] = dk_s[...].astype(BF16)
            dv_ref[...] = dv_s[...].astype(BF16)

    blk = lambda off: pl.BlockSpec((B, W), lambda g, i: (i, g + off))
    full = lambda off: pl.BlockSpec((T, W), lambda g, i: (0, g + off))
    out = _out((T, N_HEADS * HEAD_DIM), BF16)
    return pl.pallas_call(
        body, name=name, grid=(steps, n_blocks),
        in_specs=[blk(0), full(steps), full(2 * steps), blk(0), blk(0)],
        out_specs=[blk(0), full(0), full(0)],
        out_shape=[out, out, out],
        scratch_shapes=[pltpu.VMEM((T, W), F32)] * 2,
        compiler_params=_params(2, VMEM_LIMIT),
    )(*_hbm(qkv, qkv, qkv, do, o))


NEAR = BAND - PAD + REL_CLIP
FAR = BAND - NEAR
NEAR_REL = 2 * REL_CLIP
BIAS_ROWS = 8


def _rel_onehot(i, transposed):
    shape = (NEAR, NEAR_REL) if transposed else (NEAR_REL, NEAR)
    j = FAR + lax.broadcasted_iota(jnp.int32, shape, 0 if transposed else 1)
    r = lax.broadcasted_iota(jnp.int32, shape, 1 if transposed else 0)
    idx = jnp.clip(i + PAD - j, -REL_CLIP, REL_CLIP) + REL_CLIP
    return jnp.where(idx - 1 == r, 1.0, 0.0).astype(BF16)


def _bias_table(rel_bias, name):
    def body(near_ref, far_ref, o_ref):
        rb = near_ref[...]
        hi, lo = _split2(rb)
        lo2 = (rb - hi.astype(F32) - lo.astype(F32)).astype(BF16)
        far = jnp.broadcast_to(far_ref[...], (N_HEADS, FAR))
        for k in range(BIAS_ROWS):
            onehot = _rel_onehot(pl.program_id(0) * BIAS_ROWS + k, False)
            o_ref[k, :, :FAR] = far
            o_ref[k, :, FAR:] = _dot(hi, onehot) + _dot(lo, onehot) + _dot(lo2, onehot)

    return pl.pallas_call(
        body, name=name, grid=(CHUNK // BIAS_ROWS,),
        in_specs=[pl.BlockSpec((N_HEADS, NEAR_REL), lambda i: (0, 0)), pl.BlockSpec((N_HEADS, 1), lambda i: (0, 0))],
        out_specs=pl.BlockSpec((BIAS_ROWS, N_HEADS, BAND), lambda i: (i, 0, 0)),
        out_shape=_out((CHUNK, N_HEADS, BAND), F32),
        compiler_params=_params(1),
    )(*_hbm(rel_bias[:, 1:], rel_bias[:, N_REL - 1:]))


def _bias_grad(dbias_t, name):
    def body(d_ref, near_ref, far_ref):
        near, far = None, None
        for k in range(BIAS_ROWS):
            onehot = _rel_onehot(pl.program_id(0) * BIAS_ROWS + k, True)
            hi, lo = _split2(d_ref[k, :, FAR:])
            part = _dot(hi, onehot) + _dot(lo, onehot)
            rest = jnp.sum(d_ref[k, :, :FAR], axis=1, keepdims=True)
            near, far = (part, rest) if near is None else (near + part, far + rest)
        first = pl.program_id(0) == 0
        _accumulate(near_ref, near, first)
        _accumulate(far_ref, jnp.broadcast_to(far, far_ref.shape), first)

    near, far = pl.pallas_call(
        body, name=name, grid=(CHUNK // BIAS_ROWS,),
        in_specs=[pl.BlockSpec((BIAS_ROWS, N_HEADS, BAND), lambda i: (i, 0, 0))],
        out_specs=[pl.BlockSpec((N_HEADS, NEAR_REL), lambda i: (0, 0)), pl.BlockSpec((N_HEADS, 128), lambda i: (0, 0))],
        out_shape=[_out((N_HEADS, NEAR_REL), F32), _out((N_HEADS, 128), F32)],
        compiler_params=_params(1),
    )(*_hbm(dbias_t))
    return jnp.pad(near, ((0, 0), (1, 0))).at[:, N_REL - 1].add(far[:, 0])


def _ch_probs(scores, bias, valid):
    z = jnp.where(valid, scores * ATT_SCALE + bias, NEG_INF)
    e = jnp.exp(z - jnp.max(z, axis=-1, keepdims=True))
    return e / jnp.sum(e, axis=-1, keepdims=True)


CH_HEADS = [(pair, h) for pair in range(N_HEADS // 2) for h in range(2)]
CH_COLS = [slice(pair * PAIR, (pair + 1) * PAIR) for pair in range(N_HEADS // 2)]


CH_GROUP = 2
CH_Q = CH_GROUP * CHUNK
CH_WIN = (LOOKBACK + CH_GROUP) * CHUNK


def _ch_valid(n):
    row_chunk = lax.broadcasted_iota(jnp.int32, (CH_Q, CH_WIN), 0) // CHUNK
    slot = lax.broadcasted_iota(jnp.int32, (CH_Q, CH_WIN), 1)
    ahead = slot // CHUNK - row_chunk
    return (ahead >= 0) & (ahead <= LOOKBACK) & (n * CH_Q + slot >= PAD)


def _ch_group_bias(bias):
    shifted = [jnp.pad(bias, ((0, 0), (0, 0), (c * CHUNK, (CH_GROUP - 1 - c) * CHUNK))) for c in range(CH_GROUP)]
    return jnp.concatenate(shifted, axis=1)


def _ch_fold_bias_grad(dbias):
    parts = [dbias[:, c * CHUNK:(c + 1) * CHUNK, c * CHUNK:c * CHUNK + BAND] for c in range(CH_GROUP)]
    return sum(parts[1:], parts[0])


def _ch_fwd(qkv, bias, name):
    T = qkv.shape[0]
    W = N_HEADS * HEAD_DIM

    def body(q_ref, k_ref, v_ref, b_ref, o_ref, kp, vp):
        n = pl.program_id(0)

        @pl.when(n == 0)
        def _():
            _ch_load_padded(k_ref, v_ref, kp, vp)

        win = pl.ds(pl.multiple_of(n * CH_Q, CH_Q), CH_WIN)
        valid = _ch_valid(n)
        lanes = _pair_lanes()
        scores = [_dot(_only(lanes[h], q_ref[:, CH_COLS[pair]]), kp[win, CH_COLS[pair]], NT) for pair, h in CH_HEADS]
        probs = [_ch_probs(s, b_ref[2 * pair + h], valid).astype(BF16) for s, (pair, h) in zip(scores, CH_HEADS)]
        outs = [_dot(p, _only(lanes[h], vp[win, CH_COLS[pair]])) for p, (pair, h) in zip(probs, CH_HEADS)]
        for pair, cols in enumerate(CH_COLS):
            o_ref[:, cols] = outs[2 * pair] + outs[2 * pair + 1]

    full = lambda col: pl.BlockSpec((T, W), lambda n: (0, col))
    return pl.pallas_call(
        body, name=name, grid=(T // CH_Q,),
        in_specs=[pl.BlockSpec((CH_Q, W), lambda n: (n, 3)), full(4), full(5),
                  pl.BlockSpec((N_HEADS, CH_Q, CH_WIN), lambda n: (0, 0, 0))],
        out_specs=pl.BlockSpec((CH_Q, W), lambda n: (n, 0)),
        out_shape=_out((T, W), F32),
        scratch_shapes=[pltpu.VMEM((PAD + T, W), BF16)] * 2,
        compiler_params=_params(1, VMEM_LIMIT),
    )(*_hbm(qkv, qkv, qkv, bias))


def _ch_load_padded(k_ref, v_ref, kp, vp):
    for src, dst in ((k_ref, kp), (v_ref, vp)):
        dst[:PAD, :] = jnp.zeros((PAD, dst.shape[1]), dst.dtype)
        dst[PAD:, :] = src[...]


def _ch_bwd(qkv, bias, do, name):
    T = qkv.shape[0]
    W = N_HEADS * HEAD_DIM
    n_chunks = T // CH_Q

    def body(q_ref, k_ref, v_ref, b_ref, do_ref, dq_ref, dk_ref, dv_ref, db_ref, kp, vp, dk_s, dv_s):
        n = pl.program_id(0)

        @pl.when(n == 0)
        def _():
            _ch_load_padded(k_ref, v_ref, kp, vp)
            dk_s[...] = jnp.zeros_like(dk_s)
            dv_s[...] = jnp.zeros_like(dv_s)
            db_ref[...] = jnp.zeros_like(db_ref)

        win = pl.ds(pl.multiple_of(n * CH_Q, CH_Q), CH_WIN)
        valid = _ch_valid(n)
        lanes = _pair_lanes()
        kws = [kp[win, cols] for cols in CH_COLS]
        vws = [vp[win, cols] for cols in CH_COLS]
        qs = [_only(lanes[h], q_ref[:, CH_COLS[pair]]) for pair, h in CH_HEADS]
        dos = [_only(lanes[h], do_ref[:, CH_COLS[pair]].astype(BF16)) for pair, h in CH_HEADS]
        scores = [_dot(q, kws[pair], NT) for q, (pair, _) in zip(qs, CH_HEADS)]
        dps = [_dot(do, vws[pair], NT) for do, (pair, _) in zip(dos, CH_HEADS)]
        probs = [_ch_probs(s, b_ref[2 * pair + h], valid) for s, (pair, h) in zip(scores, CH_HEADS)]
        dzs = [p * (dp - jnp.sum(dp * p, axis=-1, keepdims=True)) for p, dp in zip(probs, dps)]
        for k, dz in enumerate(dzs):
            db_ref[k] += dz
        dzbs = [(dz * ATT_SCALE).astype(BF16) for dz in dzs]
        dqs = [_dot(dz, _only(lanes[h], kws[pair])) for dz, (pair, h) in zip(dzbs, CH_HEADS)]
        dks = [_dot(dz, q, TN) for dz, q in zip(dzbs, qs)]
        dvs = [_dot(p.astype(BF16), do, TN) for p, do in zip(probs, dos)]
        for pair, cols in enumerate(CH_COLS):
            dq_ref[:, cols] = (dqs[2 * pair] + dqs[2 * pair + 1]).astype(BF16)
            dk_s[win, cols] += dks[2 * pair] + dks[2 * pair + 1]
            dv_s[win, cols] += dvs[2 * pair] + dvs[2 * pair + 1]

        @pl.when(n == n_chunks - 1)
        def _():
            dk_ref[...] = dk_s[PAD:, :].astype(BF16)
            dv_ref[...] = dv_s[PAD:, :].astype(BF16)

    full = lambda col: pl.BlockSpec((T, W), lambda n: (0, col))
    blk = lambda col: pl.BlockSpec((CH_Q, W), lambda n: (n, col))
    tab = pl.BlockSpec((N_HEADS, CH_Q, CH_WIN), lambda n: (0, 0, 0))
    out = _out((T, W), BF16)
    return pl.pallas_call(
        body, name=name, grid=(n_chunks,),
        in_specs=[blk(3), full(4), full(5), tab, blk(0)],
        out_specs=[blk(0), full(0), full(0), tab],
        out_shape=[out, out, out, _out((N_HEADS, CH_Q, CH_WIN), F32)],
        scratch_shapes=[pltpu.VMEM((PAD + T, W), BF16)] * 2 + [pltpu.VMEM((PAD + T, W), F32)] * 2,
        compiler_params=_params(1, VMEM_LIMIT),
    )(*_hbm(qkv, qkv, qkv, bias, do))


def _rows_split(a, parts):
    return a.reshape(a.shape[:-2] + (parts, a.shape[-2] // parts, a.shape[-1]))


def _cast_into_slot0(c, ws, name):
    parts = 2
    ws = [_rows_split(_rows_split(w, 2), parts) for w in ws]
    n = len(ws)

    def body(c_ref, *refs):
        for src, dst in zip(refs[:n], refs[n:]):
            dst[0, 0, 0] = src[0, 0].astype(BF16)

    outs = pl.pallas_call(
        body, name=name,
        grid_spec=pltpu.PrefetchScalarGridSpec(
            num_scalar_prefetch=1, grid=(2, parts),
            in_specs=[pl.BlockSpec((1, 1) + w.shape[2:], lambda d, r, c_ref: (d ^ c_ref[0], r, 0, 0)) for w in ws],
            out_specs=[pl.BlockSpec((1, 1, 1) + w.shape[2:], lambda d, r, c_ref: (0, d, r, 0, 0)) for w in ws]),
        out_shape=[_out((N_CHIPS,) + w.shape, BF16) for w in ws],
        compiler_params=_params(2, VMEM_LIMIT),
    )(c, *_hbm(*ws))
    return [o.reshape(N_CHIPS, 2, o.shape[2] * o.shape[3], o.shape[4]) for o in outs]


def _chip_order(me, c, lands, name):
    parts = 2
    xs = [_rows_split(x, parts) for x in lands]

    def body(me_ref, c_ref, *refs):
        n = len(refs) // 2
        for src, dst in zip(refs[:n], refs[n:]):
            dst[...] = src[...]

    outs = pl.pallas_call(
        body, name=name,
        grid_spec=pltpu.PrefetchScalarGridSpec(
            num_scalar_prefetch=2, grid=(N_CHIPS, 2, parts),
            in_specs=[pl.BlockSpec((1, 1, 1) + x.shape[3:],
                                   lambda j, h, r, me_ref, c_ref: (j ^ me_ref[0], h ^ c_ref[0], r, 0, 0)) for x in xs],
            out_specs=[pl.BlockSpec((1, 1, 1) + x.shape[3:], lambda j, h, r, me_ref, c_ref: (j, h, r, 0, 0))
                       for x in xs]),
        out_shape=[_out(x.shape, x.dtype) for x in xs],
        compiler_params=_params(3, VMEM_LIMIT),
    )(me, c, *_hbm(*xs))
    return [o.reshape(o.shape[0], 2 * parts * o.shape[3], o.shape[4]) for o in outs]


def _pair_add(c, mine, got, permuted, name):
    parts = 2
    mine = [_rows_split(m, parts) for m in mine]
    got = [_rows_split(g, parts) for g in got]
    n = len(mine)

    def body(c_ref, *refs):
        for a, b, o in zip(refs[:n], refs[n:2 * n], refs[2 * n:]):
            o[0, 0] = (a[0, 0, 0] + b[0, 0].astype(F32)).astype(BF16)

    def mine_spec(m, perm):
        if perm:
            return pl.BlockSpec((1, 1, 1) + m.shape[3:], lambda j, r, c_ref: (j, 0, r, 0, 0))
        return pl.BlockSpec((1, 1, 1) + m.shape[3:], lambda j, r, c_ref: (j, c_ref[0], r, 0, 0))

    def got_spec(g):
        return pl.BlockSpec((1, 1) + g.shape[2:], lambda j, r, c_ref: (j, r, 0, 0))

    outs = pl.pallas_call(
        body, name=name,
        grid_spec=pltpu.PrefetchScalarGridSpec(
            num_scalar_prefetch=1, grid=(N_CHIPS, parts),
            in_specs=[mine_spec(m, perm) for m, perm in zip(mine, permuted)] + [got_spec(g) for g in got],
            out_specs=[got_spec(g) for g in got]),
        out_shape=[_out(g.shape, BF16) for g in got],
        compiler_params=_params(2, VMEM_LIMIT),
    )(c, *_hbm(*mine, *got))
    return [o.reshape(o.shape[0], o.shape[1] * o.shape[2], o.shape[3]) for o in outs]


def _chip_add(me, partials, landed, permuted, name):
    parts = 2
    ps = [_rows_split(x, parts) for x in partials]
    ls = [_rows_split(x, parts) for x in landed]
    n = len(ps)

    def body(me_ref, *refs):
        for own, got, o in zip(refs[:n], refs[n:2 * n], refs[2 * n:]):
            acc = own[0, 0].astype(F32)
            for r in range(N_CHIPS - 1):
                acc = acc + got[r, 0].astype(F32)
            o[0] = acc

    def own_spec(x, perm):
        if perm:
            return pl.BlockSpec((1, 1) + x.shape[2:], lambda r, me_ref: (0, r, 0, 0))
        return pl.BlockSpec((1, 1) + x.shape[2:], lambda r, me_ref: (me_ref[0], r, 0, 0))

    outs = pl.pallas_call(
        body, name=name,
        grid_spec=pltpu.PrefetchScalarGridSpec(
            num_scalar_prefetch=1, grid=(parts,),
            in_specs=[own_spec(x, perm) for x, perm in zip(ps, permuted)]
            + [pl.BlockSpec((N_CHIPS - 1, 1) + x.shape[2:], lambda r, me_ref: (0, r, 0, 0)) for x in ls],
            out_specs=[pl.BlockSpec((1,) + x.shape[2:], lambda r, me_ref: (r, 0, 0)) for x in ps]),
        out_shape=[_out(x.shape[1:], F32) for x in ps],
        compiler_params=_params(1, VMEM_LIMIT),
    )(me, *_hbm(*ps, *ls))
    return [o.reshape(o.shape[0] * o.shape[1], o.shape[2]) for o in outs]


def _adamw_math(w, g, m, v):
    m = ADAM_B1 * m + (1.0 - ADAM_B1) * g
    v = ADAM_B2 * v + (1.0 - ADAM_B2) * (g * g)
    m_hat = m / (1.0 - ADAM_B1 ** ADAM_STEP)
    v_hat = v / (1.0 - ADAM_B2 ** ADAM_STEP)
    delta = -ADAM_LR * (m_hat / (jnp.sqrt(v_hat) + ADAM_EPS) + ADAM_WD * w)
    return delta, m, v


def _adamw(ws, gs, ms, vs, parts, name):
    n = len(ws)
    flat = [_rows_split(a, parts) for a in (*ws, *gs, *ms, *vs)]

    def body(*refs):
        ins, outs = refs[:4 * n], refs[4 * n:]
        for k in range(n):
            d, m, v = _adamw_math(ins[k][...], ins[n + k][...], ins[2 * n + k][...], ins[3 * n + k][...])
            outs[k][...] = d
            outs[n + k][...] = m
            outs[2 * n + k][...] = v

    spec = lambda a: pl.BlockSpec((1,) + a.shape[1:], lambda i: (i, 0, 0))
    outs = pl.pallas_call(
        body, name=name, grid=(parts,),
        in_specs=[spec(a) for a in flat], out_specs=[spec(a) for a in flat[:n]] * 3,
        out_shape=[_out(a.shape, F32) for a in flat[:n]] * 3,
        compiler_params=_params(1, VMEM_LIMIT),
    )(*_hbm(*flat))
    outs = [o.reshape(o.shape[0] * o.shape[1], o.shape[2]) for o in outs]
    return outs[:n], outs[n:2 * n], outs[2 * n:]


def _adamw_halves(c, ws, owns, others, ms, vs, name):
    parts = 4
    n = len(ws)
    whole = [_rows_split(_rows_split(a, 2), parts) for a in (*ws, *ms, *vs)]
    halves = [_rows_split(a, parts) for a in (*owns, *others)]

    def body(c_ref, *refs):
        ins, outs = refs[:5 * n], refs[5 * n:]
        mine = pl.program_id(0) == c_ref[0]
        for k in range(n):
            g = jnp.where(mine, ins[3 * n + k][0], ins[4 * n + k][0])
            d, m, v = _adamw_math(ins[k][0, 0], g, ins[n + k][0, 0], ins[2 * n + k][0, 0])
            for slot, val in enumerate((g, d, m, v)):
                outs[slot * n + k][0, 0] = val

    wspec = lambda a: pl.BlockSpec((1, 1) + a.shape[2:], lambda h, r, c_ref: (h, r, 0, 0))
    hspec = lambda a: pl.BlockSpec((1,) + a.shape[1:], lambda h, r, c_ref: (r, 0, 0))
    outs = pl.pallas_call(
        body, name=name,
        grid_spec=pltpu.PrefetchScalarGridSpec(
            num_scalar_prefetch=1, grid=(2, parts),
            in_specs=[wspec(a) for a in whole] + [hspec(a) for a in halves],
            out_specs=[wspec(a) for a in whole[:n]] * 4),
        out_shape=[_out(a.shape, F32) for a in whole[:n]] * 4,
        compiler_params=_params(2, VMEM_LIMIT),
    )(c, *_hbm(*whole, *halves))
    outs = [o.reshape(2 * parts * o.shape[2], o.shape[3]) for o in outs]
    return outs[:n], outs[n:2 * n], outs[2 * n:3 * n], outs[3 * n:]


def _place():
    x, y, c = lax.axis_index("x"), lax.axis_index("y"), lax.axis_index("c")
    peers = [(x ^ (r >> 1), y ^ (r & 1), c) for r in (1, 2, 3)]
    return x, y, c, peers


def _handshake(peers):
    barrier = pltpu.get_barrier_semaphore()
    for peer in peers:
        pl.semaphore_signal(barrier, inc=1, device_id=peer, device_id_type=MESH)
    pl.semaphore_wait(barrier, len(peers))


ANY = pl.BlockSpec(memory_space=pl.ANY)
HBM = pl.BlockSpec(memory_space=pltpu.HBM)
SEM = pl.BlockSpec(memory_space=pltpu.SEMAPHORE)
SPLIT_COPY = pltpu.SideEffectType.DATAFLOW_SIDE_EFFECTING


def _in_hbm(a):
    return pltpu.with_memory_space_constraint(a, pltpu.HBM)


def _split_start(body, name, collective_id, operands, n_sems, after=None):
    n = len(operands)
    extra = [] if after is None else [after]

    def wrapped(*refs):
        at = n + len(extra)
        body(refs[:n], refs[at], refs[at + 1])
        token = refs[-1]
        token[...] = jnp.zeros_like(token)

    outs = pl.pallas_call(
        wrapped, name=name,
        in_specs=[HBM] * n + [ANY] * len(extra),
        out_shape=(pltpu.SemaphoreType.DMA((n_sems,)), pltpu.SemaphoreType.DMA((n_sems,)),
                   *[pltpu.HBM(a.shape, a.dtype) for a in operands], jax.ShapeDtypeStruct((8, 128), F32)),
        out_specs=(SEM, SEM, *[HBM] * n, pl.BlockSpec(memory_space=pltpu.VMEM)),
        input_output_aliases={i: 2 + i for i in range(n)},
        compiler_params=pltpu.CompilerParams(has_side_effects=SPLIT_COPY, collective_id=collective_id),
    )(*[_in_hbm(a) for a in operands], *extra)
    return outs[0], outs[1], list(outs[2:2 + n]), outs[-1]


def _split_wait(body, name, send_sem, recv_sem, operands, after):
    n = len(operands)

    def wrapped(*refs):
        body(refs[:n], refs[n], refs[n + 1])

    outs = pl.pallas_call(
        wrapped, name=name,
        in_specs=[HBM] * n + [SEM, SEM, ANY],
        out_shape=tuple(pltpu.HBM(a.shape, a.dtype) for a in operands),
        out_specs=tuple([HBM] * n),
        input_output_aliases={i: i for i in range(n)},
        compiler_params=pltpu.CompilerParams(has_side_effects=SPLIT_COPY),
    )(*operands, send_sem, recv_sem, after)
    return list(outs)


def _gather_copies(lands, send_sem, recv_sem):
    peers = _place()[3]
    return [pltpu.make_async_remote_copy(
        src_ref=land.at[0, 0], dst_ref=land.at[r + 1, 0],
        send_sem=send_sem.at[a * 3 + r], recv_sem=recv_sem.at[a * 3 + r],
        device_id=peers[r], device_id_type=MESH) for a, land in enumerate(lands) for r in range(3)]


def _gather_start(lands, name, collective_id, after):
    def body(refs, send_sem, recv_sem):
        _handshake(_place()[3])
        for cp in _gather_copies(refs, send_sem, recv_sem):
            cp.start()

    return _split_start(body, name, collective_id, list(lands), 3 * len(lands), after)


def _gather_wait(send_sem, recv_sem, operands, after, name):
    def body(refs, send_sem, recv_sem):
        for cp in _gather_copies(refs, send_sem, recv_sem):
            cp.wait_send()
            cp.wait_recv()

    return _split_wait(body, name, send_sem, recv_sem, operands, after)


def _gather_finish(lands, with_ici, name):
    n = len(lands)

    def body(*refs):
        land = refs[n:2 * n]
        send_ici, recv_ici, send_d2d, recv_d2d = refs[2 * n:]
        x, y, c, _ = _place()
        ici = _gather_copies(land, send_ici, recv_ici) if with_ici else []
        for cp in ici:
            cp.start()
        passed = [pltpu.make_async_remote_copy(
            src_ref=land[a].at[r + 1, 0], dst_ref=land[a].at[r + 1, 1],
            send_sem=send_d2d.at[a * 3 + r], recv_sem=recv_d2d.at[a * 3 + r],
            device_id=(x, y, 1 - c), device_id_type=MESH) for a in range(n) for r in range(3)]
        for k, cp in enumerate(passed):
            if with_ici:
                ici[k].wait_recv()
            cp.start()
        for cp in passed:
            cp.wait_recv()
        for cp in ici:
            cp.wait_send()
        for cp in passed:
            cp.wait_send()

    outs = pl.pallas_call(
        body, name=name,
        in_specs=[ANY] * n, out_specs=[ANY] * n,
        out_shape=[_out(l.shape, l.dtype) for l in lands],
        input_output_aliases={a: a for a in range(n)},
        scratch_shapes=[pltpu.SemaphoreType.DMA((3 * n,))] * 4,
    )(*lands)
    return list(outs)


def _slabs(land):
    return land.reshape(N_CHIPS, 2 * land.shape[2], land.shape[3])


def _pair_swap(grads, permuted, name):
    n = len(grads)

    def body(*refs):
        src, dst = refs[:n], refs[n:2 * n]
        send_sem, recv_sem = refs[2 * n:]
        x, y, c, _ = _place()
        copies = [pltpu.make_async_remote_copy(
            src_ref=src[a].at[:, 1] if permuted[a] else src[a].at[:, 1 - c], dst_ref=dst[a],
            send_sem=send_sem.at[a], recv_sem=recv_sem.at[a],
            device_id=(x, y, 1 - c), device_id_type=MESH) for a in range(n)]
        for cp in copies:
            cp.start()
        for cp in copies:
            cp.wait()

    return pl.pallas_call(
        body, name=name,
        in_specs=[ANY] * n, out_specs=[ANY] * n,
        out_shape=[_out((N_CHIPS,) + g.shape[2:], g.dtype) for g in grads],
        scratch_shapes=[pltpu.SemaphoreType.DMA((n,))] * 2,
    )(*grads)


def _scatter_copies(refs, permuted, send_sem, recv_sem):
    n = len(refs) // 2
    x, y, _, peers = _place()
    me = 2 * x + y
    return [pltpu.make_async_remote_copy(
        src_ref=refs[a].at[r + 1] if permuted[a] else refs[a].at[me ^ (r + 1)], dst_ref=refs[n + a].at[r],
        send_sem=send_sem.at[a * 3 + r], recv_sem=recv_sem.at[a * 3 + r],
        device_id=peers[r], device_id_type=MESH) for a in range(n) for r in range(3)]


def _scatter_start(partials, permuted, name, collective_id):
    def body(refs, send_sem, recv_sem):
        _handshake(_place()[3])
        for cp in _scatter_copies(refs, permuted, send_sem, recv_sem):
            cp.start()

    lands = [lax.empty((N_CHIPS - 1,) + p.shape[1:], p.dtype) for p in partials]
    return _split_start(body, name, collective_id, list(partials) + lands, 3 * len(partials))


def _scatter_wait(send_sem, recv_sem, operands, permuted, after, name):
    def body(refs, send_sem, recv_sem):
        for cp in _scatter_copies(refs, permuted, send_sem, recv_sem):
            cp.wait_send()
            cp.wait_recv()

    return _split_wait(body, name, send_sem, recv_sem, operands, after)


def _pair_join(halves, name):
    n = len(halves)

    def body(*refs):
        src, dst = refs[:n], refs[n:2 * n]
        send_sem, recv_sem = refs[2 * n:]
        x, y, c, _ = _place()
        copies = [pltpu.make_async_remote_copy(
            src_ref=src[a], dst_ref=dst[a], send_sem=send_sem.at[a], recv_sem=recv_sem.at[a],
            device_id=(x, y, 1 - c), device_id_type=MESH) for a in range(n)]
        for cp in copies:
            cp.start()
        for cp in copies:
            cp.wait()

    return pl.pallas_call(
        body, name=name,
        in_specs=[ANY] * n, out_specs=[ANY] * n,
        out_shape=[_out(h.shape, F32) for h in halves],
        scratch_shapes=[pltpu.SemaphoreType.DMA((n,))] * 2,
    )(*halves)


def _all_sum_small(v, after, name):
    R, C = v.shape
    n_dev = 8

    def body(v_ref, after_ref, o_ref, buf, send_sem, recv_sem):
        x, y, c, _ = _place()
        me = 4 * x + 2 * y + c
        buf[me] = v_ref[...]
        copies = []
        for k in range(1, n_dev):
            peer = (x ^ (k >> 2), y ^ ((k >> 1) & 1), c ^ (k & 1))
            copies.append(pltpu.make_async_remote_copy(
                src_ref=v_ref, dst_ref=buf.at[me], send_sem=send_sem.at[k - 1], recv_sem=recv_sem.at[k - 1],
                device_id=peer, device_id_type=MESH))
        for cp in copies:
            cp.start()
        for cp in copies:
            cp.wait()
        acc = buf[0]
        for m in range(1, n_dev):
            acc = acc + buf[m]
        o_ref[...] = acc

    return pl.pallas_call(
        body, name=name,
        in_specs=[pl.BlockSpec(memory_space=pltpu.VMEM), ANY], out_specs=pl.BlockSpec(memory_space=pltpu.VMEM),
        out_shape=jax.ShapeDtypeStruct((R, C), F32),
        scratch_shapes=[pltpu.VMEM((n_dev, R, C), F32), pltpu.SemaphoreType.DMA((n_dev - 1,)),
                        pltpu.SemaphoreType.DMA((n_dev - 1,))],
    )(v, after)


class _WholeWeights:
    def __init__(self, w):
        self.w = w

    def weights(self, group, after=None):
        return self.w, None

    def grads_ready(self, group, gw):
        return None


def _local_step(x, p, target, gains, rel_bias, hooks):
    T, D = x.shape
    S = N_CHIPS

    tied = lambda gain, token: gain if token is None else gain + token[0, 0]
    w, token = hooks.weights("first")
    w = dict(w)
    h1, xn1, g1, u1, a1, f1 = _ffn_fwd(x, tied(gains["ffn1_pre"], token), gains["ffn1_post"], w["ffn1_gate"],
                                       w["ffn1_up"], w["ffn1_down"], "ffn1_fwd")
    more, token = hooks.weights("in", h1)
    w.update(more)
    qkv, un = _norm_proj(h1, tied(gains["mix_pre"], token), w["in"], "qkv_proj")
    bias = _ch_group_bias(_bias_table(rel_bias, "bias_table").transpose(1, 0, 2))
    o_a = _sb_fwd(qkv, "sb_fwd")
    o_b = _ch_fwd(qkv, bias, "ch_fwd")
    w.update(hooks.weights("rest", o_b)[0])
    w_out = w["out"].reshape(D, D)
    h2, mixed, mo = _mix_out_fwd(h1, o_a, o_b, gains["out_sb"], gains["out_ch"], w_out, gains["mix_post"],
                                 "mix_out_fwd")
    h3, xn2, g2, u2, a2, f2 = _ffn_fwd(h2, gains["ffn2_pre"], gains["ffn2_post"], w["ffn2_gate"], w["ffn2_up"],
                                       w["ffn2_down"], "ffn2_fwd")
    w_ple_proj = w["ple_proj"].transpose(1, 0, 2).reshape(p.shape[1], D)
    w_ple_gate = w["ple_gate"].reshape(D, D)

    loss, dh3, dproj, dgate, dg_ple = _ple_loss(h3, p, target, w_ple_proj, w_ple_gate, gains["ple_post"], "ple_loss")
    gw, gg = {}, {"ple_post": dg_ple}
    gw["ple_proj"] = _mm_tn(p[None], dproj, p.shape[1], "dw_ple_proj")
    row_sharded = lambda pair: tuple(o.reshape(S, D // S, D) for o in pair)
    gw["ple_gate"] = row_sharded(_mm_tn(h3[None], dgate[None], 512, "dw_ple_gate"))

    def ffn_bwd(tag, dh, x_in, xn, g_act, u_act, a_act, f, group):
        dgp, dup, df, gg[tag + "_post"] = _ffn_bwd_act(dh, f, gains[tag + "_post"], w[tag + "_down"], g_act, u_act,
                                                       tag + "_bwd_act")
        gw[tag + "_gate"] = _mm_tn(dgp, xn[None], dgp.shape[2], "dw_" + tag + "_gate")
        gw[tag + "_up"] = _mm_tn(dup, xn[None], dup.shape[2], "dw_" + tag + "_up")
        gw[tag + "_down"] = _mm_tn(a_act, df[None], a_act.shape[2], "dw_" + tag + "_down")
        g_pre = gains[tag + "_pre"]
        if group is not None:
            token = hooks.grads_ready(group, gw)
            g_pre = g_pre if token is None else g_pre + token[0, 0]
        dx, gg[tag + "_pre"] = _proj_bwd([dgp, dup], [w[tag + "_gate"], w[tag + "_up"]], x_in, g_pre, dh,
                                         tag + "_bwd_in")
        return dx

    dh2 = ffn_bwd("ffn2", dh3, h2, xn2, g2, u2, a2, f2, None)
    dmo, do_a, do_b, gg["mix_post"], gg["out_sb"], gg["out_ch"] = _mix_out_bwd(
        dh2, mo, gains["mix_post"], w_out, o_a, o_b, gains["out_sb"], gains["out_ch"], "mix_out_bwd")
    gw["out"] = row_sharded(_mm_tn(mixed[None], dmo[None], 512, "dw_out"))
    token = hooks.grads_ready("early", gw)
    if token is not None:
        do_a = do_a + token[0, 0]
    dq_a, dk_a, dv_a = _sb_bwd(qkv, do_a, o_a, "sb_bwd")
    dq_b, dk_b, dv_b, dbias = _ch_bwd(qkv, bias, do_b, "ch_bwd")
    g_rel = _bias_grad(_ch_fold_bias_grad(dbias).transpose(1, 0, 2), "bias_grad")
    dqkv = jnp.concatenate([dq_a, dk_a, dv_a, dq_b, dk_b, dv_b], axis=1)
    gw["in"] = _mm_tn(un[None], dqkv, 512, "dw_in", groups=S)
    dh1, gg["mix_pre"] = _proj_bwd([dqkv], [w["in"]], h1, gains["mix_pre"], dh2, "qkv_bwd_in")
    dx = ffn_bwd("ffn1", dh1, x, xn1, g1, u1, a1, f1, "late")
    return loss, dx, gw, gg, g_rel


BIG = ["ffn1_gate", "ffn1_up", "ffn1_down", "in", "out", "ffn2_gate", "ffn2_up", "ffn2_down", "ple_proj", "ple_gate"]
GAINS = ["ffn1_pre", "ffn1_post", "mix_pre", "mix_post", "out_sb", "out_ch", "ffn2_pre", "ffn2_post", "ple_post"]
TRANSPOSED = ("w_ffn1_gate", "w_ffn1_up", "w_ffn2_gate", "w_ffn2_up")
PERMUTED = ("ffn1_gate", "ffn1_up", "ffn1_down", "ffn2_gate", "ffn2_up", "ffn2_down")
W_GROUPS = {"first": ["ffn1_gate", "ffn1_up", "ffn1_down"], "in": ["in"],
            "rest": ["out", "ffn2_gate", "ffn2_up", "ffn2_down", "ple_proj", "ple_gate"]}
G_GROUPS = {"early": ["ple_proj", "ple_gate", "ffn2_gate", "ffn2_up", "ffn2_down", "out"],
            "late": ["in", "ffn1_gate", "ffn1_up", "ffn1_down"]}
ORDER = ["g_ffn1_pre", "g_ffn1_post", "w_ffn1_gate", "w_ffn1_up", "w_ffn1_down", "g_mix_pre", "g_mix_post", "w_in",
         "g_out_sb", "g_out_ch", "rel_bias", "w_out", "g_ffn2_pre", "g_ffn2_post", "w_ffn2_gate", "w_ffn2_up",
         "w_ffn2_down", "w_ple_proj", "w_ple_gate", "g_ple_post"]


def kernel(x, p, g_ffn1_pre, g_ffn1_post, w_ffn1_gate, w_ffn1_up, w_ffn1_down, g_mix_pre, g_mix_post, w_in, g_out_sb, g_out_ch, rel_bias, w_out, g_ffn2_pre, g_ffn2_post, w_ffn2_gate, w_ffn2_up, w_ffn2_down, w_ple_proj, w_ple_gate, g_ple_post, loss_target, m_g_ffn1_pre, m_g_ffn1_post, m_w_ffn1_gate, m_w_ffn1_up, m_w_ffn1_down, m_g_mix_pre, m_g_mix_post, m_w_in, m_g_out_sb, m_g_out_ch, m_rel_bias, m_w_out, m_g_ffn2_pre, m_g_ffn2_post, m_w_ffn2_gate, m_w_ffn2_up, m_w_ffn2_down, m_w_ple_proj, m_w_ple_gate, m_g_ple_post, v_g_ffn1_pre, v_g_ffn1_post, v_w_ffn1_gate, v_w_ffn1_up, v_w_ffn1_down, v_g_mix_pre, v_g_mix_post, v_w_in, v_g_out_sb, v_g_out_ch, v_rel_bias, v_w_out, v_g_ffn2_pre, v_g_ffn2_post, v_w_ffn2_gate, v_w_ffn2_up, v_w_ffn2_down, v_w_ple_proj, v_w_ple_gate, v_g_ple_post):
    args = dict(locals())
    take = lambda a, n: a[0].T if n in TRANSPOSED else a[0]
    wts = {n: take(args[n], n) for n in ORDER}
    ms = {n: take(args["m_" + n], n) for n in ORDER}
    vs = {n: take(args["v_" + n], n) for n in ORDER}
    gains = {n: wts["g_" + n][None] for n in GAINS}

    c_idx = lax.axis_index("c").astype(jnp.int32).reshape(1)
    me_idx = (2 * lax.axis_index("x") + lax.axis_index("y")).astype(jnp.int32).reshape(1)
    south = lax.axis_index("c") == 0

    lands = dict(zip(BIG, _cast_into_slot0(c_idx, [wts["w_" + n] for n in BIG], "cast_weights")))

    def in_order(names, zones):
        plain = [n for n in names if n not in PERMUTED]
        fixed = dict(zip(plain, _chip_order(me_idx, c_idx, [zones[n] for n in plain], "chip_order_" + plain[0]))
                     ) if plain else {}
        return {n: fixed[n] if n in fixed else _slabs(zones[n]) for n in names}

    class Overlapped:
        def __init__(self):
            self.started = {}
            self.flying = None

        def start(self, group, collective_id, after):
            self.flying = _gather_start([lands[n] for n in W_GROUPS[group]], "gather_%s_start" % group,
                                        collective_id, after)
            return self.flying[3]

        def weights(self, group, after=None):
            names = W_GROUPS[group]
            if group == "first":
                zones = _gather_finish([lands[n] for n in names], True, "gather_first")
                return in_order(names, dict(zip(names, zones))), self.start("in", 1, zones[0])
            send_sem, recv_sem, zones, _ = self.flying
            zones = _gather_wait(send_sem, recv_sem, zones, after, "gather_%s_wait" % group)
            zones = _gather_finish(zones, False, "gather_%s_finish" % group)
            token = self.start("rest", 4, zones[0]) if group == "in" else None
            return in_order(names, dict(zip(names, zones))), token

        def grads_ready(self, group, gw):
            self.started[group] = reduce_start(G_GROUPS[group], gw, group, {"early": 2, "late": 3}[group])
            return self.started[group][-1]

    def reduce_start(names, gw, tag, cid):
        perm = [n in PERMUTED for n in names]
        halved = lambda g: g.reshape(N_CHIPS, 2, g.shape[1] // 2, g.shape[2])
        mine = [halved(gw[n][0]) for n in names]
        got = _pair_swap([halved(gw[n][1]) for n in names], perm, "grad_pair_swap_" + tag)
        partial = _pair_add(c_idx, mine, got, perm, "grad_pair_add_" + tag)
        send_sem, recv_sem, operands, token = _scatter_start(partial, perm, "grad_scatter_start_" + tag, cid)
        return names, perm, send_sem, recv_sem, operands, token

    def reduce_finish(state, after, tag):
        names, perm, send_sem, recv_sem, operands, _ = state
        operands = _scatter_wait(send_sem, recv_sem, operands, perm, after, "grad_scatter_wait_" + tag)
        n = len(names)
        own = _chip_add(me_idx, operands[:n], operands[n:], perm, "grad_chip_add_" + tag)
        return own, _pair_join(own, "grad_pair_join_" + tag)

    hooks = Overlapped()
    loss, dx, gw, gg, g_rel = _local_step(x[0], p[0, 0], loss_target[0], gains, wts["rel_bias"], hooks)

    grads, delta, new_m, new_v = {}, {}, {}, {}

    def finish(group, after):
        own, other = reduce_finish(hooks.started[group], after, group)
        names = ["w_" + n for n in G_GROUPS[group]]
        g, d, m, v = _adamw_halves(c_idx, [wts[n] for n in names], own, other, [ms[n] for n in names],
                                   [vs[n] for n in names], "adamw_" + group)
        for n, gg_, dd, mm, vv in zip(names, g, d, m, v):
            grads[n], delta[n], new_m[n], new_v[n] = gg_, dd, mm, vv
        return d[0]

    finish("late", finish("early", dx))

    pieces = [gg[n].reshape(-1, 128) for n in GAINS] + [jnp.pad(g_rel, ((0, 0), (0, N_REL_PAD - N_REL))).reshape(-1, 128)]
    summed = _all_sum_small(jnp.concatenate(pieces + [loss], axis=0), delta["w_in"], "small_grad_sum")
    at = 0
    for n, piece in zip(GAINS, pieces[:-1]):
        grads["g_" + n] = summed[at:at + piece.shape[0]].reshape(1, -1)[0]
        at += piece.shape[0]
    grads["rel_bias"] = summed[at:at + pieces[-1].shape[0]].reshape(N_HEADS, N_REL_PAD)[:, :N_REL]
    loss = summed[at + pieces[-1].shape[0], 0]

    small = ["g_" + n for n in GAINS] + ["rel_bias"]
    as_rows = lambda a: (a.reshape(-1, 128) if a.size % 128 == 0 else jnp.pad(a, ((0, 0), (0, N_REL_PAD - N_REL))).reshape(-1, 128))
    d, m, v = _adamw([as_rows(wts[n]) for n in small], [as_rows(grads[n]) for n in small],
                     [as_rows(ms[n]) for n in small], [as_rows(vs[n]) for n in small], 1, "adamw_small")
    for n, dd, mm, vv in zip(small, d, m, v):
        back = (lambda a: a.reshape(N_HEADS, N_REL_PAD)[:, :N_REL]) if n == "rel_bias" else (lambda a: a.reshape(-1))
        delta[n], new_m[n], new_v[n] = back(dd), back(mm), back(vv)

    outs = [loss, dx[None]]
    for table in (grads, delta, new_m, new_v):
        outs += [(table[n].T if n in TRANSPOSED else table[n])[None] for n in ORDER]
    return tuple(outs)
```

```python
import functools

import jax
import jax.numpy as jnp
from jax import lax
from jax.experimental import pallas as pl
from jax.experimental.pallas import tpu as pltpu

F32 = jnp.float32
BF16 = jnp.bfloat16
EPS = 1e-6
N_CHIPS = 4
HEAD_DIM = 64
N_HEADS = 8
CHUNK = 64
LOOKBACK = 8
BAND = (LOOKBACK + 1) * CHUNK
PAD = LOOKBACK * CHUNK
REL_CLIP = 128
N_REL = 2 * REL_CLIP + 1
N_REL_PAD = 384
SB_BLOCK = 256
PAIR = 2 * HEAD_DIM
SB_PAIRS = 2
ATT_SCALE = HEAD_DIM ** -0.5
NEG_INF = -1e30
ROW_BLOCK = 512
VMEM_LIMIT = 48 * 1024 * 1024
MESH = pl.DeviceIdType.MESH

ADAM_LR = 0.001
ADAM_B1 = 0.9
ADAM_B2 = 0.999
ADAM_EPS = 1e-08
ADAM_WD = 0.01
ADAM_STEP = 10

NT = (((1,), (1,)), ((), ()))
TN = (((0,), (0,)), ((), ()))


def _params(n_grid, vmem=None):
    return pltpu.CompilerParams(dimension_semantics=("arbitrary",) * n_grid, vmem_limit_bytes=vmem)


def _hbm(*arrays):
    return [pltpu.with_memory_space_constraint(a, pltpu.HBM) for a in arrays]


def _out(shape, dtype):
    return pltpu.HBM(shape, dtype)


def _dot(a, b, dims=None):
    if dims is None:
        return jnp.dot(a, b, preferred_element_type=F32)
    return lax.dot_general(a, b, dims, preferred_element_type=F32)


def _sigmoid(x):
    return 1.0 / (1.0 + jnp.exp(-x))


def _rms_fwd(x, g):
    r = lax.rsqrt(jnp.mean(x * x, axis=-1, keepdims=True) + EPS)
    return x * r * g


def _rms_bwd(x, g, dy):
    r = lax.rsqrt(jnp.mean(x * x, axis=-1, keepdims=True) + EPS)
    xh = x * r
    dg = jnp.sum(dy * xh, axis=0, keepdims=True)
    t = dy * g
    dx = r * (t - xh * jnp.mean(t * xh, axis=-1, keepdims=True))
    return dx, dg


def _accumulate(ref, val, first):
    @pl.when(first)
    def _():
        ref[...] = val

    @pl.when(jnp.logical_not(first))
    def _():
        ref[...] += val


def _split2(x):
    hi = x.astype(BF16)
    lo = (x - hi.astype(F32)).astype(BF16)
    return hi, lo


def _ffn_fwd(x, g_pre, g_post, wg, wu, wd, name):
    T, D = x.shape
    S, FS, _ = wg.shape
    tm = min(ROW_BLOCK, T)

    def body(x_ref, gpre_ref, gpost_ref, wg_ref, wu_ref, wd_ref,
             h_ref, xn_ref, g_ref, u_ref, a_ref, f_ref, xn_s, acc_s):
        k = pl.program_id(1)

        @pl.when(k == 0)
        def _():
            xn_s[...] = _rms_fwd(x_ref[...], gpre_ref[...]).astype(BF16)
            xn_ref[...] = xn_s[...]

        xn = xn_s[...]
        g = _dot(xn, wg_ref[0], NT)
        u = _dot(xn, wu_ref[0], NT)
        g_ref[0] = g
        u_ref[0] = u
        a = (g * _sigmoid(g) * u).astype(BF16)
        a_ref[0] = a
        _accumulate(acc_s, _dot(a, wd_ref[0]), k == 0)

        @pl.when(k == S - 1)
        def _():
            f = acc_s[...]
            f_ref[...] = f
            h_ref[...] = x_ref[...] + 0.5 * _rms_fwd(f, gpost_ref[...])

    row = pl.BlockSpec((tm, D), lambda i, k: (i, 0))
    vec = pl.BlockSpec((1, D), lambda i, k: (0, 0))
    act = pl.BlockSpec((1, tm, FS), lambda i, k: (k, i, 0))
    return pl.pallas_call(
        body, name=name, grid=(T // tm, S),
        in_specs=[row, vec, vec] + [pl.BlockSpec((1, FS, D), lambda i, k: (k, 0, 0))] * 3,
        out_specs=[row, row, act, act, act, row],
        out_shape=[_out((T, D), F32), _out((T, D), BF16),
                   _out((S, T, FS), F32), _out((S, T, FS), F32),
                   _out((S, T, FS), BF16), _out((T, D), F32)],
        scratch_shapes=[pltpu.VMEM((tm, D), BF16), pltpu.VMEM((tm, D), F32)],
        compiler_params=_params(2, VMEM_LIMIT),
    )(*_hbm(x, g_pre, g_post, wg, wu, wd))


def _ffn_bwd_act(dh, f, g_post, wd, g_act, u_act, name):
    T, D = dh.shape
    S, FS, _ = wd.shape
    tm = min(ROW_BLOCK, T)

    def body(dh_ref, f_ref, gpost_ref, wd_ref, g_ref, u_ref, dgp_ref, dup_ref, df_ref, dgain_ref, df_s):
        i, k = pl.program_id(0), pl.program_id(1)

        @pl.when(k == 0)
        def _():
            df, dgain = _rms_bwd(f_ref[...], gpost_ref[...], 0.5 * dh_ref[...])
            df_s[...] = df.astype(BF16)
            df_ref[...] = df_s[...]
            _accumulate(dgain_ref, dgain, i == 0)

        da = _dot(df_s[...], wd_ref[0], NT)
        g = g_ref[0]
        s = _sigmoid(g)
        dup_ref[0] = (da * (g * s)).astype(BF16)
        dgp_ref[0] = (da * u_ref[0] * (s * (1.0 + g * (1.0 - s)))).astype(BF16)

    row = pl.BlockSpec((tm, D), lambda i, k: (i, 0))
    vec = pl.BlockSpec((1, D), lambda i, k: (0, 0))
    act = pl.BlockSpec((1, tm, FS), lambda i, k: (k, i, 0))
    return pl.pallas_call(
        body, name=name, grid=(T // tm, S),
        in_specs=[row, row, vec, pl.BlockSpec((1, FS, D), lambda i, k: (k, 0, 0)), act, act],
        out_specs=[act, act, row, vec],
        out_shape=[_out((S, T, FS), BF16), _out((S, T, FS), BF16),
                   _out((T, D), BF16), _out((1, D), F32)],
        scratch_shapes=[pltpu.VMEM((tm, D), BF16)],
        compiler_params=_params(2, VMEM_LIMIT),
    )(*_hbm(dh, f, g_post, wd, g_act, u_act))


def _proj_bwd(dys, ws, x, g_pre, dh, name):
    T, D = x.shape
    n = len(dys)
    flat = dys[0].ndim == 2
    S = ws[0].shape[0]
    N = ws[0].shape[2] if flat else ws[0].shape[1]
    tm = min(ROW_BLOCK, T)

    def body(*refs):
        dy_refs, w_refs = refs[:n], refs[n:2 * n]
        x_ref, gpre_ref, dh_ref, dx_ref, dgain_ref, acc_s = refs[2 * n:]
        i, k = pl.program_id(0), pl.program_id(1)
        part = None
        for dy_ref, w_ref in zip(dy_refs, w_refs):
            term = _dot(dy_ref[...], w_ref[0], NT) if flat else _dot(dy_ref[0], w_ref[0])
            part = term if part is None else part + term
        _accumulate(acc_s, part, k == 0)

        @pl.when(k == S - 1)
        def _():
            dx, dgain = _rms_bwd(x_ref[...], gpre_ref[...], acc_s[...])
            dx_ref[...] = dh_ref[...] + dx
            _accumulate(dgain_ref, dgain, i == 0)

    row = pl.BlockSpec((tm, D), lambda i, k: (i, 0))
    vec = pl.BlockSpec((1, D), lambda i, k: (0, 0))
    return pl.pallas_call(
        body, name=name, grid=(T // tm, S),
        in_specs=[pl.BlockSpec((tm, N), lambda i, k: (i, k)) if flat else pl.BlockSpec((1, tm, N), lambda i, k: (k, i, 0))] * n
        + [pl.BlockSpec((1,) + ws[0].shape[1:], lambda i, k: (k, 0, 0))] * n + [row, vec, row],
        out_specs=[row, vec],
        out_shape=[_out((T, D), F32), _out((1, D), F32)],
        scratch_shapes=[pltpu.VMEM((tm, D), F32)],
        compiler_params=_params(2, VMEM_LIMIT),
    )(*_hbm(*dys, *ws, x, g_pre, dh))


def _mm_tn(a, b, bm, name, groups=None):
    ga, T, M = a.shape
    if groups is None:
        gb, _, N = b.shape
        b_spec = pl.BlockSpec((1, T, N), (lambda g, m: (g, 0, 0)) if gb > 1 else (lambda g, m: (0, 0, 0)))
    else:
        gb, N = groups, b.shape[1] // groups
        b_spec = pl.BlockSpec((T, N), lambda g, m: (0, g))
    G = max(ga, gb)

    def body(a_ref, b_ref, o_ref, narrow_ref):
        bv = b_ref[0] if groups is None else b_ref[...]
        o_ref[0] = _dot(a_ref[0].astype(BF16), bv.astype(BF16), TN)
        narrow_ref[0] = o_ref[0].astype(BF16)

    out = pl.BlockSpec((1, bm, N), lambda g, m: (g, m, 0))
    return pl.pallas_call(
        body, name=name, grid=(G, M // bm),
        in_specs=[pl.BlockSpec((1, T, bm), (lambda g, m: (g, 0, m)) if ga > 1 else (lambda g, m: (0, 0, m))), b_spec],
        out_specs=[out, out],
        out_shape=[_out((G, M, N), F32), _out((G, M, N), BF16)],
        compiler_params=_params(2, VMEM_LIMIT),
    )(*_hbm(a, b))


def _norm_proj(x, g_pre, w, name):
    T, D = x.shape
    S, _, N = w.shape
    tm = min(ROW_BLOCK, T)

    def body(x_ref, g_ref, w_ref, o_ref, xn_ref, xn_s):
        @pl.when(pl.program_id(1) == 0)
        def _():
            xn_s[...] = _rms_fwd(x_ref[...], g_ref[...]).astype(BF16)
            xn_ref[...] = xn_s[...]

        o_ref[...] = _dot(xn_s[...], w_ref[0]).astype(BF16)

    row = pl.BlockSpec((tm, D), lambda i, k: (i, 0))
    return pl.pallas_call(
        body, name=name, grid=(T // tm, S),
        in_specs=[row, pl.BlockSpec((1, D), lambda i, k: (0, 0)), pl.BlockSpec((1, D, N), lambda i, k: (k, 0, 0))],
        out_specs=[pl.BlockSpec((tm, N), lambda i, k: (i, k)), row],
        out_shape=[_out((T, S * N), BF16), _out((T, D), BF16)],
        scratch_shapes=[pltpu.VMEM((tm, D), BF16)],
        compiler_params=_params(2, VMEM_LIMIT),
    )(*_hbm(x, g_pre, w))


def _mix_out_fwd(h, o_a, o_b, g_sb, g_ch, w_out, g_post, name):
    T, D = h.shape
    W = g_sb.shape[1]
    tm = min(ROW_BLOCK, T)

    def body(h_ref, oa_ref, ob_ref, gsb_ref, gch_ref, w_ref, gpost_ref, h2_ref, mixed_ref, mo_ref):
        mixed_ref[:, :W] = _rms_fwd(oa_ref[...], gsb_ref[...]).astype(BF16)
        mixed_ref[:, W:] = _rms_fwd(ob_ref[...], gch_ref[...]).astype(BF16)
        mo = _dot(mixed_ref[...], w_ref[...])
        mo_ref[...] = mo
        h2_ref[...] = h_ref[...] + _rms_fwd(mo, gpost_ref[...])

    row = pl.BlockSpec((tm, D), lambda i: (i, 0))
    part = pl.BlockSpec((tm, W), lambda i: (i, 0))
    half = pl.BlockSpec((1, W), lambda i: (0, 0))
    return pl.pallas_call(
        body, name=name, grid=(T // tm,),
        in_specs=[row, part, part, half, half, pl.BlockSpec((D, D), lambda i: (0, 0)), pl.BlockSpec((1, D), lambda i: (0, 0))],
        out_specs=[row, row, row],
        out_shape=[_out((T, D), F32), _out((T, D), BF16),
                   _out((T, D), F32)],
        compiler_params=_params(1, VMEM_LIMIT),
    )(*_hbm(h, o_a, o_b, g_sb, g_ch, w_out, g_post))


def _mix_out_bwd(dh, mo, g_post, w_out, o_a, o_b, g_sb, g_ch, name):
    T, D = dh.shape
    W = g_sb.shape[1]
    tm = min(ROW_BLOCK, T)

    def body(dh_ref, mo_ref, gpost_ref, w_ref, oa_ref, ob_ref, gsb_ref, gch_ref,
             dmo_ref, doa_ref, dob_ref, dgpost_ref, dgsb_ref, dgch_ref):
        first = pl.program_id(0) == 0
        dmo, dgpost = _rms_bwd(mo_ref[...], gpost_ref[...], dh_ref[...])
        dmo_ref[...] = dmo.astype(BF16)
        dmix = _dot(dmo_ref[...], w_ref[...], NT)
        doa_ref[...], dgsb = _rms_bwd(oa_ref[...], gsb_ref[...], dmix[:, :W])
        dob_ref[...], dgch = _rms_bwd(ob_ref[...], gch_ref[...], dmix[:, W:])
        _accumulate(dgpost_ref, dgpost, first)
        _accumulate(dgsb_ref, dgsb, first)
        _accumulate(dgch_ref, dgch, first)

    row = pl.BlockSpec((tm, D), lambda i: (i, 0))
    part = pl.BlockSpec((tm, W), lambda i: (i, 0))
    vec = pl.BlockSpec((1, D), lambda i: (0, 0))
    half = pl.BlockSpec((1, W), lambda i: (0, 0))
    return pl.pallas_call(
        body, name=name, grid=(T // tm,),
        in_specs=[row, row, vec, pl.BlockSpec((D, D), lambda i: (0, 0)), part, part, half, half],
        out_specs=[row, part, part, vec, half, half],
        out_shape=[_out((T, D), BF16), _out((T, W), F32),
                   _out((T, W), F32), _out((1, D), F32),
                   _out((1, W), F32), _out((1, W), F32)],
        compiler_params=_params(1, VMEM_LIMIT),
    )(*_hbm(dh, mo, g_post, w_out, o_a, o_b, g_sb, g_ch))


def _ple_loss(h, p, target, w_proj, w_gate, g_post, name):
    T, D = h.shape
    P = p.shape[1]
    S = N_CHIPS
    C = D // S
    tm = min(ROW_BLOCK, T)

    def body(h_ref, p_ref, t_ref, wp_ref, wg_ref, g_ref, loss_ref, dh_ref, dproj_ref, dgate_ref, dgain_ref):
        first = pl.program_id(0) == 0
        h3 = h_ref[...]
        proj = _dot(p_ref[...].astype(BF16), wp_ref[...])
        s = _sigmoid(_dot(h3.astype(BF16), wg_ref[...]))
        e = proj * s
        diff = h3 + _rms_fwd(e, g_ref[...]) - t_ref[...]
        part = 0.5 * jnp.sum(jnp.mean(diff * diff, axis=-1, keepdims=True), axis=0, keepdims=True)
        _accumulate(loss_ref, jnp.broadcast_to(part, loss_ref.shape), first)
        dy = diff * (1.0 / D)
        de, dgain = _rms_bwd(e, g_ref[...], dy)
        _accumulate(dgain_ref, dgain, first)
        dproj = (de * s).astype(BF16)
        for j in range(S):
            dproj_ref[j] = dproj[:, j * C:(j + 1) * C]
        dgate_ref[...] = (de * proj * s * (1.0 - s)).astype(BF16)
        dh_ref[...] = dy + _dot(dgate_ref[...], wg_ref[...], NT)

    row = pl.BlockSpec((tm, D), lambda i: (i, 0))
    vec = pl.BlockSpec((1, D), lambda i: (0, 0))
    return pl.pallas_call(
        body, name=name, grid=(T // tm,),
        in_specs=[row, pl.BlockSpec((tm, P), lambda i: (i, 0)), row,
                  pl.BlockSpec((P, D), lambda i: (0, 0)), pl.BlockSpec((D, D), lambda i: (0, 0)), vec],
        out_specs=[pl.BlockSpec((8, 128), lambda i: (0, 0)), row,
                   pl.BlockSpec((S, tm, C), lambda i: (0, i, 0)), row, vec],
        out_shape=[_out((8, 128), F32), _out((T, D), F32),
                   _out((S, T, C), BF16), _out((T, D), BF16),
                   _out((1, D), F32)],
        compiler_params=_params(1, VMEM_LIMIT),
    )(*_hbm(h, p, target, w_proj, w_gate, g_post))


def _sb_scores(q, kj, mask):
    z = _dot(q, kj, NT)
    sp = jnp.maximum(z, 0.0) + jnp.log(1.0 + jnp.exp(-jnp.abs(z)))
    return z, sp if mask is None else jnp.where(mask, sp, 0.0)


def _strict_causal():
    rows = lax.broadcasted_iota(jnp.int32, (SB_BLOCK, SB_BLOCK), 0)
    cols = lax.broadcasted_iota(jnp.int32, (SB_BLOCK, SB_BLOCK), 1)
    return cols < rows


def _tri(cmp):
    r = lax.broadcasted_iota(jnp.int32, (2 * SB_BLOCK, SB_BLOCK), 0) % SB_BLOCK
    c = lax.broadcasted_iota(jnp.int32, (2 * SB_BLOCK, SB_BLOCK), 1)
    return jnp.where(cmp(r, c), 1.0, 0.0).astype(BF16)


def _cum(x, tri):
    return _dot(jnp.concatenate(_split2(x), axis=1), tri)


def _pair_lanes():
    lane = lax.broadcasted_iota(jnp.int32, (1, PAIR), 1)
    return [lane < HEAD_DIM, lane >= HEAD_DIM]


def _only(lanes, x):
    return jnp.where(lanes, x, jnp.zeros_like(x))


def _sb_fwd(qkv, name):
    T = qkv.shape[0]
    B = SB_BLOCK
    W = SB_PAIRS * PAIR
    steps = N_HEADS // (2 * SB_PAIRS)
    heads = [(p, h) for p in range(SB_PAIRS) for h in range(2)]

    def body(q_ref, k_ref, v_ref, o_ref):
        i = pl.program_id(1)
        after = _tri(lambda r, c: r > c)
        lanes = _pair_lanes()
        cols = [slice(p * PAIR, (p + 1) * PAIR) for p in range(SB_PAIRS)]
        q = {(p, h): _only(lanes[h], q_ref[:, cols[p]] * ATT_SCALE) for p, h in heads}

        def tiles(j, carries, mask):
            at = pl.ds(pl.multiple_of(j * B, B), B)
            scores = [_sb_scores(q[ph], k_ref[at, cols[ph[0]]], mask) for ph in heads]
            laters = [_cum(sp, after) for _, sp in scores]
            out = []
            for ph, (z, sp), later, (run, acc) in zip(heads, scores, laters, carries):
                a = jnp.exp(z - sp - later - run)
                if mask is not None:
                    a = jnp.where(mask, a, 0.0)
                out.append((run + later[:, 0:1] + sp[:, 0:1],
                            acc + _dot(a.astype(BF16), _only(lanes[ph[1]], v_ref[at, cols[ph[0]]]))))
            return tuple(out)

        zero = (jnp.zeros((B, 1), F32), jnp.zeros((B, PAIR), F32))
        carries = tiles(i, (zero,) * len(heads), _strict_causal())
        carries = lax.fori_loop(0, i, lambda jj, cs: tiles(i - 1 - jj, cs, None), carries)
        for p in range(SB_PAIRS):
            o_ref[:, cols[p]] = carries[2 * p][1] + carries[2 * p + 1][1]

    blk = lambda off: pl.BlockSpec((B, W), lambda g, i: (i, g + off))
    full = lambda off: pl.BlockSpec((T, W), lambda g, i: (0, g + off))
    return pl.pallas_call(
        body, name=name, grid=(steps, T // B),
        in_specs=[blk(0), full(steps), full(2 * steps)],
        out_specs=blk(0),
        out_shape=_out((T, N_HEADS * HEAD_DIM), F32),
        compiler_params=_params(2, VMEM_LIMIT),
    )(*_hbm(qkv, qkv, qkv))


def _sb_bwd(qkv, do, o, name):
    T = qkv.shape[0]
    B = SB_BLOCK
    W = SB_PAIRS * PAIR
    steps = N_HEADS // (2 * SB_PAIRS)
    n_blocks = T // B
    heads = [(p, h) for p in range(SB_PAIRS) for h in range(2)]

    def body(q_ref, k_ref, v_ref, do_ref, o_ref, dq_ref, dk_ref, dv_ref, dk_s, dv_s):
        i = pl.program_id(1)

        @pl.when(i == 0)
        def _():
            dk_s[...] = jnp.zeros_like(dk_s)
            dv_s[...] = jnp.zeros_like(dv_s)

        after = _tri(lambda r, c: r > c)
        since = _tri(lambda r, c: r >= c)
        lanes = _pair_lanes()
        cols = [slice(p * PAIR, (p + 1) * PAIR) for p in range(SB_PAIRS)]
        q = {(p, h): _only(lanes[h], q_ref[:, cols[p]] * ATT_SCALE) for p, h in heads}
        do = {(p, h): _only(lanes[h], do_ref[:, cols[p]].astype(BF16)) for p, h in heads}
        total = {ph: jnp.sum(do[ph].astype(F32) * o_ref[:, cols[ph[0]]], axis=1, keepdims=True) for ph in heads}

        def tiles(j, carries, mask):
            at = pl.ds(pl.multiple_of(j * B, B), B)
            ks = [k_ref[at, c] for c in cols]
            vs = [v_ref[at, c] for c in cols]
            scores = [_sb_scores(q[ph], ks[ph[0]], mask) for ph in heads]
            laters = [_cum(sp, after) for _, sp in scores]
            das = [_dot(do[ph], vs[ph[0]], NT) for ph in heads]
            a_s, gs = [], []
            for (z, sp), later, da, carry in zip(scores, laters, das, carries):
                a = jnp.exp(z - sp - later - carry[0])
                if mask is not None:
                    a = jnp.where(mask, a, 0.0)
                a = a.astype(BF16)
                a_s.append(a)
                gs.append(a.astype(F32) * da)
            sinces = [_cum(g, since) for g in gs]
            dzs = []
            for ph, (_, sp), g, from_s, carry in zip(heads, scores, gs, sinces, carries):
                g_before = total[ph] - carry[1] - from_s
                fail = jnp.exp(-sp)
                dz = fail * (g + g_before) - g_before
                if mask is not None:
                    dz = jnp.where(mask, dz, 0.0)
                dzs.append(dz.astype(BF16))
            out = []
            for ph, (_, sp), a, dz, later, from_s, carry in zip(heads, scores, a_s, dzs, laters, sinces, carries):
                dk_s[at, cols[ph[0]]] += _dot(dz, q[ph], TN)
                dv_s[at, cols[ph[0]]] += _dot(a, do[ph], TN)
                out.append((carry[0] + later[:, 0:1] + sp[:, 0:1], carry[1] + from_s[:, 0:1],
                            carry[2] + _dot(dz, _only(lanes[ph[1]], ks[ph[0]]))))
            return tuple(out)

        col = jnp.zeros((B, 1), F32)
        zero = (col, col, jnp.zeros((B, PAIR), F32))
        carries = tiles(i, (zero,) * len(heads), _strict_causal())
        last = lax.fori_loop(0, i, lambda jj, cs: tiles(i - 1 - jj, cs, None), carries)
        for p in range(SB_PAIRS):
            dq_ref[:, cols[p]] = ((last[2 * p][2] + last[2 * p + 1][2]) * ATT_SCALE).astype(BF16)

        @pl.when(i == n_blocks - 1)
        def _():
            dk_ref[...] = dk_s[...].astype(BF16)
            dv_ref[...] = dv_s[...].astype(BF16)

    blk = lambda off: pl.BlockSpec((B, W), lambda g, i: (i, g + off))
    full = lambda off: pl.BlockSpec((T, W), lambda g, i: (0, g + off))
    out = _out((T, N_HEADS * HEAD_DIM), BF16)
    return pl.pallas_call(
        body, name=name, grid=(steps, n_blocks),
        in_specs=[blk(0), full(steps), full(2 * steps), blk(0), blk(0)],
        out_specs=[blk(0), full(0), full(0)],
        out_shape=[out, out, out],
        scratch_shapes=[pltpu.VMEM((T, W), F32)] * 2,
        compiler_params=_params(2, VMEM_LIMIT),
    )(*_hbm(qkv, qkv, qkv, do, o))


NEAR = BAND - PAD + REL_CLIP
FAR = BAND - NEAR
NEAR_REL = 2 * REL_CLIP
BIAS_ROWS = 8


def _rel_onehot(i, transposed):
    shape = (NEAR, NEAR_REL) if transposed else (NEAR_REL, NEAR)
    j = FAR + lax.broadcasted_iota(jnp.int32, shape, 0 if transposed else 1)
    r = lax.broadcasted_iota(jnp.int32, shape, 1 if transposed else 0)
    idx = jnp.clip(i + PAD - j, -REL_CLIP, REL_CLIP) + REL_CLIP
    return jnp.where(idx - 1 == r, 1.0, 0.0).astype(BF16)


def _bias_table(rel_bias, name):
    def body(near_ref, far_ref, o_ref):
        rb = near_ref[...]
        hi, lo = _split2(rb)
        lo2 = (rb - hi.astype(F32) - lo.astype(F32)).astype(BF16)
        far = jnp.broadcast_to(far_ref[...], (N_HEADS, FAR))
        for k in range(BIAS_ROWS):
            onehot = _rel_onehot(pl.program_id(0) * BIAS_ROWS + k, False)
            o_ref[k, :, :FAR] = far
            o_ref[k, :, FAR:] = _dot(hi, onehot) + _dot(lo, onehot) + _dot(lo2, onehot)

    return pl.pallas_call(
        body, name=name, grid=(CHUNK // BIAS_ROWS,),
        in_specs=[pl.BlockSpec((N_HEADS, NEAR_REL), lambda i: (0, 0)), pl.BlockSpec((N_HEADS, 1), lambda i: (0, 0))],
        out_specs=pl.BlockSpec((BIAS_ROWS, N_HEADS, BAND), lambda i: (i, 0, 0)),
        out_shape=_out((CHUNK, N_HEADS, BAND), F32),
        compiler_params=_params(1),
    )(*_hbm(rel_bias[:, 1:], rel_bias[:, N_REL - 1:]))


def _bias_grad(dbias_t, name):
    def body(d_ref, near_ref, far_ref):
        near, far = None, None
        for k in range(BIAS_ROWS):
            onehot = _rel_onehot(pl.program_id(0) * BIAS_ROWS + k, True)
            hi, lo = _split2(d_ref[k, :, FAR:])
            part = _dot(hi, onehot) + _dot(lo, onehot)
            rest = jnp.sum(d_ref[k, :, :FAR], axis=1, keepdims=True)
            near, far = (part, rest) if near is None else (near + part, far + rest)
        first = pl.program_id(0) == 0
        _accumulate(near_ref, near, first)
        _accumulate(far_ref, jnp.broadcast_to(far, far_ref.shape), first)

    near, far = pl.pallas_call(
        body, name=name, grid=(CHUNK // BIAS_ROWS,),
        in_specs=[pl.BlockSpec((BIAS_ROWS, N_HEADS, BAND), lambda i: (i, 0, 0))],
        out_specs=[pl.BlockSpec((N_HEADS, NEAR_REL), lambda i: (0, 0)), pl.BlockSpec((N_HEADS, 128), lambda i: (0, 0))],
        out_shape=[_out((N_HEADS, NEAR_REL), F32), _out((N_HEADS, 128), F32)],
        compiler_params=_params(1),
    )(*_hbm(dbias_t))
    return jnp.pad(near, ((0, 0), (1, 0))).at[:, N_REL - 1].add(far[:, 0])


def _ch_probs(scores, bias, valid):
    z = jnp.where(valid, scores * ATT_SCALE + bias, NEG_INF)
    e = jnp.exp(z - jnp.max(z, axis=-1, keepdims=True))
    return e / jnp.sum(e, axis=-1, keepdims=True)


CH_HEADS = [(pair, h) for pair in range(N_HEADS // 2) for h in range(2)]
CH_COLS = [slice(pair * PAIR, (pair + 1) * PAIR) for pair in range(N_HEADS // 2)]


CH_GROUP = 2
CH_Q = CH_GROUP * CHUNK
CH_WIN = (LOOKBACK + CH_GROUP) * CHUNK


def _ch_valid(n):
    row_chunk = lax.broadcasted_iota(jnp.int32, (CH_Q, CH_WIN), 0) // CHUNK
    slot = lax.broadcasted_iota(jnp.int32, (CH_Q, CH_WIN), 1)
    ahead = slot // CHUNK - row_chunk
    return (ahead >= 0) & (ahead <= LOOKBACK) & (n * CH_Q + slot >= PAD)


def _ch_group_bias(bias):
    shifted = [jnp.pad(bias, ((0, 0), (0, 0), (c * CHUNK, (CH_GROUP - 1 - c) * CHUNK))) for c in range(CH_GROUP)]
    return jnp.concatenate(shifted, axis=1)


def _ch_fold_bias_grad(dbias):
    parts = [dbias[:, c * CHUNK:(c + 1) * CHUNK, c * CHUNK:c * CHUNK + BAND] for c in range(CH_GROUP)]
    return sum(parts[1:], parts[0])


def _ch_fwd(qkv, bias, name):
    T = qkv.shape[0]
    W = N_HEADS * HEAD_DIM

    def body(q_ref, k_ref, v_ref, b_ref, o_ref, kp, vp):
        n = pl.program_id(0)

        @pl.when(n == 0)
        def _():
            _ch_load_padded(k_ref, v_ref, kp, vp)

        win = pl.ds(pl.multiple_of(n * CH_Q, CH_Q), CH_WIN)
        valid = _ch_valid(n)
        lanes = _pair_lanes()
        scores = [_dot(_only(lanes[h], q_ref[:, CH_COLS[pair]]), kp[win, CH_COLS[pair]], NT) for pair, h in CH_HEADS]
        probs = [_ch_probs(s, b_ref[2 * pair + h], valid).astype(BF16) for s, (pair, h) in zip(scores, CH_HEADS)]
        outs = [_dot(p, _only(lanes[h], vp[win, CH_COLS[pair]])) for p, (pair, h) in zip(probs, CH_HEADS)]
        for pair, cols in enumerate(CH_COLS):
            o_ref[:, cols] = outs[2 * pair] + outs[2 * pair + 1]

    full = lambda col: pl.BlockSpec((T, W), lambda n: (0, col))
    return pl.pallas_call(
        body, name=name, grid=(T // CH_Q,),
        in_specs=[pl.BlockSpec((CH_Q, W), lambda n: (n, 3)), full(4), full(5),
                  pl.BlockSpec((N_HEADS, CH_Q, CH_WIN), lambda n: (0, 0, 0))],
        out_specs=pl.BlockSpec((CH_Q, W), lambda n: (n, 0)),
        out_shape=_out((T, W), F32),
        scratch_shapes=[pltpu.VMEM((PAD + T, W), BF16)] * 2,
        compiler_params=_params(1, VMEM_LIMIT),
    )(*_hbm(qkv, qkv, qkv, bias))


def _ch_load_padded(k_ref, v_ref, kp, vp):
    for src, dst in ((k_ref, kp), (v_ref, vp)):
        dst[:PAD, :] = jnp.zeros((PAD, dst.shape[1]), dst.dtype)
        dst[PAD:, :] = src[...]


def _ch_bwd(qkv, bias, do, name):
    T = qkv.shape[0]
    W = N_HEADS * HEAD_DIM
    n_chunks = T // CH_Q

    def body(q_ref, k_ref, v_ref, b_ref, do_ref, dq_ref, dk_ref, dv_ref, db_ref, kp, vp, dk_s, dv_s):
        n = pl.program_id(0)

        @pl.when(n == 0)
        def _():
            _ch_load_padded(k_ref, v_ref, kp, vp)
            dk_s[...] = jnp.zeros_like(dk_s)
            dv_s[...] = jnp.zeros_like(dv_s)
            db_ref[...] = jnp.zeros_like(db_ref)

        win = pl.ds(pl.multiple_of(n * CH_Q, CH_Q), CH_WIN)
        valid = _ch_valid(n)
        lanes = _pair_lanes()
        kws = [kp[win, cols] for cols in CH_COLS]
        vws = [vp[win, cols] for cols in CH_COLS]
        qs = [_only(lanes[h], q_ref[:, CH_COLS[pair]]) for pair, h in CH_HEADS]
        dos = [_only(lanes[h], do_ref[:, CH_COLS[pair]].astype(BF16)) for pair, h in CH_HEADS]
        scores = [_dot(q, kws[pair], NT) for q, (pair, _) in zip(qs, CH_HEADS)]
        dps = [_dot(do, vws[pair], NT) for do, (pair, _) in zip(dos, CH_HEADS)]
        probs = [_ch_probs(s, b_ref[2 * pair + h], valid) for s, (pair, h) in zip(scores, CH_HEADS)]
        dzs = [p * (dp - jnp.sum(dp * p, axis=-1, keepdims=True)) for p, dp in zip(probs, dps)]
        for k, dz in enumerate(dzs):
            db_ref[k] += dz
        dzbs = [(dz * ATT_SCALE).astype(BF16) for dz in dzs]
        dqs = [_dot(dz, _only(lanes[h], kws[pair])) for dz, (pair, h) in zip(dzbs, CH_HEADS)]
        dks = [_dot(dz, q, TN) for dz, q in zip(dzbs, qs)]
        dvs = [_dot(p.astype(BF16), do, TN) for p, do in zip(probs, dos)]
        for pair, cols in enumerate(CH_COLS):
            dq_ref[:, cols] = (dqs[2 * pair] + dqs[2 * pair + 1]).astype(BF16)
            dk_s[win, cols] += dks[2 * pair] + dks[2 * pair + 1]
            dv_s[win, cols] += dvs[2 * pair] + dvs[2 * pair + 1]

        @pl.when(n == n_chunks - 1)
        def _():
            dk_ref[...] = dk_s[PAD:, :].astype(BF16)
            dv_ref[...] = dv_s[PAD:, :].astype(BF16)

    full = lambda col: pl.BlockSpec((T, W), lambda n: (0, col))
    blk = lambda col: pl.BlockSpec((CH_Q, W), lambda n: (n, col))
    tab = pl.BlockSpec((N_HEADS, CH_Q, CH_WIN), lambda n: (0, 0, 0))
    out = _out((T, W), BF16)
    return pl.pallas_call(
        body, name=name, grid=(n_chunks,),
        in_specs=[blk(3), full(4), full(5), tab, blk(0)],
        out_specs=[blk(0), full(0), full(0), tab],
        out_shape=[out, out, out, _out((N_HEADS, CH_Q, CH_WIN), F32)],
        scratch_shapes=[pltpu.VMEM((PAD + T, W), BF16)] * 2 + [pltpu.VMEM((PAD + T, W), F32)] * 2,
        compiler_params=_params(1, VMEM_LIMIT),
    )(*_hbm(qkv, qkv, qkv, bias, do))


def _rows_split(a, parts):
    return a.reshape(a.shape[:-2] + (parts, a.shape[-2] // parts, a.shape[-1]))


def _cast_into_slot0(c, ws, name):
    parts = 2
    ws = [_rows_split(_rows_split(w, 2), parts) for w in ws]
    n = len(ws)

    def body(c_ref, *refs):
        for src, dst in zip(refs[:n], refs[n:]):
            dst[0, 0, 0] = src[0, 0].astype(BF16)

    outs = pl.pallas_call(
        body, name=name,
        grid_spec=pltpu.PrefetchScalarGridSpec(
            num_scalar_prefetch=1, grid=(2, parts),
            in_specs=[pl.BlockSpec((1, 1) + w.shape[2:], lambda d, r, c_ref: (d ^ c_ref[0], r, 0, 0)) for w in ws],
            out_specs=[pl.BlockSpec((1, 1, 1) + w.shape[2:], lambda d, r, c_ref: (0, d, r, 0, 0)) for w in ws]),
        out_shape=[_out((N_CHIPS,) + w.shape, BF16) for w in ws],
        compiler_params=_params(2, VMEM_LIMIT),
    )(c, *_hbm(*ws))
    return [o.reshape(N_CHIPS, 2, o.shape[2] * o.shape[3], o.shape[4]) for o in outs]


def _chip_order(me, c, lands, name):
    parts = 2
    xs = [_rows_split(x, parts) for x in lands]

    def body(me_ref, c_ref, *refs):
        n = len(refs) // 2
        for src, dst in zip(refs[:n], refs[n:]):
            dst[...] = src[...]

    outs = pl.pallas_call(
        body, name=name,
        grid_spec=pltpu.PrefetchScalarGridSpec(
            num_scalar_prefetch=2, grid=(N_CHIPS, 2, parts),
            in_specs=[pl.BlockSpec((1, 1, 1) + x.shape[3:],
                                   lambda j, h, r, me_ref, c_ref: (j ^ me_ref[0], h ^ c_ref[0], r, 0, 0)) for x in xs],
            out_specs=[pl.BlockSpec((1, 1, 1) + x.shape[3:], lambda j, h, r, me_ref, c_ref: (j, h, r, 0, 0))
                       for x in xs]),
        out_shape=[_out(x.shape, x.dtype) for x in xs],
        compiler_params=_params(3, VMEM_LIMIT),
    )(me, c, *_hbm(*xs))
    return [o.reshape(o.shape[0], 2 * parts * o.shape[3], o.shape[4]) for o in outs]


def _pair_add(c, mine, got, permuted, name):
    parts = 2
    mine = [_rows_split(m, parts) for m in mine]
    got = [_rows_split(g, parts) for g in got]
    n = len(mine)

    def body(c_ref, *refs):
        for a, b, o in zip(refs[:n], refs[n:2 * n], refs[2 * n:]):
            o[0, 0] = (a[0, 0, 0] + b[0, 0].astype(F32)).astype(BF16)

    def mine_spec(m, perm):
        if perm:
            return pl.BlockSpec((1, 1, 1) + m.shape[3:], lambda j, r, c_ref: (j, 0, r, 0, 0))
        return pl.BlockSpec((1, 1, 1) + m.shape[3:], lambda j, r, c_ref: (j, c_ref[0], r, 0, 0))

    def got_spec(g):
        return pl.BlockSpec((1, 1) + g.shape[2:], lambda j, r, c_ref: (j, r, 0, 0))

    outs = pl.pallas_call(
        body, name=name,
        grid_spec=pltpu.PrefetchScalarGridSpec(
            num_scalar_prefetch=1, grid=(N_CHIPS, parts),
            in_specs=[mine_spec(m, perm) for m, perm in zip(mine, permuted)] + [got_spec(g) for g in got],
            out_specs=[got_spec(g) for g in got]),
        out_shape=[_out(g.shape, BF16) for g in got],
        compiler_params=_params(2, VMEM_LIMIT),
    )(c, *_hbm(*mine, *got))
    return [o.reshape(o.shape[0], o.shape[1] * o.shape[2], o.shape[3]) for o in outs]


def _chip_add(me, partials, landed, permuted, name):
    parts = 2
    ps = [_rows_split(x, parts) for x in partials]
    ls = [_rows_split(x, parts) for x in landed]
    n = len(ps)

    def body(me_ref, *refs):
        for own, got, o in zip(refs[:n], refs[n:2 * n], refs[2 * n:]):
            acc = own[0, 0].astype(F32)
            for r in range(N_CHIPS - 1):
                acc = acc + got[r, 0].astype(F32)
            o[0] = acc

    def own_spec(x, perm):
        if perm:
            return pl.BlockSpec((1, 1) + x.shape[2:], lambda r, me_ref: (0, r, 0, 0))
        return pl.BlockSpec((1, 1) + x.shape[2:], lambda r, me_ref: (me_ref[0], r, 0, 0))

    outs = pl.pallas_call(
        body, name=name,
        grid_spec=pltpu.PrefetchScalarGridSpec(
            num_scalar_prefetch=1, grid=(parts,),
            in_specs=[own_spec(x, perm) for x, perm in zip(ps, permuted)]
            + [pl.BlockSpec((N_CHIPS - 1, 1) + x.shape[2:], lambda r, me_ref: (0, r, 0, 0)) for x in ls],
            out_specs=[pl.BlockSpec((1,) + x.shape[2:], lambda r, me_ref: (r, 0, 0)) for x in ps]),
        out_shape=[_out(x.shape[1:], F32) for x in ps],
        compiler_params=_params(1, VMEM_LIMIT),
    )(me, *_hbm(*ps, *ls))
    return [o.reshape(o.shape[0] * o.shape[1], o.shape[2]) for o in outs]


def _adamw_math(w, g, m, v):
    m = ADAM_B1 * m + (1.0 - ADAM_B1) * g
    v = ADAM_B2 * v + (1.0 - ADAM_B2) * (g * g)
    m_hat = m / (1.0 - ADAM_B1 ** ADAM_STEP)
    v_hat = v / (1.0 - ADAM_B2 ** ADAM_STEP)
    delta = -ADAM_LR * (m_hat / (jnp.sqrt(v_hat) + ADAM_EPS) + ADAM_WD * w)
    return delta, m, v


def _adamw(ws, gs, ms, vs, parts, name):
    n = len(ws)
    flat = [_rows_split(a, parts) for a in (*ws, *gs, *ms, *vs)]

    def body(*refs):
        ins, outs = refs[:4 * n], refs[4 * n:]
        for k in range(n):
            d, m, v = _adamw_math(ins[k][...], ins[n + k][...], ins[2 * n + k][...], ins[3 * n + k][...])
            outs[k][...] = d
            outs[n + k][...] = m
            outs[2 * n + k][...] = v

    spec = lambda a: pl.BlockSpec((1,) + a.shape[1:], lambda i: (i, 0, 0))
    outs = pl.pallas_call(
        body, name=name, grid=(parts,),
        in_specs=[spec(a) for a in flat], out_specs=[spec(a) for a in flat[:n]] * 3,
        out_shape=[_out(a.shape, F32) for a in flat[:n]] * 3,
        compiler_params=_params(1, VMEM_LIMIT),
    )(*_hbm(*flat))
    outs = [o.reshape(o.shape[0] * o.shape[1], o.shape[2]) for o in outs]
    return outs[:n], outs[n:2 * n], outs[2 * n:]


def _adamw_halves(c, ws, owns, others, ms, vs, name):
    parts = 4
    n = len(ws)
    whole = [_rows_split(_rows_split(a, 2), parts) for a in (*ws, *ms, *vs)]
    halves = [_rows_split(a, parts) for a in (*owns, *others)]

    def body(c_ref, *refs):
        ins, outs = refs[:5 * n], refs[5 * n:]
        mine = pl.program_id(0) == c_ref[0]
        for k in range(n):
            g = jnp.where(mine, ins[3 * n + k][0], ins[4 * n + k][0])
            d, m, v = _adamw_math(ins[k][0, 0], g, ins[n + k][0, 0], ins[2 * n + k][0, 0])
            for slot, val in enumerate((g, d, m, v)):
                outs[slot * n + k][0, 0] = val

    wspec = lambda a: pl.BlockSpec((1, 1) + a.shape[2:], lambda h, r, c_ref: (h, r, 0, 0))
    hspec = lambda a: pl.BlockSpec((1,) + a.shape[1:], lambda h, r, c_ref: (r, 0, 0))
    outs = pl.pallas_call(
        body, name=name,
        grid_spec=pltpu.PrefetchScalarGridSpec(
            num_scalar_prefetch=1, grid=(2, parts),
            in_specs=[wspec(a) for a in whole] + [hspec(a) for a in halves],
            out_specs=[wspec(a) for a in whole[:n]] * 4),
        out_shape=[_out(a.shape, F32) for a in whole[:n]] * 4,
        compiler_params=_params(2, VMEM_LIMIT),
    )(c, *_hbm(*whole, *halves))
    outs = [o.reshape(2 * parts * o.shape[2], o.shape[3]) for o in outs]
    return outs[:n], outs[n:2 * n], outs[2 * n:3 * n], outs[3 * n:]


def _place():
    x, y, c = lax.axis_index("x"), lax.axis_index("y"), lax.axis_index("c")
    peers = [(x ^ (r >> 1), y ^ (r & 1), c) for r in (1, 2, 3)]
    return x, y, c, peers


def _handshake(peers):
    barrier = pltpu.get_barrier_semaphore()
    for peer in peers:
        pl.semaphore_signal(barrier, inc=1, device_id=peer, device_id_type=MESH)
    pl.semaphore_wait(barrier, len(peers))


ANY = pl.BlockSpec(memory_space=pl.ANY)
HBM = pl.BlockSpec(memory_space=pltpu.HBM)
SEM = pl.BlockSpec(memory_space=pltpu.SEMAPHORE)
SPLIT_COPY = pltpu.SideEffectType.DATAFLOW_SIDE_EFFECTING


def _in_hbm(a):
    return pltpu.with_memory_space_constraint(a, pltpu.HBM)


def _split_start(body, name, collective_id, operands, n_sems, after=None):
    n = len(operands)
    extra = [] if after is None else [after]

    def wrapped(*refs):
        at = n + len(extra)
        body(refs[:n], refs[at], refs[at + 1])
        token = refs[-1]
        token[...] = jnp.zeros_like(token)

    outs = pl.pallas_call(
        wrapped, name=name,
        in_specs=[HBM] * n + [ANY] * len(extra),
        out_shape=(pltpu.SemaphoreType.DMA((n_sems,)), pltpu.SemaphoreType.DMA((n_sems,)),
                   *[pltpu.HBM(a.shape, a.dtype) for a in operands], jax.ShapeDtypeStruct((8, 128), F32)),
        out_specs=(SEM, SEM, *[HBM] * n, pl.BlockSpec(memory_space=pltpu.VMEM)),
        input_output_aliases={i: 2 + i for i in range(n)},
        compiler_params=pltpu.CompilerParams(has_side_effects=SPLIT_COPY, collective_id=collective_id),
    )(*[_in_hbm(a) for a in operands], *extra)
    return outs[0], outs[1], list(outs[2:2 + n]), outs[-1]


def _split_wait(body, name, send_sem, recv_sem, operands, after):
    n = len(operands)

    def wrapped(*refs):
        body(refs[:n], refs[n], refs[n + 1])

    outs = pl.pallas_call(
        wrapped, name=name,
        in_specs=[HBM] * n + [SEM, SEM, ANY],
        out_shape=tuple(pltpu.HBM(a.shape, a.dtype) for a in operands),
        out_specs=tuple([HBM] * n),
        input_output_aliases={i: i for i in range(n)},
        compiler_params=pltpu.CompilerParams(has_side_effects=SPLIT_COPY),
    )(*operands, send_sem, recv_sem, after)
    return list(outs)


def _gather_copies(lands, send_sem, recv_sem):
    peers = _place()[3]
    return [pltpu.make_async_remote_copy(
        src_ref=land.at[0, 0], dst_ref=land.at[r + 1, 0],
        send_sem=send_sem.at[a * 3 + r], recv_sem=recv_sem.at[a * 3 + r],
        device_id=peers[r], device_id_type=MESH) for a, land in enumerate(lands) for r in range(3)]


def _gather_start(lands, name, collective_id, after):
    def body(refs, send_sem, recv_sem):
        _handshake(_place()[3])
        for cp in _gather_copies(refs, send_sem, recv_sem):
            cp.start()

    return _split_start(body, name, collective_id, list(lands), 3 * len(lands), after)


def _gather_wait(send_sem, recv_sem, operands, after, name):
    def body(refs, send_sem, recv_sem):
        for cp in _gather_copies(refs, send_sem, recv_sem):
            cp.wait_send()
            cp.wait_recv()

    return _split_wait(body, name, send_sem, recv_sem, operands, after)


def _gather_finish(lands, with_ici, name):
    n = len(lands)

    def body(*refs):
        land = refs[n:2 * n]
        send_ici, recv_ici, send_d2d, recv_d2d = refs[2 * n:]
        x, y, c, _ = _place()
        ici = _gather_copies(land, send_ici, recv_ici) if with_ici else []
        for cp in ici:
            cp.start()
        passed = [pltpu.make_async_remote_copy(
            src_ref=land[a].at[r + 1, 0], dst_ref=land[a].at[r + 1, 1],
            send_sem=send_d2d.at[a * 3 + r], recv_sem=recv_d2d.at[a * 3 + r],
            device_id=(x, y, 1 - c), device_id_type=MESH) for a in range(n) for r in range(3)]
        for k, cp in enumerate(passed):
            if with_ici:
                ici[k].wait_recv()
            cp.start()
        for cp in passed:
            cp.wait_recv()
        for cp in ici:
            cp.wait_send()
        for cp in passed:
            cp.wait_send()

    outs = pl.pallas_call(
        body, name=name,
        in_specs=[ANY] * n, out_specs=[ANY] * n,
        out_shape=[_out(l.shape, l.dtype) for l in lands],
        input_output_aliases={a: a for a in range(n)},
        scratch_shapes=[pltpu.SemaphoreType.DMA((3 * n,))] * 4,
    )(*lands)
    return list(outs)


def _slabs(land):
    return land.reshape(N_CHIPS, 2 * land.shape[2], land.shape[3])


def _pair_swap(grads, permuted, name):
    n = len(grads)

    def body(*refs):
        src, dst = refs[:n], refs[n:2 * n]
        send_sem, recv_sem = refs[2 * n:]
        x, y, c, _ = _place()
        copies = [pltpu.make_async_remote_copy(
            src_ref=src[a].at[:, 1] if permuted[a] else src[a].at[:, 1 - c], dst_ref=dst[a],
            send_sem=send_sem.at[a], recv_sem=recv_sem.at[a],
            device_id=(x, y, 1 - c), device_id_type=MESH) for a in range(n)]
        for cp in copies:
            cp.start()
        for cp in copies:
            cp.wait()

    return pl.pallas_call(
        body, name=name,
        in_specs=[ANY] * n, out_specs=[ANY] * n,
        out_shape=[_out((N_CHIPS,) + g.shape[2:], g.dtype) for g in grads],
        scratch_shapes=[pltpu.SemaphoreType.DMA((n,))] * 2,
    )(*grads)


def _swap_copies(refs, permuted, send_sem, recv_sem):
    n = len(refs) // 2
    x, y, c, _ = _place()
    return [pltpu.make_async_remote_copy(
        src_ref=refs[a].at[:, 1] if permuted[a] else refs[a].at[:, 1 - c], dst_ref=refs[n + a],
        send_sem=send_sem.at[a], recv_sem=recv_sem.at[a],
        device_id=(x, y, 1 - c), device_id_type=MESH) for a in range(n)]


def _pair_swap_start(grads, permuted, name, collective_id):
    def body(refs, send_sem, recv_sem):
        x, y, c, _ = _place()
        _handshake([(x, y, 1 - c)])
        for cp in _swap_copies(refs, permuted, send_sem, recv_sem):
            cp.start()

    lands = [lax.empty((N_CHIPS,) + g.shape[2:], g.dtype) for g in grads]
    return _split_start(body, name, collective_id, list(grads) + lands, len(grads))


def _pair_swap_wait(send_sem, recv_sem, operands, permuted, after, name):
    def body(refs, send_sem, recv_sem):
        for cp in _swap_copies(refs, permuted, send_sem, recv_sem):
            cp.wait_send()
            cp.wait_recv()

    return _split_wait(body, name, send_sem, recv_sem, operands, after)


def _scatter_copies(refs, permuted, send_sem, recv_sem):
    n = len(refs) // 2
    x, y, _, peers = _place()
    me = 2 * x + y
    return [pltpu.make_async_remote_copy(
        src_ref=refs[a].at[r + 1] if permuted[a] else refs[a].at[me ^ (r + 1)], dst_ref=refs[n + a].at[r],
        send_sem=send_sem.at[a * 3 + r], recv_sem=recv_sem.at[a * 3 + r],
        device_id=peers[r], device_id_type=MESH) for a in range(n) for r in range(3)]


def _scatter_start(partials, permuted, name, collective_id):
    def body(refs, send_sem, recv_sem):
        _handshake(_place()[3])
        for cp in _scatter_copies(refs, permuted, send_sem, recv_sem):
            cp.start()

    lands = [lax.empty((N_CHIPS - 1,) + p.shape[1:], p.dtype) for p in partials]
    return _split_start(body, name, collective_id, list(partials) + lands, 3 * len(partials))


def _scatter_wait(send_sem, recv_sem, operands, permuted, after, name):
    def body(refs, send_sem, recv_sem):
        for cp in _scatter_copies(refs, permuted, send_sem, recv_sem):
            cp.wait_send()
            cp.wait_recv()

    return _split_wait(body, name, send_sem, recv_sem, operands, after)


def _pair_join(halves, name):
    n = len(halves)

    def body(*refs):
        src, dst = refs[:n], refs[n:2 * n]
        send_sem, recv_sem = refs[2 * n:]
        x, y, c, _ = _place()
        copies = [pltpu.make_async_remote_copy(
            src_ref=src[a], dst_ref=dst[a], send_sem=send_sem.at[a], recv_sem=recv_sem.at[a],
            device_id=(x, y, 1 - c), device_id_type=MESH) for a in range(n)]
        for cp in copies:
            cp.start()
        for cp in copies:
            cp.wait()

    return pl.pallas_call(
        body, name=name,
        in_specs=[ANY] * n, out_specs=[ANY] * n,
        out_shape=[_out(h.shape, F32) for h in halves],
        scratch_shapes=[pltpu.SemaphoreType.DMA((n,))] * 2,
    )(*halves)


def _all_sum_small(v, after, name):
    R, C = v.shape
    n_dev = 8

    def body(v_ref, after_ref, o_ref, buf, send_sem, recv_sem):
        x, y, c, _ = _place()
        me = 4 * x + 2 * y + c
        buf[me] = v_ref[...]
        copies = []
        for k in range(1, n_dev):
            peer = (x ^ (k >> 2), y ^ ((k >> 1) & 1), c ^ (k & 1))
            copies.append(pltpu.make_async_remote_copy(
                src_ref=v_ref, dst_ref=buf.at[me], send_sem=send_sem.at[k - 1], recv_sem=recv_sem.at[k - 1],
                device_id=peer, device_id_type=MESH))
        for cp in copies:
            cp.start()
        for cp in copies:
            cp.wait()
        acc = buf[0]
        for m in range(1, n_dev):
            acc = acc + buf[m]
        o_ref[...] = acc

    return pl.pallas_call(
        body, name=name,
        in_specs=[pl.BlockSpec(memory_space=pltpu.VMEM), ANY], out_specs=pl.BlockSpec(memory_space=pltpu.VMEM),
        out_shape=jax.ShapeDtypeStruct((R, C), F32),
        scratch_shapes=[pltpu.VMEM((n_dev, R, C), F32), pltpu.SemaphoreType.DMA((n_dev - 1,)),
                        pltpu.SemaphoreType.DMA((n_dev - 1,))],
    )(v, after)


class _WholeWeights:
    def __init__(self, w):
        self.w = w

    def weights(self, group, after=None):
        return self.w, None

    def grads_ready(self, group, gw):
        return None

    def grads_sent(self, group, after):
        return None


def _local_step(x, p, target, gains, rel_bias, hooks):
    T, D = x.shape
    S = N_CHIPS

    tied = lambda gain, token: gain if token is None else gain + token[0, 0]
    w, token = hooks.weights("first")
    w = dict(w)
    h1, xn1, g1, u1, a1, f1 = _ffn_fwd(x, tied(gains["ffn1_pre"], token), gains["ffn1_post"], w["ffn1_gate"],
                                       w["ffn1_up"], w["ffn1_down"], "ffn1_fwd")
    more, token = hooks.weights("in", h1)
    w.update(more)
    qkv, un = _norm_proj(h1, tied(gains["mix_pre"], token), w["in"], "qkv_proj")
    bias = _ch_group_bias(_bias_table(rel_bias, "bias_table").transpose(1, 0, 2))
    o_a = _sb_fwd(qkv, "sb_fwd")
    o_b = _ch_fwd(qkv, bias, "ch_fwd")
    w.update(hooks.weights("rest", o_b)[0])
    w_out = w["out"].reshape(D, D)
    h2, mixed, mo = _mix_out_fwd(h1, o_a, o_b, gains["out_sb"], gains["out_ch"], w_out, gains["mix_post"],
                                 "mix_out_fwd")
    h3, xn2, g2, u2, a2, f2 = _ffn_fwd(h2, gains["ffn2_pre"], gains["ffn2_post"], w["ffn2_gate"], w["ffn2_up"],
                                       w["ffn2_down"], "ffn2_fwd")
    w_ple_proj = w["ple_proj"].transpose(1, 0, 2).reshape(p.shape[1], D)
    w_ple_gate = w["ple_gate"].reshape(D, D)

    loss, dh3, dproj, dgate, dg_ple = _ple_loss(h3, p, target, w_ple_proj, w_ple_gate, gains["ple_post"], "ple_loss")
    gw, gg = {}, {"ple_post": dg_ple}
    gw["ple_proj"] = _mm_tn(p[None], dproj, p.shape[1], "dw_ple_proj")
    row_sharded = lambda pair: tuple(o.reshape(S, D // S, D) for o in pair)
    gw["ple_gate"] = row_sharded(_mm_tn(h3[None], dgate[None], 512, "dw_ple_gate"))

    def ffn_bwd(tag, dh, x_in, xn, g_act, u_act, a_act, f, group):
        dgp, dup, df, gg[tag + "_post"] = _ffn_bwd_act(dh, f, gains[tag + "_post"], w[tag + "_down"], g_act, u_act,
                                                       tag + "_bwd_act")
        gw[tag + "_gate"] = _mm_tn(dgp, xn[None], dgp.shape[2], "dw_" + tag + "_gate")
        gw[tag + "_up"] = _mm_tn(dup, xn[None], dup.shape[2], "dw_" + tag + "_up")
        gw[tag + "_down"] = _mm_tn(a_act, df[None], a_act.shape[2], "dw_" + tag + "_down")
        g_pre = gains[tag + "_pre"]
        if group is not None:
            token = hooks.grads_ready(group, gw)
            g_pre = g_pre if token is None else g_pre + token[0, 0]
        dx, gg[tag + "_pre"] = _proj_bwd([dgp, dup], [w[tag + "_gate"], w[tag + "_up"]], x_in, g_pre, dh,
                                         tag + "_bwd_in")
        return dx

    dh2 = ffn_bwd("ffn2", dh3, h2, xn2, g2, u2, a2, f2, None)
    dmo, do_a, do_b, gg["mix_post"], gg["out_sb"], gg["out_ch"] = _mix_out_bwd(
        dh2, mo, gains["mix_post"], w_out, o_a, o_b, gains["out_sb"], gains["out_ch"], "mix_out_bwd")
    gw["out"] = row_sharded(_mm_tn(mixed[None], dmo[None], 512, "dw_out"))
    token = hooks.grads_ready("early", gw)
    if token is not None:
        do_a = do_a + token[0, 0]
    dq_a, dk_a, dv_a = _sb_bwd(qkv, do_a, o_a, "sb_bwd")
    token = hooks.grads_sent("early", dq_a)
    if token is not None:
        do_b = do_b + token[0, 0]
    dq_b, dk_b, dv_b, dbias = _ch_bwd(qkv, bias, do_b, "ch_bwd")
    g_rel = _bias_grad(_ch_fold_bias_grad(dbias).transpose(1, 0, 2), "bias_grad")
    dqkv = jnp.concatenate([dq_a, dk_a, dv_a, dq_b, dk_b, dv_b], axis=1)
    gw["in"] = _mm_tn(un[None], dqkv, 512, "dw_in", groups=S)
    dh1, gg["mix_pre"] = _proj_bwd([dqkv], [w["in"]], h1, gains["mix_pre"], dh2, "qkv_bwd_in")
    dx = ffn_bwd("ffn1", dh1, x, xn1, g1, u1, a1, f1, "late")
    return loss, dx, gw, gg, g_rel


BIG = ["ffn1_gate", "ffn1_up", "ffn1_down", "in", "out", "ffn2_gate", "ffn2_up", "ffn2_down", "ple_proj", "ple_gate"]
GAINS = ["ffn1_pre", "ffn1_post", "mix_pre", "mix_post", "out_sb", "out_ch", "ffn2_pre", "ffn2_post", "ple_post"]
TRANSPOSED = ("w_ffn1_gate", "w_ffn1_up", "w_ffn2_gate", "w_ffn2_up")
PERMUTED = ("ffn1_gate", "ffn1_up", "ffn1_down", "ffn2_gate", "ffn2_up", "ffn2_down")
W_GROUPS = {"first": ["ffn1_gate", "ffn1_up", "ffn1_down"], "in": ["in"],
            "rest": ["out", "ffn2_gate", "ffn2_up", "ffn2_down", "ple_proj", "ple_gate"]}
G_GROUPS = {"early": ["ple_proj", "ple_gate", "ffn2_gate", "ffn2_up", "ffn2_down", "out"],
            "late": ["in", "ffn1_gate", "ffn1_up", "ffn1_down"]}
ORDER = ["g_ffn1_pre", "g_ffn1_post", "w_ffn1_gate", "w_ffn1_up", "w_ffn1_down", "g_mix_pre", "g_mix_post", "w_in",
         "g_out_sb", "g_out_ch", "rel_bias", "w_out", "g_ffn2_pre", "g_ffn2_post", "w_ffn2_gate", "w_ffn2_up",
         "w_ffn2_down", "w_ple_proj", "w_ple_gate", "g_ple_post"]


def kernel(x, p, g_ffn1_pre, g_ffn1_post, w_ffn1_gate, w_ffn1_up, w_ffn1_down, g_mix_pre, g_mix_post, w_in, g_out_sb, g_out_ch, rel_bias, w_out, g_ffn2_pre, g_ffn2_post, w_ffn2_gate, w_ffn2_up, w_ffn2_down, w_ple_proj, w_ple_gate, g_ple_post, loss_target, m_g_ffn1_pre, m_g_ffn1_post, m_w_ffn1_gate, m_w_ffn1_up, m_w_ffn1_down, m_g_mix_pre, m_g_mix_post, m_w_in, m_g_out_sb, m_g_out_ch, m_rel_bias, m_w_out, m_g_ffn2_pre, m_g_ffn2_post, m_w_ffn2_gate, m_w_ffn2_up, m_w_ffn2_down, m_w_ple_proj, m_w_ple_gate, m_g_ple_post, v_g_ffn1_pre, v_g_ffn1_post, v_w_ffn1_gate, v_w_ffn1_up, v_w_ffn1_down, v_g_mix_pre, v_g_mix_post, v_w_in, v_g_out_sb, v_g_out_ch, v_rel_bias, v_w_out, v_g_ffn2_pre, v_g_ffn2_post, v_w_ffn2_gate, v_w_ffn2_up, v_w_ffn2_down, v_w_ple_proj, v_w_ple_gate, v_g_ple_post):
    args = dict(locals())
    take = lambda a, n: a[0].T if n in TRANSPOSED else a[0]
    wts = {n: take(args[n], n) for n in ORDER}
    ms = {n: take(args["m_" + n], n) for n in ORDER}
    vs = {n: take(args["v_" + n], n) for n in ORDER}
    gains = {n: wts["g_" + n][None] for n in GAINS}

    c_idx = lax.axis_index("c").astype(jnp.int32).reshape(1)
    me_idx = (2 * lax.axis_index("x") + lax.axis_index("y")).astype(jnp.int32).reshape(1)
    south = lax.axis_index("c") == 0

    lands = dict(zip(BIG, _cast_into_slot0(c_idx, [wts["w_" + n] for n in BIG], "cast_weights")))

    def in_order(names, zones):
        plain = [n for n in names if n not in PERMUTED]
        fixed = dict(zip(plain, _chip_order(me_idx, c_idx, [zones[n] for n in plain], "chip_order_" + plain[0]))
                     ) if plain else {}
        return {n: fixed[n] if n in fixed else _slabs(zones[n]) for n in names}

    class Overlapped:
        def __init__(self):
            self.started = {}
            self.flying = {}

        def start(self, group, collective_id, after):
            self.flying[group] = _gather_start([lands[n] for n in W_GROUPS[group]], "gather_%s_start" % group,
                                               collective_id, after)
            return self.flying[group][3]

        def weights(self, group, after=None):
            names = W_GROUPS[group]
            if group == "first":
                zones = _gather_finish([lands[n] for n in names], True, "gather_first")
                token = self.start("rest", 4, self.start("in", 1, zones[0]))
                return in_order(names, dict(zip(names, zones))), token
            send_sem, recv_sem, zones, _ = self.flying[group]
            zones = _gather_wait(send_sem, recv_sem, zones, after, "gather_%s_wait" % group)
            zones = _gather_finish(zones, False, "gather_%s_finish" % group)
            return in_order(names, dict(zip(names, zones))), None

        def grads_ready(self, group, gw):
            names = G_GROUPS[group]
            perm = [n in PERMUTED for n in names]
            halved = lambda g: g.reshape(N_CHIPS, 2, g.shape[1] // 2, g.shape[2])
            mine = [halved(gw[n][0]) for n in names]
            narrow = [halved(gw[n][1]) for n in names]
            if group == "late":
                return self.scatter(group, names, perm, mine, _pair_swap(narrow, perm, "grad_pair_swap_late"))
            self.swapping = names, perm, mine, _pair_swap_start(narrow, perm, "grad_pair_swap_start_early", 5)
            return self.swapping[3][3]

        def grads_sent(self, group, after):
            names, perm, mine, (send_sem, recv_sem, operands, _) = self.swapping
            operands = _pair_swap_wait(send_sem, recv_sem, operands, perm, after, "grad_pair_swap_wait_early")
            return self.scatter(group, names, perm, mine, operands[len(names):])

        def scatter(self, group, names, perm, mine, got):
            partial = _pair_add(c_idx, mine, got, perm, "grad_pair_add_" + group)
            send_sem, recv_sem, operands, token = _scatter_start(partial, perm, "grad_scatter_start_" + group,
                                                                 {"early": 2, "late": 3}[group])
            self.started[group] = names, perm, send_sem, recv_sem, operands, token
            return token

    def reduce_finish(state, after, tag):
        names, perm, send_sem, recv_sem, operands, _ = state
        operands = _scatter_wait(send_sem, recv_sem, operands, perm, after, "grad_scatter_wait_" + tag)
        n = len(names)
        own = _chip_add(me_idx, operands[:n], operands[n:], perm, "grad_chip_add_" + tag)
        return own, _pair_join(own, "grad_pair_join_" + tag)

    hooks = Overlapped()
    loss, dx, gw, gg, g_rel = _local_step(x[0], p[0, 0], loss_target[0], gains, wts["rel_bias"], hooks)

    grads, delta, new_m, new_v = {}, {}, {}, {}

    def finish(group, after):
        own, other = reduce_finish(hooks.started[group], after, group)
        names = ["w_" + n for n in G_GROUPS[group]]
        g, d, m, v = _adamw_halves(c_idx, [wts[n] for n in names], own, other, [ms[n] for n in names],
                                   [vs[n] for n in names], "adamw_" + group)
        for n, gg_, dd, mm, vv in zip(names, g, d, m, v):
            grads[n], delta[n], new_m[n], new_v[n] = gg_, dd, mm, vv
        return d[0]

    finish("late", finish("early", dx))

    pieces = [gg[n].reshape(-1, 128) for n in GAINS] + [jnp.pad(g_rel, ((0, 0), (0, N_REL_PAD - N_REL))).reshape(-1, 128)]
    summed = _all_sum_small(jnp.concatenate(pieces + [loss], axis=0), delta["w_in"], "small_grad_sum")
    at = 0
    for n, piece in zip(GAINS, pieces[:-1]):
        grads["g_" + n] = summed[at:at + piece.shape[0]].reshape(1, -1)[0]
        at += piece.shape[0]
    grads["rel_bias"] = summed[at:at + pieces[-1].shape[0]].reshape(N_HEADS, N_REL_PAD)[:, :N_REL]
    loss = summed[at + pieces[-1].shape[0], 0]

    small = ["g_" + n for n in GAINS] + ["rel_bias"]
    as_rows = lambda a: (a.reshape(-1, 128) if a.size % 128 == 0 else jnp.pad(a, ((0, 0), (0, N_REL_PAD - N_REL))).reshape(-1, 128))
    d, m, v = _adamw([as_rows(wts[n]) for n in small], [as_rows(grads[n]) for n in small],
                     [as_rows(ms[n]) for n in small], [as_rows(vs[n]) for n in small], 1, "adamw_small")
    for n, dd, mm, vv in zip(small, d, m, v):
        back = (lambda a: a.reshape(N_HEADS, N_REL_PAD)[:, :N_REL]) if n == "rel_bias" else (lambda a: a.reshape(-1))
        delta[n], new_m[n], new_v[n] = back(dd), back(mm), back(vv)

    outs = [loss, dx[None]]
    for table in (grads, delta, new_m, new_v):
        outs += [(table[n].T if n in TRANSPOSED else table[n])[None] for n in ORDER]
    return tuple(outs)
```

```python
import functools

import jax
import jax.numpy as jnp
from jax import lax
from jax.experimental import pallas as pl
from jax.experimental.pallas import tpu as pltpu

F32 = jnp.float32
BF16 = jnp.bfloat16
EPS = 1e-6
N_CHIPS = 4
HEAD_DIM = 64
N_HEADS = 8
CHUNK = 64
LOOKBACK = 8
BAND = (LOOKBACK + 1) * CHUNK
PAD = LOOKBACK * CHUNK
REL_CLIP = 128
N_REL = 2 * REL_CLIP + 1
N_REL_PAD = 384
SB_BLOCK = 256
PAIR = 2 * HEAD_DIM
SB_PAIRS = 2
ATT_SCALE = HEAD_DIM ** -0.5
NEG_INF = -1e30
ROW_BLOCK = 512
VMEM_LIMIT = 48 * 1024 * 1024
MESH = pl.DeviceIdType.MESH

ADAM_LR = 0.001
ADAM_B1 = 0.9
ADAM_B2 = 0.999
ADAM_EPS = 1e-08
ADAM_WD = 0.01
ADAM_STEP = 10

NT = (((1,), (1,)), ((), ()))
TN = (((0,), (0,)), ((), ()))


def _params(n_grid, vmem=None):
    return pltpu.CompilerParams(dimension_semantics=("arbitrary",) * n_grid, vmem_limit_bytes=vmem)


def _hbm(*arrays):
    return [pltpu.with_memory_space_constraint(a, pltpu.HBM) for a in arrays]


def _out(shape, dtype):
    return pltpu.HBM(shape, dtype)


def _dot(a, b, dims=None):
    if dims is None:
        return jnp.dot(a, b, preferred_element_type=F32)
    return lax.dot_general(a, b, dims, preferred_element_type=F32)


def _sigmoid(x):
    return 1.0 / (1.0 + jnp.exp(-x))


def _rms_fwd(x, g):
    r = lax.rsqrt(jnp.mean(x * x, axis=-1, keepdims=True) + EPS)
    return x * r * g


def _rms_bwd(x, g, dy):
    r = lax.rsqrt(jnp.mean(x * x, axis=-1, keepdims=True) + EPS)
    xh = x * r
    dg = jnp.sum(dy * xh, axis=0, keepdims=True)
    t = dy * g
    dx = r * (t - xh * jnp.mean(t * xh, axis=-1, keepdims=True))
    return dx, dg


def _accumulate(ref, val, first):
    @pl.when(first)
    def _():
        ref[...] = val

    @pl.when(jnp.logical_not(first))
    def _():
        ref[...] += val


def _split2(x):
    hi = x.astype(BF16)
    lo = (x - hi.astype(F32)).astype(BF16)
    return hi, lo


def _ffn_fwd(x, g_pre, g_post, wg, wu, wd, name):
    T, D = x.shape
    S, FS, _ = wg.shape
    tm = min(ROW_BLOCK, T)

    def body(x_ref, gpre_ref, gpost_ref, wg_ref, wu_ref, wd_ref,
             h_ref, xn_ref, g_ref, u_ref, a_ref, f_ref, xn_s, acc_s):
        k = pl.program_id(1)

        @pl.when(k == 0)
        def _():
            xn_s[...] = _rms_fwd(x_ref[...], gpre_ref[...]).astype(BF16)
            xn_ref[...] = xn_s[...]

        xn = xn_s[...]
        g = _dot(xn, wg_ref[0], NT)
        u = _dot(xn, wu_ref[0], NT)
        g_ref[0] = g
        u_ref[0] = u
        a = (g * _sigmoid(g) * u).astype(BF16)
        a_ref[0] = a
        _accumulate(acc_s, _dot(a, wd_ref[0]), k == 0)

        @pl.when(k == S - 1)
        def _():
            f = acc_s[...]
            f_ref[...] = f
            h_ref[...] = x_ref[...] + 0.5 * _rms_fwd(f, gpost_ref[...])

    row = pl.BlockSpec((tm, D), lambda i, k: (i, 0))
    vec = pl.BlockSpec((1, D), lambda i, k: (0, 0))
    act = pl.BlockSpec((1, tm, FS), lambda i, k: (k, i, 0))
    return pl.pallas_call(
        body, name=name, grid=(T // tm, S),
        in_specs=[row, vec, vec] + [pl.BlockSpec((1, FS, D), lambda i, k: (k, 0, 0))] * 3,
        out_specs=[row, row, act, act, act, row],
        out_shape=[_out((T, D), F32), _out((T, D), BF16),
                   _out((S, T, FS), F32), _out((S, T, FS), F32),
                   _out((S, T, FS), BF16), _out((T, D), F32)],
        scratch_shapes=[pltpu.VMEM((tm, D), BF16), pltpu.VMEM((tm, D), F32)],
        compiler_params=_params(2, VMEM_LIMIT),
    )(*_hbm(x, g_pre, g_post, wg, wu, wd))


def _ffn_bwd_act(dh, f, g_post, wd, g_act, u_act, name):
    T, D = dh.shape
    S, FS, _ = wd.shape
    tm = min(ROW_BLOCK, T)

    def body(dh_ref, f_ref, gpost_ref, wd_ref, g_ref, u_ref, dgp_ref, dup_ref, df_ref, dgain_ref, df_s):
        i, k = pl.program_id(0), pl.program_id(1)

        @pl.when(k == 0)
        def _():
            df, dgain = _rms_bwd(f_ref[...], gpost_ref[...], 0.5 * dh_ref[...])
            df_s[...] = df.astype(BF16)
            df_ref[...] = df_s[...]
            _accumulate(dgain_ref, dgain, i == 0)

        da = _dot(df_s[...], wd_ref[0], NT)
        g = g_ref[0]
        s = _sigmoid(g)
        dup_ref[0] = (da * (g * s)).astype(BF16)
        dgp_ref[0] = (da * u_ref[0] * (s * (1.0 + g * (1.0 - s)))).astype(BF16)

    row = pl.BlockSpec((tm, D), lambda i, k: (i, 0))
    vec = pl.BlockSpec((1, D), lambda i, k: (0, 0))
    act = pl.BlockSpec((1, tm, FS), lambda i, k: (k, i, 0))
    return pl.pallas_call(
        body, name=name, grid=(T // tm, S),
        in_specs=[row, row, vec, pl.BlockSpec((1, FS, D), lambda i, k: (k, 0, 0)), act, act],
        out_specs=[act, act, row, vec],
        out_shape=[_out((S, T, FS), BF16), _out((S, T, FS), BF16),
                   _out((T, D), BF16), _out((1, D), F32)],
        scratch_shapes=[pltpu.VMEM((tm, D), BF16)],
        compiler_params=_params(2, VMEM_LIMIT),
    )(*_hbm(dh, f, g_post, wd, g_act, u_act))


def _proj_bwd(dys, ws, x, g_pre, dh, name):
    T, D = x.shape
    n = len(dys)
    flat = dys[0].ndim == 2
    S = ws[0].shape[0]
    N = ws[0].shape[2] if flat else ws[0].shape[1]
    tm = min(ROW_BLOCK, T)

    def body(*refs):
        dy_refs, w_refs = refs[:n], refs[n:2 * n]
        x_ref, gpre_ref, dh_ref, dx_ref, dgain_ref, acc_s = refs[2 * n:]
        i, k = pl.program_id(0), pl.program_id(1)
        part = None
        for dy_ref, w_ref in zip(dy_refs, w_refs):
            term = _dot(dy_ref[...], w_ref[0], NT) if flat else _dot(dy_ref[0], w_ref[0])
            part = term if part is None else part + term
        _accumulate(acc_s, part, k == 0)

        @pl.when(k == S - 1)
        def _():
            dx, dgain = _rms_bwd(x_ref[...], gpre_ref[...], acc_s[...])
            dx_ref[...] = dh_ref[...] + dx
            _accumulate(dgain_ref, dgain, i == 0)

    row = pl.BlockSpec((tm, D), lambda i, k: (i, 0))
    vec = pl.BlockSpec((1, D), lambda i, k: (0, 0))
    return pl.pallas_call(
        body, name=name, grid=(T // tm, S),
        in_specs=[pl.BlockSpec((tm, N), lambda i, k: (i, k)) if flat else pl.BlockSpec((1, tm, N), lambda i, k: (k, i, 0))] * n
        + [pl.BlockSpec((1,) + ws[0].shape[1:], lambda i, k: (k, 0, 0))] * n + [row, vec, row],
        out_specs=[row, vec],
        out_shape=[_out((T, D), F32), _out((1, D), F32)],
        scratch_shapes=[pltpu.VMEM((tm, D), F32)],
        compiler_params=_params(2, VMEM_LIMIT),
    )(*_hbm(*dys, *ws, x, g_pre, dh))


def _mm_tn(a, b, bm, name, groups=None):
    ga, T, M = a.shape
    if groups is None:
        gb, _, N = b.shape
        b_spec = pl.BlockSpec((1, T, N), (lambda g, m: (g, 0, 0)) if gb > 1 else (lambda g, m: (0, 0, 0)))
    else:
        gb, N = groups, b.shape[1] // groups
        b_spec = pl.BlockSpec((T, N), lambda g, m: (0, g))
    G = max(ga, gb)

    def body(a_ref, b_ref, o_ref, narrow_ref):
        bv = b_ref[0] if groups is None else b_ref[...]
        o_ref[0] = _dot(a_ref[0].astype(BF16), bv.astype(BF16), TN)
        narrow_ref[0] = o_ref[0].astype(BF16)

    out = pl.BlockSpec((1, bm, N), lambda g, m: (g, m, 0))
    return pl.pallas_call(
        body, name=name, grid=(G, M // bm),
        in_specs=[pl.BlockSpec((1, T, bm), (lambda g, m: (g, 0, m)) if ga > 1 else (lambda g, m: (0, 0, m))), b_spec],
        out_specs=[out, out],
        out_shape=[_out((G, M, N), F32), _out((G, M, N), BF16)],
        compiler_params=_params(2, VMEM_LIMIT),
    )(*_hbm(a, b))


def _norm_proj(x, g_pre, w, name):
    T, D = x.shape
    S, _, N = w.shape
    tm = min(ROW_BLOCK, T)

    def body(x_ref, g_ref, w_ref, o_ref, xn_ref, xn_s):
        @pl.when(pl.program_id(1) == 0)
        def _():
            xn_s[...] = _rms_fwd(x_ref[...], g_ref[...]).astype(BF16)
            xn_ref[...] = xn_s[...]

        o_ref[...] = _dot(xn_s[...], w_ref[0]).astype(BF16)

    row = pl.BlockSpec((tm, D), lambda i, k: (i, 0))
    return pl.pallas_call(
        body, name=name, grid=(T // tm, S),
        in_specs=[row, pl.BlockSpec((1, D), lambda i, k: (0, 0)), pl.BlockSpec((1, D, N), lambda i, k: (k, 0, 0))],
        out_specs=[pl.BlockSpec((tm, N), lambda i, k: (i, k)), row],
        out_shape=[_out((T, S * N), BF16), _out((T, D), BF16)],
        scratch_shapes=[pltpu.VMEM((tm, D), BF16)],
        compiler_params=_params(2, VMEM_LIMIT),
    )(*_hbm(x, g_pre, w))


def _mix_out_fwd(h, o_a, o_b, g_sb, g_ch, w_out, g_post, name):
    T, D = h.shape
    W = g_sb.shape[1]
    tm = min(ROW_BLOCK, T)

    def body(h_ref, oa_ref, ob_ref, gsb_ref, gch_ref, w_ref, gpost_ref, h2_ref, mixed_ref, mo_ref):
        mixed_ref[:, :W] = _rms_fwd(oa_ref[...], gsb_ref[...]).astype(BF16)
        mixed_ref[:, W:] = _rms_fwd(ob_ref[...], gch_ref[...]).astype(BF16)
        mo = _dot(mixed_ref[...], w_ref[...])
        mo_ref[...] = mo
        h2_ref[...] = h_ref[...] + _rms_fwd(mo, gpost_ref[...])

    row = pl.BlockSpec((tm, D), lambda i: (i, 0))
    part = pl.BlockSpec((tm, W), lambda i: (i, 0))
    half = pl.BlockSpec((1, W), lambda i: (0, 0))
    return pl.pallas_call(
        body, name=name, grid=(T // tm,),
        in_specs=[row, part, part, half, half, pl.BlockSpec((D, D), lambda i: (0, 0)), pl.BlockSpec((1, D), lambda i: (0, 0))],
        out_specs=[row, row, row],
        out_shape=[_out((T, D), F32), _out((T, D), BF16),
                   _out((T, D), F32)],
        compiler_params=_params(1, VMEM_LIMIT),
    )(*_hbm(h, o_a, o_b, g_sb, g_ch, w_out, g_post))


def _mix_out_bwd(dh, mo, g_post, w_out, o_a, o_b, g_sb, g_ch, name):
    T, D = dh.shape
    W = g_sb.shape[1]
    tm = min(ROW_BLOCK, T)

    def body(dh_ref, mo_ref, gpost_ref, w_ref, oa_ref, ob_ref, gsb_ref, gch_ref,
             dmo_ref, doa_ref, dob_ref, dgpost_ref, dgsb_ref, dgch_ref):
        first = pl.program_id(0) == 0
        dmo, dgpost = _rms_bwd(mo_ref[...], gpost_ref[...], dh_ref[...])
        dmo_ref[...] = dmo.astype(BF16)
        dmix = _dot(dmo_ref[...], w_ref[...], NT)
        doa_ref[...], dgsb = _rms_bwd(oa_ref[...], gsb_ref[...], dmix[:, :W])
        dob_ref[...], dgch = _rms_bwd(ob_ref[...], gch_ref[...], dmix[:, W:])
        _accumulate(dgpost_ref, dgpost, first)
        _accumulate(dgsb_ref, dgsb, first)
        _accumulate(dgch_ref, dgch, first)

    row = pl.BlockSpec((tm, D), lambda i: (i, 0))
    part = pl.BlockSpec((tm, W), lambda i: (i, 0))
    vec = pl.BlockSpec((1, D), lambda i: (0, 0))
    half = pl.BlockSpec((1, W), lambda i: (0, 0))
    return pl.pallas_call(
        body, name=name, grid=(T // tm,),
        in_specs=[row, row, vec, pl.BlockSpec((D, D), lambda i: (0, 0)), part, part, half, half],
        out_specs=[row, part, part, vec, half, half],
        out_shape=[_out((T, D), BF16), _out((T, W), F32),
                   _out((T, W), F32), _out((1, D), F32),
                   _out((1, W), F32), _out((1, W), F32)],
        compiler_params=_params(1, VMEM_LIMIT),
    )(*_hbm(dh, mo, g_post, w_out, o_a, o_b, g_sb, g_ch))


def _ple_loss(h, p, target, w_proj, w_gate, g_post, name):
    T, D = h.shape
    P = p.shape[1]
    S = N_CHIPS
    C = D // S
    tm = min(ROW_BLOCK, T)

    def body(h_ref, p_ref, t_ref, wp_ref, wg_ref, g_ref, loss_ref, dh_ref, dproj_ref, dgate_ref, dgain_ref):
        first = pl.program_id(0) == 0
        h3 = h_ref[...]
        proj = _dot(p_ref[...].astype(BF16), wp_ref[...])
        s = _sigmoid(_dot(h3.astype(BF16), wg_ref[...]))
        e = proj * s
        diff = h3 + _rms_fwd(e, g_ref[...]) - t_ref[...]
        part = 0.5 * jnp.sum(jnp.mean(diff * diff, axis=-1, keepdims=True), axis=0, keepdims=True)
        _accumulate(loss_ref, jnp.broadcast_to(part, loss_ref.shape), first)
        dy = diff * (1.0 / D)
        de, dgain = _rms_bwd(e, g_ref[...], dy)
        _accumulate(dgain_ref, dgain, first)
        dproj = (de * s).astype(BF16)
        for j in range(S):
            dproj_ref[j] = dproj[:, j * C:(j + 1) * C]
        dgate_ref[...] = (de * proj * s * (1.0 - s)).astype(BF16)
        dh_ref[...] = dy + _dot(dgate_ref[...], wg_ref[...], NT)

    row = pl.BlockSpec((tm, D), lambda i: (i, 0))
    vec = pl.BlockSpec((1, D), lambda i: (0, 0))
    return pl.pallas_call(
        body, name=name, grid=(T // tm,),
        in_specs=[row, pl.BlockSpec((tm, P), lambda i: (i, 0)), row,
                  pl.BlockSpec((P, D), lambda i: (0, 0)), pl.BlockSpec((D, D), lambda i: (0, 0)), vec],
        out_specs=[pl.BlockSpec((8, 128), lambda i: (0, 0)), row,
                   pl.BlockSpec((S, tm, C), lambda i: (0, i, 0)), row, vec],
        out_shape=[_out((8, 128), F32), _out((T, D), F32),
                   _out((S, T, C), BF16), _out((T, D), BF16),
                   _out((1, D), F32)],
        compiler_params=_params(1, VMEM_LIMIT),
    )(*_hbm(h, p, target, w_proj, w_gate, g_post))


def _sb_scores(q, kj, mask):
    z = _dot(q, kj, NT)
    sp = jnp.maximum(z, 0.0) + jnp.log(1.0 + jnp.exp(-jnp.abs(z)))
    return z, sp if mask is None else jnp.where(mask, sp, 0.0)


def _strict_causal():
    rows = lax.broadcasted_iota(jnp.int32, (SB_BLOCK, SB_BLOCK), 0)
    cols = lax.broadcasted_iota(jnp.int32, (SB_BLOCK, SB_BLOCK), 1)
    return cols < rows


def _tri(cmp):
    r = lax.broadcasted_iota(jnp.int32, (2 * SB_BLOCK, SB_BLOCK), 0) % SB_BLOCK
    c = lax.broadcasted_iota(jnp.int32, (2 * SB_BLOCK, SB_BLOCK), 1)
    return jnp.where(cmp(r, c), 1.0, 0.0).astype(BF16)


def _cum(x, tri):
    return _dot(jnp.concatenate(_split2(x), axis=1), tri)


def _pair_lanes():
    lane = lax.broadcasted_iota(jnp.int32, (1, PAIR), 1)
    return [lane < HEAD_DIM, lane >= HEAD_DIM]


def _only(lanes, x):
    return jnp.where(lanes, x, jnp.zeros_like(x))


def _sb_fwd(qkv, name):
    T = qkv.shape[0]
    B = SB_BLOCK
    W = SB_PAIRS * PAIR
    steps = N_HEADS // (2 * SB_PAIRS)
    heads = [(p, h) for p in range(SB_PAIRS) for h in range(2)]

    def body(q_ref, k_ref, v_ref, o_ref):
        i = pl.program_id(1)
        after = _tri(lambda r, c: r > c)
        lanes = _pair_lanes()
        cols = [slice(p * PAIR, (p + 1) * PAIR) for p in range(SB_PAIRS)]
        q = {(p, h): _only(lanes[h], q_ref[:, cols[p]] * ATT_SCALE) for p, h in heads}

        def tiles(j, carries, mask):
            at = pl.ds(pl.multiple_of(j * B, B), B)
            scores = [_sb_scores(q[ph], k_ref[at, cols[ph[0]]], mask) for ph in heads]
            laters = [_cum(sp, after) for _, sp in scores]
            out = []
            for ph, (z, sp), later, (run, acc) in zip(heads, scores, laters, carries):
                a = jnp.exp(z - sp - later - run)
                if mask is not None:
                    a = jnp.where(mask, a, 0.0)
                out.append((run + later[:, 0:1] + sp[:, 0:1],
                            acc + _dot(a.astype(BF16), _only(lanes[ph[1]], v_ref[at, cols[ph[0]]]))))
            return tuple(out)

        zero = (jnp.zeros((B, 1), F32), jnp.zeros((B, PAIR), F32))
        carries = tiles(i, (zero,) * len(heads), _strict_causal())
        carries = lax.fori_loop(0, i, lambda jj, cs: tiles(i - 1 - jj, cs, None), carries)
        for p in range(SB_PAIRS):
            o_ref[:, cols[p]] = carries[2 * p][1] + carries[2 * p + 1][1]

    blk = lambda off: pl.BlockSpec((B, W), lambda g, i: (i, g + off))
    full = lambda off: pl.BlockSpec((T, W), lambda g, i: (0, g + off))
    return pl.pallas_call(
        body, name=name, grid=(steps, T // B),
        in_specs=[blk(0), full(steps), full(2 * steps)],
        out_specs=blk(0),
        out_shape=_out((T, N_HEADS * HEAD_DIM), F32),
        compiler_params=_params(2, VMEM_LIMIT),
    )(*_hbm(qkv, qkv, qkv))


def _sb_bwd(qkv, do, o, name):
    T = qkv.shape[0]
    B = SB_BLOCK
    W = SB_PAIRS * PAIR
    steps = N_HEADS // (2 * SB_PAIRS)
    n_blocks = T // B
    heads = [(p, h) for p in range(SB_PAIRS) for h in range(2)]

    def body(q_ref, k_ref, v_ref, do_ref, o_ref, dq_ref, dk_ref, dv_ref, dk_s, dv_s):
        i = pl.program_id(1)

        @pl.when(i == 0)
        def _():
            dk_s[...] = jnp.zeros_like(dk_s)
            dv_s[...] = jnp.zeros_like(dv_s)

        after = _tri(lambda r, c: r > c)
        since = _tri(lambda r, c: r >= c)
        lanes = _pair_lanes()
        cols = [slice(p * PAIR, (p + 1) * PAIR) for p in range(SB_PAIRS)]
        q = {(p, h): _only(lanes[h], q_ref[:, cols[p]] * ATT_SCALE) for p, h in heads}
        do = {(p, h): _only(lanes[h], do_ref[:, cols[p]].astype(BF16)) for p, h in heads}
        total = {ph: jnp.sum(do[ph].astype(F32) * o_ref[:, cols[ph[0]]], axis=1, keepdims=True) for ph in heads}

        def tiles(j, carries, mask):
            at = pl.ds(pl.multiple_of(j * B, B), B)
            ks = [k_ref[at, c] for c in cols]
            vs = [v_ref[at, c] for c in cols]
            scores = [_sb_scores(q[ph], ks[ph[0]], mask) for ph in heads]
            laters = [_cum(sp, after) for _, sp in scores]
            das = [_dot(do[ph], vs[ph[0]], NT) for ph in heads]
            a_s, gs = [], []
            for (z, sp), later, da, carry in zip(scores, laters, das, carries):
                a = jnp.exp(z - sp - later - carry[0])
                if mask is not None:
                    a = jnp.where(mask, a, 0.0)
                a = a.astype(BF16)
                a_s.append(a)
                gs.append(a.astype(F32) * da)
            sinces = [_cum(g, since) for g in gs]
            dzs = []
            for ph, (_, sp), g, from_s, carry in zip(heads, scores, gs, sinces, carries):
                g_before = total[ph] - carry[1] - from_s
                fail = jnp.exp(-sp)
                dz = fail * (g + g_before) - g_before
                if mask is not None:
                    dz = jnp.where(mask, dz, 0.0)
                dzs.append(dz.astype(BF16))
            out = []
            for ph, (_, sp), a, dz, later, from_s, carry in zip(heads, scores, a_s, dzs, laters, sinces, carries):
                dk_s[at, cols[ph[0]]] += _dot(dz, q[ph], TN)
                dv_s[at, cols[ph[0]]] += _dot(a, do[ph], TN)
                out.append((carry[0] + later[:, 0:1] + sp[:, 0:1], carry[1] + from_s[:, 0:1],
                            carry[2] + _dot(dz, _only(lanes[ph[1]], ks[ph[0]]))))
            return tuple(out)

        col = jnp.zeros((B, 1), F32)
        zero = (col, col, jnp.zeros((B, PAIR), F32))
        carries = tiles(i, (zero,) * len(heads), _strict_causal())
        last = lax.fori_loop(0, i, lambda jj, cs: tiles(i - 1 - jj, cs, None), carries)
        for p in range(SB_PAIRS):
            dq_ref[:, cols[p]] = ((last[2 * p][2] + last[2 * p + 1][2]) * ATT_SCALE).astype(BF16)

        @pl.when(i == n_blocks - 1)
        def _():
            dk_ref[...] = dk_s[...].astype(BF16)
            dv_ref[...] = dv_s[...].astype(BF16)

    blk = lambda off: pl.BlockSpec((B, W), lambda g, i: (i, g + off))
    full = lambda off: pl.BlockSpec((T, W), lambda g, i: (0, g + off))
    out = _out((T, N_HEADS * HEAD_DIM), BF16)
    return pl.pallas_call(
        body, name=name, grid=(steps, n_blocks),
        in_specs=[blk(0), full(steps), full(2 * steps), blk(0), blk(0)],
        out_specs=[blk(0), full(0), full(0)],
        out_shape=[out, out, out],
        scratch_shapes=[pltpu.VMEM((T, W), F32)] * 2,
        compiler_params=_params(2, VMEM_LIMIT),
    )(*_hbm(qkv, qkv, qkv, do, o))


NEAR = BAND - PAD + REL_CLIP
FAR = BAND - NEAR
NEAR_REL = 2 * REL_CLIP
BIAS_ROWS = 8


def _rel_onehot(i, transposed):
    shape = (NEAR, NEAR_REL) if transposed else (NEAR_REL, NEAR)
    j = FAR + lax.broadcasted_iota(jnp.int32, shape, 0 if transposed else 1)
    r = lax.broadcasted_iota(jnp.int32, shape, 1 if transposed else 0)
    idx = jnp.clip(i + PAD - j, -REL_CLIP, REL_CLIP) + REL_CLIP
    return jnp.where(idx - 1 == r, 1.0, 0.0).astype(BF16)


def _bias_table(rel_bias, name):
    def body(near_ref, far_ref, o_ref):
        rb = near_ref[...]
        hi, lo = _split2(rb)
        lo2 = (rb - hi.astype(F32) - lo.astype(F32)).astype(BF16)
        far = jnp.broadcast_to(far_ref[...], (N_HEADS, FAR))
        for k in range(BIAS_ROWS):
            onehot = _rel_onehot(pl.program_id(0) * BIAS_ROWS + k, False)
            o_ref[k, :, :FAR] = far
            o_ref[k, :, FAR:] = _dot(hi, onehot) + _dot(lo, onehot) + _dot(lo2, onehot)

    return pl.pallas_call(
        body, name=name, grid=(CHUNK // BIAS_ROWS,),
        in_specs=[pl.BlockSpec((N_HEADS, NEAR_REL), lambda i: (0, 0)), pl.BlockSpec((N_HEADS, 1), lambda i: (0, 0))],
        out_specs=pl.BlockSpec((BIAS_ROWS, N_HEADS, BAND), lambda i: (i, 0, 0)),
        out_shape=_out((CHUNK, N_HEADS, BAND), F32),
        compiler_params=_params(1),
    )(*_hbm(rel_bias[:, 1:], rel_bias[:, N_REL - 1:]))


def _bias_grad(dbias_t, name):
    def body(d_ref, near_ref, far_ref):
        near, far = None, None
        for k in range(BIAS_ROWS):
            onehot = _rel_onehot(pl.program_id(0) * BIAS_ROWS + k, True)
            hi, lo = _split2(d_ref[k, :, FAR:])
            part = _dot(hi, onehot) + _dot(lo, onehot)
            rest = jnp.sum(d_ref[k, :, :FAR], axis=1, keepdims=True)
            near, far = (part, rest) if near is None else (near + part, far + rest)
        first = pl.program_id(0) == 0
        _accumulate(near_ref, near, first)
        _accumulate(far_ref, jnp.broadcast_to(far, far_ref.shape), first)

    near, far = pl.pallas_call(
        body, name=name, grid=(CHUNK // BIAS_ROWS,),
        in_specs=[pl.BlockSpec((BIAS_ROWS, N_HEADS, BAND), lambda i: (i, 0, 0))],
        out_specs=[pl.BlockSpec((N_HEADS, NEAR_REL), lambda i: (0, 0)), pl.BlockSpec((N_HEADS, 128), lambda i: (0, 0))],
        out_shape=[_out((N_HEADS, NEAR_REL), F32), _out((N_HEADS, 128), F32)],
        compiler_params=_params(1),
    )(*_hbm(dbias_t))
    return jnp.pad(near, ((0, 0), (1, 0))).at[:, N_REL - 1].add(far[:, 0])


def _ch_probs(scores, bias, valid):
    z = jnp.where(valid, scores * ATT_SCALE + bias, NEG_INF)
    e = jnp.exp(z - jnp.max(z, axis=-1, keepdims=True))
    return e / jnp.sum(e, axis=-1, keepdims=True)


CH_HEADS = [(pair, h) for pair in range(N_HEADS // 2) for h in range(2)]
CH_COLS = [slice(pair * PAIR, (pair + 1) * PAIR) for pair in range(N_HEADS // 2)]


CH_GROUP = 2
CH_Q = CH_GROUP * CHUNK
CH_WIN = (LOOKBACK + CH_GROUP) * CHUNK


def _ch_valid(n):
    row_chunk = lax.broadcasted_iota(jnp.int32, (CH_Q, CH_WIN), 0) // CHUNK
    slot = lax.broadcasted_iota(jnp.int32, (CH_Q, CH_WIN), 1)
    ahead = slot // CHUNK - row_chunk
    return (ahead >= 0) & (ahead <= LOOKBACK) & (n * CH_Q + slot >= PAD)


def _ch_group_bias(bias):
    shifted = [jnp.pad(bias, ((0, 0), (0, 0), (c * CHUNK, (CH_GROUP - 1 - c) * CHUNK))) for c in range(CH_GROUP)]
    return jnp.concatenate(shifted, axis=1)


def _ch_fold_bias_grad(dbias):
    parts = [dbias[:, c * CHUNK:(c + 1) * CHUNK, c * CHUNK:c * CHUNK + BAND] for c in range(CH_GROUP)]
    return sum(parts[1:], parts[0])


def _ch_fwd(qkv, bias, name):
    T = qkv.shape[0]
    W = N_HEADS * HEAD_DIM

    def body(q_ref, k_ref, v_ref, b_ref, o_ref, kp, vp):
        n = pl.program_id(0)

        @pl.when(n == 0)
        def _():
            _ch_load_padded(k_ref, v_ref, kp, vp)

        win = pl.ds(pl.multiple_of(n * CH_Q, CH_Q), CH_WIN)
        valid = _ch_valid(n)
        lanes = _pair_lanes()
        scores = [_dot(_only(lanes[h], q_ref[:, CH_COLS[pair]]), kp[win, CH_COLS[pair]], NT) for pair, h in CH_HEADS]
        probs = [_ch_probs(s, b_ref[2 * pair + h], valid).astype(BF16) for s, (pair, h) in zip(scores, CH_HEADS)]
        outs = [_dot(p, _only(lanes[h], vp[win, CH_COLS[pair]])) for p, (pair, h) in zip(probs, CH_HEADS)]
        for pair, cols in enumerate(CH_COLS):
            o_ref[:, cols] = outs[2 * pair] + outs[2 * pair + 1]

    full = lambda col: pl.BlockSpec((T, W), lambda n: (0, col))
    return pl.pallas_call(
        body, name=name, grid=(T // CH_Q,),
        in_specs=[pl.BlockSpec((CH_Q, W), lambda n: (n, 3)), full(4), full(5),
                  pl.BlockSpec((N_HEADS, CH_Q, CH_WIN), lambda n: (0, 0, 0))],
        out_specs=pl.BlockSpec((CH_Q, W), lambda n: (n, 0)),
        out_shape=_out((T, W), F32),
        scratch_shapes=[pltpu.VMEM((PAD + T, W), BF16)] * 2,
        compiler_params=_params(1, VMEM_LIMIT),
    )(*_hbm(qkv, qkv, qkv, bias))


def _ch_load_padded(k_ref, v_ref, kp, vp):
    for src, dst in ((k_ref, kp), (v_ref, vp)):
        dst[:PAD, :] = jnp.zeros((PAD, dst.shape[1]), dst.dtype)
        dst[PAD:, :] = src[...]


def _ch_bwd(qkv, bias, do, name):
    T = qkv.shape[0]
    W = N_HEADS * HEAD_DIM
    n_chunks = T // CH_Q

    def body(q_ref, k_ref, v_ref, b_ref, do_ref, dq_ref, dk_ref, dv_ref, db_ref, kp, vp, dk_s, dv_s):
        n = pl.program_id(0)

        @pl.when(n == 0)
        def _():
            _ch_load_padded(k_ref, v_ref, kp, vp)
            dk_s[...] = jnp.zeros_like(dk_s)
            dv_s[...] = jnp.zeros_like(dv_s)
            db_ref[...] = jnp.zeros_like(db_ref)

        win = pl.ds(pl.multiple_of(n * CH_Q, CH_Q), CH_WIN)
        valid = _ch_valid(n)
        lanes = _pair_lanes()
        kws = [kp[win, cols] for cols in CH_COLS]
        vws = [vp[win, cols] for cols in CH_COLS]
        qs = [_only(lanes[h], q_ref[:, CH_COLS[pair]]) for pair, h in CH_HEADS]
        dos = [_only(lanes[h], do_ref[:, CH_COLS[pair]].astype(BF16)) for pair, h in CH_HEADS]
        scores = [_dot(q, kws[pair], NT) for q, (pair, _) in zip(qs, CH_HEADS)]
        dps = [_dot(do, vws[pair], NT) for do, (pair, _) in zip(dos, CH_HEADS)]
        probs = [_ch_probs(s, b_ref[2 * pair + h], valid) for s, (pair, h) in zip(scores, CH_HEADS)]
        dzs = [p * (dp - jnp.sum(dp * p, axis=-1, keepdims=True)) for p, dp in zip(probs, dps)]
        for k, dz in enumerate(dzs):
            db_ref[k] += dz
        dzbs = [(dz * ATT_SCALE).astype(BF16) for dz in dzs]
        dqs = [_dot(dz, _only(lanes[h], kws[pair])) for dz, (pair, h) in zip(dzbs, CH_HEADS)]
        dks = [_dot(dz, q, TN) for dz, q in zip(dzbs, qs)]
        dvs = [_dot(p.astype(BF16), do, TN) for p, do in zip(probs, dos)]
        for pair, cols in enumerate(CH_COLS):
            dq_ref[:, cols] = (dqs[2 * pair] + dqs[2 * pair + 1]).astype(BF16)
            dk_s[win, cols] += dks[2 * pair] + dks[2 * pair + 1]
            dv_s[win, cols] += dvs[2 * pair] + dvs[2 * pair + 1]

        @pl.when(n == n_chunks - 1)
        def _():
            dk_ref[...] = dk_s[PAD:, :].astype(BF16)
            dv_ref[...] = dv_s[PAD:, :].astype(BF16)

    full = lambda col: pl.BlockSpec((T, W), lambda n: (0, col))
    blk = lambda col: pl.BlockSpec((CH_Q, W), lambda n: (n, col))
    tab = pl.BlockSpec((N_HEADS, CH_Q, CH_WIN), lambda n: (0, 0, 0))
    out = _out((T, W), BF16)
    return pl.pallas_call(
        body, name=name, grid=(n_chunks,),
        in_specs=[blk(3), full(4), full(5), tab, blk(0)],
        out_specs=[blk(0), full(0), full(0), tab],
        out_shape=[out, out, out, _out((N_HEADS, CH_Q, CH_WIN), F32)],
        scratch_shapes=[pltpu.VMEM((PAD + T, W), BF16)] * 2 + [pltpu.VMEM((PAD + T, W), F32)] * 2,
        compiler_params=_params(1, VMEM_LIMIT),
    )(*_hbm(qkv, qkv, qkv, bias, do))


def _rows_split(a, parts):
    return a.reshape(a.shape[:-2] + (parts, a.shape[-2] // parts, a.shape[-1]))


def _cast_into_slot0(c, ws, name):
    parts = 2
    ws = [_rows_split(_rows_split(w, 2), parts) for w in ws]
    n = len(ws)

    def body(c_ref, *refs):
        for src, dst in zip(refs[:n], refs[n:]):
            dst[0, 0, 0] = src[0, 0].astype(BF16)

    outs = pl.pallas_call(
        body, name=name,
        grid_spec=pltpu.PrefetchScalarGridSpec(
            num_scalar_prefetch=1, grid=(2, parts),
            in_specs=[pl.BlockSpec((1, 1) + w.shape[2:], lambda d, r, c_ref: (d ^ c_ref[0], r, 0, 0)) for w in ws],
            out_specs=[pl.BlockSpec((1, 1, 1) + w.shape[2:], lambda d, r, c_ref: (0, d, r, 0, 0)) for w in ws]),
        out_shape=[_out((N_CHIPS,) + w.shape, BF16) for w in ws],
        compiler_params=_params(2, VMEM_LIMIT),
    )(c, *_hbm(*ws))
    return [o.reshape(N_CHIPS, 2, o.shape[2] * o.shape[3], o.shape[4]) for o in outs]


def _chip_order(me, c, lands, name):
    parts = 2
    xs = [_rows_split(x, parts) for x in lands]

    def body(me_ref, c_ref, *refs):
        n = len(refs) // 2
        for src, dst in zip(refs[:n], refs[n:]):
            dst[...] = src[...]

    outs = pl.pallas_call(
        body, name=name,
        grid_spec=pltpu.PrefetchScalarGridSpec(
            num_scalar_prefetch=2, grid=(N_CHIPS, 2, parts),
            in_specs=[pl.BlockSpec((1, 1, 1) + x.shape[3:],
                                   lambda j, h, r, me_ref, c_ref: (j ^ me_ref[0], h ^ c_ref[0], r, 0, 0)) for x in xs],
            out_specs=[pl.BlockSpec((1, 1, 1) + x.shape[3:], lambda j, h, r, me_ref, c_ref: (j, h, r, 0, 0))
                       for x in xs]),
        out_shape=[_out(x.shape, x.dtype) for x in xs],
        compiler_params=_params(3, VMEM_LIMIT),
    )(me, c, *_hbm(*xs))
    return [o.reshape(o.shape[0], 2 * parts * o.shape[3], o.shape[4]) for o in outs]


def _pair_add(c, mine, got, permuted, name):
    parts = 2
    mine = [_rows_split(m, parts) for m in mine]
    got = [_rows_split(g, parts) for g in got]
    n = len(mine)

    def body(c_ref, *refs):
        for a, b, o in zip(refs[:n], refs[n:2 * n], refs[2 * n:]):
            o[0, 0] = (a[0, 0, 0] + b[0, 0].astype(F32)).astype(BF16)

    def mine_spec(m, perm):
        if perm:
            return pl.BlockSpec((1, 1, 1) + m.shape[3:], lambda j, r, c_ref: (j, 0, r, 0, 0))
        return pl.BlockSpec((1, 1, 1) + m.shape[3:], lambda j, r, c_ref: (j, c_ref[0], r, 0, 0))

    def got_spec(g):
        return pl.BlockSpec((1, 1) + g.shape[2:], lambda j, r, c_ref: (j, r, 0, 0))

    outs = pl.pallas_call(
        body, name=name,
        grid_spec=pltpu.PrefetchScalarGridSpec(
            num_scalar_prefetch=1, grid=(N_CHIPS, parts),
            in_specs=[mine_spec(m, perm) for m, perm in zip(mine, permuted)] + [got_spec(g) for g in got],
            out_specs=[got_spec(g) for g in got]),
        out_shape=[_out(g.shape, BF16) for g in got],
        compiler_params=_params(2, VMEM_LIMIT),
    )(c, *_hbm(*mine, *got))
    return [o.reshape(o.shape[0], o.shape[1] * o.shape[2], o.shape[3]) for o in outs]


def _chip_add(me, partials, landed, permuted, name):
    parts = 2
    ps = [_rows_split(x, parts) for x in partials]
    ls = [_rows_split(x, parts) for x in landed]
    n = len(ps)

    def body(me_ref, *refs):
        for own, got, o in zip(refs[:n], refs[n:2 * n], refs[2 * n:]):
            acc = own[0, 0].astype(F32)
            for r in range(N_CHIPS - 1):
                acc = acc + got[r, 0].astype(F32)
            o[0] = acc

    def own_spec(x, perm):
        if perm:
            return pl.BlockSpec((1, 1) + x.shape[2:], lambda r, me_ref: (0, r, 0, 0))
        return pl.BlockSpec((1, 1) + x.shape[2:], lambda r, me_ref: (me_ref[0], r, 0, 0))

    outs = pl.pallas_call(
        body, name=name,
        grid_spec=pltpu.PrefetchScalarGridSpec(
            num_scalar_prefetch=1, grid=(parts,),
            in_specs=[own_spec(x, perm) for x, perm in zip(ps, permuted)]
            + [pl.BlockSpec((N_CHIPS - 1, 1) + x.shape[2:], lambda r, me_ref: (0, r, 0, 0)) for x in ls],
            out_specs=[pl.BlockSpec((1,) + x.shape[2:], lambda r, me_ref: (r, 0, 0)) for x in ps]),
        out_shape=[_out(x.shape[1:], F32) for x in ps],
        compiler_params=_params(1, VMEM_LIMIT),
    )(me, *_hbm(*ps, *ls))
    return [o.reshape(o.shape[0] * o.shape[1], o.shape[2]) for o in outs]


def _adamw_math(w, g, m, v):
    m = ADAM_B1 * m + (1.0 - ADAM_B1) * g
    v = ADAM_B2 * v + (1.0 - ADAM_B2) * (g * g)
    m_hat = m / (1.0 - ADAM_B1 ** ADAM_STEP)
    v_hat = v / (1.0 - ADAM_B2 ** ADAM_STEP)
    delta = -ADAM_LR * (m_hat / (jnp.sqrt(v_hat) + ADAM_EPS) + ADAM_WD * w)
    return delta, m, v


def _adamw(ws, gs, ms, vs, parts, name):
    n = len(ws)
    flat = [_rows_split(a, parts) for a in (*ws, *gs, *ms, *vs)]

    def body(*refs):
        ins, outs = refs[:4 * n], refs[4 * n:]
        for k in range(n):
            d, m, v = _adamw_math(ins[k][...], ins[n + k][...], ins[2 * n + k][...], ins[3 * n + k][...])
            outs[k][...] = d
            outs[n + k][...] = m
            outs[2 * n + k][...] = v

    spec = lambda a: pl.BlockSpec((1,) + a.shape[1:], lambda i: (i, 0, 0))
    outs = pl.pallas_call(
        body, name=name, grid=(parts,),
        in_specs=[spec(a) for a in flat], out_specs=[spec(a) for a in flat[:n]] * 3,
        out_shape=[_out(a.shape, F32) for a in flat[:n]] * 3,
        compiler_params=_params(1, VMEM_LIMIT),
    )(*_hbm(*flat))
    outs = [o.reshape(o.shape[0] * o.shape[1], o.shape[2]) for o in outs]
    return outs[:n], outs[n:2 * n], outs[2 * n:]


def _adamw_half(half, ws, gs, ms, vs, carried, after, name):
    parts = 4
    n = len(ws)
    halved = lambda a: _rows_split(_rows_split(a, 2), parts)
    whole = [halved(a) for a in (*ws, *ms, *vs)]
    halves = [_rows_split(a, parts) for a in gs]
    prev = [] if carried is None else [halved(a) for group in carried for a in group]
    extra = [] if after is None else [after]

    def body(h_ref, *refs):
        ins, outs = refs[:4 * n], refs[4 * n + len(prev) + len(extra):]
        for k in range(n):
            g = ins[3 * n + k][0]
            d, m, v = _adamw_math(ins[k][0, 0], g, ins[n + k][0, 0], ins[2 * n + k][0, 0])
            for slot, val in enumerate((g, d, m, v)):
                outs[slot * n + k][0, 0] = val

    wspec = lambda a: pl.BlockSpec((1, 1) + a.shape[2:], lambda r, h_ref: (h_ref[0], r, 0, 0))
    hspec = lambda a: pl.BlockSpec((1,) + a.shape[1:], lambda r, h_ref: (r, 0, 0))
    outs = pl.pallas_call(
        body, name=name,
        grid_spec=pltpu.PrefetchScalarGridSpec(
            num_scalar_prefetch=1, grid=(parts,),
            in_specs=[wspec(a) for a in whole] + [hspec(a) for a in halves] + [ANY] * (len(prev) + len(extra)),
            out_specs=[wspec(a) for a in whole[:n]] * 4),
        out_shape=[_out(a.shape, F32) for a in whole[:n]] * 4,
        input_output_aliases={1 + 4 * n + j: j for j in range(len(prev))},
        compiler_params=_params(1, VMEM_LIMIT),
    )(half, *_hbm(*whole, *halves, *prev), *extra)
    outs = [o.reshape(2 * parts * o.shape[2], o.shape[3]) for o in outs]
    return outs[:n], outs[n:2 * n], outs[2 * n:3 * n], outs[3 * n:]


def _place():
    x, y, c = lax.axis_index("x"), lax.axis_index("y"), lax.axis_index("c")
    peers = [(x ^ (r >> 1), y ^ (r & 1), c) for r in (1, 2, 3)]
    return x, y, c, peers


def _handshake(peers):
    barrier = pltpu.get_barrier_semaphore()
    for peer in peers:
        pl.semaphore_signal(barrier, inc=1, device_id=peer, device_id_type=MESH)
    pl.semaphore_wait(barrier, len(peers))


ANY = pl.BlockSpec(memory_space=pl.ANY)
HBM = pl.BlockSpec(memory_space=pltpu.HBM)
SEM = pl.BlockSpec(memory_space=pltpu.SEMAPHORE)
SPLIT_COPY = pltpu.SideEffectType.DATAFLOW_SIDE_EFFECTING


def _in_hbm(a):
    return pltpu.with_memory_space_constraint(a, pltpu.HBM)


def _split_start(body, name, collective_id, operands, n_sems, after=None):
    n = len(operands)
    extra = [] if after is None else [after]

    def wrapped(*refs):
        at = n + len(extra)
        body(refs[:n], refs[at], refs[at + 1])
        token = refs[-1]
        token[...] = jnp.zeros_like(token)

    outs = pl.pallas_call(
        wrapped, name=name,
        in_specs=[HBM] * n + [ANY] * len(extra),
        out_shape=(pltpu.SemaphoreType.DMA((n_sems,)), pltpu.SemaphoreType.DMA((n_sems,)),
                   *[pltpu.HBM(a.shape, a.dtype) for a in operands], jax.ShapeDtypeStruct((8, 128), F32)),
        out_specs=(SEM, SEM, *[HBM] * n, pl.BlockSpec(memory_space=pltpu.VMEM)),
        input_output_aliases={i: 2 + i for i in range(n)},
        compiler_params=pltpu.CompilerParams(has_side_effects=SPLIT_COPY, collective_id=collective_id),
    )(*[_in_hbm(a) for a in operands], *extra)
    return outs[0], outs[1], list(outs[2:2 + n]), outs[-1]


def _split_wait(body, name, send_sem, recv_sem, operands, after):
    n = len(operands)

    def wrapped(*refs):
        body(refs[:n], refs[n], refs[n + 1])

    outs = pl.pallas_call(
        wrapped, name=name,
        in_specs=[HBM] * n + [SEM, SEM, ANY],
        out_shape=tuple(pltpu.HBM(a.shape, a.dtype) for a in operands),
        out_specs=tuple([HBM] * n),
        input_output_aliases={i: i for i in range(n)},
        compiler_params=pltpu.CompilerParams(has_side_effects=SPLIT_COPY),
    )(*operands, send_sem, recv_sem, after)
    return list(outs)


def _gather_copies(lands, send_sem, recv_sem):
    peers = _place()[3]
    return [pltpu.make_async_remote_copy(
        src_ref=land.at[0, 0], dst_ref=land.at[r + 1, 0],
        send_sem=send_sem.at[a * 3 + r], recv_sem=recv_sem.at[a * 3 + r],
        device_id=peers[r], device_id_type=MESH) for a, land in enumerate(lands) for r in range(3)]


def _gather_start(lands, name, collective_id, after):
    def body(refs, send_sem, recv_sem):
        _handshake(_place()[3])
        for cp in _gather_copies(refs, send_sem, recv_sem):
            cp.start()

    return _split_start(body, name, collective_id, list(lands), 3 * len(lands), after)


def _gather_wait(send_sem, recv_sem, operands, after, name):
    def body(refs, send_sem, recv_sem):
        for cp in _gather_copies(refs, send_sem, recv_sem):
            cp.wait_send()
            cp.wait_recv()

    return _split_wait(body, name, send_sem, recv_sem, operands, after)


def _gather_finish(lands, with_ici, name):
    n = len(lands)

    def body(*refs):
        land = refs[n:2 * n]
        send_ici, recv_ici, send_d2d, recv_d2d = refs[2 * n:]
        x, y, c, _ = _place()
        ici = _gather_copies(land, send_ici, recv_ici) if with_ici else []
        for cp in ici:
            cp.start()
        passed = [pltpu.make_async_remote_copy(
            src_ref=land[a].at[r + 1, 0], dst_ref=land[a].at[r + 1, 1],
            send_sem=send_d2d.at[a * 3 + r], recv_sem=recv_d2d.at[a * 3 + r],
            device_id=(x, y, 1 - c), device_id_type=MESH) for a in range(n) for r in range(3)]
        for k, cp in enumerate(passed):
            if with_ici:
                ici[k].wait_recv()
            cp.start()
        for cp in passed:
            cp.wait_recv()
        for cp in ici:
            cp.wait_send()
        for cp in passed:
            cp.wait_send()

    outs = pl.pallas_call(
        body, name=name,
        in_specs=[ANY] * n, out_specs=[ANY] * n,
        out_shape=[_out(l.shape, l.dtype) for l in lands],
        input_output_aliases={a: a for a in range(n)},
        scratch_shapes=[pltpu.SemaphoreType.DMA((3 * n,))] * 4,
    )(*lands)
    return list(outs)


def _slabs(land):
    return land.reshape(N_CHIPS, 2 * land.shape[2], land.shape[3])


def _pair_swap(grads, permuted, name):
    n = len(grads)

    def body(*refs):
        src, dst = refs[:n], refs[n:2 * n]
        send_sem, recv_sem = refs[2 * n:]
        x, y, c, _ = _place()
        copies = [pltpu.make_async_remote_copy(
            src_ref=src[a].at[:, 1] if permuted[a] else src[a].at[:, 1 - c], dst_ref=dst[a],
            send_sem=send_sem.at[a], recv_sem=recv_sem.at[a],
            device_id=(x, y, 1 - c), device_id_type=MESH) for a in range(n)]
        for cp in copies:
            cp.start()
        for cp in copies:
            cp.wait()

    return pl.pallas_call(
        body, name=name,
        in_specs=[ANY] * n, out_specs=[ANY] * n,
        out_shape=[_out((N_CHIPS,) + g.shape[2:], g.dtype) for g in grads],
        scratch_shapes=[pltpu.SemaphoreType.DMA((n,))] * 2,
    )(*grads)


def _swap_copies(refs, permuted, send_sem, recv_sem):
    n = len(refs) // 2
    x, y, c, _ = _place()
    return [pltpu.make_async_remote_copy(
        src_ref=refs[a].at[:, 1] if permuted[a] else refs[a].at[:, 1 - c], dst_ref=refs[n + a],
        send_sem=send_sem.at[a], recv_sem=recv_sem.at[a],
        device_id=(x, y, 1 - c), device_id_type=MESH) for a in range(n)]


def _pair_swap_start(grads, permuted, name, collective_id):
    def body(refs, send_sem, recv_sem):
        x, y, c, _ = _place()
        _handshake([(x, y, 1 - c)])
        for cp in _swap_copies(refs, permuted, send_sem, recv_sem):
            cp.start()

    lands = [lax.empty((N_CHIPS,) + g.shape[2:], g.dtype) for g in grads]
    return _split_start(body, name, collective_id, list(grads) + lands, len(grads))


def _pair_swap_wait(send_sem, recv_sem, operands, permuted, after, name):
    def body(refs, send_sem, recv_sem):
        for cp in _swap_copies(refs, permuted, send_sem, recv_sem):
            cp.wait_send()
            cp.wait_recv()

    return _split_wait(body, name, send_sem, recv_sem, operands, after)


def _scatter_copies(refs, permuted, send_sem, recv_sem):
    n = len(refs) // 2
    x, y, _, peers = _place()
    me = 2 * x + y
    return [pltpu.make_async_remote_copy(
        src_ref=refs[a].at[r + 1] if permuted[a] else refs[a].at[me ^ (r + 1)], dst_ref=refs[n + a].at[r],
        send_sem=send_sem.at[a * 3 + r], recv_sem=recv_sem.at[a * 3 + r],
        device_id=peers[r], device_id_type=MESH) for a in range(n) for r in range(3)]


def _scatter_start(partials, permuted, name, collective_id):
    def body(refs, send_sem, recv_sem):
        _handshake(_place()[3])
        for cp in _scatter_copies(refs, permuted, send_sem, recv_sem):
            cp.start()

    lands = [lax.empty((N_CHIPS - 1,) + p.shape[1:], p.dtype) for p in partials]
    return _split_start(body, name, collective_id, list(partials) + lands, 3 * len(partials))


def _scatter_wait(send_sem, recv_sem, operands, permuted, after, name):
    def body(refs, send_sem, recv_sem):
        for cp in _scatter_copies(refs, permuted, send_sem, recv_sem):
            cp.wait_send()
            cp.wait_recv()

    return _split_wait(body, name, send_sem, recv_sem, operands, after)


def _join_copies(refs, send_sem, recv_sem):
    n = len(refs) // 2
    x, y, c, _ = _place()
    return [pltpu.make_async_remote_copy(
        src_ref=refs[a], dst_ref=refs[n + a], send_sem=send_sem.at[a], recv_sem=recv_sem.at[a],
        device_id=(x, y, 1 - c), device_id_type=MESH) for a in range(n)]


def _pair_join_start(halves, name, collective_id):
    def body(refs, send_sem, recv_sem):
        x, y, c, _ = _place()
        _handshake([(x, y, 1 - c)])
        for cp in _join_copies(refs, send_sem, recv_sem):
            cp.start()

    lands = [lax.empty(h.shape, h.dtype) for h in halves]
    return _split_start(body, name, collective_id, list(halves) + lands, len(halves))


def _pair_join_wait(send_sem, recv_sem, operands, after, name):
    def body(refs, send_sem, recv_sem):
        for cp in _join_copies(refs, send_sem, recv_sem):
            cp.wait_send()
            cp.wait_recv()

    return _split_wait(body, name, send_sem, recv_sem, operands, after)


def _all_sum_small(v, after, name):
    R, C = v.shape
    n_dev = 8

    def body(v_ref, after_ref, o_ref, buf, send_sem, recv_sem):
        x, y, c, _ = _place()
        me = 4 * x + 2 * y + c
        buf[me] = v_ref[...]
        copies = []
        for k in range(1, n_dev):
            peer = (x ^ (k >> 2), y ^ ((k >> 1) & 1), c ^ (k & 1))
            copies.append(pltpu.make_async_remote_copy(
                src_ref=v_ref, dst_ref=buf.at[me], send_sem=send_sem.at[k - 1], recv_sem=recv_sem.at[k - 1],
                device_id=peer, device_id_type=MESH))
        for cp in copies:
            cp.start()
        for cp in copies:
            cp.wait()
        acc = buf[0]
        for m in range(1, n_dev):
            acc = acc + buf[m]
        o_ref[...] = acc

    return pl.pallas_call(
        body, name=name,
        in_specs=[pl.BlockSpec(memory_space=pltpu.VMEM), ANY], out_specs=pl.BlockSpec(memory_space=pltpu.VMEM),
        out_shape=jax.ShapeDtypeStruct((R, C), F32),
        scratch_shapes=[pltpu.VMEM((n_dev, R, C), F32), pltpu.SemaphoreType.DMA((n_dev - 1,)),
                        pltpu.SemaphoreType.DMA((n_dev - 1,))],
    )(v, after)


class _WholeWeights:
    def __init__(self, w):
        self.w = w

    def weights(self, group, after=None):
        return self.w, None

    def grads_ready(self, group, gw):
        return None

    def grads_sent(self, group, after):
        return None


def _local_step(x, p, target, gains, rel_bias, hooks):
    T, D = x.shape
    S = N_CHIPS

    tied = lambda gain, token: gain if token is None else gain + token[0, 0]
    w, token = hooks.weights("first")
    w = dict(w)
    h1, xn1, g1, u1, a1, f1 = _ffn_fwd(x, tied(gains["ffn1_pre"], token), gains["ffn1_post"], w["ffn1_gate"],
                                       w["ffn1_up"], w["ffn1_down"], "ffn1_fwd")
    more, token = hooks.weights("in", h1)
    w.update(more)
    qkv, un = _norm_proj(h1, tied(gains["mix_pre"], token), w["in"], "qkv_proj")
    bias = _ch_group_bias(_bias_table(rel_bias, "bias_table").transpose(1, 0, 2))
    o_a = _sb_fwd(qkv, "sb_fwd")
    o_b = _ch_fwd(qkv, bias, "ch_fwd")
    w.update(hooks.weights("rest", o_b)[0])
    w_out = w["out"].reshape(D, D)
    h2, mixed, mo = _mix_out_fwd(h1, o_a, o_b, gains["out_sb"], gains["out_ch"], w_out, gains["mix_post"],
                                 "mix_out_fwd")
    h3, xn2, g2, u2, a2, f2 = _ffn_fwd(h2, gains["ffn2_pre"], gains["ffn2_post"], w["ffn2_gate"], w["ffn2_up"],
                                       w["ffn2_down"], "ffn2_fwd")
    w_ple_proj = w["ple_proj"].transpose(1, 0, 2).reshape(p.shape[1], D)
    w_ple_gate = w["ple_gate"].reshape(D, D)

    loss, dh3, dproj, dgate, dg_ple = _ple_loss(h3, p, target, w_ple_proj, w_ple_gate, gains["ple_post"], "ple_loss")
    gw, gg = {}, {"ple_post": dg_ple}
    gw["ple_proj"] = _mm_tn(p[None], dproj, p.shape[1], "dw_ple_proj")
    row_sharded = lambda pair: tuple(o.reshape(S, D // S, D) for o in pair)
    gw["ple_gate"] = row_sharded(_mm_tn(h3[None], dgate[None], 512, "dw_ple_gate"))

    def ffn_bwd(tag, dh, x_in, xn, g_act, u_act, a_act, f, group):
        dgp, dup, df, gg[tag + "_post"] = _ffn_bwd_act(dh, f, gains[tag + "_post"], w[tag + "_down"], g_act, u_act,
                                                       tag + "_bwd_act")
        gw[tag + "_gate"] = _mm_tn(dgp, xn[None], dgp.shape[2], "dw_" + tag + "_gate")
        gw[tag + "_up"] = _mm_tn(dup, xn[None], dup.shape[2], "dw_" + tag + "_up")
        gw[tag + "_down"] = _mm_tn(a_act, df[None], a_act.shape[2], "dw_" + tag + "_down")
        g_pre = gains[tag + "_pre"]
        if group is not None:
            token = hooks.grads_ready(group, gw)
            g_pre = g_pre if token is None else g_pre + token[0, 0]
        dx, gg[tag + "_pre"] = _proj_bwd([dgp, dup], [w[tag + "_gate"], w[tag + "_up"]], x_in, g_pre, dh,
                                         tag + "_bwd_in")
        return dx

    dh2 = ffn_bwd("ffn2", dh3, h2, xn2, g2, u2, a2, f2, None)
    dmo, do_a, do_b, gg["mix_post"], gg["out_sb"], gg["out_ch"] = _mix_out_bwd(
        dh2, mo, gains["mix_post"], w_out, o_a, o_b, gains["out_sb"], gains["out_ch"], "mix_out_bwd")
    gw["out"] = row_sharded(_mm_tn(mixed[None], dmo[None], 512, "dw_out"))
    token = hooks.grads_ready("early", gw)
    if token is not None:
        do_a = do_a + token[0, 0]
    dq_a, dk_a, dv_a = _sb_bwd(qkv, do_a, o_a, "sb_bwd")
    token = hooks.grads_sent("early", dq_a)
    if token is not None:
        do_b = do_b + token[0, 0]
    dq_b, dk_b, dv_b, dbias = _ch_bwd(qkv, bias, do_b, "ch_bwd")
    g_rel = _bias_grad(_ch_fold_bias_grad(dbias).transpose(1, 0, 2), "bias_grad")
    dqkv = jnp.concatenate([dq_a, dk_a, dv_a, dq_b, dk_b, dv_b], axis=1)
    gw["in"] = _mm_tn(un[None], dqkv, 512, "dw_in", groups=S)
    dh1, gg["mix_pre"] = _proj_bwd([dqkv], [w["in"]], h1, gains["mix_pre"], dh2, "qkv_bwd_in")
    dx = ffn_bwd("ffn1", dh1, x, xn1, g1, u1, a1, f1, "late")
    return loss, dx, gw, gg, g_rel


BIG = ["ffn1_gate", "ffn1_up", "ffn1_down", "in", "out", "ffn2_gate", "ffn2_up", "ffn2_down", "ple_proj", "ple_gate"]
GAINS = ["ffn1_pre", "ffn1_post", "mix_pre", "mix_post", "out_sb", "out_ch", "ffn2_pre", "ffn2_post", "ple_post"]
TRANSPOSED = ("w_ffn1_gate", "w_ffn1_up", "w_ffn2_gate", "w_ffn2_up")
PERMUTED = ("ffn1_gate", "ffn1_up", "ffn1_down", "ffn2_gate", "ffn2_up", "ffn2_down")
W_GROUPS = {"first": ["ffn1_gate", "ffn1_up", "ffn1_down"], "in": ["in"],
            "rest": ["out", "ffn2_gate", "ffn2_up", "ffn2_down", "ple_proj", "ple_gate"]}
G_GROUPS = {"early": ["ple_proj", "ple_gate", "ffn2_gate", "ffn2_up", "ffn2_down", "out"],
            "late": ["in", "ffn1_gate", "ffn1_up", "ffn1_down"]}
ORDER = ["g_ffn1_pre", "g_ffn1_post", "w_ffn1_gate", "w_ffn1_up", "w_ffn1_down", "g_mix_pre", "g_mix_post", "w_in",
         "g_out_sb", "g_out_ch", "rel_bias", "w_out", "g_ffn2_pre", "g_ffn2_post", "w_ffn2_gate", "w_ffn2_up",
         "w_ffn2_down", "w_ple_proj", "w_ple_gate", "g_ple_post"]


def kernel(x, p, g_ffn1_pre, g_ffn1_post, w_ffn1_gate, w_ffn1_up, w_ffn1_down, g_mix_pre, g_mix_post, w_in, g_out_sb, g_out_ch, rel_bias, w_out, g_ffn2_pre, g_ffn2_post, w_ffn2_gate, w_ffn2_up, w_ffn2_down, w_ple_proj, w_ple_gate, g_ple_post, loss_target, m_g_ffn1_pre, m_g_ffn1_post, m_w_ffn1_gate, m_w_ffn1_up, m_w_ffn1_down, m_g_mix_pre, m_g_mix_post, m_w_in, m_g_out_sb, m_g_out_ch, m_rel_bias, m_w_out, m_g_ffn2_pre, m_g_ffn2_post, m_w_ffn2_gate, m_w_ffn2_up, m_w_ffn2_down, m_w_ple_proj, m_w_ple_gate, m_g_ple_post, v_g_ffn1_pre, v_g_ffn1_post, v_w_ffn1_gate, v_w_ffn1_up, v_w_ffn1_down, v_g_mix_pre, v_g_mix_post, v_w_in, v_g_out_sb, v_g_out_ch, v_rel_bias, v_w_out, v_g_ffn2_pre, v_g_ffn2_post, v_w_ffn2_gate, v_w_ffn2_up, v_w_ffn2_down, v_w_ple_proj, v_w_ple_gate, v_g_ple_post):
    args = dict(locals())
    take = lambda a, n: a[0].T if n in TRANSPOSED else a[0]
    wts = {n: take(args[n], n) for n in ORDER}
    ms = {n: take(args["m_" + n], n) for n in ORDER}
    vs = {n: take(args["v_" + n], n) for n in ORDER}
    gains = {n: wts["g_" + n][None] for n in GAINS}

    c_idx = lax.axis_index("c").astype(jnp.int32).reshape(1)
    me_idx = (2 * lax.axis_index("x") + lax.axis_index("y")).astype(jnp.int32).reshape(1)
    south = lax.axis_index("c") == 0

    lands = dict(zip(BIG, _cast_into_slot0(c_idx, [wts["w_" + n] for n in BIG], "cast_weights")))

    def in_order(names, zones):
        plain = [n for n in names if n not in PERMUTED]
        fixed = dict(zip(plain, _chip_order(me_idx, c_idx, [zones[n] for n in plain], "chip_order_" + plain[0]))
                     ) if plain else {}
        return {n: fixed[n] if n in fixed else _slabs(zones[n]) for n in names}

    class Overlapped:
        def __init__(self):
            self.started = {}
            self.flying = {}

        def start(self, group, collective_id, after):
            self.flying[group] = _gather_start([lands[n] for n in W_GROUPS[group]], "gather_%s_start" % group,
                                               collective_id, after)
            return self.flying[group][3]

        def weights(self, group, after=None):
            names = W_GROUPS[group]
            if group == "first":
                zones = _gather_finish([lands[n] for n in names], True, "gather_first")
                token = self.start("rest", 4, self.start("in", 1, zones[0]))
                return in_order(names, dict(zip(names, zones))), token
            send_sem, recv_sem, zones, _ = self.flying[group]
            zones = _gather_wait(send_sem, recv_sem, zones, after, "gather_%s_wait" % group)
            zones = _gather_finish(zones, False, "gather_%s_finish" % group)
            return in_order(names, dict(zip(names, zones))), None

        def grads_ready(self, group, gw):
            names = G_GROUPS[group]
            perm = [n in PERMUTED for n in names]
            halved = lambda g: g.reshape(N_CHIPS, 2, g.shape[1] // 2, g.shape[2])
            mine = [halved(gw[n][0]) for n in names]
            narrow = [halved(gw[n][1]) for n in names]
            if group == "late":
                return self.scatter(group, names, perm, mine, _pair_swap(narrow, perm, "grad_pair_swap_late"))
            self.swapping = names, perm, mine, _pair_swap_start(narrow, perm, "grad_pair_swap_start_early", 5)
            return self.swapping[3][3]

        def grads_sent(self, group, after):
            names, perm, mine, (send_sem, recv_sem, operands, _) = self.swapping
            operands = _pair_swap_wait(send_sem, recv_sem, operands, perm, after, "grad_pair_swap_wait_early")
            return self.scatter(group, names, perm, mine, operands[len(names):])

        def scatter(self, group, names, perm, mine, got):
            partial = _pair_add(c_idx, mine, got, perm, "grad_pair_add_" + group)
            send_sem, recv_sem, operands, token = _scatter_start(partial, perm, "grad_scatter_start_" + group,
                                                                 {"early": 2, "late": 3}[group])
            self.started[group] = names, perm, send_sem, recv_sem, operands, token
            return token

    def reduce_finish(state, after, tag):
        names, perm, send_sem, recv_sem, operands, _ = state
        operands = _scatter_wait(send_sem, recv_sem, operands, perm, after, "grad_scatter_wait_" + tag)
        n = len(names)
        return _chip_add(me_idx, operands[:n], operands[n:], perm, "grad_chip_add_" + tag)

    hooks = Overlapped()
    loss, dx, gw, gg, g_rel = _local_step(x[0], p[0, 0], loss_target[0], gains, wts["rel_bias"], hooks)

    grads, delta, new_m, new_v = {}, {}, {}, {}

    def finish(group, after):
        own = reduce_finish(hooks.started[group], after, group)
        names = ["w_" + n for n in G_GROUPS[group]]
        state = [wts[n] for n in names], [ms[n] for n in names], [vs[n] for n in names]
        send_sem, recv_sem, operands, token = _pair_join_start(own, "grad_pair_join_start_" + group,
                                                               {"early": 6, "late": 7}[group])
        half = _adamw_half(c_idx, state[0], own, state[1], state[2], None, token, "adamw_own_" + group)
        other = _pair_join_wait(send_sem, recv_sem, operands, half[1][0], "grad_pair_join_wait_" + group)[len(own):]
        g, d, m, v = _adamw_half(1 - c_idx, state[0], other, state[1], state[2], half, None, "adamw_other_" + group)
        for n, gg_, dd, mm, vv in zip(names, g, d, m, v):
            grads[n], delta[n], new_m[n], new_v[n] = gg_, dd, mm, vv
        return d[0]

    finish("late", finish("early", dx))

    pieces = [gg[n].reshape(-1, 128) for n in GAINS] + [jnp.pad(g_rel, ((0, 0), (0, N_REL_PAD - N_REL))).reshape(-1, 128)]
    summed = _all_sum_small(jnp.concatenate(pieces + [loss], axis=0), delta["w_in"], "small_grad_sum")
    at = 0
    for n, piece in zip(GAINS, pieces[:-1]):
        grads["g_" + n] = summed[at:at + piece.shape[0]].reshape(1, -1)[0]
        at += piece.shape[0]
    grads["rel_bias"] = summed[at:at + pieces[-1].shape[0]].reshape(N_HEADS, N_REL_PAD)[:, :N_REL]
    loss = summed[at + pieces[-1].shape[0], 0]

    small = ["g_" + n for n in GAINS] + ["rel_bias"]
    as_rows = lambda a: (a.reshape(-1, 128) if a.size % 128 == 0 else jnp.pad(a, ((0, 0), (0, N_REL_PAD - N_REL))).reshape(-1, 128))
    d, m, v = _adamw([as_rows(wts[n]) for n in small], [as_rows(grads[n]) for n in small],
                     [as_rows(ms[n]) for n in small], [as_rows(vs[n]) for n in small], 1, "adamw_small")
    for n, dd, mm, vv in zip(small, d, m, v):
        back = (lambda a: a.reshape(N_HEADS, N_REL_PAD)[:, :N_REL]) if n == "rel_bias" else (lambda a: a.reshape(-1))
        delta[n], new_m[n], new_v[n] = back(dd), back(mm), back(vv)

    outs = [loss, dx[None]]
    for table in (grads, delta, new_m, new_v):
        outs += [(table[n].T if n in TRANSPOSED else table[n])[None] for n in ORDER]
    return tuple(outs)
```

```python
import functools

import jax
import jax.numpy as jnp
from jax import lax
from jax.experimental import pallas as pl
from jax.experimental.pallas import tpu as pltpu

F32 = jnp.float32
BF16 = jnp.bfloat16
EPS = 1e-6
N_CHIPS = 4
HEAD_DIM = 64
N_HEADS = 8
CHUNK = 64
LOOKBACK = 8
BAND = (LOOKBACK + 1) * CHUNK
PAD = LOOKBACK * CHUNK
REL_CLIP = 128
N_REL = 2 * REL_CLIP + 1
N_REL_PAD = 384
SB_BLOCK = 256
PAIR = 2 * HEAD_DIM
SB_PAIRS = 2
ATT_SCALE = HEAD_DIM ** -0.5
NEG_INF = -1e30
ROW_BLOCK = 512
WIDE_ROW_BLOCK = 1024
VMEM_LIMIT_WIDE = 56 * 1024 * 1024
VMEM_LIMIT = 48 * 1024 * 1024
MESH = pl.DeviceIdType.MESH

ADAM_LR = 0.001
ADAM_B1 = 0.9
ADAM_B2 = 0.999
ADAM_EPS = 1e-08
ADAM_WD = 0.01
ADAM_STEP = 10

NT = (((1,), (1,)), ((), ()))
TN = (((0,), (0,)), ((), ()))


def _params(n_grid, vmem=None):
    return pltpu.CompilerParams(dimension_semantics=("arbitrary",) * n_grid, vmem_limit_bytes=vmem)


def _hbm(*arrays):
    return [pltpu.with_memory_space_constraint(a, pltpu.HBM) for a in arrays]


def _out(shape, dtype):
    return pltpu.HBM(shape, dtype)


def _dot(a, b, dims=None):
    if dims is None:
        return jnp.dot(a, b, preferred_element_type=F32)
    return lax.dot_general(a, b, dims, preferred_element_type=F32)


def _sigmoid(x):
    return 1.0 / (1.0 + jnp.exp(-x))


def _rms_fwd(x, g):
    r = lax.rsqrt(jnp.mean(x * x, axis=-1, keepdims=True) + EPS)
    return x * r * g


def _rms_bwd(x, g, dy):
    r = lax.rsqrt(jnp.mean(x * x, axis=-1, keepdims=True) + EPS)
    xh = x * r
    dg = jnp.sum(dy * xh, axis=0, keepdims=True)
    t = dy * g
    dx = r * (t - xh * jnp.mean(t * xh, axis=-1, keepdims=True))
    return dx, dg


def _accumulate(ref, val, first):
    @pl.when(first)
    def _():
        ref[...] = val

    @pl.when(jnp.logical_not(first))
    def _():
        ref[...] += val


def _split2(x):
    hi = x.astype(BF16)
    lo = (x - hi.astype(F32)).astype(BF16)
    return hi, lo


def _ffn_fwd(x, g_pre, g_post, wg, wu, wd, name):
    T, D = x.shape
    S, FS, _ = wg.shape
    tm = min(ROW_BLOCK, T)

    def body(x_ref, gpre_ref, gpost_ref, wg_ref, wu_ref, wd_ref,
             h_ref, xn_ref, g_ref, u_ref, a_ref, f_ref, xn_s, acc_s):
        k = pl.program_id(1)

        @pl.when(k == 0)
        def _():
            xn_s[...] = _rms_fwd(x_ref[...], gpre_ref[...]).astype(BF16)
            xn_ref[...] = xn_s[...]

        xn = xn_s[...]
        g = _dot(xn, wg_ref[0], NT)
        u = _dot(xn, wu_ref[0], NT)
        g_ref[0] = g
        u_ref[0] = u
        a = (g * _sigmoid(g) * u).astype(BF16)
        a_ref[0] = a
        _accumulate(acc_s, _dot(a, wd_ref[0]), k == 0)

        @pl.when(k == S - 1)
        def _():
            f = acc_s[...]
            f_ref[...] = f
            h_ref[...] = x_ref[...] + 0.5 * _rms_fwd(f, gpost_ref[...])

    row = pl.BlockSpec((tm, D), lambda i, k: (i, 0))
    vec = pl.BlockSpec((1, D), lambda i, k: (0, 0))
    act = pl.BlockSpec((1, tm, FS), lambda i, k: (k, i, 0))
    return pl.pallas_call(
        body, name=name, grid=(T // tm, S),
        in_specs=[row, vec, vec] + [pl.BlockSpec((1, FS, D), lambda i, k: (k, 0, 0))] * 3,
        out_specs=[row, row, act, act, act, row],
        out_shape=[_out((T, D), F32), _out((T, D), BF16),
                   _out((S, T, FS), F32), _out((S, T, FS), F32),
                   _out((S, T, FS), BF16), _out((T, D), F32)],
        scratch_shapes=[pltpu.VMEM((tm, D), BF16), pltpu.VMEM((tm, D), F32)],
        compiler_params=_params(2, VMEM_LIMIT),
    )(*_hbm(x, g_pre, g_post, wg, wu, wd))


def _ffn_bwd_act(dh, f, g_post, wd, g_act, u_act, name):
    T, D = dh.shape
    S, FS, _ = wd.shape
    tm = min(WIDE_ROW_BLOCK, T)

    def body(dh_ref, f_ref, gpost_ref, wd_ref, g_ref, u_ref, dgp_ref, dup_ref, df_ref, dgain_ref, df_s):
        i, k = pl.program_id(0), pl.program_id(1)

        @pl.when(k == 0)
        def _():
            df, dgain = _rms_bwd(f_ref[...], gpost_ref[...], 0.5 * dh_ref[...])
            df_s[...] = df.astype(BF16)
            df_ref[...] = df_s[...]
            _accumulate(dgain_ref, dgain, i == 0)

        da = _dot(df_s[...], wd_ref[0], NT)
        g = g_ref[0]
        s = _sigmoid(g)
        dup_ref[0] = (da * (g * s)).astype(BF16)
        dgp_ref[0] = (da * u_ref[0] * (s * (1.0 + g * (1.0 - s)))).astype(BF16)

    row = pl.BlockSpec((tm, D), lambda i, k: (i, 0))
    vec = pl.BlockSpec((1, D), lambda i, k: (0, 0))
    act = pl.BlockSpec((1, tm, FS), lambda i, k: (k, i, 0))
    return pl.pallas_call(
        body, name=name, grid=(T // tm, S),
        in_specs=[row, row, vec, pl.BlockSpec((1, FS, D), lambda i, k: (k, 0, 0)), act, act],
        out_specs=[act, act, row, vec],
        out_shape=[_out((S, T, FS), BF16), _out((S, T, FS), BF16),
                   _out((T, D), BF16), _out((1, D), F32)],
        scratch_shapes=[pltpu.VMEM((tm, D), BF16)],
        compiler_params=_params(2, VMEM_LIMIT_WIDE),
    )(*_hbm(dh, f, g_post, wd, g_act, u_act))


def _proj_bwd(dys, ws, x, g_pre, dh, name):
    T, D = x.shape
    n = len(dys)
    flat = dys[0].ndim == 2
    S = ws[0].shape[0]
    N = ws[0].shape[2] if flat else ws[0].shape[1]
    tm = min(WIDE_ROW_BLOCK, T)

    def body(*refs):
        dy_refs, w_refs = refs[:n], refs[n:2 * n]
        x_ref, gpre_ref, dh_ref, dx_ref, dgain_ref, acc_s = refs[2 * n:]
        i, k = pl.program_id(0), pl.program_id(1)
        part = None
        for dy_ref, w_ref in zip(dy_refs, w_refs):
            term = _dot(dy_ref[...], w_ref[0], NT) if flat else _dot(dy_ref[0], w_ref[0])
            part = term if part is None else part + term
        _accumulate(acc_s, part, k == 0)

        @pl.when(k == S - 1)
        def _():
            dx, dgain = _rms_bwd(x_ref[...], gpre_ref[...], acc_s[...])
            dx_ref[...] = dh_ref[...] + dx
            _accumulate(dgain_ref, dgain, i == 0)

    row = pl.BlockSpec((tm, D), lambda i, k: (i, 0))
    vec = pl.BlockSpec((1, D), lambda i, k: (0, 0))
    return pl.pallas_call(
        body, name=name, grid=(T // tm, S),
        in_specs=[pl.BlockSpec((tm, N), lambda i, k: (i, k)) if flat else pl.BlockSpec((1, tm, N), lambda i, k: (k, i, 0))] * n
        + [pl.BlockSpec((1,) + ws[0].shape[1:], lambda i, k: (k, 0, 0))] * n + [row, vec, row],
        out_specs=[row, vec],
        out_shape=[_out((T, D), F32), _out((1, D), F32)],
        scratch_shapes=[pltpu.VMEM((tm, D), F32)],
        compiler_params=_params(2, VMEM_LIMIT_WIDE),
    )(*_hbm(*dys, *ws, x, g_pre, dh))


def _mm_tn(a, b, bm, name, groups=None):
    ga, T, M = a.shape
    if groups is None:
        gb, _, N = b.shape
        b_spec = pl.BlockSpec((1, T, N), (lambda g, m: (g, 0, 0)) if gb > 1 else (lambda g, m: (0, 0, 0)))
    else:
        gb, N = groups, b.shape[1] // groups
        b_spec = pl.BlockSpec((T, N), lambda g, m: (0, g))
    G = max(ga, gb)

    def body(a_ref, b_ref, o_ref, narrow_ref):
        bv = b_ref[0] if groups is None else b_ref[...]
        o_ref[0] = _dot(a_ref[0].astype(BF16), bv.astype(BF16), TN)
        narrow_ref[0] = o_ref[0].astype(BF16)

    out = pl.BlockSpec((1, bm, N), lambda g, m: (g, m, 0))
    return pl.pallas_call(
        body, name=name, grid=(G, M // bm),
        in_specs=[pl.BlockSpec((1, T, bm), (lambda g, m: (g, 0, m)) if ga > 1 else (lambda g, m: (0, 0, m))), b_spec],
        out_specs=[out, out],
        out_shape=[_out((G, M, N), F32), _out((G, M, N), BF16)],
        compiler_params=_params(2, VMEM_LIMIT),
    )(*_hbm(a, b))


def _norm_proj(x, g_pre, w, name):
    T, D = x.shape
    S, _, N = w.shape
    tm = min(WIDE_ROW_BLOCK, T)

    def body(x_ref, g_ref, w_ref, o_ref, xn_ref, xn_s):
        @pl.when(pl.program_id(1) == 0)
        def _():
            xn_s[...] = _rms_fwd(x_ref[...], g_ref[...]).astype(BF16)
            xn_ref[...] = xn_s[...]

        o_ref[...] = _dot(xn_s[...], w_ref[0]).astype(BF16)

    row = pl.BlockSpec((tm, D), lambda i, k: (i, 0))
    return pl.pallas_call(
        body, name=name, grid=(T // tm, S),
        in_specs=[row, pl.BlockSpec((1, D), lambda i, k: (0, 0)), pl.BlockSpec((1, D, N), lambda i, k: (k, 0, 0))],
        out_specs=[pl.BlockSpec((tm, N), lambda i, k: (i, k)), row],
        out_shape=[_out((T, S * N), BF16), _out((T, D), BF16)],
        scratch_shapes=[pltpu.VMEM((tm, D), BF16)],
        compiler_params=_params(2, VMEM_LIMIT_WIDE),
    )(*_hbm(x, g_pre, w))


def _mix_out_fwd(h, o_a, o_b, g_sb, g_ch, w_out, g_post, name):
    T, D = h.shape
    W = g_sb.shape[1]
    tm = min(ROW_BLOCK, T)

    def body(h_ref, oa_ref, ob_ref, gsb_ref, gch_ref, w_ref, gpost_ref, h2_ref, mixed_ref, mo_ref):
        mixed_ref[:, :W] = _rms_fwd(oa_ref[...], gsb_ref[...]).astype(BF16)
        mixed_ref[:, W:] = _rms_fwd(ob_ref[...], gch_ref[...]).astype(BF16)
        mo = _dot(mixed_ref[...], w_ref[...])
        mo_ref[...] = mo
        h2_ref[...] = h_ref[...] + _rms_fwd(mo, gpost_ref[...])

    row = pl.BlockSpec((tm, D), lambda i: (i, 0))
    part = pl.BlockSpec((tm, W), lambda i: (i, 0))
    half = pl.BlockSpec((1, W), lambda i: (0, 0))
    return pl.pallas_call(
        body, name=name, grid=(T // tm,),
        in_specs=[row, part, part, half, half, pl.BlockSpec((D, D), lambda i: (0, 0)), pl.BlockSpec((1, D), lambda i: (0, 0))],
        out_specs=[row, row, row],
        out_shape=[_out((T, D), F32), _out((T, D), BF16),
                   _out((T, D), F32)],
        compiler_params=_params(1, VMEM_LIMIT),
    )(*_hbm(h, o_a, o_b, g_sb, g_ch, w_out, g_post))


def _mix_out_bwd(dh, mo, g_post, w_out, o_a, o_b, g_sb, g_ch, name):
    T, D = dh.shape
    W = g_sb.shape[1]
    tm = min(ROW_BLOCK, T)

    def body(dh_ref, mo_ref, gpost_ref, w_ref, oa_ref, ob_ref, gsb_ref, gch_ref,
             dmo_ref, doa_ref, dob_ref, dgpost_ref, dgsb_ref, dgch_ref):
        first = pl.program_id(0) == 0
        dmo, dgpost = _rms_bwd(mo_ref[...], gpost_ref[...], dh_ref[...])
        dmo_ref[...] = dmo.astype(BF16)
        dmix = _dot(dmo_ref[...], w_ref[...], NT)
        doa_ref[...], dgsb = _rms_bwd(oa_ref[...], gsb_ref[...], dmix[:, :W])
        dob_ref[...], dgch = _rms_bwd(ob_ref[...], gch_ref[...], dmix[:, W:])
        _accumulate(dgpost_ref, dgpost, first)
        _accumulate(dgsb_ref, dgsb, first)
        _accumulate(dgch_ref, dgch, first)

    row = pl.BlockSpec((tm, D), lambda i: (i, 0))
    part = pl.BlockSpec((tm, W), lambda i: (i, 0))
    vec = pl.BlockSpec((1, D), lambda i: (0, 0))
    half = pl.BlockSpec((1, W), lambda i: (0, 0))
    return pl.pallas_call(
        body, name=name, grid=(T // tm,),
        in_specs=[row, row, vec, pl.BlockSpec((D, D), lambda i: (0, 0)), part, part, half, half],
        out_specs=[row, part, part, vec, half, half],
        out_shape=[_out((T, D), BF16), _out((T, W), F32),
                   _out((T, W), F32), _out((1, D), F32),
                   _out((1, W), F32), _out((1, W), F32)],
        compiler_params=_params(1, VMEM_LIMIT),
    )(*_hbm(dh, mo, g_post, w_out, o_a, o_b, g_sb, g_ch))


def _ple_loss(h, p, target, w_proj, w_gate, g_post, name):
    T, D = h.shape
    P = p.shape[1]
    S = N_CHIPS
    C = D // S
    tm = min(ROW_BLOCK, T)

    def body(h_ref, p_ref, t_ref, wp_ref, wg_ref, g_ref, loss_ref, dh_ref, dproj_ref, dgate_ref, dgain_ref):
        first = pl.program_id(0) == 0
        h3 = h_ref[...]
        proj = _dot(p_ref[...].astype(BF16), wp_ref[...])
        s = _sigmoid(_dot(h3.astype(BF16), wg_ref[...]))
        e = proj * s
        diff = h3 + _rms_fwd(e, g_ref[...]) - t_ref[...]
        part = 0.5 * jnp.sum(jnp.mean(diff * diff, axis=-1, keepdims=True), axis=0, keepdims=True)
        _accumulate(loss_ref, jnp.broadcast_to(part, loss_ref.shape), first)
        dy = diff * (1.0 / D)
        de, dgain = _rms_bwd(e, g_ref[...], dy)
        _accumulate(dgain_ref, dgain, first)
        dproj = (de * s).astype(BF16)
        for j in range(S):
            dproj_ref[j] = dproj[:, j * C:(j + 1) * C]
        dgate_ref[...] = (de * proj * s * (1.0 - s)).astype(BF16)
        dh_ref[...] = dy + _dot(dgate_ref[...], wg_ref[...], NT)

    row = pl.BlockSpec((tm, D), lambda i: (i, 0))
    vec = pl.BlockSpec((1, D), lambda i: (0, 0))
    return pl.pallas_call(
        body, name=name, grid=(T // tm,),
        in_specs=[row, pl.BlockSpec((tm, P), lambda i: (i, 0)), row,
                  pl.BlockSpec((P, D), lambda i: (0, 0)), pl.BlockSpec((D, D), lambda i: (0, 0)), vec],
        out_specs=[pl.BlockSpec((8, 128), lambda i: (0, 0)), row,
                   pl.BlockSpec((S, tm, C), lambda i: (0, i, 0)), row, vec],
        out_shape=[_out((8, 128), F32), _out((T, D), F32),
                   _out((S, T, C), BF16), _out((T, D), BF16),
                   _out((1, D), F32)],
        compiler_params=_params(1, VMEM_LIMIT),
    )(*_hbm(h, p, target, w_proj, w_gate, g_post))


def _sb_scores(q, kj, mask):
    z = _dot(q, kj, NT)
    sp = jnp.maximum(z, 0.0) + jnp.log(1.0 + jnp.exp(-jnp.abs(z)))
    return z, sp if mask is None else jnp.where(mask, sp, 0.0)


def _strict_causal():
    rows = lax.broadcasted_iota(jnp.int32, (SB_BLOCK, SB_BLOCK), 0)
    cols = lax.broadcasted_iota(jnp.int32, (SB_BLOCK, SB_BLOCK), 1)
    return cols < rows


def _tri(cmp):
    r = lax.broadcasted_iota(jnp.int32, (2 * SB_BLOCK, SB_BLOCK), 0) % SB_BLOCK
    c = lax.broadcasted_iota(jnp.int32, (2 * SB_BLOCK, SB_BLOCK), 1)
    return jnp.where(cmp(r, c), 1.0, 0.0).astype(BF16)


def _cum(x, tri):
    return _dot(jnp.concatenate(_split2(x), axis=1), tri)


def _pair_lanes():
    lane = lax.broadcasted_iota(jnp.int32, (1, PAIR), 1)
    return [lane < HEAD_DIM, lane >= HEAD_DIM]


def _only(lanes, x):
    return jnp.where(lanes, x, jnp.zeros_like(x))


def _sb_fwd(qkv, name):
    T = qkv.shape[0]
    B = SB_BLOCK
    W = SB_PAIRS * PAIR
    steps = N_HEADS // (2 * SB_PAIRS)
    heads = [(p, h) for p in range(SB_PAIRS) for h in range(2)]

    def body(q_ref, k_ref, v_ref, o_ref):
        i = pl.program_id(1)
        after = _tri(lambda r, c: r > c)
        lanes = _pair_lanes()
        cols = [slice(p * PAIR, (p + 1) * PAIR) for p in range(SB_PAIRS)]
        q = {(p, h): _only(lanes[h], q_ref[:, cols[p]] * ATT_SCALE) for p, h in heads}

        def tiles(j, carries, mask):
            at = pl.ds(pl.multiple_of(j * B, B), B)
            scores = [_sb_scores(q[ph], k_ref[at, cols[ph[0]]], mask) for ph in heads]
            laters = [_cum(sp, after) for _, sp in scores]
            out = []
            for ph, (z, sp), later, (run, acc) in zip(heads, scores, laters, carries):
                a = jnp.exp(z - sp - later - run)
                if mask is not None:
                    a = jnp.where(mask, a, 0.0)
                out.append((run + later[:, 0:1] + sp[:, 0:1],
                            acc + _dot(a.astype(BF16), _only(lanes[ph[1]], v_ref[at, cols[ph[0]]]))))
            return tuple(out)

        zero = (jnp.zeros((B, 1), F32), jnp.zeros((B, PAIR), F32))
        carries = tiles(i, (zero,) * len(heads), _strict_causal())
        carries = lax.fori_loop(0, i, lambda jj, cs: tiles(i - 1 - jj, cs, None), carries)
        for p in range(SB_PAIRS):
            o_ref[:, cols[p]] = carries[2 * p][1] + carries[2 * p + 1][1]

    blk = lambda off: pl.BlockSpec((B, W), lambda g, i: (i, g + off))
    full = lambda off: pl.BlockSpec((T, W), lambda g, i: (0, g + off))
    return pl.pallas_call(
        body, name=name, grid=(steps, T // B),
        in_specs=[blk(0), full(steps), full(2 * steps)],
        out_specs=blk(0),
        out_shape=_out((T, N_HEADS * HEAD_DIM), F32),
        compiler_params=_params(2, VMEM_LIMIT),
    )(*_hbm(qkv, qkv, qkv))


def _sb_bwd(qkv, do, o, name):
    T = qkv.shape[0]
    B = SB_BLOCK
    W = SB_PAIRS * PAIR
    steps = N_HEADS // (2 * SB_PAIRS)
    n_blocks = T // B
    heads = [(p, h) for p in range(SB_PAIRS) for h in range(2)]

    def body(q_ref, k_ref, v_ref, do_ref, o_ref, dq_ref, dk_ref, dv_ref, dk_s, dv_s):
        i = pl.program_id(1)

        @pl.when(i == 0)
        def _():
            dk_s[...] = jnp.zeros_like(dk_s)
            dv_s[...] = jnp.zeros_like(dv_s)

        after = _tri(lambda r, c: r > c)
        since = _tri(lambda r, c: r >= c)
        lanes = _pair_lanes()
        cols = [slice(p * PAIR, (p + 1) * PAIR) for p in range(SB_PAIRS)]
        q = {(p, h): _only(lanes[h], q_ref[:, cols[p]] * ATT_SCALE) for p, h in heads}
        do = {(p, h): _only(lanes[h], do_ref[:, cols[p]].astype(BF16)) for p, h in heads}
        total = {ph: jnp.sum(do[ph].astype(F32) * o_ref[:, cols[ph[0]]], axis=1, keepdims=True) for ph in heads}

        def tiles(j, carries, mask):
            at = pl.ds(pl.multiple_of(j * B, B), B)
            ks = [k_ref[at, c] for c in cols]
            vs = [v_ref[at, c] for c in cols]
            scores = [_sb_scores(q[ph], ks[ph[0]], mask) for ph in heads]
            laters = [_cum(sp, after) for _, sp in scores]
            das = [_dot(do[ph], vs[ph[0]], NT) for ph in heads]
            a_s, gs = [], []
            for (z, sp), later, da, carry in zip(scores, laters, das, carries):
                a = jnp.exp(z - sp - later - carry[0])
                if mask is not None:
                    a = jnp.where(mask, a, 0.0)
                a = a.astype(BF16)
                a_s.append(a)
                gs.append(a.astype(F32) * da)
            sinces = [_cum(g, since) for g in gs]
            dzs = []
            for ph, (_, sp), g, from_s, carry in zip(heads, scores, gs, sinces, carries):
                g_before = total[ph] - carry[1] - from_s
                fail = jnp.exp(-sp)
                dz = fail * (g + g_before) - g_before
                if mask is not None:
                    dz = jnp.where(mask, dz, 0.0)
                dzs.append(dz.astype(BF16))
            out = []
            for ph, (_, sp), a, dz, later, from_s, carry in zip(heads, scores, a_s, dzs, laters, sinces, carries):
                dk_s[at, cols[ph[0]]] += _dot(dz, q[ph], TN)
                dv_s[at, cols[ph[0]]] += _dot(a, do[ph], TN)
                out.append((carry[0] + later[:, 0:1] + sp[:, 0:1], carry[1] + from_s[:, 0:1],
                            carry[2] + _dot(dz, _only(lanes[ph[1]], ks[ph[0]]))))
            return tuple(out)

        col = jnp.zeros((B, 1), F32)
        zero = (col, col, jnp.zeros((B, PAIR), F32))
        carries = tiles(i, (zero,) * len(heads), _strict_causal())
        last = lax.fori_loop(0, i, lambda jj, cs: tiles(i - 1 - jj, cs, None), carries)
        for p in range(SB_PAIRS):
            dq_ref[:, cols[p]] = ((last[2 * p][2] + last[2 * p + 1][2]) * ATT_SCALE).astype(BF16)

        @pl.when(i == n_blocks - 1)
        def _():
            dk_ref[...] = dk_s[...].astype(BF16)
            dv_ref[...] = dv_s[...].astype(BF16)

    blk = lambda off: pl.BlockSpec((B, W), lambda g, i: (i, g + off))
    full = lambda off: pl.BlockSpec((T, W), lambda g, i: (0, g + off))
    out = _out((T, N_HEADS * HEAD_DIM), BF16)
    return pl.pallas_call(
        body, name=name, grid=(steps, n_blocks),
        in_specs=[blk(0), full(steps), full(2 * steps), blk(0), blk(0)],
        out_specs=[blk(0), full(0), full(0)],
        out_shape=[out, out, out],
        scratch_shapes=[pltpu.VMEM((T, W), F32)] * 2,
        compiler_params=_params(2, VMEM_LIMIT),
    )(*_hbm(qkv, qkv, qkv, do, o))


NEAR = BAND - PAD + REL_CLIP
FAR = BAND - NEAR
NEAR_REL = 2 * REL_CLIP
BIAS_ROWS = 8


def _rel_onehot(i, transposed):
    shape = (NEAR, NEAR_REL) if transposed else (NEAR_REL, NEAR)
    j = FAR + lax.broadcasted_iota(jnp.int32, shape, 0 if transposed else 1)
    r = lax.broadcasted_iota(jnp.int32, shape, 1 if transposed else 0)
    idx = jnp.clip(i + PAD - j, -REL_CLIP, REL_CLIP) + REL_CLIP
    return jnp.where(idx - 1 == r, 1.0, 0.0).astype(BF16)


def _bias_table(rel_bias, name):
    def body(near_ref, far_ref, o_ref):
        rb = near_ref[...]
        hi, lo = _split2(rb)
        lo2 = (rb - hi.astype(F32) - lo.astype(F32)).astype(BF16)
        far = jnp.broadcast_to(far_ref[...], (N_HEADS, FAR))
        for k in range(BIAS_ROWS):
            onehot = _rel_onehot(pl.program_id(0) * BIAS_ROWS + k, False)
            o_ref[k, :, :FAR] = far
            o_ref[k, :, FAR:] = _dot(hi, onehot) + _dot(lo, onehot) + _dot(lo2, onehot)

    return pl.pallas_call(
        body, name=name, grid=(CHUNK // BIAS_ROWS,),
        in_specs=[pl.BlockSpec((N_HEADS, NEAR_REL), lambda i: (0, 0)), pl.BlockSpec((N_HEADS, 1), lambda i: (0, 0))],
        out_specs=pl.BlockSpec((BIAS_ROWS, N_HEADS, BAND), lambda i: (i, 0, 0)),
        out_shape=_out((CHUNK, N_HEADS, BAND), F32),
        compiler_params=_params(1),
    )(*_hbm(rel_bias[:, 1:], rel_bias[:, N_REL - 1:]))


def _bias_grad(dbias_t, name):
    def body(d_ref, near_ref, far_ref):
        near, far = None, None
        for k in range(BIAS_ROWS):
            onehot = _rel_onehot(pl.program_id(0) * BIAS_ROWS + k, True)
            hi, lo = _split2(d_ref[k, :, FAR:])
            part = _dot(hi, onehot) + _dot(lo, onehot)
            rest = jnp.sum(d_ref[k, :, :FAR], axis=1, keepdims=True)
            near, far = (part, rest) if near is None else (near + part, far + rest)
        first = pl.program_id(0) == 0
        _accumulate(near_ref, near, first)
        _accumulate(far_ref, jnp.broadcast_to(far, far_ref.shape), first)

    near, far = pl.pallas_call(
        body, name=name, grid=(CHUNK // BIAS_ROWS,),
        in_specs=[pl.BlockSpec((BIAS_ROWS, N_HEADS, BAND), lambda i: (i, 0, 0))],
        out_specs=[pl.BlockSpec((N_HEADS, NEAR_REL), lambda i: (0, 0)), pl.BlockSpec((N_HEADS, 128), lambda i: (0, 0))],
        out_shape=[_out((N_HEADS, NEAR_REL), F32), _out((N_HEADS, 128), F32)],
        compiler_params=_params(1),
    )(*_hbm(dbias_t))
    return jnp.pad(near, ((0, 0), (1, 0))).at[:, N_REL - 1].add(far[:, 0])


def _ch_probs(scores, bias, valid):
    z = jnp.where(valid, scores * ATT_SCALE + bias, NEG_INF)
    e = jnp.exp(z - jnp.max(z, axis=-1, keepdims=True))
    return e / jnp.sum(e, axis=-1, keepdims=True)


CH_HEADS = [(pair, h) for pair in range(N_HEADS // 2) for h in range(2)]
CH_COLS = [slice(pair * PAIR, (pair + 1) * PAIR) for pair in range(N_HEADS // 2)]


CH_GROUP = 2
CH_Q = CH_GROUP * CHUNK
CH_WIN = (LOOKBACK + CH_GROUP) * CHUNK


def _ch_valid(n):
    row_chunk = lax.broadcasted_iota(jnp.int32, (CH_Q, CH_WIN), 0) // CHUNK
    slot = lax.broadcasted_iota(jnp.int32, (CH_Q, CH_WIN), 1)
    ahead = slot // CHUNK - row_chunk
    return (ahead >= 0) & (ahead <= LOOKBACK) & (n * CH_Q + slot >= PAD)


def _ch_group_bias(bias):
    shifted = [jnp.pad(bias, ((0, 0), (0, 0), (c * CHUNK, (CH_GROUP - 1 - c) * CHUNK))) for c in range(CH_GROUP)]
    return jnp.concatenate(shifted, axis=1)


def _ch_fold_bias_grad(dbias):
    parts = [dbias[:, c * CHUNK:(c + 1) * CHUNK, c * CHUNK:c * CHUNK + BAND] for c in range(CH_GROUP)]
    return sum(parts[1:], parts[0])


def _ch_fwd(qkv, bias, name):
    T = qkv.shape[0]
    W = N_HEADS * HEAD_DIM

    def body(q_ref, k_ref, v_ref, b_ref, o_ref, kp, vp):
        n = pl.program_id(0)

        @pl.when(n == 0)
        def _():
            _ch_load_padded(k_ref, v_ref, kp, vp)

        win = pl.ds(pl.multiple_of(n * CH_Q, CH_Q), CH_WIN)
        valid = _ch_valid(n)
        lanes = _pair_lanes()
        scores = [_dot(_only(lanes[h], q_ref[:, CH_COLS[pair]]), kp[win, CH_COLS[pair]], NT) for pair, h in CH_HEADS]
        probs = [_ch_probs(s, b_ref[2 * pair + h], valid).astype(BF16) for s, (pair, h) in zip(scores, CH_HEADS)]
        outs = [_dot(p, _only(lanes[h], vp[win, CH_COLS[pair]])) for p, (pair, h) in zip(probs, CH_HEADS)]
        for pair, cols in enumerate(CH_COLS):
            o_ref[:, cols] = outs[2 * pair] + outs[2 * pair + 1]

    full = lambda col: pl.BlockSpec((T, W), lambda n: (0, col))
    return pl.pallas_call(
        body, name=name, grid=(T // CH_Q,),
        in_specs=[pl.BlockSpec((CH_Q, W), lambda n: (n, 3)), full(4), full(5),
                  pl.BlockSpec((N_HEADS, CH_Q, CH_WIN), lambda n: (0, 0, 0))],
        out_specs=pl.BlockSpec((CH_Q, W), lambda n: (n, 0)),
        out_shape=_out((T, W), F32),
        scratch_shapes=[pltpu.VMEM((PAD + T, W), BF16)] * 2,
        compiler_params=_params(1, VMEM_LIMIT),
    )(*_hbm(qkv, qkv, qkv, bias))


def _ch_load_padded(k_ref, v_ref, kp, vp):
    for src, dst in ((k_ref, kp), (v_ref, vp)):
        dst[:PAD, :] = jnp.zeros((PAD, dst.shape[1]), dst.dtype)
        dst[PAD:, :] = src[...]


def _ch_bwd(qkv, bias, do, name):
    T = qkv.shape[0]
    W = N_HEADS * HEAD_DIM
    n_chunks = T // CH_Q

    def body(q_ref, k_ref, v_ref, b_ref, do_ref, dq_ref, dk_ref, dv_ref, db_ref, kp, vp, dk_s, dv_s):
        n = pl.program_id(0)

        @pl.when(n == 0)
        def _():
            _ch_load_padded(k_ref, v_ref, kp, vp)
            dk_s[...] = jnp.zeros_like(dk_s)
            dv_s[...] = jnp.zeros_like(dv_s)
            db_ref[...] = jnp.zeros_like(db_ref)

        win = pl.ds(pl.multiple_of(n * CH_Q, CH_Q), CH_WIN)
        valid = _ch_valid(n)
        lanes = _pair_lanes()
        kws = [kp[win, cols] for cols in CH_COLS]
        vws = [vp[win, cols] for cols in CH_COLS]
        qs = [_only(lanes[h], q_ref[:, CH_COLS[pair]]) for pair, h in CH_HEADS]
        dos = [_only(lanes[h], do_ref[:, CH_COLS[pair]].astype(BF16)) for pair, h in CH_HEADS]
        scores = [_dot(q, kws[pair], NT) for q, (pair, _) in zip(qs, CH_HEADS)]
        dps = [_dot(do, vws[pair], NT) for do, (pair, _) in zip(dos, CH_HEADS)]
        probs = [_ch_probs(s, b_ref[2 * pair + h], valid) for s, (pair, h) in zip(scores, CH_HEADS)]
        dzs = [p * (dp - jnp.sum(dp * p, axis=-1, keepdims=True)) for p, dp in zip(probs, dps)]
        for k, dz in enumerate(dzs):
            db_ref[k] += dz
        dzbs = [(dz * ATT_SCALE).astype(BF16) for dz in dzs]
        dqs = [_dot(dz, _only(lanes[h], kws[pair])) for dz, (pair, h) in zip(dzbs, CH_HEADS)]
        dks = [_dot(dz, q, TN) for dz, q in zip(dzbs, qs)]
        dvs = [_dot(p.astype(BF16), do, TN) for p, do in zip(probs, dos)]
        for pair, cols in enumerate(CH_COLS):
            dq_ref[:, cols] = (dqs[2 * pair] + dqs[2 * pair + 1]).astype(BF16)
            dk_s[win, cols] += dks[2 * pair] + dks[2 * pair + 1]
            dv_s[win, cols] += dvs[2 * pair] + dvs[2 * pair + 1]

        @pl.when(n == n_chunks - 1)
        def _():
            dk_ref[...] = dk_s[PAD:, :].astype(BF16)
            dv_ref[...] = dv_s[PAD:, :].astype(BF16)

    full = lambda col: pl.BlockSpec((T, W), lambda n: (0, col))
    blk = lambda col: pl.BlockSpec((CH_Q, W), lambda n: (n, col))
    tab = pl.BlockSpec((N_HEADS, CH_Q, CH_WIN), lambda n: (0, 0, 0))
    out = _out((T, W), BF16)
    return pl.pallas_call(
        body, name=name, grid=(n_chunks,),
        in_specs=[blk(3), full(4), full(5), tab, blk(0)],
        out_specs=[blk(0), full(0), full(0), tab],
        out_shape=[out, out, out, _out((N_HEADS, CH_Q, CH_WIN), F32)],
        scratch_shapes=[pltpu.VMEM((PAD + T, W), BF16)] * 2 + [pltpu.VMEM((PAD + T, W), F32)] * 2,
        compiler_params=_params(1, VMEM_LIMIT),
    )(*_hbm(qkv, qkv, qkv, bias, do))


def _rows_split(a, parts):
    return a.reshape(a.shape[:-2] + (parts, a.shape[-2] // parts, a.shape[-1]))


def _cast_into_slot0(c, ws, name):
    parts = 2
    ws = [_rows_split(_rows_split(w, 2), parts) for w in ws]
    n = len(ws)

    def body(c_ref, *refs):
        for src, dst in zip(refs[:n], refs[n:]):
            dst[0, 0, 0] = src[0, 0].astype(BF16)

    outs = pl.pallas_call(
        body, name=name,
        grid_spec=pltpu.PrefetchScalarGridSpec(
            num_scalar_prefetch=1, grid=(2, parts),
            in_specs=[pl.BlockSpec((1, 1) + w.shape[2:], lambda d, r, c_ref: (d ^ c_ref[0], r, 0, 0)) for w in ws],
            out_specs=[pl.BlockSpec((1, 1, 1) + w.shape[2:], lambda d, r, c_ref: (0, d, r, 0, 0)) for w in ws]),
        out_shape=[_out((N_CHIPS,) + w.shape, BF16) for w in ws],
        compiler_params=_params(2, VMEM_LIMIT),
    )(c, *_hbm(*ws))
    return [o.reshape(N_CHIPS, 2, o.shape[2] * o.shape[3], o.shape[4]) for o in outs]


def _chip_order(me, c, lands, name):
    parts = 2
    xs = [_rows_split(x, parts) for x in lands]

    def body(me_ref, c_ref, *refs):
        n = len(refs) // 2
        for src, dst in zip(refs[:n], refs[n:]):
            dst[...] = src[...]

    outs = pl.pallas_call(
        body, name=name,
        grid_spec=pltpu.PrefetchScalarGridSpec(
            num_scalar_prefetch=2, grid=(N_CHIPS, 2, parts),
            in_specs=[pl.BlockSpec((1, 1, 1) + x.shape[3:],
                                   lambda j, h, r, me_ref, c_ref: (j ^ me_ref[0], h ^ c_ref[0], r, 0, 0)) for x in xs],
            out_specs=[pl.BlockSpec((1, 1, 1) + x.shape[3:], lambda j, h, r, me_ref, c_ref: (j, h, r, 0, 0))
                       for x in xs]),
        out_shape=[_out(x.shape, x.dtype) for x in xs],
        compiler_params=_params(3, VMEM_LIMIT),
    )(me, c, *_hbm(*xs))
    return [o.reshape(o.shape[0], 2 * parts * o.shape[3], o.shape[4]) for o in outs]


def _pair_add(c, mine, got, permuted, name):
    parts = 2
    mine = [_rows_split(m, parts) for m in mine]
    got = [_rows_split(g, parts) for g in got]
    n = len(mine)

    def body(c_ref, *refs):
        for a, b, o in zip(refs[:n], refs[n:2 * n], refs[2 * n:]):
            o[0, 0] = (a[0, 0, 0] + b[0, 0].astype(F32)).astype(BF16)

    def mine_spec(m, perm):
        if perm:
            return pl.BlockSpec((1, 1, 1) + m.shape[3:], lambda j, r, c_ref: (j, 0, r, 0, 0))
        return pl.BlockSpec((1, 1, 1) + m.shape[3:], lambda j, r, c_ref: (j, c_ref[0], r, 0, 0))

    def got_spec(g):
        return pl.BlockSpec((1, 1) + g.shape[2:], lambda j, r, c_ref: (j, r, 0, 0))

    outs = pl.pallas_call(
        body, name=name,
        grid_spec=pltpu.PrefetchScalarGridSpec(
            num_scalar_prefetch=1, grid=(N_CHIPS, parts),
            in_specs=[mine_spec(m, perm) for m, perm in zip(mine, permuted)] + [got_spec(g) for g in got],
            out_specs=[got_spec(g) for g in got]),
        out_shape=[_out(g.shape, BF16) for g in got],
        compiler_params=_params(2, VMEM_LIMIT),
    )(c, *_hbm(*mine, *got))
    return [o.reshape(o.shape[0], o.shape[1] * o.shape[2], o.shape[3]) for o in outs]


def _chip_add(me, partials, landed, permuted, name):
    parts = 2
    ps = [_rows_split(x, parts) for x in partials]
    ls = [_rows_split(x, parts) for x in landed]
    n = len(ps)

    def body(me_ref, *refs):
        for own, got, o in zip(refs[:n], refs[n:2 * n], refs[2 * n:]):
            acc = own[0, 0].astype(F32)
            for r in range(N_CHIPS - 1):
                acc = acc + got[r, 0].astype(F32)
            o[0] = acc

    def own_spec(x, perm):
        if perm:
            return pl.BlockSpec((1, 1) + x.shape[2:], lambda r, me_ref: (0, r, 0, 0))
        return pl.BlockSpec((1, 1) + x.shape[2:], lambda r, me_ref: (me_ref[0], r, 0, 0))

    outs = pl.pallas_call(
        body, name=name,
        grid_spec=pltpu.PrefetchScalarGridSpec(
            num_scalar_prefetch=1, grid=(parts,),
            in_specs=[own_spec(x, perm) for x, perm in zip(ps, permuted)]
            + [pl.BlockSpec((N_CHIPS - 1, 1) + x.shape[2:], lambda r, me_ref: (0, r, 0, 0)) for x in ls],
            out_specs=[pl.BlockSpec((1,) + x.shape[2:], lambda r, me_ref: (r, 0, 0)) for x in ps]),
        out_shape=[_out(x.shape[1:], F32) for x in ps],
        compiler_params=_params(1, VMEM_LIMIT),
    )(me, *_hbm(*ps, *ls))
    return [o.reshape(o.shape[0] * o.shape[1], o.shape[2]) for o in outs]


def _adamw_math(w, g, m, v):
    m = ADAM_B1 * m + (1.0 - ADAM_B1) * g
    v = ADAM_B2 * v + (1.0 - ADAM_B2) * (g * g)
    m_hat = m / (1.0 - ADAM_B1 ** ADAM_STEP)
    v_hat = v / (1.0 - ADAM_B2 ** ADAM_STEP)
    delta = -ADAM_LR * (m_hat / (jnp.sqrt(v_hat) + ADAM_EPS) + ADAM_WD * w)
    return delta, m, v


def _adamw(ws, gs, ms, vs, parts, name):
    n = len(ws)
    flat = [_rows_split(a, parts) for a in (*ws, *gs, *ms, *vs)]

    def body(*refs):
        ins, outs = refs[:4 * n], refs[4 * n:]
        for k in range(n):
            d, m, v = _adamw_math(ins[k][...], ins[n + k][...], ins[2 * n + k][...], ins[3 * n + k][...])
            outs[k][...] = d
            outs[n + k][...] = m
            outs[2 * n + k][...] = v

    spec = lambda a: pl.BlockSpec((1,) + a.shape[1:], lambda i: (i, 0, 0))
    outs = pl.pallas_call(
        body, name=name, grid=(parts,),
        in_specs=[spec(a) for a in flat], out_specs=[spec(a) for a in flat[:n]] * 3,
        out_shape=[_out(a.shape, F32) for a in flat[:n]] * 3,
        compiler_params=_params(1, VMEM_LIMIT),
    )(*_hbm(*flat))
    outs = [o.reshape(o.shape[0] * o.shape[1], o.shape[2]) for o in outs]
    return outs[:n], outs[n:2 * n], outs[2 * n:]


def _adamw_halves(c, ws, owns, others, ms, vs, name):
    parts = 4
    n = len(ws)
    whole = [_rows_split(_rows_split(a, 2), parts) for a in (*ws, *ms, *vs)]
    halves = [_rows_split(a, parts) for a in (*owns, *others)]

    def body(c_ref, *refs):
        ins, outs = refs[:5 * n], refs[5 * n:]
        mine = pl.program_id(0) == c_ref[0]
        for k in range(n):
            g = jnp.where(mine, ins[3 * n + k][0], ins[4 * n + k][0])
            d, m, v = _adamw_math(ins[k][0, 0], g, ins[n + k][0, 0], ins[2 * n + k][0, 0])
            for slot, val in enumerate((g, d, m, v)):
                outs[slot * n + k][0, 0] = val

    wspec = lambda a: pl.BlockSpec((1, 1) + a.shape[2:], lambda h, r, c_ref: (h, r, 0, 0))
    hspec = lambda a: pl.BlockSpec((1,) + a.shape[1:], lambda h, r, c_ref: (r, 0, 0))
    outs = pl.pallas_call(
        body, name=name,
        grid_spec=pltpu.PrefetchScalarGridSpec(
            num_scalar_prefetch=1, grid=(2, parts),
            in_specs=[wspec(a) for a in whole] + [hspec(a) for a in halves],
            out_specs=[wspec(a) for a in whole[:n]] * 4),
        out_shape=[_out(a.shape, F32) for a in whole[:n]] * 4,
        compiler_params=_params(2, VMEM_LIMIT),
    )(c, *_hbm(*whole, *halves))
    outs = [o.reshape(2 * parts * o.shape[2], o.shape[3]) for o in outs]
    return outs[:n], outs[n:2 * n], outs[2 * n:3 * n], outs[3 * n:]


def _place():
    x, y, c = lax.axis_index("x"), lax.axis_index("y"), lax.axis_index("c")
    peers = [(x ^ (r >> 1), y ^ (r & 1), c) for r in (1, 2, 3)]
    return x, y, c, peers


def _handshake(peers):
    barrier = pltpu.get_barrier_semaphore()
    for peer in peers:
        pl.semaphore_signal(barrier, inc=1, device_id=peer, device_id_type=MESH)
    pl.semaphore_wait(barrier, len(peers))


ANY = pl.BlockSpec(memory_space=pl.ANY)
HBM = pl.BlockSpec(memory_space=pltpu.HBM)
SEM = pl.BlockSpec(memory_space=pltpu.SEMAPHORE)
SPLIT_COPY = pltpu.SideEffectType.DATAFLOW_SIDE_EFFECTING


def _in_hbm(a):
    return pltpu.with_memory_space_constraint(a, pltpu.HBM)


def _split_start(body, name, collective_id, operands, n_sems, after=None):
    n = len(operands)
    extra = [] if after is None else [after]

    def wrapped(*refs):
        at = n + len(extra)
        body(refs[:n], refs[at], refs[at + 1])
        token = refs[-1]
        token[...] = jnp.zeros_like(token)

    outs = pl.pallas_call(
        wrapped, name=name,
        in_specs=[HBM] * n + [ANY] * len(extra),
        out_shape=(pltpu.SemaphoreType.DMA((n_sems,)), pltpu.SemaphoreType.DMA((n_sems,)),
                   *[pltpu.HBM(a.shape, a.dtype) for a in operands], jax.ShapeDtypeStruct((8, 128), F32)),
        out_specs=(SEM, SEM, *[HBM] * n, pl.BlockSpec(memory_space=pltpu.VMEM)),
        input_output_aliases={i: 2 + i for i in range(n)},
        compiler_params=pltpu.CompilerParams(has_side_effects=SPLIT_COPY, collective_id=collective_id),
    )(*[_in_hbm(a) for a in operands], *extra)
    return outs[0], outs[1], list(outs[2:2 + n]), outs[-1]


def _split_wait(body, name, send_sem, recv_sem, operands, after):
    n = len(operands)

    def wrapped(*refs):
        body(refs[:n], refs[n], refs[n + 1])

    outs = pl.pallas_call(
        wrapped, name=name,
        in_specs=[HBM] * n + [SEM, SEM, ANY],
        out_shape=tuple(pltpu.HBM(a.shape, a.dtype) for a in operands),
        out_specs=tuple([HBM] * n),
        input_output_aliases={i: i for i in range(n)},
        compiler_params=pltpu.CompilerParams(has_side_effects=SPLIT_COPY),
    )(*operands, send_sem, recv_sem, after)
    return list(outs)


def _gather_copies(lands, send_sem, recv_sem):
    peers = _place()[3]
    return [pltpu.make_async_remote_copy(
        src_ref=land.at[0, 0], dst_ref=land.at[r + 1, 0],
        send_sem=send_sem.at[a * 3 + r], recv_sem=recv_sem.at[a * 3 + r],
        device_id=peers[r], device_id_type=MESH) for a, land in enumerate(lands) for r in range(3)]


def _gather_start(lands, name, collective_id, after):
    def body(refs, send_sem, recv_sem):
        _handshake(_place()[3])
        for cp in _gather_copies(refs, send_sem, recv_sem):
            cp.start()

    return _split_start(body, name, collective_id, list(lands), 3 * len(lands), after)


def _gather_wait(send_sem, recv_sem, operands, after, name):
    def body(refs, send_sem, recv_sem):
        for cp in _gather_copies(refs, send_sem, recv_sem):
            cp.wait_send()
            cp.wait_recv()

    return _split_wait(body, name, send_sem, recv_sem, operands, after)


def _gather_finish(lands, with_ici, name):
    n = len(lands)

    def body(*refs):
        land = refs[n:2 * n]
        send_ici, recv_ici, send_d2d, recv_d2d = refs[2 * n:]
        x, y, c, _ = _place()
        ici = _gather_copies(land, send_ici, recv_ici) if with_ici else []
        for cp in ici:
            cp.start()
        passed = [pltpu.make_async_remote_copy(
            src_ref=land[a].at[r + 1, 0], dst_ref=land[a].at[r + 1, 1],
            send_sem=send_d2d.at[a * 3 + r], recv_sem=recv_d2d.at[a * 3 + r],
            device_id=(x, y, 1 - c), device_id_type=MESH) for a in range(n) for r in range(3)]
        for k, cp in enumerate(passed):
            if with_ici:
                ici[k].wait_recv()
            cp.start()
        for cp in passed:
            cp.wait_recv()
        for cp in ici:
            cp.wait_send()
        for cp in passed:
            cp.wait_send()

    outs = pl.pallas_call(
        body, name=name,
        in_specs=[ANY] * n, out_specs=[ANY] * n,
        out_shape=[_out(l.shape, l.dtype) for l in lands],
        input_output_aliases={a: a for a in range(n)},
        scratch_shapes=[pltpu.SemaphoreType.DMA((3 * n,))] * 4,
    )(*lands)
    return list(outs)


def _slabs(land):
    return land.reshape(N_CHIPS, 2 * land.shape[2], land.shape[3])


def _pair_swap(grads, permuted, name):
    n = len(grads)

    def body(*refs):
        src, dst = refs[:n], refs[n:2 * n]
        send_sem, recv_sem = refs[2 * n:]
        x, y, c, _ = _place()
        copies = [pltpu.make_async_remote_copy(
            src_ref=src[a].at[:, 1] if permuted[a] else src[a].at[:, 1 - c], dst_ref=dst[a],
            send_sem=send_sem.at[a], recv_sem=recv_sem.at[a],
            device_id=(x, y, 1 - c), device_id_type=MESH) for a in range(n)]
        for cp in copies:
            cp.start()
        for cp in copies:
            cp.wait()

    return pl.pallas_call(
        body, name=name,
        in_specs=[ANY] * n, out_specs=[ANY] * n,
        out_shape=[_out((N_CHIPS,) + g.shape[2:], g.dtype) for g in grads],
        scratch_shapes=[pltpu.SemaphoreType.DMA((n,))] * 2,
    )(*grads)


def _swap_copies(refs, permuted, send_sem, recv_sem):
    n = len(refs) // 2
    x, y, c, _ = _place()
    return [pltpu.make_async_remote_copy(
        src_ref=refs[a].at[:, 1] if permuted[a] else refs[a].at[:, 1 - c], dst_ref=refs[n + a],
        send_sem=send_sem.at[a], recv_sem=recv_sem.at[a],
        device_id=(x, y, 1 - c), device_id_type=MESH) for a in range(n)]


def _pair_swap_start(grads, permuted, name, collective_id):
    def body(refs, send_sem, recv_sem):
        x, y, c, _ = _place()
        _handshake([(x, y, 1 - c)])
        for cp in _swap_copies(refs, permuted, send_sem, recv_sem):
            cp.start()

    lands = [lax.empty((N_CHIPS,) + g.shape[2:], g.dtype) for g in grads]
    return _split_start(body, name, collective_id, list(grads) + lands, len(grads))


def _pair_swap_wait(send_sem, recv_sem, operands, permuted, after, name):
    def body(refs, send_sem, recv_sem):
        for cp in _swap_copies(refs, permuted, send_sem, recv_sem):
            cp.wait_send()
            cp.wait_recv()

    return _split_wait(body, name, send_sem, recv_sem, operands, after)


def _scatter_copies(refs, permuted, send_sem, recv_sem):
    n = len(refs) // 2
    x, y, _, peers = _place()
    me = 2 * x + y
    return [pltpu.make_async_remote_copy(
        src_ref=refs[a].at[r + 1] if permuted[a] else refs[a].at[me ^ (r + 1)], dst_ref=refs[n + a].at[r],
        send_sem=send_sem.at[a * 3 + r], recv_sem=recv_sem.at[a * 3 + r],
        device_id=peers[r], device_id_type=MESH) for a in range(n) for r in range(3)]


def _scatter_start(partials, permuted, name, collective_id):
    def body(refs, send_sem, recv_sem):
        _handshake(_place()[3])
        for cp in _scatter_copies(refs, permuted, send_sem, recv_sem):
            cp.start()

    lands = [lax.empty((N_CHIPS - 1,) + p.shape[1:], p.dtype) for p in partials]
    return _split_start(body, name, collective_id, list(partials) + lands, 3 * len(partials))


def _scatter_wait(send_sem, recv_sem, operands, permuted, after, name):
    def body(refs, send_sem, recv_sem):
        for cp in _scatter_copies(refs, permuted, send_sem, recv_sem):
            cp.wait_send()
            cp.wait_recv()

    return _split_wait(body, name, send_sem, recv_sem, operands, after)


def _pair_join(halves, name):
    n = len(halves)

    def body(*refs):
        src, dst = refs[:n], refs[n:2 * n]
        send_sem, recv_sem = refs[2 * n:]
        x, y, c, _ = _place()
        copies = [pltpu.make_async_remote_copy(
            src_ref=src[a], dst_ref=dst[a], send_sem=send_sem.at[a], recv_sem=recv_sem.at[a],
            device_id=(x, y, 1 - c), device_id_type=MESH) for a in range(n)]
        for cp in copies:
            cp.start()
        for cp in copies:
            cp.wait()

    return pl.pallas_call(
        body, name=name,
        in_specs=[ANY] * n, out_specs=[ANY] * n,
        out_shape=[_out(h.shape, F32) for h in halves],
        scratch_shapes=[pltpu.SemaphoreType.DMA((n,))] * 2,
    )(*halves)


def _all_sum_small(v, after, name):
    R, C = v.shape
    n_dev = 8

    def body(v_ref, after_ref, o_ref, buf, send_sem, recv_sem):
        x, y, c, _ = _place()
        me = 4 * x + 2 * y + c
        buf[me] = v_ref[...]
        copies = []
        for k in range(1, n_dev):
            peer = (x ^ (k >> 2), y ^ ((k >> 1) & 1), c ^ (k & 1))
            copies.append(pltpu.make_async_remote_copy(
                src_ref=v_ref, dst_ref=buf.at[me], send_sem=send_sem.at[k - 1], recv_sem=recv_sem.at[k - 1],
                device_id=peer, device_id_type=MESH))
        for cp in copies:
            cp.start()
        for cp in copies:
            cp.wait()
        acc = buf[0]
        for m in range(1, n_dev):
            acc = acc + buf[m]
        o_ref[...] = acc

    return pl.pallas_call(
        body, name=name,
        in_specs=[pl.BlockSpec(memory_space=pltpu.VMEM), ANY], out_specs=pl.BlockSpec(memory_space=pltpu.VMEM),
        out_shape=jax.ShapeDtypeStruct((R, C), F32),
        scratch_shapes=[pltpu.VMEM((n_dev, R, C), F32), pltpu.SemaphoreType.DMA((n_dev - 1,)),
                        pltpu.SemaphoreType.DMA((n_dev - 1,))],
    )(v, after)


class _WholeWeights:
    def __init__(self, w):
        self.w = w

    def weights(self, group, after=None):
        return self.w, None

    def grads_ready(self, group, gw):
        return None

    def grads_sent(self, group, after):
        return None


def _local_step(x, p, target, gains, rel_bias, hooks):
    T, D = x.shape
    S = N_CHIPS

    tied = lambda gain, token: gain if token is None else gain + token[0, 0]
    w, token = hooks.weights("first")
    w = dict(w)
    h1, xn1, g1, u1, a1, f1 = _ffn_fwd(x, tied(gains["ffn1_pre"], token), gains["ffn1_post"], w["ffn1_gate"],
                                       w["ffn1_up"], w["ffn1_down"], "ffn1_fwd")
    more, token = hooks.weights("in", h1)
    w.update(more)
    qkv, un = _norm_proj(h1, tied(gains["mix_pre"], token), w["in"], "qkv_proj")
    bias = _ch_group_bias(_bias_table(rel_bias, "bias_table").transpose(1, 0, 2))
    o_a = _sb_fwd(qkv, "sb_fwd")
    o_b = _ch_fwd(qkv, bias, "ch_fwd")
    w.update(hooks.weights("rest", o_b)[0])
    w_out = w["out"].reshape(D, D)
    h2, mixed, mo = _mix_out_fwd(h1, o_a, o_b, gains["out_sb"], gains["out_ch"], w_out, gains["mix_post"],
                                 "mix_out_fwd")
    h3, xn2, g2, u2, a2, f2 = _ffn_fwd(h2, gains["ffn2_pre"], gains["ffn2_post"], w["ffn2_gate"], w["ffn2_up"],
                                       w["ffn2_down"], "ffn2_fwd")
    w_ple_proj = w["ple_proj"].transpose(1, 0, 2).reshape(p.shape[1], D)
    w_ple_gate = w["ple_gate"].reshape(D, D)

    loss, dh3, dproj, dgate, dg_ple = _ple_loss(h3, p, target, w_ple_proj, w_ple_gate, gains["ple_post"], "ple_loss")
    gw, gg = {}, {"ple_post": dg_ple}
    gw["ple_proj"] = _mm_tn(p[None], dproj, p.shape[1], "dw_ple_proj")
    row_sharded = lambda pair: tuple(o.reshape(S, D // S, D) for o in pair)
    gw["ple_gate"] = row_sharded(_mm_tn(h3[None], dgate[None], 512, "dw_ple_gate"))

    def ffn_bwd(tag, dh, x_in, xn, g_act, u_act, a_act, f, group):
        dgp, dup, df, gg[tag + "_post"] = _ffn_bwd_act(dh, f, gains[tag + "_post"], w[tag + "_down"], g_act, u_act,
                                                       tag + "_bwd_act")
        gw[tag + "_gate"] = _mm_tn(dgp, xn[None], dgp.shape[2], "dw_" + tag + "_gate")
        gw[tag + "_up"] = _mm_tn(dup, xn[None], dup.shape[2], "dw_" + tag + "_up")
        gw[tag + "_down"] = _mm_tn(a_act, df[None], a_act.shape[2], "dw_" + tag + "_down")
        g_pre = gains[tag + "_pre"]
        if group is not None:
            token = hooks.grads_ready(group, gw)
            g_pre = g_pre if token is None else g_pre + token[0, 0]
        dx, gg[tag + "_pre"] = _proj_bwd([dgp, dup], [w[tag + "_gate"], w[tag + "_up"]], x_in, g_pre, dh,
                                         tag + "_bwd_in")
        return dx

    dh2 = ffn_bwd("ffn2", dh3, h2, xn2, g2, u2, a2, f2, None)
    dmo, do_a, do_b, gg["mix_post"], gg["out_sb"], gg["out_ch"] = _mix_out_bwd(
        dh2, mo, gains["mix_post"], w_out, o_a, o_b, gains["out_sb"], gains["out_ch"], "mix_out_bwd")
    gw["out"] = row_sharded(_mm_tn(mixed[None], dmo[None], 512, "dw_out"))
    token = hooks.grads_ready("early", gw)
    if token is not None:
        do_a = do_a + token[0, 0]
    dq_a, dk_a, dv_a = _sb_bwd(qkv, do_a, o_a, "sb_bwd")
    token = hooks.grads_sent("early", dq_a)
    if token is not None:
        do_b = do_b + token[0, 0]
    dq_b, dk_b, dv_b, dbias = _ch_bwd(qkv, bias, do_b, "ch_bwd")
    g_rel = _bias_grad(_ch_fold_bias_grad(dbias).transpose(1, 0, 2), "bias_grad")
    dqkv = jnp.concatenate([dq_a, dk_a, dv_a, dq_b, dk_b, dv_b], axis=1)
    gw["in"] = _mm_tn(un[None], dqkv, 512, "dw_in", groups=S)
    dh1, gg["mix_pre"] = _proj_bwd([dqkv], [w["in"]], h1, gains["mix_pre"], dh2, "qkv_bwd_in")
    dx = ffn_bwd("ffn1", dh1, x, xn1, g1, u1, a1, f1, "late")
    return loss, dx, gw, gg, g_rel


BIG = ["ffn1_gate", "ffn1_up", "ffn1_down", "in", "out", "ffn2_gate", "ffn2_up", "ffn2_down", "ple_proj", "ple_gate"]
GAINS = ["ffn1_pre", "ffn1_post", "mix_pre", "mix_post", "out_sb", "out_ch", "ffn2_pre", "ffn2_post", "ple_post"]
TRANSPOSED = ("w_ffn1_gate", "w_ffn1_up", "w_ffn2_gate", "w_ffn2_up")
PERMUTED = ("ffn1_gate", "ffn1_up", "ffn1_down", "ffn2_gate", "ffn2_up", "ffn2_down")
W_GROUPS = {"first": ["ffn1_gate", "ffn1_up", "ffn1_down"], "in": ["in"],
            "rest": ["out", "ffn2_gate", "ffn2_up", "ffn2_down", "ple_proj", "ple_gate"]}
G_GROUPS = {"early": ["ple_proj", "ple_gate", "ffn2_gate", "ffn2_up", "ffn2_down", "out"],
            "late": ["in", "ffn1_gate", "ffn1_up", "ffn1_down"]}
ORDER = ["g_ffn1_pre", "g_ffn1_post", "w_ffn1_gate", "w_ffn1_up", "w_ffn1_down", "g_mix_pre", "g_mix_post", "w_in",
         "g_out_sb", "g_out_ch", "rel_bias", "w_out", "g_ffn2_pre", "g_ffn2_post", "w_ffn2_gate", "w_ffn2_up",
         "w_ffn2_down", "w_ple_proj", "w_ple_gate", "g_ple_post"]


def kernel(x, p, g_ffn1_pre, g_ffn1_post, w_ffn1_gate, w_ffn1_up, w_ffn1_down, g_mix_pre, g_mix_post, w_in, g_out_sb, g_out_ch, rel_bias, w_out, g_ffn2_pre, g_ffn2_post, w_ffn2_gate, w_ffn2_up, w_ffn2_down, w_ple_proj, w_ple_gate, g_ple_post, loss_target, m_g_ffn1_pre, m_g_ffn1_post, m_w_ffn1_gate, m_w_ffn1_up, m_w_ffn1_down, m_g_mix_pre, m_g_mix_post, m_w_in, m_g_out_sb, m_g_out_ch, m_rel_bias, m_w_out, m_g_ffn2_pre, m_g_ffn2_post, m_w_ffn2_gate, m_w_ffn2_up, m_w_ffn2_down, m_w_ple_proj, m_w_ple_gate, m_g_ple_post, v_g_ffn1_pre, v_g_ffn1_post, v_w_ffn1_gate, v_w_ffn1_up, v_w_ffn1_down, v_g_mix_pre, v_g_mix_post, v_w_in, v_g_out_sb, v_g_out_ch, v_rel_bias, v_w_out, v_g_ffn2_pre, v_g_ffn2_post, v_w_ffn2_gate, v_w_ffn2_up, v_w_ffn2_down, v_w_ple_proj, v_w_ple_gate, v_g_ple_post):
    args = dict(locals())
    take = lambda a, n: a[0].T if n in TRANSPOSED else a[0]
    wts = {n: take(args[n], n) for n in ORDER}
    ms = {n: take(args["m_" + n], n) for n in ORDER}
    vs = {n: take(args["v_" + n], n) for n in ORDER}
    gains = {n: wts["g_" + n][None] for n in GAINS}

    c_idx = lax.axis_index("c").astype(jnp.int32).reshape(1)
    me_idx = (2 * lax.axis_index("x") + lax.axis_index("y")).astype(jnp.int32).reshape(1)
    south = lax.axis_index("c") == 0

    lands = dict(zip(BIG, _cast_into_slot0(c_idx, [wts["w_" + n] for n in BIG], "cast_weights")))

    def in_order(names, zones):
        plain = [n for n in names if n not in PERMUTED]
        fixed = dict(zip(plain, _chip_order(me_idx, c_idx, [zones[n] for n in plain], "chip_order_" + plain[0]))
                     ) if plain else {}
        return {n: fixed[n] if n in fixed else _slabs(zones[n]) for n in names}

    class Overlapped:
        def __init__(self):
            self.started = {}
            self.flying = {}

        def start(self, group, collective_id, after):
            self.flying[group] = _gather_start([lands[n] for n in W_GROUPS[group]], "gather_%s_start" % group,
                                               collective_id, after)
            return self.flying[group][3]

        def weights(self, group, after=None):
            names = W_GROUPS[group]
            if group == "first":
                zones = _gather_finish([lands[n] for n in names], True, "gather_first")
                token = self.start("rest", 4, self.start("in", 1, zones[0]))
                return in_order(names, dict(zip(names, zones))), token
            send_sem, recv_sem, zones, _ = self.flying[group]
            zones = _gather_wait(send_sem, recv_sem, zones, after, "gather_%s_wait" % group)
            zones = _gather_finish(zones, False, "gather_%s_finish" % group)
            return in_order(names, dict(zip(names, zones))), None

        def grads_ready(self, group, gw):
            names = G_GROUPS[group]
            perm = [n in PERMUTED for n in names]
            halved = lambda g: g.reshape(N_CHIPS, 2, g.shape[1] // 2, g.shape[2])
            mine = [halved(gw[n][0]) for n in names]
            narrow = [halved(gw[n][1]) for n in names]
            if group == "late":
                return self.scatter(group, names, perm, mine, _pair_swap(narrow, perm, "grad_pair_swap_late"))
            self.swapping = names, perm, mine, _pair_swap_start(narrow, perm, "grad_pair_swap_start_early", 5)
            return self.swapping[3][3]

        def grads_sent(self, group, after):
            names, perm, mine, (send_sem, recv_sem, operands, _) = self.swapping
            operands = _pair_swap_wait(send_sem, recv_sem, operands, perm, after, "grad_pair_swap_wait_early")
            return self.scatter(group, names, perm, mine, operands[len(names):])

        def scatter(self, group, names, perm, mine, got):
            partial = _pair_add(c_idx, mine, got, perm, "grad_pair_add_" + group)
            send_sem, recv_sem, operands, token = _scatter_start(partial, perm, "grad_scatter_start_" + group,
                                                                 {"early": 2, "late": 3}[group])
            self.started[group] = names, perm, send_sem, recv_sem, operands, token
            return token

    def reduce_finish(state, after, tag):
        names, perm, send_sem, recv_sem, operands, _ = state
        operands = _scatter_wait(send_sem, recv_sem, operands, perm, after, "grad_scatter_wait_" + tag)
        n = len(names)
        own = _chip_add(me_idx, operands[:n], operands[n:], perm, "grad_chip_add_" + tag)
        return own, _pair_join(own, "grad_pair_join_" + tag)

    hooks = Overlapped()
    loss, dx, gw, gg, g_rel = _local_step(x[0], p[0, 0], loss_target[0], gains, wts["rel_bias"], hooks)

    grads, delta, new_m, new_v = {}, {}, {}, {}

    def finish(group, after):
        own, other = reduce_finish(hooks.started[group], after, group)
        names = ["w_" + n for n in G_GROUPS[group]]
        g, d, m, v = _adamw_halves(c_idx, [wts[n] for n in names], own, other, [ms[n] for n in names],
                                   [vs[n] for n in names], "adamw_" + group)
        for n, gg_, dd, mm, vv in zip(names, g, d, m, v):
            grads[n], delta[n], new_m[n], new_v[n] = gg_, dd, mm, vv
        return d[0]

    finish("late", finish("early", dx))

    pieces = [gg[n].reshape(-1, 128) for n in GAINS] + [jnp.pad(g_rel, ((0, 0), (0, N_REL_PAD - N_REL))).reshape(-1, 128)]
    summed = _all_sum_small(jnp.concatenate(pieces + [loss], axis=0), delta["w_in"], "small_grad_sum")
    at = 0
    for n, piece in zip(GAINS, pieces[:-1]):
        grads["g_" + n] = summed[at:at + piece.shape[0]].reshape(1, -1)[0]
        at += piece.shape[0]
    grads["rel_bias"] = summed[at:at + pieces[-1].shape[0]].reshape(N_HEADS, N_REL_PAD)[:, :N_REL]
    loss = summed[at + pieces[-1].shape[0], 0]

    small = ["g_" + n for n in GAINS] + ["rel_bias"]
    as_rows = lambda a: (a.reshape(-1, 128) if a.size % 128 == 0 else jnp.pad(a, ((0, 0), (0, N_REL_PAD - N_REL))).reshape(-1, 128))
    d, m, v = _adamw([as_rows(wts[n]) for n in small], [as_rows(grads[n]) for n in small],
                     [as_rows(ms[n]) for n in small], [as_rows(vs[n]) for n in small], 1, "adamw_small")
    for n, dd, mm, vv in zip(small, d, m, v):
        back = (lambda a: a.reshape(N_HEADS, N_REL_PAD)[:, :N_REL]) if n == "rel_bias" else (lambda a: a.reshape(-1))
        delta[n], new_m[n], new_v[n] = back(dd), back(mm), back(vv)

    outs = [loss, dx[None]]
    for table in (grads, delta, new_m, new_v):
        outs += [(table[n].T if n in TRANSPOSED else table[n])[None] for n in ORDER]
    return tuple(outs)
```

```python
import functools

import jax
import jax.numpy as jnp
from jax import lax
from jax.experimental import pallas as pl
from jax.experimental.pallas import tpu as pltpu

F32 = jnp.float32
BF16 = jnp.bfloat16
EPS = 1e-6
N_CHIPS = 4
HEAD_DIM = 64
N_HEADS = 8
CHUNK = 64
LOOKBACK = 8
BAND = (LOOKBACK + 1) * CHUNK
PAD = LOOKBACK * CHUNK
REL_CLIP = 128
N_REL = 2 * REL_CLIP + 1
N_REL_PAD = 384
SB_BLOCK = 256
PAIR = 2 * HEAD_DIM
SB_PAIRS = 2
ATT_SCALE = HEAD_DIM ** -0.5
NEG_INF = -1e30
ROW_BLOCK = 512
WIDE_ROW_BLOCK = 1024
VMEM_LIMIT_WIDE = 56 * 1024 * 1024
VMEM_LIMIT = 48 * 1024 * 1024
MESH = pl.DeviceIdType.MESH

ADAM_LR = 0.001
ADAM_B1 = 0.9
ADAM_B2 = 0.999
ADAM_EPS = 1e-08
ADAM_WD = 0.01
ADAM_STEP = 10

NT = (((1,), (1,)), ((), ()))
TN = (((0,), (0,)), ((), ()))


def _params(n_grid, vmem=None):
    return pltpu.CompilerParams(dimension_semantics=("arbitrary",) * n_grid, vmem_limit_bytes=vmem)


def _hbm(*arrays):
    return [pltpu.with_memory_space_constraint(a, pltpu.HBM) for a in arrays]


def _out(shape, dtype):
    return pltpu.HBM(shape, dtype)


def _dot(a, b, dims=None):
    if dims is None:
        return jnp.dot(a, b, preferred_element_type=F32)
    return lax.dot_general(a, b, dims, preferred_element_type=F32)


def _sigmoid(x):
    return 1.0 / (1.0 + jnp.exp(-x))


def _rms_fwd(x, g):
    r = lax.rsqrt(jnp.mean(x * x, axis=-1, keepdims=True) + EPS)
    return x * r * g


def _rms_bwd(x, g, dy):
    r = lax.rsqrt(jnp.mean(x * x, axis=-1, keepdims=True) + EPS)
    xh = x * r
    dg = jnp.sum(dy * xh, axis=0, keepdims=True)
    t = dy * g
    dx = r * (t - xh * jnp.mean(t * xh, axis=-1, keepdims=True))
    return dx, dg


def _accumulate(ref, val, first):
    @pl.when(first)
    def _():
        ref[...] = val

    @pl.when(jnp.logical_not(first))
    def _():
        ref[...] += val


def _split2(x):
    hi = x.astype(BF16)
    lo = (x - hi.astype(F32)).astype(BF16)
    return hi, lo


def _ffn_fwd(x, g_pre, g_post, wg, wu, wd, name):
    T, D = x.shape
    S, FS, _ = wg.shape
    tm = min(WIDE_ROW_BLOCK, T)

    def body(x_ref, gpre_ref, gpost_ref, wg_ref, wu_ref, wd_ref,
             h_ref, xn_ref, g_ref, u_ref, a_ref, f_ref):
        k = pl.program_id(1)

        @pl.when(k == 0)
        def _():
            xn_ref[...] = _rms_fwd(x_ref[...], gpre_ref[...]).astype(BF16)

        xn = xn_ref[...]
        g = _dot(xn, wg_ref[0], NT)
        u = _dot(xn, wu_ref[0], NT)
        g_ref[0] = g
        u_ref[0] = u
        a = (g * _sigmoid(g) * u).astype(BF16)
        a_ref[0] = a
        _accumulate(f_ref, _dot(a, wd_ref[0]), k == 0)

        @pl.when(k == S - 1)
        def _():
            h_ref[...] = x_ref[...] + 0.5 * _rms_fwd(f_ref[...], gpost_ref[...])

    row = pl.BlockSpec((tm, D), lambda i, k: (i, 0))
    vec = pl.BlockSpec((1, D), lambda i, k: (0, 0))
    act = pl.BlockSpec((1, tm, FS), lambda i, k: (k, i, 0))
    return pl.pallas_call(
        body, name=name, grid=(T // tm, S),
        in_specs=[row, vec, vec] + [pl.BlockSpec((1, FS, D), lambda i, k: (k, 0, 0))] * 3,
        out_specs=[row, row, act, act, act, row],
        out_shape=[_out((T, D), F32), _out((T, D), BF16),
                   _out((S, T, FS), F32), _out((S, T, FS), F32),
                   _out((S, T, FS), BF16), _out((T, D), F32)],
        compiler_params=_params(2, VMEM_LIMIT_WIDE),
    )(*_hbm(x, g_pre, g_post, wg, wu, wd))


def _ffn_bwd_act(dh, f, g_post, wd, g_act, u_act, name):
    T, D = dh.shape
    S, FS, _ = wd.shape
    tm = min(WIDE_ROW_BLOCK, T)

    def body(dh_ref, f_ref, gpost_ref, wd_ref, g_ref, u_ref, dgp_ref, dup_ref, df_ref, dgain_ref, df_s):
        i, k = pl.program_id(0), pl.program_id(1)

        @pl.when(k == 0)
        def _():
            df, dgain = _rms_bwd(f_ref[...], gpost_ref[...], 0.5 * dh_ref[...])
            df_s[...] = df.astype(BF16)
            df_ref[...] = df_s[...]
            _accumulate(dgain_ref, dgain, i == 0)

        da = _dot(df_s[...], wd_ref[0], NT)
        g = g_ref[0]
        s = _sigmoid(g)
        dup_ref[0] = (da * (g * s)).astype(BF16)
        dgp_ref[0] = (da * u_ref[0] * (s * (1.0 + g * (1.0 - s)))).astype(BF16)

    row = pl.BlockSpec((tm, D), lambda i, k: (i, 0))
    vec = pl.BlockSpec((1, D), lambda i, k: (0, 0))
    act = pl.BlockSpec((1, tm, FS), lambda i, k: (k, i, 0))
    return pl.pallas_call(
        body, name=name, grid=(T // tm, S),
        in_specs=[row, row, vec, pl.BlockSpec((1, FS, D), lambda i, k: (k, 0, 0)), act, act],
        out_specs=[act, act, row, vec],
        out_shape=[_out((S, T, FS), BF16), _out((S, T, FS), BF16),
                   _out((T, D), BF16), _out((1, D), F32)],
        scratch_shapes=[pltpu.VMEM((tm, D), BF16)],
        compiler_params=_params(2, VMEM_LIMIT_WIDE),
    )(*_hbm(dh, f, g_post, wd, g_act, u_act))


def _proj_bwd(dys, ws, x, g_pre, dh, name):
    T, D = x.shape
    n = len(dys)
    flat = dys[0].ndim == 2
    S = ws[0].shape[0]
    N = ws[0].shape[2] if flat else ws[0].shape[1]
    tm = min(WIDE_ROW_BLOCK, T)

    def body(*refs):
        dy_refs, w_refs = refs[:n], refs[n:2 * n]
        x_ref, gpre_ref, dh_ref, dx_ref, dgain_ref, acc_s = refs[2 * n:]
        i, k = pl.program_id(0), pl.program_id(1)
        part = None
        for dy_ref, w_ref in zip(dy_refs, w_refs):
            term = _dot(dy_ref[...], w_ref[0], NT) if flat else _dot(dy_ref[0], w_ref[0])
            part = term if part is None else part + term
        _accumulate(acc_s, part, k == 0)

        @pl.when(k == S - 1)
        def _():
            dx, dgain = _rms_bwd(x_ref[...], gpre_ref[...], acc_s[...])
            dx_ref[...] = dh_ref[...] + dx
            _accumulate(dgain_ref, dgain, i == 0)

    row = pl.BlockSpec((tm, D), lambda i, k: (i, 0))
    vec = pl.BlockSpec((1, D), lambda i, k: (0, 0))
    return pl.pallas_call(
        body, name=name, grid=(T // tm, S),
        in_specs=[pl.BlockSpec((tm, N), lambda i, k: (i, k)) if flat else pl.BlockSpec((1, tm, N), lambda i, k: (k, i, 0))] * n
        + [pl.BlockSpec((1,) + ws[0].shape[1:], lambda i, k: (k, 0, 0))] * n + [row, vec, row],
        out_specs=[row, vec],
        out_shape=[_out((T, D), F32), _out((1, D), F32)],
        scratch_shapes=[pltpu.VMEM((tm, D), F32)],
        compiler_params=_params(2, VMEM_LIMIT_WIDE),
    )(*_hbm(*dys, *ws, x, g_pre, dh))


def _mm_tn(a, b, bm, name, groups=None):
    ga, T, M = a.shape
    if groups is None:
        gb, _, N = b.shape
        b_spec = pl.BlockSpec((1, T, N), (lambda g, m: (g, 0, 0)) if gb > 1 else (lambda g, m: (0, 0, 0)))
    else:
        gb, N = groups, b.shape[1] // groups
        b_spec = pl.BlockSpec((T, N), lambda g, m: (0, g))
    G = max(ga, gb)

    def body(a_ref, b_ref, o_ref, narrow_ref):
        bv = b_ref[0] if groups is None else b_ref[...]
        o_ref[0] = _dot(a_ref[0].astype(BF16), bv.astype(BF16), TN)
        narrow_ref[0] = o_ref[0].astype(BF16)

    out = pl.BlockSpec((1, bm, N), lambda g, m: (g, m, 0))
    return pl.pallas_call(
        body, name=name, grid=(G, M // bm),
        in_specs=[pl.BlockSpec((1, T, bm), (lambda g, m: (g, 0, m)) if ga > 1 else (lambda g, m: (0, 0, m))), b_spec],
        out_specs=[out, out],
        out_shape=[_out((G, M, N), F32), _out((G, M, N), BF16)],
        compiler_params=_params(2, VMEM_LIMIT),
    )(*_hbm(a, b))


def _norm_proj(x, g_pre, w, name):
    T, D = x.shape
    S, _, N = w.shape
    tm = min(WIDE_ROW_BLOCK, T)

    def body(x_ref, g_ref, w_ref, o_ref, xn_ref, xn_s):
        @pl.when(pl.program_id(1) == 0)
        def _():
            xn_s[...] = _rms_fwd(x_ref[...], g_ref[...]).astype(BF16)
            xn_ref[...] = xn_s[...]

        o_ref[...] = _dot(xn_s[...], w_ref[0]).astype(BF16)

    row = pl.BlockSpec((tm, D), lambda i, k: (i, 0))
    return pl.pallas_call(
        body, name=name, grid=(T // tm, S),
        in_specs=[row, pl.BlockSpec((1, D), lambda i, k: (0, 0)), pl.BlockSpec((1, D, N), lambda i, k: (k, 0, 0))],
        out_specs=[pl.BlockSpec((tm, N), lambda i, k: (i, k)), row],
        out_shape=[_out((T, S * N), BF16), _out((T, D), BF16)],
        scratch_shapes=[pltpu.VMEM((tm, D), BF16)],
        compiler_params=_params(2, VMEM_LIMIT_WIDE),
    )(*_hbm(x, g_pre, w))


def _mix_out_fwd(h, o_a, o_b, g_sb, g_ch, w_out, g_post, name):
    T, D = h.shape
    W = g_sb.shape[1]
    tm = min(WIDE_ROW_BLOCK, T)

    def body(h_ref, oa_ref, ob_ref, gsb_ref, gch_ref, w_ref, gpost_ref, h2_ref, mixed_ref, mo_ref):
        mixed_ref[:, :W] = _rms_fwd(oa_ref[...], gsb_ref[...]).astype(BF16)
        mixed_ref[:, W:] = _rms_fwd(ob_ref[...], gch_ref[...]).astype(BF16)
        mo = _dot(mixed_ref[...], w_ref[...])
        mo_ref[...] = mo
        h2_ref[...] = h_ref[...] + _rms_fwd(mo, gpost_ref[...])

    row = pl.BlockSpec((tm, D), lambda i: (i, 0))
    part = pl.BlockSpec((tm, W), lambda i: (i, 0))
    half = pl.BlockSpec((1, W), lambda i: (0, 0))
    return pl.pallas_call(
        body, name=name, grid=(T // tm,),
        in_specs=[row, part, part, half, half, pl.BlockSpec((D, D), lambda i: (0, 0)), pl.BlockSpec((1, D), lambda i: (0, 0))],
        out_specs=[row, row, row],
        out_shape=[_out((T, D), F32), _out((T, D), BF16),
                   _out((T, D), F32)],
        compiler_params=_params(1, VMEM_LIMIT_WIDE),
    )(*_hbm(h, o_a, o_b, g_sb, g_ch, w_out, g_post))


def _mix_out_bwd(dh, mo, g_post, w_out, o_a, o_b, g_sb, g_ch, name):
    T, D = dh.shape
    W = g_sb.shape[1]
    tm = min(WIDE_ROW_BLOCK, T)

    def body(dh_ref, mo_ref, gpost_ref, w_ref, oa_ref, ob_ref, gsb_ref, gch_ref,
             dmo_ref, doa_ref, dob_ref, dgpost_ref, dgsb_ref, dgch_ref):
        first = pl.program_id(0) == 0
        dmo, dgpost = _rms_bwd(mo_ref[...], gpost_ref[...], dh_ref[...])
        dmo_ref[...] = dmo.astype(BF16)
        dmix = _dot(dmo_ref[...], w_ref[...], NT)
        doa_ref[...], dgsb = _rms_bwd(oa_ref[...], gsb_ref[...], dmix[:, :W])
        dob_ref[...], dgch = _rms_bwd(ob_ref[...], gch_ref[...], dmix[:, W:])
        _accumulate(dgpost_ref, dgpost, first)
        _accumulate(dgsb_ref, dgsb, first)
        _accumulate(dgch_ref, dgch, first)

    row = pl.BlockSpec((tm, D), lambda i: (i, 0))
    part = pl.BlockSpec((tm, W), lambda i: (i, 0))
    vec = pl.BlockSpec((1, D), lambda i: (0, 0))
    half = pl.BlockSpec((1, W), lambda i: (0, 0))
    return pl.pallas_call(
        body, name=name, grid=(T // tm,),
        in_specs=[row, row, vec, pl.BlockSpec((D, D), lambda i: (0, 0)), part, part, half, half],
        out_specs=[row, part, part, vec, half, half],
        out_shape=[_out((T, D), BF16), _out((T, W), F32),
                   _out((T, W), F32), _out((1, D), F32),
                   _out((1, W), F32), _out((1, W), F32)],
        compiler_params=_params(1, VMEM_LIMIT_WIDE),
    )(*_hbm(dh, mo, g_post, w_out, o_a, o_b, g_sb, g_ch))


def _ple_loss(h, p, target, w_proj, w_gate, g_post, name):
    T, D = h.shape
    P = p.shape[1]
    S = N_CHIPS
    C = D // S
    tm = min(ROW_BLOCK, T)

    def body(h_ref, p_ref, t_ref, wp_ref, wg_ref, g_ref, loss_ref, dh_ref, dproj_ref, dgate_ref, dgain_ref):
        first = pl.program_id(0) == 0
        h3 = h_ref[...]
        proj = _dot(p_ref[...].astype(BF16), wp_ref[...])
        s = _sigmoid(_dot(h3.astype(BF16), wg_ref[...]))
        e = proj * s
        diff = h3 + _rms_fwd(e, g_ref[...]) - t_ref[...]
        part = 0.5 * jnp.sum(jnp.mean(diff * diff, axis=-1, keepdims=True), axis=0, keepdims=True)
        _accumulate(loss_ref, jnp.broadcast_to(part, loss_ref.shape), first)
        dy = diff * (1.0 / D)
        de, dgain = _rms_bwd(e, g_ref[...], dy)
        _accumulate(dgain_ref, dgain, first)
        dproj = (de * s).astype(BF16)
        for j in range(S):
            dproj_ref[j] = dproj[:, j * C:(j + 1) * C]
        dgate_ref[...] = (de * proj * s * (1.0 - s)).astype(BF16)
        dh_ref[...] = dy + _dot(dgate_ref[...], wg_ref[...], NT)

    row = pl.BlockSpec((tm, D), lambda i: (i, 0))
    vec = pl.BlockSpec((1, D), lambda i: (0, 0))
    return pl.pallas_call(
        body, name=name, grid=(T // tm,),
        in_specs=[row, pl.BlockSpec((tm, P), lambda i: (i, 0)), row,
                  pl.BlockSpec((P, D), lambda i: (0, 0)), pl.BlockSpec((D, D), lambda i: (0, 0)), vec],
        out_specs=[pl.BlockSpec((8, 128), lambda i: (0, 0)), row,
                   pl.BlockSpec((S, tm, C), lambda i: (0, i, 0)), row, vec],
        out_shape=[_out((8, 128), F32), _out((T, D), F32),
                   _out((S, T, C), BF16), _out((T, D), BF16),
                   _out((1, D), F32)],
        compiler_params=_params(1, VMEM_LIMIT_WIDE),
    )(*_hbm(h, p, target, w_proj, w_gate, g_post))


def _sb_scores(q, kj, mask):
    z = _dot(q, kj, NT)
    sp = jnp.maximum(z, 0.0) + jnp.log(1.0 + jnp.exp(-jnp.abs(z)))
    return z, sp if mask is None else jnp.where(mask, sp, 0.0)


def _strict_causal():
    rows = lax.broadcasted_iota(jnp.int32, (SB_BLOCK, SB_BLOCK), 0)
    cols = lax.broadcasted_iota(jnp.int32, (SB_BLOCK, SB_BLOCK), 1)
    return cols < rows


def _tri(cmp):
    r = lax.broadcasted_iota(jnp.int32, (2 * SB_BLOCK, SB_BLOCK), 0) % SB_BLOCK
    c = lax.broadcasted_iota(jnp.int32, (2 * SB_BLOCK, SB_BLOCK), 1)
    return jnp.where(cmp(r, c), 1.0, 0.0).astype(BF16)


def _cum(x, tri):
    return _dot(jnp.concatenate(_split2(x), axis=1), tri)


def _pair_lanes():
    lane = lax.broadcasted_iota(jnp.int32, (1, PAIR), 1)
    return [lane < HEAD_DIM, lane >= HEAD_DIM]


def _only(lanes, x):
    return jnp.where(lanes, x, jnp.zeros_like(x))


def _sb_fwd(qkv, name):
    T = qkv.shape[0]
    B = SB_BLOCK
    W = SB_PAIRS * PAIR
    steps = N_HEADS // (2 * SB_PAIRS)
    heads = [(p, h) for p in range(SB_PAIRS) for h in range(2)]

    def body(q_ref, k_ref, v_ref, o_ref):
        i = pl.program_id(1)
        after = _tri(lambda r, c: r > c)
        lanes = _pair_lanes()
        cols = [slice(p * PAIR, (p + 1) * PAIR) for p in range(SB_PAIRS)]
        q = {(p, h): _only(lanes[h], q_ref[:, cols[p]] * ATT_SCALE) for p, h in heads}

        def tiles(j, carries, mask):
            at = pl.ds(pl.multiple_of(j * B, B), B)
            scores = [_sb_scores(q[ph], k_ref[at, cols[ph[0]]], mask) for ph in heads]
            laters = [_cum(sp, after) for _, sp in scores]
            out = []
            for ph, (z, sp), later, (run, acc) in zip(heads, scores, laters, carries):
                a = jnp.exp(z - sp - later - run)
                if mask is not None:
                    a = jnp.where(mask, a, 0.0)
                out.append((run + later[:, 0:1] + sp[:, 0:1],
                            acc + _dot(a.astype(BF16), _only(lanes[ph[1]], v_ref[at, cols[ph[0]]]))))
            return tuple(out)

        zero = (jnp.zeros((B, 1), F32), jnp.zeros((B, PAIR), F32))
        carries = tiles(i, (zero,) * len(heads), _strict_causal())
        carries = lax.fori_loop(0, i, lambda jj, cs: tiles(i - 1 - jj, cs, None), carries)
        for p in range(SB_PAIRS):
            o_ref[:, cols[p]] = carries[2 * p][1] + carries[2 * p + 1][1]

    blk = lambda off: pl.BlockSpec((B, W), lambda g, i: (i, g + off))
    full = lambda off: pl.BlockSpec((T, W), lambda g, i: (0, g + off))
    return pl.pallas_call(
        body, name=name, grid=(steps, T // B),
        in_specs=[blk(0), full(steps), full(2 * steps)],
        out_specs=blk(0),
        out_shape=_out((T, N_HEADS * HEAD_DIM), F32),
        compiler_params=_params(2, VMEM_LIMIT),
    )(*_hbm(qkv, qkv, qkv))


def _sb_bwd(qkv, do, o, name):
    T = qkv.shape[0]
    B = SB_BLOCK
    W = SB_PAIRS * PAIR
    steps = N_HEADS // (2 * SB_PAIRS)
    n_blocks = T // B
    heads = [(p, h) for p in range(SB_PAIRS) for h in range(2)]

    def body(q_ref, k_ref, v_ref, do_ref, o_ref, dq_ref, dk_ref, dv_ref, dk_s, dv_s):
        i = pl.program_id(1)

        @pl.when(i == 0)
        def _():
            dk_s[...] = jnp.zeros_like(dk_s)
            dv_s[...] = jnp.zeros_like(dv_s)

        after = _tri(lambda r, c: r > c)
        since = _tri(lambda r, c: r >= c)
        lanes = _pair_lanes()
        cols = [slice(p * PAIR, (p + 1) * PAIR) for p in range(SB_PAIRS)]
        q = {(p, h): _only(lanes[h], q_ref[:, cols[p]] * ATT_SCALE) for p, h in heads}
        do = {(p, h): _only(lanes[h], do_ref[:, cols[p]].astype(BF16)) for p, h in heads}
        total = {ph: jnp.sum(do[ph].astype(F32) * o_ref[:, cols[ph[0]]], axis=1, keepdims=True) for ph in heads}

        def tiles(j, carries, mask):
            at = pl.ds(pl.multiple_of(j * B, B), B)
            ks = [k_ref[at, c] for c in cols]
            vs = [v_ref[at, c] for c in cols]
            scores = [_sb_scores(q[ph], ks[ph[0]], mask) for ph in heads]
            laters = [_cum(sp, after) for _, sp in scores]
            das = [_dot(do[ph], vs[ph[0]], NT) for ph in heads]
            a_s, gs = [], []
            for (z, sp), later, da, carry in zip(scores, laters, das, carries):
                a = jnp.exp(z - sp - later - carry[0])
                if mask is not None:
                    a = jnp.where(mask, a, 0.0)
                a = a.astype(BF16)
                a_s.append(a)
                gs.append(a.astype(F32) * da)
            sinces = [_cum(g, since) for g in gs]
            dzs = []
            for ph, (_, sp), g, from_s, carry in zip(heads, scores, gs, sinces, carries):
                g_before = total[ph] - carry[1] - from_s
                fail = jnp.exp(-sp)
                dz = fail * (g + g_before) - g_before
                if mask is not None:
                    dz = jnp.where(mask, dz, 0.0)
                dzs.append(dz.astype(BF16))
            out = []
            for ph, (_, sp), a, dz, later, from_s, carry in zip(heads, scores, a_s, dzs, laters, sinces, carries):
                dk_s[at, cols[ph[0]]] += _dot(dz, q[ph], TN)
                dv_s[at, cols[ph[0]]] += _dot(a, do[ph], TN)
                out.append((carry[0] + later[:, 0:1] + sp[:, 0:1], carry[1] + from_s[:, 0:1],
                            carry[2] + _dot(dz, _only(lanes[ph[1]], ks[ph[0]]))))
            return tuple(out)

        col = jnp.zeros((B, 1), F32)
        zero = (col, col, jnp.zeros((B, PAIR), F32))
        carries = tiles(i, (zero,) * len(heads), _strict_causal())
        last = lax.fori_loop(0, i, lambda jj, cs: tiles(i - 1 - jj, cs, None), carries)
        for p in range(SB_PAIRS):
            dq_ref[:, cols[p]] = ((last[2 * p][2] + last[2 * p + 1][2]) * ATT_SCALE).astype(BF16)

        @pl.when(i == n_blocks - 1)
        def _():
            dk_ref[...] = dk_s[...].astype(BF16)
            dv_ref[...] = dv_s[...].astype(BF16)

    blk = lambda off: pl.BlockSpec((B, W), lambda g, i: (i, g + off))
    full = lambda off: pl.BlockSpec((T, W), lambda g, i: (0, g + off))
    out = _out((T, N_HEADS * HEAD_DIM), BF16)
    return pl.pallas_call(
        body, name=name, grid=(steps, n_blocks),
        in_specs=[blk(0), full(steps), full(2 * steps), blk(0), blk(0)],
        out_specs=[blk(0), full(0), full(0)],
        out_shape=[out, out, out],
        scratch_shapes=[pltpu.VMEM((T, W), F32)] * 2,
        compiler_params=_params(2, VMEM_LIMIT),
    )(*_hbm(qkv, qkv, qkv, do, o))


NEAR = BAND - PAD + REL_CLIP
FAR = BAND - NEAR
NEAR_REL = 2 * REL_CLIP
BIAS_ROWS = 8


def _rel_onehot(i, transposed):
    shape = (NEAR, NEAR_REL) if transposed else (NEAR_REL, NEAR)
    j = FAR + lax.broadcasted_iota(jnp.int32, shape, 0 if transposed else 1)
    r = lax.broadcasted_iota(jnp.int32, shape, 1 if transposed else 0)
    idx = jnp.clip(i + PAD - j, -REL_CLIP, REL_CLIP) + REL_CLIP
    return jnp.where(idx - 1 == r, 1.0, 0.0).astype(BF16)


def _bias_table(rel_bias, name):
    def body(near_ref, far_ref, o_ref):
        rb = near_ref[...]
        hi, lo = _split2(rb)
        lo2 = (rb - hi.astype(F32) - lo.astype(F32)).astype(BF16)
        far = jnp.broadcast_to(far_ref[...], (N_HEADS, FAR))
        for k in range(BIAS_ROWS):
            onehot = _rel_onehot(pl.program_id(0) * BIAS_ROWS + k, False)
            o_ref[k, :, :FAR] = far
            o_ref[k, :, FAR:] = _dot(hi, onehot) + _dot(lo, onehot) + _dot(lo2, onehot)

    return pl.pallas_call(
        body, name=name, grid=(CHUNK // BIAS_ROWS,),
        in_specs=[pl.BlockSpec((N_HEADS, NEAR_REL), lambda i: (0, 0)), pl.BlockSpec((N_HEADS, 1), lambda i: (0, 0))],
        out_specs=pl.BlockSpec((BIAS_ROWS, N_HEADS, BAND), lambda i: (i, 0, 0)),
        out_shape=_out((CHUNK, N_HEADS, BAND), F32),
        compiler_params=_params(1),
    )(*_hbm(rel_bias[:, 1:], rel_bias[:, N_REL - 1:]))


def _bias_grad(dbias_t, name):
    def body(d_ref, near_ref, far_ref):
        near, far = None, None
        for k in range(BIAS_ROWS):
            onehot = _rel_onehot(pl.program_id(0) * BIAS_ROWS + k, True)
            hi, lo = _split2(d_ref[k, :, FAR:])
            part = _dot(hi, onehot) + _dot(lo, onehot)
            rest = jnp.sum(d_ref[k, :, :FAR], axis=1, keepdims=True)
            near, far = (part, rest) if near is None else (near + part, far + rest)
        first = pl.program_id(0) == 0
        _accumulate(near_ref, near, first)
        _accumulate(far_ref, jnp.broadcast_to(far, far_ref.shape), first)

    near, far = pl.pallas_call(
        body, name=name, grid=(CHUNK // BIAS_ROWS,),
        in_specs=[pl.BlockSpec((BIAS_ROWS, N_HEADS, BAND), lambda i: (i, 0, 0))],
        out_specs=[pl.BlockSpec((N_HEADS, NEAR_REL), lambda i: (0, 0)), pl.BlockSpec((N_HEADS, 128), lambda i: (0, 0))],
        out_shape=[_out((N_HEADS, NEAR_REL), F32), _out((N_HEADS, 128), F32)],
        compiler_params=_params(1),
    )(*_hbm(dbias_t))
    return jnp.pad(near, ((0, 0), (1, 0))).at[:, N_REL - 1].add(far[:, 0])


def _ch_probs(scores, bias, valid):
    z = jnp.where(valid, scores * ATT_SCALE + bias, NEG_INF)
    e = jnp.exp(z - jnp.max(z, axis=-1, keepdims=True))
    return e / jnp.sum(e, axis=-1, keepdims=True)


CH_HEADS = [(pair, h) for pair in range(N_HEADS // 2) for h in range(2)]
CH_COLS = [slice(pair * PAIR, (pair + 1) * PAIR) for pair in range(N_HEADS // 2)]


CH_GROUP = 2
CH_Q = CH_GROUP * CHUNK
CH_WIN = (LOOKBACK + CH_GROUP) * CHUNK


def _ch_valid(n):
    row_chunk = lax.broadcasted_iota(jnp.int32, (CH_Q, CH_WIN), 0) // CHUNK
    slot = lax.broadcasted_iota(jnp.int32, (CH_Q, CH_WIN), 1)
    ahead = slot // CHUNK - row_chunk
    return (ahead >= 0) & (ahead <= LOOKBACK) & (n * CH_Q + slot >= PAD)


def _ch_group_bias(bias):
    shifted = [jnp.pad(bias, ((0, 0), (0, 0), (c * CHUNK, (CH_GROUP - 1 - c) * CHUNK))) for c in range(CH_GROUP)]
    return jnp.concatenate(shifted, axis=1)


def _ch_fold_bias_grad(dbias):
    parts = [dbias[:, c * CHUNK:(c + 1) * CHUNK, c * CHUNK:c * CHUNK + BAND] for c in range(CH_GROUP)]
    return sum(parts[1:], parts[0])


def _ch_fwd(qkv, bias, name):
    T = qkv.shape[0]
    W = N_HEADS * HEAD_DIM

    def body(q_ref, k_ref, v_ref, b_ref, o_ref, kp, vp):
        n = pl.program_id(0)

        @pl.when(n == 0)
        def _():
            _ch_load_padded(k_ref, v_ref, kp, vp)

        win = pl.ds(pl.multiple_of(n * CH_Q, CH_Q), CH_WIN)
        valid = _ch_valid(n)
        lanes = _pair_lanes()
        scores = [_dot(_only(lanes[h], q_ref[:, CH_COLS[pair]]), kp[win, CH_COLS[pair]], NT) for pair, h in CH_HEADS]
        probs = [_ch_probs(s, b_ref[2 * pair + h], valid).astype(BF16) for s, (pair, h) in zip(scores, CH_HEADS)]
        outs = [_dot(p, _only(lanes[h], vp[win, CH_COLS[pair]])) for p, (pair, h) in zip(probs, CH_HEADS)]
        for pair, cols in enumerate(CH_COLS):
            o_ref[:, cols] = outs[2 * pair] + outs[2 * pair + 1]

    full = lambda col: pl.BlockSpec((T, W), lambda n: (0, col))
    return pl.pallas_call(
        body, name=name, grid=(T // CH_Q,),
        in_specs=[pl.BlockSpec((CH_Q, W), lambda n: (n, 3)), full(4), full(5),
                  pl.BlockSpec((N_HEADS, CH_Q, CH_WIN), lambda n: (0, 0, 0))],
        out_specs=pl.BlockSpec((CH_Q, W), lambda n: (n, 0)),
        out_shape=_out((T, W), F32),
        scratch_shapes=[pltpu.VMEM((PAD + T, W), BF16)] * 2,
        compiler_params=_params(1, VMEM_LIMIT),
    )(*_hbm(qkv, qkv, qkv, bias))


def _ch_load_padded(k_ref, v_ref, kp, vp):
    for src, dst in ((k_ref, kp), (v_ref, vp)):
        dst[:PAD, :] = jnp.zeros((PAD, dst.shape[1]), dst.dtype)
        dst[PAD:, :] = src[...]


def _ch_bwd(qkv, bias, do, name):
    T = qkv.shape[0]
    W = N_HEADS * HEAD_DIM
    n_chunks = T // CH_Q

    def body(q_ref, k_ref, v_ref, b_ref, do_ref, dq_ref, dk_ref, dv_ref, db_ref, kp, vp, dk_s, dv_s):
        n = pl.program_id(0)

        @pl.when(n == 0)
        def _():
            _ch_load_padded(k_ref, v_ref, kp, vp)
            dk_s[...] = jnp.zeros_like(dk_s)
            dv_s[...] = jnp.zeros_like(dv_s)
            db_ref[...] = jnp.zeros_like(db_ref)

        win = pl.ds(pl.multiple_of(n * CH_Q, CH_Q), CH_WIN)
        valid = _ch_valid(n)
        lanes = _pair_lanes()
        kws = [kp[win, cols] for cols in CH_COLS]
        vws = [vp[win, cols] for cols in CH_COLS]
        qs = [_only(lanes[h], q_ref[:, CH_COLS[pair]]) for pair, h in CH_HEADS]
        dos = [_only(lanes[h], do_ref[:, CH_COLS[pair]].astype(BF16)) for pair, h in CH_HEADS]
        scores = [_dot(q, kws[pair], NT) for q, (pair, _) in zip(qs, CH_HEADS)]
        dps = [_dot(do, vws[pair], NT) for do, (pair, _) in zip(dos, CH_HEADS)]
        probs = [_ch_probs(s, b_ref[2 * pair + h], valid) for s, (pair, h) in zip(scores, CH_HEADS)]
        dzs = [p * (dp - jnp.sum(dp * p, axis=-1, keepdims=True)) for p, dp in zip(probs, dps)]
        for k, dz in enumerate(dzs):
            db_ref[k] += dz
        dzbs = [(dz * ATT_SCALE).astype(BF16) for dz in dzs]
        dqs = [_dot(dz, _only(lanes[h], kws[pair])) for dz, (pair, h) in zip(dzbs, CH_HEADS)]
        dks = [_dot(dz, q, TN) for dz, q in zip(dzbs, qs)]
        dvs = [_dot(p.astype(BF16), do, TN) for p, do in zip(probs, dos)]
        for pair, cols in enumerate(CH_COLS):
            dq_ref[:, cols] = (dqs[2 * pair] + dqs[2 * pair + 1]).astype(BF16)
            dk_s[win, cols] += dks[2 * pair] + dks[2 * pair + 1]
            dv_s[win, cols] += dvs[2 * pair] + dvs[2 * pair + 1]

        @pl.when(n == n_chunks - 1)
        def _():
            dk_ref[...] = dk_s[PAD:, :].astype(BF16)
            dv_ref[...] = dv_s[PAD:, :].astype(BF16)

    full = lambda col: pl.BlockSpec((T, W), lambda n: (0, col))
    blk = lambda col: pl.BlockSpec((CH_Q, W), lambda n: (n, col))
    tab = pl.BlockSpec((N_HEADS, CH_Q, CH_WIN), lambda n: (0, 0, 0))
    out = _out((T, W), BF16)
    return pl.pallas_call(
        body, name=name, grid=(n_chunks,),
        in_specs=[blk(3), full(4), full(5), tab, blk(0)],
        out_specs=[blk(0), full(0), full(0), tab],
        out_shape=[out, out, out, _out((N_HEADS, CH_Q, CH_WIN), F32)],
        scratch_shapes=[pltpu.VMEM((PAD + T, W), BF16)] * 2 + [pltpu.VMEM((PAD + T, W), F32)] * 2,
        compiler_params=_params(1, VMEM_LIMIT),
    )(*_hbm(qkv, qkv, qkv, bias, do))


def _rows_split(a, parts):
    return a.reshape(a.shape[:-2] + (parts, a.shape[-2] // parts, a.shape[-1]))


def _cast_into_slot0(c, ws, name):
    parts = 2
    ws = [_rows_split(_rows_split(w, 2), parts) for w in ws]
    n = len(ws)

    def body(c_ref, *refs):
        for src, dst in zip(refs[:n], refs[n:]):
            dst[0, 0, 0] = src[0, 0].astype(BF16)

    outs = pl.pallas_call(
        body, name=name,
        grid_spec=pltpu.PrefetchScalarGridSpec(
            num_scalar_prefetch=1, grid=(2, parts),
            in_specs=[pl.BlockSpec((1, 1) + w.shape[2:], lambda d, r, c_ref: (d ^ c_ref[0], r, 0, 0)) for w in ws],
            out_specs=[pl.BlockSpec((1, 1, 1) + w.shape[2:], lambda d, r, c_ref: (0, d, r, 0, 0)) for w in ws]),
        out_shape=[_out((N_CHIPS,) + w.shape, BF16) for w in ws],
        compiler_params=_params(2, VMEM_LIMIT),
    )(c, *_hbm(*ws))
    return [o.reshape(N_CHIPS, 2, o.shape[2] * o.shape[3], o.shape[4]) for o in outs]


def _chip_order(me, c, lands, name):
    parts = 2
    xs = [_rows_split(x, parts) for x in lands]

    def body(me_ref, c_ref, *refs):
        n = len(refs) // 2
        for src, dst in zip(refs[:n], refs[n:]):
            dst[...] = src[...]

    outs = pl.pallas_call(
        body, name=name,
        grid_spec=pltpu.PrefetchScalarGridSpec(
            num_scalar_prefetch=2, grid=(N_CHIPS, 2, parts),
            in_specs=[pl.BlockSpec((1, 1, 1) + x.shape[3:],
                                   lambda j, h, r, me_ref, c_ref: (j ^ me_ref[0], h ^ c_ref[0], r, 0, 0)) for x in xs],
            out_specs=[pl.BlockSpec((1, 1, 1) + x.shape[3:], lambda j, h, r, me_ref, c_ref: (j, h, r, 0, 0))
                       for x in xs]),
        out_shape=[_out(x.shape, x.dtype) for x in xs],
        compiler_params=_params(3, VMEM_LIMIT),
    )(me, c, *_hbm(*xs))
    return [o.reshape(o.shape[0], 2 * parts * o.shape[3], o.shape[4]) for o in outs]


def _pair_add(c, mine, got, permuted, name):
    parts = 2
    mine = [_rows_split(m, parts) for m in mine]
    got = [_rows_split(g, parts) for g in got]
    n = len(mine)

    def body(c_ref, *refs):
        for a, b, o in zip(refs[:n], refs[n:2 * n], refs[2 * n:]):
            o[0, 0] = (a[0, 0, 0] + b[0, 0].astype(F32)).astype(BF16)

    def mine_spec(m, perm):
        if perm:
            return pl.BlockSpec((1, 1, 1) + m.shape[3:], lambda j, r, c_ref: (j, 0, r, 0, 0))
        return pl.BlockSpec((1, 1, 1) + m.shape[3:], lambda j, r, c_ref: (j, c_ref[0], r, 0, 0))

    def got_spec(g):
        return pl.BlockSpec((1, 1) + g.shape[2:], lambda j, r, c_ref: (j, r, 0, 0))

    outs = pl.pallas_call(
        body, name=name,
        grid_spec=pltpu.PrefetchScalarGridSpec(
            num_scalar_prefetch=1, grid=(N_CHIPS, parts),
            in_specs=[mine_spec(m, perm) for m, perm in zip(mine, permuted)] + [got_spec(g) for g in got],
            out_specs=[got_spec(g) for g in got]),
        out_shape=[_out(g.shape, BF16) for g in got],
        compiler_params=_params(2, VMEM_LIMIT),
    )(c, *_hbm(*mine, *got))
    return [o.reshape(o.shape[0], o.shape[1] * o.shape[2], o.shape[3]) for o in outs]


def _chip_add(me, partials, landed, permuted, name):
    parts = 2
    ps = [_rows_split(x, parts) for x in partials]
    ls = [_rows_split(x, parts) for x in landed]
    n = len(ps)

    def body(me_ref, *refs):
        for own, got, o in zip(refs[:n], refs[n:2 * n], refs[2 * n:]):
            acc = own[0, 0].astype(F32)
            for r in range(N_CHIPS - 1):
                acc = acc + got[r, 0].astype(F32)
            o[0] = acc

    def own_spec(x, perm):
        if perm:
            return pl.BlockSpec((1, 1) + x.shape[2:], lambda r, me_ref: (0, r, 0, 0))
        return pl.BlockSpec((1, 1) + x.shape[2:], lambda r, me_ref: (me_ref[0], r, 0, 0))

    outs = pl.pallas_call(
        body, name=name,
        grid_spec=pltpu.PrefetchScalarGridSpec(
            num_scalar_prefetch=1, grid=(parts,),
            in_specs=[own_spec(x, perm) for x, perm in zip(ps, permuted)]
            + [pl.BlockSpec((N_CHIPS - 1, 1) + x.shape[2:], lambda r, me_ref: (0, r, 0, 0)) for x in ls],
            out_specs=[pl.BlockSpec((1,) + x.shape[2:], lambda r, me_ref: (r, 0, 0)) for x in ps]),
        out_shape=[_out(x.shape[1:], F32) for x in ps],
        compiler_params=_params(1, VMEM_LIMIT),
    )(me, *_hbm(*ps, *ls))
    return [o.reshape(o.shape[0] * o.shape[1], o.shape[2]) for o in outs]


def _adamw_math(w, g, m, v):
    m = ADAM_B1 * m + (1.0 - ADAM_B1) * g
    v = ADAM_B2 * v + (1.0 - ADAM_B2) * (g * g)
    m_hat = m / (1.0 - ADAM_B1 ** ADAM_STEP)
    v_hat = v / (1.0 - ADAM_B2 ** ADAM_STEP)
    delta = -ADAM_LR * (m_hat / (jnp.sqrt(v_hat) + ADAM_EPS) + ADAM_WD * w)
    return delta, m, v


def _adamw(ws, gs, ms, vs, parts, name):
    n = len(ws)
    flat = [_rows_split(a, parts) for a in (*ws, *gs, *ms, *vs)]

    def body(*refs):
        ins, outs = refs[:4 * n], refs[4 * n:]
        for k in range(n):
            d, m, v = _adamw_math(ins[k][...], ins[n + k][...], ins[2 * n + k][...], ins[3 * n + k][...])
            outs[k][...] = d
            outs[n + k][...] = m
            outs[2 * n + k][...] = v

    spec = lambda a: pl.BlockSpec((1,) + a.shape[1:], lambda i: (i, 0, 0))
    outs = pl.pallas_call(
        body, name=name, grid=(parts,),
        in_specs=[spec(a) for a in flat], out_specs=[spec(a) for a in flat[:n]] * 3,
        out_shape=[_out(a.shape, F32) for a in flat[:n]] * 3,
        compiler_params=_params(1, VMEM_LIMIT),
    )(*_hbm(*flat))
    outs = [o.reshape(o.shape[0] * o.shape[1], o.shape[2]) for o in outs]
    return outs[:n], outs[n:2 * n], outs[2 * n:]


def _adamw_halves(c, ws, owns, others, ms, vs, name):
    parts = 4
    n = len(ws)
    whole = [_rows_split(_rows_split(a, 2), parts) for a in (*ws, *ms, *vs)]
    halves = [_rows_split(a, parts) for a in (*owns, *others)]

    def body(c_ref, *refs):
        ins, outs = refs[:5 * n], refs[5 * n:]
        mine = pl.program_id(0) == c_ref[0]
        for k in range(n):
            g = jnp.where(mine, ins[3 * n + k][0], ins[4 * n + k][0])
            d, m, v = _adamw_math(ins[k][0, 0], g, ins[n + k][0, 0], ins[2 * n + k][0, 0])
            for slot, val in enumerate((g, d, m, v)):
                outs[slot * n + k][0, 0] = val

    wspec = lambda a: pl.BlockSpec((1, 1) + a.shape[2:], lambda h, r, c_ref: (h, r, 0, 0))
    hspec = lambda a: pl.BlockSpec((1,) + a.shape[1:], lambda h, r, c_ref: (r, 0, 0))
    outs = pl.pallas_call(
        body, name=name,
        grid_spec=pltpu.PrefetchScalarGridSpec(
            num_scalar_prefetch=1, grid=(2, parts),
            in_specs=[wspec(a) for a in whole] + [hspec(a) for a in halves],
            out_specs=[wspec(a) for a in whole[:n]] * 4),
        out_shape=[_out(a.shape, F32) for a in whole[:n]] * 4,
        compiler_params=_params(2, VMEM_LIMIT),
    )(c, *_hbm(*whole, *halves))
    outs = [o.reshape(2 * parts * o.shape[2], o.shape[3]) for o in outs]
    return outs[:n], outs[n:2 * n], outs[2 * n:3 * n], outs[3 * n:]


def _place():
    x, y, c = lax.axis_index("x"), lax.axis_index("y"), lax.axis_index("c")
    peers = [(x ^ (r >> 1), y ^ (r & 1), c) for r in (1, 2, 3)]
    return x, y, c, peers


def _handshake(peers):
    barrier = pltpu.get_barrier_semaphore()
    for peer in peers:
        pl.semaphore_signal(barrier, inc=1, device_id=peer, device_id_type=MESH)
    pl.semaphore_wait(barrier, len(peers))


ANY = pl.BlockSpec(memory_space=pl.ANY)
HBM = pl.BlockSpec(memory_space=pltpu.HBM)
SEM = pl.BlockSpec(memory_space=pltpu.SEMAPHORE)
SPLIT_COPY = pltpu.SideEffectType.DATAFLOW_SIDE_EFFECTING


def _in_hbm(a):
    return pltpu.with_memory_space_constraint(a, pltpu.HBM)


def _split_start(body, name, collective_id, operands, n_sems, after=None):
    n = len(operands)
    extra = [] if after is None else [after]

    def wrapped(*refs):
        at = n + len(extra)
        body(refs[:n], refs[at], refs[at + 1])
        token = refs[-1]
        token[...] = jnp.zeros_like(token)

    outs = pl.pallas_call(
        wrapped, name=name,
        in_specs=[HBM] * n + [ANY] * len(extra),
        out_shape=(pltpu.SemaphoreType.DMA((n_sems,)), pltpu.SemaphoreType.DMA((n_sems,)),
                   *[pltpu.HBM(a.shape, a.dtype) for a in operands], jax.ShapeDtypeStruct((8, 128), F32)),
        out_specs=(SEM, SEM, *[HBM] * n, pl.BlockSpec(memory_space=pltpu.VMEM)),
        input_output_aliases={i: 2 + i for i in range(n)},
        compiler_params=pltpu.CompilerParams(has_side_effects=SPLIT_COPY, collective_id=collective_id),
    )(*[_in_hbm(a) for a in operands], *extra)
    return outs[0], outs[1], list(outs[2:2 + n]), outs[-1]


def _split_wait(body, name, send_sem, recv_sem, operands, after):
    n = len(operands)

    def wrapped(*refs):
        body(refs[:n], refs[n], refs[n + 1])

    outs = pl.pallas_call(
        wrapped, name=name,
        in_specs=[HBM] * n + [SEM, SEM, ANY],
        out_shape=tuple(pltpu.HBM(a.shape, a.dtype) for a in operands),
        out_specs=tuple([HBM] * n),
        input_output_aliases={i: i for i in range(n)},
        compiler_params=pltpu.CompilerParams(has_side_effects=SPLIT_COPY),
    )(*operands, send_sem, recv_sem, after)
    return list(outs)


def _gather_copies(lands, send_sem, recv_sem):
    peers = _place()[3]
    return [pltpu.make_async_remote_copy(
        src_ref=land.at[0, 0], dst_ref=land.at[r + 1, 0],
        send_sem=send_sem.at[a * 3 + r], recv_sem=recv_sem.at[a * 3 + r],
        device_id=peers[r], device_id_type=MESH) for a, land in enumerate(lands) for r in range(3)]


def _gather_start(lands, name, collective_id, after):
    def body(refs, send_sem, recv_sem):
        _handshake(_place()[3])
        for cp in _gather_copies(refs, send_sem, recv_sem):
            cp.start()

    return _split_start(body, name, collective_id, list(lands), 3 * len(lands), after)


def _gather_wait(send_sem, recv_sem, operands, after, name):
    def body(refs, send_sem, recv_sem):
        for cp in _gather_copies(refs, send_sem, recv_sem):
            cp.wait_send()
            cp.wait_recv()

    return _split_wait(body, name, send_sem, recv_sem, operands, after)


def _gather_finish(lands, with_ici, name):
    n = len(lands)

    def body(*refs):
        land = refs[n:2 * n]
        send_ici, recv_ici, send_d2d, recv_d2d = refs[2 * n:]
        x, y, c, _ = _place()
        ici = _gather_copies(land, send_ici, recv_ici) if with_ici else []
        for cp in ici:
            cp.start()
        passed = [pltpu.make_async_remote_copy(
            src_ref=land[a].at[r + 1, 0], dst_ref=land[a].at[r + 1, 1],
            send_sem=send_d2d.at[a * 3 + r], recv_sem=recv_d2d.at[a * 3 + r],
            device_id=(x, y, 1 - c), device_id_type=MESH) for a in range(n) for r in range(3)]
        for k, cp in enumerate(passed):
            if with_ici:
                ici[k].wait_recv()
            cp.start()
        for cp in passed:
            cp.wait_recv()
        for cp in ici:
            cp.wait_send()
        for cp in passed:
            cp.wait_send()

    outs = pl.pallas_call(
        body, name=name,
        in_specs=[ANY] * n, out_specs=[ANY] * n,
        out_shape=[_out(l.shape, l.dtype) for l in lands],
        input_output_aliases={a: a for a in range(n)},
        scratch_shapes=[pltpu.SemaphoreType.DMA((3 * n,))] * 4,
    )(*lands)
    return list(outs)


def _slabs(land):
    return land.reshape(N_CHIPS, 2 * land.shape[2], land.shape[3])


def _pair_swap(grads, permuted, name):
    n = len(grads)

    def body(*refs):
        src, dst = refs[:n], refs[n:2 * n]
        send_sem, recv_sem = refs[2 * n:]
        x, y, c, _ = _place()
        copies = [pltpu.make_async_remote_copy(
            src_ref=src[a].at[:, 1] if permuted[a] else src[a].at[:, 1 - c], dst_ref=dst[a],
            send_sem=send_sem.at[a], recv_sem=recv_sem.at[a],
            device_id=(x, y, 1 - c), device_id_type=MESH) for a in range(n)]
        for cp in copies:
            cp.start()
        for cp in copies:
            cp.wait()

    return pl.pallas_call(
        body, name=name,
        in_specs=[ANY] * n, out_specs=[ANY] * n,
        out_shape=[_out((N_CHIPS,) + g.shape[2:], g.dtype) for g in grads],
        scratch_shapes=[pltpu.SemaphoreType.DMA((n,))] * 2,
    )(*grads)


def _swap_copies(refs, permuted, send_sem, recv_sem):
    n = len(refs) // 2
    x, y, c, _ = _place()
    return [pltpu.make_async_remote_copy(
        src_ref=refs[a].at[:, 1] if permuted[a] else refs[a].at[:, 1 - c], dst_ref=refs[n + a],
        send_sem=send_sem.at[a], recv_sem=recv_sem.at[a],
        device_id=(x, y, 1 - c), device_id_type=MESH) for a in range(n)]


def _pair_swap_start(grads, permuted, name, collective_id):
    def body(refs, send_sem, recv_sem):
        x, y, c, _ = _place()
        _handshake([(x, y, 1 - c)])
        for cp in _swap_copies(refs, permuted, send_sem, recv_sem):
            cp.start()

    lands = [lax.empty((N_CHIPS,) + g.shape[2:], g.dtype) for g in grads]
    return _split_start(body, name, collective_id, list(grads) + lands, len(grads))


def _pair_swap_wait(send_sem, recv_sem, operands, permuted, after, name):
    def body(refs, send_sem, recv_sem):
        for cp in _swap_copies(refs, permuted, send_sem, recv_sem):
            cp.wait_send()
            cp.wait_recv()

    return _split_wait(body, name, send_sem, recv_sem, operands, after)


def _scatter_copies(refs, permuted, send_sem, recv_sem):
    n = len(refs) // 2
    x, y, _, peers = _place()
    me = 2 * x + y
    return [pltpu.make_async_remote_copy(
        src_ref=refs[a].at[r + 1] if permuted[a] else refs[a].at[me ^ (r + 1)], dst_ref=refs[n + a].at[r],
        send_sem=send_sem.at[a * 3 + r], recv_sem=recv_sem.at[a * 3 + r],
        device_id=peers[r], device_id_type=MESH) for a in range(n) for r in range(3)]


def _scatter_start(partials, permuted, name, collective_id):
    def body(refs, send_sem, recv_sem):
        _handshake(_place()[3])
        for cp in _scatter_copies(refs, permuted, send_sem, recv_sem):
            cp.start()

    lands = [lax.empty((N_CHIPS - 1,) + p.shape[1:], p.dtype) for p in partials]
    return _split_start(body, name, collective_id, list(partials) + lands, 3 * len(partials))


def _scatter_wait(send_sem, recv_sem, operands, permuted, after, name):
    def body(refs, send_sem, recv_sem):
        for cp in _scatter_copies(refs, permuted, send_sem, recv_sem):
            cp.wait_send()
            cp.wait_recv()

    return _split_wait(body, name, send_sem, recv_sem, operands, after)


def _pair_join(halves, name):
    n = len(halves)

    def body(*refs):
        src, dst = refs[:n], refs[n:2 * n]
        send_sem, recv_sem = refs[2 * n:]
        x, y, c, _ = _place()
        copies = [pltpu.make_async_remote_copy(
            src_ref=src[a], dst_ref=dst[a], send_sem=send_sem.at[a], recv_sem=recv_sem.at[a],
            device_id=(x, y, 1 - c), device_id_type=MESH) for a in range(n)]
        for cp in copies:
            cp.start()
        for cp in copies:
            cp.wait()

    return pl.pallas_call(
        body, name=name,
        in_specs=[ANY] * n, out_specs=[ANY] * n,
        out_shape=[_out(h.shape, F32) for h in halves],
        scratch_shapes=[pltpu.SemaphoreType.DMA((n,))] * 2,
    )(*halves)


def _all_sum_small(v, after, name):
    R, C = v.shape
    n_dev = 8

    def body(v_ref, after_ref, o_ref, buf, send_sem, recv_sem):
        x, y, c, _ = _place()
        me = 4 * x + 2 * y + c
        buf[me] = v_ref[...]
        copies = []
        for k in range(1, n_dev):
            peer = (x ^ (k >> 2), y ^ ((k >> 1) & 1), c ^ (k & 1))
            copies.append(pltpu.make_async_remote_copy(
                src_ref=v_ref, dst_ref=buf.at[me], send_sem=send_sem.at[k - 1], recv_sem=recv_sem.at[k - 1],
                device_id=peer, device_id_type=MESH))
        for cp in copies:
            cp.start()
        for cp in copies:
            cp.wait()
        acc = buf[0]
        for m in range(1, n_dev):
            acc = acc + buf[m]
        o_ref[...] = acc

    return pl.pallas_call(
        body, name=name,
        in_specs=[pl.BlockSpec(memory_space=pltpu.VMEM), ANY], out_specs=pl.BlockSpec(memory_space=pltpu.VMEM),
        out_shape=jax.ShapeDtypeStruct((R, C), F32),
        scratch_shapes=[pltpu.VMEM((n_dev, R, C), F32), pltpu.SemaphoreType.DMA((n_dev - 1,)),
                        pltpu.SemaphoreType.DMA((n_dev - 1,))],
    )(v, after)


class _WholeWeights:
    def __init__(self, w):
        self.w = w

    def weights(self, group, after=None):
        return self.w, None

    def grads_ready(self, group, gw):
        return None

    def grads_sent(self, group, after):
        return None


def _local_step(x, p, target, gains, rel_bias, hooks):
    T, D = x.shape
    S = N_CHIPS

    tied = lambda gain, token: gain if token is None else gain + token[0, 0]
    w, token = hooks.weights("first")
    w = dict(w)
    h1, xn1, g1, u1, a1, f1 = _ffn_fwd(x, tied(gains["ffn1_pre"], token), gains["ffn1_post"], w["ffn1_gate"],
                                       w["ffn1_up"], w["ffn1_down"], "ffn1_fwd")
    more, token = hooks.weights("in", h1)
    w.update(more)
    qkv, un = _norm_proj(h1, tied(gains["mix_pre"], token), w["in"], "qkv_proj")
    bias = _ch_group_bias(_bias_table(rel_bias, "bias_table").transpose(1, 0, 2))
    o_a = _sb_fwd(qkv, "sb_fwd")
    o_b = _ch_fwd(qkv, bias, "ch_fwd")
    w.update(hooks.weights("rest", o_b)[0])
    w_out = w["out"].reshape(D, D)
    h2, mixed, mo = _mix_out_fwd(h1, o_a, o_b, gains["out_sb"], gains["out_ch"], w_out, gains["mix_post"],
                                 "mix_out_fwd")
    h3, xn2, g2, u2, a2, f2 = _ffn_fwd(h2, gains["ffn2_pre"], gains["ffn2_post"], w["ffn2_gate"], w["ffn2_up"],
                                       w["ffn2_down"], "ffn2_fwd")
    w_ple_proj = w["ple_proj"].transpose(1, 0, 2).reshape(p.shape[1], D)
    w_ple_gate = w["ple_gate"].reshape(D, D)

    loss, dh3, dproj, dgate, dg_ple = _ple_loss(h3, p, target, w_ple_proj, w_ple_gate, gains["ple_post"], "ple_loss")
    gw, gg = {}, {"ple_post": dg_ple}
    gw["ple_proj"] = _mm_tn(p[None], dproj, p.shape[1], "dw_ple_proj")
    row_sharded = lambda pair: tuple(o.reshape(S, D // S, D) for o in pair)
    gw["ple_gate"] = row_sharded(_mm_tn(h3[None], dgate[None], 512, "dw_ple_gate"))

    def ffn_bwd(tag, dh, x_in, xn, g_act, u_act, a_act, f, group):
        dgp, dup, df, gg[tag + "_post"] = _ffn_bwd_act(dh, f, gains[tag + "_post"], w[tag + "_down"], g_act, u_act,
                                                       tag + "_bwd_act")
        gw[tag + "_gate"] = _mm_tn(dgp, xn[None], dgp.shape[2], "dw_" + tag + "_gate")
        gw[tag + "_up"] = _mm_tn(dup, xn[None], dup.shape[2], "dw_" + tag + "_up")
        gw[tag + "_down"] = _mm_tn(a_act, df[None], a_act.shape[2], "dw_" + tag + "_down")
        g_pre = gains[tag + "_pre"]
        if group is not None:
            token = hooks.grads_ready(group, gw)
            g_pre = g_pre if token is None else g_pre + token[0, 0]
        dx, gg[tag + "_pre"] = _proj_bwd([dgp, dup], [w[tag + "_gate"], w[tag + "_up"]], x_in, g_pre, dh,
                                         tag + "_bwd_in")
        return dx

    dh2 = ffn_bwd("ffn2", dh3, h2, xn2, g2, u2, a2, f2, None)
    dmo, do_a, do_b, gg["mix_post"], gg["out_sb"], gg["out_ch"] = _mix_out_bwd(
        dh2, mo, gains["mix_post"], w_out, o_a, o_b, gains["out_sb"], gains["out_ch"], "mix_out_bwd")
    gw["out"] = row_sharded(_mm_tn(mixed[None], dmo[None], 512, "dw_out"))
    token = hooks.grads_ready("early", gw)
    if token is not None:
        do_a = do_a + token[0, 0]
    dq_a, dk_a, dv_a = _sb_bwd(qkv, do_a, o_a, "sb_bwd")
    token = hooks.grads_sent("early", dq_a)
    if token is not None:
        do_b = do_b + token[0, 0]
    dq_b, dk_b, dv_b, dbias = _ch_bwd(qkv, bias, do_b, "ch_bwd")
    g_rel = _bias_grad(_ch_fold_bias_grad(dbias).transpose(1, 0, 2), "bias_grad")
    dqkv = jnp.concatenate([dq_a, dk_a, dv_a, dq_b, dk_b, dv_b], axis=1)
    gw["in"] = _mm_tn(un[None], dqkv, 512, "dw_in", groups=S)
    dh1, gg["mix_pre"] = _proj_bwd([dqkv], [w["in"]], h1, gains["mix_pre"], dh2, "qkv_bwd_in")
    dx = ffn_bwd("ffn1", dh1, x, xn1, g1, u1, a1, f1, "late")
    return loss, dx, gw, gg, g_rel


BIG = ["ffn1_gate", "ffn1_up", "ffn1_down", "in", "out", "ffn2_gate", "ffn2_up", "ffn2_down", "ple_proj", "ple_gate"]
GAINS = ["ffn1_pre", "ffn1_post", "mix_pre", "mix_post", "out_sb", "out_ch", "ffn2_pre", "ffn2_post", "ple_post"]
TRANSPOSED = ("w_ffn1_gate", "w_ffn1_up", "w_ffn2_gate", "w_ffn2_up")
PERMUTED = ("ffn1_gate", "ffn1_up", "ffn1_down", "ffn2_gate", "ffn2_up", "ffn2_down")
W_GROUPS = {"first": ["ffn1_gate", "ffn1_up", "ffn1_down"], "in": ["in"],
            "rest": ["out", "ffn2_gate", "ffn2_up", "ffn2_down", "ple_proj", "ple_gate"]}
G_GROUPS = {"early": ["ple_proj", "ple_gate", "ffn2_gate", "ffn2_up", "ffn2_down", "out"],
            "late": ["in", "ffn1_gate", "ffn1_up", "ffn1_down"]}
ORDER = ["g_ffn1_pre", "g_ffn1_post", "w_ffn1_gate", "w_ffn1_up", "w_ffn1_down", "g_mix_pre", "g_mix_post", "w_in",
         "g_out_sb", "g_out_ch", "rel_bias", "w_out", "g_ffn2_pre", "g_ffn2_post", "w_ffn2_gate", "w_ffn2_up",
         "w_ffn2_down", "w_ple_proj", "w_ple_gate", "g_ple_post"]


def kernel(x, p, g_ffn1_pre, g_ffn1_post, w_ffn1_gate, w_ffn1_up, w_ffn1_down, g_mix_pre, g_mix_post, w_in, g_out_sb, g_out_ch, rel_bias, w_out, g_ffn2_pre, g_ffn2_post, w_ffn2_gate, w_ffn2_up, w_ffn2_down, w_ple_proj, w_ple_gate, g_ple_post, loss_target, m_g_ffn1_pre, m_g_ffn1_post, m_w_ffn1_gate, m_w_ffn1_up, m_w_ffn1_down, m_g_mix_pre, m_g_mix_post, m_w_in, m_g_out_sb, m_g_out_ch, m_rel_bias, m_w_out, m_g_ffn2_pre, m_g_ffn2_post, m_w_ffn2_gate, m_w_ffn2_up, m_w_ffn2_down, m_w_ple_proj, m_w_ple_gate, m_g_ple_post, v_g_ffn1_pre, v_g_ffn1_post, v_w_ffn1_gate, v_w_ffn1_up, v_w_ffn1_down, v_g_mix_pre, v_g_mix_post, v_w_in, v_g_out_sb, v_g_out_ch, v_rel_bias, v_w_out, v_g_ffn2_pre, v_g_ffn2_post, v_w_ffn2_gate, v_w_ffn2_up, v_w_ffn2_down, v_w_ple_proj, v_w_ple_gate, v_g_ple_post):
    args = dict(locals())
    take = lambda a, n: a[0].T if n in TRANSPOSED else a[0]
    wts = {n: take(args[n], n) for n in ORDER}
    ms = {n: take(args["m_" + n], n) for n in ORDER}
    vs = {n: take(args["v_" + n], n) for n in ORDER}
    gains = {n: wts["g_" + n][None] for n in GAINS}

    c_idx = lax.axis_index("c").astype(jnp.int32).reshape(1)
    me_idx = (2 * lax.axis_index("x") + lax.axis_index("y")).astype(jnp.int32).reshape(1)
    south = lax.axis_index("c") == 0

    lands = dict(zip(BIG, _cast_into_slot0(c_idx, [wts["w_" + n] for n in BIG], "cast_weights")))

    def in_order(names, zones):
        plain = [n for n in names if n not in PERMUTED]
        fixed = dict(zip(plain, _chip_order(me_idx, c_idx, [zones[n] for n in plain], "chip_order_" + plain[0]))
                     ) if plain else {}
        return {n: fixed[n] if n in fixed else _slabs(zones[n]) for n in names}

    class Overlapped:
        def __init__(self):
            self.started = {}
            self.flying = {}

        def start(self, group, collective_id, after):
            self.flying[group] = _gather_start([lands[n] for n in W_GROUPS[group]], "gather_%s_start" % group,
                                               collective_id, after)
            return self.flying[group][3]

        def weights(self, group, after=None):
            names = W_GROUPS[group]
            if group == "first":
                zones = _gather_finish([lands[n] for n in names], True, "gather_first")
                token = self.start("rest", 4, self.start("in", 1, zones[0]))
                return in_order(names, dict(zip(names, zones))), token
            send_sem, recv_sem, zones, _ = self.flying[group]
            zones = _gather_wait(send_sem, recv_sem, zones, after, "gather_%s_wait" % group)
            zones = _gather_finish(zones, False, "gather_%s_finish" % group)
            return in_order(names, dict(zip(names, zones))), None

        def grads_ready(self, group, gw):
            names = G_GROUPS[group]
            perm = [n in PERMUTED for n in names]
            halved = lambda g: g.reshape(N_CHIPS, 2, g.shape[1] // 2, g.shape[2])
            mine = [halved(gw[n][0]) for n in names]
            narrow = [halved(gw[n][1]) for n in names]
            if group == "late":
                return self.scatter(group, names, perm, mine, _pair_swap(narrow, perm, "grad_pair_swap_late"))
            self.swapping = names, perm, mine, _pair_swap_start(narrow, perm, "grad_pair_swap_start_early", 5)
            return self.swapping[3][3]

        def grads_sent(self, group, after):
            names, perm, mine, (send_sem, recv_sem, operands, _) = self.swapping
            operands = _pair_swap_wait(send_sem, recv_sem, operands, perm, after, "grad_pair_swap_wait_early")
            return self.scatter(group, names, perm, mine, operands[len(names):])

        def scatter(self, group, names, perm, mine, got):
            partial = _pair_add(c_idx, mine, got, perm, "grad_pair_add_" + group)
            send_sem, recv_sem, operands, token = _scatter_start(partial, perm, "grad_scatter_start_" + group,
                                                                 {"early": 2, "late": 3}[group])
            self.started[group] = names, perm, send_sem, recv_sem, operands, token
            return token

    def reduce_finish(state, after, tag):
        names, perm, send_sem, recv_sem, operands, _ = state
        operands = _scatter_wait(send_sem, recv_sem, operands, perm, after, "grad_scatter_wait_" + tag)
        n = len(names)
        own = _chip_add(me_idx, operands[:n], operands[n:], perm, "grad_chip_add_" + tag)
        return own, _pair_join(own, "grad_pair_join_" + tag)

    hooks = Overlapped()
    loss, dx, gw, gg, g_rel = _local_step(x[0], p[0, 0], loss_target[0], gains, wts["rel_bias"], hooks)

    grads, delta, new_m, new_v = {}, {}, {}, {}

    def finish(group, after):
        own, other = reduce_finish(hooks.started[group], after, group)
        names = ["w_" + n for n in G_GROUPS[group]]
        g, d, m, v = _adamw_halves(c_idx, [wts[n] for n in names], own, other, [ms[n] for n in names],
                                   [vs[n] for n in names], "adamw_" + group)
        for n, gg_, dd, mm, vv in zip(names, g, d, m, v):
            grads[n], delta[n], new_m[n], new_v[n] = gg_, dd, mm, vv
        return d[0]

    finish("late", finish("early", dx))

    pieces = [gg[n].reshape(-1, 128) for n in GAINS] + [jnp.pad(g_rel, ((0, 0), (0, N_REL_PAD - N_REL))).reshape(-1, 128)]
    summed = _all_sum_small(jnp.concatenate(pieces + [loss], axis=0), delta["w_in"], "small_grad_sum")
    at = 0
    for n, piece in zip(GAINS, pieces[:-1]):
        grads["g_" + n] = summed[at:at + piece.shape[0]].reshape(1, -1)[0]
        at += piece.shape[0]
    grads["rel_bias"] = summed[at:at + pieces[-1].shape[0]].reshape(N_HEADS, N_REL_PAD)[:, :N_REL]
    loss = summed[at + pieces[-1].shape[0], 0]

    small = ["g_" + n for n in GAINS] + ["rel_bias"]
    as_rows = lambda a: (a.reshape(-1, 128) if a.size % 128 == 0 else jnp.pad(a, ((0, 0), (0, N_REL_PAD - N_REL))).reshape(-1, 128))
    d, m, v = _adamw([as_rows(wts[n]) for n in small], [as_rows(grads[n]) for n in small],
                     [as_rows(ms[n]) for n in small], [as_rows(vs[n]) for n in small], 1, "adamw_small")
    for n, dd, mm, vv in zip(small, d, m, v):
        back = (lambda a: a.reshape(N_HEADS, N_REL_PAD)[:, :N_REL]) if n == "rel_bias" else (lambda a: a.reshape(-1))
        delta[n], new_m[n], new_v[n] = back(dd), back(mm), back(vv)

    outs = [loss, dx[None]]
    for table in (grads, delta, new_m, new_v):
        outs += [(table[n].T if n in TRANSPOSED else table[n])[None] for n in ORDER]
    return tuple(outs)
```

```python
import functools

import jax
import jax.numpy as jnp
from jax import lax
from jax.experimental import pallas as pl
from jax.experimental.pallas import tpu as pltpu

F32 = jnp.float32
BF16 = jnp.bfloat16
EPS = 1e-6
N_CHIPS = 4
HEAD_DIM = 64
N_HEADS = 8
CHUNK = 64
LOOKBACK = 8
BAND = (LOOKBACK + 1) * CHUNK
PAD = LOOKBACK * CHUNK
REL_CLIP = 128
N_REL = 2 * REL_CLIP + 1
N_REL_PAD = 384
SB_BLOCK = 256
PAIR = 2 * HEAD_DIM
SB_PAIRS = 2
ATT_SCALE = HEAD_DIM ** -0.5
NEG_INF = -1e30
ROW_BLOCK = 512
WIDE_ROW_BLOCK = 1024
VMEM_LIMIT_WIDE = 56 * 1024 * 1024
VMEM_LIMIT = 48 * 1024 * 1024
MESH = pl.DeviceIdType.MESH

ADAM_LR = 0.001
ADAM_B1 = 0.9
ADAM_B2 = 0.999
ADAM_EPS = 1e-08
ADAM_WD = 0.01
ADAM_STEP = 10

NT = (((1,), (1,)), ((), ()))
TN = (((0,), (0,)), ((), ()))


def _params(n_grid, vmem=None):
    return pltpu.CompilerParams(dimension_semantics=("arbitrary",) * n_grid, vmem_limit_bytes=vmem)


def _hbm(*arrays):
    return [pltpu.with_memory_space_constraint(a, pltpu.HBM) for a in arrays]


def _out(shape, dtype):
    return pltpu.HBM(shape, dtype)


def _dot(a, b, dims=None):
    if dims is None:
        return jnp.dot(a, b, preferred_element_type=F32)
    return lax.dot_general(a, b, dims, preferred_element_type=F32)


def _sigmoid(x):
    return 1.0 / (1.0 + jnp.exp(-x))


def _rms_fwd(x, g):
    r = lax.rsqrt(jnp.mean(x * x, axis=-1, keepdims=True) + EPS)
    return x * r * g


def _rms_bwd(x, g, dy):
    r = lax.rsqrt(jnp.mean(x * x, axis=-1, keepdims=True) + EPS)
    xh = x * r
    dg = jnp.sum(dy * xh, axis=0, keepdims=True)
    t = dy * g
    dx = r * (t - xh * jnp.mean(t * xh, axis=-1, keepdims=True))
    return dx, dg


def _accumulate(ref, val, first):
    @pl.when(first)
    def _():
        ref[...] = val

    @pl.when(jnp.logical_not(first))
    def _():
        ref[...] += val


def _split2(x):
    hi = x.astype(BF16)
    lo = (x - hi.astype(F32)).astype(BF16)
    return hi, lo


def _ffn_fwd(x, g_pre, g_post, wg, wu, wd, name):
    T, D = x.shape
    S, FS, _ = wg.shape
    tm = min(WIDE_ROW_BLOCK, T)

    def body(x_ref, gpre_ref, gpost_ref, wg_ref, wu_ref, wd_ref,
             h_ref, xn_ref, g_ref, u_ref, a_ref, f_ref):
        k = pl.program_id(1)

        @pl.when(k == 0)
        def _():
            xn_ref[...] = _rms_fwd(x_ref[...], gpre_ref[...]).astype(BF16)

        xn = xn_ref[...]
        g = _dot(xn, wg_ref[0], NT)
        u = _dot(xn, wu_ref[0], NT)
        g_ref[0] = g
        u_ref[0] = u
        a = (g * _sigmoid(g) * u).astype(BF16)
        a_ref[0] = a
        _accumulate(f_ref, _dot(a, wd_ref[0]), k == 0)

        @pl.when(k == S - 1)
        def _():
            h_ref[...] = x_ref[...] + 0.5 * _rms_fwd(f_ref[...], gpost_ref[...])

    row = pl.BlockSpec((tm, D), lambda i, k: (i, 0))
    vec = pl.BlockSpec((1, D), lambda i, k: (0, 0))
    act = pl.BlockSpec((1, tm, FS), lambda i, k: (k, i, 0))
    return pl.pallas_call(
        body, name=name, grid=(T // tm, S),
        in_specs=[row, vec, vec] + [pl.BlockSpec((1, FS, D), lambda i, k: (k, 0, 0))] * 3,
        out_specs=[row, row, act, act, act, row],
        out_shape=[_out((T, D), F32), _out((T, D), BF16),
                   _out((S, T, FS), F32), _out((S, T, FS), F32),
                   _out((S, T, FS), BF16), _out((T, D), F32)],
        compiler_params=_params(2, VMEM_LIMIT_WIDE),
    )(*_hbm(x, g_pre, g_post, wg, wu, wd))


def _ffn_bwd_act(dh, f, g_post, wd, g_act, u_act, name):
    T, D = dh.shape
    S, FS, _ = wd.shape
    tm = min(WIDE_ROW_BLOCK, T)

    def body(dh_ref, f_ref, gpost_ref, wd_ref, g_ref, u_ref, dgp_ref, dup_ref, df_ref, dgain_ref, df_s):
        i, k = pl.program_id(0), pl.program_id(1)

        @pl.when(k == 0)
        def _():
            df, dgain = _rms_bwd(f_ref[...], gpost_ref[...], 0.5 * dh_ref[...])
            df_s[...] = df.astype(BF16)
            df_ref[...] = df_s[...]
            _accumulate(dgain_ref, dgain, i == 0)

        da = _dot(df_s[...], wd_ref[0], NT)
        g = g_ref[0]
        s = _sigmoid(g)
        dup_ref[0] = (da * (g * s)).astype(BF16)
        dgp_ref[0] = (da * u_ref[0] * (s * (1.0 + g * (1.0 - s)))).astype(BF16)

    row = pl.BlockSpec((tm, D), lambda i, k: (i, 0))
    vec = pl.BlockSpec((1, D), lambda i, k: (0, 0))
    act = pl.BlockSpec((1, tm, FS), lambda i, k: (k, i, 0))
    return pl.pallas_call(
        body, name=name, grid=(T // tm, S),
        in_specs=[row, row, vec, pl.BlockSpec((1, FS, D), lambda i, k: (k, 0, 0)), act, act],
        out_specs=[act, act, row, vec],
        out_shape=[_out((S, T, FS), BF16), _out((S, T, FS), BF16),
                   _out((T, D), BF16), _out((1, D), F32)],
        scratch_shapes=[pltpu.VMEM((tm, D), BF16)],
        compiler_params=_params(2, VMEM_LIMIT_WIDE),
    )(*_hbm(dh, f, g_post, wd, g_act, u_act))


def _proj_bwd(dys, ws, x, g_pre, dh, name):
    T, D = x.shape
    n = len(dys)
    flat = dys[0].ndim == 2
    S = ws[0].shape[0]
    N = ws[0].shape[2] if flat else ws[0].shape[1]
    tm = min(WIDE_ROW_BLOCK, T)

    def body(*refs):
        dy_refs, w_refs = refs[:n], refs[n:2 * n]
        x_ref, gpre_ref, dh_ref, dx_ref, dgain_ref, acc_s = refs[2 * n:]
        i, k = pl.program_id(0), pl.program_id(1)
        part = None
        for dy_ref, w_ref in zip(dy_refs, w_refs):
            term = _dot(dy_ref[...], w_ref[0], NT) if flat else _dot(dy_ref[0], w_ref[0])
            part = term if part is None else part + term
        _accumulate(acc_s, part, k == 0)

        @pl.when(k == S - 1)
        def _():
            dx, dgain = _rms_bwd(x_ref[...], gpre_ref[...], acc_s[...])
            dx_ref[...] = dh_ref[...] + dx
            _accumulate(dgain_ref, dgain, i == 0)

    row = pl.BlockSpec((tm, D), lambda i, k: (i, 0))
    vec = pl.BlockSpec((1, D), lambda i, k: (0, 0))
    return pl.pallas_call(
        body, name=name, grid=(T // tm, S),
        in_specs=[pl.BlockSpec((tm, N), lambda i, k: (i, k)) if flat else pl.BlockSpec((1, tm, N), lambda i, k: (k, i, 0))] * n
        + [pl.BlockSpec((1,) + ws[0].shape[1:], lambda i, k: (k, 0, 0))] * n + [row, vec, row],
        out_specs=[row, vec],
        out_shape=[_out((T, D), F32), _out((1, D), F32)],
        scratch_shapes=[pltpu.VMEM((tm, D), F32)],
        compiler_params=_params(2, VMEM_LIMIT_WIDE),
    )(*_hbm(*dys, *ws, x, g_pre, dh))


def _mm_tn(a, b, bm, name, groups=None):
    ga, T, M = a.shape
    if groups is None:
        gb, _, N = b.shape
        b_spec = pl.BlockSpec((1, T, N), (lambda g, m: (g, 0, 0)) if gb > 1 else (lambda g, m: (0, 0, 0)))
    else:
        gb, N = groups, b.shape[1] // groups
        b_spec = pl.BlockSpec((T, N), lambda g, m: (0, g))
    G = max(ga, gb)

    def body(a_ref, b_ref, o_ref, narrow_ref):
        bv = b_ref[0] if groups is None else b_ref[...]
        o_ref[0] = _dot(a_ref[0].astype(BF16), bv.astype(BF16), TN)
        narrow_ref[0] = o_ref[0].astype(BF16)

    out = pl.BlockSpec((1, bm, N), lambda g, m: (g, m, 0))
    return pl.pallas_call(
        body, name=name, grid=(G, M // bm),
        in_specs=[pl.BlockSpec((1, T, bm), (lambda g, m: (g, 0, m)) if ga > 1 else (lambda g, m: (0, 0, m))), b_spec],
        out_specs=[out, out],
        out_shape=[_out((G, M, N), F32), _out((G, M, N), BF16)],
        compiler_params=_params(2, VMEM_LIMIT),
    )(*_hbm(a, b))


def _norm_proj(x, g_pre, w, name):
    T, D = x.shape
    S, _, N = w.shape
    tm = min(WIDE_ROW_BLOCK, T)

    def body(x_ref, g_ref, w_ref, o_ref, xn_ref, xn_s):
        @pl.when(pl.program_id(1) == 0)
        def _():
            xn_s[...] = _rms_fwd(x_ref[...], g_ref[...]).astype(BF16)
            xn_ref[...] = xn_s[...]

        o_ref[...] = _dot(xn_s[...], w_ref[0]).astype(BF16)

    row = pl.BlockSpec((tm, D), lambda i, k: (i, 0))
    return pl.pallas_call(
        body, name=name, grid=(T // tm, S),
        in_specs=[row, pl.BlockSpec((1, D), lambda i, k: (0, 0)), pl.BlockSpec((1, D, N), lambda i, k: (k, 0, 0))],
        out_specs=[pl.BlockSpec((tm, N), lambda i, k: (i, k)), row],
        out_shape=[_out((T, S * N), BF16), _out((T, D), BF16)],
        scratch_shapes=[pltpu.VMEM((tm, D), BF16)],
        compiler_params=_params(2, VMEM_LIMIT_WIDE),
    )(*_hbm(x, g_pre, w))


def _mix_out_fwd(h, o_a, o_b, g_sb, g_ch, w_out, g_post, name):
    T, D = h.shape
    W = g_sb.shape[1]
    tm = min(WIDE_ROW_BLOCK, T)

    def body(h_ref, oa_ref, ob_ref, gsb_ref, gch_ref, w_ref, gpost_ref, h2_ref, mixed_ref, mo_ref):
        mixed_ref[:, :W] = _rms_fwd(oa_ref[...], gsb_ref[...]).astype(BF16)
        mixed_ref[:, W:] = _rms_fwd(ob_ref[...], gch_ref[...]).astype(BF16)
        mo = _dot(mixed_ref[...], w_ref[...])
        mo_ref[...] = mo
        h2_ref[...] = h_ref[...] + _rms_fwd(mo, gpost_ref[...])

    row = pl.BlockSpec((tm, D), lambda i: (i, 0))
    part = pl.BlockSpec((tm, W), lambda i: (i, 0))
    half = pl.BlockSpec((1, W), lambda i: (0, 0))
    return pl.pallas_call(
        body, name=name, grid=(T // tm,),
        in_specs=[row, part, part, half, half, pl.BlockSpec((D, D), lambda i: (0, 0)), pl.BlockSpec((1, D), lambda i: (0, 0))],
        out_specs=[row, row, row],
        out_shape=[_out((T, D), F32), _out((T, D), BF16),
                   _out((T, D), F32)],
        compiler_params=_params(1, VMEM_LIMIT_WIDE),
    )(*_hbm(h, o_a, o_b, g_sb, g_ch, w_out, g_post))


def _mix_out_bwd(dh, mo, g_post, w_out, o_a, o_b, g_sb, g_ch, name):
    T, D = dh.shape
    W = g_sb.shape[1]
    tm = min(WIDE_ROW_BLOCK, T)

    def body(dh_ref, mo_ref, gpost_ref, w_ref, oa_ref, ob_ref, gsb_ref, gch_ref,
             dmo_ref, doa_ref, dob_ref, dgpost_ref, dgsb_ref, dgch_ref):
        first = pl.program_id(0) == 0
        dmo, dgpost = _rms_bwd(mo_ref[...], gpost_ref[...], dh_ref[...])
        dmo_ref[...] = dmo.astype(BF16)
        dmix = _dot(dmo_ref[...], w_ref[...], NT)
        doa_ref[...], dgsb = _rms_bwd(oa_ref[...], gsb_ref[...], dmix[:, :W])
        dob_ref[...], dgch = _rms_bwd(ob_ref[...], gch_ref[...], dmix[:, W:])
        _accumulate(dgpost_ref, dgpost, first)
        _accumulate(dgsb_ref, dgsb, first)
        _accumulate(dgch_ref, dgch, first)

    row = pl.BlockSpec((tm, D), lambda i: (i, 0))
    part = pl.BlockSpec((tm, W), lambda i: (i, 0))
    vec = pl.BlockSpec((1, D), lambda i: (0, 0))
    half = pl.BlockSpec((1, W), lambda i: (0, 0))
    return pl.pallas_call(
        body, name=name, grid=(T // tm,),
        in_specs=[row, row, vec, pl.BlockSpec((D, D), lambda i: (0, 0)), part, part, half, half],
        out_specs=[row, part, part, vec, half, half],
        out_shape=[_out((T, D), BF16), _out((T, W), F32),
                   _out((T, W), F32), _out((1, D), F32),
                   _out((1, W), F32), _out((1, W), F32)],
        compiler_params=_params(1, VMEM_LIMIT_WIDE),
    )(*_hbm(dh, mo, g_post, w_out, o_a, o_b, g_sb, g_ch))


def _ple_loss(h, p, target, w_proj, w_gate, g_post, name):
    T, D = h.shape
    P = p.shape[1]
    S = N_CHIPS
    C = D // S
    tm = min(ROW_BLOCK, T)

    def body(h_ref, p_ref, t_ref, wp_ref, wg_ref, g_ref, loss_ref, dh_ref, dproj_ref, dgate_ref, dgain_ref):
        first = pl.program_id(0) == 0
        h3 = h_ref[...]
        proj = _dot(p_ref[...].astype(BF16), wp_ref[...])
        s = _sigmoid(_dot(h3.astype(BF16), wg_ref[...]))
        e = proj * s
        diff = h3 + _rms_fwd(e, g_ref[...]) - t_ref[...]
        part = 0.5 * jnp.sum(jnp.mean(diff * diff, axis=-1, keepdims=True), axis=0, keepdims=True)
        _accumulate(loss_ref, jnp.broadcast_to(part, loss_ref.shape), first)
        dy = diff * (1.0 / D)
        de, dgain = _rms_bwd(e, g_ref[...], dy)
        _accumulate(dgain_ref, dgain, first)
        dproj = (de * s).astype(BF16)
        for j in range(S):
            dproj_ref[j] = dproj[:, j * C:(j + 1) * C]
        dgate_ref[...] = (de * proj * s * (1.0 - s)).astype(BF16)
        dh_ref[...] = dy + _dot(dgate_ref[...], wg_ref[...], NT)

    row = pl.BlockSpec((tm, D), lambda i: (i, 0))
    vec = pl.BlockSpec((1, D), lambda i: (0, 0))
    return pl.pallas_call(
        body, name=name, grid=(T // tm,),
        in_specs=[row, pl.BlockSpec((tm, P), lambda i: (i, 0)), row,
                  pl.BlockSpec((P, D), lambda i: (0, 0)), pl.BlockSpec((D, D), lambda i: (0, 0)), vec],
        out_specs=[pl.BlockSpec((8, 128), lambda i: (0, 0)), row,
                   pl.BlockSpec((S, tm, C), lambda i: (0, i, 0)), row, vec],
        out_shape=[_out((8, 128), F32), _out((T, D), F32),
                   _out((S, T, C), BF16), _out((T, D), BF16),
                   _out((1, D), F32)],
        compiler_params=_params(1, VMEM_LIMIT_WIDE),
    )(*_hbm(h, p, target, w_proj, w_gate, g_post))


def _sb_scores(q, kj, mask):
    z = _dot(q, kj, NT)
    sp = jnp.maximum(z, 0.0) + jnp.log(1.0 + jnp.exp(-jnp.abs(z)))
    return z, sp if mask is None else jnp.where(mask, sp, 0.0)


def _strict_causal():
    rows = lax.broadcasted_iota(jnp.int32, (SB_BLOCK, SB_BLOCK), 0)
    cols = lax.broadcasted_iota(jnp.int32, (SB_BLOCK, SB_BLOCK), 1)
    return cols < rows


def _tri(cmp):
    r = lax.broadcasted_iota(jnp.int32, (2 * SB_BLOCK, SB_BLOCK), 0) % SB_BLOCK
    c = lax.broadcasted_iota(jnp.int32, (2 * SB_BLOCK, SB_BLOCK), 1)
    return jnp.where(cmp(r, c), 1.0, 0.0).astype(BF16)


def _cum(x, tri):
    return _dot(jnp.concatenate(_split2(x), axis=1), tri)


def _pair_lanes():
    lane = lax.broadcasted_iota(jnp.int32, (1, PAIR), 1)
    return [lane < HEAD_DIM, lane >= HEAD_DIM]


def _only(lanes, x):
    return jnp.where(lanes, x, jnp.zeros_like(x))


def _sb_fwd(qkv, name):
    T = qkv.shape[0]
    B = SB_BLOCK
    W = SB_PAIRS * PAIR
    steps = N_HEADS // (2 * SB_PAIRS)
    heads = [(p, h) for p in range(SB_PAIRS) for h in range(2)]

    def body(q_ref, k_ref, v_ref, o_ref):
        i = pl.program_id(1)
        after = _tri(lambda r, c: r > c)
        lanes = _pair_lanes()
        cols = [slice(p * PAIR, (p + 1) * PAIR) for p in range(SB_PAIRS)]
        q = {(p, h): _only(lanes[h], q_ref[:, cols[p]] * ATT_SCALE) for p, h in heads}

        def tiles(j, carries, mask):
            at = pl.ds(pl.multiple_of(j * B, B), B)
            scores = [_sb_scores(q[ph], k_ref[at, cols[ph[0]]], mask) for ph in heads]
            laters = [_cum(sp, after) for _, sp in scores]
            out = []
            for ph, (z, sp), later, (run, acc) in zip(heads, scores, laters, carries):
                a = jnp.exp(z - sp - later - run)
                if mask is not None:
                    a = jnp.where(mask, a, 0.0)
                out.append((run + later[:, 0:1] + sp[:, 0:1],
                            acc + _dot(a.astype(BF16), _only(lanes[ph[1]], v_ref[at, cols[ph[0]]]))))
            return tuple(out)

        zero = (jnp.zeros((B, 1), F32), jnp.zeros((B, PAIR), F32))
        carries = tiles(i, (zero,) * len(heads), _strict_causal())
        carries = lax.fori_loop(0, i, lambda jj, cs: tiles(i - 1 - jj, cs, None), carries)
        for p in range(SB_PAIRS):
            o_ref[:, cols[p]] = carries[2 * p][1] + carries[2 * p + 1][1]

    blk = lambda off: pl.BlockSpec((B, W), lambda g, i: (i, g + off))
    full = lambda off: pl.BlockSpec((T, W), lambda g, i: (0, g + off))
    return pl.pallas_call(
        body, name=name, grid=(steps, T // B),
        in_specs=[blk(0), full(steps), full(2 * steps)],
        out_specs=blk(0),
        out_shape=_out((T, N_HEADS * HEAD_DIM), F32),
        compiler_params=_params(2, VMEM_LIMIT),
    )(*_hbm(qkv, qkv, qkv))


def _sb_bwd(qkv, do, o, name):
    T = qkv.shape[0]
    B = SB_BLOCK
    W = SB_PAIRS * PAIR
    steps = N_HEADS // (2 * SB_PAIRS)
    n_blocks = T // B
    heads = [(p, h) for p in range(SB_PAIRS) for h in range(2)]

    def body(q_ref, k_ref, v_ref, do_ref, o_ref, dq_ref, dk_ref, dv_ref, dk_s, dv_s):
        i = pl.program_id(1)

        @pl.when(i == 0)
        def _():
            dk_s[...] = jnp.zeros_like(dk_s)
            dv_s[...] = jnp.zeros_like(dv_s)

        after = _tri(lambda r, c: r > c)
        since = _tri(lambda r, c: r >= c)
        lanes = _pair_lanes()
        cols = [slice(p * PAIR, (p + 1) * PAIR) for p in range(SB_PAIRS)]
        q = {(p, h): _only(lanes[h], q_ref[:, cols[p]] * ATT_SCALE) for p, h in heads}
        do = {(p, h): _only(lanes[h], do_ref[:, cols[p]].astype(BF16)) for p, h in heads}
        total = {ph: jnp.sum(do[ph].astype(F32) * o_ref[:, cols[ph[0]]], axis=1, keepdims=True) for ph in heads}

        def tiles(j, carries, mask):
            at = pl.ds(pl.multiple_of(j * B, B), B)
            ks = [k_ref[at, c] for c in cols]
            vs = [v_ref[at, c] for c in cols]
            scores = [_sb_scores(q[ph], ks[ph[0]], mask) for ph in heads]
            laters = [_cum(sp, after) for _, sp in scores]
            das = [_dot(do[ph], vs[ph[0]], NT) for ph in heads]
            a_s, gs = [], []
            for (z, sp), later, da, carry in zip(scores, laters, das, carries):
                a = jnp.exp(z - sp - later - carry[0])
                if mask is not None:
                    a = jnp.where(mask, a, 0.0)
                a = a.astype(BF16)
                a_s.append(a)
                gs.append(a.astype(F32) * da)
            sinces = [_cum(g, since) for g in gs]
            dzs = []
            for ph, (_, sp), g, from_s, carry in zip(heads, scores, gs, sinces, carries):
                g_before = total[ph] - carry[1] - from_s
                fail = jnp.exp(-sp)
                dz = fail * (g + g_before) - g_before
                if mask is not None:
                    dz = jnp.where(mask, dz, 0.0)
                dzs.append(dz.astype(BF16))
            out = []
            for ph, (_, sp), a, dz, later, from_s, carry in zip(heads, scores, a_s, dzs, laters, sinces, carries):
                dk_s[at, cols[ph[0]]] += _dot(dz, q[ph], TN)
                dv_s[at, cols[ph[0]]] += _dot(a, do[ph], TN)
                out.append((carry[0] + later[:, 0:1] + sp[:, 0:1], carry[1] + from_s[:, 0:1],
                            carry[2] + _dot(dz, _only(lanes[ph[1]], ks[ph[0]]))))
            return tuple(out)

        col = jnp.zeros((B, 1), F32)
        zero = (col, col, jnp.zeros((B, PAIR), F32))
        carries = tiles(i, (zero,) * len(heads), _strict_causal())
        last = lax.fori_loop(0, i, lambda jj, cs: tiles(i - 1 - jj, cs, None), carries)
        for p in range(SB_PAIRS):
            dq_ref[:, cols[p]] = ((last[2 * p][2] + last[2 * p + 1][2]) * ATT_SCALE).astype(BF16)

        @pl.when(i == n_blocks - 1)
        def _():
            dk_ref[...] = dk_s[...].astype(BF16)
            dv_ref[...] = dv_s[...].astype(BF16)

    blk = lambda off: pl.BlockSpec((B, W), lambda g, i: (i, g + off))
    full = lambda off: pl.BlockSpec((T, W), lambda g, i: (0, g + off))
    out = _out((T, N_HEADS * HEAD_DIM), BF16)
    return pl.pallas_call(
        body, name=name, grid=(steps, n_blocks),
        in_specs=[blk(0), full(steps), full(2 * steps), blk(0), blk(0)],
        out_specs=[blk(0), full(0), full(0)],
        out_shape=[out, out, out],
        scratch_shapes=[pltpu.VMEM((T, W), F32)] * 2,
        compiler_params=_params(2, VMEM_LIMIT),
    )(*_hbm(qkv, qkv, qkv, do, o))


NEAR = BAND - PAD + REL_CLIP
FAR = BAND - NEAR
NEAR_REL = 2 * REL_CLIP
BIAS_ROWS = 8


def _rel_onehot(i, transposed):
    shape = (NEAR, NEAR_REL) if transposed else (NEAR_REL, NEAR)
    j = FAR + lax.broadcasted_iota(jnp.int32, shape, 0 if transposed else 1)
    r = lax.broadcasted_iota(jnp.int32, shape, 1 if transposed else 0)
    idx = jnp.clip(i + PAD - j, -REL_CLIP, REL_CLIP) + REL_CLIP
    return jnp.where(idx - 1 == r, 1.0, 0.0).astype(BF16)


def _bias_table(rel_bias, name):
    def body(near_ref, far_ref, o_ref):
        rb = near_ref[...]
        hi, lo = _split2(rb)
        lo2 = (rb - hi.astype(F32) - lo.astype(F32)).astype(BF16)
        far = jnp.broadcast_to(far_ref[...], (N_HEADS, FAR))
        for k in range(BIAS_ROWS):
            onehot = _rel_onehot(pl.program_id(0) * BIAS_ROWS + k, False)
            o_ref[k, :, :FAR] = far
            o_ref[k, :, FAR:] = _dot(hi, onehot) + _dot(lo, onehot) + _dot(lo2, onehot)

    return pl.pallas_call(
        body, name=name, grid=(CHUNK // BIAS_ROWS,),
        in_specs=[pl.BlockSpec((N_HEADS, NEAR_REL), lambda i: (0, 0)), pl.BlockSpec((N_HEADS, 1), lambda i: (0, 0))],
        out_specs=pl.BlockSpec((BIAS_ROWS, N_HEADS, BAND), lambda i: (i, 0, 0)),
        out_shape=_out((CHUNK, N_HEADS, BAND), F32),
        compiler_params=_params(1),
    )(*_hbm(rel_bias[:, 1:], rel_bias[:, N_REL - 1:]))


def _bias_grad(dbias_t, name):
    def body(d_ref, near_ref, far_ref):
        near, far = None, None
        for k in range(BIAS_ROWS):
            onehot = _rel_onehot(pl.program_id(0) * BIAS_ROWS + k, True)
            hi, lo = _split2(d_ref[k, :, FAR:])
            part = _dot(hi, onehot) + _dot(lo, onehot)
            rest = jnp.sum(d_ref[k, :, :FAR], axis=1, keepdims=True)
            near, far = (part, rest) if near is None else (near + part, far + rest)
        first = pl.program_id(0) == 0
        _accumulate(near_ref, near, first)
        _accumulate(far_ref, jnp.broadcast_to(far, far_ref.shape), first)

    near, far = pl.pallas_call(
        body, name=name, grid=(CHUNK // BIAS_ROWS,),
        in_specs=[pl.BlockSpec((BIAS_ROWS, N_HEADS, BAND), lambda i: (i, 0, 0))],
        out_specs=[pl.BlockSpec((N_HEADS, NEAR_REL), lambda i: (0, 0)), pl.BlockSpec((N_HEADS, 128), lambda i: (0, 0))],
        out_shape=[_out((N_HEADS, NEAR_REL), F32), _out((N_HEADS, 128), F32)],
        compiler_params=_params(1),
    )(*_hbm(dbias_t))
    return jnp.pad(near, ((0, 0), (1, 0))).at[:, N_REL - 1].add(far[:, 0])


def _ch_probs(scores, bias, valid):
    z = jnp.where(valid, scores * ATT_SCALE + bias, NEG_INF)
    e = jnp.exp(z - jnp.max(z, axis=-1, keepdims=True))
    return e / jnp.sum(e, axis=-1, keepdims=True)


CH_HEADS = [(pair, h) for pair in range(N_HEADS // 2) for h in range(2)]
CH_COLS = [slice(pair * PAIR, (pair + 1) * PAIR) for pair in range(N_HEADS // 2)]


CH_GROUP = 2
CH_Q = CH_GROUP * CHUNK
CH_WIN = (LOOKBACK + CH_GROUP) * CHUNK


def _ch_valid(n):
    row_chunk = lax.broadcasted_iota(jnp.int32, (CH_Q, CH_WIN), 0) // CHUNK
    slot = lax.broadcasted_iota(jnp.int32, (CH_Q, CH_WIN), 1)
    ahead = slot // CHUNK - row_chunk
    return (ahead >= 0) & (ahead <= LOOKBACK) & (n * CH_Q + slot >= PAD)


def _ch_group_bias(bias):
    shifted = [jnp.pad(bias, ((0, 0), (0, 0), (c * CHUNK, (CH_GROUP - 1 - c) * CHUNK))) for c in range(CH_GROUP)]
    return jnp.concatenate(shifted, axis=1)


def _ch_fold_bias_grad(dbias):
    parts = [dbias[:, c * CHUNK:(c + 1) * CHUNK, c * CHUNK:c * CHUNK + BAND] for c in range(CH_GROUP)]
    return sum(parts[1:], parts[0])


def _ch_fwd(qkv, bias, name):
    T = qkv.shape[0]
    W = N_HEADS * HEAD_DIM

    def body(q_ref, k_ref, v_ref, b_ref, o_ref, kp, vp):
        n = pl.program_id(0)

        @pl.when(n == 0)
        def _():
            _ch_load_padded(k_ref, v_ref, kp, vp)

        win = pl.ds(pl.multiple_of(n * CH_Q, CH_Q), CH_WIN)
        valid = _ch_valid(n)
        lanes = _pair_lanes()
        scores = [_dot(_only(lanes[h], q_ref[:, CH_COLS[pair]]), kp[win, CH_COLS[pair]], NT) for pair, h in CH_HEADS]
        probs = [_ch_probs(s, b_ref[2 * pair + h], valid).astype(BF16) for s, (pair, h) in zip(scores, CH_HEADS)]
        outs = [_dot(p, _only(lanes[h], vp[win, CH_COLS[pair]])) for p, (pair, h) in zip(probs, CH_HEADS)]
        for pair, cols in enumerate(CH_COLS):
            o_ref[:, cols] = outs[2 * pair] + outs[2 * pair + 1]

    full = lambda col: pl.BlockSpec((T, W), lambda n: (0, col))
    return pl.pallas_call(
        body, name=name, grid=(T // CH_Q,),
        in_specs=[pl.BlockSpec((CH_Q, W), lambda n: (n, 3)), full(4), full(5),
                  pl.BlockSpec((N_HEADS, CH_Q, CH_WIN), lambda n: (0, 0, 0))],
        out_specs=pl.BlockSpec((CH_Q, W), lambda n: (n, 0)),
        out_shape=_out((T, W), F32),
        scratch_shapes=[pltpu.VMEM((PAD + T, W), BF16)] * 2,
        compiler_params=_params(1, VMEM_LIMIT),
    )(*_hbm(qkv, qkv, qkv, bias))


def _ch_load_padded(k_ref, v_ref, kp, vp):
    for src, dst in ((k_ref, kp), (v_ref, vp)):
        dst[:PAD, :] = jnp.zeros((PAD, dst.shape[1]), dst.dtype)
        dst[PAD:, :] = src[...]


def _ch_bwd(qkv, bias, do, name):
    T = qkv.shape[0]
    W = N_HEADS * HEAD_DIM
    n_chunks = T // CH_Q

    def body(q_ref, k_ref, v_ref, b_ref, do_ref, dq_ref, dk_ref, dv_ref, db_ref, kp, vp, dk_s, dv_s):
        n = pl.program_id(0)

        @pl.when(n == 0)
        def _():
            _ch_load_padded(k_ref, v_ref, kp, vp)
            dk_s[...] = jnp.zeros_like(dk_s)
            dv_s[...] = jnp.zeros_like(dv_s)
            db_ref[...] = jnp.zeros_like(db_ref)

        win = pl.ds(pl.multiple_of(n * CH_Q, CH_Q), CH_WIN)
        valid = _ch_valid(n)
        lanes = _pair_lanes()
        kws = [kp[win, cols] for cols in CH_COLS]
        vws = [vp[win, cols] for cols in CH_COLS]
        qs = [_only(lanes[h], q_ref[:, CH_COLS[pair]]) for pair, h in CH_HEADS]
        dos = [_only(lanes[h], do_ref[:, CH_COLS[pair]].astype(BF16)) for pair, h in CH_HEADS]
        scores = [_dot(q, kws[pair], NT) for q, (pair, _) in zip(qs, CH_HEADS)]
        dps = [_dot(do, vws[pair], NT) for do, (pair, _) in zip(dos, CH_HEADS)]
        probs = [_ch_probs(s, b_ref[2 * pair + h], valid) for s, (pair, h) in zip(scores, CH_HEADS)]
        dzs = [p * (dp - jnp.sum(dp * p, axis=-1, keepdims=True)) for p, dp in zip(probs, dps)]
        for k, dz in enumerate(dzs):
            db_ref[k] += dz
        dzbs = [(dz * ATT_SCALE).astype(BF16) for dz in dzs]
        dqs = [_dot(dz, _only(lanes[h], kws[pair])) for dz, (pair, h) in zip(dzbs, CH_HEADS)]
        dks = [_dot(dz, q, TN) for dz, q in zip(dzbs, qs)]
        dvs = [_dot(p.astype(BF16), do, TN) for p, do in zip(probs, dos)]
        for pair, cols in enumerate(CH_COLS):
            dq_ref[:, cols] = (dqs[2 * pair] + dqs[2 * pair + 1]).astype(BF16)
            dk_s[win, cols] += dks[2 * pair] + dks[2 * pair + 1]
            dv_s[win, cols] += dvs[2 * pair] + dvs[2 * pair + 1]

        @pl.when(n == n_chunks - 1)
        def _():
            dk_ref[...] = dk_s[PAD:, :].astype(BF16)
            dv_ref[...] = dv_s[PAD:, :].astype(BF16)

    full = lambda col: pl.BlockSpec((T, W), lambda n: (0, col))
    blk = lambda col: pl.BlockSpec((CH_Q, W), lambda n: (n, col))
    tab = pl.BlockSpec((N_HEADS, CH_Q, CH_WIN), lambda n: (0, 0, 0))
    out = _out((T, W), BF16)
    return pl.pallas_call(
        body, name=name, grid=(n_chunks,),
        in_specs=[blk(3), full(4), full(5), tab, blk(0)],
        out_specs=[blk(0), full(0), full(0), tab],
        out_shape=[out, out, out, _out((N_HEADS, CH_Q, CH_WIN), F32)],
        scratch_shapes=[pltpu.VMEM((PAD + T, W), BF16)] * 2 + [pltpu.VMEM((PAD + T, W), F32)] * 2,
        compiler_params=_params(1, VMEM_LIMIT),
    )(*_hbm(qkv, qkv, qkv, bias, do))


def _rows_split(a, parts):
    return a.reshape(a.shape[:-2] + (parts, a.shape[-2] // parts, a.shape[-1]))


def _cast_into_own_slot(me, c, ws, in_chip_order, name):
    parts = 2
    ws = [_rows_split(_rows_split(w, 2), parts) for w in ws]
    n = len(ws)

    def body(me_ref, c_ref, *refs):
        for src, dst in zip(refs[:n], refs[n:]):
            dst[0, 0, 0] = src[0, 0].astype(BF16)

    def specs(w, plain):
        block = (1, 1) + w.shape[2:]
        if plain:
            return (pl.BlockSpec(block, lambda d, r, me_ref, c_ref: (d, r, 0, 0)),
                    pl.BlockSpec((1,) + block, lambda d, r, me_ref, c_ref: (me_ref[0], d, r, 0, 0)))
        return (pl.BlockSpec(block, lambda d, r, me_ref, c_ref: (d ^ c_ref[0], r, 0, 0)),
                pl.BlockSpec((1,) + block, lambda d, r, me_ref, c_ref: (0, d, r, 0, 0)))

    both = [specs(w, plain) for w, plain in zip(ws, in_chip_order)]
    outs = pl.pallas_call(
        body, name=name,
        grid_spec=pltpu.PrefetchScalarGridSpec(
            num_scalar_prefetch=2, grid=(2, parts),
            in_specs=[s[0] for s in both], out_specs=[s[1] for s in both]),
        out_shape=[_out((N_CHIPS,) + w.shape, BF16) for w in ws],
        compiler_params=_params(2, VMEM_LIMIT),
    )(me, c, *_hbm(*ws))
    return [o.reshape(N_CHIPS, 2, o.shape[2] * o.shape[3], o.shape[4]) for o in outs]


def _zone_slots(in_chip_order):
    x, y, c, _ = _place()
    me = 2 * x + y
    if in_chip_order:
        return (me, c), (lambda r: (me, c)), (lambda r: (me ^ r, c)), (lambda r: (me ^ r, c))
    return (0, 0), (lambda r: (r, 0)), (lambda r: (r, 0)), (lambda r: (r, 1))


def _pair_add(c, mine, got, permuted, name):
    parts = 2
    mine = [_rows_split(m, parts) for m in mine]
    got = [_rows_split(g, parts) for g in got]
    n = len(mine)

    def body(c_ref, *refs):
        for a, b, o in zip(refs[:n], refs[n:2 * n], refs[2 * n:]):
            o[0, 0] = (a[0, 0, 0] + b[0, 0].astype(F32)).astype(BF16)

    def mine_spec(m, perm):
        if perm:
            return pl.BlockSpec((1, 1, 1) + m.shape[3:], lambda j, r, c_ref: (j, 0, r, 0, 0))
        return pl.BlockSpec((1, 1, 1) + m.shape[3:], lambda j, r, c_ref: (j, c_ref[0], r, 0, 0))

    def got_spec(g):
        return pl.BlockSpec((1, 1) + g.shape[2:], lambda j, r, c_ref: (j, r, 0, 0))

    outs = pl.pallas_call(
        body, name=name,
        grid_spec=pltpu.PrefetchScalarGridSpec(
            num_scalar_prefetch=1, grid=(N_CHIPS, parts),
            in_specs=[mine_spec(m, perm) for m, perm in zip(mine, permuted)] + [got_spec(g) for g in got],
            out_specs=[got_spec(g) for g in got]),
        out_shape=[_out(g.shape, BF16) for g in got],
        compiler_params=_params(2, VMEM_LIMIT),
    )(c, *_hbm(*mine, *got))
    return [o.reshape(o.shape[0], o.shape[1] * o.shape[2], o.shape[3]) for o in outs]


def _chip_add(me, partials, landed, permuted, name):
    parts = 2
    ps = [_rows_split(x, parts) for x in partials]
    ls = [_rows_split(x, parts) for x in landed]
    n = len(ps)

    def body(me_ref, *refs):
        for own, got, o in zip(refs[:n], refs[n:2 * n], refs[2 * n:]):
            acc = own[0, 0].astype(F32)
            for r in range(N_CHIPS - 1):
                acc = acc + got[r, 0].astype(F32)
            o[0] = acc

    def own_spec(x, perm):
        if perm:
            return pl.BlockSpec((1, 1) + x.shape[2:], lambda r, me_ref: (0, r, 0, 0))
        return pl.BlockSpec((1, 1) + x.shape[2:], lambda r, me_ref: (me_ref[0], r, 0, 0))

    outs = pl.pallas_call(
        body, name=name,
        grid_spec=pltpu.PrefetchScalarGridSpec(
            num_scalar_prefetch=1, grid=(parts,),
            in_specs=[own_spec(x, perm) for x, perm in zip(ps, permuted)]
            + [pl.BlockSpec((N_CHIPS - 1, 1) + x.shape[2:], lambda r, me_ref: (0, r, 0, 0)) for x in ls],
            out_specs=[pl.BlockSpec((1,) + x.shape[2:], lambda r, me_ref: (r, 0, 0)) for x in ps]),
        out_shape=[_out(x.shape[1:], F32) for x in ps],
        compiler_params=_params(1, VMEM_LIMIT),
    )(me, *_hbm(*ps, *ls))
    return [o.reshape(o.shape[0] * o.shape[1], o.shape[2]) for o in outs]


def _adamw_math(w, g, m, v):
    m = ADAM_B1 * m + (1.0 - ADAM_B1) * g
    v = ADAM_B2 * v + (1.0 - ADAM_B2) * (g * g)
    m_hat = m / (1.0 - ADAM_B1 ** ADAM_STEP)
    v_hat = v / (1.0 - ADAM_B2 ** ADAM_STEP)
    delta = -ADAM_LR * (m_hat / (jnp.sqrt(v_hat) + ADAM_EPS) + ADAM_WD * w)
    return delta, m, v


def _adamw(ws, gs, ms, vs, parts, name):
    n = len(ws)
    flat = [_rows_split(a, parts) for a in (*ws, *gs, *ms, *vs)]

    def body(*refs):
        ins, outs = refs[:4 * n], refs[4 * n:]
        for k in range(n):
            d, m, v = _adamw_math(ins[k][...], ins[n + k][...], ins[2 * n + k][...], ins[3 * n + k][...])
            outs[k][...] = d
            outs[n + k][...] = m
            outs[2 * n + k][...] = v

    spec = lambda a: pl.BlockSpec((1,) + a.shape[1:], lambda i: (i, 0, 0))
    outs = pl.pallas_call(
        body, name=name, grid=(parts,),
        in_specs=[spec(a) for a in flat], out_specs=[spec(a) for a in flat[:n]] * 3,
        out_shape=[_out(a.shape, F32) for a in flat[:n]] * 3,
        compiler_params=_params(1, VMEM_LIMIT),
    )(*_hbm(*flat))
    outs = [o.reshape(o.shape[0] * o.shape[1], o.shape[2]) for o in outs]
    return outs[:n], outs[n:2 * n], outs[2 * n:]


def _adamw_halves(c, ws, owns, others, ms, vs, name):
    parts = 4
    n = len(ws)
    whole = [_rows_split(_rows_split(a, 2), parts) for a in (*ws, *ms, *vs)]
    halves = [_rows_split(a, parts) for a in (*owns, *others)]

    def body(c_ref, *refs):
        ins, outs = refs[:5 * n], refs[5 * n:]
        mine = pl.program_id(0) == c_ref[0]
        for k in range(n):
            g = jnp.where(mine, ins[3 * n + k][0], ins[4 * n + k][0])
            d, m, v = _adamw_math(ins[k][0, 0], g, ins[n + k][0, 0], ins[2 * n + k][0, 0])
            for slot, val in enumerate((g, d, m, v)):
                outs[slot * n + k][0, 0] = val

    wspec = lambda a: pl.BlockSpec((1, 1) + a.shape[2:], lambda h, r, c_ref: (h, r, 0, 0))
    hspec = lambda a: pl.BlockSpec((1,) + a.shape[1:], lambda h, r, c_ref: (r, 0, 0))
    outs = pl.pallas_call(
        body, name=name,
        grid_spec=pltpu.PrefetchScalarGridSpec(
            num_scalar_prefetch=1, grid=(2, parts),
            in_specs=[wspec(a) for a in whole] + [hspec(a) for a in halves],
            out_specs=[wspec(a) for a in whole[:n]] * 4),
        out_shape=[_out(a.shape, F32) for a in whole[:n]] * 4,
        compiler_params=_params(2, VMEM_LIMIT),
    )(c, *_hbm(*whole, *halves))
    outs = [o.reshape(2 * parts * o.shape[2], o.shape[3]) for o in outs]
    return outs[:n], outs[n:2 * n], outs[2 * n:3 * n], outs[3 * n:]


def _place():
    x, y, c = lax.axis_index("x"), lax.axis_index("y"), lax.axis_index("c")
    peers = [(x ^ (r >> 1), y ^ (r & 1), c) for r in (1, 2, 3)]
    return x, y, c, peers


def _handshake(peers):
    barrier = pltpu.get_barrier_semaphore()
    for peer in peers:
        pl.semaphore_signal(barrier, inc=1, device_id=peer, device_id_type=MESH)
    pl.semaphore_wait(barrier, len(peers))


ANY = pl.BlockSpec(memory_space=pl.ANY)
HBM = pl.BlockSpec(memory_space=pltpu.HBM)
SEM = pl.BlockSpec(memory_space=pltpu.SEMAPHORE)
SPLIT_COPY = pltpu.SideEffectType.DATAFLOW_SIDE_EFFECTING


def _in_hbm(a):
    return pltpu.with_memory_space_constraint(a, pltpu.HBM)


def _split_start(body, name, collective_id, operands, n_sems, after=None):
    n = len(operands)
    extra = [] if after is None else [after]

    def wrapped(*refs):
        at = n + len(extra)
        body(refs[:n], refs[at], refs[at + 1])
        token = refs[-1]
        token[...] = jnp.zeros_like(token)

    outs = pl.pallas_call(
        wrapped, name=name,
        in_specs=[HBM] * n + [ANY] * len(extra),
        out_shape=(pltpu.SemaphoreType.DMA((n_sems,)), pltpu.SemaphoreType.DMA((n_sems,)),
                   *[pltpu.HBM(a.shape, a.dtype) for a in operands], jax.ShapeDtypeStruct((8, 128), F32)),
        out_specs=(SEM, SEM, *[HBM] * n, pl.BlockSpec(memory_space=pltpu.VMEM)),
        input_output_aliases={i: 2 + i for i in range(n)},
        compiler_params=pltpu.CompilerParams(has_side_effects=SPLIT_COPY, collective_id=collective_id),
    )(*[_in_hbm(a) for a in operands], *extra)
    return outs[0], outs[1], list(outs[2:2 + n]), outs[-1]


def _split_wait(body, name, send_sem, recv_sem, operands, after):
    n = len(operands)

    def wrapped(*refs):
        body(refs[:n], refs[n], refs[n + 1])

    outs = pl.pallas_call(
        wrapped, name=name,
        in_specs=[HBM] * n + [SEM, SEM, ANY],
        out_shape=tuple(pltpu.HBM(a.shape, a.dtype) for a in operands),
        out_specs=tuple([HBM] * n),
        input_output_aliases={i: i for i in range(n)},
        compiler_params=pltpu.CompilerParams(has_side_effects=SPLIT_COPY),
    )(*operands, send_sem, recv_sem, after)
    return list(outs)


def _gather_copies(lands, in_chip_order, send_sem, recv_sem):
    peers = _place()[3]
    copies = []
    for a, (land, plain) in enumerate(zip(lands, in_chip_order)):
        own, sent_to, _, _ = _zone_slots(plain)
        copies += [pltpu.make_async_remote_copy(
            src_ref=land.at[own], dst_ref=land.at[sent_to(r + 1)],
            send_sem=send_sem.at[a * 3 + r], recv_sem=recv_sem.at[a * 3 + r],
            device_id=peers[r], device_id_type=MESH) for r in range(3)]
    return copies


def _gather_start(lands, in_chip_order, name, collective_id, after):
    def body(refs, send_sem, recv_sem):
        _handshake(_place()[3])
        for cp in _gather_copies(refs, in_chip_order, send_sem, recv_sem):
            cp.start()

    return _split_start(body, name, collective_id, list(lands), 3 * len(lands), after)


def _gather_wait(send_sem, recv_sem, operands, in_chip_order, after, name):
    def body(refs, send_sem, recv_sem):
        for cp in _gather_copies(refs, in_chip_order, send_sem, recv_sem):
            cp.wait_send()
            cp.wait_recv()

    return _split_wait(body, name, send_sem, recv_sem, operands, after)


def _gather_finish(lands, in_chip_order, with_ici, name):
    n = len(lands)

    def body(*refs):
        land = refs[n:2 * n]
        send_ici, recv_ici, send_d2d, recv_d2d = refs[2 * n:]
        x, y, c, _ = _place()
        ici = _gather_copies(land, in_chip_order, send_ici, recv_ici) if with_ici else []
        for cp in ici:
            cp.start()
        passed = []
        for a in range(n):
            _, _, received, kept = _zone_slots(in_chip_order[a])
            passed += [pltpu.make_async_remote_copy(
                src_ref=land[a].at[received(r + 1)], dst_ref=land[a].at[kept(r + 1)],
                send_sem=send_d2d.at[a * 3 + r], recv_sem=recv_d2d.at[a * 3 + r],
                device_id=(x, y, 1 - c), device_id_type=MESH) for r in range(3)]
        for k, cp in enumerate(passed):
            if with_ici:
                ici[k].wait_recv()
            cp.start()
        for cp in passed:
            cp.wait_recv()
        for cp in ici:
            cp.wait_send()
        for cp in passed:
            cp.wait_send()

    outs = pl.pallas_call(
        body, name=name,
        in_specs=[ANY] * n, out_specs=[ANY] * n,
        out_shape=[_out(l.shape, l.dtype) for l in lands],
        input_output_aliases={a: a for a in range(n)},
        scratch_shapes=[pltpu.SemaphoreType.DMA((3 * n,))] * 4,
    )(*lands)
    return list(outs)


def _slabs(land):
    return land.reshape(N_CHIPS, 2 * land.shape[2], land.shape[3])


def _pair_swap(grads, permuted, name):
    n = len(grads)

    def body(*refs):
        src, dst = refs[:n], refs[n:2 * n]
        send_sem, recv_sem = refs[2 * n:]
        x, y, c, _ = _place()
        copies = [pltpu.make_async_remote_copy(
            src_ref=src[a].at[:, 1] if permuted[a] else src[a].at[:, 1 - c], dst_ref=dst[a],
            send_sem=send_sem.at[a], recv_sem=recv_sem.at[a],
            device_id=(x, y, 1 - c), device_id_type=MESH) for a in range(n)]
        for cp in copies:
            cp.start()
        for cp in copies:
            cp.wait()

    return pl.pallas_call(
        body, name=name,
        in_specs=[ANY] * n, out_specs=[ANY] * n,
        out_shape=[_out((N_CHIPS,) + g.shape[2:], g.dtype) for g in grads],
        scratch_shapes=[pltpu.SemaphoreType.DMA((n,))] * 2,
    )(*grads)


def _swap_copies(refs, permuted, send_sem, recv_sem):
    n = len(refs) // 2
    x, y, c, _ = _place()
    return [pltpu.make_async_remote_copy(
        src_ref=refs[a].at[:, 1] if permuted[a] else refs[a].at[:, 1 - c], dst_ref=refs[n + a],
        send_sem=send_sem.at[a], recv_sem=recv_sem.at[a],
        device_id=(x, y, 1 - c), device_id_type=MESH) for a in range(n)]


def _pair_swap_start(grads, permuted, name, collective_id):
    def body(refs, send_sem, recv_sem):
        x, y, c, _ = _place()
        _handshake([(x, y, 1 - c)])
        for cp in _swap_copies(refs, permuted, send_sem, recv_sem):
            cp.start()

    lands = [lax.empty((N_CHIPS,) + g.shape[2:], g.dtype) for g in grads]
    return _split_start(body, name, collective_id, list(grads) + lands, len(grads))


def _pair_swap_wait(send_sem, recv_sem, operands, permuted, after, name):
    def body(refs, send_sem, recv_sem):
        for cp in _swap_copies(refs, permuted, send_sem, recv_sem):
            cp.wait_send()
            cp.wait_recv()

    return _split_wait(body, name, send_sem, recv_sem, operands, after)


def _scatter_copies(refs, permuted, send_sem, recv_sem):
    n = len(refs) // 2
    x, y, _, peers = _place()
    me = 2 * x + y
    return [pltpu.make_async_remote_copy(
        src_ref=refs[a].at[r + 1] if permuted[a] else refs[a].at[me ^ (r + 1)], dst_ref=refs[n + a].at[r],
        send_sem=send_sem.at[a * 3 + r], recv_sem=recv_sem.at[a * 3 + r],
        device_id=peers[r], device_id_type=MESH) for a in range(n) for r in range(3)]


def _scatter_start(partials, permuted, name, collective_id):
    def body(refs, send_sem, recv_sem):
        _handshake(_place()[3])
        for cp in _scatter_copies(refs, permuted, send_sem, recv_sem):
            cp.start()

    lands = [lax.empty((N_CHIPS - 1,) + p.shape[1:], p.dtype) for p in partials]
    return _split_start(body, name, collective_id, list(partials) + lands, 3 * len(partials))


def _scatter_wait(send_sem, recv_sem, operands, permuted, after, name):
    def body(refs, send_sem, recv_sem):
        for cp in _scatter_copies(refs, permuted, send_sem, recv_sem):
            cp.wait_send()
            cp.wait_recv()

    return _split_wait(body, name, send_sem, recv_sem, operands, after)


def _pair_join(halves, name):
    n = len(halves)

    def body(*refs):
        src, dst = refs[:n], refs[n:2 * n]
        send_sem, recv_sem = refs[2 * n:]
        x, y, c, _ = _place()
        copies = [pltpu.make_async_remote_copy(
            src_ref=src[a], dst_ref=dst[a], send_sem=send_sem.at[a], recv_sem=recv_sem.at[a],
            device_id=(x, y, 1 - c), device_id_type=MESH) for a in range(n)]
        for cp in copies:
            cp.start()
        for cp in copies:
            cp.wait()

    return pl.pallas_call(
        body, name=name,
        in_specs=[ANY] * n, out_specs=[ANY] * n,
        out_shape=[_out(h.shape, F32) for h in halves],
        scratch_shapes=[pltpu.SemaphoreType.DMA((n,))] * 2,
    )(*halves)


def _all_sum_small(v, after, name):
    R, C = v.shape
    n_dev = 8

    def body(v_ref, after_ref, o_ref, buf, send_sem, recv_sem):
        x, y, c, _ = _place()
        me = 4 * x + 2 * y + c
        buf[me] = v_ref[...]
        copies = []
        for k in range(1, n_dev):
            peer = (x ^ (k >> 2), y ^ ((k >> 1) & 1), c ^ (k & 1))
            copies.append(pltpu.make_async_remote_copy(
                src_ref=v_ref, dst_ref=buf.at[me], send_sem=send_sem.at[k - 1], recv_sem=recv_sem.at[k - 1],
                device_id=peer, device_id_type=MESH))
        for cp in copies:
            cp.start()
        for cp in copies:
            cp.wait()
        acc = buf[0]
        for m in range(1, n_dev):
            acc = acc + buf[m]
        o_ref[...] = acc

    return pl.pallas_call(
        body, name=name,
        in_specs=[pl.BlockSpec(memory_space=pltpu.VMEM), ANY], out_specs=pl.BlockSpec(memory_space=pltpu.VMEM),
        out_shape=jax.ShapeDtypeStruct((R, C), F32),
        scratch_shapes=[pltpu.VMEM((n_dev, R, C), F32), pltpu.SemaphoreType.DMA((n_dev - 1,)),
                        pltpu.SemaphoreType.DMA((n_dev - 1,))],
    )(v, after)


class _WholeWeights:
    def __init__(self, w):
        self.w = w

    def weights(self, group, after=None):
        return self.w, None

    def grads_ready(self, group, gw):
        return None

    def grads_sent(self, group, after):
        return None


def _local_step(x, p, target, gains, rel_bias, hooks):
    T, D = x.shape
    S = N_CHIPS

    tied = lambda gain, token: gain if token is None else gain + token[0, 0]
    w, token = hooks.weights("first")
    w = dict(w)
    h1, xn1, g1, u1, a1, f1 = _ffn_fwd(x, tied(gains["ffn1_pre"], token), gains["ffn1_post"], w["ffn1_gate"],
                                       w["ffn1_up"], w["ffn1_down"], "ffn1_fwd")
    more, token = hooks.weights("in", h1)
    w.update(more)
    qkv, un = _norm_proj(h1, tied(gains["mix_pre"], token), w["in"], "qkv_proj")
    bias = _ch_group_bias(_bias_table(rel_bias, "bias_table").transpose(1, 0, 2))
    o_a = _sb_fwd(qkv, "sb_fwd")
    o_b = _ch_fwd(qkv, bias, "ch_fwd")
    w.update(hooks.weights("rest", o_b)[0])
    w_out = w["out"].reshape(D, D)
    h2, mixed, mo = _mix_out_fwd(h1, o_a, o_b, gains["out_sb"], gains["out_ch"], w_out, gains["mix_post"],
                                 "mix_out_fwd")
    h3, xn2, g2, u2, a2, f2 = _ffn_fwd(h2, gains["ffn2_pre"], gains["ffn2_post"], w["ffn2_gate"], w["ffn2_up"],
                                       w["ffn2_down"], "ffn2_fwd")
    w_ple_proj = w["ple_proj"].transpose(1, 0, 2).reshape(p.shape[1], D)
    w_ple_gate = w["ple_gate"].reshape(D, D)

    loss, dh3, dproj, dgate, dg_ple = _ple_loss(h3, p, target, w_ple_proj, w_ple_gate, gains["ple_post"], "ple_loss")
    gw, gg = {}, {"ple_post": dg_ple}
    gw["ple_proj"] = _mm_tn(p[None], dproj, p.shape[1], "dw_ple_proj")
    row_sharded = lambda pair: tuple(o.reshape(S, D // S, D) for o in pair)
    gw["ple_gate"] = row_sharded(_mm_tn(h3[None], dgate[None], 512, "dw_ple_gate"))

    def ffn_bwd(tag, dh, x_in, xn, g_act, u_act, a_act, f, group):
        dgp, dup, df, gg[tag + "_post"] = _ffn_bwd_act(dh, f, gains[tag + "_post"], w[tag + "_down"], g_act, u_act,
                                                       tag + "_bwd_act")
        gw[tag + "_gate"] = _mm_tn(dgp, xn[None], dgp.shape[2], "dw_" + tag + "_gate")
        gw[tag + "_up"] = _mm_tn(dup, xn[None], dup.shape[2], "dw_" + tag + "_up")
        gw[tag + "_down"] = _mm_tn(a_act, df[None], a_act.shape[2], "dw_" + tag + "_down")
        g_pre = gains[tag + "_pre"]
        if group is not None:
            token = hooks.grads_ready(group, gw)
            g_pre = g_pre if token is None else g_pre + token[0, 0]
        dx, gg[tag + "_pre"] = _proj_bwd([dgp, dup], [w[tag + "_gate"], w[tag + "_up"]], x_in, g_pre, dh,
                                         tag + "_bwd_in")
        return dx

    dh2 = ffn_bwd("ffn2", dh3, h2, xn2, g2, u2, a2, f2, None)
    dmo, do_a, do_b, gg["mix_post"], gg["out_sb"], gg["out_ch"] = _mix_out_bwd(
        dh2, mo, gains["mix_post"], w_out, o_a, o_b, gains["out_sb"], gains["out_ch"], "mix_out_bwd")
    gw["out"] = row_sharded(_mm_tn(mixed[None], dmo[None], 512, "dw_out"))
    token = hooks.grads_ready("early", gw)
    if token is not None:
        do_a = do_a + token[0, 0]
    dq_a, dk_a, dv_a = _sb_bwd(qkv, do_a, o_a, "sb_bwd")
    token = hooks.grads_sent("early", dq_a)
    if token is not None:
        do_b = do_b + token[0, 0]
    dq_b, dk_b, dv_b, dbias = _ch_bwd(qkv, bias, do_b, "ch_bwd")
    g_rel = _bias_grad(_ch_fold_bias_grad(dbias).transpose(1, 0, 2), "bias_grad")
    dqkv = jnp.concatenate([dq_a, dk_a, dv_a, dq_b, dk_b, dv_b], axis=1)
    gw["in"] = _mm_tn(un[None], dqkv, 512, "dw_in", groups=S)
    dh1, gg["mix_pre"] = _proj_bwd([dqkv], [w["in"]], h1, gains["mix_pre"], dh2, "qkv_bwd_in")
    dx = ffn_bwd("ffn1", dh1, x, xn1, g1, u1, a1, f1, "late")
    return loss, dx, gw, gg, g_rel


BIG = ["ffn1_gate", "ffn1_up", "ffn1_down", "in", "out", "ffn2_gate", "ffn2_up", "ffn2_down", "ple_proj", "ple_gate"]
GAINS = ["ffn1_pre", "ffn1_post", "mix_pre", "mix_post", "out_sb", "out_ch", "ffn2_pre", "ffn2_post", "ple_post"]
TRANSPOSED = ("w_ffn1_gate", "w_ffn1_up", "w_ffn2_gate", "w_ffn2_up")
PERMUTED = ("ffn1_gate", "ffn1_up", "ffn1_down", "ffn2_gate", "ffn2_up", "ffn2_down")
W_GROUPS = {"first": ["ffn1_gate", "ffn1_up", "ffn1_down"], "in": ["in"],
            "rest": ["out", "ffn2_gate", "ffn2_up", "ffn2_down", "ple_proj", "ple_gate"]}
G_GROUPS = {"early": ["ple_proj", "ple_gate", "ffn2_gate", "ffn2_up", "ffn2_down", "out"],
            "late": ["in", "ffn1_gate", "ffn1_up", "ffn1_down"]}
ORDER = ["g_ffn1_pre", "g_ffn1_post", "w_ffn1_gate", "w_ffn1_up", "w_ffn1_down", "g_mix_pre", "g_mix_post", "w_in",
         "g_out_sb", "g_out_ch", "rel_bias", "w_out", "g_ffn2_pre", "g_ffn2_post", "w_ffn2_gate", "w_ffn2_up",
         "w_ffn2_down", "w_ple_proj", "w_ple_gate", "g_ple_post"]


def kernel(x, p, g_ffn1_pre, g_ffn1_post, w_ffn1_gate, w_ffn1_up, w_ffn1_down, g_mix_pre, g_mix_post, w_in, g_out_sb, g_out_ch, rel_bias, w_out, g_ffn2_pre, g_ffn2_post, w_ffn2_gate, w_ffn2_up, w_ffn2_down, w_ple_proj, w_ple_gate, g_ple_post, loss_target, m_g_ffn1_pre, m_g_ffn1_post, m_w_ffn1_gate, m_w_ffn1_up, m_w_ffn1_down, m_g_mix_pre, m_g_mix_post, m_w_in, m_g_out_sb, m_g_out_ch, m_rel_bias, m_w_out, m_g_ffn2_pre, m_g_ffn2_post, m_w_ffn2_gate, m_w_ffn2_up, m_w_ffn2_down, m_w_ple_proj, m_w_ple_gate, m_g_ple_post, v_g_ffn1_pre, v_g_ffn1_post, v_w_ffn1_gate, v_w_ffn1_up, v_w_ffn1_down, v_g_mix_pre, v_g_mix_post, v_w_in, v_g_out_sb, v_g_out_ch, v_rel_bias, v_w_out, v_g_ffn2_pre, v_g_ffn2_post, v_w_ffn2_gate, v_w_ffn2_up, v_w_ffn2_down, v_w_ple_proj, v_w_ple_gate, v_g_ple_post):
    args = dict(locals())
    take = lambda a, n: a[0].T if n in TRANSPOSED else a[0]
    wts = {n: take(args[n], n) for n in ORDER}
    ms = {n: take(args["m_" + n], n) for n in ORDER}
    vs = {n: take(args["v_" + n], n) for n in ORDER}
    gains = {n: wts["g_" + n][None] for n in GAINS}

    c_idx = lax.axis_index("c").astype(jnp.int32).reshape(1)
    me_idx = (2 * lax.axis_index("x") + lax.axis_index("y")).astype(jnp.int32).reshape(1)
    south = lax.axis_index("c") == 0

    plain = lambda names: [n not in PERMUTED for n in names]
    lands = dict(zip(BIG, _cast_into_own_slot(me_idx, c_idx, [wts["w_" + n] for n in BIG], plain(BIG), "cast_weights")))

    class Overlapped:
        def __init__(self):
            self.started = {}
            self.flying = {}

        def start(self, group, collective_id, after):
            names = W_GROUPS[group]
            self.flying[group] = _gather_start([lands[n] for n in names], plain(names), "gather_%s_start" % group,
                                               collective_id, after)
            return self.flying[group][3]

        def weights(self, group, after=None):
            names = W_GROUPS[group]
            token = None
            if group == "first":
                zones = _gather_finish([lands[n] for n in names], plain(names), True, "gather_first")
                token = self.start("rest", 4, self.start("in", 1, zones[0]))
            else:
                send_sem, recv_sem, zones, _ = self.flying[group]
                zones = _gather_wait(send_sem, recv_sem, zones, plain(names), after, "gather_%s_wait" % group)
                zones = _gather_finish(zones, plain(names), False, "gather_%s_finish" % group)
            return {n: _slabs(z) for n, z in zip(names, zones)}, token

        def grads_ready(self, group, gw):
            names = G_GROUPS[group]
            perm = [n in PERMUTED for n in names]
            halved = lambda g: g.reshape(N_CHIPS, 2, g.shape[1] // 2, g.shape[2])
            mine = [halved(gw[n][0]) for n in names]
            narrow = [halved(gw[n][1]) for n in names]
            if group == "late":
                return self.scatter(group, names, perm, mine, _pair_swap(narrow, perm, "grad_pair_swap_late"))
            self.swapping = names, perm, mine, _pair_swap_start(narrow, perm, "grad_pair_swap_start_early", 5)
            return self.swapping[3][3]

        def grads_sent(self, group, after):
            names, perm, mine, (send_sem, recv_sem, operands, _) = self.swapping
            operands = _pair_swap_wait(send_sem, recv_sem, operands, perm, after, "grad_pair_swap_wait_early")
            return self.scatter(group, names, perm, mine, operands[len(names):])

        def scatter(self, group, names, perm, mine, got):
            partial = _pair_add(c_idx, mine, got, perm, "grad_pair_add_" + group)
            send_sem, recv_sem, operands, token = _scatter_start(partial, perm, "grad_scatter_start_" + group,
                                                                 {"early": 2, "late": 3}[group])
            self.started[group] = names, perm, send_sem, recv_sem, operands, token
            return token

    def reduce_finish(state, after, tag):
        names, perm, send_sem, recv_sem, operands, _ = state
        operands = _scatter_wait(send_sem, recv_sem, operands, perm, after, "grad_scatter_wait_" + tag)
        n = len(names)
        own = _chip_add(me_idx, operands[:n], operands[n:], perm, "grad_chip_add_" + tag)
        return own, _pair_join(own, "grad_pair_join_" + tag)

    hooks = Overlapped()
    loss, dx, gw, gg, g_rel = _local_step(x[0], p[0, 0], loss_target[0], gains, wts["rel_bias"], hooks)

    grads, delta, new_m, new_v = {}, {}, {}, {}

    def finish(group, after):
        own, other = reduce_finish(hooks.started[group], after, group)
        names = ["w_" + n for n in G_GROUPS[group]]
        g, d, m, v = _adamw_halves(c_idx, [wts[n] for n in names], own, other, [ms[n] for n in names],
                                   [vs[n] for n in names], "adamw_" + group)
        for n, gg_, dd, mm, vv in zip(names, g, d, m, v):
            grads[n], delta[n], new_m[n], new_v[n] = gg_, dd, mm, vv
        return d[0]

    finish("late", finish("early", dx))

    pieces = [gg[n].reshape(-1, 128) for n in GAINS] + [jnp.pad(g_rel, ((0, 0), (0, N_REL_PAD - N_REL))).reshape(-1, 128)]
    summed = _all_sum_small(jnp.concatenate(pieces + [loss], axis=0), delta["w_in"], "small_grad_sum")
    at = 0
    for n, piece in zip(GAINS, pieces[:-1]):
        grads["g_" + n] = summed[at:at + piece.shape[0]].reshape(1, -1)[0]
        at += piece.shape[0]
    grads["rel_bias"] = summed[at:at + pieces[-1].shape[0]].reshape(N_HEADS, N_REL_PAD)[:, :N_REL]
    loss = summed[at + pieces[-1].shape[0], 0]

    small = ["g_" + n for n in GAINS] + ["rel_bias"]
    as_rows = lambda a: (a.reshape(-1, 128) if a.size % 128 == 0 else jnp.pad(a, ((0, 0), (0, N_REL_PAD - N_REL))).reshape(-1, 128))
    d, m, v = _adamw([as_rows(wts[n]) for n in small], [as_rows(grads[n]) for n in small],
                     [as_rows(ms[n]) for n in small], [as_rows(vs[n]) for n in small], 1, "adamw_small")
    for n, dd, mm, vv in zip(small, d, m, v):
        back = (lambda a: a.reshape(N_HEADS, N_REL_PAD)[:, :N_REL]) if n == "rel_bias" else (lambda a: a.reshape(-1))
        delta[n], new_m[n], new_v[n] = back(dd), back(mm), back(vv)

    outs = [loss, dx[None]]
    for table in (grads, delta, new_m, new_v):
        outs += [(table[n].T if n in TRANSPOSED else table[n])[None] for n in ORDER]
    return tuple(outs)
```

```python
import functools

import jax
import jax.numpy as jnp
from jax import lax
from jax.experimental import pallas as pl
from jax.experimental.pallas import tpu as pltpu

F32 = jnp.float32
BF16 = jnp.bfloat16
EPS = 1e-6
N_CHIPS = 4
HEAD_DIM = 64
N_HEADS = 8
CHUNK = 64
LOOKBACK = 8
BAND = (LOOKBACK + 1) * CHUNK
PAD = LOOKBACK * CHUNK
REL_CLIP = 128
N_REL = 2 * REL_CLIP + 1
N_REL_PAD = 384
SB_BLOCK = 256
PAIR = 2 * HEAD_DIM
SB_PAIRS = 2
ATT_SCALE = HEAD_DIM ** -0.5
NEG_INF = -1e30
ROW_BLOCK = 512
WIDE_ROW_BLOCK = 1024
VMEM_LIMIT_WIDE = 56 * 1024 * 1024
VMEM_LIMIT = 48 * 1024 * 1024
MESH = pl.DeviceIdType.MESH

ADAM_LR = 0.001
ADAM_B1 = 0.9
ADAM_B2 = 0.999
ADAM_EPS = 1e-08
ADAM_WD = 0.01
ADAM_STEP = 10

NT = (((1,), (1,)), ((), ()))
TN = (((0,), (0,)), ((), ()))


def _params(n_grid, vmem=None):
    return pltpu.CompilerParams(dimension_semantics=("arbitrary",) * n_grid, vmem_limit_bytes=vmem)


def _hbm(*arrays):
    return [pltpu.with_memory_space_constraint(a, pltpu.HBM) for a in arrays]


def _out(shape, dtype):
    return pltpu.HBM(shape, dtype)


def _dot(a, b, dims=None):
    if dims is None:
        return jnp.dot(a, b, preferred_element_type=F32)
    return lax.dot_general(a, b, dims, preferred_element_type=F32)


def _sigmoid(x):
    return 1.0 / (1.0 + jnp.exp(-x))


def _rms_fwd(x, g):
    r = lax.rsqrt(jnp.mean(x * x, axis=-1, keepdims=True) + EPS)
    return x * r * g


def _rms_bwd(x, g, dy):
    r = lax.rsqrt(jnp.mean(x * x, axis=-1, keepdims=True) + EPS)
    xh = x * r
    dg = jnp.sum(dy * xh, axis=0, keepdims=True)
    t = dy * g
    dx = r * (t - xh * jnp.mean(t * xh, axis=-1, keepdims=True))
    return dx, dg


def _accumulate(ref, val, first):
    @pl.when(first)
    def _():
        ref[...] = val

    @pl.when(jnp.logical_not(first))
    def _():
        ref[...] += val


def _split2(x):
    hi = x.astype(BF16)
    lo = (x - hi.astype(F32)).astype(BF16)
    return hi, lo


def _ffn_fwd(x, g_pre, g_post, wg, wu, wd, name):
    T, D = x.shape
    S, FS, _ = wg.shape
    tm = min(WIDE_ROW_BLOCK, T)

    def body(x_ref, gpre_ref, gpost_ref, wg_ref, wu_ref, wd_ref,
             h_ref, xn_ref, g_ref, u_ref, a_ref, f_ref):
        k = pl.program_id(1)

        @pl.when(k == 0)
        def _():
            xn_ref[...] = _rms_fwd(x_ref[...], gpre_ref[...]).astype(BF16)

        xn = xn_ref[...]
        g = _dot(xn, wg_ref[0], NT)
        u = _dot(xn, wu_ref[0], NT)
        g_ref[0] = g
        u_ref[0] = u
        a = (g * _sigmoid(g) * u).astype(BF16)
        a_ref[0] = a
        _accumulate(f_ref, _dot(a, wd_ref[0]), k == 0)

        @pl.when(k == S - 1)
        def _():
            h_ref[...] = x_ref[...] + 0.5 * _rms_fwd(f_ref[...], gpost_ref[...])

    row = pl.BlockSpec((tm, D), lambda i, k: (i, 0))
    vec = pl.BlockSpec((1, D), lambda i, k: (0, 0))
    act = pl.BlockSpec((1, tm, FS), lambda i, k: (k, i, 0))
    return pl.pallas_call(
        body, name=name, grid=(T // tm, S),
        in_specs=[row, vec, vec] + [pl.BlockSpec((1, FS, D), lambda i, k: (k, 0, 0))] * 3,
        out_specs=[row, row, act, act, act, row],
        out_shape=[_out((T, D), F32), _out((T, D), BF16),
                   _out((S, T, FS), F32), _out((S, T, FS), F32),
                   _out((S, T, FS), BF16), _out((T, D), F32)],
        compiler_params=_params(2, VMEM_LIMIT_WIDE),
    )(*_hbm(x, g_pre, g_post, wg, wu, wd))


def _ffn_up(x, g_pre, wg, wu, name):
    T, D = x.shape
    S, FS, _ = wg.shape
    tm = min(WIDE_ROW_BLOCK, T)

    def body(x_ref, gpre_ref, wg_ref, wu_ref, xn_ref, g_ref, u_ref, a_ref):
        @pl.when(pl.program_id(1) == 0)
        def _():
            xn_ref[...] = _rms_fwd(x_ref[...], gpre_ref[...]).astype(BF16)

        xn = xn_ref[...]
        g = _dot(xn, wg_ref[0], NT)
        u = _dot(xn, wu_ref[0], NT)
        g_ref[0] = g
        u_ref[0] = u
        a_ref[0] = (g * _sigmoid(g) * u).astype(BF16)

    row = pl.BlockSpec((tm, D), lambda i, k: (i, 0))
    act = pl.BlockSpec((1, tm, FS), lambda i, k: (k, i, 0))
    return pl.pallas_call(
        body, name=name, grid=(T // tm, S),
        in_specs=[row, pl.BlockSpec((1, D), lambda i, k: (0, 0))] + [pl.BlockSpec((1, FS, D), lambda i, k: (k, 0, 0))] * 2,
        out_specs=[row, act, act, act],
        out_shape=[_out((T, D), BF16), _out((S, T, FS), F32), _out((S, T, FS), F32), _out((S, T, FS), BF16)],
        compiler_params=_params(2, VMEM_LIMIT_WIDE),
    )(*_hbm(x, g_pre, wg, wu))


def _ffn_down(x, a, g_post, wd, name):
    T, D = x.shape
    S, FS, _ = wd.shape
    tm = min(WIDE_ROW_BLOCK, T)

    def body(x_ref, a_ref, gpost_ref, wd_ref, h_ref, f_ref):
        k = pl.program_id(1)
        _accumulate(f_ref, _dot(a_ref[0], wd_ref[0]), k == 0)

        @pl.when(k == S - 1)
        def _():
            h_ref[...] = x_ref[...] + 0.5 * _rms_fwd(f_ref[...], gpost_ref[...])

    row = pl.BlockSpec((tm, D), lambda i, k: (i, 0))
    return pl.pallas_call(
        body, name=name, grid=(T // tm, S),
        in_specs=[row, pl.BlockSpec((1, tm, FS), lambda i, k: (k, i, 0)), pl.BlockSpec((1, D), lambda i, k: (0, 0)),
                  pl.BlockSpec((1, FS, D), lambda i, k: (k, 0, 0))],
        out_specs=[row, row],
        out_shape=[_out((T, D), F32), _out((T, D), F32)],
        compiler_params=_params(2, VMEM_LIMIT_WIDE),
    )(*_hbm(x, a, g_post, wd))


def _ffn_bwd_act(dh, f, g_post, wd, g_act, u_act, name):
    T, D = dh.shape
    S, FS, _ = wd.shape
    tm = min(WIDE_ROW_BLOCK, T)

    def body(dh_ref, f_ref, gpost_ref, wd_ref, g_ref, u_ref, dgp_ref, dup_ref, df_ref, dgain_ref, df_s):
        i, k = pl.program_id(0), pl.program_id(1)

        @pl.when(k == 0)
        def _():
            df, dgain = _rms_bwd(f_ref[...], gpost_ref[...], 0.5 * dh_ref[...])
            df_s[...] = df.astype(BF16)
            df_ref[...] = df_s[...]
            _accumulate(dgain_ref, dgain, i == 0)

        da = _dot(df_s[...], wd_ref[0], NT)
        g = g_ref[0]
        s = _sigmoid(g)
        dup_ref[0] = (da * (g * s)).astype(BF16)
        dgp_ref[0] = (da * u_ref[0] * (s * (1.0 + g * (1.0 - s)))).astype(BF16)

    row = pl.BlockSpec((tm, D), lambda i, k: (i, 0))
    vec = pl.BlockSpec((1, D), lambda i, k: (0, 0))
    act = pl.BlockSpec((1, tm, FS), lambda i, k: (k, i, 0))
    return pl.pallas_call(
        body, name=name, grid=(T // tm, S),
        in_specs=[row, row, vec, pl.BlockSpec((1, FS, D), lambda i, k: (k, 0, 0)), act, act],
        out_specs=[act, act, row, vec],
        out_shape=[_out((S, T, FS), BF16), _out((S, T, FS), BF16),
                   _out((T, D), BF16), _out((1, D), F32)],
        scratch_shapes=[pltpu.VMEM((tm, D), BF16)],
        compiler_params=_params(2, VMEM_LIMIT_WIDE),
    )(*_hbm(dh, f, g_post, wd, g_act, u_act))


def _proj_bwd(dys, ws, x, g_pre, dh, name):
    T, D = x.shape
    n = len(dys)
    flat = dys[0].ndim == 2
    S = ws[0].shape[0]
    N = ws[0].shape[2] if flat else ws[0].shape[1]
    tm = min(WIDE_ROW_BLOCK, T)

    def body(*refs):
        dy_refs, w_refs = refs[:n], refs[n:2 * n]
        x_ref, gpre_ref, dh_ref, dx_ref, dgain_ref, acc_s = refs[2 * n:]
        i, k = pl.program_id(0), pl.program_id(1)
        part = None
        for dy_ref, w_ref in zip(dy_refs, w_refs):
            term = _dot(dy_ref[...], w_ref[0], NT) if flat else _dot(dy_ref[0], w_ref[0])
            part = term if part is None else part + term
        _accumulate(acc_s, part, k == 0)

        @pl.when(k == S - 1)
        def _():
            dx, dgain = _rms_bwd(x_ref[...], gpre_ref[...], acc_s[...])
            dx_ref[...] = dh_ref[...] + dx
            _accumulate(dgain_ref, dgain, i == 0)

    row = pl.BlockSpec((tm, D), lambda i, k: (i, 0))
    vec = pl.BlockSpec((1, D), lambda i, k: (0, 0))
    return pl.pallas_call(
        body, name=name, grid=(T // tm, S),
        in_specs=[pl.BlockSpec((tm, N), lambda i, k: (i, k)) if flat else pl.BlockSpec((1, tm, N), lambda i, k: (k, i, 0))] * n
        + [pl.BlockSpec((1,) + ws[0].shape[1:], lambda i, k: (k, 0, 0))] * n + [row, vec, row],
        out_specs=[row, vec],
        out_shape=[_out((T, D), F32), _out((1, D), F32)],
        scratch_shapes=[pltpu.VMEM((tm, D), F32)],
        compiler_params=_params(2, VMEM_LIMIT_WIDE),
    )(*_hbm(*dys, *ws, x, g_pre, dh))


def _mm_tn(a, b, bm, name, groups=None):
    ga, T, M = a.shape
    if groups is None:
        gb, _, N = b.shape
        b_spec = pl.BlockSpec((1, T, N), (lambda g, m: (g, 0, 0)) if gb > 1 else (lambda g, m: (0, 0, 0)))
    else:
        gb, N = groups, b.shape[1] // groups
        b_spec = pl.BlockSpec((T, N), lambda g, m: (0, g))
    G = max(ga, gb)

    def body(a_ref, b_ref, o_ref, narrow_ref):
        bv = b_ref[0] if groups is None else b_ref[...]
        o_ref[0] = _dot(a_ref[0].astype(BF16), bv.astype(BF16), TN)
        narrow_ref[0] = o_ref[0].astype(BF16)

    out = pl.BlockSpec((1, bm, N), lambda g, m: (g, m, 0))
    return pl.pallas_call(
        body, name=name, grid=(G, M // bm),
        in_specs=[pl.BlockSpec((1, T, bm), (lambda g, m: (g, 0, m)) if ga > 1 else (lambda g, m: (0, 0, m))), b_spec],
        out_specs=[out, out],
        out_shape=[_out((G, M, N), F32), _out((G, M, N), BF16)],
        compiler_params=_params(2, VMEM_LIMIT),
    )(*_hbm(a, b))


def _norm_proj(x, g_pre, w, name):
    T, D = x.shape
    S, _, N = w.shape
    tm = min(WIDE_ROW_BLOCK, T)

    def body(x_ref, g_ref, w_ref, o_ref, xn_ref, xn_s):
        @pl.when(pl.program_id(1) == 0)
        def _():
            xn_s[...] = _rms_fwd(x_ref[...], g_ref[...]).astype(BF16)
            xn_ref[...] = xn_s[...]

        o_ref[...] = _dot(xn_s[...], w_ref[0]).astype(BF16)

    row = pl.BlockSpec((tm, D), lambda i, k: (i, 0))
    return pl.pallas_call(
        body, name=name, grid=(T // tm, S),
        in_specs=[row, pl.BlockSpec((1, D), lambda i, k: (0, 0)), pl.BlockSpec((1, D, N), lambda i, k: (k, 0, 0))],
        out_specs=[pl.BlockSpec((tm, N), lambda i, k: (i, k)), row],
        out_shape=[_out((T, S * N), BF16), _out((T, D), BF16)],
        scratch_shapes=[pltpu.VMEM((tm, D), BF16)],
        compiler_params=_params(2, VMEM_LIMIT_WIDE),
    )(*_hbm(x, g_pre, w))


def _mix_out_fwd(h, o_a, o_b, g_sb, g_ch, w_out, g_post, name):
    T, D = h.shape
    W = g_sb.shape[1]
    tm = min(WIDE_ROW_BLOCK, T)

    def body(h_ref, oa_ref, ob_ref, gsb_ref, gch_ref, w_ref, gpost_ref, h2_ref, mixed_ref, mo_ref):
        mixed_ref[:, :W] = _rms_fwd(oa_ref[...], gsb_ref[...]).astype(BF16)
        mixed_ref[:, W:] = _rms_fwd(ob_ref[...], gch_ref[...]).astype(BF16)
        mo = _dot(mixed_ref[...], w_ref[...])
        mo_ref[...] = mo
        h2_ref[...] = h_ref[...] + _rms_fwd(mo, gpost_ref[...])

    row = pl.BlockSpec((tm, D), lambda i: (i, 0))
    part = pl.BlockSpec((tm, W), lambda i: (i, 0))
    half = pl.BlockSpec((1, W), lambda i: (0, 0))
    return pl.pallas_call(
        body, name=name, grid=(T // tm,),
        in_specs=[row, part, part, half, half, pl.BlockSpec((D, D), lambda i: (0, 0)), pl.BlockSpec((1, D), lambda i: (0, 0))],
        out_specs=[row, row, row],
        out_shape=[_out((T, D), F32), _out((T, D), BF16),
                   _out((T, D), F32)],
        compiler_params=_params(1, VMEM_LIMIT_WIDE),
    )(*_hbm(h, o_a, o_b, g_sb, g_ch, w_out, g_post))


def _mix_out_bwd(dh, mo, g_post, w_out, o_a, o_b, g_sb, g_ch, name):
    T, D = dh.shape
    W = g_sb.shape[1]
    tm = min(WIDE_ROW_BLOCK, T)

    def body(dh_ref, mo_ref, gpost_ref, w_ref, oa_ref, ob_ref, gsb_ref, gch_ref,
             dmo_ref, doa_ref, dob_ref, dgpost_ref, dgsb_ref, dgch_ref):
        first = pl.program_id(0) == 0
        dmo, dgpost = _rms_bwd(mo_ref[...], gpost_ref[...], dh_ref[...])
        dmo_ref[...] = dmo.astype(BF16)
        dmix = _dot(dmo_ref[...], w_ref[...], NT)
        doa_ref[...], dgsb = _rms_bwd(oa_ref[...], gsb_ref[...], dmix[:, :W])
        dob_ref[...], dgch = _rms_bwd(ob_ref[...], gch_ref[...], dmix[:, W:])
        _accumulate(dgpost_ref, dgpost, first)
        _accumulate(dgsb_ref, dgsb, first)
        _accumulate(dgch_ref, dgch, first)

    row = pl.BlockSpec((tm, D), lambda i: (i, 0))
    part = pl.BlockSpec((tm, W), lambda i: (i, 0))
    vec = pl.BlockSpec((1, D), lambda i: (0, 0))
    half = pl.BlockSpec((1, W), lambda i: (0, 0))
    return pl.pallas_call(
        body, name=name, grid=(T // tm,),
        in_specs=[row, row, vec, pl.BlockSpec((D, D), lambda i: (0, 0)), part, part, half, half],
        out_specs=[row, part, part, vec, half, half],
        out_shape=[_out((T, D), BF16), _out((T, W), F32),
                   _out((T, W), F32), _out((1, D), F32),
                   _out((1, W), F32), _out((1, W), F32)],
        compiler_params=_params(1, VMEM_LIMIT_WIDE),
    )(*_hbm(dh, mo, g_post, w_out, o_a, o_b, g_sb, g_ch))


def _ple_loss(h, p, target, w_proj, w_gate, g_post, name):
    T, D = h.shape
    P = p.shape[1]
    S = N_CHIPS
    C = D // S
    tm = min(ROW_BLOCK, T)

    def body(h_ref, p_ref, t_ref, wp_ref, wg_ref, g_ref, loss_ref, dh_ref, dproj_ref, dgate_ref, dgain_ref):
        first = pl.program_id(0) == 0
        h3 = h_ref[...]
        proj = _dot(p_ref[...].astype(BF16), wp_ref[...])
        s = _sigmoid(_dot(h3.astype(BF16), wg_ref[...]))
        e = proj * s
        diff = h3 + _rms_fwd(e, g_ref[...]) - t_ref[...]
        part = 0.5 * jnp.sum(jnp.mean(diff * diff, axis=-1, keepdims=True), axis=0, keepdims=True)
        _accumulate(loss_ref, jnp.broadcast_to(part, loss_ref.shape), first)
        dy = diff * (1.0 / D)
        de, dgain = _rms_bwd(e, g_ref[...], dy)
        _accumulate(dgain_ref, dgain, first)
        dproj = (de * s).astype(BF16)
        for j in range(S):
            dproj_ref[j] = dproj[:, j * C:(j + 1) * C]
        dgate_ref[...] = (de * proj * s * (1.0 - s)).astype(BF16)
        dh_ref[...] = dy + _dot(dgate_ref[...], wg_ref[...], NT)

    row = pl.BlockSpec((tm, D), lambda i: (i, 0))
    vec = pl.BlockSpec((1, D), lambda i: (0, 0))
    return pl.pallas_call(
        body, name=name, grid=(T // tm,),
        in_specs=[row, pl.BlockSpec((tm, P), lambda i: (i, 0)), row,
                  pl.BlockSpec((P, D), lambda i: (0, 0)), pl.BlockSpec((D, D), lambda i: (0, 0)), vec],
        out_specs=[pl.BlockSpec((8, 128), lambda i: (0, 0)), row,
                   pl.BlockSpec((S, tm, C), lambda i: (0, i, 0)), row, vec],
        out_shape=[_out((8, 128), F32), _out((T, D), F32),
                   _out((S, T, C), BF16), _out((T, D), BF16),
                   _out((1, D), F32)],
        compiler_params=_params(1, VMEM_LIMIT_WIDE),
    )(*_hbm(h, p, target, w_proj, w_gate, g_post))


def _sb_scores(q, kj, mask):
    z = _dot(q, kj, NT)
    sp = jnp.maximum(z, 0.0) + jnp.log(1.0 + jnp.exp(-jnp.abs(z)))
    return z, sp if mask is None else jnp.where(mask, sp, 0.0)


def _strict_causal():
    rows = lax.broadcasted_iota(jnp.int32, (SB_BLOCK, SB_BLOCK), 0)
    cols = lax.broadcasted_iota(jnp.int32, (SB_BLOCK, SB_BLOCK), 1)
    return cols < rows


def _tri(cmp):
    r = lax.broadcasted_iota(jnp.int32, (2 * SB_BLOCK, SB_BLOCK), 0) % SB_BLOCK
    c = lax.broadcasted_iota(jnp.int32, (2 * SB_BLOCK, SB_BLOCK), 1)
    return jnp.where(cmp(r, c), 1.0, 0.0).astype(BF16)


def _cum(x, tri):
    return _dot(jnp.concatenate(_split2(x), axis=1), tri)


def _pair_lanes():
    lane = lax.broadcasted_iota(jnp.int32, (1, PAIR), 1)
    return [lane < HEAD_DIM, lane >= HEAD_DIM]


def _only(lanes, x):
    return jnp.where(lanes, x, jnp.zeros_like(x))


def _sb_fwd(qkv, name):
    T = qkv.shape[0]
    B = SB_BLOCK
    W = SB_PAIRS * PAIR
    steps = N_HEADS // (2 * SB_PAIRS)
    heads = [(p, h) for p in range(SB_PAIRS) for h in range(2)]

    def body(q_ref, k_ref, v_ref, o_ref):
        i = pl.program_id(1)
        after = _tri(lambda r, c: r > c)
        lanes = _pair_lanes()
        cols = [slice(p * PAIR, (p + 1) * PAIR) for p in range(SB_PAIRS)]
        q = {(p, h): _only(lanes[h], q_ref[:, cols[p]] * ATT_SCALE) for p, h in heads}

        def tiles(j, carries, mask):
            at = pl.ds(pl.multiple_of(j * B, B), B)
            scores = [_sb_scores(q[ph], k_ref[at, cols[ph[0]]], mask) for ph in heads]
            laters = [_cum(sp, after) for _, sp in scores]
            out = []
            for ph, (z, sp), later, (run, acc) in zip(heads, scores, laters, carries):
                a = jnp.exp(z - sp - later - run)
                if mask is not None:
                    a = jnp.where(mask, a, 0.0)
                out.append((run + later[:, 0:1] + sp[:, 0:1],
                            acc + _dot(a.astype(BF16), _only(lanes[ph[1]], v_ref[at, cols[ph[0]]]))))
            return tuple(out)

        zero = (jnp.zeros((B, 1), F32), jnp.zeros((B, PAIR), F32))
        carries = tiles(i, (zero,) * len(heads), _strict_causal())
        carries = lax.fori_loop(0, i, lambda jj, cs: tiles(i - 1 - jj, cs, None), carries)
        for p in range(SB_PAIRS):
            o_ref[:, cols[p]] = carries[2 * p][1] + carries[2 * p + 1][1]

    blk = lambda off: pl.BlockSpec((B, W), lambda g, i: (i, g + off))
    full = lambda off: pl.BlockSpec((T, W), lambda g, i: (0, g + off))
    return pl.pallas_call(
        body, name=name, grid=(steps, T // B),
        in_specs=[blk(0), full(steps), full(2 * steps)],
        out_specs=blk(0),
        out_shape=_out((T, N_HEADS * HEAD_DIM), F32),
        compiler_params=_params(2, VMEM_LIMIT),
    )(*_hbm(qkv, qkv, qkv))


def _sb_bwd(qkv, do, o, after, name):
    T = qkv.shape[0]
    B = SB_BLOCK
    W = SB_PAIRS * PAIR
    steps = N_HEADS // (2 * SB_PAIRS)
    n_blocks = T // B
    heads = [(p, h) for p in range(SB_PAIRS) for h in range(2)]

    def body(q_ref, k_ref, v_ref, do_ref, o_ref, dq_ref, dk_ref, dv_ref, dk_s, dv_s):
        i = pl.program_id(1)

        @pl.when(i == 0)
        def _():
            dk_s[...] = jnp.zeros_like(dk_s)
            dv_s[...] = jnp.zeros_like(dv_s)

        after = _tri(lambda r, c: r > c)
        since = _tri(lambda r, c: r >= c)
        lanes = _pair_lanes()
        cols = [slice(p * PAIR, (p + 1) * PAIR) for p in range(SB_PAIRS)]
        q = {(p, h): _only(lanes[h], q_ref[:, cols[p]] * ATT_SCALE) for p, h in heads}
        do = {(p, h): _only(lanes[h], do_ref[:, cols[p]].astype(BF16)) for p, h in heads}
        total = {ph: jnp.sum(do[ph].astype(F32) * o_ref[:, cols[ph[0]]], axis=1, keepdims=True) for ph in heads}

        def tiles(j, carries, mask):
            at = pl.ds(pl.multiple_of(j * B, B), B)
            ks = [k_ref[at, c] for c in cols]
            vs = [v_ref[at, c] for c in cols]
            scores = [_sb_scores(q[ph], ks[ph[0]], mask) for ph in heads]
            laters = [_cum(sp, after) for _, sp in scores]
            das = [_dot(do[ph], vs[ph[0]], NT) for ph in heads]
            a_s, gs = [], []
            for (z, sp), later, da, carry in zip(scores, laters, das, carries):
                a = jnp.exp(z - sp - later - carry[0])
                if mask is not None:
                    a = jnp.where(mask, a, 0.0)
                a = a.astype(BF16)
                a_s.append(a)
                gs.append(a.astype(F32) * da)
            sinces = [_cum(g, since) for g in gs]
            dzs = []
            for ph, (_, sp), g, from_s, carry in zip(heads, scores, gs, sinces, carries):
                g_before = total[ph] - carry[1] - from_s
                fail = jnp.exp(-sp)
                dz = fail * (g + g_before) - g_before
                if mask is not None:
                    dz = jnp.where(mask, dz, 0.0)
                dzs.append(dz.astype(BF16))
            out = []
            for ph, (_, sp), a, dz, later, from_s, carry in zip(heads, scores, a_s, dzs, laters, sinces, carries):
                dk_s[at, cols[ph[0]]] += _dot(dz, q[ph], TN)
                dv_s[at, cols[ph[0]]] += _dot(a, do[ph], TN)
                out.append((carry[0] + later[:, 0:1] + sp[:, 0:1], carry[1] + from_s[:, 0:1],
                            carry[2] + _dot(dz, _only(lanes[ph[1]], ks[ph[0]]))))
            return tuple(out)

        col = jnp.zeros((B, 1), F32)
        zero = (col, col, jnp.zeros((B, PAIR), F32))
        carries = tiles(i, (zero,) * len(heads), _strict_causal())
        last = lax.fori_loop(0, i, lambda jj, cs: tiles(i - 1 - jj, cs, None), carries)
        for p in range(SB_PAIRS):
            dq_ref[:, cols[p]] = ((last[2 * p][2] + last[2 * p + 1][2]) * ATT_SCALE).astype(BF16)

        @pl.when(i == n_blocks - 1)
        def _():
            dk_ref[...] = dk_s[...].astype(BF16)
            dv_ref[...] = dv_s[...].astype(BF16)

    blk = lambda off: pl.BlockSpec((B, W), lambda g, i: (i, g + off))
    full = lambda off: pl.BlockSpec((T, W), lambda g, i: (0, g + off))
    out = _out((T, N_HEADS * HEAD_DIM), BF16)
    return pl.pallas_call(
        lambda after_ref, *refs: body(*refs), name=name, grid=(steps, n_blocks),
        in_specs=[ANY, blk(0), full(steps), full(2 * steps), blk(0), blk(0)],
        out_specs=[blk(0), full(0), full(0)],
        out_shape=[out, out, out],
        scratch_shapes=[pltpu.VMEM((T, W), F32)] * 2,
        compiler_params=_params(2, VMEM_LIMIT),
    )(after, *_hbm(qkv, qkv, qkv, do, o))


NEAR = BAND - PAD + REL_CLIP
FAR = BAND - NEAR
NEAR_REL = 2 * REL_CLIP
BIAS_ROWS = 8


def _rel_onehot(i, transposed):
    shape = (NEAR, NEAR_REL) if transposed else (NEAR_REL, NEAR)
    j = FAR + lax.broadcasted_iota(jnp.int32, shape, 0 if transposed else 1)
    r = lax.broadcasted_iota(jnp.int32, shape, 1 if transposed else 0)
    idx = jnp.clip(i + PAD - j, -REL_CLIP, REL_CLIP) + REL_CLIP
    return jnp.where(idx - 1 == r, 1.0, 0.0).astype(BF16)


def _bias_table(rel_bias, name):
    def body(near_ref, far_ref, o_ref):
        rb = near_ref[...]
        hi, lo = _split2(rb)
        lo2 = (rb - hi.astype(F32) - lo.astype(F32)).astype(BF16)
        far = jnp.broadcast_to(far_ref[...], (N_HEADS, FAR))
        for k in range(BIAS_ROWS):
            onehot = _rel_onehot(pl.program_id(0) * BIAS_ROWS + k, False)
            o_ref[k, :, :FAR] = far
            o_ref[k, :, FAR:] = _dot(hi, onehot) + _dot(lo, onehot) + _dot(lo2, onehot)

    return pl.pallas_call(
        body, name=name, grid=(CHUNK // BIAS_ROWS,),
        in_specs=[pl.BlockSpec((N_HEADS, NEAR_REL), lambda i: (0, 0)), pl.BlockSpec((N_HEADS, 1), lambda i: (0, 0))],
        out_specs=pl.BlockSpec((BIAS_ROWS, N_HEADS, BAND), lambda i: (i, 0, 0)),
        out_shape=_out((CHUNK, N_HEADS, BAND), F32),
        compiler_params=_params(1),
    )(*_hbm(rel_bias[:, 1:], rel_bias[:, N_REL - 1:]))


def _bias_grad(dbias_t, name):
    def body(d_ref, near_ref, far_ref):
        near, far = None, None
        for k in range(BIAS_ROWS):
            onehot = _rel_onehot(pl.program_id(0) * BIAS_ROWS + k, True)
            hi, lo = _split2(d_ref[k, :, FAR:])
            part = _dot(hi, onehot) + _dot(lo, onehot)
            rest = jnp.sum(d_ref[k, :, :FAR], axis=1, keepdims=True)
            near, far = (part, rest) if near is None else (near + part, far + rest)
        first = pl.program_id(0) == 0
        _accumulate(near_ref, near, first)
        _accumulate(far_ref, jnp.broadcast_to(far, far_ref.shape), first)

    near, far = pl.pallas_call(
        body, name=name, grid=(CHUNK // BIAS_ROWS,),
        in_specs=[pl.BlockSpec((BIAS_ROWS, N_HEADS, BAND), lambda i: (i, 0, 0))],
        out_specs=[pl.BlockSpec((N_HEADS, NEAR_REL), lambda i: (0, 0)), pl.BlockSpec((N_HEADS, 128), lambda i: (0, 0))],
        out_shape=[_out((N_HEADS, NEAR_REL), F32), _out((N_HEADS, 128), F32)],
        compiler_params=_params(1),
    )(*_hbm(dbias_t))
    return jnp.pad(near, ((0, 0), (1, 0))).at[:, N_REL - 1].add(far[:, 0])


def _ch_probs(scores, bias, valid):
    z = jnp.where(valid, scores * ATT_SCALE + bias, NEG_INF)
    e = jnp.exp(z - jnp.max(z, axis=-1, keepdims=True))
    return e / jnp.sum(e, axis=-1, keepdims=True)


CH_HEADS = [(pair, h) for pair in range(N_HEADS // 2) for h in range(2)]
CH_COLS = [slice(pair * PAIR, (pair + 1) * PAIR) for pair in range(N_HEADS // 2)]


CH_GROUP = 2
CH_Q = CH_GROUP * CHUNK
CH_WIN = (LOOKBACK + CH_GROUP) * CHUNK


def _ch_valid(n):
    row_chunk = lax.broadcasted_iota(jnp.int32, (CH_Q, CH_WIN), 0) // CHUNK
    slot = lax.broadcasted_iota(jnp.int32, (CH_Q, CH_WIN), 1)
    ahead = slot // CHUNK - row_chunk
    return (ahead >= 0) & (ahead <= LOOKBACK) & (n * CH_Q + slot >= PAD)


def _ch_group_bias(bias):
    shifted = [jnp.pad(bias, ((0, 0), (0, 0), (c * CHUNK, (CH_GROUP - 1 - c) * CHUNK))) for c in range(CH_GROUP)]
    return jnp.concatenate(shifted, axis=1)


def _ch_fold_bias_grad(dbias):
    parts = [dbias[:, c * CHUNK:(c + 1) * CHUNK, c * CHUNK:c * CHUNK + BAND] for c in range(CH_GROUP)]
    return sum(parts[1:], parts[0])


def _ch_fwd(qkv, bias, name):
    T = qkv.shape[0]
    W = N_HEADS * HEAD_DIM

    def body(q_ref, k_ref, v_ref, b_ref, o_ref, kp, vp):
        n = pl.program_id(0)

        @pl.when(n == 0)
        def _():
            _ch_load_padded(k_ref, v_ref, kp, vp)

        win = pl.ds(pl.multiple_of(n * CH_Q, CH_Q), CH_WIN)
        valid = _ch_valid(n)
        lanes = _pair_lanes()
        scores = [_dot(_only(lanes[h], q_ref[:, CH_COLS[pair]]), kp[win, CH_COLS[pair]], NT) for pair, h in CH_HEADS]
        probs = [_ch_probs(s, b_ref[2 * pair + h], valid).astype(BF16) for s, (pair, h) in zip(scores, CH_HEADS)]
        outs = [_dot(p, _only(lanes[h], vp[win, CH_COLS[pair]])) for p, (pair, h) in zip(probs, CH_HEADS)]
        for pair, cols in enumerate(CH_COLS):
            o_ref[:, cols] = outs[2 * pair] + outs[2 * pair + 1]

    full = lambda col: pl.BlockSpec((T, W), lambda n: (0, col))
    return pl.pallas_call(
        body, name=name, grid=(T // CH_Q,),
        in_specs=[pl.BlockSpec((CH_Q, W), lambda n: (n, 3)), full(4), full(5),
                  pl.BlockSpec((N_HEADS, CH_Q, CH_WIN), lambda n: (0, 0, 0))],
        out_specs=pl.BlockSpec((CH_Q, W), lambda n: (n, 0)),
        out_shape=_out((T, W), F32),
        scratch_shapes=[pltpu.VMEM((PAD + T, W), BF16)] * 2,
        compiler_params=_params(1, VMEM_LIMIT),
    )(*_hbm(qkv, qkv, qkv, bias))


def _ch_load_padded(k_ref, v_ref, kp, vp):
    for src, dst in ((k_ref, kp), (v_ref, vp)):
        dst[:PAD, :] = jnp.zeros((PAD, dst.shape[1]), dst.dtype)
        dst[PAD:, :] = src[...]


def _ch_bwd(qkv, bias, do, after, name):
    T = qkv.shape[0]
    W = N_HEADS * HEAD_DIM
    n_chunks = T // CH_Q

    def body(q_ref, k_ref, v_ref, b_ref, do_ref, dq_ref, dk_ref, dv_ref, db_ref, kp, vp, dk_s, dv_s):
        n = pl.program_id(0)

        @pl.when(n == 0)
        def _():
            _ch_load_padded(k_ref, v_ref, kp, vp)
            dk_s[...] = jnp.zeros_like(dk_s)
            dv_s[...] = jnp.zeros_like(dv_s)
            db_ref[...] = jnp.zeros_like(db_ref)

        win = pl.ds(pl.multiple_of(n * CH_Q, CH_Q), CH_WIN)
        valid = _ch_valid(n)
        lanes = _pair_lanes()
        kws = [kp[win, cols] for cols in CH_COLS]
        vws = [vp[win, cols] for cols in CH_COLS]
        qs = [_only(lanes[h], q_ref[:, CH_COLS[pair]]) for pair, h in CH_HEADS]
        dos = [_only(lanes[h], do_ref[:, CH_COLS[pair]].astype(BF16)) for pair, h in CH_HEADS]
        scores = [_dot(q, kws[pair], NT) for q, (pair, _) in zip(qs, CH_HEADS)]
        dps = [_dot(do, vws[pair], NT) for do, (pair, _) in zip(dos, CH_HEADS)]
        probs = [_ch_probs(s, b_ref[2 * pair + h], valid) for s, (pair, h) in zip(scores, CH_HEADS)]
        dzs = [p * (dp - jnp.sum(dp * p, axis=-1, keepdims=True)) for p, dp in zip(probs, dps)]
        for k, dz in enumerate(dzs):
            db_ref[k] += dz
        dzbs = [(dz * ATT_SCALE).astype(BF16) for dz in dzs]
        dqs = [_dot(dz, _only(lanes[h], kws[pair])) for dz, (pair, h) in zip(dzbs, CH_HEADS)]
        dks = [_dot(dz, q, TN) for dz, q in zip(dzbs, qs)]
        dvs = [_dot(p.astype(BF16), do, TN) for p, do in zip(probs, dos)]
        for pair, cols in enumerate(CH_COLS):
            dq_ref[:, cols] = (dqs[2 * pair] + dqs[2 * pair + 1]).astype(BF16)
            dk_s[win, cols] += dks[2 * pair] + dks[2 * pair + 1]
            dv_s[win, cols] += dvs[2 * pair] + dvs[2 * pair + 1]

        @pl.when(n == n_chunks - 1)
        def _():
            dk_ref[...] = dk_s[PAD:, :].astype(BF16)
            dv_ref[...] = dv_s[PAD:, :].astype(BF16)

    full = lambda col: pl.BlockSpec((T, W), lambda n: (0, col))
    blk = lambda col: pl.BlockSpec((CH_Q, W), lambda n: (n, col))
    tab = pl.BlockSpec((N_HEADS, CH_Q, CH_WIN), lambda n: (0, 0, 0))
    out = _out((T, W), BF16)
    return pl.pallas_call(
        lambda after_ref, *refs: body(*refs), name=name, grid=(n_chunks,),
        in_specs=[ANY, blk(3), full(4), full(5), tab, blk(0)],
        out_specs=[blk(0), full(0), full(0), tab],
        out_shape=[out, out, out, _out((N_HEADS, CH_Q, CH_WIN), F32)],
        scratch_shapes=[pltpu.VMEM((PAD + T, W), BF16)] * 2 + [pltpu.VMEM((PAD + T, W), F32)] * 2,
        compiler_params=_params(1, VMEM_LIMIT),
    )(after, *_hbm(qkv, qkv, qkv, bias, do))


def _rows_split(a, parts):
    return a.reshape(a.shape[:-2] + (parts, a.shape[-2] // parts, a.shape[-1]))


def _cast_into_own_slot(me, c, ws, in_chip_order, name):
    parts = 2
    ws = [_rows_split(_rows_split(w, 2), parts) for w in ws]
    n = len(ws)

    def body(me_ref, c_ref, *refs):
        for src, dst in zip(refs[:n], refs[n:]):
            dst[0, 0, 0] = src[0, 0].astype(BF16)

    def specs(w, plain):
        block = (1, 1) + w.shape[2:]
        if plain:
            return (pl.BlockSpec(block, lambda d, r, me_ref, c_ref: (d, r, 0, 0)),
                    pl.BlockSpec((1,) + block, lambda d, r, me_ref, c_ref: (me_ref[0], d, r, 0, 0)))
        return (pl.BlockSpec(block, lambda d, r, me_ref, c_ref: (d ^ c_ref[0], r, 0, 0)),
                pl.BlockSpec((1,) + block, lambda d, r, me_ref, c_ref: (0, d, r, 0, 0)))

    both = [specs(w, plain) for w, plain in zip(ws, in_chip_order)]
    outs = pl.pallas_call(
        body, name=name,
        grid_spec=pltpu.PrefetchScalarGridSpec(
            num_scalar_prefetch=2, grid=(2, parts),
            in_specs=[s[0] for s in both], out_specs=[s[1] for s in both]),
        out_shape=[_out((N_CHIPS,) + w.shape, BF16) for w in ws],
        compiler_params=_params(2, VMEM_LIMIT),
    )(me, c, *_hbm(*ws))
    return [o.reshape(N_CHIPS, 2, o.shape[2] * o.shape[3], o.shape[4]) for o in outs]


def _zone_slots(in_chip_order):
    x, y, c, _ = _place()
    me = 2 * x + y
    if in_chip_order:
        return (me, c), (lambda r: (me, c)), (lambda r: (me ^ r, c)), (lambda r: (me ^ r, c))
    return (0, 0), (lambda r: (r, 0)), (lambda r: (r, 0)), (lambda r: (r, 1))


def _pair_add(c, mine, got, permuted, name):
    parts = 2
    mine = [_rows_split(m, parts) for m in mine]
    got = [_rows_split(g, parts) for g in got]
    n = len(mine)

    def body(c_ref, *refs):
        for a, b, o in zip(refs[:n], refs[n:2 * n], refs[2 * n:]):
            o[0, 0] = (a[0, 0, 0] + b[0, 0].astype(F32)).astype(BF16)

    def mine_spec(m, perm):
        if perm:
            return pl.BlockSpec((1, 1, 1) + m.shape[3:], lambda j, r, c_ref: (j, 0, r, 0, 0))
        return pl.BlockSpec((1, 1, 1) + m.shape[3:], lambda j, r, c_ref: (j, c_ref[0], r, 0, 0))

    def got_spec(g):
        return pl.BlockSpec((1, 1) + g.shape[2:], lambda j, r, c_ref: (j, r, 0, 0))

    outs = pl.pallas_call(
        body, name=name,
        grid_spec=pltpu.PrefetchScalarGridSpec(
            num_scalar_prefetch=1, grid=(N_CHIPS, parts),
            in_specs=[mine_spec(m, perm) for m, perm in zip(mine, permuted)] + [got_spec(g) for g in got],
            out_specs=[got_spec(g) for g in got]),
        out_shape=[_out(g.shape, BF16) for g in got],
        compiler_params=_params(2, VMEM_LIMIT),
    )(c, *_hbm(*mine, *got))
    return [o.reshape(o.shape[0], o.shape[1] * o.shape[2], o.shape[3]) for o in outs]


def _chip_add(me, partials, landed, permuted, name):
    parts = 2
    ps = [_rows_split(x, parts) for x in partials]
    ls = [_rows_split(x, parts) for x in landed]
    n = len(ps)

    def body(me_ref, *refs):
        for own, got, o in zip(refs[:n], refs[n:2 * n], refs[2 * n:]):
            acc = own[0, 0].astype(F32)
            for r in range(N_CHIPS - 1):
                acc = acc + got[r, 0].astype(F32)
            o[0] = acc

    def own_spec(x, perm):
        if perm:
            return pl.BlockSpec((1, 1) + x.shape[2:], lambda r, me_ref: (0, r, 0, 0))
        return pl.BlockSpec((1, 1) + x.shape[2:], lambda r, me_ref: (me_ref[0], r, 0, 0))

    outs = pl.pallas_call(
        body, name=name,
        grid_spec=pltpu.PrefetchScalarGridSpec(
            num_scalar_prefetch=1, grid=(parts,),
            in_specs=[own_spec(x, perm) for x, perm in zip(ps, permuted)]
            + [pl.BlockSpec((N_CHIPS - 1, 1) + x.shape[2:], lambda r, me_ref: (0, r, 0, 0)) for x in ls],
            out_specs=[pl.BlockSpec((1,) + x.shape[2:], lambda r, me_ref: (r, 0, 0)) for x in ps]),
        out_shape=[_out(x.shape[1:], F32) for x in ps],
        compiler_params=_params(1, VMEM_LIMIT),
    )(me, *_hbm(*ps, *ls))
    return [o.reshape(o.shape[0] * o.shape[1], o.shape[2]) for o in outs]


def _adamw_math(w, g, m, v):
    m = ADAM_B1 * m + (1.0 - ADAM_B1) * g
    v = ADAM_B2 * v + (1.0 - ADAM_B2) * (g * g)
    m_hat = m / (1.0 - ADAM_B1 ** ADAM_STEP)
    v_hat = v / (1.0 - ADAM_B2 ** ADAM_STEP)
    delta = -ADAM_LR * (m_hat / (jnp.sqrt(v_hat) + ADAM_EPS) + ADAM_WD * w)
    return delta, m, v


def _adamw(ws, gs, ms, vs, parts, name):
    n = len(ws)
    flat = [_rows_split(a, parts) for a in (*ws, *gs, *ms, *vs)]

    def body(*refs):
        ins, outs = refs[:4 * n], refs[4 * n:]
        for k in range(n):
            d, m, v = _adamw_math(ins[k][...], ins[n + k][...], ins[2 * n + k][...], ins[3 * n + k][...])
            outs[k][...] = d
            outs[n + k][...] = m
            outs[2 * n + k][...] = v

    spec = lambda a: pl.BlockSpec((1,) + a.shape[1:], lambda i: (i, 0, 0))
    outs = pl.pallas_call(
        body, name=name, grid=(parts,),
        in_specs=[spec(a) for a in flat], out_specs=[spec(a) for a in flat[:n]] * 3,
        out_shape=[_out(a.shape, F32) for a in flat[:n]] * 3,
        compiler_params=_params(1, VMEM_LIMIT),
    )(*_hbm(*flat))
    outs = [o.reshape(o.shape[0] * o.shape[1], o.shape[2]) for o in outs]
    return outs[:n], outs[n:2 * n], outs[2 * n:]


def _adamw_halves(c, ws, owns, others, ms, vs, name):
    parts = 4
    n = len(ws)
    whole = [_rows_split(_rows_split(a, 2), parts) for a in (*ws, *ms, *vs)]
    halves = [_rows_split(a, parts) for a in (*owns, *others)]

    def body(c_ref, *refs):
        ins, outs = refs[:5 * n], refs[5 * n:]
        mine = pl.program_id(0) == c_ref[0]
        for k in range(n):
            g = jnp.where(mine, ins[3 * n + k][0], ins[4 * n + k][0])
            d, m, v = _adamw_math(ins[k][0, 0], g, ins[n + k][0, 0], ins[2 * n + k][0, 0])
            for slot, val in enumerate((g, d, m, v)):
                outs[slot * n + k][0, 0] = val

    wspec = lambda a: pl.BlockSpec((1, 1) + a.shape[2:], lambda h, r, c_ref: (h, r, 0, 0))
    hspec = lambda a: pl.BlockSpec((1,) + a.shape[1:], lambda h, r, c_ref: (r, 0, 0))
    outs = pl.pallas_call(
        body, name=name,
        grid_spec=pltpu.PrefetchScalarGridSpec(
            num_scalar_prefetch=1, grid=(2, parts),
            in_specs=[wspec(a) for a in whole] + [hspec(a) for a in halves],
            out_specs=[wspec(a) for a in whole[:n]] * 4),
        out_shape=[_out(a.shape, F32) for a in whole[:n]] * 4,
        compiler_params=_params(2, VMEM_LIMIT),
    )(c, *_hbm(*whole, *halves))
    outs = [o.reshape(2 * parts * o.shape[2], o.shape[3]) for o in outs]
    return outs[:n], outs[n:2 * n], outs[2 * n:3 * n], outs[3 * n:]


def _place():
    x, y, c = lax.axis_index("x"), lax.axis_index("y"), lax.axis_index("c")
    peers = [(x ^ (r >> 1), y ^ (r & 1), c) for r in (1, 2, 3)]
    return x, y, c, peers


def _handshake(peers):
    barrier = pltpu.get_barrier_semaphore()
    for peer in peers:
        pl.semaphore_signal(barrier, inc=1, device_id=peer, device_id_type=MESH)
    pl.semaphore_wait(barrier, len(peers))


ANY = pl.BlockSpec(memory_space=pl.ANY)
HBM = pl.BlockSpec(memory_space=pltpu.HBM)
SEM = pl.BlockSpec(memory_space=pltpu.SEMAPHORE)
SPLIT_COPY = pltpu.SideEffectType.DATAFLOW_SIDE_EFFECTING


def _in_hbm(a):
    return pltpu.with_memory_space_constraint(a, pltpu.HBM)


def _split_start(body, name, collective_id, operands, n_sems, after=None):
    n = len(operands)
    extra = [] if after is None else [after]

    def wrapped(*refs):
        at = n + len(extra)
        body(refs[:n], refs[at], refs[at + 1])
        token = refs[-1]
        token[...] = jnp.zeros_like(token)

    outs = pl.pallas_call(
        wrapped, name=name,
        in_specs=[HBM] * n + [ANY] * len(extra),
        out_shape=(pltpu.SemaphoreType.DMA((n_sems,)), pltpu.SemaphoreType.DMA((n_sems,)),
                   *[pltpu.HBM(a.shape, a.dtype) for a in operands], jax.ShapeDtypeStruct((8, 128), F32)),
        out_specs=(SEM, SEM, *[HBM] * n, pl.BlockSpec(memory_space=pltpu.VMEM)),
        input_output_aliases={i: 2 + i for i in range(n)},
        compiler_params=pltpu.CompilerParams(has_side_effects=SPLIT_COPY, collective_id=collective_id),
    )(*[_in_hbm(a) for a in operands], *extra)
    return outs[0], outs[1], list(outs[2:2 + n]), outs[-1]


def _split_wait(body, name, send_sem, recv_sem, operands, after):
    n = len(operands)

    def wrapped(*refs):
        body(refs[:n], refs[n], refs[n + 1])

    outs = pl.pallas_call(
        wrapped, name=name,
        in_specs=[HBM] * n + [SEM, SEM, ANY],
        out_shape=tuple(pltpu.HBM(a.shape, a.dtype) for a in operands),
        out_specs=tuple([HBM] * n),
        input_output_aliases={i: i for i in range(n)},
        compiler_params=pltpu.CompilerParams(has_side_effects=SPLIT_COPY),
    )(*operands, send_sem, recv_sem, after)
    return list(outs)


def _gather_copies(lands, in_chip_order, send_sem, recv_sem):
    peers = _place()[3]
    copies = []
    for a, (land, plain) in enumerate(zip(lands, in_chip_order)):
        own, sent_to, _, _ = _zone_slots(plain)
        copies += [pltpu.make_async_remote_copy(
            src_ref=land.at[own], dst_ref=land.at[sent_to(r + 1)],
            send_sem=send_sem.at[a * 3 + r], recv_sem=recv_sem.at[a * 3 + r],
            device_id=peers[r], device_id_type=MESH) for r in range(3)]
    return copies


def _gather_start(lands, in_chip_order, name, collective_id, after):
    def body(refs, send_sem, recv_sem):
        _handshake(_place()[3])
        for cp in _gather_copies(refs, in_chip_order, send_sem, recv_sem):
            cp.start()

    return _split_start(body, name, collective_id, list(lands), 3 * len(lands), after)


def _gather_wait(send_sem, recv_sem, operands, in_chip_order, after, name):
    def body(refs, send_sem, recv_sem):
        for cp in _gather_copies(refs, in_chip_order, send_sem, recv_sem):
            cp.wait_send()
            cp.wait_recv()

    return _split_wait(body, name, send_sem, recv_sem, operands, after)


def _gather_finish(lands, in_chip_order, with_ici, name):
    n = len(lands)

    def body(*refs):
        land = refs[n:2 * n]
        send_ici, recv_ici, send_d2d, recv_d2d = refs[2 * n:]
        x, y, c, _ = _place()
        ici = _gather_copies(land, in_chip_order, send_ici, recv_ici) if with_ici else []
        for cp in ici:
            cp.start()
        passed = []
        for a in range(n):
            _, _, received, kept = _zone_slots(in_chip_order[a])
            passed += [pltpu.make_async_remote_copy(
                src_ref=land[a].at[received(r + 1)], dst_ref=land[a].at[kept(r + 1)],
                send_sem=send_d2d.at[a * 3 + r], recv_sem=recv_d2d.at[a * 3 + r],
                device_id=(x, y, 1 - c), device_id_type=MESH) for r in range(3)]
        for k, cp in enumerate(passed):
            if with_ici:
                ici[k].wait_recv()
            cp.start()
        for cp in passed:
            cp.wait_recv()
        for cp in ici:
            cp.wait_send()
        for cp in passed:
            cp.wait_send()

    outs = pl.pallas_call(
        body, name=name,
        in_specs=[ANY] * n, out_specs=[ANY] * n,
        out_shape=[_out(l.shape, l.dtype) for l in lands],
        input_output_aliases={a: a for a in range(n)},
        scratch_shapes=[pltpu.SemaphoreType.DMA((3 * n,))] * 4,
    )(*lands)
    return list(outs)


def _slabs(land):
    return land.reshape(N_CHIPS, 2 * land.shape[2], land.shape[3])


def _pair_swap(grads, permuted, name):
    n = len(grads)

    def body(*refs):
        src, dst = refs[:n], refs[n:2 * n]
        send_sem, recv_sem = refs[2 * n:]
        x, y, c, _ = _place()
        copies = [pltpu.make_async_remote_copy(
            src_ref=src[a].at[:, 1] if permuted[a] else src[a].at[:, 1 - c], dst_ref=dst[a],
            send_sem=send_sem.at[a], recv_sem=recv_sem.at[a],
            device_id=(x, y, 1 - c), device_id_type=MESH) for a in range(n)]
        for cp in copies:
            cp.start()
        for cp in copies:
            cp.wait()

    return pl.pallas_call(
        body, name=name,
        in_specs=[ANY] * n, out_specs=[ANY] * n,
        out_shape=[_out((N_CHIPS,) + g.shape[2:], g.dtype) for g in grads],
        scratch_shapes=[pltpu.SemaphoreType.DMA((n,))] * 2,
    )(*grads)


def _swap_copies(refs, permuted, send_sem, recv_sem):
    n = len(refs) // 2
    x, y, c, _ = _place()
    return [pltpu.make_async_remote_copy(
        src_ref=refs[a].at[:, 1] if permuted[a] else refs[a].at[:, 1 - c], dst_ref=refs[n + a],
        send_sem=send_sem.at[a], recv_sem=recv_sem.at[a],
        device_id=(x, y, 1 - c), device_id_type=MESH) for a in range(n)]


def _pair_swap_start(grads, permuted, name, collective_id):
    def body(refs, send_sem, recv_sem):
        x, y, c, _ = _place()
        _handshake([(x, y, 1 - c)])
        for cp in _swap_copies(refs, permuted, send_sem, recv_sem):
            cp.start()

    lands = [lax.empty((N_CHIPS,) + g.shape[2:], g.dtype) for g in grads]
    return _split_start(body, name, collective_id, list(grads) + lands, len(grads))


def _pair_swap_wait(send_sem, recv_sem, operands, permuted, after, name):
    def body(refs, send_sem, recv_sem):
        for cp in _swap_copies(refs, permuted, send_sem, recv_sem):
            cp.wait_send()
            cp.wait_recv()

    return _split_wait(body, name, send_sem, recv_sem, operands, after)


def _scatter_copies(refs, permuted, send_sem, recv_sem):
    n = len(refs) // 2
    x, y, _, peers = _place()
    me = 2 * x + y
    return [pltpu.make_async_remote_copy(
        src_ref=refs[a].at[r + 1] if permuted[a] else refs[a].at[me ^ (r + 1)], dst_ref=refs[n + a].at[r],
        send_sem=send_sem.at[a * 3 + r], recv_sem=recv_sem.at[a * 3 + r],
        device_id=peers[r], device_id_type=MESH) for a in range(n) for r in range(3)]


def _scatter_start(partials, permuted, name, collective_id):
    def body(refs, send_sem, recv_sem):
        _handshake(_place()[3])
        for cp in _scatter_copies(refs, permuted, send_sem, recv_sem):
            cp.start()

    lands = [lax.empty((N_CHIPS - 1,) + p.shape[1:], p.dtype) for p in partials]
    return _split_start(body, name, collective_id, list(partials) + lands, 3 * len(partials))


def _scatter_wait(send_sem, recv_sem, operands, permuted, after, name):
    def body(refs, send_sem, recv_sem):
        for cp in _scatter_copies(refs, permuted, send_sem, recv_sem):
            cp.wait_send()
            cp.wait_recv()

    return _split_wait(body, name, send_sem, recv_sem, operands, after)


def _pair_join(halves, name):
    n = len(halves)

    def body(*refs):
        src, dst = refs[:n], refs[n:2 * n]
        send_sem, recv_sem = refs[2 * n:]
        x, y, c, _ = _place()
        copies = [pltpu.make_async_remote_copy(
            src_ref=src[a], dst_ref=dst[a], send_sem=send_sem.at[a], recv_sem=recv_sem.at[a],
            device_id=(x, y, 1 - c), device_id_type=MESH) for a in range(n)]
        for cp in copies:
            cp.start()
        for cp in copies:
            cp.wait()

    return pl.pallas_call(
        body, name=name,
        in_specs=[ANY] * n, out_specs=[ANY] * n,
        out_shape=[_out(h.shape, F32) for h in halves],
        scratch_shapes=[pltpu.SemaphoreType.DMA((n,))] * 2,
    )(*halves)


def _all_sum_small(v, after, name):
    R, C = v.shape
    n_dev = 8

    def body(v_ref, after_ref, o_ref, buf, send_sem, recv_sem):
        x, y, c, _ = _place()
        me = 4 * x + 2 * y + c
        buf[me] = v_ref[...]
        copies = []
        for k in range(1, n_dev):
            peer = (x ^ (k >> 2), y ^ ((k >> 1) & 1), c ^ (k & 1))
            copies.append(pltpu.make_async_remote_copy(
                src_ref=v_ref, dst_ref=buf.at[me], send_sem=send_sem.at[k - 1], recv_sem=recv_sem.at[k - 1],
                device_id=peer, device_id_type=MESH))
        for cp in copies:
            cp.start()
        for cp in copies:
            cp.wait()
        acc = buf[0]
        for m in range(1, n_dev):
            acc = acc + buf[m]
        o_ref[...] = acc

    return pl.pallas_call(
        body, name=name,
        in_specs=[pl.BlockSpec(memory_space=pltpu.VMEM), ANY], out_specs=pl.BlockSpec(memory_space=pltpu.VMEM),
        out_shape=jax.ShapeDtypeStruct((R, C), F32),
        scratch_shapes=[pltpu.VMEM((n_dev, R, C), F32), pltpu.SemaphoreType.DMA((n_dev - 1,)),
                        pltpu.SemaphoreType.DMA((n_dev - 1,))],
    )(v, after)


class _WholeWeights:
    def __init__(self, w):
        self.w = w

    def weights(self, group, after=None):
        return self.w, None

    def grads_ready(self, group, gw):
        return None

    def grads_sent(self, group, after):
        return None


def _local_step(x, p, target, gains, rel_bias, hooks):
    T, D = x.shape
    S = N_CHIPS

    tied = lambda gain, token: gain if token is None else gain + token[0, 0]
    w, token = hooks.weights("first")
    w = dict(w)
    xn1, g1, u1, a1 = _ffn_up(x, tied(gains["ffn1_pre"], token), w["ffn1_gate"], w["ffn1_up"], "ffn1_up")
    w.update(hooks.weights("down", a1)[0])
    h1, f1 = _ffn_down(x, a1, gains["ffn1_post"], w["ffn1_down"], "ffn1_down")
    more, token = hooks.weights("in", h1)
    w.update(more)
    qkv, un = _norm_proj(h1, tied(gains["mix_pre"], token), w["in"], "qkv_proj")
    bias = _ch_group_bias(_bias_table(rel_bias, "bias_table").transpose(1, 0, 2))
    o_a = _sb_fwd(qkv, "sb_fwd")
    o_b = _ch_fwd(qkv, bias, "ch_fwd")
    w.update(hooks.weights("rest", o_b)[0])
    w_out = w["out"].reshape(D, D)
    h2, mixed, mo = _mix_out_fwd(h1, o_a, o_b, gains["out_sb"], gains["out_ch"], w_out, gains["mix_post"],
                                 "mix_out_fwd")
    h3, xn2, g2, u2, a2, f2 = _ffn_fwd(h2, gains["ffn2_pre"], gains["ffn2_post"], w["ffn2_gate"], w["ffn2_up"],
                                       w["ffn2_down"], "ffn2_fwd")
    w_ple_proj = w["ple_proj"].transpose(1, 0, 2).reshape(p.shape[1], D)
    w_ple_gate = w["ple_gate"].reshape(D, D)

    loss, dh3, dproj, dgate, dg_ple = _ple_loss(h3, p, target, w_ple_proj, w_ple_gate, gains["ple_post"], "ple_loss")
    gw, gg = {}, {"ple_post": dg_ple}
    gw["ple_proj"] = _mm_tn(p[None], dproj, p.shape[1], "dw_ple_proj")
    row_sharded = lambda pair: tuple(o.reshape(S, D // S, D) for o in pair)
    gw["ple_gate"] = row_sharded(_mm_tn(h3[None], dgate[None], 512, "dw_ple_gate"))

    def ffn_bwd(tag, dh, x_in, xn, g_act, u_act, a_act, f, group):
        dgp, dup, df, gg[tag + "_post"] = _ffn_bwd_act(dh, f, gains[tag + "_post"], w[tag + "_down"], g_act, u_act,
                                                       tag + "_bwd_act")
        gw[tag + "_gate"] = _mm_tn(dgp, xn[None], dgp.shape[2], "dw_" + tag + "_gate")
        gw[tag + "_up"] = _mm_tn(dup, xn[None], dup.shape[2], "dw_" + tag + "_up")
        gw[tag + "_down"] = _mm_tn(a_act, df[None], a_act.shape[2], "dw_" + tag + "_down")
        g_pre = gains[tag + "_pre"]
        if group is not None:
            token = hooks.grads_ready(group, gw)
            g_pre = g_pre if token is None else g_pre + token[0, 0]
        dx, gg[tag + "_pre"] = _proj_bwd([dgp, dup], [w[tag + "_gate"], w[tag + "_up"]], x_in, g_pre, dh,
                                         tag + "_bwd_in")
        return dx

    dh2 = ffn_bwd("ffn2", dh3, h2, xn2, g2, u2, a2, f2, None)
    dmo, do_a, do_b, gg["mix_post"], gg["out_sb"], gg["out_ch"] = _mix_out_bwd(
        dh2, mo, gains["mix_post"], w_out, o_a, o_b, gains["out_sb"], gains["out_ch"], "mix_out_bwd")
    gw["out"] = row_sharded(_mm_tn(mixed[None], dmo[None], 512, "dw_out"))
    token = hooks.grads_ready("early", gw)
    dq_a, dk_a, dv_a = _sb_bwd(qkv, do_a, o_a, do_a if token is None else token, "sb_bwd")
    token = hooks.grads_sent("early", dq_a)
    dq_b, dk_b, dv_b, dbias = _ch_bwd(qkv, bias, do_b, do_b if token is None else token, "ch_bwd")
    g_rel = _bias_grad(_ch_fold_bias_grad(dbias).transpose(1, 0, 2), "bias_grad")
    dqkv = jnp.concatenate([dq_a, dk_a, dv_a, dq_b, dk_b, dv_b], axis=1)
    gw["in"] = _mm_tn(un[None], dqkv, 512, "dw_in", groups=S)
    dh1, gg["mix_pre"] = _proj_bwd([dqkv], [w["in"]], h1, gains["mix_pre"], dh2, "qkv_bwd_in")
    dx = ffn_bwd("ffn1", dh1, x, xn1, g1, u1, a1, f1, "late")
    return loss, dx, gw, gg, g_rel


BIG = ["ffn1_gate", "ffn1_up", "ffn1_down", "in", "out", "ffn2_gate", "ffn2_up", "ffn2_down", "ple_proj", "ple_gate"]
GAINS = ["ffn1_pre", "ffn1_post", "mix_pre", "mix_post", "out_sb", "out_ch", "ffn2_pre", "ffn2_post", "ple_post"]
TRANSPOSED = ("w_ffn1_gate", "w_ffn1_up", "w_ffn2_gate", "w_ffn2_up")
PERMUTED = ("ffn1_gate", "ffn1_up", "ffn1_down", "ffn2_gate", "ffn2_up", "ffn2_down")
W_GROUPS = {"first": ["ffn1_gate", "ffn1_up"], "down": ["ffn1_down"], "in": ["in"],
            "rest": ["out", "ffn2_gate", "ffn2_up", "ffn2_down", "ple_proj", "ple_gate"]}
G_GROUPS = {"early": ["ple_proj", "ple_gate", "ffn2_gate", "ffn2_up", "ffn2_down", "out"],
            "late": ["in", "ffn1_gate", "ffn1_up", "ffn1_down"]}
ORDER = ["g_ffn1_pre", "g_ffn1_post", "w_ffn1_gate", "w_ffn1_up", "w_ffn1_down", "g_mix_pre", "g_mix_post", "w_in",
         "g_out_sb", "g_out_ch", "rel_bias", "w_out", "g_ffn2_pre", "g_ffn2_post", "w_ffn2_gate", "w_ffn2_up",
         "w_ffn2_down", "w_ple_proj", "w_ple_gate", "g_ple_post"]


def kernel(x, p, g_ffn1_pre, g_ffn1_post, w_ffn1_gate, w_ffn1_up, w_ffn1_down, g_mix_pre, g_mix_post, w_in, g_out_sb, g_out_ch, rel_bias, w_out, g_ffn2_pre, g_ffn2_post, w_ffn2_gate, w_ffn2_up, w_ffn2_down, w_ple_proj, w_ple_gate, g_ple_post, loss_target, m_g_ffn1_pre, m_g_ffn1_post, m_w_ffn1_gate, m_w_ffn1_up, m_w_ffn1_down, m_g_mix_pre, m_g_mix_post, m_w_in, m_g_out_sb, m_g_out_ch, m_rel_bias, m_w_out, m_g_ffn2_pre, m_g_ffn2_post, m_w_ffn2_gate, m_w_ffn2_up, m_w_ffn2_down, m_w_ple_proj, m_w_ple_gate, m_g_ple_post, v_g_ffn1_pre, v_g_ffn1_post, v_w_ffn1_gate, v_w_ffn1_up, v_w_ffn1_down, v_g_mix_pre, v_g_mix_post, v_w_in, v_g_out_sb, v_g_out_ch, v_rel_bias, v_w_out, v_g_ffn2_pre, v_g_ffn2_post, v_w_ffn2_gate, v_w_ffn2_up, v_w_ffn2_down, v_w_ple_proj, v_w_ple_gate, v_g_ple_post):
    args = dict(locals())
    take = lambda a, n: a[0].T if n in TRANSPOSED else a[0]
    wts = {n: take(args[n], n) for n in ORDER}
    ms = {n: take(args["m_" + n], n) for n in ORDER}
    vs = {n: take(args["v_" + n], n) for n in ORDER}
    gains = {n: wts["g_" + n][None] for n in GAINS}

    c_idx = lax.axis_index("c").astype(jnp.int32).reshape(1)
    me_idx = (2 * lax.axis_index("x") + lax.axis_index("y")).astype(jnp.int32).reshape(1)
    south = lax.axis_index("c") == 0

    plain = lambda names: [n not in PERMUTED for n in names]
    lands = dict(zip(BIG, _cast_into_own_slot(me_idx, c_idx, [wts["w_" + n] for n in BIG], plain(BIG), "cast_weights")))

    class Overlapped:
        def __init__(self):
            self.started = {}
            self.flying = {}

        def start(self, group, collective_id, after):
            names = W_GROUPS[group]
            self.flying[group] = _gather_start([lands[n] for n in names], plain(names), "gather_%s_start" % group,
                                               collective_id, after)
            return self.flying[group][3]

        def weights(self, group, after=None):
            names = W_GROUPS[group]
            token = None
            if group == "first":
                zones = _gather_finish([lands[n] for n in names], plain(names), True, "gather_first")
                token = self.start("rest", 4, self.start("in", 1, self.start("down", 6, zones[0])))
            else:
                send_sem, recv_sem, zones, _ = self.flying[group]
                zones = _gather_wait(send_sem, recv_sem, zones, plain(names), after, "gather_%s_wait" % group)
                zones = _gather_finish(zones, plain(names), False, "gather_%s_finish" % group)
            return {n: _slabs(z) for n, z in zip(names, zones)}, token

        def grads_ready(self, group, gw):
            names = G_GROUPS[group]
            perm = [n in PERMUTED for n in names]
            halved = lambda g: g.reshape(N_CHIPS, 2, g.shape[1] // 2, g.shape[2])
            mine = [halved(gw[n][0]) for n in names]
            narrow = [halved(gw[n][1]) for n in names]
            if group == "late":
                return self.scatter(group, names, perm, mine, _pair_swap(narrow, perm, "grad_pair_swap_late"))
            self.swapping = names, perm, mine, _pair_swap_start(narrow, perm, "grad_pair_swap_start_early", 5)
            return self.swapping[3][3]

        def grads_sent(self, group, after):
            names, perm, mine, (send_sem, recv_sem, operands, _) = self.swapping
            operands = _pair_swap_wait(send_sem, recv_sem, operands, perm, after, "grad_pair_swap_wait_early")
            return self.scatter(group, names, perm, mine, operands[len(names):])

        def scatter(self, group, names, perm, mine, got):
            partial = _pair_add(c_idx, mine, got, perm, "grad_pair_add_" + group)
            send_sem, recv_sem, operands, token = _scatter_start(partial, perm, "grad_scatter_start_" + group,
                                                                 {"early": 2, "late": 3}[group])
            self.started[group] = names, perm, send_sem, recv_sem, operands, token
            return token

    def reduce_finish(state, after, tag):
        names, perm, send_sem, recv_sem, operands, _ = state
        operands = _scatter_wait(send_sem, recv_sem, operands, perm, after, "grad_scatter_wait_" + tag)
        n = len(names)
        own = _chip_add(me_idx, operands[:n], operands[n:], perm, "grad_chip_add_" + tag)
        return own, _pair_join(own, "grad_pair_join_" + tag)

    hooks = Overlapped()
    loss, dx, gw, gg, g_rel = _local_step(x[0], p[0, 0], loss_target[0], gains, wts["rel_bias"], hooks)

    grads, delta, new_m, new_v = {}, {}, {}, {}

    def finish(group, after):
        own, other = reduce_finish(hooks.started[group], after, group)
        names = ["w_" + n for n in G_GROUPS[group]]
        g, d, m, v = _adamw_halves(c_idx, [wts[n] for n in names], own, other, [ms[n] for n in names],
                                   [vs[n] for n in names], "adamw_" + group)
        for n, gg_, dd, mm, vv in zip(names, g, d, m, v):
            grads[n], delta[n], new_m[n], new_v[n] = gg_, dd, mm, vv
        return d[0]

    finish("late", finish("early", dx))

    pieces = [gg[n].reshape(-1, 128) for n in GAINS] + [jnp.pad(g_rel, ((0, 0), (0, N_REL_PAD - N_REL))).reshape(-1, 128)]
    summed = _all_sum_small(jnp.concatenate(pieces + [loss], axis=0), delta["w_in"], "small_grad_sum")
    at = 0
    for n, piece in zip(GAINS, pieces[:-1]):
        grads["g_" + n] = summed[at:at + piece.shape[0]].reshape(1, -1)[0]
        at += piece.shape[0]
    grads["rel_bias"] = summed[at:at + pieces[-1].shape[0]].reshape(N_HEADS, N_REL_PAD)[:, :N_REL]
    loss = summed[at + pieces[-1].shape[0], 0]

    small = ["g_" + n for n in GAINS] + ["rel_bias"]
    as_rows = lambda a: (a.reshape(-1, 128) if a.size % 128 == 0 else jnp.pad(a, ((0, 0), (0, N_REL_PAD - N_REL))).reshape(-1, 128))
    d, m, v = _adamw([as_rows(wts[n]) for n in small], [as_rows(grads[n]) for n in small],
                     [as_rows(ms[n]) for n in small], [as_rows(vs[n]) for n in small], 1, "adamw_small")
    for n, dd, mm, vv in zip(small, d, m, v):
        back = (lambda a: a.reshape(N_HEADS, N_REL_PAD)[:, :N_REL]) if n == "rel_bias" else (lambda a: a.reshape(-1))
        delta[n], new_m[n], new_v[n] = back(dd), back(mm), back(vv)

    outs = [loss, dx[None]]
    for table in (grads, delta, new_m, new_v):
        outs += [(table[n].T if n in TRANSPOSED else table[n])[None] for n in ORDER]
    return tuple(outs)
```

```python
import jax
import jax.numpy as jnp
from jax import lax
from jax.experimental import pallas as pl
from jax.experimental.pallas import tpu as pltpu

F32 = jnp.float32
BF16 = jnp.bfloat16
EPS = 1e-6
N_CHIPS = 4
HEAD_DIM = 64
N_HEADS = 8
CHUNK = 64
LOOKBACK = 8
BAND = (LOOKBACK + 1) * CHUNK
PAD = LOOKBACK * CHUNK
REL_CLIP = 128
N_REL = 2 * REL_CLIP + 1
N_REL_PAD = 384
SB_BLOCK = 256
PAIR = 2 * HEAD_DIM
SB_PAIRS = 2
ATT_SCALE = HEAD_DIM ** -0.5
NEG_INF = -1e30
ROW_BLOCK = 512
WIDE_ROW_BLOCK = 1024
VMEM_LIMIT_WIDE = 56 * 1024 * 1024
VMEM_LIMIT = 48 * 1024 * 1024
MESH = pl.DeviceIdType.MESH

ADAM_LR = 0.001
ADAM_B1 = 0.9
ADAM_B2 = 0.999
ADAM_EPS = 1e-08
ADAM_WD = 0.01
ADAM_STEP = 10

NT = (((1,), (1,)), ((), ()))
TN = (((0,), (0,)), ((), ()))


def _params(n_grid, vmem=None):
    return pltpu.CompilerParams(dimension_semantics=("arbitrary",) * n_grid, vmem_limit_bytes=vmem)


def _hbm(*arrays):
    return [pltpu.with_memory_space_constraint(a, pltpu.HBM) for a in arrays]


def _out(shape, dtype):
    return pltpu.HBM(shape, dtype)


def _dot(a, b, dims=None):
    if dims is None:
        return jnp.dot(a, b, preferred_element_type=F32)
    return lax.dot_general(a, b, dims, preferred_element_type=F32)


def _sigmoid(x):
    return 1.0 / (1.0 + jnp.exp(-x))


def _rms_fwd(x, g):
    r = lax.rsqrt(jnp.mean(x * x, axis=-1, keepdims=True) + EPS)
    return x * r * g


def _rms_bwd(x, g, dy):
    r = lax.rsqrt(jnp.mean(x * x, axis=-1, keepdims=True) + EPS)
    xh = x * r
    dg = jnp.sum(dy * xh, axis=0, keepdims=True)
    t = dy * g
    dx = r * (t - xh * jnp.mean(t * xh, axis=-1, keepdims=True))
    return dx, dg


def _accumulate(ref, val, first):
    @pl.when(first)
    def _():
        ref[...] = val

    @pl.when(jnp.logical_not(first))
    def _():
        ref[...] += val


def _split2(x):
    hi = x.astype(BF16)
    lo = (x - hi.astype(F32)).astype(BF16)
    return hi, lo


def _ffn_fwd(x, g_pre, g_post, wg, wu, wd, name):
    T, D = x.shape
    S, FS, _ = wg.shape
    tm = min(WIDE_ROW_BLOCK, T)

    def body(x_ref, gpre_ref, gpost_ref, wg_ref, wu_ref, wd_ref,
             h_ref, xn_ref, g_ref, u_ref, a_ref, f_ref):
        k = pl.program_id(1)

        @pl.when(k == 0)
        def _():
            xn_ref[...] = _rms_fwd(x_ref[...], gpre_ref[...]).astype(BF16)

        xn = xn_ref[...]
        g = _dot(xn, wg_ref[0], NT)
        u = _dot(xn, wu_ref[0], NT)
        g_ref[0] = g
        u_ref[0] = u
        a = (g * _sigmoid(g) * u).astype(BF16)
        a_ref[0] = a
        _accumulate(f_ref, _dot(a, wd_ref[0]), k == 0)

        @pl.when(k == S - 1)
        def _():
            h_ref[...] = x_ref[...] + 0.5 * _rms_fwd(f_ref[...], gpost_ref[...])

    row = pl.BlockSpec((tm, D), lambda i, k: (i, 0))
    vec = pl.BlockSpec((1, D), lambda i, k: (0, 0))
    act = pl.BlockSpec((1, tm, FS), lambda i, k: (k, i, 0))
    return pl.pallas_call(
        body, name=name, grid=(T // tm, S),
        in_specs=[row, vec, vec] + [pl.BlockSpec((1, FS, D), lambda i, k: (k, 0, 0))] * 3,
        out_specs=[row, row, act, act, act, row],
        out_shape=[_out((T, D), F32), _out((T, D), BF16),
                   _out((S, T, FS), F32), _out((S, T, FS), F32),
                   _out((S, T, FS), BF16), _out((T, D), F32)],
        compiler_params=_params(2, VMEM_LIMIT_WIDE),
    )(*_hbm(x, g_pre, g_post, wg, wu, wd))


def _ffn_up(x, g_pre, wg, wu, name):
    T, D = x.shape
    S, FS, _ = wg.shape
    tm = min(WIDE_ROW_BLOCK, T)

    def body(x_ref, gpre_ref, wg_ref, wu_ref, xn_ref, g_ref, u_ref, a_ref):
        @pl.when(pl.program_id(1) == 0)
        def _():
            xn_ref[...] = _rms_fwd(x_ref[...], gpre_ref[...]).astype(BF16)

        xn = xn_ref[...]
        g = _dot(xn, wg_ref[0], NT)
        u = _dot(xn, wu_ref[0], NT)
        g_ref[0] = g
        u_ref[0] = u
        a_ref[0] = (g * _sigmoid(g) * u).astype(BF16)

    row = pl.BlockSpec((tm, D), lambda i, k: (i, 0))
    act = pl.BlockSpec((1, tm, FS), lambda i, k: (k, i, 0))
    return pl.pallas_call(
        body, name=name, grid=(T // tm, S),
        in_specs=[row, pl.BlockSpec((1, D), lambda i, k: (0, 0))] + [pl.BlockSpec((1, FS, D), lambda i, k: (k, 0, 0))] * 2,
        out_specs=[row, act, act, act],
        out_shape=[_out((T, D), BF16), _out((S, T, FS), F32), _out((S, T, FS), F32), _out((S, T, FS), BF16)],
        compiler_params=_params(2, VMEM_LIMIT_WIDE),
    )(*_hbm(x, g_pre, wg, wu))


def _ffn_down(x, a, g_post, wd, name):
    T, D = x.shape
    S, FS, _ = wd.shape
    tm = min(WIDE_ROW_BLOCK, T)

    def body(x_ref, a_ref, gpost_ref, wd_ref, h_ref, f_ref):
        k = pl.program_id(1)
        _accumulate(f_ref, _dot(a_ref[0], wd_ref[0]), k == 0)

        @pl.when(k == S - 1)
        def _():
            h_ref[...] = x_ref[...] + 0.5 * _rms_fwd(f_ref[...], gpost_ref[...])

    row = pl.BlockSpec((tm, D), lambda i, k: (i, 0))
    return pl.pallas_call(
        body, name=name, grid=(T // tm, S),
        in_specs=[row, pl.BlockSpec((1, tm, FS), lambda i, k: (k, i, 0)), pl.BlockSpec((1, D), lambda i, k: (0, 0)),
                  pl.BlockSpec((1, FS, D), lambda i, k: (k, 0, 0))],
        out_specs=[row, row],
        out_shape=[_out((T, D), F32), _out((T, D), F32)],
        compiler_params=_params(2, VMEM_LIMIT_WIDE),
    )(*_hbm(x, a, g_post, wd))


def _ffn_bwd_act(dh, f, g_post, wd, g_act, u_act, name):
    T, D = dh.shape
    S, FS, _ = wd.shape
    tm = min(WIDE_ROW_BLOCK, T)

    def body(dh_ref, f_ref, gpost_ref, wd_ref, g_ref, u_ref, dgp_ref, dup_ref, df_ref, dgain_ref, df_s):
        i, k = pl.program_id(0), pl.program_id(1)

        @pl.when(k == 0)
        def _():
            df, dgain = _rms_bwd(f_ref[...], gpost_ref[...], 0.5 * dh_ref[...])
            df_s[...] = df.astype(BF16)
            df_ref[...] = df_s[...]
            _accumulate(dgain_ref, dgain, i == 0)

        da = _dot(df_s[...], wd_ref[0], NT)
        g = g_ref[0]
        s = _sigmoid(g)
        dup_ref[0] = (da * (g * s)).astype(BF16)
        dgp_ref[0] = (da * u_ref[0] * (s * (1.0 + g * (1.0 - s)))).astype(BF16)

    row = pl.BlockSpec((tm, D), lambda i, k: (i, 0))
    vec = pl.BlockSpec((1, D), lambda i, k: (0, 0))
    act = pl.BlockSpec((1, tm, FS), lambda i, k: (k, i, 0))
    return pl.pallas_call(
        body, name=name, grid=(T // tm, S),
        in_specs=[row, row, vec, pl.BlockSpec((1, FS, D), lambda i, k: (k, 0, 0)), act, act],
        out_specs=[act, act, row, vec],
        out_shape=[_out((S, T, FS), BF16), _out((S, T, FS), BF16),
                   _out((T, D), BF16), _out((1, D), F32)],
        scratch_shapes=[pltpu.VMEM((tm, D), BF16)],
        compiler_params=_params(2, VMEM_LIMIT_WIDE),
    )(*_hbm(dh, f, g_post, wd, g_act, u_act))


def _proj_bwd(dys, ws, x, g_pre, dh, name):
    T, D = x.shape
    n = len(dys)
    S, N, _ = ws[0].shape
    tm = min(WIDE_ROW_BLOCK, T)

    def body(*refs):
        dy_refs, w_refs = refs[:n], refs[n:2 * n]
        x_ref, gpre_ref, dh_ref, dx_ref, dgain_ref, acc_s = refs[2 * n:]
        i, k = pl.program_id(0), pl.program_id(1)
        part = None
        for dy_ref, w_ref in zip(dy_refs, w_refs):
            term = _dot(dy_ref[0], w_ref[0])
            part = term if part is None else part + term
        _accumulate(acc_s, part, k == 0)

        @pl.when(k == S - 1)
        def _():
            dx, dgain = _rms_bwd(x_ref[...], gpre_ref[...], acc_s[...])
            dx_ref[...] = dh_ref[...] + dx
            _accumulate(dgain_ref, dgain, i == 0)

    row = pl.BlockSpec((tm, D), lambda i, k: (i, 0))
    vec = pl.BlockSpec((1, D), lambda i, k: (0, 0))
    return pl.pallas_call(
        body, name=name, grid=(T // tm, S),
        in_specs=[pl.BlockSpec((1, tm, N), lambda i, k: (k, i, 0))] * n
        + [pl.BlockSpec((1, N, D), lambda i, k: (k, 0, 0))] * n + [row, vec, row],
        out_specs=[row, vec],
        out_shape=[_out((T, D), F32), _out((1, D), F32)],
        scratch_shapes=[pltpu.VMEM((tm, D), F32)],
        compiler_params=_params(2, VMEM_LIMIT_WIDE),
    )(*_hbm(*dys, *ws, x, g_pre, dh))


def _mm_tn(a, b, bm, name):
    ga, T, M = a.shape
    gb, _, N = b.shape
    b_spec = pl.BlockSpec((1, T, N), (lambda g, m: (g, 0, 0)) if gb > 1 else (lambda g, m: (0, 0, 0)))
    G = max(ga, gb)

    def body(a_ref, b_ref, o_ref, narrow_ref):
        o_ref[0] = _dot(a_ref[0].astype(BF16), b_ref[0].astype(BF16), TN)
        narrow_ref[0] = o_ref[0].astype(BF16)

    out = pl.BlockSpec((1, bm, N), lambda g, m: (g, m, 0))
    return pl.pallas_call(
        body, name=name, grid=(G, M // bm),
        in_specs=[pl.BlockSpec((1, T, bm), (lambda g, m: (g, 0, m)) if ga > 1 else (lambda g, m: (0, 0, m))), b_spec],
        out_specs=[out, out],
        out_shape=[_out((G, M, N), F32), _out((G, M, N), BF16)],
        compiler_params=_params(2, VMEM_LIMIT),
    )(*_hbm(a, b))


QKV_PIECE = 256


def _qkv_shard(dy_refs, k, n_col):
    width = dy_refs[0].shape[1]
    parts = []
    for col in range(k * n_col, (k + 1) * n_col, QKV_PIECE):
        parts.append(dy_refs[col // width][:, col % width:col % width + QKV_PIECE])
    return jnp.concatenate(parts, axis=1)


def _qkv_bwd_in(dys, w, x, g_pre, dh, name):
    T, D = x.shape
    n = len(dys)
    S, _, N = w.shape
    tm = min(WIDE_ROW_BLOCK, T)

    def body(*refs):
        dy_refs = refs[:n]
        w_ref, x_ref, gpre_ref, dh_ref, dx_ref, dgain_ref, acc_s = refs[n:]
        i, k = pl.program_id(0), pl.program_id(1)
        for shard in range(S):
            @pl.when(k == shard)
            def _(shard=shard):
                part = _dot(_qkv_shard(dy_refs, shard, N), w_ref[0], NT)
                if shard == 0:
                    acc_s[...] = part
                else:
                    acc_s[...] += part

        @pl.when(k == S - 1)
        def _():
            dx, dgain = _rms_bwd(x_ref[...], gpre_ref[...], acc_s[...])
            dx_ref[...] = dh_ref[...] + dx
            _accumulate(dgain_ref, dgain, i == 0)

    row = pl.BlockSpec((tm, D), lambda i, k: (i, 0))
    vec = pl.BlockSpec((1, D), lambda i, k: (0, 0))
    return pl.pallas_call(
        body, name=name, grid=(T // tm, S),
        in_specs=[pl.BlockSpec((tm, dy.shape[1]), lambda i, k: (i, 0)) for dy in dys]
        + [pl.BlockSpec((1, D, N), lambda i, k: (k, 0, 0)), row, vec, row],
        out_specs=[row, vec],
        out_shape=[_out((T, D), F32), _out((1, D), F32)],
        scratch_shapes=[pltpu.VMEM((tm, D), F32)],
        compiler_params=_params(2, VMEM_LIMIT_WIDE),
    )(*_hbm(*dys, w, x, g_pre, dh))


def _dw_in(a, dys, n_col, bm, name):
    T, M = a.shape
    n = len(dys)
    S = n * dys[0].shape[1] // n_col

    def body(*refs):
        a_ref, dy_refs = refs[0], refs[1:1 + n]
        o_ref, narrow_ref = refs[1 + n:]
        k = pl.program_id(1)
        for shard in range(S):
            @pl.when(k == shard)
            def _(shard=shard):
                o_ref[0] = _dot(a_ref[...], _qkv_shard(dy_refs, shard, n_col), TN)
                narrow_ref[0] = o_ref[0].astype(BF16)

    out = pl.BlockSpec((1, bm, n_col), lambda m, k: (k, m, 0))
    return pl.pallas_call(
        body, name=name, grid=(M // bm, S),
        in_specs=[pl.BlockSpec((T, bm), lambda m, k: (0, m))]
        + [pl.BlockSpec((T, dy.shape[1]), lambda m, k: (0, 0)) for dy in dys],
        out_specs=[out, out],
        out_shape=[_out((S, M, n_col), F32), _out((S, M, n_col), BF16)],
        compiler_params=_params(2, VMEM_LIMIT_WIDE),
    )(*_hbm(a, *dys))


def _norm_proj(x, g_pre, w, name):
    T, D = x.shape
    S, _, N = w.shape
    tm = min(WIDE_ROW_BLOCK, T)

    def body(x_ref, g_ref, w_ref, o_ref, xn_ref, xn_s):
        @pl.when(pl.program_id(1) == 0)
        def _():
            xn_s[...] = _rms_fwd(x_ref[...], g_ref[...]).astype(BF16)
            xn_ref[...] = xn_s[...]

        o_ref[...] = _dot(xn_s[...], w_ref[0]).astype(BF16)

    row = pl.BlockSpec((tm, D), lambda i, k: (i, 0))
    return pl.pallas_call(
        body, name=name, grid=(T // tm, S),
        in_specs=[row, pl.BlockSpec((1, D), lambda i, k: (0, 0)), pl.BlockSpec((1, D, N), lambda i, k: (k, 0, 0))],
        out_specs=[pl.BlockSpec((tm, N), lambda i, k: (i, k)), row],
        out_shape=[_out((T, S * N), BF16), _out((T, D), BF16)],
        scratch_shapes=[pltpu.VMEM((tm, D), BF16)],
        compiler_params=_params(2, VMEM_LIMIT_WIDE),
    )(*_hbm(x, g_pre, w))


def _mix_out_fwd(h, o_a, o_b, g_sb, g_ch, w_out, g_post, name):
    T, D = h.shape
    W = g_sb.shape[1]
    tm = min(WIDE_ROW_BLOCK, T)

    def body(h_ref, oa_ref, ob_ref, gsb_ref, gch_ref, w_ref, gpost_ref, h2_ref, mixed_ref, mo_ref):
        mixed_ref[:, :W] = _rms_fwd(oa_ref[...], gsb_ref[...]).astype(BF16)
        mixed_ref[:, W:] = _rms_fwd(ob_ref[...], gch_ref[...]).astype(BF16)
        mo = _dot(mixed_ref[...], w_ref[...])
        mo_ref[...] = mo
        h2_ref[...] = h_ref[...] + _rms_fwd(mo, gpost_ref[...])

    row = pl.BlockSpec((tm, D), lambda i: (i, 0))
    part = pl.BlockSpec((tm, W), lambda i: (i, 0))
    half = pl.BlockSpec((1, W), lambda i: (0, 0))
    return pl.pallas_call(
        body, name=name, grid=(T // tm,),
        in_specs=[row, part, part, half, half, pl.BlockSpec((D, D), lambda i: (0, 0)), pl.BlockSpec((1, D), lambda i: (0, 0))],
        out_specs=[row, row, row],
        out_shape=[_out((T, D), F32), _out((T, D), BF16),
                   _out((T, D), F32)],
        compiler_params=_params(1, VMEM_LIMIT_WIDE),
    )(*_hbm(h, o_a, o_b, g_sb, g_ch, w_out, g_post))


def _mix_out_bwd(dh, mo, g_post, w_out, o_a, o_b, g_sb, g_ch, name):
    T, D = dh.shape
    W = g_sb.shape[1]
    tm = min(WIDE_ROW_BLOCK, T)

    def body(dh_ref, mo_ref, gpost_ref, w_ref, oa_ref, ob_ref, gsb_ref, gch_ref,
             dmo_ref, doa_ref, dob_ref, dgpost_ref, dgsb_ref, dgch_ref):
        first = pl.program_id(0) == 0
        dmo, dgpost = _rms_bwd(mo_ref[...], gpost_ref[...], dh_ref[...])
        dmo_ref[...] = dmo.astype(BF16)
        dmix = _dot(dmo_ref[...], w_ref[...], NT)
        doa_ref[...], dgsb = _rms_bwd(oa_ref[...], gsb_ref[...], dmix[:, :W])
        dob_ref[...], dgch = _rms_bwd(ob_ref[...], gch_ref[...], dmix[:, W:])
        _accumulate(dgpost_ref, dgpost, first)
        _accumulate(dgsb_ref, dgsb, first)
        _accumulate(dgch_ref, dgch, first)

    row = pl.BlockSpec((tm, D), lambda i: (i, 0))
    part = pl.BlockSpec((tm, W), lambda i: (i, 0))
    vec = pl.BlockSpec((1, D), lambda i: (0, 0))
    half = pl.BlockSpec((1, W), lambda i: (0, 0))
    return pl.pallas_call(
        body, name=name, grid=(T // tm,),
        in_specs=[row, row, vec, pl.BlockSpec((D, D), lambda i: (0, 0)), part, part, half, half],
        out_specs=[row, part, part, vec, half, half],
        out_shape=[_out((T, D), BF16), _out((T, W), F32),
                   _out((T, W), F32), _out((1, D), F32),
                   _out((1, W), F32), _out((1, W), F32)],
        compiler_params=_params(1, VMEM_LIMIT_WIDE),
    )(*_hbm(dh, mo, g_post, w_out, o_a, o_b, g_sb, g_ch))


def _ple_loss(h, p, target, w_proj, w_gate, g_post, name):
    T, D = h.shape
    P = p.shape[1]
    S = N_CHIPS
    C = D // S
    tm = min(ROW_BLOCK, T)

    def body(h_ref, p_ref, t_ref, wp_ref, wg_ref, g_ref, loss_ref, dh_ref, dproj_ref, dgate_ref, dgain_ref):
        first = pl.program_id(0) == 0
        h3 = h_ref[...]
        proj = _dot(p_ref[...].astype(BF16), wp_ref[...])
        s = _sigmoid(_dot(h3.astype(BF16), wg_ref[...]))
        e = proj * s
        diff = h3 + _rms_fwd(e, g_ref[...]) - t_ref[...]
        part = 0.5 * jnp.sum(jnp.mean(diff * diff, axis=-1, keepdims=True), axis=0, keepdims=True)
        _accumulate(loss_ref, jnp.broadcast_to(part, loss_ref.shape), first)
        dy = diff * (1.0 / D)
        de, dgain = _rms_bwd(e, g_ref[...], dy)
        _accumulate(dgain_ref, dgain, first)
        dproj = (de * s).astype(BF16)
        for j in range(S):
            dproj_ref[j] = dproj[:, j * C:(j + 1) * C]
        dgate_ref[...] = (de * proj * s * (1.0 - s)).astype(BF16)
        dh_ref[...] = dy + _dot(dgate_ref[...], wg_ref[...], NT)

    row = pl.BlockSpec((tm, D), lambda i: (i, 0))
    vec = pl.BlockSpec((1, D), lambda i: (0, 0))
    return pl.pallas_call(
        body, name=name, grid=(T // tm,),
        in_specs=[row, pl.BlockSpec((tm, P), lambda i: (i, 0)), row,
                  pl.BlockSpec((P, D), lambda i: (0, 0)), pl.BlockSpec((D, D), lambda i: (0, 0)), vec],
        out_specs=[pl.BlockSpec((8, 128), lambda i: (0, 0)), row,
                   pl.BlockSpec((S, tm, C), lambda i: (0, i, 0)), row, vec],
        out_shape=[_out((8, 128), F32), _out((T, D), F32),
                   _out((S, T, C), BF16), _out((T, D), BF16),
                   _out((1, D), F32)],
        compiler_params=_params(1, VMEM_LIMIT_WIDE),
    )(*_hbm(h, p, target, w_proj, w_gate, g_post))


def _sb_scores(q, kj, mask):
    z = _dot(q, kj, NT)
    sp = jnp.maximum(z, 0.0) + jnp.log(1.0 + jnp.exp(-jnp.abs(z)))
    return z, sp if mask is None else jnp.where(mask, sp, 0.0)


def _strict_causal():
    rows = lax.broadcasted_iota(jnp.int32, (SB_BLOCK, SB_BLOCK), 0)
    cols = lax.broadcasted_iota(jnp.int32, (SB_BLOCK, SB_BLOCK), 1)
    return cols < rows


def _tri(cmp):
    r = lax.broadcasted_iota(jnp.int32, (2 * SB_BLOCK, SB_BLOCK), 0) % SB_BLOCK
    c = lax.broadcasted_iota(jnp.int32, (2 * SB_BLOCK, SB_BLOCK), 1)
    return jnp.where(cmp(r, c), 1.0, 0.0).astype(BF16)


def _cum(x, tri):
    return _dot(jnp.concatenate(_split2(x), axis=1), tri)


def _pair_lanes():
    lane = lax.broadcasted_iota(jnp.int32, (1, PAIR), 1)
    return [lane < HEAD_DIM, lane >= HEAD_DIM]


def _only(lanes, x):
    return jnp.where(lanes, x, jnp.zeros_like(x))


def _sb_fwd(qkv, name):
    T = qkv.shape[0]
    B = SB_BLOCK
    W = SB_PAIRS * PAIR
    steps = N_HEADS // (2 * SB_PAIRS)
    heads = [(p, h) for p in range(SB_PAIRS) for h in range(2)]

    def body(q_ref, k_ref, v_ref, o_ref):
        i = pl.program_id(1)
        after = _tri(lambda r, c: r > c)
        lanes = _pair_lanes()
        cols = [slice(p * PAIR, (p + 1) * PAIR) for p in range(SB_PAIRS)]
        q = {(p, h): _only(lanes[h], q_ref[:, cols[p]] * ATT_SCALE) for p, h in heads}

        def tiles(j, carries, mask):
            at = pl.ds(pl.multiple_of(j * B, B), B)
            scores = [_sb_scores(q[ph], k_ref[at, cols[ph[0]]], mask) for ph in heads]
            laters = [_cum(sp, after) for _, sp in scores]
            out = []
            for ph, (z, sp), later, (run, acc) in zip(heads, scores, laters, carries):
                a = jnp.exp(z - sp - later - run)
                if mask is not None:
                    a = jnp.where(mask, a, 0.0)
                out.append((run + later[:, 0:1] + sp[:, 0:1],
                            acc + _dot(a.astype(BF16), _only(lanes[ph[1]], v_ref[at, cols[ph[0]]]))))
            return tuple(out)

        zero = (jnp.zeros((B, 1), F32), jnp.zeros((B, PAIR), F32))
        carries = tiles(i, (zero,) * len(heads), _strict_causal())
        carries = lax.fori_loop(0, i, lambda jj, cs: tiles(i - 1 - jj, cs, None), carries)
        for p in range(SB_PAIRS):
            o_ref[:, cols[p]] = carries[2 * p][1] + carries[2 * p + 1][1]

    blk = lambda off: pl.BlockSpec((B, W), lambda g, i: (i, g + off))
    full = lambda off: pl.BlockSpec((T, W), lambda g, i: (0, g + off))
    return pl.pallas_call(
        body, name=name, grid=(steps, T // B),
        in_specs=[blk(0), full(steps), full(2 * steps)],
        out_specs=blk(0),
        out_shape=_out((T, N_HEADS * HEAD_DIM), F32),
        compiler_params=_params(2, VMEM_LIMIT),
    )(*_hbm(qkv, qkv, qkv))


def _sb_bwd(qkv, do, o, after, name):
    T = qkv.shape[0]
    B = SB_BLOCK
    W = SB_PAIRS * PAIR
    steps = N_HEADS // (2 * SB_PAIRS)
    n_blocks = T // B
    heads = [(p, h) for p in range(SB_PAIRS) for h in range(2)]

    def body(q_ref, k_ref, v_ref, do_ref, o_ref, dq_ref, dk_ref, dv_ref, dk_s, dv_s):
        i = pl.program_id(1)

        @pl.when(i == 0)
        def _():
            dk_s[...] = jnp.zeros_like(dk_s)
            dv_s[...] = jnp.zeros_like(dv_s)

        after = _tri(lambda r, c: r > c)
        since = _tri(lambda r, c: r >= c)
        lanes = _pair_lanes()
        cols = [slice(p * PAIR, (p + 1) * PAIR) for p in range(SB_PAIRS)]
        q = {(p, h): _only(lanes[h], q_ref[:, cols[p]] * ATT_SCALE) for p, h in heads}
        do = {(p, h): _only(lanes[h], do_ref[:, cols[p]].astype(BF16)) for p, h in heads}
        total = {ph: jnp.sum(do[ph].astype(F32) * o_ref[:, cols[ph[0]]], axis=1, keepdims=True) for ph in heads}

        def tiles(j, carries, mask):
            at = pl.ds(pl.multiple_of(j * B, B), B)
            ks = [k_ref[at, c] for c in cols]
            vs = [v_ref[at, c] for c in cols]
            scores = [_sb_scores(q[ph], ks[ph[0]], mask) for ph in heads]
            laters = [_cum(sp, after) for _, sp in scores]
            das = [_dot(do[ph], vs[ph[0]], NT) for ph in heads]
            a_s, gs = [], []
            for (z, sp), later, da, carry in zip(scores, laters, das, carries):
                a = jnp.exp(z - sp - later - carry[0])
                if mask is not None:
                    a = jnp.where(mask, a, 0.0)
                a = a.astype(BF16)
                a_s.append(a)
                gs.append(a.astype(F32) * da)
            sinces = [_cum(g, since) for g in gs]
            dzs = []
            for ph, (_, sp), g, from_s, carry in zip(heads, scores, gs, sinces, carries):
                g_before = total[ph] - carry[1] - from_s
                fail = jnp.exp(-sp)
                dz = fail * (g + g_before) - g_before
                if mask is not None:
                    dz = jnp.where(mask, dz, 0.0)
                dzs.append(dz.astype(BF16))
            out = []
            for ph, (_, sp), a, dz, later, from_s, carry in zip(heads, scores, a_s, dzs, laters, sinces, carries):
                dk_s[at, cols[ph[0]]] += _dot(dz, q[ph], TN)
                dv_s[at, cols[ph[0]]] += _dot(a, do[ph], TN)
                out.append((carry[0] + later[:, 0:1] + sp[:, 0:1], carry[1] + from_s[:, 0:1],
                            carry[2] + _dot(dz, _only(lanes[ph[1]], ks[ph[0]]))))
            return tuple(out)

        col = jnp.zeros((B, 1), F32)
        zero = (col, col, jnp.zeros((B, PAIR), F32))
        carries = tiles(i, (zero,) * len(heads), _strict_causal())
        last = lax.fori_loop(0, i, lambda jj, cs: tiles(i - 1 - jj, cs, None), carries)
        for p in range(SB_PAIRS):
            dq_ref[:, cols[p]] = ((last[2 * p][2] + last[2 * p + 1][2]) * ATT_SCALE).astype(BF16)

        @pl.when(i == n_blocks - 1)
        def _():
            dk_ref[...] = dk_s[...].astype(BF16)
            dv_ref[...] = dv_s[...].astype(BF16)

    blk = lambda off: pl.BlockSpec((B, W), lambda g, i: (i, g + off))
    full = lambda off: pl.BlockSpec((T, W), lambda g, i: (0, g + off))
    out = _out((T, N_HEADS * HEAD_DIM), BF16)
    return pl.pallas_call(
        lambda after_ref, *refs: body(*refs), name=name, grid=(steps, n_blocks),
        in_specs=[ANY, blk(0), full(steps), full(2 * steps), blk(0), blk(0)],
        out_specs=[blk(0), full(0), full(0)],
        out_shape=[out, out, out],
        scratch_shapes=[pltpu.VMEM((T, W), F32)] * 2,
        compiler_params=_params(2, VMEM_LIMIT),
    )(after, *_hbm(qkv, qkv, qkv, do, o))


NEAR = BAND - PAD + REL_CLIP
FAR = BAND - NEAR
NEAR_REL = 2 * REL_CLIP
BIAS_ROWS = 8


def _rel_onehot(i, transposed):
    shape = (NEAR, NEAR_REL) if transposed else (NEAR_REL, NEAR)
    j = FAR + lax.broadcasted_iota(jnp.int32, shape, 0 if transposed else 1)
    r = lax.broadcasted_iota(jnp.int32, shape, 1 if transposed else 0)
    idx = jnp.clip(i + PAD - j, -REL_CLIP, REL_CLIP) + REL_CLIP
    return jnp.where(idx - 1 == r, 1.0, 0.0).astype(BF16)


def _bias_table(rel_bias, name):
    def body(near_ref, far_ref, o_ref):
        rb = near_ref[...]
        hi, lo = _split2(rb)
        lo2 = (rb - hi.astype(F32) - lo.astype(F32)).astype(BF16)
        far = jnp.broadcast_to(far_ref[...], (N_HEADS, FAR))
        for k in range(BIAS_ROWS):
            onehot = _rel_onehot(pl.program_id(0) * BIAS_ROWS + k, False)
            o_ref[k, :, :FAR] = far
            o_ref[k, :, FAR:] = _dot(hi, onehot) + _dot(lo, onehot) + _dot(lo2, onehot)

    return pl.pallas_call(
        body, name=name, grid=(CHUNK // BIAS_ROWS,),
        in_specs=[pl.BlockSpec((N_HEADS, NEAR_REL), lambda i: (0, 0)), pl.BlockSpec((N_HEADS, 1), lambda i: (0, 0))],
        out_specs=pl.BlockSpec((BIAS_ROWS, N_HEADS, BAND), lambda i: (i, 0, 0)),
        out_shape=_out((CHUNK, N_HEADS, BAND), F32),
        compiler_params=_params(1),
    )(*_hbm(rel_bias[:, 1:], rel_bias[:, N_REL - 1:]))


def _bias_grad(dbias_t, name):
    def body(d_ref, near_ref, far_ref):
        near, far = None, None
        for k in range(BIAS_ROWS):
            onehot = _rel_onehot(pl.program_id(0) * BIAS_ROWS + k, True)
            hi, lo = _split2(d_ref[k, :, FAR:])
            part = _dot(hi, onehot) + _dot(lo, onehot)
            rest = jnp.sum(d_ref[k, :, :FAR], axis=1, keepdims=True)
            near, far = (part, rest) if near is None else (near + part, far + rest)
        first = pl.program_id(0) == 0
        _accumulate(near_ref, near, first)
        _accumulate(far_ref, jnp.broadcast_to(far, far_ref.shape), first)

    near, far = pl.pallas_call(
        body, name=name, grid=(CHUNK // BIAS_ROWS,),
        in_specs=[pl.BlockSpec((BIAS_ROWS, N_HEADS, BAND), lambda i: (i, 0, 0))],
        out_specs=[pl.BlockSpec((N_HEADS, NEAR_REL), lambda i: (0, 0)), pl.BlockSpec((N_HEADS, 128), lambda i: (0, 0))],
        out_shape=[_out((N_HEADS, NEAR_REL), F32), _out((N_HEADS, 128), F32)],
        compiler_params=_params(1),
    )(*_hbm(dbias_t))
    return jnp.pad(near, ((0, 0), (1, 0))).at[:, N_REL - 1].add(far[:, 0])


def _ch_probs(scores, bias, valid):
    z = jnp.where(valid, scores * ATT_SCALE + bias, NEG_INF)
    e = jnp.exp(z - jnp.max(z, axis=-1, keepdims=True))
    return e / jnp.sum(e, axis=-1, keepdims=True)


CH_HEADS = [(pair, h) for pair in range(N_HEADS // 2) for h in range(2)]
CH_COLS = [slice(pair * PAIR, (pair + 1) * PAIR) for pair in range(N_HEADS // 2)]


CH_GROUP = 2
CH_Q = CH_GROUP * CHUNK
CH_WIN = (LOOKBACK + CH_GROUP) * CHUNK


def _ch_valid(n):
    row_chunk = lax.broadcasted_iota(jnp.int32, (CH_Q, CH_WIN), 0) // CHUNK
    slot = lax.broadcasted_iota(jnp.int32, (CH_Q, CH_WIN), 1)
    ahead = slot // CHUNK - row_chunk
    return (ahead >= 0) & (ahead <= LOOKBACK) & (n * CH_Q + slot >= PAD)


def _ch_group_bias(bias):
    shifted = [jnp.pad(bias, ((0, 0), (0, 0), (c * CHUNK, (CH_GROUP - 1 - c) * CHUNK))) for c in range(CH_GROUP)]
    return jnp.concatenate(shifted, axis=1)


def _ch_fold_bias_grad(dbias):
    parts = [dbias[:, c * CHUNK:(c + 1) * CHUNK, c * CHUNK:c * CHUNK + BAND] for c in range(CH_GROUP)]
    return sum(parts[1:], parts[0])


def _ch_fwd(qkv, bias, name):
    T = qkv.shape[0]
    W = N_HEADS * HEAD_DIM

    def body(q_ref, k_ref, v_ref, b_ref, o_ref, kp, vp):
        n = pl.program_id(0)

        @pl.when(n == 0)
        def _():
            _ch_load_padded(k_ref, v_ref, kp, vp)

        win = pl.ds(pl.multiple_of(n * CH_Q, CH_Q), CH_WIN)
        valid = _ch_valid(n)
        lanes = _pair_lanes()
        scores = [_dot(_only(lanes[h], q_ref[:, CH_COLS[pair]]), kp[win, CH_COLS[pair]], NT) for pair, h in CH_HEADS]
        probs = [_ch_probs(s, b_ref[2 * pair + h], valid).astype(BF16) for s, (pair, h) in zip(scores, CH_HEADS)]
        outs = [_dot(p, _only(lanes[h], vp[win, CH_COLS[pair]])) for p, (pair, h) in zip(probs, CH_HEADS)]
        for pair, cols in enumerate(CH_COLS):
            o_ref[:, cols] = outs[2 * pair] + outs[2 * pair + 1]

    full = lambda col: pl.BlockSpec((T, W), lambda n: (0, col))
    return pl.pallas_call(
        body, name=name, grid=(T // CH_Q,),
        in_specs=[pl.BlockSpec((CH_Q, W), lambda n: (n, 3)), full(4), full(5),
                  pl.BlockSpec((N_HEADS, CH_Q, CH_WIN), lambda n: (0, 0, 0))],
        out_specs=pl.BlockSpec((CH_Q, W), lambda n: (n, 0)),
        out_shape=_out((T, W), F32),
        scratch_shapes=[pltpu.VMEM((PAD + T, W), BF16)] * 2,
        compiler_params=_params(1, VMEM_LIMIT),
    )(*_hbm(qkv, qkv, qkv, bias))


def _ch_load_padded(k_ref, v_ref, kp, vp):
    for src, dst in ((k_ref, kp), (v_ref, vp)):
        dst[:PAD, :] = jnp.zeros((PAD, dst.shape[1]), dst.dtype)
        dst[PAD:, :] = src[...]


def _ch_bwd(qkv, bias, do, after, name):
    T = qkv.shape[0]
    W = N_HEADS * HEAD_DIM
    n_chunks = T // CH_Q

    def body(q_ref, k_ref, v_ref, b_ref, do_ref, dq_ref, dk_ref, dv_ref, db_ref, kp, vp, dk_s, dv_s):
        n = pl.program_id(0)

        @pl.when(n == 0)
        def _():
            _ch_load_padded(k_ref, v_ref, kp, vp)
            dk_s[...] = jnp.zeros_like(dk_s)
            dv_s[...] = jnp.zeros_like(dv_s)
            db_ref[...] = jnp.zeros_like(db_ref)

        win = pl.ds(pl.multiple_of(n * CH_Q, CH_Q), CH_WIN)
        valid = _ch_valid(n)
        lanes = _pair_lanes()
        kws = [kp[win, cols] for cols in CH_COLS]
        vws = [vp[win, cols] for cols in CH_COLS]
        qs = [_only(lanes[h], q_ref[:, CH_COLS[pair]]) for pair, h in CH_HEADS]
        dos = [_only(lanes[h], do_ref[:, CH_COLS[pair]].astype(BF16)) for pair, h in CH_HEADS]
        scores = [_dot(q, kws[pair], NT) for q, (pair, _) in zip(qs, CH_HEADS)]
        dps = [_dot(do, vws[pair], NT) for do, (pair, _) in zip(dos, CH_HEADS)]
        probs = [_ch_probs(s, b_ref[2 * pair + h], valid) for s, (pair, h) in zip(scores, CH_HEADS)]
        dzs = [p * (dp - jnp.sum(dp * p, axis=-1, keepdims=True)) for p, dp in zip(probs, dps)]
        for k, dz in enumerate(dzs):
            db_ref[k] += dz
        dzbs = [(dz * ATT_SCALE).astype(BF16) for dz in dzs]
        dqs = [_dot(dz, _only(lanes[h], kws[pair])) for dz, (pair, h) in zip(dzbs, CH_HEADS)]
        dks = [_dot(dz, q, TN) for dz, q in zip(dzbs, qs)]
        dvs = [_dot(p.astype(BF16), do, TN) for p, do in zip(probs, dos)]
        for pair, cols in enumerate(CH_COLS):
            dq_ref[:, cols] = (dqs[2 * pair] + dqs[2 * pair + 1]).astype(BF16)
            dk_s[win, cols] += dks[2 * pair] + dks[2 * pair + 1]
            dv_s[win, cols] += dvs[2 * pair] + dvs[2 * pair + 1]

        @pl.when(n == n_chunks - 1)
        def _():
            dk_ref[...] = dk_s[PAD:, :].astype(BF16)
            dv_ref[...] = dv_s[PAD:, :].astype(BF16)

    full = lambda col: pl.BlockSpec((T, W), lambda n: (0, col))
    blk = lambda col: pl.BlockSpec((CH_Q, W), lambda n: (n, col))
    tab = pl.BlockSpec((N_HEADS, CH_Q, CH_WIN), lambda n: (0, 0, 0))
    out = _out((T, W), BF16)
    return pl.pallas_call(
        lambda after_ref, *refs: body(*refs), name=name, grid=(n_chunks,),
        in_specs=[ANY, blk(3), full(4), full(5), tab, blk(0)],
        out_specs=[blk(0), full(0), full(0), tab],
        out_shape=[out, out, out, _out((N_HEADS, CH_Q, CH_WIN), F32)],
        scratch_shapes=[pltpu.VMEM((PAD + T, W), BF16)] * 2 + [pltpu.VMEM((PAD + T, W), F32)] * 2,
        compiler_params=_params(1, VMEM_LIMIT),
    )(after, *_hbm(qkv, qkv, qkv, bias, do))


def _rows_split(a, parts):
    return a.reshape(a.shape[:-2] + (parts, a.shape[-2] // parts, a.shape[-1]))


def _cast_into_own_slot(me, c, ws, in_chip_order, name):
    parts = 2
    ws = [_rows_split(_rows_split(w, 2), parts) for w in ws]
    n = len(ws)

    def body(me_ref, c_ref, *refs):
        for src, dst in zip(refs[:n], refs[n:]):
            dst[0, 0, 0] = src[0, 0].astype(BF16)

    def specs(w, plain):
        block = (1, 1) + w.shape[2:]
        if plain:
            return (pl.BlockSpec(block, lambda d, r, me_ref, c_ref: (d, r, 0, 0)),
                    pl.BlockSpec((1,) + block, lambda d, r, me_ref, c_ref: (me_ref[0], d, r, 0, 0)))
        return (pl.BlockSpec(block, lambda d, r, me_ref, c_ref: (d ^ c_ref[0], r, 0, 0)),
                pl.BlockSpec((1,) + block, lambda d, r, me_ref, c_ref: (0, d, r, 0, 0)))

    both = [specs(w, plain) for w, plain in zip(ws, in_chip_order)]
    outs = pl.pallas_call(
        body, name=name,
        grid_spec=pltpu.PrefetchScalarGridSpec(
            num_scalar_prefetch=2, grid=(2, parts),
            in_specs=[s[0] for s in both], out_specs=[s[1] for s in both]),
        out_shape=[_out((N_CHIPS,) + w.shape, BF16) for w in ws],
        compiler_params=_params(2, VMEM_LIMIT),
    )(me, c, *_hbm(*ws))
    return [o.reshape(N_CHIPS, 2, o.shape[2] * o.shape[3], o.shape[4]) for o in outs]


def _zone_slots(in_chip_order):
    x, y, c, _ = _place()
    me = 2 * x + y
    if in_chip_order:
        return (me, c), (lambda r: (me, c)), (lambda r: (me ^ r, c)), (lambda r: (me ^ r, c))
    return (0, 0), (lambda r: (r, 0)), (lambda r: (r, 0)), (lambda r: (r, 1))


def _pair_add(c, mine, got, permuted, name):
    parts = 2
    mine = [_rows_split(m, parts) for m in mine]
    got = [_rows_split(g, parts) for g in got]
    n = len(mine)

    def body(c_ref, *refs):
        for a, b, o in zip(refs[:n], refs[n:2 * n], refs[2 * n:]):
            o[0, 0] = (a[0, 0, 0] + b[0, 0].astype(F32)).astype(BF16)

    def mine_spec(m, perm):
        if perm:
            return pl.BlockSpec((1, 1, 1) + m.shape[3:], lambda j, r, c_ref: (j, 0, r, 0, 0))
        return pl.BlockSpec((1, 1, 1) + m.shape[3:], lambda j, r, c_ref: (j, c_ref[0], r, 0, 0))

    def got_spec(g):
        return pl.BlockSpec((1, 1) + g.shape[2:], lambda j, r, c_ref: (j, r, 0, 0))

    outs = pl.pallas_call(
        body, name=name,
        grid_spec=pltpu.PrefetchScalarGridSpec(
            num_scalar_prefetch=1, grid=(N_CHIPS, parts),
            in_specs=[mine_spec(m, perm) for m, perm in zip(mine, permuted)] + [got_spec(g) for g in got],
            out_specs=[got_spec(g) for g in got]),
        out_shape=[_out(g.shape, BF16) for g in got],
        compiler_params=_params(2, VMEM_LIMIT),
    )(c, *_hbm(*mine, *got))
    return [o.reshape(o.shape[0], o.shape[1] * o.shape[2], o.shape[3]) for o in outs]


def _chip_add(me, partials, landed, permuted, name):
    parts = 2
    ps = [_rows_split(x, parts) for x in partials]
    ls = [_rows_split(x, parts) for x in landed]
    n = len(ps)

    def body(me_ref, *refs):
        for own, got, o in zip(refs[:n], refs[n:2 * n], refs[2 * n:]):
            acc = own[0, 0].astype(F32)
            for r in range(N_CHIPS - 1):
                acc = acc + got[r, 0].astype(F32)
            o[0] = acc

    def own_spec(x, perm):
        if perm:
            return pl.BlockSpec((1, 1) + x.shape[2:], lambda r, me_ref: (0, r, 0, 0))
        return pl.BlockSpec((1, 1) + x.shape[2:], lambda r, me_ref: (me_ref[0], r, 0, 0))

    outs = pl.pallas_call(
        body, name=name,
        grid_spec=pltpu.PrefetchScalarGridSpec(
            num_scalar_prefetch=1, grid=(parts,),
            in_specs=[own_spec(x, perm) for x, perm in zip(ps, permuted)]
            + [pl.BlockSpec((N_CHIPS - 1, 1) + x.shape[2:], lambda r, me_ref: (0, r, 0, 0)) for x in ls],
            out_specs=[pl.BlockSpec((1,) + x.shape[2:], lambda r, me_ref: (r, 0, 0)) for x in ps]),
        out_shape=[_out(x.shape[1:], F32) for x in ps],
        compiler_params=_params(1, VMEM_LIMIT),
    )(me, *_hbm(*ps, *ls))
    return [o.reshape(o.shape[0] * o.shape[1], o.shape[2]) for o in outs]


def _adamw_math(w, g, m, v):
    m = ADAM_B1 * m + (1.0 - ADAM_B1) * g
    v = ADAM_B2 * v + (1.0 - ADAM_B2) * (g * g)
    m_hat = m / (1.0 - ADAM_B1 ** ADAM_STEP)
    v_hat = v / (1.0 - ADAM_B2 ** ADAM_STEP)
    delta = -ADAM_LR * (m_hat / (jnp.sqrt(v_hat) + ADAM_EPS) + ADAM_WD * w)
    return delta, m, v


def _adamw(ws, gs, ms, vs, parts, name):
    n = len(ws)
    flat = [_rows_split(a, parts) for a in (*ws, *gs, *ms, *vs)]

    def body(*refs):
        ins, outs = refs[:4 * n], refs[4 * n:]
        for k in range(n):
            d, m, v = _adamw_math(ins[k][...], ins[n + k][...], ins[2 * n + k][...], ins[3 * n + k][...])
            outs[k][...] = d
            outs[n + k][...] = m
            outs[2 * n + k][...] = v

    spec = lambda a: pl.BlockSpec((1,) + a.shape[1:], lambda i: (i, 0, 0))
    outs = pl.pallas_call(
        body, name=name, grid=(parts,),
        in_specs=[spec(a) for a in flat], out_specs=[spec(a) for a in flat[:n]] * 3,
        out_shape=[_out(a.shape, F32) for a in flat[:n]] * 3,
        compiler_params=_params(1, VMEM_LIMIT),
    )(*_hbm(*flat))
    outs = [o.reshape(o.shape[0] * o.shape[1], o.shape[2]) for o in outs]
    return outs[:n], outs[n:2 * n], outs[2 * n:]


def _adamw_halves(c, ws, owns, others, ms, vs, name):
    parts = 4
    n = len(ws)
    whole = [_rows_split(_rows_split(a, 2), parts) for a in (*ws, *ms, *vs)]
    halves = [_rows_split(a, parts) for a in (*owns, *others)]

    def body(c_ref, *refs):
        ins, outs = refs[:5 * n], refs[5 * n:]
        mine = pl.program_id(0) == c_ref[0]
        for k in range(n):
            g = jnp.where(mine, ins[3 * n + k][0], ins[4 * n + k][0])
            d, m, v = _adamw_math(ins[k][0, 0], g, ins[n + k][0, 0], ins[2 * n + k][0, 0])
            for slot, val in enumerate((g, d, m, v)):
                outs[slot * n + k][0, 0] = val

    wspec = lambda a: pl.BlockSpec((1, 1) + a.shape[2:], lambda h, r, c_ref: (h, r, 0, 0))
    hspec = lambda a: pl.BlockSpec((1,) + a.shape[1:], lambda h, r, c_ref: (r, 0, 0))
    outs = pl.pallas_call(
        body, name=name,
        grid_spec=pltpu.PrefetchScalarGridSpec(
            num_scalar_prefetch=1, grid=(2, parts),
            in_specs=[wspec(a) for a in whole] + [hspec(a) for a in halves],
            out_specs=[wspec(a) for a in whole[:n]] * 4),
        out_shape=[_out(a.shape, F32) for a in whole[:n]] * 4,
        compiler_params=_params(2, VMEM_LIMIT),
    )(c, *_hbm(*whole, *halves))
    outs = [o.reshape(2 * parts * o.shape[2], o.shape[3]) for o in outs]
    return outs[:n], outs[n:2 * n], outs[2 * n:3 * n], outs[3 * n:]


def _place():
    x, y, c = lax.axis_index("x"), lax.axis_index("y"), lax.axis_index("c")
    peers = [(x ^ (r >> 1), y ^ (r & 1), c) for r in (1, 2, 3)]
    return x, y, c, peers


def _handshake(peers):
    barrier = pltpu.get_barrier_semaphore()
    for peer in peers:
        pl.semaphore_signal(barrier, inc=1, device_id=peer, device_id_type=MESH)
    pl.semaphore_wait(barrier, len(peers))


ANY = pl.BlockSpec(memory_space=pl.ANY)
HBM = pl.BlockSpec(memory_space=pltpu.HBM)
SEM = pl.BlockSpec(memory_space=pltpu.SEMAPHORE)
SPLIT_COPY = pltpu.SideEffectType.DATAFLOW_SIDE_EFFECTING


def _split_start(body, name, collective_id, operands, n_sems, after=None):
    n = len(operands)
    extra = [] if after is None else [after]

    def wrapped(*refs):
        at = n + len(extra)
        body(refs[:n], refs[at], refs[at + 1])
        token = refs[-1]
        token[...] = jnp.zeros_like(token)

    outs = pl.pallas_call(
        wrapped, name=name,
        in_specs=[HBM] * n + [ANY] * len(extra),
        out_shape=(pltpu.SemaphoreType.DMA((n_sems,)), pltpu.SemaphoreType.DMA((n_sems,)),
                   *[pltpu.HBM(a.shape, a.dtype) for a in operands], jax.ShapeDtypeStruct((8, 128), F32)),
        out_specs=(SEM, SEM, *[HBM] * n, pl.BlockSpec(memory_space=pltpu.VMEM)),
        input_output_aliases={i: 2 + i for i in range(n)},
        compiler_params=pltpu.CompilerParams(has_side_effects=SPLIT_COPY, collective_id=collective_id),
    )(*_hbm(*operands), *extra)
    return outs[0], outs[1], list(outs[2:2 + n]), outs[-1]


def _split_wait(body, name, send_sem, recv_sem, operands, after):
    n = len(operands)

    def wrapped(*refs):
        body(refs[:n], refs[n], refs[n + 1])

    outs = pl.pallas_call(
        wrapped, name=name,
        in_specs=[HBM] * n + [SEM, SEM, ANY],
        out_shape=tuple(pltpu.HBM(a.shape, a.dtype) for a in operands),
        out_specs=tuple([HBM] * n),
        input_output_aliases={i: i for i in range(n)},
        compiler_params=pltpu.CompilerParams(has_side_effects=SPLIT_COPY),
    )(*operands, send_sem, recv_sem, after)
    return list(outs)


def _gather_copies(lands, in_chip_order, send_sem, recv_sem):
    peers = _place()[3]
    copies = []
    for a, (land, plain) in enumerate(zip(lands, in_chip_order)):
        own, sent_to, _, _ = _zone_slots(plain)
        copies += [pltpu.make_async_remote_copy(
            src_ref=land.at[own], dst_ref=land.at[sent_to(r + 1)],
            send_sem=send_sem.at[a * 3 + r], recv_sem=recv_sem.at[a * 3 + r],
            device_id=peers[r], device_id_type=MESH) for r in range(3)]
    return copies


def _gather_start(lands, in_chip_order, name, collective_id, after):
    def body(refs, send_sem, recv_sem):
        _handshake(_place()[3])
        for cp in _gather_copies(refs, in_chip_order, send_sem, recv_sem):
            cp.start()

    return _split_start(body, name, collective_id, list(lands), 3 * len(lands), after)


def _gather_wait(send_sem, recv_sem, operands, in_chip_order, after, name):
    def body(refs, send_sem, recv_sem):
        for cp in _gather_copies(refs, in_chip_order, send_sem, recv_sem):
            cp.wait_send()
            cp.wait_recv()

    return _split_wait(body, name, send_sem, recv_sem, operands, after)


def _gather_finish(lands, in_chip_order, with_ici, name):
    n = len(lands)

    def body(*refs):
        land = refs[n:2 * n]
        send_ici, recv_ici, send_d2d, recv_d2d = refs[2 * n:]
        x, y, c, _ = _place()
        ici = _gather_copies(land, in_chip_order, send_ici, recv_ici) if with_ici else []
        for cp in ici:
            cp.start()
        passed = []
        for a in range(n):
            _, _, received, kept = _zone_slots(in_chip_order[a])
            passed += [pltpu.make_async_remote_copy(
                src_ref=land[a].at[received(r + 1)], dst_ref=land[a].at[kept(r + 1)],
                send_sem=send_d2d.at[a * 3 + r], recv_sem=recv_d2d.at[a * 3 + r],
                device_id=(x, y, 1 - c), device_id_type=MESH) for r in range(3)]
        for k, cp in enumerate(passed):
            if with_ici:
                ici[k].wait_recv()
            cp.start()
        for cp in passed:
            cp.wait_recv()
        for cp in ici:
            cp.wait_send()
        for cp in passed:
            cp.wait_send()

    outs = pl.pallas_call(
        body, name=name,
        in_specs=[ANY] * n, out_specs=[ANY] * n,
        out_shape=[_out(l.shape, l.dtype) for l in lands],
        input_output_aliases={a: a for a in range(n)},
        scratch_shapes=[pltpu.SemaphoreType.DMA((3 * n,))] * 4,
    )(*lands)
    return list(outs)


def _slabs(land):
    return land.reshape(N_CHIPS, 2 * land.shape[2], land.shape[3])


def _pair_swap(grads, permuted, name):
    n = len(grads)

    def body(*refs):
        src, dst = refs[:n], refs[n:2 * n]
        send_sem, recv_sem = refs[2 * n:]
        x, y, c, _ = _place()
        copies = [pltpu.make_async_remote_copy(
            src_ref=src[a].at[:, 1] if permuted[a] else src[a].at[:, 1 - c], dst_ref=dst[a],
            send_sem=send_sem.at[a], recv_sem=recv_sem.at[a],
            device_id=(x, y, 1 - c), device_id_type=MESH) for a in range(n)]
        for cp in copies:
            cp.start()
        for cp in copies:
            cp.wait()

    return pl.pallas_call(
        body, name=name,
        in_specs=[ANY] * n, out_specs=[ANY] * n,
        out_shape=[_out((N_CHIPS,) + g.shape[2:], g.dtype) for g in grads],
        scratch_shapes=[pltpu.SemaphoreType.DMA((n,))] * 2,
    )(*grads)


def _swap_copies(refs, permuted, send_sem, recv_sem):
    n = len(refs) // 2
    x, y, c, _ = _place()
    return [pltpu.make_async_remote_copy(
        src_ref=refs[a].at[:, 1] if permuted[a] else refs[a].at[:, 1 - c], dst_ref=refs[n + a],
        send_sem=send_sem.at[a], recv_sem=recv_sem.at[a],
        device_id=(x, y, 1 - c), device_id_type=MESH) for a in range(n)]


def _pair_swap_start(grads, permuted, name, collective_id):
    def body(refs, send_sem, recv_sem):
        x, y, c, _ = _place()
        _handshake([(x, y, 1 - c)])
        for cp in _swap_copies(refs, permuted, send_sem, recv_sem):
            cp.start()

    lands = [lax.empty((N_CHIPS,) + g.shape[2:], g.dtype) for g in grads]
    return _split_start(body, name, collective_id, list(grads) + lands, len(grads))


def _pair_swap_wait(send_sem, recv_sem, operands, permuted, after, name):
    def body(refs, send_sem, recv_sem):
        for cp in _swap_copies(refs, permuted, send_sem, recv_sem):
            cp.wait_send()
            cp.wait_recv()

    return _split_wait(body, name, send_sem, recv_sem, operands, after)


def _scatter_copies(refs, permuted, send_sem, recv_sem):
    n = len(refs) // 2
    x, y, _, peers = _place()
    me = 2 * x + y
    return [pltpu.make_async_remote_copy(
        src_ref=refs[a].at[r + 1] if permuted[a] else refs[a].at[me ^ (r + 1)], dst_ref=refs[n + a].at[r],
        send_sem=send_sem.at[a * 3 + r], recv_sem=recv_sem.at[a * 3 + r],
        device_id=peers[r], device_id_type=MESH) for a in range(n) for r in range(3)]


def _scatter_start(partials, permuted, name, collective_id):
    def body(refs, send_sem, recv_sem):
        _handshake(_place()[3])
        for cp in _scatter_copies(refs, permuted, send_sem, recv_sem):
            cp.start()

    lands = [lax.empty((N_CHIPS - 1,) + p.shape[1:], p.dtype) for p in partials]
    return _split_start(body, name, collective_id, list(partials) + lands, 3 * len(partials))


def _scatter_wait(send_sem, recv_sem, operands, permuted, after, name):
    def body(refs, send_sem, recv_sem):
        for cp in _scatter_copies(refs, permuted, send_sem, recv_sem):
            cp.wait_send()
            cp.wait_recv()

    return _split_wait(body, name, send_sem, recv_sem, operands, after)


def _pair_join(halves, name):
    n = len(halves)

    def body(*refs):
        src, dst = refs[:n], refs[n:2 * n]
        send_sem, recv_sem = refs[2 * n:]
        x, y, c, _ = _place()
        copies = [pltpu.make_async_remote_copy(
            src_ref=src[a], dst_ref=dst[a], send_sem=send_sem.at[a], recv_sem=recv_sem.at[a],
            device_id=(x, y, 1 - c), device_id_type=MESH) for a in range(n)]
        for cp in copies:
            cp.start()
        for cp in copies:
            cp.wait()

    return pl.pallas_call(
        body, name=name,
        in_specs=[ANY] * n, out_specs=[ANY] * n,
        out_shape=[_out(h.shape, F32) for h in halves],
        scratch_shapes=[pltpu.SemaphoreType.DMA((n,))] * 2,
    )(*halves)


def _all_sum_small(v, after, name):
    R, C = v.shape
    n_dev = 8

    def body(v_ref, after_ref, o_ref, buf, send_sem, recv_sem):
        x, y, c, _ = _place()
        me = 4 * x + 2 * y + c
        buf[me] = v_ref[...]
        copies = []
        for k in range(1, n_dev):
            peer = (x ^ (k >> 2), y ^ ((k >> 1) & 1), c ^ (k & 1))
            copies.append(pltpu.make_async_remote_copy(
                src_ref=v_ref, dst_ref=buf.at[me], send_sem=send_sem.at[k - 1], recv_sem=recv_sem.at[k - 1],
                device_id=peer, device_id_type=MESH))
        for cp in copies:
            cp.start()
        for cp in copies:
            cp.wait()
        acc = buf[0]
        for m in range(1, n_dev):
            acc = acc + buf[m]
        o_ref[...] = acc

    return pl.pallas_call(
        body, name=name,
        in_specs=[pl.BlockSpec(memory_space=pltpu.VMEM), ANY], out_specs=pl.BlockSpec(memory_space=pltpu.VMEM),
        out_shape=jax.ShapeDtypeStruct((R, C), F32),
        scratch_shapes=[pltpu.VMEM((n_dev, R, C), F32), pltpu.SemaphoreType.DMA((n_dev - 1,)),
                        pltpu.SemaphoreType.DMA((n_dev - 1,))],
    )(v, after)


class _WholeWeights:
    def __init__(self, w):
        self.w = w

    def weights(self, group, after=None):
        return self.w, None

    def grads_ready(self, group, gw):
        return None

    def grads_sent(self, group, after):
        return None


def _local_step(x, p, target, gains, rel_bias, hooks):
    T, D = x.shape
    S = N_CHIPS

    tied = lambda gain, token: gain if token is None else gain + token[0, 0]
    w, token = hooks.weights("first")
    w = dict(w)
    xn1, g1, u1, a1 = _ffn_up(x, tied(gains["ffn1_pre"], token), w["ffn1_gate"], w["ffn1_up"], "ffn1_up")
    w.update(hooks.weights("down", a1)[0])
    h1, f1 = _ffn_down(x, a1, gains["ffn1_post"], w["ffn1_down"], "ffn1_down")
    more, token = hooks.weights("in", h1)
    w.update(more)
    qkv, un = _norm_proj(h1, tied(gains["mix_pre"], token), w["in"], "qkv_proj")
    bias = _ch_group_bias(_bias_table(rel_bias, "bias_table").transpose(1, 0, 2))
    o_a = _sb_fwd(qkv, "sb_fwd")
    o_b = _ch_fwd(qkv, bias, "ch_fwd")
    w.update(hooks.weights("rest", o_b)[0])
    w_out = w["out"].reshape(D, D)
    h2, mixed, mo = _mix_out_fwd(h1, o_a, o_b, gains["out_sb"], gains["out_ch"], w_out, gains["mix_post"],
                                 "mix_out_fwd")
    h3, xn2, g2, u2, a2, f2 = _ffn_fwd(h2, gains["ffn2_pre"], gains["ffn2_post"], w["ffn2_gate"], w["ffn2_up"],
                                       w["ffn2_down"], "ffn2_fwd")
    w_ple_proj = w["ple_proj"].transpose(1, 0, 2).reshape(p.shape[1], D)
    w_ple_gate = w["ple_gate"].reshape(D, D)

    loss, dh3, dproj, dgate, dg_ple = _ple_loss(h3, p, target, w_ple_proj, w_ple_gate, gains["ple_post"], "ple_loss")
    gw, gg = {}, {"ple_post": dg_ple}
    gw["ple_proj"] = _mm_tn(p[None], dproj, p.shape[1], "dw_ple_proj")
    row_sharded = lambda pair: tuple(o.reshape(S, D // S, D) for o in pair)
    gw["ple_gate"] = row_sharded(_mm_tn(h3[None], dgate[None], 512, "dw_ple_gate"))

    def ffn_bwd(tag, dh, x_in, xn, g_act, u_act, a_act, f, group):
        dgp, dup, df, gg[tag + "_post"] = _ffn_bwd_act(dh, f, gains[tag + "_post"], w[tag + "_down"], g_act, u_act,
                                                       tag + "_bwd_act")
        gw[tag + "_gate"] = _mm_tn(dgp, xn[None], dgp.shape[2], "dw_" + tag + "_gate")
        gw[tag + "_up"] = _mm_tn(dup, xn[None], dup.shape[2], "dw_" + tag + "_up")
        gw[tag + "_down"] = _mm_tn(a_act, df[None], a_act.shape[2], "dw_" + tag + "_down")
        g_pre = gains[tag + "_pre"]
        if group is not None:
            token = hooks.grads_ready(group, gw)
            g_pre = g_pre if token is None else g_pre + token[0, 0]
        dx, gg[tag + "_pre"] = _proj_bwd([dgp, dup], [w[tag + "_gate"], w[tag + "_up"]], x_in, g_pre, dh,
                                         tag + "_bwd_in")
        return dx

    dh2 = ffn_bwd("ffn2", dh3, h2, xn2, g2, u2, a2, f2, None)
    dmo, do_a, do_b, gg["mix_post"], gg["out_sb"], gg["out_ch"] = _mix_out_bwd(
        dh2, mo, gains["mix_post"], w_out, o_a, o_b, gains["out_sb"], gains["out_ch"], "mix_out_bwd")
    gw["out"] = row_sharded(_mm_tn(mixed[None], dmo[None], 512, "dw_out"))
    token = hooks.grads_ready("early", gw)
    dq_a, dk_a, dv_a = _sb_bwd(qkv, do_a, o_a, do_a if token is None else token, "sb_bwd")
    token = hooks.grads_sent("early", dq_a)
    dq_b, dk_b, dv_b, dbias = _ch_bwd(qkv, bias, do_b, do_b if token is None else token, "ch_bwd")
    g_rel = _bias_grad(_ch_fold_bias_grad(dbias).transpose(1, 0, 2), "bias_grad")
    dqkv = [dq_a, dk_a, dv_a, dq_b, dk_b, dv_b]
    gw["in"] = _dw_in(un, dqkv, w["in"].shape[2], 512, "dw_in")
    dh1, gg["mix_pre"] = _qkv_bwd_in(dqkv, w["in"], h1, gains["mix_pre"], dh2, "qkv_bwd_in")
    dx = ffn_bwd("ffn1", dh1, x, xn1, g1, u1, a1, f1, "late")
    return loss, dx, gw, gg, g_rel


BIG = ["ffn1_gate", "ffn1_up", "ffn1_down", "in", "out", "ffn2_gate", "ffn2_up", "ffn2_down", "ple_proj", "ple_gate"]
GAINS = ["ffn1_pre", "ffn1_post", "mix_pre", "mix_post", "out_sb", "out_ch", "ffn2_pre", "ffn2_post", "ple_post"]
TRANSPOSED = ("w_ffn1_gate", "w_ffn1_up", "w_ffn2_gate", "w_ffn2_up")
PERMUTED = ("ffn1_gate", "ffn1_up", "ffn1_down", "ffn2_gate", "ffn2_up", "ffn2_down")
W_GROUPS = {"first": ["ffn1_gate", "ffn1_up"], "down": ["ffn1_down"], "in": ["in"],
            "rest": ["out", "ffn2_gate", "ffn2_up", "ffn2_down", "ple_proj", "ple_gate"]}
G_GROUPS = {"early": ["ple_proj", "ple_gate", "ffn2_gate", "ffn2_up", "ffn2_down", "out"],
            "late": ["in", "ffn1_gate", "ffn1_up", "ffn1_down"]}
ORDER = ["g_ffn1_pre", "g_ffn1_post", "w_ffn1_gate", "w_ffn1_up", "w_ffn1_down", "g_mix_pre", "g_mix_post", "w_in",
         "g_out_sb", "g_out_ch", "rel_bias", "w_out", "g_ffn2_pre", "g_ffn2_post", "w_ffn2_gate", "w_ffn2_up",
         "w_ffn2_down", "w_ple_proj", "w_ple_gate", "g_ple_post"]


def kernel(x, p, g_ffn1_pre, g_ffn1_post, w_ffn1_gate, w_ffn1_up, w_ffn1_down, g_mix_pre, g_mix_post, w_in, g_out_sb, g_out_ch, rel_bias, w_out, g_ffn2_pre, g_ffn2_post, w_ffn2_gate, w_ffn2_up, w_ffn2_down, w_ple_proj, w_ple_gate, g_ple_post, loss_target, m_g_ffn1_pre, m_g_ffn1_post, m_w_ffn1_gate, m_w_ffn1_up, m_w_ffn1_down, m_g_mix_pre, m_g_mix_post, m_w_in, m_g_out_sb, m_g_out_ch, m_rel_bias, m_w_out, m_g_ffn2_pre, m_g_ffn2_post, m_w_ffn2_gate, m_w_ffn2_up, m_w_ffn2_down, m_w_ple_proj, m_w_ple_gate, m_g_ple_post, v_g_ffn1_pre, v_g_ffn1_post, v_w_ffn1_gate, v_w_ffn1_up, v_w_ffn1_down, v_g_mix_pre, v_g_mix_post, v_w_in, v_g_out_sb, v_g_out_ch, v_rel_bias, v_w_out, v_g_ffn2_pre, v_g_ffn2_post, v_w_ffn2_gate, v_w_ffn2_up, v_w_ffn2_down, v_w_ple_proj, v_w_ple_gate, v_g_ple_post):
    args = dict(locals())
    take = lambda a, n: a[0].T if n in TRANSPOSED else a[0]
    wts = {n: take(args[n], n) for n in ORDER}
    ms = {n: take(args["m_" + n], n) for n in ORDER}
    vs = {n: take(args["v_" + n], n) for n in ORDER}
    gains = {n: wts["g_" + n][None] for n in GAINS}

    c_idx = lax.axis_index("c").astype(jnp.int32).reshape(1)
    me_idx = (2 * lax.axis_index("x") + lax.axis_index("y")).astype(jnp.int32).reshape(1)
    south = lax.axis_index("c") == 0

    plain = lambda names: [n not in PERMUTED for n in names]
    lands = dict(zip(BIG, _cast_into_own_slot(me_idx, c_idx, [wts["w_" + n] for n in BIG], plain(BIG), "cast_weights")))

    class Overlapped:
        def __init__(self):
            self.started = {}
            self.flying = {}

        def start(self, group, collective_id, after):
            names = W_GROUPS[group]
            self.flying[group] = _gather_start([lands[n] for n in names], plain(names), "gather_%s_start" % group,
                                               collective_id, after)
            return self.flying[group][3]

        def weights(self, group, after=None):
            names = W_GROUPS[group]
            token = None
            if group == "first":
                zones = _gather_finish([lands[n] for n in names], plain(names), True, "gather_first")
                token = self.start("rest", 4, self.start("in", 1, self.start("down", 6, zones[0])))
            else:
                send_sem, recv_sem, zones, _ = self.flying[group]
                zones = _gather_wait(send_sem, recv_sem, zones, plain(names), after, "gather_%s_wait" % group)
                zones = _gather_finish(zones, plain(names), False, "gather_%s_finish" % group)
            return {n: _slabs(z) for n, z in zip(names, zones)}, token

        def grads_ready(self, group, gw):
            names = G_GROUPS[group]
            perm = [n in PERMUTED for n in names]
            halved = lambda g: g.reshape(N_CHIPS, 2, g.shape[1] // 2, g.shape[2])
            mine = [halved(gw[n][0]) for n in names]
            narrow = [halved(gw[n][1]) for n in names]
            if group == "late":
                return self.scatter(group, names, perm, mine, _pair_swap(narrow, perm, "grad_pair_swap_late"))
            self.swapping = names, perm, mine, _pair_swap_start(narrow, perm, "grad_pair_swap_start_early", 5)
            return self.swapping[3][3]

        def grads_sent(self, group, after):
            names, perm, mine, (send_sem, recv_sem, operands, _) = self.swapping
            operands = _pair_swap_wait(send_sem, recv_sem, operands, perm, after, "grad_pair_swap_wait_early")
            return self.scatter(group, names, perm, mine, operands[len(names):])

        def scatter(self, group, names, perm, mine, got):
            partial = _pair_add(c_idx, mine, got, perm, "grad_pair_add_" + group)
            send_sem, recv_sem, operands, token = _scatter_start(partial, perm, "grad_scatter_start_" + group,
                                                                 {"early": 2, "late": 3}[group])
            self.started[group] = names, perm, send_sem, recv_sem, operands, token
            return token

    def reduce_finish(state, after, tag):
        names, perm, send_sem, recv_sem, operands, _ = state
        operands = _scatter_wait(send_sem, recv_sem, operands, perm, after, "grad_scatter_wait_" + tag)
        n = len(names)
        own = _chip_add(me_idx, operands[:n], operands[n:], perm, "grad_chip_add_" + tag)
        return own, _pair_join(own, "grad_pair_join_" + tag)

    hooks = Overlapped()
    loss, dx, gw, gg, g_rel = _local_step(x[0], p[0, 0], loss_target[0], gains, wts["rel_bias"], hooks)

    grads, delta, new_m, new_v = {}, {}, {}, {}

    def finish(group, after):
        own, other = reduce_finish(hooks.started[group], after, group)
        names = ["w_" + n for n in G_GROUPS[group]]
        g, d, m, v = _adamw_halves(c_idx, [wts[n] for n in names], own, other, [ms[n] for n in names],
                                   [vs[n] for n in names], "adamw_" + group)
        for n, gg_, dd, mm, vv in zip(names, g, d, m, v):
            grads[n], delta[n], new_m[n], new_v[n] = gg_, dd, mm, vv
        return d[0]

    finish("late", finish("early", dx))

    pieces = [gg[n].reshape(-1, 128) for n in GAINS] + [jnp.pad(g_rel, ((0, 0), (0, N_REL_PAD - N_REL))).reshape(-1, 128)]
    summed = _all_sum_small(jnp.concatenate(pieces + [loss], axis=0), delta["w_in"], "small_grad_sum")
    at = 0
    for n, piece in zip(GAINS, pieces[:-1]):
        grads["g_" + n] = summed[at:at + piece.shape[0]].reshape(1, -1)[0]
        at += piece.shape[0]
    grads["rel_bias"] = summed[at:at + pieces[-1].shape[0]].reshape(N_HEADS, N_REL_PAD)[:, :N_REL]
    loss = summed[at + pieces[-1].shape[0], 0]

    small = ["g_" + n for n in GAINS] + ["rel_bias"]
    as_rows = lambda a: (a.reshape(-1, 128) if a.size % 128 == 0 else jnp.pad(a, ((0, 0), (0, N_REL_PAD - N_REL))).reshape(-1, 128))
    d, m, v = _adamw([as_rows(wts[n]) for n in small], [as_rows(grads[n]) for n in small],
                     [as_rows(ms[n]) for n in small], [as_rows(vs[n]) for n in small], 1, "adamw_small")
    for n, dd, mm, vv in zip(small, d, m, v):
        back = (lambda a: a.reshape(N_HEADS, N_REL_PAD)[:, :N_REL]) if n == "rel_bias" else (lambda a: a.reshape(-1))
        delta[n], new_m[n], new_v[n] = back(dd), back(mm), back(vv)

    outs = [loss, dx[None]]
    for table in (grads, delta, new_m, new_v):
        outs += [(table[n].T if n in TRANSPOSED else table[n])[None] for n in ORDER]
    return tuple(outs)
```

```python
import jax
import jax.numpy as jnp
from jax import lax
from jax.experimental import pallas as pl
from jax.experimental.pallas import tpu as pltpu

F32 = jnp.float32
BF16 = jnp.bfloat16
EPS = 1e-6
N_CHIPS = 4
HEAD_DIM = 64
N_HEADS = 8
CHUNK = 64
LOOKBACK = 8
BAND = (LOOKBACK + 1) * CHUNK
PAD = LOOKBACK * CHUNK
REL_CLIP = 128
N_REL = 2 * REL_CLIP + 1
N_REL_PAD = 384
SB_BLOCK = 256
PAIR = 2 * HEAD_DIM
SB_PAIRS = 2
SB_FWD_PAIRS = 4
ATT_SCALE = HEAD_DIM ** -0.5
NEG_INF = -1e30
ROW_BLOCK = 512
WIDE_ROW_BLOCK = 1024
VMEM_LIMIT_WIDE = 56 * 1024 * 1024
VMEM_LIMIT = 48 * 1024 * 1024
MESH = pl.DeviceIdType.MESH

ADAM_LR = 0.001
ADAM_B1 = 0.9
ADAM_B2 = 0.999
ADAM_EPS = 1e-08
ADAM_WD = 0.01
ADAM_STEP = 10

NT = (((1,), (1,)), ((), ()))
TN = (((0,), (0,)), ((), ()))


def _params(n_grid, vmem=None):
    return pltpu.CompilerParams(dimension_semantics=("arbitrary",) * n_grid, vmem_limit_bytes=vmem)


def _hbm(*arrays):
    return [pltpu.with_memory_space_constraint(a, pltpu.HBM) for a in arrays]


def _out(shape, dtype):
    return pltpu.HBM(shape, dtype)


def _dot(a, b, dims=None):
    if dims is None:
        return jnp.dot(a, b, preferred_element_type=F32)
    return lax.dot_general(a, b, dims, preferred_element_type=F32)


def _sigmoid(x):
    return 1.0 / (1.0 + jnp.exp(-x))


def _rms_fwd(x, g):
    r = lax.rsqrt(jnp.mean(x * x, axis=-1, keepdims=True) + EPS)
    return x * r * g


def _rms_bwd(x, g, dy):
    r = lax.rsqrt(jnp.mean(x * x, axis=-1, keepdims=True) + EPS)
    xh = x * r
    dg = jnp.sum(dy * xh, axis=0, keepdims=True)
    t = dy * g
    dx = r * (t - xh * jnp.mean(t * xh, axis=-1, keepdims=True))
    return dx, dg


def _accumulate(ref, val, first):
    @pl.when(first)
    def _():
        ref[...] = val

    @pl.when(jnp.logical_not(first))
    def _():
        ref[...] += val


def _split2(x):
    hi = x.astype(BF16)
    lo = (x - hi.astype(F32)).astype(BF16)
    return hi, lo


def _ffn_fwd(x, g_pre, g_post, wg, wu, wd, name):
    T, D = x.shape
    S, FS, _ = wg.shape
    tm = min(WIDE_ROW_BLOCK, T)

    def body(x_ref, gpre_ref, gpost_ref, wg_ref, wu_ref, wd_ref,
             h_ref, xn_ref, g_ref, u_ref, a_ref, f_ref):
        k = pl.program_id(1)

        @pl.when(k == 0)
        def _():
            xn_ref[...] = _rms_fwd(x_ref[...], gpre_ref[...]).astype(BF16)

        xn = xn_ref[...]
        g = _dot(xn, wg_ref[0], NT)
        u = _dot(xn, wu_ref[0], NT)
        g_ref[0] = g
        u_ref[0] = u
        a = (g * _sigmoid(g) * u).astype(BF16)
        a_ref[0] = a
        _accumulate(f_ref, _dot(a, wd_ref[0]), k == 0)

        @pl.when(k == S - 1)
        def _():
            h_ref[...] = x_ref[...] + 0.5 * _rms_fwd(f_ref[...], gpost_ref[...])

    row = pl.BlockSpec((tm, D), lambda i, k: (i, 0))
    vec = pl.BlockSpec((1, D), lambda i, k: (0, 0))
    act = pl.BlockSpec((1, tm, FS), lambda i, k: (k, i, 0))
    return pl.pallas_call(
        body, name=name, grid=(T // tm, S),
        in_specs=[row, vec, vec] + [pl.BlockSpec((1, FS, D), lambda i, k: (k, 0, 0))] * 3,
        out_specs=[row, row, act, act, act, row],
        out_shape=[_out((T, D), F32), _out((T, D), BF16),
                   _out((S, T, FS), F32), _out((S, T, FS), F32),
                   _out((S, T, FS), BF16), _out((T, D), F32)],
        compiler_params=_params(2, VMEM_LIMIT_WIDE),
    )(*_hbm(x, g_pre, g_post, wg, wu, wd))


def _ffn_up(x, g_pre, wg, wu, name):
    T, D = x.shape
    S, FS, _ = wg.shape
    tm = min(WIDE_ROW_BLOCK, T)

    def body(x_ref, gpre_ref, wg_ref, wu_ref, xn_ref, g_ref, u_ref, a_ref):
        @pl.when(pl.program_id(1) == 0)
        def _():
            xn_ref[...] = _rms_fwd(x_ref[...], gpre_ref[...]).astype(BF16)

        xn = xn_ref[...]
        g = _dot(xn, wg_ref[0], NT)
        u = _dot(xn, wu_ref[0], NT)
        g_ref[0] = g
        u_ref[0] = u
        a_ref[0] = (g * _sigmoid(g) * u).astype(BF16)

    row = pl.BlockSpec((tm, D), lambda i, k: (i, 0))
    act = pl.BlockSpec((1, tm, FS), lambda i, k: (k, i, 0))
    return pl.pallas_call(
        body, name=name, grid=(T // tm, S),
        in_specs=[row, pl.BlockSpec((1, D), lambda i, k: (0, 0))] + [pl.BlockSpec((1, FS, D), lambda i, k: (k, 0, 0))] * 2,
        out_specs=[row, act, act, act],
        out_shape=[_out((T, D), BF16), _out((S, T, FS), F32), _out((S, T, FS), F32), _out((S, T, FS), BF16)],
        compiler_params=_params(2, VMEM_LIMIT_WIDE),
    )(*_hbm(x, g_pre, wg, wu))


def _ffn_down(x, a, g_post, wd, name):
    T, D = x.shape
    S, FS, _ = wd.shape
    tm = min(WIDE_ROW_BLOCK, T)

    def body(x_ref, a_ref, gpost_ref, wd_ref, h_ref, f_ref):
        k = pl.program_id(1)
        _accumulate(f_ref, _dot(a_ref[0], wd_ref[0]), k == 0)

        @pl.when(k == S - 1)
        def _():
            h_ref[...] = x_ref[...] + 0.5 * _rms_fwd(f_ref[...], gpost_ref[...])

    row = pl.BlockSpec((tm, D), lambda i, k: (i, 0))
    return pl.pallas_call(
        body, name=name, grid=(T // tm, S),
        in_specs=[row, pl.BlockSpec((1, tm, FS), lambda i, k: (k, i, 0)), pl.BlockSpec((1, D), lambda i, k: (0, 0)),
                  pl.BlockSpec((1, FS, D), lambda i, k: (k, 0, 0))],
        out_specs=[row, row],
        out_shape=[_out((T, D), F32), _out((T, D), F32)],
        compiler_params=_params(2, VMEM_LIMIT_WIDE),
    )(*_hbm(x, a, g_post, wd))


def _ffn_bwd_act(dh, f, g_post, wd, g_act, u_act, name):
    T, D = dh.shape
    S, FS, _ = wd.shape
    tm = min(WIDE_ROW_BLOCK, T)

    def body(dh_ref, f_ref, gpost_ref, wd_ref, g_ref, u_ref, dgp_ref, dup_ref, df_ref, dgain_ref, df_s):
        i, k = pl.program_id(0), pl.program_id(1)

        @pl.when(k == 0)
        def _():
            df, dgain = _rms_bwd(f_ref[...], gpost_ref[...], 0.5 * dh_ref[...])
            df_s[...] = df.astype(BF16)
            df_ref[...] = df_s[...]
            _accumulate(dgain_ref, dgain, i == 0)

        da = _dot(df_s[...], wd_ref[0], NT)
        g = g_ref[0]
        s = _sigmoid(g)
        dup_ref[0] = (da * (g * s)).astype(BF16)
        dgp_ref[0] = (da * u_ref[0] * (s * (1.0 + g * (1.0 - s)))).astype(BF16)

    row = pl.BlockSpec((tm, D), lambda i, k: (i, 0))
    vec = pl.BlockSpec((1, D), lambda i, k: (0, 0))
    act = pl.BlockSpec((1, tm, FS), lambda i, k: (k, i, 0))
    return pl.pallas_call(
        body, name=name, grid=(T // tm, S),
        in_specs=[row, row, vec, pl.BlockSpec((1, FS, D), lambda i, k: (k, 0, 0)), act, act],
        out_specs=[act, act, row, vec],
        out_shape=[_out((S, T, FS), BF16), _out((S, T, FS), BF16),
                   _out((T, D), BF16), _out((1, D), F32)],
        scratch_shapes=[pltpu.VMEM((tm, D), BF16)],
        compiler_params=_params(2, VMEM_LIMIT_WIDE),
    )(*_hbm(dh, f, g_post, wd, g_act, u_act))


def _proj_bwd(dys, ws, x, g_pre, dh, name):
    T, D = x.shape
    n = len(dys)
    S, N, _ = ws[0].shape
    tm = min(WIDE_ROW_BLOCK, T)

    def body(*refs):
        dy_refs, w_refs = refs[:n], refs[n:2 * n]
        x_ref, gpre_ref, dh_ref, dx_ref, dgain_ref, acc_s = refs[2 * n:]
        i, k = pl.program_id(0), pl.program_id(1)
        part = None
        for dy_ref, w_ref in zip(dy_refs, w_refs):
            term = _dot(dy_ref[0], w_ref[0])
            part = term if part is None else part + term
        _accumulate(acc_s, part, k == 0)

        @pl.when(k == S - 1)
        def _():
            dx, dgain = _rms_bwd(x_ref[...], gpre_ref[...], acc_s[...])
            dx_ref[...] = dh_ref[...] + dx
            _accumulate(dgain_ref, dgain, i == 0)

    row = pl.BlockSpec((tm, D), lambda i, k: (i, 0))
    vec = pl.BlockSpec((1, D), lambda i, k: (0, 0))
    return pl.pallas_call(
        body, name=name, grid=(T // tm, S),
        in_specs=[pl.BlockSpec((1, tm, N), lambda i, k: (k, i, 0))] * n
        + [pl.BlockSpec((1, N, D), lambda i, k: (k, 0, 0))] * n + [row, vec, row],
        out_specs=[row, vec],
        out_shape=[_out((T, D), F32), _out((1, D), F32)],
        scratch_shapes=[pltpu.VMEM((tm, D), F32)],
        compiler_params=_params(2, VMEM_LIMIT_WIDE),
    )(*_hbm(*dys, *ws, x, g_pre, dh))


def _mm_tn(a, b, bm, name):
    ga, T, M = a.shape
    gb, _, N = b.shape
    b_spec = pl.BlockSpec((1, T, N), (lambda g, m: (g, 0, 0)) if gb > 1 else (lambda g, m: (0, 0, 0)))
    G = max(ga, gb)

    def body(a_ref, b_ref, o_ref, narrow_ref):
        o_ref[0] = _dot(a_ref[0].astype(BF16), b_ref[0].astype(BF16), TN)
        narrow_ref[0] = o_ref[0].astype(BF16)

    out = pl.BlockSpec((1, bm, N), lambda g, m: (g, m, 0))
    return pl.pallas_call(
        body, name=name, grid=(G, M // bm),
        in_specs=[pl.BlockSpec((1, T, bm), (lambda g, m: (g, 0, m)) if ga > 1 else (lambda g, m: (0, 0, m))), b_spec],
        out_specs=[out, out],
        out_shape=[_out((G, M, N), F32), _out((G, M, N), BF16)],
        compiler_params=_params(2, VMEM_LIMIT),
    )(*_hbm(a, b))


QKV_PIECE = 256


def _qkv_shard(dy_refs, k, n_col):
    width = dy_refs[0].shape[1]
    parts = []
    for col in range(k * n_col, (k + 1) * n_col, QKV_PIECE):
        parts.append(dy_refs[col // width][:, col % width:col % width + QKV_PIECE])
    return jnp.concatenate(parts, axis=1)


def _qkv_bwd_in(dys, w, x, g_pre, dh, name):
    T, D = x.shape
    n = len(dys)
    S, _, N = w.shape
    tm = min(WIDE_ROW_BLOCK, T)

    def body(*refs):
        dy_refs = refs[:n]
        w_ref, x_ref, gpre_ref, dh_ref, dx_ref, dgain_ref, acc_s = refs[n:]
        i, k = pl.program_id(0), pl.program_id(1)
        for shard in range(S):
            @pl.when(k == shard)
            def _(shard=shard):
                part = _dot(_qkv_shard(dy_refs, shard, N), w_ref[0], NT)
                if shard == 0:
                    acc_s[...] = part
                else:
                    acc_s[...] += part

        @pl.when(k == S - 1)
        def _():
            dx, dgain = _rms_bwd(x_ref[...], gpre_ref[...], acc_s[...])
            dx_ref[...] = dh_ref[...] + dx
            _accumulate(dgain_ref, dgain, i == 0)

    row = pl.BlockSpec((tm, D), lambda i, k: (i, 0))
    vec = pl.BlockSpec((1, D), lambda i, k: (0, 0))
    return pl.pallas_call(
        body, name=name, grid=(T // tm, S),
        in_specs=[pl.BlockSpec((tm, dy.shape[1]), lambda i, k: (i, 0)) for dy in dys]
        + [pl.BlockSpec((1, D, N), lambda i, k: (k, 0, 0)), row, vec, row],
        out_specs=[row, vec],
        out_shape=[_out((T, D), F32), _out((1, D), F32)],
        scratch_shapes=[pltpu.VMEM((tm, D), F32)],
        compiler_params=_params(2, VMEM_LIMIT_WIDE),
    )(*_hbm(*dys, w, x, g_pre, dh))


def _dw_in(a, dys, n_col, bm, name):
    T, M = a.shape
    n = len(dys)
    S = n * dys[0].shape[1] // n_col

    def body(*refs):
        a_ref, dy_refs = refs[0], refs[1:1 + n]
        o_ref, narrow_ref = refs[1 + n:]
        k = pl.program_id(1)
        for shard in range(S):
            @pl.when(k == shard)
            def _(shard=shard):
                o_ref[0] = _dot(a_ref[...], _qkv_shard(dy_refs, shard, n_col), TN)
                narrow_ref[0] = o_ref[0].astype(BF16)

    out = pl.BlockSpec((1, bm, n_col), lambda m, k: (k, m, 0))
    return pl.pallas_call(
        body, name=name, grid=(M // bm, S),
        in_specs=[pl.BlockSpec((T, bm), lambda m, k: (0, m))]
        + [pl.BlockSpec((T, dy.shape[1]), lambda m, k: (0, 0)) for dy in dys],
        out_specs=[out, out],
        out_shape=[_out((S, M, n_col), F32), _out((S, M, n_col), BF16)],
        compiler_params=_params(2, VMEM_LIMIT_WIDE),
    )(*_hbm(a, *dys))


def _norm_proj(x, g_pre, w, name):
    T, D = x.shape
    S, _, N = w.shape
    tm = min(WIDE_ROW_BLOCK, T)

    def body(x_ref, g_ref, w_ref, o_ref, xn_ref, xn_s):
        @pl.when(pl.program_id(1) == 0)
        def _():
            xn_s[...] = _rms_fwd(x_ref[...], g_ref[...]).astype(BF16)
            xn_ref[...] = xn_s[...]

        o_ref[...] = _dot(xn_s[...], w_ref[0]).astype(BF16)

    row = pl.BlockSpec((tm, D), lambda i, k: (i, 0))
    return pl.pallas_call(
        body, name=name, grid=(T // tm, S),
        in_specs=[row, pl.BlockSpec((1, D), lambda i, k: (0, 0)), pl.BlockSpec((1, D, N), lambda i, k: (k, 0, 0))],
        out_specs=[pl.BlockSpec((tm, N), lambda i, k: (i, k)), row],
        out_shape=[_out((T, S * N), BF16), _out((T, D), BF16)],
        scratch_shapes=[pltpu.VMEM((tm, D), BF16)],
        compiler_params=_params(2, VMEM_LIMIT_WIDE),
    )(*_hbm(x, g_pre, w))


def _mix_out_fwd(h, o_a, o_b, g_sb, g_ch, w_out, g_post, name):
    T, D = h.shape
    W = g_sb.shape[1]
    tm = min(WIDE_ROW_BLOCK, T)

    def body(h_ref, oa_ref, ob_ref, gsb_ref, gch_ref, w_ref, gpost_ref, h2_ref, mixed_ref, mo_ref):
        mixed_ref[:, :W] = _rms_fwd(oa_ref[...], gsb_ref[...]).astype(BF16)
        mixed_ref[:, W:] = _rms_fwd(ob_ref[...], gch_ref[...]).astype(BF16)
        mo = _dot(mixed_ref[...], w_ref[...])
        mo_ref[...] = mo
        h2_ref[...] = h_ref[...] + _rms_fwd(mo, gpost_ref[...])

    row = pl.BlockSpec((tm, D), lambda i: (i, 0))
    part = pl.BlockSpec((tm, W), lambda i: (i, 0))
    half = pl.BlockSpec((1, W), lambda i: (0, 0))
    return pl.pallas_call(
        body, name=name, grid=(T // tm,),
        in_specs=[row, part, part, half, half, pl.BlockSpec((D, D), lambda i: (0, 0)), pl.BlockSpec((1, D), lambda i: (0, 0))],
        out_specs=[row, row, row],
        out_shape=[_out((T, D), F32), _out((T, D), BF16),
                   _out((T, D), F32)],
        compiler_params=_params(1, VMEM_LIMIT_WIDE),
    )(*_hbm(h, o_a, o_b, g_sb, g_ch, w_out, g_post))


def _mix_out_bwd(dh, mo, g_post, w_out, o_a, o_b, g_sb, g_ch, name):
    T, D = dh.shape
    W = g_sb.shape[1]
    tm = min(WIDE_ROW_BLOCK, T)

    def body(dh_ref, mo_ref, gpost_ref, w_ref, oa_ref, ob_ref, gsb_ref, gch_ref,
             dmo_ref, doa_ref, dob_ref, dgpost_ref, dgsb_ref, dgch_ref):
        first = pl.program_id(0) == 0
        dmo, dgpost = _rms_bwd(mo_ref[...], gpost_ref[...], dh_ref[...])
        dmo_ref[...] = dmo.astype(BF16)
        dmix = _dot(dmo_ref[...], w_ref[...], NT)
        doa_ref[...], dgsb = _rms_bwd(oa_ref[...], gsb_ref[...], dmix[:, :W])
        dob_ref[...], dgch = _rms_bwd(ob_ref[...], gch_ref[...], dmix[:, W:])
        _accumulate(dgpost_ref, dgpost, first)
        _accumulate(dgsb_ref, dgsb, first)
        _accumulate(dgch_ref, dgch, first)

    row = pl.BlockSpec((tm, D), lambda i: (i, 0))
    part = pl.BlockSpec((tm, W), lambda i: (i, 0))
    vec = pl.BlockSpec((1, D), lambda i: (0, 0))
    half = pl.BlockSpec((1, W), lambda i: (0, 0))
    return pl.pallas_call(
        body, name=name, grid=(T // tm,),
        in_specs=[row, row, vec, pl.BlockSpec((D, D), lambda i: (0, 0)), part, part, half, half],
        out_specs=[row, part, part, vec, half, half],
        out_shape=[_out((T, D), BF16), _out((T, W), F32),
                   _out((T, W), F32), _out((1, D), F32),
                   _out((1, W), F32), _out((1, W), F32)],
        compiler_params=_params(1, VMEM_LIMIT_WIDE),
    )(*_hbm(dh, mo, g_post, w_out, o_a, o_b, g_sb, g_ch))


def _ple_loss(h, p, target, w_proj, w_gate, g_post, name):
    T, D = h.shape
    P = p.shape[1]
    S = N_CHIPS
    C = D // S
    tm = min(ROW_BLOCK, T)

    def body(h_ref, p_ref, t_ref, wp_ref, wg_ref, g_ref, loss_ref, dh_ref, dproj_ref, dgate_ref, dgain_ref):
        first = pl.program_id(0) == 0
        h3 = h_ref[...]
        proj = _dot(p_ref[...].astype(BF16), wp_ref[...])
        s = _sigmoid(_dot(h3.astype(BF16), wg_ref[...]))
        e = proj * s
        diff = h3 + _rms_fwd(e, g_ref[...]) - t_ref[...]
        part = 0.5 * jnp.sum(jnp.mean(diff * diff, axis=-1, keepdims=True), axis=0, keepdims=True)
        _accumulate(loss_ref, jnp.broadcast_to(part, loss_ref.shape), first)
        dy = diff * (1.0 / D)
        de, dgain = _rms_bwd(e, g_ref[...], dy)
        _accumulate(dgain_ref, dgain, first)
        dproj = (de * s).astype(BF16)
        for j in range(S):
            dproj_ref[j] = dproj[:, j * C:(j + 1) * C]
        dgate_ref[...] = (de * proj * s * (1.0 - s)).astype(BF16)
        dh_ref[...] = dy + _dot(dgate_ref[...], wg_ref[...], NT)

    row = pl.BlockSpec((tm, D), lambda i: (i, 0))
    vec = pl.BlockSpec((1, D), lambda i: (0, 0))
    return pl.pallas_call(
        body, name=name, grid=(T // tm,),
        in_specs=[row, pl.BlockSpec((tm, P), lambda i: (i, 0)), row,
                  pl.BlockSpec((P, D), lambda i: (0, 0)), pl.BlockSpec((D, D), lambda i: (0, 0)), vec],
        out_specs=[pl.BlockSpec((8, 128), lambda i: (0, 0)), row,
                   pl.BlockSpec((S, tm, C), lambda i: (0, i, 0)), row, vec],
        out_shape=[_out((8, 128), F32), _out((T, D), F32),
                   _out((S, T, C), BF16), _out((T, D), BF16),
                   _out((1, D), F32)],
        compiler_params=_params(1, VMEM_LIMIT_WIDE),
    )(*_hbm(h, p, target, w_proj, w_gate, g_post))


def _sb_scores(q, kj, mask):
    z = _dot(q, kj, NT)
    sp = jnp.maximum(z, 0.0) + jnp.log(1.0 + jnp.exp(-jnp.abs(z)))
    return z, sp if mask is None else jnp.where(mask, sp, 0.0)


def _strict_causal():
    rows = lax.broadcasted_iota(jnp.int32, (SB_BLOCK, SB_BLOCK), 0)
    cols = lax.broadcasted_iota(jnp.int32, (SB_BLOCK, SB_BLOCK), 1)
    return cols < rows


def _tri(cmp):
    r = lax.broadcasted_iota(jnp.int32, (2 * SB_BLOCK, SB_BLOCK), 0) % SB_BLOCK
    c = lax.broadcasted_iota(jnp.int32, (2 * SB_BLOCK, SB_BLOCK), 1)
    return jnp.where(cmp(r, c), 1.0, 0.0).astype(BF16)


def _cum(x, tri):
    return _dot(jnp.concatenate(_split2(x), axis=1), tri)


def _pair_lanes():
    lane = lax.broadcasted_iota(jnp.int32, (1, PAIR), 1)
    return [lane < HEAD_DIM, lane >= HEAD_DIM]


def _only(lanes, x):
    return jnp.where(lanes, x, jnp.zeros_like(x))


def _sb_fwd(qkv, name):
    T = qkv.shape[0]
    B = SB_BLOCK
    W = SB_FWD_PAIRS * PAIR
    steps = N_HEADS // (2 * SB_FWD_PAIRS)
    heads = [(p, h) for p in range(SB_FWD_PAIRS) for h in range(2)]

    def body(q_ref, k_ref, v_ref, o_ref):
        i = pl.program_id(1)
        after = _tri(lambda r, c: r > c)
        lanes = _pair_lanes()
        cols = [slice(p * PAIR, (p + 1) * PAIR) for p in range(SB_FWD_PAIRS)]
        q = {(p, h): _only(lanes[h], q_ref[:, cols[p]] * ATT_SCALE) for p, h in heads}

        def tiles(j, carries, mask):
            at = pl.ds(pl.multiple_of(j * B, B), B)
            scores = [_sb_scores(q[ph], k_ref[at, cols[ph[0]]], mask) for ph in heads]
            laters = [_cum(sp, after) for _, sp in scores]
            out = []
            for ph, (z, sp), later, (run, acc) in zip(heads, scores, laters, carries):
                a = jnp.exp(z - sp - later - run)
                if mask is not None:
                    a = jnp.where(mask, a, 0.0)
                out.append((run + later[:, 0:1] + sp[:, 0:1],
                            acc + _dot(a.astype(BF16), _only(lanes[ph[1]], v_ref[at, cols[ph[0]]]))))
            return tuple(out)

        zero = (jnp.zeros((B, 1), F32), jnp.zeros((B, PAIR), F32))
        carries = tiles(i, (zero,) * len(heads), _strict_causal())
        carries = lax.fori_loop(0, i, lambda jj, cs: tiles(i - 1 - jj, cs, None), carries)
        for p in range(SB_FWD_PAIRS):
            o_ref[:, cols[p]] = carries[2 * p][1] + carries[2 * p + 1][1]

    blk = lambda off: pl.BlockSpec((B, W), lambda g, i: (i, g + off))
    full = lambda off: pl.BlockSpec((T, W), lambda g, i: (0, g + off))
    return pl.pallas_call(
        body, name=name, grid=(steps, T // B),
        in_specs=[blk(0), full(steps), full(2 * steps)],
        out_specs=blk(0),
        out_shape=_out((T, N_HEADS * HEAD_DIM), F32),
        compiler_params=_params(2, VMEM_LIMIT),
    )(*_hbm(qkv, qkv, qkv))


def _sb_bwd(qkv, do, o, after, name):
    T = qkv.shape[0]
    B = SB_BLOCK
    W = SB_PAIRS * PAIR
    steps = N_HEADS // (2 * SB_PAIRS)
    n_blocks = T // B
    heads = [(p, h) for p in range(SB_PAIRS) for h in range(2)]

    def body(q_ref, k_ref, v_ref, do_ref, o_ref, dq_ref, dk_ref, dv_ref, dk_s, dv_s):
        i = pl.program_id(1)

        @pl.when(i == 0)
        def _():
            dk_s[...] = jnp.zeros_like(dk_s)
            dv_s[...] = jnp.zeros_like(dv_s)

        after = _tri(lambda r, c: r > c)
        since = _tri(lambda r, c: r >= c)
        lanes = _pair_lanes()
        cols = [slice(p * PAIR, (p + 1) * PAIR) for p in range(SB_PAIRS)]
        q = {(p, h): _only(lanes[h], q_ref[:, cols[p]] * ATT_SCALE) for p, h in heads}
        do = {(p, h): _only(lanes[h], do_ref[:, cols[p]].astype(BF16)) for p, h in heads}
        total = {ph: jnp.sum(do[ph].astype(F32) * o_ref[:, cols[ph[0]]], axis=1, keepdims=True) for ph in heads}

        def tiles(j, carries, mask):
            at = pl.ds(pl.multiple_of(j * B, B), B)
            ks = [k_ref[at, c] for c in cols]
            vs = [v_ref[at, c] for c in cols]
            scores = [_sb_scores(q[ph], ks[ph[0]], mask) for ph in heads]
            laters = [_cum(sp, after) for _, sp in scores]
            das = [_dot(do[ph], vs[ph[0]], NT) for ph in heads]
            a_s, gs = [], []
            for (z, sp), later, da, carry in zip(scores, laters, das, carries):
                a = jnp.exp(z - sp - later - carry[0])
                if mask is not None:
                    a = jnp.where(mask, a, 0.0)
                a = a.astype(BF16)
                a_s.append(a)
                gs.append(a.astype(F32) * da)
            sinces = [_cum(g, since) for g in gs]
            dzs = []
            for ph, (_, sp), g, from_s, carry in zip(heads, scores, gs, sinces, carries):
                g_before = total[ph] - carry[1] - from_s
                fail = jnp.exp(-sp)
                dz = fail * (g + g_before) - g_before
                if mask is not None:
                    dz = jnp.where(mask, dz, 0.0)
                dzs.append(dz.astype(BF16))
            out = []
            for ph, (_, sp), a, dz, later, from_s, carry in zip(heads, scores, a_s, dzs, laters, sinces, carries):
                dk_s[at, cols[ph[0]]] += _dot(dz, q[ph], TN)
                dv_s[at, cols[ph[0]]] += _dot(a, do[ph], TN)
                out.append((carry[0] + later[:, 0:1] + sp[:, 0:1], carry[1] + from_s[:, 0:1],
                            carry[2] + _dot(dz, _only(lanes[ph[1]], ks[ph[0]]))))
            return tuple(out)

        col = jnp.zeros((B, 1), F32)
        zero = (col, col, jnp.zeros((B, PAIR), F32))
        carries = tiles(i, (zero,) * len(heads), _strict_causal())
        last = lax.fori_loop(0, i, lambda jj, cs: tiles(i - 1 - jj, cs, None), carries)
        for p in range(SB_PAIRS):
            dq_ref[:, cols[p]] = ((last[2 * p][2] + last[2 * p + 1][2]) * ATT_SCALE).astype(BF16)

        @pl.when(i == n_blocks - 1)
        def _():
            dk_ref[...] = dk_s[...].astype(BF16)
            dv_ref[...] = dv_s[...].astype(BF16)

    blk = lambda off: pl.BlockSpec((B, W), lambda g, i: (i, g + off))
    full = lambda off: pl.BlockSpec((T, W), lambda g, i: (0, g + off))
    out = _out((T, N_HEADS * HEAD_DIM), BF16)
    return pl.pallas_call(
        lambda after_ref, *refs: body(*refs), name=name, grid=(steps, n_blocks),
        in_specs=[ANY, blk(0), full(steps), full(2 * steps), blk(0), blk(0)],
        out_specs=[blk(0), full(0), full(0)],
        out_shape=[out, out, out],
        scratch_shapes=[pltpu.VMEM((T, W), F32)] * 2,
        compiler_params=_params(2, VMEM_LIMIT),
    )(after, *_hbm(qkv, qkv, qkv, do, o))


NEAR = BAND - PAD + REL_CLIP
FAR = BAND - NEAR
NEAR_REL = 2 * REL_CLIP
BIAS_ROWS = 8


def _rel_onehot(i, transposed):
    shape = (NEAR, NEAR_REL) if transposed else (NEAR_REL, NEAR)
    j = FAR + lax.broadcasted_iota(jnp.int32, shape, 0 if transposed else 1)
    r = lax.broadcasted_iota(jnp.int32, shape, 1 if transposed else 0)
    idx = jnp.clip(i + PAD - j, -REL_CLIP, REL_CLIP) + REL_CLIP
    return jnp.where(idx - 1 == r, 1.0, 0.0).astype(BF16)


def _bias_table(rel_bias, name):
    def body(near_ref, far_ref, o_ref):
        rb = near_ref[...]
        hi, lo = _split2(rb)
        lo2 = (rb - hi.astype(F32) - lo.astype(F32)).astype(BF16)
        far = jnp.broadcast_to(far_ref[...], (N_HEADS, FAR))
        for k in range(BIAS_ROWS):
            onehot = _rel_onehot(pl.program_id(0) * BIAS_ROWS + k, False)
            o_ref[k, :, :FAR] = far
            o_ref[k, :, FAR:] = _dot(hi, onehot) + _dot(lo, onehot) + _dot(lo2, onehot)

    return pl.pallas_call(
        body, name=name, grid=(CHUNK // BIAS_ROWS,),
        in_specs=[pl.BlockSpec((N_HEADS, NEAR_REL), lambda i: (0, 0)), pl.BlockSpec((N_HEADS, 1), lambda i: (0, 0))],
        out_specs=pl.BlockSpec((BIAS_ROWS, N_HEADS, BAND), lambda i: (i, 0, 0)),
        out_shape=_out((CHUNK, N_HEADS, BAND), F32),
        compiler_params=_params(1),
    )(*_hbm(rel_bias[:, 1:], rel_bias[:, N_REL - 1:]))


def _bias_grad(dbias_t, name):
    def body(d_ref, near_ref, far_ref):
        near, far = None, None
        for k in range(BIAS_ROWS):
            onehot = _rel_onehot(pl.program_id(0) * BIAS_ROWS + k, True)
            hi, lo = _split2(d_ref[k, :, FAR:])
            part = _dot(hi, onehot) + _dot(lo, onehot)
            rest = jnp.sum(d_ref[k, :, :FAR], axis=1, keepdims=True)
            near, far = (part, rest) if near is None else (near + part, far + rest)
        first = pl.program_id(0) == 0
        _accumulate(near_ref, near, first)
        _accumulate(far_ref, jnp.broadcast_to(far, far_ref.shape), first)

    near, far = pl.pallas_call(
        body, name=name, grid=(CHUNK // BIAS_ROWS,),
        in_specs=[pl.BlockSpec((BIAS_ROWS, N_HEADS, BAND), lambda i: (i, 0, 0))],
        out_specs=[pl.BlockSpec((N_HEADS, NEAR_REL), lambda i: (0, 0)), pl.BlockSpec((N_HEADS, 128), lambda i: (0, 0))],
        out_shape=[_out((N_HEADS, NEAR_REL), F32), _out((N_HEADS, 128), F32)],
        compiler_params=_params(1),
    )(*_hbm(dbias_t))
    return jnp.pad(near, ((0, 0), (1, 0))).at[:, N_REL - 1].add(far[:, 0])


def _ch_probs(scores, bias, valid):
    z = jnp.where(valid, scores * ATT_SCALE + bias, NEG_INF)
    e = jnp.exp(z - jnp.max(z, axis=-1, keepdims=True))
    return e / jnp.sum(e, axis=-1, keepdims=True)


CH_HEADS = [(pair, h) for pair in range(N_HEADS // 2) for h in range(2)]
CH_COLS = [slice(pair * PAIR, (pair + 1) * PAIR) for pair in range(N_HEADS // 2)]


CH_GROUP = 2
CH_Q = CH_GROUP * CHUNK
CH_WIN = (LOOKBACK + CH_GROUP) * CHUNK


def _ch_valid(n):
    row_chunk = lax.broadcasted_iota(jnp.int32, (CH_Q, CH_WIN), 0) // CHUNK
    slot = lax.broadcasted_iota(jnp.int32, (CH_Q, CH_WIN), 1)
    ahead = slot // CHUNK - row_chunk
    return (ahead >= 0) & (ahead <= LOOKBACK) & (n * CH_Q + slot >= PAD)


def _ch_group_bias(bias):
    shifted = [jnp.pad(bias, ((0, 0), (0, 0), (c * CHUNK, (CH_GROUP - 1 - c) * CHUNK))) for c in range(CH_GROUP)]
    return jnp.concatenate(shifted, axis=1)


def _ch_fold_bias_grad(dbias):
    parts = [dbias[:, c * CHUNK:(c + 1) * CHUNK, c * CHUNK:c * CHUNK + BAND] for c in range(CH_GROUP)]
    return sum(parts[1:], parts[0])


def _ch_fwd(qkv, bias, name):
    T = qkv.shape[0]
    W = N_HEADS * HEAD_DIM

    def body(q_ref, k_ref, v_ref, b_ref, o_ref, kp, vp):
        n = pl.program_id(0)

        @pl.when(n == 0)
        def _():
            _ch_load_padded(k_ref, v_ref, kp, vp)

        win = pl.ds(pl.multiple_of(n * CH_Q, CH_Q), CH_WIN)
        valid = _ch_valid(n)
        lanes = _pair_lanes()
        scores = [_dot(_only(lanes[h], q_ref[:, CH_COLS[pair]]), kp[win, CH_COLS[pair]], NT) for pair, h in CH_HEADS]
        probs = [_ch_probs(s, b_ref[2 * pair + h], valid).astype(BF16) for s, (pair, h) in zip(scores, CH_HEADS)]
        outs = [_dot(p, _only(lanes[h], vp[win, CH_COLS[pair]])) for p, (pair, h) in zip(probs, CH_HEADS)]
        for pair, cols in enumerate(CH_COLS):
            o_ref[:, cols] = outs[2 * pair] + outs[2 * pair + 1]

    full = lambda col: pl.BlockSpec((T, W), lambda n: (0, col))
    return pl.pallas_call(
        body, name=name, grid=(T // CH_Q,),
        in_specs=[pl.BlockSpec((CH_Q, W), lambda n: (n, 3)), full(4), full(5),
                  pl.BlockSpec((N_HEADS, CH_Q, CH_WIN), lambda n: (0, 0, 0))],
        out_specs=pl.BlockSpec((CH_Q, W), lambda n: (n, 0)),
        out_shape=_out((T, W), F32),
        scratch_shapes=[pltpu.VMEM((PAD + T, W), BF16)] * 2,
        compiler_params=_params(1, VMEM_LIMIT),
    )(*_hbm(qkv, qkv, qkv, bias))


def _ch_load_padded(k_ref, v_ref, kp, vp):
    for src, dst in ((k_ref, kp), (v_ref, vp)):
        dst[:PAD, :] = jnp.zeros((PAD, dst.shape[1]), dst.dtype)
        dst[PAD:, :] = src[...]


def _ch_bwd(qkv, bias, do, after, name):
    T = qkv.shape[0]
    W = N_HEADS * HEAD_DIM
    n_chunks = T // CH_Q

    def body(q_ref, k_ref, v_ref, b_ref, do_ref, dq_ref, dk_ref, dv_ref, db_ref, kp, vp, dk_s, dv_s):
        n = pl.program_id(0)

        @pl.when(n == 0)
        def _():
            _ch_load_padded(k_ref, v_ref, kp, vp)
            dk_s[...] = jnp.zeros_like(dk_s)
            dv_s[...] = jnp.zeros_like(dv_s)
            db_ref[...] = jnp.zeros_like(db_ref)

        win = pl.ds(pl.multiple_of(n * CH_Q, CH_Q), CH_WIN)
        valid = _ch_valid(n)
        lanes = _pair_lanes()
        kws = [kp[win, cols] for cols in CH_COLS]
        vws = [vp[win, cols] for cols in CH_COLS]
        qs = [_only(lanes[h], q_ref[:, CH_COLS[pair]]) for pair, h in CH_HEADS]
        dos = [_only(lanes[h], do_ref[:, CH_COLS[pair]].astype(BF16)) for pair, h in CH_HEADS]
        scores = [_dot(q, kws[pair], NT) for q, (pair, _) in zip(qs, CH_HEADS)]
        dps = [_dot(do, vws[pair], NT) for do, (pair, _) in zip(dos, CH_HEADS)]
        probs = [_ch_probs(s, b_ref[2 * pair + h], valid) for s, (pair, h) in zip(scores, CH_HEADS)]
        dzs = [p * (dp - jnp.sum(dp * p, axis=-1, keepdims=True)) for p, dp in zip(probs, dps)]
        for k, dz in enumerate(dzs):
            db_ref[k] += dz
        dzbs = [(dz * ATT_SCALE).astype(BF16) for dz in dzs]
        dqs = [_dot(dz, _only(lanes[h], kws[pair])) for dz, (pair, h) in zip(dzbs, CH_HEADS)]
        dks = [_dot(dz, q, TN) for dz, q in zip(dzbs, qs)]
        dvs = [_dot(p.astype(BF16), do, TN) for p, do in zip(probs, dos)]
        for pair, cols in enumerate(CH_COLS):
            dq_ref[:, cols] = (dqs[2 * pair] + dqs[2 * pair + 1]).astype(BF16)
            dk_s[win, cols] += dks[2 * pair] + dks[2 * pair + 1]
            dv_s[win, cols] += dvs[2 * pair] + dvs[2 * pair + 1]

        @pl.when(n == n_chunks - 1)
        def _():
            dk_ref[...] = dk_s[PAD:, :].astype(BF16)
            dv_ref[...] = dv_s[PAD:, :].astype(BF16)

    full = lambda col: pl.BlockSpec((T, W), lambda n: (0, col))
    blk = lambda col: pl.BlockSpec((CH_Q, W), lambda n: (n, col))
    tab = pl.BlockSpec((N_HEADS, CH_Q, CH_WIN), lambda n: (0, 0, 0))
    out = _out((T, W), BF16)
    return pl.pallas_call(
        lambda after_ref, *refs: body(*refs), name=name, grid=(n_chunks,),
        in_specs=[ANY, blk(3), full(4), full(5), tab, blk(0)],
        out_specs=[blk(0), full(0), full(0), tab],
        out_shape=[out, out, out, _out((N_HEADS, CH_Q, CH_WIN), F32)],
        scratch_shapes=[pltpu.VMEM((PAD + T, W), BF16)] * 2 + [pltpu.VMEM((PAD + T, W), F32)] * 2,
        compiler_params=_params(1, VMEM_LIMIT),
    )(after, *_hbm(qkv, qkv, qkv, bias, do))


def _rows_split(a, parts):
    return a.reshape(a.shape[:-2] + (parts, a.shape[-2] // parts, a.shape[-1]))


def _cast_into_own_slot(me, c, ws, in_chip_order, name):
    parts = 2
    ws = [_rows_split(_rows_split(w, 2), parts) for w in ws]
    n = len(ws)

    def body(me_ref, c_ref, *refs):
        for src, dst in zip(refs[:n], refs[n:]):
            dst[0, 0, 0] = src[0, 0].astype(BF16)

    def specs(w, plain):
        block = (1, 1) + w.shape[2:]
        if plain:
            return (pl.BlockSpec(block, lambda d, r, me_ref, c_ref: (d, r, 0, 0)),
                    pl.BlockSpec((1,) + block, lambda d, r, me_ref, c_ref: (me_ref[0], d, r, 0, 0)))
        return (pl.BlockSpec(block, lambda d, r, me_ref, c_ref: (d ^ c_ref[0], r, 0, 0)),
                pl.BlockSpec((1,) + block, lambda d, r, me_ref, c_ref: (0, d, r, 0, 0)))

    both = [specs(w, plain) for w, plain in zip(ws, in_chip_order)]
    outs = pl.pallas_call(
        body, name=name,
        grid_spec=pltpu.PrefetchScalarGridSpec(
            num_scalar_prefetch=2, grid=(2, parts),
            in_specs=[s[0] for s in both], out_specs=[s[1] for s in both]),
        out_shape=[_out((N_CHIPS,) + w.shape, BF16) for w in ws],
        compiler_params=_params(2, VMEM_LIMIT),
    )(me, c, *_hbm(*ws))
    return [o.reshape(N_CHIPS, 2, o.shape[2] * o.shape[3], o.shape[4]) for o in outs]


def _zone_slots(in_chip_order):
    x, y, c, _ = _place()
    me = 2 * x + y
    if in_chip_order:
        return (me, c), (lambda r: (me, c)), (lambda r: (me ^ r, c)), (lambda r: (me ^ r, c))
    return (0, 0), (lambda r: (r, 0)), (lambda r: (r, 0)), (lambda r: (r, 1))


def _pair_add(c, mine, got, permuted, name):
    parts = 2
    mine = [_rows_split(m, parts) for m in mine]
    got = [_rows_split(g, parts) for g in got]
    n = len(mine)

    def body(c_ref, *refs):
        for a, b, o in zip(refs[:n], refs[n:2 * n], refs[2 * n:]):
            o[0, 0] = (a[0, 0, 0] + b[0, 0].astype(F32)).astype(BF16)

    def mine_spec(m, perm):
        if perm:
            return pl.BlockSpec((1, 1, 1) + m.shape[3:], lambda j, r, c_ref: (j, 0, r, 0, 0))
        return pl.BlockSpec((1, 1, 1) + m.shape[3:], lambda j, r, c_ref: (j, c_ref[0], r, 0, 0))

    def got_spec(g):
        return pl.BlockSpec((1, 1) + g.shape[2:], lambda j, r, c_ref: (j, r, 0, 0))

    outs = pl.pallas_call(
        body, name=name,
        grid_spec=pltpu.PrefetchScalarGridSpec(
            num_scalar_prefetch=1, grid=(N_CHIPS, parts),
            in_specs=[mine_spec(m, perm) for m, perm in zip(mine, permuted)] + [got_spec(g) for g in got],
            out_specs=[got_spec(g) for g in got]),
        out_shape=[_out(g.shape, BF16) for g in got],
        compiler_params=_params(2, VMEM_LIMIT),
    )(c, *_hbm(*mine, *got))
    return [o.reshape(o.shape[0], o.shape[1] * o.shape[2], o.shape[3]) for o in outs]


def _chip_add(me, partials, landed, permuted, name):
    parts = 2
    ps = [_rows_split(x, parts) for x in partials]
    ls = [_rows_split(x, parts) for x in landed]
    n = len(ps)

    def body(me_ref, *refs):
        for own, got, o in zip(refs[:n], refs[n:2 * n], refs[2 * n:]):
            acc = own[0, 0].astype(F32)
            for r in range(N_CHIPS - 1):
                acc = acc + got[r, 0].astype(F32)
            o[0] = acc

    def own_spec(x, perm):
        if perm:
            return pl.BlockSpec((1, 1) + x.shape[2:], lambda r, me_ref: (0, r, 0, 0))
        return pl.BlockSpec((1, 1) + x.shape[2:], lambda r, me_ref: (me_ref[0], r, 0, 0))

    outs = pl.pallas_call(
        body, name=name,
        grid_spec=pltpu.PrefetchScalarGridSpec(
            num_scalar_prefetch=1, grid=(parts,),
            in_specs=[own_spec(x, perm) for x, perm in zip(ps, permuted)]
            + [pl.BlockSpec((N_CHIPS - 1, 1) + x.shape[2:], lambda r, me_ref: (0, r, 0, 0)) for x in ls],
            out_specs=[pl.BlockSpec((1,) + x.shape[2:], lambda r, me_ref: (r, 0, 0)) for x in ps]),
        out_shape=[_out(x.shape[1:], F32) for x in ps],
        compiler_params=_params(1, VMEM_LIMIT),
    )(me, *_hbm(*ps, *ls))
    return [o.reshape(o.shape[0] * o.shape[1], o.shape[2]) for o in outs]


def _adamw_math(w, g, m, v):
    m = ADAM_B1 * m + (1.0 - ADAM_B1) * g
    v = ADAM_B2 * v + (1.0 - ADAM_B2) * (g * g)
    m_hat = m / (1.0 - ADAM_B1 ** ADAM_STEP)
    v_hat = v / (1.0 - ADAM_B2 ** ADAM_STEP)
    delta = -ADAM_LR * (m_hat / (jnp.sqrt(v_hat) + ADAM_EPS) + ADAM_WD * w)
    return delta, m, v


def _adamw(ws, gs, ms, vs, parts, name):
    n = len(ws)
    flat = [_rows_split(a, parts) for a in (*ws, *gs, *ms, *vs)]

    def body(*refs):
        ins, outs = refs[:4 * n], refs[4 * n:]
        for k in range(n):
            d, m, v = _adamw_math(ins[k][...], ins[n + k][...], ins[2 * n + k][...], ins[3 * n + k][...])
            outs[k][...] = d
            outs[n + k][...] = m
            outs[2 * n + k][...] = v

    spec = lambda a: pl.BlockSpec((1,) + a.shape[1:], lambda i: (i, 0, 0))
    outs = pl.pallas_call(
        body, name=name, grid=(parts,),
        in_specs=[spec(a) for a in flat], out_specs=[spec(a) for a in flat[:n]] * 3,
        out_shape=[_out(a.shape, F32) for a in flat[:n]] * 3,
        compiler_params=_params(1, VMEM_LIMIT),
    )(*_hbm(*flat))
    outs = [o.reshape(o.shape[0] * o.shape[1], o.shape[2]) for o in outs]
    return outs[:n], outs[n:2 * n], outs[2 * n:]


def _adamw_halves(c, ws, owns, others, ms, vs, name):
    parts = 4
    n = len(ws)
    whole = [_rows_split(_rows_split(a, 2), parts) for a in (*ws, *ms, *vs)]
    halves = [_rows_split(a, parts) for a in (*owns, *others)]

    def body(c_ref, *refs):
        ins, outs = refs[:5 * n], refs[5 * n:]
        mine = pl.program_id(0) == c_ref[0]
        for k in range(n):
            g = jnp.where(mine, ins[3 * n + k][0], ins[4 * n + k][0])
            d, m, v = _adamw_math(ins[k][0, 0], g, ins[n + k][0, 0], ins[2 * n + k][0, 0])
            for slot, val in enumerate((g, d, m, v)):
                outs[slot * n + k][0, 0] = val

    wspec = lambda a: pl.BlockSpec((1, 1) + a.shape[2:], lambda h, r, c_ref: (h, r, 0, 0))
    hspec = lambda a: pl.BlockSpec((1,) + a.shape[1:], lambda h, r, c_ref: (r, 0, 0))
    outs = pl.pallas_call(
        body, name=name,
        grid_spec=pltpu.PrefetchScalarGridSpec(
            num_scalar_prefetch=1, grid=(2, parts),
            in_specs=[wspec(a) for a in whole] + [hspec(a) for a in halves],
            out_specs=[wspec(a) for a in whole[:n]] * 4),
        out_shape=[_out(a.shape, F32) for a in whole[:n]] * 4,
        compiler_params=_params(2, VMEM_LIMIT),
    )(c, *_hbm(*whole, *halves))
    outs = [o.reshape(2 * parts * o.shape[2], o.shape[3]) for o in outs]
    return outs[:n], outs[n:2 * n], outs[2 * n:3 * n], outs[3 * n:]


def _place():
    x, y, c = lax.axis_index("x"), lax.axis_index("y"), lax.axis_index("c")
    peers = [(x ^ (r >> 1), y ^ (r & 1), c) for r in (1, 2, 3)]
    return x, y, c, peers


def _handshake(peers):
    barrier = pltpu.get_barrier_semaphore()
    for peer in peers:
        pl.semaphore_signal(barrier, inc=1, device_id=peer, device_id_type=MESH)
    pl.semaphore_wait(barrier, len(peers))


ANY = pl.BlockSpec(memory_space=pl.ANY)
HBM = pl.BlockSpec(memory_space=pltpu.HBM)
SEM = pl.BlockSpec(memory_space=pltpu.SEMAPHORE)
SPLIT_COPY = pltpu.SideEffectType.DATAFLOW_SIDE_EFFECTING


def _split_start(body, name, collective_id, operands, n_sems, after=None):
    n = len(operands)
    extra = [] if after is None else [after]

    def wrapped(*refs):
        at = n + len(extra)
        body(refs[:n], refs[at], refs[at + 1])
        token = refs[-1]
        token[...] = jnp.zeros_like(token)

    outs = pl.pallas_call(
        wrapped, name=name,
        in_specs=[HBM] * n + [ANY] * len(extra),
        out_shape=(pltpu.SemaphoreType.DMA((n_sems,)), pltpu.SemaphoreType.DMA((n_sems,)),
                   *[pltpu.HBM(a.shape, a.dtype) for a in operands], jax.ShapeDtypeStruct((8, 128), F32)),
        out_specs=(SEM, SEM, *[HBM] * n, pl.BlockSpec(memory_space=pltpu.VMEM)),
        input_output_aliases={i: 2 + i for i in range(n)},
        compiler_params=pltpu.CompilerParams(has_side_effects=SPLIT_COPY, collective_id=collective_id),
    )(*_hbm(*operands), *extra)
    return outs[0], outs[1], list(outs[2:2 + n]), outs[-1]


def _split_wait(body, name, send_sem, recv_sem, operands, after):
    n = len(operands)

    def wrapped(*refs):
        body(refs[:n], refs[n], refs[n + 1])

    outs = pl.pallas_call(
        wrapped, name=name,
        in_specs=[HBM] * n + [SEM, SEM, ANY],
        out_shape=tuple(pltpu.HBM(a.shape, a.dtype) for a in operands),
        out_specs=tuple([HBM] * n),
        input_output_aliases={i: i for i in range(n)},
        compiler_params=pltpu.CompilerParams(has_side_effects=SPLIT_COPY),
    )(*operands, send_sem, recv_sem, after)
    return list(outs)


def _gather_copies(lands, in_chip_order, send_sem, recv_sem):
    peers = _place()[3]
    copies = []
    for a, (land, plain) in enumerate(zip(lands, in_chip_order)):
        own, sent_to, _, _ = _zone_slots(plain)
        copies += [pltpu.make_async_remote_copy(
            src_ref=land.at[own], dst_ref=land.at[sent_to(r + 1)],
            send_sem=send_sem.at[a * 3 + r], recv_sem=recv_sem.at[a * 3 + r],
            device_id=peers[r], device_id_type=MESH) for r in range(3)]
    return copies


def _gather_start(lands, in_chip_order, name, collective_id, after):
    def body(refs, send_sem, recv_sem):
        _handshake(_place()[3])
        for cp in _gather_copies(refs, in_chip_order, send_sem, recv_sem):
            cp.start()

    return _split_start(body, name, collective_id, list(lands), 3 * len(lands), after)


def _gather_wait(send_sem, recv_sem, operands, in_chip_order, after, name):
    def body(refs, send_sem, recv_sem):
        for cp in _gather_copies(refs, in_chip_order, send_sem, recv_sem):
            cp.wait_send()
            cp.wait_recv()

    return _split_wait(body, name, send_sem, recv_sem, operands, after)


def _gather_finish(lands, in_chip_order, with_ici, name):
    n = len(lands)

    def body(*refs):
        land = refs[n:2 * n]
        send_ici, recv_ici, send_d2d, recv_d2d = refs[2 * n:]
        x, y, c, _ = _place()
        ici = _gather_copies(land, in_chip_order, send_ici, recv_ici) if with_ici else []
        for cp in ici:
            cp.start()
        passed = []
        for a in range(n):
            _, _, received, kept = _zone_slots(in_chip_order[a])
            passed += [pltpu.make_async_remote_copy(
                src_ref=land[a].at[received(r + 1)], dst_ref=land[a].at[kept(r + 1)],
                send_sem=send_d2d.at[a * 3 + r], recv_sem=recv_d2d.at[a * 3 + r],
                device_id=(x, y, 1 - c), device_id_type=MESH) for r in range(3)]
        for k, cp in enumerate(passed):
            if with_ici:
                ici[k].wait_recv()
            cp.start()
        for cp in passed:
            cp.wait_recv()
        for cp in ici:
            cp.wait_send()
        for cp in passed:
            cp.wait_send()

    outs = pl.pallas_call(
        body, name=name,
        in_specs=[ANY] * n, out_specs=[ANY] * n,
        out_shape=[_out(l.shape, l.dtype) for l in lands],
        input_output_aliases={a: a for a in range(n)},
        scratch_shapes=[pltpu.SemaphoreType.DMA((3 * n,))] * 4,
    )(*lands)
    return list(outs)


def _slabs(land):
    return land.reshape(N_CHIPS, 2 * land.shape[2], land.shape[3])


def _pair_swap(grads, permuted, name):
    n = len(grads)

    def body(*refs):
        src, dst = refs[:n], refs[n:2 * n]
        send_sem, recv_sem = refs[2 * n:]
        x, y, c, _ = _place()
        copies = [pltpu.make_async_remote_copy(
            src_ref=src[a].at[:, 1] if permuted[a] else src[a].at[:, 1 - c], dst_ref=dst[a],
            send_sem=send_sem.at[a], recv_sem=recv_sem.at[a],
            device_id=(x, y, 1 - c), device_id_type=MESH) for a in range(n)]
        for cp in copies:
            cp.start()
        for cp in copies:
            cp.wait()

    return pl.pallas_call(
        body, name=name,
        in_specs=[ANY] * n, out_specs=[ANY] * n,
        out_shape=[_out((N_CHIPS,) + g.shape[2:], g.dtype) for g in grads],
        scratch_shapes=[pltpu.SemaphoreType.DMA((n,))] * 2,
    )(*grads)


def _swap_copies(refs, permuted, send_sem, recv_sem):
    n = len(refs) // 2
    x, y, c, _ = _place()
    return [pltpu.make_async_remote_copy(
        src_ref=refs[a].at[:, 1] if permuted[a] else refs[a].at[:, 1 - c], dst_ref=refs[n + a],
        send_sem=send_sem.at[a], recv_sem=recv_sem.at[a],
        device_id=(x, y, 1 - c), device_id_type=MESH) for a in range(n)]


def _pair_swap_start(grads, permuted, name, collective_id):
    def body(refs, send_sem, recv_sem):
        x, y, c, _ = _place()
        _handshake([(x, y, 1 - c)])
        for cp in _swap_copies(refs, permuted, send_sem, recv_sem):
            cp.start()

    lands = [lax.empty((N_CHIPS,) + g.shape[2:], g.dtype) for g in grads]
    return _split_start(body, name, collective_id, list(grads) + lands, len(grads))


def _pair_swap_wait(send_sem, recv_sem, operands, permuted, after, name):
    def body(refs, send_sem, recv_sem):
        for cp in _swap_copies(refs, permuted, send_sem, recv_sem):
            cp.wait_send()
            cp.wait_recv()

    return _split_wait(body, name, send_sem, recv_sem, operands, after)


def _scatter_copies(refs, permuted, send_sem, recv_sem):
    n = len(refs) // 2
    x, y, _, peers = _place()
    me = 2 * x + y
    return [pltpu.make_async_remote_copy(
        src_ref=refs[a].at[r + 1] if permuted[a] else refs[a].at[me ^ (r + 1)], dst_ref=refs[n + a].at[r],
        send_sem=send_sem.at[a * 3 + r], recv_sem=recv_sem.at[a * 3 + r],
        device_id=peers[r], device_id_type=MESH) for a in range(n) for r in range(3)]


def _scatter_start(partials, permuted, name, collective_id):
    def body(refs, send_sem, recv_sem):
        _handshake(_place()[3])
        for cp in _scatter_copies(refs, permuted, send_sem, recv_sem):
            cp.start()

    lands = [lax.empty((N_CHIPS - 1,) + p.shape[1:], p.dtype) for p in partials]
    return _split_start(body, name, collective_id, list(partials) + lands, 3 * len(partials))


def _scatter_wait(send_sem, recv_sem, operands, permuted, after, name):
    def body(refs, send_sem, recv_sem):
        for cp in _scatter_copies(refs, permuted, send_sem, recv_sem):
            cp.wait_send()
            cp.wait_recv()

    return _split_wait(body, name, send_sem, recv_sem, operands, after)


def _pair_join(halves, name):
    n = len(halves)

    def body(*refs):
        src, dst = refs[:n], refs[n:2 * n]
        send_sem, recv_sem = refs[2 * n:]
        x, y, c, _ = _place()
        copies = [pltpu.make_async_remote_copy(
            src_ref=src[a], dst_ref=dst[a], send_sem=send_sem.at[a], recv_sem=recv_sem.at[a],
            device_id=(x, y, 1 - c), device_id_type=MESH) for a in range(n)]
        for cp in copies:
            cp.start()
        for cp in copies:
            cp.wait()

    return pl.pallas_call(
        body, name=name,
        in_specs=[ANY] * n, out_specs=[ANY] * n,
        out_shape=[_out(h.shape, F32) for h in halves],
        scratch_shapes=[pltpu.SemaphoreType.DMA((n,))] * 2,
    )(*halves)


def _all_sum_small(v, after, name):
    R, C = v.shape
    n_dev = 8

    def body(v_ref, after_ref, o_ref, buf, send_sem, recv_sem):
        x, y, c, _ = _place()
        me = 4 * x + 2 * y + c
        buf[me] = v_ref[...]
        copies = []
        for k in range(1, n_dev):
            peer = (x ^ (k >> 2), y ^ ((k >> 1) & 1), c ^ (k & 1))
            copies.append(pltpu.make_async_remote_copy(
                src_ref=v_ref, dst_ref=buf.at[me], send_sem=send_sem.at[k - 1], recv_sem=recv_sem.at[k - 1],
                device_id=peer, device_id_type=MESH))
        for cp in copies:
            cp.start()
        for cp in copies:
            cp.wait()
        acc = buf[0]
        for m in range(1, n_dev):
            acc = acc + buf[m]
        o_ref[...] = acc

    return pl.pallas_call(
        body, name=name,
        in_specs=[pl.BlockSpec(memory_space=pltpu.VMEM), ANY], out_specs=pl.BlockSpec(memory_space=pltpu.VMEM),
        out_shape=jax.ShapeDtypeStruct((R, C), F32),
        scratch_shapes=[pltpu.VMEM((n_dev, R, C), F32), pltpu.SemaphoreType.DMA((n_dev - 1,)),
                        pltpu.SemaphoreType.DMA((n_dev - 1,))],
    )(v, after)


class _WholeWeights:
    def __init__(self, w):
        self.w = w

    def weights(self, group, after=None):
        return self.w, None

    def grads_ready(self, group, gw):
        return None

    def grads_sent(self, group, after):
        return None


def _local_step(x, p, target, gains, rel_bias, hooks):
    T, D = x.shape
    S = N_CHIPS

    tied = lambda gain, token: gain if token is None else gain + token[0, 0]
    w, token = hooks.weights("first")
    w = dict(w)
    xn1, g1, u1, a1 = _ffn_up(x, tied(gains["ffn1_pre"], token), w["ffn1_gate"], w["ffn1_up"], "ffn1_up")
    w.update(hooks.weights("down", a1)[0])
    h1, f1 = _ffn_down(x, a1, gains["ffn1_post"], w["ffn1_down"], "ffn1_down")
    more, token = hooks.weights("in", h1)
    w.update(more)
    qkv, un = _norm_proj(h1, tied(gains["mix_pre"], token), w["in"], "qkv_proj")
    bias = _ch_group_bias(_bias_table(rel_bias, "bias_table").transpose(1, 0, 2))
    o_a = _sb_fwd(qkv, "sb_fwd")
    o_b = _ch_fwd(qkv, bias, "ch_fwd")
    w.update(hooks.weights("rest", o_b)[0])
    w_out = w["out"].reshape(D, D)
    h2, mixed, mo = _mix_out_fwd(h1, o_a, o_b, gains["out_sb"], gains["out_ch"], w_out, gains["mix_post"],
                                 "mix_out_fwd")
    h3, xn2, g2, u2, a2, f2 = _ffn_fwd(h2, gains["ffn2_pre"], gains["ffn2_post"], w["ffn2_gate"], w["ffn2_up"],
                                       w["ffn2_down"], "ffn2_fwd")
    w_ple_proj = w["ple_proj"].transpose(1, 0, 2).reshape(p.shape[1], D)
    w_ple_gate = w["ple_gate"].reshape(D, D)

    loss, dh3, dproj, dgate, dg_ple = _ple_loss(h3, p, target, w_ple_proj, w_ple_gate, gains["ple_post"], "ple_loss")
    gw, gg = {}, {"ple_post": dg_ple}
    gw["ple_proj"] = _mm_tn(p[None], dproj, p.shape[1], "dw_ple_proj")
    row_sharded = lambda pair: tuple(o.reshape(S, D // S, D) for o in pair)
    gw["ple_gate"] = row_sharded(_mm_tn(h3[None], dgate[None], 512, "dw_ple_gate"))

    def ffn_bwd(tag, dh, x_in, xn, g_act, u_act, a_act, f, group):
        dgp, dup, df, gg[tag + "_post"] = _ffn_bwd_act(dh, f, gains[tag + "_post"], w[tag + "_down"], g_act, u_act,
                                                       tag + "_bwd_act")
        gw[tag + "_gate"] = _mm_tn(dgp, xn[None], dgp.shape[2], "dw_" + tag + "_gate")
        gw[tag + "_up"] = _mm_tn(dup, xn[None], dup.shape[2], "dw_" + tag + "_up")
        gw[tag + "_down"] = _mm_tn(a_act, df[None], a_act.shape[2], "dw_" + tag + "_down")
        g_pre = gains[tag + "_pre"]
        if group is not None:
            token = hooks.grads_ready(group, gw)
            g_pre = g_pre if token is None else g_pre + token[0, 0]
        dx, gg[tag + "_pre"] = _proj_bwd([dgp, dup], [w[tag + "_gate"], w[tag + "_up"]], x_in, g_pre, dh,
                                         tag + "_bwd_in")
        return dx

    dh2 = ffn_bwd("ffn2", dh3, h2, xn2, g2, u2, a2, f2, None)
    dmo, do_a, do_b, gg["mix_post"], gg["out_sb"], gg["out_ch"] = _mix_out_bwd(
        dh2, mo, gains["mix_post"], w_out, o_a, o_b, gains["out_sb"], gains["out_ch"], "mix_out_bwd")
    gw["out"] = row_sharded(_mm_tn(mixed[None], dmo[None], 512, "dw_out"))
    token = hooks.grads_ready("early", gw)
    dq_a, dk_a, dv_a = _sb_bwd(qkv, do_a, o_a, do_a if token is None else token, "sb_bwd")
    token = hooks.grads_sent("early", dq_a)
    dq_b, dk_b, dv_b, dbias = _ch_bwd(qkv, bias, do_b, do_b if token is None else token, "ch_bwd")
    g_rel = _bias_grad(_ch_fold_bias_grad(dbias).transpose(1, 0, 2), "bias_grad")
    dqkv = [dq_a, dk_a, dv_a, dq_b, dk_b, dv_b]
    gw["in"] = _dw_in(un, dqkv, w["in"].shape[2], 512, "dw_in")
    dh1, gg["mix_pre"] = _qkv_bwd_in(dqkv, w["in"], h1, gains["mix_pre"], dh2, "qkv_bwd_in")
    dx = ffn_bwd("ffn1", dh1, x, xn1, g1, u1, a1, f1, "late")
    return loss, dx, gw, gg, g_rel


BIG = ["ffn1_gate", "ffn1_up", "ffn1_down", "in", "out", "ffn2_gate", "ffn2_up", "ffn2_down", "ple_proj", "ple_gate"]
GAINS = ["ffn1_pre", "ffn1_post", "mix_pre", "mix_post", "out_sb", "out_ch", "ffn2_pre", "ffn2_post", "ple_post"]
TRANSPOSED = ("w_ffn1_gate", "w_ffn1_up", "w_ffn2_gate", "w_ffn2_up")
PERMUTED = ("ffn1_gate", "ffn1_up", "ffn1_down", "ffn2_gate", "ffn2_up", "ffn2_down")
W_GROUPS = {"first": ["ffn1_gate", "ffn1_up"], "down": ["ffn1_down"], "in": ["in"],
            "rest": ["out", "ffn2_gate", "ffn2_up", "ffn2_down", "ple_proj", "ple_gate"]}
G_GROUPS = {"early": ["ple_proj", "ple_gate", "ffn2_gate", "ffn2_up", "ffn2_down", "out"],
            "late": ["in", "ffn1_gate", "ffn1_up", "ffn1_down"]}
ORDER = ["g_ffn1_pre", "g_ffn1_post", "w_ffn1_gate", "w_ffn1_up", "w_ffn1_down", "g_mix_pre", "g_mix_post", "w_in",
         "g_out_sb", "g_out_ch", "rel_bias", "w_out", "g_ffn2_pre", "g_ffn2_post", "w_ffn2_gate", "w_ffn2_up",
         "w_ffn2_down", "w_ple_proj", "w_ple_gate", "g_ple_post"]


def kernel(x, p, g_ffn1_pre, g_ffn1_post, w_ffn1_gate, w_ffn1_up, w_ffn1_down, g_mix_pre, g_mix_post, w_in, g_out_sb, g_out_ch, rel_bias, w_out, g_ffn2_pre, g_ffn2_post, w_ffn2_gate, w_ffn2_up, w_ffn2_down, w_ple_proj, w_ple_gate, g_ple_post, loss_target, m_g_ffn1_pre, m_g_ffn1_post, m_w_ffn1_gate, m_w_ffn1_up, m_w_ffn1_down, m_g_mix_pre, m_g_mix_post, m_w_in, m_g_out_sb, m_g_out_ch, m_rel_bias, m_w_out, m_g_ffn2_pre, m_g_ffn2_post, m_w_ffn2_gate, m_w_ffn2_up, m_w_ffn2_down, m_w_ple_proj, m_w_ple_gate, m_g_ple_post, v_g_ffn1_pre, v_g_ffn1_post, v_w_ffn1_gate, v_w_ffn1_up, v_w_ffn1_down, v_g_mix_pre, v_g_mix_post, v_w_in, v_g_out_sb, v_g_out_ch, v_rel_bias, v_w_out, v_g_ffn2_pre, v_g_ffn2_post, v_w_ffn2_gate, v_w_ffn2_up, v_w_ffn2_down, v_w_ple_proj, v_w_ple_gate, v_g_ple_post):
    args = dict(locals())
    take = lambda a, n: a[0].T if n in TRANSPOSED else a[0]
    wts = {n: take(args[n], n) for n in ORDER}
    ms = {n: take(args["m_" + n], n) for n in ORDER}
    vs = {n: take(args["v_" + n], n) for n in ORDER}
    gains = {n: wts["g_" + n][None] for n in GAINS}

    c_idx = lax.axis_index("c").astype(jnp.int32).reshape(1)
    me_idx = (2 * lax.axis_index("x") + lax.axis_index("y")).astype(jnp.int32).reshape(1)
    south = lax.axis_index("c") == 0

    plain = lambda names: [n not in PERMUTED for n in names]
    lands = dict(zip(BIG, _cast_into_own_slot(me_idx, c_idx, [wts["w_" + n] for n in BIG], plain(BIG), "cast_weights")))

    class Overlapped:
        def __init__(self):
            self.started = {}
            self.flying = {}

        def start(self, group, collective_id, after):
            names = W_GROUPS[group]
            self.flying[group] = _gather_start([lands[n] for n in names], plain(names), "gather_%s_start" % group,
                                               collective_id, after)
            return self.flying[group][3]

        def weights(self, group, after=None):
            names = W_GROUPS[group]
            token = None
            if group == "first":
                zones = _gather_finish([lands[n] for n in names], plain(names), True, "gather_first")
                token = self.start("rest", 4, self.start("in", 1, self.start("down", 6, zones[0])))
            else:
                send_sem, recv_sem, zones, _ = self.flying[group]
                zones = _gather_wait(send_sem, recv_sem, zones, plain(names), after, "gather_%s_wait" % group)
                zones = _gather_finish(zones, plain(names), False, "gather_%s_finish" % group)
            return {n: _slabs(z) for n, z in zip(names, zones)}, token

        def grads_ready(self, group, gw):
            names = G_GROUPS[group]
            perm = [n in PERMUTED for n in names]
            halved = lambda g: g.reshape(N_CHIPS, 2, g.shape[1] // 2, g.shape[2])
            mine = [halved(gw[n][0]) for n in names]
            narrow = [halved(gw[n][1]) for n in names]
            if group == "late":
                return self.scatter(group, names, perm, mine, _pair_swap(narrow, perm, "grad_pair_swap_late"))
            self.swapping = names, perm, mine, _pair_swap_start(narrow, perm, "grad_pair_swap_start_early", 5)
            return self.swapping[3][3]

        def grads_sent(self, group, after):
            names, perm, mine, (send_sem, recv_sem, operands, _) = self.swapping
            operands = _pair_swap_wait(send_sem, recv_sem, operands, perm, after, "grad_pair_swap_wait_early")
            return self.scatter(group, names, perm, mine, operands[len(names):])

        def scatter(self, group, names, perm, mine, got):
            partial = _pair_add(c_idx, mine, got, perm, "grad_pair_add_" + group)
            send_sem, recv_sem, operands, token = _scatter_start(partial, perm, "grad_scatter_start_" + group,
                                                                 {"early": 2, "late": 3}[group])
            self.started[group] = names, perm, send_sem, recv_sem, operands, token
            return token

    def reduce_finish(state, after, tag):
        names, perm, send_sem, recv_sem, operands, _ = state
        operands = _scatter_wait(send_sem, recv_sem, operands, perm, after, "grad_scatter_wait_" + tag)
        n = len(names)
        own = _chip_add(me_idx, operands[:n], operands[n:], perm, "grad_chip_add_" + tag)
        return own, _pair_join(own, "grad_pair_join_" + tag)

    hooks = Overlapped()
    loss, dx, gw, gg, g_rel = _local_step(x[0], p[0, 0], loss_target[0], gains, wts["rel_bias"], hooks)

    grads, delta, new_m, new_v = {}, {}, {}, {}

    def finish(group, after):
        own, other = reduce_finish(hooks.started[group], after, group)
        names = ["w_" + n for n in G_GROUPS[group]]
        g, d, m, v = _adamw_halves(c_idx, [wts[n] for n in names], own, other, [ms[n] for n in names],
                                   [vs[n] for n in names], "adamw_" + group)
        for n, gg_, dd, mm, vv in zip(names, g, d, m, v):
            grads[n], delta[n], new_m[n], new_v[n] = gg_, dd, mm, vv
        return d[0]

    finish("late", finish("early", dx))

    pieces = [gg[n].reshape(-1, 128) for n in GAINS] + [jnp.pad(g_rel, ((0, 0), (0, N_REL_PAD - N_REL))).reshape(-1, 128)]
    summed = _all_sum_small(jnp.concatenate(pieces + [loss], axis=0), delta["w_in"], "small_grad_sum")
    at = 0
    for n, piece in zip(GAINS, pieces[:-1]):
        grads["g_" + n] = summed[at:at + piece.shape[0]].reshape(1, -1)[0]
        at += piece.shape[0]
    grads["rel_bias"] = summed[at:at + pieces[-1].shape[0]].reshape(N_HEADS, N_REL_PAD)[:, :N_REL]
    loss = summed[at + pieces[-1].shape[0], 0]

    small = ["g_" + n for n in GAINS] + ["rel_bias"]
    as_rows = lambda a: (a.reshape(-1, 128) if a.size % 128 == 0 else jnp.pad(a, ((0, 0), (0, N_REL_PAD - N_REL))).reshape(-1, 128))
    d, m, v = _adamw([as_rows(wts[n]) for n in small], [as_rows(grads[n]) for n in small],
                     [as_rows(ms[n]) for n in small], [as_rows(vs[n]) for n in small], 1, "adamw_small")
    for n, dd, mm, vv in zip(small, d, m, v):
        back = (lambda a: a.reshape(N_HEADS, N_REL_PAD)[:, :N_REL]) if n == "rel_bias" else (lambda a: a.reshape(-1))
        delta[n], new_m[n], new_v[n] = back(dd), back(mm), back(vv)

    outs = [loss, dx[None]]
    for table in (grads, delta, new_m, new_v):
        outs += [(table[n].T if n in TRANSPOSED else table[n])[None] for n in ORDER]
    return tuple(outs)
```

```python
import jax
import jax.numpy as jnp
from jax import lax
from jax.experimental import pallas as pl
from jax.experimental.pallas import tpu as pltpu

F32 = jnp.float32
BF16 = jnp.bfloat16
EPS = 1e-6
N_CHIPS = 4
HEAD_DIM = 64
N_HEADS = 8
CHUNK = 64
LOOKBACK = 8
BAND = (LOOKBACK + 1) * CHUNK
PAD = LOOKBACK * CHUNK
REL_CLIP = 128
N_REL = 2 * REL_CLIP + 1
N_REL_PAD = 384
SB_BLOCK = 256
PAIR = 2 * HEAD_DIM
SB_PAIRS = 2
SB_FWD_PAIRS = 4
ATT_SCALE = HEAD_DIM ** -0.5
NEG_INF = -1e30
ROW_BLOCK = 512
WIDE_ROW_BLOCK = 1024
VMEM_LIMIT_WIDE = 56 * 1024 * 1024
VMEM_LIMIT = 48 * 1024 * 1024
MESH = pl.DeviceIdType.MESH

ADAM_LR = 0.001
ADAM_B1 = 0.9
ADAM_B2 = 0.999
ADAM_EPS = 1e-08
ADAM_WD = 0.01
ADAM_STEP = 10

NT = (((1,), (1,)), ((), ()))
TN = (((0,), (0,)), ((), ()))


def _params(n_grid, vmem=None):
    return pltpu.CompilerParams(dimension_semantics=("arbitrary",) * n_grid, vmem_limit_bytes=vmem)


def _hbm(*arrays):
    return [pltpu.with_memory_space_constraint(a, pltpu.HBM) for a in arrays]


def _out(shape, dtype):
    return pltpu.HBM(shape, dtype)


def _dot(a, b, dims=None):
    if dims is None:
        return jnp.dot(a, b, preferred_element_type=F32)
    return lax.dot_general(a, b, dims, preferred_element_type=F32)


def _sigmoid(x):
    return 1.0 / (1.0 + jnp.exp(-x))


def _rms_fwd(x, g):
    r = lax.rsqrt(jnp.mean(x * x, axis=-1, keepdims=True) + EPS)
    return x * r * g


def _rms_bwd(x, g, dy):
    r = lax.rsqrt(jnp.mean(x * x, axis=-1, keepdims=True) + EPS)
    xh = x * r
    dg = jnp.sum(dy * xh, axis=0, keepdims=True)
    t = dy * g
    dx = r * (t - xh * jnp.mean(t * xh, axis=-1, keepdims=True))
    return dx, dg


def _accumulate(ref, val, first):
    @pl.when(first)
    def _():
        ref[...] = val

    @pl.when(jnp.logical_not(first))
    def _():
        ref[...] += val


def _split2(x):
    hi = x.astype(BF16)
    lo = (x - hi.astype(F32)).astype(BF16)
    return hi, lo


def _ffn_fwd(x, g_pre, g_post, wg, wu, wd, name):
    T, D = x.shape
    S, FS, _ = wg.shape
    tm = min(WIDE_ROW_BLOCK, T)

    def body(x_ref, gpre_ref, gpost_ref, wg_ref, wu_ref, wd_ref,
             h_ref, xn_ref, g_ref, u_ref, a_ref, f_ref):
        k = pl.program_id(1)

        @pl.when(k == 0)
        def _():
            xn_ref[...] = _rms_fwd(x_ref[...], gpre_ref[...]).astype(BF16)

        xn = xn_ref[...]
        g = _dot(xn, wg_ref[0], NT)
        u = _dot(xn, wu_ref[0], NT)
        g_ref[0] = g
        u_ref[0] = u
        a = (g * _sigmoid(g) * u).astype(BF16)
        a_ref[0] = a
        _accumulate(f_ref, _dot(a, wd_ref[0]), k == 0)

        @pl.when(k == S - 1)
        def _():
            h_ref[...] = x_ref[...] + 0.5 * _rms_fwd(f_ref[...], gpost_ref[...])

    row = pl.BlockSpec((tm, D), lambda i, k: (i, 0))
    vec = pl.BlockSpec((1, D), lambda i, k: (0, 0))
    act = pl.BlockSpec((1, tm, FS), lambda i, k: (k, i, 0))
    return pl.pallas_call(
        body, name=name, grid=(T // tm, S),
        in_specs=[row, vec, vec] + [pl.BlockSpec((1, FS, D), lambda i, k: (k, 0, 0))] * 3,
        out_specs=[row, row, act, act, act, row],
        out_shape=[_out((T, D), F32), _out((T, D), BF16),
                   _out((S, T, FS), F32), _out((S, T, FS), F32),
                   _out((S, T, FS), BF16), _out((T, D), F32)],
        compiler_params=_params(2, VMEM_LIMIT_WIDE),
    )(*_hbm(x, g_pre, g_post, wg, wu, wd))


def _ffn_up(x, g_pre, wg, wu, name):
    T, D = x.shape
    S, FS, _ = wg.shape
    tm = min(WIDE_ROW_BLOCK, T)

    def body(x_ref, gpre_ref, wg_ref, wu_ref, xn_ref, g_ref, u_ref, a_ref):
        @pl.when(pl.program_id(1) == 0)
        def _():
            xn_ref[...] = _rms_fwd(x_ref[...], gpre_ref[...]).astype(BF16)

        xn = xn_ref[...]
        g = _dot(xn, wg_ref[0], NT)
        u = _dot(xn, wu_ref[0], NT)
        g_ref[0] = g
        u_ref[0] = u
        a_ref[0] = (g * _sigmoid(g) * u).astype(BF16)

    row = pl.BlockSpec((tm, D), lambda i, k: (i, 0))
    act = pl.BlockSpec((1, tm, FS), lambda i, k: (k, i, 0))
    return pl.pallas_call(
        body, name=name, grid=(T // tm, S),
        in_specs=[row, pl.BlockSpec((1, D), lambda i, k: (0, 0))] + [pl.BlockSpec((1, FS, D), lambda i, k: (k, 0, 0))] * 2,
        out_specs=[row, act, act, act],
        out_shape=[_out((T, D), BF16), _out((S, T, FS), F32), _out((S, T, FS), F32), _out((S, T, FS), BF16)],
        compiler_params=_params(2, VMEM_LIMIT_WIDE),
    )(*_hbm(x, g_pre, wg, wu))


def _ffn_down(x, a, g_post, wd, name):
    T, D = x.shape
    S, FS, _ = wd.shape
    tm = min(WIDE_ROW_BLOCK, T)

    def body(x_ref, a_ref, gpost_ref, wd_ref, h_ref, f_ref):
        k = pl.program_id(1)
        _accumulate(f_ref, _dot(a_ref[0], wd_ref[0]), k == 0)

        @pl.when(k == S - 1)
        def _():
            h_ref[...] = x_ref[...] + 0.5 * _rms_fwd(f_ref[...], gpost_ref[...])

    row = pl.BlockSpec((tm, D), lambda i, k: (i, 0))
    return pl.pallas_call(
        body, name=name, grid=(T // tm, S),
        in_specs=[row, pl.BlockSpec((1, tm, FS), lambda i, k: (k, i, 0)), pl.BlockSpec((1, D), lambda i, k: (0, 0)),
                  pl.BlockSpec((1, FS, D), lambda i, k: (k, 0, 0))],
        out_specs=[row, row],
        out_shape=[_out((T, D), F32), _out((T, D), F32)],
        compiler_params=_params(2, VMEM_LIMIT_WIDE),
    )(*_hbm(x, a, g_post, wd))


def _ffn_bwd_act(dh, f, g_post, wd, g_act, u_act, name):
    T, D = dh.shape
    S, FS, _ = wd.shape
    tm = min(WIDE_ROW_BLOCK, T)

    def body(dh_ref, f_ref, gpost_ref, wd_ref, g_ref, u_ref, dgp_ref, dup_ref, df_ref, dgain_ref, df_s):
        i, k = pl.program_id(0), pl.program_id(1)

        @pl.when(k == 0)
        def _():
            df, dgain = _rms_bwd(f_ref[...], gpost_ref[...], 0.5 * dh_ref[...])
            df_s[...] = df.astype(BF16)
            df_ref[...] = df_s[...]
            _accumulate(dgain_ref, dgain, i == 0)

        da = _dot(df_s[...], wd_ref[0], NT)
        g = g_ref[0]
        s = _sigmoid(g)
        dup_ref[0] = (da * (g * s)).astype(BF16)
        dgp_ref[0] = (da * u_ref[0] * (s * (1.0 + g * (1.0 - s)))).astype(BF16)

    row = pl.BlockSpec((tm, D), lambda i, k: (i, 0))
    vec = pl.BlockSpec((1, D), lambda i, k: (0, 0))
    act = pl.BlockSpec((1, tm, FS), lambda i, k: (k, i, 0))
    return pl.pallas_call(
        body, name=name, grid=(T // tm, S),
        in_specs=[row, row, vec, pl.BlockSpec((1, FS, D), lambda i, k: (k, 0, 0)), act, act],
        out_specs=[act, act, row, vec],
        out_shape=[_out((S, T, FS), BF16), _out((S, T, FS), BF16),
                   _out((T, D), BF16), _out((1, D), F32)],
        scratch_shapes=[pltpu.VMEM((tm, D), BF16)],
        compiler_params=_params(2, VMEM_LIMIT_WIDE),
    )(*_hbm(dh, f, g_post, wd, g_act, u_act))


def _proj_bwd(dys, ws, x, g_pre, dh, name):
    T, D = x.shape
    n = len(dys)
    S, N, _ = ws[0].shape
    tm = min(WIDE_ROW_BLOCK, T)

    def body(*refs):
        dy_refs, w_refs = refs[:n], refs[n:2 * n]
        x_ref, gpre_ref, dh_ref, dx_ref, dgain_ref, acc_s = refs[2 * n:]
        i, k = pl.program_id(0), pl.program_id(1)
        part = None
        for dy_ref, w_ref in zip(dy_refs, w_refs):
            term = _dot(dy_ref[0], w_ref[0])
            part = term if part is None else part + term
        _accumulate(acc_s, part, k == 0)

        @pl.when(k == S - 1)
        def _():
            dx, dgain = _rms_bwd(x_ref[...], gpre_ref[...], acc_s[...])
            dx_ref[...] = dh_ref[...] + dx
            _accumulate(dgain_ref, dgain, i == 0)

    row = pl.BlockSpec((tm, D), lambda i, k: (i, 0))
    vec = pl.BlockSpec((1, D), lambda i, k: (0, 0))
    return pl.pallas_call(
        body, name=name, grid=(T // tm, S),
        in_specs=[pl.BlockSpec((1, tm, N), lambda i, k: (k, i, 0))] * n
        + [pl.BlockSpec((1, N, D), lambda i, k: (k, 0, 0))] * n + [row, vec, row],
        out_specs=[row, vec],
        out_shape=[_out((T, D), F32), _out((1, D), F32)],
        scratch_shapes=[pltpu.VMEM((tm, D), F32)],
        compiler_params=_params(2, VMEM_LIMIT_WIDE),
    )(*_hbm(*dys, *ws, x, g_pre, dh))


def _mm_tn(a, b, bm, name):
    ga, T, M = a.shape
    gb, _, N = b.shape
    b_spec = pl.BlockSpec((1, T, N), (lambda g, m: (g, 0, 0)) if gb > 1 else (lambda g, m: (0, 0, 0)))
    G = max(ga, gb)

    def body(a_ref, b_ref, o_ref, narrow_ref):
        o_ref[0] = _dot(a_ref[0].astype(BF16), b_ref[0].astype(BF16), TN)
        narrow_ref[0] = o_ref[0].astype(BF16)

    out = pl.BlockSpec((1, bm, N), lambda g, m: (g, m, 0))
    return pl.pallas_call(
        body, name=name, grid=(G, M // bm),
        in_specs=[pl.BlockSpec((1, T, bm), (lambda g, m: (g, 0, m)) if ga > 1 else (lambda g, m: (0, 0, m))), b_spec],
        out_specs=[out, out],
        out_shape=[_out((G, M, N), F32), _out((G, M, N), BF16)],
        compiler_params=_params(2, VMEM_LIMIT),
    )(*_hbm(a, b))


QKV_PIECE = 256


def _qkv_shard(dy_refs, k, n_col):
    width = dy_refs[0].shape[1]
    parts = []
    for col in range(k * n_col, (k + 1) * n_col, QKV_PIECE):
        parts.append(dy_refs[col // width][:, col % width:col % width + QKV_PIECE])
    return jnp.concatenate(parts, axis=1)


def _qkv_bwd_in(dys, w, x, g_pre, dh, name):
    T, D = x.shape
    n = len(dys)
    S, _, N = w.shape
    tm = min(WIDE_ROW_BLOCK, T)

    def body(*refs):
        dy_refs = refs[:n]
        w_ref, x_ref, gpre_ref, dh_ref, dx_ref, dgain_ref, acc_s = refs[n:]
        i, k = pl.program_id(0), pl.program_id(1)
        for shard in range(S):
            @pl.when(k == shard)
            def _(shard=shard):
                part = _dot(_qkv_shard(dy_refs, shard, N), w_ref[0], NT)
                if shard == 0:
                    acc_s[...] = part
                else:
                    acc_s[...] += part

        @pl.when(k == S - 1)
        def _():
            dx, dgain = _rms_bwd(x_ref[...], gpre_ref[...], acc_s[...])
            dx_ref[...] = dh_ref[...] + dx
            _accumulate(dgain_ref, dgain, i == 0)

    row = pl.BlockSpec((tm, D), lambda i, k: (i, 0))
    vec = pl.BlockSpec((1, D), lambda i, k: (0, 0))
    return pl.pallas_call(
        body, name=name, grid=(T // tm, S),
        in_specs=[pl.BlockSpec((tm, dy.shape[1]), lambda i, k: (i, 0)) for dy in dys]
        + [pl.BlockSpec((1, D, N), lambda i, k: (k, 0, 0)), row, vec, row],
        out_specs=[row, vec],
        out_shape=[_out((T, D), F32), _out((1, D), F32)],
        scratch_shapes=[pltpu.VMEM((tm, D), F32)],
        compiler_params=_params(2, VMEM_LIMIT_WIDE),
    )(*_hbm(*dys, w, x, g_pre, dh))


def _dw_in(a, dys, n_col, bm, name):
    T, M = a.shape
    n = len(dys)
    S = n * dys[0].shape[1] // n_col

    def body(*refs):
        a_ref, dy_refs = refs[0], refs[1:1 + n]
        o_ref, narrow_ref = refs[1 + n:]
        k = pl.program_id(1)
        for shard in range(S):
            @pl.when(k == shard)
            def _(shard=shard):
                o_ref[0] = _dot(a_ref[...], _qkv_shard(dy_refs, shard, n_col), TN)
                narrow_ref[0] = o_ref[0].astype(BF16)

    out = pl.BlockSpec((1, bm, n_col), lambda m, k: (k, m, 0))
    return pl.pallas_call(
        body, name=name, grid=(M // bm, S),
        in_specs=[pl.BlockSpec((T, bm), lambda m, k: (0, m))]
        + [pl.BlockSpec((T, dy.shape[1]), lambda m, k: (0, 0)) for dy in dys],
        out_specs=[out, out],
        out_shape=[_out((S, M, n_col), F32), _out((S, M, n_col), BF16)],
        compiler_params=_params(2, VMEM_LIMIT_WIDE),
    )(*_hbm(a, *dys))


def _norm_proj(x, g_pre, w, name):
    T, D = x.shape
    S, _, N = w.shape
    tm = min(WIDE_ROW_BLOCK, T)

    def body(x_ref, g_ref, w_ref, o_ref, xn_ref, xn_s):
        @pl.when(pl.program_id(1) == 0)
        def _():
            xn_s[...] = _rms_fwd(x_ref[...], g_ref[...]).astype(BF16)
            xn_ref[...] = xn_s[...]

        o_ref[...] = _dot(xn_s[...], w_ref[0]).astype(BF16)

    row = pl.BlockSpec((tm, D), lambda i, k: (i, 0))
    return pl.pallas_call(
        body, name=name, grid=(T // tm, S),
        in_specs=[row, pl.BlockSpec((1, D), lambda i, k: (0, 0)), pl.BlockSpec((1, D, N), lambda i, k: (k, 0, 0))],
        out_specs=[pl.BlockSpec((tm, N), lambda i, k: (i, k)), row],
        out_shape=[_out((T, S * N), BF16), _out((T, D), BF16)],
        scratch_shapes=[pltpu.VMEM((tm, D), BF16)],
        compiler_params=_params(2, VMEM_LIMIT_WIDE),
    )(*_hbm(x, g_pre, w))


def _mix_out_fwd(h, o_a, o_b, g_sb, g_ch, w_out, g_post, name):
    T, D = h.shape
    W = g_sb.shape[1]
    tm = min(WIDE_ROW_BLOCK, T)

    def body(h_ref, oa_ref, ob_ref, gsb_ref, gch_ref, w_ref, gpost_ref, h2_ref, mixed_ref, mo_ref):
        mixed_ref[:, :W] = _rms_fwd(oa_ref[...], gsb_ref[...]).astype(BF16)
        mixed_ref[:, W:] = _rms_fwd(ob_ref[...], gch_ref[...]).astype(BF16)
        mo = _dot(mixed_ref[...], w_ref[...])
        mo_ref[...] = mo
        h2_ref[...] = h_ref[...] + _rms_fwd(mo, gpost_ref[...])

    row = pl.BlockSpec((tm, D), lambda i: (i, 0))
    part = pl.BlockSpec((tm, W), lambda i: (i, 0))
    half = pl.BlockSpec((1, W), lambda i: (0, 0))
    return pl.pallas_call(
        body, name=name, grid=(T // tm,),
        in_specs=[row, part, part, half, half, pl.BlockSpec((D, D), lambda i: (0, 0)), pl.BlockSpec((1, D), lambda i: (0, 0))],
        out_specs=[row, row, row],
        out_shape=[_out((T, D), F32), _out((T, D), BF16),
                   _out((T, D), F32)],
        compiler_params=_params(1, VMEM_LIMIT_WIDE),
    )(*_hbm(h, o_a, o_b, g_sb, g_ch, w_out, g_post))


def _mix_out_bwd(dh, mo, g_post, w_out, o_a, o_b, g_sb, g_ch, name):
    T, D = dh.shape
    W = g_sb.shape[1]
    tm = min(WIDE_ROW_BLOCK, T)

    def body(dh_ref, mo_ref, gpost_ref, w_ref, oa_ref, ob_ref, gsb_ref, gch_ref,
             dmo_ref, doa_ref, dob_ref, dgpost_ref, dgsb_ref, dgch_ref):
        first = pl.program_id(0) == 0
        dmo, dgpost = _rms_bwd(mo_ref[...], gpost_ref[...], dh_ref[...])
        dmo_ref[...] = dmo.astype(BF16)
        dmix = _dot(dmo_ref[...], w_ref[...], NT)
        doa_ref[...], dgsb = _rms_bwd(oa_ref[...], gsb_ref[...], dmix[:, :W])
        dob_ref[...], dgch = _rms_bwd(ob_ref[...], gch_ref[...], dmix[:, W:])
        _accumulate(dgpost_ref, dgpost, first)
        _accumulate(dgsb_ref, dgsb, first)
        _accumulate(dgch_ref, dgch, first)

    row = pl.BlockSpec((tm, D), lambda i: (i, 0))
    part = pl.BlockSpec((tm, W), lambda i: (i, 0))
    vec = pl.BlockSpec((1, D), lambda i: (0, 0))
    half = pl.BlockSpec((1, W), lambda i: (0, 0))
    return pl.pallas_call(
        body, name=name, grid=(T // tm,),
        in_specs=[row, row, vec, pl.BlockSpec((D, D), lambda i: (0, 0)), part, part, half, half],
        out_specs=[row, part, part, vec, half, half],
        out_shape=[_out((T, D), BF16), _out((T, W), F32),
                   _out((T, W), F32), _out((1, D), F32),
                   _out((1, W), F32), _out((1, W), F32)],
        compiler_params=_params(1, VMEM_LIMIT_WIDE),
    )(*_hbm(dh, mo, g_post, w_out, o_a, o_b, g_sb, g_ch))


def _ple_loss(h, p, target, w_proj, w_gate, g_post, name):
    T, D = h.shape
    P = p.shape[1]
    S = N_CHIPS
    C = D // S
    tm = min(ROW_BLOCK, T)

    def body(h_ref, p_ref, t_ref, wp_ref, wg_ref, g_ref, loss_ref, dh_ref, dproj_ref, dgate_ref, dgain_ref):
        first = pl.program_id(0) == 0
        h3 = h_ref[...]
        proj = _dot(p_ref[...].astype(BF16), wp_ref[...])
        s = _sigmoid(_dot(h3.astype(BF16), wg_ref[...]))
        e = proj * s
        diff = h3 + _rms_fwd(e, g_ref[...]) - t_ref[...]
        part = 0.5 * jnp.sum(jnp.mean(diff * diff, axis=-1, keepdims=True), axis=0, keepdims=True)
        _accumulate(loss_ref, jnp.broadcast_to(part, loss_ref.shape), first)
        dy = diff * (1.0 / D)
        de, dgain = _rms_bwd(e, g_ref[...], dy)
        _accumulate(dgain_ref, dgain, first)
        dproj = (de * s).astype(BF16)
        for j in range(S):
            dproj_ref[j] = dproj[:, j * C:(j + 1) * C]
        dgate_ref[...] = (de * proj * s * (1.0 - s)).astype(BF16)
        dh_ref[...] = dy + _dot(dgate_ref[...], wg_ref[...], NT)

    row = pl.BlockSpec((tm, D), lambda i: (i, 0))
    vec = pl.BlockSpec((1, D), lambda i: (0, 0))
    return pl.pallas_call(
        body, name=name, grid=(T // tm,),
        in_specs=[row, pl.BlockSpec((tm, P), lambda i: (i, 0)), row,
                  pl.BlockSpec((P, D), lambda i: (0, 0)), pl.BlockSpec((D, D), lambda i: (0, 0)), vec],
        out_specs=[pl.BlockSpec((8, 128), lambda i: (0, 0)), row,
                   pl.BlockSpec((S, tm, C), lambda i: (0, i, 0)), row, vec],
        out_shape=[_out((8, 128), F32), _out((T, D), F32),
                   _out((S, T, C), BF16), _out((T, D), BF16),
                   _out((1, D), F32)],
        compiler_params=_params(1, VMEM_LIMIT_WIDE),
    )(*_hbm(h, p, target, w_proj, w_gate, g_post))


def _sb_scores(q, kj, mask):
    z = _dot(q, kj, NT)
    sp = jnp.maximum(z, 0.0) + jnp.log(1.0 + jnp.exp(-jnp.abs(z)))
    return z, sp if mask is None else jnp.where(mask, sp, 0.0)


def _strict_causal():
    rows = lax.broadcasted_iota(jnp.int32, (SB_BLOCK, SB_BLOCK), 0)
    cols = lax.broadcasted_iota(jnp.int32, (SB_BLOCK, SB_BLOCK), 1)
    return cols < rows


def _tri(cmp):
    r = lax.broadcasted_iota(jnp.int32, (2 * SB_BLOCK, SB_BLOCK), 0) % SB_BLOCK
    c = lax.broadcasted_iota(jnp.int32, (2 * SB_BLOCK, SB_BLOCK), 1)
    return jnp.where(cmp(r, c), 1.0, 0.0).astype(BF16)


def _cum(x, tri):
    return _dot(jnp.concatenate(_split2(x), axis=1), tri)


def _pair_lanes():
    lane = lax.broadcasted_iota(jnp.int32, (1, PAIR), 1)
    return [lane < HEAD_DIM, lane >= HEAD_DIM]


def _only(lanes, x):
    return jnp.where(lanes, x, jnp.zeros_like(x))


def _sb_fwd(qkv, name):
    T = qkv.shape[0]
    B = SB_BLOCK
    W = SB_FWD_PAIRS * PAIR
    steps = N_HEADS // (2 * SB_FWD_PAIRS)
    heads = [(p, h) for p in range(SB_FWD_PAIRS) for h in range(2)]

    def body(q_ref, k_ref, v_ref, o_ref):
        i = pl.program_id(1)
        after = _tri(lambda r, c: r > c)
        lanes = _pair_lanes()
        cols = [slice(p * PAIR, (p + 1) * PAIR) for p in range(SB_FWD_PAIRS)]
        q = {(p, h): _only(lanes[h], q_ref[:, cols[p]] * ATT_SCALE) for p, h in heads}

        def tiles(j, carries, mask):
            at = pl.ds(pl.multiple_of(j * B, B), B)
            scores = [_sb_scores(q[ph], k_ref[at, cols[ph[0]]], mask) for ph in heads]
            laters = [_cum(sp, after) for _, sp in scores]
            out = []
            for ph, (z, sp), later, (run, acc) in zip(heads, scores, laters, carries):
                a = jnp.exp(z - sp - later - run)
                if mask is not None:
                    a = jnp.where(mask, a, 0.0)
                out.append((run + later[:, 0:1] + sp[:, 0:1],
                            acc + _dot(a.astype(BF16), _only(lanes[ph[1]], v_ref[at, cols[ph[0]]]))))
            return tuple(out)

        zero = (jnp.zeros((B, 1), F32), jnp.zeros((B, PAIR), F32))
        carries = tiles(i, (zero,) * len(heads), _strict_causal())
        carries = lax.fori_loop(0, i, lambda jj, cs: tiles(i - 1 - jj, cs, None), carries)
        for p in range(SB_FWD_PAIRS):
            o_ref[:, cols[p]] = carries[2 * p][1] + carries[2 * p + 1][1]

    blk = lambda off: pl.BlockSpec((B, W), lambda g, i: (i, g + off))
    full = lambda off: pl.BlockSpec((T, W), lambda g, i: (0, g + off))
    return pl.pallas_call(
        body, name=name, grid=(steps, T // B),
        in_specs=[blk(0), full(steps), full(2 * steps)],
        out_specs=blk(0),
        out_shape=_out((T, N_HEADS * HEAD_DIM), F32),
        compiler_params=_params(2, VMEM_LIMIT),
    )(*_hbm(qkv, qkv, qkv))


def _sb_bwd(qkv, do, o, after, name):
    T = qkv.shape[0]
    B = SB_BLOCK
    W = SB_PAIRS * PAIR
    steps = N_HEADS // (2 * SB_PAIRS)
    n_blocks = T // B
    heads = [(p, h) for p in range(SB_PAIRS) for h in range(2)]

    def body(q_ref, k_ref, v_ref, do_ref, o_ref, dq_ref, dk_ref, dv_ref, dk_s, dv_s):
        i = pl.program_id(1)

        @pl.when(i == 0)
        def _():
            dk_s[...] = jnp.zeros_like(dk_s)
            dv_s[...] = jnp.zeros_like(dv_s)

        after = _tri(lambda r, c: r > c)
        since = _tri(lambda r, c: r >= c)
        lanes = _pair_lanes()
        cols = [slice(p * PAIR, (p + 1) * PAIR) for p in range(SB_PAIRS)]
        q = {(p, h): _only(lanes[h], q_ref[:, cols[p]] * ATT_SCALE) for p, h in heads}
        do = {(p, h): _only(lanes[h], do_ref[:, cols[p]].astype(BF16)) for p, h in heads}
        total = {ph: jnp.sum(do[ph].astype(F32) * o_ref[:, cols[ph[0]]], axis=1, keepdims=True) for ph in heads}

        def tiles(j, carries, mask):
            at = pl.ds(pl.multiple_of(j * B, B), B)
            ks = [k_ref[at, c] for c in cols]
            vs = [v_ref[at, c] for c in cols]
            scores = [_sb_scores(q[ph], ks[ph[0]], mask) for ph in heads]
            laters = [_cum(sp, after) for _, sp in scores]
            das = [_dot(do[ph], vs[ph[0]], NT) for ph in heads]
            a_s, gs = [], []
            for (z, sp), later, da, carry in zip(scores, laters, das, carries):
                a = jnp.exp(z - sp - later - carry[0])
                if mask is not None:
                    a = jnp.where(mask, a, 0.0)
                a = a.astype(BF16)
                a_s.append(a)
                gs.append(a.astype(F32) * da)
            sinces = [_cum(g, since) for g in gs]
            dzs = []
            for ph, (_, sp), g, from_s, carry in zip(heads, scores, gs, sinces, carries):
                g_before = total[ph] - carry[1] - from_s
                fail = jnp.exp(-sp)
                dz = fail * (g + g_before) - g_before
                if mask is not None:
                    dz = jnp.where(mask, dz, 0.0)
                dzs.append(dz.astype(BF16))
            out = []
            for ph, (_, sp), a, dz, later, from_s, carry in zip(heads, scores, a_s, dzs, laters, sinces, carries):
                dk_s[at, cols[ph[0]]] += _dot(dz, q[ph], TN)
                dv_s[at, cols[ph[0]]] += _dot(a, do[ph], TN)
                out.append((carry[0] + later[:, 0:1] + sp[:, 0:1], carry[1] + from_s[:, 0:1],
                            carry[2] + _dot(dz, _only(lanes[ph[1]], ks[ph[0]]))))
            return tuple(out)

        col = jnp.zeros((B, 1), F32)
        zero = (col, col, jnp.zeros((B, PAIR), F32))
        carries = tiles(i, (zero,) * len(heads), _strict_causal())
        last = lax.fori_loop(0, i, lambda jj, cs: tiles(i - 1 - jj, cs, None), carries)
        for p in range(SB_PAIRS):
            dq_ref[:, cols[p]] = ((last[2 * p][2] + last[2 * p + 1][2]) * ATT_SCALE).astype(BF16)

        @pl.when(i == n_blocks - 1)
        def _():
            dk_ref[...] = dk_s[...].astype(BF16)
            dv_ref[...] = dv_s[...].astype(BF16)

    blk = lambda off: pl.BlockSpec((B, W), lambda g, i: (i, g + off))
    full = lambda off: pl.BlockSpec((T, W), lambda g, i: (0, g + off))
    out = _out((T, N_HEADS * HEAD_DIM), BF16)
    return pl.pallas_call(
        lambda after_ref, *refs: body(*refs), name=name, grid=(steps, n_blocks),
        in_specs=[ANY, blk(0), full(steps), full(2 * steps), blk(0), blk(0)],
        out_specs=[blk(0), full(0), full(0)],
        out_shape=[out, out, out],
        scratch_shapes=[pltpu.VMEM((T, W), F32)] * 2,
        compiler_params=_params(2, VMEM_LIMIT),
    )(after, *_hbm(qkv, qkv, qkv, do, o))


NEAR = BAND - PAD + REL_CLIP
FAR = BAND - NEAR
NEAR_REL = 2 * REL_CLIP
BIAS_ROWS = 8


def _rel_onehot(i, transposed):
    shape = (NEAR, NEAR_REL) if transposed else (NEAR_REL, NEAR)
    j = FAR + lax.broadcasted_iota(jnp.int32, shape, 0 if transposed else 1)
    r = lax.broadcasted_iota(jnp.int32, shape, 1 if transposed else 0)
    idx = jnp.clip(i + PAD - j, -REL_CLIP, REL_CLIP) + REL_CLIP
    return jnp.where(idx - 1 == r, 1.0, 0.0).astype(BF16)


def _bias_table(rel_bias, name):
    def body(near_ref, far_ref, o_ref):
        rb = near_ref[...]
        hi, lo = _split2(rb)
        lo2 = (rb - hi.astype(F32) - lo.astype(F32)).astype(BF16)
        far = jnp.broadcast_to(far_ref[...], (N_HEADS, FAR))
        for k in range(BIAS_ROWS):
            onehot = _rel_onehot(pl.program_id(0) * BIAS_ROWS + k, False)
            o_ref[k, :, :FAR] = far
            o_ref[k, :, FAR:] = _dot(hi, onehot) + _dot(lo, onehot) + _dot(lo2, onehot)

    return pl.pallas_call(
        body, name=name, grid=(CHUNK // BIAS_ROWS,),
        in_specs=[pl.BlockSpec((N_HEADS, NEAR_REL), lambda i: (0, 0)), pl.BlockSpec((N_HEADS, 1), lambda i: (0, 0))],
        out_specs=pl.BlockSpec((BIAS_ROWS, N_HEADS, BAND), lambda i: (i, 0, 0)),
        out_shape=_out((CHUNK, N_HEADS, BAND), F32),
        compiler_params=_params(1),
    )(*_hbm(rel_bias[:, 1:], rel_bias[:, N_REL - 1:]))


def _bias_grad(dbias_t, name):
    def body(d_ref, near_ref, far_ref):
        near, far = None, None
        for k in range(BIAS_ROWS):
            onehot = _rel_onehot(pl.program_id(0) * BIAS_ROWS + k, True)
            hi, lo = _split2(d_ref[k, :, FAR:])
            part = _dot(hi, onehot) + _dot(lo, onehot)
            rest = jnp.sum(d_ref[k, :, :FAR], axis=1, keepdims=True)
            near, far = (part, rest) if near is None else (near + part, far + rest)
        first = pl.program_id(0) == 0
        _accumulate(near_ref, near, first)
        _accumulate(far_ref, jnp.broadcast_to(far, far_ref.shape), first)

    near, far = pl.pallas_call(
        body, name=name, grid=(CHUNK // BIAS_ROWS,),
        in_specs=[pl.BlockSpec((BIAS_ROWS, N_HEADS, BAND), lambda i: (i, 0, 0))],
        out_specs=[pl.BlockSpec((N_HEADS, NEAR_REL), lambda i: (0, 0)), pl.BlockSpec((N_HEADS, 128), lambda i: (0, 0))],
        out_shape=[_out((N_HEADS, NEAR_REL), F32), _out((N_HEADS, 128), F32)],
        compiler_params=_params(1),
    )(*_hbm(dbias_t))
    return jnp.pad(near, ((0, 0), (1, 0))).at[:, N_REL - 1].add(far[:, 0])


def _ch_probs(scores, bias, valid):
    z = jnp.where(valid, scores * ATT_SCALE + bias, NEG_INF)
    e = jnp.exp(z - jnp.max(z, axis=-1, keepdims=True))
    return e / jnp.sum(e, axis=-1, keepdims=True)


CH_HEADS = [(pair, h) for pair in range(N_HEADS // 2) for h in range(2)]
CH_COLS = [slice(pair * PAIR, (pair + 1) * PAIR) for pair in range(N_HEADS // 2)]


CH_GROUP = 4
ONCE = pl.Buffered(1)
CH_Q = CH_GROUP * CHUNK
CH_WIN = (LOOKBACK + CH_GROUP) * CHUNK


def _ch_valid(n):
    row_chunk = lax.broadcasted_iota(jnp.int32, (CH_Q, CH_WIN), 0) // CHUNK
    slot = lax.broadcasted_iota(jnp.int32, (CH_Q, CH_WIN), 1)
    ahead = slot // CHUNK - row_chunk
    return (ahead >= 0) & (ahead <= LOOKBACK) & (n * CH_Q + slot >= PAD)


def _ch_group_bias(bias):
    shifted = [jnp.pad(bias, ((0, 0), (0, 0), (c * CHUNK, (CH_GROUP - 1 - c) * CHUNK))) for c in range(CH_GROUP)]
    return jnp.concatenate(shifted, axis=1)


def _ch_fold_bias_grad(dbias):
    parts = [dbias[:, c * CHUNK:(c + 1) * CHUNK, c * CHUNK:c * CHUNK + BAND] for c in range(CH_GROUP)]
    return sum(parts[1:], parts[0])


def _ch_fwd(qkv, bias, name):
    T = qkv.shape[0]
    W = N_HEADS * HEAD_DIM

    def body(q_ref, k_ref, v_ref, b_ref, o_ref, kp, vp):
        n = pl.program_id(0)

        @pl.when(n == 0)
        def _():
            _ch_load_padded(k_ref, v_ref, kp, vp)

        win = pl.ds(pl.multiple_of(n * CH_Q, CH_Q), CH_WIN)
        valid = _ch_valid(n)
        lanes = _pair_lanes()
        scores = [_dot(_only(lanes[h], q_ref[:, CH_COLS[pair]]), kp[win, CH_COLS[pair]], NT) for pair, h in CH_HEADS]
        probs = [_ch_probs(s, b_ref[2 * pair + h], valid).astype(BF16) for s, (pair, h) in zip(scores, CH_HEADS)]
        outs = [_dot(p, _only(lanes[h], vp[win, CH_COLS[pair]])) for p, (pair, h) in zip(probs, CH_HEADS)]
        for pair, cols in enumerate(CH_COLS):
            o_ref[:, cols] = outs[2 * pair] + outs[2 * pair + 1]

    full = lambda col: pl.BlockSpec((T, W), lambda n: (0, col), pipeline_mode=ONCE)
    return pl.pallas_call(
        body, name=name, grid=(T // CH_Q,),
        in_specs=[pl.BlockSpec((CH_Q, W), lambda n: (n, 3)), full(4), full(5),
                  pl.BlockSpec((N_HEADS, CH_Q, CH_WIN), lambda n: (0, 0, 0), pipeline_mode=ONCE)],
        out_specs=pl.BlockSpec((CH_Q, W), lambda n: (n, 0)),
        out_shape=_out((T, W), F32),
        scratch_shapes=[pltpu.VMEM((PAD + T, W), BF16)] * 2,
        compiler_params=_params(1, VMEM_LIMIT_WIDE),
    )(*_hbm(qkv, qkv, qkv, bias))


def _ch_load_padded(k_ref, v_ref, kp, vp):
    for src, dst in ((k_ref, kp), (v_ref, vp)):
        dst[:PAD, :] = jnp.zeros((PAD, dst.shape[1]), dst.dtype)
        dst[PAD:, :] = src[...]


def _ch_bwd(qkv, bias, do, after, name):
    T = qkv.shape[0]
    W = N_HEADS * HEAD_DIM
    n_chunks = T // CH_Q

    def body(q_ref, k_ref, v_ref, b_ref, do_ref, dq_ref, dk_ref, dv_ref, db_ref, kp, vp, dk_s, dv_s):
        n = pl.program_id(0)

        @pl.when(n == 0)
        def _():
            _ch_load_padded(k_ref, v_ref, kp, vp)
            dk_s[...] = jnp.zeros_like(dk_s)
            dv_s[...] = jnp.zeros_like(dv_s)
            db_ref[...] = jnp.zeros_like(db_ref)

        win = pl.ds(pl.multiple_of(n * CH_Q, CH_Q), CH_WIN)
        valid = _ch_valid(n)
        lanes = _pair_lanes()
        kws = [kp[win, cols] for cols in CH_COLS]
        vws = [vp[win, cols] for cols in CH_COLS]
        qs = [_only(lanes[h], q_ref[:, CH_COLS[pair]]) for pair, h in CH_HEADS]
        dos = [_only(lanes[h], do_ref[:, CH_COLS[pair]].astype(BF16)) for pair, h in CH_HEADS]
        scores = [_dot(q, kws[pair], NT) for q, (pair, _) in zip(qs, CH_HEADS)]
        dps = [_dot(do, vws[pair], NT) for do, (pair, _) in zip(dos, CH_HEADS)]
        probs = [_ch_probs(s, b_ref[2 * pair + h], valid) for s, (pair, h) in zip(scores, CH_HEADS)]
        dzs = [p * (dp - jnp.sum(dp * p, axis=-1, keepdims=True)) for p, dp in zip(probs, dps)]
        for k, dz in enumerate(dzs):
            db_ref[k] += dz
        dzbs = [(dz * ATT_SCALE).astype(BF16) for dz in dzs]
        dqs = [_dot(dz, _only(lanes[h], kws[pair])) for dz, (pair, h) in zip(dzbs, CH_HEADS)]
        dks = [_dot(dz, q, TN) for dz, q in zip(dzbs, qs)]
        dvs = [_dot(p.astype(BF16), do, TN) for p, do in zip(probs, dos)]
        for pair, cols in enumerate(CH_COLS):
            dq_ref[:, cols] = (dqs[2 * pair] + dqs[2 * pair + 1]).astype(BF16)
            dk_s[win, cols] += dks[2 * pair] + dks[2 * pair + 1]
            dv_s[win, cols] += dvs[2 * pair] + dvs[2 * pair + 1]

        @pl.when(n == n_chunks - 1)
        def _():
            dk_ref[...] = dk_s[PAD:, :].astype(BF16)
            dv_ref[...] = dv_s[PAD:, :].astype(BF16)

    full = lambda col: pl.BlockSpec((T, W), lambda n: (0, col), pipeline_mode=ONCE)
    blk = lambda col: pl.BlockSpec((CH_Q, W), lambda n: (n, col))
    tab = pl.BlockSpec((N_HEADS, CH_Q, CH_WIN), lambda n: (0, 0, 0), pipeline_mode=ONCE)
    out = _out((T, W), BF16)
    return pl.pallas_call(
        lambda after_ref, *refs: body(*refs), name=name, grid=(n_chunks,),
        in_specs=[ANY, blk(3), full(4), full(5), tab, blk(0)],
        out_specs=[blk(0), full(0), full(0), tab],
        out_shape=[out, out, out, _out((N_HEADS, CH_Q, CH_WIN), F32)],
        scratch_shapes=[pltpu.VMEM((PAD + T, W), BF16)] * 2 + [pltpu.VMEM((PAD + T, W), F32)] * 2,
        compiler_params=_params(1, VMEM_LIMIT_WIDE),
    )(after, *_hbm(qkv, qkv, qkv, bias, do))


def _rows_split(a, parts):
    return a.reshape(a.shape[:-2] + (parts, a.shape[-2] // parts, a.shape[-1]))


def _cast_into_own_slot(me, c, ws, in_chip_order, name):
    parts = 2
    ws = [_rows_split(_rows_split(w, 2), parts) for w in ws]
    n = len(ws)

    def body(me_ref, c_ref, *refs):
        for src, dst in zip(refs[:n], refs[n:]):
            dst[0, 0, 0] = src[0, 0].astype(BF16)

    def specs(w, plain):
        block = (1, 1) + w.shape[2:]
        if plain:
            return (pl.BlockSpec(block, lambda d, r, me_ref, c_ref: (d, r, 0, 0)),
                    pl.BlockSpec((1,) + block, lambda d, r, me_ref, c_ref: (me_ref[0], d, r, 0, 0)))
        return (pl.BlockSpec(block, lambda d, r, me_ref, c_ref: (d ^ c_ref[0], r, 0, 0)),
                pl.BlockSpec((1,) + block, lambda d, r, me_ref, c_ref: (0, d, r, 0, 0)))

    both = [specs(w, plain) for w, plain in zip(ws, in_chip_order)]
    outs = pl.pallas_call(
        body, name=name,
        grid_spec=pltpu.PrefetchScalarGridSpec(
            num_scalar_prefetch=2, grid=(2, parts),
            in_specs=[s[0] for s in both], out_specs=[s[1] for s in both]),
        out_shape=[_out((N_CHIPS,) + w.shape, BF16) for w in ws],
        compiler_params=_params(2, VMEM_LIMIT),
    )(me, c, *_hbm(*ws))
    return [o.reshape(N_CHIPS, 2, o.shape[2] * o.shape[3], o.shape[4]) for o in outs]


def _zone_slots(in_chip_order):
    x, y, c, _ = _place()
    me = 2 * x + y
    if in_chip_order:
        return (me, c), (lambda r: (me, c)), (lambda r: (me ^ r, c)), (lambda r: (me ^ r, c))
    return (0, 0), (lambda r: (r, 0)), (lambda r: (r, 0)), (lambda r: (r, 1))


def _pair_add(c, mine, got, permuted, name):
    parts = 2
    mine = [_rows_split(m, parts) for m in mine]
    got = [_rows_split(g, parts) for g in got]
    n = len(mine)

    def body(c_ref, *refs):
        for a, b, o in zip(refs[:n], refs[n:2 * n], refs[2 * n:]):
            o[0, 0] = (a[0, 0, 0] + b[0, 0].astype(F32)).astype(BF16)

    def mine_spec(m, perm):
        if perm:
            return pl.BlockSpec((1, 1, 1) + m.shape[3:], lambda j, r, c_ref: (j, 0, r, 0, 0))
        return pl.BlockSpec((1, 1, 1) + m.shape[3:], lambda j, r, c_ref: (j, c_ref[0], r, 0, 0))

    def got_spec(g):
        return pl.BlockSpec((1, 1) + g.shape[2:], lambda j, r, c_ref: (j, r, 0, 0))

    outs = pl.pallas_call(
        body, name=name,
        grid_spec=pltpu.PrefetchScalarGridSpec(
            num_scalar_prefetch=1, grid=(N_CHIPS, parts),
            in_specs=[mine_spec(m, perm) for m, perm in zip(mine, permuted)] + [got_spec(g) for g in got],
            out_specs=[got_spec(g) for g in got]),
        out_shape=[_out(g.shape, BF16) for g in got],
        compiler_params=_params(2, VMEM_LIMIT),
    )(c, *_hbm(*mine, *got))
    return [o.reshape(o.shape[0], o.shape[1] * o.shape[2], o.shape[3]) for o in outs]


def _chip_add(me, partials, landed, permuted, name):
    parts = 2
    ps = [_rows_split(x, parts) for x in partials]
    ls = [_rows_split(x, parts) for x in landed]
    n = len(ps)

    def body(me_ref, *refs):
        for own, got, o in zip(refs[:n], refs[n:2 * n], refs[2 * n:]):
            acc = own[0, 0].astype(F32)
            for r in range(N_CHIPS - 1):
                acc = acc + got[r, 0].astype(F32)
            o[0] = acc

    def own_spec(x, perm):
        if perm:
            return pl.BlockSpec((1, 1) + x.shape[2:], lambda r, me_ref: (0, r, 0, 0))
        return pl.BlockSpec((1, 1) + x.shape[2:], lambda r, me_ref: (me_ref[0], r, 0, 0))

    outs = pl.pallas_call(
        body, name=name,
        grid_spec=pltpu.PrefetchScalarGridSpec(
            num_scalar_prefetch=1, grid=(parts,),
            in_specs=[own_spec(x, perm) for x, perm in zip(ps, permuted)]
            + [pl.BlockSpec((N_CHIPS - 1, 1) + x.shape[2:], lambda r, me_ref: (0, r, 0, 0)) for x in ls],
            out_specs=[pl.BlockSpec((1,) + x.shape[2:], lambda r, me_ref: (r, 0, 0)) for x in ps]),
        out_shape=[_out(x.shape[1:], F32) for x in ps],
        compiler_params=_params(1, VMEM_LIMIT),
    )(me, *_hbm(*ps, *ls))
    return [o.reshape(o.shape[0] * o.shape[1], o.shape[2]) for o in outs]


def _adamw_math(w, g, m, v):
    m = ADAM_B1 * m + (1.0 - ADAM_B1) * g
    v = ADAM_B2 * v + (1.0 - ADAM_B2) * (g * g)
    m_hat = m / (1.0 - ADAM_B1 ** ADAM_STEP)
    v_hat = v / (1.0 - ADAM_B2 ** ADAM_STEP)
    delta = -ADAM_LR * (m_hat / (jnp.sqrt(v_hat) + ADAM_EPS) + ADAM_WD * w)
    return delta, m, v


def _adamw(ws, gs, ms, vs, parts, name):
    n = len(ws)
    flat = [_rows_split(a, parts) for a in (*ws, *gs, *ms, *vs)]

    def body(*refs):
        ins, outs = refs[:4 * n], refs[4 * n:]
        for k in range(n):
            d, m, v = _adamw_math(ins[k][...], ins[n + k][...], ins[2 * n + k][...], ins[3 * n + k][...])
            outs[k][...] = d
            outs[n + k][...] = m
            outs[2 * n + k][...] = v

    spec = lambda a: pl.BlockSpec((1,) + a.shape[1:], lambda i: (i, 0, 0))
    outs = pl.pallas_call(
        body, name=name, grid=(parts,),
        in_specs=[spec(a) for a in flat], out_specs=[spec(a) for a in flat[:n]] * 3,
        out_shape=[_out(a.shape, F32) for a in flat[:n]] * 3,
        compiler_params=_params(1, VMEM_LIMIT),
    )(*_hbm(*flat))
    outs = [o.reshape(o.shape[0] * o.shape[1], o.shape[2]) for o in outs]
    return outs[:n], outs[n:2 * n], outs[2 * n:]


def _adamw_halves(c, ws, owns, others, ms, vs, name):
    parts = 4
    n = len(ws)
    whole = [_rows_split(_rows_split(a, 2), parts) for a in (*ws, *ms, *vs)]
    halves = [_rows_split(a, parts) for a in (*owns, *others)]

    def body(c_ref, *refs):
        ins, outs = refs[:5 * n], refs[5 * n:]
        mine = pl.program_id(0) == c_ref[0]
        for k in range(n):
            g = jnp.where(mine, ins[3 * n + k][0], ins[4 * n + k][0])
            d, m, v = _adamw_math(ins[k][0, 0], g, ins[n + k][0, 0], ins[2 * n + k][0, 0])
            for slot, val in enumerate((g, d, m, v)):
                outs[slot * n + k][0, 0] = val

    wspec = lambda a: pl.BlockSpec((1, 1) + a.shape[2:], lambda h, r, c_ref: (h, r, 0, 0))
    hspec = lambda a: pl.BlockSpec((1,) + a.shape[1:], lambda h, r, c_ref: (r, 0, 0))
    outs = pl.pallas_call(
        body, name=name,
        grid_spec=pltpu.PrefetchScalarGridSpec(
            num_scalar_prefetch=1, grid=(2, parts),
            in_specs=[wspec(a) for a in whole] + [hspec(a) for a in halves],
            out_specs=[wspec(a) for a in whole[:n]] * 4),
        out_shape=[_out(a.shape, F32) for a in whole[:n]] * 4,
        compiler_params=_params(2, VMEM_LIMIT),
    )(c, *_hbm(*whole, *halves))
    outs = [o.reshape(2 * parts * o.shape[2], o.shape[3]) for o in outs]
    return outs[:n], outs[n:2 * n], outs[2 * n:3 * n], outs[3 * n:]


def _place():
    x, y, c = lax.axis_index("x"), lax.axis_index("y"), lax.axis_index("c")
    peers = [(x ^ (r >> 1), y ^ (r & 1), c) for r in (1, 2, 3)]
    return x, y, c, peers


def _handshake(peers):
    barrier = pltpu.get_barrier_semaphore()
    for peer in peers:
        pl.semaphore_signal(barrier, inc=1, device_id=peer, device_id_type=MESH)
    pl.semaphore_wait(barrier, len(peers))


ANY = pl.BlockSpec(memory_space=pl.ANY)
HBM = pl.BlockSpec(memory_space=pltpu.HBM)
SEM = pl.BlockSpec(memory_space=pltpu.SEMAPHORE)
SPLIT_COPY = pltpu.SideEffectType.DATAFLOW_SIDE_EFFECTING


def _split_start(body, name, collective_id, operands, n_sems, after=None):
    n = len(operands)
    extra = [] if after is None else [after]

    def wrapped(*refs):
        at = n + len(extra)
        body(refs[:n], refs[at], refs[at + 1])
        token = refs[-1]
        token[...] = jnp.zeros_like(token)

    outs = pl.pallas_call(
        wrapped, name=name,
        in_specs=[HBM] * n + [ANY] * len(extra),
        out_shape=(pltpu.SemaphoreType.DMA((n_sems,)), pltpu.SemaphoreType.DMA((n_sems,)),
                   *[pltpu.HBM(a.shape, a.dtype) for a in operands], jax.ShapeDtypeStruct((8, 128), F32)),
        out_specs=(SEM, SEM, *[HBM] * n, pl.BlockSpec(memory_space=pltpu.VMEM)),
        input_output_aliases={i: 2 + i for i in range(n)},
        compiler_params=pltpu.CompilerParams(has_side_effects=SPLIT_COPY, collective_id=collective_id),
    )(*_hbm(*operands), *extra)
    return outs[0], outs[1], list(outs[2:2 + n]), outs[-1]


def _split_wait(body, name, send_sem, recv_sem, operands, after):
    n = len(operands)

    def wrapped(*refs):
        body(refs[:n], refs[n], refs[n + 1])

    outs = pl.pallas_call(
        wrapped, name=name,
        in_specs=[HBM] * n + [SEM, SEM, ANY],
        out_shape=tuple(pltpu.HBM(a.shape, a.dtype) for a in operands),
        out_specs=tuple([HBM] * n),
        input_output_aliases={i: i for i in range(n)},
        compiler_params=pltpu.CompilerParams(has_side_effects=SPLIT_COPY),
    )(*operands, send_sem, recv_sem, after)
    return list(outs)


def _gather_copies(lands, in_chip_order, send_sem, recv_sem):
    peers = _place()[3]
    copies = []
    for a, (land, plain) in enumerate(zip(lands, in_chip_order)):
        own, sent_to, _, _ = _zone_slots(plain)
        copies += [pltpu.make_async_remote_copy(
            src_ref=land.at[own], dst_ref=land.at[sent_to(r + 1)],
            send_sem=send_sem.at[a * 3 + r], recv_sem=recv_sem.at[a * 3 + r],
            device_id=peers[r], device_id_type=MESH) for r in range(3)]
    return copies


def _gather_start(lands, in_chip_order, name, collective_id, after):
    def body(refs, send_sem, recv_sem):
        _handshake(_place()[3])
        for cp in _gather_copies(refs, in_chip_order, send_sem, recv_sem):
            cp.start()

    return _split_start(body, name, collective_id, list(lands), 3 * len(lands), after)


def _gather_wait(send_sem, recv_sem, operands, in_chip_order, after, name):
    def body(refs, send_sem, recv_sem):
        for cp in _gather_copies(refs, in_chip_order, send_sem, recv_sem):
            cp.wait_send()
            cp.wait_recv()

    return _split_wait(body, name, send_sem, recv_sem, operands, after)


def _gather_finish(lands, in_chip_order, with_ici, name):
    n = len(lands)

    def body(*refs):
        land = refs[n:2 * n]
        send_ici, recv_ici, send_d2d, recv_d2d = refs[2 * n:]
        x, y, c, _ = _place()
        ici = _gather_copies(land, in_chip_order, send_ici, recv_ici) if with_ici else []
        for cp in ici:
            cp.start()
        passed = []
        for a in range(n):
            _, _, received, kept = _zone_slots(in_chip_order[a])
            passed += [pltpu.make_async_remote_copy(
                src_ref=land[a].at[received(r + 1)], dst_ref=land[a].at[kept(r + 1)],
                send_sem=send_d2d.at[a * 3 + r], recv_sem=recv_d2d.at[a * 3 + r],
                device_id=(x, y, 1 - c), device_id_type=MESH) for r in range(3)]
        for k, cp in enumerate(passed):
            if with_ici:
                ici[k].wait_recv()
            cp.start()
        for cp in passed:
            cp.wait_recv()
        for cp in ici:
            cp.wait_send()
        for cp in passed:
            cp.wait_send()

    outs = pl.pallas_call(
        body, name=name,
        in_specs=[ANY] * n, out_specs=[ANY] * n,
        out_shape=[_out(l.shape, l.dtype) for l in lands],
        input_output_aliases={a: a for a in range(n)},
        scratch_shapes=[pltpu.SemaphoreType.DMA((3 * n,))] * 4,
    )(*lands)
    return list(outs)


def _slabs(land):
    return land.reshape(N_CHIPS, 2 * land.shape[2], land.shape[3])


def _pair_swap(grads, permuted, name):
    n = len(grads)

    def body(*refs):
        src, dst = refs[:n], refs[n:2 * n]
        send_sem, recv_sem = refs[2 * n:]
        x, y, c, _ = _place()
        copies = [pltpu.make_async_remote_copy(
            src_ref=src[a].at[:, 1] if permuted[a] else src[a].at[:, 1 - c], dst_ref=dst[a],
            send_sem=send_sem.at[a], recv_sem=recv_sem.at[a],
            device_id=(x, y, 1 - c), device_id_type=MESH) for a in range(n)]
        for cp in copies:
            cp.start()
        for cp in copies:
            cp.wait()

    return pl.pallas_call(
        body, name=name,
        in_specs=[ANY] * n, out_specs=[ANY] * n,
        out_shape=[_out((N_CHIPS,) + g.shape[2:], g.dtype) for g in grads],
        scratch_shapes=[pltpu.SemaphoreType.DMA((n,))] * 2,
    )(*grads)


def _swap_copies(refs, permuted, send_sem, recv_sem):
    n = len(refs) // 2
    x, y, c, _ = _place()
    return [pltpu.make_async_remote_copy(
        src_ref=refs[a].at[:, 1] if permuted[a] else refs[a].at[:, 1 - c], dst_ref=refs[n + a],
        send_sem=send_sem.at[a], recv_sem=recv_sem.at[a],
        device_id=(x, y, 1 - c), device_id_type=MESH) for a in range(n)]


def _pair_swap_start(grads, permuted, name, collective_id):
    def body(refs, send_sem, recv_sem):
        x, y, c, _ = _place()
        _handshake([(x, y, 1 - c)])
        for cp in _swap_copies(refs, permuted, send_sem, recv_sem):
            cp.start()

    lands = [lax.empty((N_CHIPS,) + g.shape[2:], g.dtype) for g in grads]
    return _split_start(body, name, collective_id, list(grads) + lands, len(grads))


def _pair_swap_wait(send_sem, recv_sem, operands, permuted, after, name):
    def body(refs, send_sem, recv_sem):
        for cp in _swap_copies(refs, permuted, send_sem, recv_sem):
            cp.wait_send()
            cp.wait_recv()

    return _split_wait(body, name, send_sem, recv_sem, operands, after)


def _scatter_copies(refs, permuted, send_sem, recv_sem):
    n = len(refs) // 2
    x, y, _, peers = _place()
    me = 2 * x + y
    return [pltpu.make_async_remote_copy(
        src_ref=refs[a].at[r + 1] if permuted[a] else refs[a].at[me ^ (r + 1)], dst_ref=refs[n + a].at[r],
        send_sem=send_sem.at[a * 3 + r], recv_sem=recv_sem.at[a * 3 + r],
        device_id=peers[r], device_id_type=MESH) for a in range(n) for r in range(3)]


def _scatter_start(partials, permuted, name, collective_id):
    def body(refs, send_sem, recv_sem):
        _handshake(_place()[3])
        for cp in _scatter_copies(refs, permuted, send_sem, recv_sem):
            cp.start()

    lands = [lax.empty((N_CHIPS - 1,) + p.shape[1:], p.dtype) for p in partials]
    return _split_start(body, name, collective_id, list(partials) + lands, 3 * len(partials))


def _scatter_wait(send_sem, recv_sem, operands, permuted, after, name):
    def body(refs, send_sem, recv_sem):
        for cp in _scatter_copies(refs, permuted, send_sem, recv_sem):
            cp.wait_send()
            cp.wait_recv()

    return _split_wait(body, name, send_sem, recv_sem, operands, after)


def _pair_join(halves, name):
    n = len(halves)

    def body(*refs):
        src, dst = refs[:n], refs[n:2 * n]
        send_sem, recv_sem = refs[2 * n:]
        x, y, c, _ = _place()
        copies = [pltpu.make_async_remote_copy(
            src_ref=src[a], dst_ref=dst[a], send_sem=send_sem.at[a], recv_sem=recv_sem.at[a],
            device_id=(x, y, 1 - c), device_id_type=MESH) for a in range(n)]
        for cp in copies:
            cp.start()
        for cp in copies:
            cp.wait()

    return pl.pallas_call(
        body, name=name,
        in_specs=[ANY] * n, out_specs=[ANY] * n,
        out_shape=[_out(h.shape, F32) for h in halves],
        scratch_shapes=[pltpu.SemaphoreType.DMA((n,))] * 2,
    )(*halves)


def _all_sum_small(v, after, name):
    R, C = v.shape
    n_dev = 8

    def body(v_ref, after_ref, o_ref, buf, send_sem, recv_sem):
        x, y, c, _ = _place()
        me = 4 * x + 2 * y + c
        buf[me] = v_ref[...]
        copies = []
        for k in range(1, n_dev):
            peer = (x ^ (k >> 2), y ^ ((k >> 1) & 1), c ^ (k & 1))
            copies.append(pltpu.make_async_remote_copy(
                src_ref=v_ref, dst_ref=buf.at[me], send_sem=send_sem.at[k - 1], recv_sem=recv_sem.at[k - 1],
                device_id=peer, device_id_type=MESH))
        for cp in copies:
            cp.start()
        for cp in copies:
            cp.wait()
        acc = buf[0]
        for m in range(1, n_dev):
            acc = acc + buf[m]
        o_ref[...] = acc

    return pl.pallas_call(
        body, name=name,
        in_specs=[pl.BlockSpec(memory_space=pltpu.VMEM), ANY], out_specs=pl.BlockSpec(memory_space=pltpu.VMEM),
        out_shape=jax.ShapeDtypeStruct((R, C), F32),
        scratch_shapes=[pltpu.VMEM((n_dev, R, C), F32), pltpu.SemaphoreType.DMA((n_dev - 1,)),
                        pltpu.SemaphoreType.DMA((n_dev - 1,))],
    )(v, after)


class _WholeWeights:
    def __init__(self, w):
        self.w = w

    def weights(self, group, after=None):
        return self.w, None

    def grads_ready(self, group, gw):
        return None

    def grads_sent(self, group, after):
        return None


def _local_step(x, p, target, gains, rel_bias, hooks):
    T, D = x.shape
    S = N_CHIPS

    tied = lambda gain, token: gain if token is None else gain + token[0, 0]
    w, token = hooks.weights("first")
    w = dict(w)
    xn1, g1, u1, a1 = _ffn_up(x, tied(gains["ffn1_pre"], token), w["ffn1_gate"], w["ffn1_up"], "ffn1_up")
    w.update(hooks.weights("down", a1)[0])
    h1, f1 = _ffn_down(x, a1, gains["ffn1_post"], w["ffn1_down"], "ffn1_down")
    more, token = hooks.weights("in", h1)
    w.update(more)
    qkv, un = _norm_proj(h1, tied(gains["mix_pre"], token), w["in"], "qkv_proj")
    bias = _ch_group_bias(_bias_table(rel_bias, "bias_table").transpose(1, 0, 2))
    o_a = _sb_fwd(qkv, "sb_fwd")
    o_b = _ch_fwd(qkv, bias, "ch_fwd")
    w.update(hooks.weights("rest", o_b)[0])
    w_out = w["out"].reshape(D, D)
    h2, mixed, mo = _mix_out_fwd(h1, o_a, o_b, gains["out_sb"], gains["out_ch"], w_out, gains["mix_post"],
                                 "mix_out_fwd")
    h3, xn2, g2, u2, a2, f2 = _ffn_fwd(h2, gains["ffn2_pre"], gains["ffn2_post"], w["ffn2_gate"], w["ffn2_up"],
                                       w["ffn2_down"], "ffn2_fwd")
    w_ple_proj = w["ple_proj"].transpose(1, 0, 2).reshape(p.shape[1], D)
    w_ple_gate = w["ple_gate"].reshape(D, D)

    loss, dh3, dproj, dgate, dg_ple = _ple_loss(h3, p, target, w_ple_proj, w_ple_gate, gains["ple_post"], "ple_loss")
    gw, gg = {}, {"ple_post": dg_ple}
    gw["ple_proj"] = _mm_tn(p[None], dproj, p.shape[1], "dw_ple_proj")
    row_sharded = lambda pair: tuple(o.reshape(S, D // S, D) for o in pair)
    gw["ple_gate"] = row_sharded(_mm_tn(h3[None], dgate[None], 512, "dw_ple_gate"))

    def ffn_bwd(tag, dh, x_in, xn, g_act, u_act, a_act, f, group):
        dgp, dup, df, gg[tag + "_post"] = _ffn_bwd_act(dh, f, gains[tag + "_post"], w[tag + "_down"], g_act, u_act,
                                                       tag + "_bwd_act")
        gw[tag + "_gate"] = _mm_tn(dgp, xn[None], dgp.shape[2], "dw_" + tag + "_gate")
        gw[tag + "_up"] = _mm_tn(dup, xn[None], dup.shape[2], "dw_" + tag + "_up")
        gw[tag + "_down"] = _mm_tn(a_act, df[None], a_act.shape[2], "dw_" + tag + "_down")
        g_pre = gains[tag + "_pre"]
        if group is not None:
            token = hooks.grads_ready(group, gw)
            g_pre = g_pre if token is None else g_pre + token[0, 0]
        dx, gg[tag + "_pre"] = _proj_bwd([dgp, dup], [w[tag + "_gate"], w[tag + "_up"]], x_in, g_pre, dh,
                                         tag + "_bwd_in")
        return dx

    dh2 = ffn_bwd("ffn2", dh3, h2, xn2, g2, u2, a2, f2, None)
    dmo, do_a, do_b, gg["mix_post"], gg["out_sb"], gg["out_ch"] = _mix_out_bwd(
        dh2, mo, gains["mix_post"], w_out, o_a, o_b, gains["out_sb"], gains["out_ch"], "mix_out_bwd")
    gw["out"] = row_sharded(_mm_tn(mixed[None], dmo[None], 512, "dw_out"))
    token = hooks.grads_ready("early", gw)
    dq_a, dk_a, dv_a = _sb_bwd(qkv, do_a, o_a, do_a if token is None else token, "sb_bwd")
    token = hooks.grads_sent("early", dq_a)
    dq_b, dk_b, dv_b, dbias = _ch_bwd(qkv, bias, do_b, do_b if token is None else token, "ch_bwd")
    g_rel = _bias_grad(_ch_fold_bias_grad(dbias).transpose(1, 0, 2), "bias_grad")
    dqkv = [dq_a, dk_a, dv_a, dq_b, dk_b, dv_b]
    gw["in"] = _dw_in(un, dqkv, w["in"].shape[2], 512, "dw_in")
    dh1, gg["mix_pre"] = _qkv_bwd_in(dqkv, w["in"], h1, gains["mix_pre"], dh2, "qkv_bwd_in")
    dx = ffn_bwd("ffn1", dh1, x, xn1, g1, u1, a1, f1, "late")
    return loss, dx, gw, gg, g_rel


BIG = ["ffn1_gate", "ffn1_up", "ffn1_down", "in", "out", "ffn2_gate", "ffn2_up", "ffn2_down", "ple_proj", "ple_gate"]
GAINS = ["ffn1_pre", "ffn1_post", "mix_pre", "mix_post", "out_sb", "out_ch", "ffn2_pre", "ffn2_post", "ple_post"]
TRANSPOSED = ("w_ffn1_gate", "w_ffn1_up", "w_ffn2_gate", "w_ffn2_up")
PERMUTED = ("ffn1_gate", "ffn1_up", "ffn1_down", "ffn2_gate", "ffn2_up", "ffn2_down")
W_GROUPS = {"first": ["ffn1_gate", "ffn1_up"], "down": ["ffn1_down"], "in": ["in"],
            "rest": ["out", "ffn2_gate", "ffn2_up", "ffn2_down", "ple_proj", "ple_gate"]}
G_GROUPS = {"early": ["ple_proj", "ple_gate", "ffn2_gate", "ffn2_up", "ffn2_down", "out"],
            "late": ["in", "ffn1_gate", "ffn1_up", "ffn1_down"]}
ORDER = ["g_ffn1_pre", "g_ffn1_post", "w_ffn1_gate", "w_ffn1_up", "w_ffn1_down", "g_mix_pre", "g_mix_post", "w_in",
         "g_out_sb", "g_out_ch", "rel_bias", "w_out", "g_ffn2_pre", "g_ffn2_post", "w_ffn2_gate", "w_ffn2_up",
         "w_ffn2_down", "w_ple_proj", "w_ple_gate", "g_ple_post"]


def kernel(x, p, g_ffn1_pre, g_ffn1_post, w_ffn1_gate, w_ffn1_up, w_ffn1_down, g_mix_pre, g_mix_post, w_in, g_out_sb, g_out_ch, rel_bias, w_out, g_ffn2_pre, g_ffn2_post, w_ffn2_gate, w_ffn2_up, w_ffn2_down, w_ple_proj, w_ple_gate, g_ple_post, loss_target, m_g_ffn1_pre, m_g_ffn1_post, m_w_ffn1_gate, m_w_ffn1_up, m_w_ffn1_down, m_g_mix_pre, m_g_mix_post, m_w_in, m_g_out_sb, m_g_out_ch, m_rel_bias, m_w_out, m_g_ffn2_pre, m_g_ffn2_post, m_w_ffn2_gate, m_w_ffn2_up, m_w_ffn2_down, m_w_ple_proj, m_w_ple_gate, m_g_ple_post, v_g_ffn1_pre, v_g_ffn1_post, v_w_ffn1_gate, v_w_ffn1_up, v_w_ffn1_down, v_g_mix_pre, v_g_mix_post, v_w_in, v_g_out_sb, v_g_out_ch, v_rel_bias, v_w_out, v_g_ffn2_pre, v_g_ffn2_post, v_w_ffn2_gate, v_w_ffn2_up, v_w_ffn2_down, v_w_ple_proj, v_w_ple_gate, v_g_ple_post):
    args = dict(locals())
    take = lambda a, n: a[0].T if n in TRANSPOSED else a[0]
    wts = {n: take(args[n], n) for n in ORDER}
    ms = {n: take(args["m_" + n], n) for n in ORDER}
    vs = {n: take(args["v_" + n], n) for n in ORDER}
    gains = {n: wts["g_" + n][None] for n in GAINS}

    c_idx = lax.axis_index("c").astype(jnp.int32).reshape(1)
    me_idx = (2 * lax.axis_index("x") + lax.axis_index("y")).astype(jnp.int32).reshape(1)
    south = lax.axis_index("c") == 0

    plain = lambda names: [n not in PERMUTED for n in names]
    lands = dict(zip(BIG, _cast_into_own_slot(me_idx, c_idx, [wts["w_" + n] for n in BIG], plain(BIG), "cast_weights")))

    class Overlapped:
        def __init__(self):
            self.started = {}
            self.flying = {}

        def start(self, group, collective_id, after):
            names = W_GROUPS[group]
            self.flying[group] = _gather_start([lands[n] for n in names], plain(names), "gather_%s_start" % group,
                                               collective_id, after)
            return self.flying[group][3]

        def weights(self, group, after=None):
            names = W_GROUPS[group]
            token = None
            if group == "first":
                zones = _gather_finish([lands[n] for n in names], plain(names), True, "gather_first")
                token = self.start("rest", 4, self.start("in", 1, self.start("down", 6, zones[0])))
            else:
                send_sem, recv_sem, zones, _ = self.flying[group]
                zones = _gather_wait(send_sem, recv_sem, zones, plain(names), after, "gather_%s_wait" % group)
                zones = _gather_finish(zones, plain(names), False, "gather_%s_finish" % group)
            return {n: _slabs(z) for n, z in zip(names, zones)}, token

        def grads_ready(self, group, gw):
            names = G_GROUPS[group]
            perm = [n in PERMUTED for n in names]
            halved = lambda g: g.reshape(N_CHIPS, 2, g.shape[1] // 2, g.shape[2])
            mine = [halved(gw[n][0]) for n in names]
            narrow = [halved(gw[n][1]) for n in names]
            if group == "late":
                return self.scatter(group, names, perm, mine, _pair_swap(narrow, perm, "grad_pair_swap_late"))
            self.swapping = names, perm, mine, _pair_swap_start(narrow, perm, "grad_pair_swap_start_early", 5)
            return self.swapping[3][3]

        def grads_sent(self, group, after):
            names, perm, mine, (send_sem, recv_sem, operands, _) = self.swapping
            operands = _pair_swap_wait(send_sem, recv_sem, operands, perm, after, "grad_pair_swap_wait_early")
            return self.scatter(group, names, perm, mine, operands[len(names):])

        def scatter(self, group, names, perm, mine, got):
            partial = _pair_add(c_idx, mine, got, perm, "grad_pair_add_" + group)
            send_sem, recv_sem, operands, token = _scatter_start(partial, perm, "grad_scatter_start_" + group,
                                                                 {"early": 2, "late": 3}[group])
            self.started[group] = names, perm, send_sem, recv_sem, operands, token
            return token

    def reduce_finish(state, after, tag):
        names, perm, send_sem, recv_sem, operands, _ = state
        operands = _scatter_wait(send_sem, recv_sem, operands, perm, after, "grad_scatter_wait_" + tag)
        n = len(names)
        own = _chip_add(me_idx, operands[:n], operands[n:], perm, "grad_chip_add_" + tag)
        return own, _pair_join(own, "grad_pair_join_" + tag)

    hooks = Overlapped()
    loss, dx, gw, gg, g_rel = _local_step(x[0], p[0, 0], loss_target[0], gains, wts["rel_bias"], hooks)

    grads, delta, new_m, new_v = {}, {}, {}, {}

    def finish(group, after):
        own, other = reduce_finish(hooks.started[group], after, group)
        names = ["w_" + n for n in G_GROUPS[group]]
        g, d, m, v = _adamw_halves(c_idx, [wts[n] for n in names], own, other, [ms[n] for n in names],
                                   [vs[n] for n in names], "adamw_" + group)
        for n, gg_, dd, mm, vv in zip(names, g, d, m, v):
            grads[n], delta[n], new_m[n], new_v[n] = gg_, dd, mm, vv
        return d[0]

    finish("late", finish("early", dx))

    pieces = [gg[n].reshape(-1, 128) for n in GAINS] + [jnp.pad(g_rel, ((0, 0), (0, N_REL_PAD - N_REL))).reshape(-1, 128)]
    summed = _all_sum_small(jnp.concatenate(pieces + [loss], axis=0), delta["w_in"], "small_grad_sum")
    at = 0
    for n, piece in zip(GAINS, pieces[:-1]):
        grads["g_" + n] = summed[at:at + piece.shape[0]].reshape(1, -1)[0]
        at += piece.shape[0]
    grads["rel_bias"] = summed[at:at + pieces[-1].shape[0]].reshape(N_HEADS, N_REL_PAD)[:, :N_REL]
    loss = summed[at + pieces[-1].shape[0], 0]

    small = ["g_" + n for n in GAINS] + ["rel_bias"]
    as_rows = lambda a: (a.reshape(-1, 128) if a.size % 128 == 0 else jnp.pad(a, ((0, 0), (0, N_REL_PAD - N_REL))).reshape(-1, 128))
    d, m, v = _adamw([as_rows(wts[n]) for n in small], [as_rows(grads[n]) for n in small],
                     [as_rows(ms[n]) for n in small], [as_rows(vs[n]) for n in small], 1, "adamw_small")
    for n, dd, mm, vv in zip(small, d, m, v):
        back = (lambda a: a.reshape(N_HEADS, N_REL_PAD)[:, :N_REL]) if n == "rel_bias" else (lambda a: a.reshape(-1))
        delta[n], new_m[n], new_v[n] = back(dd), back(mm), back(vv)

    outs = [loss, dx[None]]
    for table in (grads, delta, new_m, new_v):
        outs += [(table[n].T if n in TRANSPOSED else table[n])[None] for n in ORDER]
    return tuple(outs)
```

```python
import jax
import jax.numpy as jnp
from jax import lax
from jax.experimental import pallas as pl
from jax.experimental.pallas import tpu as pltpu

F32 = jnp.float32
BF16 = jnp.bfloat16
EPS = 1e-6
N_CHIPS = 4
HEAD_DIM = 64
N_HEADS = 8
CHUNK = 64
LOOKBACK = 8
BAND = (LOOKBACK + 1) * CHUNK
PAD = LOOKBACK * CHUNK
REL_CLIP = 128
N_REL = 2 * REL_CLIP + 1
N_REL_PAD = 384
SB_BLOCK = 256
PAIR = 2 * HEAD_DIM
SB_PAIRS = 2
SB_FWD_PAIRS = 4
ATT_SCALE = HEAD_DIM ** -0.5
NEG_INF = -1e30
ROW_BLOCK = 512
WIDE_ROW_BLOCK = 1024
VMEM_LIMIT_WIDE = 56 * 1024 * 1024
VMEM_LIMIT = 48 * 1024 * 1024
MESH = pl.DeviceIdType.MESH

ADAM_LR = 0.001
ADAM_B1 = 0.9
ADAM_B2 = 0.999
ADAM_EPS = 1e-08
ADAM_WD = 0.01
ADAM_STEP = 10

NT = (((1,), (1,)), ((), ()))
TN = (((0,), (0,)), ((), ()))


def _params(n_grid, vmem=None):
    return pltpu.CompilerParams(dimension_semantics=("arbitrary",) * n_grid, vmem_limit_bytes=vmem)


def _hbm(*arrays):
    return [pltpu.with_memory_space_constraint(a, pltpu.HBM) for a in arrays]


def _out(shape, dtype):
    return pltpu.HBM(shape, dtype)


def _dot(a, b, dims=None):
    if dims is None:
        return jnp.dot(a, b, preferred_element_type=F32)
    return lax.dot_general(a, b, dims, preferred_element_type=F32)


def _sigmoid(x):
    return 1.0 / (1.0 + jnp.exp(-x))


def _rms_fwd(x, g):
    r = lax.rsqrt(jnp.mean(x * x, axis=-1, keepdims=True) + EPS)
    return x * r * g


def _rms_bwd(x, g, dy):
    r = lax.rsqrt(jnp.mean(x * x, axis=-1, keepdims=True) + EPS)
    xh = x * r
    dg = jnp.sum(dy * xh, axis=0, keepdims=True)
    t = dy * g
    dx = r * (t - xh * jnp.mean(t * xh, axis=-1, keepdims=True))
    return dx, dg


def _accumulate(ref, val, first):
    @pl.when(first)
    def _():
        ref[...] = val

    @pl.when(jnp.logical_not(first))
    def _():
        ref[...] += val


def _split2(x):
    hi = x.astype(BF16)
    lo = (x - hi.astype(F32)).astype(BF16)
    return hi, lo


def _ffn_fwd(x, g_pre, g_post, wg, wu, wd, name):
    T, D = x.shape
    S, FS, _ = wg.shape
    tm = min(WIDE_ROW_BLOCK, T)

    def body(x_ref, gpre_ref, gpost_ref, wg_ref, wu_ref, wd_ref,
             h_ref, xn_ref, g_ref, u_ref, a_ref, f_ref):
        k = pl.program_id(1)

        @pl.when(k == 0)
        def _():
            xn_ref[...] = _rms_fwd(x_ref[...], gpre_ref[...]).astype(BF16)

        xn = xn_ref[...]
        g = _dot(xn, wg_ref[0], NT)
        u = _dot(xn, wu_ref[0], NT)
        g_ref[0] = g
        u_ref[0] = u
        a = (g * _sigmoid(g) * u).astype(BF16)
        a_ref[0] = a
        _accumulate(f_ref, _dot(a, wd_ref[0]), k == 0)

        @pl.when(k == S - 1)
        def _():
            h_ref[...] = x_ref[...] + 0.5 * _rms_fwd(f_ref[...], gpost_ref[...])

    row = pl.BlockSpec((tm, D), lambda i, k: (i, 0))
    vec = pl.BlockSpec((1, D), lambda i, k: (0, 0))
    act = pl.BlockSpec((1, tm, FS), lambda i, k: (k, i, 0))
    return pl.pallas_call(
        body, name=name, grid=(T // tm, S),
        in_specs=[row, vec, vec] + [pl.BlockSpec((1, FS, D), lambda i, k: (k, 0, 0))] * 3,
        out_specs=[row, row, act, act, act, row],
        out_shape=[_out((T, D), F32), _out((T, D), BF16),
                   _out((S, T, FS), F32), _out((S, T, FS), F32),
                   _out((S, T, FS), BF16), _out((T, D), F32)],
        compiler_params=_params(2, VMEM_LIMIT_WIDE),
    )(*_hbm(x, g_pre, g_post, wg, wu, wd))


def _ffn_up(x, g_pre, wg, wu, name):
    T, D = x.shape
    S, FS, _ = wg.shape
    tm = min(WIDE_ROW_BLOCK, T)

    def body(x_ref, gpre_ref, wg_ref, wu_ref, xn_ref, g_ref, u_ref, a_ref):
        @pl.when(pl.program_id(1) == 0)
        def _():
            xn_ref[...] = _rms_fwd(x_ref[...], gpre_ref[...]).astype(BF16)

        xn = xn_ref[...]
        g = _dot(xn, wg_ref[0], NT)
        u = _dot(xn, wu_ref[0], NT)
        g_ref[0] = g
        u_ref[0] = u
        a_ref[0] = (g * _sigmoid(g) * u).astype(BF16)

    row = pl.BlockSpec((tm, D), lambda i, k: (i, 0))
    act = pl.BlockSpec((1, tm, FS), lambda i, k: (k, i, 0))
    return pl.pallas_call(
        body, name=name, grid=(T // tm, S),
        in_specs=[row, pl.BlockSpec((1, D), lambda i, k: (0, 0))] + [pl.BlockSpec((1, FS, D), lambda i, k: (k, 0, 0))] * 2,
        out_specs=[row, act, act, act],
        out_shape=[_out((T, D), BF16), _out((S, T, FS), F32), _out((S, T, FS), F32), _out((S, T, FS), BF16)],
        compiler_params=_params(2, VMEM_LIMIT_WIDE),
    )(*_hbm(x, g_pre, wg, wu))


def _ffn_down(x, a, g_post, wd, name):
    T, D = x.shape
    S, FS, _ = wd.shape
    tm = min(WIDE_ROW_BLOCK, T)

    def body(x_ref, a_ref, gpost_ref, wd_ref, h_ref, f_ref):
        k = pl.program_id(1)
        _accumulate(f_ref, _dot(a_ref[0], wd_ref[0]), k == 0)

        @pl.when(k == S - 1)
        def _():
            h_ref[...] = x_ref[...] + 0.5 * _rms_fwd(f_ref[...], gpost_ref[...])

    row = pl.BlockSpec((tm, D), lambda i, k: (i, 0))
    return pl.pallas_call(
        body, name=name, grid=(T // tm, S),
        in_specs=[row, pl.BlockSpec((1, tm, FS), lambda i, k: (k, i, 0)), pl.BlockSpec((1, D), lambda i, k: (0, 0)),
                  pl.BlockSpec((1, FS, D), lambda i, k: (k, 0, 0))],
        out_specs=[row, row],
        out_shape=[_out((T, D), F32), _out((T, D), F32)],
        compiler_params=_params(2, VMEM_LIMIT_WIDE),
    )(*_hbm(x, a, g_post, wd))


def _ffn_bwd_act(dh, f, g_post, wd, g_act, u_act, name):
    T, D = dh.shape
    S, FS, _ = wd.shape
    tm = min(WIDE_ROW_BLOCK, T)

    def body(dh_ref, f_ref, gpost_ref, wd_ref, g_ref, u_ref, dgp_ref, dup_ref, df_ref, dgain_ref, df_s):
        i, k = pl.program_id(0), pl.program_id(1)

        @pl.when(k == 0)
        def _():
            df, dgain = _rms_bwd(f_ref[...], gpost_ref[...], 0.5 * dh_ref[...])
            df_s[...] = df.astype(BF16)
            df_ref[...] = df_s[...]
            _accumulate(dgain_ref, dgain, i == 0)

        da = _dot(df_s[...], wd_ref[0], NT)
        g = g_ref[0]
        s = _sigmoid(g)
        dup_ref[0] = (da * (g * s)).astype(BF16)
        dgp_ref[0] = (da * u_ref[0] * (s * (1.0 + g * (1.0 - s)))).astype(BF16)

    row = pl.BlockSpec((tm, D), lambda i, k: (i, 0))
    vec = pl.BlockSpec((1, D), lambda i, k: (0, 0))
    act = pl.BlockSpec((1, tm, FS), lambda i, k: (k, i, 0))
    return pl.pallas_call(
        body, name=name, grid=(T // tm, S),
        in_specs=[row, row, vec, pl.BlockSpec((1, FS, D), lambda i, k: (k, 0, 0)), act, act],
        out_specs=[act, act, row, vec],
        out_shape=[_out((S, T, FS), BF16), _out((S, T, FS), BF16),
                   _out((T, D), BF16), _out((1, D), F32)],
        scratch_shapes=[pltpu.VMEM((tm, D), BF16)],
        compiler_params=_params(2, VMEM_LIMIT_WIDE),
    )(*_hbm(dh, f, g_post, wd, g_act, u_act))


def _proj_bwd(dys, ws, x, g_pre, dh, name):
    T, D = x.shape
    n = len(dys)
    S, N, _ = ws[0].shape
    tm = min(WIDE_ROW_BLOCK, T)

    def body(*refs):
        dy_refs, w_refs = refs[:n], refs[n:2 * n]
        x_ref, gpre_ref, dh_ref, dx_ref, dgain_ref, acc_s = refs[2 * n:]
        i, k = pl.program_id(0), pl.program_id(1)
        part = None
        for dy_ref, w_ref in zip(dy_refs, w_refs):
            term = _dot(dy_ref[0], w_ref[0])
            part = term if part is None else part + term
        _accumulate(acc_s, part, k == 0)

        @pl.when(k == S - 1)
        def _():
            dx, dgain = _rms_bwd(x_ref[...], gpre_ref[...], acc_s[...])
            dx_ref[...] = dh_ref[...] + dx
            _accumulate(dgain_ref, dgain, i == 0)

    row = pl.BlockSpec((tm, D), lambda i, k: (i, 0))
    vec = pl.BlockSpec((1, D), lambda i, k: (0, 0))
    return pl.pallas_call(
        body, name=name, grid=(T // tm, S),
        in_specs=[pl.BlockSpec((1, tm, N), lambda i, k: (k, i, 0))] * n
        + [pl.BlockSpec((1, N, D), lambda i, k: (k, 0, 0))] * n + [row, vec, row],
        out_specs=[row, vec],
        out_shape=[_out((T, D), F32), _out((1, D), F32)],
        scratch_shapes=[pltpu.VMEM((tm, D), F32)],
        compiler_params=_params(2, VMEM_LIMIT_WIDE),
    )(*_hbm(*dys, *ws, x, g_pre, dh))


def _mm_tn(a, b, bm, name):
    ga, T, M = a.shape
    gb, _, N = b.shape
    b_spec = pl.BlockSpec((1, T, N), (lambda g, m: (g, 0, 0)) if gb > 1 else (lambda g, m: (0, 0, 0)))
    G = max(ga, gb)

    def body(a_ref, b_ref, o_ref, narrow_ref):
        o_ref[0] = _dot(a_ref[0].astype(BF16), b_ref[0].astype(BF16), TN)
        narrow_ref[0] = o_ref[0].astype(BF16)

    out = pl.BlockSpec((1, bm, N), lambda g, m: (g, m, 0))
    return pl.pallas_call(
        body, name=name, grid=(G, M // bm),
        in_specs=[pl.BlockSpec((1, T, bm), (lambda g, m: (g, 0, m)) if ga > 1 else (lambda g, m: (0, 0, m))), b_spec],
        out_specs=[out, out],
        out_shape=[_out((G, M, N), F32), _out((G, M, N), BF16)],
        compiler_params=_params(2, VMEM_LIMIT),
    )(*_hbm(a, b))


QKV_PIECE = 256


def _qkv_shard(dy_refs, k, n_col):
    width = dy_refs[0].shape[1]
    parts = []
    for col in range(k * n_col, (k + 1) * n_col, QKV_PIECE):
        parts.append(dy_refs[col // width][:, col % width:col % width + QKV_PIECE])
    return jnp.concatenate(parts, axis=1)


def _qkv_bwd_in(dys, w, x, g_pre, dh, name):
    T, D = x.shape
    n = len(dys)
    S, _, N = w.shape
    tm = min(WIDE_ROW_BLOCK, T)

    def body(*refs):
        dy_refs = refs[:n]
        w_ref, x_ref, gpre_ref, dh_ref, dx_ref, dgain_ref, acc_s = refs[n:]
        i, k = pl.program_id(0), pl.program_id(1)
        for shard in range(S):
            @pl.when(k == shard)
            def _(shard=shard):
                part = _dot(_qkv_shard(dy_refs, shard, N), w_ref[0], NT)
                if shard == 0:
                    acc_s[...] = part
                else:
                    acc_s[...] += part

        @pl.when(k == S - 1)
        def _():
            dx, dgain = _rms_bwd(x_ref[...], gpre_ref[...], acc_s[...])
            dx_ref[...] = dh_ref[...] + dx
            _accumulate(dgain_ref, dgain, i == 0)

    row = pl.BlockSpec((tm, D), lambda i, k: (i, 0))
    vec = pl.BlockSpec((1, D), lambda i, k: (0, 0))
    return pl.pallas_call(
        body, name=name, grid=(T // tm, S),
        in_specs=[pl.BlockSpec((tm, dy.shape[1]), lambda i, k: (i, 0)) for dy in dys]
        + [pl.BlockSpec((1, D, N), lambda i, k: (k, 0, 0)), row, vec, row],
        out_specs=[row, vec],
        out_shape=[_out((T, D), F32), _out((1, D), F32)],
        scratch_shapes=[pltpu.VMEM((tm, D), F32)],
        compiler_params=_params(2, VMEM_LIMIT_WIDE),
    )(*_hbm(*dys, w, x, g_pre, dh))


def _dw_in(a, dys, n_col, bm, name):
    T, M = a.shape
    n = len(dys)
    S = n * dys[0].shape[1] // n_col

    def body(*refs):
        a_ref, dy_refs = refs[0], refs[1:1 + n]
        o_ref, narrow_ref = refs[1 + n:]
        k = pl.program_id(1)
        for shard in range(S):
            @pl.when(k == shard)
            def _(shard=shard):
                o_ref[0] = _dot(a_ref[...], _qkv_shard(dy_refs, shard, n_col), TN)
                narrow_ref[0] = o_ref[0].astype(BF16)

    out = pl.BlockSpec((1, bm, n_col), lambda m, k: (k, m, 0))
    return pl.pallas_call(
        body, name=name, grid=(M // bm, S),
        in_specs=[pl.BlockSpec((T, bm), lambda m, k: (0, m))]
        + [pl.BlockSpec((T, dy.shape[1]), lambda m, k: (0, 0)) for dy in dys],
        out_specs=[out, out],
        out_shape=[_out((S, M, n_col), F32), _out((S, M, n_col), BF16)],
        compiler_params=_params(2, VMEM_LIMIT_WIDE),
    )(*_hbm(a, *dys))


def _norm_proj(x, g_pre, w, name):
    T, D = x.shape
    S, _, N = w.shape
    tm = min(WIDE_ROW_BLOCK, T)

    def body(x_ref, g_ref, w_ref, o_ref, xn_ref, xn_s):
        @pl.when(pl.program_id(1) == 0)
        def _():
            xn_s[...] = _rms_fwd(x_ref[...], g_ref[...]).astype(BF16)
            xn_ref[...] = xn_s[...]

        o_ref[...] = _dot(xn_s[...], w_ref[0]).astype(BF16)

    row = pl.BlockSpec((tm, D), lambda i, k: (i, 0))
    return pl.pallas_call(
        body, name=name, grid=(T // tm, S),
        in_specs=[row, pl.BlockSpec((1, D), lambda i, k: (0, 0)), pl.BlockSpec((1, D, N), lambda i, k: (k, 0, 0))],
        out_specs=[pl.BlockSpec((tm, N), lambda i, k: (i, k)), row],
        out_shape=[_out((T, S * N), BF16), _out((T, D), BF16)],
        scratch_shapes=[pltpu.VMEM((tm, D), BF16)],
        compiler_params=_params(2, VMEM_LIMIT_WIDE),
    )(*_hbm(x, g_pre, w))


def _mix_out_fwd(h, o_a, o_b, g_sb, g_ch, w_out, g_post, name):
    T, D = h.shape
    W = g_sb.shape[1]
    tm = min(WIDE_ROW_BLOCK, T)

    def body(h_ref, oa_ref, ob_ref, gsb_ref, gch_ref, w_ref, gpost_ref, h2_ref, mixed_ref, mo_ref):
        mixed_ref[:, :W] = _rms_fwd(oa_ref[...], gsb_ref[...]).astype(BF16)
        mixed_ref[:, W:] = _rms_fwd(ob_ref[...], gch_ref[...]).astype(BF16)
        mo = _dot(mixed_ref[...], w_ref[...])
        mo_ref[...] = mo
        h2_ref[...] = h_ref[...] + _rms_fwd(mo, gpost_ref[...])

    row = pl.BlockSpec((tm, D), lambda i: (i, 0))
    part = pl.BlockSpec((tm, W), lambda i: (i, 0))
    half = pl.BlockSpec((1, W), lambda i: (0, 0))
    return pl.pallas_call(
        body, name=name, grid=(T // tm,),
        in_specs=[row, part, part, half, half, pl.BlockSpec((D, D), lambda i: (0, 0)), pl.BlockSpec((1, D), lambda i: (0, 0))],
        out_specs=[row, row, row],
        out_shape=[_out((T, D), F32), _out((T, D), BF16),
                   _out((T, D), F32)],
        compiler_params=_params(1, VMEM_LIMIT_WIDE),
    )(*_hbm(h, o_a, o_b, g_sb, g_ch, w_out, g_post))


def _mix_out_bwd(dh, mo, g_post, w_out, o_a, o_b, g_sb, g_ch, name):
    T, D = dh.shape
    W = g_sb.shape[1]
    tm = min(WIDE_ROW_BLOCK, T)

    def body(dh_ref, mo_ref, gpost_ref, w_ref, oa_ref, ob_ref, gsb_ref, gch_ref,
             dmo_ref, doa_ref, dob_ref, dgpost_ref, dgsb_ref, dgch_ref):
        first = pl.program_id(0) == 0
        dmo, dgpost = _rms_bwd(mo_ref[...], gpost_ref[...], dh_ref[...])
        dmo_ref[...] = dmo.astype(BF16)
        dmix = _dot(dmo_ref[...], w_ref[...], NT)
        doa_ref[...], dgsb = _rms_bwd(oa_ref[...], gsb_ref[...], dmix[:, :W])
        dob_ref[...], dgch = _rms_bwd(ob_ref[...], gch_ref[...], dmix[:, W:])
        _accumulate(dgpost_ref, dgpost, first)
        _accumulate(dgsb_ref, dgsb, first)
        _accumulate(dgch_ref, dgch, first)

    row = pl.BlockSpec((tm, D), lambda i: (i, 0))
    part = pl.BlockSpec((tm, W), lambda i: (i, 0))
    vec = pl.BlockSpec((1, D), lambda i: (0, 0))
    half = pl.BlockSpec((1, W), lambda i: (0, 0))
    return pl.pallas_call(
        body, name=name, grid=(T // tm,),
        in_specs=[row, row, vec, pl.BlockSpec((D, D), lambda i: (0, 0)), part, part, half, half],
        out_specs=[row, part, part, vec, half, half],
        out_shape=[_out((T, D), BF16), _out((T, W), F32),
                   _out((T, W), F32), _out((1, D), F32),
                   _out((1, W), F32), _out((1, W), F32)],
        compiler_params=_params(1, VMEM_LIMIT_WIDE),
    )(*_hbm(dh, mo, g_post, w_out, o_a, o_b, g_sb, g_ch))


def _ple_loss(h, p, target, w_proj, w_gate, g_post, name):
    T, D = h.shape
    P = p.shape[1]
    S = N_CHIPS
    C = D // S
    tm = min(ROW_BLOCK, T)

    def body(h_ref, p_ref, t_ref, wp_ref, wg_ref, g_ref, loss_ref, dh_ref, dproj_ref, dgate_ref, dgain_ref):
        first = pl.program_id(0) == 0
        h3 = h_ref[...]
        proj = _dot(p_ref[...].astype(BF16), wp_ref[...])
        s = _sigmoid(_dot(h3.astype(BF16), wg_ref[...]))
        e = proj * s
        diff = h3 + _rms_fwd(e, g_ref[...]) - t_ref[...]
        part = 0.5 * jnp.sum(jnp.mean(diff * diff, axis=-1, keepdims=True), axis=0, keepdims=True)
        _accumulate(loss_ref, jnp.broadcast_to(part, loss_ref.shape), first)
        dy = diff * (1.0 / D)
        de, dgain = _rms_bwd(e, g_ref[...], dy)
        _accumulate(dgain_ref, dgain, first)
        dproj = (de * s).astype(BF16)
        for j in range(S):
            dproj_ref[j] = dproj[:, j * C:(j + 1) * C]
        dgate_ref[...] = (de * proj * s * (1.0 - s)).astype(BF16)
        dh_ref[...] = dy + _dot(dgate_ref[...], wg_ref[...], NT)

    row = pl.BlockSpec((tm, D), lambda i: (i, 0))
    vec = pl.BlockSpec((1, D), lambda i: (0, 0))
    return pl.pallas_call(
        body, name=name, grid=(T // tm,),
        in_specs=[row, pl.BlockSpec((tm, P), lambda i: (i, 0)), row,
                  pl.BlockSpec((P, D), lambda i: (0, 0)), pl.BlockSpec((D, D), lambda i: (0, 0)), vec],
        out_specs=[pl.BlockSpec((8, 128), lambda i: (0, 0)), row,
                   pl.BlockSpec((S, tm, C), lambda i: (0, i, 0)), row, vec],
        out_shape=[_out((8, 128), F32), _out((T, D), F32),
                   _out((S, T, C), BF16), _out((T, D), BF16),
                   _out((1, D), F32)],
        compiler_params=_params(1, VMEM_LIMIT_WIDE),
    )(*_hbm(h, p, target, w_proj, w_gate, g_post))


def _sb_scores(q, kj, mask):
    z = _dot(q, kj, NT)
    sp = jnp.maximum(z, 0.0) + jnp.log(1.0 + jnp.exp(-jnp.abs(z)))
    return z, sp if mask is None else jnp.where(mask, sp, 0.0)


def _strict_causal():
    rows = lax.broadcasted_iota(jnp.int32, (SB_BLOCK, SB_BLOCK), 0)
    cols = lax.broadcasted_iota(jnp.int32, (SB_BLOCK, SB_BLOCK), 1)
    return cols < rows


def _tri(cmp):
    r = lax.broadcasted_iota(jnp.int32, (2 * SB_BLOCK, SB_BLOCK), 0) % SB_BLOCK
    c = lax.broadcasted_iota(jnp.int32, (2 * SB_BLOCK, SB_BLOCK), 1)
    return jnp.where(cmp(r, c), 1.0, 0.0).astype(BF16)


def _cum(x, tri):
    return _dot(jnp.concatenate(_split2(x), axis=1), tri)


def _pair_lanes():
    lane = lax.broadcasted_iota(jnp.int32, (1, PAIR), 1)
    return [lane < HEAD_DIM, lane >= HEAD_DIM]


def _only(lanes, x):
    return jnp.where(lanes, x, jnp.zeros_like(x))


def _sb_fwd(qkv, name):
    T = qkv.shape[0]
    B = SB_BLOCK
    W = SB_FWD_PAIRS * PAIR
    steps = N_HEADS // (2 * SB_FWD_PAIRS)
    heads = [(p, h) for p in range(SB_FWD_PAIRS) for h in range(2)]

    def body(q_ref, k_ref, v_ref, o_ref):
        i = pl.program_id(1)
        after = _tri(lambda r, c: r > c)
        lanes = _pair_lanes()
        cols = [slice(p * PAIR, (p + 1) * PAIR) for p in range(SB_FWD_PAIRS)]
        q = {(p, h): _only(lanes[h], q_ref[:, cols[p]] * ATT_SCALE) for p, h in heads}

        def tiles(j, carries, mask):
            at = pl.ds(pl.multiple_of(j * B, B), B)
            scores = [_sb_scores(q[ph], k_ref[at, cols[ph[0]]], mask) for ph in heads]
            laters = [_cum(sp, after) for _, sp in scores]
            out = []
            for ph, (z, sp), later, (run, acc) in zip(heads, scores, laters, carries):
                a = jnp.exp(z - sp - later - run)
                if mask is not None:
                    a = jnp.where(mask, a, 0.0)
                out.append((run + later[:, 0:1] + sp[:, 0:1],
                            acc + _dot(a.astype(BF16), _only(lanes[ph[1]], v_ref[at, cols[ph[0]]]))))
            return tuple(out)

        zero = (jnp.zeros((B, 1), F32), jnp.zeros((B, PAIR), F32))
        carries = tiles(i, (zero,) * len(heads), _strict_causal())
        carries = lax.fori_loop(0, i, lambda jj, cs: tiles(i - 1 - jj, cs, None), carries)
        for p in range(SB_FWD_PAIRS):
            o_ref[:, cols[p]] = carries[2 * p][1] + carries[2 * p + 1][1]

    blk = lambda off: pl.BlockSpec((B, W), lambda g, i: (i, g + off))
    full = lambda off: pl.BlockSpec((T, W), lambda g, i: (0, g + off))
    return pl.pallas_call(
        body, name=name, grid=(steps, T // B),
        in_specs=[blk(0), full(steps), full(2 * steps)],
        out_specs=blk(0),
        out_shape=_out((T, N_HEADS * HEAD_DIM), F32),
        compiler_params=_params(2, VMEM_LIMIT),
    )(*_hbm(qkv, qkv, qkv))


def _sb_bwd(qkv, do, o, after, name):
    T = qkv.shape[0]
    B = SB_BLOCK
    W = SB_PAIRS * PAIR
    steps = N_HEADS // (2 * SB_PAIRS)
    n_blocks = T // B
    heads = [(p, h) for p in range(SB_PAIRS) for h in range(2)]

    def body(q_ref, k_ref, v_ref, do_ref, o_ref, dq_ref, dk_ref, dv_ref, dk_s, dv_s):
        i = pl.program_id(1)

        @pl.when(i == 0)
        def _():
            dk_s[...] = jnp.zeros_like(dk_s)
            dv_s[...] = jnp.zeros_like(dv_s)

        after = _tri(lambda r, c: r > c)
        since = _tri(lambda r, c: r >= c)
        lanes = _pair_lanes()
        cols = [slice(p * PAIR, (p + 1) * PAIR) for p in range(SB_PAIRS)]
        q = {(p, h): _only(lanes[h], q_ref[:, cols[p]] * ATT_SCALE) for p, h in heads}
        do = {(p, h): _only(lanes[h], do_ref[:, cols[p]].astype(BF16)) for p, h in heads}
        total = {ph: jnp.sum(do[ph].astype(F32) * o_ref[:, cols[ph[0]]], axis=1, keepdims=True) for ph in heads}

        def tiles(j, carries, mask):
            at = pl.ds(pl.multiple_of(j * B, B), B)
            ks = [k_ref[at, c] for c in cols]
            vs = [v_ref[at, c] for c in cols]
            scores = [_sb_scores(q[ph], ks[ph[0]], mask) for ph in heads]
            laters = [_cum(sp, after) for _, sp in scores]
            das = [_dot(do[ph], vs[ph[0]], NT) for ph in heads]
            a_s, gs = [], []
            for (z, sp), later, da, carry in zip(scores, laters, das, carries):
                a = jnp.exp(z - sp - later - carry[0])
                if mask is not None:
                    a = jnp.where(mask, a, 0.0)
                a = a.astype(BF16)
                a_s.append(a)
                gs.append(a.astype(F32) * da)
            sinces = [_cum(g, since) for g in gs]
            dzs = []
            for ph, (_, sp), g, from_s, carry in zip(heads, scores, gs, sinces, carries):
                g_before = total[ph] - carry[1] - from_s
                fail = jnp.exp(-sp)
                dz = fail * (g + g_before) - g_before
                if mask is not None:
                    dz = jnp.where(mask, dz, 0.0)
                dzs.append(dz.astype(BF16))
            out = []
            for ph, (_, sp), a, dz, later, from_s, carry in zip(heads, scores, a_s, dzs, laters, sinces, carries):
                dk_s[at, cols[ph[0]]] += _dot(dz, q[ph], TN)
                dv_s[at, cols[ph[0]]] += _dot(a, do[ph], TN)
                out.append((carry[0] + later[:, 0:1] + sp[:, 0:1], carry[1] + from_s[:, 0:1],
                            carry[2] + _dot(dz, _only(lanes[ph[1]], ks[ph[0]]))))
            return tuple(out)

        col = jnp.zeros((B, 1), F32)
        zero = (col, col, jnp.zeros((B, PAIR), F32))
        carries = tiles(i, (zero,) * len(heads), _strict_causal())
        last = lax.fori_loop(0, i, lambda jj, cs: tiles(i - 1 - jj, cs, None), carries)
        for p in range(SB_PAIRS):
            dq_ref[:, cols[p]] = ((last[2 * p][2] + last[2 * p + 1][2]) * ATT_SCALE).astype(BF16)

        @pl.when(i == n_blocks - 1)
        def _():
            dk_ref[...] = dk_s[...].astype(BF16)
            dv_ref[...] = dv_s[...].astype(BF16)

    blk = lambda off: pl.BlockSpec((B, W), lambda g, i: (i, g + off))
    full = lambda off: pl.BlockSpec((T, W), lambda g, i: (0, g + off))
    out = _out((T, N_HEADS * HEAD_DIM), BF16)
    return pl.pallas_call(
        lambda after_ref, *refs: body(*refs), name=name, grid=(steps, n_blocks),
        in_specs=[ANY, blk(0), full(steps), full(2 * steps), blk(0), blk(0)],
        out_specs=[blk(0), full(0), full(0)],
        out_shape=[out, out, out],
        scratch_shapes=[pltpu.VMEM((T, W), F32)] * 2,
        compiler_params=_params(2, VMEM_LIMIT),
    )(after, *_hbm(qkv, qkv, qkv, do, o))


NEAR = BAND - PAD + REL_CLIP
FAR = BAND - NEAR
NEAR_REL = 2 * REL_CLIP
BIAS_ROWS = 8


def _rel_onehot(i, transposed):
    shape = (NEAR, NEAR_REL) if transposed else (NEAR_REL, NEAR)
    j = FAR + lax.broadcasted_iota(jnp.int32, shape, 0 if transposed else 1)
    r = lax.broadcasted_iota(jnp.int32, shape, 1 if transposed else 0)
    idx = jnp.clip(i + PAD - j, -REL_CLIP, REL_CLIP) + REL_CLIP
    return jnp.where(idx - 1 == r, 1.0, 0.0).astype(BF16)


def _bias_table(rel_bias, name):
    def body(near_ref, far_ref, o_ref):
        rb = near_ref[...]
        hi, lo = _split2(rb)
        lo2 = (rb - hi.astype(F32) - lo.astype(F32)).astype(BF16)
        far = jnp.broadcast_to(far_ref[...], (N_HEADS, FAR))
        for k in range(BIAS_ROWS):
            onehot = _rel_onehot(pl.program_id(0) * BIAS_ROWS + k, False)
            o_ref[k, :, :FAR] = far
            o_ref[k, :, FAR:] = _dot(hi, onehot) + _dot(lo, onehot) + _dot(lo2, onehot)

    return pl.pallas_call(
        body, name=name, grid=(CHUNK // BIAS_ROWS,),
        in_specs=[pl.BlockSpec((N_HEADS, NEAR_REL), lambda i: (0, 0)), pl.BlockSpec((N_HEADS, 1), lambda i: (0, 0))],
        out_specs=pl.BlockSpec((BIAS_ROWS, N_HEADS, BAND), lambda i: (i, 0, 0)),
        out_shape=_out((CHUNK, N_HEADS, BAND), F32),
        compiler_params=_params(1),
    )(*_hbm(rel_bias[:, 1:], rel_bias[:, N_REL - 1:]))


def _bias_grad(dbias_t, name):
    def body(d_ref, near_ref, far_ref):
        near, far = None, None
        for k in range(BIAS_ROWS):
            onehot = _rel_onehot(pl.program_id(0) * BIAS_ROWS + k, True)
            hi, lo = _split2(d_ref[k, :, FAR:])
            part = _dot(hi, onehot) + _dot(lo, onehot)
            rest = jnp.sum(d_ref[k, :, :FAR], axis=1, keepdims=True)
            near, far = (part, rest) if near is None else (near + part, far + rest)
        first = pl.program_id(0) == 0
        _accumulate(near_ref, near, first)
        _accumulate(far_ref, jnp.broadcast_to(far, far_ref.shape), first)

    near, far = pl.pallas_call(
        body, name=name, grid=(CHUNK // BIAS_ROWS,),
        in_specs=[pl.BlockSpec((BIAS_ROWS, N_HEADS, BAND), lambda i: (i, 0, 0))],
        out_specs=[pl.BlockSpec((N_HEADS, NEAR_REL), lambda i: (0, 0)), pl.BlockSpec((N_HEADS, 128), lambda i: (0, 0))],
        out_shape=[_out((N_HEADS, NEAR_REL), F32), _out((N_HEADS, 128), F32)],
        compiler_params=_params(1),
    )(*_hbm(dbias_t))
    return jnp.pad(near, ((0, 0), (1, 0))).at[:, N_REL - 1].add(far[:, 0])


def _ch_probs(scores, bias, valid):
    z = jnp.where(valid, scores * ATT_SCALE + bias, NEG_INF)
    e = jnp.exp(z - jnp.max(z, axis=-1, keepdims=True))
    return e / jnp.sum(e, axis=-1, keepdims=True)


CH_HEADS = [(pair, h) for pair in range(N_HEADS // 2) for h in range(2)]
CH_COLS = [slice(pair * PAIR, (pair + 1) * PAIR) for pair in range(N_HEADS // 2)]


CH_GROUP = 2
CH_Q = CH_GROUP * CHUNK
CH_WIN = (LOOKBACK + CH_GROUP) * CHUNK


def _ch_valid(n):
    row_chunk = lax.broadcasted_iota(jnp.int32, (CH_Q, CH_WIN), 0) // CHUNK
    slot = lax.broadcasted_iota(jnp.int32, (CH_Q, CH_WIN), 1)
    ahead = slot // CHUNK - row_chunk
    return (ahead >= 0) & (ahead <= LOOKBACK) & (n * CH_Q + slot >= PAD)


def _ch_group_bias(bias):
    shifted = [jnp.pad(bias, ((0, 0), (0, 0), (c * CHUNK, (CH_GROUP - 1 - c) * CHUNK))) for c in range(CH_GROUP)]
    return jnp.concatenate(shifted, axis=1)


def _ch_fold_bias_grad(dbias):
    parts = [dbias[:, c * CHUNK:(c + 1) * CHUNK, c * CHUNK:c * CHUNK + BAND] for c in range(CH_GROUP)]
    return sum(parts[1:], parts[0])


def _ch_fwd(qkv, bias, name):
    T = qkv.shape[0]
    W = N_HEADS * HEAD_DIM

    def body(q_ref, k_ref, v_ref, b_ref, o_ref, kp, vp):
        n = pl.program_id(0)

        @pl.when(n == 0)
        def _():
            _ch_load_padded(k_ref, v_ref, kp, vp)

        win = pl.ds(pl.multiple_of(n * CH_Q, CH_Q), CH_WIN)
        valid = _ch_valid(n)
        lanes = _pair_lanes()
        scores = [_dot(_only(lanes[h], q_ref[:, CH_COLS[pair]]), kp[win, CH_COLS[pair]], NT) for pair, h in CH_HEADS]
        probs = [_ch_probs(s, b_ref[2 * pair + h], valid).astype(BF16) for s, (pair, h) in zip(scores, CH_HEADS)]
        outs = [_dot(p, _only(lanes[h], vp[win, CH_COLS[pair]])) for p, (pair, h) in zip(probs, CH_HEADS)]
        for pair, cols in enumerate(CH_COLS):
            o_ref[:, cols] = outs[2 * pair] + outs[2 * pair + 1]

    full = lambda col: pl.BlockSpec((T, W), lambda n: (0, col))
    return pl.pallas_call(
        body, name=name, grid=(T // CH_Q,),
        in_specs=[pl.BlockSpec((CH_Q, W), lambda n: (n, 3)), full(4), full(5),
                  pl.BlockSpec((N_HEADS, CH_Q, CH_WIN), lambda n: (0, 0, 0))],
        out_specs=pl.BlockSpec((CH_Q, W), lambda n: (n, 0)),
        out_shape=_out((T, W), F32),
        scratch_shapes=[pltpu.VMEM((PAD + T, W), BF16)] * 2,
        compiler_params=_params(1, VMEM_LIMIT),
    )(*_hbm(qkv, qkv, qkv, bias))


def _ch_load_padded(k_ref, v_ref, kp, vp):
    for src, dst in ((k_ref, kp), (v_ref, vp)):
        dst[:PAD, :] = jnp.zeros((PAD, dst.shape[1]), dst.dtype)
        dst[PAD:, :] = src[...]


def _ch_bwd(qkv, bias, do, after, name):
    T = qkv.shape[0]
    W = N_HEADS * HEAD_DIM
    n_chunks = T // CH_Q

    def body(q_ref, k_ref, v_ref, b_ref, do_ref, dq_ref, dk_ref, dv_ref, db_ref, kp, vp, dk_s, dv_s):
        n = pl.program_id(0)

        @pl.when(n == 0)
        def _():
            _ch_load_padded(k_ref, v_ref, kp, vp)
            dk_s[...] = jnp.zeros_like(dk_s)
            dv_s[...] = jnp.zeros_like(dv_s)
            db_ref[...] = jnp.zeros_like(db_ref)

        win = pl.ds(pl.multiple_of(n * CH_Q, CH_Q), CH_WIN)
        valid = _ch_valid(n)
        lanes = _pair_lanes()
        kws = [kp[win, cols] for cols in CH_COLS]
        vws = [vp[win, cols] for cols in CH_COLS]
        qs = [_only(lanes[h], q_ref[:, CH_COLS[pair]]) for pair, h in CH_HEADS]
        dos = [_only(lanes[h], do_ref[:, CH_COLS[pair]].astype(BF16)) for pair, h in CH_HEADS]
        scores = [_dot(q, kws[pair], NT) for q, (pair, _) in zip(qs, CH_HEADS)]
        dps = [_dot(do, vws[pair], NT) for do, (pair, _) in zip(dos, CH_HEADS)]
        probs = [_ch_probs(s, b_ref[2 * pair + h], valid) for s, (pair, h) in zip(scores, CH_HEADS)]
        dzs = [p * (dp - jnp.sum(dp * p, axis=-1, keepdims=True)) for p, dp in zip(probs, dps)]
        for k, dz in enumerate(dzs):
            db_ref[k] += dz
        dzbs = [(dz * ATT_SCALE).astype(BF16) for dz in dzs]
        dqs = [_dot(dz, _only(lanes[h], kws[pair])) for dz, (pair, h) in zip(dzbs, CH_HEADS)]
        dks = [_dot(dz, q, TN) for dz, q in zip(dzbs, qs)]
        dvs = [_dot(p.astype(BF16), do, TN) for p, do in zip(probs, dos)]
        for pair, cols in enumerate(CH_COLS):
            dq_ref[:, cols] = (dqs[2 * pair] + dqs[2 * pair + 1]).astype(BF16)
            dk_s[win, cols] += dks[2 * pair] + dks[2 * pair + 1]
            dv_s[win, cols] += dvs[2 * pair] + dvs[2 * pair + 1]

        @pl.when(n == n_chunks - 1)
        def _():
            dk_ref[...] = dk_s[PAD:, :].astype(BF16)
            dv_ref[...] = dv_s[PAD:, :].astype(BF16)

    full = lambda col: pl.BlockSpec((T, W), lambda n: (0, col))
    blk = lambda col: pl.BlockSpec((CH_Q, W), lambda n: (n, col))
    tab = pl.BlockSpec((N_HEADS, CH_Q, CH_WIN), lambda n: (0, 0, 0))
    out = _out((T, W), BF16)
    return pl.pallas_call(
        lambda after_ref, *refs: body(*refs), name=name, grid=(n_chunks,),
        in_specs=[ANY, blk(3), full(4), full(5), tab, blk(0)],
        out_specs=[blk(0), full(0), full(0), tab],
        out_shape=[out, out, out, _out((N_HEADS, CH_Q, CH_WIN), F32)],
        scratch_shapes=[pltpu.VMEM((PAD + T, W), BF16)] * 2 + [pltpu.VMEM((PAD + T, W), F32)] * 2,
        compiler_params=_params(1, VMEM_LIMIT),
    )(after, *_hbm(qkv, qkv, qkv, bias, do))


def _rows_split(a, parts):
    return a.reshape(a.shape[:-2] + (parts, a.shape[-2] // parts, a.shape[-1]))


def _cast_into_own_slot(me, c, ws, in_chip_order, name):
    parts = 2
    ws = [_rows_split(_rows_split(w, 2), parts) for w in ws]
    n = len(ws)

    def body(me_ref, c_ref, *refs):
        for src, dst in zip(refs[:n], refs[n:]):
            dst[0, 0, 0] = src[0, 0].astype(BF16)

    def specs(w, plain):
        block = (1, 1) + w.shape[2:]
        if plain:
            return (pl.BlockSpec(block, lambda d, r, me_ref, c_ref: (d, r, 0, 0)),
                    pl.BlockSpec((1,) + block, lambda d, r, me_ref, c_ref: (me_ref[0], d, r, 0, 0)))
        return (pl.BlockSpec(block, lambda d, r, me_ref, c_ref: (d ^ c_ref[0], r, 0, 0)),
                pl.BlockSpec((1,) + block, lambda d, r, me_ref, c_ref: (0, d, r, 0, 0)))

    both = [specs(w, plain) for w, plain in zip(ws, in_chip_order)]
    outs = pl.pallas_call(
        body, name=name,
        grid_spec=pltpu.PrefetchScalarGridSpec(
            num_scalar_prefetch=2, grid=(2, parts),
            in_specs=[s[0] for s in both], out_specs=[s[1] for s in both]),
        out_shape=[_out((N_CHIPS,) + w.shape, BF16) for w in ws],
        compiler_params=_params(2, VMEM_LIMIT),
    )(me, c, *_hbm(*ws))
    return [o.reshape(N_CHIPS, 2, o.shape[2] * o.shape[3], o.shape[4]) for o in outs]


def _zone_slots(in_chip_order):
    x, y, c, _ = _place()
    me = 2 * x + y
    if in_chip_order:
        return (me, c), (lambda r: (me, c)), (lambda r: (me ^ r, c)), (lambda r: (me ^ r, c))
    return (0, 0), (lambda r: (r, 0)), (lambda r: (r, 0)), (lambda r: (r, 1))


def _pair_add(c, mine, got, permuted, name):
    parts = 2
    mine = [_rows_split(m, parts) for m in mine]
    got = [_rows_split(g, parts) for g in got]
    n = len(mine)

    def body(c_ref, *refs):
        for a, b, o in zip(refs[:n], refs[n:2 * n], refs[2 * n:]):
            o[0, 0] = (a[0, 0, 0] + b[0, 0].astype(F32)).astype(BF16)

    def mine_spec(m, perm):
        if perm:
            return pl.BlockSpec((1, 1, 1) + m.shape[3:], lambda j, r, c_ref: (j, 0, r, 0, 0))
        return pl.BlockSpec((1, 1, 1) + m.shape[3:], lambda j, r, c_ref: (j, c_ref[0], r, 0, 0))

    def got_spec(g):
        return pl.BlockSpec((1, 1) + g.shape[2:], lambda j, r, c_ref: (j, r, 0, 0))

    outs = pl.pallas_call(
        body, name=name,
        grid_spec=pltpu.PrefetchScalarGridSpec(
            num_scalar_prefetch=1, grid=(N_CHIPS, parts),
            in_specs=[mine_spec(m, perm) for m, perm in zip(mine, permuted)] + [got_spec(g) for g in got],
            out_specs=[got_spec(g) for g in got]),
        out_shape=[_out(g.shape, BF16) for g in got],
        compiler_params=_params(2, VMEM_LIMIT),
    )(c, *_hbm(*mine, *got))
    return [o.reshape(o.shape[0], o.shape[1] * o.shape[2], o.shape[3]) for o in outs]


def _chip_add(me, partials, landed, permuted, name):
    parts = 2
    ps = [_rows_split(x, parts) for x in partials]
    ls = [_rows_split(x, parts) for x in landed]
    n = len(ps)

    def body(me_ref, *refs):
        for own, got, o in zip(refs[:n], refs[n:2 * n], refs[2 * n:]):
            acc = own[0, 0].astype(F32)
            for r in range(N_CHIPS - 1):
                acc = acc + got[r, 0].astype(F32)
            o[0] = acc

    def own_spec(x, perm):
        if perm:
            return pl.BlockSpec((1, 1) + x.shape[2:], lambda r, me_ref: (0, r, 0, 0))
        return pl.BlockSpec((1, 1) + x.shape[2:], lambda r, me_ref: (me_ref[0], r, 0, 0))

    outs = pl.pallas_call(
        body, name=name,
        grid_spec=pltpu.PrefetchScalarGridSpec(
            num_scalar_prefetch=1, grid=(parts,),
            in_specs=[own_spec(x, perm) for x, perm in zip(ps, permuted)]
            + [pl.BlockSpec((N_CHIPS - 1, 1) + x.shape[2:], lambda r, me_ref: (0, r, 0, 0)) for x in ls],
            out_specs=[pl.BlockSpec((1,) + x.shape[2:], lambda r, me_ref: (r, 0, 0)) for x in ps]),
        out_shape=[_out(x.shape[1:], F32) for x in ps],
        compiler_params=_params(1, VMEM_LIMIT),
    )(me, *_hbm(*ps, *ls))
    return [o.reshape(o.shape[0] * o.shape[1], o.shape[2]) for o in outs]


def _adamw_math(w, g, m, v):
    m = ADAM_B1 * m + (1.0 - ADAM_B1) * g
    v = ADAM_B2 * v + (1.0 - ADAM_B2) * (g * g)
    m_hat = m / (1.0 - ADAM_B1 ** ADAM_STEP)
    v_hat = v / (1.0 - ADAM_B2 ** ADAM_STEP)
    delta = -ADAM_LR * (m_hat / (jnp.sqrt(v_hat) + ADAM_EPS) + ADAM_WD * w)
    return delta, m, v


def _adamw(ws, gs, ms, vs, parts, name):
    n = len(ws)
    flat = [_rows_split(a, parts) for a in (*ws, *gs, *ms, *vs)]

    def body(*refs):
        ins, outs = refs[:4 * n], refs[4 * n:]
        for k in range(n):
            d, m, v = _adamw_math(ins[k][...], ins[n + k][...], ins[2 * n + k][...], ins[3 * n + k][...])
            outs[k][...] = d
            outs[n + k][...] = m
            outs[2 * n + k][...] = v

    spec = lambda a: pl.BlockSpec((1,) + a.shape[1:], lambda i: (i, 0, 0))
    outs = pl.pallas_call(
        body, name=name, grid=(parts,),
        in_specs=[spec(a) for a in flat], out_specs=[spec(a) for a in flat[:n]] * 3,
        out_shape=[_out(a.shape, F32) for a in flat[:n]] * 3,
        compiler_params=_params(1, VMEM_LIMIT),
    )(*_hbm(*flat))
    outs = [o.reshape(o.shape[0] * o.shape[1], o.shape[2]) for o in outs]
    return outs[:n], outs[n:2 * n], outs[2 * n:]


def _adamw_halves(c, ws, owns, others, ms, vs, name):
    parts = 4
    n = len(ws)
    whole = [_rows_split(_rows_split(a, 2), parts) for a in (*ws, *ms, *vs)]
    halves = [_rows_split(a, parts) for a in (*owns, *others)]

    def body(c_ref, *refs):
        ins, outs = refs[:5 * n], refs[5 * n:]
        mine = pl.program_id(0) == c_ref[0]
        for k in range(n):
            g = jnp.where(mine, ins[3 * n + k][0], ins[4 * n + k][0])
            d, m, v = _adamw_math(ins[k][0, 0], g, ins[n + k][0, 0], ins[2 * n + k][0, 0])
            for slot, val in enumerate((g, d, m, v)):
                outs[slot * n + k][0, 0] = val

    wspec = lambda a: pl.BlockSpec((1, 1) + a.shape[2:], lambda h, r, c_ref: (h, r, 0, 0))
    hspec = lambda a: pl.BlockSpec((1,) + a.shape[1:], lambda h, r, c_ref: (r, 0, 0))
    outs = pl.pallas_call(
        body, name=name,
        grid_spec=pltpu.PrefetchScalarGridSpec(
            num_scalar_prefetch=1, grid=(2, parts),
            in_specs=[wspec(a) for a in whole] + [hspec(a) for a in halves],
            out_specs=[wspec(a) for a in whole[:n]] * 4),
        out_shape=[_out(a.shape, F32) for a in whole[:n]] * 4,
        compiler_params=_params(2, VMEM_LIMIT),
    )(c, *_hbm(*whole, *halves))
    outs = [o.reshape(2 * parts * o.shape[2], o.shape[3]) for o in outs]
    return outs[:n], outs[n:2 * n], outs[2 * n:3 * n], outs[3 * n:]


def _place():
    x, y, c = lax.axis_index("x"), lax.axis_index("y"), lax.axis_index("c")
    peers = [(x ^ (r >> 1), y ^ (r & 1), c) for r in (1, 2, 3)]
    return x, y, c, peers


def _handshake(peers):
    barrier = pltpu.get_barrier_semaphore()
    for peer in peers:
        pl.semaphore_signal(barrier, inc=1, device_id=peer, device_id_type=MESH)
    pl.semaphore_wait(barrier, len(peers))


ANY = pl.BlockSpec(memory_space=pl.ANY)
HBM = pl.BlockSpec(memory_space=pltpu.HBM)
SEM = pl.BlockSpec(memory_space=pltpu.SEMAPHORE)
SPLIT_COPY = pltpu.SideEffectType.DATAFLOW_SIDE_EFFECTING


def _split_start(body, name, collective_id, operands, n_sems, after=None):
    n = len(operands)
    extra = [] if after is None else [after]

    def wrapped(*refs):
        at = n + len(extra)
        body(refs[:n], refs[at], refs[at + 1])
        token = refs[-1]
        token[...] = jnp.zeros_like(token)

    outs = pl.pallas_call(
        wrapped, name=name,
        in_specs=[HBM] * n + [ANY] * len(extra),
        out_shape=(pltpu.SemaphoreType.DMA((n_sems,)), pltpu.SemaphoreType.DMA((n_sems,)),
                   *[pltpu.HBM(a.shape, a.dtype) for a in operands], jax.ShapeDtypeStruct((8, 128), F32)),
        out_specs=(SEM, SEM, *[HBM] * n, pl.BlockSpec(memory_space=pltpu.VMEM)),
        input_output_aliases={i: 2 + i for i in range(n)},
        compiler_params=pltpu.CompilerParams(has_side_effects=SPLIT_COPY, collective_id=collective_id),
    )(*_hbm(*operands), *extra)
    return outs[0], outs[1], list(outs[2:2 + n]), outs[-1]


def _split_wait(body, name, send_sem, recv_sem, operands, after):
    n = len(operands)

    def wrapped(*refs):
        body(refs[:n], refs[n], refs[n + 1])

    outs = pl.pallas_call(
        wrapped, name=name,
        in_specs=[HBM] * n + [SEM, SEM, ANY],
        out_shape=tuple(pltpu.HBM(a.shape, a.dtype) for a in operands),
        out_specs=tuple([HBM] * n),
        input_output_aliases={i: i for i in range(n)},
        compiler_params=pltpu.CompilerParams(has_side_effects=SPLIT_COPY),
    )(*operands, send_sem, recv_sem, after)
    return list(outs)


def _gather_copies(lands, in_chip_order, send_sem, recv_sem):
    peers = _place()[3]
    copies = []
    for a, (land, plain) in enumerate(zip(lands, in_chip_order)):
        own, sent_to, _, _ = _zone_slots(plain)
        copies += [pltpu.make_async_remote_copy(
            src_ref=land.at[own], dst_ref=land.at[sent_to(r + 1)],
            send_sem=send_sem.at[a * 3 + r], recv_sem=recv_sem.at[a * 3 + r],
            device_id=peers[r], device_id_type=MESH) for r in range(3)]
    return copies


def _gather_start(lands, in_chip_order, name, collective_id, after):
    def body(refs, send_sem, recv_sem):
        _handshake(_place()[3])
        for cp in _gather_copies(refs, in_chip_order, send_sem, recv_sem):
            cp.start()

    return _split_start(body, name, collective_id, list(lands), 3 * len(lands), after)


def _gather_wait(send_sem, recv_sem, operands, in_chip_order, after, name):
    def body(refs, send_sem, recv_sem):
        for cp in _gather_copies(refs, in_chip_order, send_sem, recv_sem):
            cp.wait_send()
            cp.wait_recv()

    return _split_wait(body, name, send_sem, recv_sem, operands, after)


def _gather_finish(lands, in_chip_order, with_ici, name):
    n = len(lands)

    def body(*refs):
        land = refs[n:2 * n]
        send_ici, recv_ici, send_d2d, recv_d2d = refs[2 * n:]
        x, y, c, _ = _place()
        ici = _gather_copies(land, in_chip_order, send_ici, recv_ici) if with_ici else []
        for cp in ici:
            cp.start()
        passed = []
        for a in range(n):
            _, _, received, kept = _zone_slots(in_chip_order[a])
            passed += [pltpu.make_async_remote_copy(
                src_ref=land[a].at[received(r + 1)], dst_ref=land[a].at[kept(r + 1)],
                send_sem=send_d2d.at[a * 3 + r], recv_sem=recv_d2d.at[a * 3 + r],
                device_id=(x, y, 1 - c), device_id_type=MESH) for r in range(3)]
        for k, cp in enumerate(passed):
            if with_ici:
                ici[k].wait_recv()
            cp.start()
        for cp in passed:
            cp.wait_recv()
        for cp in ici:
            cp.wait_send()
        for cp in passed:
            cp.wait_send()

    outs = pl.pallas_call(
        body, name=name,
        in_specs=[ANY] * n, out_specs=[ANY] * n,
        out_shape=[_out(l.shape, l.dtype) for l in lands],
        input_output_aliases={a: a for a in range(n)},
        scratch_shapes=[pltpu.SemaphoreType.DMA((3 * n,))] * 4,
    )(*lands)
    return list(outs)


def _pass_copies(lands, in_chip_order, send_sem, recv_sem):
    x, y, c, _ = _place()
    copies = []
    for a, (land, plain) in enumerate(zip(lands, in_chip_order)):
        _, _, received, kept = _zone_slots(plain)
        copies += [pltpu.make_async_remote_copy(
            src_ref=land.at[received(r + 1)], dst_ref=land.at[kept(r + 1)],
            send_sem=send_sem.at[a * 3 + r], recv_sem=recv_sem.at[a * 3 + r],
            device_id=(x, y, 1 - c), device_id_type=MESH) for r in range(3)]
    return copies


def _pass_start(lands, in_chip_order, name, collective_id):
    def body(refs, send_sem, recv_sem):
        x, y, c, _ = _place()
        _handshake([(x, y, 1 - c)])
        for cp in _pass_copies(refs, in_chip_order, send_sem, recv_sem):
            cp.start()

    return _split_start(body, name, collective_id, list(lands), 3 * len(lands))


def _pass_wait(send_sem, recv_sem, lands, in_chip_order, after, name):
    def body(refs, send_sem, recv_sem):
        for cp in _pass_copies(refs, in_chip_order, send_sem, recv_sem):
            cp.wait_send()
            cp.wait_recv()

    return _split_wait(body, name, send_sem, recv_sem, lands, after)


def _slabs(land):
    return land.reshape(N_CHIPS, 2 * land.shape[2], land.shape[3])


def _pair_swap(grads, permuted, name):
    n = len(grads)

    def body(*refs):
        src, dst = refs[:n], refs[n:2 * n]
        send_sem, recv_sem = refs[2 * n:]
        x, y, c, _ = _place()
        copies = [pltpu.make_async_remote_copy(
            src_ref=src[a].at[:, 1] if permuted[a] else src[a].at[:, 1 - c], dst_ref=dst[a],
            send_sem=send_sem.at[a], recv_sem=recv_sem.at[a],
            device_id=(x, y, 1 - c), device_id_type=MESH) for a in range(n)]
        for cp in copies:
            cp.start()
        for cp in copies:
            cp.wait()

    return pl.pallas_call(
        body, name=name,
        in_specs=[ANY] * n, out_specs=[ANY] * n,
        out_shape=[_out((N_CHIPS,) + g.shape[2:], g.dtype) for g in grads],
        scratch_shapes=[pltpu.SemaphoreType.DMA((n,))] * 2,
    )(*grads)


def _swap_copies(refs, permuted, send_sem, recv_sem):
    n = len(refs) // 2
    x, y, c, _ = _place()
    return [pltpu.make_async_remote_copy(
        src_ref=refs[a].at[:, 1] if permuted[a] else refs[a].at[:, 1 - c], dst_ref=refs[n + a],
        send_sem=send_sem.at[a], recv_sem=recv_sem.at[a],
        device_id=(x, y, 1 - c), device_id_type=MESH) for a in range(n)]


def _pair_swap_start(grads, permuted, name, collective_id):
    def body(refs, send_sem, recv_sem):
        x, y, c, _ = _place()
        _handshake([(x, y, 1 - c)])
        for cp in _swap_copies(refs, permuted, send_sem, recv_sem):
            cp.start()

    lands = [lax.empty((N_CHIPS,) + g.shape[2:], g.dtype) for g in grads]
    return _split_start(body, name, collective_id, list(grads) + lands, len(grads))


def _pair_swap_wait(send_sem, recv_sem, operands, permuted, after, name):
    def body(refs, send_sem, recv_sem):
        for cp in _swap_copies(refs, permuted, send_sem, recv_sem):
            cp.wait_send()
            cp.wait_recv()

    return _split_wait(body, name, send_sem, recv_sem, operands, after)


def _scatter_copies(refs, permuted, send_sem, recv_sem):
    n = len(refs) // 2
    x, y, _, peers = _place()
    me = 2 * x + y
    return [pltpu.make_async_remote_copy(
        src_ref=refs[a].at[r + 1] if permuted[a] else refs[a].at[me ^ (r + 1)], dst_ref=refs[n + a].at[r],
        send_sem=send_sem.at[a * 3 + r], recv_sem=recv_sem.at[a * 3 + r],
        device_id=peers[r], device_id_type=MESH) for a in range(n) for r in range(3)]


def _scatter_start(partials, permuted, name, collective_id):
    def body(refs, send_sem, recv_sem):
        _handshake(_place()[3])
        for cp in _scatter_copies(refs, permuted, send_sem, recv_sem):
            cp.start()

    lands = [lax.empty((N_CHIPS - 1,) + p.shape[1:], p.dtype) for p in partials]
    return _split_start(body, name, collective_id, list(partials) + lands, 3 * len(partials))


def _scatter_wait(send_sem, recv_sem, operands, permuted, after, name):
    def body(refs, send_sem, recv_sem):
        for cp in _scatter_copies(refs, permuted, send_sem, recv_sem):
            cp.wait_send()
            cp.wait_recv()

    return _split_wait(body, name, send_sem, recv_sem, operands, after)


def _pair_join(halves, name):
    n = len(halves)

    def body(*refs):
        src, dst = refs[:n], refs[n:2 * n]
        send_sem, recv_sem = refs[2 * n:]
        x, y, c, _ = _place()
        copies = [pltpu.make_async_remote_copy(
            src_ref=src[a], dst_ref=dst[a], send_sem=send_sem.at[a], recv_sem=recv_sem.at[a],
            device_id=(x, y, 1 - c), device_id_type=MESH) for a in range(n)]
        for cp in copies:
            cp.start()
        for cp in copies:
            cp.wait()

    return pl.pallas_call(
        body, name=name,
        in_specs=[ANY] * n, out_specs=[ANY] * n,
        out_shape=[_out(h.shape, F32) for h in halves],
        scratch_shapes=[pltpu.SemaphoreType.DMA((n,))] * 2,
    )(*halves)


def _all_sum_small(v, after, name):
    R, C = v.shape
    n_dev = 8

    def body(v_ref, after_ref, o_ref, buf, send_sem, recv_sem):
        x, y, c, _ = _place()
        me = 4 * x + 2 * y + c
        buf[me] = v_ref[...]
        copies = []
        for k in range(1, n_dev):
            peer = (x ^ (k >> 2), y ^ ((k >> 1) & 1), c ^ (k & 1))
            copies.append(pltpu.make_async_remote_copy(
                src_ref=v_ref, dst_ref=buf.at[me], send_sem=send_sem.at[k - 1], recv_sem=recv_sem.at[k - 1],
                device_id=peer, device_id_type=MESH))
        for cp in copies:
            cp.start()
        for cp in copies:
            cp.wait()
        acc = buf[0]
        for m in range(1, n_dev):
            acc = acc + buf[m]
        o_ref[...] = acc

    return pl.pallas_call(
        body, name=name,
        in_specs=[pl.BlockSpec(memory_space=pltpu.VMEM), ANY], out_specs=pl.BlockSpec(memory_space=pltpu.VMEM),
        out_shape=jax.ShapeDtypeStruct((R, C), F32),
        scratch_shapes=[pltpu.VMEM((n_dev, R, C), F32), pltpu.SemaphoreType.DMA((n_dev - 1,)),
                        pltpu.SemaphoreType.DMA((n_dev - 1,))],
    )(v, after)


class _WholeWeights:
    def __init__(self, w):
        self.w = w

    def weights(self, group, after=None):
        return ({} if group == "passed" else self.w), None

    def grads_ready(self, group, gw):
        return None

    def grads_sent(self, group, after):
        return None


def _local_step(x, p, target, gains, rel_bias, hooks):
    T, D = x.shape
    S = N_CHIPS

    tied = lambda gain, token: gain if token is None else gain + token[0, 0]
    w, token = hooks.weights("first")
    w = dict(w)
    xn1, g1, u1, a1 = _ffn_up(x, tied(gains["ffn1_pre"], token), w["ffn1_gate"], w["ffn1_up"], "ffn1_up")
    w.update(hooks.weights("down", a1)[0])
    h1, f1 = _ffn_down(x, a1, gains["ffn1_post"], w["ffn1_down"], "ffn1_down")
    more, token = hooks.weights("in", h1)
    w.update(more)
    qkv, un = _norm_proj(h1, tied(gains["mix_pre"], token), w["in"], "qkv_proj")
    bias = _ch_group_bias(_bias_table(rel_bias, "bias_table").transpose(1, 0, 2))
    o_a = _sb_fwd(qkv, "sb_fwd")
    o_b = _ch_fwd(qkv, bias, "ch_fwd")
    more, token = hooks.weights("rest", o_b)
    w.update(more)
    w_out = w["out"].reshape(D, D)
    h2, mixed, mo = _mix_out_fwd(h1, o_a, o_b, gains["out_sb"], gains["out_ch"], w_out,
                                 tied(gains["mix_post"], token), "mix_out_fwd")
    w.update(hooks.weights("passed", h2)[0])
    h3, xn2, g2, u2, a2, f2 = _ffn_fwd(h2, gains["ffn2_pre"], gains["ffn2_post"], w["ffn2_gate"], w["ffn2_up"],
                                       w["ffn2_down"], "ffn2_fwd")
    w_ple_proj = w["ple_proj"].transpose(1, 0, 2).reshape(p.shape[1], D)
    w_ple_gate = w["ple_gate"].reshape(D, D)

    loss, dh3, dproj, dgate, dg_ple = _ple_loss(h3, p, target, w_ple_proj, w_ple_gate, gains["ple_post"], "ple_loss")
    gw, gg = {}, {"ple_post": dg_ple}
    gw["ple_proj"] = _mm_tn(p[None], dproj, p.shape[1], "dw_ple_proj")
    row_sharded = lambda pair: tuple(o.reshape(S, D // S, D) for o in pair)
    gw["ple_gate"] = row_sharded(_mm_tn(h3[None], dgate[None], 512, "dw_ple_gate"))

    def ffn_bwd(tag, dh, x_in, xn, g_act, u_act, a_act, f, group):
        dgp, dup, df, gg[tag + "_post"] = _ffn_bwd_act(dh, f, gains[tag + "_post"], w[tag + "_down"], g_act, u_act,
                                                       tag + "_bwd_act")
        gw[tag + "_gate"] = _mm_tn(dgp, xn[None], dgp.shape[2], "dw_" + tag + "_gate")
        gw[tag + "_up"] = _mm_tn(dup, xn[None], dup.shape[2], "dw_" + tag + "_up")
        gw[tag + "_down"] = _mm_tn(a_act, df[None], a_act.shape[2], "dw_" + tag + "_down")
        g_pre = gains[tag + "_pre"]
        if group is not None:
            token = hooks.grads_ready(group, gw)
            g_pre = g_pre if token is None else g_pre + token[0, 0]
        dx, gg[tag + "_pre"] = _proj_bwd([dgp, dup], [w[tag + "_gate"], w[tag + "_up"]], x_in, g_pre, dh,
                                         tag + "_bwd_in")
        return dx

    dh2 = ffn_bwd("ffn2", dh3, h2, xn2, g2, u2, a2, f2, None)
    dmo, do_a, do_b, gg["mix_post"], gg["out_sb"], gg["out_ch"] = _mix_out_bwd(
        dh2, mo, gains["mix_post"], w_out, o_a, o_b, gains["out_sb"], gains["out_ch"], "mix_out_bwd")
    gw["out"] = row_sharded(_mm_tn(mixed[None], dmo[None], 512, "dw_out"))
    token = hooks.grads_ready("early", gw)
    dq_a, dk_a, dv_a = _sb_bwd(qkv, do_a, o_a, do_a if token is None else token, "sb_bwd")
    token = hooks.grads_sent("early", dq_a)
    dq_b, dk_b, dv_b, dbias = _ch_bwd(qkv, bias, do_b, do_b if token is None else token, "ch_bwd")
    g_rel = _bias_grad(_ch_fold_bias_grad(dbias).transpose(1, 0, 2), "bias_grad")
    dqkv = [dq_a, dk_a, dv_a, dq_b, dk_b, dv_b]
    gw["in"] = _dw_in(un, dqkv, w["in"].shape[2], 512, "dw_in")
    dh1, gg["mix_pre"] = _qkv_bwd_in(dqkv, w["in"], h1, gains["mix_pre"], dh2, "qkv_bwd_in")
    dx = ffn_bwd("ffn1", dh1, x, xn1, g1, u1, a1, f1, "late")
    return loss, dx, gw, gg, g_rel


BIG = ["ffn1_gate", "ffn1_up", "ffn1_down", "in", "out", "ffn2_gate", "ffn2_up", "ffn2_down", "ple_proj", "ple_gate"]
GAINS = ["ffn1_pre", "ffn1_post", "mix_pre", "mix_post", "out_sb", "out_ch", "ffn2_pre", "ffn2_post", "ple_post"]
TRANSPOSED = ("w_ffn1_gate", "w_ffn1_up", "w_ffn2_gate", "w_ffn2_up")
PERMUTED = ("ffn1_gate", "ffn1_up", "ffn1_down", "ffn2_gate", "ffn2_up", "ffn2_down")
W_GROUPS = {"first": ["ffn1_gate", "ffn1_up"], "down": ["ffn1_down"], "in": ["in"],
            "rest": ["out", "ffn2_gate", "ffn2_up", "ffn2_down", "ple_proj", "ple_gate"]}
G_GROUPS = {"early": ["ple_proj", "ple_gate", "ffn2_gate", "ffn2_up", "ffn2_down", "out"],
            "late": ["in", "ffn1_gate", "ffn1_up", "ffn1_down"]}
ORDER = ["g_ffn1_pre", "g_ffn1_post", "w_ffn1_gate", "w_ffn1_up", "w_ffn1_down", "g_mix_pre", "g_mix_post", "w_in",
         "g_out_sb", "g_out_ch", "rel_bias", "w_out", "g_ffn2_pre", "g_ffn2_post", "w_ffn2_gate", "w_ffn2_up",
         "w_ffn2_down", "w_ple_proj", "w_ple_gate", "g_ple_post"]


def kernel(x, p, g_ffn1_pre, g_ffn1_post, w_ffn1_gate, w_ffn1_up, w_ffn1_down, g_mix_pre, g_mix_post, w_in, g_out_sb, g_out_ch, rel_bias, w_out, g_ffn2_pre, g_ffn2_post, w_ffn2_gate, w_ffn2_up, w_ffn2_down, w_ple_proj, w_ple_gate, g_ple_post, loss_target, m_g_ffn1_pre, m_g_ffn1_post, m_w_ffn1_gate, m_w_ffn1_up, m_w_ffn1_down, m_g_mix_pre, m_g_mix_post, m_w_in, m_g_out_sb, m_g_out_ch, m_rel_bias, m_w_out, m_g_ffn2_pre, m_g_ffn2_post, m_w_ffn2_gate, m_w_ffn2_up, m_w_ffn2_down, m_w_ple_proj, m_w_ple_gate, m_g_ple_post, v_g_ffn1_pre, v_g_ffn1_post, v_w_ffn1_gate, v_w_ffn1_up, v_w_ffn1_down, v_g_mix_pre, v_g_mix_post, v_w_in, v_g_out_sb, v_g_out_ch, v_rel_bias, v_w_out, v_g_ffn2_pre, v_g_ffn2_post, v_w_ffn2_gate, v_w_ffn2_up, v_w_ffn2_down, v_w_ple_proj, v_w_ple_gate, v_g_ple_post):
    args = dict(locals())
    take = lambda a, n: a[0].T if n in TRANSPOSED else a[0]
    wts = {n: take(args[n], n) for n in ORDER}
    ms = {n: take(args["m_" + n], n) for n in ORDER}
    vs = {n: take(args["v_" + n], n) for n in ORDER}
    gains = {n: wts["g_" + n][None] for n in GAINS}

    c_idx = lax.axis_index("c").astype(jnp.int32).reshape(1)
    me_idx = (2 * lax.axis_index("x") + lax.axis_index("y")).astype(jnp.int32).reshape(1)
    south = lax.axis_index("c") == 0

    plain = lambda names: [n not in PERMUTED for n in names]
    lands = dict(zip(BIG, _cast_into_own_slot(me_idx, c_idx, [wts["w_" + n] for n in BIG], plain(BIG), "cast_weights")))

    class Overlapped:
        def __init__(self):
            self.started = {}
            self.flying = {}

        def start(self, group, collective_id, after):
            names = W_GROUPS[group]
            self.flying[group] = _gather_start([lands[n] for n in names], plain(names), "gather_%s_start" % group,
                                               collective_id, after)
            return self.flying[group][3]

        def weights(self, group, after=None):
            names = W_GROUPS.get(group)
            token = None
            if group == "first":
                zones = _gather_finish([lands[n] for n in names], plain(names), True, "gather_first")
                token = self.start("rest", 4, self.start("in", 1, self.start("down", 6, zones[0])))
            elif group == "passed":
                names, (send_sem, recv_sem, zones, _) = self.passing
                zones = _pass_wait(send_sem, recv_sem, zones, plain(names), after, "gather_rest_pass_wait")
            else:
                send_sem, recv_sem, zones, _ = self.flying[group]
                zones = _gather_wait(send_sem, recv_sem, zones, plain(names), after, "gather_%s_wait" % group)
                if group == "rest":
                    self.passing = names[1:], _pass_start(zones[1:], plain(names[1:]), "gather_rest_pass_start", 7)
                    names, zones, token = names[:1], zones[:1], self.passing[1][3]
                zones = _gather_finish(zones, plain(names), False, "gather_%s_finish" % group)
            return {n: _slabs(z) for n, z in zip(names, zones)}, token

        def grads_ready(self, group, gw):
            names = G_GROUPS[group]
            perm = [n in PERMUTED for n in names]
            halved = lambda g: g.reshape(N_CHIPS, 2, g.shape[1] // 2, g.shape[2])
            mine = [halved(gw[n][0]) for n in names]
            narrow = [halved(gw[n][1]) for n in names]
            if group == "late":
                return self.scatter(group, names, perm, mine, _pair_swap(narrow, perm, "grad_pair_swap_late"))
            self.swapping = names, perm, mine, _pair_swap_start(narrow, perm, "grad_pair_swap_start_early", 5)
            return self.swapping[3][3]

        def grads_sent(self, group, after):
            names, perm, mine, (send_sem, recv_sem, operands, _) = self.swapping
            operands = _pair_swap_wait(send_sem, recv_sem, operands, perm, after, "grad_pair_swap_wait_early")
            return self.scatter(group, names, perm, mine, operands[len(names):])

        def scatter(self, group, names, perm, mine, got):
            partial = _pair_add(c_idx, mine, got, perm, "grad_pair_add_" + group)
            send_sem, recv_sem, operands, token = _scatter_start(partial, perm, "grad_scatter_start_" + group,
                                                                 {"early": 2, "late": 3}[group])
            self.started[group] = names, perm, send_sem, recv_sem, operands, token
            return token

    def reduce_finish(state, after, tag):
        names, perm, send_sem, recv_sem, operands, _ = state
        operands = _scatter_wait(send_sem, recv_sem, operands, perm, after, "grad_scatter_wait_" + tag)
        n = len(names)
        own = _chip_add(me_idx, operands[:n], operands[n:], perm, "grad_chip_add_" + tag)
        return own, _pair_join(own, "grad_pair_join_" + tag)

    hooks = Overlapped()
    loss, dx, gw, gg, g_rel = _local_step(x[0], p[0, 0], loss_target[0], gains, wts["rel_bias"], hooks)

    grads, delta, new_m, new_v = {}, {}, {}, {}

    def finish(group, after):
        own, other = reduce_finish(hooks.started[group], after, group)
        names = ["w_" + n for n in G_GROUPS[group]]
        g, d, m, v = _adamw_halves(c_idx, [wts[n] for n in names], own, other, [ms[n] for n in names],
                                   [vs[n] for n in names], "adamw_" + group)
        for n, gg_, dd, mm, vv in zip(names, g, d, m, v):
            grads[n], delta[n], new_m[n], new_v[n] = gg_, dd, mm, vv
        return d[0]

    finish("late", finish("early", dx))

    pieces = [gg[n].reshape(-1, 128) for n in GAINS] + [jnp.pad(g_rel, ((0, 0), (0, N_REL_PAD - N_REL))).reshape(-1, 128)]
    summed = _all_sum_small(jnp.concatenate(pieces + [loss], axis=0), delta["w_in"], "small_grad_sum")
    at = 0
    for n, piece in zip(GAINS, pieces[:-1]):
        grads["g_" + n] = summed[at:at + piece.shape[0]].reshape(1, -1)[0]
        at += piece.shape[0]
    grads["rel_bias"] = summed[at:at + pieces[-1].shape[0]].reshape(N_HEADS, N_REL_PAD)[:, :N_REL]
    loss = summed[at + pieces[-1].shape[0], 0]

    small = ["g_" + n for n in GAINS] + ["rel_bias"]
    as_rows = lambda a: (a.reshape(-1, 128) if a.size % 128 == 0 else jnp.pad(a, ((0, 0), (0, N_REL_PAD - N_REL))).reshape(-1, 128))
    d, m, v = _adamw([as_rows(wts[n]) for n in small], [as_rows(grads[n]) for n in small],
                     [as_rows(ms[n]) for n in small], [as_rows(vs[n]) for n in small], 1, "adamw_small")
    for n, dd, mm, vv in zip(small, d, m, v):
        back = (lambda a: a.reshape(N_HEADS, N_REL_PAD)[:, :N_REL]) if n == "rel_bias" else (lambda a: a.reshape(-1))
        delta[n], new_m[n], new_v[n] = back(dd), back(mm), back(vv)

    outs = [loss, dx[None]]
    for table in (grads, delta, new_m, new_v):
        outs += [(table[n].T if n in TRANSPOSED else table[n])[None] for n in ORDER]
    return tuple(outs)
```

```python
import jax
import jax.numpy as jnp
from jax import lax
from jax.experimental import pallas as pl
from jax.experimental.pallas import tpu as pltpu

F32 = jnp.float32
BF16 = jnp.bfloat16
EPS = 1e-6
N_CHIPS = 4
HEAD_DIM = 64
N_HEADS = 8
CHUNK = 64
LOOKBACK = 8
BAND = (LOOKBACK + 1) * CHUNK
PAD = LOOKBACK * CHUNK
REL_CLIP = 128
N_REL = 2 * REL_CLIP + 1
N_REL_PAD = 384
SB_BLOCK = 256
PAIR = 2 * HEAD_DIM
SB_PAIRS = 2
SB_FWD_PAIRS = 4
ATT_SCALE = HEAD_DIM ** -0.5
NEG_INF = -1e30
ROW_BLOCK = 512
WIDE_ROW_BLOCK = 1024
VMEM_LIMIT_WIDE = 56 * 1024 * 1024
VMEM_LIMIT = 48 * 1024 * 1024
MESH = pl.DeviceIdType.MESH

ADAM_LR = 0.001
ADAM_B1 = 0.9
ADAM_B2 = 0.999
ADAM_EPS = 1e-08
ADAM_WD = 0.01
ADAM_STEP = 10

NT = (((1,), (1,)), ((), ()))
TN = (((0,), (0,)), ((), ()))


def _params(n_grid, vmem=None):
    return pltpu.CompilerParams(dimension_semantics=("arbitrary",) * n_grid, vmem_limit_bytes=vmem)


def _hbm(*arrays):
    return [pltpu.with_memory_space_constraint(a, pltpu.HBM) for a in arrays]


def _out(shape, dtype):
    return pltpu.HBM(shape, dtype)


def _dot(a, b, dims=None):
    if dims is None:
        return jnp.dot(a, b, preferred_element_type=F32)
    return lax.dot_general(a, b, dims, preferred_element_type=F32)


def _sigmoid(x):
    return 1.0 / (1.0 + jnp.exp(-x))


def _rms_fwd(x, g):
    r = lax.rsqrt(jnp.mean(x * x, axis=-1, keepdims=True) + EPS)
    return x * r * g


def _rms_bwd(x, g, dy):
    r = lax.rsqrt(jnp.mean(x * x, axis=-1, keepdims=True) + EPS)
    xh = x * r
    dg = jnp.sum(dy * xh, axis=0, keepdims=True)
    t = dy * g
    dx = r * (t - xh * jnp.mean(t * xh, axis=-1, keepdims=True))
    return dx, dg


def _accumulate(ref, val, first):
    @pl.when(first)
    def _():
        ref[...] = val

    @pl.when(jnp.logical_not(first))
    def _():
        ref[...] += val


def _split2(x):
    hi = x.astype(BF16)
    lo = (x - hi.astype(F32)).astype(BF16)
    return hi, lo


def _ffn_fwd(x, g_pre, g_post, wg, wu, wd, name):
    T, D = x.shape
    S, FS, _ = wg.shape
    tm = min(WIDE_ROW_BLOCK, T)

    def body(x_ref, gpre_ref, gpost_ref, wg_ref, wu_ref, wd_ref,
             h_ref, xn_ref, g_ref, u_ref, a_ref, f_ref):
        k = pl.program_id(1)

        @pl.when(k == 0)
        def _():
            xn_ref[...] = _rms_fwd(x_ref[...], gpre_ref[...]).astype(BF16)

        xn = xn_ref[...]
        g = _dot(xn, wg_ref[0], NT)
        u = _dot(xn, wu_ref[0], NT)
        g_ref[0] = g
        u_ref[0] = u
        a = (g * _sigmoid(g) * u).astype(BF16)
        a_ref[0] = a
        _accumulate(f_ref, _dot(a, wd_ref[0]), k == 0)

        @pl.when(k == S - 1)
        def _():
            h_ref[...] = x_ref[...] + 0.5 * _rms_fwd(f_ref[...], gpost_ref[...])

    row = pl.BlockSpec((tm, D), lambda i, k: (i, 0))
    vec = pl.BlockSpec((1, D), lambda i, k: (0, 0))
    act = pl.BlockSpec((1, tm, FS), lambda i, k: (k, i, 0))
    return pl.pallas_call(
        body, name=name, grid=(T // tm, S),
        in_specs=[row, vec, vec] + [pl.BlockSpec((1, FS, D), lambda i, k: (k, 0, 0))] * 3,
        out_specs=[row, row, act, act, act, row],
        out_shape=[_out((T, D), F32), _out((T, D), BF16),
                   _out((S, T, FS), F32), _out((S, T, FS), F32),
                   _out((S, T, FS), BF16), _out((T, D), F32)],
        compiler_params=_params(2, VMEM_LIMIT_WIDE),
    )(*_hbm(x, g_pre, g_post, wg, wu, wd))


def _ffn_up(x, g_pre, wg, wu, name):
    T, D = x.shape
    S, FS, _ = wg.shape
    tm = min(WIDE_ROW_BLOCK, T)

    def body(x_ref, gpre_ref, wg_ref, wu_ref, xn_ref, g_ref, u_ref, a_ref):
        @pl.when(pl.program_id(1) == 0)
        def _():
            xn_ref[...] = _rms_fwd(x_ref[...], gpre_ref[...]).astype(BF16)

        xn = xn_ref[...]
        g = _dot(xn, wg_ref[0], NT)
        u = _dot(xn, wu_ref[0], NT)
        g_ref[0] = g
        u_ref[0] = u
        a_ref[0] = (g * _sigmoid(g) * u).astype(BF16)

    row = pl.BlockSpec((tm, D), lambda i, k: (i, 0))
    act = pl.BlockSpec((1, tm, FS), lambda i, k: (k, i, 0))
    return pl.pallas_call(
        body, name=name, grid=(T // tm, S),
        in_specs=[row, pl.BlockSpec((1, D), lambda i, k: (0, 0))] + [pl.BlockSpec((1, FS, D), lambda i, k: (k, 0, 0))] * 2,
        out_specs=[row, act, act, act],
        out_shape=[_out((T, D), BF16), _out((S, T, FS), F32), _out((S, T, FS), F32), _out((S, T, FS), BF16)],
        compiler_params=_params(2, VMEM_LIMIT_WIDE),
    )(*_hbm(x, g_pre, wg, wu))


def _ffn_down(x, a, g_post, wd, name):
    T, D = x.shape
    S, FS, _ = wd.shape
    tm = min(WIDE_ROW_BLOCK, T)

    def body(x_ref, a_ref, gpost_ref, wd_ref, h_ref, f_ref):
        k = pl.program_id(1)
        _accumulate(f_ref, _dot(a_ref[0], wd_ref[0]), k == 0)

        @pl.when(k == S - 1)
        def _():
            h_ref[...] = x_ref[...] + 0.5 * _rms_fwd(f_ref[...], gpost_ref[...])

    row = pl.BlockSpec((tm, D), lambda i, k: (i, 0))
    return pl.pallas_call(
        body, name=name, grid=(T // tm, S),
        in_specs=[row, pl.BlockSpec((1, tm, FS), lambda i, k: (k, i, 0)), pl.BlockSpec((1, D), lambda i, k: (0, 0)),
                  pl.BlockSpec((1, FS, D), lambda i, k: (k, 0, 0))],
        out_specs=[row, row],
        out_shape=[_out((T, D), F32), _out((T, D), F32)],
        compiler_params=_params(2, VMEM_LIMIT_WIDE),
    )(*_hbm(x, a, g_post, wd))


def _ffn_bwd_act(dh, f, g_post, wd, g_act, u_act, name):
    T, D = dh.shape
    S, FS, _ = wd.shape
    tm = min(WIDE_ROW_BLOCK, T)

    def body(dh_ref, f_ref, gpost_ref, wd_ref, g_ref, u_ref, dgp_ref, dup_ref, df_ref, dgain_ref, df_s):
        i, k = pl.program_id(0), pl.program_id(1)

        @pl.when(k == 0)
        def _():
            df, dgain = _rms_bwd(f_ref[...], gpost_ref[...], 0.5 * dh_ref[...])
            df_s[...] = df.astype(BF16)
            df_ref[...] = df_s[...]
            _accumulate(dgain_ref, dgain, i == 0)

        da = _dot(df_s[...], wd_ref[0], NT)
        g = g_ref[0]
        s = _sigmoid(g)
        dup_ref[0] = (da * (g * s)).astype(BF16)
        dgp_ref[0] = (da * u_ref[0] * (s * (1.0 + g * (1.0 - s)))).astype(BF16)

    row = pl.BlockSpec((tm, D), lambda i, k: (i, 0))
    vec = pl.BlockSpec((1, D), lambda i, k: (0, 0))
    act = pl.BlockSpec((1, tm, FS), lambda i, k: (k, i, 0))
    return pl.pallas_call(
        body, name=name, grid=(T // tm, S),
        in_specs=[row, row, vec, pl.BlockSpec((1, FS, D), lambda i, k: (k, 0, 0)), act, act],
        out_specs=[act, act, row, vec],
        out_shape=[_out((S, T, FS), BF16), _out((S, T, FS), BF16),
                   _out((T, D), BF16), _out((1, D), F32)],
        scratch_shapes=[pltpu.VMEM((tm, D), BF16)],
        compiler_params=_params(2, VMEM_LIMIT_WIDE),
    )(*_hbm(dh, f, g_post, wd, g_act, u_act))


def _proj_bwd(dys, ws, x, g_pre, dh, name):
    T, D = x.shape
    n = len(dys)
    S, N, _ = ws[0].shape
    tm = min(WIDE_ROW_BLOCK, T)

    def body(*refs):
        dy_refs, w_refs = refs[:n], refs[n:2 * n]
        x_ref, gpre_ref, dh_ref, dx_ref, dgain_ref, acc_s = refs[2 * n:]
        i, k = pl.program_id(0), pl.program_id(1)
        part = None
        for dy_ref, w_ref in zip(dy_refs, w_refs):
            term = _dot(dy_ref[0], w_ref[0])
            part = term if part is None else part + term
        _accumulate(acc_s, part, k == 0)

        @pl.when(k == S - 1)
        def _():
            dx, dgain = _rms_bwd(x_ref[...], gpre_ref[...], acc_s[...])
            dx_ref[...] = dh_ref[...] + dx
            _accumulate(dgain_ref, dgain, i == 0)

    row = pl.BlockSpec((tm, D), lambda i, k: (i, 0))
    vec = pl.BlockSpec((1, D), lambda i, k: (0, 0))
    return pl.pallas_call(
        body, name=name, grid=(T // tm, S),
        in_specs=[pl.BlockSpec((1, tm, N), lambda i, k: (k, i, 0))] * n
        + [pl.BlockSpec((1, N, D), lambda i, k: (k, 0, 0))] * n + [row, vec, row],
        out_specs=[row, vec],
        out_shape=[_out((T, D), F32), _out((1, D), F32)],
        scratch_shapes=[pltpu.VMEM((tm, D), F32)],
        compiler_params=_params(2, VMEM_LIMIT_WIDE),
    )(*_hbm(*dys, *ws, x, g_pre, dh))


def _mm_tn(a, b, bm, name):
    ga, T, M = a.shape
    gb, _, N = b.shape
    b_spec = pl.BlockSpec((1, T, N), (lambda g, m: (g, 0, 0)) if gb > 1 else (lambda g, m: (0, 0, 0)))
    G = max(ga, gb)

    def body(a_ref, b_ref, o_ref, narrow_ref):
        o_ref[0] = _dot(a_ref[0].astype(BF16), b_ref[0].astype(BF16), TN)
        narrow_ref[0] = o_ref[0].astype(BF16)

    out = pl.BlockSpec((1, bm, N), lambda g, m: (g, m, 0))
    return pl.pallas_call(
        body, name=name, grid=(G, M // bm),
        in_specs=[pl.BlockSpec((1, T, bm), (lambda g, m: (g, 0, m)) if ga > 1 else (lambda g, m: (0, 0, m))), b_spec],
        out_specs=[out, out],
        out_shape=[_out((G, M, N), F32), _out((G, M, N), BF16)],
        compiler_params=_params(2, VMEM_LIMIT),
    )(*_hbm(a, b))


QKV_PIECE = 256


def _qkv_shard(dy_refs, k, n_col):
    width = dy_refs[0].shape[1]
    parts = []
    for col in range(k * n_col, (k + 1) * n_col, QKV_PIECE):
        parts.append(dy_refs[col // width][:, col % width:col % width + QKV_PIECE])
    return jnp.concatenate(parts, axis=1)


def _qkv_bwd_in(dys, w, x, g_pre, dh, name):
    T, D = x.shape
    n = len(dys)
    S, _, N = w.shape
    tm = min(WIDE_ROW_BLOCK, T)

    def body(*refs):
        dy_refs = refs[:n]
        w_ref, x_ref, gpre_ref, dh_ref, dx_ref, dgain_ref, acc_s = refs[n:]
        i, k = pl.program_id(0), pl.program_id(1)
        for shard in range(S):
            @pl.when(k == shard)
            def _(shard=shard):
                part = _dot(_qkv_shard(dy_refs, shard, N), w_ref[0], NT)
                if shard == 0:
                    acc_s[...] = part
                else:
                    acc_s[...] += part

        @pl.when(k == S - 1)
        def _():
            dx, dgain = _rms_bwd(x_ref[...], gpre_ref[...], acc_s[...])
            dx_ref[...] = dh_ref[...] + dx
            _accumulate(dgain_ref, dgain, i == 0)

    row = pl.BlockSpec((tm, D), lambda i, k: (i, 0))
    vec = pl.BlockSpec((1, D), lambda i, k: (0, 0))
    return pl.pallas_call(
        body, name=name, grid=(T // tm, S),
        in_specs=[pl.BlockSpec((tm, dy.shape[1]), lambda i, k: (i, 0)) for dy in dys]
        + [pl.BlockSpec((1, D, N), lambda i, k: (k, 0, 0)), row, vec, row],
        out_specs=[row, vec],
        out_shape=[_out((T, D), F32), _out((1, D), F32)],
        scratch_shapes=[pltpu.VMEM((tm, D), F32)],
        compiler_params=_params(2, VMEM_LIMIT_WIDE),
    )(*_hbm(*dys, w, x, g_pre, dh))


def _dw_in(a, dys, n_col, bm, name):
    T, M = a.shape
    n = len(dys)
    S = n * dys[0].shape[1] // n_col

    def body(*refs):
        a_ref, dy_refs = refs[0], refs[1:1 + n]
        o_ref, narrow_ref = refs[1 + n:]
        k = pl.program_id(1)
        for shard in range(S):
            @pl.when(k == shard)
            def _(shard=shard):
                o_ref[0] = _dot(a_ref[...], _qkv_shard(dy_refs, shard, n_col), TN)
                narrow_ref[0] = o_ref[0].astype(BF16)

    out = pl.BlockSpec((1, bm, n_col), lambda m, k: (k, m, 0))
    return pl.pallas_call(
        body, name=name, grid=(M // bm, S),
        in_specs=[pl.BlockSpec((T, bm), lambda m, k: (0, m))]
        + [pl.BlockSpec((T, dy.shape[1]), lambda m, k: (0, 0)) for dy in dys],
        out_specs=[out, out],
        out_shape=[_out((S, M, n_col), F32), _out((S, M, n_col), BF16)],
        compiler_params=_params(2, VMEM_LIMIT_WIDE),
    )(*_hbm(a, *dys))


def _norm_proj(x, g_pre, w, name):
    T, D = x.shape
    S, _, N = w.shape
    tm = min(WIDE_ROW_BLOCK, T)

    def body(x_ref, g_ref, w_ref, o_ref, xn_ref, xn_s):
        @pl.when(pl.program_id(1) == 0)
        def _():
            xn_s[...] = _rms_fwd(x_ref[...], g_ref[...]).astype(BF16)
            xn_ref[...] = xn_s[...]

        o_ref[...] = _dot(xn_s[...], w_ref[0]).astype(BF16)

    row = pl.BlockSpec((tm, D), lambda i, k: (i, 0))
    return pl.pallas_call(
        body, name=name, grid=(T // tm, S),
        in_specs=[row, pl.BlockSpec((1, D), lambda i, k: (0, 0)), pl.BlockSpec((1, D, N), lambda i, k: (k, 0, 0))],
        out_specs=[pl.BlockSpec((tm, N), lambda i, k: (i, k)), row],
        out_shape=[_out((T, S * N), BF16), _out((T, D), BF16)],
        scratch_shapes=[pltpu.VMEM((tm, D), BF16)],
        compiler_params=_params(2, VMEM_LIMIT_WIDE),
    )(*_hbm(x, g_pre, w))


def _mix_out_fwd(h, o_a, o_b, g_sb, g_ch, w_out, g_post, name):
    T, D = h.shape
    W = g_sb.shape[1]
    tm = min(WIDE_ROW_BLOCK, T)

    def body(h_ref, oa_ref, ob_ref, gsb_ref, gch_ref, w_ref, gpost_ref, h2_ref, mixed_ref, mo_ref):
        mixed_ref[:, :W] = _rms_fwd(oa_ref[...], gsb_ref[...]).astype(BF16)
        mixed_ref[:, W:] = _rms_fwd(ob_ref[...], gch_ref[...]).astype(BF16)
        mo = _dot(mixed_ref[...], w_ref[...])
        mo_ref[...] = mo
        h2_ref[...] = h_ref[...] + _rms_fwd(mo, gpost_ref[...])

    row = pl.BlockSpec((tm, D), lambda i: (i, 0))
    part = pl.BlockSpec((tm, W), lambda i: (i, 0))
    half = pl.BlockSpec((1, W), lambda i: (0, 0))
    return pl.pallas_call(
        body, name=name, grid=(T // tm,),
        in_specs=[row, part, part, half, half, pl.BlockSpec((D, D), lambda i: (0, 0)), pl.BlockSpec((1, D), lambda i: (0, 0))],
        out_specs=[row, row, row],
        out_shape=[_out((T, D), F32), _out((T, D), BF16),
                   _out((T, D), F32)],
        compiler_params=_params(1, VMEM_LIMIT_WIDE),
    )(*_hbm(h, o_a, o_b, g_sb, g_ch, w_out, g_post))


def _mix_out_bwd(dh, mo, g_post, w_out, o_a, o_b, g_sb, g_ch, name):
    T, D = dh.shape
    W = g_sb.shape[1]
    tm = min(WIDE_ROW_BLOCK, T)

    def body(dh_ref, mo_ref, gpost_ref, w_ref, oa_ref, ob_ref, gsb_ref, gch_ref,
             dmo_ref, doa_ref, dob_ref, dgpost_ref, dgsb_ref, dgch_ref):
        first = pl.program_id(0) == 0
        dmo, dgpost = _rms_bwd(mo_ref[...], gpost_ref[...], dh_ref[...])
        dmo_ref[...] = dmo.astype(BF16)
        dmix = _dot(dmo_ref[...], w_ref[...], NT)
        doa_ref[...], dgsb = _rms_bwd(oa_ref[...], gsb_ref[...], dmix[:, :W])
        dob_ref[...], dgch = _rms_bwd(ob_ref[...], gch_ref[...], dmix[:, W:])
        _accumulate(dgpost_ref, dgpost, first)
        _accumulate(dgsb_ref, dgsb, first)
        _accumulate(dgch_ref, dgch, first)

    row = pl.BlockSpec((tm, D), lambda i: (i, 0))
    part = pl.BlockSpec((tm, W), lambda i: (i, 0))
    vec = pl.BlockSpec((1, D), lambda i: (0, 0))
    half = pl.BlockSpec((1, W), lambda i: (0, 0))
    return pl.pallas_call(
        body, name=name, grid=(T // tm,),
        in_specs=[row, row, vec, pl.BlockSpec((D, D), lambda i: (0, 0)), part, part, half, half],
        out_specs=[row, part, part, vec, half, half],
        out_shape=[_out((T, D), BF16), _out((T, W), F32),
                   _out((T, W), F32), _out((1, D), F32),
                   _out((1, W), F32), _out((1, W), F32)],
        compiler_params=_params(1, VMEM_LIMIT_WIDE),
    )(*_hbm(dh, mo, g_post, w_out, o_a, o_b, g_sb, g_ch))


def _ple_loss(h, p, target, w_proj, w_gate, g_post, name):
    T, D = h.shape
    P = p.shape[1]
    S = N_CHIPS
    C = D // S
    tm = min(ROW_BLOCK, T)

    def body(h_ref, p_ref, t_ref, wp_ref, wg_ref, g_ref, loss_ref, dh_ref, dproj_ref, dgate_ref, dgain_ref):
        first = pl.program_id(0) == 0
        h3 = h_ref[...]
        proj = _dot(p_ref[...].astype(BF16), wp_ref[...])
        s = _sigmoid(_dot(h3.astype(BF16), wg_ref[...]))
        e = proj * s
        diff = h3 + _rms_fwd(e, g_ref[...]) - t_ref[...]
        part = 0.5 * jnp.sum(jnp.mean(diff * diff, axis=-1, keepdims=True), axis=0, keepdims=True)
        _accumulate(loss_ref, jnp.broadcast_to(part, loss_ref.shape), first)
        dy = diff * (1.0 / D)
        de, dgain = _rms_bwd(e, g_ref[...], dy)
        _accumulate(dgain_ref, dgain, first)
        dproj = (de * s).astype(BF16)
        for j in range(S):
            dproj_ref[j] = dproj[:, j * C:(j + 1) * C]
        dgate_ref[...] = (de * proj * s * (1.0 - s)).astype(BF16)
        dh_ref[...] = dy + _dot(dgate_ref[...], wg_ref[...], NT)

    row = pl.BlockSpec((tm, D), lambda i: (i, 0))
    vec = pl.BlockSpec((1, D), lambda i: (0, 0))
    return pl.pallas_call(
        body, name=name, grid=(T // tm,),
        in_specs=[row, pl.BlockSpec((tm, P), lambda i: (i, 0)), row,
                  pl.BlockSpec((P, D), lambda i: (0, 0)), pl.BlockSpec((D, D), lambda i: (0, 0)), vec],
        out_specs=[pl.BlockSpec((8, 128), lambda i: (0, 0)), row,
                   pl.BlockSpec((S, tm, C), lambda i: (0, i, 0)), row, vec],
        out_shape=[_out((8, 128), F32), _out((T, D), F32),
                   _out((S, T, C), BF16), _out((T, D), BF16),
                   _out((1, D), F32)],
        compiler_params=_params(1, VMEM_LIMIT_WIDE),
    )(*_hbm(h, p, target, w_proj, w_gate, g_post))


def _sb_scores(q, kj, mask):
    z = _dot(q, kj, NT)
    sp = jnp.maximum(z, 0.0) + jnp.log(1.0 + jnp.exp(-jnp.abs(z)))
    return z, sp if mask is None else jnp.where(mask, sp, 0.0)


def _strict_causal():
    rows = lax.broadcasted_iota(jnp.int32, (SB_BLOCK, SB_BLOCK), 0)
    cols = lax.broadcasted_iota(jnp.int32, (SB_BLOCK, SB_BLOCK), 1)
    return cols < rows


def _tri(cmp):
    r = lax.broadcasted_iota(jnp.int32, (2 * SB_BLOCK, SB_BLOCK), 0) % SB_BLOCK
    c = lax.broadcasted_iota(jnp.int32, (2 * SB_BLOCK, SB_BLOCK), 1)
    return jnp.where(cmp(r, c), 1.0, 0.0).astype(BF16)


def _cum(x, tri):
    return _dot(jnp.concatenate(_split2(x), axis=1), tri)


def _pair_lanes():
    lane = lax.broadcasted_iota(jnp.int32, (1, PAIR), 1)
    return [lane < HEAD_DIM, lane >= HEAD_DIM]


def _only(lanes, x):
    return jnp.where(lanes, x, jnp.zeros_like(x))


def _sb_fwd(qkv, name):
    T = qkv.shape[0]
    B = SB_BLOCK
    W = SB_FWD_PAIRS * PAIR
    steps = N_HEADS // (2 * SB_FWD_PAIRS)
    heads = [(p, h) for p in range(SB_FWD_PAIRS) for h in range(2)]

    def body(q_ref, k_ref, v_ref, o_ref):
        i = pl.program_id(1)
        after = _tri(lambda r, c: r > c)
        lanes = _pair_lanes()
        cols = [slice(p * PAIR, (p + 1) * PAIR) for p in range(SB_FWD_PAIRS)]
        q = {(p, h): _only(lanes[h], q_ref[:, cols[p]] * ATT_SCALE) for p, h in heads}

        def tiles(j, carries, mask):
            at = pl.ds(pl.multiple_of(j * B, B), B)
            scores = [_sb_scores(q[ph], k_ref[at, cols[ph[0]]], mask) for ph in heads]
            laters = [_cum(sp, after) for _, sp in scores]
            out = []
            for ph, (z, sp), later, (run, acc) in zip(heads, scores, laters, carries):
                a = jnp.exp(z - sp - later - run)
                if mask is not None:
                    a = jnp.where(mask, a, 0.0)
                out.append((run + later[:, 0:1] + sp[:, 0:1],
                            acc + _dot(a.astype(BF16), _only(lanes[ph[1]], v_ref[at, cols[ph[0]]]))))
            return tuple(out)

        zero = (jnp.zeros((B, 1), F32), jnp.zeros((B, PAIR), F32))
        carries = tiles(i, (zero,) * len(heads), _strict_causal())
        carries = lax.fori_loop(0, i, lambda jj, cs: tiles(i - 1 - jj, cs, None), carries)
        for p in range(SB_FWD_PAIRS):
            o_ref[:, cols[p]] = carries[2 * p][1] + carries[2 * p + 1][1]

    blk = lambda off: pl.BlockSpec((B, W), lambda g, i: (i, g + off))
    full = lambda off: pl.BlockSpec((T, W), lambda g, i: (0, g + off))
    return pl.pallas_call(
        body, name=name, grid=(steps, T // B),
        in_specs=[blk(0), full(steps), full(2 * steps)],
        out_specs=blk(0),
        out_shape=_out((T, N_HEADS * HEAD_DIM), F32),
        compiler_params=_params(2, VMEM_LIMIT),
    )(*_hbm(qkv, qkv, qkv))


def _sb_bwd(qkv, do, o, after, name):
    T = qkv.shape[0]
    B = SB_BLOCK
    W = SB_PAIRS * PAIR
    steps = N_HEADS // (2 * SB_PAIRS)
    n_blocks = T // B
    heads = [(p, h) for p in range(SB_PAIRS) for h in range(2)]

    def body(q_ref, k_ref, v_ref, do_ref, o_ref, dq_ref, dk_ref, dv_ref, dk_s, dv_s):
        i = pl.program_id(1)

        @pl.when(i == 0)
        def _():
            dk_s[...] = jnp.zeros_like(dk_s)
            dv_s[...] = jnp.zeros_like(dv_s)

        after = _tri(lambda r, c: r > c)
        since = _tri(lambda r, c: r >= c)
        lanes = _pair_lanes()
        cols = [slice(p * PAIR, (p + 1) * PAIR) for p in range(SB_PAIRS)]
        q = {(p, h): _only(lanes[h], q_ref[:, cols[p]] * ATT_SCALE) for p, h in heads}
        do = {(p, h): _only(lanes[h], do_ref[:, cols[p]].astype(BF16)) for p, h in heads}
        total = {ph: jnp.sum(do[ph].astype(F32) * o_ref[:, cols[ph[0]]], axis=1, keepdims=True) for ph in heads}

        def tiles(j, carries, mask):
            at = pl.ds(pl.multiple_of(j * B, B), B)
            ks = [k_ref[at, c] for c in cols]
            vs = [v_ref[at, c] for c in cols]
            scores = [_sb_scores(q[ph], ks[ph[0]], mask) for ph in heads]
            laters = [_cum(sp, after) for _, sp in scores]
            das = [_dot(do[ph], vs[ph[0]], NT) for ph in heads]
            a_s, gs = [], []
            for (z, sp), later, da, carry in zip(scores, laters, das, carries):
                a = jnp.exp(z - sp - later - carry[0])
                if mask is not None:
                    a = jnp.where(mask, a, 0.0)
                a = a.astype(BF16)
                a_s.append(a)
                gs.append(a.astype(F32) * da)
            sinces = [_cum(g, since) for g in gs]
            dzs = []
            for ph, (_, sp), g, from_s, carry in zip(heads, scores, gs, sinces, carries):
                g_before = total[ph] - carry[1] - from_s
                fail = jnp.exp(-sp)
                dz = fail * (g + g_before) - g_before
                if mask is not None:
                    dz = jnp.where(mask, dz, 0.0)
                dzs.append(dz.astype(BF16))
            out = []
            for ph, (_, sp), a, dz, later, from_s, carry in zip(heads, scores, a_s, dzs, laters, sinces, carries):
                dk_s[at, cols[ph[0]]] += _dot(dz, q[ph], TN)
                dv_s[at, cols[ph[0]]] += _dot(a, do[ph], TN)
                out.append((carry[0] + later[:, 0:1] + sp[:, 0:1], carry[1] + from_s[:, 0:1],
                            carry[2] + _dot(dz, _only(lanes[ph[1]], ks[ph[0]]))))
            return tuple(out)

        col = jnp.zeros((B, 1), F32)
        zero = (col, col, jnp.zeros((B, PAIR), F32))
        carries = tiles(i, (zero,) * len(heads), _strict_causal())
        last = lax.fori_loop(0, i, lambda jj, cs: tiles(i - 1 - jj, cs, None), carries)
        for p in range(SB_PAIRS):
            dq_ref[:, cols[p]] = ((last[2 * p][2] + last[2 * p + 1][2]) * ATT_SCALE).astype(BF16)

        @pl.when(i == n_blocks - 1)
        def _():
            dk_ref[...] = dk_s[...].astype(BF16)
            dv_ref[...] = dv_s[...].astype(BF16)

    blk = lambda off: pl.BlockSpec((B, W), lambda g, i: (i, g + off))
    full = lambda off: pl.BlockSpec((T, W), lambda g, i: (0, g + off))
    out = _out((T, N_HEADS * HEAD_DIM), BF16)
    return pl.pallas_call(
        lambda after_ref, *refs: body(*refs), name=name, grid=(steps, n_blocks),
        in_specs=[ANY, blk(0), full(steps), full(2 * steps), blk(0), blk(0)],
        out_specs=[blk(0), full(0), full(0)],
        out_shape=[out, out, out],
        scratch_shapes=[pltpu.VMEM((T, W), F32)] * 2,
        compiler_params=_params(2, VMEM_LIMIT),
    )(after, *_hbm(qkv, qkv, qkv, do, o))


NEAR = BAND - PAD + REL_CLIP
FAR = BAND - NEAR
NEAR_REL = 2 * REL_CLIP
BIAS_ROWS = 8


def _rel_onehot(i, transposed):
    shape = (NEAR, NEAR_REL) if transposed else (NEAR_REL, NEAR)
    j = FAR + lax.broadcasted_iota(jnp.int32, shape, 0 if transposed else 1)
    r = lax.broadcasted_iota(jnp.int32, shape, 1 if transposed else 0)
    idx = jnp.clip(i + PAD - j, -REL_CLIP, REL_CLIP) + REL_CLIP
    return jnp.where(idx - 1 == r, 1.0, 0.0).astype(BF16)


def _bias_table(rel_bias, name):
    def body(near_ref, far_ref, o_ref):
        rb = near_ref[...]
        hi, lo = _split2(rb)
        lo2 = (rb - hi.astype(F32) - lo.astype(F32)).astype(BF16)
        far = jnp.broadcast_to(far_ref[...], (N_HEADS, FAR))
        for k in range(BIAS_ROWS):
            onehot = _rel_onehot(pl.program_id(0) * BIAS_ROWS + k, False)
            o_ref[k, :, :FAR] = far
            o_ref[k, :, FAR:] = _dot(hi, onehot) + _dot(lo, onehot) + _dot(lo2, onehot)

    return pl.pallas_call(
        body, name=name, grid=(CHUNK // BIAS_ROWS,),
        in_specs=[pl.BlockSpec((N_HEADS, NEAR_REL), lambda i: (0, 0)), pl.BlockSpec((N_HEADS, 1), lambda i: (0, 0))],
        out_specs=pl.BlockSpec((BIAS_ROWS, N_HEADS, BAND), lambda i: (i, 0, 0)),
        out_shape=_out((CHUNK, N_HEADS, BAND), F32),
        compiler_params=_params(1),
    )(*_hbm(rel_bias[:, 1:], rel_bias[:, N_REL - 1:]))


def _bias_grad(dbias_t, name):
    def body(d_ref, near_ref, far_ref):
        near, far = None, None
        for k in range(BIAS_ROWS):
            onehot = _rel_onehot(pl.program_id(0) * BIAS_ROWS + k, True)
            hi, lo = _split2(d_ref[k, :, FAR:])
            part = _dot(hi, onehot) + _dot(lo, onehot)
            rest = jnp.sum(d_ref[k, :, :FAR], axis=1, keepdims=True)
            near, far = (part, rest) if near is None else (near + part, far + rest)
        first = pl.program_id(0) == 0
        _accumulate(near_ref, near, first)
        _accumulate(far_ref, jnp.broadcast_to(far, far_ref.shape), first)

    near, far = pl.pallas_call(
        body, name=name, grid=(CHUNK // BIAS_ROWS,),
        in_specs=[pl.BlockSpec((BIAS_ROWS, N_HEADS, BAND), lambda i: (i, 0, 0))],
        out_specs=[pl.BlockSpec((N_HEADS, NEAR_REL), lambda i: (0, 0)), pl.BlockSpec((N_HEADS, 128), lambda i: (0, 0))],
        out_shape=[_out((N_HEADS, NEAR_REL), F32), _out((N_HEADS, 128), F32)],
        compiler_params=_params(1),
    )(*_hbm(dbias_t))
    return jnp.pad(near, ((0, 0), (1, 0))).at[:, N_REL - 1].add(far[:, 0])


def _ch_probs(scores, bias, valid):
    z = jnp.where(valid, scores * ATT_SCALE + bias, NEG_INF)
    e = jnp.exp(z - jnp.max(z, axis=-1, keepdims=True))
    return e / jnp.sum(e, axis=-1, keepdims=True)


CH_HEADS = [(pair, h) for pair in range(N_HEADS // 2) for h in range(2)]
CH_COLS = [slice(pair * PAIR, (pair + 1) * PAIR) for pair in range(N_HEADS // 2)]


CH_GROUP = 2
CH_Q = CH_GROUP * CHUNK
CH_WIN = (LOOKBACK + CH_GROUP) * CHUNK


def _ch_valid(n):
    row_chunk = lax.broadcasted_iota(jnp.int32, (CH_Q, CH_WIN), 0) // CHUNK
    slot = lax.broadcasted_iota(jnp.int32, (CH_Q, CH_WIN), 1)
    ahead = slot // CHUNK - row_chunk
    return (ahead >= 0) & (ahead <= LOOKBACK) & (n * CH_Q + slot >= PAD)


def _ch_group_bias(bias):
    shifted = [jnp.pad(bias, ((0, 0), (0, 0), (c * CHUNK, (CH_GROUP - 1 - c) * CHUNK))) for c in range(CH_GROUP)]
    return jnp.concatenate(shifted, axis=1)


def _ch_fold_bias_grad(dbias):
    parts = [dbias[:, c * CHUNK:(c + 1) * CHUNK, c * CHUNK:c * CHUNK + BAND] for c in range(CH_GROUP)]
    return sum(parts[1:], parts[0])


def _ch_fwd(qkv, bias, name):
    T = qkv.shape[0]
    W = N_HEADS * HEAD_DIM

    def body(q_ref, k_ref, v_ref, b_ref, o_ref, kp, vp):
        n = pl.program_id(0)

        @pl.when(n == 0)
        def _():
            _ch_load_padded(k_ref, v_ref, kp, vp)

        win = pl.ds(pl.multiple_of(n * CH_Q, CH_Q), CH_WIN)
        valid = _ch_valid(n)
        lanes = _pair_lanes()
        scores = [_dot(_only(lanes[h], q_ref[:, CH_COLS[pair]]), kp[win, CH_COLS[pair]], NT) for pair, h in CH_HEADS]
        probs = [_ch_probs(s, b_ref[2 * pair + h], valid).astype(BF16) for s, (pair, h) in zip(scores, CH_HEADS)]
        outs = [_dot(p, _only(lanes[h], vp[win, CH_COLS[pair]])) for p, (pair, h) in zip(probs, CH_HEADS)]
        for pair, cols in enumerate(CH_COLS):
            o_ref[:, cols] = outs[2 * pair] + outs[2 * pair + 1]

    full = lambda col: pl.BlockSpec((T, W), lambda n: (0, col))
    return pl.pallas_call(
        body, name=name, grid=(T // CH_Q,),
        in_specs=[pl.BlockSpec((CH_Q, W), lambda n: (n, 3)), full(4), full(5),
                  pl.BlockSpec((N_HEADS, CH_Q, CH_WIN), lambda n: (0, 0, 0))],
        out_specs=pl.BlockSpec((CH_Q, W), lambda n: (n, 0)),
        out_shape=_out((T, W), F32),
        scratch_shapes=[pltpu.VMEM((PAD + T, W), BF16)] * 2,
        compiler_params=_params(1, VMEM_LIMIT),
    )(*_hbm(qkv, qkv, qkv, bias))


def _ch_load_padded(k_ref, v_ref, kp, vp):
    for src, dst in ((k_ref, kp), (v_ref, vp)):
        dst[:PAD, :] = jnp.zeros((PAD, dst.shape[1]), dst.dtype)
        dst[PAD:, :] = src[...]


def _ch_bwd(qkv, bias, do, after, name):
    T = qkv.shape[0]
    W = N_HEADS * HEAD_DIM
    n_chunks = T // CH_Q

    def body(q_ref, k_ref, v_ref, b_ref, do_ref, dq_ref, dk_ref, dv_ref, db_ref, kp, vp, dk_s, dv_s):
        n = pl.program_id(0)

        @pl.when(n == 0)
        def _():
            _ch_load_padded(k_ref, v_ref, kp, vp)
            dk_s[...] = jnp.zeros_like(dk_s)
            dv_s[...] = jnp.zeros_like(dv_s)
            db_ref[...] = jnp.zeros_like(db_ref)

        win = pl.ds(pl.multiple_of(n * CH_Q, CH_Q), CH_WIN)
        valid = _ch_valid(n)
        lanes = _pair_lanes()
        kws = [kp[win, cols] for cols in CH_COLS]
        vws = [vp[win, cols] for cols in CH_COLS]
        qs = [_only(lanes[h], q_ref[:, CH_COLS[pair]]) for pair, h in CH_HEADS]
        dos = [_only(lanes[h], do_ref[:, CH_COLS[pair]].astype(BF16)) for pair, h in CH_HEADS]
        scores = [_dot(q, kws[pair], NT) for q, (pair, _) in zip(qs, CH_HEADS)]
        dps = [_dot(do, vws[pair], NT) for do, (pair, _) in zip(dos, CH_HEADS)]
        probs = [_ch_probs(s, b_ref[2 * pair + h], valid) for s, (pair, h) in zip(scores, CH_HEADS)]
        dzs = [p * (dp - jnp.sum(dp * p, axis=-1, keepdims=True)) for p, dp in zip(probs, dps)]
        for k, dz in enumerate(dzs):
            db_ref[k] += dz
        dzbs = [(dz * ATT_SCALE).astype(BF16) for dz in dzs]
        dqs = [_dot(dz, _only(lanes[h], kws[pair])) for dz, (pair, h) in zip(dzbs, CH_HEADS)]
        dks = [_dot(dz, q, TN) for dz, q in zip(dzbs, qs)]
        dvs = [_dot(p.astype(BF16), do, TN) for p, do in zip(probs, dos)]
        for pair, cols in enumerate(CH_COLS):
            dq_ref[:, cols] = (dqs[2 * pair] + dqs[2 * pair + 1]).astype(BF16)
            dk_s[win, cols] += dks[2 * pair] + dks[2 * pair + 1]
            dv_s[win, cols] += dvs[2 * pair] + dvs[2 * pair + 1]

        @pl.when(n == n_chunks - 1)
        def _():
            dk_ref[...] = dk_s[PAD:, :].astype(BF16)
            dv_ref[...] = dv_s[PAD:, :].astype(BF16)

    full = lambda col: pl.BlockSpec((T, W), lambda n: (0, col))
    blk = lambda col: pl.BlockSpec((CH_Q, W), lambda n: (n, col))
    tab = pl.BlockSpec((N_HEADS, CH_Q, CH_WIN), lambda n: (0, 0, 0))
    out = _out((T, W), BF16)
    return pl.pallas_call(
        lambda after_ref, *refs: body(*refs), name=name, grid=(n_chunks,),
        in_specs=[ANY, blk(3), full(4), full(5), tab, blk(0)],
        out_specs=[blk(0), full(0), full(0), tab],
        out_shape=[out, out, out, _out((N_HEADS, CH_Q, CH_WIN), F32)],
        scratch_shapes=[pltpu.VMEM((PAD + T, W), BF16)] * 2 + [pltpu.VMEM((PAD + T, W), F32)] * 2,
        compiler_params=_params(1, VMEM_LIMIT),
    )(after, *_hbm(qkv, qkv, qkv, bias, do))


def _rows_split(a, parts):
    return a.reshape(a.shape[:-2] + (parts, a.shape[-2] // parts, a.shape[-1]))


def _cast_into_own_slot(me, c, ws, in_chip_order, name):
    parts = 2
    ws = [_rows_split(_rows_split(w, 2), parts) for w in ws]
    n = len(ws)

    def body(me_ref, c_ref, *refs):
        for src, dst in zip(refs[:n], refs[n:]):
            dst[0, 0, 0] = src[0, 0].astype(BF16)

    def specs(w, plain):
        block = (1, 1) + w.shape[2:]
        if plain:
            return (pl.BlockSpec(block, lambda d, r, me_ref, c_ref: (d, r, 0, 0)),
                    pl.BlockSpec((1,) + block, lambda d, r, me_ref, c_ref: (me_ref[0], d, r, 0, 0)))
        return (pl.BlockSpec(block, lambda d, r, me_ref, c_ref: (d ^ c_ref[0], r, 0, 0)),
                pl.BlockSpec((1,) + block, lambda d, r, me_ref, c_ref: (0, d, r, 0, 0)))

    both = [specs(w, plain) for w, plain in zip(ws, in_chip_order)]
    outs = pl.pallas_call(
        body, name=name,
        grid_spec=pltpu.PrefetchScalarGridSpec(
            num_scalar_prefetch=2, grid=(2, parts),
            in_specs=[s[0] for s in both], out_specs=[s[1] for s in both]),
        out_shape=[_out((N_CHIPS,) + w.shape, BF16) for w in ws],
        compiler_params=_params(2, VMEM_LIMIT),
    )(me, c, *_hbm(*ws))
    return [o.reshape(N_CHIPS, 2, o.shape[2] * o.shape[3], o.shape[4]) for o in outs]


def _zone_slots(in_chip_order):
    x, y, c, _ = _place()
    me = 2 * x + y
    if in_chip_order:
        return (me, c), (lambda r: (me, c)), (lambda r: (me ^ r, c)), (lambda r: (me ^ r, c))
    return (0, 0), (lambda r: (r, 0)), (lambda r: (r, 0)), (lambda r: (r, 1))


def _pair_add(c, mine, got, permuted, name):
    parts = 2
    mine = [_rows_split(m, parts) for m in mine]
    got = [_rows_split(g, parts) for g in got]
    n = len(mine)

    def body(c_ref, *refs):
        for a, b, o in zip(refs[:n], refs[n:2 * n], refs[2 * n:]):
            o[0, 0] = (a[0, 0, 0] + b[0, 0].astype(F32)).astype(BF16)

    def mine_spec(m, perm):
        if perm:
            return pl.BlockSpec((1, 1, 1) + m.shape[3:], lambda j, r, c_ref: (j, 0, r, 0, 0))
        return pl.BlockSpec((1, 1, 1) + m.shape[3:], lambda j, r, c_ref: (j, c_ref[0], r, 0, 0))

    def got_spec(g):
        return pl.BlockSpec((1, 1) + g.shape[2:], lambda j, r, c_ref: (j, r, 0, 0))

    outs = pl.pallas_call(
        body, name=name,
        grid_spec=pltpu.PrefetchScalarGridSpec(
            num_scalar_prefetch=1, grid=(N_CHIPS, parts),
            in_specs=[mine_spec(m, perm) for m, perm in zip(mine, permuted)] + [got_spec(g) for g in got],
            out_specs=[got_spec(g) for g in got]),
        out_shape=[_out(g.shape, BF16) for g in got],
        compiler_params=_params(2, VMEM_LIMIT),
    )(c, *_hbm(*mine, *got))
    return [o.reshape(o.shape[0], o.shape[1] * o.shape[2], o.shape[3]) for o in outs]


def _chip_add(me, partials, landed, permuted, name):
    parts = 2
    ps = [_rows_split(x, parts) for x in partials]
    ls = [_rows_split(x, parts) for x in landed]
    n = len(ps)

    def body(me_ref, *refs):
        for own, got, o in zip(refs[:n], refs[n:2 * n], refs[2 * n:]):
            acc = own[0, 0].astype(F32)
            for r in range(N_CHIPS - 1):
                acc = acc + got[r, 0].astype(F32)
            o[0] = acc

    def own_spec(x, perm):
        if perm:
            return pl.BlockSpec((1, 1) + x.shape[2:], lambda r, me_ref: (0, r, 0, 0))
        return pl.BlockSpec((1, 1) + x.shape[2:], lambda r, me_ref: (me_ref[0], r, 0, 0))

    outs = pl.pallas_call(
        body, name=name,
        grid_spec=pltpu.PrefetchScalarGridSpec(
            num_scalar_prefetch=1, grid=(parts,),
            in_specs=[own_spec(x, perm) for x, perm in zip(ps, permuted)]
            + [pl.BlockSpec((N_CHIPS - 1, 1) + x.shape[2:], lambda r, me_ref: (0, r, 0, 0)) for x in ls],
            out_specs=[pl.BlockSpec((1,) + x.shape[2:], lambda r, me_ref: (r, 0, 0)) for x in ps]),
        out_shape=[_out(x.shape[1:], F32) for x in ps],
        compiler_params=_params(1, VMEM_LIMIT),
    )(me, *_hbm(*ps, *ls))
    return [o.reshape(o.shape[0] * o.shape[1], o.shape[2]) for o in outs]


def _adamw_math(w, g, m, v):
    m = ADAM_B1 * m + (1.0 - ADAM_B1) * g
    v = ADAM_B2 * v + (1.0 - ADAM_B2) * (g * g)
    m_hat = m / (1.0 - ADAM_B1 ** ADAM_STEP)
    v_hat = v / (1.0 - ADAM_B2 ** ADAM_STEP)
    delta = -ADAM_LR * (m_hat / (jnp.sqrt(v_hat) + ADAM_EPS) + ADAM_WD * w)
    return delta, m, v


def _adamw(ws, gs, ms, vs, parts, name):
    n = len(ws)
    flat = [_rows_split(a, parts) for a in (*ws, *gs, *ms, *vs)]

    def body(*refs):
        ins, outs = refs[:4 * n], refs[4 * n:]
        for k in range(n):
            d, m, v = _adamw_math(ins[k][...], ins[n + k][...], ins[2 * n + k][...], ins[3 * n + k][...])
            outs[k][...] = d
            outs[n + k][...] = m
            outs[2 * n + k][...] = v

    spec = lambda a: pl.BlockSpec((1,) + a.shape[1:], lambda i: (i, 0, 0))
    outs = pl.pallas_call(
        body, name=name, grid=(parts,),
        in_specs=[spec(a) for a in flat], out_specs=[spec(a) for a in flat[:n]] * 3,
        out_shape=[_out(a.shape, F32) for a in flat[:n]] * 3,
        compiler_params=_params(1, VMEM_LIMIT),
    )(*_hbm(*flat))
    outs = [o.reshape(o.shape[0] * o.shape[1], o.shape[2]) for o in outs]
    return outs[:n], outs[n:2 * n], outs[2 * n:]


def _adamw_halves(c, ws, owns, others, ms, vs, name):
    parts = 4
    n = len(ws)
    whole = [_rows_split(_rows_split(a, 2), parts) for a in (*ws, *ms, *vs)]
    halves = [_rows_split(a, parts) for a in (*owns, *others)]

    def body(c_ref, *refs):
        ins, outs = refs[:5 * n], refs[5 * n:]
        mine = pl.program_id(0) == c_ref[0]
        for k in range(n):
            g = jnp.where(mine, ins[3 * n + k][0], ins[4 * n + k][0])
            d, m, v = _adamw_math(ins[k][0, 0], g, ins[n + k][0, 0], ins[2 * n + k][0, 0])
            for slot, val in enumerate((g, d, m, v)):
                outs[slot * n + k][0, 0] = val

    wspec = lambda a: pl.BlockSpec((1, 1) + a.shape[2:], lambda h, r, c_ref: (h, r, 0, 0))
    hspec = lambda a: pl.BlockSpec((1,) + a.shape[1:], lambda h, r, c_ref: (r, 0, 0))
    outs = pl.pallas_call(
        body, name=name,
        grid_spec=pltpu.PrefetchScalarGridSpec(
            num_scalar_prefetch=1, grid=(2, parts),
            in_specs=[wspec(a) for a in whole] + [hspec(a) for a in halves],
            out_specs=[wspec(a) for a in whole[:n]] * 4),
        out_shape=[_out(a.shape, F32) for a in whole[:n]] * 4,
        compiler_params=_params(2, VMEM_LIMIT),
    )(c, *_hbm(*whole, *halves))
    outs = [o.reshape(2 * parts * o.shape[2], o.shape[3]) for o in outs]
    return outs[:n], outs[n:2 * n], outs[2 * n:3 * n], outs[3 * n:]


def _place():
    x, y, c = lax.axis_index("x"), lax.axis_index("y"), lax.axis_index("c")
    peers = [(x ^ (r >> 1), y ^ (r & 1), c) for r in (1, 2, 3)]
    return x, y, c, peers


def _handshake(peers):
    barrier = pltpu.get_barrier_semaphore()
    for peer in peers:
        pl.semaphore_signal(barrier, inc=1, device_id=peer, device_id_type=MESH)
    pl.semaphore_wait(barrier, len(peers))


ANY = pl.BlockSpec(memory_space=pl.ANY)
HBM = pl.BlockSpec(memory_space=pltpu.HBM)
SEM = pl.BlockSpec(memory_space=pltpu.SEMAPHORE)
SPLIT_COPY = pltpu.SideEffectType.DATAFLOW_SIDE_EFFECTING


def _split_start(body, name, collective_id, operands, n_sems, after=None):
    n = len(operands)
    extra = [] if after is None else [after]

    def wrapped(*refs):
        at = n + len(extra)
        body(refs[:n], refs[at], refs[at + 1])
        token = refs[-1]
        token[...] = jnp.zeros_like(token)

    outs = pl.pallas_call(
        wrapped, name=name,
        in_specs=[HBM] * n + [ANY] * len(extra),
        out_shape=(pltpu.SemaphoreType.DMA((n_sems,)), pltpu.SemaphoreType.DMA((n_sems,)),
                   *[pltpu.HBM(a.shape, a.dtype) for a in operands], jax.ShapeDtypeStruct((8, 128), F32)),
        out_specs=(SEM, SEM, *[HBM] * n, pl.BlockSpec(memory_space=pltpu.VMEM)),
        input_output_aliases={i: 2 + i for i in range(n)},
        compiler_params=pltpu.CompilerParams(has_side_effects=SPLIT_COPY, collective_id=collective_id),
    )(*_hbm(*operands), *extra)
    return outs[0], outs[1], list(outs[2:2 + n]), outs[-1]


def _split_wait(body, name, send_sem, recv_sem, operands, after):
    n = len(operands)

    def wrapped(*refs):
        body(refs[:n], refs[n], refs[n + 1])

    outs = pl.pallas_call(
        wrapped, name=name,
        in_specs=[HBM] * n + [SEM, SEM, ANY],
        out_shape=tuple(pltpu.HBM(a.shape, a.dtype) for a in operands),
        out_specs=tuple([HBM] * n),
        input_output_aliases={i: i for i in range(n)},
        compiler_params=pltpu.CompilerParams(has_side_effects=SPLIT_COPY),
    )(*operands, send_sem, recv_sem, after)
    return list(outs)


def _gather_copies(lands, in_chip_order, send_sem, recv_sem):
    peers = _place()[3]
    copies = []
    for a, (land, plain) in enumerate(zip(lands, in_chip_order)):
        own, sent_to, _, _ = _zone_slots(plain)
        copies += [pltpu.make_async_remote_copy(
            src_ref=land.at[own], dst_ref=land.at[sent_to(r + 1)],
            send_sem=send_sem.at[a * 3 + r], recv_sem=recv_sem.at[a * 3 + r],
            device_id=peers[r], device_id_type=MESH) for r in range(3)]
    return copies


def _gather_start(lands, in_chip_order, name, collective_id, after):
    def body(refs, send_sem, recv_sem):
        _handshake(_place()[3])
        for cp in _gather_copies(refs, in_chip_order, send_sem, recv_sem):
            cp.start()

    return _split_start(body, name, collective_id, list(lands), 3 * len(lands), after)


def _gather_wait(send_sem, recv_sem, operands, in_chip_order, after, name):
    def body(refs, send_sem, recv_sem):
        for cp in _gather_copies(refs, in_chip_order, send_sem, recv_sem):
            cp.wait_send()
            cp.wait_recv()

    return _split_wait(body, name, send_sem, recv_sem, operands, after)


def _gather_finish(lands, in_chip_order, with_ici, name):
    n = len(lands)

    def body(*refs):
        land = refs[n:2 * n]
        send_ici, recv_ici, send_d2d, recv_d2d = refs[2 * n:]
        x, y, c, _ = _place()
        ici = _gather_copies(land, in_chip_order, send_ici, recv_ici) if with_ici else []
        for cp in ici:
            cp.start()
        passed = []
        for a in range(n):
            _, _, received, kept = _zone_slots(in_chip_order[a])
            passed += [pltpu.make_async_remote_copy(
                src_ref=land[a].at[received(r + 1)], dst_ref=land[a].at[kept(r + 1)],
                send_sem=send_d2d.at[a * 3 + r], recv_sem=recv_d2d.at[a * 3 + r],
                device_id=(x, y, 1 - c), device_id_type=MESH) for r in range(3)]
        for k, cp in enumerate(passed):
            if with_ici:
                ici[k].wait_recv()
            cp.start()
        for cp in passed:
            cp.wait_recv()
        for cp in ici:
            cp.wait_send()
        for cp in passed:
            cp.wait_send()

    outs = pl.pallas_call(
        body, name=name,
        in_specs=[ANY] * n, out_specs=[ANY] * n,
        out_shape=[_out(l.shape, l.dtype) for l in lands],
        input_output_aliases={a: a for a in range(n)},
        scratch_shapes=[pltpu.SemaphoreType.DMA((3 * n,))] * 4,
    )(*lands)
    return list(outs)


def _pass_copies(lands, in_chip_order, send_sem, recv_sem):
    x, y, c, _ = _place()
    copies = []
    for a, (land, plain) in enumerate(zip(lands, in_chip_order)):
        _, _, received, kept = _zone_slots(plain)
        copies += [pltpu.make_async_remote_copy(
            src_ref=land.at[received(r + 1)], dst_ref=land.at[kept(r + 1)],
            send_sem=send_sem.at[a * 3 + r], recv_sem=recv_sem.at[a * 3 + r],
            device_id=(x, y, 1 - c), device_id_type=MESH) for r in range(3)]
    return copies


def _pass_start(lands, in_chip_order, name, collective_id):
    def body(refs, send_sem, recv_sem):
        x, y, c, _ = _place()
        _handshake([(x, y, 1 - c)])
        for cp in _pass_copies(refs, in_chip_order, send_sem, recv_sem):
            cp.start()

    return _split_start(body, name, collective_id, list(lands), 3 * len(lands))


def _pass_wait(send_sem, recv_sem, lands, in_chip_order, after, name):
    def body(refs, send_sem, recv_sem):
        for cp in _pass_copies(refs, in_chip_order, send_sem, recv_sem):
            cp.wait_send()
            cp.wait_recv()

    return _split_wait(body, name, send_sem, recv_sem, lands, after)


def _slabs(land):
    return land.reshape(N_CHIPS, 2 * land.shape[2], land.shape[3])


def _pair_swap(grads, permuted, name):
    n = len(grads)

    def body(*refs):
        src, dst = refs[:n], refs[n:2 * n]
        send_sem, recv_sem = refs[2 * n:]
        x, y, c, _ = _place()
        copies = [pltpu.make_async_remote_copy(
            src_ref=src[a].at[:, 1] if permuted[a] else src[a].at[:, 1 - c], dst_ref=dst[a],
            send_sem=send_sem.at[a], recv_sem=recv_sem.at[a],
            device_id=(x, y, 1 - c), device_id_type=MESH) for a in range(n)]
        for cp in copies:
            cp.start()
        for cp in copies:
            cp.wait()

    return pl.pallas_call(
        body, name=name,
        in_specs=[ANY] * n, out_specs=[ANY] * n,
        out_shape=[_out((N_CHIPS,) + g.shape[2:], g.dtype) for g in grads],
        scratch_shapes=[pltpu.SemaphoreType.DMA((n,))] * 2,
    )(*grads)


def _swap_copies(refs, permuted, send_sem, recv_sem):
    n = len(refs) // 2
    x, y, c, _ = _place()
    return [pltpu.make_async_remote_copy(
        src_ref=refs[a].at[:, 1] if permuted[a] else refs[a].at[:, 1 - c], dst_ref=refs[n + a],
        send_sem=send_sem.at[a], recv_sem=recv_sem.at[a],
        device_id=(x, y, 1 - c), device_id_type=MESH) for a in range(n)]


def _pair_swap_start(grads, permuted, name, collective_id):
    def body(refs, send_sem, recv_sem):
        x, y, c, _ = _place()
        _handshake([(x, y, 1 - c)])
        for cp in _swap_copies(refs, permuted, send_sem, recv_sem):
            cp.start()

    lands = [lax.empty((N_CHIPS,) + g.shape[2:], g.dtype) for g in grads]
    return _split_start(body, name, collective_id, list(grads) + lands, len(grads))


def _pair_swap_wait(send_sem, recv_sem, operands, permuted, after, name):
    def body(refs, send_sem, recv_sem):
        for cp in _swap_copies(refs, permuted, send_sem, recv_sem):
            cp.wait_send()
            cp.wait_recv()

    return _split_wait(body, name, send_sem, recv_sem, operands, after)


def _scatter_copies(refs, permuted, send_sem, recv_sem):
    n = len(refs) // 2
    x, y, _, peers = _place()
    me = 2 * x + y
    return [pltpu.make_async_remote_copy(
        src_ref=refs[a].at[r + 1] if permuted[a] else refs[a].at[me ^ (r + 1)], dst_ref=refs[n + a].at[r],
        send_sem=send_sem.at[a * 3 + r], recv_sem=recv_sem.at[a * 3 + r],
        device_id=peers[r], device_id_type=MESH) for a in range(n) for r in range(3)]


def _scatter_start(partials, permuted, name, collective_id):
    def body(refs, send_sem, recv_sem):
        _handshake(_place()[3])
        for cp in _scatter_copies(refs, permuted, send_sem, recv_sem):
            cp.start()

    lands = [lax.empty((N_CHIPS - 1,) + p.shape[1:], p.dtype) for p in partials]
    return _split_start(body, name, collective_id, list(partials) + lands, 3 * len(partials))


def _scatter_wait(send_sem, recv_sem, operands, permuted, after, name):
    def body(refs, send_sem, recv_sem):
        for cp in _scatter_copies(refs, permuted, send_sem, recv_sem):
            cp.wait_send()
            cp.wait_recv()

    return _split_wait(body, name, send_sem, recv_sem, operands, after)


def _pair_join(halves, name):
    n = len(halves)

    def body(*refs):
        src, dst = refs[:n], refs[n:2 * n]
        send_sem, recv_sem = refs[2 * n:]
        x, y, c, _ = _place()
        copies = [pltpu.make_async_remote_copy(
            src_ref=src[a], dst_ref=dst[a], send_sem=send_sem.at[a], recv_sem=recv_sem.at[a],
            device_id=(x, y, 1 - c), device_id_type=MESH) for a in range(n)]
        for cp in copies:
            cp.start()
        for cp in copies:
            cp.wait()

    return pl.pallas_call(
        body, name=name,
        in_specs=[ANY] * n, out_specs=[ANY] * n,
        out_shape=[_out(h.shape, F32) for h in halves],
        scratch_shapes=[pltpu.SemaphoreType.DMA((n,))] * 2,
    )(*halves)


def _all_sum_small(v, after, name):
    R, C = v.shape
    n_dev = 8

    def body(v_ref, after_ref, o_ref, buf, send_sem, recv_sem):
        x, y, c, _ = _place()
        me = 4 * x + 2 * y + c
        buf[me] = v_ref[...]
        copies = []
        for k in range(1, n_dev):
            peer = (x ^ (k >> 2), y ^ ((k >> 1) & 1), c ^ (k & 1))
            copies.append(pltpu.make_async_remote_copy(
                src_ref=v_ref, dst_ref=buf.at[me], send_sem=send_sem.at[k - 1], recv_sem=recv_sem.at[k - 1],
                device_id=peer, device_id_type=MESH))
        for cp in copies:
            cp.start()
        for cp in copies:
            cp.wait()
        acc = buf[0]
        for m in range(1, n_dev):
            acc = acc + buf[m]
        o_ref[...] = acc

    return pl.pallas_call(
        body, name=name,
        in_specs=[pl.BlockSpec(memory_space=pltpu.VMEM), ANY], out_specs=pl.BlockSpec(memory_space=pltpu.VMEM),
        out_shape=jax.ShapeDtypeStruct((R, C), F32),
        scratch_shapes=[pltpu.VMEM((n_dev, R, C), F32), pltpu.SemaphoreType.DMA((n_dev - 1,)),
                        pltpu.SemaphoreType.DMA((n_dev - 1,))],
    )(v, after)


class _WholeWeights:
    def __init__(self, w):
        self.w = w

    def weights(self, group, after=None):
        return ({} if group == "passed" else self.w), None

    def grads_ready(self, group, gw):
        return None

    def grads_sent(self, group, after):
        return None


def _local_step(x, p, target, gains, rel_bias, hooks):
    T, D = x.shape
    S = N_CHIPS

    tied = lambda gain, token: gain if token is None else gain + token[0, 0]
    w, token = hooks.weights("first")
    w = dict(w)
    xn1, g1, u1, a1 = _ffn_up(x, tied(gains["ffn1_pre"], token), w["ffn1_gate"], w["ffn1_up"], "ffn1_up")
    w.update(hooks.weights("down", a1)[0])
    h1, f1 = _ffn_down(x, a1, gains["ffn1_post"], w["ffn1_down"], "ffn1_down")
    more, token = hooks.weights("in", h1)
    w.update(more)
    qkv, un = _norm_proj(h1, tied(gains["mix_pre"], token), w["in"], "qkv_proj")
    bias = _ch_group_bias(_bias_table(rel_bias, "bias_table").transpose(1, 0, 2))
    o_a = _sb_fwd(qkv, "sb_fwd")
    o_b = _ch_fwd(qkv, bias, "ch_fwd")
    more, token = hooks.weights("rest", o_b)
    w.update(more)
    w_out = w["out"].reshape(D, D)
    h2, mixed, mo = _mix_out_fwd(h1, o_a, o_b, gains["out_sb"], gains["out_ch"], w_out,
                                 tied(gains["mix_post"], token), "mix_out_fwd")
    w.update(hooks.weights("passed", h2)[0])
    h3, xn2, g2, u2, a2, f2 = _ffn_fwd(h2, gains["ffn2_pre"], gains["ffn2_post"], w["ffn2_gate"], w["ffn2_up"],
                                       w["ffn2_down"], "ffn2_fwd")
    w_ple_proj = w["ple_proj"].transpose(1, 0, 2).reshape(p.shape[1], D)
    w_ple_gate = w["ple_gate"].reshape(D, D)

    loss, dh3, dproj, dgate, dg_ple = _ple_loss(h3, p, target, w_ple_proj, w_ple_gate, gains["ple_post"], "ple_loss")
    gw, gg = {}, {"ple_post": dg_ple}
    gw["ple_proj"] = _mm_tn(p[None], dproj, p.shape[1], "dw_ple_proj")
    row_sharded = lambda pair: tuple(o.reshape(S, D // S, D) for o in pair)
    gw["ple_gate"] = row_sharded(_mm_tn(h3[None], dgate[None], 512, "dw_ple_gate"))

    def ffn_bwd(tag, dh, x_in, xn, g_act, u_act, a_act, f, group):
        dgp, dup, df, gg[tag + "_post"] = _ffn_bwd_act(dh, f, gains[tag + "_post"], w[tag + "_down"], g_act, u_act,
                                                       tag + "_bwd_act")
        gw[tag + "_gate"] = _mm_tn(dgp, xn[None], dgp.shape[2], "dw_" + tag + "_gate")
        gw[tag + "_up"] = _mm_tn(dup, xn[None], dup.shape[2], "dw_" + tag + "_up")
        gw[tag + "_down"] = _mm_tn(a_act, df[None], a_act.shape[2], "dw_" + tag + "_down")
        g_pre = gains[tag + "_pre"]
        if group is not None:
            token = hooks.grads_ready(group, gw)
            g_pre = g_pre if token is None else g_pre + token[0, 0]
        dx, gg[tag + "_pre"] = _proj_bwd([dgp, dup], [w[tag + "_gate"], w[tag + "_up"]], x_in, g_pre, dh,
                                         tag + "_bwd_in")
        return dx

    dh2 = ffn_bwd("ffn2", dh3, h2, xn2, g2, u2, a2, f2, None)
    dmo, do_a, do_b, gg["mix_post"], gg["out_sb"], gg["out_ch"] = _mix_out_bwd(
        dh2, mo, gains["mix_post"], w_out, o_a, o_b, gains["out_sb"], gains["out_ch"], "mix_out_bwd")
    gw["out"] = row_sharded(_mm_tn(mixed[None], dmo[None], 512, "dw_out"))
    token = hooks.grads_ready("early", gw)
    dq_a, dk_a, dv_a = _sb_bwd(qkv, do_a, o_a, do_a if token is None else token, "sb_bwd")
    token = hooks.grads_sent("early", dq_a)
    dq_b, dk_b, dv_b, dbias = _ch_bwd(qkv, bias, do_b, do_b if token is None else token, "ch_bwd")
    g_rel = _bias_grad(_ch_fold_bias_grad(dbias).transpose(1, 0, 2), "bias_grad")
    dqkv = [dq_a, dk_a, dv_a, dq_b, dk_b, dv_b]
    gw["in"] = _dw_in(un, dqkv, w["in"].shape[2], 512, "dw_in")
    dh1, gg["mix_pre"] = _qkv_bwd_in(dqkv, w["in"], h1, gains["mix_pre"], dh2, "qkv_bwd_in")
    dx = ffn_bwd("ffn1", dh1, x, xn1, g1, u1, a1, f1, "late")
    return loss, dx, gw, gg, g_rel


BIG = ["ffn1_gate", "ffn1_up", "ffn1_down", "in", "out", "ffn2_gate", "ffn2_up", "ffn2_down", "ple_proj", "ple_gate"]
GAINS = ["ffn1_pre", "ffn1_post", "mix_pre", "mix_post", "out_sb", "out_ch", "ffn2_pre", "ffn2_post", "ple_post"]
TRANSPOSED = ("w_ffn1_gate", "w_ffn1_up", "w_ffn2_gate", "w_ffn2_up")
PERMUTED = ("ffn1_gate", "ffn1_up", "ffn1_down", "ffn2_gate", "ffn2_up", "ffn2_down")
W_GROUPS = {"first": ["ffn1_gate", "ffn1_up"], "down": ["ffn1_down"], "in": ["in"],
            "rest": ["out", "ffn2_gate", "ffn2_up", "ffn2_down", "ple_proj", "ple_gate"]}
G_GROUPS = {"early": ["ple_proj", "ple_gate", "ffn2_gate", "ffn2_up", "ffn2_down", "out"],
            "late": ["in", "ffn1_gate", "ffn1_up", "ffn1_down"]}
ORDER = ["g_ffn1_pre", "g_ffn1_post", "w_ffn1_gate", "w_ffn1_up", "w_ffn1_down", "g_mix_pre", "g_mix_post", "w_in",
         "g_out_sb", "g_out_ch", "rel_bias", "w_out", "g_ffn2_pre", "g_ffn2_post", "w_ffn2_gate", "w_ffn2_up",
         "w_ffn2_down", "w_ple_proj", "w_ple_gate", "g_ple_post"]


def kernel(x, p, g_ffn1_pre, g_ffn1_post, w_ffn1_gate, w_ffn1_up, w_ffn1_down, g_mix_pre, g_mix_post, w_in, g_out_sb, g_out_ch, rel_bias, w_out, g_ffn2_pre, g_ffn2_post, w_ffn2_gate, w_ffn2_up, w_ffn2_down, w_ple_proj, w_ple_gate, g_ple_post, loss_target, m_g_ffn1_pre, m_g_ffn1_post, m_w_ffn1_gate, m_w_ffn1_up, m_w_ffn1_down, m_g_mix_pre, m_g_mix_post, m_w_in, m_g_out_sb, m_g_out_ch, m_rel_bias, m_w_out, m_g_ffn2_pre, m_g_ffn2_post, m_w_ffn2_gate, m_w_ffn2_up, m_w_ffn2_down, m_w_ple_proj, m_w_ple_gate, m_g_ple_post, v_g_ffn1_pre, v_g_ffn1_post, v_w_ffn1_gate, v_w_ffn1_up, v_w_ffn1_down, v_g_mix_pre, v_g_mix_post, v_w_in, v_g_out_sb, v_g_out_ch, v_rel_bias, v_w_out, v_g_ffn2_pre, v_g_ffn2_post, v_w_ffn2_gate, v_w_ffn2_up, v_w_ffn2_down, v_w_ple_proj, v_w_ple_gate, v_g_ple_post):
    args = dict(locals())
    take = lambda a, n: a[0].T if n in TRANSPOSED else a[0]
    wts = {n: take(args[n], n) for n in ORDER}
    ms = {n: take(args["m_" + n], n) for n in ORDER}
    vs = {n: take(args["v_" + n], n) for n in ORDER}
    gains = {n: wts["g_" + n][None] for n in GAINS}

    c_idx = lax.axis_index("c").astype(jnp.int32).reshape(1)
    me_idx = (2 * lax.axis_index("x") + lax.axis_index("y")).astype(jnp.int32).reshape(1)
    south = lax.axis_index("c") == 0

    plain = lambda names: [n not in PERMUTED for n in names]
    lands = dict(zip(BIG, _cast_into_own_slot(me_idx, c_idx, [wts["w_" + n] for n in BIG], plain(BIG), "cast_weights")))

    class Overlapped:
        def __init__(self):
            self.started = {}
            self.flying = {}

        def start(self, group, collective_id, after):
            names = W_GROUPS[group]
            self.flying[group] = _gather_start([lands[n] for n in names], plain(names), "gather_%s_start" % group,
                                               collective_id, after)
            return self.flying[group][3]

        def weights(self, group, after=None):
            names = W_GROUPS.get(group)
            token = None
            if group == "first":
                zones = _gather_finish([lands[n] for n in names], plain(names), True, "gather_first")
                token = self.start("rest", 4, self.start("in", 1, self.start("down", 6, zones[0])))
            elif group == "passed":
                names, (send_sem, recv_sem, zones, _) = self.passing
                zones = _pass_wait(send_sem, recv_sem, zones, plain(names), after, "gather_rest_pass_wait")
            else:
                send_sem, recv_sem, zones, _ = self.flying[group]
                zones = _gather_wait(send_sem, recv_sem, zones, plain(names), after, "gather_%s_wait" % group)
                if group == "rest":
                    self.passing = names[1:], _pass_start(zones[1:], plain(names[1:]), "gather_rest_pass_start", 7)
                    names, zones, token = names[:1], zones[:1], self.passing[1][3]
                zones = _gather_finish(zones, plain(names), False, "gather_%s_finish" % group)
            return {n: _slabs(z) for n, z in zip(names, zones)}, token

        def grads_ready(self, group, gw):
            names = G_GROUPS[group]
            perm = [n in PERMUTED for n in names]
            halved = lambda g: g.reshape(N_CHIPS, 2, g.shape[1] // 2, g.shape[2])
            mine = [halved(gw[n][0]) for n in names]
            narrow = [halved(gw[n][1]) for n in names]
            if group == "late":
                return self.scatter(group, names, perm, mine, _pair_swap(narrow, perm, "grad_pair_swap_late"))
            self.swapping = names, perm, mine, _pair_swap_start(narrow, perm, "grad_pair_swap_start_early", 5)
            return self.swapping[3][3]

        def grads_sent(self, group, after):
            names, perm, mine, (send_sem, recv_sem, operands, _) = self.swapping
            operands = _pair_swap_wait(send_sem, recv_sem, operands, perm, after, "grad_pair_swap_wait_early")
            return self.scatter(group, names, perm, mine, operands[len(names):])

        def scatter(self, group, names, perm, mine, got):
            partial = _pair_add(c_idx, mine, got, perm, "grad_pair_add_" + group)
            send_sem, recv_sem, operands, token = _scatter_start(partial, perm, "grad_scatter_start_" + group,
                                                                 {"early": 2, "late": 3}[group])
            self.started[group] = names, perm, send_sem, recv_sem, operands, token
            return token

    def reduce_finish(state, after, tag):
        names, perm, send_sem, recv_sem, operands, _ = state
        operands = _scatter_wait(send_sem, recv_sem, operands, perm, after, "grad_scatter_wait_" + tag)
        n = len(names)
        own = _chip_add(me_idx, operands[:n], operands[n:], perm, "grad_chip_add_" + tag)
        return own, _pair_join(own, "grad_pair_join_" + tag)

    hooks = Overlapped()
    loss, dx, gw, gg, g_rel = _local_step(x[0], p[0, 0], loss_target[0], gains, wts["rel_bias"], hooks)

    grads, delta, new_m, new_v = {}, {}, {}, {}

    def finish(group, after):
        own, other = reduce_finish(hooks.started[group], after, group)
        names = ["w_" + n for n in G_GROUPS[group]]
        g, d, m, v = _adamw_halves(c_idx, [wts[n] for n in names], own, other, [ms[n] for n in names],
                                   [vs[n] for n in names], "adamw_" + group)
        for n, gg_, dd, mm, vv in zip(names, g, d, m, v):
            grads[n], delta[n], new_m[n], new_v[n] = gg_, dd, mm, vv
        return d[0]

    early_done = finish("early", dx)

    pieces = [gg[n].reshape(-1, 128) for n in GAINS] + [jnp.pad(g_rel, ((0, 0), (0, N_REL_PAD - N_REL))).reshape(-1, 128)]
    summed = _all_sum_small(jnp.concatenate(pieces + [loss], axis=0), early_done, "small_grad_sum")
    finish("late", summed)
    at = 0
    for n, piece in zip(GAINS, pieces[:-1]):
        grads["g_" + n] = summed[at:at + piece.shape[0]].reshape(1, -1)[0]
        at += piece.shape[0]
    grads["rel_bias"] = summed[at:at + pieces[-1].shape[0]].reshape(N_HEADS, N_REL_PAD)[:, :N_REL]
    loss = summed[at + pieces[-1].shape[0], 0]

    small = ["g_" + n for n in GAINS] + ["rel_bias"]
    as_rows = lambda a: (a.reshape(-1, 128) if a.size % 128 == 0 else jnp.pad(a, ((0, 0), (0, N_REL_PAD - N_REL))).reshape(-1, 128))
    d, m, v = _adamw([as_rows(wts[n]) for n in small], [as_rows(grads[n]) for n in small],
                     [as_rows(ms[n]) for n in small], [as_rows(vs[n]) for n in small], 1, "adamw_small")
    for n, dd, mm, vv in zip(small, d, m, v):
        back = (lambda a: a.reshape(N_HEADS, N_REL_PAD)[:, :N_REL]) if n == "rel_bias" else (lambda a: a.reshape(-1))
        delta[n], new_m[n], new_v[n] = back(dd), back(mm), back(vv)

    outs = [loss, dx[None]]
    for table in (grads, delta, new_m, new_v):
        outs += [(table[n].T if n in TRANSPOSED else table[n])[None] for n in ORDER]
    return tuple(outs)
```

```python
import jax
import jax.numpy as jnp
from jax import lax
from jax.experimental import pallas as pl
from jax.experimental.pallas import tpu as pltpu

F32 = jnp.float32
BF16 = jnp.bfloat16
EPS = 1e-6
N_CHIPS = 4
HEAD_DIM = 64
N_HEADS = 8
CHUNK = 64
LOOKBACK = 8
BAND = (LOOKBACK + 1) * CHUNK
PAD = LOOKBACK * CHUNK
REL_CLIP = 128
N_REL = 2 * REL_CLIP + 1
N_REL_PAD = 384
SB_BLOCK = 256
PAIR = 2 * HEAD_DIM
SB_PAIRS = 2
SB_FWD_PAIRS = 4
ATT_SCALE = HEAD_DIM ** -0.5
NEG_INF = -1e30
ROW_BLOCK = 512
WIDE_ROW_BLOCK = 1024
VMEM_LIMIT_WIDE = 56 * 1024 * 1024
VMEM_LIMIT = 48 * 1024 * 1024
MESH = pl.DeviceIdType.MESH

ADAM_LR = 0.001
ADAM_B1 = 0.9
ADAM_B2 = 0.999
ADAM_EPS = 1e-08
ADAM_WD = 0.01
ADAM_STEP = 10

NT = (((1,), (1,)), ((), ()))
TN = (((0,), (0,)), ((), ()))


def _params(n_grid, vmem=None):
    return pltpu.CompilerParams(dimension_semantics=("arbitrary",) * n_grid, vmem_limit_bytes=vmem)


def _hbm(*arrays):
    return [pltpu.with_memory_space_constraint(a, pltpu.HBM) for a in arrays]


def _out(shape, dtype):
    return pltpu.HBM(shape, dtype)


def _dot(a, b, dims=None):
    if dims is None:
        return jnp.dot(a, b, preferred_element_type=F32)
    return lax.dot_general(a, b, dims, preferred_element_type=F32)


def _sigmoid(x):
    return 1.0 / (1.0 + jnp.exp(-x))


def _rms_fwd(x, g):
    r = lax.rsqrt(jnp.mean(x * x, axis=-1, keepdims=True) + EPS)
    return x * r * g


def _rms_bwd(x, g, dy):
    r = lax.rsqrt(jnp.mean(x * x, axis=-1, keepdims=True) + EPS)
    xh = x * r
    dg = jnp.sum(dy * xh, axis=0, keepdims=True)
    t = dy * g
    dx = r * (t - xh * jnp.mean(t * xh, axis=-1, keepdims=True))
    return dx, dg


def _accumulate(ref, val, first):
    @pl.when(first)
    def _():
        ref[...] = val

    @pl.when(jnp.logical_not(first))
    def _():
        ref[...] += val


def _split2(x):
    hi = x.astype(BF16)
    lo = (x - hi.astype(F32)).astype(BF16)
    return hi, lo


def _ffn_fwd(x, g_pre, g_post, wg, wu, wd, name):
    T, D = x.shape
    S, FS, _ = wg.shape
    tm = min(WIDE_ROW_BLOCK, T)

    def body(x_ref, gpre_ref, gpost_ref, wg_ref, wu_ref, wd_ref,
             h_ref, xn_ref, g_ref, u_ref, a_ref, f_ref):
        k = pl.program_id(1)

        @pl.when(k == 0)
        def _():
            xn_ref[...] = _rms_fwd(x_ref[...], gpre_ref[...]).astype(BF16)

        xn = xn_ref[...]
        g = _dot(xn, wg_ref[0], NT)
        u = _dot(xn, wu_ref[0], NT)
        g_ref[0] = g
        u_ref[0] = u
        a = (g * _sigmoid(g) * u).astype(BF16)
        a_ref[0] = a
        _accumulate(f_ref, _dot(a, wd_ref[0]), k == 0)

        @pl.when(k == S - 1)
        def _():
            h_ref[...] = x_ref[...] + 0.5 * _rms_fwd(f_ref[...], gpost_ref[...])

    row = pl.BlockSpec((tm, D), lambda i, k: (i, 0))
    vec = pl.BlockSpec((1, D), lambda i, k: (0, 0))
    act = pl.BlockSpec((1, tm, FS), lambda i, k: (k, i, 0))
    return pl.pallas_call(
        body, name=name, grid=(T // tm, S),
        in_specs=[row, vec, vec] + [pl.BlockSpec((1, FS, D), lambda i, k: (k, 0, 0))] * 3,
        out_specs=[row, row, act, act, act, row],
        out_shape=[_out((T, D), F32), _out((T, D), BF16),
                   _out((S, T, FS), F32), _out((S, T, FS), F32),
                   _out((S, T, FS), BF16), _out((T, D), F32)],
        compiler_params=_params(2, VMEM_LIMIT_WIDE),
    )(*_hbm(x, g_pre, g_post, wg, wu, wd))


def _ffn_up(x, g_pre, wg, wu, name):
    T, D = x.shape
    S, FS, _ = wg.shape
    tm = min(WIDE_ROW_BLOCK, T)

    def body(x_ref, gpre_ref, wg_ref, wu_ref, xn_ref, g_ref, u_ref, a_ref):
        @pl.when(pl.program_id(1) == 0)
        def _():
            xn_ref[...] = _rms_fwd(x_ref[...], gpre_ref[...]).astype(BF16)

        xn = xn_ref[...]
        g = _dot(xn, wg_ref[0], NT)
        u = _dot(xn, wu_ref[0], NT)
        g_ref[0] = g
        u_ref[0] = u
        a_ref[0] = (g * _sigmoid(g) * u).astype(BF16)

    row = pl.BlockSpec((tm, D), lambda i, k: (i, 0))
    act = pl.BlockSpec((1, tm, FS), lambda i, k: (k, i, 0))
    return pl.pallas_call(
        body, name=name, grid=(T // tm, S),
        in_specs=[row, pl.BlockSpec((1, D), lambda i, k: (0, 0))] + [pl.BlockSpec((1, FS, D), lambda i, k: (k, 0, 0))] * 2,
        out_specs=[row, act, act, act],
        out_shape=[_out((T, D), BF16), _out((S, T, FS), F32), _out((S, T, FS), F32), _out((S, T, FS), BF16)],
        compiler_params=_params(2, VMEM_LIMIT_WIDE),
    )(*_hbm(x, g_pre, wg, wu))


def _ffn_down(x, a, g_post, wd, name):
    T, D = x.shape
    S, FS, _ = wd.shape
    tm = min(WIDE_ROW_BLOCK, T)

    def body(x_ref, a_ref, gpost_ref, wd_ref, h_ref, f_ref):
        k = pl.program_id(1)
        _accumulate(f_ref, _dot(a_ref[0], wd_ref[0]), k == 0)

        @pl.when(k == S - 1)
        def _():
            h_ref[...] = x_ref[...] + 0.5 * _rms_fwd(f_ref[...], gpost_ref[...])

    row = pl.BlockSpec((tm, D), lambda i, k: (i, 0))
    return pl.pallas_call(
        body, name=name, grid=(T // tm, S),
        in_specs=[row, pl.BlockSpec((1, tm, FS), lambda i, k: (k, i, 0)), pl.BlockSpec((1, D), lambda i, k: (0, 0)),
                  pl.BlockSpec((1, FS, D), lambda i, k: (k, 0, 0))],
        out_specs=[row, row],
        out_shape=[_out((T, D), F32), _out((T, D), F32)],
        compiler_params=_params(2, VMEM_LIMIT_WIDE),
    )(*_hbm(x, a, g_post, wd))


def _ffn_bwd_act(dh, f, g_post, wd, g_act, u_act, name):
    T, D = dh.shape
    S, FS, _ = wd.shape
    tm = min(WIDE_ROW_BLOCK, T)

    def body(dh_ref, f_ref, gpost_ref, wd_ref, g_ref, u_ref, dgp_ref, dup_ref, df_ref, dgain_ref, df_s):
        i, k = pl.program_id(0), pl.program_id(1)

        @pl.when(k == 0)
        def _():
            df, dgain = _rms_bwd(f_ref[...], gpost_ref[...], 0.5 * dh_ref[...])
            df_s[...] = df.astype(BF16)
            df_ref[...] = df_s[...]
            _accumulate(dgain_ref, dgain, i == 0)

        da = _dot(df_s[...], wd_ref[0], NT)
        g = g_ref[0]
        s = _sigmoid(g)
        dup_ref[0] = (da * (g * s)).astype(BF16)
        dgp_ref[0] = (da * u_ref[0] * (s * (1.0 + g * (1.0 - s)))).astype(BF16)

    row = pl.BlockSpec((tm, D), lambda i, k: (i, 0))
    vec = pl.BlockSpec((1, D), lambda i, k: (0, 0))
    act = pl.BlockSpec((1, tm, FS), lambda i, k: (k, i, 0))
    return pl.pallas_call(
        body, name=name, grid=(T // tm, S),
        in_specs=[row, row, vec, pl.BlockSpec((1, FS, D), lambda i, k: (k, 0, 0)), act, act],
        out_specs=[act, act, row, vec],
        out_shape=[_out((S, T, FS), BF16), _out((S, T, FS), BF16),
                   _out((T, D), BF16), _out((1, D), F32)],
        scratch_shapes=[pltpu.VMEM((tm, D), BF16)],
        compiler_params=_params(2, VMEM_LIMIT_WIDE),
    )(*_hbm(dh, f, g_post, wd, g_act, u_act))


def _proj_bwd(dys, ws, x, g_pre, dh, name):
    T, D = x.shape
    n = len(dys)
    S, N, _ = ws[0].shape
    tm = min(WIDE_ROW_BLOCK, T)

    def body(*refs):
        dy_refs, w_refs = refs[:n], refs[n:2 * n]
        x_ref, gpre_ref, dh_ref, dx_ref, dgain_ref, acc_s = refs[2 * n:]
        i, k = pl.program_id(0), pl.program_id(1)
        part = None
        for dy_ref, w_ref in zip(dy_refs, w_refs):
            term = _dot(dy_ref[0], w_ref[0])
            part = term if part is None else part + term
        _accumulate(acc_s, part, k == 0)

        @pl.when(k == S - 1)
        def _():
            dx, dgain = _rms_bwd(x_ref[...], gpre_ref[...], acc_s[...])
            dx_ref[...] = dh_ref[...] + dx
            _accumulate(dgain_ref, dgain, i == 0)

    row = pl.BlockSpec((tm, D), lambda i, k: (i, 0))
    vec = pl.BlockSpec((1, D), lambda i, k: (0, 0))
    return pl.pallas_call(
        body, name=name, grid=(T // tm, S),
        in_specs=[pl.BlockSpec((1, tm, N), lambda i, k: (k, i, 0))] * n
        + [pl.BlockSpec((1, N, D), lambda i, k: (k, 0, 0))] * n + [row, vec, row],
        out_specs=[row, vec],
        out_shape=[_out((T, D), F32), _out((1, D), F32)],
        scratch_shapes=[pltpu.VMEM((tm, D), F32)],
        compiler_params=_params(2, VMEM_LIMIT_WIDE),
    )(*_hbm(*dys, *ws, x, g_pre, dh))


def _mm_tn(a, b, bm, name):
    ga, T, M = a.shape
    gb, _, N = b.shape
    b_spec = pl.BlockSpec((1, T, N), (lambda g, m: (g, 0, 0)) if gb > 1 else (lambda g, m: (0, 0, 0)))
    G = max(ga, gb)

    def body(a_ref, b_ref, o_ref, narrow_ref):
        o_ref[0] = _dot(a_ref[0].astype(BF16), b_ref[0].astype(BF16), TN)
        narrow_ref[0] = o_ref[0].astype(BF16)

    out = pl.BlockSpec((1, bm, N), lambda g, m: (g, m, 0))
    return pl.pallas_call(
        body, name=name, grid=(G, M // bm),
        in_specs=[pl.BlockSpec((1, T, bm), (lambda g, m: (g, 0, m)) if ga > 1 else (lambda g, m: (0, 0, m))), b_spec],
        out_specs=[out, out],
        out_shape=[_out((G, M, N), F32), _out((G, M, N), BF16)],
        compiler_params=_params(2, VMEM_LIMIT),
    )(*_hbm(a, b))


QKV_PIECE = 256


def _qkv_shard(dy_refs, k, n_col):
    width = dy_refs[0].shape[1]
    parts = []
    for col in range(k * n_col, (k + 1) * n_col, QKV_PIECE):
        parts.append(dy_refs[col // width][:, col % width:col % width + QKV_PIECE])
    return jnp.concatenate(parts, axis=1)


def _qkv_bwd_in(dys, w, x, g_pre, dh, name):
    T, D = x.shape
    n = len(dys)
    S, _, N = w.shape
    tm = min(WIDE_ROW_BLOCK, T)

    def body(*refs):
        dy_refs = refs[:n]
        w_ref, x_ref, gpre_ref, dh_ref, dx_ref, dgain_ref, acc_s = refs[n:]
        i, k = pl.program_id(0), pl.program_id(1)
        for shard in range(S):
            @pl.when(k == shard)
            def _(shard=shard):
                part = _dot(_qkv_shard(dy_refs, shard, N), w_ref[0], NT)
                if shard == 0:
                    acc_s[...] = part
                else:
                    acc_s[...] += part

        @pl.when(k == S - 1)
        def _():
            dx, dgain = _rms_bwd(x_ref[...], gpre_ref[...], acc_s[...])
            dx_ref[...] = dh_ref[...] + dx
            _accumulate(dgain_ref, dgain, i == 0)

    row = pl.BlockSpec((tm, D), lambda i, k: (i, 0))
    vec = pl.BlockSpec((1, D), lambda i, k: (0, 0))
    return pl.pallas_call(
        body, name=name, grid=(T // tm, S),
        in_specs=[pl.BlockSpec((tm, dy.shape[1]), lambda i, k: (i, 0)) for dy in dys]
        + [pl.BlockSpec((1, D, N), lambda i, k: (k, 0, 0)), row, vec, row],
        out_specs=[row, vec],
        out_shape=[_out((T, D), F32), _out((1, D), F32)],
        scratch_shapes=[pltpu.VMEM((tm, D), F32)],
        compiler_params=_params(2, VMEM_LIMIT_WIDE),
    )(*_hbm(*dys, w, x, g_pre, dh))


def _dw_in(a, dys, n_col, bm, name):
    T, M = a.shape
    n = len(dys)
    S = n * dys[0].shape[1] // n_col

    def body(*refs):
        a_ref, dy_refs = refs[0], refs[1:1 + n]
        o_ref, narrow_ref = refs[1 + n:]
        k = pl.program_id(1)
        for shard in range(S):
            @pl.when(k == shard)
            def _(shard=shard):
                o_ref[0] = _dot(a_ref[...], _qkv_shard(dy_refs, shard, n_col), TN)
                narrow_ref[0] = o_ref[0].astype(BF16)

    out = pl.BlockSpec((1, bm, n_col), lambda m, k: (k, m, 0))
    return pl.pallas_call(
        body, name=name, grid=(M // bm, S),
        in_specs=[pl.BlockSpec((T, bm), lambda m, k: (0, m))]
        + [pl.BlockSpec((T, dy.shape[1]), lambda m, k: (0, 0)) for dy in dys],
        out_specs=[out, out],
        out_shape=[_out((S, M, n_col), F32), _out((S, M, n_col), BF16)],
        compiler_params=_params(2, VMEM_LIMIT_WIDE),
    )(*_hbm(a, *dys))


def _norm_proj(x, g_pre, w, name):
    T, D = x.shape
    S, _, N = w.shape
    tm = min(WIDE_ROW_BLOCK, T)

    def body(x_ref, g_ref, w_ref, o_ref, xn_ref, xn_s):
        @pl.when(pl.program_id(1) == 0)
        def _():
            xn_s[...] = _rms_fwd(x_ref[...], g_ref[...]).astype(BF16)
            xn_ref[...] = xn_s[...]

        o_ref[...] = _dot(xn_s[...], w_ref[0]).astype(BF16)

    row = pl.BlockSpec((tm, D), lambda i, k: (i, 0))
    return pl.pallas_call(
        body, name=name, grid=(T // tm, S),
        in_specs=[row, pl.BlockSpec((1, D), lambda i, k: (0, 0)), pl.BlockSpec((1, D, N), lambda i, k: (k, 0, 0))],
        out_specs=[pl.BlockSpec((tm, N), lambda i, k: (i, k)), row],
        out_shape=[_out((T, S * N), BF16), _out((T, D), BF16)],
        scratch_shapes=[pltpu.VMEM((tm, D), BF16)],
        compiler_params=_params(2, VMEM_LIMIT_WIDE),
    )(*_hbm(x, g_pre, w))


def _mix_out_fwd(h, o_a, o_b, g_sb, g_ch, w_out, g_post, name):
    T, D = h.shape
    W = g_sb.shape[1]
    tm = min(WIDE_ROW_BLOCK, T)

    def body(h_ref, oa_ref, ob_ref, gsb_ref, gch_ref, w_ref, gpost_ref, h2_ref, mixed_ref, mo_ref):
        mixed_ref[:, :W] = _rms_fwd(oa_ref[...], gsb_ref[...]).astype(BF16)
        mixed_ref[:, W:] = _rms_fwd(ob_ref[...], gch_ref[...]).astype(BF16)
        mo = _dot(mixed_ref[...], w_ref[...])
        mo_ref[...] = mo
        h2_ref[...] = h_ref[...] + _rms_fwd(mo, gpost_ref[...])

    row = pl.BlockSpec((tm, D), lambda i: (i, 0))
    part = pl.BlockSpec((tm, W), lambda i: (i, 0))
    half = pl.BlockSpec((1, W), lambda i: (0, 0))
    return pl.pallas_call(
        body, name=name, grid=(T // tm,),
        in_specs=[row, part, part, half, half, pl.BlockSpec((D, D), lambda i: (0, 0)), pl.BlockSpec((1, D), lambda i: (0, 0))],
        out_specs=[row, row, row],
        out_shape=[_out((T, D), F32), _out((T, D), BF16),
                   _out((T, D), F32)],
        compiler_params=_params(1, VMEM_LIMIT_WIDE),
    )(*_hbm(h, o_a, o_b, g_sb, g_ch, w_out, g_post))


def _mix_out_bwd(dh, mo, g_post, w_out, o_a, o_b, g_sb, g_ch, name):
    T, D = dh.shape
    W = g_sb.shape[1]
    tm = min(WIDE_ROW_BLOCK, T)

    def body(dh_ref, mo_ref, gpost_ref, w_ref, oa_ref, ob_ref, gsb_ref, gch_ref,
             dmo_ref, doa_ref, dob_ref, dgpost_ref, dgsb_ref, dgch_ref):
        first = pl.program_id(0) == 0
        dmo, dgpost = _rms_bwd(mo_ref[...], gpost_ref[...], dh_ref[...])
        dmo_ref[...] = dmo.astype(BF16)
        dmix = _dot(dmo_ref[...], w_ref[...], NT)
        doa_ref[...], dgsb = _rms_bwd(oa_ref[...], gsb_ref[...], dmix[:, :W])
        dob_ref[...], dgch = _rms_bwd(ob_ref[...], gch_ref[...], dmix[:, W:])
        _accumulate(dgpost_ref, dgpost, first)
        _accumulate(dgsb_ref, dgsb, first)
        _accumulate(dgch_ref, dgch, first)

    row = pl.BlockSpec((tm, D), lambda i: (i, 0))
    part = pl.BlockSpec((tm, W), lambda i: (i, 0))
    vec = pl.BlockSpec((1, D), lambda i: (0, 0))
    half = pl.BlockSpec((1, W), lambda i: (0, 0))
    return pl.pallas_call(
        body, name=name, grid=(T // tm,),
        in_specs=[row, row, vec, pl.BlockSpec((D, D), lambda i: (0, 0)), part, part, half, half],
        out_specs=[row, part, part, vec, half, half],
        out_shape=[_out((T, D), BF16), _out((T, W), F32),
                   _out((T, W), F32), _out((1, D), F32),
                   _out((1, W), F32), _out((1, W), F32)],
        compiler_params=_params(1, VMEM_LIMIT_WIDE),
    )(*_hbm(dh, mo, g_post, w_out, o_a, o_b, g_sb, g_ch))


def _ple_loss(h, p, target, w_proj, w_gate, g_post, name):
    T, D = h.shape
    P = p.shape[1]
    S = N_CHIPS
    C = D // S
    tm = min(ROW_BLOCK, T)

    def body(h_ref, p_ref, t_ref, wp_ref, wg_ref, g_ref, loss_ref, dh_ref, dproj_ref, dgate_ref, dgain_ref):
        first = pl.program_id(0) == 0
        h3 = h_ref[...]
        proj = _dot(p_ref[...].astype(BF16), wp_ref[...])
        s = _sigmoid(_dot(h3.astype(BF16), wg_ref[...]))
        e = proj * s
        diff = h3 + _rms_fwd(e, g_ref[...]) - t_ref[...]
        part = 0.5 * jnp.sum(jnp.mean(diff * diff, axis=-1, keepdims=True), axis=0, keepdims=True)
        _accumulate(loss_ref, jnp.broadcast_to(part, loss_ref.shape), first)
        dy = diff * (1.0 / D)
        de, dgain = _rms_bwd(e, g_ref[...], dy)
        _accumulate(dgain_ref, dgain, first)
        dproj = (de * s).astype(BF16)
        for j in range(S):
            dproj_ref[j] = dproj[:, j * C:(j + 1) * C]
        dgate_ref[...] = (de * proj * s * (1.0 - s)).astype(BF16)
        dh_ref[...] = dy + _dot(dgate_ref[...], wg_ref[...], NT)

    row = pl.BlockSpec((tm, D), lambda i: (i, 0))
    vec = pl.BlockSpec((1, D), lambda i: (0, 0))
    return pl.pallas_call(
        body, name=name, grid=(T // tm,),
        in_specs=[row, pl.BlockSpec((tm, P), lambda i: (i, 0)), row,
                  pl.BlockSpec((P, D), lambda i: (0, 0)), pl.BlockSpec((D, D), lambda i: (0, 0)), vec],
        out_specs=[pl.BlockSpec((8, 128), lambda i: (0, 0)), row,
                   pl.BlockSpec((S, tm, C), lambda i: (0, i, 0)), row, vec],
        out_shape=[_out((8, 128), F32), _out((T, D), F32),
                   _out((S, T, C), BF16), _out((T, D), BF16),
                   _out((1, D), F32)],
        compiler_params=_params(1, VMEM_LIMIT_WIDE),
    )(*_hbm(h, p, target, w_proj, w_gate, g_post))


def _sb_scores(q, kj, mask):
    z = _dot(q, kj, NT)
    sp = jnp.maximum(z, 0.0) + jnp.log(1.0 + jnp.exp(-jnp.abs(z)))
    return z, sp if mask is None else jnp.where(mask, sp, 0.0)


def _strict_causal():
    rows = lax.broadcasted_iota(jnp.int32, (SB_BLOCK, SB_BLOCK), 0)
    cols = lax.broadcasted_iota(jnp.int32, (SB_BLOCK, SB_BLOCK), 1)
    return cols < rows


def _tri(cmp):
    r = lax.broadcasted_iota(jnp.int32, (2 * SB_BLOCK, SB_BLOCK), 0) % SB_BLOCK
    c = lax.broadcasted_iota(jnp.int32, (2 * SB_BLOCK, SB_BLOCK), 1)
    return jnp.where(cmp(r, c), 1.0, 0.0).astype(BF16)


def _cum(x, tri):
    return _dot(jnp.concatenate(_split2(x), axis=1), tri)


def _pair_lanes():
    lane = lax.broadcasted_iota(jnp.int32, (1, PAIR), 1)
    return [lane < HEAD_DIM, lane >= HEAD_DIM]


def _only(lanes, x):
    return jnp.where(lanes, x, jnp.zeros_like(x))


def _sb_fwd(qkv, name):
    T = qkv.shape[0]
    B = SB_BLOCK
    W = SB_FWD_PAIRS * PAIR
    steps = N_HEADS // (2 * SB_FWD_PAIRS)
    heads = [(p, h) for p in range(SB_FWD_PAIRS) for h in range(2)]

    def body(q_ref, k_ref, v_ref, o_ref):
        i = pl.program_id(1)
        after = _tri(lambda r, c: r > c)
        lanes = _pair_lanes()
        cols = [slice(p * PAIR, (p + 1) * PAIR) for p in range(SB_FWD_PAIRS)]
        q = {(p, h): _only(lanes[h], q_ref[:, cols[p]] * ATT_SCALE) for p, h in heads}

        def tiles(j, carries, mask):
            at = pl.ds(pl.multiple_of(j * B, B), B)
            scores = [_sb_scores(q[ph], k_ref[at, cols[ph[0]]], mask) for ph in heads]
            laters = [_cum(sp, after) for _, sp in scores]
            out = []
            for ph, (z, sp), later, (run, acc) in zip(heads, scores, laters, carries):
                a = jnp.exp(z - sp - later - run)
                if mask is not None:
                    a = jnp.where(mask, a, 0.0)
                out.append((run + later[:, 0:1] + sp[:, 0:1],
                            acc + _dot(a.astype(BF16), _only(lanes[ph[1]], v_ref[at, cols[ph[0]]]))))
            return tuple(out)

        zero = (jnp.zeros((B, 1), F32), jnp.zeros((B, PAIR), F32))
        carries = tiles(i, (zero,) * len(heads), _strict_causal())
        carries = lax.fori_loop(0, i, lambda jj, cs: tiles(i - 1 - jj, cs, None), carries)
        for p in range(SB_FWD_PAIRS):
            o_ref[:, cols[p]] = carries[2 * p][1] + carries[2 * p + 1][1]

    blk = lambda off: pl.BlockSpec((B, W), lambda g, i: (i, g + off))
    full = lambda off: pl.BlockSpec((T, W), lambda g, i: (0, g + off))
    return pl.pallas_call(
        body, name=name, grid=(steps, T // B),
        in_specs=[blk(0), full(steps), full(2 * steps)],
        out_specs=blk(0),
        out_shape=_out((T, N_HEADS * HEAD_DIM), F32),
        compiler_params=_params(2, VMEM_LIMIT),
    )(*_hbm(qkv, qkv, qkv))


def _sb_bwd(qkv, do, o, after, name):
    T = qkv.shape[0]
    B = SB_BLOCK
    W = SB_PAIRS * PAIR
    steps = N_HEADS // (2 * SB_PAIRS)
    n_blocks = T // B
    heads = [(p, h) for p in range(SB_PAIRS) for h in range(2)]

    def body(q_ref, k_ref, v_ref, do_ref, o_ref, dq_ref, dk_ref, dv_ref, dk_s, dv_s):
        i = pl.program_id(1)

        @pl.when(i == 0)
        def _():
            dk_s[...] = jnp.zeros_like(dk_s)
            dv_s[...] = jnp.zeros_like(dv_s)

        after = _tri(lambda r, c: r > c)
        since = _tri(lambda r, c: r >= c)
        lanes = _pair_lanes()
        cols = [slice(p * PAIR, (p + 1) * PAIR) for p in range(SB_PAIRS)]
        q = {(p, h): _only(lanes[h], q_ref[:, cols[p]] * ATT_SCALE) for p, h in heads}
        do = {(p, h): _only(lanes[h], do_ref[:, cols[p]].astype(BF16)) for p, h in heads}
        total = {ph: jnp.sum(do[ph].astype(F32) * o_ref[:, cols[ph[0]]], axis=1, keepdims=True) for ph in heads}

        def tiles(j, carries, mask):
            at = pl.ds(pl.multiple_of(j * B, B), B)
            ks = [k_ref[at, c] for c in cols]
            vs = [v_ref[at, c] for c in cols]
            scores = [_sb_scores(q[ph], ks[ph[0]], mask) for ph in heads]
            laters = [_cum(sp, after) for _, sp in scores]
            das = [_dot(do[ph], vs[ph[0]], NT) for ph in heads]
            a_s, gs = [], []
            for (z, sp), later, da, carry in zip(scores, laters, das, carries):
                a = jnp.exp(z - sp - later - carry[0])
                if mask is not None:
                    a = jnp.where(mask, a, 0.0)
                a = a.astype(BF16)
                a_s.append(a)
                gs.append(a.astype(F32) * da)
            sinces = [_cum(g, since) for g in gs]
            dzs = []
            for ph, (_, sp), g, from_s, carry in zip(heads, scores, gs, sinces, carries):
                g_before = total[ph] - carry[1] - from_s
                fail = jnp.exp(-sp)
                dz = fail * (g + g_before) - g_before
                if mask is not None:
                    dz = jnp.where(mask, dz, 0.0)
                dzs.append(dz.astype(BF16))
            out = []
            for ph, (_, sp), a, dz, later, from_s, carry in zip(heads, scores, a_s, dzs, laters, sinces, carries):
                dk_s[at, cols[ph[0]]] += _dot(dz, q[ph], TN)
                dv_s[at, cols[ph[0]]] += _dot(a, do[ph], TN)
                out.append((carry[0] + later[:, 0:1] + sp[:, 0:1], carry[1] + from_s[:, 0:1],
                            carry[2] + _dot(dz, _only(lanes[ph[1]], ks[ph[0]]))))
            return tuple(out)

        col = jnp.zeros((B, 1), F32)
        zero = (col, col, jnp.zeros((B, PAIR), F32))
        carries = tiles(i, (zero,) * len(heads), _strict_causal())
        last = lax.fori_loop(0, i, lambda jj, cs: tiles(i - 1 - jj, cs, None), carries)
        for p in range(SB_PAIRS):
            dq_ref[:, cols[p]] = ((last[2 * p][2] + last[2 * p + 1][2]) * ATT_SCALE).astype(BF16)

        @pl.when(i == n_blocks - 1)
        def _():
            dk_ref[...] = dk_s[...].astype(BF16)
            dv_ref[...] = dv_s[...].astype(BF16)

    blk = lambda off: pl.BlockSpec((B, W), lambda g, i: (i, g + off))
    full = lambda off: pl.BlockSpec((T, W), lambda g, i: (0, g + off))
    out = _out((T, N_HEADS * HEAD_DIM), BF16)
    return pl.pallas_call(
        lambda after_ref, *refs: body(*refs), name=name, grid=(steps, n_blocks),
        in_specs=[ANY, blk(0), full(steps), full(2 * steps), blk(0), blk(0)],
        out_specs=[blk(0), full(0), full(0)],
        out_shape=[out, out, out],
        scratch_shapes=[pltpu.VMEM((T, W), F32)] * 2,
        compiler_params=_params(2, VMEM_LIMIT),
    )(after, *_hbm(qkv, qkv, qkv, do, o))


NEAR = BAND - PAD + REL_CLIP
FAR = BAND - NEAR
NEAR_REL = 2 * REL_CLIP
BIAS_ROWS = 8


def _rel_onehot(i, transposed):
    shape = (NEAR, NEAR_REL) if transposed else (NEAR_REL, NEAR)
    j = FAR + lax.broadcasted_iota(jnp.int32, shape, 0 if transposed else 1)
    r = lax.broadcasted_iota(jnp.int32, shape, 1 if transposed else 0)
    idx = jnp.clip(i + PAD - j, -REL_CLIP, REL_CLIP) + REL_CLIP
    return jnp.where(idx - 1 == r, 1.0, 0.0).astype(BF16)


def _bias_table(rel_bias, name):
    def body(near_ref, far_ref, o_ref):
        rb = near_ref[...]
        hi, lo = _split2(rb)
        lo2 = (rb - hi.astype(F32) - lo.astype(F32)).astype(BF16)
        far = jnp.broadcast_to(far_ref[...], (N_HEADS, FAR))
        for k in range(BIAS_ROWS):
            onehot = _rel_onehot(pl.program_id(0) * BIAS_ROWS + k, False)
            o_ref[k, :, :FAR] = far
            o_ref[k, :, FAR:] = _dot(hi, onehot) + _dot(lo, onehot) + _dot(lo2, onehot)

    return pl.pallas_call(
        body, name=name, grid=(CHUNK // BIAS_ROWS,),
        in_specs=[pl.BlockSpec((N_HEADS, NEAR_REL), lambda i: (0, 0)), pl.BlockSpec((N_HEADS, 1), lambda i: (0, 0))],
        out_specs=pl.BlockSpec((BIAS_ROWS, N_HEADS, BAND), lambda i: (i, 0, 0)),
        out_shape=_out((CHUNK, N_HEADS, BAND), F32),
        compiler_params=_params(1),
    )(*_hbm(rel_bias[:, 1:], rel_bias[:, N_REL - 1:]))


def _bias_grad(dbias_t, name):
    def body(d_ref, near_ref, far_ref):
        near, far = None, None
        for k in range(BIAS_ROWS):
            onehot = _rel_onehot(pl.program_id(0) * BIAS_ROWS + k, True)
            hi, lo = _split2(d_ref[k, :, FAR:])
            part = _dot(hi, onehot) + _dot(lo, onehot)
            rest = jnp.sum(d_ref[k, :, :FAR], axis=1, keepdims=True)
            near, far = (part, rest) if near is None else (near + part, far + rest)
        first = pl.program_id(0) == 0
        _accumulate(near_ref, near, first)
        _accumulate(far_ref, jnp.broadcast_to(far, far_ref.shape), first)

    near, far = pl.pallas_call(
        body, name=name, grid=(CHUNK // BIAS_ROWS,),
        in_specs=[pl.BlockSpec((BIAS_ROWS, N_HEADS, BAND), lambda i: (i, 0, 0))],
        out_specs=[pl.BlockSpec((N_HEADS, NEAR_REL), lambda i: (0, 0)), pl.BlockSpec((N_HEADS, 128), lambda i: (0, 0))],
        out_shape=[_out((N_HEADS, NEAR_REL), F32), _out((N_HEADS, 128), F32)],
        compiler_params=_params(1),
    )(*_hbm(dbias_t))
    return jnp.pad(near, ((0, 0), (1, 0))).at[:, N_REL - 1].add(far[:, 0])


def _ch_probs(scores, bias, valid):
    z = jnp.where(valid, scores * ATT_SCALE + bias, NEG_INF)
    e = jnp.exp(z - jnp.max(z, axis=-1, keepdims=True))
    return e / jnp.sum(e, axis=-1, keepdims=True)


CH_HEADS = [(pair, h) for pair in range(N_HEADS // 2) for h in range(2)]
CH_COLS = [slice(pair * PAIR, (pair + 1) * PAIR) for pair in range(N_HEADS // 2)]


CH_GROUP = 2
CH_Q = CH_GROUP * CHUNK
CH_WIN = (LOOKBACK + CH_GROUP) * CHUNK


def _ch_valid(n):
    row_chunk = lax.broadcasted_iota(jnp.int32, (CH_Q, CH_WIN), 0) // CHUNK
    slot = lax.broadcasted_iota(jnp.int32, (CH_Q, CH_WIN), 1)
    ahead = slot // CHUNK - row_chunk
    return (ahead >= 0) & (ahead <= LOOKBACK) & (n * CH_Q + slot >= PAD)


def _ch_group_bias(bias):
    shifted = [jnp.pad(bias, ((0, 0), (0, 0), (c * CHUNK, (CH_GROUP - 1 - c) * CHUNK))) for c in range(CH_GROUP)]
    return jnp.concatenate(shifted, axis=1)


def _ch_fold_bias_grad(dbias):
    parts = [dbias[:, c * CHUNK:(c + 1) * CHUNK, c * CHUNK:c * CHUNK + BAND] for c in range(CH_GROUP)]
    return sum(parts[1:], parts[0])


def _ch_fwd(qkv, bias, name):
    T = qkv.shape[0]
    W = N_HEADS * HEAD_DIM

    def body(q_ref, k_ref, v_ref, b_ref, o_ref, kp, vp):
        n = pl.program_id(0)

        @pl.when(n == 0)
        def _():
            _ch_load_padded(k_ref, v_ref, kp, vp)

        win = pl.ds(pl.multiple_of(n * CH_Q, CH_Q), CH_WIN)
        valid = _ch_valid(n)
        lanes = _pair_lanes()
        scores = [_dot(_only(lanes[h], q_ref[:, CH_COLS[pair]]), kp[win, CH_COLS[pair]], NT) for pair, h in CH_HEADS]
        probs = [_ch_probs(s, b_ref[2 * pair + h], valid).astype(BF16) for s, (pair, h) in zip(scores, CH_HEADS)]
        outs = [_dot(p, _only(lanes[h], vp[win, CH_COLS[pair]])) for p, (pair, h) in zip(probs, CH_HEADS)]
        for pair, cols in enumerate(CH_COLS):
            o_ref[:, cols] = outs[2 * pair] + outs[2 * pair + 1]

    full = lambda col: pl.BlockSpec((T, W), lambda n: (0, col))
    return pl.pallas_call(
        body, name=name, grid=(T // CH_Q,),
        in_specs=[pl.BlockSpec((CH_Q, W), lambda n: (n, 3)), full(4), full(5),
                  pl.BlockSpec((N_HEADS, CH_Q, CH_WIN), lambda n: (0, 0, 0))],
        out_specs=pl.BlockSpec((CH_Q, W), lambda n: (n, 0)),
        out_shape=_out((T, W), F32),
        scratch_shapes=[pltpu.VMEM((PAD + T, W), BF16)] * 2,
        compiler_params=_params(1, VMEM_LIMIT),
    )(*_hbm(qkv, qkv, qkv, bias))


def _ch_load_padded(k_ref, v_ref, kp, vp):
    for src, dst in ((k_ref, kp), (v_ref, vp)):
        dst[:PAD, :] = jnp.zeros((PAD, dst.shape[1]), dst.dtype)
        dst[PAD:, :] = src[...]


def _ch_bwd(qkv, bias, do, after, name):
    T = qkv.shape[0]
    W = N_HEADS * HEAD_DIM
    n_chunks = T // CH_Q

    def body(q_ref, k_ref, v_ref, b_ref, do_ref, dq_ref, dk_ref, dv_ref, db_ref, kp, vp, dk_s, dv_s):
        n = pl.program_id(0)

        @pl.when(n == 0)
        def _():
            _ch_load_padded(k_ref, v_ref, kp, vp)
            dk_s[...] = jnp.zeros_like(dk_s)
            dv_s[...] = jnp.zeros_like(dv_s)
            db_ref[...] = jnp.zeros_like(db_ref)

        win = pl.ds(pl.multiple_of(n * CH_Q, CH_Q), CH_WIN)
        valid = _ch_valid(n)
        lanes = _pair_lanes()
        kws = [kp[win, cols] for cols in CH_COLS]
        vws = [vp[win, cols] for cols in CH_COLS]
        qs = [_only(lanes[h], q_ref[:, CH_COLS[pair]]) for pair, h in CH_HEADS]
        dos = [_only(lanes[h], do_ref[:, CH_COLS[pair]].astype(BF16)) for pair, h in CH_HEADS]
        scores = [_dot(q, kws[pair], NT) for q, (pair, _) in zip(qs, CH_HEADS)]
        dps = [_dot(do, vws[pair], NT) for do, (pair, _) in zip(dos, CH_HEADS)]
        probs = [_ch_probs(s, b_ref[2 * pair + h], valid) for s, (pair, h) in zip(scores, CH_HEADS)]
        dzs = [p * (dp - jnp.sum(dp * p, axis=-1, keepdims=True)) for p, dp in zip(probs, dps)]
        for k, dz in enumerate(dzs):
            db_ref[k] += dz
        dzbs = [(dz * ATT_SCALE).astype(BF16) for dz in dzs]
        dqs = [_dot(dz, _only(lanes[h], kws[pair])) for dz, (pair, h) in zip(dzbs, CH_HEADS)]
        dks = [_dot(dz, q, TN) for dz, q in zip(dzbs, qs)]
        dvs = [_dot(p.astype(BF16), do, TN) for p, do in zip(probs, dos)]
        for pair, cols in enumerate(CH_COLS):
            dq_ref[:, cols] = (dqs[2 * pair] + dqs[2 * pair + 1]).astype(BF16)
            dk_s[win, cols] += dks[2 * pair] + dks[2 * pair + 1]
            dv_s[win, cols] += dvs[2 * pair] + dvs[2 * pair + 1]

        @pl.when(n == n_chunks - 1)
        def _():
            dk_ref[...] = dk_s[PAD:, :].astype(BF16)
            dv_ref[...] = dv_s[PAD:, :].astype(BF16)

    full = lambda col: pl.BlockSpec((T, W), lambda n: (0, col))
    blk = lambda col: pl.BlockSpec((CH_Q, W), lambda n: (n, col))
    tab = pl.BlockSpec((N_HEADS, CH_Q, CH_WIN), lambda n: (0, 0, 0))
    out = _out((T, W), BF16)
    return pl.pallas_call(
        lambda after_ref, *refs: body(*refs), name=name, grid=(n_chunks,),
        in_specs=[ANY, blk(3), full(4), full(5), tab, blk(0)],
        out_specs=[blk(0), full(0), full(0), tab],
        out_shape=[out, out, out, _out((N_HEADS, CH_Q, CH_WIN), F32)],
        scratch_shapes=[pltpu.VMEM((PAD + T, W), BF16)] * 2 + [pltpu.VMEM((PAD + T, W), F32)] * 2,
        compiler_params=_params(1, VMEM_LIMIT),
    )(after, *_hbm(qkv, qkv, qkv, bias, do))


def _rows_split(a, parts):
    return a.reshape(a.shape[:-2] + (parts, a.shape[-2] // parts, a.shape[-1]))


def _cast_into_own_slot(me, c, ws, in_chip_order, name):
    parts = 2
    ws = [_rows_split(_rows_split(w, 2), parts) for w in ws]
    n = len(ws)

    def body(me_ref, c_ref, *refs):
        for src, dst in zip(refs[:n], refs[n:]):
            dst[0, 0, 0] = src[0, 0].astype(BF16)

    def specs(w, plain):
        block = (1, 1) + w.shape[2:]
        if plain:
            return (pl.BlockSpec(block, lambda d, r, me_ref, c_ref: (d, r, 0, 0)),
                    pl.BlockSpec((1,) + block, lambda d, r, me_ref, c_ref: (me_ref[0], d, r, 0, 0)))
        return (pl.BlockSpec(block, lambda d, r, me_ref, c_ref: (d ^ c_ref[0], r, 0, 0)),
                pl.BlockSpec((1,) + block, lambda d, r, me_ref, c_ref: (0, d, r, 0, 0)))

    both = [specs(w, plain) for w, plain in zip(ws, in_chip_order)]
    outs = pl.pallas_call(
        body, name=name,
        grid_spec=pltpu.PrefetchScalarGridSpec(
            num_scalar_prefetch=2, grid=(2, parts),
            in_specs=[s[0] for s in both], out_specs=[s[1] for s in both]),
        out_shape=[_out((N_CHIPS,) + w.shape, BF16) for w in ws],
        compiler_params=_params(2, VMEM_LIMIT),
    )(me, c, *_hbm(*ws))
    return [o.reshape(N_CHIPS, 2, o.shape[2] * o.shape[3], o.shape[4]) for o in outs]


def _zone_slots(in_chip_order):
    x, y, c, _ = _place()
    me = 2 * x + y
    if in_chip_order:
        return (me, c), (lambda r: (me, c)), (lambda r: (me ^ r, c)), (lambda r: (me ^ r, c))
    return (0, 0), (lambda r: (r, 0)), (lambda r: (r, 0)), (lambda r: (r, 1))


def _pair_add(c, mine, got, permuted, name):
    parts = 2
    mine = [_rows_split(m, parts) for m in mine]
    got = [_rows_split(g, parts) for g in got]
    n = len(mine)

    def body(c_ref, *refs):
        for a, b, o in zip(refs[:n], refs[n:2 * n], refs[2 * n:]):
            o[0, 0] = (a[0, 0, 0] + b[0, 0].astype(F32)).astype(BF16)

    def mine_spec(m, perm):
        if perm:
            return pl.BlockSpec((1, 1, 1) + m.shape[3:], lambda j, r, c_ref: (j, 0, r, 0, 0))
        return pl.BlockSpec((1, 1, 1) + m.shape[3:], lambda j, r, c_ref: (j, c_ref[0], r, 0, 0))

    def got_spec(g):
        return pl.BlockSpec((1, 1) + g.shape[2:], lambda j, r, c_ref: (j, r, 0, 0))

    outs = pl.pallas_call(
        body, name=name,
        grid_spec=pltpu.PrefetchScalarGridSpec(
            num_scalar_prefetch=1, grid=(N_CHIPS, parts),
            in_specs=[mine_spec(m, perm) for m, perm in zip(mine, permuted)] + [got_spec(g) for g in got],
            out_specs=[got_spec(g) for g in got]),
        out_shape=[_out(g.shape, BF16) for g in got],
        compiler_params=_params(2, VMEM_LIMIT),
    )(c, *_hbm(*mine, *got))
    return [o.reshape(o.shape[0], o.shape[1] * o.shape[2], o.shape[3]) for o in outs]


def _chip_add(me, partials, landed, permuted, name):
    parts = 2
    ps = [_rows_split(x, parts) for x in partials]
    ls = [_rows_split(x, parts) for x in landed]
    n = len(ps)

    def body(me_ref, *refs):
        for own, got, o in zip(refs[:n], refs[n:2 * n], refs[2 * n:]):
            acc = own[0, 0].astype(F32)
            for r in range(N_CHIPS - 1):
                acc = acc + got[r, 0].astype(F32)
            o[0] = acc

    def own_spec(x, perm):
        if perm:
            return pl.BlockSpec((1, 1) + x.shape[2:], lambda r, me_ref: (0, r, 0, 0))
        return pl.BlockSpec((1, 1) + x.shape[2:], lambda r, me_ref: (me_ref[0], r, 0, 0))

    outs = pl.pallas_call(
        body, name=name,
        grid_spec=pltpu.PrefetchScalarGridSpec(
            num_scalar_prefetch=1, grid=(parts,),
            in_specs=[own_spec(x, perm) for x, perm in zip(ps, permuted)]
            + [pl.BlockSpec((N_CHIPS - 1, 1) + x.shape[2:], lambda r, me_ref: (0, r, 0, 0)) for x in ls],
            out_specs=[pl.BlockSpec((1,) + x.shape[2:], lambda r, me_ref: (r, 0, 0)) for x in ps]),
        out_shape=[_out(x.shape[1:], F32) for x in ps],
        compiler_params=_params(1, VMEM_LIMIT),
    )(me, *_hbm(*ps, *ls))
    return [o.reshape(o.shape[0] * o.shape[1], o.shape[2]) for o in outs]


def _adamw_math(w, g, m, v):
    m = ADAM_B1 * m + (1.0 - ADAM_B1) * g
    v = ADAM_B2 * v + (1.0 - ADAM_B2) * (g * g)
    m_hat = m / (1.0 - ADAM_B1 ** ADAM_STEP)
    v_hat = v / (1.0 - ADAM_B2 ** ADAM_STEP)
    delta = -ADAM_LR * (m_hat / (jnp.sqrt(v_hat) + ADAM_EPS) + ADAM_WD * w)
    return delta, m, v


def _adamw(ws, gs, ms, vs, parts, name):
    n = len(ws)
    flat = [_rows_split(a, parts) for a in (*ws, *gs, *ms, *vs)]

    def body(*refs):
        ins, outs = refs[:4 * n], refs[4 * n:]
        for k in range(n):
            d, m, v = _adamw_math(ins[k][...], ins[n + k][...], ins[2 * n + k][...], ins[3 * n + k][...])
            outs[k][...] = d
            outs[n + k][...] = m
            outs[2 * n + k][...] = v

    spec = lambda a: pl.BlockSpec((1,) + a.shape[1:], lambda i: (i, 0, 0))
    outs = pl.pallas_call(
        body, name=name, grid=(parts,),
        in_specs=[spec(a) for a in flat], out_specs=[spec(a) for a in flat[:n]] * 3,
        out_shape=[_out(a.shape, F32) for a in flat[:n]] * 3,
        compiler_params=_params(1, VMEM_LIMIT),
    )(*_hbm(*flat))
    outs = [o.reshape(o.shape[0] * o.shape[1], o.shape[2]) for o in outs]
    return outs[:n], outs[n:2 * n], outs[2 * n:]


def _adamw_halves(c, ws, owns, others, ms, vs, name):
    parts = 4
    n = len(ws)
    whole = [_rows_split(_rows_split(a, 2), parts) for a in (*ws, *ms, *vs)]
    halves = [_rows_split(a, parts) for a in (*owns, *others)]

    def body(c_ref, *refs):
        ins, outs = refs[:5 * n], refs[5 * n:]
        mine = pl.program_id(0) == c_ref[0]
        for k in range(n):
            g = jnp.where(mine, ins[3 * n + k][0], ins[4 * n + k][0])
            d, m, v = _adamw_math(ins[k][0, 0], g, ins[n + k][0, 0], ins[2 * n + k][0, 0])
            for slot, val in enumerate((g, d, m, v)):
                outs[slot * n + k][0, 0] = val

    wspec = lambda a: pl.BlockSpec((1, 1) + a.shape[2:], lambda h, r, c_ref: (h, r, 0, 0))
    hspec = lambda a: pl.BlockSpec((1,) + a.shape[1:], lambda h, r, c_ref: (r, 0, 0))
    outs = pl.pallas_call(
        body, name=name,
        grid_spec=pltpu.PrefetchScalarGridSpec(
            num_scalar_prefetch=1, grid=(2, parts),
            in_specs=[wspec(a) for a in whole] + [hspec(a) for a in halves],
            out_specs=[wspec(a) for a in whole[:n]] * 4),
        out_shape=[_out(a.shape, F32) for a in whole[:n]] * 4,
        compiler_params=_params(2, VMEM_LIMIT),
    )(c, *_hbm(*whole, *halves))
    outs = [o.reshape(2 * parts * o.shape[2], o.shape[3]) for o in outs]
    return outs[:n], outs[n:2 * n], outs[2 * n:3 * n], outs[3 * n:]


def _place():
    x, y, c = lax.axis_index("x"), lax.axis_index("y"), lax.axis_index("c")
    peers = [(x ^ (r >> 1), y ^ (r & 1), c) for r in (1, 2, 3)]
    return x, y, c, peers


def _handshake(peers):
    barrier = pltpu.get_barrier_semaphore()
    for peer in peers:
        pl.semaphore_signal(barrier, inc=1, device_id=peer, device_id_type=MESH)
    pl.semaphore_wait(barrier, len(peers))


ANY = pl.BlockSpec(memory_space=pl.ANY)
HBM = pl.BlockSpec(memory_space=pltpu.HBM)
SEM = pl.BlockSpec(memory_space=pltpu.SEMAPHORE)
SPLIT_COPY = pltpu.SideEffectType.DATAFLOW_SIDE_EFFECTING


def _split_start(body, name, collective_id, operands, n_sems, after=None):
    n = len(operands)
    extra = [] if after is None else [after]

    def wrapped(*refs):
        at = n + len(extra)
        body(refs[:n], refs[at], refs[at + 1])
        token = refs[-1]
        token[...] = jnp.zeros_like(token)

    outs = pl.pallas_call(
        wrapped, name=name,
        in_specs=[HBM] * n + [ANY] * len(extra),
        out_shape=(pltpu.SemaphoreType.DMA((n_sems,)), pltpu.SemaphoreType.DMA((n_sems,)),
                   *[pltpu.HBM(a.shape, a.dtype) for a in operands], jax.ShapeDtypeStruct((8, 128), F32)),
        out_specs=(SEM, SEM, *[HBM] * n, pl.BlockSpec(memory_space=pltpu.VMEM)),
        input_output_aliases={i: 2 + i for i in range(n)},
        compiler_params=pltpu.CompilerParams(has_side_effects=SPLIT_COPY, collective_id=collective_id),
    )(*_hbm(*operands), *extra)
    return outs[0], outs[1], list(outs[2:2 + n]), outs[-1]


def _split_wait(body, name, send_sem, recv_sem, operands, after):
    n = len(operands)

    def wrapped(*refs):
        body(refs[:n], refs[n], refs[n + 1])

    outs = pl.pallas_call(
        wrapped, name=name,
        in_specs=[HBM] * n + [SEM, SEM, ANY],
        out_shape=tuple(pltpu.HBM(a.shape, a.dtype) for a in operands),
        out_specs=tuple([HBM] * n),
        input_output_aliases={i: i for i in range(n)},
        compiler_params=pltpu.CompilerParams(has_side_effects=SPLIT_COPY),
    )(*operands, send_sem, recv_sem, after)
    return list(outs)


def _gather_copies(lands, in_chip_order, send_sem, recv_sem):
    peers = _place()[3]
    copies = []
    for a, (land, plain) in enumerate(zip(lands, in_chip_order)):
        own, sent_to, _, _ = _zone_slots(plain)
        copies += [pltpu.make_async_remote_copy(
            src_ref=land.at[own], dst_ref=land.at[sent_to(r + 1)],
            send_sem=send_sem.at[a * 3 + r], recv_sem=recv_sem.at[a * 3 + r],
            device_id=peers[r], device_id_type=MESH) for r in range(3)]
    return copies


def _gather_start(lands, in_chip_order, name, collective_id, after):
    def body(refs, send_sem, recv_sem):
        _handshake(_place()[3])
        for cp in _gather_copies(refs, in_chip_order, send_sem, recv_sem):
            cp.start()

    return _split_start(body, name, collective_id, list(lands), 3 * len(lands), after)


def _gather_wait(send_sem, recv_sem, operands, in_chip_order, after, name):
    def body(refs, send_sem, recv_sem):
        for cp in _gather_copies(refs, in_chip_order, send_sem, recv_sem):
            cp.wait_send()
            cp.wait_recv()

    return _split_wait(body, name, send_sem, recv_sem, operands, after)


def _gather_finish(lands, in_chip_order, with_ici, name):
    n = len(lands)

    def body(*refs):
        land = refs[n:2 * n]
        send_ici, recv_ici, send_d2d, recv_d2d = refs[2 * n:]
        x, y, c, _ = _place()
        ici = _gather_copies(land, in_chip_order, send_ici, recv_ici) if with_ici else []
        for cp in ici:
            cp.start()
        passed = []
        for a in range(n):
            _, _, received, kept = _zone_slots(in_chip_order[a])
            passed += [pltpu.make_async_remote_copy(
                src_ref=land[a].at[received(r + 1)], dst_ref=land[a].at[kept(r + 1)],
                send_sem=send_d2d.at[a * 3 + r], recv_sem=recv_d2d.at[a * 3 + r],
                device_id=(x, y, 1 - c), device_id_type=MESH) for r in range(3)]
        for k, cp in enumerate(passed):
            if with_ici:
                ici[k].wait_recv()
            cp.start()
        for cp in passed:
            cp.wait_recv()
        for cp in ici:
            cp.wait_send()
        for cp in passed:
            cp.wait_send()

    outs = pl.pallas_call(
        body, name=name,
        in_specs=[ANY] * n, out_specs=[ANY] * n,
        out_shape=[_out(l.shape, l.dtype) for l in lands],
        input_output_aliases={a: a for a in range(n)},
        scratch_shapes=[pltpu.SemaphoreType.DMA((3 * n,))] * 4,
    )(*lands)
    return list(outs)


def _pass_copies(lands, in_chip_order, send_sem, recv_sem):
    x, y, c, _ = _place()
    copies = []
    for a, (land, plain) in enumerate(zip(lands, in_chip_order)):
        _, _, received, kept = _zone_slots(plain)
        copies += [pltpu.make_async_remote_copy(
            src_ref=land.at[received(r + 1)], dst_ref=land.at[kept(r + 1)],
            send_sem=send_sem.at[a * 3 + r], recv_sem=recv_sem.at[a * 3 + r],
            device_id=(x, y, 1 - c), device_id_type=MESH) for r in range(3)]
    return copies


def _pass_start(lands, in_chip_order, name, collective_id):
    def body(refs, send_sem, recv_sem):
        x, y, c, _ = _place()
        _handshake([(x, y, 1 - c)])
        for cp in _pass_copies(refs, in_chip_order, send_sem, recv_sem):
            cp.start()

    return _split_start(body, name, collective_id, list(lands), 3 * len(lands))


def _pass_wait(send_sem, recv_sem, lands, in_chip_order, after, name):
    def body(refs, send_sem, recv_sem):
        for cp in _pass_copies(refs, in_chip_order, send_sem, recv_sem):
            cp.wait_send()
            cp.wait_recv()

    return _split_wait(body, name, send_sem, recv_sem, lands, after)


def _slabs(land):
    return land.reshape(N_CHIPS, 2 * land.shape[2], land.shape[3])


def _pair_swap(grads, permuted, name):
    n = len(grads)

    def body(*refs):
        src, dst = refs[:n], refs[n:2 * n]
        send_sem, recv_sem = refs[2 * n:]
        x, y, c, _ = _place()
        copies = [pltpu.make_async_remote_copy(
            src_ref=src[a].at[:, 1] if permuted[a] else src[a].at[:, 1 - c], dst_ref=dst[a],
            send_sem=send_sem.at[a], recv_sem=recv_sem.at[a],
            device_id=(x, y, 1 - c), device_id_type=MESH) for a in range(n)]
        for cp in copies:
            cp.start()
        for cp in copies:
            cp.wait()

    return pl.pallas_call(
        body, name=name,
        in_specs=[ANY] * n, out_specs=[ANY] * n,
        out_shape=[_out((N_CHIPS,) + g.shape[2:], g.dtype) for g in grads],
        scratch_shapes=[pltpu.SemaphoreType.DMA((n,))] * 2,
    )(*grads)


def _swap_copies(refs, permuted, send_sem, recv_sem):
    n = len(refs) // 2
    x, y, c, _ = _place()
    return [pltpu.make_async_remote_copy(
        src_ref=refs[a].at[:, 1] if permuted[a] else refs[a].at[:, 1 - c], dst_ref=refs[n + a],
        send_sem=send_sem.at[a], recv_sem=recv_sem.at[a],
        device_id=(x, y, 1 - c), device_id_type=MESH) for a in range(n)]


def _pair_swap_start(grads, permuted, name, collective_id):
    def body(refs, send_sem, recv_sem):
        x, y, c, _ = _place()
        _handshake([(x, y, 1 - c)])
        for cp in _swap_copies(refs, permuted, send_sem, recv_sem):
            cp.start()

    lands = [lax.empty((N_CHIPS,) + g.shape[2:], g.dtype) for g in grads]
    return _split_start(body, name, collective_id, list(grads) + lands, len(grads))


def _pair_swap_wait(send_sem, recv_sem, operands, permuted, after, name):
    def body(refs, send_sem, recv_sem):
        for cp in _swap_copies(refs, permuted, send_sem, recv_sem):
            cp.wait_send()
            cp.wait_recv()

    return _split_wait(body, name, send_sem, recv_sem, operands, after)


def _scatter_copies(refs, permuted, send_sem, recv_sem):
    n = len(refs) // 2
    x, y, _, peers = _place()
    me = 2 * x + y
    return [pltpu.make_async_remote_copy(
        src_ref=refs[a].at[r + 1] if permuted[a] else refs[a].at[me ^ (r + 1)], dst_ref=refs[n + a].at[r],
        send_sem=send_sem.at[a * 3 + r], recv_sem=recv_sem.at[a * 3 + r],
        device_id=peers[r], device_id_type=MESH) for a in range(n) for r in range(3)]


def _scatter_start(partials, permuted, name, collective_id):
    def body(refs, send_sem, recv_sem):
        _handshake(_place()[3])
        for cp in _scatter_copies(refs, permuted, send_sem, recv_sem):
            cp.start()

    lands = [lax.empty((N_CHIPS - 1,) + p.shape[1:], p.dtype) for p in partials]
    return _split_start(body, name, collective_id, list(partials) + lands, 3 * len(partials))


def _scatter_wait(send_sem, recv_sem, operands, permuted, after, name):
    def body(refs, send_sem, recv_sem):
        for cp in _scatter_copies(refs, permuted, send_sem, recv_sem):
            cp.wait_send()
            cp.wait_recv()

    return _split_wait(body, name, send_sem, recv_sem, operands, after)


def _pair_join(halves, name):
    n = len(halves)

    def body(*refs):
        src, dst = refs[:n], refs[n:2 * n]
        send_sem, recv_sem = refs[2 * n:]
        x, y, c, _ = _place()
        copies = [pltpu.make_async_remote_copy(
            src_ref=src[a], dst_ref=dst[a], send_sem=send_sem.at[a], recv_sem=recv_sem.at[a],
            device_id=(x, y, 1 - c), device_id_type=MESH) for a in range(n)]
        for cp in copies:
            cp.start()
        for cp in copies:
            cp.wait()

    return pl.pallas_call(
        body, name=name,
        in_specs=[ANY] * n, out_specs=[ANY] * n,
        out_shape=[_out(h.shape, F32) for h in halves],
        scratch_shapes=[pltpu.SemaphoreType.DMA((n,))] * 2,
    )(*halves)


def _join_copies(refs, send_sem, recv_sem):
    n = len(refs) // 2
    x, y, c, _ = _place()
    return [pltpu.make_async_remote_copy(
        src_ref=refs[a], dst_ref=refs[n + a], send_sem=send_sem.at[a], recv_sem=recv_sem.at[a],
        device_id=(x, y, 1 - c), device_id_type=MESH) for a in range(n)]


def _pair_join_start(halves, name, collective_id):
    def body(refs, send_sem, recv_sem):
        x, y, c, _ = _place()
        _handshake([(x, y, 1 - c)])
        for cp in _join_copies(refs, send_sem, recv_sem):
            cp.start()

    lands = [lax.empty(h.shape, h.dtype) for h in halves]
    return _split_start(body, name, collective_id, list(halves) + lands, len(halves))


def _pair_join_wait(send_sem, recv_sem, operands, after, name):
    def body(refs, send_sem, recv_sem):
        for cp in _join_copies(refs, send_sem, recv_sem):
            cp.wait_send()
            cp.wait_recv()

    return _split_wait(body, name, send_sem, recv_sem, operands, after)


def _all_sum_small(v, after, name):
    R, C = v.shape
    n_dev = 8

    def body(v_ref, after_ref, o_ref, buf, send_sem, recv_sem):
        x, y, c, _ = _place()
        me = 4 * x + 2 * y + c
        buf[me] = v_ref[...]
        copies = []
        for k in range(1, n_dev):
            peer = (x ^ (k >> 2), y ^ ((k >> 1) & 1), c ^ (k & 1))
            copies.append(pltpu.make_async_remote_copy(
                src_ref=v_ref, dst_ref=buf.at[me], send_sem=send_sem.at[k - 1], recv_sem=recv_sem.at[k - 1],
                device_id=peer, device_id_type=MESH))
        for cp in copies:
            cp.start()
        for cp in copies:
            cp.wait()
        acc = buf[0]
        for m in range(1, n_dev):
            acc = acc + buf[m]
        o_ref[...] = acc

    return pl.pallas_call(
        body, name=name,
        in_specs=[pl.BlockSpec(memory_space=pltpu.VMEM), ANY], out_specs=pl.BlockSpec(memory_space=pltpu.VMEM),
        out_shape=jax.ShapeDtypeStruct((R, C), F32),
        scratch_shapes=[pltpu.VMEM((n_dev, R, C), F32), pltpu.SemaphoreType.DMA((n_dev - 1,)),
                        pltpu.SemaphoreType.DMA((n_dev - 1,))],
    )(v, after)


class _WholeWeights:
    def __init__(self, w):
        self.w = w

    def weights(self, group, after=None):
        return ({} if group == "passed" else self.w), None

    def grads_ready(self, group, gw):
        return None

    def grads_sent(self, group, after):
        return None


def _local_step(x, p, target, gains, rel_bias, hooks):
    T, D = x.shape
    S = N_CHIPS

    tied = lambda gain, token: gain if token is None else gain + token[0, 0]
    w, token = hooks.weights("first")
    w = dict(w)
    xn1, g1, u1, a1 = _ffn_up(x, tied(gains["ffn1_pre"], token), w["ffn1_gate"], w["ffn1_up"], "ffn1_up")
    w.update(hooks.weights("down", a1)[0])
    h1, f1 = _ffn_down(x, a1, gains["ffn1_post"], w["ffn1_down"], "ffn1_down")
    more, token = hooks.weights("in", h1)
    w.update(more)
    qkv, un = _norm_proj(h1, tied(gains["mix_pre"], token), w["in"], "qkv_proj")
    bias = _ch_group_bias(_bias_table(rel_bias, "bias_table").transpose(1, 0, 2))
    o_a = _sb_fwd(qkv, "sb_fwd")
    o_b = _ch_fwd(qkv, bias, "ch_fwd")
    more, token = hooks.weights("rest", o_b)
    w.update(more)
    w_out = w["out"].reshape(D, D)
    h2, mixed, mo = _mix_out_fwd(h1, o_a, o_b, gains["out_sb"], gains["out_ch"], w_out,
                                 tied(gains["mix_post"], token), "mix_out_fwd")
    w.update(hooks.weights("passed", h2)[0])
    h3, xn2, g2, u2, a2, f2 = _ffn_fwd(h2, gains["ffn2_pre"], gains["ffn2_post"], w["ffn2_gate"], w["ffn2_up"],
                                       w["ffn2_down"], "ffn2_fwd")
    w_ple_proj = w["ple_proj"].transpose(1, 0, 2).reshape(p.shape[1], D)
    w_ple_gate = w["ple_gate"].reshape(D, D)

    loss, dh3, dproj, dgate, dg_ple = _ple_loss(h3, p, target, w_ple_proj, w_ple_gate, gains["ple_post"], "ple_loss")
    gw, gg = {}, {"ple_post": dg_ple}
    gw["ple_proj"] = _mm_tn(p[None], dproj, p.shape[1], "dw_ple_proj")
    row_sharded = lambda pair: tuple(o.reshape(S, D // S, D) for o in pair)
    gw["ple_gate"] = row_sharded(_mm_tn(h3[None], dgate[None], 512, "dw_ple_gate"))

    def ffn_bwd(tag, dh, x_in, xn, g_act, u_act, a_act, f, group):
        dgp, dup, df, gg[tag + "_post"] = _ffn_bwd_act(dh, f, gains[tag + "_post"], w[tag + "_down"], g_act, u_act,
                                                       tag + "_bwd_act")
        gw[tag + "_gate"] = _mm_tn(dgp, xn[None], dgp.shape[2], "dw_" + tag + "_gate")
        gw[tag + "_up"] = _mm_tn(dup, xn[None], dup.shape[2], "dw_" + tag + "_up")
        gw[tag + "_down"] = _mm_tn(a_act, df[None], a_act.shape[2], "dw_" + tag + "_down")
        g_pre = gains[tag + "_pre"]
        if group is not None:
            token = hooks.grads_ready(group, gw)
            g_pre = g_pre if token is None else g_pre + token[0, 0]
        dx, gg[tag + "_pre"] = _proj_bwd([dgp, dup], [w[tag + "_gate"], w[tag + "_up"]], x_in, g_pre, dh,
                                         tag + "_bwd_in")
        return dx

    dh2 = ffn_bwd("ffn2", dh3, h2, xn2, g2, u2, a2, f2, None)
    dmo, do_a, do_b, gg["mix_post"], gg["out_sb"], gg["out_ch"] = _mix_out_bwd(
        dh2, mo, gains["mix_post"], w_out, o_a, o_b, gains["out_sb"], gains["out_ch"], "mix_out_bwd")
    gw["out"] = row_sharded(_mm_tn(mixed[None], dmo[None], 512, "dw_out"))
    token = hooks.grads_ready("early", gw)
    dq_a, dk_a, dv_a = _sb_bwd(qkv, do_a, o_a, do_a if token is None else token, "sb_bwd")
    token = hooks.grads_sent("early", dq_a)
    dq_b, dk_b, dv_b, dbias = _ch_bwd(qkv, bias, do_b, do_b if token is None else token, "ch_bwd")
    g_rel = _bias_grad(_ch_fold_bias_grad(dbias).transpose(1, 0, 2), "bias_grad")
    dqkv = [dq_a, dk_a, dv_a, dq_b, dk_b, dv_b]
    gw["in"] = _dw_in(un, dqkv, w["in"].shape[2], 512, "dw_in")
    dh1, gg["mix_pre"] = _qkv_bwd_in(dqkv, w["in"], h1, gains["mix_pre"], dh2, "qkv_bwd_in")
    dx = ffn_bwd("ffn1", dh1, x, xn1, g1, u1, a1, f1, "late")
    return loss, dx, gw, gg, g_rel


BIG = ["ffn1_gate", "ffn1_up", "ffn1_down", "in", "out", "ffn2_gate", "ffn2_up", "ffn2_down", "ple_proj", "ple_gate"]
GAINS = ["ffn1_pre", "ffn1_post", "mix_pre", "mix_post", "out_sb", "out_ch", "ffn2_pre", "ffn2_post", "ple_post"]
TRANSPOSED = ("w_ffn1_gate", "w_ffn1_up", "w_ffn2_gate", "w_ffn2_up")
PERMUTED = ("ffn1_gate", "ffn1_up", "ffn1_down", "ffn2_gate", "ffn2_up", "ffn2_down")
W_GROUPS = {"first": ["ffn1_gate", "ffn1_up"], "down": ["ffn1_down"], "in": ["in"],
            "rest": ["out", "ffn2_gate", "ffn2_up", "ffn2_down", "ple_proj", "ple_gate"]}
G_GROUPS = {"early": ["ple_proj", "ple_gate", "ffn2_gate", "ffn2_up", "ffn2_down", "out"],
            "late": ["in", "ffn1_gate", "ffn1_up", "ffn1_down"]}
ORDER = ["g_ffn1_pre", "g_ffn1_post", "w_ffn1_gate", "w_ffn1_up", "w_ffn1_down", "g_mix_pre", "g_mix_post", "w_in",
         "g_out_sb", "g_out_ch", "rel_bias", "w_out", "g_ffn2_pre", "g_ffn2_post", "w_ffn2_gate", "w_ffn2_up",
         "w_ffn2_down", "w_ple_proj", "w_ple_gate", "g_ple_post"]


def kernel(x, p, g_ffn1_pre, g_ffn1_post, w_ffn1_gate, w_ffn1_up, w_ffn1_down, g_mix_pre, g_mix_post, w_in, g_out_sb, g_out_ch, rel_bias, w_out, g_ffn2_pre, g_ffn2_post, w_ffn2_gate, w_ffn2_up, w_ffn2_down, w_ple_proj, w_ple_gate, g_ple_post, loss_target, m_g_ffn1_pre, m_g_ffn1_post, m_w_ffn1_gate, m_w_ffn1_up, m_w_ffn1_down, m_g_mix_pre, m_g_mix_post, m_w_in, m_g_out_sb, m_g_out_ch, m_rel_bias, m_w_out, m_g_ffn2_pre, m_g_ffn2_post, m_w_ffn2_gate, m_w_ffn2_up, m_w_ffn2_down, m_w_ple_proj, m_w_ple_gate, m_g_ple_post, v_g_ffn1_pre, v_g_ffn1_post, v_w_ffn1_gate, v_w_ffn1_up, v_w_ffn1_down, v_g_mix_pre, v_g_mix_post, v_w_in, v_g_out_sb, v_g_out_ch, v_rel_bias, v_w_out, v_g_ffn2_pre, v_g_ffn2_post, v_w_ffn2_gate, v_w_ffn2_up, v_w_ffn2_down, v_w_ple_proj, v_w_ple_gate, v_g_ple_post):
    args = dict(locals())
    take = lambda a, n: a[0].T if n in TRANSPOSED else a[0]
    wts = {n: take(args[n], n) for n in ORDER}
    ms = {n: take(args["m_" + n], n) for n in ORDER}
    vs = {n: take(args["v_" + n], n) for n in ORDER}
    gains = {n: wts["g_" + n][None] for n in GAINS}

    c_idx = lax.axis_index("c").astype(jnp.int32).reshape(1)
    me_idx = (2 * lax.axis_index("x") + lax.axis_index("y")).astype(jnp.int32).reshape(1)
    south = lax.axis_index("c") == 0

    plain = lambda names: [n not in PERMUTED for n in names]
    lands = dict(zip(BIG, _cast_into_own_slot(me_idx, c_idx, [wts["w_" + n] for n in BIG], plain(BIG), "cast_weights")))

    class Overlapped:
        def __init__(self):
            self.started = {}
            self.flying = {}

        def start(self, group, collective_id, after):
            names = W_GROUPS[group]
            self.flying[group] = _gather_start([lands[n] for n in names], plain(names), "gather_%s_start" % group,
                                               collective_id, after)
            return self.flying[group][3]

        def weights(self, group, after=None):
            names = W_GROUPS.get(group)
            token = None
            if group == "first":
                zones = _gather_finish([lands[n] for n in names], plain(names), True, "gather_first")
                token = self.start("rest", 4, self.start("in", 1, self.start("down", 6, zones[0])))
            elif group == "passed":
                names, (send_sem, recv_sem, zones, _) = self.passing
                zones = _pass_wait(send_sem, recv_sem, zones, plain(names), after, "gather_rest_pass_wait")
            else:
                send_sem, recv_sem, zones, _ = self.flying[group]
                zones = _gather_wait(send_sem, recv_sem, zones, plain(names), after, "gather_%s_wait" % group)
                if group == "rest":
                    self.passing = names[1:], _pass_start(zones[1:], plain(names[1:]), "gather_rest_pass_start", 7)
                    names, zones, token = names[:1], zones[:1], self.passing[1][3]
                zones = _gather_finish(zones, plain(names), False, "gather_%s_finish" % group)
            return {n: _slabs(z) for n, z in zip(names, zones)}, token

        def grads_ready(self, group, gw):
            names = G_GROUPS[group]
            perm = [n in PERMUTED for n in names]
            halved = lambda g: g.reshape(N_CHIPS, 2, g.shape[1] // 2, g.shape[2])
            mine = [halved(gw[n][0]) for n in names]
            narrow = [halved(gw[n][1]) for n in names]
            if group == "late":
                token = self.scatter(group, names, perm, mine, _pair_swap(narrow, perm, "grad_pair_swap_late"))
                return self.join_start("early", token)
            self.swapping = names, perm, mine, _pair_swap_start(narrow, perm, "grad_pair_swap_start_early", 5)
            return self.swapping[3][3]

        def grads_sent(self, group, after):
            names, perm, mine, (send_sem, recv_sem, operands, _) = self.swapping
            operands = _pair_swap_wait(send_sem, recv_sem, operands, perm, after, "grad_pair_swap_wait_early")
            return self.scatter(group, names, perm, mine, operands[len(names):])

        def scatter(self, group, names, perm, mine, got):
            partial = _pair_add(c_idx, mine, got, perm, "grad_pair_add_" + group)
            send_sem, recv_sem, operands, token = _scatter_start(partial, perm, "grad_scatter_start_" + group,
                                                                 {"early": 2, "late": 3}[group])
            self.started[group] = names, perm, send_sem, recv_sem, operands, token
            return token

        def join_start(self, group, after):
            own = chip_sum(self.started[group], after, group)
            self.joining = _pair_join_start(own, "grad_pair_join_start_" + group, 8)
            return self.joining[3]

    def chip_sum(state, after, tag):
        names, perm, send_sem, recv_sem, operands, _ = state
        operands = _scatter_wait(send_sem, recv_sem, operands, perm, after, "grad_scatter_wait_" + tag)
        n = len(names)
        return _chip_add(me_idx, operands[:n], operands[n:], perm, "grad_chip_add_" + tag)

    def reduce_finish(group, after):
        if group == "early":
            send_sem, recv_sem, operands, _ = hooks.joining
            operands = _pair_join_wait(send_sem, recv_sem, operands, after, "grad_pair_join_wait_" + group)
            return operands[:len(operands) // 2], operands[len(operands) // 2:]
        own = chip_sum(hooks.started[group], after, group)
        return own, _pair_join(own, "grad_pair_join_" + group)

    hooks = Overlapped()
    loss, dx, gw, gg, g_rel = _local_step(x[0], p[0, 0], loss_target[0], gains, wts["rel_bias"], hooks)

    grads, delta, new_m, new_v = {}, {}, {}, {}

    def finish(group, after):
        own, other = reduce_finish(group, after)
        names = ["w_" + n for n in G_GROUPS[group]]
        g, d, m, v = _adamw_halves(c_idx, [wts[n] for n in names], own, other, [ms[n] for n in names],
                                   [vs[n] for n in names], "adamw_" + group)
        for n, gg_, dd, mm, vv in zip(names, g, d, m, v):
            grads[n], delta[n], new_m[n], new_v[n] = gg_, dd, mm, vv
        return d[0]

    early_done = finish("early", dx)

    pieces = [gg[n].reshape(-1, 128) for n in GAINS] + [jnp.pad(g_rel, ((0, 0), (0, N_REL_PAD - N_REL))).reshape(-1, 128)]
    summed = _all_sum_small(jnp.concatenate(pieces + [loss], axis=0), early_done, "small_grad_sum")
    finish("late", summed)
    at = 0
    for n, piece in zip(GAINS, pieces[:-1]):
        grads["g_" + n] = summed[at:at + piece.shape[0]].reshape(1, -1)[0]
        at += piece.shape[0]
    grads["rel_bias"] = summed[at:at + pieces[-1].shape[0]].reshape(N_HEADS, N_REL_PAD)[:, :N_REL]
    loss = summed[at + pieces[-1].shape[0], 0]

    small = ["g_" + n for n in GAINS] + ["rel_bias"]
    as_rows = lambda a: (a.reshape(-1, 128) if a.size % 128 == 0 else jnp.pad(a, ((0, 0), (0, N_REL_PAD - N_REL))).reshape(-1, 128))
    d, m, v = _adamw([as_rows(wts[n]) for n in small], [as_rows(grads[n]) for n in small],
                     [as_rows(ms[n]) for n in small], [as_rows(vs[n]) for n in small], 1, "adamw_small")
    for n, dd, mm, vv in zip(small, d, m, v):
        back = (lambda a: a.reshape(N_HEADS, N_REL_PAD)[:, :N_REL]) if n == "rel_bias" else (lambda a: a.reshape(-1))
        delta[n], new_m[n], new_v[n] = back(dd), back(mm), back(vv)

    outs = [loss, dx[None]]
    for table in (grads, delta, new_m, new_v):
        outs += [(table[n].T if n in TRANSPOSED else table[n])[None] for n in ORDER]
    return tuple(outs)
```

```python
import jax
import jax.numpy as jnp
from jax import lax
from jax.experimental import pallas as pl
from jax.experimental.pallas import tpu as pltpu

F32 = jnp.float32
BF16 = jnp.bfloat16
EPS = 1e-6
N_CHIPS = 4
HEAD_DIM = 64
N_HEADS = 8
CHUNK = 64
LOOKBACK = 8
BAND = (LOOKBACK + 1) * CHUNK
PAD = LOOKBACK * CHUNK
REL_CLIP = 128
N_REL = 2 * REL_CLIP + 1
N_REL_PAD = 384
SB_BLOCK = 256
PAIR = 2 * HEAD_DIM
SB_PAIRS = 2
SB_FWD_PAIRS = 4
ATT_SCALE = HEAD_DIM ** -0.5
NEG_INF = -1e30
ROW_BLOCK = 512
WIDE_ROW_BLOCK = 1024
VMEM_LIMIT_WIDE = 56 * 1024 * 1024
VMEM_LIMIT = 48 * 1024 * 1024
MESH = pl.DeviceIdType.MESH

ADAM_LR = 0.001
ADAM_B1 = 0.9
ADAM_B2 = 0.999
ADAM_EPS = 1e-08
ADAM_WD = 0.01
ADAM_STEP = 10

NT = (((1,), (1,)), ((), ()))
TN = (((0,), (0,)), ((), ()))


def _params(n_grid, vmem=None):
    return pltpu.CompilerParams(dimension_semantics=("arbitrary",) * n_grid, vmem_limit_bytes=vmem)


def _hbm(*arrays):
    return [pltpu.with_memory_space_constraint(a, pltpu.HBM) for a in arrays]


def _out(shape, dtype):
    return pltpu.HBM(shape, dtype)


def _dot(a, b, dims=None):
    if dims is None:
        return jnp.dot(a, b, preferred_element_type=F32)
    return lax.dot_general(a, b, dims, preferred_element_type=F32)


def _sigmoid(x):
    return 1.0 / (1.0 + jnp.exp(-x))


def _rms_fwd(x, g):
    r = lax.rsqrt(jnp.mean(x * x, axis=-1, keepdims=True) + EPS)
    return x * r * g


def _rms_bwd(x, g, dy):
    r = lax.rsqrt(jnp.mean(x * x, axis=-1, keepdims=True) + EPS)
    xh = x * r
    dg = jnp.sum(dy * xh, axis=0, keepdims=True)
    t = dy * g
    dx = r * (t - xh * jnp.mean(t * xh, axis=-1, keepdims=True))
    return dx, dg


def _accumulate(ref, val, first):
    @pl.when(first)
    def _():
        ref[...] = val

    @pl.when(jnp.logical_not(first))
    def _():
        ref[...] += val


def _split2(x):
    hi = x.astype(BF16)
    lo = (x - hi.astype(F32)).astype(BF16)
    return hi, lo


def _ffn_fwd(x, g_pre, g_post, wg, wu, wd, name):
    T, D = x.shape
    S, FS, _ = wg.shape
    tm = min(WIDE_ROW_BLOCK, T)

    def body(x_ref, gpre_ref, gpost_ref, wg_ref, wu_ref, wd_ref,
             h_ref, xn_ref, g_ref, u_ref, a_ref, f_ref):
        k = pl.program_id(1)

        @pl.when(k == 0)
        def _():
            xn_ref[...] = _rms_fwd(x_ref[...], gpre_ref[...]).astype(BF16)

        xn = xn_ref[...]
        g = _dot(xn, wg_ref[0], NT)
        u = _dot(xn, wu_ref[0], NT)
        g_ref[0] = g
        u_ref[0] = u
        a = (g * _sigmoid(g) * u).astype(BF16)
        a_ref[0] = a
        _accumulate(f_ref, _dot(a, wd_ref[0]), k == 0)

        @pl.when(k == S - 1)
        def _():
            h_ref[...] = x_ref[...] + 0.5 * _rms_fwd(f_ref[...], gpost_ref[...])

    row = pl.BlockSpec((tm, D), lambda i, k: (i, 0))
    vec = pl.BlockSpec((1, D), lambda i, k: (0, 0))
    act = pl.BlockSpec((1, tm, FS), lambda i, k: (k, i, 0))
    return pl.pallas_call(
        body, name=name, grid=(T // tm, S),
        in_specs=[row, vec, vec] + [pl.BlockSpec((1, FS, D), lambda i, k: (k, 0, 0))] * 3,
        out_specs=[row, row, act, act, act, row],
        out_shape=[_out((T, D), F32), _out((T, D), BF16),
                   _out((S, T, FS), F32), _out((S, T, FS), F32),
                   _out((S, T, FS), BF16), _out((T, D), F32)],
        compiler_params=_params(2, VMEM_LIMIT_WIDE),
    )(*_hbm(x, g_pre, g_post, wg, wu, wd))


def _ffn_up(x, g_pre, wg, wu, name):
    T, D = x.shape
    S, FS, _ = wg.shape
    tm = min(WIDE_ROW_BLOCK, T)

    def body(x_ref, gpre_ref, wg_ref, wu_ref, xn_ref, g_ref, u_ref, a_ref):
        @pl.when(pl.program_id(1) == 0)
        def _():
            xn_ref[...] = _rms_fwd(x_ref[...], gpre_ref[...]).astype(BF16)

        xn = xn_ref[...]
        g = _dot(xn, wg_ref[0], NT)
        u = _dot(xn, wu_ref[0], NT)
        g_ref[0] = g
        u_ref[0] = u
        a_ref[0] = (g * _sigmoid(g) * u).astype(BF16)

    row = pl.BlockSpec((tm, D), lambda i, k: (i, 0))
    act = pl.BlockSpec((1, tm, FS), lambda i, k: (k, i, 0))
    return pl.pallas_call(
        body, name=name, grid=(T // tm, S),
        in_specs=[row, pl.BlockSpec((1, D), lambda i, k: (0, 0))] + [pl.BlockSpec((1, FS, D), lambda i, k: (k, 0, 0))] * 2,
        out_specs=[row, act, act, act],
        out_shape=[_out((T, D), BF16), _out((S, T, FS), F32), _out((S, T, FS), F32), _out((S, T, FS), BF16)],
        compiler_params=_params(2, VMEM_LIMIT_WIDE),
    )(*_hbm(x, g_pre, wg, wu))


def _ffn_down(x, a, g_post, wd, name):
    T, D = x.shape
    S, FS, _ = wd.shape
    tm = min(WIDE_ROW_BLOCK, T)

    def body(x_ref, a_ref, gpost_ref, wd_ref, h_ref, f_ref):
        k = pl.program_id(1)
        _accumulate(f_ref, _dot(a_ref[0], wd_ref[0]), k == 0)

        @pl.when(k == S - 1)
        def _():
            h_ref[...] = x_ref[...] + 0.5 * _rms_fwd(f_ref[...], gpost_ref[...])

    row = pl.BlockSpec((tm, D), lambda i, k: (i, 0))
    return pl.pallas_call(
        body, name=name, grid=(T // tm, S),
        in_specs=[row, pl.BlockSpec((1, tm, FS), lambda i, k: (k, i, 0)), pl.BlockSpec((1, D), lambda i, k: (0, 0)),
                  pl.BlockSpec((1, FS, D), lambda i, k: (k, 0, 0))],
        out_specs=[row, row],
        out_shape=[_out((T, D), F32), _out((T, D), F32)],
        compiler_params=_params(2, VMEM_LIMIT_WIDE),
    )(*_hbm(x, a, g_post, wd))


def _ffn_bwd_act(dh, f, g_post, wd, g_act, u_act, name):
    T, D = dh.shape
    S, FS, _ = wd.shape
    tm = min(WIDE_ROW_BLOCK, T)

    def body(dh_ref, f_ref, gpost_ref, wd_ref, g_ref, u_ref, dgp_ref, dup_ref, df_ref, dgain_ref, df_s):
        i, k = pl.program_id(0), pl.program_id(1)

        @pl.when(k == 0)
        def _():
            df, dgain = _rms_bwd(f_ref[...], gpost_ref[...], 0.5 * dh_ref[...])
            df_s[...] = df.astype(BF16)
            df_ref[...] = df_s[...]
            _accumulate(dgain_ref, dgain, i == 0)

        da = _dot(df_s[...], wd_ref[0], NT)
        g = g_ref[0]
        s = _sigmoid(g)
        dup_ref[0] = (da * (g * s)).astype(BF16)
        dgp_ref[0] = (da * u_ref[0] * (s * (1.0 + g * (1.0 - s)))).astype(BF16)

    row = pl.BlockSpec((tm, D), lambda i, k: (i, 0))
    vec = pl.BlockSpec((1, D), lambda i, k: (0, 0))
    act = pl.BlockSpec((1, tm, FS), lambda i, k: (k, i, 0))
    return pl.pallas_call(
        body, name=name, grid=(T // tm, S),
        in_specs=[row, row, vec, pl.BlockSpec((1, FS, D), lambda i, k: (k, 0, 0)), act, act],
        out_specs=[act, act, row, vec],
        out_shape=[_out((S, T, FS), BF16), _out((S, T, FS), BF16),
                   _out((T, D), BF16), _out((1, D), F32)],
        scratch_shapes=[pltpu.VMEM((tm, D), BF16)],
        compiler_params=_params(2, VMEM_LIMIT_WIDE),
    )(*_hbm(dh, f, g_post, wd, g_act, u_act))


def _proj_bwd(dys, ws, x, g_pre, dh, name):
    T, D = x.shape
    n = len(dys)
    S, N, _ = ws[0].shape
    tm = min(WIDE_ROW_BLOCK, T)

    def body(*refs):
        dy_refs, w_refs = refs[:n], refs[n:2 * n]
        x_ref, gpre_ref, dh_ref, dx_ref, dgain_ref, acc_s = refs[2 * n:]
        i, k = pl.program_id(0), pl.program_id(1)
        part = None
        for dy_ref, w_ref in zip(dy_refs, w_refs):
            term = _dot(dy_ref[0], w_ref[0])
            part = term if part is None else part + term
        _accumulate(acc_s, part, k == 0)

        @pl.when(k == S - 1)
        def _():
            dx, dgain = _rms_bwd(x_ref[...], gpre_ref[...], acc_s[...])
            dx_ref[...] = dh_ref[...] + dx
            _accumulate(dgain_ref, dgain, i == 0)

    row = pl.BlockSpec((tm, D), lambda i, k: (i, 0))
    vec = pl.BlockSpec((1, D), lambda i, k: (0, 0))
    return pl.pallas_call(
        body, name=name, grid=(T // tm, S),
        in_specs=[pl.BlockSpec((1, tm, N), lambda i, k: (k, i, 0))] * n
        + [pl.BlockSpec((1, N, D), lambda i, k: (k, 0, 0))] * n + [row, vec, row],
        out_specs=[row, vec],
        out_shape=[_out((T, D), F32), _out((1, D), F32)],
        scratch_shapes=[pltpu.VMEM((tm, D), F32)],
        compiler_params=_params(2, VMEM_LIMIT_WIDE),
    )(*_hbm(*dys, *ws, x, g_pre, dh))


def _mm_tn(a, b, bm, name):
    ga, T, M = a.shape
    gb, _, N = b.shape
    b_spec = pl.BlockSpec((1, T, N), (lambda g, m: (g, 0, 0)) if gb > 1 else (lambda g, m: (0, 0, 0)))
    G = max(ga, gb)

    def body(a_ref, b_ref, o_ref, narrow_ref):
        o_ref[0] = _dot(a_ref[0].astype(BF16), b_ref[0].astype(BF16), TN)
        narrow_ref[0] = o_ref[0].astype(BF16)

    out = pl.BlockSpec((1, bm, N), lambda g, m: (g, m, 0))
    return pl.pallas_call(
        body, name=name, grid=(G, M // bm),
        in_specs=[pl.BlockSpec((1, T, bm), (lambda g, m: (g, 0, m)) if ga > 1 else (lambda g, m: (0, 0, m))), b_spec],
        out_specs=[out, out],
        out_shape=[_out((G, M, N), F32), _out((G, M, N), BF16)],
        compiler_params=_params(2, VMEM_LIMIT),
    )(*_hbm(a, b))


QKV_PIECE = 256


def _qkv_shard(dy_refs, k, n_col):
    width = dy_refs[0].shape[1]
    parts = []
    for col in range(k * n_col, (k + 1) * n_col, QKV_PIECE):
        parts.append(dy_refs[col // width][:, col % width:col % width + QKV_PIECE])
    return jnp.concatenate(parts, axis=1)


def _qkv_bwd_in(dys, w, x, g_pre, dh, name):
    T, D = x.shape
    n = len(dys)
    S, _, N = w.shape
    tm = min(WIDE_ROW_BLOCK, T)

    def body(*refs):
        dy_refs = refs[:n]
        w_ref, x_ref, gpre_ref, dh_ref, dx_ref, dgain_ref, acc_s = refs[n:]
        i, k = pl.program_id(0), pl.program_id(1)
        for shard in range(S):
            @pl.when(k == shard)
            def _(shard=shard):
                part = _dot(_qkv_shard(dy_refs, shard, N), w_ref[0], NT)
                if shard == 0:
                    acc_s[...] = part
                else:
                    acc_s[...] += part

        @pl.when(k == S - 1)
        def _():
            dx, dgain = _rms_bwd(x_ref[...], gpre_ref[...], acc_s[...])
            dx_ref[...] = dh_ref[...] + dx
            _accumulate(dgain_ref, dgain, i == 0)

    row = pl.BlockSpec((tm, D), lambda i, k: (i, 0))
    vec = pl.BlockSpec((1, D), lambda i, k: (0, 0))
    return pl.pallas_call(
        body, name=name, grid=(T // tm, S),
        in_specs=[pl.BlockSpec((tm, dy.shape[1]), lambda i, k: (i, 0)) for dy in dys]
        + [pl.BlockSpec((1, D, N), lambda i, k: (k, 0, 0)), row, vec, row],
        out_specs=[row, vec],
        out_shape=[_out((T, D), F32), _out((1, D), F32)],
        scratch_shapes=[pltpu.VMEM((tm, D), F32)],
        compiler_params=_params(2, VMEM_LIMIT_WIDE),
    )(*_hbm(*dys, w, x, g_pre, dh))


def _dw_in(a, dys, n_col, bm, name):
    T, M = a.shape
    n = len(dys)
    S = n * dys[0].shape[1] // n_col

    def body(*refs):
        a_ref, dy_refs = refs[0], refs[1:1 + n]
        o_ref, narrow_ref = refs[1 + n:]
        k = pl.program_id(1)
        for shard in range(S):
            @pl.when(k == shard)
            def _(shard=shard):
                o_ref[0] = _dot(a_ref[...], _qkv_shard(dy_refs, shard, n_col), TN)
                narrow_ref[0] = o_ref[0].astype(BF16)

    out = pl.BlockSpec((1, bm, n_col), lambda m, k: (k, m, 0))
    return pl.pallas_call(
        body, name=name, grid=(M // bm, S),
        in_specs=[pl.BlockSpec((T, bm), lambda m, k: (0, m))]
        + [pl.BlockSpec((T, dy.shape[1]), lambda m, k: (0, 0)) for dy in dys],
        out_specs=[out, out],
        out_shape=[_out((S, M, n_col), F32), _out((S, M, n_col), BF16)],
        compiler_params=_params(2, VMEM_LIMIT_WIDE),
    )(*_hbm(a, *dys))


def _norm_proj(x, g_pre, w, name):
    T, D = x.shape
    S, _, N = w.shape
    tm = min(WIDE_ROW_BLOCK, T)

    def body(x_ref, g_ref, w_ref, o_ref, xn_ref, xn_s):
        @pl.when(pl.program_id(1) == 0)
        def _():
            xn_s[...] = _rms_fwd(x_ref[...], g_ref[...]).astype(BF16)
            xn_ref[...] = xn_s[...]

        o_ref[...] = _dot(xn_s[...], w_ref[0]).astype(BF16)

    row = pl.BlockSpec((tm, D), lambda i, k: (i, 0))
    return pl.pallas_call(
        body, name=name, grid=(T // tm, S),
        in_specs=[row, pl.BlockSpec((1, D), lambda i, k: (0, 0)), pl.BlockSpec((1, D, N), lambda i, k: (k, 0, 0))],
        out_specs=[pl.BlockSpec((tm, N), lambda i, k: (i, k)), row],
        out_shape=[_out((T, S * N), BF16), _out((T, D), BF16)],
        scratch_shapes=[pltpu.VMEM((tm, D), BF16)],
        compiler_params=_params(2, VMEM_LIMIT_WIDE),
    )(*_hbm(x, g_pre, w))


def _mix_out_fwd(h, o_a, o_b, g_sb, g_ch, w_out, g_post, name):
    T, D = h.shape
    W = g_sb.shape[1]
    tm = min(WIDE_ROW_BLOCK, T)

    def body(h_ref, oa_ref, ob_ref, gsb_ref, gch_ref, w_ref, gpost_ref, h2_ref, mixed_ref, mo_ref):
        mixed_ref[:, :W] = _rms_fwd(oa_ref[...], gsb_ref[...]).astype(BF16)
        mixed_ref[:, W:] = _rms_fwd(ob_ref[...], gch_ref[...]).astype(BF16)
        mo = _dot(mixed_ref[...], w_ref[...])
        mo_ref[...] = mo
        h2_ref[...] = h_ref[...] + _rms_fwd(mo, gpost_ref[...])

    row = pl.BlockSpec((tm, D), lambda i: (i, 0))
    part = pl.BlockSpec((tm, W), lambda i: (i, 0))
    half = pl.BlockSpec((1, W), lambda i: (0, 0))
    return pl.pallas_call(
        body, name=name, grid=(T // tm,),
        in_specs=[row, part, part, half, half, pl.BlockSpec((D, D), lambda i: (0, 0)), pl.BlockSpec((1, D), lambda i: (0, 0))],
        out_specs=[row, row, row],
        out_shape=[_out((T, D), F32), _out((T, D), BF16),
                   _out((T, D), F32)],
        compiler_params=_params(1, VMEM_LIMIT_WIDE),
    )(*_hbm(h, o_a, o_b, g_sb, g_ch, w_out, g_post))


def _mix_out_bwd(dh, mo, g_post, w_out, o_a, o_b, g_sb, g_ch, name):
    T, D = dh.shape
    W = g_sb.shape[1]
    tm = min(WIDE_ROW_BLOCK, T)

    def body(dh_ref, mo_ref, gpost_ref, w_ref, oa_ref, ob_ref, gsb_ref, gch_ref,
             dmo_ref, doa_ref, dob_ref, dgpost_ref, dgsb_ref, dgch_ref):
        first = pl.program_id(0) == 0
        dmo, dgpost = _rms_bwd(mo_ref[...], gpost_ref[...], dh_ref[...])
        dmo_ref[...] = dmo.astype(BF16)
        dmix = _dot(dmo_ref[...], w_ref[...], NT)
        doa_ref[...], dgsb = _rms_bwd(oa_ref[...], gsb_ref[...], dmix[:, :W])
        dob_ref[...], dgch = _rms_bwd(ob_ref[...], gch_ref[...], dmix[:, W:])
        _accumulate(dgpost_ref, dgpost, first)
        _accumulate(dgsb_ref, dgsb, first)
        _accumulate(dgch_ref, dgch, first)

    row = pl.BlockSpec((tm, D), lambda i: (i, 0))
    part = pl.BlockSpec((tm, W), lambda i: (i, 0))
    vec = pl.BlockSpec((1, D), lambda i: (0, 0))
    half = pl.BlockSpec((1, W), lambda i: (0, 0))
    return pl.pallas_call(
        body, name=name, grid=(T // tm,),
        in_specs=[row, row, vec, pl.BlockSpec((D, D), lambda i: (0, 0)), part, part, half, half],
        out_specs=[row, part, part, vec, half, half],
        out_shape=[_out((T, D), BF16), _out((T, W), F32),
                   _out((T, W), F32), _out((1, D), F32),
                   _out((1, W), F32), _out((1, W), F32)],
        compiler_params=_params(1, VMEM_LIMIT_WIDE),
    )(*_hbm(dh, mo, g_post, w_out, o_a, o_b, g_sb, g_ch))


def _ple_loss(h, p, target, w_proj, w_gate, g_post, name):
    T, D = h.shape
    P = p.shape[1]
    S = N_CHIPS
    C = D // S
    tm = min(ROW_BLOCK, T)

    def body(h_ref, p_ref, t_ref, wp_ref, wg_ref, g_ref, loss_ref, dh_ref, dproj_ref, dgate_ref, dgain_ref):
        first = pl.program_id(0) == 0
        h3 = h_ref[...]
        proj = _dot(p_ref[...].astype(BF16), wp_ref[...])
        s = _sigmoid(_dot(h3.astype(BF16), wg_ref[...]))
        e = proj * s
        diff = h3 + _rms_fwd(e, g_ref[...]) - t_ref[...]
        part = 0.5 * jnp.sum(jnp.mean(diff * diff, axis=-1, keepdims=True), axis=0, keepdims=True)
        _accumulate(loss_ref, jnp.broadcast_to(part, loss_ref.shape), first)
        dy = diff * (1.0 / D)
        de, dgain = _rms_bwd(e, g_ref[...], dy)
        _accumulate(dgain_ref, dgain, first)
        dproj = (de * s).astype(BF16)
        for j in range(S):
            dproj_ref[j] = dproj[:, j * C:(j + 1) * C]
        dgate_ref[...] = (de * proj * s * (1.0 - s)).astype(BF16)
        dh_ref[...] = dy + _dot(dgate_ref[...], wg_ref[...], NT)

    row = pl.BlockSpec((tm, D), lambda i: (i, 0))
    vec = pl.BlockSpec((1, D), lambda i: (0, 0))
    return pl.pallas_call(
        body, name=name, grid=(T // tm,),
        in_specs=[row, pl.BlockSpec((tm, P), lambda i: (i, 0)), row,
                  pl.BlockSpec((P, D), lambda i: (0, 0)), pl.BlockSpec((D, D), lambda i: (0, 0)), vec],
        out_specs=[pl.BlockSpec((8, 128), lambda i: (0, 0)), row,
                   pl.BlockSpec((S, tm, C), lambda i: (0, i, 0)), row, vec],
        out_shape=[_out((8, 128), F32), _out((T, D), F32),
                   _out((S, T, C), BF16), _out((T, D), BF16),
                   _out((1, D), F32)],
        compiler_params=_params(1, VMEM_LIMIT_WIDE),
    )(*_hbm(h, p, target, w_proj, w_gate, g_post))


def _sb_scores(q, kj, mask):
    z = _dot(q, kj, NT)
    sp = jnp.maximum(z, 0.0) + jnp.log(1.0 + jnp.exp(-jnp.abs(z)))
    return z, sp if mask is None else jnp.where(mask, sp, 0.0)


def _strict_causal():
    rows = lax.broadcasted_iota(jnp.int32, (SB_BLOCK, SB_BLOCK), 0)
    cols = lax.broadcasted_iota(jnp.int32, (SB_BLOCK, SB_BLOCK), 1)
    return cols < rows


def _tri(cmp):
    r = lax.broadcasted_iota(jnp.int32, (2 * SB_BLOCK, SB_BLOCK), 0) % SB_BLOCK
    c = lax.broadcasted_iota(jnp.int32, (2 * SB_BLOCK, SB_BLOCK), 1)
    return jnp.where(cmp(r, c), 1.0, 0.0).astype(BF16)


def _cum(x, tri):
    return _dot(jnp.concatenate(_split2(x), axis=1), tri)


def _pair_lanes():
    lane = lax.broadcasted_iota(jnp.int32, (1, PAIR), 1)
    return [lane < HEAD_DIM, lane >= HEAD_DIM]


def _only(lanes, x):
    return jnp.where(lanes, x, jnp.zeros_like(x))


def _sb_fwd(qkv, name):
    T = qkv.shape[0]
    B = SB_BLOCK
    W = SB_FWD_PAIRS * PAIR
    steps = N_HEADS // (2 * SB_FWD_PAIRS)
    heads = [(p, h) for p in range(SB_FWD_PAIRS) for h in range(2)]

    def body(q_ref, k_ref, v_ref, o_ref):
        i = pl.program_id(1)
        after = _tri(lambda r, c: r > c)
        lanes = _pair_lanes()
        cols = [slice(p * PAIR, (p + 1) * PAIR) for p in range(SB_FWD_PAIRS)]
        q = {(p, h): _only(lanes[h], q_ref[:, cols[p]] * ATT_SCALE) for p, h in heads}

        def tiles(j, carries, mask):
            at = pl.ds(pl.multiple_of(j * B, B), B)
            scores = [_sb_scores(q[ph], k_ref[at, cols[ph[0]]], mask) for ph in heads]
            laters = [_cum(sp, after) for _, sp in scores]
            out = []
            for ph, (z, sp), later, (run, acc) in zip(heads, scores, laters, carries):
                a = jnp.exp(z - sp - later - run)
                if mask is not None:
                    a = jnp.where(mask, a, 0.0)
                out.append((run + later[:, 0:1] + sp[:, 0:1],
                            acc + _dot(a.astype(BF16), _only(lanes[ph[1]], v_ref[at, cols[ph[0]]]))))
            return tuple(out)

        zero = (jnp.zeros((B, 1), F32), jnp.zeros((B, PAIR), F32))
        carries = tiles(i, (zero,) * len(heads), _strict_causal())
        carries = lax.fori_loop(0, i, lambda jj, cs: tiles(i - 1 - jj, cs, None), carries)
        for p in range(SB_FWD_PAIRS):
            o_ref[:, cols[p]] = carries[2 * p][1] + carries[2 * p + 1][1]

    blk = lambda off: pl.BlockSpec((B, W), lambda g, i: (i, g + off))
    full = lambda off: pl.BlockSpec((T, W), lambda g, i: (0, g + off))
    return pl.pallas_call(
        body, name=name, grid=(steps, T // B),
        in_specs=[blk(0), full(steps), full(2 * steps)],
        out_specs=blk(0),
        out_shape=_out((T, N_HEADS * HEAD_DIM), F32),
        compiler_params=_params(2, VMEM_LIMIT),
    )(*_hbm(qkv, qkv, qkv))


def _sb_bwd(qkv, do, o, after, name):
    T = qkv.shape[0]
    B = SB_BLOCK
    W = SB_PAIRS * PAIR
    steps = N_HEADS // (2 * SB_PAIRS)
    n_blocks = T // B
    heads = [(p, h) for p in range(SB_PAIRS) for h in range(2)]

    def body(q_ref, k_ref, v_ref, do_ref, o_ref, dq_ref, dk_ref, dv_ref, dk_s, dv_s):
        i = pl.program_id(1)

        @pl.when(i == 0)
        def _():
            dk_s[...] = jnp.zeros_like(dk_s)
            dv_s[...] = jnp.zeros_like(dv_s)

        after = _tri(lambda r, c: r > c)
        since = _tri(lambda r, c: r >= c)
        lanes = _pair_lanes()
        cols = [slice(p * PAIR, (p + 1) * PAIR) for p in range(SB_PAIRS)]
        q = {(p, h): _only(lanes[h], q_ref[:, cols[p]] * ATT_SCALE) for p, h in heads}
        do = {(p, h): _only(lanes[h], do_ref[:, cols[p]].astype(BF16)) for p, h in heads}
        total = {ph: jnp.sum(do[ph].astype(F32) * o_ref[:, cols[ph[0]]], axis=1, keepdims=True) for ph in heads}

        def tiles(j, carries, mask):
            at = pl.ds(pl.multiple_of(j * B, B), B)
            ks = [k_ref[at, c] for c in cols]
            vs = [v_ref[at, c] for c in cols]
            scores = [_sb_scores(q[ph], ks[ph[0]], mask) for ph in heads]
            laters = [_cum(sp, after) for _, sp in scores]
            das = [_dot(do[ph], vs[ph[0]], NT) for ph in heads]
            a_s, gs = [], []
            for (z, sp), later, da, carry in zip(scores, laters, das, carries):
                a = jnp.exp(z - sp - later - carry[0])
                if mask is not None:
                    a = jnp.where(mask, a, 0.0)
                a = a.astype(BF16)
                a_s.append(a)
                gs.append(a.astype(F32) * da)
            sinces = [_cum(g, since) for g in gs]
            dzs = []
            for ph, (_, sp), g, from_s, carry in zip(heads, scores, gs, sinces, carries):
                g_before = total[ph] - carry[1] - from_s
                fail = jnp.exp(-sp)
                dz = fail * (g + g_before) - g_before
                if mask is not None:
                    dz = jnp.where(mask, dz, 0.0)
                dzs.append(dz.astype(BF16))
            out = []
            for ph, (_, sp), a, dz, later, from_s, carry in zip(heads, scores, a_s, dzs, laters, sinces, carries):
                dk_s[at, cols[ph[0]]] += _dot(dz, q[ph], TN)
                dv_s[at, cols[ph[0]]] += _dot(a, do[ph], TN)
                out.append((carry[0] + later[:, 0:1] + sp[:, 0:1], carry[1] + from_s[:, 0:1],
                            carry[2] + _dot(dz, _only(lanes[ph[1]], ks[ph[0]]))))
            return tuple(out)

        col = jnp.zeros((B, 1), F32)
        zero = (col, col, jnp.zeros((B, PAIR), F32))
        carries = tiles(i, (zero,) * len(heads), _strict_causal())
        last = lax.fori_loop(0, i, lambda jj, cs: tiles(i - 1 - jj, cs, None), carries)
        for p in range(SB_PAIRS):
            dq_ref[:, cols[p]] = ((last[2 * p][2] + last[2 * p + 1][2]) * ATT_SCALE).astype(BF16)

        @pl.when(i == n_blocks - 1)
        def _():
            dk_ref[...] = dk_s[...].astype(BF16)
            dv_ref[...] = dv_s[...].astype(BF16)

    blk = lambda off: pl.BlockSpec((B, W), lambda g, i: (i, g + off))
    full = lambda off: pl.BlockSpec((T, W), lambda g, i: (0, g + off))
    out = _out((T, N_HEADS * HEAD_DIM), BF16)
    return pl.pallas_call(
        lambda after_ref, *refs: body(*refs), name=name, grid=(steps, n_blocks),
        in_specs=[ANY, blk(0), full(steps), full(2 * steps), blk(0), blk(0)],
        out_specs=[blk(0), full(0), full(0)],
        out_shape=[out, out, out],
        scratch_shapes=[pltpu.VMEM((T, W), F32)] * 2,
        compiler_params=_params(2, VMEM_LIMIT),
    )(after, *_hbm(qkv, qkv, qkv, do, o))


NEAR = BAND - PAD + REL_CLIP
FAR = BAND - NEAR
NEAR_REL = 2 * REL_CLIP
BIAS_ROWS = 8


def _rel_onehot(i, transposed):
    shape = (NEAR, NEAR_REL) if transposed else (NEAR_REL, NEAR)
    j = FAR + lax.broadcasted_iota(jnp.int32, shape, 0 if transposed else 1)
    r = lax.broadcasted_iota(jnp.int32, shape, 1 if transposed else 0)
    idx = jnp.clip(i + PAD - j, -REL_CLIP, REL_CLIP) + REL_CLIP
    return jnp.where(idx - 1 == r, 1.0, 0.0).astype(BF16)


def _bias_table(rel_bias, name):
    def body(near_ref, far_ref, o_ref):
        rb = near_ref[...]
        hi, lo = _split2(rb)
        lo2 = (rb - hi.astype(F32) - lo.astype(F32)).astype(BF16)
        far = jnp.broadcast_to(far_ref[...], (N_HEADS, FAR))
        for k in range(BIAS_ROWS):
            onehot = _rel_onehot(pl.program_id(0) * BIAS_ROWS + k, False)
            o_ref[k, :, :FAR] = far
            o_ref[k, :, FAR:] = _dot(hi, onehot) + _dot(lo, onehot) + _dot(lo2, onehot)

    return pl.pallas_call(
        body, name=name, grid=(CHUNK // BIAS_ROWS,),
        in_specs=[pl.BlockSpec((N_HEADS, NEAR_REL), lambda i: (0, 0)), pl.BlockSpec((N_HEADS, 1), lambda i: (0, 0))],
        out_specs=pl.BlockSpec((BIAS_ROWS, N_HEADS, BAND), lambda i: (i, 0, 0)),
        out_shape=_out((CHUNK, N_HEADS, BAND), F32),
        compiler_params=_params(1),
    )(*_hbm(rel_bias[:, 1:], rel_bias[:, N_REL - 1:]))


def _bias_grad(dbias_t, name):
    def body(d_ref, near_ref, far_ref):
        near, far = None, None
        for k in range(BIAS_ROWS):
            onehot = _rel_onehot(pl.program_id(0) * BIAS_ROWS + k, True)
            hi, lo = _split2(d_ref[k, :, FAR:])
            part = _dot(hi, onehot) + _dot(lo, onehot)
            rest = jnp.sum(d_ref[k, :, :FAR], axis=1, keepdims=True)
            near, far = (part, rest) if near is None else (near + part, far + rest)
        first = pl.program_id(0) == 0
        _accumulate(near_ref, near, first)
        _accumulate(far_ref, jnp.broadcast_to(far, far_ref.shape), first)

    near, far = pl.pallas_call(
        body, name=name, grid=(CHUNK // BIAS_ROWS,),
        in_specs=[pl.BlockSpec((BIAS_ROWS, N_HEADS, BAND), lambda i: (i, 0, 0))],
        out_specs=[pl.BlockSpec((N_HEADS, NEAR_REL), lambda i: (0, 0)), pl.BlockSpec((N_HEADS, 128), lambda i: (0, 0))],
        out_shape=[_out((N_HEADS, NEAR_REL), F32), _out((N_HEADS, 128), F32)],
        compiler_params=_params(1),
    )(*_hbm(dbias_t))
    return jnp.pad(near, ((0, 0), (1, 0))).at[:, N_REL - 1].add(far[:, 0])


def _ch_probs(scores, bias, valid):
    z = jnp.where(valid, scores * ATT_SCALE + bias, NEG_INF)
    e = jnp.exp(z - jnp.max(z, axis=-1, keepdims=True))
    return e / jnp.sum(e, axis=-1, keepdims=True)


CH_HEADS = [(pair, h) for pair in range(N_HEADS // 2) for h in range(2)]
CH_COLS = [slice(pair * PAIR, (pair + 1) * PAIR) for pair in range(N_HEADS // 2)]


CH_GROUP = 2
CH_Q = CH_GROUP * CHUNK
CH_WIN = (LOOKBACK + CH_GROUP) * CHUNK


def _ch_valid(n):
    row_chunk = lax.broadcasted_iota(jnp.int32, (CH_Q, CH_WIN), 0) // CHUNK
    slot = lax.broadcasted_iota(jnp.int32, (CH_Q, CH_WIN), 1)
    ahead = slot // CHUNK - row_chunk
    return (ahead >= 0) & (ahead <= LOOKBACK) & (n * CH_Q + slot >= PAD)


def _ch_group_bias(bias):
    shifted = [jnp.pad(bias, ((0, 0), (0, 0), (c * CHUNK, (CH_GROUP - 1 - c) * CHUNK))) for c in range(CH_GROUP)]
    return jnp.concatenate(shifted, axis=1)


def _ch_fold_bias_grad(dbias):
    parts = [dbias[:, c * CHUNK:(c + 1) * CHUNK, c * CHUNK:c * CHUNK + BAND] for c in range(CH_GROUP)]
    return sum(parts[1:], parts[0])


def _ch_fwd(qkv, bias, name):
    T = qkv.shape[0]
    W = N_HEADS * HEAD_DIM

    def body(q_ref, k_ref, v_ref, b_ref, o_ref, kp, vp):
        n = pl.program_id(0)

        @pl.when(n == 0)
        def _():
            _ch_load_padded(k_ref, v_ref, kp, vp)

        win = pl.ds(pl.multiple_of(n * CH_Q, CH_Q), CH_WIN)
        valid = _ch_valid(n)
        lanes = _pair_lanes()
        scores = [_dot(_only(lanes[h], q_ref[:, CH_COLS[pair]]), kp[win, CH_COLS[pair]], NT) for pair, h in CH_HEADS]
        probs = [_ch_probs(s, b_ref[2 * pair + h], valid).astype(BF16) for s, (pair, h) in zip(scores, CH_HEADS)]
        outs = [_dot(p, _only(lanes[h], vp[win, CH_COLS[pair]])) for p, (pair, h) in zip(probs, CH_HEADS)]
        for pair, cols in enumerate(CH_COLS):
            o_ref[:, cols] = outs[2 * pair] + outs[2 * pair + 1]

    full = lambda col: pl.BlockSpec((T, W), lambda n: (0, col))
    return pl.pallas_call(
        body, name=name, grid=(T // CH_Q,),
        in_specs=[pl.BlockSpec((CH_Q, W), lambda n: (n, 3)), full(4), full(5),
                  pl.BlockSpec((N_HEADS, CH_Q, CH_WIN), lambda n: (0, 0, 0))],
        out_specs=pl.BlockSpec((CH_Q, W), lambda n: (n, 0)),
        out_shape=_out((T, W), F32),
        scratch_shapes=[pltpu.VMEM((PAD + T, W), BF16)] * 2,
        compiler_params=_params(1, VMEM_LIMIT),
    )(*_hbm(qkv, qkv, qkv, bias))


def _ch_load_padded(k_ref, v_ref, kp, vp):
    for src, dst in ((k_ref, kp), (v_ref, vp)):
        dst[:PAD, :] = jnp.zeros((PAD, dst.shape[1]), dst.dtype)
        dst[PAD:, :] = src[...]


def _ch_bwd(qkv, bias, do, after, name):
    T = qkv.shape[0]
    W = N_HEADS * HEAD_DIM
    n_chunks = T // CH_Q

    def body(q_ref, k_ref, v_ref, b_ref, do_ref, dq_ref, dk_ref, dv_ref, db_ref, kp, vp, dk_s, dv_s):
        n = pl.program_id(0)

        @pl.when(n == 0)
        def _():
            _ch_load_padded(k_ref, v_ref, kp, vp)
            dk_s[...] = jnp.zeros_like(dk_s)
            dv_s[...] = jnp.zeros_like(dv_s)
            db_ref[...] = jnp.zeros_like(db_ref)

        win = pl.ds(pl.multiple_of(n * CH_Q, CH_Q), CH_WIN)
        valid = _ch_valid(n)
        lanes = _pair_lanes()
        kws = [kp[win, cols] for cols in CH_COLS]
        vws = [vp[win, cols] for cols in CH_COLS]
        qs = [_only(lanes[h], q_ref[:, CH_COLS[pair]]) for pair, h in CH_HEADS]
        dos = [_only(lanes[h], do_ref[:, CH_COLS[pair]].astype(BF16)) for pair, h in CH_HEADS]
        scores = [_dot(q, kws[pair], NT) for q, (pair, _) in zip(qs, CH_HEADS)]
        dps = [_dot(do, vws[pair], NT) for do, (pair, _) in zip(dos, CH_HEADS)]
        probs = [_ch_probs(s, b_ref[2 * pair + h], valid) for s, (pair, h) in zip(scores, CH_HEADS)]
        dzs = [p * (dp - jnp.sum(dp * p, axis=-1, keepdims=True)) for p, dp in zip(probs, dps)]
        for k, dz in enumerate(dzs):
            db_ref[k] += dz
        dzbs = [(dz * ATT_SCALE).astype(BF16) for dz in dzs]
        dqs = [_dot(dz, _only(lanes[h], kws[pair])) for dz, (pair, h) in zip(dzbs, CH_HEADS)]
        dks = [_dot(dz, q, TN) for dz, q in zip(dzbs, qs)]
        dvs = [_dot(p.astype(BF16), do, TN) for p, do in zip(probs, dos)]
        for pair, cols in enumerate(CH_COLS):
            dq_ref[:, cols] = (dqs[2 * pair] + dqs[2 * pair + 1]).astype(BF16)
            dk_s[win, cols] += dks[2 * pair] + dks[2 * pair + 1]
            dv_s[win, cols] += dvs[2 * pair] + dvs[2 * pair + 1]

        @pl.when(n == n_chunks - 1)
        def _():
            dk_ref[...] = dk_s[PAD:, :].astype(BF16)
            dv_ref[...] = dv_s[PAD:, :].astype(BF16)

    full = lambda col: pl.BlockSpec((T, W), lambda n: (0, col))
    blk = lambda col: pl.BlockSpec((CH_Q, W), lambda n: (n, col))
    tab = pl.BlockSpec((N_HEADS, CH_Q, CH_WIN), lambda n: (0, 0, 0))
    out = _out((T, W), BF16)
    return pl.pallas_call(
        lambda after_ref, *refs: body(*refs), name=name, grid=(n_chunks,),
        in_specs=[ANY, blk(3), full(4), full(5), tab, blk(0)],
        out_specs=[blk(0), full(0), full(0), tab],
        out_shape=[out, out, out, _out((N_HEADS, CH_Q, CH_WIN), F32)],
        scratch_shapes=[pltpu.VMEM((PAD + T, W), BF16)] * 2 + [pltpu.VMEM((PAD + T, W), F32)] * 2,
        compiler_params=_params(1, VMEM_LIMIT),
    )(after, *_hbm(qkv, qkv, qkv, bias, do))


def _rows_split(a, parts):
    return a.reshape(a.shape[:-2] + (parts, a.shape[-2] // parts, a.shape[-1]))


def _cast_into_own_slot(me, c, ws, in_chip_order, name):
    parts = 2
    ws = [_rows_split(_rows_split(w, 2), parts) for w in ws]
    n = len(ws)

    def body(me_ref, c_ref, *refs):
        for src, dst in zip(refs[:n], refs[n:]):
            dst[0, 0, 0] = src[0, 0].astype(BF16)

    def specs(w, plain):
        block = (1, 1) + w.shape[2:]
        if plain:
            return (pl.BlockSpec(block, lambda d, r, me_ref, c_ref: (d, r, 0, 0)),
                    pl.BlockSpec((1,) + block, lambda d, r, me_ref, c_ref: (me_ref[0], d, r, 0, 0)))
        return (pl.BlockSpec(block, lambda d, r, me_ref, c_ref: (d ^ c_ref[0], r, 0, 0)),
                pl.BlockSpec((1,) + block, lambda d, r, me_ref, c_ref: (0, d, r, 0, 0)))

    both = [specs(w, plain) for w, plain in zip(ws, in_chip_order)]
    outs = pl.pallas_call(
        body, name=name,
        grid_spec=pltpu.PrefetchScalarGridSpec(
            num_scalar_prefetch=2, grid=(2, parts),
            in_specs=[s[0] for s in both], out_specs=[s[1] for s in both]),
        out_shape=[_out((N_CHIPS,) + w.shape, BF16) for w in ws],
        compiler_params=_params(2, VMEM_LIMIT),
    )(me, c, *_hbm(*ws))
    return [o.reshape(N_CHIPS, 2, o.shape[2] * o.shape[3], o.shape[4]) for o in outs]


def _zone_slots(in_chip_order):
    x, y, c, _ = _place()
    me = 2 * x + y
    if in_chip_order:
        return (me, c), (lambda r: (me, c)), (lambda r: (me ^ r, c)), (lambda r: (me ^ r, c))
    return (0, 0), (lambda r: (r, 0)), (lambda r: (r, 0)), (lambda r: (r, 1))


def _pair_add(c, mine, got, permuted, name):
    parts = 2
    mine = [_rows_split(m, parts) for m in mine]
    got = [_rows_split(g, parts) for g in got]
    n = len(mine)

    def body(c_ref, *refs):
        for a, b, o in zip(refs[:n], refs[n:2 * n], refs[2 * n:]):
            o[0, 0] = (a[0, 0, 0] + b[0, 0].astype(F32)).astype(BF16)

    def mine_spec(m, perm):
        if perm:
            return pl.BlockSpec((1, 1, 1) + m.shape[3:], lambda j, r, c_ref: (j, 0, r, 0, 0))
        return pl.BlockSpec((1, 1, 1) + m.shape[3:], lambda j, r, c_ref: (j, c_ref[0], r, 0, 0))

    def got_spec(g):
        return pl.BlockSpec((1, 1) + g.shape[2:], lambda j, r, c_ref: (j, r, 0, 0))

    outs = pl.pallas_call(
        body, name=name,
        grid_spec=pltpu.PrefetchScalarGridSpec(
            num_scalar_prefetch=1, grid=(N_CHIPS, parts),
            in_specs=[mine_spec(m, perm) for m, perm in zip(mine, permuted)] + [got_spec(g) for g in got],
            out_specs=[got_spec(g) for g in got]),
        out_shape=[_out(g.shape, BF16) for g in got],
        compiler_params=_params(2, VMEM_LIMIT),
    )(c, *_hbm(*mine, *got))
    return [o.reshape(o.shape[0], o.shape[1] * o.shape[2], o.shape[3]) for o in outs]


def _chip_add(me, partials, landed, permuted, name):
    parts = 2
    ps = [_rows_split(x, parts) for x in partials]
    ls = [_rows_split(x, parts) for x in landed]
    n = len(ps)

    def body(me_ref, *refs):
        for own, got, o in zip(refs[:n], refs[n:2 * n], refs[2 * n:]):
            acc = own[0, 0].astype(F32)
            for r in range(N_CHIPS - 1):
                acc = acc + got[r, 0].astype(F32)
            o[0] = acc

    def own_spec(x, perm):
        if perm:
            return pl.BlockSpec((1, 1) + x.shape[2:], lambda r, me_ref: (0, r, 0, 0))
        return pl.BlockSpec((1, 1) + x.shape[2:], lambda r, me_ref: (me_ref[0], r, 0, 0))

    outs = pl.pallas_call(
        body, name=name,
        grid_spec=pltpu.PrefetchScalarGridSpec(
            num_scalar_prefetch=1, grid=(parts,),
            in_specs=[own_spec(x, perm) for x, perm in zip(ps, permuted)]
            + [pl.BlockSpec((N_CHIPS - 1, 1) + x.shape[2:], lambda r, me_ref: (0, r, 0, 0)) for x in ls],
            out_specs=[pl.BlockSpec((1,) + x.shape[2:], lambda r, me_ref: (r, 0, 0)) for x in ps]),
        out_shape=[_out(x.shape[1:], F32) for x in ps],
        compiler_params=_params(1, VMEM_LIMIT),
    )(me, *_hbm(*ps, *ls))
    return [o.reshape(o.shape[0] * o.shape[1], o.shape[2]) for o in outs]


def _adamw_math(w, g, m, v):
    m = ADAM_B1 * m + (1.0 - ADAM_B1) * g
    v = ADAM_B2 * v + (1.0 - ADAM_B2) * (g * g)
    m_hat = m / (1.0 - ADAM_B1 ** ADAM_STEP)
    v_hat = v / (1.0 - ADAM_B2 ** ADAM_STEP)
    delta = -ADAM_LR * (m_hat / (jnp.sqrt(v_hat) + ADAM_EPS) + ADAM_WD * w)
    return delta, m, v


def _adamw(ws, gs, ms, vs, parts, name):
    n = len(ws)
    flat = [_rows_split(a, parts) for a in (*ws, *gs, *ms, *vs)]

    def body(*refs):
        ins, outs = refs[:4 * n], refs[4 * n:]
        for k in range(n):
            d, m, v = _adamw_math(ins[k][...], ins[n + k][...], ins[2 * n + k][...], ins[3 * n + k][...])
            outs[k][...] = d
            outs[n + k][...] = m
            outs[2 * n + k][...] = v

    spec = lambda a: pl.BlockSpec((1,) + a.shape[1:], lambda i: (i, 0, 0))
    outs = pl.pallas_call(
        body, name=name, grid=(parts,),
        in_specs=[spec(a) for a in flat], out_specs=[spec(a) for a in flat[:n]] * 3,
        out_shape=[_out(a.shape, F32) for a in flat[:n]] * 3,
        compiler_params=_params(1, VMEM_LIMIT),
    )(*_hbm(*flat))
    outs = [o.reshape(o.shape[0] * o.shape[1], o.shape[2]) for o in outs]
    return outs[:n], outs[n:2 * n], outs[2 * n:]


def _adamw_halves(c, ws, owns, others, ms, vs, name):
    parts = 4
    n = len(ws)
    whole = [_rows_split(_rows_split(a, 2), parts) for a in (*ws, *ms, *vs)]
    halves = [_rows_split(a, parts) for a in (*owns, *others)]

    def body(c_ref, *refs):
        ins, outs = refs[:5 * n], refs[5 * n:]
        mine = pl.program_id(0) == c_ref[0]
        for k in range(n):
            g = jnp.where(mine, ins[3 * n + k][0], ins[4 * n + k][0])
            d, m, v = _adamw_math(ins[k][0, 0], g, ins[n + k][0, 0], ins[2 * n + k][0, 0])
            for slot, val in enumerate((g, d, m, v)):
                outs[slot * n + k][0, 0] = val

    wspec = lambda a: pl.BlockSpec((1, 1) + a.shape[2:], lambda h, r, c_ref: (h, r, 0, 0))
    hspec = lambda a: pl.BlockSpec((1,) + a.shape[1:], lambda h, r, c_ref: (r, 0, 0))
    outs = pl.pallas_call(
        body, name=name,
        grid_spec=pltpu.PrefetchScalarGridSpec(
            num_scalar_prefetch=1, grid=(2, parts),
            in_specs=[wspec(a) for a in whole] + [hspec(a) for a in halves],
            out_specs=[wspec(a) for a in whole[:n]] * 4),
        out_shape=[_out(a.shape, F32) for a in whole[:n]] * 4,
        compiler_params=_params(2, VMEM_LIMIT),
    )(c, *_hbm(*whole, *halves))
    outs = [o.reshape(2 * parts * o.shape[2], o.shape[3]) for o in outs]
    return outs[:n], outs[n:2 * n], outs[2 * n:3 * n], outs[3 * n:]


def _place():
    x, y, c = lax.axis_index("x"), lax.axis_index("y"), lax.axis_index("c")
    peers = [(x ^ (r >> 1), y ^ (r & 1), c) for r in (1, 2, 3)]
    return x, y, c, peers


def _handshake(peers):
    barrier = pltpu.get_barrier_semaphore()
    for peer in peers:
        pl.semaphore_signal(barrier, inc=1, device_id=peer, device_id_type=MESH)
    pl.semaphore_wait(barrier, len(peers))


ANY = pl.BlockSpec(memory_space=pl.ANY)
HBM = pl.BlockSpec(memory_space=pltpu.HBM)
SEM = pl.BlockSpec(memory_space=pltpu.SEMAPHORE)
SPLIT_COPY = pltpu.SideEffectType.DATAFLOW_SIDE_EFFECTING


def _split_start(body, name, collective_id, operands, n_sems, after=None):
    n = len(operands)
    extra = [] if after is None else [after]

    def wrapped(*refs):
        at = n + len(extra)
        body(refs[:n], refs[at], refs[at + 1])
        token = refs[-1]
        token[...] = jnp.zeros_like(token)

    outs = pl.pallas_call(
        wrapped, name=name,
        in_specs=[HBM] * n + [ANY] * len(extra),
        out_shape=(pltpu.SemaphoreType.DMA((n_sems,)), pltpu.SemaphoreType.DMA((n_sems,)),
                   *[pltpu.HBM(a.shape, a.dtype) for a in operands], jax.ShapeDtypeStruct((8, 128), F32)),
        out_specs=(SEM, SEM, *[HBM] * n, pl.BlockSpec(memory_space=pltpu.VMEM)),
        input_output_aliases={i: 2 + i for i in range(n)},
        compiler_params=pltpu.CompilerParams(has_side_effects=SPLIT_COPY, collective_id=collective_id),
    )(*_hbm(*operands), *extra)
    return outs[0], outs[1], list(outs[2:2 + n]), outs[-1]


def _split_wait(body, name, send_sem, recv_sem, operands, after):
    n = len(operands)

    def wrapped(*refs):
        body(refs[:n], refs[n], refs[n + 1])

    outs = pl.pallas_call(
        wrapped, name=name,
        in_specs=[HBM] * n + [SEM, SEM, ANY],
        out_shape=tuple(pltpu.HBM(a.shape, a.dtype) for a in operands),
        out_specs=tuple([HBM] * n),
        input_output_aliases={i: i for i in range(n)},
        compiler_params=pltpu.CompilerParams(has_side_effects=SPLIT_COPY),
    )(*operands, send_sem, recv_sem, after)
    return list(outs)


def _gather_copies(lands, in_chip_order, send_sem, recv_sem):
    peers = _place()[3]
    copies = []
    for a, (land, plain) in enumerate(zip(lands, in_chip_order)):
        own, sent_to, _, _ = _zone_slots(plain)
        copies += [pltpu.make_async_remote_copy(
            src_ref=land.at[own], dst_ref=land.at[sent_to(r + 1)],
            send_sem=send_sem.at[a * 3 + r], recv_sem=recv_sem.at[a * 3 + r],
            device_id=peers[r], device_id_type=MESH) for r in range(3)]
    return copies


def _gather_start(lands, in_chip_order, name, collective_id, after):
    def body(refs, send_sem, recv_sem):
        _handshake(_place()[3])
        for cp in _gather_copies(refs, in_chip_order, send_sem, recv_sem):
            cp.start()

    return _split_start(body, name, collective_id, list(lands), 3 * len(lands), after)


def _gather_wait(send_sem, recv_sem, operands, in_chip_order, after, name):
    def body(refs, send_sem, recv_sem):
        for cp in _gather_copies(refs, in_chip_order, send_sem, recv_sem):
            cp.wait_send()
            cp.wait_recv()

    return _split_wait(body, name, send_sem, recv_sem, operands, after)


def _gather_finish(lands, in_chip_order, with_ici, name):
    n = len(lands)

    def body(*refs):
        land = refs[n:2 * n]
        send_ici, recv_ici, send_d2d, recv_d2d = refs[2 * n:]
        x, y, c, _ = _place()
        ici = _gather_copies(land, in_chip_order, send_ici, recv_ici) if with_ici else []
        for cp in ici:
            cp.start()
        passed = []
        for a in range(n):
            _, _, received, kept = _zone_slots(in_chip_order[a])
            passed += [pltpu.make_async_remote_copy(
                src_ref=land[a].at[received(r + 1)], dst_ref=land[a].at[kept(r + 1)],
                send_sem=send_d2d.at[a * 3 + r], recv_sem=recv_d2d.at[a * 3 + r],
                device_id=(x, y, 1 - c), device_id_type=MESH) for r in range(3)]
        for k, cp in enumerate(passed):
            if with_ici:
                ici[k].wait_recv()
            cp.start()
        for cp in passed:
            cp.wait_recv()
        for cp in ici:
            cp.wait_send()
        for cp in passed:
            cp.wait_send()

    outs = pl.pallas_call(
        body, name=name,
        in_specs=[ANY] * n, out_specs=[ANY] * n,
        out_shape=[_out(l.shape, l.dtype) for l in lands],
        input_output_aliases={a: a for a in range(n)},
        scratch_shapes=[pltpu.SemaphoreType.DMA((3 * n,))] * 4,
    )(*lands)
    return list(outs)


def _pass_copies(lands, in_chip_order, send_sem, recv_sem):
    x, y, c, _ = _place()
    copies = []
    for a, (land, plain) in enumerate(zip(lands, in_chip_order)):
        _, _, received, kept = _zone_slots(plain)
        copies += [pltpu.make_async_remote_copy(
            src_ref=land.at[received(r + 1)], dst_ref=land.at[kept(r + 1)],
            send_sem=send_sem.at[a * 3 + r], recv_sem=recv_sem.at[a * 3 + r],
            device_id=(x, y, 1 - c), device_id_type=MESH) for r in range(3)]
    return copies


def _pass_start(lands, in_chip_order, name, collective_id):
    def body(refs, send_sem, recv_sem):
        x, y, c, _ = _place()
        _handshake([(x, y, 1 - c)])
        for cp in _pass_copies(refs, in_chip_order, send_sem, recv_sem):
            cp.start()

    return _split_start(body, name, collective_id, list(lands), 3 * len(lands))


def _pass_wait(send_sem, recv_sem, lands, in_chip_order, after, name):
    def body(refs, send_sem, recv_sem):
        for cp in _pass_copies(refs, in_chip_order, send_sem, recv_sem):
            cp.wait_send()
            cp.wait_recv()

    return _split_wait(body, name, send_sem, recv_sem, lands, after)


def _slabs(land):
    return land.reshape(N_CHIPS, 2 * land.shape[2], land.shape[3])


def _pair_swap(grads, permuted, name):
    n = len(grads)

    def body(*refs):
        src, dst = refs[:n], refs[n:2 * n]
        send_sem, recv_sem = refs[2 * n:]
        x, y, c, _ = _place()
        copies = [pltpu.make_async_remote_copy(
            src_ref=src[a].at[:, 1] if permuted[a] else src[a].at[:, 1 - c], dst_ref=dst[a],
            send_sem=send_sem.at[a], recv_sem=recv_sem.at[a],
            device_id=(x, y, 1 - c), device_id_type=MESH) for a in range(n)]
        for cp in copies:
            cp.start()
        for cp in copies:
            cp.wait()

    return pl.pallas_call(
        body, name=name,
        in_specs=[ANY] * n, out_specs=[ANY] * n,
        out_shape=[_out((N_CHIPS,) + g.shape[2:], g.dtype) for g in grads],
        scratch_shapes=[pltpu.SemaphoreType.DMA((n,))] * 2,
    )(*grads)


def _swap_copies(refs, permuted, send_sem, recv_sem):
    n = len(refs) // 2
    x, y, c, _ = _place()
    return [pltpu.make_async_remote_copy(
        src_ref=refs[a].at[:, 1] if permuted[a] else refs[a].at[:, 1 - c], dst_ref=refs[n + a],
        send_sem=send_sem.at[a], recv_sem=recv_sem.at[a],
        device_id=(x, y, 1 - c), device_id_type=MESH) for a in range(n)]


def _pair_swap_start(grads, permuted, name, collective_id):
    def body(refs, send_sem, recv_sem):
        x, y, c, _ = _place()
        _handshake([(x, y, 1 - c)])
        for cp in _swap_copies(refs, permuted, send_sem, recv_sem):
            cp.start()

    lands = [lax.empty((N_CHIPS,) + g.shape[2:], g.dtype) for g in grads]
    return _split_start(body, name, collective_id, list(grads) + lands, len(grads))


def _pair_swap_wait(send_sem, recv_sem, operands, permuted, after, name):
    def body(refs, send_sem, recv_sem):
        for cp in _swap_copies(refs, permuted, send_sem, recv_sem):
            cp.wait_send()
            cp.wait_recv()

    return _split_wait(body, name, send_sem, recv_sem, operands, after)


def _scatter_copies(refs, permuted, send_sem, recv_sem):
    n = len(refs) // 2
    x, y, _, peers = _place()
    me = 2 * x + y
    return [pltpu.make_async_remote_copy(
        src_ref=refs[a].at[r + 1] if permuted[a] else refs[a].at[me ^ (r + 1)], dst_ref=refs[n + a].at[r],
        send_sem=send_sem.at[a * 3 + r], recv_sem=recv_sem.at[a * 3 + r],
        device_id=peers[r], device_id_type=MESH) for a in range(n) for r in range(3)]


def _scatter_start(partials, permuted, name, collective_id):
    def body(refs, send_sem, recv_sem):
        _handshake(_place()[3])
        for cp in _scatter_copies(refs, permuted, send_sem, recv_sem):
            cp.start()

    lands = [lax.empty((N_CHIPS - 1,) + p.shape[1:], p.dtype) for p in partials]
    return _split_start(body, name, collective_id, list(partials) + lands, 3 * len(partials))


def _scatter_wait(send_sem, recv_sem, operands, permuted, after, name):
    def body(refs, send_sem, recv_sem):
        for cp in _scatter_copies(refs, permuted, send_sem, recv_sem):
            cp.wait_send()
            cp.wait_recv()

    return _split_wait(body, name, send_sem, recv_sem, operands, after)


def _pair_join(halves, name):
    n = len(halves)

    def body(*refs):
        src, dst = refs[:n], refs[n:2 * n]
        send_sem, recv_sem = refs[2 * n:]
        x, y, c, _ = _place()
        copies = [pltpu.make_async_remote_copy(
            src_ref=src[a], dst_ref=dst[a], send_sem=send_sem.at[a], recv_sem=recv_sem.at[a],
            device_id=(x, y, 1 - c), device_id_type=MESH) for a in range(n)]
        for cp in copies:
            cp.start()
        for cp in copies:
            cp.wait()

    return pl.pallas_call(
        body, name=name,
        in_specs=[ANY] * n, out_specs=[ANY] * n,
        out_shape=[_out(h.shape, F32) for h in halves],
        scratch_shapes=[pltpu.SemaphoreType.DMA((n,))] * 2,
    )(*halves)


def _all_sum_small(vs, after, name):
    rows = [v.shape[0] for v in vs]
    n, R, C = len(vs), sum(rows), vs[0].shape[1]
    n_dev = 8

    def body(*refs):
        after_ref, o_ref, mine, buf, send_sem, recv_sem = refs[n:]
        x, y, c, _ = _place()
        me = 4 * x + 2 * y + c
        at = 0
        for v_ref, r in zip(refs[:n], rows):
            mine[at:at + r] = v_ref[...]
            at += r
        buf[me] = mine[...]
        copies = []
        for k in range(1, n_dev):
            peer = (x ^ (k >> 2), y ^ ((k >> 1) & 1), c ^ (k & 1))
            copies.append(pltpu.make_async_remote_copy(
                src_ref=mine, dst_ref=buf.at[me], send_sem=send_sem.at[k - 1], recv_sem=recv_sem.at[k - 1],
                device_id=peer, device_id_type=MESH))
        for cp in copies:
            cp.start()
        for cp in copies:
            cp.wait()
        acc = buf[0]
        for m in range(1, n_dev):
            acc = acc + buf[m]
        o_ref[...] = acc

    return pl.pallas_call(
        body, name=name,
        in_specs=[pl.BlockSpec(memory_space=pltpu.VMEM)] * n + [ANY], out_specs=pl.BlockSpec(memory_space=pltpu.VMEM),
        out_shape=jax.ShapeDtypeStruct((R, C), F32),
        scratch_shapes=[pltpu.VMEM((R, C), F32), pltpu.VMEM((n_dev, R, C), F32),
                        pltpu.SemaphoreType.DMA((n_dev - 1,)), pltpu.SemaphoreType.DMA((n_dev - 1,))],
    )(*vs, after)


class _WholeWeights:
    def __init__(self, w):
        self.w = w

    def weights(self, group, after=None):
        return ({} if group == "passed" else self.w), None

    def grads_ready(self, group, gw):
        return None

    def grads_sent(self, group, after):
        return None


def _local_step(x, p, target, gains, rel_bias, hooks):
    T, D = x.shape
    S = N_CHIPS

    tied = lambda gain, token: gain if token is None else gain + token[0, 0]
    w, token = hooks.weights("first")
    w = dict(w)
    xn1, g1, u1, a1 = _ffn_up(x, tied(gains["ffn1_pre"], token), w["ffn1_gate"], w["ffn1_up"], "ffn1_up")
    w.update(hooks.weights("down", a1)[0])
    h1, f1 = _ffn_down(x, a1, gains["ffn1_post"], w["ffn1_down"], "ffn1_down")
    more, token = hooks.weights("in", h1)
    w.update(more)
    qkv, un = _norm_proj(h1, tied(gains["mix_pre"], token), w["in"], "qkv_proj")
    bias = _ch_group_bias(_bias_table(rel_bias, "bias_table").transpose(1, 0, 2))
    o_a = _sb_fwd(qkv, "sb_fwd")
    o_b = _ch_fwd(qkv, bias, "ch_fwd")
    more, token = hooks.weights("rest", o_b)
    w.update(more)
    w_out = w["out"].reshape(D, D)
    h2, mixed, mo = _mix_out_fwd(h1, o_a, o_b, gains["out_sb"], gains["out_ch"], w_out,
                                 tied(gains["mix_post"], token), "mix_out_fwd")
    w.update(hooks.weights("passed", h2)[0])
    h3, xn2, g2, u2, a2, f2 = _ffn_fwd(h2, gains["ffn2_pre"], gains["ffn2_post"], w["ffn2_gate"], w["ffn2_up"],
                                       w["ffn2_down"], "ffn2_fwd")
    w_ple_proj = w["ple_proj"].transpose(1, 0, 2).reshape(p.shape[1], D)
    w_ple_gate = w["ple_gate"].reshape(D, D)

    loss, dh3, dproj, dgate, dg_ple = _ple_loss(h3, p, target, w_ple_proj, w_ple_gate, gains["ple_post"], "ple_loss")
    gw, gg = {}, {"ple_post": dg_ple}
    gw["ple_proj"] = _mm_tn(p[None], dproj, p.shape[1], "dw_ple_proj")
    row_sharded = lambda pair: tuple(o.reshape(S, D // S, D) for o in pair)
    gw["ple_gate"] = row_sharded(_mm_tn(h3[None], dgate[None], 512, "dw_ple_gate"))

    def ffn_bwd(tag, dh, x_in, xn, g_act, u_act, a_act, f, group):
        dgp, dup, df, gg[tag + "_post"] = _ffn_bwd_act(dh, f, gains[tag + "_post"], w[tag + "_down"], g_act, u_act,
                                                       tag + "_bwd_act")
        gw[tag + "_gate"] = _mm_tn(dgp, xn[None], dgp.shape[2], "dw_" + tag + "_gate")
        gw[tag + "_up"] = _mm_tn(dup, xn[None], dup.shape[2], "dw_" + tag + "_up")
        gw[tag + "_down"] = _mm_tn(a_act, df[None], a_act.shape[2], "dw_" + tag + "_down")
        g_pre = gains[tag + "_pre"]
        if group is not None:
            token = hooks.grads_ready(group, gw)
            g_pre = g_pre if token is None else g_pre + token[0, 0]
        dx, gg[tag + "_pre"] = _proj_bwd([dgp, dup], [w[tag + "_gate"], w[tag + "_up"]], x_in, g_pre, dh,
                                         tag + "_bwd_in")
        return dx

    dh2 = ffn_bwd("ffn2", dh3, h2, xn2, g2, u2, a2, f2, None)
    dmo, do_a, do_b, gg["mix_post"], gg["out_sb"], gg["out_ch"] = _mix_out_bwd(
        dh2, mo, gains["mix_post"], w_out, o_a, o_b, gains["out_sb"], gains["out_ch"], "mix_out_bwd")
    gw["out"] = row_sharded(_mm_tn(mixed[None], dmo[None], 512, "dw_out"))
    token = hooks.grads_ready("early", gw)
    dq_a, dk_a, dv_a = _sb_bwd(qkv, do_a, o_a, do_a if token is None else token, "sb_bwd")
    token = hooks.grads_sent("early", dq_a)
    dq_b, dk_b, dv_b, dbias = _ch_bwd(qkv, bias, do_b, do_b if token is None else token, "ch_bwd")
    g_rel = _bias_grad(_ch_fold_bias_grad(dbias).transpose(1, 0, 2), "bias_grad")
    dqkv = [dq_a, dk_a, dv_a, dq_b, dk_b, dv_b]
    gw["in"] = _dw_in(un, dqkv, w["in"].shape[2], 512, "dw_in")
    dh1, gg["mix_pre"] = _qkv_bwd_in(dqkv, w["in"], h1, gains["mix_pre"], dh2, "qkv_bwd_in")
    dx = ffn_bwd("ffn1", dh1, x, xn1, g1, u1, a1, f1, "late")
    return loss, dx, gw, gg, g_rel


BIG = ["ffn1_gate", "ffn1_up", "ffn1_down", "in", "out", "ffn2_gate", "ffn2_up", "ffn2_down", "ple_proj", "ple_gate"]
GAINS = ["ffn1_pre", "ffn1_post", "mix_pre", "mix_post", "out_sb", "out_ch", "ffn2_pre", "ffn2_post", "ple_post"]
TRANSPOSED = ("w_ffn1_gate", "w_ffn1_up", "w_ffn2_gate", "w_ffn2_up")
PERMUTED = ("ffn1_gate", "ffn1_up", "ffn1_down", "ffn2_gate", "ffn2_up", "ffn2_down")
W_GROUPS = {"first": ["ffn1_gate", "ffn1_up"], "down": ["ffn1_down"], "in": ["in"],
            "rest": ["out", "ffn2_gate", "ffn2_up", "ffn2_down", "ple_proj", "ple_gate"]}
G_GROUPS = {"early": ["ple_proj", "ple_gate", "ffn2_gate", "ffn2_up", "ffn2_down", "out"],
            "late": ["in", "ffn1_gate", "ffn1_up", "ffn1_down"]}
ORDER = ["g_ffn1_pre", "g_ffn1_post", "w_ffn1_gate", "w_ffn1_up", "w_ffn1_down", "g_mix_pre", "g_mix_post", "w_in",
         "g_out_sb", "g_out_ch", "rel_bias", "w_out", "g_ffn2_pre", "g_ffn2_post", "w_ffn2_gate", "w_ffn2_up",
         "w_ffn2_down", "w_ple_proj", "w_ple_gate", "g_ple_post"]


def kernel(x, p, g_ffn1_pre, g_ffn1_post, w_ffn1_gate, w_ffn1_up, w_ffn1_down, g_mix_pre, g_mix_post, w_in, g_out_sb, g_out_ch, rel_bias, w_out, g_ffn2_pre, g_ffn2_post, w_ffn2_gate, w_ffn2_up, w_ffn2_down, w_ple_proj, w_ple_gate, g_ple_post, loss_target, m_g_ffn1_pre, m_g_ffn1_post, m_w_ffn1_gate, m_w_ffn1_up, m_w_ffn1_down, m_g_mix_pre, m_g_mix_post, m_w_in, m_g_out_sb, m_g_out_ch, m_rel_bias, m_w_out, m_g_ffn2_pre, m_g_ffn2_post, m_w_ffn2_gate, m_w_ffn2_up, m_w_ffn2_down, m_w_ple_proj, m_w_ple_gate, m_g_ple_post, v_g_ffn1_pre, v_g_ffn1_post, v_w_ffn1_gate, v_w_ffn1_up, v_w_ffn1_down, v_g_mix_pre, v_g_mix_post, v_w_in, v_g_out_sb, v_g_out_ch, v_rel_bias, v_w_out, v_g_ffn2_pre, v_g_ffn2_post, v_w_ffn2_gate, v_w_ffn2_up, v_w_ffn2_down, v_w_ple_proj, v_w_ple_gate, v_g_ple_post):
    args = dict(locals())
    take = lambda a, n: a[0].T if n in TRANSPOSED else a[0]
    wts = {n: take(args[n], n) for n in ORDER}
    ms = {n: take(args["m_" + n], n) for n in ORDER}
    vs = {n: take(args["v_" + n], n) for n in ORDER}
    gains = {n: wts["g_" + n][None] for n in GAINS}

    c_idx = lax.axis_index("c").astype(jnp.int32).reshape(1)
    me_idx = (2 * lax.axis_index("x") + lax.axis_index("y")).astype(jnp.int32).reshape(1)
    south = lax.axis_index("c") == 0

    plain = lambda names: [n not in PERMUTED for n in names]
    lands = dict(zip(BIG, _cast_into_own_slot(me_idx, c_idx, [wts["w_" + n] for n in BIG], plain(BIG), "cast_weights")))

    class Overlapped:
        def __init__(self):
            self.started = {}
            self.flying = {}

        def start(self, group, collective_id, after):
            names = W_GROUPS[group]
            self.flying[group] = _gather_start([lands[n] for n in names], plain(names), "gather_%s_start" % group,
                                               collective_id, after)
            return self.flying[group][3]

        def weights(self, group, after=None):
            names = W_GROUPS.get(group)
            token = None
            if group == "first":
                zones = _gather_finish([lands[n] for n in names], plain(names), True, "gather_first")
                token = self.start("rest", 4, self.start("in", 1, self.start("down", 6, zones[0])))
            elif group == "passed":
                names, (send_sem, recv_sem, zones, _) = self.passing
                zones = _pass_wait(send_sem, recv_sem, zones, plain(names), after, "gather_rest_pass_wait")
            else:
                send_sem, recv_sem, zones, _ = self.flying[group]
                zones = _gather_wait(send_sem, recv_sem, zones, plain(names), after, "gather_%s_wait" % group)
                if group == "rest":
                    self.passing = names[1:], _pass_start(zones[1:], plain(names[1:]), "gather_rest_pass_start", 7)
                    names, zones, token = names[:1], zones[:1], self.passing[1][3]
                zones = _gather_finish(zones, plain(names), False, "gather_%s_finish" % group)
            return {n: _slabs(z) for n, z in zip(names, zones)}, token

        def grads_ready(self, group, gw):
            names = G_GROUPS[group]
            perm = [n in PERMUTED for n in names]
            halved = lambda g: g.reshape(N_CHIPS, 2, g.shape[1] // 2, g.shape[2])
            mine = [halved(gw[n][0]) for n in names]
            narrow = [halved(gw[n][1]) for n in names]
            if group == "late":
                return self.scatter(group, names, perm, mine, _pair_swap(narrow, perm, "grad_pair_swap_late"))
            self.swapping = names, perm, mine, _pair_swap_start(narrow, perm, "grad_pair_swap_start_early", 5)
            return self.swapping[3][3]

        def grads_sent(self, group, after):
            names, perm, mine, (send_sem, recv_sem, operands, _) = self.swapping
            operands = _pair_swap_wait(send_sem, recv_sem, operands, perm, after, "grad_pair_swap_wait_early")
            return self.scatter(group, names, perm, mine, operands[len(names):])

        def scatter(self, group, names, perm, mine, got):
            partial = _pair_add(c_idx, mine, got, perm, "grad_pair_add_" + group)
            send_sem, recv_sem, operands, token = _scatter_start(partial, perm, "grad_scatter_start_" + group,
                                                                 {"early": 2, "late": 3}[group])
            self.started[group] = names, perm, send_sem, recv_sem, operands, token
            return token

    def reduce_finish(state, after, tag):
        names, perm, send_sem, recv_sem, operands, _ = state
        operands = _scatter_wait(send_sem, recv_sem, operands, perm, after, "grad_scatter_wait_" + tag)
        n = len(names)
        own = _chip_add(me_idx, operands[:n], operands[n:], perm, "grad_chip_add_" + tag)
        return own, _pair_join(own, "grad_pair_join_" + tag)

    hooks = Overlapped()
    loss, dx, gw, gg, g_rel = _local_step(x[0], p[0, 0], loss_target[0], gains, wts["rel_bias"], hooks)

    grads, delta, new_m, new_v = {}, {}, {}, {}

    def finish(group, after):
        own, other = reduce_finish(hooks.started[group], after, group)
        names = ["w_" + n for n in G_GROUPS[group]]
        g, d, m, v = _adamw_halves(c_idx, [wts[n] for n in names], own, other, [ms[n] for n in names],
                                   [vs[n] for n in names], "adamw_" + group)
        for n, gg_, dd, mm, vv in zip(names, g, d, m, v):
            grads[n], delta[n], new_m[n], new_v[n] = gg_, dd, mm, vv
        return d[0]

    early_done = finish("early", dx)

    pieces = [gg[n].reshape(-1, 128) for n in GAINS] + [jnp.pad(g_rel, ((0, 0), (0, N_REL_PAD - N_REL))).reshape(-1, 128)]
    summed = _all_sum_small(pieces + [loss], early_done, "small_grad_sum")
    finish("late", summed)
    at = 0
    for n, piece in zip(GAINS, pieces[:-1]):
        grads["g_" + n] = summed[at:at + piece.shape[0]].reshape(1, -1)[0]
        at += piece.shape[0]
    grads["rel_bias"] = summed[at:at + pieces[-1].shape[0]].reshape(N_HEADS, N_REL_PAD)[:, :N_REL]
    loss = summed[at + pieces[-1].shape[0], 0]

    small = ["g_" + n for n in GAINS] + ["rel_bias"]
    as_rows = lambda a: (a.reshape(-1, 128) if a.size % 128 == 0 else jnp.pad(a, ((0, 0), (0, N_REL_PAD - N_REL))).reshape(-1, 128))
    d, m, v = _adamw([as_rows(wts[n]) for n in small], [as_rows(grads[n]) for n in small],
                     [as_rows(ms[n]) for n in small], [as_rows(vs[n]) for n in small], 1, "adamw_small")
    for n, dd, mm, vv in zip(small, d, m, v):
        back = (lambda a: a.reshape(N_HEADS, N_REL_PAD)[:, :N_REL]) if n == "rel_bias" else (lambda a: a.reshape(-1))
        delta[n], new_m[n], new_v[n] = back(dd), back(mm), back(vv)

    outs = [loss, dx[None]]
    for table in (grads, delta, new_m, new_v):
        outs += [(table[n].T if n in TRANSPOSED else table[n])[None] for n in ORDER]
    return tuple(outs)
```

```python
import jax
import jax.numpy as jnp
from jax import lax
from jax.experimental import pallas as pl
from jax.experimental.pallas import tpu as pltpu

F32 = jnp.float32
BF16 = jnp.bfloat16
EPS = 1e-6
N_CHIPS = 4
HEAD_DIM = 64
N_HEADS = 8
CHUNK = 64
LOOKBACK = 8
BAND = (LOOKBACK + 1) * CHUNK
PAD = LOOKBACK * CHUNK
REL_CLIP = 128
N_REL = 2 * REL_CLIP + 1
N_REL_PAD = 384
SB_BLOCK = 256
PAIR = 2 * HEAD_DIM
SB_PAIRS = 2
SB_FWD_PAIRS = 4
ATT_SCALE = HEAD_DIM ** -0.5
NEG_INF = -1e30
ROW_BLOCK = 512
WIDE_ROW_BLOCK = 1024
VMEM_LIMIT_WIDE = 56 * 1024 * 1024
VMEM_LIMIT = 48 * 1024 * 1024
MESH = pl.DeviceIdType.MESH

ADAM_LR = 0.001
ADAM_B1 = 0.9
ADAM_B2 = 0.999
ADAM_EPS = 1e-08
ADAM_WD = 0.01
ADAM_STEP = 10

NT = (((1,), (1,)), ((), ()))
TN = (((0,), (0,)), ((), ()))


def _params(n_grid, vmem=None):
    return pltpu.CompilerParams(dimension_semantics=("arbitrary",) * n_grid, vmem_limit_bytes=vmem)


def _hbm(*arrays):
    return [pltpu.with_memory_space_constraint(a, pltpu.HBM) for a in arrays]


def _out(shape, dtype):
    return pltpu.HBM(shape, dtype)


def _dot(a, b, dims=None):
    if dims is None:
        return jnp.dot(a, b, preferred_element_type=F32)
    return lax.dot_general(a, b, dims, preferred_element_type=F32)


def _sigmoid(x):
    return 1.0 / (1.0 + jnp.exp(-x))


def _rms_fwd(x, g):
    r = lax.rsqrt(jnp.mean(x * x, axis=-1, keepdims=True) + EPS)
    return x * r * g


def _rms_bwd(x, g, dy):
    r = lax.rsqrt(jnp.mean(x * x, axis=-1, keepdims=True) + EPS)
    xh = x * r
    dg = jnp.sum(dy * xh, axis=0, keepdims=True)
    t = dy * g
    dx = r * (t - xh * jnp.mean(t * xh, axis=-1, keepdims=True))
    return dx, dg


def _accumulate(ref, val, first):
    @pl.when(first)
    def _():
        ref[...] = val

    @pl.when(jnp.logical_not(first))
    def _():
        ref[...] += val


def _split2(x):
    hi = x.astype(BF16)
    lo = (x - hi.astype(F32)).astype(BF16)
    return hi, lo


def _ffn_fwd(x, g_pre, g_post, wg, wu, wd, name):
    T, D = x.shape
    S, FS, _ = wg.shape
    tm = min(WIDE_ROW_BLOCK, T)

    def body(x_ref, gpre_ref, gpost_ref, wg_ref, wu_ref, wd_ref,
             h_ref, xn_ref, g_ref, u_ref, a_ref, f_ref):
        k = pl.program_id(1)

        @pl.when(k == 0)
        def _():
            xn_ref[...] = _rms_fwd(x_ref[...], gpre_ref[...]).astype(BF16)

        xn = xn_ref[...]
        g = _dot(xn, wg_ref[0], NT)
        u = _dot(xn, wu_ref[0], NT)
        g_ref[0] = g
        u_ref[0] = u
        a = (g * _sigmoid(g) * u).astype(BF16)
        a_ref[0] = a
        _accumulate(f_ref, _dot(a, wd_ref[0]), k == 0)

        @pl.when(k == S - 1)
        def _():
            h_ref[...] = x_ref[...] + 0.5 * _rms_fwd(f_ref[...], gpost_ref[...])

    row = pl.BlockSpec((tm, D), lambda i, k: (i, 0))
    vec = pl.BlockSpec((1, D), lambda i, k: (0, 0))
    act = pl.BlockSpec((1, tm, FS), lambda i, k: (k, i, 0))
    return pl.pallas_call(
        body, name=name, grid=(T // tm, S),
        in_specs=[row, vec, vec] + [pl.BlockSpec((1, FS, D), lambda i, k: (k, 0, 0))] * 3,
        out_specs=[row, row, act, act, act, row],
        out_shape=[_out((T, D), F32), _out((T, D), BF16),
                   _out((S, T, FS), F32), _out((S, T, FS), F32),
                   _out((S, T, FS), BF16), _out((T, D), F32)],
        compiler_params=_params(2, VMEM_LIMIT_WIDE),
    )(*_hbm(x, g_pre, g_post, wg, wu, wd))


def _ffn_up(x, g_pre, wg, wu, name):
    T, D = x.shape
    S, FS, _ = wg.shape
    tm = min(WIDE_ROW_BLOCK, T)

    def body(x_ref, gpre_ref, wg_ref, wu_ref, xn_ref, g_ref, u_ref, a_ref):
        @pl.when(pl.program_id(1) == 0)
        def _():
            xn_ref[...] = _rms_fwd(x_ref[...], gpre_ref[...]).astype(BF16)

        xn = xn_ref[...]
        g = _dot(xn, wg_ref[0], NT)
        u = _dot(xn, wu_ref[0], NT)
        g_ref[0] = g
        u_ref[0] = u
        a_ref[0] = (g * _sigmoid(g) * u).astype(BF16)

    row = pl.BlockSpec((tm, D), lambda i, k: (i, 0))
    act = pl.BlockSpec((1, tm, FS), lambda i, k: (k, i, 0))
    return pl.pallas_call(
        body, name=name, grid=(T // tm, S),
        in_specs=[row, pl.BlockSpec((1, D), lambda i, k: (0, 0))] + [pl.BlockSpec((1, FS, D), lambda i, k: (k, 0, 0))] * 2,
        out_specs=[row, act, act, act],
        out_shape=[_out((T, D), BF16), _out((S, T, FS), F32), _out((S, T, FS), F32), _out((S, T, FS), BF16)],
        compiler_params=_params(2, VMEM_LIMIT_WIDE),
    )(*_hbm(x, g_pre, wg, wu))


def _ffn_down(x, a, g_post, wd, name):
    T, D = x.shape
    S, FS, _ = wd.shape
    tm = min(WIDE_ROW_BLOCK, T)

    def body(x_ref, a_ref, gpost_ref, wd_ref, h_ref, f_ref):
        k = pl.program_id(1)
        _accumulate(f_ref, _dot(a_ref[0], wd_ref[0]), k == 0)

        @pl.when(k == S - 1)
        def _():
            h_ref[...] = x_ref[...] + 0.5 * _rms_fwd(f_ref[...], gpost_ref[...])

    row = pl.BlockSpec((tm, D), lambda i, k: (i, 0))
    return pl.pallas_call(
        body, name=name, grid=(T // tm, S),
        in_specs=[row, pl.BlockSpec((1, tm, FS), lambda i, k: (k, i, 0)), pl.BlockSpec((1, D), lambda i, k: (0, 0)),
                  pl.BlockSpec((1, FS, D), lambda i, k: (k, 0, 0))],
        out_specs=[row, row],
        out_shape=[_out((T, D), F32), _out((T, D), F32)],
        compiler_params=_params(2, VMEM_LIMIT_WIDE),
    )(*_hbm(x, a, g_post, wd))


def _ffn_bwd_act(dh, f, g_post, wd, g_act, u_act, name):
    T, D = dh.shape
    S, FS, _ = wd.shape
    tm = min(WIDE_ROW_BLOCK, T)

    def body(dh_ref, f_ref, gpost_ref, wd_ref, g_ref, u_ref, dgp_ref, dup_ref, df_ref, dgain_ref, df_s):
        i, k = pl.program_id(0), pl.program_id(1)

        @pl.when(k == 0)
        def _():
            df, dgain = _rms_bwd(f_ref[...], gpost_ref[...], 0.5 * dh_ref[...])
            df_s[...] = df.astype(BF16)
            df_ref[...] = df_s[...]
            _accumulate(dgain_ref, dgain, i == 0)

        da = _dot(df_s[...], wd_ref[0], NT)
        g = g_ref[0]
        s = _sigmoid(g)
        dup_ref[0] = (da * (g * s)).astype(BF16)
        dgp_ref[0] = (da * u_ref[0] * (s * (1.0 + g * (1.0 - s)))).astype(BF16)

    row = pl.BlockSpec((tm, D), lambda i, k: (i, 0))
    vec = pl.BlockSpec((1, D), lambda i, k: (0, 0))
    act = pl.BlockSpec((1, tm, FS), lambda i, k: (k, i, 0))
    return pl.pallas_call(
        body, name=name, grid=(T // tm, S),
        in_specs=[row, row, vec, pl.BlockSpec((1, FS, D), lambda i, k: (k, 0, 0)), act, act],
        out_specs=[act, act, row, vec],
        out_shape=[_out((S, T, FS), BF16), _out((S, T, FS), BF16),
                   _out((T, D), BF16), _out((1, D), F32)],
        scratch_shapes=[pltpu.VMEM((tm, D), BF16)],
        compiler_params=_params(2, VMEM_LIMIT_WIDE),
    )(*_hbm(dh, f, g_post, wd, g_act, u_act))


def _proj_bwd(dys, ws, x, g_pre, dh, name):
    T, D = x.shape
    n = len(dys)
    S, N, _ = ws[0].shape
    tm = min(WIDE_ROW_BLOCK, T)

    def body(*refs):
        dy_refs, w_refs = refs[:n], refs[n:2 * n]
        x_ref, gpre_ref, dh_ref, dx_ref, dgain_ref, acc_s = refs[2 * n:]
        i, k = pl.program_id(0), pl.program_id(1)
        part = None
        for dy_ref, w_ref in zip(dy_refs, w_refs):
            term = _dot(dy_ref[0], w_ref[0])
            part = term if part is None else part + term
        _accumulate(acc_s, part, k == 0)

        @pl.when(k == S - 1)
        def _():
            dx, dgain = _rms_bwd(x_ref[...], gpre_ref[...], acc_s[...])
            dx_ref[...] = dh_ref[...] + dx
            _accumulate(dgain_ref, dgain, i == 0)

    row = pl.BlockSpec((tm, D), lambda i, k: (i, 0))
    vec = pl.BlockSpec((1, D), lambda i, k: (0, 0))
    return pl.pallas_call(
        body, name=name, grid=(T // tm, S),
        in_specs=[pl.BlockSpec((1, tm, N), lambda i, k: (k, i, 0))] * n
        + [pl.BlockSpec((1, N, D), lambda i, k: (k, 0, 0))] * n + [row, vec, row],
        out_specs=[row, vec],
        out_shape=[_out((T, D), F32), _out((1, D), F32)],
        scratch_shapes=[pltpu.VMEM((tm, D), F32)],
        compiler_params=_params(2, VMEM_LIMIT_WIDE),
    )(*_hbm(*dys, *ws, x, g_pre, dh))


def _mm_tn(a, b, bm, name):
    ga, T, M = a.shape
    gb, _, N = b.shape
    b_spec = pl.BlockSpec((1, T, N), (lambda g, m: (g, 0, 0)) if gb > 1 else (lambda g, m: (0, 0, 0)))
    G = max(ga, gb)

    def body(a_ref, b_ref, o_ref, narrow_ref):
        o_ref[0] = _dot(a_ref[0].astype(BF16), b_ref[0].astype(BF16), TN)
        narrow_ref[0] = o_ref[0].astype(BF16)

    out = pl.BlockSpec((1, bm, N), lambda g, m: (g, m, 0))
    return pl.pallas_call(
        body, name=name, grid=(G, M // bm),
        in_specs=[pl.BlockSpec((1, T, bm), (lambda g, m: (g, 0, m)) if ga > 1 else (lambda g, m: (0, 0, m))), b_spec],
        out_specs=[out, out],
        out_shape=[_out((G, M, N), F32), _out((G, M, N), BF16)],
        compiler_params=_params(2, VMEM_LIMIT),
    )(*_hbm(a, b))


QKV_PIECE = 256


def _qkv_shard(dy_refs, k, n_col):
    width = dy_refs[0].shape[1]
    parts = []
    for col in range(k * n_col, (k + 1) * n_col, QKV_PIECE):
        parts.append(dy_refs[col // width][:, col % width:col % width + QKV_PIECE])
    return jnp.concatenate(parts, axis=1)


def _qkv_bwd_in(dys, w, x, g_pre, dh, name):
    T, D = x.shape
    n = len(dys)
    S, _, N = w.shape
    tm = min(WIDE_ROW_BLOCK, T)

    def body(*refs):
        dy_refs = refs[:n]
        w_ref, x_ref, gpre_ref, dh_ref, dx_ref, dgain_ref, acc_s = refs[n:]
        i, k = pl.program_id(0), pl.program_id(1)
        for shard in range(S):
            @pl.when(k == shard)
            def _(shard=shard):
                part = _dot(_qkv_shard(dy_refs, shard, N), w_ref[0], NT)
                if shard == 0:
                    acc_s[...] = part
                else:
                    acc_s[...] += part

        @pl.when(k == S - 1)
        def _():
            dx, dgain = _rms_bwd(x_ref[...], gpre_ref[...], acc_s[...])
            dx_ref[...] = dh_ref[...] + dx
            _accumulate(dgain_ref, dgain, i == 0)

    row = pl.BlockSpec((tm, D), lambda i, k: (i, 0))
    vec = pl.BlockSpec((1, D), lambda i, k: (0, 0))
    return pl.pallas_call(
        body, name=name, grid=(T // tm, S),
        in_specs=[pl.BlockSpec((tm, dy.shape[1]), lambda i, k: (i, 0)) for dy in dys]
        + [pl.BlockSpec((1, D, N), lambda i, k: (k, 0, 0)), row, vec, row],
        out_specs=[row, vec],
        out_shape=[_out((T, D), F32), _out((1, D), F32)],
        scratch_shapes=[pltpu.VMEM((tm, D), F32)],
        compiler_params=_params(2, VMEM_LIMIT_WIDE),
    )(*_hbm(*dys, w, x, g_pre, dh))


def _dw_in(a, dys, n_col, bm, name):
    T, M = a.shape
    n = len(dys)
    S = n * dys[0].shape[1] // n_col

    def body(*refs):
        a_ref, dy_refs = refs[0], refs[1:1 + n]
        o_ref, narrow_ref = refs[1 + n:]
        k = pl.program_id(1)
        for shard in range(S):
            @pl.when(k == shard)
            def _(shard=shard):
                o_ref[0] = _dot(a_ref[...], _qkv_shard(dy_refs, shard, n_col), TN)
                narrow_ref[0] = o_ref[0].astype(BF16)

    out = pl.BlockSpec((1, bm, n_col), lambda m, k: (k, m, 0))
    return pl.pallas_call(
        body, name=name, grid=(M // bm, S),
        in_specs=[pl.BlockSpec((T, bm), lambda m, k: (0, m))]
        + [pl.BlockSpec((T, dy.shape[1]), lambda m, k: (0, 0)) for dy in dys],
        out_specs=[out, out],
        out_shape=[_out((S, M, n_col), F32), _out((S, M, n_col), BF16)],
        compiler_params=_params(2, VMEM_LIMIT_WIDE),
    )(*_hbm(a, *dys))


def _norm_proj(x, g_pre, w, name):
    T, D = x.shape
    S, _, N = w.shape
    tm = min(WIDE_ROW_BLOCK, T)

    def body(x_ref, g_ref, w_ref, o_ref, xn_ref, xn_s):
        @pl.when(pl.program_id(1) == 0)
        def _():
            xn_s[...] = _rms_fwd(x_ref[...], g_ref[...]).astype(BF16)
            xn_ref[...] = xn_s[...]

        o_ref[...] = _dot(xn_s[...], w_ref[0]).astype(BF16)

    row = pl.BlockSpec((tm, D), lambda i, k: (i, 0))
    return pl.pallas_call(
        body, name=name, grid=(T // tm, S),
        in_specs=[row, pl.BlockSpec((1, D), lambda i, k: (0, 0)), pl.BlockSpec((1, D, N), lambda i, k: (k, 0, 0))],
        out_specs=[pl.BlockSpec((tm, N), lambda i, k: (i, k)), row],
        out_shape=[_out((T, S * N), BF16), _out((T, D), BF16)],
        scratch_shapes=[pltpu.VMEM((tm, D), BF16)],
        compiler_params=_params(2, VMEM_LIMIT_WIDE),
    )(*_hbm(x, g_pre, w))


def _mix_out_fwd(h, o_a, o_b, g_sb, g_ch, w_out, g_post, name):
    T, D = h.shape
    W = g_sb.shape[1]
    tm = min(WIDE_ROW_BLOCK, T)

    def body(h_ref, oa_ref, ob_ref, gsb_ref, gch_ref, w_ref, gpost_ref, h2_ref, mixed_ref, mo_ref):
        mixed_ref[:, :W] = _rms_fwd(oa_ref[...], gsb_ref[...]).astype(BF16)
        mixed_ref[:, W:] = _rms_fwd(ob_ref[...], gch_ref[...]).astype(BF16)
        mo = _dot(mixed_ref[...], w_ref[...])
        mo_ref[...] = mo
        h2_ref[...] = h_ref[...] + _rms_fwd(mo, gpost_ref[...])

    row = pl.BlockSpec((tm, D), lambda i: (i, 0))
    part = pl.BlockSpec((tm, W), lambda i: (i, 0))
    half = pl.BlockSpec((1, W), lambda i: (0, 0))
    return pl.pallas_call(
        body, name=name, grid=(T // tm,),
        in_specs=[row, part, part, half, half, pl.BlockSpec((D, D), lambda i: (0, 0)), pl.BlockSpec((1, D), lambda i: (0, 0))],
        out_specs=[row, row, row],
        out_shape=[_out((T, D), F32), _out((T, D), BF16),
                   _out((T, D), F32)],
        compiler_params=_params(1, VMEM_LIMIT_WIDE),
    )(*_hbm(h, o_a, o_b, g_sb, g_ch, w_out, g_post))


def _mix_out_bwd(dh, mo, g_post, w_out, o_a, o_b, g_sb, g_ch, name):
    T, D = dh.shape
    W = g_sb.shape[1]
    tm = min(WIDE_ROW_BLOCK, T)

    def body(dh_ref, mo_ref, gpost_ref, w_ref, oa_ref, ob_ref, gsb_ref, gch_ref,
             dmo_ref, doa_ref, dob_ref, dgpost_ref, dgsb_ref, dgch_ref):
        first = pl.program_id(0) == 0
        dmo, dgpost = _rms_bwd(mo_ref[...], gpost_ref[...], dh_ref[...])
        dmo_ref[...] = dmo.astype(BF16)
        dmix = _dot(dmo_ref[...], w_ref[...], NT)
        doa_ref[...], dgsb = _rms_bwd(oa_ref[...], gsb_ref[...], dmix[:, :W])
        dob_ref[...], dgch = _rms_bwd(ob_ref[...], gch_ref[...], dmix[:, W:])
        _accumulate(dgpost_ref, dgpost, first)
        _accumulate(dgsb_ref, dgsb, first)
        _accumulate(dgch_ref, dgch, first)

    row = pl.BlockSpec((tm, D), lambda i: (i, 0))
    part = pl.BlockSpec((tm, W), lambda i: (i, 0))
    vec = pl.BlockSpec((1, D), lambda i: (0, 0))
    half = pl.BlockSpec((1, W), lambda i: (0, 0))
    return pl.pallas_call(
        body, name=name, grid=(T // tm,),
        in_specs=[row, row, vec, pl.BlockSpec((D, D), lambda i: (0, 0)), part, part, half, half],
        out_specs=[row, part, part, vec, half, half],
        out_shape=[_out((T, D), BF16), _out((T, W), F32),
                   _out((T, W), F32), _out((1, D), F32),
                   _out((1, W), F32), _out((1, W), F32)],
        compiler_params=_params(1, VMEM_LIMIT_WIDE),
    )(*_hbm(dh, mo, g_post, w_out, o_a, o_b, g_sb, g_ch))


def _ple_loss(h, p, target, w_proj, w_gate, g_post, name):
    T, D = h.shape
    P = p.shape[1]
    S = N_CHIPS
    C = D // S
    tm = min(ROW_BLOCK, T)

    def body(h_ref, p_ref, t_ref, wp_ref, wg_ref, g_ref, loss_ref, dh_ref, dproj_ref, dgate_ref, dgain_ref):
        first = pl.program_id(0) == 0
        h3 = h_ref[...]
        proj = _dot(p_ref[...].astype(BF16), wp_ref[...])
        s = _sigmoid(_dot(h3.astype(BF16), wg_ref[...]))
        e = proj * s
        diff = h3 + _rms_fwd(e, g_ref[...]) - t_ref[...]
        part = 0.5 * jnp.sum(jnp.mean(diff * diff, axis=-1, keepdims=True), axis=0, keepdims=True)
        _accumulate(loss_ref, jnp.broadcast_to(part, loss_ref.shape), first)
        dy = diff * (1.0 / D)
        de, dgain = _rms_bwd(e, g_ref[...], dy)
        _accumulate(dgain_ref, dgain, first)
        dproj = (de * s).astype(BF16)
        for j in range(S):
            dproj_ref[j] = dproj[:, j * C:(j + 1) * C]
        dgate_ref[...] = (de * proj * s * (1.0 - s)).astype(BF16)
        dh_ref[...] = dy + _dot(dgate_ref[...], wg_ref[...], NT)

    row = pl.BlockSpec((tm, D), lambda i: (i, 0))
    vec = pl.BlockSpec((1, D), lambda i: (0, 0))
    return pl.pallas_call(
        body, name=name, grid=(T // tm,),
        in_specs=[row, pl.BlockSpec((tm, P), lambda i: (i, 0)), row,
                  pl.BlockSpec((P, D), lambda i: (0, 0)), pl.BlockSpec((D, D), lambda i: (0, 0)), vec],
        out_specs=[pl.BlockSpec((8, 128), lambda i: (0, 0)), row,
                   pl.BlockSpec((S, tm, C), lambda i: (0, i, 0)), row, vec],
        out_shape=[_out((8, 128), F32), _out((T, D), F32),
                   _out((S, T, C), BF16), _out((T, D), BF16),
                   _out((1, D), F32)],
        compiler_params=_params(1, VMEM_LIMIT_WIDE),
    )(*_hbm(h, p, target, w_proj, w_gate, g_post))


def _sb_scores(q, kj, mask):
    z = _dot(q, kj, NT)
    sp = jnp.maximum(z, 0.0) + jnp.log(1.0 + jnp.exp(-jnp.abs(z)))
    return z, sp if mask is None else jnp.where(mask, sp, 0.0)


def _strict_causal():
    rows = lax.broadcasted_iota(jnp.int32, (SB_BLOCK, SB_BLOCK), 0)
    cols = lax.broadcasted_iota(jnp.int32, (SB_BLOCK, SB_BLOCK), 1)
    return cols < rows


def _tri(cmp):
    r = lax.broadcasted_iota(jnp.int32, (2 * SB_BLOCK, SB_BLOCK), 0) % SB_BLOCK
    c = lax.broadcasted_iota(jnp.int32, (2 * SB_BLOCK, SB_BLOCK), 1)
    return jnp.where(cmp(r, c), 1.0, 0.0).astype(BF16)


def _cum(x, tri):
    return _dot(jnp.concatenate(_split2(x), axis=1), tri)


def _pair_lanes():
    lane = lax.broadcasted_iota(jnp.int32, (1, PAIR), 1)
    return [lane < HEAD_DIM, lane >= HEAD_DIM]


def _only(lanes, x):
    return jnp.where(lanes, x, jnp.zeros_like(x))


def _sb_fwd(qkv, name):
    T = qkv.shape[0]
    B = SB_BLOCK
    W = SB_FWD_PAIRS * PAIR
    steps = N_HEADS // (2 * SB_FWD_PAIRS)
    heads = [(p, h) for p in range(SB_FWD_PAIRS) for h in range(2)]

    def body(q_ref, k_ref, v_ref, o_ref):
        i = pl.program_id(1)
        after = _tri(lambda r, c: r > c)
        lanes = _pair_lanes()
        cols = [slice(p * PAIR, (p + 1) * PAIR) for p in range(SB_FWD_PAIRS)]
        q = {(p, h): _only(lanes[h], q_ref[:, cols[p]] * ATT_SCALE) for p, h in heads}

        def tiles(j, carries, mask):
            at = pl.ds(pl.multiple_of(j * B, B), B)
            scores = [_sb_scores(q[ph], k_ref[at, cols[ph[0]]], mask) for ph in heads]
            laters = [_cum(sp, after) for _, sp in scores]
            out = []
            for ph, (z, sp), later, (run, acc) in zip(heads, scores, laters, carries):
                a = jnp.exp(z - sp - later - run)
                if mask is not None:
                    a = jnp.where(mask, a, 0.0)
                out.append((run + later[:, 0:1] + sp[:, 0:1],
                            acc + _dot(a.astype(BF16), _only(lanes[ph[1]], v_ref[at, cols[ph[0]]]))))
            return tuple(out)

        zero = (jnp.zeros((B, 1), F32), jnp.zeros((B, PAIR), F32))
        carries = tiles(i, (zero,) * len(heads), _strict_causal())
        carries = lax.fori_loop(0, i, lambda jj, cs: tiles(i - 1 - jj, cs, None), carries)
        for p in range(SB_FWD_PAIRS):
            o_ref[:, cols[p]] = carries[2 * p][1] + carries[2 * p + 1][1]

    blk = lambda off: pl.BlockSpec((B, W), lambda g, i: (i, g + off))
    full = lambda off: pl.BlockSpec((T, W), lambda g, i: (0, g + off))
    return pl.pallas_call(
        body, name=name, grid=(steps, T // B),
        in_specs=[blk(0), full(steps), full(2 * steps)],
        out_specs=blk(0),
        out_shape=_out((T, N_HEADS * HEAD_DIM), F32),
        compiler_params=_params(2, VMEM_LIMIT),
    )(*_hbm(qkv, qkv, qkv))


def _sb_bwd(qkv, do, o, after, name):
    T = qkv.shape[0]
    B = SB_BLOCK
    W = SB_PAIRS * PAIR
    steps = N_HEADS // (2 * SB_PAIRS)
    n_blocks = T // B
    heads = [(p, h) for p in range(SB_PAIRS) for h in range(2)]

    def body(q_ref, k_ref, v_ref, do_ref, o_ref, dq_ref, dk_ref, dv_ref, dk_s, dv_s):
        i = pl.program_id(1)

        @pl.when(i == 0)
        def _():
            dk_s[...] = jnp.zeros_like(dk_s)
            dv_s[...] = jnp.zeros_like(dv_s)

        after = _tri(lambda r, c: r > c)
        since = _tri(lambda r, c: r >= c)
        lanes = _pair_lanes()
        cols = [slice(p * PAIR, (p + 1) * PAIR) for p in range(SB_PAIRS)]
        q = {(p, h): _only(lanes[h], q_ref[:, cols[p]] * ATT_SCALE) for p, h in heads}
        do = {(p, h): _only(lanes[h], do_ref[:, cols[p]].astype(BF16)) for p, h in heads}
        total = {ph: jnp.sum(do[ph].astype(F32) * o_ref[:, cols[ph[0]]], axis=1, keepdims=True) for ph in heads}

        def tiles(j, carries, mask):
            at = pl.ds(pl.multiple_of(j * B, B), B)
            ks = [k_ref[at, c] for c in cols]
            vs = [v_ref[at, c] for c in cols]
            scores = [_sb_scores(q[ph], ks[ph[0]], mask) for ph in heads]
            laters = [_cum(sp, after) for _, sp in scores]
            das = [_dot(do[ph], vs[ph[0]], NT) for ph in heads]
            a_s, gs = [], []
            for (z, sp), later, da, carry in zip(scores, laters, das, carries):
                a = jnp.exp(z - sp - later - carry[0])
                if mask is not None:
                    a = jnp.where(mask, a, 0.0)
                a = a.astype(BF16)
                a_s.append(a)
                gs.append(a.astype(F32) * da)
            sinces = [_cum(g, since) for g in gs]
            dzs = []
            for ph, (_, sp), g, from_s, carry in zip(heads, scores, gs, sinces, carries):
                g_before = total[ph] - carry[1] - from_s
                fail = jnp.exp(-sp)
                dz = fail * (g + g_before) - g_before
                if mask is not None:
                    dz = jnp.where(mask, dz, 0.0)
                dzs.append(dz.astype(BF16))
            out = []
            for ph, (_, sp), a, dz, later, from_s, carry in zip(heads, scores, a_s, dzs, laters, sinces, carries):
                dk_s[at, cols[ph[0]]] += _dot(dz, q[ph], TN)
                dv_s[at, cols[ph[0]]] += _dot(a, do[ph], TN)
                out.append((carry[0] + later[:, 0:1] + sp[:, 0:1], carry[1] + from_s[:, 0:1],
                            carry[2] + _dot(dz, _only(lanes[ph[1]], ks[ph[0]]))))
            return tuple(out)

        col = jnp.zeros((B, 1), F32)
        zero = (col, col, jnp.zeros((B, PAIR), F32))
        carries = tiles(i, (zero,) * len(heads), _strict_causal())
        last = lax.fori_loop(0, i, lambda jj, cs: tiles(i - 1 - jj, cs, None), carries)
        for p in range(SB_PAIRS):
            dq_ref[:, cols[p]] = ((last[2 * p][2] + last[2 * p + 1][2]) * ATT_SCALE).astype(BF16)

        @pl.when(i == n_blocks - 1)
        def _():
            dk_ref[...] = dk_s[...].astype(BF16)
            dv_ref[...] = dv_s[...].astype(BF16)

    blk = lambda off: pl.BlockSpec((B, W), lambda g, i: (i, g + off))
    full = lambda off: pl.BlockSpec((T, W), lambda g, i: (0, g + off))
    out = _out((T, N_HEADS * HEAD_DIM), BF16)
    return pl.pallas_call(
        lambda after_ref, *refs: body(*refs), name=name, grid=(steps, n_blocks),
        in_specs=[ANY, blk(0), full(steps), full(2 * steps), blk(0), blk(0)],
        out_specs=[blk(0), full(0), full(0)],
        out_shape=[out, out, out],
        scratch_shapes=[pltpu.VMEM((T, W), F32)] * 2,
        compiler_params=_params(2, VMEM_LIMIT),
    )(after, *_hbm(qkv, qkv, qkv, do, o))


NEAR = BAND - PAD + REL_CLIP
FAR = BAND - NEAR
NEAR_REL = 2 * REL_CLIP
BIAS_ROWS = 8


def _rel_onehot(i, transposed):
    shape = (NEAR, NEAR_REL) if transposed else (NEAR_REL, NEAR)
    j = FAR + lax.broadcasted_iota(jnp.int32, shape, 0 if transposed else 1)
    r = lax.broadcasted_iota(jnp.int32, shape, 1 if transposed else 0)
    idx = jnp.clip(i + PAD - j, -REL_CLIP, REL_CLIP) + REL_CLIP
    return jnp.where(idx - 1 == r, 1.0, 0.0).astype(BF16)


def _bias_table(rel_bias, name):
    def body(near_ref, far_ref, o_ref):
        rb = near_ref[...]
        hi, lo = _split2(rb)
        lo2 = (rb - hi.astype(F32) - lo.astype(F32)).astype(BF16)
        far = jnp.broadcast_to(far_ref[...], (N_HEADS, FAR))
        for k in range(BIAS_ROWS):
            onehot = _rel_onehot(pl.program_id(0) * BIAS_ROWS + k, False)
            o_ref[k, :, :FAR] = far
            o_ref[k, :, FAR:] = _dot(hi, onehot) + _dot(lo, onehot) + _dot(lo2, onehot)

    return pl.pallas_call(
        body, name=name, grid=(CHUNK // BIAS_ROWS,),
        in_specs=[pl.BlockSpec((N_HEADS, NEAR_REL), lambda i: (0, 0)), pl.BlockSpec((N_HEADS, 1), lambda i: (0, 0))],
        out_specs=pl.BlockSpec((BIAS_ROWS, N_HEADS, BAND), lambda i: (i, 0, 0)),
        out_shape=_out((CHUNK, N_HEADS, BAND), F32),
        compiler_params=_params(1),
    )(*_hbm(rel_bias[:, 1:], rel_bias[:, N_REL - 1:]))


def _bias_grad(dbias_t, name):
    def body(d_ref, near_ref, far_ref):
        near, far = None, None
        for k in range(BIAS_ROWS):
            onehot = _rel_onehot(pl.program_id(0) * BIAS_ROWS + k, True)
            hi, lo = _split2(d_ref[k, :, FAR:])
            part = _dot(hi, onehot) + _dot(lo, onehot)
            rest = jnp.sum(d_ref[k, :, :FAR], axis=1, keepdims=True)
            near, far = (part, rest) if near is None else (near + part, far + rest)
        first = pl.program_id(0) == 0
        _accumulate(near_ref, near, first)
        _accumulate(far_ref, jnp.broadcast_to(far, far_ref.shape), first)

    near, far = pl.pallas_call(
        body, name=name, grid=(CHUNK // BIAS_ROWS,),
        in_specs=[pl.BlockSpec((BIAS_ROWS, N_HEADS, BAND), lambda i: (i, 0, 0))],
        out_specs=[pl.BlockSpec((N_HEADS, NEAR_REL), lambda i: (0, 0)), pl.BlockSpec((N_HEADS, 128), lambda i: (0, 0))],
        out_shape=[_out((N_HEADS, NEAR_REL), F32), _out((N_HEADS, 128), F32)],
        compiler_params=_params(1),
    )(*_hbm(dbias_t))
    return jnp.pad(near, ((0, 0), (1, 0))).at[:, N_REL - 1].add(far[:, 0])


def _ch_probs(scores, bias, valid):
    z = jnp.where(valid, scores * ATT_SCALE + bias, NEG_INF)
    e = jnp.exp(z - jnp.max(z, axis=-1, keepdims=True))
    return e / jnp.sum(e, axis=-1, keepdims=True)


CH_HEADS = [(pair, h) for pair in range(N_HEADS // 2) for h in range(2)]
CH_COLS = [slice(pair * PAIR, (pair + 1) * PAIR) for pair in range(N_HEADS // 2)]


CH_GROUP = 2
CH_Q = CH_GROUP * CHUNK
CH_WIN = (LOOKBACK + CH_GROUP) * CHUNK


def _ch_valid(n):
    row_chunk = lax.broadcasted_iota(jnp.int32, (CH_Q, CH_WIN), 0) // CHUNK
    slot = lax.broadcasted_iota(jnp.int32, (CH_Q, CH_WIN), 1)
    ahead = slot // CHUNK - row_chunk
    return (ahead >= 0) & (ahead <= LOOKBACK) & (n * CH_Q + slot >= PAD)


def _ch_group_bias(bias):
    shifted = [jnp.pad(bias, ((0, 0), (0, 0), (c * CHUNK, (CH_GROUP - 1 - c) * CHUNK))) for c in range(CH_GROUP)]
    return jnp.concatenate(shifted, axis=1)


def _ch_fold_bias_grad(dbias):
    parts = [dbias[:, c * CHUNK:(c + 1) * CHUNK, c * CHUNK:c * CHUNK + BAND] for c in range(CH_GROUP)]
    return sum(parts[1:], parts[0])


def _ch_fwd(qkv, bias, name):
    T = qkv.shape[0]
    W = N_HEADS * HEAD_DIM

    def body(q_ref, k_ref, v_ref, b_ref, o_ref, kp, vp):
        n = pl.program_id(0)

        @pl.when(n == 0)
        def _():
            _ch_load_padded(k_ref, v_ref, kp, vp)

        win = pl.ds(pl.multiple_of(n * CH_Q, CH_Q), CH_WIN)
        valid = _ch_valid(n)
        lanes = _pair_lanes()
        scores = [_dot(_only(lanes[h], q_ref[:, CH_COLS[pair]]), kp[win, CH_COLS[pair]], NT) for pair, h in CH_HEADS]
        probs = [_ch_probs(s, b_ref[2 * pair + h], valid).astype(BF16) for s, (pair, h) in zip(scores, CH_HEADS)]
        outs = [_dot(p, _only(lanes[h], vp[win, CH_COLS[pair]])) for p, (pair, h) in zip(probs, CH_HEADS)]
        for pair, cols in enumerate(CH_COLS):
            o_ref[:, cols] = outs[2 * pair] + outs[2 * pair + 1]

    full = lambda col: pl.BlockSpec((T, W), lambda n: (0, col))
    return pl.pallas_call(
        body, name=name, grid=(T // CH_Q,),
        in_specs=[pl.BlockSpec((CH_Q, W), lambda n: (n, 3)), full(4), full(5),
                  pl.BlockSpec((N_HEADS, CH_Q, CH_WIN), lambda n: (0, 0, 0))],
        out_specs=pl.BlockSpec((CH_Q, W), lambda n: (n, 0)),
        out_shape=_out((T, W), F32),
        scratch_shapes=[pltpu.VMEM((PAD + T, W), BF16)] * 2,
        compiler_params=_params(1, VMEM_LIMIT),
    )(*_hbm(qkv, qkv, qkv, bias))


def _ch_load_padded(k_ref, v_ref, kp, vp):
    for src, dst in ((k_ref, kp), (v_ref, vp)):
        dst[:PAD, :] = jnp.zeros((PAD, dst.shape[1]), dst.dtype)
        dst[PAD:, :] = src[...]


def _ch_bwd(qkv, bias, do, after, name):
    T = qkv.shape[0]
    W = N_HEADS * HEAD_DIM
    n_chunks = T // CH_Q

    def body(q_ref, k_ref, v_ref, b_ref, do_ref, dq_ref, dk_ref, dv_ref, db_ref, kp, vp, dk_s, dv_s):
        n = pl.program_id(0)

        @pl.when(n == 0)
        def _():
            _ch_load_padded(k_ref, v_ref, kp, vp)
            dk_s[...] = jnp.zeros_like(dk_s)
            dv_s[...] = jnp.zeros_like(dv_s)
            db_ref[...] = jnp.zeros_like(db_ref)

        win = pl.ds(pl.multiple_of(n * CH_Q, CH_Q), CH_WIN)
        valid = _ch_valid(n)
        lanes = _pair_lanes()
        kws = [kp[win, cols] for cols in CH_COLS]
        vws = [vp[win, cols] for cols in CH_COLS]
        qs = [_only(lanes[h], q_ref[:, CH_COLS[pair]]) for pair, h in CH_HEADS]
        dos = [_only(lanes[h], do_ref[:, CH_COLS[pair]].astype(BF16)) for pair, h in CH_HEADS]
        scores = [_dot(q, kws[pair], NT) for q, (pair, _) in zip(qs, CH_HEADS)]
        dps = [_dot(do, vws[pair], NT) for do, (pair, _) in zip(dos, CH_HEADS)]
        probs = [_ch_probs(s, b_ref[2 * pair + h], valid) for s, (pair, h) in zip(scores, CH_HEADS)]
        dzs = [p * (dp - jnp.sum(dp * p, axis=-1, keepdims=True)) for p, dp in zip(probs, dps)]
        for k, dz in enumerate(dzs):
            db_ref[k] += dz
        dzbs = [(dz * ATT_SCALE).astype(BF16) for dz in dzs]
        dqs = [_dot(dz, _only(lanes[h], kws[pair])) for dz, (pair, h) in zip(dzbs, CH_HEADS)]
        dks = [_dot(dz, q, TN) for dz, q in zip(dzbs, qs)]
        dvs = [_dot(p.astype(BF16), do, TN) for p, do in zip(probs, dos)]
        for pair, cols in enumerate(CH_COLS):
            dq_ref[:, cols] = (dqs[2 * pair] + dqs[2 * pair + 1]).astype(BF16)
            dk_s[win, cols] += dks[2 * pair] + dks[2 * pair + 1]
            dv_s[win, cols] += dvs[2 * pair] + dvs[2 * pair + 1]

        @pl.when(n == n_chunks - 1)
        def _():
            dk_ref[...] = dk_s[PAD:, :].astype(BF16)
            dv_ref[...] = dv_s[PAD:, :].astype(BF16)

    full = lambda col: pl.BlockSpec((T, W), lambda n: (0, col))
    blk = lambda col: pl.BlockSpec((CH_Q, W), lambda n: (n, col))
    tab = pl.BlockSpec((N_HEADS, CH_Q, CH_WIN), lambda n: (0, 0, 0))
    out = _out((T, W), BF16)
    return pl.pallas_call(
        lambda after_ref, *refs: body(*refs), name=name, grid=(n_chunks,),
        in_specs=[ANY, blk(3), full(4), full(5), tab, blk(0)],
        out_specs=[blk(0), full(0), full(0), tab],
        out_shape=[out, out, out, _out((N_HEADS, CH_Q, CH_WIN), F32)],
        scratch_shapes=[pltpu.VMEM((PAD + T, W), BF16)] * 2 + [pltpu.VMEM((PAD + T, W), F32)] * 2,
        compiler_params=_params(1, VMEM_LIMIT),
    )(after, *_hbm(qkv, qkv, qkv, bias, do))


def _rows_split(a, parts):
    return a.reshape(a.shape[:-2] + (parts, a.shape[-2] // parts, a.shape[-1]))


def _cast_into_own_slot(me, c, ws, in_chip_order, name):
    parts = 2
    ws = [_rows_split(_rows_split(w, 2), parts) for w in ws]
    n = len(ws)

    def body(me_ref, c_ref, *refs):
        for src, dst in zip(refs[:n], refs[n:]):
            dst[0, 0, 0] = src[0, 0].astype(BF16)

    def specs(w, plain):
        block = (1, 1) + w.shape[2:]
        if plain:
            return (pl.BlockSpec(block, lambda d, r, me_ref, c_ref: (d, r, 0, 0)),
                    pl.BlockSpec((1,) + block, lambda d, r, me_ref, c_ref: (me_ref[0], d, r, 0, 0)))
        return (pl.BlockSpec(block, lambda d, r, me_ref, c_ref: (d ^ c_ref[0], r, 0, 0)),
                pl.BlockSpec((1,) + block, lambda d, r, me_ref, c_ref: (0, d, r, 0, 0)))

    both = [specs(w, plain) for w, plain in zip(ws, in_chip_order)]
    outs = pl.pallas_call(
        body, name=name,
        grid_spec=pltpu.PrefetchScalarGridSpec(
            num_scalar_prefetch=2, grid=(2, parts),
            in_specs=[s[0] for s in both], out_specs=[s[1] for s in both]),
        out_shape=[_out((N_CHIPS,) + w.shape, BF16) for w in ws],
        compiler_params=_params(2, VMEM_LIMIT),
    )(me, c, *_hbm(*ws))
    return [o.reshape(N_CHIPS, 2, o.shape[2] * o.shape[3], o.shape[4]) for o in outs]


def _zone_slots(in_chip_order):
    x, y, c, _ = _place()
    me = 2 * x + y
    if in_chip_order:
        return (me, c), (lambda r: (me, c)), (lambda r: (me ^ r, c)), (lambda r: (me ^ r, c))
    return (0, 0), (lambda r: (r, 0)), (lambda r: (r, 0)), (lambda r: (r, 1))


def _pair_add(c, mine, got, permuted, name):
    parts = 2
    mine = [_rows_split(m, parts) for m in mine]
    got = [_rows_split(g, parts) for g in got]
    n = len(mine)

    def body(c_ref, *refs):
        for a, b, o in zip(refs[:n], refs[n:2 * n], refs[2 * n:]):
            o[0, 0] = (a[0, 0, 0] + b[0, 0].astype(F32)).astype(BF16)

    def mine_spec(m, perm):
        if perm:
            return pl.BlockSpec((1, 1, 1) + m.shape[3:], lambda j, r, c_ref: (j, 0, r, 0, 0))
        return pl.BlockSpec((1, 1, 1) + m.shape[3:], lambda j, r, c_ref: (j, c_ref[0], r, 0, 0))

    def got_spec(g):
        return pl.BlockSpec((1, 1) + g.shape[2:], lambda j, r, c_ref: (j, r, 0, 0))

    outs = pl.pallas_call(
        body, name=name,
        grid_spec=pltpu.PrefetchScalarGridSpec(
            num_scalar_prefetch=1, grid=(N_CHIPS, parts),
            in_specs=[mine_spec(m, perm) for m, perm in zip(mine, permuted)] + [got_spec(g) for g in got],
            out_specs=[got_spec(g) for g in got]),
        out_shape=[_out(g.shape, BF16) for g in got],
        compiler_params=_params(2, VMEM_LIMIT),
    )(c, *_hbm(*mine, *got))
    return [o.reshape(o.shape[0], o.shape[1] * o.shape[2], o.shape[3]) for o in outs]


def _chip_add(me, partials, landed, permuted, name):
    parts = 2
    ps = [_rows_split(x, parts) for x in partials]
    ls = [_rows_split(x, parts) for x in landed]
    n = len(ps)

    def body(me_ref, *refs):
        for own, got, o in zip(refs[:n], refs[n:2 * n], refs[2 * n:]):
            acc = own[0, 0].astype(F32)
            for r in range(N_CHIPS - 1):
                acc = acc + got[r, 0].astype(F32)
            o[0] = acc

    def own_spec(x, perm):
        if perm:
            return pl.BlockSpec((1, 1) + x.shape[2:], lambda r, me_ref: (0, r, 0, 0))
        return pl.BlockSpec((1, 1) + x.shape[2:], lambda r, me_ref: (me_ref[0], r, 0, 0))

    outs = pl.pallas_call(
        body, name=name,
        grid_spec=pltpu.PrefetchScalarGridSpec(
            num_scalar_prefetch=1, grid=(parts,),
            in_specs=[own_spec(x, perm) for x, perm in zip(ps, permuted)]
            + [pl.BlockSpec((N_CHIPS - 1, 1) + x.shape[2:], lambda r, me_ref: (0, r, 0, 0)) for x in ls],
            out_specs=[pl.BlockSpec((1,) + x.shape[2:], lambda r, me_ref: (r, 0, 0)) for x in ps]),
        out_shape=[_out(x.shape[1:], F32) for x in ps],
        compiler_params=_params(1, VMEM_LIMIT),
    )(me, *_hbm(*ps, *ls))
    return [o.reshape(o.shape[0] * o.shape[1], o.shape[2]) for o in outs]


def _adamw_math(w, g, m, v):
    m = ADAM_B1 * m + (1.0 - ADAM_B1) * g
    v = ADAM_B2 * v + (1.0 - ADAM_B2) * (g * g)
    m_hat = m / (1.0 - ADAM_B1 ** ADAM_STEP)
    v_hat = v / (1.0 - ADAM_B2 ** ADAM_STEP)
    delta = -ADAM_LR * (m_hat / (jnp.sqrt(v_hat) + ADAM_EPS) + ADAM_WD * w)
    return delta, m, v


def _adamw(ws, gs, ms, vs, parts, name):
    n = len(ws)
    flat = [_rows_split(a, parts) for a in (*ws, *gs, *ms, *vs)]

    def body(*refs):
        ins, outs = refs[:4 * n], refs[4 * n:]
        for k in range(n):
            d, m, v = _adamw_math(ins[k][...], ins[n + k][...], ins[2 * n + k][...], ins[3 * n + k][...])
            outs[k][...] = d
            outs[n + k][...] = m
            outs[2 * n + k][...] = v

    spec = lambda a: pl.BlockSpec((1,) + a.shape[1:], lambda i: (i, 0, 0))
    outs = pl.pallas_call(
        body, name=name, grid=(parts,),
        in_specs=[spec(a) for a in flat], out_specs=[spec(a) for a in flat[:n]] * 3,
        out_shape=[_out(a.shape, F32) for a in flat[:n]] * 3,
        compiler_params=_params(1, VMEM_LIMIT),
    )(*_hbm(*flat))
    outs = [o.reshape(o.shape[0] * o.shape[1], o.shape[2]) for o in outs]
    return outs[:n], outs[n:2 * n], outs[2 * n:]


def _adamw_halves(c, ws, owns, others, ms, vs, name):
    parts = 4
    n = len(ws)
    whole = [_rows_split(_rows_split(a, 2), parts) for a in (*ws, *ms, *vs)]
    halves = [_rows_split(a, parts) for a in (*owns, *others)]

    def body(c_ref, *refs):
        ins, outs = refs[:5 * n], refs[5 * n:]
        mine = pl.program_id(0) == c_ref[0]
        for k in range(n):
            g = jnp.where(mine, ins[3 * n + k][0], ins[4 * n + k][0])
            d, m, v = _adamw_math(ins[k][0, 0], g, ins[n + k][0, 0], ins[2 * n + k][0, 0])
            for slot, val in enumerate((g, d, m, v)):
                outs[slot * n + k][0, 0] = val

    wspec = lambda a: pl.BlockSpec((1, 1) + a.shape[2:], lambda h, r, c_ref: (h, r, 0, 0))
    hspec = lambda a: pl.BlockSpec((1,) + a.shape[1:], lambda h, r, c_ref: (r, 0, 0))
    outs = pl.pallas_call(
        body, name=name,
        grid_spec=pltpu.PrefetchScalarGridSpec(
            num_scalar_prefetch=1, grid=(2, parts),
            in_specs=[wspec(a) for a in whole] + [hspec(a) for a in halves],
            out_specs=[wspec(a) for a in whole[:n]] * 4),
        out_shape=[_out(a.shape, F32) for a in whole[:n]] * 4,
        compiler_params=_params(2, VMEM_LIMIT),
    )(c, *_hbm(*whole, *halves))
    outs = [o.reshape(2 * parts * o.shape[2], o.shape[3]) for o in outs]
    return outs[:n], outs[n:2 * n], outs[2 * n:3 * n], outs[3 * n:]


def _place():
    x, y, c = lax.axis_index("x"), lax.axis_index("y"), lax.axis_index("c")
    peers = [(x ^ (r >> 1), y ^ (r & 1), c) for r in (1, 2, 3)]
    return x, y, c, peers


def _handshake(peers):
    barrier = pltpu.get_barrier_semaphore()
    for peer in peers:
        pl.semaphore_signal(barrier, inc=1, device_id=peer, device_id_type=MESH)
    pl.semaphore_wait(barrier, len(peers))


ANY = pl.BlockSpec(memory_space=pl.ANY)
HBM = pl.BlockSpec(memory_space=pltpu.HBM)
SEM = pl.BlockSpec(memory_space=pltpu.SEMAPHORE)
SPLIT_COPY = pltpu.SideEffectType.DATAFLOW_SIDE_EFFECTING


def _split_start(body, name, collective_id, operands, n_sems, after=None):
    n = len(operands)
    extra = [] if after is None else [after]

    def wrapped(*refs):
        at = n + len(extra)
        body(refs[:n], refs[at], refs[at + 1])
        token = refs[-1]
        token[...] = jnp.zeros_like(token)

    outs = pl.pallas_call(
        wrapped, name=name,
        in_specs=[HBM] * n + [ANY] * len(extra),
        out_shape=(pltpu.SemaphoreType.DMA((n_sems,)), pltpu.SemaphoreType.DMA((n_sems,)),
                   *[pltpu.HBM(a.shape, a.dtype) for a in operands], jax.ShapeDtypeStruct((8, 128), F32)),
        out_specs=(SEM, SEM, *[HBM] * n, pl.BlockSpec(memory_space=pltpu.VMEM)),
        input_output_aliases={i: 2 + i for i in range(n)},
        compiler_params=pltpu.CompilerParams(has_side_effects=SPLIT_COPY, collective_id=collective_id),
    )(*_hbm(*operands), *extra)
    return outs[0], outs[1], list(outs[2:2 + n]), outs[-1]


def _split_wait(body, name, send_sem, recv_sem, operands, after):
    n = len(operands)

    def wrapped(*refs):
        body(refs[:n], refs[n], refs[n + 1])

    outs = pl.pallas_call(
        wrapped, name=name,
        in_specs=[HBM] * n + [SEM, SEM, ANY],
        out_shape=tuple(pltpu.HBM(a.shape, a.dtype) for a in operands),
        out_specs=tuple([HBM] * n),
        input_output_aliases={i: i for i in range(n)},
        compiler_params=pltpu.CompilerParams(has_side_effects=SPLIT_COPY),
    )(*operands, send_sem, recv_sem, after)
    return list(outs)


def _gather_copies(lands, in_chip_order, send_sem, recv_sem):
    peers = _place()[3]
    copies = []
    for a, (land, plain) in enumerate(zip(lands, in_chip_order)):
        own, sent_to, _, _ = _zone_slots(plain)
        copies += [pltpu.make_async_remote_copy(
            src_ref=land.at[own], dst_ref=land.at[sent_to(r + 1)],
            send_sem=send_sem.at[a * 3 + r], recv_sem=recv_sem.at[a * 3 + r],
            device_id=peers[r], device_id_type=MESH) for r in range(3)]
    return copies


def _gather_start(lands, in_chip_order, name, collective_id, after):
    def body(refs, send_sem, recv_sem):
        _handshake(_place()[3])
        for cp in _gather_copies(refs, in_chip_order, send_sem, recv_sem):
            cp.start()

    return _split_start(body, name, collective_id, list(lands), 3 * len(lands), after)


def _gather_wait(send_sem, recv_sem, operands, in_chip_order, after, name):
    def body(refs, send_sem, recv_sem):
        for cp in _gather_copies(refs, in_chip_order, send_sem, recv_sem):
            cp.wait_send()
            cp.wait_recv()

    return _split_wait(body, name, send_sem, recv_sem, operands, after)


def _gather_finish(lands, in_chip_order, with_ici, name):
    n = len(lands)

    def body(*refs):
        land = refs[n:2 * n]
        send_ici, recv_ici, send_d2d, recv_d2d = refs[2 * n:]
        x, y, c, _ = _place()
        ici = _gather_copies(land, in_chip_order, send_ici, recv_ici) if with_ici else []
        for cp in ici:
            cp.start()
        passed = []
        for a in range(n):
            _, _, received, kept = _zone_slots(in_chip_order[a])
            passed += [pltpu.make_async_remote_copy(
                src_ref=land[a].at[received(r + 1)], dst_ref=land[a].at[kept(r + 1)],
                send_sem=send_d2d.at[a * 3 + r], recv_sem=recv_d2d.at[a * 3 + r],
                device_id=(x, y, 1 - c), device_id_type=MESH) for r in range(3)]
        for k, cp in enumerate(passed):
            if with_ici:
                ici[k].wait_recv()
            cp.start()
        for cp in passed:
            cp.wait_recv()
        for cp in ici:
            cp.wait_send()
        for cp in passed:
            cp.wait_send()

    outs = pl.pallas_call(
        body, name=name,
        in_specs=[ANY] * n, out_specs=[ANY] * n,
        out_shape=[_out(l.shape, l.dtype) for l in lands],
        input_output_aliases={a: a for a in range(n)},
        scratch_shapes=[pltpu.SemaphoreType.DMA((3 * n,))] * 4,
    )(*lands)
    return list(outs)


def _pass_copies(lands, in_chip_order, send_sem, recv_sem):
    x, y, c, _ = _place()
    copies = []
    for a, (land, plain) in enumerate(zip(lands, in_chip_order)):
        _, _, received, kept = _zone_slots(plain)
        copies += [pltpu.make_async_remote_copy(
            src_ref=land.at[received(r + 1)], dst_ref=land.at[kept(r + 1)],
            send_sem=send_sem.at[a * 3 + r], recv_sem=recv_sem.at[a * 3 + r],
            device_id=(x, y, 1 - c), device_id_type=MESH) for r in range(3)]
    return copies


def _pass_start(lands, in_chip_order, name, collective_id):
    def body(refs, send_sem, recv_sem):
        x, y, c, _ = _place()
        _handshake([(x, y, 1 - c)])
        for cp in _pass_copies(refs, in_chip_order, send_sem, recv_sem):
            cp.start()

    return _split_start(body, name, collective_id, list(lands), 3 * len(lands))


def _pass_wait(send_sem, recv_sem, lands, in_chip_order, after, name):
    def body(refs, send_sem, recv_sem):
        for cp in _pass_copies(refs, in_chip_order, send_sem, recv_sem):
            cp.wait_send()
            cp.wait_recv()

    return _split_wait(body, name, send_sem, recv_sem, lands, after)


def _slabs(land):
    return land.reshape(N_CHIPS, 2 * land.shape[2], land.shape[3])


def _pair_swap(grads, permuted, name):
    n = len(grads)

    def body(*refs):
        src, dst = refs[:n], refs[n:2 * n]
        send_sem, recv_sem = refs[2 * n:]
        x, y, c, _ = _place()
        copies = [pltpu.make_async_remote_copy(
            src_ref=src[a].at[:, 1] if permuted[a] else src[a].at[:, 1 - c], dst_ref=dst[a],
            send_sem=send_sem.at[a], recv_sem=recv_sem.at[a],
            device_id=(x, y, 1 - c), device_id_type=MESH) for a in range(n)]
        for cp in copies:
            cp.start()
        for cp in copies:
            cp.wait()

    return pl.pallas_call(
        body, name=name,
        in_specs=[ANY] * n, out_specs=[ANY] * n,
        out_shape=[_out((N_CHIPS,) + g.shape[2:], g.dtype) for g in grads],
        scratch_shapes=[pltpu.SemaphoreType.DMA((n,))] * 2,
    )(*grads)


def _swap_copies(refs, permuted, send_sem, recv_sem):
    n = len(refs) // 2
    x, y, c, _ = _place()
    return [pltpu.make_async_remote_copy(
        src_ref=refs[a].at[:, 1] if permuted[a] else refs[a].at[:, 1 - c], dst_ref=refs[n + a],
        send_sem=send_sem.at[a], recv_sem=recv_sem.at[a],
        device_id=(x, y, 1 - c), device_id_type=MESH) for a in range(n)]


def _pair_swap_start(grads, permuted, name, collective_id):
    def body(refs, send_sem, recv_sem):
        x, y, c, _ = _place()
        _handshake([(x, y, 1 - c)])
        for cp in _swap_copies(refs, permuted, send_sem, recv_sem):
            cp.start()

    lands = [lax.empty((N_CHIPS,) + g.shape[2:], g.dtype) for g in grads]
    return _split_start(body, name, collective_id, list(grads) + lands, len(grads))


def _pair_swap_wait(send_sem, recv_sem, operands, permuted, after, name):
    def body(refs, send_sem, recv_sem):
        for cp in _swap_copies(refs, permuted, send_sem, recv_sem):
            cp.wait_send()
            cp.wait_recv()

    return _split_wait(body, name, send_sem, recv_sem, operands, after)


def _scatter_copies(refs, permuted, send_sem, recv_sem):
    n = len(refs) // 2
    x, y, _, peers = _place()
    me = 2 * x + y
    return [pltpu.make_async_remote_copy(
        src_ref=refs[a].at[r + 1] if permuted[a] else refs[a].at[me ^ (r + 1)], dst_ref=refs[n + a].at[r],
        send_sem=send_sem.at[a * 3 + r], recv_sem=recv_sem.at[a * 3 + r],
        device_id=peers[r], device_id_type=MESH) for a in range(n) for r in range(3)]


def _scatter_start(partials, permuted, name, collective_id):
    def body(refs, send_sem, recv_sem):
        _handshake(_place()[3])
        for cp in _scatter_copies(refs, permuted, send_sem, recv_sem):
            cp.start()

    lands = [lax.empty((N_CHIPS - 1,) + p.shape[1:], p.dtype) for p in partials]
    return _split_start(body, name, collective_id, list(partials) + lands, 3 * len(partials))


def _scatter_wait(send_sem, recv_sem, operands, permuted, after, name):
    def body(refs, send_sem, recv_sem):
        for cp in _scatter_copies(refs, permuted, send_sem, recv_sem):
            cp.wait_send()
            cp.wait_recv()

    return _split_wait(body, name, send_sem, recv_sem, operands, after)


def _pair_join(halves, name):
    n = len(halves)

    def body(*refs):
        src, dst = refs[:n], refs[n:2 * n]
        send_sem, recv_sem = refs[2 * n:]
        x, y, c, _ = _place()
        copies = [pltpu.make_async_remote_copy(
            src_ref=src[a], dst_ref=dst[a], send_sem=send_sem.at[a], recv_sem=recv_sem.at[a],
            device_id=(x, y, 1 - c), device_id_type=MESH) for a in range(n)]
        for cp in copies:
            cp.start()
        for cp in copies:
            cp.wait()

    return pl.pallas_call(
        body, name=name,
        in_specs=[ANY] * n, out_specs=[ANY] * n,
        out_shape=[_out(h.shape, F32) for h in halves],
        scratch_shapes=[pltpu.SemaphoreType.DMA((n,))] * 2,
    )(*halves)


def _all_sum_small(vs, after, name):
    rows = [v.shape[0] for v in vs]
    n, R, C = len(vs), sum(rows), vs[0].shape[1]
    n_dev = 8

    def body(*refs):
        after_ref, o_ref, mine, buf, send_sem, recv_sem = refs[n:]
        x, y, c, _ = _place()
        me = 4 * x + 2 * y + c
        at = 0
        for v_ref, r in zip(refs[:n], rows):
            mine[at:at + r] = v_ref[...]
            at += r
        buf[me] = mine[...]
        copies = []
        for k in range(1, n_dev):
            peer = (x ^ (k >> 2), y ^ ((k >> 1) & 1), c ^ (k & 1))
            copies.append(pltpu.make_async_remote_copy(
                src_ref=mine, dst_ref=buf.at[me], send_sem=send_sem.at[k - 1], recv_sem=recv_sem.at[k - 1],
                device_id=peer, device_id_type=MESH))
        for cp in copies:
            cp.start()
        for cp in copies:
            cp.wait()
        acc = buf[0]
        for m in range(1, n_dev):
            acc = acc + buf[m]
        o_ref[...] = acc

    return pl.pallas_call(
        body, name=name,
        in_specs=[pl.BlockSpec(memory_space=pltpu.VMEM)] * n + [ANY], out_specs=pl.BlockSpec(memory_space=pltpu.VMEM),
        out_shape=jax.ShapeDtypeStruct((R, C), F32),
        scratch_shapes=[pltpu.VMEM((R, C), F32), pltpu.VMEM((n_dev, R, C), F32),
                        pltpu.SemaphoreType.DMA((n_dev - 1,)), pltpu.SemaphoreType.DMA((n_dev - 1,))],
    )(*vs, after)


class _WholeWeights:
    def __init__(self, w):
        self.w = w

    def weights(self, group, after=None):
        return ({} if group == "passed" else self.w), None

    def grads_ready(self, group, gw):
        return None

    def grads_sent(self, group, after):
        return None


def _local_step(x, p, target, gains, rel_bias, hooks):
    T, D = x.shape
    S = N_CHIPS

    tied = lambda gain, token: gain if token is None else gain + token[0, 0]
    w, token = hooks.weights("first")
    w = dict(w)
    xn1, g1, u1, a1 = _ffn_up(x, tied(gains["ffn1_pre"], token), w["ffn1_gate"], w["ffn1_up"], "ffn1_up")
    w.update(hooks.weights("down", a1)[0])
    h1, f1 = _ffn_down(x, a1, gains["ffn1_post"], w["ffn1_down"], "ffn1_down")
    more, token = hooks.weights("in", h1)
    w.update(more)
    qkv, un = _norm_proj(h1, tied(gains["mix_pre"], token), w["in"], "qkv_proj")
    bias = _ch_group_bias(_bias_table(rel_bias, "bias_table").transpose(1, 0, 2))
    o_a = _sb_fwd(qkv, "sb_fwd")
    o_b = _ch_fwd(qkv, bias, "ch_fwd")
    more, token = hooks.weights("rest", o_b)
    w.update(more)
    w_out = w["out"].reshape(D, D)
    h2, mixed, mo = _mix_out_fwd(h1, o_a, o_b, gains["out_sb"], gains["out_ch"], w_out,
                                 tied(gains["mix_post"], token), "mix_out_fwd")
    w.update(hooks.weights("passed", h2)[0])
    h3, xn2, g2, u2, a2, f2 = _ffn_fwd(h2, gains["ffn2_pre"], gains["ffn2_post"], w["ffn2_gate"], w["ffn2_up"],
                                       w["ffn2_down"], "ffn2_fwd")
    w_ple_proj = w["ple_proj"].transpose(1, 0, 2).reshape(p.shape[1], D)
    w_ple_gate = w["ple_gate"].reshape(D, D)

    loss, dh3, dproj, dgate, dg_ple = _ple_loss(h3, p, target, w_ple_proj, w_ple_gate, gains["ple_post"], "ple_loss")
    gw, gg = {}, {"ple_post": dg_ple}
    gw["ple_proj"] = _mm_tn(p[None], dproj, p.shape[1], "dw_ple_proj")
    row_sharded = lambda pair: tuple(o.reshape(S, D // S, D) for o in pair)
    gw["ple_gate"] = row_sharded(_mm_tn(h3[None], dgate[None], 512, "dw_ple_gate"))

    def ffn_bwd(tag, dh, x_in, xn, g_act, u_act, a_act, f, group):
        dgp, dup, df, gg[tag + "_post"] = _ffn_bwd_act(dh, f, gains[tag + "_post"], w[tag + "_down"], g_act, u_act,
                                                       tag + "_bwd_act")
        gw[tag + "_gate"] = _mm_tn(dgp, xn[None], dgp.shape[2], "dw_" + tag + "_gate")
        gw[tag + "_up"] = _mm_tn(dup, xn[None], dup.shape[2], "dw_" + tag + "_up")
        gw[tag + "_down"] = _mm_tn(a_act, df[None], a_act.shape[2], "dw_" + tag + "_down")
        g_pre = gains[tag + "_pre"]
        if group is not None:
            token = hooks.grads_ready(group, gw)
            g_pre = g_pre if token is None else g_pre + token[0, 0]
        dx, gg[tag + "_pre"] = _proj_bwd([dgp, dup], [w[tag + "_gate"], w[tag + "_up"]], x_in, g_pre, dh,
                                         tag + "_bwd_in")
        return dx

    dh2 = ffn_bwd("ffn2", dh3, h2, xn2, g2, u2, a2, f2, None)
    dmo, do_a, do_b, gg["mix_post"], gg["out_sb"], gg["out_ch"] = _mix_out_bwd(
        dh2, mo, gains["mix_post"], w_out, o_a, o_b, gains["out_sb"], gains["out_ch"], "mix_out_bwd")
    gw["out"] = row_sharded(_mm_tn(mixed[None], dmo[None], 512, "dw_out"))
    dq_a, dk_a, dv_a = _sb_bwd(qkv, do_a, o_a, do_a, "sb_bwd")
    dq_b, dk_b, dv_b, dbias = _ch_bwd(qkv, bias, do_b, do_b, "ch_bwd")
    g_rel = _bias_grad(_ch_fold_bias_grad(dbias).transpose(1, 0, 2), "bias_grad")
    dqkv = [dq_a, dk_a, dv_a, dq_b, dk_b, dv_b]
    gw["in"] = _dw_in(un, dqkv, w["in"].shape[2], 512, "dw_in")
    token = hooks.grads_ready("early", gw)
    dh1, gg["mix_pre"] = _qkv_bwd_in(dqkv, w["in"], h1, tied(gains["mix_pre"], token), dh2, "qkv_bwd_in")
    gains = {**gains, "ffn1_post": tied(gains["ffn1_post"], hooks.grads_sent("early", dh1))}
    dx = ffn_bwd("ffn1", dh1, x, xn1, g1, u1, a1, f1, "late")
    return loss, dx, gw, gg, g_rel


BIG = ["ffn1_gate", "ffn1_up", "ffn1_down", "in", "out", "ffn2_gate", "ffn2_up", "ffn2_down", "ple_proj", "ple_gate"]
GAINS = ["ffn1_pre", "ffn1_post", "mix_pre", "mix_post", "out_sb", "out_ch", "ffn2_pre", "ffn2_post", "ple_post"]
TRANSPOSED = ("w_ffn1_gate", "w_ffn1_up", "w_ffn2_gate", "w_ffn2_up")
PERMUTED = ("ffn1_gate", "ffn1_up", "ffn1_down", "ffn2_gate", "ffn2_up", "ffn2_down")
W_GROUPS = {"first": ["ffn1_gate", "ffn1_up"], "down": ["ffn1_down"], "in": ["in"],
            "rest": ["out", "ffn2_gate", "ffn2_up", "ffn2_down", "ple_proj", "ple_gate"]}
G_GROUPS = {"early": ["ple_proj", "ple_gate", "ffn2_gate", "ffn2_up", "ffn2_down", "out", "in"],
            "late": ["ffn1_gate", "ffn1_up", "ffn1_down"]}
ORDER = ["g_ffn1_pre", "g_ffn1_post", "w_ffn1_gate", "w_ffn1_up", "w_ffn1_down", "g_mix_pre", "g_mix_post", "w_in",
         "g_out_sb", "g_out_ch", "rel_bias", "w_out", "g_ffn2_pre", "g_ffn2_post", "w_ffn2_gate", "w_ffn2_up",
         "w_ffn2_down", "w_ple_proj", "w_ple_gate", "g_ple_post"]


def kernel(x, p, g_ffn1_pre, g_ffn1_post, w_ffn1_gate, w_ffn1_up, w_ffn1_down, g_mix_pre, g_mix_post, w_in, g_out_sb, g_out_ch, rel_bias, w_out, g_ffn2_pre, g_ffn2_post, w_ffn2_gate, w_ffn2_up, w_ffn2_down, w_ple_proj, w_ple_gate, g_ple_post, loss_target, m_g_ffn1_pre, m_g_ffn1_post, m_w_ffn1_gate, m_w_ffn1_up, m_w_ffn1_down, m_g_mix_pre, m_g_mix_post, m_w_in, m_g_out_sb, m_g_out_ch, m_rel_bias, m_w_out, m_g_ffn2_pre, m_g_ffn2_post, m_w_ffn2_gate, m_w_ffn2_up, m_w_ffn2_down, m_w_ple_proj, m_w_ple_gate, m_g_ple_post, v_g_ffn1_pre, v_g_ffn1_post, v_w_ffn1_gate, v_w_ffn1_up, v_w_ffn1_down, v_g_mix_pre, v_g_mix_post, v_w_in, v_g_out_sb, v_g_out_ch, v_rel_bias, v_w_out, v_g_ffn2_pre, v_g_ffn2_post, v_w_ffn2_gate, v_w_ffn2_up, v_w_ffn2_down, v_w_ple_proj, v_w_ple_gate, v_g_ple_post):
    args = dict(locals())
    take = lambda a, n: a[0].T if n in TRANSPOSED else a[0]
    wts = {n: take(args[n], n) for n in ORDER}
    ms = {n: take(args["m_" + n], n) for n in ORDER}
    vs = {n: take(args["v_" + n], n) for n in ORDER}
    gains = {n: wts["g_" + n][None] for n in GAINS}

    c_idx = lax.axis_index("c").astype(jnp.int32).reshape(1)
    me_idx = (2 * lax.axis_index("x") + lax.axis_index("y")).astype(jnp.int32).reshape(1)
    south = lax.axis_index("c") == 0

    plain = lambda names: [n not in PERMUTED for n in names]
    lands = dict(zip(BIG, _cast_into_own_slot(me_idx, c_idx, [wts["w_" + n] for n in BIG], plain(BIG), "cast_weights")))

    class Overlapped:
        def __init__(self):
            self.started = {}
            self.flying = {}

        def start(self, group, collective_id, after):
            names = W_GROUPS[group]
            self.flying[group] = _gather_start([lands[n] for n in names], plain(names), "gather_%s_start" % group,
                                               collective_id, after)
            return self.flying[group][3]

        def weights(self, group, after=None):
            names = W_GROUPS.get(group)
            token = None
            if group == "first":
                zones = _gather_finish([lands[n] for n in names], plain(names), True, "gather_first")
                token = self.start("rest", 4, self.start("in", 1, self.start("down", 6, zones[0])))
            elif group == "passed":
                names, (send_sem, recv_sem, zones, _) = self.passing
                zones = _pass_wait(send_sem, recv_sem, zones, plain(names), after, "gather_rest_pass_wait")
            else:
                send_sem, recv_sem, zones, _ = self.flying[group]
                zones = _gather_wait(send_sem, recv_sem, zones, plain(names), after, "gather_%s_wait" % group)
                if group == "rest":
                    self.passing = names[1:], _pass_start(zones[1:], plain(names[1:]), "gather_rest_pass_start", 7)
                    names, zones, token = names[:1], zones[:1], self.passing[1][3]
                zones = _gather_finish(zones, plain(names), False, "gather_%s_finish" % group)
            return {n: _slabs(z) for n, z in zip(names, zones)}, token

        def grads_ready(self, group, gw):
            names = G_GROUPS[group]
            perm = [n in PERMUTED for n in names]
            halved = lambda g: g.reshape(N_CHIPS, 2, g.shape[1] // 2, g.shape[2])
            mine = [halved(gw[n][0]) for n in names]
            narrow = [halved(gw[n][1]) for n in names]
            if group == "late":
                return self.scatter(group, names, perm, mine, _pair_swap(narrow, perm, "grad_pair_swap_late"))
            self.swapping = names, perm, mine, _pair_swap_start(narrow, perm, "grad_pair_swap_start_early", 5)
            return self.swapping[3][3]

        def grads_sent(self, group, after):
            names, perm, mine, (send_sem, recv_sem, operands, _) = self.swapping
            operands = _pair_swap_wait(send_sem, recv_sem, operands, perm, after, "grad_pair_swap_wait_early")
            return self.scatter(group, names, perm, mine, operands[len(names):])

        def scatter(self, group, names, perm, mine, got):
            partial = _pair_add(c_idx, mine, got, perm, "grad_pair_add_" + group)
            send_sem, recv_sem, operands, token = _scatter_start(partial, perm, "grad_scatter_start_" + group,
                                                                 {"early": 2, "late": 3}[group])
            self.started[group] = names, perm, send_sem, recv_sem, operands, token
            return token

    def reduce_finish(state, after, tag):
        names, perm, send_sem, recv_sem, operands, _ = state
        operands = _scatter_wait(send_sem, recv_sem, operands, perm, after, "grad_scatter_wait_" + tag)
        n = len(names)
        own = _chip_add(me_idx, operands[:n], operands[n:], perm, "grad_chip_add_" + tag)
        return own, _pair_join(own, "grad_pair_join_" + tag)

    hooks = Overlapped()
    loss, dx, gw, gg, g_rel = _local_step(x[0], p[0, 0], loss_target[0], gains, wts["rel_bias"], hooks)

    grads, delta, new_m, new_v = {}, {}, {}, {}

    def finish(group, after):
        own, other = reduce_finish(hooks.started[group], after, group)
        names = ["w_" + n for n in G_GROUPS[group]]
        g, d, m, v = _adamw_halves(c_idx, [wts[n] for n in names], own, other, [ms[n] for n in names],
                                   [vs[n] for n in names], "adamw_" + group)
        for n, gg_, dd, mm, vv in zip(names, g, d, m, v):
            grads[n], delta[n], new_m[n], new_v[n] = gg_, dd, mm, vv
        return d[0]

    early_done = finish("early", dx)

    pieces = [gg[n].reshape(-1, 128) for n in GAINS] + [jnp.pad(g_rel, ((0, 0), (0, N_REL_PAD - N_REL))).reshape(-1, 128)]
    summed = _all_sum_small(pieces + [loss], early_done, "small_grad_sum")
    finish("late", summed)
    at = 0
    for n, piece in zip(GAINS, pieces[:-1]):
        grads["g_" + n] = summed[at:at + piece.shape[0]].reshape(1, -1)[0]
        at += piece.shape[0]
    grads["rel_bias"] = summed[at:at + pieces[-1].shape[0]].reshape(N_HEADS, N_REL_PAD)[:, :N_REL]
    loss = summed[at + pieces[-1].shape[0], 0]

    small = ["g_" + n for n in GAINS] + ["rel_bias"]
    as_rows = lambda a: (a.reshape(-1, 128) if a.size % 128 == 0 else jnp.pad(a, ((0, 0), (0, N_REL_PAD - N_REL))).reshape(-1, 128))
    d, m, v = _adamw([as_rows(wts[n]) for n in small], [as_rows(grads[n]) for n in small],
                     [as_rows(ms[n]) for n in small], [as_rows(vs[n]) for n in small], 1, "adamw_small")
    for n, dd, mm, vv in zip(small, d, m, v):
        back = (lambda a: a.reshape(N_HEADS, N_REL_PAD)[:, :N_REL]) if n == "rel_bias" else (lambda a: a.reshape(-1))
        delta[n], new_m[n], new_v[n] = back(dd), back(mm), back(vv)

    outs = [loss, dx[None]]
    for table in (grads, delta, new_m, new_v):
        outs += [(table[n].T if n in TRANSPOSED else table[n])[None] for n in ORDER]
    return tuple(outs)
```

```python
import jax
import jax.numpy as jnp
from jax import lax
from jax.experimental import pallas as pl
from jax.experimental.pallas import tpu as pltpu

F32 = jnp.float32
BF16 = jnp.bfloat16
EPS = 1e-6
N_CHIPS = 4
HEAD_DIM = 64
N_HEADS = 8
CHUNK = 64
LOOKBACK = 8
BAND = (LOOKBACK + 1) * CHUNK
PAD = LOOKBACK * CHUNK
REL_CLIP = 128
N_REL = 2 * REL_CLIP + 1
N_REL_PAD = 384
SB_BLOCK = 256
PAIR = 2 * HEAD_DIM
SB_PAIRS = 2
SB_FWD_PAIRS = 4
ATT_SCALE = HEAD_DIM ** -0.5
NEG_INF = -1e30
ROW_BLOCK = 512
WIDE_ROW_BLOCK = 1024
VMEM_LIMIT_WIDE = 56 * 1024 * 1024
VMEM_LIMIT = 48 * 1024 * 1024
MESH = pl.DeviceIdType.MESH

ADAM_LR = 0.001
ADAM_B1 = 0.9
ADAM_B2 = 0.999
ADAM_EPS = 1e-08
ADAM_WD = 0.01
ADAM_STEP = 10

NT = (((1,), (1,)), ((), ()))
TN = (((0,), (0,)), ((), ()))


def _params(n_grid, vmem=None):
    return pltpu.CompilerParams(dimension_semantics=("arbitrary",) * n_grid, vmem_limit_bytes=vmem)


def _hbm(*arrays):
    return [pltpu.with_memory_space_constraint(a, pltpu.HBM) for a in arrays]


def _out(shape, dtype):
    return pltpu.HBM(shape, dtype)


def _dot(a, b, dims=None):
    if dims is None:
        return jnp.dot(a, b, preferred_element_type=F32)
    return lax.dot_general(a, b, dims, preferred_element_type=F32)


def _sigmoid(x):
    return 1.0 / (1.0 + jnp.exp(-x))


def _rms_fwd(x, g):
    r = lax.rsqrt(jnp.mean(x * x, axis=-1, keepdims=True) + EPS)
    return x * r * g


def _rms_bwd(x, g, dy):
    r = lax.rsqrt(jnp.mean(x * x, axis=-1, keepdims=True) + EPS)
    xh = x * r
    dg = jnp.sum(dy * xh, axis=0, keepdims=True)
    t = dy * g
    dx = r * (t - xh * jnp.mean(t * xh, axis=-1, keepdims=True))
    return dx, dg


def _accumulate(ref, val, first):
    @pl.when(first)
    def _():
        ref[...] = val

    @pl.when(jnp.logical_not(first))
    def _():
        ref[...] += val


def _split2(x):
    hi = x.astype(BF16)
    lo = (x - hi.astype(F32)).astype(BF16)
    return hi, lo


def _ffn_fwd(x, g_pre, g_post, wg, wu, wd, name):
    T, D = x.shape
    S, FS, _ = wg.shape
    tm = min(WIDE_ROW_BLOCK, T)

    def body(x_ref, gpre_ref, gpost_ref, wg_ref, wu_ref, wd_ref,
             h_ref, xn_ref, g_ref, u_ref, a_ref, f_ref):
        k = pl.program_id(1)

        @pl.when(k == 0)
        def _():
            xn_ref[...] = _rms_fwd(x_ref[...], gpre_ref[...]).astype(BF16)

        xn = xn_ref[...]
        g = _dot(xn, wg_ref[0], NT)
        u = _dot(xn, wu_ref[0], NT)
        g_ref[0] = g
        u_ref[0] = u
        a = (g * _sigmoid(g) * u).astype(BF16)
        a_ref[0] = a
        _accumulate(f_ref, _dot(a, wd_ref[0]), k == 0)

        @pl.when(k == S - 1)
        def _():
            h_ref[...] = x_ref[...] + 0.5 * _rms_fwd(f_ref[...], gpost_ref[...])

    row = pl.BlockSpec((tm, D), lambda i, k: (i, 0))
    vec = pl.BlockSpec((1, D), lambda i, k: (0, 0))
    act = pl.BlockSpec((1, tm, FS), lambda i, k: (k, i, 0))
    return pl.pallas_call(
        body, name=name, grid=(T // tm, S),
        in_specs=[row, vec, vec] + [pl.BlockSpec((1, FS, D), lambda i, k: (k, 0, 0))] * 3,
        out_specs=[row, row, act, act, act, row],
        out_shape=[_out((T, D), F32), _out((T, D), BF16),
                   _out((S, T, FS), F32), _out((S, T, FS), F32),
                   _out((S, T, FS), BF16), _out((T, D), F32)],
        compiler_params=_params(2, VMEM_LIMIT_WIDE),
    )(*_hbm(x, g_pre, g_post, wg, wu, wd))


def _ffn_up(x, g_pre, wg, wu, name):
    T, D = x.shape
    S, FS, _ = wg.shape
    tm = min(WIDE_ROW_BLOCK, T)

    def body(x_ref, gpre_ref, wg_ref, wu_ref, xn_ref, g_ref, u_ref, a_ref):
        @pl.when(pl.program_id(1) == 0)
        def _():
            xn_ref[...] = _rms_fwd(x_ref[...], gpre_ref[...]).astype(BF16)

        xn = xn_ref[...]
        g = _dot(xn, wg_ref[0], NT)
        u = _dot(xn, wu_ref[0], NT)
        g_ref[0] = g
        u_ref[0] = u
        a_ref[0] = (g * _sigmoid(g) * u).astype(BF16)

    row = pl.BlockSpec((tm, D), lambda i, k: (i, 0))
    act = pl.BlockSpec((1, tm, FS), lambda i, k: (k, i, 0))
    return pl.pallas_call(
        body, name=name, grid=(T // tm, S),
        in_specs=[row, pl.BlockSpec((1, D), lambda i, k: (0, 0))] + [pl.BlockSpec((1, FS, D), lambda i, k: (k, 0, 0))] * 2,
        out_specs=[row, act, act, act],
        out_shape=[_out((T, D), BF16), _out((S, T, FS), F32), _out((S, T, FS), F32), _out((S, T, FS), BF16)],
        compiler_params=_params(2, VMEM_LIMIT_WIDE),
    )(*_hbm(x, g_pre, wg, wu))


def _ffn_down(x, a, g_post, wd, name):
    T, D = x.shape
    S, FS, _ = wd.shape
    tm = min(WIDE_ROW_BLOCK, T)

    def body(x_ref, a_ref, gpost_ref, wd_ref, h_ref, f_ref):
        k = pl.program_id(1)
        _accumulate(f_ref, _dot(a_ref[0], wd_ref[0]), k == 0)

        @pl.when(k == S - 1)
        def _():
            h_ref[...] = x_ref[...] + 0.5 * _rms_fwd(f_ref[...], gpost_ref[...])

    row = pl.BlockSpec((tm, D), lambda i, k: (i, 0))
    return pl.pallas_call(
        body, name=name, grid=(T // tm, S),
        in_specs=[row, pl.BlockSpec((1, tm, FS), lambda i, k: (k, i, 0)), pl.BlockSpec((1, D), lambda i, k: (0, 0)),
                  pl.BlockSpec((1, FS, D), lambda i, k: (k, 0, 0))],
        out_specs=[row, row],
        out_shape=[_out((T, D), F32), _out((T, D), F32)],
        compiler_params=_params(2, VMEM_LIMIT_WIDE),
    )(*_hbm(x, a, g_post, wd))


def _ffn_bwd_act(dh, f, g_post, wd, g_act, u_act, name):
    T, D = dh.shape
    S, FS, _ = wd.shape
    tm = min(WIDE_ROW_BLOCK, T)

    def body(dh_ref, f_ref, gpost_ref, wd_ref, g_ref, u_ref, dgp_ref, dup_ref, df_ref, dgain_ref, df_s):
        i, k = pl.program_id(0), pl.program_id(1)

        @pl.when(k == 0)
        def _():
            df, dgain = _rms_bwd(f_ref[...], gpost_ref[...], 0.5 * dh_ref[...])
            df_s[...] = df.astype(BF16)
            df_ref[...] = df_s[...]
            _accumulate(dgain_ref, dgain, i == 0)

        da = _dot(df_s[...], wd_ref[0], NT)
        g = g_ref[0]
        s = _sigmoid(g)
        dup_ref[0] = (da * (g * s)).astype(BF16)
        dgp_ref[0] = (da * u_ref[0] * (s * (1.0 + g * (1.0 - s)))).astype(BF16)

    row = pl.BlockSpec((tm, D), lambda i, k: (i, 0))
    vec = pl.BlockSpec((1, D), lambda i, k: (0, 0))
    act = pl.BlockSpec((1, tm, FS), lambda i, k: (k, i, 0))
    return pl.pallas_call(
        body, name=name, grid=(T // tm, S),
        in_specs=[row, row, vec, pl.BlockSpec((1, FS, D), lambda i, k: (k, 0, 0)), act, act],
        out_specs=[act, act, row, vec],
        out_shape=[_out((S, T, FS), BF16), _out((S, T, FS), BF16),
                   _out((T, D), BF16), _out((1, D), F32)],
        scratch_shapes=[pltpu.VMEM((tm, D), BF16)],
        compiler_params=_params(2, VMEM_LIMIT_WIDE),
    )(*_hbm(dh, f, g_post, wd, g_act, u_act))


def _proj_bwd(dys, ws, x, g_pre, dh, name):
    T, D = x.shape
    n = len(dys)
    S, N, _ = ws[0].shape
    tm = min(WIDE_ROW_BLOCK, T)

    def body(*refs):
        dy_refs, w_refs = refs[:n], refs[n:2 * n]
        x_ref, gpre_ref, dh_ref, dx_ref, dgain_ref, acc_s = refs[2 * n:]
        i, k = pl.program_id(0), pl.program_id(1)
        part = None
        for dy_ref, w_ref in zip(dy_refs, w_refs):
            term = _dot(dy_ref[0], w_ref[0])
            part = term if part is None else part + term
        _accumulate(acc_s, part, k == 0)

        @pl.when(k == S - 1)
        def _():
            dx, dgain = _rms_bwd(x_ref[...], gpre_ref[...], acc_s[...])
            dx_ref[...] = dh_ref[...] + dx
            _accumulate(dgain_ref, dgain, i == 0)

    row = pl.BlockSpec((tm, D), lambda i, k: (i, 0))
    vec = pl.BlockSpec((1, D), lambda i, k: (0, 0))
    return pl.pallas_call(
        body, name=name, grid=(T // tm, S),
        in_specs=[pl.BlockSpec((1, tm, N), lambda i, k: (k, i, 0))] * n
        + [pl.BlockSpec((1, N, D), lambda i, k: (k, 0, 0))] * n + [row, vec, row],
        out_specs=[row, vec],
        out_shape=[_out((T, D), F32), _out((1, D), F32)],
        scratch_shapes=[pltpu.VMEM((tm, D), F32)],
        compiler_params=_params(2, VMEM_LIMIT_WIDE),
    )(*_hbm(*dys, *ws, x, g_pre, dh))


def _mm_tn(a, b, bm, name, after=None):
    ga, T, M = a.shape
    gb, _, N = b.shape
    b_spec = pl.BlockSpec((1, T, N), (lambda g, m: (g, 0, 0)) if gb > 1 else (lambda g, m: (0, 0, 0)))
    G = max(ga, gb)
    extra = [] if after is None else [after]

    def body(a_ref, b_ref, *refs):
        o_ref, narrow_ref = refs[-2:]
        o_ref[0] = _dot(a_ref[0].astype(BF16), b_ref[0].astype(BF16), TN)
        narrow_ref[0] = o_ref[0].astype(BF16)

    out = pl.BlockSpec((1, bm, N), lambda g, m: (g, m, 0))
    return pl.pallas_call(
        body, name=name, grid=(G, M // bm),
        in_specs=[pl.BlockSpec((1, T, bm), (lambda g, m: (g, 0, m)) if ga > 1 else (lambda g, m: (0, 0, m))), b_spec]
        + [ANY] * len(extra),
        out_specs=[out, out],
        out_shape=[_out((G, M, N), F32), _out((G, M, N), BF16)],
        compiler_params=_params(2, VMEM_LIMIT),
    )(*_hbm(a, b), *extra)


QKV_PIECE = 256


def _qkv_shard(dy_refs, k, n_col):
    width = dy_refs[0].shape[1]
    parts = []
    for col in range(k * n_col, (k + 1) * n_col, QKV_PIECE):
        parts.append(dy_refs[col // width][:, col % width:col % width + QKV_PIECE])
    return jnp.concatenate(parts, axis=1)


def _qkv_bwd_in(dys, w, x, g_pre, dh, name):
    T, D = x.shape
    n = len(dys)
    S, _, N = w.shape
    tm = min(WIDE_ROW_BLOCK, T)

    def body(*refs):
        dy_refs = refs[:n]
        w_ref, x_ref, gpre_ref, dh_ref, dx_ref, dgain_ref, acc_s = refs[n:]
        i, k = pl.program_id(0), pl.program_id(1)
        for shard in range(S):
            @pl.when(k == shard)
            def _(shard=shard):
                part = _dot(_qkv_shard(dy_refs, shard, N), w_ref[0], NT)
                if shard == 0:
                    acc_s[...] = part
                else:
                    acc_s[...] += part

        @pl.when(k == S - 1)
        def _():
            dx, dgain = _rms_bwd(x_ref[...], gpre_ref[...], acc_s[...])
            dx_ref[...] = dh_ref[...] + dx
            _accumulate(dgain_ref, dgain, i == 0)

    row = pl.BlockSpec((tm, D), lambda i, k: (i, 0))
    vec = pl.BlockSpec((1, D), lambda i, k: (0, 0))
    return pl.pallas_call(
        body, name=name, grid=(T // tm, S),
        in_specs=[pl.BlockSpec((tm, dy.shape[1]), lambda i, k: (i, 0)) for dy in dys]
        + [pl.BlockSpec((1, D, N), lambda i, k: (k, 0, 0)), row, vec, row],
        out_specs=[row, vec],
        out_shape=[_out((T, D), F32), _out((1, D), F32)],
        scratch_shapes=[pltpu.VMEM((tm, D), F32)],
        compiler_params=_params(2, VMEM_LIMIT_WIDE),
    )(*_hbm(*dys, w, x, g_pre, dh))


def _dw_in(a, dys, n_col, bm, name):
    T, M = a.shape
    n = len(dys)
    S = n * dys[0].shape[1] // n_col

    def body(*refs):
        a_ref, dy_refs = refs[0], refs[1:1 + n]
        o_ref, narrow_ref = refs[1 + n:]
        k = pl.program_id(1)
        for shard in range(S):
            @pl.when(k == shard)
            def _(shard=shard):
                o_ref[0] = _dot(a_ref[...], _qkv_shard(dy_refs, shard, n_col), TN)
                narrow_ref[0] = o_ref[0].astype(BF16)

    out = pl.BlockSpec((1, bm, n_col), lambda m, k: (k, m, 0))
    return pl.pallas_call(
        body, name=name, grid=(M // bm, S),
        in_specs=[pl.BlockSpec((T, bm), lambda m, k: (0, m))]
        + [pl.BlockSpec((T, dy.shape[1]), lambda m, k: (0, 0)) for dy in dys],
        out_specs=[out, out],
        out_shape=[_out((S, M, n_col), F32), _out((S, M, n_col), BF16)],
        compiler_params=_params(2, VMEM_LIMIT_WIDE),
    )(*_hbm(a, *dys))


def _norm_proj(x, g_pre, w, name):
    T, D = x.shape
    S, _, N = w.shape
    tm = min(WIDE_ROW_BLOCK, T)

    def body(x_ref, g_ref, w_ref, o_ref, xn_ref, xn_s):
        @pl.when(pl.program_id(1) == 0)
        def _():
            xn_s[...] = _rms_fwd(x_ref[...], g_ref[...]).astype(BF16)
            xn_ref[...] = xn_s[...]

        o_ref[...] = _dot(xn_s[...], w_ref[0]).astype(BF16)

    row = pl.BlockSpec((tm, D), lambda i, k: (i, 0))
    return pl.pallas_call(
        body, name=name, grid=(T // tm, S),
        in_specs=[row, pl.BlockSpec((1, D), lambda i, k: (0, 0)), pl.BlockSpec((1, D, N), lambda i, k: (k, 0, 0))],
        out_specs=[pl.BlockSpec((tm, N), lambda i, k: (i, k)), row],
        out_shape=[_out((T, S * N), BF16), _out((T, D), BF16)],
        scratch_shapes=[pltpu.VMEM((tm, D), BF16)],
        compiler_params=_params(2, VMEM_LIMIT_WIDE),
    )(*_hbm(x, g_pre, w))


def _mix_out_fwd(h, o_a, o_b, g_sb, g_ch, w_out, g_post, name):
    T, D = h.shape
    W = g_sb.shape[1]
    tm = min(WIDE_ROW_BLOCK, T)

    def body(h_ref, oa_ref, ob_ref, gsb_ref, gch_ref, w_ref, gpost_ref, h2_ref, mixed_ref, mo_ref):
        mixed_ref[:, :W] = _rms_fwd(oa_ref[...], gsb_ref[...]).astype(BF16)
        mixed_ref[:, W:] = _rms_fwd(ob_ref[...], gch_ref[...]).astype(BF16)
        mo = _dot(mixed_ref[...], w_ref[...])
        mo_ref[...] = mo
        h2_ref[...] = h_ref[...] + _rms_fwd(mo, gpost_ref[...])

    row = pl.BlockSpec((tm, D), lambda i: (i, 0))
    part = pl.BlockSpec((tm, W), lambda i: (i, 0))
    half = pl.BlockSpec((1, W), lambda i: (0, 0))
    return pl.pallas_call(
        body, name=name, grid=(T // tm,),
        in_specs=[row, part, part, half, half, pl.BlockSpec((D, D), lambda i: (0, 0)), pl.BlockSpec((1, D), lambda i: (0, 0))],
        out_specs=[row, row, row],
        out_shape=[_out((T, D), F32), _out((T, D), BF16),
                   _out((T, D), F32)],
        compiler_params=_params(1, VMEM_LIMIT_WIDE),
    )(*_hbm(h, o_a, o_b, g_sb, g_ch, w_out, g_post))


def _mix_out_bwd(dh, mo, g_post, w_out, o_a, o_b, g_sb, g_ch, name):
    T, D = dh.shape
    W = g_sb.shape[1]
    tm = min(WIDE_ROW_BLOCK, T)

    def body(dh_ref, mo_ref, gpost_ref, w_ref, oa_ref, ob_ref, gsb_ref, gch_ref,
             dmo_ref, doa_ref, dob_ref, dgpost_ref, dgsb_ref, dgch_ref):
        first = pl.program_id(0) == 0
        dmo, dgpost = _rms_bwd(mo_ref[...], gpost_ref[...], dh_ref[...])
        dmo_ref[...] = dmo.astype(BF16)
        dmix = _dot(dmo_ref[...], w_ref[...], NT)
        doa_ref[...], dgsb = _rms_bwd(oa_ref[...], gsb_ref[...], dmix[:, :W])
        dob_ref[...], dgch = _rms_bwd(ob_ref[...], gch_ref[...], dmix[:, W:])
        _accumulate(dgpost_ref, dgpost, first)
        _accumulate(dgsb_ref, dgsb, first)
        _accumulate(dgch_ref, dgch, first)

    row = pl.BlockSpec((tm, D), lambda i: (i, 0))
    part = pl.BlockSpec((tm, W), lambda i: (i, 0))
    vec = pl.BlockSpec((1, D), lambda i: (0, 0))
    half = pl.BlockSpec((1, W), lambda i: (0, 0))
    return pl.pallas_call(
        body, name=name, grid=(T // tm,),
        in_specs=[row, row, vec, pl.BlockSpec((D, D), lambda i: (0, 0)), part, part, half, half],
        out_specs=[row, part, part, vec, half, half],
        out_shape=[_out((T, D), BF16), _out((T, W), F32),
                   _out((T, W), F32), _out((1, D), F32),
                   _out((1, W), F32), _out((1, W), F32)],
        compiler_params=_params(1, VMEM_LIMIT_WIDE),
    )(*_hbm(dh, mo, g_post, w_out, o_a, o_b, g_sb, g_ch))


def _ple_loss(h, p, target, w_proj, w_gate, g_post, name):
    T, D = h.shape
    P = p.shape[1]
    S = N_CHIPS
    C = D // S
    tm = min(ROW_BLOCK, T)

    def body(h_ref, p_ref, t_ref, wp_ref, wg_ref, g_ref, loss_ref, dh_ref, dproj_ref, dgate_ref, dgain_ref):
        first = pl.program_id(0) == 0
        h3 = h_ref[...]
        proj = _dot(p_ref[...].astype(BF16), wp_ref[...])
        s = _sigmoid(_dot(h3.astype(BF16), wg_ref[...]))
        e = proj * s
        diff = h3 + _rms_fwd(e, g_ref[...]) - t_ref[...]
        part = 0.5 * jnp.sum(jnp.mean(diff * diff, axis=-1, keepdims=True), axis=0, keepdims=True)
        _accumulate(loss_ref, jnp.broadcast_to(part, loss_ref.shape), first)
        dy = diff * (1.0 / D)
        de, dgain = _rms_bwd(e, g_ref[...], dy)
        _accumulate(dgain_ref, dgain, first)
        dproj = (de * s).astype(BF16)
        for j in range(S):
            dproj_ref[j] = dproj[:, j * C:(j + 1) * C]
        dgate_ref[...] = (de * proj * s * (1.0 - s)).astype(BF16)
        dh_ref[...] = dy + _dot(dgate_ref[...], wg_ref[...], NT)

    row = pl.BlockSpec((tm, D), lambda i: (i, 0))
    vec = pl.BlockSpec((1, D), lambda i: (0, 0))
    return pl.pallas_call(
        body, name=name, grid=(T // tm,),
        in_specs=[row, pl.BlockSpec((tm, P), lambda i: (i, 0)), row,
                  pl.BlockSpec((P, D), lambda i: (0, 0)), pl.BlockSpec((D, D), lambda i: (0, 0)), vec],
        out_specs=[pl.BlockSpec((8, 128), lambda i: (0, 0)), row,
                   pl.BlockSpec((S, tm, C), lambda i: (0, i, 0)), row, vec],
        out_shape=[_out((8, 128), F32), _out((T, D), F32),
                   _out((S, T, C), BF16), _out((T, D), BF16),
                   _out((1, D), F32)],
        compiler_params=_params(1, VMEM_LIMIT_WIDE),
    )(*_hbm(h, p, target, w_proj, w_gate, g_post))


def _sb_scores(q, kj, mask):
    z = _dot(q, kj, NT)
    sp = jnp.maximum(z, 0.0) + jnp.log(1.0 + jnp.exp(-jnp.abs(z)))
    return z, sp if mask is None else jnp.where(mask, sp, 0.0)


def _strict_causal():
    rows = lax.broadcasted_iota(jnp.int32, (SB_BLOCK, SB_BLOCK), 0)
    cols = lax.broadcasted_iota(jnp.int32, (SB_BLOCK, SB_BLOCK), 1)
    return cols < rows


def _tri(cmp):
    r = lax.broadcasted_iota(jnp.int32, (2 * SB_BLOCK, SB_BLOCK), 0) % SB_BLOCK
    c = lax.broadcasted_iota(jnp.int32, (2 * SB_BLOCK, SB_BLOCK), 1)
    return jnp.where(cmp(r, c), 1.0, 0.0).astype(BF16)


def _cum(x, tri):
    return _dot(jnp.concatenate(_split2(x), axis=1), tri)


def _pair_lanes():
    lane = lax.broadcasted_iota(jnp.int32, (1, PAIR), 1)
    return [lane < HEAD_DIM, lane >= HEAD_DIM]


def _only(lanes, x):
    return jnp.where(lanes, x, jnp.zeros_like(x))


def _sb_fwd(qkv, name):
    T = qkv.shape[0]
    B = SB_BLOCK
    W = SB_FWD_PAIRS * PAIR
    steps = N_HEADS // (2 * SB_FWD_PAIRS)
    heads = [(p, h) for p in range(SB_FWD_PAIRS) for h in range(2)]

    def body(q_ref, k_ref, v_ref, o_ref):
        i = pl.program_id(1)
        after = _tri(lambda r, c: r > c)
        lanes = _pair_lanes()
        cols = [slice(p * PAIR, (p + 1) * PAIR) for p in range(SB_FWD_PAIRS)]
        q = {(p, h): _only(lanes[h], q_ref[:, cols[p]] * ATT_SCALE) for p, h in heads}

        def tiles(j, carries, mask):
            at = pl.ds(pl.multiple_of(j * B, B), B)
            scores = [_sb_scores(q[ph], k_ref[at, cols[ph[0]]], mask) for ph in heads]
            laters = [_cum(sp, after) for _, sp in scores]
            out = []
            for ph, (z, sp), later, (run, acc) in zip(heads, scores, laters, carries):
                a = jnp.exp(z - sp - later - run)
                if mask is not None:
                    a = jnp.where(mask, a, 0.0)
                out.append((run + later[:, 0:1] + sp[:, 0:1],
                            acc + _dot(a.astype(BF16), _only(lanes[ph[1]], v_ref[at, cols[ph[0]]]))))
            return tuple(out)

        zero = (jnp.zeros((B, 1), F32), jnp.zeros((B, PAIR), F32))
        carries = tiles(i, (zero,) * len(heads), _strict_causal())
        carries = lax.fori_loop(0, i, lambda jj, cs: tiles(i - 1 - jj, cs, None), carries)
        for p in range(SB_FWD_PAIRS):
            o_ref[:, cols[p]] = carries[2 * p][1] + carries[2 * p + 1][1]

    blk = lambda off: pl.BlockSpec((B, W), lambda g, i: (i, g + off))
    full = lambda off: pl.BlockSpec((T, W), lambda g, i: (0, g + off))
    return pl.pallas_call(
        body, name=name, grid=(steps, T // B),
        in_specs=[blk(0), full(steps), full(2 * steps)],
        out_specs=blk(0),
        out_shape=_out((T, N_HEADS * HEAD_DIM), F32),
        compiler_params=_params(2, VMEM_LIMIT),
    )(*_hbm(qkv, qkv, qkv))


def _sb_bwd(qkv, do, o, after, name):
    T = qkv.shape[0]
    B = SB_BLOCK
    W = SB_PAIRS * PAIR
    steps = N_HEADS // (2 * SB_PAIRS)
    n_blocks = T // B
    heads = [(p, h) for p in range(SB_PAIRS) for h in range(2)]

    def body(q_ref, k_ref, v_ref, do_ref, o_ref, dq_ref, dk_ref, dv_ref, dk_s, dv_s):
        i = pl.program_id(1)

        @pl.when(i == 0)
        def _():
            dk_s[...] = jnp.zeros_like(dk_s)
            dv_s[...] = jnp.zeros_like(dv_s)

        after = _tri(lambda r, c: r > c)
        since = _tri(lambda r, c: r >= c)
        lanes = _pair_lanes()
        cols = [slice(p * PAIR, (p + 1) * PAIR) for p in range(SB_PAIRS)]
        q = {(p, h): _only(lanes[h], q_ref[:, cols[p]] * ATT_SCALE) for p, h in heads}
        do = {(p, h): _only(lanes[h], do_ref[:, cols[p]].astype(BF16)) for p, h in heads}
        total = {ph: jnp.sum(do[ph].astype(F32) * o_ref[:, cols[ph[0]]], axis=1, keepdims=True) for ph in heads}

        def tiles(j, carries, mask):
            at = pl.ds(pl.multiple_of(j * B, B), B)
            ks = [k_ref[at, c] for c in cols]
            vs = [v_ref[at, c] for c in cols]
            scores = [_sb_scores(q[ph], ks[ph[0]], mask) for ph in heads]
            laters = [_cum(sp, after) for _, sp in scores]
            das = [_dot(do[ph], vs[ph[0]], NT) for ph in heads]
            a_s, gs = [], []
            for (z, sp), later, da, carry in zip(scores, laters, das, carries):
                a = jnp.exp(z - sp - later - carry[0])
                if mask is not None:
                    a = jnp.where(mask, a, 0.0)
                a = a.astype(BF16)
                a_s.append(a)
                gs.append(a.astype(F32) * da)
            sinces = [_cum(g, since) for g in gs]
            dzs = []
            for ph, (_, sp), g, from_s, carry in zip(heads, scores, gs, sinces, carries):
                g_before = total[ph] - carry[1] - from_s
                fail = jnp.exp(-sp)
                dz = fail * (g + g_before) - g_before
                if mask is not None:
                    dz = jnp.where(mask, dz, 0.0)
                dzs.append(dz.astype(BF16))
            out = []
            for ph, (_, sp), a, dz, later, from_s, carry in zip(heads, scores, a_s, dzs, laters, sinces, carries):
                dk_s[at, cols[ph[0]]] += _dot(dz, q[ph], TN)
                dv_s[at, cols[ph[0]]] += _dot(a, do[ph], TN)
                out.append((carry[0] + later[:, 0:1] + sp[:, 0:1], carry[1] + from_s[:, 0:1],
                            carry[2] + _dot(dz, _only(lanes[ph[1]], ks[ph[0]]))))
            return tuple(out)

        col = jnp.zeros((B, 1), F32)
        zero = (col, col, jnp.zeros((B, PAIR), F32))
        carries = tiles(i, (zero,) * len(heads), _strict_causal())
        last = lax.fori_loop(0, i, lambda jj, cs: tiles(i - 1 - jj, cs, None), carries)
        for p in range(SB_PAIRS):
            dq_ref[:, cols[p]] = ((last[2 * p][2] + last[2 * p + 1][2]) * ATT_SCALE).astype(BF16)

        @pl.when(i == n_blocks - 1)
        def _():
            dk_ref[...] = dk_s[...].astype(BF16)
            dv_ref[...] = dv_s[...].astype(BF16)

    blk = lambda off: pl.BlockSpec((B, W), lambda g, i: (i, g + off))
    full = lambda off: pl.BlockSpec((T, W), lambda g, i: (0, g + off))
    out = _out((T, N_HEADS * HEAD_DIM), BF16)
    return pl.pallas_call(
        lambda after_ref, *refs: body(*refs), name=name, grid=(steps, n_blocks),
        in_specs=[ANY, blk(0), full(steps), full(2 * steps), blk(0), blk(0)],
        out_specs=[blk(0), full(0), full(0)],
        out_shape=[out, out, out],
        scratch_shapes=[pltpu.VMEM((T, W), F32)] * 2,
        compiler_params=_params(2, VMEM_LIMIT),
    )(after, *_hbm(qkv, qkv, qkv, do, o))


NEAR = BAND - PAD + REL_CLIP
FAR = BAND - NEAR
NEAR_REL = 2 * REL_CLIP
BIAS_ROWS = 8


def _rel_onehot(i, transposed):
    shape = (NEAR, NEAR_REL) if transposed else (NEAR_REL, NEAR)
    j = FAR + lax.broadcasted_iota(jnp.int32, shape, 0 if transposed else 1)
    r = lax.broadcasted_iota(jnp.int32, shape, 1 if transposed else 0)
    idx = jnp.clip(i + PAD - j, -REL_CLIP, REL_CLIP) + REL_CLIP
    return jnp.where(idx - 1 == r, 1.0, 0.0).astype(BF16)


def _bias_table(rel_bias, name):
    def body(near_ref, far_ref, o_ref):
        rb = near_ref[...]
        hi, lo = _split2(rb)
        lo2 = (rb - hi.astype(F32) - lo.astype(F32)).astype(BF16)
        far = jnp.broadcast_to(far_ref[...], (N_HEADS, FAR))
        for k in range(BIAS_ROWS):
            onehot = _rel_onehot(pl.program_id(0) * BIAS_ROWS + k, False)
            o_ref[k, :, :FAR] = far
            o_ref[k, :, FAR:] = _dot(hi, onehot) + _dot(lo, onehot) + _dot(lo2, onehot)

    return pl.pallas_call(
        body, name=name, grid=(CHUNK // BIAS_ROWS,),
        in_specs=[pl.BlockSpec((N_HEADS, NEAR_REL), lambda i: (0, 0)), pl.BlockSpec((N_HEADS, 1), lambda i: (0, 0))],
        out_specs=pl.BlockSpec((BIAS_ROWS, N_HEADS, BAND), lambda i: (i, 0, 0)),
        out_shape=_out((CHUNK, N_HEADS, BAND), F32),
        compiler_params=_params(1),
    )(*_hbm(rel_bias[:, 1:], rel_bias[:, N_REL - 1:]))


def _bias_grad(dbias_t, name):
    def body(d_ref, near_ref, far_ref):
        near, far = None, None
        for k in range(BIAS_ROWS):
            onehot = _rel_onehot(pl.program_id(0) * BIAS_ROWS + k, True)
            hi, lo = _split2(d_ref[k, :, FAR:])
            part = _dot(hi, onehot) + _dot(lo, onehot)
            rest = jnp.sum(d_ref[k, :, :FAR], axis=1, keepdims=True)
            near, far = (part, rest) if near is None else (near + part, far + rest)
        first = pl.program_id(0) == 0
        _accumulate(near_ref, near, first)
        _accumulate(far_ref, jnp.broadcast_to(far, far_ref.shape), first)

    near, far = pl.pallas_call(
        body, name=name, grid=(CHUNK // BIAS_ROWS,),
        in_specs=[pl.BlockSpec((BIAS_ROWS, N_HEADS, BAND), lambda i: (i, 0, 0))],
        out_specs=[pl.BlockSpec((N_HEADS, NEAR_REL), lambda i: (0, 0)), pl.BlockSpec((N_HEADS, 128), lambda i: (0, 0))],
        out_shape=[_out((N_HEADS, NEAR_REL), F32), _out((N_HEADS, 128), F32)],
        compiler_params=_params(1),
    )(*_hbm(dbias_t))
    return jnp.pad(near, ((0, 0), (1, 0))).at[:, N_REL - 1].add(far[:, 0])


def _ch_probs(scores, bias, valid):
    z = jnp.where(valid, scores * ATT_SCALE + bias, NEG_INF)
    e = jnp.exp(z - jnp.max(z, axis=-1, keepdims=True))
    return e / jnp.sum(e, axis=-1, keepdims=True)


CH_HEADS = [(pair, h) for pair in range(N_HEADS // 2) for h in range(2)]
CH_COLS = [slice(pair * PAIR, (pair + 1) * PAIR) for pair in range(N_HEADS // 2)]


CH_GROUP = 2
CH_Q = CH_GROUP * CHUNK
CH_WIN = (LOOKBACK + CH_GROUP) * CHUNK


def _ch_valid(n):
    row_chunk = lax.broadcasted_iota(jnp.int32, (CH_Q, CH_WIN), 0) // CHUNK
    slot = lax.broadcasted_iota(jnp.int32, (CH_Q, CH_WIN), 1)
    ahead = slot // CHUNK - row_chunk
    return (ahead >= 0) & (ahead <= LOOKBACK) & (n * CH_Q + slot >= PAD)


def _ch_group_bias(bias):
    shifted = [jnp.pad(bias, ((0, 0), (0, 0), (c * CHUNK, (CH_GROUP - 1 - c) * CHUNK))) for c in range(CH_GROUP)]
    return jnp.concatenate(shifted, axis=1)


def _ch_fold_bias_grad(dbias):
    parts = [dbias[:, c * CHUNK:(c + 1) * CHUNK, c * CHUNK:c * CHUNK + BAND] for c in range(CH_GROUP)]
    return sum(parts[1:], parts[0])


def _ch_fwd(qkv, bias, name):
    T = qkv.shape[0]
    W = N_HEADS * HEAD_DIM

    def body(q_ref, k_ref, v_ref, b_ref, o_ref, kp, vp):
        n = pl.program_id(0)

        @pl.when(n == 0)
        def _():
            _ch_load_padded(k_ref, v_ref, kp, vp)

        win = pl.ds(pl.multiple_of(n * CH_Q, CH_Q), CH_WIN)
        valid = _ch_valid(n)
        lanes = _pair_lanes()
        scores = [_dot(_only(lanes[h], q_ref[:, CH_COLS[pair]]), kp[win, CH_COLS[pair]], NT) for pair, h in CH_HEADS]
        probs = [_ch_probs(s, b_ref[2 * pair + h], valid).astype(BF16) for s, (pair, h) in zip(scores, CH_HEADS)]
        outs = [_dot(p, _only(lanes[h], vp[win, CH_COLS[pair]])) for p, (pair, h) in zip(probs, CH_HEADS)]
        for pair, cols in enumerate(CH_COLS):
            o_ref[:, cols] = outs[2 * pair] + outs[2 * pair + 1]

    full = lambda col: pl.BlockSpec((T, W), lambda n: (0, col))
    return pl.pallas_call(
        body, name=name, grid=(T // CH_Q,),
        in_specs=[pl.BlockSpec((CH_Q, W), lambda n: (n, 3)), full(4), full(5),
                  pl.BlockSpec((N_HEADS, CH_Q, CH_WIN), lambda n: (0, 0, 0))],
        out_specs=pl.BlockSpec((CH_Q, W), lambda n: (n, 0)),
        out_shape=_out((T, W), F32),
        scratch_shapes=[pltpu.VMEM((PAD + T, W), BF16)] * 2,
        compiler_params=_params(1, VMEM_LIMIT),
    )(*_hbm(qkv, qkv, qkv, bias))


def _ch_load_padded(k_ref, v_ref, kp, vp):
    for src, dst in ((k_ref, kp), (v_ref, vp)):
        dst[:PAD, :] = jnp.zeros((PAD, dst.shape[1]), dst.dtype)
        dst[PAD:, :] = src[...]


def _ch_bwd(qkv, bias, do, after, name):
    T = qkv.shape[0]
    W = N_HEADS * HEAD_DIM
    n_chunks = T // CH_Q

    def body(q_ref, k_ref, v_ref, b_ref, do_ref, dq_ref, dk_ref, dv_ref, db_ref, kp, vp, dk_s, dv_s):
        n = pl.program_id(0)

        @pl.when(n == 0)
        def _():
            _ch_load_padded(k_ref, v_ref, kp, vp)
            dk_s[...] = jnp.zeros_like(dk_s)
            dv_s[...] = jnp.zeros_like(dv_s)
            db_ref[...] = jnp.zeros_like(db_ref)

        win = pl.ds(pl.multiple_of(n * CH_Q, CH_Q), CH_WIN)
        valid = _ch_valid(n)
        lanes = _pair_lanes()
        kws = [kp[win, cols] for cols in CH_COLS]
        vws = [vp[win, cols] for cols in CH_COLS]
        qs = [_only(lanes[h], q_ref[:, CH_COLS[pair]]) for pair, h in CH_HEADS]
        dos = [_only(lanes[h], do_ref[:, CH_COLS[pair]].astype(BF16)) for pair, h in CH_HEADS]
        scores = [_dot(q, kws[pair], NT) for q, (pair, _) in zip(qs, CH_HEADS)]
        dps = [_dot(do, vws[pair], NT) for do, (pair, _) in zip(dos, CH_HEADS)]
        probs = [_ch_probs(s, b_ref[2 * pair + h], valid) for s, (pair, h) in zip(scores, CH_HEADS)]
        dzs = [p * (dp - jnp.sum(dp * p, axis=-1, keepdims=True)) for p, dp in zip(probs, dps)]
        for k, dz in enumerate(dzs):
            db_ref[k] += dz
        dzbs = [(dz * ATT_SCALE).astype(BF16) for dz in dzs]
        dqs = [_dot(dz, _only(lanes[h], kws[pair])) for dz, (pair, h) in zip(dzbs, CH_HEADS)]
        dks = [_dot(dz, q, TN) for dz, q in zip(dzbs, qs)]
        dvs = [_dot(p.astype(BF16), do, TN) for p, do in zip(probs, dos)]
        for pair, cols in enumerate(CH_COLS):
            dq_ref[:, cols] = (dqs[2 * pair] + dqs[2 * pair + 1]).astype(BF16)
            dk_s[win, cols] += dks[2 * pair] + dks[2 * pair + 1]
            dv_s[win, cols] += dvs[2 * pair] + dvs[2 * pair + 1]

        @pl.when(n == n_chunks - 1)
        def _():
            dk_ref[...] = dk_s[PAD:, :].astype(BF16)
            dv_ref[...] = dv_s[PAD:, :].astype(BF16)

    full = lambda col: pl.BlockSpec((T, W), lambda n: (0, col))
    blk = lambda col: pl.BlockSpec((CH_Q, W), lambda n: (n, col))
    tab = pl.BlockSpec((N_HEADS, CH_Q, CH_WIN), lambda n: (0, 0, 0))
    out = _out((T, W), BF16)
    return pl.pallas_call(
        lambda after_ref, *refs: body(*refs), name=name, grid=(n_chunks,),
        in_specs=[ANY, blk(3), full(4), full(5), tab, blk(0)],
        out_specs=[blk(0), full(0), full(0), tab],
        out_shape=[out, out, out, _out((N_HEADS, CH_Q, CH_WIN), F32)],
        scratch_shapes=[pltpu.VMEM((PAD + T, W), BF16)] * 2 + [pltpu.VMEM((PAD + T, W), F32)] * 2,
        compiler_params=_params(1, VMEM_LIMIT),
    )(after, *_hbm(qkv, qkv, qkv, bias, do))


def _rows_split(a, parts):
    return a.reshape(a.shape[:-2] + (parts, a.shape[-2] // parts, a.shape[-1]))


def _cast_into_own_slot(me, c, ws, in_chip_order, name):
    parts = 2
    ws = [_rows_split(_rows_split(w, 2), parts) for w in ws]
    n = len(ws)

    def body(me_ref, c_ref, *refs):
        for src, dst in zip(refs[:n], refs[n:]):
            dst[0, 0, 0] = src[0, 0].astype(BF16)

    def specs(w, plain):
        block = (1, 1) + w.shape[2:]
        if plain:
            return (pl.BlockSpec(block, lambda d, r, me_ref, c_ref: (d, r, 0, 0)),
                    pl.BlockSpec((1,) + block, lambda d, r, me_ref, c_ref: (me_ref[0], d, r, 0, 0)))
        return (pl.BlockSpec(block, lambda d, r, me_ref, c_ref: (d ^ c_ref[0], r, 0, 0)),
                pl.BlockSpec((1,) + block, lambda d, r, me_ref, c_ref: (0, d, r, 0, 0)))

    both = [specs(w, plain) for w, plain in zip(ws, in_chip_order)]
    outs = pl.pallas_call(
        body, name=name,
        grid_spec=pltpu.PrefetchScalarGridSpec(
            num_scalar_prefetch=2, grid=(2, parts),
            in_specs=[s[0] for s in both], out_specs=[s[1] for s in both]),
        out_shape=[_out((N_CHIPS,) + w.shape, BF16) for w in ws],
        compiler_params=_params(2, VMEM_LIMIT),
    )(me, c, *_hbm(*ws))
    return [o.reshape(N_CHIPS, 2, o.shape[2] * o.shape[3], o.shape[4]) for o in outs]


def _zone_slots(in_chip_order):
    x, y, c, _ = _place()
    me = 2 * x + y
    if in_chip_order:
        return (me, c), (lambda r: (me, c)), (lambda r: (me ^ r, c)), (lambda r: (me ^ r, c))
    return (0, 0), (lambda r: (r, 0)), (lambda r: (r, 0)), (lambda r: (r, 1))


def _pair_add(c, mine, got, permuted, name):
    parts = 2
    mine = [_rows_split(m, parts) for m in mine]
    got = [_rows_split(g, parts) for g in got]
    n = len(mine)

    def body(c_ref, *refs):
        for a, b, o in zip(refs[:n], refs[n:2 * n], refs[2 * n:]):
            o[0, 0] = (a[0, 0, 0] + b[0, 0].astype(F32)).astype(BF16)

    def mine_spec(m, perm):
        if perm:
            return pl.BlockSpec((1, 1, 1) + m.shape[3:], lambda j, r, c_ref: (j, 0, r, 0, 0))
        return pl.BlockSpec((1, 1, 1) + m.shape[3:], lambda j, r, c_ref: (j, c_ref[0], r, 0, 0))

    def got_spec(g):
        return pl.BlockSpec((1, 1) + g.shape[2:], lambda j, r, c_ref: (j, r, 0, 0))

    outs = pl.pallas_call(
        body, name=name,
        grid_spec=pltpu.PrefetchScalarGridSpec(
            num_scalar_prefetch=1, grid=(N_CHIPS, parts),
            in_specs=[mine_spec(m, perm) for m, perm in zip(mine, permuted)] + [got_spec(g) for g in got],
            out_specs=[got_spec(g) for g in got]),
        out_shape=[_out(g.shape, BF16) for g in got],
        compiler_params=_params(2, VMEM_LIMIT),
    )(c, *_hbm(*mine, *got))
    return [o.reshape(o.shape[0], o.shape[1] * o.shape[2], o.shape[3]) for o in outs]


def _chip_add(me, partials, landed, permuted, name):
    parts = 2
    ps = [_rows_split(x, parts) for x in partials]
    ls = [_rows_split(x, parts) for x in landed]
    n = len(ps)

    def body(me_ref, *refs):
        for own, got, o in zip(refs[:n], refs[n:2 * n], refs[2 * n:]):
            acc = own[0, 0].astype(F32)
            for r in range(N_CHIPS - 1):
                acc = acc + got[r, 0].astype(F32)
            o[0] = acc

    def own_spec(x, perm):
        if perm:
            return pl.BlockSpec((1, 1) + x.shape[2:], lambda r, me_ref: (0, r, 0, 0))
        return pl.BlockSpec((1, 1) + x.shape[2:], lambda r, me_ref: (me_ref[0], r, 0, 0))

    outs = pl.pallas_call(
        body, name=name,
        grid_spec=pltpu.PrefetchScalarGridSpec(
            num_scalar_prefetch=1, grid=(parts,),
            in_specs=[own_spec(x, perm) for x, perm in zip(ps, permuted)]
            + [pl.BlockSpec((N_CHIPS - 1, 1) + x.shape[2:], lambda r, me_ref: (0, r, 0, 0)) for x in ls],
            out_specs=[pl.BlockSpec((1,) + x.shape[2:], lambda r, me_ref: (r, 0, 0)) for x in ps]),
        out_shape=[_out(x.shape[1:], F32) for x in ps],
        compiler_params=_params(1, VMEM_LIMIT),
    )(me, *_hbm(*ps, *ls))
    return [o.reshape(o.shape[0] * o.shape[1], o.shape[2]) for o in outs]


def _adamw_math(w, g, m, v):
    m = ADAM_B1 * m + (1.0 - ADAM_B1) * g
    v = ADAM_B2 * v + (1.0 - ADAM_B2) * (g * g)
    m_hat = m / (1.0 - ADAM_B1 ** ADAM_STEP)
    v_hat = v / (1.0 - ADAM_B2 ** ADAM_STEP)
    delta = -ADAM_LR * (m_hat / (jnp.sqrt(v_hat) + ADAM_EPS) + ADAM_WD * w)
    return delta, m, v


def _adamw(ws, gs, ms, vs, parts, name):
    n = len(ws)
    flat = [_rows_split(a, parts) for a in (*ws, *gs, *ms, *vs)]

    def body(*refs):
        ins, outs = refs[:4 * n], refs[4 * n:]
        for k in range(n):
            d, m, v = _adamw_math(ins[k][...], ins[n + k][...], ins[2 * n + k][...], ins[3 * n + k][...])
            outs[k][...] = d
            outs[n + k][...] = m
            outs[2 * n + k][...] = v

    spec = lambda a: pl.BlockSpec((1,) + a.shape[1:], lambda i: (i, 0, 0))
    outs = pl.pallas_call(
        body, name=name, grid=(parts,),
        in_specs=[spec(a) for a in flat], out_specs=[spec(a) for a in flat[:n]] * 3,
        out_shape=[_out(a.shape, F32) for a in flat[:n]] * 3,
        compiler_params=_params(1, VMEM_LIMIT),
    )(*_hbm(*flat))
    outs = [o.reshape(o.shape[0] * o.shape[1], o.shape[2]) for o in outs]
    return outs[:n], outs[n:2 * n], outs[2 * n:]


def _adamw_halves(c, ws, owns, others, ms, vs, name):
    parts = 4
    n = len(ws)
    whole = [_rows_split(_rows_split(a, 2), parts) for a in (*ws, *ms, *vs)]
    halves = [_rows_split(a, parts) for a in (*owns, *others)]

    def body(c_ref, *refs):
        ins, outs = refs[:5 * n], refs[5 * n:]
        mine = pl.program_id(0) == c_ref[0]
        for k in range(n):
            g = jnp.where(mine, ins[3 * n + k][0], ins[4 * n + k][0])
            d, m, v = _adamw_math(ins[k][0, 0], g, ins[n + k][0, 0], ins[2 * n + k][0, 0])
            for slot, val in enumerate((g, d, m, v)):
                outs[slot * n + k][0, 0] = val

    wspec = lambda a: pl.BlockSpec((1, 1) + a.shape[2:], lambda h, r, c_ref: (h, r, 0, 0))
    hspec = lambda a: pl.BlockSpec((1,) + a.shape[1:], lambda h, r, c_ref: (r, 0, 0))
    outs = pl.pallas_call(
        body, name=name,
        grid_spec=pltpu.PrefetchScalarGridSpec(
            num_scalar_prefetch=1, grid=(2, parts),
            in_specs=[wspec(a) for a in whole] + [hspec(a) for a in halves],
            out_specs=[wspec(a) for a in whole[:n]] * 4),
        out_shape=[_out(a.shape, F32) for a in whole[:n]] * 4,
        compiler_params=_params(2, VMEM_LIMIT),
    )(c, *_hbm(*whole, *halves))
    outs = [o.reshape(2 * parts * o.shape[2], o.shape[3]) for o in outs]
    return outs[:n], outs[n:2 * n], outs[2 * n:3 * n], outs[3 * n:]


def _place():
    x, y, c = lax.axis_index("x"), lax.axis_index("y"), lax.axis_index("c")
    peers = [(x ^ (r >> 1), y ^ (r & 1), c) for r in (1, 2, 3)]
    return x, y, c, peers


def _handshake(peers):
    barrier = pltpu.get_barrier_semaphore()
    for peer in peers:
        pl.semaphore_signal(barrier, inc=1, device_id=peer, device_id_type=MESH)
    pl.semaphore_wait(barrier, len(peers))


ANY = pl.BlockSpec(memory_space=pl.ANY)
HBM = pl.BlockSpec(memory_space=pltpu.HBM)
SEM = pl.BlockSpec(memory_space=pltpu.SEMAPHORE)
SPLIT_COPY = pltpu.SideEffectType.DATAFLOW_SIDE_EFFECTING


def _split_start(body, name, collective_id, operands, n_sems, after=None):
    n = len(operands)
    extra = [] if after is None else [after]

    def wrapped(*refs):
        at = n + len(extra)
        body(refs[:n], refs[at], refs[at + 1])
        token = refs[-1]
        token[...] = jnp.zeros_like(token)

    outs = pl.pallas_call(
        wrapped, name=name,
        in_specs=[HBM] * n + [ANY] * len(extra),
        out_shape=(pltpu.SemaphoreType.DMA((n_sems,)), pltpu.SemaphoreType.DMA((n_sems,)),
                   *[pltpu.HBM(a.shape, a.dtype) for a in operands], jax.ShapeDtypeStruct((8, 128), F32)),
        out_specs=(SEM, SEM, *[HBM] * n, pl.BlockSpec(memory_space=pltpu.VMEM)),
        input_output_aliases={i: 2 + i for i in range(n)},
        compiler_params=pltpu.CompilerParams(has_side_effects=SPLIT_COPY, collective_id=collective_id),
    )(*_hbm(*operands), *extra)
    return outs[0], outs[1], list(outs[2:2 + n]), outs[-1]


def _split_wait(body, name, send_sem, recv_sem, operands, after):
    n = len(operands)

    def wrapped(*refs):
        body(refs[:n], refs[n], refs[n + 1])

    outs = pl.pallas_call(
        wrapped, name=name,
        in_specs=[HBM] * n + [SEM, SEM, ANY],
        out_shape=tuple(pltpu.HBM(a.shape, a.dtype) for a in operands),
        out_specs=tuple([HBM] * n),
        input_output_aliases={i: i for i in range(n)},
        compiler_params=pltpu.CompilerParams(has_side_effects=SPLIT_COPY),
    )(*operands, send_sem, recv_sem, after)
    return list(outs)


def _gather_copies(lands, in_chip_order, send_sem, recv_sem):
    peers = _place()[3]
    copies = []
    for a, (land, plain) in enumerate(zip(lands, in_chip_order)):
        own, sent_to, _, _ = _zone_slots(plain)
        copies += [pltpu.make_async_remote_copy(
            src_ref=land.at[own], dst_ref=land.at[sent_to(r + 1)],
            send_sem=send_sem.at[a * 3 + r], recv_sem=recv_sem.at[a * 3 + r],
            device_id=peers[r], device_id_type=MESH) for r in range(3)]
    return copies


def _gather_start(lands, in_chip_order, name, collective_id, after):
    def body(refs, send_sem, recv_sem):
        _handshake(_place()[3])
        for cp in _gather_copies(refs, in_chip_order, send_sem, recv_sem):
            cp.start()

    return _split_start(body, name, collective_id, list(lands), 3 * len(lands), after)


def _gather_wait(send_sem, recv_sem, operands, in_chip_order, after, name):
    def body(refs, send_sem, recv_sem):
        for cp in _gather_copies(refs, in_chip_order, send_sem, recv_sem):
            cp.wait_send()
            cp.wait_recv()

    return _split_wait(body, name, send_sem, recv_sem, operands, after)


def _gather_finish(lands, in_chip_order, with_ici, name):
    n = len(lands)

    def body(*refs):
        land = refs[n:2 * n]
        send_ici, recv_ici, send_d2d, recv_d2d = refs[2 * n:]
        x, y, c, _ = _place()
        ici = _gather_copies(land, in_chip_order, send_ici, recv_ici) if with_ici else []
        for cp in ici:
            cp.start()
        passed = []
        for a in range(n):
            _, _, received, kept = _zone_slots(in_chip_order[a])
            passed += [pltpu.make_async_remote_copy(
                src_ref=land[a].at[received(r + 1)], dst_ref=land[a].at[kept(r + 1)],
                send_sem=send_d2d.at[a * 3 + r], recv_sem=recv_d2d.at[a * 3 + r],
                device_id=(x, y, 1 - c), device_id_type=MESH) for r in range(3)]
        for k, cp in enumerate(passed):
            if with_ici:
                ici[k].wait_recv()
            cp.start()
        for cp in passed:
            cp.wait_recv()
        for cp in ici:
            cp.wait_send()
        for cp in passed:
            cp.wait_send()

    outs = pl.pallas_call(
        body, name=name,
        in_specs=[ANY] * n, out_specs=[ANY] * n,
        out_shape=[_out(l.shape, l.dtype) for l in lands],
        input_output_aliases={a: a for a in range(n)},
        scratch_shapes=[pltpu.SemaphoreType.DMA((3 * n,))] * 4,
    )(*lands)
    return list(outs)


def _pass_copies(lands, in_chip_order, send_sem, recv_sem):
    x, y, c, _ = _place()
    copies = []
    for a, (land, plain) in enumerate(zip(lands, in_chip_order)):
        _, _, received, kept = _zone_slots(plain)
        copies += [pltpu.make_async_remote_copy(
            src_ref=land.at[received(r + 1)], dst_ref=land.at[kept(r + 1)],
            send_sem=send_sem.at[a * 3 + r], recv_sem=recv_sem.at[a * 3 + r],
            device_id=(x, y, 1 - c), device_id_type=MESH) for r in range(3)]
    return copies


def _pass_start(lands, in_chip_order, name, collective_id):
    def body(refs, send_sem, recv_sem):
        x, y, c, _ = _place()
        _handshake([(x, y, 1 - c)])
        for cp in _pass_copies(refs, in_chip_order, send_sem, recv_sem):
            cp.start()

    return _split_start(body, name, collective_id, list(lands), 3 * len(lands))


def _pass_wait(send_sem, recv_sem, lands, in_chip_order, after, name):
    def body(refs, send_sem, recv_sem):
        for cp in _pass_copies(refs, in_chip_order, send_sem, recv_sem):
            cp.wait_send()
            cp.wait_recv()

    return _split_wait(body, name, send_sem, recv_sem, lands, after)


def _slabs(land):
    return land.reshape(N_CHIPS, 2 * land.shape[2], land.shape[3])


def _pair_swap(grads, permuted, name):
    n = len(grads)

    def body(*refs):
        src, dst = refs[:n], refs[n:2 * n]
        send_sem, recv_sem = refs[2 * n:]
        x, y, c, _ = _place()
        copies = [pltpu.make_async_remote_copy(
            src_ref=src[a].at[:, 1] if permuted[a] else src[a].at[:, 1 - c], dst_ref=dst[a],
            send_sem=send_sem.at[a], recv_sem=recv_sem.at[a],
            device_id=(x, y, 1 - c), device_id_type=MESH) for a in range(n)]
        for cp in copies:
            cp.start()
        for cp in copies:
            cp.wait()

    return pl.pallas_call(
        body, name=name,
        in_specs=[ANY] * n, out_specs=[ANY] * n,
        out_shape=[_out((N_CHIPS,) + g.shape[2:], g.dtype) for g in grads],
        scratch_shapes=[pltpu.SemaphoreType.DMA((n,))] * 2,
    )(*grads)


def _swap_copies(refs, permuted, send_sem, recv_sem):
    n = len(refs) // 2
    x, y, c, _ = _place()
    return [pltpu.make_async_remote_copy(
        src_ref=refs[a].at[:, 1] if permuted[a] else refs[a].at[:, 1 - c], dst_ref=refs[n + a],
        send_sem=send_sem.at[a], recv_sem=recv_sem.at[a],
        device_id=(x, y, 1 - c), device_id_type=MESH) for a in range(n)]


def _pair_swap_start(grads, permuted, name, collective_id):
    def body(refs, send_sem, recv_sem):
        x, y, c, _ = _place()
        _handshake([(x, y, 1 - c)])
        for cp in _swap_copies(refs, permuted, send_sem, recv_sem):
            cp.start()

    lands = [lax.empty((N_CHIPS,) + g.shape[2:], g.dtype) for g in grads]
    return _split_start(body, name, collective_id, list(grads) + lands, len(grads))


def _pair_swap_wait(send_sem, recv_sem, operands, permuted, after, name):
    def body(refs, send_sem, recv_sem):
        for cp in _swap_copies(refs, permuted, send_sem, recv_sem):
            cp.wait_send()
            cp.wait_recv()

    return _split_wait(body, name, send_sem, recv_sem, operands, after)


def _scatter_copies(refs, permuted, send_sem, recv_sem):
    n = len(refs) // 2
    x, y, _, peers = _place()
    me = 2 * x + y
    return [pltpu.make_async_remote_copy(
        src_ref=refs[a].at[r + 1] if permuted[a] else refs[a].at[me ^ (r + 1)], dst_ref=refs[n + a].at[r],
        send_sem=send_sem.at[a * 3 + r], recv_sem=recv_sem.at[a * 3 + r],
        device_id=peers[r], device_id_type=MESH) for a in range(n) for r in range(3)]


def _scatter_start(partials, permuted, name, collective_id):
    def body(refs, send_sem, recv_sem):
        _handshake(_place()[3])
        for cp in _scatter_copies(refs, permuted, send_sem, recv_sem):
            cp.start()

    lands = [lax.empty((N_CHIPS - 1,) + p.shape[1:], p.dtype) for p in partials]
    return _split_start(body, name, collective_id, list(partials) + lands, 3 * len(partials))


def _scatter_wait(send_sem, recv_sem, operands, permuted, after, name):
    def body(refs, send_sem, recv_sem):
        for cp in _scatter_copies(refs, permuted, send_sem, recv_sem):
            cp.wait_send()
            cp.wait_recv()

    return _split_wait(body, name, send_sem, recv_sem, operands, after)


def _pair_join(halves, name):
    n = len(halves)

    def body(*refs):
        src, dst = refs[:n], refs[n:2 * n]
        send_sem, recv_sem = refs[2 * n:]
        x, y, c, _ = _place()
        copies = [pltpu.make_async_remote_copy(
            src_ref=src[a], dst_ref=dst[a], send_sem=send_sem.at[a], recv_sem=recv_sem.at[a],
            device_id=(x, y, 1 - c), device_id_type=MESH) for a in range(n)]
        for cp in copies:
            cp.start()
        for cp in copies:
            cp.wait()

    return pl.pallas_call(
        body, name=name,
        in_specs=[ANY] * n, out_specs=[ANY] * n,
        out_shape=[_out(h.shape, F32) for h in halves],
        scratch_shapes=[pltpu.SemaphoreType.DMA((n,))] * 2,
    )(*halves)


def _all_sum_small(vs, after, name):
    rows = [v.shape[0] for v in vs]
    n, R, C = len(vs), sum(rows), vs[0].shape[1]
    n_dev = 8

    def body(*refs):
        after_ref, o_ref, mine, buf, send_sem, recv_sem = refs[n:]
        x, y, c, _ = _place()
        me = 4 * x + 2 * y + c
        at = 0
        for v_ref, r in zip(refs[:n], rows):
            mine[at:at + r] = v_ref[...]
            at += r
        buf[me] = mine[...]
        copies = []
        for k in range(1, n_dev):
            peer = (x ^ (k >> 2), y ^ ((k >> 1) & 1), c ^ (k & 1))
            copies.append(pltpu.make_async_remote_copy(
                src_ref=mine, dst_ref=buf.at[me], send_sem=send_sem.at[k - 1], recv_sem=recv_sem.at[k - 1],
                device_id=peer, device_id_type=MESH))
        for cp in copies:
            cp.start()
        for cp in copies:
            cp.wait()
        acc = buf[0]
        for m in range(1, n_dev):
            acc = acc + buf[m]
        o_ref[...] = acc

    return pl.pallas_call(
        body, name=name,
        in_specs=[pl.BlockSpec(memory_space=pltpu.VMEM)] * n + [ANY], out_specs=pl.BlockSpec(memory_space=pltpu.VMEM),
        out_shape=jax.ShapeDtypeStruct((R, C), F32),
        scratch_shapes=[pltpu.VMEM((R, C), F32), pltpu.VMEM((n_dev, R, C), F32),
                        pltpu.SemaphoreType.DMA((n_dev - 1,)), pltpu.SemaphoreType.DMA((n_dev - 1,))],
    )(*vs, after)


class _WholeWeights:
    def __init__(self, w):
        self.w = w

    def weights(self, group, after=None):
        return ({} if group == "passed" else self.w), None

    def grads_ready(self, group, gw):
        return None

    def grads_sent(self, group, after):
        return None


def _local_step(x, p, target, gains, rel_bias, hooks):
    T, D = x.shape
    S = N_CHIPS

    tied = lambda gain, token: gain if token is None else gain + token[0, 0]
    w, token = hooks.weights("first")
    w = dict(w)
    xn1, g1, u1, a1 = _ffn_up(x, tied(gains["ffn1_pre"], token), w["ffn1_gate"], w["ffn1_up"], "ffn1_up")
    w.update(hooks.weights("down", a1)[0])
    h1, f1 = _ffn_down(x, a1, gains["ffn1_post"], w["ffn1_down"], "ffn1_down")
    more, token = hooks.weights("in", h1)
    w.update(more)
    qkv, un = _norm_proj(h1, tied(gains["mix_pre"], token), w["in"], "qkv_proj")
    bias = _ch_group_bias(_bias_table(rel_bias, "bias_table").transpose(1, 0, 2))
    o_a = _sb_fwd(qkv, "sb_fwd")
    o_b = _ch_fwd(qkv, bias, "ch_fwd")
    more, token = hooks.weights("rest", o_b)
    w.update(more)
    w_out = w["out"].reshape(D, D)
    h2, mixed, mo = _mix_out_fwd(h1, o_a, o_b, gains["out_sb"], gains["out_ch"], w_out,
                                 tied(gains["mix_post"], token), "mix_out_fwd")
    w.update(hooks.weights("passed", h2)[0])
    h3, xn2, g2, u2, a2, f2 = _ffn_fwd(h2, gains["ffn2_pre"], gains["ffn2_post"], w["ffn2_gate"], w["ffn2_up"],
                                       w["ffn2_down"], "ffn2_fwd")
    w_ple_proj = w["ple_proj"].transpose(1, 0, 2).reshape(p.shape[1], D)
    w_ple_gate = w["ple_gate"].reshape(D, D)

    loss, dh3, dproj, dgate, dg_ple = _ple_loss(h3, p, target, w_ple_proj, w_ple_gate, gains["ple_post"], "ple_loss")
    gw, gg = {}, {"ple_post": dg_ple}
    gw["ple_proj"] = _mm_tn(p[None], dproj, p.shape[1], "dw_ple_proj")
    row_sharded = lambda pair: tuple(o.reshape(S, D // S, D) for o in pair)
    gw["ple_gate"] = row_sharded(_mm_tn(h3[None], dgate[None], 512, "dw_ple_gate"))

    def ffn_bwd(tag, dh, x_in, xn, g_act, u_act, a_act, f, group):
        dgp, dup, df, gg[tag + "_post"] = _ffn_bwd_act(dh, f, gains[tag + "_post"], w[tag + "_down"], g_act, u_act,
                                                       tag + "_bwd_act")
        gw[tag + "_gate"] = _mm_tn(dgp, xn[None], dgp.shape[2], "dw_" + tag + "_gate")
        gw[tag + "_up"] = _mm_tn(dup, xn[None], dup.shape[2], "dw_" + tag + "_up")
        token = None if group is None else hooks.grads_ready(group + "_first", gw)
        gw[tag + "_down"] = _mm_tn(a_act, df[None], a_act.shape[2], "dw_" + tag + "_down", token)
        g_pre = gains[tag + "_pre"]
        if group is not None:
            token = hooks.grads_ready(group, gw)
            g_pre = g_pre if token is None else g_pre + token[0, 0]
        dx, gg[tag + "_pre"] = _proj_bwd([dgp, dup], [w[tag + "_gate"], w[tag + "_up"]], x_in, g_pre, dh,
                                         tag + "_bwd_in")
        return dx

    dh2 = ffn_bwd("ffn2", dh3, h2, xn2, g2, u2, a2, f2, None)
    dmo, do_a, do_b, gg["mix_post"], gg["out_sb"], gg["out_ch"] = _mix_out_bwd(
        dh2, mo, gains["mix_post"], w_out, o_a, o_b, gains["out_sb"], gains["out_ch"], "mix_out_bwd")
    gw["out"] = row_sharded(_mm_tn(mixed[None], dmo[None], 512, "dw_out"))
    token = hooks.grads_ready("early", gw)
    dq_a, dk_a, dv_a = _sb_bwd(qkv, do_a, o_a, do_a if token is None else token, "sb_bwd")
    token = hooks.grads_sent("early", dq_a)
    dq_b, dk_b, dv_b, dbias = _ch_bwd(qkv, bias, do_b, do_b if token is None else token, "ch_bwd")
    g_rel = _bias_grad(_ch_fold_bias_grad(dbias).transpose(1, 0, 2), "bias_grad")
    dqkv = [dq_a, dk_a, dv_a, dq_b, dk_b, dv_b]
    gw["in"] = _dw_in(un, dqkv, w["in"].shape[2], 512, "dw_in")
    dh1, gg["mix_pre"] = _qkv_bwd_in(dqkv, w["in"], h1, gains["mix_pre"], dh2, "qkv_bwd_in")
    dx = ffn_bwd("ffn1", dh1, x, xn1, g1, u1, a1, f1, "late")
    return loss, dx, gw, gg, g_rel


BIG = ["ffn1_gate", "ffn1_up", "ffn1_down", "in", "out", "ffn2_gate", "ffn2_up", "ffn2_down", "ple_proj", "ple_gate"]
GAINS = ["ffn1_pre", "ffn1_post", "mix_pre", "mix_post", "out_sb", "out_ch", "ffn2_pre", "ffn2_post", "ple_post"]
TRANSPOSED = ("w_ffn1_gate", "w_ffn1_up", "w_ffn2_gate", "w_ffn2_up")
PERMUTED = ("ffn1_gate", "ffn1_up", "ffn1_down", "ffn2_gate", "ffn2_up", "ffn2_down")
W_GROUPS = {"first": ["ffn1_gate", "ffn1_up"], "down": ["ffn1_down"], "in": ["in"],
            "rest": ["out", "ffn2_gate", "ffn2_up", "ffn2_down", "ple_proj", "ple_gate"]}
G_GROUPS = {"early": ["ple_proj", "ple_gate", "ffn2_gate", "ffn2_up", "ffn2_down", "out"],
            "late": ["in", "ffn1_gate", "ffn1_up", "ffn1_down"]}
ORDER = ["g_ffn1_pre", "g_ffn1_post", "w_ffn1_gate", "w_ffn1_up", "w_ffn1_down", "g_mix_pre", "g_mix_post", "w_in",
         "g_out_sb", "g_out_ch", "rel_bias", "w_out", "g_ffn2_pre", "g_ffn2_post", "w_ffn2_gate", "w_ffn2_up",
         "w_ffn2_down", "w_ple_proj", "w_ple_gate", "g_ple_post"]


def kernel(x, p, g_ffn1_pre, g_ffn1_post, w_ffn1_gate, w_ffn1_up, w_ffn1_down, g_mix_pre, g_mix_post, w_in, g_out_sb, g_out_ch, rel_bias, w_out, g_ffn2_pre, g_ffn2_post, w_ffn2_gate, w_ffn2_up, w_ffn2_down, w_ple_proj, w_ple_gate, g_ple_post, loss_target, m_g_ffn1_pre, m_g_ffn1_post, m_w_ffn1_gate, m_w_ffn1_up, m_w_ffn1_down, m_g_mix_pre, m_g_mix_post, m_w_in, m_g_out_sb, m_g_out_ch, m_rel_bias, m_w_out, m_g_ffn2_pre, m_g_ffn2_post, m_w_ffn2_gate, m_w_ffn2_up, m_w_ffn2_down, m_w_ple_proj, m_w_ple_gate, m_g_ple_post, v_g_ffn1_pre, v_g_ffn1_post, v_w_ffn1_gate, v_w_ffn1_up, v_w_ffn1_down, v_g_mix_pre, v_g_mix_post, v_w_in, v_g_out_sb, v_g_out_ch, v_rel_bias, v_w_out, v_g_ffn2_pre, v_g_ffn2_post, v_w_ffn2_gate, v_w_ffn2_up, v_w_ffn2_down, v_w_ple_proj, v_w_ple_gate, v_g_ple_post):
    args = dict(locals())
    take = lambda a, n: a[0].T if n in TRANSPOSED else a[0]
    wts = {n: take(args[n], n) for n in ORDER}
    ms = {n: take(args["m_" + n], n) for n in ORDER}
    vs = {n: take(args["v_" + n], n) for n in ORDER}
    gains = {n: wts["g_" + n][None] for n in GAINS}

    c_idx = lax.axis_index("c").astype(jnp.int32).reshape(1)
    me_idx = (2 * lax.axis_index("x") + lax.axis_index("y")).astype(jnp.int32).reshape(1)
    south = lax.axis_index("c") == 0

    plain = lambda names: [n not in PERMUTED for n in names]
    lands = dict(zip(BIG, _cast_into_own_slot(me_idx, c_idx, [wts["w_" + n] for n in BIG], plain(BIG), "cast_weights")))

    class Overlapped:
        def __init__(self):
            self.started = {}
            self.flying = {}

        def start(self, group, collective_id, after):
            names = W_GROUPS[group]
            self.flying[group] = _gather_start([lands[n] for n in names], plain(names), "gather_%s_start" % group,
                                               collective_id, after)
            return self.flying[group][3]

        def weights(self, group, after=None):
            names = W_GROUPS.get(group)
            token = None
            if group == "first":
                zones = _gather_finish([lands[n] for n in names], plain(names), True, "gather_first")
                token = self.start("rest", 4, self.start("in", 1, self.start("down", 6, zones[0])))
            elif group == "passed":
                names, (send_sem, recv_sem, zones, _) = self.passing
                zones = _pass_wait(send_sem, recv_sem, zones, plain(names), after, "gather_rest_pass_wait")
            else:
                send_sem, recv_sem, zones, _ = self.flying[group]
                zones = _gather_wait(send_sem, recv_sem, zones, plain(names), after, "gather_%s_wait" % group)
                if group == "rest":
                    self.passing = names[1:], _pass_start(zones[1:], plain(names[1:]), "gather_rest_pass_start", 7)
                    names, zones, token = names[:1], zones[:1], self.passing[1][3]
                zones = _gather_finish(zones, plain(names), False, "gather_%s_finish" % group)
            return {n: _slabs(z) for n, z in zip(names, zones)}, token

        def grads_ready(self, group, gw):
            names = G_GROUPS["late"][:-1] if group == "late_first" else G_GROUPS[group]
            perm = [n in PERMUTED for n in names]
            halved = lambda g: g.reshape(N_CHIPS, 2, g.shape[1] // 2, g.shape[2])
            mine = [halved(gw[n][0]) for n in names]
            narrow = [halved(gw[n][1]) for n in names]
            if group == "late_first":
                self.swapping_late = _pair_swap_start(narrow, perm, "grad_pair_swap_start_late", 8)
                return self.swapping_late[3]
            if group == "late":
                send_sem, recv_sem, operands, _ = self.swapping_late
                operands = _pair_swap_wait(send_sem, recv_sem, operands, perm[:-1], narrow[-1],
                                           "grad_pair_swap_wait_late")
                got = operands[len(names) - 1:] + list(_pair_swap(narrow[-1:], perm[-1:], "grad_pair_swap_late"))
                return self.scatter(group, names, perm, mine, got)
            self.swapping = names, perm, mine, _pair_swap_start(narrow, perm, "grad_pair_swap_start_early", 5)
            return self.swapping[3][3]

        def grads_sent(self, group, after):
            names, perm, mine, (send_sem, recv_sem, operands, _) = self.swapping
            operands = _pair_swap_wait(send_sem, recv_sem, operands, perm, after, "grad_pair_swap_wait_early")
            return self.scatter(group, names, perm, mine, operands[len(names):])

        def scatter(self, group, names, perm, mine, got):
            partial = _pair_add(c_idx, mine, got, perm, "grad_pair_add_" + group)
            send_sem, recv_sem, operands, token = _scatter_start(partial, perm, "grad_scatter_start_" + group,
                                                                 {"early": 2, "late": 3}[group])
            self.started[group] = names, perm, send_sem, recv_sem, operands, token
            return token

    def reduce_finish(state, after, tag):
        names, perm, send_sem, recv_sem, operands, _ = state
        operands = _scatter_wait(send_sem, recv_sem, operands, perm, after, "grad_scatter_wait_" + tag)
        n = len(names)
        own = _chip_add(me_idx, operands[:n], operands[n:], perm, "grad_chip_add_" + tag)
        return own, _pair_join(own, "grad_pair_join_" + tag)

    hooks = Overlapped()
    loss, dx, gw, gg, g_rel = _local_step(x[0], p[0, 0], loss_target[0], gains, wts["rel_bias"], hooks)

    grads, delta, new_m, new_v = {}, {}, {}, {}

    def finish(group, after):
        own, other = reduce_finish(hooks.started[group], after, group)
        names = ["w_" + n for n in G_GROUPS[group]]
        g, d, m, v = _adamw_halves(c_idx, [wts[n] for n in names], own, other, [ms[n] for n in names],
                                   [vs[n] for n in names], "adamw_" + group)
        for n, gg_, dd, mm, vv in zip(names, g, d, m, v):
            grads[n], delta[n], new_m[n], new_v[n] = gg_, dd, mm, vv
        return d[0]

    early_done = finish("early", dx)

    pieces = [gg[n].reshape(-1, 128) for n in GAINS] + [jnp.pad(g_rel, ((0, 0), (0, N_REL_PAD - N_REL))).reshape(-1, 128)]
    summed = _all_sum_small(pieces + [loss], early_done, "small_grad_sum")
    finish("late", summed)
    at = 0
    for n, piece in zip(GAINS, pieces[:-1]):
        grads["g_" + n] = summed[at:at + piece.shape[0]].reshape(1, -1)[0]
        at += piece.shape[0]
    grads["rel_bias"] = summed[at:at + pieces[-1].shape[0]].reshape(N_HEADS, N_REL_PAD)[:, :N_REL]
    loss = summed[at + pieces[-1].shape[0], 0]

    small = ["g_" + n for n in GAINS] + ["rel_bias"]
    as_rows = lambda a: (a.reshape(-1, 128) if a.size % 128 == 0 else jnp.pad(a, ((0, 0), (0, N_REL_PAD - N_REL))).reshape(-1, 128))
    d, m, v = _adamw([as_rows(wts[n]) for n in small], [as_rows(grads[n]) for n in small],
                     [as_rows(ms[n]) for n in small], [as_rows(vs[n]) for n in small], 1, "adamw_small")
    for n, dd, mm, vv in zip(small, d, m, v):
        back = (lambda a: a.reshape(N_HEADS, N_REL_PAD)[:, :N_REL]) if n == "rel_bias" else (lambda a: a.reshape(-1))
        delta[n], new_m[n], new_v[n] = back(dd), back(mm), back(vv)

    outs = [loss, dx[None]]
    for table in (grads, delta, new_m, new_v):
        outs += [(table[n].T if n in TRANSPOSED else table[n])[None] for n in ORDER]
    return tuple(outs)
```

```python
import jax
import jax.numpy as jnp
from jax import lax
from jax.experimental import pallas as pl
from jax.experimental.pallas import tpu as pltpu

F32 = jnp.float32
BF16 = jnp.bfloat16
EPS = 1e-6
N_CHIPS = 4
HEAD_DIM = 64
N_HEADS = 8
CHUNK = 64
LOOKBACK = 8
BAND = (LOOKBACK + 1) * CHUNK
PAD = LOOKBACK * CHUNK
REL_CLIP = 128
N_REL = 2 * REL_CLIP + 1
N_REL_PAD = 384
SB_BLOCK = 256
PAIR = 2 * HEAD_DIM
SB_PAIRS = 2
SB_FWD_PAIRS = 4
ATT_SCALE = HEAD_DIM ** -0.5
NEG_INF = -1e30
ROW_BLOCK = 512
WIDE_ROW_BLOCK = 1024
VMEM_LIMIT_WIDE = 56 * 1024 * 1024
VMEM_LIMIT = 48 * 1024 * 1024
MESH = pl.DeviceIdType.MESH

ADAM_LR = 0.001
ADAM_B1 = 0.9
ADAM_B2 = 0.999
ADAM_EPS = 1e-08
ADAM_WD = 0.01
ADAM_STEP = 10

NT = (((1,), (1,)), ((), ()))
TN = (((0,), (0,)), ((), ()))


def _params(n_grid, vmem=None):
    return pltpu.CompilerParams(dimension_semantics=("arbitrary",) * n_grid, vmem_limit_bytes=vmem)


def _hbm(*arrays):
    return [pltpu.with_memory_space_constraint(a, pltpu.HBM) for a in arrays]


def _out(shape, dtype):
    return pltpu.HBM(shape, dtype)


def _dot(a, b, dims=None):
    if dims is None:
        return jnp.dot(a, b, preferred_element_type=F32)
    return lax.dot_general(a, b, dims, preferred_element_type=F32)


def _sigmoid(x):
    return 1.0 / (1.0 + jnp.exp(-x))


def _rms_fwd(x, g):
    r = lax.rsqrt(jnp.mean(x * x, axis=-1, keepdims=True) + EPS)
    return x * r * g


def _rms_bwd(x, g, dy):
    r = lax.rsqrt(jnp.mean(x * x, axis=-1, keepdims=True) + EPS)
    xh = x * r
    dg = jnp.sum(dy * xh, axis=0, keepdims=True)
    t = dy * g
    dx = r * (t - xh * jnp.mean(t * xh, axis=-1, keepdims=True))
    return dx, dg


def _accumulate(ref, val, first):
    @pl.when(first)
    def _():
        ref[...] = val

    @pl.when(jnp.logical_not(first))
    def _():
        ref[...] += val


def _split2(x):
    hi = x.astype(BF16)
    lo = (x - hi.astype(F32)).astype(BF16)
    return hi, lo


def _ffn_fwd(x, g_pre, g_post, wg, wu, wd, name):
    T, D = x.shape
    S, FS, _ = wg.shape
    tm = min(WIDE_ROW_BLOCK, T)

    def body(x_ref, gpre_ref, gpost_ref, wg_ref, wu_ref, wd_ref,
             h_ref, xn_ref, g_ref, u_ref, a_ref, f_ref):
        k = pl.program_id(1)

        @pl.when(k == 0)
        def _():
            xn_ref[...] = _rms_fwd(x_ref[...], gpre_ref[...]).astype(BF16)

        xn = xn_ref[...]
        g = _dot(xn, wg_ref[0], NT)
        u = _dot(xn, wu_ref[0], NT)
        g_ref[0] = g
        u_ref[0] = u
        a = (g * _sigmoid(g) * u).astype(BF16)
        a_ref[0] = a
        _accumulate(f_ref, _dot(a, wd_ref[0]), k == 0)

        @pl.when(k == S - 1)
        def _():
            h_ref[...] = x_ref[...] + 0.5 * _rms_fwd(f_ref[...], gpost_ref[...])

    row = pl.BlockSpec((tm, D), lambda i, k: (i, 0))
    vec = pl.BlockSpec((1, D), lambda i, k: (0, 0))
    act = pl.BlockSpec((1, tm, FS), lambda i, k: (k, i, 0))
    return pl.pallas_call(
        body, name=name, grid=(T // tm, S),
        in_specs=[row, vec, vec] + [pl.BlockSpec((1, FS, D), lambda i, k: (k, 0, 0))] * 3,
        out_specs=[row, row, act, act, act, row],
        out_shape=[_out((T, D), F32), _out((T, D), BF16),
                   _out((S, T, FS), F32), _out((S, T, FS), F32),
                   _out((S, T, FS), BF16), _out((T, D), F32)],
        compiler_params=_params(2, VMEM_LIMIT_WIDE),
    )(*_hbm(x, g_pre, g_post, wg, wu, wd))


def _ffn_up(x, g_pre, wg, wu, name):
    T, D = x.shape
    S, FS, _ = wg.shape
    tm = min(WIDE_ROW_BLOCK, T)

    def body(x_ref, gpre_ref, wg_ref, wu_ref, xn_ref, g_ref, u_ref, a_ref):
        @pl.when(pl.program_id(1) == 0)
        def _():
            xn_ref[...] = _rms_fwd(x_ref[...], gpre_ref[...]).astype(BF16)

        xn = xn_ref[...]
        g = _dot(xn, wg_ref[0], NT)
        u = _dot(xn, wu_ref[0], NT)
        g_ref[0] = g
        u_ref[0] = u
        a_ref[0] = (g * _sigmoid(g) * u).astype(BF16)

    row = pl.BlockSpec((tm, D), lambda i, k: (i, 0))
    act = pl.BlockSpec((1, tm, FS), lambda i, k: (k, i, 0))
    return pl.pallas_call(
        body, name=name, grid=(T // tm, S),
        in_specs=[row, pl.BlockSpec((1, D), lambda i, k: (0, 0))] + [pl.BlockSpec((1, FS, D), lambda i, k: (k, 0, 0))] * 2,
        out_specs=[row, act, act, act],
        out_shape=[_out((T, D), BF16), _out((S, T, FS), F32), _out((S, T, FS), F32), _out((S, T, FS), BF16)],
        compiler_params=_params(2, VMEM_LIMIT_WIDE),
    )(*_hbm(x, g_pre, wg, wu))


def _ffn_down(x, a, g_post, wd, name):
    T, D = x.shape
    S, FS, _ = wd.shape
    tm = min(WIDE_ROW_BLOCK, T)

    def body(x_ref, a_ref, gpost_ref, wd_ref, h_ref, f_ref):
        k = pl.program_id(1)
        _accumulate(f_ref, _dot(a_ref[0], wd_ref[0]), k == 0)

        @pl.when(k == S - 1)
        def _():
            h_ref[...] = x_ref[...] + 0.5 * _rms_fwd(f_ref[...], gpost_ref[...])

    row = pl.BlockSpec((tm, D), lambda i, k: (i, 0))
    return pl.pallas_call(
        body, name=name, grid=(T // tm, S),
        in_specs=[row, pl.BlockSpec((1, tm, FS), lambda i, k: (k, i, 0)), pl.BlockSpec((1, D), lambda i, k: (0, 0)),
                  pl.BlockSpec((1, FS, D), lambda i, k: (k, 0, 0))],
        out_specs=[row, row],
        out_shape=[_out((T, D), F32), _out((T, D), F32)],
        compiler_params=_params(2, VMEM_LIMIT_WIDE),
    )(*_hbm(x, a, g_post, wd))


def _ffn_bwd_act(dh, f, g_post, wd, g_act, u_act, name):
    T, D = dh.shape
    S, FS, _ = wd.shape
    tm = min(WIDE_ROW_BLOCK, T)

    def body(dh_ref, f_ref, gpost_ref, wd_ref, g_ref, u_ref, dgp_ref, dup_ref, df_ref, dgain_ref, df_s):
        i, k = pl.program_id(0), pl.program_id(1)

        @pl.when(k == 0)
        def _():
            df, dgain = _rms_bwd(f_ref[...], gpost_ref[...], 0.5 * dh_ref[...])
            df_s[...] = df.astype(BF16)
            df_ref[...] = df_s[...]
            _accumulate(dgain_ref, dgain, i == 0)

        da = _dot(df_s[...], wd_ref[0], NT)
        g = g_ref[0]
        s = _sigmoid(g)
        dup_ref[0] = (da * (g * s)).astype(BF16)
        dgp_ref[0] = (da * u_ref[0] * (s * (1.0 + g * (1.0 - s)))).astype(BF16)

    row = pl.BlockSpec((tm, D), lambda i, k: (i, 0))
    vec = pl.BlockSpec((1, D), lambda i, k: (0, 0))
    act = pl.BlockSpec((1, tm, FS), lambda i, k: (k, i, 0))
    return pl.pallas_call(
        body, name=name, grid=(T // tm, S),
        in_specs=[row, row, vec, pl.BlockSpec((1, FS, D), lambda i, k: (k, 0, 0)), act, act],
        out_specs=[act, act, row, vec],
        out_shape=[_out((S, T, FS), BF16), _out((S, T, FS), BF16),
                   _out((T, D), BF16), _out((1, D), F32)],
        scratch_shapes=[pltpu.VMEM((tm, D), BF16)],
        compiler_params=_params(2, VMEM_LIMIT_WIDE),
    )(*_hbm(dh, f, g_post, wd, g_act, u_act))


def _proj_bwd(dys, ws, x, g_pre, dh, name):
    T, D = x.shape
    n = len(dys)
    S, N, _ = ws[0].shape
    tm = min(WIDE_ROW_BLOCK, T)

    def body(*refs):
        dy_refs, w_refs = refs[:n], refs[n:2 * n]
        x_ref, gpre_ref, dh_ref, dx_ref, dgain_ref, acc_s = refs[2 * n:]
        i, k = pl.program_id(0), pl.program_id(1)
        part = None
        for dy_ref, w_ref in zip(dy_refs, w_refs):
            term = _dot(dy_ref[0], w_ref[0])
            part = term if part is None else part + term
        _accumulate(acc_s, part, k == 0)

        @pl.when(k == S - 1)
        def _():
            dx, dgain = _rms_bwd(x_ref[...], gpre_ref[...], acc_s[...])
            dx_ref[...] = dh_ref[...] + dx
            _accumulate(dgain_ref, dgain, i == 0)

    row = pl.BlockSpec((tm, D), lambda i, k: (i, 0))
    vec = pl.BlockSpec((1, D), lambda i, k: (0, 0))
    return pl.pallas_call(
        body, name=name, grid=(T // tm, S),
        in_specs=[pl.BlockSpec((1, tm, N), lambda i, k: (k, i, 0))] * n
        + [pl.BlockSpec((1, N, D), lambda i, k: (k, 0, 0))] * n + [row, vec, row],
        out_specs=[row, vec],
        out_shape=[_out((T, D), F32), _out((1, D), F32)],
        scratch_shapes=[pltpu.VMEM((tm, D), F32)],
        compiler_params=_params(2, VMEM_LIMIT_WIDE),
    )(*_hbm(*dys, *ws, x, g_pre, dh))


def _mm_tn(a, b, bm, name, after=None):
    ga, T, M = a.shape
    gb, _, N = b.shape
    b_spec = pl.BlockSpec((1, T, N), (lambda g, m: (g, 0, 0)) if gb > 1 else (lambda g, m: (0, 0, 0)))
    G = max(ga, gb)
    extra = [] if after is None else [after]

    def body(a_ref, b_ref, *refs):
        o_ref, narrow_ref = refs[-2:]
        o_ref[0] = _dot(a_ref[0].astype(BF16), b_ref[0].astype(BF16), TN)
        narrow_ref[0] = o_ref[0].astype(BF16)

    out = pl.BlockSpec((1, bm, N), lambda g, m: (g, m, 0))
    return pl.pallas_call(
        body, name=name, grid=(G, M // bm),
        in_specs=[pl.BlockSpec((1, T, bm), (lambda g, m: (g, 0, m)) if ga > 1 else (lambda g, m: (0, 0, m))), b_spec]
        + [ANY] * len(extra),
        out_specs=[out, out],
        out_shape=[_out((G, M, N), F32), _out((G, M, N), BF16)],
        compiler_params=_params(2, VMEM_LIMIT),
    )(*_hbm(a, b), *extra)


QKV_PIECE = 256


def _qkv_shard(dy_refs, k, n_col):
    width = dy_refs[0].shape[1]
    parts = []
    for col in range(k * n_col, (k + 1) * n_col, QKV_PIECE):
        parts.append(dy_refs[col // width][:, col % width:col % width + QKV_PIECE])
    return jnp.concatenate(parts, axis=1)


def _qkv_bwd_in(dys, w, x, g_pre, dh, name):
    T, D = x.shape
    n = len(dys)
    S, _, N = w.shape
    tm = min(WIDE_ROW_BLOCK, T)

    def body(*refs):
        dy_refs = refs[:n]
        w_ref, x_ref, gpre_ref, dh_ref, dx_ref, dgain_ref, acc_s = refs[n:]
        i, k = pl.program_id(0), pl.program_id(1)
        for shard in range(S):
            @pl.when(k == shard)
            def _(shard=shard):
                part = _dot(_qkv_shard(dy_refs, shard, N), w_ref[0], NT)
                if shard == 0:
                    acc_s[...] = part
                else:
                    acc_s[...] += part

        @pl.when(k == S - 1)
        def _():
            dx, dgain = _rms_bwd(x_ref[...], gpre_ref[...], acc_s[...])
            dx_ref[...] = dh_ref[...] + dx
            _accumulate(dgain_ref, dgain, i == 0)

    row = pl.BlockSpec((tm, D), lambda i, k: (i, 0))
    vec = pl.BlockSpec((1, D), lambda i, k: (0, 0))
    return pl.pallas_call(
        body, name=name, grid=(T // tm, S),
        in_specs=[pl.BlockSpec((tm, dy.shape[1]), lambda i, k: (i, 0)) for dy in dys]
        + [pl.BlockSpec((1, D, N), lambda i, k: (k, 0, 0)), row, vec, row],
        out_specs=[row, vec],
        out_shape=[_out((T, D), F32), _out((1, D), F32)],
        scratch_shapes=[pltpu.VMEM((tm, D), F32)],
        compiler_params=_params(2, VMEM_LIMIT_WIDE),
    )(*_hbm(*dys, w, x, g_pre, dh))


def _dw_in(a, dys, n_col, bm, name):
    T, M = a.shape
    n = len(dys)
    S = n * dys[0].shape[1] // n_col

    def body(*refs):
        a_ref, dy_refs = refs[0], refs[1:1 + n]
        o_ref, narrow_ref = refs[1 + n:]
        k = pl.program_id(1)
        for shard in range(S):
            @pl.when(k == shard)
            def _(shard=shard):
                o_ref[0] = _dot(a_ref[...], _qkv_shard(dy_refs, shard, n_col), TN)
                narrow_ref[0] = o_ref[0].astype(BF16)

    out = pl.BlockSpec((1, bm, n_col), lambda m, k: (k, m, 0))
    return pl.pallas_call(
        body, name=name, grid=(M // bm, S),
        in_specs=[pl.BlockSpec((T, bm), lambda m, k: (0, m))]
        + [pl.BlockSpec((T, dy.shape[1]), lambda m, k: (0, 0)) for dy in dys],
        out_specs=[out, out],
        out_shape=[_out((S, M, n_col), F32), _out((S, M, n_col), BF16)],
        compiler_params=_params(2, VMEM_LIMIT_WIDE),
    )(*_hbm(a, *dys))


def _norm_proj(x, g_pre, w, name):
    T, D = x.shape
    S, _, N = w.shape
    tm = min(WIDE_ROW_BLOCK, T)

    def body(x_ref, g_ref, w_ref, o_ref, xn_ref, xn_s):
        @pl.when(pl.program_id(1) == 0)
        def _():
            xn_s[...] = _rms_fwd(x_ref[...], g_ref[...]).astype(BF16)
            xn_ref[...] = xn_s[...]

        o_ref[...] = _dot(xn_s[...], w_ref[0]).astype(BF16)

    row = pl.BlockSpec((tm, D), lambda i, k: (i, 0))
    return pl.pallas_call(
        body, name=name, grid=(T // tm, S),
        in_specs=[row, pl.BlockSpec((1, D), lambda i, k: (0, 0)), pl.BlockSpec((1, D, N), lambda i, k: (k, 0, 0))],
        out_specs=[pl.BlockSpec((tm, N), lambda i, k: (i, k)), row],
        out_shape=[_out((T, S * N), BF16), _out((T, D), BF16)],
        scratch_shapes=[pltpu.VMEM((tm, D), BF16)],
        compiler_params=_params(2, VMEM_LIMIT_WIDE),
    )(*_hbm(x, g_pre, w))


def _mix_out_fwd(h, o_a, o_b, g_sb, g_ch, w_out, g_post, name):
    T, D = h.shape
    W = g_sb.shape[1]
    tm = min(WIDE_ROW_BLOCK, T)

    def body(h_ref, oa_ref, ob_ref, gsb_ref, gch_ref, w_ref, gpost_ref, h2_ref, mixed_ref, mo_ref):
        mixed_ref[:, :W] = _rms_fwd(oa_ref[...], gsb_ref[...]).astype(BF16)
        mixed_ref[:, W:] = _rms_fwd(ob_ref[...], gch_ref[...]).astype(BF16)
        mo = _dot(mixed_ref[...], w_ref[...])
        mo_ref[...] = mo
        h2_ref[...] = h_ref[...] + _rms_fwd(mo, gpost_ref[...])

    row = pl.BlockSpec((tm, D), lambda i: (i, 0))
    part = pl.BlockSpec((tm, W), lambda i: (i, 0))
    half = pl.BlockSpec((1, W), lambda i: (0, 0))
    return pl.pallas_call(
        body, name=name, grid=(T // tm,),
        in_specs=[row, part, part, half, half, pl.BlockSpec((D, D), lambda i: (0, 0)), pl.BlockSpec((1, D), lambda i: (0, 0))],
        out_specs=[row, row, row],
        out_shape=[_out((T, D), F32), _out((T, D), BF16),
                   _out((T, D), F32)],
        compiler_params=_params(1, VMEM_LIMIT_WIDE),
    )(*_hbm(h, o_a, o_b, g_sb, g_ch, w_out, g_post))


def _mix_out_bwd(dh, mo, g_post, w_out, o_a, o_b, g_sb, g_ch, name):
    T, D = dh.shape
    W = g_sb.shape[1]
    tm = min(WIDE_ROW_BLOCK, T)

    def body(dh_ref, mo_ref, gpost_ref, w_ref, oa_ref, ob_ref, gsb_ref, gch_ref,
             dmo_ref, doa_ref, dob_ref, dgpost_ref, dgsb_ref, dgch_ref):
        first = pl.program_id(0) == 0
        dmo, dgpost = _rms_bwd(mo_ref[...], gpost_ref[...], dh_ref[...])
        dmo_ref[...] = dmo.astype(BF16)
        dmix = _dot(dmo_ref[...], w_ref[...], NT)
        doa_ref[...], dgsb = _rms_bwd(oa_ref[...], gsb_ref[...], dmix[:, :W])
        dob_ref[...], dgch = _rms_bwd(ob_ref[...], gch_ref[...], dmix[:, W:])
        _accumulate(dgpost_ref, dgpost, first)
        _accumulate(dgsb_ref, dgsb, first)
        _accumulate(dgch_ref, dgch, first)

    row = pl.BlockSpec((tm, D), lambda i: (i, 0))
    part = pl.BlockSpec((tm, W), lambda i: (i, 0))
    vec = pl.BlockSpec((1, D), lambda i: (0, 0))
    half = pl.BlockSpec((1, W), lambda i: (0, 0))
    return pl.pallas_call(
        body, name=name, grid=(T // tm,),
        in_specs=[row, row, vec, pl.BlockSpec((D, D), lambda i: (0, 0)), part, part, half, half],
        out_specs=[row, part, part, vec, half, half],
        out_shape=[_out((T, D), BF16), _out((T, W), F32),
                   _out((T, W), F32), _out((1, D), F32),
                   _out((1, W), F32), _out((1, W), F32)],
        compiler_params=_params(1, VMEM_LIMIT_WIDE),
    )(*_hbm(dh, mo, g_post, w_out, o_a, o_b, g_sb, g_ch))


def _ple_loss(h, p, target, w_proj, w_gate, g_post, name):
    T, D = h.shape
    P = p.shape[1]
    S = N_CHIPS
    C = D // S
    tm = min(ROW_BLOCK, T)

    def body(h_ref, p_ref, t_ref, wp_ref, wg_ref, g_ref, loss_ref, dh_ref, dproj_ref, dgate_ref, dgain_ref):
        first = pl.program_id(0) == 0
        h3 = h_ref[...]
        proj = _dot(p_ref[...].astype(BF16), wp_ref[...])
        s = _sigmoid(_dot(h3.astype(BF16), wg_ref[...]))
        e = proj * s
        diff = h3 + _rms_fwd(e, g_ref[...]) - t_ref[...]
        part = 0.5 * jnp.sum(jnp.mean(diff * diff, axis=-1, keepdims=True), axis=0, keepdims=True)
        _accumulate(loss_ref, jnp.broadcast_to(part, loss_ref.shape), first)
        dy = diff * (1.0 / D)
        de, dgain = _rms_bwd(e, g_ref[...], dy)
        _accumulate(dgain_ref, dgain, first)
        dproj = (de * s).astype(BF16)
        for j in range(S):
            dproj_ref[j] = dproj[:, j * C:(j + 1) * C]
        dgate_ref[...] = (de * proj * s * (1.0 - s)).astype(BF16)
        dh_ref[...] = dy + _dot(dgate_ref[...], wg_ref[...], NT)

    row = pl.BlockSpec((tm, D), lambda i: (i, 0))
    vec = pl.BlockSpec((1, D), lambda i: (0, 0))
    return pl.pallas_call(
        body, name=name, grid=(T // tm,),
        in_specs=[row, pl.BlockSpec((tm, P), lambda i: (i, 0)), row,
                  pl.BlockSpec((P, D), lambda i: (0, 0)), pl.BlockSpec((D, D), lambda i: (0, 0)), vec],
        out_specs=[pl.BlockSpec((8, 128), lambda i: (0, 0)), row,
                   pl.BlockSpec((S, tm, C), lambda i: (0, i, 0)), row, vec],
        out_shape=[_out((8, 128), F32), _out((T, D), F32),
                   _out((S, T, C), BF16), _out((T, D), BF16),
                   _out((1, D), F32)],
        compiler_params=_params(1, VMEM_LIMIT_WIDE),
    )(*_hbm(h, p, target, w_proj, w_gate, g_post))


def _sb_scores(q, kj, mask):
    z = _dot(q, kj, NT)
    sp = jnp.maximum(z, 0.0) + jnp.log(1.0 + jnp.exp(-jnp.abs(z)))
    return z, sp if mask is None else jnp.where(mask, sp, 0.0)


def _strict_causal():
    rows = lax.broadcasted_iota(jnp.int32, (SB_BLOCK, SB_BLOCK), 0)
    cols = lax.broadcasted_iota(jnp.int32, (SB_BLOCK, SB_BLOCK), 1)
    return cols < rows


def _tri(cmp):
    r = lax.broadcasted_iota(jnp.int32, (2 * SB_BLOCK, SB_BLOCK), 0) % SB_BLOCK
    c = lax.broadcasted_iota(jnp.int32, (2 * SB_BLOCK, SB_BLOCK), 1)
    return jnp.where(cmp(r, c), 1.0, 0.0).astype(BF16)


def _cum(x, tri):
    return _dot(jnp.concatenate(_split2(x), axis=1), tri)


def _pair_lanes():
    lane = lax.broadcasted_iota(jnp.int32, (1, PAIR), 1)
    return [lane < HEAD_DIM, lane >= HEAD_DIM]


def _only(lanes, x):
    return jnp.where(lanes, x, jnp.zeros_like(x))


def _sb_fwd(qkv, name):
    T = qkv.shape[0]
    B = SB_BLOCK
    W = SB_FWD_PAIRS * PAIR
    steps = N_HEADS // (2 * SB_FWD_PAIRS)
    heads = [(p, h) for p in range(SB_FWD_PAIRS) for h in range(2)]

    def body(q_ref, k_ref, v_ref, o_ref):
        i = pl.program_id(1)
        after = _tri(lambda r, c: r > c)
        lanes = _pair_lanes()
        cols = [slice(p * PAIR, (p + 1) * PAIR) for p in range(SB_FWD_PAIRS)]
        q = {(p, h): _only(lanes[h], q_ref[:, cols[p]] * ATT_SCALE) for p, h in heads}

        def tiles(j, carries, mask):
            at = pl.ds(pl.multiple_of(j * B, B), B)
            scores = [_sb_scores(q[ph], k_ref[at, cols[ph[0]]], mask) for ph in heads]
            laters = [_cum(sp, after) for _, sp in scores]
            out = []
            for ph, (z, sp), later, (run, acc) in zip(heads, scores, laters, carries):
                a = jnp.exp(z - sp - later - run)
                if mask is not None:
                    a = jnp.where(mask, a, 0.0)
                out.append((run + later[:, 0:1] + sp[:, 0:1],
                            acc + _dot(a.astype(BF16), _only(lanes[ph[1]], v_ref[at, cols[ph[0]]]))))
            return tuple(out)

        zero = (jnp.zeros((B, 1), F32), jnp.zeros((B, PAIR), F32))
        carries = tiles(i, (zero,) * len(heads), _strict_causal())
        carries = lax.fori_loop(0, i, lambda jj, cs: tiles(i - 1 - jj, cs, None), carries)
        for p in range(SB_FWD_PAIRS):
            o_ref[:, cols[p]] = carries[2 * p][1] + carries[2 * p + 1][1]

    blk = lambda off: pl.BlockSpec((B, W), lambda g, i: (i, g + off))
    full = lambda off: pl.BlockSpec((T, W), lambda g, i: (0, g + off))
    return pl.pallas_call(
        body, name=name, grid=(steps, T // B),
        in_specs=[blk(0), full(steps), full(2 * steps)],
        out_specs=blk(0),
        out_shape=_out((T, N_HEADS * HEAD_DIM), F32),
        compiler_params=_params(2, VMEM_LIMIT),
    )(*_hbm(qkv, qkv, qkv))


def _sb_bwd(qkv, do, o, after, name):
    T = qkv.shape[0]
    B = SB_BLOCK
    W = SB_PAIRS * PAIR
    steps = N_HEADS // (2 * SB_PAIRS)
    n_blocks = T // B
    heads = [(p, h) for p in range(SB_PAIRS) for h in range(2)]

    def body(q_ref, k_ref, v_ref, do_ref, o_ref, dq_ref, dk_ref, dv_ref, dk_s, dv_s):
        i = pl.program_id(1)

        @pl.when(i == 0)
        def _():
            dk_s[...] = jnp.zeros_like(dk_s)
            dv_s[...] = jnp.zeros_like(dv_s)

        after = _tri(lambda r, c: r > c)
        since = _tri(lambda r, c: r >= c)
        lanes = _pair_lanes()
        cols = [slice(p * PAIR, (p + 1) * PAIR) for p in range(SB_PAIRS)]
        q = {(p, h): _only(lanes[h], q_ref[:, cols[p]] * ATT_SCALE) for p, h in heads}
        do = {(p, h): _only(lanes[h], do_ref[:, cols[p]].astype(BF16)) for p, h in heads}
        total = {ph: jnp.sum(do[ph].astype(F32) * o_ref[:, cols[ph[0]]], axis=1, keepdims=True) for ph in heads}

        def tiles(j, carries, mask):
            at = pl.ds(pl.multiple_of(j * B, B), B)
            ks = [k_ref[at, c] for c in cols]
            vs = [v_ref[at, c] for c in cols]
            scores = [_sb_scores(q[ph], ks[ph[0]], mask) for ph in heads]
            laters = [_cum(sp, after) for _, sp in scores]
            das = [_dot(do[ph], vs[ph[0]], NT) for ph in heads]
            a_s, gs = [], []
            for (z, sp), later, da, carry in zip(scores, laters, das, carries):
                a = jnp.exp(z - sp - later - carry[0])
                if mask is not None:
                    a = jnp.where(mask, a, 0.0)
                a = a.astype(BF16)
                a_s.append(a)
                gs.append(a.astype(F32) * da)
            sinces = [_cum(g, since) for g in gs]
            dzs = []
            for ph, (_, sp), g, from_s, carry in zip(heads, scores, gs, sinces, carries):
                g_before = total[ph] - carry[1] - from_s
                fail = jnp.exp(-sp)
                dz = fail * (g + g_before) - g_before
                if mask is not None:
                    dz = jnp.where(mask, dz, 0.0)
                dzs.append(dz.astype(BF16))
            out = []
            for ph, (_, sp), a, dz, later, from_s, carry in zip(heads, scores, a_s, dzs, laters, sinces, carries):
                dk_s[at, cols[ph[0]]] += _dot(dz, q[ph], TN)
                dv_s[at, cols[ph[0]]] += _dot(a, do[ph], TN)
                out.append((carry[0] + later[:, 0:1] + sp[:, 0:1], carry[1] + from_s[:, 0:1],
                            carry[2] + _dot(dz, _only(lanes[ph[1]], ks[ph[0]]))))
            return tuple(out)

        col = jnp.zeros((B, 1), F32)
        zero = (col, col, jnp.zeros((B, PAIR), F32))
        carries = tiles(i, (zero,) * len(heads), _strict_causal())
        last = lax.fori_loop(0, i, lambda jj, cs: tiles(i - 1 - jj, cs, None), carries)
        for p in range(SB_PAIRS):
            dq_ref[:, cols[p]] = ((last[2 * p][2] + last[2 * p + 1][2]) * ATT_SCALE).astype(BF16)

        @pl.when(i == n_blocks - 1)
        def _():
            dk_ref[...] = dk_s[...].astype(BF16)
            dv_ref[...] = dv_s[...].astype(BF16)

    blk = lambda off: pl.BlockSpec((B, W), lambda g, i: (i, g + off))
    full = lambda off: pl.BlockSpec((T, W), lambda g, i: (0, g + off))
    out = _out((T, N_HEADS * HEAD_DIM), BF16)
    return pl.pallas_call(
        lambda after_ref, *refs: body(*refs), name=name, grid=(steps, n_blocks),
        in_specs=[ANY, blk(0), full(steps), full(2 * steps), blk(0), blk(0)],
        out_specs=[blk(0), full(0), full(0)],
        out_shape=[out, out, out],
        scratch_shapes=[pltpu.VMEM((T, W), F32)] * 2,
        compiler_params=_params(2, VMEM_LIMIT),
    )(after, *_hbm(qkv, qkv, qkv, do, o))


NEAR = BAND - PAD + REL_CLIP
FAR = BAND - NEAR
NEAR_REL = 2 * REL_CLIP
BIAS_ROWS = 8


def _rel_onehot(i, transposed):
    shape = (NEAR, NEAR_REL) if transposed else (NEAR_REL, NEAR)
    j = FAR + lax.broadcasted_iota(jnp.int32, shape, 0 if transposed else 1)
    r = lax.broadcasted_iota(jnp.int32, shape, 1 if transposed else 0)
    idx = jnp.clip(i + PAD - j, -REL_CLIP, REL_CLIP) + REL_CLIP
    return jnp.where(idx - 1 == r, 1.0, 0.0).astype(BF16)


def _bias_table(rel_bias, name):
    def body(near_ref, far_ref, o_ref):
        rb = near_ref[...]
        hi, lo = _split2(rb)
        lo2 = (rb - hi.astype(F32) - lo.astype(F32)).astype(BF16)
        far = jnp.broadcast_to(far_ref[...], (N_HEADS, FAR))
        for k in range(BIAS_ROWS):
            onehot = _rel_onehot(pl.program_id(0) * BIAS_ROWS + k, False)
            o_ref[k, :, :FAR] = far
            o_ref[k, :, FAR:] = _dot(hi, onehot) + _dot(lo, onehot) + _dot(lo2, onehot)

    return pl.pallas_call(
        body, name=name, grid=(CHUNK // BIAS_ROWS,),
        in_specs=[pl.BlockSpec((N_HEADS, NEAR_REL), lambda i: (0, 0)), pl.BlockSpec((N_HEADS, 1), lambda i: (0, 0))],
        out_specs=pl.BlockSpec((BIAS_ROWS, N_HEADS, BAND), lambda i: (i, 0, 0)),
        out_shape=_out((CHUNK, N_HEADS, BAND), F32),
        compiler_params=_params(1),
    )(*_hbm(rel_bias[:, 1:], rel_bias[:, N_REL - 1:]))


def _bias_grad(dbias_t, name):
    def body(d_ref, near_ref, far_ref):
        near, far = None, None
        for k in range(BIAS_ROWS):
            onehot = _rel_onehot(pl.program_id(0) * BIAS_ROWS + k, True)
            hi, lo = _split2(d_ref[k, :, FAR:])
            part = _dot(hi, onehot) + _dot(lo, onehot)
            rest = jnp.sum(d_ref[k, :, :FAR], axis=1, keepdims=True)
            near, far = (part, rest) if near is None else (near + part, far + rest)
        first = pl.program_id(0) == 0
        _accumulate(near_ref, near, first)
        _accumulate(far_ref, jnp.broadcast_to(far, far_ref.shape), first)

    near, far = pl.pallas_call(
        body, name=name, grid=(CHUNK // BIAS_ROWS,),
        in_specs=[pl.BlockSpec((BIAS_ROWS, N_HEADS, BAND), lambda i: (i, 0, 0))],
        out_specs=[pl.BlockSpec((N_HEADS, NEAR_REL), lambda i: (0, 0)), pl.BlockSpec((N_HEADS, 128), lambda i: (0, 0))],
        out_shape=[_out((N_HEADS, NEAR_REL), F32), _out((N_HEADS, 128), F32)],
        compiler_params=_params(1),
    )(*_hbm(dbias_t))
    return jnp.pad(near, ((0, 0), (1, 0))).at[:, N_REL - 1].add(far[:, 0])


def _ch_probs(scores, bias, valid):
    z = jnp.where(valid, scores * ATT_SCALE + bias, NEG_INF)
    e = jnp.exp(z - jnp.max(z, axis=-1, keepdims=True))
    return e / jnp.sum(e, axis=-1, keepdims=True)


CH_HEADS = [(pair, h) for pair in range(N_HEADS // 2) for h in range(2)]
CH_COLS = [slice(pair * PAIR, (pair + 1) * PAIR) for pair in range(N_HEADS // 2)]


CH_GROUP = 2
CH_Q = CH_GROUP * CHUNK
CH_WIN = (LOOKBACK + CH_GROUP) * CHUNK


def _ch_valid(n):
    row_chunk = lax.broadcasted_iota(jnp.int32, (CH_Q, CH_WIN), 0) // CHUNK
    slot = lax.broadcasted_iota(jnp.int32, (CH_Q, CH_WIN), 1)
    ahead = slot // CHUNK - row_chunk
    return (ahead >= 0) & (ahead <= LOOKBACK) & (n * CH_Q + slot >= PAD)


def _ch_group_bias(bias):
    shifted = [jnp.pad(bias, ((0, 0), (0, 0), (c * CHUNK, (CH_GROUP - 1 - c) * CHUNK))) for c in range(CH_GROUP)]
    return jnp.concatenate(shifted, axis=1)


def _ch_fold_bias_grad(dbias):
    parts = [dbias[:, c * CHUNK:(c + 1) * CHUNK, c * CHUNK:c * CHUNK + BAND] for c in range(CH_GROUP)]
    return sum(parts[1:], parts[0])


def _ch_fwd(qkv, bias, name):
    T = qkv.shape[0]
    W = N_HEADS * HEAD_DIM

    def body(q_ref, k_ref, v_ref, b_ref, o_ref, kp, vp):
        n = pl.program_id(0)

        @pl.when(n == 0)
        def _():
            _ch_load_padded(k_ref, v_ref, kp, vp)

        win = pl.ds(pl.multiple_of(n * CH_Q, CH_Q), CH_WIN)
        valid = _ch_valid(n)
        lanes = _pair_lanes()
        scores = [_dot(_only(lanes[h], q_ref[:, CH_COLS[pair]]), kp[win, CH_COLS[pair]], NT) for pair, h in CH_HEADS]
        probs = [_ch_probs(s, b_ref[2 * pair + h], valid).astype(BF16) for s, (pair, h) in zip(scores, CH_HEADS)]
        outs = [_dot(p, _only(lanes[h], vp[win, CH_COLS[pair]])) for p, (pair, h) in zip(probs, CH_HEADS)]
        for pair, cols in enumerate(CH_COLS):
            o_ref[:, cols] = outs[2 * pair] + outs[2 * pair + 1]

    full = lambda col: pl.BlockSpec((T, W), lambda n: (0, col))
    return pl.pallas_call(
        body, name=name, grid=(T // CH_Q,),
        in_specs=[pl.BlockSpec((CH_Q, W), lambda n: (n, 3)), full(4), full(5),
                  pl.BlockSpec((N_HEADS, CH_Q, CH_WIN), lambda n: (0, 0, 0))],
        out_specs=pl.BlockSpec((CH_Q, W), lambda n: (n, 0)),
        out_shape=_out((T, W), F32),
        scratch_shapes=[pltpu.VMEM((PAD + T, W), BF16)] * 2,
        compiler_params=_params(1, VMEM_LIMIT),
    )(*_hbm(qkv, qkv, qkv, bias))


def _ch_load_padded(k_ref, v_ref, kp, vp):
    for src, dst in ((k_ref, kp), (v_ref, vp)):
        dst[:PAD, :] = jnp.zeros((PAD, dst.shape[1]), dst.dtype)
        dst[PAD:, :] = src[...]


def _ch_bwd(qkv, bias, do, after, name):
    T = qkv.shape[0]
    W = N_HEADS * HEAD_DIM
    n_chunks = T // CH_Q

    def body(q_ref, k_ref, v_ref, b_ref, do_ref, dq_ref, dk_ref, dv_ref, db_ref, kp, vp, dk_s, dv_s):
        n = pl.program_id(0)

        @pl.when(n == 0)
        def _():
            _ch_load_padded(k_ref, v_ref, kp, vp)
            dk_s[...] = jnp.zeros_like(dk_s)
            dv_s[...] = jnp.zeros_like(dv_s)
            db_ref[...] = jnp.zeros_like(db_ref)

        win = pl.ds(pl.multiple_of(n * CH_Q, CH_Q), CH_WIN)
        valid = _ch_valid(n)
        lanes = _pair_lanes()
        kws = [kp[win, cols] for cols in CH_COLS]
        vws = [vp[win, cols] for cols in CH_COLS]
        qs = [_only(lanes[h], q_ref[:, CH_COLS[pair]]) for pair, h in CH_HEADS]
        dos = [_only(lanes[h], do_ref[:, CH_COLS[pair]].astype(BF16)) for pair, h in CH_HEADS]
        scores = [_dot(q, kws[pair], NT) for q, (pair, _) in zip(qs, CH_HEADS)]
        dps = [_dot(do, vws[pair], NT) for do, (pair, _) in zip(dos, CH_HEADS)]
        probs = [_ch_probs(s, b_ref[2 * pair + h], valid) for s, (pair, h) in zip(scores, CH_HEADS)]
        dzs = [p * (dp - jnp.sum(dp * p, axis=-1, keepdims=True)) for p, dp in zip(probs, dps)]
        for k, dz in enumerate(dzs):
            db_ref[k] += dz
        dzbs = [(dz * ATT_SCALE).astype(BF16) for dz in dzs]
        dqs = [_dot(dz, _only(lanes[h], kws[pair])) for dz, (pair, h) in zip(dzbs, CH_HEADS)]
        dks = [_dot(dz, q, TN) for dz, q in zip(dzbs, qs)]
        dvs = [_dot(p.astype(BF16), do, TN) for p, do in zip(probs, dos)]
        for pair, cols in enumerate(CH_COLS):
            dq_ref[:, cols] = (dqs[2 * pair] + dqs[2 * pair + 1]).astype(BF16)
            dk_s[win, cols] += dks[2 * pair] + dks[2 * pair + 1]
            dv_s[win, cols] += dvs[2 * pair] + dvs[2 * pair + 1]

        @pl.when(n == n_chunks - 1)
        def _():
            dk_ref[...] = dk_s[PAD:, :].astype(BF16)
            dv_ref[...] = dv_s[PAD:, :].astype(BF16)

    full = lambda col: pl.BlockSpec((T, W), lambda n: (0, col))
    blk = lambda col: pl.BlockSpec((CH_Q, W), lambda n: (n, col))
    tab = pl.BlockSpec((N_HEADS, CH_Q, CH_WIN), lambda n: (0, 0, 0))
    out = _out((T, W), BF16)
    return pl.pallas_call(
        lambda after_ref, *refs: body(*refs), name=name, grid=(n_chunks,),
        in_specs=[ANY, blk(3), full(4), full(5), tab, blk(0)],
        out_specs=[blk(0), full(0), full(0), tab],
        out_shape=[out, out, out, _out((N_HEADS, CH_Q, CH_WIN), F32)],
        scratch_shapes=[pltpu.VMEM((PAD + T, W), BF16)] * 2 + [pltpu.VMEM((PAD + T, W), F32)] * 2,
        compiler_params=_params(1, VMEM_LIMIT),
    )(after, *_hbm(qkv, qkv, qkv, bias, do))


def _rows_split(a, parts):
    return a.reshape(a.shape[:-2] + (parts, a.shape[-2] // parts, a.shape[-1]))


def _cast_into_own_slot(me, c, ws, in_chip_order, name):
    parts = 2
    ws = [_rows_split(_rows_split(w, 2), parts) for w in ws]
    n = len(ws)

    def body(me_ref, c_ref, *refs):
        for src, dst in zip(refs[:n], refs[n:]):
            dst[0, 0, 0] = src[0, 0].astype(BF16)

    def specs(w, plain):
        block = (1, 1) + w.shape[2:]
        if plain:
            return (pl.BlockSpec(block, lambda d, r, me_ref, c_ref: (d, r, 0, 0)),
                    pl.BlockSpec((1,) + block, lambda d, r, me_ref, c_ref: (me_ref[0], d, r, 0, 0)))
        return (pl.BlockSpec(block, lambda d, r, me_ref, c_ref: (d ^ c_ref[0], r, 0, 0)),
                pl.BlockSpec((1,) + block, lambda d, r, me_ref, c_ref: (0, d, r, 0, 0)))

    both = [specs(w, plain) for w, plain in zip(ws, in_chip_order)]
    outs = pl.pallas_call(
        body, name=name,
        grid_spec=pltpu.PrefetchScalarGridSpec(
            num_scalar_prefetch=2, grid=(2, parts),
            in_specs=[s[0] for s in both], out_specs=[s[1] for s in both]),
        out_shape=[_out((N_CHIPS,) + w.shape, BF16) for w in ws],
        compiler_params=_params(2, VMEM_LIMIT),
    )(me, c, *_hbm(*ws))
    return [o.reshape(N_CHIPS, 2, o.shape[2] * o.shape[3], o.shape[4]) for o in outs]


def _zone_slots(in_chip_order):
    x, y, c, _ = _place()
    me = 2 * x + y
    if in_chip_order:
        return (me, c), (lambda r: (me, c)), (lambda r: (me ^ r, c)), (lambda r: (me ^ r, c))
    return (0, 0), (lambda r: (r, 0)), (lambda r: (r, 0)), (lambda r: (r, 1))


def _pair_add(c, mine, got, permuted, name):
    parts = 2
    mine = [_rows_split(m, parts) for m in mine]
    got = [_rows_split(g, parts) for g in got]
    n = len(mine)

    def body(c_ref, *refs):
        for a, b, o in zip(refs[:n], refs[n:2 * n], refs[2 * n:]):
            o[0, 0] = (a[0, 0, 0] + b[0, 0].astype(F32)).astype(BF16)

    def mine_spec(m, perm):
        if perm:
            return pl.BlockSpec((1, 1, 1) + m.shape[3:], lambda j, r, c_ref: (j, 0, r, 0, 0))
        return pl.BlockSpec((1, 1, 1) + m.shape[3:], lambda j, r, c_ref: (j, c_ref[0], r, 0, 0))

    def got_spec(g):
        return pl.BlockSpec((1, 1) + g.shape[2:], lambda j, r, c_ref: (j, r, 0, 0))

    outs = pl.pallas_call(
        body, name=name,
        grid_spec=pltpu.PrefetchScalarGridSpec(
            num_scalar_prefetch=1, grid=(N_CHIPS, parts),
            in_specs=[mine_spec(m, perm) for m, perm in zip(mine, permuted)] + [got_spec(g) for g in got],
            out_specs=[got_spec(g) for g in got]),
        out_shape=[_out(g.shape, BF16) for g in got],
        compiler_params=_params(2, VMEM_LIMIT),
    )(c, *_hbm(*mine, *got))
    return [o.reshape(o.shape[0], o.shape[1] * o.shape[2], o.shape[3]) for o in outs]


def _chip_add(me, partials, landed, permuted, name):
    parts = 2
    ps = [_rows_split(x, parts) for x in partials]
    ls = [_rows_split(x, parts) for x in landed]
    n = len(ps)

    def body(me_ref, *refs):
        for own, got, o in zip(refs[:n], refs[n:2 * n], refs[2 * n:]):
            acc = own[0, 0].astype(F32)
            for r in range(N_CHIPS - 1):
                acc = acc + got[r, 0].astype(F32)
            o[0] = acc

    def own_spec(x, perm):
        if perm:
            return pl.BlockSpec((1, 1) + x.shape[2:], lambda r, me_ref: (0, r, 0, 0))
        return pl.BlockSpec((1, 1) + x.shape[2:], lambda r, me_ref: (me_ref[0], r, 0, 0))

    outs = pl.pallas_call(
        body, name=name,
        grid_spec=pltpu.PrefetchScalarGridSpec(
            num_scalar_prefetch=1, grid=(parts,),
            in_specs=[own_spec(x, perm) for x, perm in zip(ps, permuted)]
            + [pl.BlockSpec((N_CHIPS - 1, 1) + x.shape[2:], lambda r, me_ref: (0, r, 0, 0)) for x in ls],
            out_specs=[pl.BlockSpec((1,) + x.shape[2:], lambda r, me_ref: (r, 0, 0)) for x in ps]),
        out_shape=[_out(x.shape[1:], F32) for x in ps],
        compiler_params=_params(1, VMEM_LIMIT),
    )(me, *_hbm(*ps, *ls))
    return [o.reshape(o.shape[0] * o.shape[1], o.shape[2]) for o in outs]


def _adamw_math(w, g, m, v):
    m = ADAM_B1 * m + (1.0 - ADAM_B1) * g
    v = ADAM_B2 * v + (1.0 - ADAM_B2) * (g * g)
    m_hat = m / (1.0 - ADAM_B1 ** ADAM_STEP)
    v_hat = v / (1.0 - ADAM_B2 ** ADAM_STEP)
    delta = -ADAM_LR * (m_hat / (jnp.sqrt(v_hat) + ADAM_EPS) + ADAM_WD * w)
    return delta, m, v


def _adamw(ws, gs, ms, vs, parts, name):
    n = len(ws)
    flat = [_rows_split(a, parts) for a in (*ws, *gs, *ms, *vs)]

    def body(*refs):
        ins, outs = refs[:4 * n], refs[4 * n:]
        for k in range(n):
            d, m, v = _adamw_math(ins[k][...], ins[n + k][...], ins[2 * n + k][...], ins[3 * n + k][...])
            outs[k][...] = d
            outs[n + k][...] = m
            outs[2 * n + k][...] = v

    spec = lambda a: pl.BlockSpec((1,) + a.shape[1:], lambda i: (i, 0, 0))
    outs = pl.pallas_call(
        body, name=name, grid=(parts,),
        in_specs=[spec(a) for a in flat], out_specs=[spec(a) for a in flat[:n]] * 3,
        out_shape=[_out(a.shape, F32) for a in flat[:n]] * 3,
        compiler_params=_params(1, VMEM_LIMIT),
    )(*_hbm(*flat))
    outs = [o.reshape(o.shape[0] * o.shape[1], o.shape[2]) for o in outs]
    return outs[:n], outs[n:2 * n], outs[2 * n:]


def _adamw_halves(c, ws, owns, others, ms, vs, name):
    parts = 4
    n = len(ws)
    whole = [_rows_split(_rows_split(a, 2), parts) for a in (*ws, *ms, *vs)]
    halves = [_rows_split(a, parts) for a in (*owns, *others)]

    def body(c_ref, *refs):
        ins, outs = refs[:5 * n], refs[5 * n:]
        mine = pl.program_id(0) == c_ref[0]
        for k in range(n):
            g = jnp.where(mine, ins[3 * n + k][0], ins[4 * n + k][0])
            d, m, v = _adamw_math(ins[k][0, 0], g, ins[n + k][0, 0], ins[2 * n + k][0, 0])
            for slot, val in enumerate((g, d, m, v)):
                outs[slot * n + k][0, 0] = val

    wspec = lambda a: pl.BlockSpec((1, 1) + a.shape[2:], lambda h, r, c_ref: (h, r, 0, 0))
    hspec = lambda a: pl.BlockSpec((1,) + a.shape[1:], lambda h, r, c_ref: (r, 0, 0))
    outs = pl.pallas_call(
        body, name=name,
        grid_spec=pltpu.PrefetchScalarGridSpec(
            num_scalar_prefetch=1, grid=(2, parts),
            in_specs=[wspec(a) for a in whole] + [hspec(a) for a in halves],
            out_specs=[wspec(a) for a in whole[:n]] * 4),
        out_shape=[_out(a.shape, F32) for a in whole[:n]] * 4,
        compiler_params=_params(2, VMEM_LIMIT),
    )(c, *_hbm(*whole, *halves))
    outs = [o.reshape(2 * parts * o.shape[2], o.shape[3]) for o in outs]
    return outs[:n], outs[n:2 * n], outs[2 * n:3 * n], outs[3 * n:]


def _place():
    x, y, c = lax.axis_index("x"), lax.axis_index("y"), lax.axis_index("c")
    peers = [(x ^ (r >> 1), y ^ (r & 1), c) for r in (1, 2, 3)]
    return x, y, c, peers


def _handshake(peers):
    barrier = pltpu.get_barrier_semaphore()
    for peer in peers:
        pl.semaphore_signal(barrier, inc=1, device_id=peer, device_id_type=MESH)
    pl.semaphore_wait(barrier, len(peers))


ANY = pl.BlockSpec(memory_space=pl.ANY)
HBM = pl.BlockSpec(memory_space=pltpu.HBM)
SEM = pl.BlockSpec(memory_space=pltpu.SEMAPHORE)
SPLIT_COPY = pltpu.SideEffectType.DATAFLOW_SIDE_EFFECTING


def _split_start(body, name, collective_id, operands, n_sems, after=None):
    n = len(operands)
    extra = [] if after is None else [after]

    def wrapped(*refs):
        at = n + len(extra)
        body(refs[:n], refs[at], refs[at + 1])
        token = refs[-1]
        token[...] = jnp.zeros_like(token)

    outs = pl.pallas_call(
        wrapped, name=name,
        in_specs=[HBM] * n + [ANY] * len(extra),
        out_shape=(pltpu.SemaphoreType.DMA((n_sems,)), pltpu.SemaphoreType.DMA((n_sems,)),
                   *[pltpu.HBM(a.shape, a.dtype) for a in operands], jax.ShapeDtypeStruct((8, 128), F32)),
        out_specs=(SEM, SEM, *[HBM] * n, pl.BlockSpec(memory_space=pltpu.VMEM)),
        input_output_aliases={i: 2 + i for i in range(n)},
        compiler_params=pltpu.CompilerParams(has_side_effects=SPLIT_COPY, collective_id=collective_id),
    )(*_hbm(*operands), *extra)
    return outs[0], outs[1], list(outs[2:2 + n]), outs[-1]


def _split_wait(body, name, send_sem, recv_sem, operands, after):
    n = len(operands)

    def wrapped(*refs):
        body(refs[:n], refs[n], refs[n + 1])

    outs = pl.pallas_call(
        wrapped, name=name,
        in_specs=[HBM] * n + [SEM, SEM, ANY],
        out_shape=tuple(pltpu.HBM(a.shape, a.dtype) for a in operands),
        out_specs=tuple([HBM] * n),
        input_output_aliases={i: i for i in range(n)},
        compiler_params=pltpu.CompilerParams(has_side_effects=SPLIT_COPY),
    )(*operands, send_sem, recv_sem, after)
    return list(outs)


def _gather_copies(lands, in_chip_order, send_sem, recv_sem):
    peers = _place()[3]
    copies = []
    for a, (land, plain) in enumerate(zip(lands, in_chip_order)):
        own, sent_to, _, _ = _zone_slots(plain)
        copies += [pltpu.make_async_remote_copy(
            src_ref=land.at[own], dst_ref=land.at[sent_to(r + 1)],
            send_sem=send_sem.at[a * 3 + r], recv_sem=recv_sem.at[a * 3 + r],
            device_id=peers[r], device_id_type=MESH) for r in range(3)]
    return copies


def _gather_start(lands, in_chip_order, name, collective_id, after):
    def body(refs, send_sem, recv_sem):
        _handshake(_place()[3])
        for cp in _gather_copies(refs, in_chip_order, send_sem, recv_sem):
            cp.start()

    return _split_start(body, name, collective_id, list(lands), 3 * len(lands), after)


def _gather_wait(send_sem, recv_sem, operands, in_chip_order, after, name):
    def body(refs, send_sem, recv_sem):
        for cp in _gather_copies(refs, in_chip_order, send_sem, recv_sem):
            cp.wait_send()
            cp.wait_recv()

    return _split_wait(body, name, send_sem, recv_sem, operands, after)


def _gather_finish(lands, in_chip_order, with_ici, name):
    n = len(lands)

    def body(*refs):
        land = refs[n:2 * n]
        send_ici, recv_ici, send_d2d, recv_d2d = refs[2 * n:]
        x, y, c, _ = _place()
        ici = _gather_copies(land, in_chip_order, send_ici, recv_ici) if with_ici else []
        for cp in ici:
            cp.start()
        passed = []
        for a in range(n):
            _, _, received, kept = _zone_slots(in_chip_order[a])
            passed += [pltpu.make_async_remote_copy(
                src_ref=land[a].at[received(r + 1)], dst_ref=land[a].at[kept(r + 1)],
                send_sem=send_d2d.at[a * 3 + r], recv_sem=recv_d2d.at[a * 3 + r],
                device_id=(x, y, 1 - c), device_id_type=MESH) for r in range(3)]
        for k, cp in enumerate(passed):
            if with_ici:
                ici[k].wait_recv()
            cp.start()
        for cp in passed:
            cp.wait_recv()
        for cp in ici:
            cp.wait_send()
        for cp in passed:
            cp.wait_send()

    outs = pl.pallas_call(
        body, name=name,
        in_specs=[ANY] * n, out_specs=[ANY] * n,
        out_shape=[_out(l.shape, l.dtype) for l in lands],
        input_output_aliases={a: a for a in range(n)},
        scratch_shapes=[pltpu.SemaphoreType.DMA((3 * n,))] * 4,
    )(*lands)
    return list(outs)


def _pass_copies(lands, in_chip_order, send_sem, recv_sem):
    x, y, c, _ = _place()
    copies = []
    for a, (land, plain) in enumerate(zip(lands, in_chip_order)):
        _, _, received, kept = _zone_slots(plain)
        copies += [pltpu.make_async_remote_copy(
            src_ref=land.at[received(r + 1)], dst_ref=land.at[kept(r + 1)],
            send_sem=send_sem.at[a * 3 + r], recv_sem=recv_sem.at[a * 3 + r],
            device_id=(x, y, 1 - c), device_id_type=MESH) for r in range(3)]
    return copies


def _pass_start(lands, in_chip_order, name, collective_id):
    def body(refs, send_sem, recv_sem):
        x, y, c, _ = _place()
        _handshake([(x, y, 1 - c)])
        for cp in _pass_copies(refs, in_chip_order, send_sem, recv_sem):
            cp.start()

    return _split_start(body, name, collective_id, list(lands), 3 * len(lands))


def _pass_wait(send_sem, recv_sem, lands, in_chip_order, after, name):
    def body(refs, send_sem, recv_sem):
        for cp in _pass_copies(refs, in_chip_order, send_sem, recv_sem):
            cp.wait_send()
            cp.wait_recv()

    return _split_wait(body, name, send_sem, recv_sem, lands, after)


def _slabs(land):
    return land.reshape(N_CHIPS, 2 * land.shape[2], land.shape[3])


def _pair_swap(grads, permuted, name):
    n = len(grads)

    def body(*refs):
        src, dst = refs[:n], refs[n:2 * n]
        send_sem, recv_sem = refs[2 * n:]
        x, y, c, _ = _place()
        copies = [pltpu.make_async_remote_copy(
            src_ref=src[a].at[:, 1] if permuted[a] else src[a].at[:, 1 - c], dst_ref=dst[a],
            send_sem=send_sem.at[a], recv_sem=recv_sem.at[a],
            device_id=(x, y, 1 - c), device_id_type=MESH) for a in range(n)]
        for cp in copies:
            cp.start()
        for cp in copies:
            cp.wait()

    return pl.pallas_call(
        body, name=name,
        in_specs=[ANY] * n, out_specs=[ANY] * n,
        out_shape=[_out((N_CHIPS,) + g.shape[2:], g.dtype) for g in grads],
        scratch_shapes=[pltpu.SemaphoreType.DMA((n,))] * 2,
    )(*grads)


def _swap_copies(refs, permuted, send_sem, recv_sem):
    n = len(refs) // 2
    x, y, c, _ = _place()
    return [pltpu.make_async_remote_copy(
        src_ref=refs[a].at[:, 1] if permuted[a] else refs[a].at[:, 1 - c], dst_ref=refs[n + a],
        send_sem=send_sem.at[a], recv_sem=recv_sem.at[a],
        device_id=(x, y, 1 - c), device_id_type=MESH) for a in range(n)]


def _pair_swap_start(grads, permuted, name, collective_id):
    def body(refs, send_sem, recv_sem):
        x, y, c, _ = _place()
        _handshake([(x, y, 1 - c)])
        for cp in _swap_copies(refs, permuted, send_sem, recv_sem):
            cp.start()

    lands = [lax.empty((N_CHIPS,) + g.shape[2:], g.dtype) for g in grads]
    return _split_start(body, name, collective_id, list(grads) + lands, len(grads))


def _pair_swap_wait(send_sem, recv_sem, operands, permuted, after, name):
    def body(refs, send_sem, recv_sem):
        for cp in _swap_copies(refs, permuted, send_sem, recv_sem):
            cp.wait_send()
            cp.wait_recv()

    return _split_wait(body, name, send_sem, recv_sem, operands, after)


def _scatter_copies(refs, permuted, send_sem, recv_sem):
    n = len(refs) // 2
    x, y, _, peers = _place()
    me = 2 * x + y
    return [pltpu.make_async_remote_copy(
        src_ref=refs[a].at[r + 1] if permuted[a] else refs[a].at[me ^ (r + 1)], dst_ref=refs[n + a].at[r],
        send_sem=send_sem.at[a * 3 + r], recv_sem=recv_sem.at[a * 3 + r],
        device_id=peers[r], device_id_type=MESH) for a in range(n) for r in range(3)]


def _scatter_start(partials, permuted, name, collective_id):
    def body(refs, send_sem, recv_sem):
        _handshake(_place()[3])
        for cp in _scatter_copies(refs, permuted, send_sem, recv_sem):
            cp.start()

    lands = [lax.empty((N_CHIPS - 1,) + p.shape[1:], p.dtype) for p in partials]
    return _split_start(body, name, collective_id, list(partials) + lands, 3 * len(partials))


def _scatter_wait(send_sem, recv_sem, operands, permuted, after, name):
    def body(refs, send_sem, recv_sem):
        for cp in _scatter_copies(refs, permuted, send_sem, recv_sem):
            cp.wait_send()
            cp.wait_recv()

    return _split_wait(body, name, send_sem, recv_sem, operands, after)


def _pair_join(halves, name):
    n = len(halves)

    def body(*refs):
        src, dst = refs[:n], refs[n:2 * n]
        send_sem, recv_sem = refs[2 * n:]
        x, y, c, _ = _place()
        copies = [pltpu.make_async_remote_copy(
            src_ref=src[a], dst_ref=dst[a], send_sem=send_sem.at[a], recv_sem=recv_sem.at[a],
            device_id=(x, y, 1 - c), device_id_type=MESH) for a in range(n)]
        for cp in copies:
            cp.start()
        for cp in copies:
            cp.wait()

    return pl.pallas_call(
        body, name=name,
        in_specs=[ANY] * n, out_specs=[ANY] * n,
        out_shape=[_out(h.shape, F32) for h in halves],
        scratch_shapes=[pltpu.SemaphoreType.DMA((n,))] * 2,
    )(*halves)


def _join_copies(refs, send_sem, recv_sem):
    n = len(refs) // 2
    x, y, c, _ = _place()
    return [pltpu.make_async_remote_copy(
        src_ref=refs[a], dst_ref=refs[n + a], send_sem=send_sem.at[a], recv_sem=recv_sem.at[a],
        device_id=(x, y, 1 - c), device_id_type=MESH) for a in range(n)]


def _pair_join_start(halves, name, collective_id):
    def body(refs, send_sem, recv_sem):
        x, y, c, _ = _place()
        _handshake([(x, y, 1 - c)])
        for cp in _join_copies(refs, send_sem, recv_sem):
            cp.start()

    lands = [lax.empty(h.shape, h.dtype) for h in halves]
    return _split_start(body, name, collective_id, list(halves) + lands, len(halves))


def _pair_join_wait(send_sem, recv_sem, operands, after, name):
    def body(refs, send_sem, recv_sem):
        for cp in _join_copies(refs, send_sem, recv_sem):
            cp.wait_send()
            cp.wait_recv()

    return _split_wait(body, name, send_sem, recv_sem, operands, after)


def _all_sum_small(vs, after, name):
    rows = [v.shape[0] for v in vs]
    n, R, C = len(vs), sum(rows), vs[0].shape[1]
    n_dev = 8

    def body(*refs):
        after_ref, o_ref, mine, buf, send_sem, recv_sem = refs[n:]
        x, y, c, _ = _place()
        me = 4 * x + 2 * y + c
        at = 0
        for v_ref, r in zip(refs[:n], rows):
            mine[at:at + r] = v_ref[...]
            at += r
        buf[me] = mine[...]
        copies = []
        for k in range(1, n_dev):
            peer = (x ^ (k >> 2), y ^ ((k >> 1) & 1), c ^ (k & 1))
            copies.append(pltpu.make_async_remote_copy(
                src_ref=mine, dst_ref=buf.at[me], send_sem=send_sem.at[k - 1], recv_sem=recv_sem.at[k - 1],
                device_id=peer, device_id_type=MESH))
        for cp in copies:
            cp.start()
        for cp in copies:
            cp.wait()
        acc = buf[0]
        for m in range(1, n_dev):
            acc = acc + buf[m]
        o_ref[...] = acc

    return pl.pallas_call(
        body, name=name,
        in_specs=[pl.BlockSpec(memory_space=pltpu.VMEM)] * n + [ANY], out_specs=pl.BlockSpec(memory_space=pltpu.VMEM),
        out_shape=jax.ShapeDtypeStruct((R, C), F32),
        scratch_shapes=[pltpu.VMEM((R, C), F32), pltpu.VMEM((n_dev, R, C), F32),
                        pltpu.SemaphoreType.DMA((n_dev - 1,)), pltpu.SemaphoreType.DMA((n_dev - 1,))],
    )(*vs, after)


class _WholeWeights:
    def __init__(self, w):
        self.w = w

    def weights(self, group, after=None):
        return ({} if group == "passed" else self.w), None

    def grads_ready(self, group, gw):
        return None

    def grads_sent(self, group, after):
        return None


def _local_step(x, p, target, gains, rel_bias, hooks):
    T, D = x.shape
    S = N_CHIPS

    tied = lambda gain, token: gain if token is None else gain + token[0, 0]
    w, token = hooks.weights("first")
    w = dict(w)
    xn1, g1, u1, a1 = _ffn_up(x, tied(gains["ffn1_pre"], token), w["ffn1_gate"], w["ffn1_up"], "ffn1_up")
    w.update(hooks.weights("down", a1)[0])
    h1, f1 = _ffn_down(x, a1, gains["ffn1_post"], w["ffn1_down"], "ffn1_down")
    more, token = hooks.weights("in", h1)
    w.update(more)
    qkv, un = _norm_proj(h1, tied(gains["mix_pre"], token), w["in"], "qkv_proj")
    bias = _ch_group_bias(_bias_table(rel_bias, "bias_table").transpose(1, 0, 2))
    o_a = _sb_fwd(qkv, "sb_fwd")
    o_b = _ch_fwd(qkv, bias, "ch_fwd")
    more, token = hooks.weights("rest", o_b)
    w.update(more)
    w_out = w["out"].reshape(D, D)
    h2, mixed, mo = _mix_out_fwd(h1, o_a, o_b, gains["out_sb"], gains["out_ch"], w_out,
                                 tied(gains["mix_post"], token), "mix_out_fwd")
    w.update(hooks.weights("passed", h2)[0])
    h3, xn2, g2, u2, a2, f2 = _ffn_fwd(h2, gains["ffn2_pre"], gains["ffn2_post"], w["ffn2_gate"], w["ffn2_up"],
                                       w["ffn2_down"], "ffn2_fwd")
    w_ple_proj = w["ple_proj"].transpose(1, 0, 2).reshape(p.shape[1], D)
    w_ple_gate = w["ple_gate"].reshape(D, D)

    loss, dh3, dproj, dgate, dg_ple = _ple_loss(h3, p, target, w_ple_proj, w_ple_gate, gains["ple_post"], "ple_loss")
    gw, gg = {}, {"ple_post": dg_ple}
    gw["ple_proj"] = _mm_tn(p[None], dproj, p.shape[1], "dw_ple_proj")
    row_sharded = lambda pair: tuple(o.reshape(S, D // S, D) for o in pair)
    gw["ple_gate"] = row_sharded(_mm_tn(h3[None], dgate[None], 512, "dw_ple_gate"))

    def ffn_bwd(tag, dh, x_in, xn, g_act, u_act, a_act, f, group):
        dgp, dup, df, gg[tag + "_post"] = _ffn_bwd_act(dh, f, gains[tag + "_post"], w[tag + "_down"], g_act, u_act,
                                                       tag + "_bwd_act")
        gw[tag + "_gate"] = _mm_tn(dgp, xn[None], dgp.shape[2], "dw_" + tag + "_gate")
        gw[tag + "_up"] = _mm_tn(dup, xn[None], dup.shape[2], "dw_" + tag + "_up")
        token = None if group is None else hooks.grads_ready(group + "_first", gw)
        gw[tag + "_down"] = _mm_tn(a_act, df[None], a_act.shape[2], "dw_" + tag + "_down", token)
        g_pre = gains[tag + "_pre"]
        if group is not None:
            token = hooks.grads_ready(group, gw)
            g_pre = g_pre if token is None else g_pre + token[0, 0]
        dx, gg[tag + "_pre"] = _proj_bwd([dgp, dup], [w[tag + "_gate"], w[tag + "_up"]], x_in, g_pre, dh,
                                         tag + "_bwd_in")
        return dx

    dh2 = ffn_bwd("ffn2", dh3, h2, xn2, g2, u2, a2, f2, None)
    dmo, do_a, do_b, gg["mix_post"], gg["out_sb"], gg["out_ch"] = _mix_out_bwd(
        dh2, mo, gains["mix_post"], w_out, o_a, o_b, gains["out_sb"], gains["out_ch"], "mix_out_bwd")
    gw["out"] = row_sharded(_mm_tn(mixed[None], dmo[None], 512, "dw_out"))
    token = hooks.grads_ready("early", gw)
    dq_a, dk_a, dv_a = _sb_bwd(qkv, do_a, o_a, do_a if token is None else token, "sb_bwd")
    token = hooks.grads_sent("early", dq_a)
    dq_b, dk_b, dv_b, dbias = _ch_bwd(qkv, bias, do_b, do_b if token is None else token, "ch_bwd")
    g_rel = _bias_grad(_ch_fold_bias_grad(dbias).transpose(1, 0, 2), "bias_grad")
    dqkv = [dq_a, dk_a, dv_a, dq_b, dk_b, dv_b]
    gw["in"] = _dw_in(un, dqkv, w["in"].shape[2], 512, "dw_in")
    dh1, gg["mix_pre"] = _qkv_bwd_in(dqkv, w["in"], h1, gains["mix_pre"], dh2, "qkv_bwd_in")
    dx = ffn_bwd("ffn1", dh1, x, xn1, g1, u1, a1, f1, "late")
    return loss, dx, gw, gg, g_rel


BIG = ["ffn1_gate", "ffn1_up", "ffn1_down", "in", "out", "ffn2_gate", "ffn2_up", "ffn2_down", "ple_proj", "ple_gate"]
GAINS = ["ffn1_pre", "ffn1_post", "mix_pre", "mix_post", "out_sb", "out_ch", "ffn2_pre", "ffn2_post", "ple_post"]
TRANSPOSED = ("w_ffn1_gate", "w_ffn1_up", "w_ffn2_gate", "w_ffn2_up")
PERMUTED = ("ffn1_gate", "ffn1_up", "ffn1_down", "ffn2_gate", "ffn2_up", "ffn2_down")
W_GROUPS = {"first": ["ffn1_gate", "ffn1_up"], "down": ["ffn1_down"], "in": ["in"],
            "rest": ["out", "ffn2_gate", "ffn2_up", "ffn2_down", "ple_proj", "ple_gate"]}
G_GROUPS = {"early": ["ple_proj", "ple_gate", "ffn2_gate", "ffn2_up", "ffn2_down", "out"],
            "late": ["in", "ffn1_gate", "ffn1_up", "ffn1_down"]}
ORDER = ["g_ffn1_pre", "g_ffn1_post", "w_ffn1_gate", "w_ffn1_up", "w_ffn1_down", "g_mix_pre", "g_mix_post", "w_in",
         "g_out_sb", "g_out_ch", "rel_bias", "w_out", "g_ffn2_pre", "g_ffn2_post", "w_ffn2_gate", "w_ffn2_up",
         "w_ffn2_down", "w_ple_proj", "w_ple_gate", "g_ple_post"]


def kernel(x, p, g_ffn1_pre, g_ffn1_post, w_ffn1_gate, w_ffn1_up, w_ffn1_down, g_mix_pre, g_mix_post, w_in, g_out_sb, g_out_ch, rel_bias, w_out, g_ffn2_pre, g_ffn2_post, w_ffn2_gate, w_ffn2_up, w_ffn2_down, w_ple_proj, w_ple_gate, g_ple_post, loss_target, m_g_ffn1_pre, m_g_ffn1_post, m_w_ffn1_gate, m_w_ffn1_up, m_w_ffn1_down, m_g_mix_pre, m_g_mix_post, m_w_in, m_g_out_sb, m_g_out_ch, m_rel_bias, m_w_out, m_g_ffn2_pre, m_g_ffn2_post, m_w_ffn2_gate, m_w_ffn2_up, m_w_ffn2_down, m_w_ple_proj, m_w_ple_gate, m_g_ple_post, v_g_ffn1_pre, v_g_ffn1_post, v_w_ffn1_gate, v_w_ffn1_up, v_w_ffn1_down, v_g_mix_pre, v_g_mix_post, v_w_in, v_g_out_sb, v_g_out_ch, v_rel_bias, v_w_out, v_g_ffn2_pre, v_g_ffn2_post, v_w_ffn2_gate, v_w_ffn2_up, v_w_ffn2_down, v_w_ple_proj, v_w_ple_gate, v_g_ple_post):
    args = dict(locals())
    take = lambda a, n: a[0].T if n in TRANSPOSED else a[0]
    wts = {n: take(args[n], n) for n in ORDER}
    ms = {n: take(args["m_" + n], n) for n in ORDER}
    vs = {n: take(args["v_" + n], n) for n in ORDER}
    gains = {n: wts["g_" + n][None] for n in GAINS}

    c_idx = lax.axis_index("c").astype(jnp.int32).reshape(1)
    me_idx = (2 * lax.axis_index("x") + lax.axis_index("y")).astype(jnp.int32).reshape(1)
    south = lax.axis_index("c") == 0

    plain = lambda names: [n not in PERMUTED for n in names]
    lands = dict(zip(BIG, _cast_into_own_slot(me_idx, c_idx, [wts["w_" + n] for n in BIG], plain(BIG), "cast_weights")))

    class Overlapped:
        def __init__(self):
            self.started = {}
            self.flying = {}

        def start(self, group, collective_id, after):
            names = W_GROUPS[group]
            self.flying[group] = _gather_start([lands[n] for n in names], plain(names), "gather_%s_start" % group,
                                               collective_id, after)
            return self.flying[group][3]

        def weights(self, group, after=None):
            names = W_GROUPS.get(group)
            token = None
            if group == "first":
                zones = _gather_finish([lands[n] for n in names], plain(names), True, "gather_first")
                token = self.start("rest", 4, self.start("in", 1, self.start("down", 6, zones[0])))
            elif group == "passed":
                names, (send_sem, recv_sem, zones, _) = self.passing
                zones = _pass_wait(send_sem, recv_sem, zones, plain(names), after, "gather_rest_pass_wait")
            else:
                send_sem, recv_sem, zones, _ = self.flying[group]
                zones = _gather_wait(send_sem, recv_sem, zones, plain(names), after, "gather_%s_wait" % group)
                if group == "rest":
                    self.passing = names[1:], _pass_start(zones[1:], plain(names[1:]), "gather_rest_pass_start", 7)
                    names, zones, token = names[:1], zones[:1], self.passing[1][3]
                zones = _gather_finish(zones, plain(names), False, "gather_%s_finish" % group)
            return {n: _slabs(z) for n, z in zip(names, zones)}, token

        def grads_ready(self, group, gw):
            names = G_GROUPS["late"][:-1] if group == "late_first" else G_GROUPS[group]
            perm = [n in PERMUTED for n in names]
            halved = lambda g: g.reshape(N_CHIPS, 2, g.shape[1] // 2, g.shape[2])
            mine = [halved(gw[n][0]) for n in names]
            narrow = [halved(gw[n][1]) for n in names]
            if group == "late_first":
                self.swapping_late = _pair_swap_start(narrow, perm, "grad_pair_swap_start_late", 8)
                return self.swapping_late[3]
            if group == "late":
                send_sem, recv_sem, operands, _ = self.swapping_late
                operands = _pair_swap_wait(send_sem, recv_sem, operands, perm[:-1], narrow[-1],
                                           "grad_pair_swap_wait_late")
                got = operands[len(names) - 1:] + list(_pair_swap(narrow[-1:], perm[-1:], "grad_pair_swap_late"))
                return self.join_start("early", self.scatter(group, names, perm, mine, got))
            self.swapping = names, perm, mine, _pair_swap_start(narrow, perm, "grad_pair_swap_start_early", 5)
            return self.swapping[3][3]

        def grads_sent(self, group, after):
            names, perm, mine, (send_sem, recv_sem, operands, _) = self.swapping
            operands = _pair_swap_wait(send_sem, recv_sem, operands, perm, after, "grad_pair_swap_wait_early")
            return self.scatter(group, names, perm, mine, operands[len(names):])

        def scatter(self, group, names, perm, mine, got):
            partial = _pair_add(c_idx, mine, got, perm, "grad_pair_add_" + group)
            send_sem, recv_sem, operands, token = _scatter_start(partial, perm, "grad_scatter_start_" + group,
                                                                 {"early": 2, "late": 3}[group])
            self.started[group] = names, perm, send_sem, recv_sem, operands, token
            return token

        def join_start(self, group, after):
            own = chip_sum(self.started[group], after, group)
            self.joining = _pair_join_start(own, "grad_pair_join_start_" + group, 9)
            return self.joining[3]

    def chip_sum(state, after, tag):
        names, perm, send_sem, recv_sem, operands, _ = state
        operands = _scatter_wait(send_sem, recv_sem, operands, perm, after, "grad_scatter_wait_" + tag)
        n = len(names)
        return _chip_add(me_idx, operands[:n], operands[n:], perm, "grad_chip_add_" + tag)

    def reduce_finish(group, after):
        if group == "early":
            send_sem, recv_sem, operands, _ = hooks.joining
            operands = _pair_join_wait(send_sem, recv_sem, operands, after, "grad_pair_join_wait_" + group)
            return operands[:len(operands) // 2], operands[len(operands) // 2:]
        own = chip_sum(hooks.started[group], after, group)
        return own, _pair_join(own, "grad_pair_join_" + group)

    hooks = Overlapped()
    loss, dx, gw, gg, g_rel = _local_step(x[0], p[0, 0], loss_target[0], gains, wts["rel_bias"], hooks)

    grads, delta, new_m, new_v = {}, {}, {}, {}

    def finish(group, after):
        own, other = reduce_finish(group, after)
        names = ["w_" + n for n in G_GROUPS[group]]
        g, d, m, v = _adamw_halves(c_idx, [wts[n] for n in names], own, other, [ms[n] for n in names],
                                   [vs[n] for n in names], "adamw_" + group)
        for n, gg_, dd, mm, vv in zip(names, g, d, m, v):
            grads[n], delta[n], new_m[n], new_v[n] = gg_, dd, mm, vv
        return d[0]

    early_done = finish("early", dx)

    pieces = [gg[n].reshape(-1, 128) for n in GAINS] + [jnp.pad(g_rel, ((0, 0), (0, N_REL_PAD - N_REL))).reshape(-1, 128)]
    summed = _all_sum_small(pieces + [loss], early_done, "small_grad_sum")
    finish("late", summed)
    at = 0
    for n, piece in zip(GAINS, pieces[:-1]):
        grads["g_" + n] = summed[at:at + piece.shape[0]].reshape(1, -1)[0]
        at += piece.shape[0]
    grads["rel_bias"] = summed[at:at + pieces[-1].shape[0]].reshape(N_HEADS, N_REL_PAD)[:, :N_REL]
    loss = summed[at + pieces[-1].shape[0], 0]

    small = ["g_" + n for n in GAINS] + ["rel_bias"]
    as_rows = lambda a: (a.reshape(-1, 128) if a.size % 128 == 0 else jnp.pad(a, ((0, 0), (0, N_REL_PAD - N_REL))).reshape(-1, 128))
    d, m, v = _adamw([as_rows(wts[n]) for n in small], [as_rows(grads[n]) for n in small],
                     [as_rows(ms[n]) for n in small], [as_rows(vs[n]) for n in small], 1, "adamw_small")
    for n, dd, mm, vv in zip(small, d, m, v):
        back = (lambda a: a.reshape(N_HEADS, N_REL_PAD)[:, :N_REL]) if n == "rel_bias" else (lambda a: a.reshape(-1))
        delta[n], new_m[n], new_v[n] = back(dd), back(mm), back(vv)

    outs = [loss, dx[None]]
    for table in (grads, delta, new_m, new_v):
        outs += [(table[n].T if n in TRANSPOSED else table[n])[None] for n in ORDER]
    return tuple(outs)
```
